```python
import math
import jax, jax.numpy as jnp
from jax import lax
import numpy as np

D_MODEL = 1024
BATCH = 8
SEQ = 4096
DEPTH = 4

N_MIXERS = 3
EPS = 1e-6
N_POOL_LAYERS = (DEPTH + 2) // 3
N_GDN_LAYERS = (DEPTH + 1) // 3
N_MLA_LAYERS = DEPTH // 3

POOL_WIDTH = 2 * D_MODEL
POOL_WINDOWS = (2, 4, 8, 16)
POOL_GROUP = POOL_WIDTH // len(POOL_WINDOWS)

GDN_HEADS = 8
GDN_DK = 128
GDN_DV = 256
GDN_CONV = 4
GDN_CHUNK = 64
GDN_QK = GDN_HEADS * GDN_DK
GDN_V = GDN_HEADS * GDN_DV
GDN_CONV_CH = 2 * GDN_QK + GDN_V
GDN_IN = 2 * GDN_QK + 2 * GDN_V + 2 * GDN_HEADS

MLA_HEADS = 16
MLA_NOPE = 128
MLA_ROPE = 64
MLA_V = 128
MLA_Q_LORA = 768
MLA_KV_LORA = 512
MLA_QK = MLA_NOPE + MLA_ROPE
MLA_WIDTH = MLA_HEADS * MLA_V
MLA_IN = MLA_Q_LORA + MLA_KV_LORA + MLA_ROPE + MLA_WIDTH
ROPE_THETA = 10000.0
Q_BLOCK = 128

kernel_name = "hybrid_pool_gdn_mla_trunk"


def rmsnorm(x, g):
    xf = x.astype(jnp.float32)
    y = xf * lax.rsqrt(jnp.mean(xf * xf, axis=-1, keepdims=True) + EPS)
    return (y * g.astype(jnp.float32)).astype(x.dtype)


def l2norm(x):
    return x * lax.rsqrt(jnp.sum(x * x, axis=-1, keepdims=True) + EPS)


def pool_mixer(h, w_in, w_grp, scale, w_out):
    B, S, _ = h.shape
    u, gate = jnp.split(h @ w_in, 2, axis=-1)
    c = jnp.pad(jnp.cumsum(u.astype(jnp.float32), axis=1), ((0, 0), (1, 0), (0, 0)))
    t = jnp.arange(S)
    groups = []
    for gi, w in enumerate(POOL_WINDOWS):
        sl = slice(gi * POOL_GROUP, (gi + 1) * POOL_GROUP)
        cg = c[:, :, sl]
        lo = jnp.pad(cg[:, :S - w + 1], ((0, 0), (w - 1, 0), (0, 0)))
        cnt = jnp.minimum(t + 1, w).astype(jnp.float32)[None, :, None]
        mean = (cg[:, 1:] - lo) / cnt
        groups.append(mean.astype(u.dtype) - u[..., sl])
    p = jnp.stack(groups, axis=2)
    p = jnp.einsum('bsgi,gio->bsgo', p, w_grp).reshape(B, S, POOL_WIDTH)
    y = p * scale * jax.nn.silu(gate)
    return y @ w_out


def causal_depthwise_conv(u, w):
    K, C = w.shape
    return lax.conv_general_dilated(u, w[:, None, :], window_strides=(1,),
                                    padding=[(K - 1, 0)],
                                    dimension_numbers=('NWC', 'WIO', 'NWC'),
                                    feature_group_count=C)


def chunk_gated_delta_rule(q, k, v, g, beta):
    B, S, H, dk = q.shape
    dv = v.shape[-1]
    C = GDN_CHUNK
    N = S // C

    def to_chunks(a):
        return a.reshape(B, N, C, H, -1).transpose(0, 3, 1, 2, 4)

    q, k, v = to_chunks(q), to_chunks(k), to_chunks(v)
    beta = beta.reshape(B, N, C, H).transpose(0, 3, 1, 2)
    g = jnp.cumsum(g.reshape(B, N, C, H).transpose(0, 3, 1, 2), axis=-1)
    k_beta = k * beta[..., None]
    v_beta = v * beta[..., None]
    causal = jnp.tril(jnp.ones((C, C), dtype=bool))
    strict = jnp.tril(jnp.ones((C, C), dtype=bool), -1)
    diff = g[..., :, None] - g[..., None, :]
    decay = jnp.exp(jnp.where(causal, diff, -jnp.inf))
    L = jnp.where(strict, jnp.einsum('bhnid,bhnjd->bhnij', k_beta, k) * decay, 0.0)
    A = jnp.eye(C, dtype=L.dtype) + L
    rhs = jnp.concatenate([v_beta, k_beta * jnp.exp(g)[..., None]], axis=-1)
    sol = lax.linalg.triangular_solve(A, rhs, left_side=True, lower=True)
    u_ps = sol[..., :dv]
    w_cd = sol[..., dv:]
    attn = jnp.where(causal, jnp.einsum('bhnid,bhnjd->bhnij', q, k) * decay, 0.0)

    def step(state, xs):
        q_c, k_c, u_c, w_c, g_c, a_c = xs
        v_new = u_c - jnp.einsum('bhck,bhkv->bhcv', w_c, state)
        o = (jnp.einsum('bhck,bhkv->bhcv', q_c * jnp.exp(g_c)[..., None], state)
             + jnp.einsum('bhij,bhjv->bhiv', a_c, v_new))
        g_last = g_c[..., -1]
        k_dec = k_c * jnp.exp(g_last[..., None] - g_c)[..., None]
        state = state * jnp.exp(g_last)[..., None, None] + jnp.einsum('bhck,bhcv->bhkv', k_dec, v_new)
        return state, o

    mv = lambda a: jnp.moveaxis(a, 2, 0)
    xs = (mv(q), mv(k), mv(u_ps), mv(w_cd), mv(g), mv(attn))
    state0 = jnp.zeros((B, H, dk, dv), jnp.float32)
    _, o = lax.scan(step, state0, xs)
    return o.transpose(1, 0, 3, 2, 4).reshape(B, S, H, dv)


def gdn_mixer(h, w_in, conv_w, a_log, dt_bias, norm_g, w_out):
    B, S, _ = h.shape
    f32 = jnp.float32
    proj = h @ w_in
    qkv, gate, b_raw, a_raw = jnp.split(
        proj, [GDN_CONV_CH, GDN_CONV_CH + GDN_V, GDN_CONV_CH + GDN_V + GDN_HEADS], axis=-1)
    qkv = jax.nn.silu(causal_depthwise_conv(qkv, conv_w))
    q, k, v = jnp.split(qkv, [GDN_QK, 2 * GDN_QK], axis=-1)
    q = l2norm(q.reshape(B, S, GDN_HEADS, GDN_DK).astype(f32)) * (GDN_DK ** -0.5)
    k = l2norm(k.reshape(B, S, GDN_HEADS, GDN_DK).astype(f32))
    v = v.reshape(B, S, GDN_HEADS, GDN_DV).astype(f32)
    beta = jax.nn.sigmoid(b_raw.astype(f32))
    g = -jnp.exp(a_log.astype(f32)) * jax.nn.softplus(a_raw.astype(f32) + dt_bias.astype(f32))
    o = chunk_gated_delta_rule(q, k, v, g, beta)
    o = rmsnorm(o, norm_g) * jax.nn.silu(gate.reshape(B, S, GDN_HEADS, GDN_DV).astype(f32))
    return o.reshape(B, S, GDN_V).astype(h.dtype) @ w_out


def rope(x, pos):
    half = x.shape[-1] // 2
    inv = ROPE_THETA ** (-jnp.arange(half, dtype=jnp.float32) / half)
    ang = pos.astype(jnp.float32)[..., None, None] * inv
    cos, sin = jnp.cos(ang), jnp.sin(ang)
    x1, x2 = x[..., :half], x[..., half:]
    return jnp.concatenate([x1 * cos - x2 * sin, x1 * sin + x2 * cos], axis=-1).astype(x.dtype)


def causal_block_attention(q, k, v):
    B, S, H, Dq = q.shape
    nb = S // Q_BLOCK
    scale = Dq ** -0.5
    qb = q.reshape(B, nb, Q_BLOCK, H, Dq).transpose(1, 0, 2, 3, 4)
    kpos = jnp.arange(S)

    def block(args):
        i, q_blk = args
        s = jnp.einsum('bqhd,bkhd->bhqk', q_blk, k).astype(jnp.float32) * scale
        qpos = i * Q_BLOCK + jnp.arange(Q_BLOCK)
        s = jnp.where(kpos[None, :] <= qpos[:, None], s, -jnp.inf)
        p = jax.nn.softmax(s, axis=-1)
        return jnp.einsum('bhqk,bkhd->bqhd', p.astype(v.dtype), v)

    o = lax.map(block, (jnp.arange(nb), qb))
    return o.transpose(1, 0, 2, 3, 4).reshape(B, S, H, v.shape[-1])


def mla_mixer(h, pos, w_in, q_norm_g, w_uq, kv_norm_g, w_ukv, w_out):
    B, S, _ = h.shape
    proj = h @ w_in
    cq, ckv, k_rope, gate = jnp.split(
        proj, [MLA_Q_LORA, MLA_Q_LORA + MLA_KV_LORA, MLA_Q_LORA + MLA_KV_LORA + MLA_ROPE], axis=-1)
    q = (rmsnorm(cq, q_norm_g) @ w_uq).reshape(B, S, MLA_HEADS, MLA_QK)
    kv = (rmsnorm(ckv, kv_norm_g) @ w_ukv).reshape(B, S, MLA_HEADS, MLA_NOPE + MLA_V)
    q_nope, q_rope = q[..., :MLA_NOPE], q[..., MLA_NOPE:]
    k_nope, v = kv[..., :MLA_NOPE], kv[..., MLA_NOPE:]
    q_rope = rope(q_rope, pos)
    k_rope = rope(k_rope[:, :, None, :], pos)
    q = jnp.concatenate([q_nope, q_rope], axis=-1)
    k = jnp.concatenate([k_nope, jnp.broadcast_to(k_rope, (B, S, MLA_HEADS, MLA_ROPE))], axis=-1)
    o = causal_block_attention(q, k, v).reshape(B, S, MLA_WIDTH)
    return (o * jax.nn.silu(gate)) @ w_out


def _fwd_setup_inputs(seed: int = 0) -> dict:
    key = jax.random.key(seed)
    ks = jax.random.split(key, 24)
    nrm = lambda k, shape, fan: jax.random.normal(k, shape, jnp.float32) * (fan ** -0.5)
    gain = lambda k, shape: 1.0 + 0.02 * jax.random.normal(k, shape, jnp.float32)
    nA, nB, nC = N_POOL_LAYERS, N_GDN_LAYERS, N_MLA_LAYERS
    x = jax.random.normal(ks[0], (BATCH, SEQ, D_MODEL), jnp.float32)
    positions = jnp.broadcast_to(jnp.arange(SEQ, dtype=jnp.int32)[None, :], (BATCH, SEQ))
    dt = jnp.exp(jax.random.uniform(ks[10], (nB, GDN_HEADS), jnp.float32,
                                    math.log(1e-3), math.log(1e-1)))
    return {
        "x": x,
        "positions": positions,
        "norm_g": gain(ks[1], (DEPTH, D_MODEL)),
        "pool_w_in": nrm(ks[2], (nA, D_MODEL, 2 * POOL_WIDTH), D_MODEL),
        "pool_w_grp": nrm(ks[3], (nA, len(POOL_WINDOWS), POOL_GROUP, POOL_GROUP), POOL_GROUP),
        "pool_scale": gain(ks[4], (nA, POOL_WIDTH)),
        "pool_w_out": nrm(ks[5], (nA, POOL_WIDTH, D_MODEL), POOL_WIDTH),
        "gdn_w_in": nrm(ks[6], (nB, D_MODEL, GDN_IN), D_MODEL),
        "gdn_conv": nrm(ks[7], (nB, GDN_CONV, GDN_CONV_CH), GDN_CONV),
        "gdn_a_log": jnp.log(jax.random.uniform(ks[8], (nB, GDN_HEADS), jnp.float32, 1.0, 16.0)),
        "gdn_dt_bias": dt + jnp.log(-jnp.expm1(-dt)),
        "gdn_norm_g": gain(ks[9], (nB, GDN_DV)),
        "gdn_w_out": nrm(ks[11], (nB, GDN_V, D_MODEL), GDN_V),
        "mla_w_in": nrm(ks[12], (nC, D_MODEL, MLA_IN), D_MODEL),
        "mla_q_norm_g": gain(ks[13], (nC, MLA_Q_LORA)),
        "mla_w_uq": nrm(ks[14], (nC, MLA_Q_LORA, MLA_HEADS * MLA_QK), MLA_Q_LORA),
        "mla_kv_norm_g": gain(ks[15], (nC, MLA_KV_LORA)),
        "mla_w_ukv": nrm(ks[16], (nC, MLA_KV_LORA, MLA_HEADS * (MLA_NOPE + MLA_V)), MLA_KV_LORA),
        "mla_w_out": nrm(ks[17], (nC, MLA_WIDTH, D_MODEL), MLA_WIDTH),
        "final_g": gain(ks[18], (D_MODEL,)),
    }


def _fwd_reference(x, positions, norm_g, pool_w_in, pool_w_grp, pool_scale, pool_w_out,
              gdn_w_in, gdn_conv, gdn_a_log, gdn_dt_bias, gdn_norm_g, gdn_w_out,
              mla_w_in, mla_q_norm_g, mla_w_uq, mla_kv_norm_g, mla_w_ukv, mla_w_out,
              final_g):
    for i in range(DEPTH):
        kind, j = i % N_MIXERS, i // N_MIXERS
        h = rmsnorm(x, norm_g[i])
        if kind == 0:
            y = pool_mixer(h, pool_w_in[j], pool_w_grp[j], pool_scale[j], pool_w_out[j])
        elif kind == 1:
            y = gdn_mixer(h, gdn_w_in[j], gdn_conv[j], gdn_a_log[j], gdn_dt_bias[j],
                          gdn_norm_g[j], gdn_w_out[j])
        else:
            y = mla_mixer(h, positions, mla_w_in[j], mla_q_norm_g[j], mla_w_uq[j],
                          mla_kv_norm_g[j], mla_w_ukv[j], mla_w_out[j])
        x = x + y.astype(x.dtype)
    return rmsnorm(x, final_g)


import jax as _jax
import jax.numpy as _jnp

TWIN_FORMAT = 'train_step'
FWD_PARAMS = ['x', 'positions', 'norm_g', 'pool_w_in', 'pool_w_grp', 'pool_scale', 'pool_w_out', 'gdn_w_in', 'gdn_conv', 'gdn_a_log', 'gdn_dt_bias', 'gdn_norm_g', 'gdn_w_out', 'mla_w_in', 'mla_q_norm_g', 'mla_w_uq', 'mla_kv_norm_g', 'mla_w_ukv', 'mla_w_out', 'final_g']
TWIN_WEIGHTS = ['norm_g', 'pool_w_in', 'pool_w_grp', 'pool_scale', 'pool_w_out', 'gdn_w_in', 'gdn_conv', 'gdn_a_log', 'gdn_dt_bias', 'gdn_norm_g', 'gdn_w_out', 'mla_w_in', 'mla_q_norm_g', 'mla_w_uq', 'mla_kv_norm_g', 'mla_w_ukv', 'mla_w_out', 'final_g']
TWIN_DIFF_INPUT = 'x'
TWIN_INPUTS = ['x', 'positions', 'norm_g', 'pool_w_in', 'pool_w_grp', 'pool_scale', 'pool_w_out', 'gdn_w_in', 'gdn_conv', 'gdn_a_log', 'gdn_dt_bias', 'gdn_norm_g', 'gdn_w_out', 'mla_w_in', 'mla_q_norm_g', 'mla_w_uq', 'mla_kv_norm_g', 'mla_w_ukv', 'mla_w_out', 'final_g', 'loss_target', 'm_norm_g', 'm_pool_w_in', 'm_pool_w_grp', 'm_pool_scale', 'm_pool_w_out', 'm_gdn_w_in', 'm_gdn_conv', 'm_gdn_a_log', 'm_gdn_dt_bias', 'm_gdn_norm_g', 'm_gdn_w_out', 'm_mla_w_in', 'm_mla_q_norm_g', 'm_mla_w_uq', 'm_mla_kv_norm_g', 'm_mla_w_ukv', 'm_mla_w_out', 'm_final_g', 'v_norm_g', 'v_pool_w_in', 'v_pool_w_grp', 'v_pool_scale', 'v_pool_w_out', 'v_gdn_w_in', 'v_gdn_conv', 'v_gdn_a_log', 'v_gdn_dt_bias', 'v_gdn_norm_g', 'v_gdn_w_out', 'v_mla_w_in', 'v_mla_q_norm_g', 'v_mla_w_uq', 'v_mla_kv_norm_g', 'v_mla_w_ukv', 'v_mla_w_out', 'v_final_g']
TWIN_OUTPUTS = ['loss', 'grad_x', 'grad_norm_g', 'grad_pool_w_in', 'grad_pool_w_grp', 'grad_pool_scale', 'grad_pool_w_out', 'grad_gdn_w_in', 'grad_gdn_conv', 'grad_gdn_a_log', 'grad_gdn_dt_bias', 'grad_gdn_norm_g', 'grad_gdn_w_out', 'grad_mla_w_in', 'grad_mla_q_norm_g', 'grad_mla_w_uq', 'grad_mla_kv_norm_g', 'grad_mla_w_ukv', 'grad_mla_w_out', 'grad_final_g', 'delta_norm_g', 'delta_pool_w_in', 'delta_pool_w_grp', 'delta_pool_scale', 'delta_pool_w_out', 'delta_gdn_w_in', 'delta_gdn_conv', 'delta_gdn_a_log', 'delta_gdn_dt_bias', 'delta_gdn_norm_g', 'delta_gdn_w_out', 'delta_mla_w_in', 'delta_mla_q_norm_g', 'delta_mla_w_uq', 'delta_mla_kv_norm_g', 'delta_mla_w_ukv', 'delta_mla_w_out', 'delta_final_g', 'new_m_norm_g', 'new_m_pool_w_in', 'new_m_pool_w_grp', 'new_m_pool_scale', 'new_m_pool_w_out', 'new_m_gdn_w_in', 'new_m_gdn_conv', 'new_m_gdn_a_log', 'new_m_gdn_dt_bias', 'new_m_gdn_norm_g', 'new_m_gdn_w_out', 'new_m_mla_w_in', 'new_m_mla_q_norm_g', 'new_m_mla_w_uq', 'new_m_mla_kv_norm_g', 'new_m_mla_w_ukv', 'new_m_mla_w_out', 'new_m_final_g', 'new_v_norm_g', 'new_v_pool_w_in', 'new_v_pool_w_grp', 'new_v_pool_scale', 'new_v_pool_w_out', 'new_v_gdn_w_in', 'new_v_gdn_conv', 'new_v_gdn_a_log', 'new_v_gdn_dt_bias', 'new_v_gdn_norm_g', 'new_v_gdn_w_out', 'new_v_mla_w_in', 'new_v_mla_q_norm_g', 'new_v_mla_w_uq', 'new_v_mla_kv_norm_g', 'new_v_mla_w_ukv', 'new_v_mla_w_out', 'new_v_final_g']
TWIN_LEAF_KINDS = {'loss': 'loss', 'grad_x': 'grad_x', 'grad_norm_g': 'grad_w', 'grad_pool_w_in': 'grad_w', 'grad_pool_w_grp': 'grad_w', 'grad_pool_scale': 'grad_w', 'grad_pool_w_out': 'grad_w', 'grad_gdn_w_in': 'grad_w', 'grad_gdn_conv': 'grad_w', 'grad_gdn_a_log': 'grad_w', 'grad_gdn_dt_bias': 'grad_w', 'grad_gdn_norm_g': 'grad_w', 'grad_gdn_w_out': 'grad_w', 'grad_mla_w_in': 'grad_w', 'grad_mla_q_norm_g': 'grad_w', 'grad_mla_w_uq': 'grad_w', 'grad_mla_kv_norm_g': 'grad_w', 'grad_mla_w_ukv': 'grad_w', 'grad_mla_w_out': 'grad_w', 'grad_final_g': 'grad_w', 'delta_norm_g': 'delta_w', 'delta_pool_w_in': 'delta_w', 'delta_pool_w_grp': 'delta_w', 'delta_pool_scale': 'delta_w', 'delta_pool_w_out': 'delta_w', 'delta_gdn_w_in': 'delta_w', 'delta_gdn_conv': 'delta_w', 'delta_gdn_a_log': 'delta_w', 'delta_gdn_dt_bias': 'delta_w', 'delta_gdn_norm_g': 'delta_w', 'delta_gdn_w_out': 'delta_w', 'delta_mla_w_in': 'delta_w', 'delta_mla_q_norm_g': 'delta_w', 'delta_mla_w_uq': 'delta_w', 'delta_mla_kv_norm_g': 'delta_w', 'delta_mla_w_ukv': 'delta_w', 'delta_mla_w_out': 'delta_w', 'delta_final_g': 'delta_w', 'new_m_norm_g': 'new_m', 'new_m_pool_w_in': 'new_m', 'new_m_pool_w_grp': 'new_m', 'new_m_pool_scale': 'new_m', 'new_m_pool_w_out': 'new_m', 'new_m_gdn_w_in': 'new_m', 'new_m_gdn_conv': 'new_m', 'new_m_gdn_a_log': 'new_m', 'new_m_gdn_dt_bias': 'new_m', 'new_m_gdn_norm_g': 'new_m', 'new_m_gdn_w_out': 'new_m', 'new_m_mla_w_in': 'new_m', 'new_m_mla_q_norm_g': 'new_m', 'new_m_mla_w_uq': 'new_m', 'new_m_mla_kv_norm_g': 'new_m', 'new_m_mla_w_ukv': 'new_m', 'new_m_mla_w_out': 'new_m', 'new_m_final_g': 'new_m', 'new_v_norm_g': 'new_v', 'new_v_pool_w_in': 'new_v', 'new_v_pool_w_grp': 'new_v', 'new_v_pool_scale': 'new_v', 'new_v_pool_w_out': 'new_v', 'new_v_gdn_w_in': 'new_v', 'new_v_gdn_conv': 'new_v', 'new_v_gdn_a_log': 'new_v', 'new_v_gdn_dt_bias': 'new_v', 'new_v_gdn_norm_g': 'new_v', 'new_v_gdn_w_out': 'new_v', 'new_v_mla_w_in': 'new_v', 'new_v_mla_q_norm_g': 'new_v', 'new_v_mla_w_uq': 'new_v', 'new_v_mla_kv_norm_g': 'new_v', 'new_v_mla_w_ukv': 'new_v', 'new_v_mla_w_out': 'new_v', 'new_v_final_g': 'new_v'}


def _forward(args):
    return _fwd_reference(*[args[k] for k in FWD_PARAMS])


def _output_shape():
    def fwd():
        inp = _fwd_setup_inputs(0)
        return _fwd_reference(*[inp[k] for k in FWD_PARAMS])
    out = _jax.eval_shape(fwd)
    return out.shape, out.dtype

N_MICROBATCH = 1
ADAM_LR = 0.001
ADAM_B1 = 0.9
ADAM_B2 = 0.999
ADAM_EPS = 1e-08
ADAM_WD = 0.01
ADAM_STEP = 10
PER_EXAMPLE_BATCH_AXIS = {'x': 0, 'positions': 0, 'loss_target': 0}
SHARED_INPUTS = []
_WEIGHT_DTYPES = {'norm_g': _jnp.float32, 'pool_w_in': _jnp.float32, 'pool_w_grp': _jnp.float32, 'pool_scale': _jnp.float32, 'pool_w_out': _jnp.float32, 'gdn_w_in': _jnp.float32, 'gdn_conv': _jnp.float32, 'gdn_a_log': _jnp.float32, 'gdn_dt_bias': _jnp.float32, 'gdn_norm_g': _jnp.float32, 'gdn_w_out': _jnp.float32, 'mla_w_in': _jnp.float32, 'mla_q_norm_g': _jnp.float32, 'mla_w_uq': _jnp.float32, 'mla_kv_norm_g': _jnp.float32, 'mla_w_ukv': _jnp.float32, 'mla_w_out': _jnp.float32, 'final_g': _jnp.float32}
MOMENT_SCALE = {'norm_g': 1.192395e-01, 'pool_w_in': 6.316881e-02, 'pool_w_grp': 6.210333e-02, 'pool_scale': 6.299893e-02, 'pool_w_out': 8.775452e-02, 'gdn_w_in': 6.147507e-02, 'gdn_conv': 6.146315e-02, 'gdn_a_log': 2.338932e-01, 'gdn_dt_bias': 2.296556e-01, 'gdn_norm_g': 1.832457e-01, 'gdn_w_out': 8.793231e-02, 'mla_w_in': 2.052002e-02, 'mla_q_norm_g': 1.805151e-02, 'mla_w_uq': 9.106188e-03, 'mla_kv_norm_g': 3.475726e-02, 'mla_w_ukv': 1.231592e-02, 'mla_w_out': 2.113934e-02, 'final_g': 3.201576e+01}


def _to_microbatches(a, axis):
    t = _jnp.moveaxis(a, axis, 0)
    t = t.reshape((N_MICROBATCH, t.shape[0] // N_MICROBATCH) + t.shape[1:])
    return _jnp.moveaxis(t, 1, axis + 1)


def setup_inputs(seed: int = 0) -> dict:
    inp = _fwd_setup_inputs(seed)
    key = _jax.random.fold_in(_jax.random.key(seed), 7919)
    shape, _ = _output_shape()
    out = dict(inp)
    out["loss_target"] = _jax.random.normal(_jax.random.fold_in(key, 0), shape, _jnp.float32)
    for i, name in enumerate(TWIN_WEIGHTS):
        w = inp[name].astype(_jnp.float32)
        if MOMENT_SCALE is None:
            s = _jnp.sqrt(_jnp.mean(_jnp.square(w)) + 1e-30)
        else:
            s = MOMENT_SCALE[name]
        km, kv = _jax.random.split(_jax.random.fold_in(key, i + 1))
        out[name] = w
        out["m_" + name] = s * _jax.random.normal(km, w.shape, _jnp.float32)
        out["v_" + name] = (s * s) * _jax.random.uniform(kv, w.shape, _jnp.float32, 0.5, 1.5)
    if N_MICROBATCH > 1:
        for name, axis in PER_EXAMPLE_BATCH_AXIS.items():
            out[name] = _to_microbatches(out[name], axis)
    return {'x': out['x'], 'positions': out['positions'], 'norm_g': out['norm_g'], 'pool_w_in': out['pool_w_in'], 'pool_w_grp': out['pool_w_grp'], 'pool_scale': out['pool_scale'], 'pool_w_out': out['pool_w_out'], 'gdn_w_in': out['gdn_w_in'], 'gdn_conv': out['gdn_conv'], 'gdn_a_log': out['gdn_a_log'], 'gdn_dt_bias': out['gdn_dt_bias'], 'gdn_norm_g': out['gdn_norm_g'], 'gdn_w_out': out['gdn_w_out'], 'mla_w_in': out['mla_w_in'], 'mla_q_norm_g': out['mla_q_norm_g'], 'mla_w_uq': out['mla_w_uq'], 'mla_kv_norm_g': out['mla_kv_norm_g'], 'mla_w_ukv': out['mla_w_ukv'], 'mla_w_out': out['mla_w_out'], 'final_g': out['final_g'], 'loss_target': out['loss_target'], 'm_norm_g': out['m_norm_g'], 'm_pool_w_in': out['m_pool_w_in'], 'm_pool_w_grp': out['m_pool_w_grp'], 'm_pool_scale': out['m_pool_scale'], 'm_pool_w_out': out['m_pool_w_out'], 'm_gdn_w_in': out['m_gdn_w_in'], 'm_gdn_conv': out['m_gdn_conv'], 'm_gdn_a_log': out['m_gdn_a_log'], 'm_gdn_dt_bias': out['m_gdn_dt_bias'], 'm_gdn_norm_g': out['m_gdn_norm_g'], 'm_gdn_w_out': out['m_gdn_w_out'], 'm_mla_w_in': out['m_mla_w_in'], 'm_mla_q_norm_g': out['m_mla_q_norm_g'], 'm_mla_w_uq': out['m_mla_w_uq'], 'm_mla_kv_norm_g': out['m_mla_kv_norm_g'], 'm_mla_w_ukv': out['m_mla_w_ukv'], 'm_mla_w_out': out['m_mla_w_out'], 'm_final_g': out['m_final_g'], 'v_norm_g': out['v_norm_g'], 'v_pool_w_in': out['v_pool_w_in'], 'v_pool_w_grp': out['v_pool_w_grp'], 'v_pool_scale': out['v_pool_scale'], 'v_pool_w_out': out['v_pool_w_out'], 'v_gdn_w_in': out['v_gdn_w_in'], 'v_gdn_conv': out['v_gdn_conv'], 'v_gdn_a_log': out['v_gdn_a_log'], 'v_gdn_dt_bias': out['v_gdn_dt_bias'], 'v_gdn_norm_g': out['v_gdn_norm_g'], 'v_gdn_w_out': out['v_gdn_w_out'], 'v_mla_w_in': out['v_mla_w_in'], 'v_mla_q_norm_g': out['v_mla_q_norm_g'], 'v_mla_w_uq': out['v_mla_w_uq'], 'v_mla_kv_norm_g': out['v_mla_kv_norm_g'], 'v_mla_w_ukv': out['v_mla_w_ukv'], 'v_mla_w_out': out['v_mla_w_out'], 'v_final_g': out['v_final_g']}


def _loss(weights, diff, rest, loss_target):
    with _jax.named_scope("forward"):
        args = {**rest, TWIN_DIFF_INPUT: diff, **{k: w.astype(_WEIGHT_DTYPES[k]) for k, w in weights.items()}}
        y = _forward(args)
    with _jax.named_scope("loss_head"):
        err = _jnp.square(y.astype(_jnp.float32) - loss_target)
        return 0.5 * _jnp.sum(_jnp.mean(err, axis=-1)) if err.ndim else 0.5 * err


def _adamw(w, g, m, v):
    m = ADAM_B1 * m + (1.0 - ADAM_B1) * g
    v = ADAM_B2 * v + (1.0 - ADAM_B2) * _jnp.square(g)
    m_hat = m / (1.0 - ADAM_B1 ** ADAM_STEP)
    v_hat = v / (1.0 - ADAM_B2 ** ADAM_STEP)
    delta = -ADAM_LR * (m_hat / (_jnp.sqrt(v_hat) + ADAM_EPS) + ADAM_WD * w)
    return delta, m, v


def reference(x, positions, norm_g, pool_w_in, pool_w_grp, pool_scale, pool_w_out, gdn_w_in, gdn_conv, gdn_a_log, gdn_dt_bias, gdn_norm_g, gdn_w_out, mla_w_in, mla_q_norm_g, mla_w_uq, mla_kv_norm_g, mla_w_ukv, mla_w_out, final_g, loss_target, m_norm_g, m_pool_w_in, m_pool_w_grp, m_pool_scale, m_pool_w_out, m_gdn_w_in, m_gdn_conv, m_gdn_a_log, m_gdn_dt_bias, m_gdn_norm_g, m_gdn_w_out, m_mla_w_in, m_mla_q_norm_g, m_mla_w_uq, m_mla_kv_norm_g, m_mla_w_ukv, m_mla_w_out, m_final_g, v_norm_g, v_pool_w_in, v_pool_w_grp, v_pool_scale, v_pool_w_out, v_gdn_w_in, v_gdn_conv, v_gdn_a_log, v_gdn_dt_bias, v_gdn_norm_g, v_gdn_w_out, v_mla_w_in, v_mla_q_norm_g, v_mla_w_uq, v_mla_kv_norm_g, v_mla_w_ukv, v_mla_w_out, v_final_g):
    given = dict(x=x, positions=positions, norm_g=norm_g, pool_w_in=pool_w_in, pool_w_grp=pool_w_grp, pool_scale=pool_scale, pool_w_out=pool_w_out, gdn_w_in=gdn_w_in, gdn_conv=gdn_conv, gdn_a_log=gdn_a_log, gdn_dt_bias=gdn_dt_bias, gdn_norm_g=gdn_norm_g, gdn_w_out=gdn_w_out, mla_w_in=mla_w_in, mla_q_norm_g=mla_q_norm_g, mla_w_uq=mla_w_uq, mla_kv_norm_g=mla_kv_norm_g, mla_w_ukv=mla_w_ukv, mla_w_out=mla_w_out, final_g=final_g, loss_target=loss_target, m_norm_g=m_norm_g, m_pool_w_in=m_pool_w_in, m_pool_w_grp=m_pool_w_grp, m_pool_scale=m_pool_scale, m_pool_w_out=m_pool_w_out, m_gdn_w_in=m_gdn_w_in, m_gdn_conv=m_gdn_conv, m_gdn_a_log=m_gdn_a_log, m_gdn_dt_bias=m_gdn_dt_bias, m_gdn_norm_g=m_gdn_norm_g, m_gdn_w_out=m_gdn_w_out, m_mla_w_in=m_mla_w_in, m_mla_q_norm_g=m_mla_q_norm_g, m_mla_w_uq=m_mla_w_uq, m_mla_kv_norm_g=m_mla_kv_norm_g, m_mla_w_ukv=m_mla_w_ukv, m_mla_w_out=m_mla_w_out, m_final_g=m_final_g, v_norm_g=v_norm_g, v_pool_w_in=v_pool_w_in, v_pool_w_grp=v_pool_w_grp, v_pool_scale=v_pool_scale, v_pool_w_out=v_pool_w_out, v_gdn_w_in=v_gdn_w_in, v_gdn_conv=v_gdn_conv, v_gdn_a_log=v_gdn_a_log, v_gdn_dt_bias=v_gdn_dt_bias, v_gdn_norm_g=v_gdn_norm_g, v_gdn_w_out=v_gdn_w_out, v_mla_w_in=v_mla_w_in, v_mla_q_norm_g=v_mla_q_norm_g, v_mla_w_uq=v_mla_w_uq, v_mla_kv_norm_g=v_mla_kv_norm_g, v_mla_w_ukv=v_mla_w_ukv, v_mla_w_out=v_mla_w_out, v_final_g=v_final_g)
    weights = {n: given[n] for n in TWIN_WEIGHTS}
    shared = {n: given[n] for n in SHARED_INPUTS}
    per_example = {n: given[n] for n in ['x', 'positions']}
    grad_fn = _jax.value_and_grad(_loss, argnums=(0, 1))

    def one_microbatch(ex, loss_target):
        ex = dict(ex)
        diff = ex.pop(TWIN_DIFF_INPUT)
        return grad_fn(weights, diff, {**shared, **ex}, loss_target)

    if N_MICROBATCH == 1:
        loss, (grad_w, grad_x) = one_microbatch(per_example, given["loss_target"])
    else:
        def body(carry, xs):
            loss_sum, grad_sum = carry
            l_k, (gw_k, gx_k) = one_microbatch(xs[0], xs[1])
            with _jax.named_scope("update"):
                return (loss_sum + l_k, _jax.tree.map(_jnp.add, grad_sum, gw_k)), gx_k

        init = (_jnp.zeros((), _jnp.float32), _jax.tree.map(_jnp.zeros_like, weights))
        (loss, grad_w), grad_x = _jax.lax.scan(body, init, (per_example, given["loss_target"]))
    with _jax.named_scope("update"):
        delta_w, new_m, new_v = {}, {}, {}
        for n in TWIN_WEIGHTS:
            delta_w[n], new_m[n], new_v[n] = _adamw(weights[n], grad_w[n], given["m_" + n], given["v_" + n])
    return (loss, grad_x, *[grad_w[n] for n in TWIN_WEIGHTS], *[delta_w[n] for n in TWIN_WEIGHTS],
            *[new_m[n] for n in TWIN_WEIGHTS], *[new_v[n] for n in TWIN_WEIGHTS])
```

```python
import functools
import math

import jax
import jax.numpy as jnp
import numpy as np
from jax import lax
from jax.experimental import pallas as pl
from jax.experimental.pallas import tpu as pltpu

F32 = jnp.float32
BF16 = jnp.bfloat16
I32 = jnp.int32

D = 1024
EPS = 1e-6
POOL_WIDTH = 2048
POOL_GROUP = 512
GDN_H, GDN_DK, GDN_DV, GDN_C = 8, 128, 256, 64
GDN_QK, GDN_V, GDN_CONV_CH, GDN_IN = 1024, 2048, 4096, 6160
GDN_IN_PAD = 6272
MLA_H, MLA_NOPE, MLA_ROPE, MLA_V = 16, 128, 64, 128
MLA_Q_LORA, MLA_KV_LORA, MLA_WIDTH, MLA_IN = 768, 512, 2048, 3392
MLA_IN_PAD = 4096
MLA_SCALE = (MLA_NOPE + MLA_ROPE) ** -0.5
ROPE_THETA = 10000.0
ADAM_LR, ADAM_B1, ADAM_B2, ADAM_EPS, ADAM_WD, ADAM_STEP = 0.001, 0.9, 0.999, 1e-08, 0.01, 10

VMEM_LIMIT_V7X = 56 * 1024 * 1024
LANES = 128
MESH = pl.DeviceIdType.MESH


def _pc(body, **kw):
    return pl.pallas_call(body, **kw)


def _cparams(sem):
    return pltpu.CompilerParams(dimension_semantics=sem, vmem_limit_bytes=VMEM_LIMIT_V7X)


def _tile(n, cap):
    t = (cap // LANES) * LANES
    while t >= LANES:
        if n % t == 0:
            return t
        t -= LANES
    return n


def _sds(shape, dt):
    return jax.ShapeDtypeStruct(shape, dt)


def mm(a, b, *, ta=False, tb=False, add=None, out_dtype=F32, name):
    if ta:
        K, M = a.shape
    else:
        M, K = a.shape
    if tb:
        N, K2 = b.shape
    else:
        K2, N = b.shape
    assert K == K2, (a.shape, b.shape, ta, tb)
    tm, tn, tk = _tile(M, 1024), _tile(N, 1024), _tile(K, 512)
    nk = K // tk
    a_spec = pl.BlockSpec((tk, tm), lambda i, j, k: (k, i)) if ta else pl.BlockSpec((tm, tk), lambda i, j, k: (i, k))
    b_spec = pl.BlockSpec((tn, tk), lambda i, j, k: (j, k)) if tb else pl.BlockSpec((tk, tn), lambda i, j, k: (k, j))
    o_spec = pl.BlockSpec((tm, tn), lambda i, j, k: (i, j))
    dn = (((0 if ta else 1,), (1 if tb else 0,)), ((), ()))
    has_add = add is not None

    def body(*refs):
        a_ref, b_ref = refs[0], refs[1]
        o_ref, acc = refs[-2], refs[-1]
        k = pl.program_id(2)

        @pl.when(k == 0)
        def _():
            acc[...] = jnp.zeros_like(acc)

        acc[...] += lax.dot_general(a_ref[...].astype(BF16), b_ref[...].astype(BF16), dn, preferred_element_type=F32)

        @pl.when(k == nk - 1)
        def _():
            r = acc[...]
            if has_add:
                r = r + refs[2][...]
            o_ref[...] = r.astype(out_dtype)

    ins = [a, b] + ([add] if has_add else [])
    specs = [a_spec, b_spec] + ([o_spec] if has_add else [])
    return _pc(body, grid=(M // tm, N // tn, nk), in_specs=specs, out_specs=o_spec, out_shape=_sds((M, N), out_dtype),
               scratch_shapes=[pltpu.VMEM((tm, tn), F32)], compiler_params=_cparams(("parallel", "parallel", "arbitrary")),
               name=name)(*ins)


def gmm(kind, a, b, *, G, name):
    S_ = a.shape[0]
    Ka = a.shape[1] // G
    if kind == "tn":
        N = b.shape[1] // G
        tk = _tile(S_, 512)
        nk = S_ // tk

        def body(a_ref, b_ref, o_ref, acc):
            k = pl.program_id(1)

            @pl.when(k == 0)
            def _():
                acc[...] = jnp.zeros_like(acc)

            acc[...] += lax.dot_general(a_ref[...].astype(BF16), b_ref[...].astype(BF16), (((0,), (0,)), ((), ())),
                                        preferred_element_type=F32)

            @pl.when(k == nk - 1)
            def _():
                o_ref[...] = acc[...]

        return _pc(body, grid=(G, nk),
                   in_specs=[pl.BlockSpec((tk, Ka), lambda g, k: (k, g)), pl.BlockSpec((tk, N), lambda g, k: (k, g))],
                   out_specs=pl.BlockSpec((None, Ka, N), lambda g, k: (g, 0, 0)), out_shape=_sds((G, Ka, N), F32),
                   scratch_shapes=[pltpu.VMEM((Ka, N), F32)], compiler_params=_cparams(("parallel", "arbitrary")), name=name)(a, b)
    N = b.shape[2] if kind == "nn" else b.shape[1]
    tm = _tile(S_, 1024)
    dn = (((1,), (0 if kind == "nn" else 1,)), ((), ()))

    def body(a_ref, b_ref, o_ref):
        o_ref[...] = lax.dot_general(a_ref[...].astype(BF16), b_ref[...].astype(BF16), dn, preferred_element_type=F32)

    bshape = (None,) + tuple(b.shape[1:])
    return _pc(body, grid=(G, S_ // tm),
               in_specs=[pl.BlockSpec((tm, Ka), lambda g, i: (i, g)), pl.BlockSpec(bshape, lambda g, i: (g, 0, 0))],
               out_specs=pl.BlockSpec((tm, N), lambda g, i: (i, g)), out_shape=_sds((S_, G * N), F32),
               compiler_params=_cparams(("parallel", "parallel")), name=name)(a, b)


def _rw_spec(ts, w, c, s):
    return pl.BlockSpec((ts, w), lambda j, i: (i, c + j * s))


def _rw_pspec(p, w, c, s):
    return pl.BlockSpec((p.shape[0], w), lambda j, i: (0, c + j * s))


def rowwise(f, tiles, params, outs, *, ncol=1, ts, name):
    S_ = tiles[0][0].shape[0]
    nin = len(tiles) + len(params)

    def body(*refs):
        res = f(pl.program_id(0), *[r[...] for r in refs[:nin]])
        for r, o in zip(refs[nin:], res):
            r[...] = o.astype(r.dtype)

    return _pc(body, grid=(ncol, S_ // ts),
               in_specs=[_rw_spec(ts, w, c, s) for (_, w, c, s) in tiles] + [_rw_pspec(*p) for p in params],
               out_specs=[_rw_spec(ts, w, 0, s) for (w, s, _) in outs],
               out_shape=[_sds((S_, w * (ncol if s else 1)), dt) for (w, s, dt) in outs],
               compiler_params=_cparams(("parallel", "parallel")), name=name)(*[t[0] for t in tiles], *[p[0] for p in params])


def rowwise_bwd(f, tiles, params, cots, *, need, adds=None, ncol=1, ts, name):
    S_ = tiles[0][0].shape[0]
    adds = adds or {}
    nt, npar, nc = len(tiles), len(params), len(cots)
    add_keys = sorted(adds)
    need_idx = [k for k in range(nt) if need[k]]

    def body(*refs):
        j, i = pl.program_id(0), pl.program_id(1)
        vals = [r[...] for r in refs[:nt + npar]]
        cvals = tuple(r[...] for r in refs[nt + npar:nt + npar + nc])
        add_refs = refs[nt + npar + nc:nt + npar + nc + len(add_keys)]
        out_refs = refs[nt + npar + nc + len(add_keys):]
        _, vjp = jax.vjp(lambda *v: tuple(f(j, *v)), *vals)
        grads = vjp(cvals)
        for n, k in enumerate(need_idx):
            g = grads[k]
            if k in adds:
                g = g + add_refs[add_keys.index(k)][...]
            out_refs[n][...] = g
        for n in range(npar):
            ref = out_refs[len(need_idx) + n]
            first = (i == 0) if params[n][3] else jnp.logical_and(i == 0, j == 0)

            @pl.when(first)
            def _():
                ref[...] = jnp.zeros_like(ref)

            ref[...] += grads[nt + n]

    in_specs = ([_rw_spec(ts, w, c, s) for (_, w, c, s) in tiles] + [_rw_pspec(*p) for p in params]
                + [_rw_spec(ts, w, c, s) for (_, w, c, s) in cots] + [_rw_spec(ts, *adds[k][1:]) for k in add_keys])
    out_specs = [_rw_spec(ts, tiles[k][1], 0, tiles[k][3]) for k in need_idx] + [_rw_pspec(p[0], p[1], p[2], p[3]) for p in params]
    out_shape = ([_sds((S_, tiles[k][1] * (ncol if tiles[k][3] else 1)), F32) for k in need_idx]
                 + [_sds(p[0].shape, F32) for p in params])
    res = _pc(body, grid=(ncol, S_ // ts), in_specs=in_specs, out_specs=out_specs, out_shape=out_shape,
              compiler_params=_cparams(("arbitrary", "arbitrary")), name=name)(
        *[t[0] for t in tiles], *[p[0] for p in params], *[c[0] for c in cots], *[adds[k][0] for k in add_keys])
    return list(res[:len(need_idx)]), list(res[len(need_idx):])


def _rms(x, g):
    r = lax.rsqrt(jnp.mean(x * x, axis=-1, keepdims=True) + EPS)
    return x * r * g


def _silu(x):
    return x * jax.nn.sigmoid(x)


@jax.custom_vjp
def _softplus(x):
    return jnp.maximum(x, 0.0) + jnp.log1p(jnp.exp(-jnp.abs(x)))


_softplus.defvjp(lambda x: (_softplus(x), x), lambda x, d: (d * jax.nn.sigmoid(x),))


def f_rms(j, x, g):
    return (_rms(x, g),)


def f_pool_gate(j, pg, gate, scale):
    return (pg * scale * _silu(gate),)


def f_ogate(j, o, gate):
    return (o * _silu(gate),)


def f_gdn_out(j, o, gate, g):
    return (_rms(o, g) * _silu(gate),)


def f_gdn_gates(j, ba, alog, dtb):
    lane = lax.broadcasted_iota(I32, (1, LANES), 1)
    gs, bs = [], []
    for h in range(GDN_H):
        eb = (lane == h).astype(F32)
        ea = (lane == GDN_H + h).astype(F32)
        b = jnp.sum(ba * eb, -1, keepdims=True)
        a = jnp.sum(ba * ea, -1, keepdims=True)
        al = jnp.sum(alog * eb, -1, keepdims=True)
        dt = jnp.sum(dtb * eb, -1, keepdims=True)
        g = -jnp.exp(al) * _softplus(a + dt)
        gs.append(jnp.broadcast_to(g, ba.shape))
        bs.append(jnp.broadcast_to(jax.nn.sigmoid(b), ba.shape))
    return jnp.concatenate(gs, 1), jnp.concatenate(bs, 1)


def _shift_dn(x, k):
    rows = lax.broadcasted_iota(I32, x.shape, 0)
    return jnp.where(rows < k, 0.0, pltpu.roll(x, k, 0))


def _shift_up(x, k):
    n = x.shape[0]
    rows = lax.broadcasted_iota(I32, x.shape, 0)
    return jnp.where(rows >= n - k, 0.0, pltpu.roll(x, n - k, 0))


def _pool_window(j):
    g = lax.div(j, POOL_GROUP // LANES)
    return jnp.where(g == 0, 2.0, jnp.where(g == 1, 4.0, jnp.where(g == 2, 8.0, 16.0))), g


def _pick(g, a2, a4, a8, a16):
    return jnp.where(g == 0, a2, jnp.where(g == 1, a4, jnp.where(g == 2, a8, a16)))


def pool_time_fwd(proj, name):
    S_ = proj.shape[0]

    def body(u_ref, p_ref):
        u = u_ref[...]
        w, g = _pool_window(pl.program_id(0))
        s2 = u + _shift_dn(u, 1)
        s4 = s2 + _shift_dn(s2, 2)
        s8 = s4 + _shift_dn(s4, 4)
        s16 = s8 + _shift_dn(s8, 8)
        t1 = (lax.broadcasted_iota(I32, u.shape, 0) + 1).astype(F32)
        p_ref[...] = (_pick(g, s2, s4, s8, s16) / jnp.minimum(t1, w) - u).astype(p_ref.dtype)

    return _pc(body, grid=(POOL_WIDTH // LANES,), in_specs=[pl.BlockSpec((S_, LANES), lambda j: (0, j))],
               out_specs=pl.BlockSpec((S_, LANES), lambda j: (0, j)), out_shape=_sds((S_, POOL_WIDTH), BF16),
               compiler_params=_cparams(("parallel",)), name=name)(proj)


def pool_time_bwd(dp, name):
    S_ = dp.shape[0]

    def body(dp_ref, du_ref):
        d = dp_ref[...]
        w, g = _pool_window(pl.program_id(0))
        t1 = (lax.broadcasted_iota(I32, d.shape, 0) + 1).astype(F32)
        q = d / jnp.minimum(t1, w)
        r2 = q + _shift_up(q, 1)
        r4 = r2 + _shift_up(r2, 2)
        r8 = r4 + _shift_up(r4, 4)
        r16 = r8 + _shift_up(r8, 8)
        du_ref[...] = _pick(g, r2, r4, r8, r16) - d

    return _pc(body, grid=(POOL_WIDTH // LANES,), in_specs=[pl.BlockSpec((S_, LANES), lambda j: (0, j))],
               out_specs=pl.BlockSpec((S_, LANES), lambda j: (0, j)), out_shape=_sds((S_, POOL_WIDTH), F32),
               compiler_params=_cparams(("parallel",)), name=name)(dp)


def _conv_post(j, a):
    n = a * lax.rsqrt(jnp.sum(a * a, axis=-1, keepdims=True) + EPS)
    nq = GDN_QK // LANES
    return jnp.where(j < nq, n * (GDN_DK ** -0.5), jnp.where(j < 2 * nq, n, a))


def _conv_pre(u, w):
    return w[3:4] * u + w[2:3] * _shift_dn(u, 1) + w[1:2] * _shift_dn(u, 2) + w[0:1] * _shift_dn(u, 3)


def gdn_conv_fwd(proj, conv_w, name):
    S_ = proj.shape[0]

    def body(u_ref, w_ref, o_ref):
        o_ref[...] = _conv_post(pl.program_id(0), _silu(_conv_pre(u_ref[...], w_ref[...])))

    return _pc(body, grid=(GDN_CONV_CH // LANES,),
               in_specs=[pl.BlockSpec((S_, LANES), lambda j: (0, j)), pl.BlockSpec((8, LANES), lambda j: (0, j))],
               out_specs=pl.BlockSpec((S_, LANES), lambda j: (0, j)), out_shape=_sds((S_, GDN_CONV_CH), F32),
               compiler_params=_cparams(("parallel",)), name=name)(proj, conv_w)


def gdn_conv_bwd(proj, conv_w, dq, dk, dv, name):
    S_ = proj.shape[0]
    nq = GDN_QK // LANES

    def body(u_ref, w_ref, dq_ref, dk_ref, dv_ref, du_ref, dw_ref):
        j = pl.program_id(0)
        u, w = u_ref[...], w_ref[...]
        c = _conv_pre(u, w)
        sig = jax.nn.sigmoid(c)
        dout = jnp.where(j < nq, dq_ref[...], jnp.where(j < 2 * nq, dk_ref[...], dv_ref[...]))
        _, vjp = jax.vjp(lambda a: _conv_post(j, a), c * sig)
        dc = vjp(dout)[0] * (sig * (1.0 + c * (1.0 - sig)))
        du_ref[...] = w[3:4] * dc + w[2:3] * _shift_up(dc, 1) + w[1:2] * _shift_up(dc, 2) + w[0:1] * _shift_up(dc, 3)
        rows = lax.broadcasted_iota(I32, (8, LANES), 0)
        dw = jnp.zeros((8, LANES), F32)
        for k in range(4):
            us = u if k == 3 else _shift_dn(u, 3 - k)
            dw = dw + jnp.where(rows == k, jnp.sum(dc * us, axis=0, keepdims=True), 0.0)
        dw_ref[...] = dw

    blk = lambda f: pl.BlockSpec((S_, LANES), f)
    return _pc(body, grid=(GDN_CONV_CH // LANES,),
               in_specs=[blk(lambda j: (0, j)), pl.BlockSpec((8, LANES), lambda j: (0, j)),
                         blk(lambda j: (0, jnp.minimum(j, nq - 1))), blk(lambda j: (0, jnp.clip(j - nq, 0, nq - 1))),
                         blk(lambda j: (0, jnp.clip(j - 2 * nq, 0, 2 * nq - 1)))],
               out_specs=[blk(lambda j: (0, j)), pl.BlockSpec((8, LANES), lambda j: (0, j))],
               out_shape=[_sds((S_, GDN_CONV_CH), F32), _sds((8, GDN_CONV_CH), F32)],
               compiler_params=_cparams(("parallel",)), name=name)(proj, conv_w, dq, dk, dv)


_NN, _NT, _TN = ((1,), (0,)), ((1,), (1,)), ((0,), (0,))


def _dot(a, b, dn, hi):
    if hi:
        return lax.dot_general(a, b, (dn, ((), ())), precision=lax.Precision.HIGHEST, preferred_element_type=F32)
    return lax.dot_general(a.astype(BF16), b.astype(BF16), (dn, ((), ())), preferred_element_type=F32)


def _make_dots(hi):
    @jax.custom_vjp
    def nn(a, b):
        return _dot(a, b, _NN, hi)

    @jax.custom_vjp
    def nt(a, b):
        return _dot(a, b, _NT, hi)

    @jax.custom_vjp
    def tn(a, b):
        return _dot(a, b, _TN, hi)

    nn.defvjp(lambda a, b: (nn(a, b), (a, b)), lambda r, d: (nt(d, r[1]), tn(r[0], d)))
    nt.defvjp(lambda a, b: (nt(a, b), (a, b)), lambda r, d: (nn(d, r[1]), tn(d, r[0])))
    tn.defvjp(lambda a, b: (tn(a, b), (a, b)), lambda r, d: (nt(r[1], d), nn(r[0], d)))
    return nn, nt, tn


_nn_hi, _nt_hi, _tn_hi = _make_dots(True)
_nn_lo, _nt_lo, _tn_lo = _make_dots(False)


def _gdn_chunk(q, k, v, gb, bb, state):
    C = GDN_C
    e0 = (lax.broadcasted_iota(I32, (1, LANES), 1) == 0).astype(F32)
    g1 = jnp.sum(gb * e0, -1, keepdims=True)
    b1 = jnp.sum(bb * e0, -1, keepdims=True)
    ri = lax.broadcasted_iota(I32, (C, C), 0)
    ci = lax.broadcasted_iota(I32, (C, C), 1)
    causal, strict = ri >= ci, ri > ci
    tri, eye = causal.astype(F32), (ri == ci).astype(F32)
    gc_c = _nn_hi(tri, jnp.broadcast_to(g1, (C, C)))
    gc_d = _nn_hi(tri, jnp.broadcast_to(g1, (C, LANES)))
    gr_c = _nn_hi(jnp.ones((C, C), F32), eye * gc_c)
    decay = jnp.where(causal, jnp.exp(jnp.where(causal, gc_c - gr_c, 0.0)), 0.0)
    kb, vb = k * b1, v * b1
    x = -jnp.where(strict, _nt_lo(kb, k) * decay, 0.0)
    ainv, p = eye + x, x
    for _ in range(5):
        p = _nn_hi(p, p)
        ainv = ainv + _nn_hi(ainv, p)
    u = _nn_hi(ainv, vb)
    w = _nn_hi(ainv, kb * jnp.exp(gc_d))
    attn = jnp.where(causal, _nt_lo(q, k) * decay, 0.0)
    v_new = u - _nn_lo(w, state)
    o = _nn_lo(q * jnp.exp(gc_d), state) + _nn_lo(attn, v_new)
    last = lax.broadcasted_iota(I32, (C, LANES), 0) == C - 1
    gl = jnp.sum(jnp.where(last, gc_d, 0.0), axis=0, keepdims=True)
    gl1 = jnp.sum(gl * e0, -1, keepdims=True)
    new_state = state * jnp.exp(gl1) + _tn_lo(k * jnp.exp(gl - gc_d), v_new)
    return o, new_state


def gdn_chunk_fwd(qkv, g_b, beta_b, name):
    S_ = qkv.shape[0]
    N = S_ // GDN_C

    def body(q_ref, k_ref, v_ref, g_ref, b_ref, o_ref, s_ref, state):
        @pl.when(pl.program_id(0) == 0)
        def _():
            state[...] = jnp.zeros_like(state)

        for h in range(GDN_H):
            kk, vv = slice(h * GDN_DK, (h + 1) * GDN_DK), slice(h * GDN_DV, (h + 1) * GDN_DV)
            st = state[h]
            s_ref[0, h] = st
            o, st2 = _gdn_chunk(q_ref[:, kk], k_ref[:, kk], v_ref[:, vv], g_ref[:, kk], b_ref[:, kk], st)
            o_ref[:, vv] = o
            state[h] = st2

    return _pc(body, grid=(N,),
               in_specs=[pl.BlockSpec((GDN_C, GDN_QK), lambda n: (n, 0)), pl.BlockSpec((GDN_C, GDN_QK), lambda n: (n, 1)),
                         pl.BlockSpec((GDN_C, GDN_V), lambda n: (n, 1)), pl.BlockSpec((GDN_C, GDN_QK), lambda n: (n, 0)),
                         pl.BlockSpec((GDN_C, GDN_QK), lambda n: (n, 0))],
               out_specs=[pl.BlockSpec((GDN_C, GDN_V), lambda n: (n, 0)),
                          pl.BlockSpec((1, GDN_H, GDN_DK, GDN_DV), lambda n: (n, 0, 0, 0))],
               out_shape=[_sds((S_, GDN_V), F32), _sds((N, GDN_H, GDN_DK, GDN_DV), F32)],
               scratch_shapes=[pltpu.VMEM((GDN_H, GDN_DK, GDN_DV), F32)],
               compiler_params=_cparams(("arbitrary",)), name=name)(qkv, qkv, qkv, g_b, beta_b)


def gdn_chunk_bwd(qkv, g_b, beta_b, states, do, name):
    S_ = qkv.shape[0]
    N = S_ // GDN_C

    def body(q_ref, k_ref, v_ref, g_ref, b_ref, s_ref, do_ref, dq_ref, dk_ref, dv_ref, dg_ref, db_ref, dstate):
        @pl.when(pl.program_id(0) == 0)
        def _():
            dstate[...] = jnp.zeros_like(dstate)

        for h in range(GDN_H):
            kk, vv = slice(h * GDN_DK, (h + 1) * GDN_DK), slice(h * GDN_DV, (h + 1) * GDN_DV)
            _, vjp = jax.vjp(_gdn_chunk, q_ref[:, kk], k_ref[:, kk], v_ref[:, vv], g_ref[:, kk], b_ref[:, kk], s_ref[0, h])
            dq, dk, dv, dg, db, ds = vjp((do_ref[:, vv], dstate[h]))
            dq_ref[:, kk] = dq
            dk_ref[:, kk] = dk
            dv_ref[:, vv] = dv
            dg_ref[:, kk] = dg
            db_ref[:, kk] = db
            dstate[h] = ds

    r = lambda n: N - 1 - n
    qk = lambda c: pl.BlockSpec((GDN_C, GDN_QK), lambda n: (r(n), c))
    vs = lambda c: pl.BlockSpec((GDN_C, GDN_V), lambda n: (r(n), c))
    return _pc(body, grid=(N,),
               in_specs=[qk(0), qk(1), vs(1), qk(0), qk(0),
                         pl.BlockSpec((1, GDN_H, GDN_DK, GDN_DV), lambda n: (r(n), 0, 0, 0)), vs(0)],
               out_specs=[qk(0), qk(0), vs(0), qk(0), qk(0)],
               out_shape=[_sds((S_, GDN_QK), F32), _sds((S_, GDN_QK), F32), _sds((S_, GDN_V), F32),
                          _sds((S_, GDN_QK), F32), _sds((S_, GDN_QK), F32)],
               scratch_shapes=[pltpu.VMEM((GDN_H, GDN_DK, GDN_DV), F32)],
               compiler_params=_cparams(("arbitrary",)), name=name)(qkv, qkv, qkv, g_b, beta_b, states, do)


def _rope_tables(pos_ref, inv_ref, cm_ref, sg_ref):
    ang = pos_ref[...] * inv_ref[...]
    return jnp.cos(ang) * cm_ref[...], jnp.sin(ang) * sg_ref[...]


def mla_prep_fwd(qpad, kv, proj, pos, rope_consts, name):
    S_ = qpad.shape[0]
    ts = 256
    W = 2 * LANES

    def body(q_ref, kv_ref, kr_ref, pos_ref, inv_ref, cm_ref, sg_ref, qh_ref, kh_ref, vh_ref):
        cs, sn = _rope_tables(pos_ref, inv_ref, cm_ref, sg_ref)
        rope = lambda r: r * cs + pltpu.roll(r, LANES // 2, 1) * sn
        krr = rope(kr_ref[...]).astype(BF16)
        for h in range(MLA_H):
            qh_ref[h, :, 0:LANES] = q_ref[:, h * W:h * W + LANES].astype(BF16)
            qh_ref[h, :, LANES:W] = rope(q_ref[:, h * W + LANES:(h + 1) * W]).astype(BF16)
            kh_ref[h, :, 0:LANES] = kv_ref[:, h * W:h * W + LANES].astype(BF16)
            kh_ref[h, :, LANES:W] = krr
            vh_ref[h] = kv_ref[:, h * W + LANES:(h + 1) * W].astype(BF16)

    one = pl.BlockSpec((1, LANES), lambda i: (0, 0))
    return _pc(body, grid=(S_ // ts,),
               in_specs=[pl.BlockSpec((ts, MLA_H * W), lambda i: (i, 0)), pl.BlockSpec((ts, MLA_H * W), lambda i: (i, 0)),
                         pl.BlockSpec((ts, LANES), lambda i: (i, 1536 // LANES)), pl.BlockSpec((ts, 1), lambda i: (i, 0)),
                         one, one, one],
               out_specs=[pl.BlockSpec((MLA_H, ts, W), lambda i: (0, i, 0)), pl.BlockSpec((MLA_H, ts, W), lambda i: (0, i, 0)),
                          pl.BlockSpec((MLA_H, ts, LANES), lambda i: (0, i, 0))],
               out_shape=[_sds((MLA_H, S_, W), BF16), _sds((MLA_H, S_, W), BF16), _sds((MLA_H, S_, LANES), BF16)],
               compiler_params=_cparams(("parallel",)), name=name)(qpad, kv, proj, pos, *rope_consts)


def mla_prep_bwd(dqh, dkh, dvh, pos, rope_consts, name):
    S_ = dqh.shape[1]
    ts = 256
    W = 2 * LANES

    def body(dq_ref, dk_ref, dv_ref, pos_ref, inv_ref, cm_ref, sg_ref, dqp_ref, dkv_ref, dkr_ref):
        cs, sn = _rope_tables(pos_ref, inv_ref, cm_ref, sg_ref)
        rope_t = lambda g: g * cs + pltpu.roll(g * sn, LANES // 2, 1)
        acc = jnp.zeros((ts, LANES), F32)
        for h in range(MLA_H):
            dqp_ref[:, h * W:h * W + LANES] = dq_ref[h, :, 0:LANES]
            dqp_ref[:, h * W + LANES:(h + 1) * W] = rope_t(dq_ref[h, :, LANES:W])
            dkv_ref[:, h * W:h * W + LANES] = dk_ref[h, :, 0:LANES]
            dkv_ref[:, h * W + LANES:(h + 1) * W] = dv_ref[h]
            acc = acc + dk_ref[h, :, LANES:W]
        dkr_ref[...] = rope_t(acc)

    one = pl.BlockSpec((1, LANES), lambda i: (0, 0))
    return _pc(body, grid=(S_ // ts,),
               in_specs=[pl.BlockSpec((MLA_H, ts, W), lambda i: (0, i, 0)), pl.BlockSpec((MLA_H, ts, W), lambda i: (0, i, 0)),
                         pl.BlockSpec((MLA_H, ts, LANES), lambda i: (0, i, 0)), pl.BlockSpec((ts, 1), lambda i: (i, 0)),
                         one, one, one],
               out_specs=[pl.BlockSpec((ts, MLA_H * W), lambda i: (i, 0)), pl.BlockSpec((ts, MLA_H * W), lambda i: (i, 0)),
                          pl.BlockSpec((ts, LANES), lambda i: (i, 0))],
               out_shape=[_sds((S_, MLA_H * W), F32), _sds((S_, MLA_H * W), F32), _sds((S_, LANES), F32)],
               compiler_params=_cparams(("parallel",)), name=name)(dqh, dkh, dvh, pos, *rope_consts)


NEG = -1e30


def _causal_scores(q, k, qi, kj, t):
    s = lax.dot_general(q, k, ((_NT), ((), ())), preferred_element_type=F32) * MLA_SCALE
    row = qi * t + lax.broadcasted_iota(I32, (t, t), 0)
    col = kj * t + lax.broadcasted_iota(I32, (t, t), 1)
    return jnp.where(col <= row, s, NEG)


def flash_fwd(qh, kh, vh, name):
    H, S_, W = qh.shape
    t = _tile(S_, 512)
    n = S_ // t

    def body(q_ref, k_ref, v_ref, o_ref, lse_ref, m_s, l_s, acc):
        qi, kj = pl.program_id(1), pl.program_id(2)

        @pl.when(kj == 0)
        def _():
            m_s[...] = jnp.full_like(m_s, NEG)
            l_s[...] = jnp.zeros_like(l_s)
            acc[...] = jnp.zeros_like(acc)

        @pl.when(kj <= qi)
        def _():
            s = _causal_scores(q_ref[...], k_ref[...], qi, kj, t)
            m_old = m_s[...]
            m_new = jnp.maximum(m_old, jnp.max(s, axis=-1, keepdims=True))
            alpha = jnp.exp(m_old - m_new)
            p = jnp.exp(s - m_new[:, :1])
            l_s[...] = alpha * l_s[...] + jnp.sum(p, axis=-1, keepdims=True)
            acc[...] = alpha * acc[...] + lax.dot_general(p.astype(BF16), v_ref[...], (_NN, ((), ())), preferred_element_type=F32)
            m_s[...] = m_new

        @pl.when(kj == n - 1)
        def _():
            o_ref[...] = acc[...] / l_s[...]
            lse_ref[...] = m_s[...] + jnp.log(l_s[...])

    return _pc(body, grid=(H, n, n),
               in_specs=[pl.BlockSpec((None, t, W), lambda h, i, j: (h, i, 0)),
                         pl.BlockSpec((None, t, W), lambda h, i, j: (h, jnp.minimum(i, j), 0)),
                         pl.BlockSpec((None, t, LANES), lambda h, i, j: (h, jnp.minimum(i, j), 0))],
               out_specs=[pl.BlockSpec((t, LANES), lambda h, i, j: (i, h)), pl.BlockSpec((None, t, LANES), lambda h, i, j: (h, i, 0))],
               out_shape=[_sds((S_, H * LANES), F32), _sds((H, S_, LANES), F32)],
               scratch_shapes=[pltpu.VMEM((t, LANES), F32)] * 3,
               compiler_params=_cparams(("parallel", "parallel", "arbitrary")), name=name)(qh, kh, vh)


def flash_bwd(qh, kh, vh, o, lse, do, name):
    H, S_, W = qh.shape
    t = _tile(S_, 512)
    n = S_ // t

    def body(q_ref, k_ref, v_ref, o_ref, lse_ref, do_ref, dq_ref, dk_ref, dv_ref):
        kj, qi = pl.program_id(1), pl.program_id(2)

        @pl.when(jnp.logical_and(kj == 0, qi == 0))
        def _():
            dq_ref[...] = jnp.zeros_like(dq_ref)

        @pl.when(qi == 0)
        def _():
            dk_ref[...] = jnp.zeros_like(dk_ref)
            dv_ref[...] = jnp.zeros_like(dv_ref)

        @pl.when(qi >= kj)
        def _():
            q, k, v = q_ref[...], k_ref[...], v_ref[...]
            do_ = do_ref[...]
            p = jnp.exp(_causal_scores(q, k, qi, kj, t) - lse_ref[:, :1])
            dob = do_.astype(BF16)
            dv_ref[...] += lax.dot_general(p.astype(BF16), dob, (_TN, ((), ())), preferred_element_type=F32)
            dp = lax.dot_general(dob, v, (_NT, ((), ())), preferred_element_type=F32)
            delta = jnp.sum(do_ * o_ref[...], axis=-1, keepdims=True)
            ds = (p * (dp - delta) * MLA_SCALE).astype(BF16)
            dk_ref[...] += lax.dot_general(ds, q, (_TN, ((), ())), preferred_element_type=F32)
            rows = pl.ds(pl.multiple_of(qi * t, t), t)
            dq_ref[rows, :] += lax.dot_general(ds, k, (_NN, ((), ())), preferred_element_type=F32)

    qrow = lambda h, j, i: jnp.maximum(i, j)
    return _pc(body, grid=(H, n, n),
               in_specs=[pl.BlockSpec((None, t, W), lambda h, j, i: (h, qrow(h, j, i), 0)),
                         pl.BlockSpec((None, t, W), lambda h, j, i: (h, j, 0)),
                         pl.BlockSpec((None, t, LANES), lambda h, j, i: (h, j, 0)),
                         pl.BlockSpec((t, LANES), lambda h, j, i: (qrow(h, j, i), h)),
                         pl.BlockSpec((None, t, LANES), lambda h, j, i: (h, qrow(h, j, i), 0)),
                         pl.BlockSpec((t, LANES), lambda h, j, i: (qrow(h, j, i), h))],
               out_specs=[pl.BlockSpec((None, S_, W), lambda h, j, i: (h, 0, 0)),
                          pl.BlockSpec((None, t, W), lambda h, j, i: (h, j, 0)),
                          pl.BlockSpec((None, t, LANES), lambda h, j, i: (h, j, 0))],
               out_shape=[_sds((H, S_, W), F32), _sds((H, S_, W), F32), _sds((H, S_, LANES), F32)],
               compiler_params=_cparams(("parallel", "arbitrary", "arbitrary")), name=name)(qh, kh, vh, o, lse, do)


def loss_head(x, target, g, name):
    S_ = x.shape[0]
    ts = 256

    def body(x_ref, t_ref, g_ref, l_ref, dx_ref, dg_ref):
        @pl.when(pl.program_id(0) == 0)
        def _():
            l_ref[...] = jnp.zeros_like(l_ref)
            dg_ref[...] = jnp.zeros_like(dg_ref)

        y, vjp = jax.vjp(_rms, x_ref[...], g_ref[...])
        err = y - t_ref[...]
        l_ref[...] += 0.5 * jnp.sum(jnp.sum(err * err, axis=-1, keepdims=True), axis=0, keepdims=True) / D
        dx, dg = vjp(err / D)
        dx_ref[...] = dx
        dg_ref[...] += dg

    row = pl.BlockSpec((ts, D), lambda i: (i, 0))
    return _pc(body, grid=(S_ // ts,), in_specs=[row, row, pl.BlockSpec((1, D), lambda i: (0, 0))],
               out_specs=[pl.BlockSpec((1, LANES), lambda i: (0, 0)), row, pl.BlockSpec((1, D), lambda i: (0, 0))],
               out_shape=[_sds((1, LANES), F32), _sds((S_, D), F32), _sds((1, D), F32)],
               compiler_params=_cparams(("arbitrary",)), name=name)(x, target, g)


def adamw(w, g, m, v, name):
    R, C = w.shape
    tr = R
    for cand in (512, 256, 128, 64, 32, 16, 8):
        if R % cand == 0 and cand * C * 4 <= 2 * 1024 * 1024:
            tr = cand
            break
    c1 = 1.0 - ADAM_B1 ** ADAM_STEP
    c2 = 1.0 - ADAM_B2 ** ADAM_STEP

    def body(w_ref, g_ref, m_ref, v_ref, d_ref, nm_ref, nv_ref):
        gg = g_ref[...]
        m2 = ADAM_B1 * m_ref[...] + (1.0 - ADAM_B1) * gg
        v2 = ADAM_B2 * v_ref[...] + (1.0 - ADAM_B2) * (gg * gg)
        d_ref[...] = -ADAM_LR * ((m2 / c1) / (jnp.sqrt(v2 / c2) + ADAM_EPS) + ADAM_WD * w_ref[...])
        nm_ref[...] = m2
        nv_ref[...] = v2

    blk = pl.BlockSpec((tr, C), lambda i: (i, 0))
    return _pc(body, grid=(R // tr,), in_specs=[blk] * 4, out_specs=[blk] * 3, out_shape=[_sds((R, C), F32)] * 3,
               compiler_params=_cparams(("parallel",)), name=name)(w, g, m, v)


def sum_slots(own, recv, skip, name):
    n, R, C = recv.shape
    tr = _tile(R, 512) if R % LANES == 0 else R
    has_own = own is not None

    def body(*refs):
        skip_ref = refs[0]
        r_ref, o_ref = refs[-2], refs[-1]
        acc = refs[1][...] if has_own else jnp.zeros(o_ref.shape, F32)
        for s in range(n):
            acc = acc + jnp.where(skip_ref[0] == s, 0.0, r_ref[s].astype(F32))
        o_ref[...] = acc

    row = pl.BlockSpec((tr, C), lambda i, sk: (i, 0))
    gs = pltpu.PrefetchScalarGridSpec(
        num_scalar_prefetch=1, grid=(R // tr,),
        in_specs=([row] if has_own else []) + [pl.BlockSpec((n, tr, C), lambda i, sk: (0, i, 0))], out_specs=row)
    ins = ([own] if has_own else []) + [recv]
    return _pc(body, grid_spec=gs, out_shape=_sds((R, C), F32), compiler_params=_cparams(("parallel",)), name=name)(skip, *ins)


def _chip_peers():
    x, y, c = lax.axis_index("x"), lax.axis_index("y"), lax.axis_index("c")
    return (x, y, c), [(1 - x, y, c), (x, 1 - y, c), (1 - x, 1 - y, c)]


def _chip_index(p):
    return 2 * p[0] + p[1]


def exchange_chips(bufs, per_dest, name):
    nb = len(bufs)

    def body(*refs):
        in_refs, out_refs = refs[:nb], refs[nb:2 * nb]
        send, recv, local = refs[2 * nb:]
        me, peers = _chip_peers()
        mine = _chip_index(me)
        copies = []
        for b in range(nb):
            own_src = in_refs[b].at[mine] if per_dest else in_refs[b]
            loc = pltpu.make_async_copy(own_src, out_refs[b].at[mine], local.at[b])
            loc.start()
            copies.append(loc)
            for k, peer in enumerate(peers):
                src = in_refs[b].at[_chip_index(peer)] if per_dest else in_refs[b]
                cp = pltpu.make_async_remote_copy(src_ref=src, dst_ref=out_refs[b].at[mine], send_sem=send.at[b, k],
                                                  recv_sem=recv.at[b, k], device_id=peer, device_id_type=MESH)
                cp.start()
                copies.append(cp)
        for cp in copies:
            cp.wait()

    anyspec = pl.BlockSpec(memory_space=pl.ANY)
    out_shape = [_sds((4,) + tuple(b.shape[-2:]), b.dtype) for b in bufs]
    return _pc(body, in_specs=[anyspec] * nb, out_specs=[anyspec] * nb, out_shape=out_shape,
               scratch_shapes=[pltpu.SemaphoreType.DMA((nb, 3)), pltpu.SemaphoreType.DMA((nb, 3)), pltpu.SemaphoreType.DMA((nb,))],
               name=name)(*bufs)


def exchange_sibling(buf, name):
    def body(in_ref, out_ref, send, recv):
        x, y, c = lax.axis_index("x"), lax.axis_index("y"), lax.axis_index("c")
        cp = pltpu.make_async_remote_copy(src_ref=in_ref, dst_ref=out_ref, send_sem=send, recv_sem=recv,
                                          device_id=(x, y, 1 - c), device_id_type=MESH)
        cp.start()
        cp.wait()

    anyspec = pl.BlockSpec(memory_space=pl.ANY)
    return _pc(body, in_specs=[anyspec], out_specs=anyspec, out_shape=_sds(buf.shape, buf.dtype),
               scratch_shapes=[pltpu.SemaphoreType.DMA, pltpu.SemaphoreType.DMA],
               name=name)(buf)


def exchange_all(buf, name):
    def body(in_ref, out_ref, send, recv, local):
        x, y, c = lax.axis_index("x"), lax.axis_index("y"), lax.axis_index("c")
        mine = 4 * x + 2 * y + c
        loc = pltpu.make_async_copy(in_ref, out_ref.at[mine], local)
        loc.start()
        copies = [loc]
        for k in range(1, 8):
            peer = (x ^ (k >> 2), y ^ ((k >> 1) & 1), c ^ (k & 1))
            cp = pltpu.make_async_remote_copy(src_ref=in_ref, dst_ref=out_ref.at[mine], send_sem=send.at[k - 1],
                                              recv_sem=recv.at[k - 1], device_id=peer, device_id_type=MESH)
            cp.start()
            copies.append(cp)
        for cp in copies:
            cp.wait()

    anyspec = pl.BlockSpec(memory_space=pl.ANY)
    return _pc(body, in_specs=[anyspec], out_specs=anyspec, out_shape=_sds((8,) + buf.shape, buf.dtype),
               scratch_shapes=[pltpu.SemaphoreType.DMA((7,)), pltpu.SemaphoreType.DMA((7,)), pltpu.SemaphoreType.DMA],
               name=name)(buf)


def _norm_fwd(x, g, name):
    return rowwise(f_rms, [(x, D, 0, 0)], [(g, D, 0, 0)], [(D, 0, BF16)], ts=512, name=name)[0]


def _norm_bwd(x, g, dh, dres, name):
    (dx,), (dg,) = rowwise_bwd(f_rms, [(x, D, 0, 0)], [(g, D, 0, 0)], [(dh, D, 0, 0)], need=[True],
                               adds={0: (dres, D, 0, 0)}, ts=256, name=name)
    return dx, dg


def pool_fwd(x, W, tag):
    h = _norm_fwd(x, W["ng"], tag + "_norm")
    proj = mm(h, W["w_in"], name=tag + "_in")
    p = pool_time_fwd(proj, tag + "_win")
    pg = gmm("nn", p, W["w_grp"], G=4, name=tag + "_grp")
    y = rowwise(f_pool_gate, [(pg, POOL_GROUP, 0, 1), (proj, POOL_GROUP, 4, 1)], [(W["scale"], POOL_GROUP, 0, 1)],
                [(POOL_GROUP, 1, BF16)], ncol=4, ts=512, name=tag + "_gate")[0]
    xn = mm(y, W["w_out"], add=x, name=tag + "_out")
    return xn, (x, h, proj, p, pg, y)


def pool_bwd(dxn, W, saved, tag):
    x, h, proj, p, pg, y = saved
    dy = mm(dxn, W["w_out"], tb=True, name=tag + "_dy")
    g = {"w_out": mm(y, dxn, ta=True, name=tag + "_dwout")}
    (dpg, dgate), (g["scale"],) = rowwise_bwd(
        f_pool_gate, [(pg, POOL_GROUP, 0, 1), (proj, POOL_GROUP, 4, 1)], [(W["scale"], POOL_GROUP, 0, 1)],
        [(dy, POOL_GROUP, 0, 1)], need=[True, True], ncol=4, ts=512, name=tag + "_dgate")
    dp = gmm("nt", dpg, W["w_grp"], G=4, name=tag + "_dp")
    g["w_grp"] = gmm("tn", p, dpg, G=4, name=tag + "_dwgrp")
    du = pool_time_bwd(dp, tag + "_dwin")
    dh = mm(du, W["w_in"][:, :POOL_WIDTH], tb=True, name=tag + "_dh_u")
    dh = mm(dgate, W["w_in"][:, POOL_WIDTH:], tb=True, add=dh, name=tag + "_dh_g")
    g["w_in"] = jnp.concatenate([mm(h, du, ta=True, name=tag + "_dwin_u"), mm(h, dgate, ta=True, name=tag + "_dwin_g")], axis=1)
    dx, g["ng"] = _norm_bwd(x, W["ng"], dh, dxn, tag + "_dnorm")
    return dx, g


def gdn_fwd(x, W, tag):
    h = _norm_fwd(x, W["ng"], tag + "_norm")
    proj = mm(h, W["w_in"], name=tag + "_in")
    qkv = gdn_conv_fwd(proj, W["conv"], tag + "_conv")
    g_b, beta_b = rowwise(f_gdn_gates, [(proj, LANES, 6144 // LANES, 0)], [(W["a_log"], LANES, 0, 0), (W["dt_bias"], LANES, 0, 0)],
                          [(GDN_QK, 0, F32), (GDN_QK, 0, F32)], ts=512, name=tag + "_gates")
    o, states = gdn_chunk_fwd(qkv, g_b, beta_b, tag + "_chunk")
    og = rowwise(f_gdn_out, [(o, GDN_DV, 0, 1), (proj, GDN_DV, 4096 // GDN_DV, 1)], [(W["norm_g"], GDN_DV, 0, 0)],
                 [(GDN_DV, 1, BF16)], ncol=GDN_H, ts=512, name=tag + "_onorm")[0]
    xn = mm(og, W["w_out"], add=x, name=tag + "_out")
    return xn, (x, h, proj, qkv, g_b, beta_b, o, states, og)


def gdn_bwd(dxn, W, saved, tag):
    x, h, proj, qkv, g_b, beta_b, o, states, og = saved
    dog = mm(dxn, W["w_out"], tb=True, name=tag + "_dog")
    g = {"w_out": mm(og, dxn, ta=True, name=tag + "_dwout")}
    (do, dgate), (g["norm_g"],) = rowwise_bwd(
        f_gdn_out, [(o, GDN_DV, 0, 1), (proj, GDN_DV, 4096 // GDN_DV, 1)], [(W["norm_g"], GDN_DV, 0, 0)],
        [(dog, GDN_DV, 0, 1)], need=[True, True], ncol=GDN_H, ts=512, name=tag + "_donorm")
    dq, dk, dv, dg_b, dbeta_b = gdn_chunk_bwd(qkv, g_b, beta_b, states, do, tag + "_dchunk")
    (dba,), (g["a_log"], g["dt_bias"]) = rowwise_bwd(
        f_gdn_gates, [(proj, LANES, 6144 // LANES, 0)], [(W["a_log"], LANES, 0, 0), (W["dt_bias"], LANES, 0, 0)],
        [(dg_b, GDN_QK, 0, 0), (dbeta_b, GDN_QK, 0, 0)], need=[True], ts=256, name=tag + "_dgates")
    dqkv, g["conv"] = gdn_conv_bwd(proj, W["conv"], dq, dk, dv, tag + "_dconv")
    w_in = W["w_in"]
    dh = mm(dqkv, w_in[:, :4096], tb=True, name=tag + "_dh_qkv")
    dh = mm(dgate, w_in[:, 4096:6144], tb=True, add=dh, name=tag + "_dh_gate")
    dh = mm(dba, w_in[:, 6144:], tb=True, add=dh, name=tag + "_dh_ba")
    g["w_in"] = jnp.concatenate([mm(h, dqkv, ta=True, name=tag + "_dwin_qkv"), mm(h, dgate, ta=True, name=tag + "_dwin_gate"),
                                 mm(h, dba, ta=True, name=tag + "_dwin_ba")[:, :2 * GDN_H]], axis=1)
    dx, g["ng"] = _norm_bwd(x, W["ng"], dh, dxn, tag + "_dnorm")
    return dx, g


def mla_fwd(x, pos, W, tag):
    h = _norm_fwd(x, W["ng"], tag + "_norm")
    proj = mm(h, W["w_in"], name=tag + "_in")
    hq = rowwise(f_rms, [(proj, MLA_Q_LORA, 0, 0)], [(W["q_g"], MLA_Q_LORA, 0, 0)], [(MLA_Q_LORA, 0, BF16)], ts=512, name=tag + "_qnorm")[0]
    hkv = rowwise(f_rms, [(proj, MLA_KV_LORA, 2, 0)], [(W["kv_g"], MLA_KV_LORA, 0, 0)], [(MLA_KV_LORA, 0, BF16)], ts=512, name=tag + "_kvnorm")[0]
    qpad = mm(hq, W["w_uq"], name=tag + "_uq")
    kv = mm(hkv, W["w_ukv"], name=tag + "_ukv")
    qh, kh, vh = mla_prep_fwd(qpad, kv, proj, pos, W["rope"], tag + "_prep")
    o, lse = flash_fwd(qh, kh, vh, tag + "_attn")
    og = rowwise(f_ogate, [(o, 512, 0, 1), (proj, 512, 4, 1)], [], [(512, 1, BF16)], ncol=4, ts=512, name=tag + "_ogate")[0]
    xn = mm(og, W["w_out"], add=x, name=tag + "_out")
    return xn, (x, h, proj, hq, hkv, qh, kh, vh, o, lse, og)


def mla_bwd(dxn, pos, W, saved, tag):
    x, h, proj, hq, hkv, qh, kh, vh, o, lse, og = saved
    dog = mm(dxn, W["w_out"], tb=True, name=tag + "_dog")
    g = {"w_out": mm(og, dxn, ta=True, name=tag + "_dwout")}
    (do, dgate), _ = rowwise_bwd(f_ogate, [(o, 512, 0, 1), (proj, 512, 4, 1)], [], [(dog, 512, 0, 1)], need=[True, True],
                                 ncol=4, ts=512, name=tag + "_dogate")
    dqh, dkh, dvh = flash_bwd(qh, kh, vh, o, lse, do, tag + "_dattn")
    dqpad, dkv, dkr = mla_prep_bwd(dqh, dkh, dvh, pos, W["rope"], tag + "_dprep")
    dhq = mm(dqpad, W["w_uq"], tb=True, name=tag + "_dhq")
    g["w_uq"] = mm(hq, dqpad, ta=True, name=tag + "_dwuq")
    dhkv = mm(dkv, W["w_ukv"], tb=True, name=tag + "_dhkv")
    g["w_ukv"] = mm(hkv, dkv, ta=True, name=tag + "_dwukv")
    (dcq,), (g["q_g"],) = rowwise_bwd(f_rms, [(proj, MLA_Q_LORA, 0, 0)], [(W["q_g"], MLA_Q_LORA, 0, 0)], [(dhq, MLA_Q_LORA, 0, 0)],
                                      need=[True], ts=256, name=tag + "_dqnorm")
    (dckv,), (g["kv_g"],) = rowwise_bwd(f_rms, [(proj, MLA_KV_LORA, 2, 0)], [(W["kv_g"], MLA_KV_LORA, 0, 0)], [(dhkv, MLA_KV_LORA, 0, 0)],
                                        need=[True], ts=256, name=tag + "_dkvnorm")
    w_in = W["w_in"]
    dh = mm(dcq, w_in[:, :768], tb=True, name=tag + "_dh_cq")
    dh = mm(dckv, w_in[:, 1024:1536], tb=True, add=dh, name=tag + "_dh_ckv")
    dh = mm(dkr, w_in[:, 1536:1664], tb=True, add=dh, name=tag + "_dh_kr")
    dh = mm(dgate, w_in[:, 2048:], tb=True, add=dh, name=tag + "_dh_gate")
    dwkr = mm(h, dkr, ta=True, name=tag + "_dwin_kr")
    g["w_in"] = jnp.concatenate([mm(h, dcq, ta=True, name=tag + "_dwin_cq"), mm(h, dckv, ta=True, name=tag + "_dwin_ckv"),
                                 dwkr[:, 0:32], dwkr[:, 64:96], mm(h, dgate, ta=True, name=tag + "_dwin_gate")], axis=1)
    dx, g["ng"] = _norm_bwd(x, W["ng"], dh, dxn, tag + "_dnorm")
    return dx, g


def _pad_cols(a, n):
    return jnp.pad(a, ((0, 0), (0, n - a.shape[1])))


def _mla_w_in_layout(w):
    z = lambda n: jnp.zeros((w.shape[0], n), w.dtype)
    kr = w[:, 1280:1344]
    return jnp.concatenate([w[:, :768], z(256), w[:, 768:1280], kr[:, :32], z(32), kr[:, 32:], z(32), z(384), w[:, 1344:]], axis=1)


def _mla_w_uq_layout(w):
    w3 = w.reshape(w.shape[0], MLA_H, MLA_NOPE + MLA_ROPE)
    z = jnp.zeros((w.shape[0], MLA_H, 32), w.dtype)
    return jnp.concatenate([w3[..., :128], w3[..., 128:160], z, w3[..., 160:192], z], axis=-1).reshape(w.shape[0], MLA_H * 256)


def _mla_w_uq_unlayout(g):
    g3 = g.reshape(g.shape[0], MLA_H, 256)
    return jnp.concatenate([g3[..., :128], g3[..., 128:160], g3[..., 192:224]], axis=-1).reshape(g.shape[0], MLA_H * 192)


def _rope_consts():
    half = MLA_ROPE // 2
    inv = ROPE_THETA ** (-jnp.arange(half, dtype=F32) / half)
    z = jnp.zeros((half,), F32)
    o = jnp.ones((half,), F32)
    row = lambda *p: jnp.concatenate(p).reshape(1, LANES)
    return row(inv, z, inv, z), row(o, z, o, z), row(-o, z, o, z)


BIG = ["pool_w_in", "pool_w_grp", "pool_w_out", "gdn_w_in", "gdn_w_out", "mla_w_in", "mla_w_uq", "mla_w_ukv", "mla_w_out"]
BIG_AXIS = {"pool_w_in": 2, "pool_w_grp": 2, "pool_w_out": 1, "gdn_w_in": 2, "gdn_w_out": 1, "mla_w_in": 2, "mla_w_uq": 2,
            "mla_w_ukv": 2, "mla_w_out": 1}
SMALL_SHARDED = ["pool_scale", "gdn_conv", "mla_q_norm_g", "mla_kv_norm_g"]
SMALL_AXIS = {"pool_scale": 1, "gdn_conv": 2, "mla_q_norm_g": 1, "mla_kv_norm_g": 1}
REPLICATED = ["norm_g", "gdn_a_log", "gdn_dt_bias", "gdn_norm_g", "final_g"]
PACK_C = 1024


def _pack(parts, dtype, row_mult):
    flat = jnp.concatenate([p.reshape(-1).astype(dtype) for p in parts])
    rows = -(-flat.shape[0] // PACK_C)
    rows = -(-rows // row_mult) * row_mult
    return jnp.pad(flat, (0, rows * PACK_C - flat.shape[0])).reshape(rows, PACK_C)


def _unpack(buf, shapes):
    lead = buf.shape[:-2]
    flat = buf.reshape(lead + (-1,))
    out, off = [], 0
    for s in shapes:
        n = int(np.prod(s))
        out.append(flat[..., off:off + n].reshape(lead + tuple(s)))
        off += n
    return out


def _unshard(g4, axis):
    a = jnp.moveaxis(g4, 0, axis)
    s = a.shape
    return a.reshape(s[:axis] + (s[axis] * s[axis + 1],) + s[axis + 2:])


def _to_shards(a, axis):
    s = a.shape
    return jnp.moveaxis(a.reshape(s[:axis] + (4, s[axis] // 4) + s[axis + 1:]), axis, 0)


def local_step(x, pos, target, P):
    rope = _rope_consts()
    ng = lambda i: P["norm_g"][i:i + 1]
    pool_w = lambda j, i: dict(ng=ng(i), w_in=P["pool_w_in"][j], w_grp=P["pool_w_grp"][j], scale=P["pool_scale"][j:j + 1],
                               w_out=P["pool_w_out"][j])
    pad_row = lambda a: _pad_cols(a, LANES)
    gdn_w = dict(ng=ng(1), w_in=_pad_cols(P["gdn_w_in"][0], GDN_IN_PAD), conv=jnp.pad(P["gdn_conv"][0], ((0, 4), (0, 0))),
                 a_log=pad_row(P["gdn_a_log"]), dt_bias=pad_row(P["gdn_dt_bias"]), norm_g=P["gdn_norm_g"], w_out=P["gdn_w_out"][0])
    mla_w = dict(ng=ng(2), w_in=_mla_w_in_layout(P["mla_w_in"][0]), q_g=P["mla_q_norm_g"], kv_g=P["mla_kv_norm_g"],
                 w_uq=_mla_w_uq_layout(P["mla_w_uq"][0]), w_ukv=P["mla_w_ukv"][0], w_out=P["mla_w_out"][0], rope=rope)
    w0, w3 = pool_w(0, 0), pool_w(1, 3)

    x1, s0 = pool_fwd(x, w0, "l0")
    x2, s1 = gdn_fwd(x1, gdn_w, "l1")
    x3, s2 = mla_fwd(x2, pos, mla_w, "l2")
    x4, s3 = pool_fwd(x3, w3, "l3")
    loss, dx4, dfinal = loss_head(x4, target, P["final_g"].reshape(1, D), "loss_head")
    dx3, g3 = pool_bwd(dx4, w3, s3, "l3")
    dx2, g2 = mla_bwd(dx3, pos, mla_w, s2, "l2")
    dx1, g1 = gdn_bwd(dx2, gdn_w, s1, "l1")
    dx0, g0 = pool_bwd(dx1, w0, s0, "l0")

    st = lambda a, b: jnp.stack([a, b])
    grads = {
        "norm_g": jnp.concatenate([g0["ng"], g1["ng"], g2["ng"], g3["ng"]], axis=0),
        "pool_w_in": st(g0["w_in"], g3["w_in"]), "pool_w_grp": st(g0["w_grp"], g3["w_grp"]),
        "pool_scale": jnp.concatenate([g0["scale"], g3["scale"]], axis=0), "pool_w_out": st(g0["w_out"], g3["w_out"]),
        "gdn_w_in": g1["w_in"][None], "gdn_conv": g1["conv"][None, :4], "gdn_a_log": g1["a_log"][:, :GDN_H],
        "gdn_dt_bias": g1["dt_bias"][:, :GDN_H], "gdn_norm_g": g1["norm_g"], "gdn_w_out": g1["w_out"][None],
        "mla_w_in": g2["w_in"][None], "mla_q_norm_g": g2["q_g"], "mla_w_uq": _mla_w_uq_unlayout(g2["w_uq"])[None],
        "mla_kv_norm_g": g2["kv_g"], "mla_w_ukv": g2["w_ukv"][None], "mla_w_out": g2["w_out"][None],
        "final_g": dfinal.reshape(D),
    }
    return loss, dx0, grads


NAMES = ["norm_g", "pool_w_in", "pool_w_grp", "pool_scale", "pool_w_out", "gdn_w_in", "gdn_conv", "gdn_a_log", "gdn_dt_bias",
         "gdn_norm_g", "gdn_w_out", "mla_w_in", "mla_q_norm_g", "mla_w_uq", "mla_kv_norm_g", "mla_w_ukv", "mla_w_out", "final_g"]


def kernel(x, positions, norm_g, pool_w_in, pool_w_grp, pool_scale, pool_w_out, gdn_w_in, gdn_conv, gdn_a_log, gdn_dt_bias, gdn_norm_g, gdn_w_out, mla_w_in, mla_q_norm_g, mla_w_uq, mla_kv_norm_g, mla_w_ukv, mla_w_out, final_g, loss_target, m_norm_g, m_pool_w_in, m_pool_w_grp, m_pool_scale, m_pool_w_out, m_gdn_w_in, m_gdn_conv, m_gdn_a_log, m_gdn_dt_bias, m_gdn_norm_g, m_gdn_w_out, m_mla_w_in, m_mla_q_norm_g, m_mla_w_uq, m_mla_kv_norm_g, m_mla_w_ukv, m_mla_w_out, m_final_g, v_norm_g, v_pool_w_in, v_pool_w_grp, v_pool_scale, v_pool_w_out, v_gdn_w_in, v_gdn_conv, v_gdn_a_log, v_gdn_dt_bias, v_gdn_norm_g, v_gdn_w_out, v_mla_w_in, v_mla_q_norm_g, v_mla_w_uq, v_mla_kv_norm_g, v_mla_w_ukv, v_mla_w_out, v_final_g):
    args = locals()
    w = {n: args[n] for n in NAMES}
    m = {n: args["m_" + n] for n in NAMES}
    v = {n: args["v_" + n] for n in NAMES}
    my_chip = (2 * lax.axis_index("x") + lax.axis_index("y")).astype(I32)
    my_dev = (2 * my_chip + lax.axis_index("c")).astype(I32)

    big_shapes = [w[n].shape for n in BIG]
    small_shapes = [w[n].shape for n in SMALL_SHARDED]
    big_all, small_all = exchange_chips([_pack([w[n] for n in BIG], BF16, 512), _pack([w[n] for n in SMALL_SHARDED], F32, 8)],
                                        False, "gather_weights")
    P = {n: w[n] for n in REPLICATED}
    for n, g4 in zip(BIG, _unpack(big_all, big_shapes)):
        P[n] = _unshard(g4, BIG_AXIS[n])
    for n, g4 in zip(SMALL_SHARDED, _unpack(small_all, small_shapes)):
        P[n] = _unshard(g4, SMALL_AXIS[n])

    S_ = x.shape[1]
    loss_part, dx, grads = local_step(x[0], positions.reshape(S_, 1).astype(F32), loss_target[0], P)

    send = jnp.stack([_pack([_to_shards(grads[n], BIG_AXIS[n])[j] for n in BIG], F32, 512) for j in range(4)])
    own = lax.dynamic_index_in_dim(send, my_chip, axis=0, keepdims=False)
    (recv,) = exchange_chips([send.astype(BF16)], True, "scatter_grads")
    part = sum_slots(own, recv, my_chip.reshape(1), "sum_chips")
    both = exchange_sibling(part, "swap_cores")
    big_g = sum_slots(part, both[None], jnp.full((1,), -1, I32), "sum_cores")
    small_names = SMALL_SHARDED + REPLICATED
    small = _pack([grads[n] for n in small_names] + [loss_part], F32, 8)
    small_g = sum_slots(None, exchange_all(small, "gather_small"), jnp.full((1,), -1, I32), "sum_small")

    g_sh = dict(zip(BIG, _unpack(big_g, big_shapes)))
    full_small = _unpack(small_g, [grads[n].shape for n in small_names] + [(1, LANES)])
    for n, a in zip(small_names, full_small[:-1]):
        if n in SMALL_AXIS:
            a = lax.dynamic_index_in_dim(_to_shards(a, SMALL_AXIS[n]), my_chip, axis=0, keepdims=False)
        g_sh[n] = a
    loss = full_small[-1][0, 0]

    out_g, out_d, out_m, out_v = [], [], [], []
    for n in NAMES:
        shp = w[n].shape
        two = (int(np.prod(shp[:-1])), shp[-1]) if len(shp) > 1 else (1, shp[0])
        d_, m_, v_ = adamw(w[n].reshape(two), g_sh[n].reshape(two), m[n].reshape(two), v[n].reshape(two), "adamw_" + n)
        out_g.append(g_sh[n].reshape(shp))
        out_d.append(d_.reshape(shp))
        out_m.append(m_.reshape(shp))
        out_v.append(v_.reshape(shp))
    return (loss, dx[None], *out_g, *out_d, *out_m, *out_v)
```

```python
import functools
import math

import jax
import jax.numpy as jnp
import numpy as np
from jax import lax
from jax.experimental import pallas as pl
from jax.experimental.pallas import tpu as pltpu

F32 = jnp.float32
BF16 = jnp.bfloat16
I32 = jnp.int32

D = 1024
EPS = 1e-6
POOL_WIDTH = 2048
POOL_GROUP = 512
GDN_H, GDN_DK, GDN_DV, GDN_C = 8, 128, 256, 64
GDN_QK, GDN_V, GDN_CONV_CH, GDN_IN = 1024, 2048, 4096, 6160
GDN_IN_PAD = 6272
MLA_H, MLA_NOPE, MLA_ROPE, MLA_V = 16, 128, 64, 128
MLA_Q_LORA, MLA_KV_LORA, MLA_WIDTH, MLA_IN = 768, 512, 2048, 3392
MLA_IN_PAD = 4096
MLA_SCALE = (MLA_NOPE + MLA_ROPE) ** -0.5
ROPE_THETA = 10000.0
ADAM_LR, ADAM_B1, ADAM_B2, ADAM_EPS, ADAM_WD, ADAM_STEP = 0.001, 0.9, 0.999, 1e-08, 0.01, 10

VMEM_LIMIT_V7X = 56 * 1024 * 1024
LANES = 128
MESH = pl.DeviceIdType.MESH


def _pc(body, **kw):
    return pl.pallas_call(body, **kw)


def _cparams(sem):
    return pltpu.CompilerParams(dimension_semantics=sem, vmem_limit_bytes=VMEM_LIMIT_V7X)


def _tile(n, cap):
    t = (cap // LANES) * LANES
    while t >= LANES:
        if n % t == 0:
            return t
        t -= LANES
    return n


def _sds(shape, dt):
    return jax.ShapeDtypeStruct(shape, dt)


def mm(a, b, *, ta=False, tb=False, add=None, out_dtype=F32, name):
    if ta:
        K, M = a.shape
    else:
        M, K = a.shape
    if tb:
        N, K2 = b.shape
    else:
        K2, N = b.shape
    assert K == K2, (a.shape, b.shape, ta, tb)
    tm, tn, tk = _tile(M, 1024), _tile(N, 1024), _tile(K, 1024)
    nk = K // tk
    a_spec = pl.BlockSpec((tk, tm), lambda i, j, k: (k, i)) if ta else pl.BlockSpec((tm, tk), lambda i, j, k: (i, k))
    b_spec = pl.BlockSpec((tn, tk), lambda i, j, k: (j, k)) if tb else pl.BlockSpec((tk, tn), lambda i, j, k: (k, j))
    o_spec = pl.BlockSpec((tm, tn), lambda i, j, k: (i, j))
    dn = (((0 if ta else 1,), (1 if tb else 0,)), ((), ()))
    has_add = add is not None

    def body(*refs):
        a_ref, b_ref = refs[0], refs[1]
        o_ref, acc = refs[-2], refs[-1]
        k = pl.program_id(2)

        @pl.when(k == 0)
        def _():
            acc[...] = jnp.zeros_like(acc)

        acc[...] += lax.dot_general(a_ref[...].astype(BF16), b_ref[...].astype(BF16), dn, preferred_element_type=F32)

        @pl.when(k == nk - 1)
        def _():
            r = acc[...]
            if has_add:
                r = r + refs[2][...]
            o_ref[...] = r.astype(out_dtype)

    ins = [a, b] + ([add] if has_add else [])
    specs = [a_spec, b_spec] + ([o_spec] if has_add else [])
    return _pc(body, grid=(M // tm, N // tn, nk), in_specs=specs, out_specs=o_spec, out_shape=_sds((M, N), out_dtype),
               scratch_shapes=[pltpu.VMEM((tm, tn), F32)], compiler_params=_cparams(("parallel", "parallel", "arbitrary")),
               name=name)(*ins)


def gmm(kind, a, b, *, G, name, out_dtype=F32):
    S_ = a.shape[0]
    Ka = a.shape[1] // G
    if kind == "tn":
        N = b.shape[1] // G
        tk = _tile(S_, 512)
        nk = S_ // tk

        def body(a_ref, b_ref, o_ref, acc):
            k = pl.program_id(1)

            @pl.when(k == 0)
            def _():
                acc[...] = jnp.zeros_like(acc)

            acc[...] += lax.dot_general(a_ref[...].astype(BF16), b_ref[...].astype(BF16), (((0,), (0,)), ((), ())),
                                        preferred_element_type=F32)

            @pl.when(k == nk - 1)
            def _():
                o_ref[...] = acc[...].astype(out_dtype)

        return _pc(body, grid=(G, nk),
                   in_specs=[pl.BlockSpec((tk, Ka), lambda g, k: (k, g)), pl.BlockSpec((tk, N), lambda g, k: (k, g))],
                   out_specs=pl.BlockSpec((None, Ka, N), lambda g, k: (g, 0, 0)), out_shape=_sds((G, Ka, N), out_dtype),
                   scratch_shapes=[pltpu.VMEM((Ka, N), F32)], compiler_params=_cparams(("parallel", "arbitrary")), name=name)(a, b)
    N = b.shape[2] if kind == "nn" else b.shape[1]
    tm = _tile(S_, 1024)
    dn = (((1,), (0 if kind == "nn" else 1,)), ((), ()))

    def body(a_ref, b_ref, o_ref):
        o_ref[...] = lax.dot_general(a_ref[...].astype(BF16), b_ref[...].astype(BF16), dn, preferred_element_type=F32)

    bshape = (None,) + tuple(b.shape[1:])
    return _pc(body, grid=(G, S_ // tm),
               in_specs=[pl.BlockSpec((tm, Ka), lambda g, i: (i, g)), pl.BlockSpec(bshape, lambda g, i: (g, 0, 0))],
               out_specs=pl.BlockSpec((tm, N), lambda g, i: (i, g)), out_shape=_sds((S_, G * N), F32),
               compiler_params=_cparams(("parallel", "parallel")), name=name)(a, b)


def _rw_spec(ts, w, c, s):
    return pl.BlockSpec((ts, w), lambda j, i: (i, c + j * s))


def _rw_pspec(p, w, c, s):
    return pl.BlockSpec((p.shape[0], w), lambda j, i: (0, c + j * s))


def rowwise(f, tiles, params, outs, *, ncol=1, ts, name):
    S_ = tiles[0][0].shape[0]
    nin = len(tiles) + len(params)

    def body(*refs):
        res = f(pl.program_id(0), *[r[...] for r in refs[:nin]])
        for r, o in zip(refs[nin:], res):
            r[...] = o.astype(r.dtype)

    return _pc(body, grid=(ncol, S_ // ts),
               in_specs=[_rw_spec(ts, w, c, s) for (_, w, c, s) in tiles] + [_rw_pspec(*p) for p in params],
               out_specs=[_rw_spec(ts, w, 0, s) for (w, s, _) in outs],
               out_shape=[_sds((S_, w * (ncol if s else 1)), dt) for (w, s, dt) in outs],
               compiler_params=_cparams(("parallel", "parallel")), name=name)(*[t[0] for t in tiles], *[p[0] for p in params])


def rowwise_bwd(f, tiles, params, cots, *, need, adds=None, place=None, ncol=1, ts, name):
    S_ = tiles[0][0].shape[0]
    adds = adds or {}
    place = place or {}
    nt, npar, nc = len(tiles), len(params), len(cots)
    add_keys = sorted(adds)
    need_idx = [k for k in range(nt) if need[k]]
    into_keys = [k for k in need_idx if k in place and not isinstance(place[k][0], int)]
    n_extra = len(add_keys) + len(into_keys)

    def body(*refs):
        j, i = pl.program_id(0), pl.program_id(1)
        vals = [r[...] for r in refs[:nt + npar]]
        cvals = tuple(r[...] for r in refs[nt + npar:nt + npar + nc])
        add_refs = refs[nt + npar + nc:nt + npar + nc + len(add_keys)]
        out_refs = refs[nt + npar + nc + n_extra:]
        _, vjp = jax.vjp(lambda *v: tuple(f(j, *v)), *vals)
        grads = vjp(cvals)
        for n, k in enumerate(need_idx):
            g = grads[k]
            if k in adds:
                g = g + add_refs[add_keys.index(k)][...]
            out_refs[n][...] = g
        for n in range(npar):
            ref = out_refs[len(need_idx) + n]
            first = (i == 0) if params[n][3] else jnp.logical_and(i == 0, j == 0)

            @pl.when(first)
            def _():
                ref[...] = jnp.zeros_like(ref)

            ref[...] += grads[nt + n]

    in_specs = ([_rw_spec(ts, w, c, s) for (_, w, c, s) in tiles] + [_rw_pspec(*p) for p in params]
                + [_rw_spec(ts, w, c, s) for (_, w, c, s) in cots] + [_rw_spec(ts, *adds[k][1:]) for k in add_keys]
                + [pl.BlockSpec(memory_space=pl.ANY) for _ in into_keys])
    out_specs, out_shape, aliases = [], [], {}
    for n, k in enumerate(need_idx):
        w, s = tiles[k][1], tiles[k][3]
        if k in place:
            dst, c0 = place[k]
            total = dst if isinstance(dst, int) else dst.shape[1]
            out_specs.append(_rw_spec(ts, w, c0, s))
            out_shape.append(_sds((S_, total), F32))
            if k in into_keys:
                aliases[nt + npar + nc + len(add_keys) + into_keys.index(k)] = n
        else:
            out_specs.append(_rw_spec(ts, w, 0, s))
            out_shape.append(_sds((S_, w * (ncol if s else 1)), F32))
    out_specs += [_rw_pspec(p[0], p[1], p[2], p[3]) for p in params]
    out_shape += [_sds(p[0].shape, F32) for p in params]
    res = _pc(body, grid=(ncol, S_ // ts), in_specs=in_specs, out_specs=out_specs, out_shape=out_shape,
              input_output_aliases=aliases, compiler_params=_cparams(("arbitrary", "arbitrary")), name=name)(
        *[t[0] for t in tiles], *[p[0] for p in params], *[c[0] for c in cots], *[adds[k][0] for k in add_keys],
        *[place[k][0] for k in into_keys])
    return list(res[:len(need_idx)]), list(res[len(need_idx):])


def _rms(x, g):
    r = lax.rsqrt(jnp.mean(x * x, axis=-1, keepdims=True) + EPS)
    return x * r * g


def _silu(x):
    return x * jax.nn.sigmoid(x)


@jax.custom_vjp
def _softplus(x):
    return jnp.maximum(x, 0.0) + jnp.log1p(jnp.exp(-jnp.abs(x)))


_softplus.defvjp(lambda x: (_softplus(x), x), lambda x, d: (d * jax.nn.sigmoid(x),))


def f_rms(j, x, g):
    return (_rms(x, g),)


def f_pool_gate(j, pg, gate, scale):
    return (pg * scale * _silu(gate),)


def f_ogate(j, o, gate):
    return (o * _silu(gate),)


def f_gdn_out(j, o, gate, g):
    return (_rms(o, g) * _silu(gate),)


def f_gdn_gates(j, ba, alog, dtb):
    lane = lax.broadcasted_iota(I32, (1, LANES), 1)
    gs, bs = [], []
    for h in range(GDN_H):
        eb = (lane == h).astype(F32)
        ea = (lane == GDN_H + h).astype(F32)
        b = jnp.sum(ba * eb, -1, keepdims=True)
        a = jnp.sum(ba * ea, -1, keepdims=True)
        al = jnp.sum(alog * eb, -1, keepdims=True)
        dt = jnp.sum(dtb * eb, -1, keepdims=True)
        g = -jnp.exp(al) * _softplus(a + dt)
        gs.append(jnp.broadcast_to(g, ba.shape))
        bs.append(jnp.broadcast_to(jax.nn.sigmoid(b), ba.shape))
    return jnp.concatenate(gs, 1), jnp.concatenate(bs, 1)


def _shift_dn(x, k):
    rows = lax.broadcasted_iota(I32, x.shape, 0)
    return jnp.where(rows < k, 0.0, pltpu.roll(x, k, 0))


def _shift_up(x, k):
    n = x.shape[0]
    rows = lax.broadcasted_iota(I32, x.shape, 0)
    return jnp.where(rows >= n - k, 0.0, pltpu.roll(x, n - k, 0))


def _pool_window(j):
    g = lax.div(j, POOL_GROUP // LANES)
    return jnp.where(g == 0, 2.0, jnp.where(g == 1, 4.0, jnp.where(g == 2, 8.0, 16.0))), g


def _pick(g, a2, a4, a8, a16):
    return jnp.where(g == 0, a2, jnp.where(g == 1, a4, jnp.where(g == 2, a8, a16)))


def pool_time_fwd(proj, name):
    S_ = proj.shape[0]

    def body(u_ref, p_ref):
        u = u_ref[...]
        w, g = _pool_window(pl.program_id(0))
        s2 = u + _shift_dn(u, 1)
        s4 = s2 + _shift_dn(s2, 2)
        s8 = s4 + _shift_dn(s4, 4)
        s16 = s8 + _shift_dn(s8, 8)
        t1 = (lax.broadcasted_iota(I32, u.shape, 0) + 1).astype(F32)
        p_ref[...] = (_pick(g, s2, s4, s8, s16) / jnp.minimum(t1, w) - u).astype(p_ref.dtype)

    return _pc(body, grid=(POOL_WIDTH // LANES,), in_specs=[pl.BlockSpec((S_, LANES), lambda j: (0, j))],
               out_specs=pl.BlockSpec((S_, LANES), lambda j: (0, j)), out_shape=_sds((S_, POOL_WIDTH), BF16),
               compiler_params=_cparams(("parallel",)), name=name)(proj)


def pool_time_bwd(dp, into, name):
    S_ = dp.shape[0]

    def body(dp_ref, _, du_ref):
        d = dp_ref[...]
        w, g = _pool_window(pl.program_id(0))
        t1 = (lax.broadcasted_iota(I32, d.shape, 0) + 1).astype(F32)
        q = d / jnp.minimum(t1, w)
        r2 = q + _shift_up(q, 1)
        r4 = r2 + _shift_up(r2, 2)
        r8 = r4 + _shift_up(r4, 4)
        r16 = r8 + _shift_up(r8, 8)
        du_ref[...] = _pick(g, r2, r4, r8, r16) - d

    return _pc(body, grid=(POOL_WIDTH // LANES,),
               in_specs=[pl.BlockSpec((S_, LANES), lambda j: (0, j)), pl.BlockSpec(memory_space=pl.ANY)],
               out_specs=pl.BlockSpec((S_, LANES), lambda j: (0, j)), out_shape=_sds(into.shape, F32),
               input_output_aliases={1: 0}, compiler_params=_cparams(("parallel",)), name=name)(dp, into)


def _conv_post(j, a):
    n = a * lax.rsqrt(jnp.sum(a * a, axis=-1, keepdims=True) + EPS)
    nq = GDN_QK // LANES
    return jnp.where(j < nq, n * (GDN_DK ** -0.5), jnp.where(j < 2 * nq, n, a))


def _conv_pre(u, w):
    return w[3:4] * u + w[2:3] * _shift_dn(u, 1) + w[1:2] * _shift_dn(u, 2) + w[0:1] * _shift_dn(u, 3)


def gdn_conv_fwd(proj, conv_w, name):
    S_ = proj.shape[0]

    def body(u_ref, w_ref, o_ref):
        o_ref[...] = _conv_post(pl.program_id(0), _silu(_conv_pre(u_ref[...], w_ref[...])))

    return _pc(body, grid=(GDN_CONV_CH // LANES,),
               in_specs=[pl.BlockSpec((S_, LANES), lambda j: (0, j)), pl.BlockSpec((8, LANES), lambda j: (0, j))],
               out_specs=pl.BlockSpec((S_, LANES), lambda j: (0, j)), out_shape=_sds((S_, GDN_CONV_CH), F32),
               compiler_params=_cparams(("parallel",)), name=name)(proj, conv_w)


def gdn_conv_bwd(proj, conv_w, dq, dk, dv, into, name):
    S_ = proj.shape[0]
    nq = GDN_QK // LANES

    def body(u_ref, w_ref, dq_ref, dk_ref, dv_ref, _, du_ref, dw_ref):
        j = pl.program_id(0)
        u, w = u_ref[...], w_ref[...]
        c = _conv_pre(u, w)
        sig = jax.nn.sigmoid(c)
        dout = jnp.where(j < nq, dq_ref[...], jnp.where(j < 2 * nq, dk_ref[...], dv_ref[...]))
        _, vjp = jax.vjp(lambda a: _conv_post(j, a), c * sig)
        dc = vjp(dout)[0] * (sig * (1.0 + c * (1.0 - sig)))
        du_ref[...] = w[3:4] * dc + w[2:3] * _shift_up(dc, 1) + w[1:2] * _shift_up(dc, 2) + w[0:1] * _shift_up(dc, 3)
        rows = lax.broadcasted_iota(I32, (8, LANES), 0)
        dw = jnp.zeros((8, LANES), F32)
        for k in range(4):
            us = u if k == 3 else _shift_dn(u, 3 - k)
            dw = dw + jnp.where(rows == k, jnp.sum(dc * us, axis=0, keepdims=True), 0.0)
        dw_ref[...] = dw

    blk = lambda f: pl.BlockSpec((S_, LANES), f)
    return _pc(body, grid=(GDN_CONV_CH // LANES,),
               in_specs=[blk(lambda j: (0, j)), pl.BlockSpec((8, LANES), lambda j: (0, j)),
                         blk(lambda j: (0, jnp.minimum(j, nq - 1))), blk(lambda j: (0, jnp.clip(j - nq, 0, nq - 1))),
                         blk(lambda j: (0, jnp.clip(j - 2 * nq, 0, 2 * nq - 1))), pl.BlockSpec(memory_space=pl.ANY)],
               out_specs=[blk(lambda j: (0, j)), pl.BlockSpec((8, LANES), lambda j: (0, j))],
               out_shape=[_sds(into.shape, F32), _sds((8, GDN_CONV_CH), F32)], input_output_aliases={5: 0},
               compiler_params=_cparams(("parallel",)), name=name)(proj, conv_w, dq, dk, dv, into)


_NN, _NT, _TN = ((1,), (0,)), ((1,), (1,)), ((0,), (0,))


def _dot(a, b, dn, hi):
    if hi:
        return lax.dot_general(a, b, (dn, ((), ())), precision=lax.Precision.HIGHEST, preferred_element_type=F32)
    return lax.dot_general(a.astype(BF16), b.astype(BF16), (dn, ((), ())), preferred_element_type=F32)


def _make_dots(hi):
    @jax.custom_vjp
    def nn(a, b):
        return _dot(a, b, _NN, hi)

    @jax.custom_vjp
    def nt(a, b):
        return _dot(a, b, _NT, hi)

    @jax.custom_vjp
    def tn(a, b):
        return _dot(a, b, _TN, hi)

    nn.defvjp(lambda a, b: (nn(a, b), (a, b)), lambda r, d: (nt(d, r[1]), tn(r[0], d)))
    nt.defvjp(lambda a, b: (nt(a, b), (a, b)), lambda r, d: (nn(d, r[1]), tn(d, r[0])))
    tn.defvjp(lambda a, b: (tn(a, b), (a, b)), lambda r, d: (nt(r[1], d), nn(r[0], d)))
    return nn, nt, tn


_nn_hi, _nt_hi, _tn_hi = _make_dots(True)
_nn_lo, _nt_lo, _tn_lo = _make_dots(False)


def _gdn_chunk(q, k, v, gb, bb, state):
    C = GDN_C
    e0 = (lax.broadcasted_iota(I32, (1, LANES), 1) == 0).astype(F32)
    g1 = jnp.sum(gb * e0, -1, keepdims=True)
    b1 = jnp.sum(bb * e0, -1, keepdims=True)
    ri = lax.broadcasted_iota(I32, (C, C), 0)
    ci = lax.broadcasted_iota(I32, (C, C), 1)
    causal, strict = ri >= ci, ri > ci
    tri, eye = causal.astype(F32), (ri == ci).astype(F32)
    gc_c = _nn_hi(tri, jnp.broadcast_to(g1, (C, C)))
    gc_d = _nn_hi(tri, jnp.broadcast_to(g1, (C, LANES)))
    gr_c = _nn_hi(jnp.ones((C, C), F32), eye * gc_c)
    decay = jnp.where(causal, jnp.exp(jnp.where(causal, gc_c - gr_c, 0.0)), 0.0)
    kb, vb = k * b1, v * b1
    x = -jnp.where(strict, _nt_lo(kb, k) * decay, 0.0)
    ainv, p = eye + x, x
    for _ in range(5):
        p = _nn_hi(p, p)
        ainv = ainv + _nn_hi(ainv, p)
    u = _nn_hi(ainv, vb)
    w = _nn_hi(ainv, kb * jnp.exp(gc_d))
    attn = jnp.where(causal, _nt_lo(q, k) * decay, 0.0)
    v_new = u - _nn_lo(w, state)
    o = _nn_lo(q * jnp.exp(gc_d), state) + _nn_lo(attn, v_new)
    last = lax.broadcasted_iota(I32, (C, LANES), 0) == C - 1
    gl = jnp.sum(jnp.where(last, gc_d, 0.0), axis=0, keepdims=True)
    gl1 = jnp.sum(gl * e0, -1, keepdims=True)
    new_state = state * jnp.exp(gl1) + _tn_lo(k * jnp.exp(gl - gc_d), v_new)
    return o, new_state


def gdn_chunk_fwd(qkv, g_b, beta_b, name):
    S_ = qkv.shape[0]
    N = S_ // GDN_C

    def body(q_ref, k_ref, v_ref, g_ref, b_ref, o_ref, s_ref, state):
        @pl.when(pl.program_id(0) == 0)
        def _():
            state[...] = jnp.zeros_like(state)

        for h in range(GDN_H):
            kk, vv = slice(h * GDN_DK, (h + 1) * GDN_DK), slice(h * GDN_DV, (h + 1) * GDN_DV)
            st = state[h]
            s_ref[0, h] = st
            o, st2 = _gdn_chunk(q_ref[:, kk], k_ref[:, kk], v_ref[:, vv], g_ref[:, kk], b_ref[:, kk], st)
            o_ref[:, vv] = o
            state[h] = st2

    return _pc(body, grid=(N,),
               in_specs=[pl.BlockSpec((GDN_C, GDN_QK), lambda n: (n, 0)), pl.BlockSpec((GDN_C, GDN_QK), lambda n: (n, 1)),
                         pl.BlockSpec((GDN_C, GDN_V), lambda n: (n, 1)), pl.BlockSpec((GDN_C, GDN_QK), lambda n: (n, 0)),
                         pl.BlockSpec((GDN_C, GDN_QK), lambda n: (n, 0))],
               out_specs=[pl.BlockSpec((GDN_C, GDN_V), lambda n: (n, 0)),
                          pl.BlockSpec((1, GDN_H, GDN_DK, GDN_DV), lambda n: (n, 0, 0, 0))],
               out_shape=[_sds((S_, GDN_V), F32), _sds((N, GDN_H, GDN_DK, GDN_DV), F32)],
               scratch_shapes=[pltpu.VMEM((GDN_H, GDN_DK, GDN_DV), F32)],
               compiler_params=_cparams(("arbitrary",)), name=name)(qkv, qkv, qkv, g_b, beta_b)


def gdn_chunk_bwd(qkv, g_b, beta_b, states, do, name):
    S_ = qkv.shape[0]
    N = S_ // GDN_C

    def body(q_ref, k_ref, v_ref, g_ref, b_ref, s_ref, do_ref, dq_ref, dk_ref, dv_ref, dg_ref, db_ref, dstate):
        @pl.when(pl.program_id(0) == 0)
        def _():
            dstate[...] = jnp.zeros_like(dstate)

        for h in range(GDN_H):
            kk, vv = slice(h * GDN_DK, (h + 1) * GDN_DK), slice(h * GDN_DV, (h + 1) * GDN_DV)
            _, vjp = jax.vjp(_gdn_chunk, q_ref[:, kk], k_ref[:, kk], v_ref[:, vv], g_ref[:, kk], b_ref[:, kk], s_ref[0, h])
            dq, dk, dv, dg, db, ds = vjp((do_ref[:, vv], dstate[h]))
            dq_ref[:, kk] = dq
            dk_ref[:, kk] = dk
            dv_ref[:, vv] = dv
            dg_ref[:, kk] = dg
            db_ref[:, kk] = db
            dstate[h] = ds

    r = lambda n: N - 1 - n
    qk = lambda c: pl.BlockSpec((GDN_C, GDN_QK), lambda n: (r(n), c))
    vs = lambda c: pl.BlockSpec((GDN_C, GDN_V), lambda n: (r(n), c))
    return _pc(body, grid=(N,),
               in_specs=[qk(0), qk(1), vs(1), qk(0), qk(0),
                         pl.BlockSpec((1, GDN_H, GDN_DK, GDN_DV), lambda n: (r(n), 0, 0, 0)), vs(0)],
               out_specs=[qk(0), qk(0), vs(0), qk(0), qk(0)],
               out_shape=[_sds((S_, GDN_QK), F32), _sds((S_, GDN_QK), F32), _sds((S_, GDN_V), F32),
                          _sds((S_, GDN_QK), F32), _sds((S_, GDN_QK), F32)],
               scratch_shapes=[pltpu.VMEM((GDN_H, GDN_DK, GDN_DV), F32)],
               compiler_params=_cparams(("arbitrary",)), name=name)(qkv, qkv, qkv, g_b, beta_b, states, do)


def _rope_tables(pos_ref, inv_ref, cm_ref, sg_ref):
    ang = pos_ref[...] * inv_ref[...]
    return jnp.cos(ang) * cm_ref[...], jnp.sin(ang) * sg_ref[...]


def mla_prep_fwd(qpad, kv, proj, pos, rope_consts, name):
    S_ = qpad.shape[0]
    ts = 256
    W = 2 * LANES

    def body(q_ref, kv_ref, kr_ref, pos_ref, inv_ref, cm_ref, sg_ref, qh_ref, kh_ref, vh_ref):
        cs, sn = _rope_tables(pos_ref, inv_ref, cm_ref, sg_ref)
        rope = lambda r: r * cs + pltpu.roll(r, LANES // 2, 1) * sn
        krr = rope(kr_ref[...]).astype(BF16)
        for h in range(MLA_H):
            qh_ref[h, :, 0:LANES] = q_ref[:, h * W:h * W + LANES].astype(BF16)
            qh_ref[h, :, LANES:W] = rope(q_ref[:, h * W + LANES:(h + 1) * W]).astype(BF16)
            kh_ref[h, :, 0:LANES] = kv_ref[:, h * W:h * W + LANES].astype(BF16)
            kh_ref[h, :, LANES:W] = krr
            vh_ref[h] = kv_ref[:, h * W + LANES:(h + 1) * W].astype(BF16)

    one = pl.BlockSpec((1, LANES), lambda i: (0, 0))
    return _pc(body, grid=(S_ // ts,),
               in_specs=[pl.BlockSpec((ts, MLA_H * W), lambda i: (i, 0)), pl.BlockSpec((ts, MLA_H * W), lambda i: (i, 0)),
                         pl.BlockSpec((ts, LANES), lambda i: (i, 1536 // LANES)), pl.BlockSpec((ts, 1), lambda i: (i, 0)),
                         one, one, one],
               out_specs=[pl.BlockSpec((MLA_H, ts, W), lambda i: (0, i, 0)), pl.BlockSpec((MLA_H, ts, W), lambda i: (0, i, 0)),
                          pl.BlockSpec((MLA_H, ts, LANES), lambda i: (0, i, 0))],
               out_shape=[_sds((MLA_H, S_, W), BF16), _sds((MLA_H, S_, W), BF16), _sds((MLA_H, S_, LANES), BF16)],
               compiler_params=_cparams(("parallel",)), name=name)(qpad, kv, proj, pos, *rope_consts)


def mla_prep_bwd(dqh, dkh, dvh, pos, rope_consts, into, name):
    S_ = dqh.shape[1]
    ts = 256
    W = 2 * LANES

    def body(dq_ref, dk_ref, dv_ref, pos_ref, inv_ref, cm_ref, sg_ref, _, dqp_ref, dkv_ref, dkr_ref):
        cs, sn = _rope_tables(pos_ref, inv_ref, cm_ref, sg_ref)
        rope_t = lambda g: g * cs + pltpu.roll(g * sn, LANES // 2, 1)
        acc = jnp.zeros((ts, LANES), F32)
        for h in range(MLA_H):
            dqp_ref[:, h * W:h * W + LANES] = dq_ref[h, :, 0:LANES]
            dqp_ref[:, h * W + LANES:(h + 1) * W] = rope_t(dq_ref[h, :, LANES:W])
            dkv_ref[:, h * W:h * W + LANES] = dk_ref[h, :, 0:LANES]
            dkv_ref[:, h * W + LANES:(h + 1) * W] = dv_ref[h]
            acc = acc + dk_ref[h, :, LANES:W]
        dkr_ref[...] = rope_t(acc)

    one = pl.BlockSpec((1, LANES), lambda i: (0, 0))
    return _pc(body, grid=(S_ // ts,),
               in_specs=[pl.BlockSpec((MLA_H, ts, W), lambda i: (0, i, 0)), pl.BlockSpec((MLA_H, ts, W), lambda i: (0, i, 0)),
                         pl.BlockSpec((MLA_H, ts, LANES), lambda i: (0, i, 0)), pl.BlockSpec((ts, 1), lambda i: (i, 0)),
                         one, one, one, pl.BlockSpec(memory_space=pl.ANY)],
               out_specs=[pl.BlockSpec((ts, MLA_H * W), lambda i: (i, 0)), pl.BlockSpec((ts, MLA_H * W), lambda i: (i, 0)),
                          pl.BlockSpec((ts, LANES), lambda i: (i, 1536 // LANES))],
               out_shape=[_sds((S_, MLA_H * W), F32), _sds((S_, MLA_H * W), F32), _sds(into.shape, F32)],
               input_output_aliases={7: 2}, compiler_params=_cparams(("parallel",)), name=name)(dqh, dkh, dvh, pos, *rope_consts, into)


NEG = -1e30


def _causal_scores(q, k, qi, kj, t):
    s = lax.dot_general(q, k, ((_NT), ((), ())), preferred_element_type=F32) * MLA_SCALE
    row = qi * t + lax.broadcasted_iota(I32, (t, t), 0)
    col = kj * t + lax.broadcasted_iota(I32, (t, t), 1)
    return jnp.where(col <= row, s, NEG)


def flash_fwd(qh, kh, vh, name):
    H, S_, W = qh.shape
    t = _tile(S_, 512)
    n = S_ // t

    def body(q_ref, k_ref, v_ref, o_ref, lse_ref, m_s, l_s, acc):
        qi, kj = pl.program_id(1), pl.program_id(2)

        @pl.when(kj == 0)
        def _():
            m_s[...] = jnp.full_like(m_s, NEG)
            l_s[...] = jnp.zeros_like(l_s)
            acc[...] = jnp.zeros_like(acc)

        @pl.when(kj <= qi)
        def _():
            s = _causal_scores(q_ref[...], k_ref[...], qi, kj, t)
            m_old = m_s[...]
            m_new = jnp.maximum(m_old, jnp.max(s, axis=-1, keepdims=True))
            alpha = jnp.exp(m_old - m_new)
            p = jnp.exp(s - m_new[:, :1])
            l_s[...] = alpha * l_s[...] + jnp.sum(p, axis=-1, keepdims=True)
            acc[...] = alpha * acc[...] + lax.dot_general(p.astype(BF16), v_ref[...], (_NN, ((), ())), preferred_element_type=F32)
            m_s[...] = m_new

        @pl.when(kj == n - 1)
        def _():
            o_ref[...] = acc[...] / l_s[...]
            lse_ref[...] = m_s[...] + jnp.log(l_s[...])

    return _pc(body, grid=(H, n, n),
               in_specs=[pl.BlockSpec((None, t, W), lambda h, i, j: (h, i, 0)),
                         pl.BlockSpec((None, t, W), lambda h, i, j: (h, jnp.minimum(i, j), 0)),
                         pl.BlockSpec((None, t, LANES), lambda h, i, j: (h, jnp.minimum(i, j), 0))],
               out_specs=[pl.BlockSpec((t, LANES), lambda h, i, j: (i, h)), pl.BlockSpec((None, t, LANES), lambda h, i, j: (h, i, 0))],
               out_shape=[_sds((S_, H * LANES), F32), _sds((H, S_, LANES), F32)],
               scratch_shapes=[pltpu.VMEM((t, LANES), F32)] * 3,
               compiler_params=_cparams(("parallel", "parallel", "arbitrary")), name=name)(qh, kh, vh)


def flash_bwd(qh, kh, vh, o, lse, do, name):
    H, S_, W = qh.shape
    t = _tile(S_, 512)
    n = S_ // t

    def body(q_ref, k_ref, v_ref, o_ref, lse_ref, do_ref, dq_ref, dk_ref, dv_ref):
        kj, qi = pl.program_id(1), pl.program_id(2)

        @pl.when(jnp.logical_and(kj == 0, qi == 0))
        def _():
            dq_ref[...] = jnp.zeros_like(dq_ref)

        @pl.when(qi == 0)
        def _():
            dk_ref[...] = jnp.zeros_like(dk_ref)
            dv_ref[...] = jnp.zeros_like(dv_ref)

        @pl.when(qi >= kj)
        def _():
            q, k, v = q_ref[...], k_ref[...], v_ref[...]
            do_ = do_ref[...]
            p = jnp.exp(_causal_scores(q, k, qi, kj, t) - lse_ref[:, :1])
            dob = do_.astype(BF16)
            dv_ref[...] += lax.dot_general(p.astype(BF16), dob, (_TN, ((), ())), preferred_element_type=F32)
            dp = lax.dot_general(dob, v, (_NT, ((), ())), preferred_element_type=F32)
            delta = jnp.sum(do_ * o_ref[...], axis=-1, keepdims=True)
            ds = (p * (dp - delta) * MLA_SCALE).astype(BF16)
            dk_ref[...] += lax.dot_general(ds, q, (_TN, ((), ())), preferred_element_type=F32)
            rows = pl.ds(pl.multiple_of(qi * t, t), t)
            dq_ref[rows, :] += lax.dot_general(ds, k, (_NN, ((), ())), preferred_element_type=F32)

    qrow = lambda h, j, i: jnp.maximum(i, j)
    return _pc(body, grid=(H, n, n),
               in_specs=[pl.BlockSpec((None, t, W), lambda h, j, i: (h, qrow(h, j, i), 0)),
                         pl.BlockSpec((None, t, W), lambda h, j, i: (h, j, 0)),
                         pl.BlockSpec((None, t, LANES), lambda h, j, i: (h, j, 0)),
                         pl.BlockSpec((t, LANES), lambda h, j, i: (qrow(h, j, i), h)),
                         pl.BlockSpec((None, t, LANES), lambda h, j, i: (h, qrow(h, j, i), 0)),
                         pl.BlockSpec((t, LANES), lambda h, j, i: (qrow(h, j, i), h))],
               out_specs=[pl.BlockSpec((None, S_, W), lambda h, j, i: (h, 0, 0)),
                          pl.BlockSpec((None, t, W), lambda h, j, i: (h, j, 0)),
                          pl.BlockSpec((None, t, LANES), lambda h, j, i: (h, j, 0))],
               out_shape=[_sds((H, S_, W), F32), _sds((H, S_, W), F32), _sds((H, S_, LANES), F32)],
               compiler_params=_cparams(("parallel", "arbitrary", "arbitrary")), name=name)(qh, kh, vh, o, lse, do)


def loss_head(x, target, g, name):
    S_ = x.shape[0]
    ts = 256

    def body(x_ref, t_ref, g_ref, l_ref, dx_ref, dg_ref):
        @pl.when(pl.program_id(0) == 0)
        def _():
            l_ref[...] = jnp.zeros_like(l_ref)
            dg_ref[...] = jnp.zeros_like(dg_ref)

        y, vjp = jax.vjp(_rms, x_ref[...], g_ref[...])
        err = y - t_ref[...]
        l_ref[...] += 0.5 * jnp.sum(jnp.sum(err * err, axis=-1, keepdims=True), axis=0, keepdims=True) / D
        dx, dg = vjp(err / D)
        dx_ref[...] = dx
        dg_ref[...] += dg

    row = pl.BlockSpec((ts, D), lambda i: (i, 0))
    return _pc(body, grid=(S_ // ts,), in_specs=[row, row, pl.BlockSpec((1, D), lambda i: (0, 0))],
               out_specs=[pl.BlockSpec((1, LANES), lambda i: (0, 0)), row, pl.BlockSpec((1, D), lambda i: (0, 0))],
               out_shape=[_sds((1, LANES), F32), _sds((S_, D), F32), _sds((1, D), F32)],
               compiler_params=_cparams(("arbitrary",)), name=name)(x, target, g)


def adamw(w, parts, m, v, name):
    R, C = w.shape
    tr = R
    for cand in (512, 256, 128, 64, 32, 16, 8):
        if R % cand == 0 and cand * C * 4 <= 1024 * 1024:
            tr = cand
            break
    c1 = 1.0 - ADAM_B1 ** ADAM_STEP
    c2 = 1.0 - ADAM_B2 ** ADAM_STEP
    npart = len(parts)

    def body(*refs):
        w_ref, m_ref, v_ref = refs[0], refs[1 + npart], refs[2 + npart]
        g_ref, d_ref, nm_ref, nv_ref = refs[3 + npart:]
        gg = None
        for p_ref in refs[1:1 + npart]:
            s = p_ref[0].astype(F32)
            for n in range(1, p_ref.shape[0]):
                s = s + p_ref[n].astype(F32)
            gg = s if gg is None else gg + s
        m2 = ADAM_B1 * m_ref[...] + (1.0 - ADAM_B1) * gg
        v2 = ADAM_B2 * v_ref[...] + (1.0 - ADAM_B2) * (gg * gg)
        g_ref[...] = gg
        d_ref[...] = -ADAM_LR * ((m2 / c1) / (jnp.sqrt(v2 / c2) + ADAM_EPS) + ADAM_WD * w_ref[...])
        nm_ref[...] = m2
        nv_ref[...] = v2

    blk = pl.BlockSpec((tr, C), lambda i: (i, 0))
    pblk = [pl.BlockSpec((p.shape[0], tr, C), lambda i: (0, i, 0)) for p in parts]
    return _pc(body, grid=(R // tr,), in_specs=[blk] + pblk + [blk, blk], out_specs=[blk] * 4, out_shape=[_sds((R, C), F32)] * 4,
               compiler_params=_cparams(("parallel",)), name=name)(w, *parts, m, v)


def sum_slots(own, recv, skip, name):
    n, R, C = recv.shape
    tr = _tile(R, 512) if R % LANES == 0 else R
    has_own = own is not None

    def body(*refs):
        skip_ref = refs[0]
        r_ref, o_ref = refs[-2], refs[-1]
        acc = refs[1][...] if has_own else jnp.zeros(o_ref.shape, F32)
        for s in range(n):
            acc = acc + jnp.where(skip_ref[0] == s, 0.0, r_ref[s].astype(F32))
        o_ref[...] = acc

    row = pl.BlockSpec((tr, C), lambda i, sk: (i, 0))
    gs = pltpu.PrefetchScalarGridSpec(
        num_scalar_prefetch=1, grid=(R // tr,),
        in_specs=([row] if has_own else []) + [pl.BlockSpec((n, tr, C), lambda i, sk: (0, i, 0))], out_specs=row)
    ins = ([own] if has_own else []) + [recv]
    return _pc(body, grid_spec=gs, out_shape=_sds((R, C), F32), compiler_params=_cparams(("parallel",)), name=name)(skip, *ins)


def _chip_peers():
    x, y, c = lax.axis_index("x"), lax.axis_index("y"), lax.axis_index("c")
    return (x, y, c), [(1 - x, y, c), (x, 1 - y, c), (1 - x, 1 - y, c)]


def _chip_index(p):
    return 2 * p[0] + p[1]


def _win(ref, axis, chip, size):
    if axis is None:
        return ref.at[chip]
    idx = [slice(None)] * len(ref.shape)
    idx[axis] = pl.ds(pl.multiple_of(chip * size, size), size)
    return ref.at[tuple(idx)]


def _remote(src, dst, send_sem, recv_sem, peer):
    return pltpu.make_async_remote_copy(src_ref=src, dst_ref=dst, send_sem=send_sem, recv_sem=recv_sem, device_id=peer,
                                        device_id_type=MESH)


def _exchange_call(body, ins, out_shape, ncopies, name):
    anyspec = pl.BlockSpec(memory_space=pl.ANY)
    return _pc(body, in_specs=[anyspec] * len(ins), out_specs=[anyspec] * len(out_shape), out_shape=out_shape,
               scratch_shapes=[pltpu.SemaphoreType.DMA((ncopies, 3)), pltpu.SemaphoreType.DMA((ncopies, 3)),
                               pltpu.SemaphoreType.DMA((ncopies,))], name=name)(*ins)


def gather_weights(items, name):
    n_in = len(items)
    pieces = [(b, l) for b, it in enumerate(items) for l in range(it[0].shape[0])]
    ncp = len(pieces)

    def body(*refs):
        in_refs, out_refs = refs[:n_in], refs[n_in:n_in + ncp]
        send, recv, local = refs[n_in + ncp:]
        me, peers = _chip_peers()
        mine = _chip_index(me)
        copies = []
        for n, (b, l) in enumerate(pieces):
            src = in_refs[b].at[l]
            dst = _win(out_refs[n], items[b][2], mine, items[b][3])
            copies.append(pltpu.make_async_copy(src, dst, local.at[n]))
            copies += [_remote(src, dst, send.at[n, k], recv.at[n, k], peer) for k, peer in enumerate(peers)]
        for cp in copies:
            cp.start()
        for cp in copies:
            cp.wait()

    out_shape = [_sds(items[b][1], items[b][0].dtype) for (b, l) in pieces]
    return _exchange_call(body, [it[0] for it in items], out_shape, ncp, name)


def scatter_grads(items, name):
    pieces = [(b, l) for b, it in enumerate(items) for l in range(len(it[0]))]
    ncp = len(pieces)

    def body(*refs):
        in_refs, out_refs = refs[:ncp], refs[ncp:ncp + len(items)]
        send, recv, local = refs[ncp + len(items):]
        me, peers = _chip_peers()
        mine = _chip_index(me)
        copies = []
        for n, (b, l) in enumerate(pieces):
            _, axis, size, _ = items[b]
            dst = out_refs[b].at[mine, l]
            copies.append(pltpu.make_async_copy(_win(in_refs[n], axis, mine, size), dst, local.at[n]))
            copies += [_remote(_win(in_refs[n], axis, _chip_index(peer), size), dst, send.at[n, k], recv.at[n, k], peer)
                       for k, peer in enumerate(peers)]
        for cp in copies:
            cp.start()
        for cp in copies:
            cp.wait()

    out_shape = [_sds((4, len(it[0])) + tuple(it[3]), it[0][0].dtype) for it in items]
    return _exchange_call(body, [p for it in items for p in it[0]], out_shape, ncp, name)


def swap_cores(bufs, name):
    nb = len(bufs)

    def body(*refs):
        in_refs, out_refs = refs[:nb], refs[nb:2 * nb]
        send, recv = refs[2 * nb:]
        x, y, c = lax.axis_index("x"), lax.axis_index("y"), lax.axis_index("c")
        copies = [_remote(in_refs[b], out_refs[b], send.at[b], recv.at[b], (x, y, 1 - c)) for b in range(nb)]
        for cp in copies:
            cp.start()
        for cp in copies:
            cp.wait()

    anyspec = pl.BlockSpec(memory_space=pl.ANY)
    return _pc(body, in_specs=[anyspec] * nb, out_specs=[anyspec] * nb, out_shape=[_sds(b.shape, b.dtype) for b in bufs],
               scratch_shapes=[pltpu.SemaphoreType.DMA((nb,)), pltpu.SemaphoreType.DMA((nb,))], name=name)(*bufs)


def exchange_all(buf, name):
    def body(in_ref, out_ref, send, recv, local):
        x, y, c = lax.axis_index("x"), lax.axis_index("y"), lax.axis_index("c")
        mine = 4 * x + 2 * y + c
        loc = pltpu.make_async_copy(in_ref, out_ref.at[mine], local)
        loc.start()
        copies = [loc]
        for k in range(1, 8):
            peer = (x ^ (k >> 2), y ^ ((k >> 1) & 1), c ^ (k & 1))
            cp = pltpu.make_async_remote_copy(src_ref=in_ref, dst_ref=out_ref.at[mine], send_sem=send.at[k - 1],
                                              recv_sem=recv.at[k - 1], device_id=peer, device_id_type=MESH)
            cp.start()
            copies.append(cp)
        for cp in copies:
            cp.wait()

    anyspec = pl.BlockSpec(memory_space=pl.ANY)
    return _pc(body, in_specs=[anyspec], out_specs=anyspec, out_shape=_sds((8,) + buf.shape, buf.dtype),
               scratch_shapes=[pltpu.SemaphoreType.DMA((7,)), pltpu.SemaphoreType.DMA((7,)), pltpu.SemaphoreType.DMA],
               name=name)(buf)


def _norm_fwd(x, g, name):
    return rowwise(f_rms, [(x, D, 0, 0)], [(g, D, 0, 0)], [(D, 0, BF16)], ts=512, name=name)[0]


def _norm_bwd(x, g, dh, dres, name):
    (dx,), (dg,) = rowwise_bwd(f_rms, [(x, D, 0, 0)], [(g, D, 0, 0)], [(dh, D, 0, 0)], need=[True],
                               adds={0: (dres, D, 0, 0)}, ts=256, name=name)
    return dx, dg


def pool_fwd(x, W, tag):
    h = _norm_fwd(x, W["ng"], tag + "_norm")
    proj = mm(h, W["w_in"], name=tag + "_in")
    p = pool_time_fwd(proj, tag + "_win")
    pg = gmm("nn", p, W["w_grp"], G=4, name=tag + "_grp")
    y = rowwise(f_pool_gate, [(pg, POOL_GROUP, 0, 1), (proj, POOL_GROUP, 4, 1)], [(W["scale"], POOL_GROUP, 0, 1)],
                [(POOL_GROUP, 1, BF16)], ncol=4, ts=512, name=tag + "_gate")[0]
    xn = mm(y, W["w_out"], add=x, name=tag + "_out")
    return xn, (x, h, proj, p, pg, y)


def pool_bwd(dxn, W, saved, tag):
    x, h, proj, p, pg, y = saved
    dy = mm(dxn, W["w_out"], tb=True, name=tag + "_dy")
    g = {"w_out": mm(y, dxn, ta=True, out_dtype=BF16, name=tag + "_dwout")}
    (dpg, dproj), (g["scale"],) = rowwise_bwd(
        f_pool_gate, [(pg, POOL_GROUP, 0, 1), (proj, POOL_GROUP, 4, 1)], [(W["scale"], POOL_GROUP, 0, 1)],
        [(dy, POOL_GROUP, 0, 1)], need=[True, True], place={1: (2 * POOL_WIDTH, 4)}, ncol=4, ts=512, name=tag + "_dgate")
    dp = gmm("nt", dpg, W["w_grp"], G=4, name=tag + "_dp")
    g["w_grp"] = gmm("tn", p, dpg, G=4, out_dtype=BF16, name=tag + "_dwgrp")
    dproj = pool_time_bwd(dp, dproj, tag + "_dwin")
    dh = mm(dproj, W["w_in"], tb=True, name=tag + "_dh")
    g["w_in"] = mm(h, dproj, ta=True, out_dtype=BF16, name=tag + "_dw_in")
    dx, g["ng"] = _norm_bwd(x, W["ng"], dh, dxn, tag + "_dnorm")
    return dx, g


def gdn_fwd(x, W, tag):
    h = _norm_fwd(x, W["ng"], tag + "_norm")
    proj = mm(h, W["w_in"], name=tag + "_in")
    qkv = gdn_conv_fwd(proj, W["conv"], tag + "_conv")
    g_b, beta_b = rowwise(f_gdn_gates, [(proj, LANES, 6144 // LANES, 0)], [(W["a_log"], LANES, 0, 0), (W["dt_bias"], LANES, 0, 0)],
                          [(GDN_QK, 0, F32), (GDN_QK, 0, F32)], ts=512, name=tag + "_gates")
    o, states = gdn_chunk_fwd(qkv, g_b, beta_b, tag + "_chunk")
    og = rowwise(f_gdn_out, [(o, GDN_DV, 0, 1), (proj, GDN_DV, 4096 // GDN_DV, 1)], [(W["norm_g"], GDN_DV, 0, 0)],
                 [(GDN_DV, 1, BF16)], ncol=GDN_H, ts=512, name=tag + "_onorm")[0]
    xn = mm(og, W["w_out"], add=x, name=tag + "_out")
    return xn, (x, h, proj, qkv, g_b, beta_b, o, states, og)


def gdn_bwd(dxn, W, saved, tag):
    x, h, proj, qkv, g_b, beta_b, o, states, og = saved
    dog = mm(dxn, W["w_out"], tb=True, name=tag + "_dog")
    g = {"w_out": mm(og, dxn, ta=True, out_dtype=BF16, name=tag + "_dwout")}
    (do, dproj), (g["norm_g"],) = rowwise_bwd(
        f_gdn_out, [(o, GDN_DV, 0, 1), (proj, GDN_DV, 4096 // GDN_DV, 1)], [(W["norm_g"], GDN_DV, 0, 0)],
        [(dog, GDN_DV, 0, 1)], need=[True, True], place={1: (GDN_IN_PAD, 4096 // GDN_DV)}, ncol=GDN_H, ts=512, name=tag + "_donorm")
    dq, dk, dv, dg_b, dbeta_b = gdn_chunk_bwd(qkv, g_b, beta_b, states, do, tag + "_dchunk")
    (dproj,), (g["a_log"], g["dt_bias"]) = rowwise_bwd(
        f_gdn_gates, [(proj, LANES, 6144 // LANES, 0)], [(W["a_log"], LANES, 0, 0), (W["dt_bias"], LANES, 0, 0)],
        [(dg_b, GDN_QK, 0, 0), (dbeta_b, GDN_QK, 0, 0)], need=[True], place={0: (dproj, 6144 // LANES)}, ts=256, name=tag + "_dgates")
    dproj, g["conv"] = gdn_conv_bwd(proj, W["conv"], dq, dk, dv, dproj, tag + "_dconv")
    dh = mm(dproj, W["w_in"], tb=True, name=tag + "_dh")
    g["w_in"] = mm(h, dproj, ta=True, out_dtype=BF16, name=tag + "_dw_in")
    dx, g["ng"] = _norm_bwd(x, W["ng"], dh, dxn, tag + "_dnorm")
    return dx, g


def mla_fwd(x, pos, W, tag):
    h = _norm_fwd(x, W["ng"], tag + "_norm")
    proj = mm(h, W["w_in"], name=tag + "_in")
    hq = rowwise(f_rms, [(proj, MLA_Q_LORA, 0, 0)], [(W["q_g"], MLA_Q_LORA, 0, 0)], [(MLA_Q_LORA, 0, BF16)], ts=512, name=tag + "_qnorm")[0]
    hkv = rowwise(f_rms, [(proj, MLA_KV_LORA, 2, 0)], [(W["kv_g"], MLA_KV_LORA, 0, 0)], [(MLA_KV_LORA, 0, BF16)], ts=512, name=tag + "_kvnorm")[0]
    qpad = mm(hq, W["w_uq"], name=tag + "_uq")
    kv = mm(hkv, W["w_ukv"], name=tag + "_ukv")
    qh, kh, vh = mla_prep_fwd(qpad, kv, proj, pos, W["rope"], tag + "_prep")
    o, lse = flash_fwd(qh, kh, vh, tag + "_attn")
    og = rowwise(f_ogate, [(o, 512, 0, 1), (proj, 512, 4, 1)], [], [(512, 1, BF16)], ncol=4, ts=512, name=tag + "_ogate")[0]
    xn = mm(og, W["w_out"], add=x, name=tag + "_out")
    return xn, (x, h, proj, hq, hkv, qh, kh, vh, o, lse, og)


def mla_bwd(dxn, pos, W, saved, tag):
    x, h, proj, hq, hkv, qh, kh, vh, o, lse, og = saved
    dog = mm(dxn, W["w_out"], tb=True, name=tag + "_dog")
    g = {"w_out": mm(og, dxn, ta=True, out_dtype=BF16, name=tag + "_dwout")}
    dproj = jnp.zeros(proj.shape, F32)
    (do, dproj), _ = rowwise_bwd(f_ogate, [(o, 512, 0, 1), (proj, 512, 4, 1)], [], [(dog, 512, 0, 1)], need=[True, True],
                                 place={1: (dproj, 4)}, ncol=4, ts=512, name=tag + "_dogate")
    dqh, dkh, dvh = flash_bwd(qh, kh, vh, o, lse, do, tag + "_dattn")
    dqpad, dkv, dproj = mla_prep_bwd(dqh, dkh, dvh, pos, W["rope"], dproj, tag + "_dprep")
    dhq = mm(dqpad, W["w_uq"], tb=True, name=tag + "_dhq")
    g["w_uq"] = mm(hq, dqpad, ta=True, out_dtype=BF16, name=tag + "_dwuq")
    dhkv = mm(dkv, W["w_ukv"], tb=True, name=tag + "_dhkv")
    g["w_ukv"] = mm(hkv, dkv, ta=True, out_dtype=BF16, name=tag + "_dwukv")
    (dproj,), (g["q_g"],) = rowwise_bwd(f_rms, [(proj, MLA_Q_LORA, 0, 0)], [(W["q_g"], MLA_Q_LORA, 0, 0)], [(dhq, MLA_Q_LORA, 0, 0)],
                                        need=[True], place={0: (dproj, 0)}, ts=256, name=tag + "_dqnorm")
    (dproj,), (g["kv_g"],) = rowwise_bwd(f_rms, [(proj, MLA_KV_LORA, 2, 0)], [(W["kv_g"], MLA_KV_LORA, 0, 0)], [(dhkv, MLA_KV_LORA, 0, 0)],
                                         need=[True], place={0: (dproj, 2)}, ts=256, name=tag + "_dkvnorm")
    dh = mm(dproj, W["w_in"], tb=True, name=tag + "_dh")
    g["w_in"] = mm(h, dproj, ta=True, out_dtype=BF16, name=tag + "_dw_in")
    dx, g["ng"] = _norm_bwd(x, W["ng"], dh, dxn, tag + "_dnorm")
    return dx, g


def _pad_cols(a, n):
    return jnp.pad(a, ((0, 0), (0, n - a.shape[1])))


def _mla_w_in_layout(w):
    z = lambda n: jnp.zeros((w.shape[0], n), w.dtype)
    kr = w[:, 1280:1344]
    return jnp.concatenate([w[:, :768], z(256), w[:, 768:1280], kr[:, :32], z(32), kr[:, 32:], z(32), z(384), w[:, 1344:]], axis=1)


def _mla_w_in_unlayout(g):
    return jnp.concatenate([g[:, :768], g[:, 1024:1536], g[:, 1536:1568], g[:, 1600:1632], g[:, 2048:]], axis=1)


def _mla_w_uq_layout(w):
    w3 = w.reshape(w.shape[0], MLA_H, MLA_NOPE + MLA_ROPE)
    z = jnp.zeros((w.shape[0], MLA_H, 32), w.dtype)
    return jnp.concatenate([w3[..., :128], w3[..., 128:160], z, w3[..., 160:192], z], axis=-1).reshape(w.shape[0], MLA_H * 256)


def _mla_w_uq_unlayout(g):
    g3 = g.reshape(g.shape[0], MLA_H, 256)
    return jnp.concatenate([g3[..., :128], g3[..., 128:160], g3[..., 192:224]], axis=-1).reshape(g.shape[0], MLA_H * 192)


def _rope_consts():
    half = MLA_ROPE // 2
    inv = ROPE_THETA ** (-jnp.arange(half, dtype=F32) / half)
    z = jnp.zeros((half,), F32)
    o = jnp.ones((half,), F32)
    row = lambda *p: jnp.concatenate(p).reshape(1, LANES)
    return row(inv, z, inv, z), row(o, z, o, z), row(-o, z, o, z)


BIG = ["pool_w_in", "pool_w_grp", "pool_w_out", "gdn_w_in", "gdn_w_out", "mla_w_in", "mla_w_uq", "mla_w_ukv", "mla_w_out"]
BIG_LAYOUT = {"pool_w_in": (1, 1024, (1024, 4096)), "pool_w_grp": (1, 128, (4, 512, 512)), "pool_w_out": (0, 512, (2048, 1024)),
              "gdn_w_in": (None, None, (4, 1024, 1540)), "gdn_w_out": (0, 512, (2048, 1024)),
              "mla_w_in": (None, None, (4, 1024, 848)), "mla_w_uq": (1, 768, (768, 3072)), "mla_w_ukv": (1, 1024, (512, 4096)),
              "mla_w_out": (0, 512, (2048, 1024))}
SMALL_SHARDED = ["pool_scale", "gdn_conv", "mla_q_norm_g", "mla_kv_norm_g"]
SMALL_AXIS = {"pool_scale": 1, "gdn_conv": 2, "mla_q_norm_g": 1, "mla_kv_norm_g": 1}
REPLICATED = ["norm_g", "gdn_a_log", "gdn_dt_bias", "gdn_norm_g", "final_g"]
PACK_C = 1024


def _pack(parts, dtype, row_mult):
    flat = jnp.concatenate([p.reshape(-1).astype(dtype) for p in parts])
    rows = -(-flat.shape[0] // PACK_C)
    rows = -(-rows // row_mult) * row_mult
    return jnp.pad(flat, (0, rows * PACK_C - flat.shape[0])).reshape(rows, PACK_C)


def _unpack(buf, shapes):
    lead = buf.shape[:-2]
    flat = buf.reshape(lead + (-1,))
    out, off = [], 0
    for s in shapes:
        n = int(np.prod(s))
        out.append(flat[..., off:off + n].reshape(lead + tuple(s)))
        off += n
    return out


def _unshard(g4, axis):
    a = jnp.moveaxis(g4, 0, axis)
    s = a.shape
    return a.reshape(s[:axis] + (s[axis] * s[axis + 1],) + s[axis + 2:])


def _to_shards(a, axis):
    s = a.shape
    return jnp.moveaxis(a.reshape(s[:axis] + (4, s[axis] // 4) + s[axis + 1:]), axis, 0)


def layer_weights(full, small, rep):
    ng = lambda i: rep["norm_g"][i:i + 1]
    pool = lambda j, i: dict(ng=ng(i), w_in=full["pool_w_in"][j], w_grp=full["pool_w_grp"][j], scale=small["pool_scale"][j:j + 1],
                             w_out=full["pool_w_out"][j])
    side_by_side = lambda a4: jnp.moveaxis(a4, 0, 1).reshape(a4.shape[1], 4 * a4.shape[2])
    gdn = dict(ng=ng(1), w_in=_pad_cols(side_by_side(full["gdn_w_in"][0]), GDN_IN_PAD),
               conv=jnp.pad(small["gdn_conv"][0], ((0, 4), (0, 0))), a_log=_pad_cols(rep["gdn_a_log"], LANES),
               dt_bias=_pad_cols(rep["gdn_dt_bias"], LANES), norm_g=rep["gdn_norm_g"], w_out=full["gdn_w_out"][0])
    mla = dict(ng=ng(2), w_in=_mla_w_in_layout(side_by_side(full["mla_w_in"][0])), q_g=small["mla_q_norm_g"],
               kv_g=small["mla_kv_norm_g"], w_uq=_mla_w_uq_layout(full["mla_w_uq"][0]), w_ukv=full["mla_w_ukv"][0],
               w_out=full["mla_w_out"][0], rope=_rope_consts())
    return dict(l0=pool(0, 0), l1=gdn, l2=mla, l3=pool(1, 3), final_g=rep["final_g"].reshape(1, D))


def local_step(x, pos, target, W):
    x1, s0 = pool_fwd(x, W["l0"], "l0")
    x2, s1 = gdn_fwd(x1, W["l1"], "l1")
    x3, s2 = mla_fwd(x2, pos, W["l2"], "l2")
    x4, s3 = pool_fwd(x3, W["l3"], "l3")
    loss, dx4, dfinal = loss_head(x4, target, W["final_g"], "loss_head")
    dx3, g3 = pool_bwd(dx4, W["l3"], s3, "l3")
    dx2, g2 = mla_bwd(dx3, pos, W["l2"], s2, "l2")
    dx1, g1 = gdn_bwd(dx2, W["l1"], s1, "l1")
    dx0, g0 = pool_bwd(dx1, W["l0"], s0, "l0")
    return loss, dx0, (g0, g1, g2, g3), dfinal


def big_grad_pieces(gl):
    g0, g1, g2, g3 = gl
    slots = lambda a: jnp.moveaxis(a.reshape(a.shape[0], 4, a.shape[1] // 4), 1, 0)
    return {"pool_w_in": [g0["w_in"], g3["w_in"]], "pool_w_grp": [g0["w_grp"], g3["w_grp"]], "pool_w_out": [g0["w_out"], g3["w_out"]],
            "gdn_w_in": [slots(g1["w_in"][:, :GDN_IN])], "gdn_w_out": [g1["w_out"]],
            "mla_w_in": [slots(_mla_w_in_unlayout(g2["w_in"]))], "mla_w_uq": [_mla_w_uq_unlayout(g2["w_uq"])],
            "mla_w_ukv": [g2["w_ukv"]], "mla_w_out": [g2["w_out"]]}


def small_grads(gl, dfinal):
    g0, g1, g2, g3 = gl
    return {"norm_g": jnp.concatenate([g0["ng"], g1["ng"], g2["ng"], g3["ng"]], axis=0),
            "pool_scale": jnp.concatenate([g0["scale"], g3["scale"]], axis=0), "gdn_conv": g1["conv"][None, :4],
            "gdn_a_log": g1["a_log"][:, :GDN_H], "gdn_dt_bias": g1["dt_bias"][:, :GDN_H], "gdn_norm_g": g1["norm_g"],
            "mla_q_norm_g": g2["q_g"], "mla_kv_norm_g": g2["kv_g"], "final_g": dfinal.reshape(D)}


NAMES = ["norm_g", "pool_w_in", "pool_w_grp", "pool_scale", "pool_w_out", "gdn_w_in", "gdn_conv", "gdn_a_log", "gdn_dt_bias",
         "gdn_norm_g", "gdn_w_out", "mla_w_in", "mla_q_norm_g", "mla_w_uq", "mla_kv_norm_g", "mla_w_ukv", "mla_w_out", "final_g"]


def kernel(x, positions, norm_g, pool_w_in, pool_w_grp, pool_scale, pool_w_out, gdn_w_in, gdn_conv, gdn_a_log, gdn_dt_bias, gdn_norm_g, gdn_w_out, mla_w_in, mla_q_norm_g, mla_w_uq, mla_kv_norm_g, mla_w_ukv, mla_w_out, final_g, loss_target, m_norm_g, m_pool_w_in, m_pool_w_grp, m_pool_scale, m_pool_w_out, m_gdn_w_in, m_gdn_conv, m_gdn_a_log, m_gdn_dt_bias, m_gdn_norm_g, m_gdn_w_out, m_mla_w_in, m_mla_q_norm_g, m_mla_w_uq, m_mla_kv_norm_g, m_mla_w_ukv, m_mla_w_out, m_final_g, v_norm_g, v_pool_w_in, v_pool_w_grp, v_pool_scale, v_pool_w_out, v_gdn_w_in, v_gdn_conv, v_gdn_a_log, v_gdn_dt_bias, v_gdn_norm_g, v_gdn_w_out, v_mla_w_in, v_mla_q_norm_g, v_mla_w_uq, v_mla_kv_norm_g, v_mla_w_ukv, v_mla_w_out, v_final_g):
    args = locals()
    w = {n: args[n] for n in NAMES}
    m = {n: args["m_" + n] for n in NAMES}
    v = {n: args["v_" + n] for n in NAMES}
    my_chip = (2 * lax.axis_index("x") + lax.axis_index("y")).astype(I32)

    small_shapes = [w[n].shape for n in SMALL_SHARDED]
    small_pack = _pack([w[n] for n in SMALL_SHARDED], F32, 8)[None]
    items = [(w[n].astype(BF16), BIG_LAYOUT[n][2], BIG_LAYOUT[n][0], BIG_LAYOUT[n][1]) for n in BIG]
    items.append((small_pack, (4,) + small_pack.shape[1:], None, None))
    got = iter(gather_weights(items, "gather_weights"))
    full = {n: [next(got) for _ in range(w[n].shape[0])] for n in BIG}
    small = {n: _unshard(a, SMALL_AXIS[n]) for n, a in zip(SMALL_SHARDED, _unpack(next(got), small_shapes))}
    W = layer_weights(full, small, {n: w[n] for n in REPLICATED})

    S_ = x.shape[1]
    loss_part, dx, gl, dfinal = local_step(x[0], positions.reshape(S_, 1).astype(F32), loss_target[0], W)

    pieces = big_grad_pieces(gl)
    recv = scatter_grads([(pieces[n], BIG_LAYOUT[n][0], BIG_LAYOUT[n][1], w[n].shape[1:]) for n in BIG], "scatter_grads")
    sib = swap_cores(recv, "swap_cores")
    sg = small_grads(gl, dfinal)
    small_names = SMALL_SHARDED + REPLICATED
    small_buf = _pack([sg[n] for n in small_names] + [loss_part], F32, 8)
    small_sum = sum_slots(None, exchange_all(small_buf, "gather_small"), jnp.full((1,), -1, I32), "sum_small")
    full_small = _unpack(small_sum, [sg[n].shape for n in small_names] + [(1, LANES)])
    loss = full_small[-1][0, 0]
    parts = {}
    for n, a in zip(small_names, full_small[:-1]):
        if n in SMALL_AXIS:
            a = lax.dynamic_index_in_dim(_to_shards(a, SMALL_AXIS[n]), my_chip, axis=0, keepdims=False)
        parts[n] = [a]
    for n, r, s in zip(BIG, recv, sib):
        parts[n] = [r, s]

    outs = []
    for n in NAMES:
        shp = w[n].shape
        two = (int(np.prod(shp[:-1])), shp[-1]) if len(shp) > 1 else (1, shp[0])
        res = adamw(w[n].reshape(two), [p.reshape((-1,) + two) for p in parts[n]], m[n].reshape(two), v[n].reshape(two), "adamw_" + n)
        outs.append([r.reshape(shp) for r in res])
    return (loss, dx[None], *[o[0] for o in outs], *[o[1] for o in outs], *[o[2] for o in outs], *[o[3] for o in outs])
```

```python
import functools
import math

import jax
import jax.numpy as jnp
import numpy as np
from jax import lax
from jax.experimental import pallas as pl
from jax.experimental.pallas import tpu as pltpu

F32 = jnp.float32
BF16 = jnp.bfloat16
I32 = jnp.int32

D = 1024
EPS = 1e-6
POOL_WIDTH = 2048
POOL_GROUP = 512
GDN_H, GDN_DK, GDN_DV, GDN_C = 8, 128, 256, 64
GDN_QK, GDN_V, GDN_CONV_CH, GDN_IN = 1024, 2048, 4096, 6160
GDN_IN_PAD = 6272
MLA_H, MLA_NOPE, MLA_ROPE, MLA_V = 16, 128, 64, 128
MLA_Q_LORA, MLA_KV_LORA, MLA_WIDTH, MLA_IN = 768, 512, 2048, 3392
MLA_IN_PAD = 4096
MLA_SCALE = (MLA_NOPE + MLA_ROPE) ** -0.5
ROPE_THETA = 10000.0
ADAM_LR, ADAM_B1, ADAM_B2, ADAM_EPS, ADAM_WD, ADAM_STEP = 0.001, 0.9, 0.999, 1e-08, 0.01, 10

VMEM_LIMIT_V7X = 56 * 1024 * 1024
LANES = 128
MESH = pl.DeviceIdType.MESH


def _pc(body, **kw):
    return pl.pallas_call(body, **kw)


def _cparams(sem):
    return pltpu.CompilerParams(dimension_semantics=sem, vmem_limit_bytes=VMEM_LIMIT_V7X)


def _tile(n, cap):
    t = (cap // LANES) * LANES
    while t >= LANES:
        if n % t == 0:
            return t
        t -= LANES
    return n


def _sds(shape, dt):
    return jax.ShapeDtypeStruct(shape, dt)


def mm(a, b, *, ta=False, tb=False, add=None, out_dtype=F32, name):
    if ta:
        K, M = a.shape
    else:
        M, K = a.shape
    if tb:
        N, K2 = b.shape
    else:
        K2, N = b.shape
    assert K == K2, (a.shape, b.shape, ta, tb)
    tm, tn, tk = _tile(M, 1024), _tile(N, 1024), _tile(K, 1024)
    nk = K // tk
    a_spec = pl.BlockSpec((tk, tm), lambda i, j, k: (k, i)) if ta else pl.BlockSpec((tm, tk), lambda i, j, k: (i, k))
    b_spec = pl.BlockSpec((tn, tk), lambda i, j, k: (j, k)) if tb else pl.BlockSpec((tk, tn), lambda i, j, k: (k, j))
    o_spec = pl.BlockSpec((tm, tn), lambda i, j, k: (i, j))
    dn = (((0 if ta else 1,), (1 if tb else 0,)), ((), ()))
    has_add = add is not None

    def body(*refs):
        a_ref, b_ref = refs[0], refs[1]
        o_ref, acc = refs[-2], refs[-1]
        k = pl.program_id(2)

        @pl.when(k == 0)
        def _():
            acc[...] = jnp.zeros_like(acc)

        acc[...] += lax.dot_general(a_ref[...].astype(BF16), b_ref[...].astype(BF16), dn, preferred_element_type=F32)

        @pl.when(k == nk - 1)
        def _():
            r = acc[...]
            if has_add:
                r = r + refs[2][...]
            o_ref[...] = r.astype(out_dtype)

    ins = [a, b] + ([add] if has_add else [])
    specs = [a_spec, b_spec] + ([o_spec] if has_add else [])
    return _pc(body, grid=(M // tm, N // tn, nk), in_specs=specs, out_specs=o_spec, out_shape=_sds((M, N), out_dtype),
               scratch_shapes=[pltpu.VMEM((tm, tn), F32)], compiler_params=_cparams(("parallel", "parallel", "arbitrary")),
               name=name)(*ins)


def gmm(kind, a, b, *, G, name, out_dtype=F32):
    S_ = a.shape[0]
    Ka = a.shape[1] // G
    if kind == "tn":
        N = b.shape[1] // G
        tk = _tile(S_, 512)
        nk = S_ // tk

        def body(a_ref, b_ref, o_ref, acc):
            k = pl.program_id(1)

            @pl.when(k == 0)
            def _():
                acc[...] = jnp.zeros_like(acc)

            acc[...] += lax.dot_general(a_ref[...].astype(BF16), b_ref[...].astype(BF16), (((0,), (0,)), ((), ())),
                                        preferred_element_type=F32)

            @pl.when(k == nk - 1)
            def _():
                o_ref[...] = acc[...].astype(out_dtype)

        return _pc(body, grid=(G, nk),
                   in_specs=[pl.BlockSpec((tk, Ka), lambda g, k: (k, g)), pl.BlockSpec((tk, N), lambda g, k: (k, g))],
                   out_specs=pl.BlockSpec((None, Ka, N), lambda g, k: (g, 0, 0)), out_shape=_sds((G, Ka, N), out_dtype),
                   scratch_shapes=[pltpu.VMEM((Ka, N), F32)], compiler_params=_cparams(("parallel", "arbitrary")), name=name)(a, b)
    N = b.shape[2] if kind == "nn" else b.shape[1]
    tm = _tile(S_, 1024)
    dn = (((1,), (0 if kind == "nn" else 1,)), ((), ()))

    def body(a_ref, b_ref, o_ref):
        o_ref[...] = lax.dot_general(a_ref[...].astype(BF16), b_ref[...].astype(BF16), dn, preferred_element_type=F32)

    bshape = (None,) + tuple(b.shape[1:])
    return _pc(body, grid=(G, S_ // tm),
               in_specs=[pl.BlockSpec((tm, Ka), lambda g, i: (i, g)), pl.BlockSpec(bshape, lambda g, i: (g, 0, 0))],
               out_specs=pl.BlockSpec((tm, N), lambda g, i: (i, g)), out_shape=_sds((S_, G * N), F32),
               compiler_params=_cparams(("parallel", "parallel")), name=name)(a, b)


def _rw_spec(ts, w, c, s):
    return pl.BlockSpec((ts, w), lambda j, i: (i, c + j * s))


def _rw_pspec(p, w, c, s):
    return pl.BlockSpec((p.shape[0], w), lambda j, i: (0, c + j * s))


def rowwise(f, tiles, params, outs, *, ncol=1, ts, name):
    S_ = tiles[0][0].shape[0]
    nin = len(tiles) + len(params)

    def body(*refs):
        res = f(pl.program_id(0), *[r[...] for r in refs[:nin]])
        for r, o in zip(refs[nin:], res):
            r[...] = o.astype(r.dtype)

    return _pc(body, grid=(ncol, S_ // ts),
               in_specs=[_rw_spec(ts, w, c, s) for (_, w, c, s) in tiles] + [_rw_pspec(*p) for p in params],
               out_specs=[_rw_spec(ts, w, 0, s) for (w, s, _) in outs],
               out_shape=[_sds((S_, w * (ncol if s else 1)), dt) for (w, s, dt) in outs],
               compiler_params=_cparams(("parallel", "parallel")), name=name)(*[t[0] for t in tiles], *[p[0] for p in params])


def rowwise_bwd(f, tiles, params, cots, *, need, adds=None, place=None, ncol=1, ts, name):
    S_ = tiles[0][0].shape[0]
    adds = adds or {}
    place = place or {}
    nt, npar, nc = len(tiles), len(params), len(cots)
    add_keys = sorted(adds)
    need_idx = [k for k in range(nt) if need[k]]
    into_keys = [k for k in need_idx if k in place and not isinstance(place[k][0], int)]
    n_extra = len(add_keys) + len(into_keys)

    def body(*refs):
        j, i = pl.program_id(0), pl.program_id(1)
        vals = [r[...] for r in refs[:nt + npar]]
        cvals = tuple(r[...] for r in refs[nt + npar:nt + npar + nc])
        add_refs = refs[nt + npar + nc:nt + npar + nc + len(add_keys)]
        out_refs = refs[nt + npar + nc + n_extra:]
        _, vjp = jax.vjp(lambda *v: tuple(f(j, *v)), *vals)
        grads = vjp(cvals)
        for n, k in enumerate(need_idx):
            g = grads[k]
            if k in adds:
                g = g + add_refs[add_keys.index(k)][...]
            out_refs[n][...] = g
        for n in range(npar):
            ref = out_refs[len(need_idx) + n]
            first = (i == 0) if params[n][3] else jnp.logical_and(i == 0, j == 0)

            @pl.when(first)
            def _():
                ref[...] = jnp.zeros_like(ref)

            ref[...] += grads[nt + n]

    in_specs = ([_rw_spec(ts, w, c, s) for (_, w, c, s) in tiles] + [_rw_pspec(*p) for p in params]
                + [_rw_spec(ts, w, c, s) for (_, w, c, s) in cots] + [_rw_spec(ts, *adds[k][1:]) for k in add_keys]
                + [pl.BlockSpec(memory_space=pl.ANY) for _ in into_keys])
    out_specs, out_shape, aliases = [], [], {}
    for n, k in enumerate(need_idx):
        w, s = tiles[k][1], tiles[k][3]
        if k in place:
            dst, c0 = place[k]
            total = dst if isinstance(dst, int) else dst.shape[1]
            out_specs.append(_rw_spec(ts, w, c0, s))
            out_shape.append(_sds((S_, total), F32))
            if k in into_keys:
                aliases[nt + npar + nc + len(add_keys) + into_keys.index(k)] = n
        else:
            out_specs.append(_rw_spec(ts, w, 0, s))
            out_shape.append(_sds((S_, w * (ncol if s else 1)), F32))
    out_specs += [_rw_pspec(p[0], p[1], p[2], p[3]) for p in params]
    out_shape += [_sds(p[0].shape, F32) for p in params]
    res = _pc(body, grid=(ncol, S_ // ts), in_specs=in_specs, out_specs=out_specs, out_shape=out_shape,
              input_output_aliases=aliases, compiler_params=_cparams(("arbitrary", "arbitrary")), name=name)(
        *[t[0] for t in tiles], *[p[0] for p in params], *[c[0] for c in cots], *[adds[k][0] for k in add_keys],
        *[place[k][0] for k in into_keys])
    return list(res[:len(need_idx)]), list(res[len(need_idx):])


def _rms(x, g):
    r = lax.rsqrt(jnp.mean(x * x, axis=-1, keepdims=True) + EPS)
    return x * r * g


def _silu(x):
    return x * jax.nn.sigmoid(x)


@jax.custom_vjp
def _softplus(x):
    return jnp.maximum(x, 0.0) + jnp.log1p(jnp.exp(-jnp.abs(x)))


_softplus.defvjp(lambda x: (_softplus(x), x), lambda x, d: (d * jax.nn.sigmoid(x),))


def f_rms(j, x, g):
    return (_rms(x, g),)


def f_pool_gate(j, pg, gate, scale):
    return (pg * scale * _silu(gate),)


def f_ogate(j, o, gate):
    return (o * _silu(gate),)


def f_gdn_out(j, o, gate, g):
    return (_rms(o, g) * _silu(gate),)


def f_gdn_gates(j, ba, alog, dtb):
    lane = lax.broadcasted_iota(I32, (1, LANES), 1)
    gs, bs = [], []
    for h in range(GDN_H):
        eb = (lane == h).astype(F32)
        ea = (lane == GDN_H + h).astype(F32)
        b = jnp.sum(ba * eb, -1, keepdims=True)
        a = jnp.sum(ba * ea, -1, keepdims=True)
        al = jnp.sum(alog * eb, -1, keepdims=True)
        dt = jnp.sum(dtb * eb, -1, keepdims=True)
        g = -jnp.exp(al) * _softplus(a + dt)
        gs.append(jnp.broadcast_to(g, ba.shape))
        bs.append(jnp.broadcast_to(jax.nn.sigmoid(b), ba.shape))
    return jnp.concatenate(gs, 1), jnp.concatenate(bs, 1)


def _shift_dn(x, k):
    rows = lax.broadcasted_iota(I32, x.shape, 0)
    return jnp.where(rows < k, 0.0, pltpu.roll(x, k, 0))


def _shift_up(x, k):
    n = x.shape[0]
    rows = lax.broadcasted_iota(I32, x.shape, 0)
    return jnp.where(rows >= n - k, 0.0, pltpu.roll(x, n - k, 0))


def _pool_window(j):
    g = lax.div(j, POOL_GROUP // LANES)
    return jnp.where(g == 0, 2.0, jnp.where(g == 1, 4.0, jnp.where(g == 2, 8.0, 16.0))), g


def _pick(g, a2, a4, a8, a16):
    return jnp.where(g == 0, a2, jnp.where(g == 1, a4, jnp.where(g == 2, a8, a16)))


def pool_time_fwd(proj, name):
    S_ = proj.shape[0]

    def body(u_ref, p_ref):
        u = u_ref[...]
        w, g = _pool_window(pl.program_id(0))
        s2 = u + _shift_dn(u, 1)
        s4 = s2 + _shift_dn(s2, 2)
        s8 = s4 + _shift_dn(s4, 4)
        s16 = s8 + _shift_dn(s8, 8)
        t1 = (lax.broadcasted_iota(I32, u.shape, 0) + 1).astype(F32)
        p_ref[...] = (_pick(g, s2, s4, s8, s16) / jnp.minimum(t1, w) - u).astype(p_ref.dtype)

    return _pc(body, grid=(POOL_WIDTH // LANES,), in_specs=[pl.BlockSpec((S_, LANES), lambda j: (0, j))],
               out_specs=pl.BlockSpec((S_, LANES), lambda j: (0, j)), out_shape=_sds((S_, POOL_WIDTH), BF16),
               compiler_params=_cparams(("parallel",)), name=name)(proj)


def pool_time_bwd(dp, into, name):
    S_ = dp.shape[0]

    def body(dp_ref, _, du_ref):
        d = dp_ref[...]
        w, g = _pool_window(pl.program_id(0))
        t1 = (lax.broadcasted_iota(I32, d.shape, 0) + 1).astype(F32)
        q = d / jnp.minimum(t1, w)
        r2 = q + _shift_up(q, 1)
        r4 = r2 + _shift_up(r2, 2)
        r8 = r4 + _shift_up(r4, 4)
        r16 = r8 + _shift_up(r8, 8)
        du_ref[...] = _pick(g, r2, r4, r8, r16) - d

    return _pc(body, grid=(POOL_WIDTH // LANES,),
               in_specs=[pl.BlockSpec((S_, LANES), lambda j: (0, j)), pl.BlockSpec(memory_space=pl.ANY)],
               out_specs=pl.BlockSpec((S_, LANES), lambda j: (0, j)), out_shape=_sds(into.shape, F32),
               input_output_aliases={1: 0}, compiler_params=_cparams(("parallel",)), name=name)(dp, into)


def _conv_post(j, a):
    n = a * lax.rsqrt(jnp.sum(a * a, axis=-1, keepdims=True) + EPS)
    nq = GDN_QK // LANES
    return jnp.where(j < nq, n * (GDN_DK ** -0.5), jnp.where(j < 2 * nq, n, a))


def _conv_pre(u, w):
    return w[3:4] * u + w[2:3] * _shift_dn(u, 1) + w[1:2] * _shift_dn(u, 2) + w[0:1] * _shift_dn(u, 3)


def gdn_conv_fwd(proj, conv_w, name):
    S_ = proj.shape[0]

    def body(u_ref, w_ref, o_ref):
        o_ref[...] = _conv_post(pl.program_id(0), _silu(_conv_pre(u_ref[...], w_ref[...])))

    return _pc(body, grid=(GDN_CONV_CH // LANES,),
               in_specs=[pl.BlockSpec((S_, LANES), lambda j: (0, j)), pl.BlockSpec((8, LANES), lambda j: (0, j))],
               out_specs=pl.BlockSpec((S_, LANES), lambda j: (0, j)), out_shape=_sds((S_, GDN_CONV_CH), F32),
               compiler_params=_cparams(("parallel",)), name=name)(proj, conv_w)


def gdn_conv_bwd(proj, conv_w, dq, dk, dv, into, name):
    S_ = proj.shape[0]
    nq = GDN_QK // LANES

    def body(u_ref, w_ref, dq_ref, dk_ref, dv_ref, _, du_ref, dw_ref):
        j = pl.program_id(0)
        u, w = u_ref[...], w_ref[...]
        c = _conv_pre(u, w)
        sig = jax.nn.sigmoid(c)
        dout = jnp.where(j < nq, dq_ref[...], jnp.where(j < 2 * nq, dk_ref[...], dv_ref[...]))
        _, vjp = jax.vjp(lambda a: _conv_post(j, a), c * sig)
        dc = vjp(dout)[0] * (sig * (1.0 + c * (1.0 - sig)))
        du_ref[...] = w[3:4] * dc + w[2:3] * _shift_up(dc, 1) + w[1:2] * _shift_up(dc, 2) + w[0:1] * _shift_up(dc, 3)
        rows = lax.broadcasted_iota(I32, (8, LANES), 0)
        dw = jnp.zeros((8, LANES), F32)
        for k in range(4):
            us = u if k == 3 else _shift_dn(u, 3 - k)
            dw = dw + jnp.where(rows == k, jnp.sum(dc * us, axis=0, keepdims=True), 0.0)
        dw_ref[...] = dw

    blk = lambda f: pl.BlockSpec((S_, LANES), f)
    return _pc(body, grid=(GDN_CONV_CH // LANES,),
               in_specs=[blk(lambda j: (0, j)), pl.BlockSpec((8, LANES), lambda j: (0, j)),
                         blk(lambda j: (0, jnp.minimum(j, nq - 1))), blk(lambda j: (0, jnp.clip(j - nq, 0, nq - 1))),
                         blk(lambda j: (0, jnp.clip(j - 2 * nq, 0, 2 * nq - 1))), pl.BlockSpec(memory_space=pl.ANY)],
               out_specs=[blk(lambda j: (0, j)), pl.BlockSpec((8, LANES), lambda j: (0, j))],
               out_shape=[_sds(into.shape, F32), _sds((8, GDN_CONV_CH), F32)], input_output_aliases={5: 0},
               compiler_params=_cparams(("parallel",)), name=name)(proj, conv_w, dq, dk, dv, into)


_NN, _NT, _TN = ((1,), (0,)), ((1,), (1,)), ((0,), (0,))


def _split(x, n):
    parts = []
    for _ in range(n):
        h = x.astype(BF16)
        parts.append(h)
        x = x - h.astype(F32)
    return parts


def _dot(a, b, dn, mode):
    d = lambda p, q: lax.dot_general(p, q, (dn, ((), ())), preferred_element_type=F32)
    if mode == "lo":
        return d(a.astype(BF16), b.astype(BF16))
    if mode == "x3":
        (ah, al), (bh, bl) = _split(a, 2), _split(b, 2)
        return d(ah, bh) + (d(ah, bl) + d(al, bh))
    b0, b1, b2 = _split(b, 3)
    ab = a.astype(BF16)
    return d(ab, b0) + (d(ab, b1) + d(ab, b2))


def _make_dots(mode):
    @jax.custom_vjp
    def nn(a, b):
        return _dot(a, b, _NN, mode)

    @jax.custom_vjp
    def nt(a, b):
        return _dot(a, b, _NT, mode)

    @jax.custom_vjp
    def tn(a, b):
        return _dot(a, b, _TN, mode)

    nn.defvjp(lambda a, b: (nn(a, b), (a, b)), lambda r, d: (nt(d, r[1]), tn(r[0], d)))
    nt.defvjp(lambda a, b: (nt(a, b), (a, b)), lambda r, d: (nn(d, r[1]), tn(d, r[0])))
    tn.defvjp(lambda a, b: (tn(a, b), (a, b)), lambda r, d: (nt(r[1], d), nn(r[0], d)))
    return nn, nt, tn


_nn_hi, _nt_hi, _tn_hi = _make_dots("x3")
_nn_lo, _nt_lo, _tn_lo = _make_dots("lo")


@jax.custom_vjp
def _nn_const(a, b):
    return _dot(a, b, _NN, "xl")


_nn_const.defvjp(lambda a, b: (_nn_const(a, b), a), lambda a, d: (jnp.zeros_like(a), _dot(a, d, _TN, "xl")))


def _gdn_chunk(q, k, v, gb, bb, state):
    C = GDN_C
    e0 = (lax.broadcasted_iota(I32, (1, LANES), 1) == 0).astype(F32)
    g1 = jnp.sum(gb * e0, -1, keepdims=True)
    b1 = jnp.sum(bb * e0, -1, keepdims=True)
    ri = lax.broadcasted_iota(I32, (C, C), 0)
    ci = lax.broadcasted_iota(I32, (C, C), 1)
    causal, strict = ri >= ci, ri > ci
    tri, eye = causal.astype(F32), (ri == ci).astype(F32)
    gc_c = _nn_const(tri, jnp.broadcast_to(g1, (C, C)))
    gc_d = _nn_const(tri, jnp.broadcast_to(g1, (C, LANES)))
    gr_c = _nn_const(jnp.ones((C, C), F32), eye * gc_c)
    decay = jnp.where(causal, jnp.exp(jnp.where(causal, gc_c - gr_c, 0.0)), 0.0)
    kb, vb = k * b1, v * b1
    x = -jnp.where(strict, _nt_lo(kb, k) * decay, 0.0)
    ainv, p = eye + x, x
    for _ in range(5):
        p = _nn_hi(p, p)
        ainv = ainv + _nn_hi(ainv, p)
    u = _nn_hi(ainv, vb)
    w = _nn_hi(ainv, kb * jnp.exp(gc_d))
    attn = jnp.where(causal, _nt_lo(q, k) * decay, 0.0)
    v_new = u - _nn_lo(w, state)
    o = _nn_lo(q * jnp.exp(gc_d), state) + _nn_lo(attn, v_new)
    last = lax.broadcasted_iota(I32, (C, LANES), 0) == C - 1
    gl = jnp.sum(jnp.where(last, gc_d, 0.0), axis=0, keepdims=True)
    gl1 = jnp.sum(gl * e0, -1, keepdims=True)
    new_state = state * jnp.exp(gl1) + _tn_lo(k * jnp.exp(gl - gc_d), v_new)
    return o, new_state


def gdn_chunk_fwd(qkv, g_b, beta_b, name):
    S_ = qkv.shape[0]
    N = S_ // GDN_C

    def body(q_ref, k_ref, v_ref, g_ref, b_ref, o_ref, s_ref, state):
        @pl.when(pl.program_id(0) == 0)
        def _():
            state[...] = jnp.zeros_like(state)

        for h in range(GDN_H):
            kk, vv = slice(h * GDN_DK, (h + 1) * GDN_DK), slice(h * GDN_DV, (h + 1) * GDN_DV)
            st = state[h]
            s_ref[0, h] = st
            o, st2 = _gdn_chunk(q_ref[:, kk], k_ref[:, kk], v_ref[:, vv], g_ref[:, kk], b_ref[:, kk], st)
            o_ref[:, vv] = o
            state[h] = st2

    return _pc(body, grid=(N,),
               in_specs=[pl.BlockSpec((GDN_C, GDN_QK), lambda n: (n, 0)), pl.BlockSpec((GDN_C, GDN_QK), lambda n: (n, 1)),
                         pl.BlockSpec((GDN_C, GDN_V), lambda n: (n, 1)), pl.BlockSpec((GDN_C, GDN_QK), lambda n: (n, 0)),
                         pl.BlockSpec((GDN_C, GDN_QK), lambda n: (n, 0))],
               out_specs=[pl.BlockSpec((GDN_C, GDN_V), lambda n: (n, 0)),
                          pl.BlockSpec((1, GDN_H, GDN_DK, GDN_DV), lambda n: (n, 0, 0, 0))],
               out_shape=[_sds((S_, GDN_V), F32), _sds((N, GDN_H, GDN_DK, GDN_DV), F32)],
               scratch_shapes=[pltpu.VMEM((GDN_H, GDN_DK, GDN_DV), F32)],
               compiler_params=_cparams(("arbitrary",)), name=name)(qkv, qkv, qkv, g_b, beta_b)


def gdn_chunk_bwd(qkv, g_b, beta_b, states, do, name):
    S_ = qkv.shape[0]
    N = S_ // GDN_C

    def body(q_ref, k_ref, v_ref, g_ref, b_ref, s_ref, do_ref, dq_ref, dk_ref, dv_ref, dg_ref, db_ref, dstate):
        @pl.when(pl.program_id(0) == 0)
        def _():
            dstate[...] = jnp.zeros_like(dstate)

        for h in range(GDN_H):
            kk, vv = slice(h * GDN_DK, (h + 1) * GDN_DK), slice(h * GDN_DV, (h + 1) * GDN_DV)
            _, vjp = jax.vjp(_gdn_chunk, q_ref[:, kk], k_ref[:, kk], v_ref[:, vv], g_ref[:, kk], b_ref[:, kk], s_ref[0, h])
            dq, dk, dv, dg, db, ds = vjp((do_ref[:, vv], dstate[h]))
            dq_ref[:, kk] = dq
            dk_ref[:, kk] = dk
            dv_ref[:, vv] = dv
            dg_ref[:, kk] = dg
            db_ref[:, kk] = db
            dstate[h] = ds

    r = lambda n: N - 1 - n
    qk = lambda c: pl.BlockSpec((GDN_C, GDN_QK), lambda n: (r(n), c))
    vs = lambda c: pl.BlockSpec((GDN_C, GDN_V), lambda n: (r(n), c))
    return _pc(body, grid=(N,),
               in_specs=[qk(0), qk(1), vs(1), qk(0), qk(0),
                         pl.BlockSpec((1, GDN_H, GDN_DK, GDN_DV), lambda n: (r(n), 0, 0, 0)), vs(0)],
               out_specs=[qk(0), qk(0), vs(0), qk(0), qk(0)],
               out_shape=[_sds((S_, GDN_QK), F32), _sds((S_, GDN_QK), F32), _sds((S_, GDN_V), F32),
                          _sds((S_, GDN_QK), F32), _sds((S_, GDN_QK), F32)],
               scratch_shapes=[pltpu.VMEM((GDN_H, GDN_DK, GDN_DV), F32)],
               compiler_params=_cparams(("arbitrary",)), name=name)(qkv, qkv, qkv, g_b, beta_b, states, do)


def _rope_tables(pos_ref, inv_ref, cm_ref, sg_ref):
    ang = pos_ref[...] * inv_ref[...]
    return jnp.cos(ang) * cm_ref[...], jnp.sin(ang) * sg_ref[...]


def mla_prep_fwd(qpad, kv, proj, pos, rope_consts, name):
    S_ = qpad.shape[0]
    ts = 256
    W = 2 * LANES

    def body(q_ref, kv_ref, kr_ref, pos_ref, inv_ref, cm_ref, sg_ref, qh_ref, kh_ref, vh_ref):
        cs, sn = _rope_tables(pos_ref, inv_ref, cm_ref, sg_ref)
        rope = lambda r: r * cs + pltpu.roll(r, LANES // 2, 1) * sn
        krr = rope(kr_ref[...]).astype(BF16)
        for h in range(MLA_H):
            qh_ref[h, :, 0:LANES] = (q_ref[:, h * W:h * W + LANES] * MLA_SCALE).astype(BF16)
            qh_ref[h, :, LANES:W] = (rope(q_ref[:, h * W + LANES:(h + 1) * W]) * MLA_SCALE).astype(BF16)
            kh_ref[h, :, 0:LANES] = kv_ref[:, h * W:h * W + LANES].astype(BF16)
            kh_ref[h, :, LANES:W] = krr
            vh_ref[h] = kv_ref[:, h * W + LANES:(h + 1) * W].astype(BF16)

    one = pl.BlockSpec((1, LANES), lambda i: (0, 0))
    return _pc(body, grid=(S_ // ts,),
               in_specs=[pl.BlockSpec((ts, MLA_H * W), lambda i: (i, 0)), pl.BlockSpec((ts, MLA_H * W), lambda i: (i, 0)),
                         pl.BlockSpec((ts, LANES), lambda i: (i, 1536 // LANES)), pl.BlockSpec((ts, 1), lambda i: (i, 0)),
                         one, one, one],
               out_specs=[pl.BlockSpec((MLA_H, ts, W), lambda i: (0, i, 0)), pl.BlockSpec((MLA_H, ts, W), lambda i: (0, i, 0)),
                          pl.BlockSpec((MLA_H, ts, LANES), lambda i: (0, i, 0))],
               out_shape=[_sds((MLA_H, S_, W), BF16), _sds((MLA_H, S_, W), BF16), _sds((MLA_H, S_, LANES), BF16)],
               compiler_params=_cparams(("parallel",)), name=name)(qpad, kv, proj, pos, *rope_consts)


def mla_prep_bwd(dqh, dkh, dvh, pos, rope_consts, into, name):
    S_ = dqh.shape[1]
    ts = 256
    W = 2 * LANES

    def body(dq_ref, dk_ref, dv_ref, pos_ref, inv_ref, cm_ref, sg_ref, _, dqp_ref, dkv_ref, dkr_ref):
        cs, sn = _rope_tables(pos_ref, inv_ref, cm_ref, sg_ref)
        rope_t = lambda g: g * cs + pltpu.roll(g * sn, LANES // 2, 1)
        acc = jnp.zeros((ts, LANES), F32)
        for h in range(MLA_H):
            dqp_ref[:, h * W:h * W + LANES] = dq_ref[h, :, 0:LANES] * MLA_SCALE
            dqp_ref[:, h * W + LANES:(h + 1) * W] = rope_t(dq_ref[h, :, LANES:W]) * MLA_SCALE
            dkv_ref[:, h * W:h * W + LANES] = dk_ref[h, :, 0:LANES]
            dkv_ref[:, h * W + LANES:(h + 1) * W] = dv_ref[h]
            acc = acc + dk_ref[h, :, LANES:W]
        dkr_ref[...] = rope_t(acc)

    one = pl.BlockSpec((1, LANES), lambda i: (0, 0))
    return _pc(body, grid=(S_ // ts,),
               in_specs=[pl.BlockSpec((MLA_H, ts, W), lambda i: (0, i, 0)), pl.BlockSpec((MLA_H, ts, W), lambda i: (0, i, 0)),
                         pl.BlockSpec((MLA_H, ts, LANES), lambda i: (0, i, 0)), pl.BlockSpec((ts, 1), lambda i: (i, 0)),
                         one, one, one, pl.BlockSpec(memory_space=pl.ANY)],
               out_specs=[pl.BlockSpec((ts, MLA_H * W), lambda i: (i, 0)), pl.BlockSpec((ts, MLA_H * W), lambda i: (i, 0)),
                          pl.BlockSpec((ts, LANES), lambda i: (i, 1536 // LANES))],
               out_shape=[_sds((S_, MLA_H * W), F32), _sds((S_, MLA_H * W), F32), _sds(into.shape, F32)],
               input_output_aliases={7: 2}, compiler_params=_cparams(("parallel",)), name=name)(dqh, dkh, dvh, pos, *rope_consts, into)


NEG = -1e30


FLASH_TILE = 1024


def _scores(q, k, diagonal):
    s = lax.dot_general(q, k, (_NT, ((), ())), preferred_element_type=F32)
    if not diagonal:
        return s
    t = s.shape[0]
    return jnp.where(lax.broadcasted_iota(I32, (t, t), 1) <= lax.broadcasted_iota(I32, (t, t), 0), s, NEG)


def flash_fwd(qh, kh, vh, name):
    H, S_, W = qh.shape
    t = _tile(S_, FLASH_TILE)
    n = S_ // t

    def body(q_ref, k_ref, v_ref, o_ref, lse_ref, m_s, l_s, acc):
        qi, kj = pl.program_id(1), pl.program_id(2)

        @pl.when(kj == 0)
        def _():
            m_s[...] = jnp.full_like(m_s, NEG)
            l_s[...] = jnp.zeros_like(l_s)
            acc[...] = jnp.zeros_like(acc)

        def step(diagonal):
            s = _scores(q_ref[...], k_ref[...], diagonal)
            m_old = m_s[...]
            m_new = jnp.maximum(m_old, jnp.max(s, axis=-1, keepdims=True))
            alpha = jnp.exp(m_old - m_new)
            p = jnp.exp(s - m_new[:, :1])
            l_s[...] = alpha * l_s[...] + jnp.sum(p, axis=-1, keepdims=True)
            acc[...] = alpha * acc[...] + lax.dot_general(p.astype(BF16), v_ref[...], (_NN, ((), ())), preferred_element_type=F32)
            m_s[...] = m_new

        pl.when(kj < qi)(lambda: step(False))
        pl.when(kj == qi)(lambda: step(True))

        @pl.when(kj == n - 1)
        def _():
            o_ref[...] = acc[...] / l_s[...]
            lse_ref[...] = m_s[...] + jnp.log(l_s[...])

    return _pc(body, grid=(H, n, n),
               in_specs=[pl.BlockSpec((None, t, W), lambda h, i, j: (h, i, 0)),
                         pl.BlockSpec((None, t, W), lambda h, i, j: (h, jnp.minimum(i, j), 0)),
                         pl.BlockSpec((None, t, LANES), lambda h, i, j: (h, jnp.minimum(i, j), 0))],
               out_specs=[pl.BlockSpec((t, LANES), lambda h, i, j: (i, h)), pl.BlockSpec((None, t, LANES), lambda h, i, j: (h, i, 0))],
               out_shape=[_sds((S_, H * LANES), F32), _sds((H, S_, LANES), F32)],
               scratch_shapes=[pltpu.VMEM((t, LANES), F32)] * 3,
               compiler_params=_cparams(("parallel", "parallel", "arbitrary")), name=name)(qh, kh, vh)


def flash_bwd(qh, kh, vh, o, lse, do, name):
    H, S_, W = qh.shape
    t = _tile(S_, FLASH_TILE)
    n = S_ // t

    def body(q_ref, k_ref, v_ref, o_ref, lse_ref, do_ref, dq_ref, dk_ref, dv_ref):
        kj, qi = pl.program_id(1), pl.program_id(2)

        @pl.when(jnp.logical_and(kj == 0, qi == 0))
        def _():
            dq_ref[...] = jnp.zeros_like(dq_ref)

        @pl.when(qi == 0)
        def _():
            dk_ref[...] = jnp.zeros_like(dk_ref)
            dv_ref[...] = jnp.zeros_like(dv_ref)

        def step(diagonal):
            q, k, v = q_ref[...], k_ref[...], v_ref[...]
            do_ = do_ref[...]
            p = jnp.exp(_scores(q, k, diagonal) - lse_ref[:, :1])
            dob = do_.astype(BF16)
            dv_ref[...] += lax.dot_general(p.astype(BF16), dob, (_TN, ((), ())), preferred_element_type=F32)
            dp = lax.dot_general(dob, v, (_NT, ((), ())), preferred_element_type=F32)
            delta = jnp.sum(do_ * o_ref[...], axis=-1, keepdims=True)
            ds = (p * (dp - delta)).astype(BF16)
            dk_ref[...] += lax.dot_general(ds, q, (_TN, ((), ())), preferred_element_type=F32)
            rows = pl.ds(pl.multiple_of(qi * t, t), t)
            dq_ref[rows, :] += lax.dot_general(ds, k, (_NN, ((), ())), preferred_element_type=F32)

        pl.when(qi > kj)(lambda: step(False))
        pl.when(qi == kj)(lambda: step(True))

    qrow = lambda h, j, i: jnp.maximum(i, j)
    return _pc(body, grid=(H, n, n),
               in_specs=[pl.BlockSpec((None, t, W), lambda h, j, i: (h, qrow(h, j, i), 0)),
                         pl.BlockSpec((None, t, W), lambda h, j, i: (h, j, 0)),
                         pl.BlockSpec((None, t, LANES), lambda h, j, i: (h, j, 0)),
                         pl.BlockSpec((t, LANES), lambda h, j, i: (qrow(h, j, i), h)),
                         pl.BlockSpec((None, t, LANES), lambda h, j, i: (h, qrow(h, j, i), 0)),
                         pl.BlockSpec((t, LANES), lambda h, j, i: (qrow(h, j, i), h))],
               out_specs=[pl.BlockSpec((None, S_, W), lambda h, j, i: (h, 0, 0)),
                          pl.BlockSpec((None, t, W), lambda h, j, i: (h, j, 0)),
                          pl.BlockSpec((None, t, LANES), lambda h, j, i: (h, j, 0))],
               out_shape=[_sds((H, S_, W), F32), _sds((H, S_, W), F32), _sds((H, S_, LANES), F32)],
               compiler_params=_cparams(("parallel", "arbitrary", "arbitrary")), name=name)(qh, kh, vh, o, lse, do)


def loss_head(x, target, g, name):
    S_ = x.shape[0]
    ts = 256

    def body(x_ref, t_ref, g_ref, l_ref, dx_ref, dg_ref):
        @pl.when(pl.program_id(0) == 0)
        def _():
            l_ref[...] = jnp.zeros_like(l_ref)
            dg_ref[...] = jnp.zeros_like(dg_ref)

        y, vjp = jax.vjp(_rms, x_ref[...], g_ref[...])
        err = y - t_ref[...]
        l_ref[...] += 0.5 * jnp.sum(jnp.sum(err * err, axis=-1, keepdims=True), axis=0, keepdims=True) / D
        dx, dg = vjp(err / D)
        dx_ref[...] = dx
        dg_ref[...] += dg

    row = pl.BlockSpec((ts, D), lambda i: (i, 0))
    return _pc(body, grid=(S_ // ts,), in_specs=[row, row, pl.BlockSpec((1, D), lambda i: (0, 0))],
               out_specs=[pl.BlockSpec((1, LANES), lambda i: (0, 0)), row, pl.BlockSpec((1, D), lambda i: (0, 0))],
               out_shape=[_sds((1, LANES), F32), _sds((S_, D), F32), _sds((1, D), F32)],
               compiler_params=_cparams(("arbitrary",)), name=name)(x, target, g)


def adamw(w, parts, m, v, name):
    R, C = w.shape
    tr = R
    for cand in (512, 256, 128, 64, 32, 16, 8):
        if R % cand == 0 and cand * C * 4 <= 1024 * 1024:
            tr = cand
            break
    c1 = 1.0 - ADAM_B1 ** ADAM_STEP
    c2 = 1.0 - ADAM_B2 ** ADAM_STEP
    npart = len(parts)

    def body(*refs):
        w_ref, m_ref, v_ref = refs[0], refs[1 + npart], refs[2 + npart]
        g_ref, d_ref, nm_ref, nv_ref = refs[3 + npart:]
        gg = None
        for p_ref in refs[1:1 + npart]:
            s = p_ref[0].astype(F32)
            for n in range(1, p_ref.shape[0]):
                s = s + p_ref[n].astype(F32)
            gg = s if gg is None else gg + s
        m2 = ADAM_B1 * m_ref[...] + (1.0 - ADAM_B1) * gg
        v2 = ADAM_B2 * v_ref[...] + (1.0 - ADAM_B2) * (gg * gg)
        g_ref[...] = gg
        d_ref[...] = -ADAM_LR * ((m2 / c1) / (jnp.sqrt(v2 / c2) + ADAM_EPS) + ADAM_WD * w_ref[...])
        nm_ref[...] = m2
        nv_ref[...] = v2

    blk = pl.BlockSpec((tr, C), lambda i: (i, 0))
    pblk = [pl.BlockSpec((p.shape[0], tr, C), lambda i: (0, i, 0)) for p in parts]
    return _pc(body, grid=(R // tr,), in_specs=[blk] + pblk + [blk, blk], out_specs=[blk] * 4, out_shape=[_sds((R, C), F32)] * 4,
               compiler_params=_cparams(("parallel",)), name=name)(w, *parts, m, v)


def sum_slots(own, recv, skip, name):
    n, R, C = recv.shape
    tr = _tile(R, 512) if R % LANES == 0 else R
    has_own = own is not None

    def body(*refs):
        skip_ref = refs[0]
        r_ref, o_ref = refs[-2], refs[-1]
        acc = refs[1][...] if has_own else jnp.zeros(o_ref.shape, F32)
        for s in range(n):
            acc = acc + jnp.where(skip_ref[0] == s, 0.0, r_ref[s].astype(F32))
        o_ref[...] = acc

    row = pl.BlockSpec((tr, C), lambda i, sk: (i, 0))
    gs = pltpu.PrefetchScalarGridSpec(
        num_scalar_prefetch=1, grid=(R // tr,),
        in_specs=([row] if has_own else []) + [pl.BlockSpec((n, tr, C), lambda i, sk: (0, i, 0))], out_specs=row)
    ins = ([own] if has_own else []) + [recv]
    return _pc(body, grid_spec=gs, out_shape=_sds((R, C), F32), compiler_params=_cparams(("parallel",)), name=name)(skip, *ins)


def _chip_peers():
    x, y, c = lax.axis_index("x"), lax.axis_index("y"), lax.axis_index("c")
    return (x, y, c), [(1 - x, y, c), (x, 1 - y, c), (1 - x, 1 - y, c)]


def _chip_index(p):
    return 2 * p[0] + p[1]


def _win(ref, axis, chip, size):
    if axis is None:
        return ref.at[chip]
    idx = [slice(None)] * len(ref.shape)
    idx[axis] = pl.ds(pl.multiple_of(chip * size, size), size)
    return ref.at[tuple(idx)]


def _remote(src, dst, send_sem, recv_sem, peer):
    return pltpu.make_async_remote_copy(src_ref=src, dst_ref=dst, send_sem=send_sem, recv_sem=recv_sem, device_id=peer,
                                        device_id_type=MESH)


def _exchange_call(body, ins, out_shape, ncopies, name):
    anyspec = pl.BlockSpec(memory_space=pl.ANY)
    return _pc(body, in_specs=[anyspec] * len(ins), out_specs=[anyspec] * len(out_shape), out_shape=out_shape,
               scratch_shapes=[pltpu.SemaphoreType.DMA((ncopies, 3)), pltpu.SemaphoreType.DMA((ncopies, 3)),
                               pltpu.SemaphoreType.DMA((ncopies,))], name=name)(*ins)


def gather_weights(items, name):
    n_in = len(items)
    pieces = [(b, l) for b, it in enumerate(items) for l in range(it[0].shape[0])]
    ncp = len(pieces)

    def body(*refs):
        in_refs, out_refs = refs[:n_in], refs[n_in:n_in + ncp]
        send, recv, local = refs[n_in + ncp:]
        me, peers = _chip_peers()
        mine = _chip_index(me)
        copies = []
        for n, (b, l) in enumerate(pieces):
            src = in_refs[b].at[l]
            dst = _win(out_refs[n], items[b][2], mine, items[b][3])
            copies.append(pltpu.make_async_copy(src, dst, local.at[n]))
            copies += [_remote(src, dst, send.at[n, k], recv.at[n, k], peer) for k, peer in enumerate(peers)]
        for cp in copies:
            cp.start()
        for cp in copies:
            cp.wait()

    out_shape = [_sds(items[b][1], items[b][0].dtype) for (b, l) in pieces]
    return _exchange_call(body, [it[0] for it in items], out_shape, ncp, name)


def scatter_grads(items, name):
    pieces = [(b, l) for b, it in enumerate(items) for l in range(len(it[0]))]
    ncp = len(pieces)

    def body(*refs):
        in_refs, out_refs = refs[:ncp], refs[ncp:ncp + len(items)]
        send, recv, local = refs[ncp + len(items):]
        me, peers = _chip_peers()
        mine = _chip_index(me)
        copies = []
        for n, (b, l) in enumerate(pieces):
            _, axis, size, _ = items[b]
            dst = out_refs[b].at[mine, l]
            copies.append(pltpu.make_async_copy(_win(in_refs[n], axis, mine, size), dst, local.at[n]))
            copies += [_remote(_win(in_refs[n], axis, _chip_index(peer), size), dst, send.at[n, k], recv.at[n, k], peer)
                       for k, peer in enumerate(peers)]
        for cp in copies:
            cp.start()
        for cp in copies:
            cp.wait()

    out_shape = [_sds((4, len(it[0])) + tuple(it[3]), it[0][0].dtype) for it in items]
    return _exchange_call(body, [p for it in items for p in it[0]], out_shape, ncp, name)


def swap_cores(bufs, name):
    nb = len(bufs)

    def body(*refs):
        in_refs, out_refs = refs[:nb], refs[nb:2 * nb]
        send, recv = refs[2 * nb:]
        x, y, c = lax.axis_index("x"), lax.axis_index("y"), lax.axis_index("c")
        copies = [_remote(in_refs[b], out_refs[b], send.at[b], recv.at[b], (x, y, 1 - c)) for b in range(nb)]
        for cp in copies:
            cp.start()
        for cp in copies:
            cp.wait()

    anyspec = pl.BlockSpec(memory_space=pl.ANY)
    return _pc(body, in_specs=[anyspec] * nb, out_specs=[anyspec] * nb, out_shape=[_sds(b.shape, b.dtype) for b in bufs],
               scratch_shapes=[pltpu.SemaphoreType.DMA((nb,)), pltpu.SemaphoreType.DMA((nb,))], name=name)(*bufs)


def exchange_all(buf, name):
    def body(in_ref, out_ref, send, recv, local):
        x, y, c = lax.axis_index("x"), lax.axis_index("y"), lax.axis_index("c")
        mine = 4 * x + 2 * y + c
        loc = pltpu.make_async_copy(in_ref, out_ref.at[mine], local)
        loc.start()
        copies = [loc]
        for k in range(1, 8):
            peer = (x ^ (k >> 2), y ^ ((k >> 1) & 1), c ^ (k & 1))
            cp = pltpu.make_async_remote_copy(src_ref=in_ref, dst_ref=out_ref.at[mine], send_sem=send.at[k - 1],
                                              recv_sem=recv.at[k - 1], device_id=peer, device_id_type=MESH)
            cp.start()
            copies.append(cp)
        for cp in copies:
            cp.wait()

    anyspec = pl.BlockSpec(memory_space=pl.ANY)
    return _pc(body, in_specs=[anyspec], out_specs=anyspec, out_shape=_sds((8,) + buf.shape, buf.dtype),
               scratch_shapes=[pltpu.SemaphoreType.DMA((7,)), pltpu.SemaphoreType.DMA((7,)), pltpu.SemaphoreType.DMA],
               name=name)(buf)


def _norm_fwd(x, g, name):
    return rowwise(f_rms, [(x, D, 0, 0)], [(g, D, 0, 0)], [(D, 0, BF16)], ts=512, name=name)[0]


def _norm_bwd(x, g, dh, dres, name):
    (dx,), (dg,) = rowwise_bwd(f_rms, [(x, D, 0, 0)], [(g, D, 0, 0)], [(dh, D, 0, 0)], need=[True],
                               adds={0: (dres, D, 0, 0)}, ts=256, name=name)
    return dx, dg


def pool_fwd(x, W, tag):
    h = _norm_fwd(x, W["ng"], tag + "_norm")
    proj = mm(h, W["w_in"], name=tag + "_in")
    p = pool_time_fwd(proj, tag + "_win")
    pg = gmm("nn", p, W["w_grp"], G=4, name=tag + "_grp")
    y = rowwise(f_pool_gate, [(pg, POOL_GROUP, 0, 1), (proj, POOL_GROUP, 4, 1)], [(W["scale"], POOL_GROUP, 0, 1)],
                [(POOL_GROUP, 1, BF16)], ncol=4, ts=512, name=tag + "_gate")[0]
    xn = mm(y, W["w_out"], add=x, name=tag + "_out")
    return xn, (x, h, proj, p, pg, y)


def pool_bwd(dxn, W, saved, tag):
    x, h, proj, p, pg, y = saved
    dy = mm(dxn, W["w_out"], tb=True, name=tag + "_dy")
    g = {"w_out": mm(y, dxn, ta=True, out_dtype=BF16, name=tag + "_dwout")}
    (dpg, dproj), (g["scale"],) = rowwise_bwd(
        f_pool_gate, [(pg, POOL_GROUP, 0, 1), (proj, POOL_GROUP, 4, 1)], [(W["scale"], POOL_GROUP, 0, 1)],
        [(dy, POOL_GROUP, 0, 1)], need=[True, True], place={1: (2 * POOL_WIDTH, 4)}, ncol=4, ts=512, name=tag + "_dgate")
    dp = gmm("nt", dpg, W["w_grp"], G=4, name=tag + "_dp")
    g["w_grp"] = gmm("tn", p, dpg, G=4, out_dtype=BF16, name=tag + "_dwgrp")
    dproj = pool_time_bwd(dp, dproj, tag + "_dwin")
    dh = mm(dproj, W["w_in"], tb=True, name=tag + "_dh")
    g["w_in"] = mm(h, dproj, ta=True, out_dtype=BF16, name=tag + "_dw_in")
    dx, g["ng"] = _norm_bwd(x, W["ng"], dh, dxn, tag + "_dnorm")
    return dx, g


def gdn_fwd(x, W, tag):
    h = _norm_fwd(x, W["ng"], tag + "_norm")
    proj = mm(h, W["w_in"], name=tag + "_in")
    qkv = gdn_conv_fwd(proj, W["conv"], tag + "_conv")
    g_b, beta_b = rowwise(f_gdn_gates, [(proj, LANES, 6144 // LANES, 0)], [(W["a_log"], LANES, 0, 0), (W["dt_bias"], LANES, 0, 0)],
                          [(GDN_QK, 0, F32), (GDN_QK, 0, F32)], ts=512, name=tag + "_gates")
    o, states = gdn_chunk_fwd(qkv, g_b, beta_b, tag + "_chunk")
    og = rowwise(f_gdn_out, [(o, GDN_DV, 0, 1), (proj, GDN_DV, 4096 // GDN_DV, 1)], [(W["norm_g"], GDN_DV, 0, 0)],
                 [(GDN_DV, 1, BF16)], ncol=GDN_H, ts=512, name=tag + "_onorm")[0]
    xn = mm(og, W["w_out"], add=x, name=tag + "_out")
    return xn, (x, h, proj, qkv, g_b, beta_b, o, states, og)


def gdn_bwd(dxn, W, saved, tag):
    x, h, proj, qkv, g_b, beta_b, o, states, og = saved
    dog = mm(dxn, W["w_out"], tb=True, name=tag + "_dog")
    g = {"w_out": mm(og, dxn, ta=True, out_dtype=BF16, name=tag + "_dwout")}
    (do, dproj), (g["norm_g"],) = rowwise_bwd(
        f_gdn_out, [(o, GDN_DV, 0, 1), (proj, GDN_DV, 4096 // GDN_DV, 1)], [(W["norm_g"], GDN_DV, 0, 0)],
        [(dog, GDN_DV, 0, 1)], need=[True, True], place={1: (GDN_IN_PAD, 4096 // GDN_DV)}, ncol=GDN_H, ts=512, name=tag + "_donorm")
    dq, dk, dv, dg_b, dbeta_b = gdn_chunk_bwd(qkv, g_b, beta_b, states, do, tag + "_dchunk")
    (dproj,), (g["a_log"], g["dt_bias"]) = rowwise_bwd(
        f_gdn_gates, [(proj, LANES, 6144 // LANES, 0)], [(W["a_log"], LANES, 0, 0), (W["dt_bias"], LANES, 0, 0)],
        [(dg_b, GDN_QK, 0, 0), (dbeta_b, GDN_QK, 0, 0)], need=[True], place={0: (dproj, 6144 // LANES)}, ts=256, name=tag + "_dgates")
    dproj, g["conv"] = gdn_conv_bwd(proj, W["conv"], dq, dk, dv, dproj, tag + "_dconv")
    dh = mm(dproj, W["w_in"], tb=True, name=tag + "_dh")
    g["w_in"] = mm(h, dproj, ta=True, out_dtype=BF16, name=tag + "_dw_in")
    dx, g["ng"] = _norm_bwd(x, W["ng"], dh, dxn, tag + "_dnorm")
    return dx, g


def mla_fwd(x, pos, W, tag):
    h = _norm_fwd(x, W["ng"], tag + "_norm")
    proj = mm(h, W["w_in"], name=tag + "_in")
    hq = rowwise(f_rms, [(proj, MLA_Q_LORA, 0, 0)], [(W["q_g"], MLA_Q_LORA, 0, 0)], [(MLA_Q_LORA, 0, BF16)], ts=512, name=tag + "_qnorm")[0]
    hkv = rowwise(f_rms, [(proj, MLA_KV_LORA, 2, 0)], [(W["kv_g"], MLA_KV_LORA, 0, 0)], [(MLA_KV_LORA, 0, BF16)], ts=512, name=tag + "_kvnorm")[0]
    qpad = mm(hq, W["w_uq"], name=tag + "_uq")
    kv = mm(hkv, W["w_ukv"], name=tag + "_ukv")
    qh, kh, vh = mla_prep_fwd(qpad, kv, proj, pos, W["rope"], tag + "_prep")
    o, lse = flash_fwd(qh, kh, vh, tag + "_attn")
    og = rowwise(f_ogate, [(o, 512, 0, 1), (proj, 512, 4, 1)], [], [(512, 1, BF16)], ncol=4, ts=512, name=tag + "_ogate")[0]
    xn = mm(og, W["w_out"], add=x, name=tag + "_out")
    return xn, (x, h, proj, hq, hkv, qh, kh, vh, o, lse, og)


def mla_bwd(dxn, pos, W, saved, tag):
    x, h, proj, hq, hkv, qh, kh, vh, o, lse, og = saved
    dog = mm(dxn, W["w_out"], tb=True, name=tag + "_dog")
    g = {"w_out": mm(og, dxn, ta=True, out_dtype=BF16, name=tag + "_dwout")}
    dproj = jnp.zeros(proj.shape, F32)
    (do, dproj), _ = rowwise_bwd(f_ogate, [(o, 512, 0, 1), (proj, 512, 4, 1)], [], [(dog, 512, 0, 1)], need=[True, True],
                                 place={1: (dproj, 4)}, ncol=4, ts=512, name=tag + "_dogate")
    dqh, dkh, dvh = flash_bwd(qh, kh, vh, o, lse, do, tag + "_dattn")
    dqpad, dkv, dproj = mla_prep_bwd(dqh, dkh, dvh, pos, W["rope"], dproj, tag + "_dprep")
    dhq = mm(dqpad, W["w_uq"], tb=True, name=tag + "_dhq")
    g["w_uq"] = mm(hq, dqpad, ta=True, out_dtype=BF16, name=tag + "_dwuq")
    dhkv = mm(dkv, W["w_ukv"], tb=True, name=tag + "_dhkv")
    g["w_ukv"] = mm(hkv, dkv, ta=True, out_dtype=BF16, name=tag + "_dwukv")
    (dproj,), (g["q_g"],) = rowwise_bwd(f_rms, [(proj, MLA_Q_LORA, 0, 0)], [(W["q_g"], MLA_Q_LORA, 0, 0)], [(dhq, MLA_Q_LORA, 0, 0)],
                                        need=[True], place={0: (dproj, 0)}, ts=256, name=tag + "_dqnorm")
    (dproj,), (g["kv_g"],) = rowwise_bwd(f_rms, [(proj, MLA_KV_LORA, 2, 0)], [(W["kv_g"], MLA_KV_LORA, 0, 0)], [(dhkv, MLA_KV_LORA, 0, 0)],
                                         need=[True], place={0: (dproj, 2)}, ts=256, name=tag + "_dkvnorm")
    dh = mm(dproj, W["w_in"], tb=True, name=tag + "_dh")
    g["w_in"] = mm(h, dproj, ta=True, out_dtype=BF16, name=tag + "_dw_in")
    dx, g["ng"] = _norm_bwd(x, W["ng"], dh, dxn, tag + "_dnorm")
    return dx, g


def _pad_cols(a, n):
    return jnp.pad(a, ((0, 0), (0, n - a.shape[1])))


def _mla_w_in_layout(w):
    z = lambda n: jnp.zeros((w.shape[0], n), w.dtype)
    kr = w[:, 1280:1344]
    return jnp.concatenate([w[:, :768], z(256), w[:, 768:1280], kr[:, :32], z(32), kr[:, 32:], z(32), z(384), w[:, 1344:]], axis=1)


def _mla_w_in_unlayout(g):
    return jnp.concatenate([g[:, :768], g[:, 1024:1536], g[:, 1536:1568], g[:, 1600:1632], g[:, 2048:]], axis=1)


def _mla_w_uq_layout(w):
    w3 = w.reshape(w.shape[0], MLA_H, MLA_NOPE + MLA_ROPE)
    z = jnp.zeros((w.shape[0], MLA_H, 32), w.dtype)
    return jnp.concatenate([w3[..., :128], w3[..., 128:160], z, w3[..., 160:192], z], axis=-1).reshape(w.shape[0], MLA_H * 256)


def _mla_w_uq_unlayout(g):
    g3 = g.reshape(g.shape[0], MLA_H, 256)
    return jnp.concatenate([g3[..., :128], g3[..., 128:160], g3[..., 192:224]], axis=-1).reshape(g.shape[0], MLA_H * 192)


def _rope_consts():
    half = MLA_ROPE // 2
    inv = ROPE_THETA ** (-jnp.arange(half, dtype=F32) / half)
    z = jnp.zeros((half,), F32)
    o = jnp.ones((half,), F32)
    row = lambda *p: jnp.concatenate(p).reshape(1, LANES)
    return row(inv, z, inv, z), row(o, z, o, z), row(-o, z, o, z)


BIG = ["pool_w_in", "pool_w_grp", "pool_w_out", "gdn_w_in", "gdn_w_out", "mla_w_in", "mla_w_uq", "mla_w_ukv", "mla_w_out"]
BIG_LAYOUT = {"pool_w_in": (1, 1024, (1024, 4096)), "pool_w_grp": (1, 128, (4, 512, 512)), "pool_w_out": (0, 512, (2048, 1024)),
              "gdn_w_in": (None, None, (4, 1024, 1540)), "gdn_w_out": (0, 512, (2048, 1024)),
              "mla_w_in": (None, None, (4, 1024, 848)), "mla_w_uq": (1, 768, (768, 3072)), "mla_w_ukv": (1, 1024, (512, 4096)),
              "mla_w_out": (0, 512, (2048, 1024))}
SMALL_SHARDED = ["pool_scale", "gdn_conv", "mla_q_norm_g", "mla_kv_norm_g"]
SMALL_AXIS = {"pool_scale": 1, "gdn_conv": 2, "mla_q_norm_g": 1, "mla_kv_norm_g": 1}
REPLICATED = ["norm_g", "gdn_a_log", "gdn_dt_bias", "gdn_norm_g", "final_g"]
PACK_C = 1024


def _pack(parts, dtype, row_mult):
    flat = jnp.concatenate([p.reshape(-1).astype(dtype) for p in parts])
    rows = -(-flat.shape[0] // PACK_C)
    rows = -(-rows // row_mult) * row_mult
    return jnp.pad(flat, (0, rows * PACK_C - flat.shape[0])).reshape(rows, PACK_C)


def _unpack(buf, shapes):
    lead = buf.shape[:-2]
    flat = buf.reshape(lead + (-1,))
    out, off = [], 0
    for s in shapes:
        n = int(np.prod(s))
        out.append(flat[..., off:off + n].reshape(lead + tuple(s)))
        off += n
    return out


def _unshard(g4, axis):
    a = jnp.moveaxis(g4, 0, axis)
    s = a.shape
    return a.reshape(s[:axis] + (s[axis] * s[axis + 1],) + s[axis + 2:])


def _to_shards(a, axis):
    s = a.shape
    return jnp.moveaxis(a.reshape(s[:axis] + (4, s[axis] // 4) + s[axis + 1:]), axis, 0)


def layer_weights(full, small, rep):
    ng = lambda i: rep["norm_g"][i:i + 1]
    pool = lambda j, i: dict(ng=ng(i), w_in=full["pool_w_in"][j], w_grp=full["pool_w_grp"][j], scale=small["pool_scale"][j:j + 1],
                             w_out=full["pool_w_out"][j])
    side_by_side = lambda a4: jnp.moveaxis(a4, 0, 1).reshape(a4.shape[1], 4 * a4.shape[2])
    gdn = dict(ng=ng(1), w_in=_pad_cols(side_by_side(full["gdn_w_in"][0]), GDN_IN_PAD),
               conv=jnp.pad(small["gdn_conv"][0], ((0, 4), (0, 0))), a_log=_pad_cols(rep["gdn_a_log"], LANES),
               dt_bias=_pad_cols(rep["gdn_dt_bias"], LANES), norm_g=rep["gdn_norm_g"], w_out=full["gdn_w_out"][0])
    mla = dict(ng=ng(2), w_in=_mla_w_in_layout(side_by_side(full["mla_w_in"][0])), q_g=small["mla_q_norm_g"],
               kv_g=small["mla_kv_norm_g"], w_uq=_mla_w_uq_layout(full["mla_w_uq"][0]), w_ukv=full["mla_w_ukv"][0],
               w_out=full["mla_w_out"][0], rope=_rope_consts())
    return dict(l0=pool(0, 0), l1=gdn, l2=mla, l3=pool(1, 3), final_g=rep["final_g"].reshape(1, D))


def local_step(x, pos, target, W):
    x1, s0 = pool_fwd(x, W["l0"], "l0")
    x2, s1 = gdn_fwd(x1, W["l1"], "l1")
    x3, s2 = mla_fwd(x2, pos, W["l2"], "l2")
    x4, s3 = pool_fwd(x3, W["l3"], "l3")
    loss, dx4, dfinal = loss_head(x4, target, W["final_g"], "loss_head")
    dx3, g3 = pool_bwd(dx4, W["l3"], s3, "l3")
    dx2, g2 = mla_bwd(dx3, pos, W["l2"], s2, "l2")
    dx1, g1 = gdn_bwd(dx2, W["l1"], s1, "l1")
    dx0, g0 = pool_bwd(dx1, W["l0"], s0, "l0")
    return loss, dx0, (g0, g1, g2, g3), dfinal


def big_grad_pieces(gl):
    g0, g1, g2, g3 = gl
    slots = lambda a: jnp.moveaxis(a.reshape(a.shape[0], 4, a.shape[1] // 4), 1, 0)
    return {"pool_w_in": [g0["w_in"], g3["w_in"]], "pool_w_grp": [g0["w_grp"], g3["w_grp"]], "pool_w_out": [g0["w_out"], g3["w_out"]],
            "gdn_w_in": [slots(g1["w_in"][:, :GDN_IN])], "gdn_w_out": [g1["w_out"]],
            "mla_w_in": [slots(_mla_w_in_unlayout(g2["w_in"]))], "mla_w_uq": [_mla_w_uq_unlayout(g2["w_uq"])],
            "mla_w_ukv": [g2["w_ukv"]], "mla_w_out": [g2["w_out"]]}


def small_grads(gl, dfinal):
    g0, g1, g2, g3 = gl
    return {"norm_g": jnp.concatenate([g0["ng"], g1["ng"], g2["ng"], g3["ng"]], axis=0),
            "pool_scale": jnp.concatenate([g0["scale"], g3["scale"]], axis=0), "gdn_conv": g1["conv"][None, :4],
            "gdn_a_log": g1["a_log"][:, :GDN_H], "gdn_dt_bias": g1["dt_bias"][:, :GDN_H], "gdn_norm_g": g1["norm_g"],
            "mla_q_norm_g": g2["q_g"], "mla_kv_norm_g": g2["kv_g"], "final_g": dfinal.reshape(D)}


NAMES = ["norm_g", "pool_w_in", "pool_w_grp", "pool_scale", "pool_w_out", "gdn_w_in", "gdn_conv", "gdn_a_log", "gdn_dt_bias",
         "gdn_norm_g", "gdn_w_out", "mla_w_in", "mla_q_norm_g", "mla_w_uq", "mla_kv_norm_g", "mla_w_ukv", "mla_w_out", "final_g"]


def kernel(x, positions, norm_g, pool_w_in, pool_w_grp, pool_scale, pool_w_out, gdn_w_in, gdn_conv, gdn_a_log, gdn_dt_bias, gdn_norm_g, gdn_w_out, mla_w_in, mla_q_norm_g, mla_w_uq, mla_kv_norm_g, mla_w_ukv, mla_w_out, final_g, loss_target, m_norm_g, m_pool_w_in, m_pool_w_grp, m_pool_scale, m_pool_w_out, m_gdn_w_in, m_gdn_conv, m_gdn_a_log, m_gdn_dt_bias, m_gdn_norm_g, m_gdn_w_out, m_mla_w_in, m_mla_q_norm_g, m_mla_w_uq, m_mla_kv_norm_g, m_mla_w_ukv, m_mla_w_out, m_final_g, v_norm_g, v_pool_w_in, v_pool_w_grp, v_pool_scale, v_pool_w_out, v_gdn_w_in, v_gdn_conv, v_gdn_a_log, v_gdn_dt_bias, v_gdn_norm_g, v_gdn_w_out, v_mla_w_in, v_mla_q_norm_g, v_mla_w_uq, v_mla_kv_norm_g, v_mla_w_ukv, v_mla_w_out, v_final_g):
    args = locals()
    w = {n: args[n] for n in NAMES}
    m = {n: args["m_" + n] for n in NAMES}
    v = {n: args["v_" + n] for n in NAMES}
    my_chip = (2 * lax.axis_index("x") + lax.axis_index("y")).astype(I32)

    small_shapes = [w[n].shape for n in SMALL_SHARDED]
    small_pack = _pack([w[n] for n in SMALL_SHARDED], F32, 8)[None]
    items = [(w[n].astype(BF16), BIG_LAYOUT[n][2], BIG_LAYOUT[n][0], BIG_LAYOUT[n][1]) for n in BIG]
    items.append((small_pack, (4,) + small_pack.shape[1:], None, None))
    got = iter(gather_weights(items, "gather_weights"))
    full = {n: [next(got) for _ in range(w[n].shape[0])] for n in BIG}
    small = {n: _unshard(a, SMALL_AXIS[n]) for n, a in zip(SMALL_SHARDED, _unpack(next(got), small_shapes))}
    W = layer_weights(full, small, {n: w[n] for n in REPLICATED})

    S_ = x.shape[1]
    loss_part, dx, gl, dfinal = local_step(x[0], positions.reshape(S_, 1).astype(F32), loss_target[0], W)

    pieces = big_grad_pieces(gl)
    recv = scatter_grads([(pieces[n], BIG_LAYOUT[n][0], BIG_LAYOUT[n][1], w[n].shape[1:]) for n in BIG], "scatter_grads")
    sib = swap_cores(recv, "swap_cores")
    sg = small_grads(gl, dfinal)
    small_names = SMALL_SHARDED + REPLICATED
    small_buf = _pack([sg[n] for n in small_names] + [loss_part], F32, 8)
    small_sum = sum_slots(None, exchange_all(small_buf, "gather_small"), jnp.full((1,), -1, I32), "sum_small")
    full_small = _unpack(small_sum, [sg[n].shape for n in small_names] + [(1, LANES)])
    loss = full_small[-1][0, 0]
    parts = {}
    for n, a in zip(small_names, full_small[:-1]):
        if n in SMALL_AXIS:
            a = lax.dynamic_index_in_dim(_to_shards(a, SMALL_AXIS[n]), my_chip, axis=0, keepdims=False)
        parts[n] = [a]
    for n, r, s in zip(BIG, recv, sib):
        parts[n] = [r, s]

    outs = []
    for n in NAMES:
        shp = w[n].shape
        two = (int(np.prod(shp[:-1])), shp[-1]) if len(shp) > 1 else (1, shp[0])
        res = adamw(w[n].reshape(two), [p.reshape((-1,) + two) for p in parts[n]], m[n].reshape(two), v[n].reshape(two), "adamw_" + n)
        outs.append([r.reshape(shp) for r in res])
    return (loss, dx[None], *[o[0] for o in outs], *[o[1] for o in outs], *[o[2] for o in outs], *[o[3] for o in outs])
```

```python
import functools
import math

import jax
import jax.numpy as jnp
import numpy as np
from jax import lax
from jax.experimental import pallas as pl
from jax.experimental.pallas import tpu as pltpu

F32 = jnp.float32
BF16 = jnp.bfloat16
I32 = jnp.int32

D = 1024
EPS = 1e-6
POOL_WIDTH = 2048
POOL_GROUP = 512
GDN_H, GDN_DK, GDN_DV, GDN_C = 8, 128, 256, 64
GDN_QK, GDN_V, GDN_CONV_CH, GDN_IN = 1024, 2048, 4096, 6160
GDN_IN_PAD = 6272
MLA_H, MLA_NOPE, MLA_ROPE, MLA_V = 16, 128, 64, 128
MLA_Q_LORA, MLA_KV_LORA, MLA_WIDTH, MLA_IN = 768, 512, 2048, 3392
MLA_IN_PAD = 4096
MLA_SCALE = (MLA_NOPE + MLA_ROPE) ** -0.5
ROPE_THETA = 10000.0
ADAM_LR, ADAM_B1, ADAM_B2, ADAM_EPS, ADAM_WD, ADAM_STEP = 0.001, 0.9, 0.999, 1e-08, 0.01, 10

VMEM_LIMIT_V7X = 56 * 1024 * 1024
LANES = 128
MESH = pl.DeviceIdType.MESH


def _pc(body, **kw):
    return pl.pallas_call(body, **kw)


def _cparams(sem):
    return pltpu.CompilerParams(dimension_semantics=sem, vmem_limit_bytes=VMEM_LIMIT_V7X)


def _tile(n, cap):
    t = (cap // LANES) * LANES
    while t >= LANES:
        if n % t == 0:
            return t
        t -= LANES
    return n


def _sds(shape, dt):
    return jax.ShapeDtypeStruct(shape, dt)


def mm(a, b, *, ta=False, tb=False, add=None, out_dtype=F32, name):
    if ta:
        K, M = a.shape
    else:
        M, K = a.shape
    if tb:
        N, K2 = b.shape
    else:
        K2, N = b.shape
    assert K == K2, (a.shape, b.shape, ta, tb)
    tm, tn, tk = _tile(M, 1024), _tile(N, 1024), _tile(K, 1024)
    nk = K // tk
    a_spec = pl.BlockSpec((tk, tm), lambda i, j, k: (k, i)) if ta else pl.BlockSpec((tm, tk), lambda i, j, k: (i, k))
    b_spec = pl.BlockSpec((tn, tk), lambda i, j, k: (j, k)) if tb else pl.BlockSpec((tk, tn), lambda i, j, k: (k, j))
    o_spec = pl.BlockSpec((tm, tn), lambda i, j, k: (i, j))
    dn = (((0 if ta else 1,), (1 if tb else 0,)), ((), ()))
    has_add = add is not None

    def body(*refs):
        a_ref, b_ref = refs[0], refs[1]
        o_ref, acc = refs[-2], refs[-1]
        k = pl.program_id(2)

        @pl.when(k == 0)
        def _():
            acc[...] = jnp.zeros_like(acc)

        acc[...] += lax.dot_general(a_ref[...].astype(BF16), b_ref[...].astype(BF16), dn, preferred_element_type=F32)

        @pl.when(k == nk - 1)
        def _():
            r = acc[...]
            if has_add:
                r = r + refs[2][...]
            o_ref[...] = r.astype(out_dtype)

    ins = [a, b] + ([add] if has_add else [])
    specs = [a_spec, b_spec] + ([o_spec] if has_add else [])
    return _pc(body, grid=(M // tm, N // tn, nk), in_specs=specs, out_specs=o_spec, out_shape=_sds((M, N), out_dtype),
               scratch_shapes=[pltpu.VMEM((tm, tn), F32)], compiler_params=_cparams(("parallel", "parallel", "arbitrary")),
               name=name)(*ins)


def gmm(kind, a, b, *, G, name, out_dtype=F32):
    S_ = a.shape[0]
    Ka = a.shape[1] // G
    if kind == "tn":
        N = b.shape[1] // G
        tk = _tile(S_, 512)
        nk = S_ // tk

        def body(a_ref, b_ref, o_ref, acc):
            k = pl.program_id(1)

            @pl.when(k == 0)
            def _():
                acc[...] = jnp.zeros_like(acc)

            acc[...] += lax.dot_general(a_ref[...].astype(BF16), b_ref[...].astype(BF16), (((0,), (0,)), ((), ())),
                                        preferred_element_type=F32)

            @pl.when(k == nk - 1)
            def _():
                o_ref[...] = acc[...].astype(out_dtype)

        return _pc(body, grid=(G, nk),
                   in_specs=[pl.BlockSpec((tk, Ka), lambda g, k: (k, g)), pl.BlockSpec((tk, N), lambda g, k: (k, g))],
                   out_specs=pl.BlockSpec((None, Ka, N), lambda g, k: (g, 0, 0)), out_shape=_sds((G, Ka, N), out_dtype),
                   scratch_shapes=[pltpu.VMEM((Ka, N), F32)], compiler_params=_cparams(("parallel", "arbitrary")), name=name)(a, b)
    N = b.shape[2] if kind == "nn" else b.shape[1]
    tm = _tile(S_, 1024)
    dn = (((1,), (0 if kind == "nn" else 1,)), ((), ()))

    def body(a_ref, b_ref, o_ref):
        o_ref[...] = lax.dot_general(a_ref[...].astype(BF16), b_ref[...].astype(BF16), dn, preferred_element_type=F32)

    bshape = (None,) + tuple(b.shape[1:])
    return _pc(body, grid=(G, S_ // tm),
               in_specs=[pl.BlockSpec((tm, Ka), lambda g, i: (i, g)), pl.BlockSpec(bshape, lambda g, i: (g, 0, 0))],
               out_specs=pl.BlockSpec((tm, N), lambda g, i: (i, g)), out_shape=_sds((S_, G * N), F32),
               compiler_params=_cparams(("parallel", "parallel")), name=name)(a, b)


def _rw_spec(ts, w, c, s):
    return pl.BlockSpec((ts, w), lambda j, i: (i, c + j * s))


def _rw_pspec(p, w, c, s):
    return pl.BlockSpec((p.shape[0], w), lambda j, i: (0, c + j * s))


def rowwise(f, tiles, params, outs, *, ncol=1, ts, name):
    S_ = tiles[0][0].shape[0]
    nin = len(tiles) + len(params)

    def body(*refs):
        res = f(pl.program_id(0), *[r[...] for r in refs[:nin]])
        for r, o in zip(refs[nin:], res):
            r[...] = o.astype(r.dtype)

    return _pc(body, grid=(ncol, S_ // ts),
               in_specs=[_rw_spec(ts, w, c, s) for (_, w, c, s) in tiles] + [_rw_pspec(*p) for p in params],
               out_specs=[_rw_spec(ts, w, 0, s) for (w, s, _) in outs],
               out_shape=[_sds((S_, w * (ncol if s else 1)), dt) for (w, s, dt) in outs],
               compiler_params=_cparams(("parallel", "parallel")), name=name)(*[t[0] for t in tiles], *[p[0] for p in params])


def rowwise_bwd(f, tiles, params, cots, *, need, adds=None, place=None, ncol=1, ts, name):
    S_ = tiles[0][0].shape[0]
    adds = adds or {}
    place = place or {}
    nt, npar, nc = len(tiles), len(params), len(cots)
    add_keys = sorted(adds)
    need_idx = [k for k in range(nt) if need[k]]
    into_keys = [k for k in need_idx if k in place and not isinstance(place[k][0], int)]
    n_extra = len(add_keys) + len(into_keys)

    def body(*refs):
        j, i = pl.program_id(0), pl.program_id(1)
        vals = [r[...] for r in refs[:nt + npar]]
        cvals = tuple(r[...] for r in refs[nt + npar:nt + npar + nc])
        add_refs = refs[nt + npar + nc:nt + npar + nc + len(add_keys)]
        out_refs = refs[nt + npar + nc + n_extra:]
        _, vjp = jax.vjp(lambda *v: tuple(f(j, *v)), *vals)
        grads = vjp(cvals)
        for n, k in enumerate(need_idx):
            g = grads[k]
            if k in adds:
                g = g + add_refs[add_keys.index(k)][...]
            out_refs[n][...] = g
        for n in range(npar):
            ref = out_refs[len(need_idx) + n]
            first = (i == 0) if params[n][3] else jnp.logical_and(i == 0, j == 0)

            @pl.when(first)
            def _():
                ref[...] = jnp.zeros_like(ref)

            ref[...] += grads[nt + n]

    in_specs = ([_rw_spec(ts, w, c, s) for (_, w, c, s) in tiles] + [_rw_pspec(*p) for p in params]
                + [_rw_spec(ts, w, c, s) for (_, w, c, s) in cots] + [_rw_spec(ts, *adds[k][1:]) for k in add_keys]
                + [pl.BlockSpec(memory_space=pl.ANY) for _ in into_keys])
    out_specs, out_shape, aliases = [], [], {}
    for n, k in enumerate(need_idx):
        w, s = tiles[k][1], tiles[k][3]
        if k in place:
            dst, c0 = place[k]
            total = dst if isinstance(dst, int) else dst.shape[1]
            out_specs.append(_rw_spec(ts, w, c0, s))
            out_shape.append(_sds((S_, total), F32))
            if k in into_keys:
                aliases[nt + npar + nc + len(add_keys) + into_keys.index(k)] = n
        else:
            out_specs.append(_rw_spec(ts, w, 0, s))
            out_shape.append(_sds((S_, w * (ncol if s else 1)), F32))
    out_specs += [_rw_pspec(p[0], p[1], p[2], p[3]) for p in params]
    out_shape += [_sds(p[0].shape, F32) for p in params]
    res = _pc(body, grid=(ncol, S_ // ts), in_specs=in_specs, out_specs=out_specs, out_shape=out_shape,
              input_output_aliases=aliases, compiler_params=_cparams(("arbitrary", "arbitrary")), name=name)(
        *[t[0] for t in tiles], *[p[0] for p in params], *[c[0] for c in cots], *[adds[k][0] for k in add_keys],
        *[place[k][0] for k in into_keys])
    return list(res[:len(need_idx)]), list(res[len(need_idx):])


def _rms(x, g):
    r = lax.rsqrt(jnp.mean(x * x, axis=-1, keepdims=True) + EPS)
    return x * r * g


def _silu(x):
    return x * jax.nn.sigmoid(x)


@jax.custom_vjp
def _softplus(x):
    return jnp.maximum(x, 0.0) + jnp.log1p(jnp.exp(-jnp.abs(x)))


_softplus.defvjp(lambda x: (_softplus(x), x), lambda x, d: (d * jax.nn.sigmoid(x),))


def f_rms(j, x, g):
    return (_rms(x, g),)


def f_pool_gate(j, pg, gate, scale):
    return (pg * scale * _silu(gate),)


def f_ogate(j, o, gate):
    return (o * _silu(gate),)


def f_gdn_out(j, o, gate, g):
    return (_rms(o, g) * _silu(gate),)


def f_gdn_gates(j, ba, alog, dtb):
    lane = lax.broadcasted_iota(I32, (1, LANES), 1)
    gs, bs = [], []
    for h in range(GDN_H):
        eb = (lane == h).astype(F32)
        ea = (lane == GDN_H + h).astype(F32)
        b = jnp.sum(ba * eb, -1, keepdims=True)
        a = jnp.sum(ba * ea, -1, keepdims=True)
        al = jnp.sum(alog * eb, -1, keepdims=True)
        dt = jnp.sum(dtb * eb, -1, keepdims=True)
        g = -jnp.exp(al) * _softplus(a + dt)
        gs.append(jnp.broadcast_to(g, ba.shape))
        bs.append(jnp.broadcast_to(jax.nn.sigmoid(b), ba.shape))
    return jnp.concatenate(gs, 1), jnp.concatenate(bs, 1)


def _shift_dn(x, k):
    rows = lax.broadcasted_iota(I32, x.shape, 0)
    return jnp.where(rows < k, 0.0, pltpu.roll(x, k, 0))


def _shift_up(x, k):
    n = x.shape[0]
    rows = lax.broadcasted_iota(I32, x.shape, 0)
    return jnp.where(rows >= n - k, 0.0, pltpu.roll(x, n - k, 0))


def _pool_window(j):
    g = lax.div(j, POOL_GROUP // LANES)
    return jnp.where(g == 0, 2.0, jnp.where(g == 1, 4.0, jnp.where(g == 2, 8.0, 16.0))), g


def _pick(g, a2, a4, a8, a16):
    return jnp.where(g == 0, a2, jnp.where(g == 1, a4, jnp.where(g == 2, a8, a16)))


def pool_time_fwd(proj, name):
    S_ = proj.shape[0]

    def body(u_ref, p_ref):
        u = u_ref[...]
        w, g = _pool_window(pl.program_id(0))
        s2 = u + _shift_dn(u, 1)
        s4 = s2 + _shift_dn(s2, 2)
        s8 = s4 + _shift_dn(s4, 4)
        s16 = s8 + _shift_dn(s8, 8)
        t1 = (lax.broadcasted_iota(I32, u.shape, 0) + 1).astype(F32)
        p_ref[...] = (_pick(g, s2, s4, s8, s16) / jnp.minimum(t1, w) - u).astype(p_ref.dtype)

    return _pc(body, grid=(POOL_WIDTH // LANES,), in_specs=[pl.BlockSpec((S_, LANES), lambda j: (0, j))],
               out_specs=pl.BlockSpec((S_, LANES), lambda j: (0, j)), out_shape=_sds((S_, POOL_WIDTH), BF16),
               compiler_params=_cparams(("parallel",)), name=name)(proj)


def pool_time_bwd(dp, into, name):
    S_ = dp.shape[0]

    def body(dp_ref, _, du_ref):
        d = dp_ref[...]
        w, g = _pool_window(pl.program_id(0))
        t1 = (lax.broadcasted_iota(I32, d.shape, 0) + 1).astype(F32)
        q = d / jnp.minimum(t1, w)
        r2 = q + _shift_up(q, 1)
        r4 = r2 + _shift_up(r2, 2)
        r8 = r4 + _shift_up(r4, 4)
        r16 = r8 + _shift_up(r8, 8)
        du_ref[...] = _pick(g, r2, r4, r8, r16) - d

    return _pc(body, grid=(POOL_WIDTH // LANES,),
               in_specs=[pl.BlockSpec((S_, LANES), lambda j: (0, j)), pl.BlockSpec(memory_space=pl.ANY)],
               out_specs=pl.BlockSpec((S_, LANES), lambda j: (0, j)), out_shape=_sds(into.shape, F32),
               input_output_aliases={1: 0}, compiler_params=_cparams(("parallel",)), name=name)(dp, into)


def _conv_post(j, a):
    n = a * lax.rsqrt(jnp.sum(a * a, axis=-1, keepdims=True) + EPS)
    nq = GDN_QK // LANES
    return jnp.where(j < nq, n * (GDN_DK ** -0.5), jnp.where(j < 2 * nq, n, a))


def _conv_pre(u, w):
    return w[3:4] * u + w[2:3] * _shift_dn(u, 1) + w[1:2] * _shift_dn(u, 2) + w[0:1] * _shift_dn(u, 3)


def gdn_conv_fwd(proj, conv_w, name):
    S_ = proj.shape[0]

    def body(u_ref, w_ref, o_ref):
        o_ref[...] = _conv_post(pl.program_id(0), _silu(_conv_pre(u_ref[...], w_ref[...])))

    return _pc(body, grid=(GDN_CONV_CH // LANES,),
               in_specs=[pl.BlockSpec((S_, LANES), lambda j: (0, j)), pl.BlockSpec((8, LANES), lambda j: (0, j))],
               out_specs=pl.BlockSpec((S_, LANES), lambda j: (0, j)), out_shape=_sds((S_, GDN_CONV_CH), F32),
               compiler_params=_cparams(("parallel",)), name=name)(proj, conv_w)


def gdn_conv_bwd(proj, conv_w, dq, dk, dv, into, name):
    S_ = proj.shape[0]
    nq = GDN_QK // LANES

    def body(u_ref, w_ref, dq_ref, dk_ref, dv_ref, _, du_ref, dw_ref):
        j = pl.program_id(0)
        u, w = u_ref[...], w_ref[...]
        c = _conv_pre(u, w)
        sig = jax.nn.sigmoid(c)
        dout = jnp.where(j < nq, dq_ref[...], jnp.where(j < 2 * nq, dk_ref[...], dv_ref[...]))
        _, vjp = jax.vjp(lambda a: _conv_post(j, a), c * sig)
        dc = vjp(dout)[0] * (sig * (1.0 + c * (1.0 - sig)))
        du_ref[...] = w[3:4] * dc + w[2:3] * _shift_up(dc, 1) + w[1:2] * _shift_up(dc, 2) + w[0:1] * _shift_up(dc, 3)
        rows = lax.broadcasted_iota(I32, (8, LANES), 0)
        dw = jnp.zeros((8, LANES), F32)
        for k in range(4):
            us = u if k == 3 else _shift_dn(u, 3 - k)
            dw = dw + jnp.where(rows == k, jnp.sum(dc * us, axis=0, keepdims=True), 0.0)
        dw_ref[...] = dw

    blk = lambda f: pl.BlockSpec((S_, LANES), f)
    return _pc(body, grid=(GDN_CONV_CH // LANES,),
               in_specs=[blk(lambda j: (0, j)), pl.BlockSpec((8, LANES), lambda j: (0, j)),
                         blk(lambda j: (0, jnp.minimum(j, nq - 1))), blk(lambda j: (0, jnp.clip(j - nq, 0, nq - 1))),
                         blk(lambda j: (0, jnp.clip(j - 2 * nq, 0, 2 * nq - 1))), pl.BlockSpec(memory_space=pl.ANY)],
               out_specs=[blk(lambda j: (0, j)), pl.BlockSpec((8, LANES), lambda j: (0, j))],
               out_shape=[_sds(into.shape, F32), _sds((8, GDN_CONV_CH), F32)], input_output_aliases={5: 0},
               compiler_params=_cparams(("parallel",)), name=name)(proj, conv_w, dq, dk, dv, into)


_NN, _NT, _TN = ((1,), (0,)), ((1,), (1,)), ((0,), (0,))


def _split(x, n):
    parts = []
    for _ in range(n):
        h = x.astype(BF16)
        parts.append(h)
        x = x - h.astype(F32)
    return parts


def _dot(a, b, dn, mode):
    d = lambda p, q: lax.dot_general(p, q, (dn, ((), ())), preferred_element_type=F32)
    if mode == "lo":
        return d(a.astype(BF16), b.astype(BF16))
    if mode == "x3":
        (ah, al), (bh, bl) = _split(a, 2), _split(b, 2)
        return d(ah, bh) + (d(ah, bl) + d(al, bh))
    b0, b1, b2 = _split(b, 3)
    ab = a.astype(BF16)
    return d(ab, b0) + (d(ab, b1) + d(ab, b2))


def _make_dots(mode):
    @jax.custom_vjp
    def nn(a, b):
        return _dot(a, b, _NN, mode)

    @jax.custom_vjp
    def nt(a, b):
        return _dot(a, b, _NT, mode)

    @jax.custom_vjp
    def tn(a, b):
        return _dot(a, b, _TN, mode)

    nn.defvjp(lambda a, b: (nn(a, b), (a, b)), lambda r, d: (nt(d, r[1]), tn(r[0], d)))
    nt.defvjp(lambda a, b: (nt(a, b), (a, b)), lambda r, d: (nn(d, r[1]), tn(d, r[0])))
    tn.defvjp(lambda a, b: (tn(a, b), (a, b)), lambda r, d: (nt(r[1], d), nn(r[0], d)))
    return nn, nt, tn


_nn_hi, _nt_hi, _tn_hi = _make_dots("x3")
_nn_lo, _nt_lo, _tn_lo = _make_dots("lo")


@jax.custom_vjp
def _nn_const(a, b):
    return _dot(a, b, _NN, "xl")


_nn_const.defvjp(lambda a, b: (_nn_const(a, b), a), lambda a, d: (jnp.zeros_like(a), _dot(a, d, _TN, "xl")))


def _each(f, *lists):
    return [f(*xs) for xs in zip(*lists)]


def _gdn_chunk(q, k, v, gb, bb, state):
    C = GDN_C
    e0 = (lax.broadcasted_iota(I32, (1, LANES), 1) == 0).astype(F32)
    ri = lax.broadcasted_iota(I32, (C, C), 0)
    ci = lax.broadcasted_iota(I32, (C, C), 1)
    causal, strict = ri >= ci, ri > ci
    tri, eye, ones = causal.astype(F32), (ri == ci).astype(F32), jnp.ones((C, C), F32)
    last = lax.broadcasted_iota(I32, (C, LANES), 0) == C - 1
    g1 = _each(lambda a: jnp.sum(a * e0, -1, keepdims=True), gb)
    b1 = _each(lambda a: jnp.sum(a * e0, -1, keepdims=True), bb)
    gc_c = _each(lambda g: _nn_const(tri, jnp.broadcast_to(g, (C, C))), g1)
    gc_d = _each(lambda g: _nn_const(tri, jnp.broadcast_to(g, (C, LANES))), g1)
    gr_c = _each(lambda g: _nn_const(ones, eye * g), gc_c)
    decay = _each(lambda a, r: jnp.where(causal, jnp.exp(jnp.where(causal, a - r, 0.0)), 0.0), gc_c, gr_c)
    kb = _each(lambda a, b: a * b, k, b1)
    vb = _each(lambda a, b: a * b, v, b1)
    x = _each(lambda a, b, d: -jnp.where(strict, _nt_lo(a, b) * d, 0.0), kb, k, decay)
    ainv, p = _each(lambda a: eye + a, x), x
    for _ in range(5):
        p = _each(lambda a: _nn_hi(a, a), p)
        ainv = _each(lambda a, b: a + _nn_hi(a, b), ainv, p)
    u = _each(_nn_hi, ainv, vb)
    w = _each(lambda a, b, g: _nn_hi(a, b * jnp.exp(g)), ainv, kb, gc_d)
    attn = _each(lambda a, b, d: jnp.where(causal, _nt_lo(a, b) * d, 0.0), q, k, decay)
    v_new = _each(lambda a, b, s: a - _nn_lo(b, s), u, w, state)
    o = _each(lambda a, g, s, t, vn: _nn_lo(a * jnp.exp(g), s) + _nn_lo(t, vn), q, gc_d, state, attn, v_new)
    gl = _each(lambda g: jnp.sum(jnp.where(last, g, 0.0), axis=0, keepdims=True), gc_d)
    new_state = _each(lambda s, g, a, gd, vn: s * jnp.exp(jnp.sum(g * e0, -1, keepdims=True)) + _tn_lo(a * jnp.exp(g - gd), vn),
                      state, gl, k, gc_d, v_new)
    return o, new_state


def _head_slices(ref, width):
    return [ref[:, h * width:(h + 1) * width] for h in range(GDN_H)]


def gdn_chunk_fwd(qkv, g_b, beta_b, name):
    S_ = qkv.shape[0]
    N = S_ // GDN_C

    def body(q_ref, k_ref, v_ref, g_ref, b_ref, o_ref, s_ref, state):
        @pl.when(pl.program_id(0) == 0)
        def _():
            state[...] = jnp.zeros_like(state)

        st = [state[h] for h in range(GDN_H)]
        s_ref[0] = state[...]
        o, st2 = _gdn_chunk(_head_slices(q_ref, GDN_DK), _head_slices(k_ref, GDN_DK), _head_slices(v_ref, GDN_DV),
                            _head_slices(g_ref, GDN_DK), _head_slices(b_ref, GDN_DK), st)
        for h in range(GDN_H):
            o_ref[:, h * GDN_DV:(h + 1) * GDN_DV] = o[h]
            state[h] = st2[h]

    return _pc(body, grid=(N,),
               in_specs=[pl.BlockSpec((GDN_C, GDN_QK), lambda n: (n, 0)), pl.BlockSpec((GDN_C, GDN_QK), lambda n: (n, 1)),
                         pl.BlockSpec((GDN_C, GDN_V), lambda n: (n, 1)), pl.BlockSpec((GDN_C, GDN_QK), lambda n: (n, 0)),
                         pl.BlockSpec((GDN_C, GDN_QK), lambda n: (n, 0))],
               out_specs=[pl.BlockSpec((GDN_C, GDN_V), lambda n: (n, 0)),
                          pl.BlockSpec((1, GDN_H, GDN_DK, GDN_DV), lambda n: (n, 0, 0, 0))],
               out_shape=[_sds((S_, GDN_V), F32), _sds((N, GDN_H, GDN_DK, GDN_DV), F32)],
               scratch_shapes=[pltpu.VMEM((GDN_H, GDN_DK, GDN_DV), F32)],
               compiler_params=_cparams(("arbitrary",)), name=name)(qkv, qkv, qkv, g_b, beta_b)


def gdn_chunk_bwd(qkv, g_b, beta_b, states, do, name):
    S_ = qkv.shape[0]
    N = S_ // GDN_C

    def body(q_ref, k_ref, v_ref, g_ref, b_ref, s_ref, do_ref, dq_ref, dk_ref, dv_ref, dg_ref, db_ref, dstate):
        @pl.when(pl.program_id(0) == 0)
        def _():
            dstate[...] = jnp.zeros_like(dstate)

        _, vjp = jax.vjp(_gdn_chunk, _head_slices(q_ref, GDN_DK), _head_slices(k_ref, GDN_DK), _head_slices(v_ref, GDN_DV),
                         _head_slices(g_ref, GDN_DK), _head_slices(b_ref, GDN_DK), [s_ref[0, h] for h in range(GDN_H)])
        dq, dk, dv, dg, db, ds = vjp((_head_slices(do_ref, GDN_DV), [dstate[h] for h in range(GDN_H)]))
        for h in range(GDN_H):
            kk, vv = slice(h * GDN_DK, (h + 1) * GDN_DK), slice(h * GDN_DV, (h + 1) * GDN_DV)
            dq_ref[:, kk] = dq[h]
            dk_ref[:, kk] = dk[h]
            dv_ref[:, vv] = dv[h]
            dg_ref[:, kk] = dg[h]
            db_ref[:, kk] = db[h]
            dstate[h] = ds[h]

    r = lambda n: N - 1 - n
    qk = lambda c: pl.BlockSpec((GDN_C, GDN_QK), lambda n: (r(n), c))
    vs = lambda c: pl.BlockSpec((GDN_C, GDN_V), lambda n: (r(n), c))
    return _pc(body, grid=(N,),
               in_specs=[qk(0), qk(1), vs(1), qk(0), qk(0),
                         pl.BlockSpec((1, GDN_H, GDN_DK, GDN_DV), lambda n: (r(n), 0, 0, 0)), vs(0)],
               out_specs=[qk(0), qk(0), vs(0), qk(0), qk(0)],
               out_shape=[_sds((S_, GDN_QK), F32), _sds((S_, GDN_QK), F32), _sds((S_, GDN_V), F32),
                          _sds((S_, GDN_QK), F32), _sds((S_, GDN_QK), F32)],
               scratch_shapes=[pltpu.VMEM((GDN_H, GDN_DK, GDN_DV), F32)],
               compiler_params=_cparams(("arbitrary",)), name=name)(qkv, qkv, qkv, g_b, beta_b, states, do)


def _rope_tables(pos_ref, inv_ref, cm_ref, sg_ref):
    ang = pos_ref[...] * inv_ref[...]
    return jnp.cos(ang) * cm_ref[...], jnp.sin(ang) * sg_ref[...]


def mla_prep_fwd(qpad, kv, proj, pos, rope_consts, name):
    S_ = qpad.shape[0]
    ts = 256
    W = 2 * LANES

    def body(q_ref, kv_ref, kr_ref, pos_ref, inv_ref, cm_ref, sg_ref, qh_ref, kh_ref, vh_ref):
        cs, sn = _rope_tables(pos_ref, inv_ref, cm_ref, sg_ref)
        rope = lambda r: r * cs + pltpu.roll(r, LANES // 2, 1) * sn
        krr = rope(kr_ref[...]).astype(BF16)
        for h in range(MLA_H):
            qh_ref[h, :, 0:LANES] = (q_ref[:, h * W:h * W + LANES] * MLA_SCALE).astype(BF16)
            qh_ref[h, :, LANES:W] = (rope(q_ref[:, h * W + LANES:(h + 1) * W]) * MLA_SCALE).astype(BF16)
            kh_ref[h, :, 0:LANES] = kv_ref[:, h * W:h * W + LANES].astype(BF16)
            kh_ref[h, :, LANES:W] = krr
            vh_ref[h] = kv_ref[:, h * W + LANES:(h + 1) * W].astype(BF16)

    one = pl.BlockSpec((1, LANES), lambda i: (0, 0))
    return _pc(body, grid=(S_ // ts,),
               in_specs=[pl.BlockSpec((ts, MLA_H * W), lambda i: (i, 0)), pl.BlockSpec((ts, MLA_H * W), lambda i: (i, 0)),
                         pl.BlockSpec((ts, LANES), lambda i: (i, 1536 // LANES)), pl.BlockSpec((ts, 1), lambda i: (i, 0)),
                         one, one, one],
               out_specs=[pl.BlockSpec((MLA_H, ts, W), lambda i: (0, i, 0)), pl.BlockSpec((MLA_H, ts, W), lambda i: (0, i, 0)),
                          pl.BlockSpec((MLA_H, ts, LANES), lambda i: (0, i, 0))],
               out_shape=[_sds((MLA_H, S_, W), BF16), _sds((MLA_H, S_, W), BF16), _sds((MLA_H, S_, LANES), BF16)],
               compiler_params=_cparams(("parallel",)), name=name)(qpad, kv, proj, pos, *rope_consts)


def mla_prep_bwd(dqh, dkh, dvh, pos, rope_consts, into, name):
    S_ = dqh.shape[1]
    ts = 256
    W = 2 * LANES

    def body(dq_ref, dk_ref, dv_ref, pos_ref, inv_ref, cm_ref, sg_ref, _, dqp_ref, dkv_ref, dkr_ref):
        cs, sn = _rope_tables(pos_ref, inv_ref, cm_ref, sg_ref)
        rope_t = lambda g: g * cs + pltpu.roll(g * sn, LANES // 2, 1)
        acc = jnp.zeros((ts, LANES), F32)
        for h in range(MLA_H):
            dqp_ref[:, h * W:h * W + LANES] = dq_ref[h, :, 0:LANES] * MLA_SCALE
            dqp_ref[:, h * W + LANES:(h + 1) * W] = rope_t(dq_ref[h, :, LANES:W]) * MLA_SCALE
            dkv_ref[:, h * W:h * W + LANES] = dk_ref[h, :, 0:LANES]
            dkv_ref[:, h * W + LANES:(h + 1) * W] = dv_ref[h]
            acc = acc + dk_ref[h, :, LANES:W]
        dkr_ref[...] = rope_t(acc)

    one = pl.BlockSpec((1, LANES), lambda i: (0, 0))
    return _pc(body, grid=(S_ // ts,),
               in_specs=[pl.BlockSpec((MLA_H, ts, W), lambda i: (0, i, 0)), pl.BlockSpec((MLA_H, ts, W), lambda i: (0, i, 0)),
                         pl.BlockSpec((MLA_H, ts, LANES), lambda i: (0, i, 0)), pl.BlockSpec((ts, 1), lambda i: (i, 0)),
                         one, one, one, pl.BlockSpec(memory_space=pl.ANY)],
               out_specs=[pl.BlockSpec((ts, MLA_H * W), lambda i: (i, 0)), pl.BlockSpec((ts, MLA_H * W), lambda i: (i, 0)),
                          pl.BlockSpec((ts, LANES), lambda i: (i, 1536 // LANES))],
               out_shape=[_sds((S_, MLA_H * W), F32), _sds((S_, MLA_H * W), F32), _sds(into.shape, F32)],
               input_output_aliases={7: 2}, compiler_params=_cparams(("parallel",)), name=name)(dqh, dkh, dvh, pos, *rope_consts, into)


NEG = -1e30


FLASH_TILE = 1024


def _scores(q, k, diagonal):
    s = lax.dot_general(q, k, (_NT, ((), ())), preferred_element_type=F32)
    if not diagonal:
        return s
    t = s.shape[0]
    return jnp.where(lax.broadcasted_iota(I32, (t, t), 1) <= lax.broadcasted_iota(I32, (t, t), 0), s, NEG)


def flash_fwd(qh, kh, vh, name):
    H, S_, W = qh.shape
    t = _tile(S_, FLASH_TILE)
    n = S_ // t

    def body(q_ref, k_ref, v_ref, o_ref, lse_ref, m_s, l_s, acc):
        qi, kj = pl.program_id(1), pl.program_id(2)

        @pl.when(kj == 0)
        def _():
            m_s[...] = jnp.full_like(m_s, NEG)
            l_s[...] = jnp.zeros_like(l_s)
            acc[...] = jnp.zeros_like(acc)

        def step(diagonal):
            s = _scores(q_ref[...], k_ref[...], diagonal)
            m_old = m_s[...]
            m_new = jnp.maximum(m_old, jnp.max(s, axis=-1, keepdims=True))
            alpha = jnp.exp(m_old - m_new)
            p = jnp.exp(s - m_new[:, :1])
            l_s[...] = alpha * l_s[...] + jnp.sum(p, axis=-1, keepdims=True)
            acc[...] = alpha * acc[...] + lax.dot_general(p.astype(BF16), v_ref[...], (_NN, ((), ())), preferred_element_type=F32)
            m_s[...] = m_new

        pl.when(kj < qi)(lambda: step(False))
        pl.when(kj == qi)(lambda: step(True))

        @pl.when(kj == n - 1)
        def _():
            o_ref[...] = acc[...] / l_s[...]
            lse_ref[...] = m_s[...] + jnp.log(l_s[...])

    return _pc(body, grid=(H, n, n),
               in_specs=[pl.BlockSpec((None, t, W), lambda h, i, j: (h, i, 0)),
                         pl.BlockSpec((None, t, W), lambda h, i, j: (h, jnp.minimum(i, j), 0)),
                         pl.BlockSpec((None, t, LANES), lambda h, i, j: (h, jnp.minimum(i, j), 0))],
               out_specs=[pl.BlockSpec((t, LANES), lambda h, i, j: (i, h)), pl.BlockSpec((None, t, LANES), lambda h, i, j: (h, i, 0))],
               out_shape=[_sds((S_, H * LANES), F32), _sds((H, S_, LANES), F32)],
               scratch_shapes=[pltpu.VMEM((t, LANES), F32)] * 3,
               compiler_params=_cparams(("parallel", "parallel", "arbitrary")), name=name)(qh, kh, vh)


def flash_bwd(qh, kh, vh, o, lse, do, name):
    H, S_, W = qh.shape
    t = _tile(S_, FLASH_TILE)
    n = S_ // t

    def body(q_ref, k_ref, v_ref, o_ref, lse_ref, do_ref, dq_ref, dk_ref, dv_ref):
        kj, qi = pl.program_id(1), pl.program_id(2)

        @pl.when(jnp.logical_and(kj == 0, qi == 0))
        def _():
            dq_ref[...] = jnp.zeros_like(dq_ref)

        @pl.when(qi == 0)
        def _():
            dk_ref[...] = jnp.zeros_like(dk_ref)
            dv_ref[...] = jnp.zeros_like(dv_ref)

        def step(diagonal):
            q, k, v = q_ref[...], k_ref[...], v_ref[...]
            do_ = do_ref[...]
            p = jnp.exp(_scores(q, k, diagonal) - lse_ref[:, :1])
            dob = do_.astype(BF16)
            dv_ref[...] += lax.dot_general(p.astype(BF16), dob, (_TN, ((), ())), preferred_element_type=F32)
            dp = lax.dot_general(dob, v, (_NT, ((), ())), preferred_element_type=F32)
            delta = jnp.sum(do_ * o_ref[...], axis=-1, keepdims=True)
            ds = (p * (dp - delta)).astype(BF16)
            dk_ref[...] += lax.dot_general(ds, q, (_TN, ((), ())), preferred_element_type=F32)
            rows = pl.ds(pl.multiple_of(qi * t, t), t)
            dq_ref[rows, :] += lax.dot_general(ds, k, (_NN, ((), ())), preferred_element_type=F32)

        pl.when(qi > kj)(lambda: step(False))
        pl.when(qi == kj)(lambda: step(True))

    qrow = lambda h, j, i: jnp.maximum(i, j)
    return _pc(body, grid=(H, n, n),
               in_specs=[pl.BlockSpec((None, t, W), lambda h, j, i: (h, qrow(h, j, i), 0)),
                         pl.BlockSpec((None, t, W), lambda h, j, i: (h, j, 0)),
                         pl.BlockSpec((None, t, LANES), lambda h, j, i: (h, j, 0)),
                         pl.BlockSpec((t, LANES), lambda h, j, i: (qrow(h, j, i), h)),
                         pl.BlockSpec((None, t, LANES), lambda h, j, i: (h, qrow(h, j, i), 0)),
                         pl.BlockSpec((t, LANES), lambda h, j, i: (qrow(h, j, i), h))],
               out_specs=[pl.BlockSpec((None, S_, W), lambda h, j, i: (h, 0, 0)),
                          pl.BlockSpec((None, t, W), lambda h, j, i: (h, j, 0)),
                          pl.BlockSpec((None, t, LANES), lambda h, j, i: (h, j, 0))],
               out_shape=[_sds((H, S_, W), F32), _sds((H, S_, W), F32), _sds((H, S_, LANES), F32)],
               compiler_params=_cparams(("parallel", "arbitrary", "arbitrary")), name=name)(qh, kh, vh, o, lse, do)


def loss_head(x, target, g, name):
    S_ = x.shape[0]
    ts = 256

    def body(x_ref, t_ref, g_ref, l_ref, dx_ref, dg_ref):
        @pl.when(pl.program_id(0) == 0)
        def _():
            l_ref[...] = jnp.zeros_like(l_ref)
            dg_ref[...] = jnp.zeros_like(dg_ref)

        y, vjp = jax.vjp(_rms, x_ref[...], g_ref[...])
        err = y - t_ref[...]
        l_ref[...] += 0.5 * jnp.sum(jnp.sum(err * err, axis=-1, keepdims=True), axis=0, keepdims=True) / D
        dx, dg = vjp(err / D)
        dx_ref[...] = dx
        dg_ref[...] += dg

    row = pl.BlockSpec((ts, D), lambda i: (i, 0))
    return _pc(body, grid=(S_ // ts,), in_specs=[row, row, pl.BlockSpec((1, D), lambda i: (0, 0))],
               out_specs=[pl.BlockSpec((1, LANES), lambda i: (0, 0)), row, pl.BlockSpec((1, D), lambda i: (0, 0))],
               out_shape=[_sds((1, LANES), F32), _sds((S_, D), F32), _sds((1, D), F32)],
               compiler_params=_cparams(("arbitrary",)), name=name)(x, target, g)


def adamw(w, parts, m, v, name):
    R, C = w.shape
    tr = R
    for cand in (512, 256, 128, 64, 32, 16, 8):
        if R % cand == 0 and cand * C * 4 <= 1024 * 1024:
            tr = cand
            break
    c1 = 1.0 - ADAM_B1 ** ADAM_STEP
    c2 = 1.0 - ADAM_B2 ** ADAM_STEP
    npart = len(parts)

    def body(*refs):
        w_ref, m_ref, v_ref = refs[0], refs[1 + npart], refs[2 + npart]
        g_ref, d_ref, nm_ref, nv_ref = refs[3 + npart:]
        gg = None
        for p_ref in refs[1:1 + npart]:
            s = p_ref[0].astype(F32)
            for n in range(1, p_ref.shape[0]):
                s = s + p_ref[n].astype(F32)
            gg = s if gg is None else gg + s
        m2 = ADAM_B1 * m_ref[...] + (1.0 - ADAM_B1) * gg
        v2 = ADAM_B2 * v_ref[...] + (1.0 - ADAM_B2) * (gg * gg)
        g_ref[...] = gg
        d_ref[...] = -ADAM_LR * ((m2 / c1) / (jnp.sqrt(v2 / c2) + ADAM_EPS) + ADAM_WD * w_ref[...])
        nm_ref[...] = m2
        nv_ref[...] = v2

    blk = pl.BlockSpec((tr, C), lambda i: (i, 0))
    pblk = [pl.BlockSpec((p.shape[0], tr, C), lambda i: (0, i, 0)) for p in parts]
    return _pc(body, grid=(R // tr,), in_specs=[blk] + pblk + [blk, blk], out_specs=[blk] * 4, out_shape=[_sds((R, C), F32)] * 4,
               compiler_params=_cparams(("parallel",)), name=name)(w, *parts, m, v)


def sum_slots(own, recv, skip, name):
    n, R, C = recv.shape
    tr = _tile(R, 512) if R % LANES == 0 else R
    has_own = own is not None

    def body(*refs):
        skip_ref = refs[0]
        r_ref, o_ref = refs[-2], refs[-1]
        acc = refs[1][...] if has_own else jnp.zeros(o_ref.shape, F32)
        for s in range(n):
            acc = acc + jnp.where(skip_ref[0] == s, 0.0, r_ref[s].astype(F32))
        o_ref[...] = acc

    row = pl.BlockSpec((tr, C), lambda i, sk: (i, 0))
    gs = pltpu.PrefetchScalarGridSpec(
        num_scalar_prefetch=1, grid=(R // tr,),
        in_specs=([row] if has_own else []) + [pl.BlockSpec((n, tr, C), lambda i, sk: (0, i, 0))], out_specs=row)
    ins = ([own] if has_own else []) + [recv]
    return _pc(body, grid_spec=gs, out_shape=_sds((R, C), F32), compiler_params=_cparams(("parallel",)), name=name)(skip, *ins)


def _chip_peers():
    x, y, c = lax.axis_index("x"), lax.axis_index("y"), lax.axis_index("c")
    return (x, y, c), [(1 - x, y, c), (x, 1 - y, c), (1 - x, 1 - y, c)]


def _chip_index(p):
    return 2 * p[0] + p[1]


def _win(ref, axis, chip, size):
    if axis is None:
        return ref.at[chip]
    idx = [slice(None)] * len(ref.shape)
    idx[axis] = pl.ds(pl.multiple_of(chip * size, size), size)
    return ref.at[tuple(idx)]


def _remote(src, dst, send_sem, recv_sem, peer):
    return pltpu.make_async_remote_copy(src_ref=src, dst_ref=dst, send_sem=send_sem, recv_sem=recv_sem, device_id=peer,
                                        device_id_type=MESH)


def _exchange_call(body, ins, out_shape, ncopies, name):
    anyspec = pl.BlockSpec(memory_space=pl.ANY)
    return _pc(body, in_specs=[anyspec] * len(ins), out_specs=[anyspec] * len(out_shape), out_shape=out_shape,
               scratch_shapes=[pltpu.SemaphoreType.DMA((ncopies, 3)), pltpu.SemaphoreType.DMA((ncopies, 3)),
                               pltpu.SemaphoreType.DMA((ncopies,))], name=name)(*ins)


def gather_weights(items, name):
    n_in = len(items)
    pieces = [(b, l) for b, it in enumerate(items) for l in range(it[0].shape[0])]
    ncp = len(pieces)

    def body(*refs):
        in_refs, out_refs = refs[:n_in], refs[n_in:n_in + ncp]
        send, recv, local = refs[n_in + ncp:]
        me, peers = _chip_peers()
        mine = _chip_index(me)
        copies = []
        for n, (b, l) in enumerate(pieces):
            src = in_refs[b].at[l]
            dst = _win(out_refs[n], items[b][2], mine, items[b][3])
            copies.append(pltpu.make_async_copy(src, dst, local.at[n]))
            copies += [_remote(src, dst, send.at[n, k], recv.at[n, k], peer) for k, peer in enumerate(peers)]
        for cp in copies:
            cp.start()
        for cp in copies:
            cp.wait()

    out_shape = [_sds(items[b][1], items[b][0].dtype) for (b, l) in pieces]
    return _exchange_call(body, [it[0] for it in items], out_shape, ncp, name)


def scatter_grads(items, name):
    pieces = [(b, l) for b, it in enumerate(items) for l in range(len(it[0]))]
    ncp = len(pieces)

    def body(*refs):
        in_refs, out_refs = refs[:ncp], refs[ncp:ncp + len(items)]
        send, recv, local = refs[ncp + len(items):]
        me, peers = _chip_peers()
        mine = _chip_index(me)
        copies = []
        for n, (b, l) in enumerate(pieces):
            _, axis, size, _ = items[b]
            dst = out_refs[b].at[mine, l]
            copies.append(pltpu.make_async_copy(_win(in_refs[n], axis, mine, size), dst, local.at[n]))
            copies += [_remote(_win(in_refs[n], axis, _chip_index(peer), size), dst, send.at[n, k], recv.at[n, k], peer)
                       for k, peer in enumerate(peers)]
        for cp in copies:
            cp.start()
        for cp in copies:
            cp.wait()

    out_shape = [_sds((4, len(it[0])) + tuple(it[3]), it[0][0].dtype) for it in items]
    return _exchange_call(body, [p for it in items for p in it[0]], out_shape, ncp, name)


def swap_cores(bufs, name):
    nb = len(bufs)

    def body(*refs):
        in_refs, out_refs = refs[:nb], refs[nb:2 * nb]
        send, recv = refs[2 * nb:]
        x, y, c = lax.axis_index("x"), lax.axis_index("y"), lax.axis_index("c")
        copies = [_remote(in_refs[b], out_refs[b], send.at[b], recv.at[b], (x, y, 1 - c)) for b in range(nb)]
        for cp in copies:
            cp.start()
        for cp in copies:
            cp.wait()

    anyspec = pl.BlockSpec(memory_space=pl.ANY)
    return _pc(body, in_specs=[anyspec] * nb, out_specs=[anyspec] * nb, out_shape=[_sds(b.shape, b.dtype) for b in bufs],
               scratch_shapes=[pltpu.SemaphoreType.DMA((nb,)), pltpu.SemaphoreType.DMA((nb,))], name=name)(*bufs)


def exchange_all(buf, name):
    def body(in_ref, out_ref, send, recv, local):
        x, y, c = lax.axis_index("x"), lax.axis_index("y"), lax.axis_index("c")
        mine = 4 * x + 2 * y + c
        loc = pltpu.make_async_copy(in_ref, out_ref.at[mine], local)
        loc.start()
        copies = [loc]
        for k in range(1, 8):
            peer = (x ^ (k >> 2), y ^ ((k >> 1) & 1), c ^ (k & 1))
            cp = pltpu.make_async_remote_copy(src_ref=in_ref, dst_ref=out_ref.at[mine], send_sem=send.at[k - 1],
                                              recv_sem=recv.at[k - 1], device_id=peer, device_id_type=MESH)
            cp.start()
            copies.append(cp)
        for cp in copies:
            cp.wait()

    anyspec = pl.BlockSpec(memory_space=pl.ANY)
    return _pc(body, in_specs=[anyspec], out_specs=anyspec, out_shape=_sds((8,) + buf.shape, buf.dtype),
               scratch_shapes=[pltpu.SemaphoreType.DMA((7,)), pltpu.SemaphoreType.DMA((7,)), pltpu.SemaphoreType.DMA],
               name=name)(buf)


def _norm_fwd(x, g, name):
    return rowwise(f_rms, [(x, D, 0, 0)], [(g, D, 0, 0)], [(D, 0, BF16)], ts=512, name=name)[0]


def _norm_bwd(x, g, dh, dres, name):
    (dx,), (dg,) = rowwise_bwd(f_rms, [(x, D, 0, 0)], [(g, D, 0, 0)], [(dh, D, 0, 0)], need=[True],
                               adds={0: (dres, D, 0, 0)}, ts=256, name=name)
    return dx, dg


def pool_fwd(x, W, tag):
    h = _norm_fwd(x, W["ng"], tag + "_norm")
    proj = mm(h, W["w_in"], name=tag + "_in")
    p = pool_time_fwd(proj, tag + "_win")
    pg = gmm("nn", p, W["w_grp"], G=4, name=tag + "_grp")
    y = rowwise(f_pool_gate, [(pg, POOL_GROUP, 0, 1), (proj, POOL_GROUP, 4, 1)], [(W["scale"], POOL_GROUP, 0, 1)],
                [(POOL_GROUP, 1, BF16)], ncol=4, ts=512, name=tag + "_gate")[0]
    xn = mm(y, W["w_out"], add=x, name=tag + "_out")
    return xn, (x, h, proj, p, pg, y)


def pool_bwd(dxn, W, saved, tag):
    x, h, proj, p, pg, y = saved
    dy = mm(dxn, W["w_out"], tb=True, name=tag + "_dy")
    g = {"w_out": mm(y, dxn, ta=True, out_dtype=BF16, name=tag + "_dwout")}
    (dpg, dproj), (g["scale"],) = rowwise_bwd(
        f_pool_gate, [(pg, POOL_GROUP, 0, 1), (proj, POOL_GROUP, 4, 1)], [(W["scale"], POOL_GROUP, 0, 1)],
        [(dy, POOL_GROUP, 0, 1)], need=[True, True], place={1: (2 * POOL_WIDTH, 4)}, ncol=4, ts=512, name=tag + "_dgate")
    dp = gmm("nt", dpg, W["w_grp"], G=4, name=tag + "_dp")
    g["w_grp"] = gmm("tn", p, dpg, G=4, out_dtype=BF16, name=tag + "_dwgrp")
    dproj = pool_time_bwd(dp, dproj, tag + "_dwin")
    dh = mm(dproj, W["w_in"], tb=True, name=tag + "_dh")
    g["w_in"] = mm(h, dproj, ta=True, out_dtype=BF16, name=tag + "_dw_in")
    dx, g["ng"] = _norm_bwd(x, W["ng"], dh, dxn, tag + "_dnorm")
    return dx, g


def gdn_fwd(x, W, tag):
    h = _norm_fwd(x, W["ng"], tag + "_norm")
    proj = mm(h, W["w_in"], name=tag + "_in")
    qkv = gdn_conv_fwd(proj, W["conv"], tag + "_conv")
    g_b, beta_b = rowwise(f_gdn_gates, [(proj, LANES, 6144 // LANES, 0)], [(W["a_log"], LANES, 0, 0), (W["dt_bias"], LANES, 0, 0)],
                          [(GDN_QK, 0, F32), (GDN_QK, 0, F32)], ts=512, name=tag + "_gates")
    o, states = gdn_chunk_fwd(qkv, g_b, beta_b, tag + "_chunk")
    og = rowwise(f_gdn_out, [(o, GDN_DV, 0, 1), (proj, GDN_DV, 4096 // GDN_DV, 1)], [(W["norm_g"], GDN_DV, 0, 0)],
                 [(GDN_DV, 1, BF16)], ncol=GDN_H, ts=512, name=tag + "_onorm")[0]
    xn = mm(og, W["w_out"], add=x, name=tag + "_out")
    return xn, (x, h, proj, qkv, g_b, beta_b, o, states, og)


def gdn_bwd(dxn, W, saved, tag):
    x, h, proj, qkv, g_b, beta_b, o, states, og = saved
    dog = mm(dxn, W["w_out"], tb=True, name=tag + "_dog")
    g = {"w_out": mm(og, dxn, ta=True, out_dtype=BF16, name=tag + "_dwout")}
    (do, dproj), (g["norm_g"],) = rowwise_bwd(
        f_gdn_out, [(o, GDN_DV, 0, 1), (proj, GDN_DV, 4096 // GDN_DV, 1)], [(W["norm_g"], GDN_DV, 0, 0)],
        [(dog, GDN_DV, 0, 1)], need=[True, True], place={1: (GDN_IN_PAD, 4096 // GDN_DV)}, ncol=GDN_H, ts=512, name=tag + "_donorm")
    dq, dk, dv, dg_b, dbeta_b = gdn_chunk_bwd(qkv, g_b, beta_b, states, do, tag + "_dchunk")
    (dproj,), (g["a_log"], g["dt_bias"]) = rowwise_bwd(
        f_gdn_gates, [(proj, LANES, 6144 // LANES, 0)], [(W["a_log"], LANES, 0, 0), (W["dt_bias"], LANES, 0, 0)],
        [(dg_b, GDN_QK, 0, 0), (dbeta_b, GDN_QK, 0, 0)], need=[True], place={0: (dproj, 6144 // LANES)}, ts=256, name=tag + "_dgates")
    dproj, g["conv"] = gdn_conv_bwd(proj, W["conv"], dq, dk, dv, dproj, tag + "_dconv")
    dh = mm(dproj, W["w_in"], tb=True, name=tag + "_dh")
    g["w_in"] = mm(h, dproj, ta=True, out_dtype=BF16, name=tag + "_dw_in")
    dx, g["ng"] = _norm_bwd(x, W["ng"], dh, dxn, tag + "_dnorm")
    return dx, g


def mla_fwd(x, pos, W, tag):
    h = _norm_fwd(x, W["ng"], tag + "_norm")
    proj = mm(h, W["w_in"], name=tag + "_in")
    hq = rowwise(f_rms, [(proj, MLA_Q_LORA, 0, 0)], [(W["q_g"], MLA_Q_LORA, 0, 0)], [(MLA_Q_LORA, 0, BF16)], ts=512, name=tag + "_qnorm")[0]
    hkv = rowwise(f_rms, [(proj, MLA_KV_LORA, 2, 0)], [(W["kv_g"], MLA_KV_LORA, 0, 0)], [(MLA_KV_LORA, 0, BF16)], ts=512, name=tag + "_kvnorm")[0]
    qpad = mm(hq, W["w_uq"], name=tag + "_uq")
    kv = mm(hkv, W["w_ukv"], name=tag + "_ukv")
    qh, kh, vh = mla_prep_fwd(qpad, kv, proj, pos, W["rope"], tag + "_prep")
    o, lse = flash_fwd(qh, kh, vh, tag + "_attn")
    og = rowwise(f_ogate, [(o, 512, 0, 1), (proj, 512, 4, 1)], [], [(512, 1, BF16)], ncol=4, ts=512, name=tag + "_ogate")[0]
    xn = mm(og, W["w_out"], add=x, name=tag + "_out")
    return xn, (x, h, proj, hq, hkv, qh, kh, vh, o, lse, og)


def mla_bwd(dxn, pos, W, saved, tag):
    x, h, proj, hq, hkv, qh, kh, vh, o, lse, og = saved
    dog = mm(dxn, W["w_out"], tb=True, name=tag + "_dog")
    g = {"w_out": mm(og, dxn, ta=True, out_dtype=BF16, name=tag + "_dwout")}
    dproj = jnp.zeros(proj.shape, F32)
    (do, dproj), _ = rowwise_bwd(f_ogate, [(o, 512, 0, 1), (proj, 512, 4, 1)], [], [(dog, 512, 0, 1)], need=[True, True],
                                 place={1: (dproj, 4)}, ncol=4, ts=512, name=tag + "_dogate")
    dqh, dkh, dvh = flash_bwd(qh, kh, vh, o, lse, do, tag + "_dattn")
    dqpad, dkv, dproj = mla_prep_bwd(dqh, dkh, dvh, pos, W["rope"], dproj, tag + "_dprep")
    dhq = mm(dqpad, W["w_uq"], tb=True, name=tag + "_dhq")
    g["w_uq"] = mm(hq, dqpad, ta=True, out_dtype=BF16, name=tag + "_dwuq")
    dhkv = mm(dkv, W["w_ukv"], tb=True, name=tag + "_dhkv")
    g["w_ukv"] = mm(hkv, dkv, ta=True, out_dtype=BF16, name=tag + "_dwukv")
    (dproj,), (g["q_g"],) = rowwise_bwd(f_rms, [(proj, MLA_Q_LORA, 0, 0)], [(W["q_g"], MLA_Q_LORA, 0, 0)], [(dhq, MLA_Q_LORA, 0, 0)],
                                        need=[True], place={0: (dproj, 0)}, ts=256, name=tag + "_dqnorm")
    (dproj,), (g["kv_g"],) = rowwise_bwd(f_rms, [(proj, MLA_KV_LORA, 2, 0)], [(W["kv_g"], MLA_KV_LORA, 0, 0)], [(dhkv, MLA_KV_LORA, 0, 0)],
                                         need=[True], place={0: (dproj, 2)}, ts=256, name=tag + "_dkvnorm")
    dh = mm(dproj, W["w_in"], tb=True, name=tag + "_dh")
    g["w_in"] = mm(h, dproj, ta=True, out_dtype=BF16, name=tag + "_dw_in")
    dx, g["ng"] = _norm_bwd(x, W["ng"], dh, dxn, tag + "_dnorm")
    return dx, g


def _pad_cols(a, n):
    return jnp.pad(a, ((0, 0), (0, n - a.shape[1])))


def _mla_w_in_layout(w):
    z = lambda n: jnp.zeros((w.shape[0], n), w.dtype)
    kr = w[:, 1280:1344]
    return jnp.concatenate([w[:, :768], z(256), w[:, 768:1280], kr[:, :32], z(32), kr[:, 32:], z(32), z(384), w[:, 1344:]], axis=1)


def _mla_w_in_unlayout(g):
    return jnp.concatenate([g[:, :768], g[:, 1024:1536], g[:, 1536:1568], g[:, 1600:1632], g[:, 2048:]], axis=1)


def _mla_w_uq_layout(w):
    w3 = w.reshape(w.shape[0], MLA_H, MLA_NOPE + MLA_ROPE)
    z = jnp.zeros((w.shape[0], MLA_H, 32), w.dtype)
    return jnp.concatenate([w3[..., :128], w3[..., 128:160], z, w3[..., 160:192], z], axis=-1).reshape(w.shape[0], MLA_H * 256)


def _mla_w_uq_unlayout(g):
    g3 = g.reshape(g.shape[0], MLA_H, 256)
    return jnp.concatenate([g3[..., :128], g3[..., 128:160], g3[..., 192:224]], axis=-1).reshape(g.shape[0], MLA_H * 192)


def _rope_consts():
    half = MLA_ROPE // 2
    inv = ROPE_THETA ** (-jnp.arange(half, dtype=F32) / half)
    z = jnp.zeros((half,), F32)
    o = jnp.ones((half,), F32)
    row = lambda *p: jnp.concatenate(p).reshape(1, LANES)
    return row(inv, z, inv, z), row(o, z, o, z), row(-o, z, o, z)


BIG = ["pool_w_in", "pool_w_grp", "pool_w_out", "gdn_w_in", "gdn_w_out", "mla_w_in", "mla_w_uq", "mla_w_ukv", "mla_w_out"]
BIG_LAYOUT = {"pool_w_in": (1, 1024, (1024, 4096)), "pool_w_grp": (1, 128, (4, 512, 512)), "pool_w_out": (0, 512, (2048, 1024)),
              "gdn_w_in": (None, None, (4, 1024, 1540)), "gdn_w_out": (0, 512, (2048, 1024)),
              "mla_w_in": (None, None, (4, 1024, 848)), "mla_w_uq": (1, 768, (768, 3072)), "mla_w_ukv": (1, 1024, (512, 4096)),
              "mla_w_out": (0, 512, (2048, 1024))}
SMALL_SHARDED = ["pool_scale", "gdn_conv", "mla_q_norm_g", "mla_kv_norm_g"]
SMALL_AXIS = {"pool_scale": 1, "gdn_conv": 2, "mla_q_norm_g": 1, "mla_kv_norm_g": 1}
REPLICATED = ["norm_g", "gdn_a_log", "gdn_dt_bias", "gdn_norm_g", "final_g"]
PACK_C = 1024


def _pack(parts, dtype, row_mult):
    flat = jnp.concatenate([p.reshape(-1).astype(dtype) for p in parts])
    rows = -(-flat.shape[0] // PACK_C)
    rows = -(-rows // row_mult) * row_mult
    return jnp.pad(flat, (0, rows * PACK_C - flat.shape[0])).reshape(rows, PACK_C)


def _unpack(buf, shapes):
    lead = buf.shape[:-2]
    flat = buf.reshape(lead + (-1,))
    out, off = [], 0
    for s in shapes:
        n = int(np.prod(s))
        out.append(flat[..., off:off + n].reshape(lead + tuple(s)))
        off += n
    return out


def _unshard(g4, axis):
    a = jnp.moveaxis(g4, 0, axis)
    s = a.shape
    return a.reshape(s[:axis] + (s[axis] * s[axis + 1],) + s[axis + 2:])


def _to_shards(a, axis):
    s = a.shape
    return jnp.moveaxis(a.reshape(s[:axis] + (4, s[axis] // 4) + s[axis + 1:]), axis, 0)


def layer_weights(full, small, rep):
    ng = lambda i: rep["norm_g"][i:i + 1]
    pool = lambda j, i: dict(ng=ng(i), w_in=full["pool_w_in"][j], w_grp=full["pool_w_grp"][j], scale=small["pool_scale"][j:j + 1],
                             w_out=full["pool_w_out"][j])
    side_by_side = lambda a4: jnp.moveaxis(a4, 0, 1).reshape(a4.shape[1], 4 * a4.shape[2])
    gdn = dict(ng=ng(1), w_in=_pad_cols(side_by_side(full["gdn_w_in"][0]), GDN_IN_PAD),
               conv=jnp.pad(small["gdn_conv"][0], ((0, 4), (0, 0))), a_log=_pad_cols(rep["gdn_a_log"], LANES),
               dt_bias=_pad_cols(rep["gdn_dt_bias"], LANES), norm_g=rep["gdn_norm_g"], w_out=full["gdn_w_out"][0])
    mla = dict(ng=ng(2), w_in=_mla_w_in_layout(side_by_side(full["mla_w_in"][0])), q_g=small["mla_q_norm_g"],
               kv_g=small["mla_kv_norm_g"], w_uq=_mla_w_uq_layout(full["mla_w_uq"][0]), w_ukv=full["mla_w_ukv"][0],
               w_out=full["mla_w_out"][0], rope=_rope_consts())
    return dict(l0=pool(0, 0), l1=gdn, l2=mla, l3=pool(1, 3), final_g=rep["final_g"].reshape(1, D))


def local_step(x, pos, target, W):
    x1, s0 = pool_fwd(x, W["l0"], "l0")
    x2, s1 = gdn_fwd(x1, W["l1"], "l1")
    x3, s2 = mla_fwd(x2, pos, W["l2"], "l2")
    x4, s3 = pool_fwd(x3, W["l3"], "l3")
    loss, dx4, dfinal = loss_head(x4, target, W["final_g"], "loss_head")
    dx3, g3 = pool_bwd(dx4, W["l3"], s3, "l3")
    dx2, g2 = mla_bwd(dx3, pos, W["l2"], s2, "l2")
    dx1, g1 = gdn_bwd(dx2, W["l1"], s1, "l1")
    dx0, g0 = pool_bwd(dx1, W["l0"], s0, "l0")
    return loss, dx0, (g0, g1, g2, g3), dfinal


def big_grad_pieces(gl):
    g0, g1, g2, g3 = gl
    slots = lambda a: jnp.moveaxis(a.reshape(a.shape[0], 4, a.shape[1] // 4), 1, 0)
    return {"pool_w_in": [g0["w_in"], g3["w_in"]], "pool_w_grp": [g0["w_grp"], g3["w_grp"]], "pool_w_out": [g0["w_out"], g3["w_out"]],
            "gdn_w_in": [slots(g1["w_in"][:, :GDN_IN])], "gdn_w_out": [g1["w_out"]],
            "mla_w_in": [slots(_mla_w_in_unlayout(g2["w_in"]))], "mla_w_uq": [_mla_w_uq_unlayout(g2["w_uq"])],
            "mla_w_ukv": [g2["w_ukv"]], "mla_w_out": [g2["w_out"]]}


def small_grads(gl, dfinal):
    g0, g1, g2, g3 = gl
    return {"norm_g": jnp.concatenate([g0["ng"], g1["ng"], g2["ng"], g3["ng"]], axis=0),
            "pool_scale": jnp.concatenate([g0["scale"], g3["scale"]], axis=0), "gdn_conv": g1["conv"][None, :4],
            "gdn_a_log": g1["a_log"][:, :GDN_H], "gdn_dt_bias": g1["dt_bias"][:, :GDN_H], "gdn_norm_g": g1["norm_g"],
            "mla_q_norm_g": g2["q_g"], "mla_kv_norm_g": g2["kv_g"], "final_g": dfinal.reshape(D)}


NAMES = ["norm_g", "pool_w_in", "pool_w_grp", "pool_scale", "pool_w_out", "gdn_w_in", "gdn_conv", "gdn_a_log", "gdn_dt_bias",
         "gdn_norm_g", "gdn_w_out", "mla_w_in", "mla_q_norm_g", "mla_w_uq", "mla_kv_norm_g", "mla_w_ukv", "mla_w_out", "final_g"]


def kernel(x, positions, norm_g, pool_w_in, pool_w_grp, pool_scale, pool_w_out, gdn_w_in, gdn_conv, gdn_a_log, gdn_dt_bias, gdn_norm_g, gdn_w_out, mla_w_in, mla_q_norm_g, mla_w_uq, mla_kv_norm_g, mla_w_ukv, mla_w_out, final_g, loss_target, m_norm_g, m_pool_w_in, m_pool_w_grp, m_pool_scale, m_pool_w_out, m_gdn_w_in, m_gdn_conv, m_gdn_a_log, m_gdn_dt_bias, m_gdn_norm_g, m_gdn_w_out, m_mla_w_in, m_mla_q_norm_g, m_mla_w_uq, m_mla_kv_norm_g, m_mla_w_ukv, m_mla_w_out, m_final_g, v_norm_g, v_pool_w_in, v_pool_w_grp, v_pool_scale, v_pool_w_out, v_gdn_w_in, v_gdn_conv, v_gdn_a_log, v_gdn_dt_bias, v_gdn_norm_g, v_gdn_w_out, v_mla_w_in, v_mla_q_norm_g, v_mla_w_uq, v_mla_kv_norm_g, v_mla_w_ukv, v_mla_w_out, v_final_g):
    args = locals()
    w = {n: args[n] for n in NAMES}
    m = {n: args["m_" + n] for n in NAMES}
    v = {n: args["v_" + n] for n in NAMES}
    my_chip = (2 * lax.axis_index("x") + lax.axis_index("y")).astype(I32)

    small_shapes = [w[n].shape for n in SMALL_SHARDED]
    small_pack = _pack([w[n] for n in SMALL_SHARDED], F32, 8)[None]
    items = [(w[n].astype(BF16), BIG_LAYOUT[n][2], BIG_LAYOUT[n][0], BIG_LAYOUT[n][1]) for n in BIG]
    items.append((small_pack, (4,) + small_pack.shape[1:], None, None))
    got = iter(gather_weights(items, "gather_weights"))
    full = {n: [next(got) for _ in range(w[n].shape[0])] for n in BIG}
    small = {n: _unshard(a, SMALL_AXIS[n]) for n, a in zip(SMALL_SHARDED, _unpack(next(got), small_shapes))}
    W = layer_weights(full, small, {n: w[n] for n in REPLICATED})

    S_ = x.shape[1]
    loss_part, dx, gl, dfinal = local_step(x[0], positions.reshape(S_, 1).astype(F32), loss_target[0], W)

    pieces = big_grad_pieces(gl)
    recv = scatter_grads([(pieces[n], BIG_LAYOUT[n][0], BIG_LAYOUT[n][1], w[n].shape[1:]) for n in BIG], "scatter_grads")
    sib = swap_cores(recv, "swap_cores")
    sg = small_grads(gl, dfinal)
    small_names = SMALL_SHARDED + REPLICATED
    small_buf = _pack([sg[n] for n in small_names] + [loss_part], F32, 8)
    small_sum = sum_slots(None, exchange_all(small_buf, "gather_small"), jnp.full((1,), -1, I32), "sum_small")
    full_small = _unpack(small_sum, [sg[n].shape for n in small_names] + [(1, LANES)])
    loss = full_small[-1][0, 0]
    parts = {}
    for n, a in zip(small_names, full_small[:-1]):
        if n in SMALL_AXIS:
            a = lax.dynamic_index_in_dim(_to_shards(a, SMALL_AXIS[n]), my_chip, axis=0, keepdims=False)
        parts[n] = [a]
    for n, r, s in zip(BIG, recv, sib):
        parts[n] = [r, s]

    outs = []
    for n in NAMES:
        shp = w[n].shape
        two = (int(np.prod(shp[:-1])), shp[-1]) if len(shp) > 1 else (1, shp[0])
        res = adamw(w[n].reshape(two), [p.reshape((-1,) + two) for p in parts[n]], m[n].reshape(two), v[n].reshape(two), "adamw_" + n)
        outs.append([r.reshape(shp) for r in res])
    return (loss, dx[None], *[o[0] for o in outs], *[o[1] for o in outs], *[o[2] for o in outs], *[o[3] for o in outs])
```

```python
import functools
import math

import jax
import jax.numpy as jnp
import numpy as np
from jax import lax
from jax.experimental import pallas as pl
from jax.experimental.pallas import tpu as pltpu

F32 = jnp.float32
BF16 = jnp.bfloat16
I32 = jnp.int32

D = 1024
EPS = 1e-6
POOL_WIDTH = 2048
POOL_GROUP = 512
GDN_H, GDN_DK, GDN_DV, GDN_C = 8, 128, 256, 64
GDN_QK, GDN_V, GDN_CONV_CH, GDN_IN = 1024, 2048, 4096, 6160
GDN_IN_PAD = 6272
MLA_H, MLA_NOPE, MLA_ROPE, MLA_V = 16, 128, 64, 128
MLA_Q_LORA, MLA_KV_LORA, MLA_WIDTH, MLA_IN = 768, 512, 2048, 3392
MLA_IN_PAD = 4096
MLA_SCALE = (MLA_NOPE + MLA_ROPE) ** -0.5
ROPE_THETA = 10000.0
ADAM_LR, ADAM_B1, ADAM_B2, ADAM_EPS, ADAM_WD, ADAM_STEP = 0.001, 0.9, 0.999, 1e-08, 0.01, 10

VMEM_LIMIT_V7X = 56 * 1024 * 1024
LANES = 128
MESH = pl.DeviceIdType.MESH


def _pc(body, **kw):
    return pl.pallas_call(body, **kw)


def _cparams(sem):
    return pltpu.CompilerParams(dimension_semantics=sem, vmem_limit_bytes=VMEM_LIMIT_V7X)


def _tile(n, cap):
    t = (cap // LANES) * LANES
    while t >= LANES:
        if n % t == 0:
            return t
        t -= LANES
    return n


def _sds(shape, dt):
    return jax.ShapeDtypeStruct(shape, dt)


def mm(a, b, *, ta=False, tb=False, add=None, out_dtype=F32, name):
    if ta:
        K, M = a.shape
    else:
        M, K = a.shape
    if tb:
        N, K2 = b.shape
    else:
        K2, N = b.shape
    assert K == K2, (a.shape, b.shape, ta, tb)
    tm, tn, tk = _tile(M, 1024), _tile(N, 1024), _tile(K, 1024)
    nk = K // tk
    a_spec = pl.BlockSpec((tk, tm), lambda i, j, k: (k, i)) if ta else pl.BlockSpec((tm, tk), lambda i, j, k: (i, k))
    b_spec = pl.BlockSpec((tn, tk), lambda i, j, k: (j, k)) if tb else pl.BlockSpec((tk, tn), lambda i, j, k: (k, j))
    o_spec = pl.BlockSpec((tm, tn), lambda i, j, k: (i, j))
    dn = (((0 if ta else 1,), (1 if tb else 0,)), ((), ()))
    has_add = add is not None

    def body(*refs):
        a_ref, b_ref = refs[0], refs[1]
        o_ref, acc = refs[-2], refs[-1]
        k = pl.program_id(2)

        @pl.when(k == 0)
        def _():
            acc[...] = jnp.zeros_like(acc)

        acc[...] += lax.dot_general(a_ref[...].astype(BF16), b_ref[...].astype(BF16), dn, preferred_element_type=F32)

        @pl.when(k == nk - 1)
        def _():
            r = acc[...]
            if has_add:
                r = r + refs[2][...]
            o_ref[...] = r.astype(out_dtype)

    ins = [a, b] + ([add] if has_add else [])
    specs = [a_spec, b_spec] + ([o_spec] if has_add else [])
    return _pc(body, grid=(M // tm, N // tn, nk), in_specs=specs, out_specs=o_spec, out_shape=_sds((M, N), out_dtype),
               scratch_shapes=[pltpu.VMEM((tm, tn), F32)], compiler_params=_cparams(("parallel", "parallel", "arbitrary")),
               name=name)(*ins)


def gmm(kind, a, b, *, G, name, out_dtype=F32):
    S_ = a.shape[0]
    Ka = a.shape[1] // G
    if kind == "tn":
        N = b.shape[1] // G
        tk = _tile(S_, 512)
        nk = S_ // tk

        def body(a_ref, b_ref, o_ref, acc):
            k = pl.program_id(1)

            @pl.when(k == 0)
            def _():
                acc[...] = jnp.zeros_like(acc)

            acc[...] += lax.dot_general(a_ref[...].astype(BF16), b_ref[...].astype(BF16), (((0,), (0,)), ((), ())),
                                        preferred_element_type=F32)

            @pl.when(k == nk - 1)
            def _():
                o_ref[...] = acc[...].astype(out_dtype)

        return _pc(body, grid=(G, nk),
                   in_specs=[pl.BlockSpec((tk, Ka), lambda g, k: (k, g)), pl.BlockSpec((tk, N), lambda g, k: (k, g))],
                   out_specs=pl.BlockSpec((None, Ka, N), lambda g, k: (g, 0, 0)), out_shape=_sds((G, Ka, N), out_dtype),
                   scratch_shapes=[pltpu.VMEM((Ka, N), F32)], compiler_params=_cparams(("parallel", "arbitrary")), name=name)(a, b)
    N = b.shape[2] if kind == "nn" else b.shape[1]
    tm = _tile(S_, 1024)
    dn = (((1,), (0 if kind == "nn" else 1,)), ((), ()))

    def body(a_ref, b_ref, o_ref):
        o_ref[...] = lax.dot_general(a_ref[...].astype(BF16), b_ref[...].astype(BF16), dn, preferred_element_type=F32)

    bshape = (None,) + tuple(b.shape[1:])
    return _pc(body, grid=(G, S_ // tm),
               in_specs=[pl.BlockSpec((tm, Ka), lambda g, i: (i, g)), pl.BlockSpec(bshape, lambda g, i: (g, 0, 0))],
               out_specs=pl.BlockSpec((tm, N), lambda g, i: (i, g)), out_shape=_sds((S_, G * N), F32),
               compiler_params=_cparams(("parallel", "parallel")), name=name)(a, b)


def _rw_spec(ts, w, c, s):
    return pl.BlockSpec((ts, w), lambda j, i: (i, c + j * s))


def _rw_pspec(p, w, c, s):
    return pl.BlockSpec((p.shape[0], w), lambda j, i: (0, c + j * s))


def rowwise(f, tiles, params, outs, *, ncol=1, ts, name):
    S_ = tiles[0][0].shape[0]
    nin = len(tiles) + len(params)

    def body(*refs):
        res = f(pl.program_id(0), *[r[...] for r in refs[:nin]])
        for r, o in zip(refs[nin:], res):
            r[...] = o.astype(r.dtype)

    return _pc(body, grid=(ncol, S_ // ts),
               in_specs=[_rw_spec(ts, w, c, s) for (_, w, c, s) in tiles] + [_rw_pspec(*p) for p in params],
               out_specs=[_rw_spec(ts, w, 0, s) for (w, s, _) in outs],
               out_shape=[_sds((S_, w * (ncol if s else 1)), dt) for (w, s, dt) in outs],
               compiler_params=_cparams(("parallel", "parallel")), name=name)(*[t[0] for t in tiles], *[p[0] for p in params])


def rowwise_bwd(f, tiles, params, cots, *, need, adds=None, place=None, ncol=1, ts, name):
    S_ = tiles[0][0].shape[0]
    adds = adds or {}
    place = place or {}
    nt, npar, nc = len(tiles), len(params), len(cots)
    add_keys = sorted(adds)
    need_idx = [k for k in range(nt) if need[k]]
    into_keys = [k for k in need_idx if k in place and not isinstance(place[k][0], int)]
    n_extra = len(add_keys) + len(into_keys)

    def body(*refs):
        j, i = pl.program_id(0), pl.program_id(1)
        vals = [r[...] for r in refs[:nt + npar]]
        cvals = tuple(r[...] for r in refs[nt + npar:nt + npar + nc])
        add_refs = refs[nt + npar + nc:nt + npar + nc + len(add_keys)]
        out_refs = refs[nt + npar + nc + n_extra:]
        _, vjp = jax.vjp(lambda *v: tuple(f(j, *v)), *vals)
        grads = vjp(cvals)
        for n, k in enumerate(need_idx):
            g = grads[k]
            if k in adds:
                g = g + add_refs[add_keys.index(k)][...]
            out_refs[n][...] = g
        for n in range(npar):
            ref = out_refs[len(need_idx) + n]
            first = (i == 0) if params[n][3] else jnp.logical_and(i == 0, j == 0)

            @pl.when(first)
            def _():
                ref[...] = jnp.zeros_like(ref)

            ref[...] += grads[nt + n]

    in_specs = ([_rw_spec(ts, w, c, s) for (_, w, c, s) in tiles] + [_rw_pspec(*p) for p in params]
                + [_rw_spec(ts, w, c, s) for (_, w, c, s) in cots] + [_rw_spec(ts, *adds[k][1:]) for k in add_keys]
                + [pl.BlockSpec(memory_space=pl.ANY) for _ in into_keys])
    out_specs, out_shape, aliases = [], [], {}
    for n, k in enumerate(need_idx):
        w, s = tiles[k][1], tiles[k][3]
        if k in place:
            dst, c0 = place[k]
            total = dst if isinstance(dst, int) else dst.shape[1]
            out_specs.append(_rw_spec(ts, w, c0, s))
            out_shape.append(_sds((S_, total), F32))
            if k in into_keys:
                aliases[nt + npar + nc + len(add_keys) + into_keys.index(k)] = n
        else:
            out_specs.append(_rw_spec(ts, w, 0, s))
            out_shape.append(_sds((S_, w * (ncol if s else 1)), F32))
    out_specs += [_rw_pspec(p[0], p[1], p[2], p[3]) for p in params]
    out_shape += [_sds(p[0].shape, F32) for p in params]
    res = _pc(body, grid=(ncol, S_ // ts), in_specs=in_specs, out_specs=out_specs, out_shape=out_shape,
              input_output_aliases=aliases, compiler_params=_cparams(("arbitrary", "arbitrary")), name=name)(
        *[t[0] for t in tiles], *[p[0] for p in params], *[c[0] for c in cots], *[adds[k][0] for k in add_keys],
        *[place[k][0] for k in into_keys])
    return list(res[:len(need_idx)]), list(res[len(need_idx):])


def _rms(x, g):
    r = lax.rsqrt(jnp.mean(x * x, axis=-1, keepdims=True) + EPS)
    return x * r * g


def _silu(x):
    return x * jax.nn.sigmoid(x)


@jax.custom_vjp
def _softplus(x):
    return jnp.maximum(x, 0.0) + jnp.log1p(jnp.exp(-jnp.abs(x)))


_softplus.defvjp(lambda x: (_softplus(x), x), lambda x, d: (d * jax.nn.sigmoid(x),))


def f_rms(j, x, g):
    return (_rms(x, g),)


def f_pool_gate(j, pg, gate, scale):
    return (pg * scale * _silu(gate),)


def f_ogate(j, o, gate):
    return (o * _silu(gate),)


def f_gdn_out(j, o, gate, g):
    return (_rms(o, g) * _silu(gate),)


def f_gdn_gates(j, ba, alog, dtb):
    lane = lax.broadcasted_iota(I32, (1, LANES), 1)
    gs, bs = [], []
    for h in range(GDN_H):
        eb = (lane == h).astype(F32)
        ea = (lane == GDN_H + h).astype(F32)
        b = jnp.sum(ba * eb, -1, keepdims=True)
        a = jnp.sum(ba * ea, -1, keepdims=True)
        al = jnp.sum(alog * eb, -1, keepdims=True)
        dt = jnp.sum(dtb * eb, -1, keepdims=True)
        g = -jnp.exp(al) * _softplus(a + dt)
        gs.append(jnp.broadcast_to(g, ba.shape))
        bs.append(jnp.broadcast_to(jax.nn.sigmoid(b), ba.shape))
    return jnp.concatenate(gs, 1), jnp.concatenate(bs, 1)


def _shift_dn(x, k):
    rows = lax.broadcasted_iota(I32, x.shape, 0)
    return jnp.where(rows < k, 0.0, pltpu.roll(x, k, 0))


def _shift_up(x, k):
    n = x.shape[0]
    rows = lax.broadcasted_iota(I32, x.shape, 0)
    return jnp.where(rows >= n - k, 0.0, pltpu.roll(x, n - k, 0))


def _pool_window(j):
    g = lax.div(j, POOL_GROUP // LANES)
    return jnp.where(g == 0, 2.0, jnp.where(g == 1, 4.0, jnp.where(g == 2, 8.0, 16.0))), g


def _pick(g, a2, a4, a8, a16):
    return jnp.where(g == 0, a2, jnp.where(g == 1, a4, jnp.where(g == 2, a8, a16)))


def pool_time_fwd(proj, name):
    S_ = proj.shape[0]

    def body(u_ref, p_ref):
        u = u_ref[...]
        w, g = _pool_window(pl.program_id(0))
        s2 = u + _shift_dn(u, 1)
        s4 = s2 + _shift_dn(s2, 2)
        s8 = s4 + _shift_dn(s4, 4)
        s16 = s8 + _shift_dn(s8, 8)
        t1 = (lax.broadcasted_iota(I32, u.shape, 0) + 1).astype(F32)
        p_ref[...] = (_pick(g, s2, s4, s8, s16) / jnp.minimum(t1, w) - u).astype(p_ref.dtype)

    return _pc(body, grid=(POOL_WIDTH // LANES,), in_specs=[pl.BlockSpec((S_, LANES), lambda j: (0, j))],
               out_specs=pl.BlockSpec((S_, LANES), lambda j: (0, j)), out_shape=_sds((S_, POOL_WIDTH), BF16),
               compiler_params=_cparams(("parallel",)), name=name)(proj)


def pool_time_bwd(dp, into, name):
    S_ = dp.shape[0]

    def body(dp_ref, _, du_ref):
        d = dp_ref[...]
        w, g = _pool_window(pl.program_id(0))
        t1 = (lax.broadcasted_iota(I32, d.shape, 0) + 1).astype(F32)
        q = d / jnp.minimum(t1, w)
        r2 = q + _shift_up(q, 1)
        r4 = r2 + _shift_up(r2, 2)
        r8 = r4 + _shift_up(r4, 4)
        r16 = r8 + _shift_up(r8, 8)
        du_ref[...] = _pick(g, r2, r4, r8, r16) - d

    return _pc(body, grid=(POOL_WIDTH // LANES,),
               in_specs=[pl.BlockSpec((S_, LANES), lambda j: (0, j)), pl.BlockSpec(memory_space=pl.ANY)],
               out_specs=pl.BlockSpec((S_, LANES), lambda j: (0, j)), out_shape=_sds(into.shape, F32),
               input_output_aliases={1: 0}, compiler_params=_cparams(("parallel",)), name=name)(dp, into)


def _conv_post(j, a):
    n = a * lax.rsqrt(jnp.sum(a * a, axis=-1, keepdims=True) + EPS)
    nq = GDN_QK // LANES
    return jnp.where(j < nq, n * (GDN_DK ** -0.5), jnp.where(j < 2 * nq, n, a))


def _conv_pre(u, w):
    return w[3:4] * u + w[2:3] * _shift_dn(u, 1) + w[1:2] * _shift_dn(u, 2) + w[0:1] * _shift_dn(u, 3)


def gdn_conv_fwd(proj, conv_w, name):
    S_ = proj.shape[0]

    def body(u_ref, w_ref, o_ref):
        o_ref[...] = _conv_post(pl.program_id(0), _silu(_conv_pre(u_ref[...], w_ref[...])))

    return _pc(body, grid=(GDN_CONV_CH // LANES,),
               in_specs=[pl.BlockSpec((S_, LANES), lambda j: (0, j)), pl.BlockSpec((8, LANES), lambda j: (0, j))],
               out_specs=pl.BlockSpec((S_, LANES), lambda j: (0, j)), out_shape=_sds((S_, GDN_CONV_CH), F32),
               compiler_params=_cparams(("parallel",)), name=name)(proj, conv_w)


def gdn_conv_bwd(proj, conv_w, dq, dk, dv, into, name):
    S_ = proj.shape[0]
    nq = GDN_QK // LANES

    def body(u_ref, w_ref, dq_ref, dk_ref, dv_ref, _, du_ref, dw_ref):
        j = pl.program_id(0)
        u, w = u_ref[...], w_ref[...]
        c = _conv_pre(u, w)
        sig = jax.nn.sigmoid(c)
        dout = jnp.where(j < nq, dq_ref[...], jnp.where(j < 2 * nq, dk_ref[...], dv_ref[...]))
        _, vjp = jax.vjp(lambda a: _conv_post(j, a), c * sig)
        dc = vjp(dout)[0] * (sig * (1.0 + c * (1.0 - sig)))
        du_ref[...] = w[3:4] * dc + w[2:3] * _shift_up(dc, 1) + w[1:2] * _shift_up(dc, 2) + w[0:1] * _shift_up(dc, 3)
        rows = lax.broadcasted_iota(I32, (8, LANES), 0)
        dw = jnp.zeros((8, LANES), F32)
        for k in range(4):
            us = u if k == 3 else _shift_dn(u, 3 - k)
            dw = dw + jnp.where(rows == k, jnp.sum(dc * us, axis=0, keepdims=True), 0.0)
        dw_ref[...] = dw

    blk = lambda f: pl.BlockSpec((S_, LANES), f)
    return _pc(body, grid=(GDN_CONV_CH // LANES,),
               in_specs=[blk(lambda j: (0, j)), pl.BlockSpec((8, LANES), lambda j: (0, j)),
                         blk(lambda j: (0, jnp.minimum(j, nq - 1))), blk(lambda j: (0, jnp.clip(j - nq, 0, nq - 1))),
                         blk(lambda j: (0, jnp.clip(j - 2 * nq, 0, 2 * nq - 1))), pl.BlockSpec(memory_space=pl.ANY)],
               out_specs=[blk(lambda j: (0, j)), pl.BlockSpec((8, LANES), lambda j: (0, j))],
               out_shape=[_sds(into.shape, F32), _sds((8, GDN_CONV_CH), F32)], input_output_aliases={5: 0},
               compiler_params=_cparams(("parallel",)), name=name)(proj, conv_w, dq, dk, dv, into)


_NN, _NT, _TN = ((1,), (0,)), ((1,), (1,)), ((0,), (0,))


def _split(x, n):
    parts = []
    for _ in range(n):
        h = x.astype(BF16)
        parts.append(h)
        x = x - h.astype(F32)
    return parts


def _dot(a, b, dn, mode):
    d = lambda p, q: lax.dot_general(p, q, (dn, ((), ())), preferred_element_type=F32)
    if mode == "lo":
        return d(a.astype(BF16), b.astype(BF16))
    if mode == "x3":
        (ah, al), (bh, bl) = _split(a, 2), _split(b, 2)
        return d(ah, bh) + (d(ah, bl) + d(al, bh))
    b0, b1, b2 = _split(b, 3)
    ab = a.astype(BF16)
    return d(ab, b0) + (d(ab, b1) + d(ab, b2))


def _make_dots(mode):
    @jax.custom_vjp
    def nn(a, b):
        return _dot(a, b, _NN, mode)

    @jax.custom_vjp
    def nt(a, b):
        return _dot(a, b, _NT, mode)

    @jax.custom_vjp
    def tn(a, b):
        return _dot(a, b, _TN, mode)

    nn.defvjp(lambda a, b: (nn(a, b), (a, b)), lambda r, d: (nt(d, r[1]), tn(r[0], d)))
    nt.defvjp(lambda a, b: (nt(a, b), (a, b)), lambda r, d: (nn(d, r[1]), tn(d, r[0])))
    tn.defvjp(lambda a, b: (tn(a, b), (a, b)), lambda r, d: (nt(r[1], d), nn(r[0], d)))
    return nn, nt, tn


_nn_hi, _nt_hi, _tn_hi = _make_dots("x3")
_nn_lo, _nt_lo, _tn_lo = _make_dots("lo")


@jax.custom_vjp
def _nn_const(a, b):
    return _dot(a, b, _NN, "xl")


_nn_const.defvjp(lambda a, b: (_nn_const(a, b), a), lambda a, d: (jnp.zeros_like(a), _dot(a, d, _TN, "xl")))


def _each(f, *lists):
    return [f(*xs) for xs in zip(*lists)]


def _gdn_chunk(q, k, v, gb, bb, state):
    C = GDN_C
    e0 = (lax.broadcasted_iota(I32, (1, LANES), 1) == 0).astype(F32)
    ri = lax.broadcasted_iota(I32, (C, C), 0)
    ci = lax.broadcasted_iota(I32, (C, C), 1)
    causal, strict = ri >= ci, ri > ci
    tri, eye, ones = causal.astype(F32), (ri == ci).astype(F32), jnp.ones((C, C), F32)
    last = lax.broadcasted_iota(I32, (C, LANES), 0) == C - 1
    g1 = _each(lambda a: jnp.sum(a * e0, -1, keepdims=True), gb)
    b1 = _each(lambda a: jnp.sum(a * e0, -1, keepdims=True), bb)
    gc_c = _each(lambda g: _nn_const(tri, jnp.broadcast_to(g, (C, C))), g1)
    gc_d = _each(lambda g: _nn_const(tri, jnp.broadcast_to(g, (C, LANES))), g1)
    gr_c = _each(lambda g: _nn_const(ones, eye * g), gc_c)
    decay = _each(lambda a, r: jnp.where(causal, jnp.exp(jnp.where(causal, a - r, 0.0)), 0.0), gc_c, gr_c)
    kb = _each(lambda a, b: a * b, k, b1)
    vb = _each(lambda a, b: a * b, v, b1)
    x = _each(lambda a, b, d: -jnp.where(strict, _nt_lo(a, b) * d, 0.0), kb, k, decay)
    ainv, p = _each(lambda a: eye + a, x), x
    for _ in range(5):
        p = _each(lambda a: _nn_hi(a, a), p)
        ainv = _each(lambda a, b: a + _nn_hi(a, b), ainv, p)
    u = _each(_nn_hi, ainv, vb)
    w = _each(lambda a, b, g: _nn_hi(a, b * jnp.exp(g)), ainv, kb, gc_d)
    attn = _each(lambda a, b, d: jnp.where(causal, _nt_lo(a, b) * d, 0.0), q, k, decay)
    v_new = _each(lambda a, b, s: a - _nn_lo(b, s), u, w, state)
    o = _each(lambda a, g, s, t, vn: _nn_lo(a * jnp.exp(g), s) + _nn_lo(t, vn), q, gc_d, state, attn, v_new)
    gl = _each(lambda g: jnp.sum(jnp.where(last, g, 0.0), axis=0, keepdims=True), gc_d)
    new_state = _each(lambda s, g, a, gd, vn: s * jnp.exp(jnp.sum(g * e0, -1, keepdims=True)) + _tn_lo(a * jnp.exp(g - gd), vn),
                      state, gl, k, gc_d, v_new)
    return o, new_state


def _head_slices(ref, width):
    return [ref[:, h * width:(h + 1) * width] for h in range(GDN_H)]


def gdn_chunk_fwd(qkv, g_b, beta_b, name):
    S_ = qkv.shape[0]
    N = S_ // GDN_C

    def body(q_ref, k_ref, v_ref, g_ref, b_ref, o_ref, s_ref, state):
        @pl.when(pl.program_id(0) == 0)
        def _():
            state[...] = jnp.zeros_like(state)

        st = [state[h] for h in range(GDN_H)]
        s_ref[0] = state[...]
        o, st2 = _gdn_chunk(_head_slices(q_ref, GDN_DK), _head_slices(k_ref, GDN_DK), _head_slices(v_ref, GDN_DV),
                            _head_slices(g_ref, GDN_DK), _head_slices(b_ref, GDN_DK), st)
        for h in range(GDN_H):
            o_ref[:, h * GDN_DV:(h + 1) * GDN_DV] = o[h]
            state[h] = st2[h]

    return _pc(body, grid=(N,),
               in_specs=[pl.BlockSpec((GDN_C, GDN_QK), lambda n: (n, 0)), pl.BlockSpec((GDN_C, GDN_QK), lambda n: (n, 1)),
                         pl.BlockSpec((GDN_C, GDN_V), lambda n: (n, 1)), pl.BlockSpec((GDN_C, GDN_QK), lambda n: (n, 0)),
                         pl.BlockSpec((GDN_C, GDN_QK), lambda n: (n, 0))],
               out_specs=[pl.BlockSpec((GDN_C, GDN_V), lambda n: (n, 0)),
                          pl.BlockSpec((1, GDN_H, GDN_DK, GDN_DV), lambda n: (n, 0, 0, 0))],
               out_shape=[_sds((S_, GDN_V), F32), _sds((N, GDN_H, GDN_DK, GDN_DV), F32)],
               scratch_shapes=[pltpu.VMEM((GDN_H, GDN_DK, GDN_DV), F32)],
               compiler_params=_cparams(("arbitrary",)), name=name)(qkv, qkv, qkv, g_b, beta_b)


def gdn_chunk_bwd(qkv, g_b, beta_b, states, do, name):
    S_ = qkv.shape[0]
    N = S_ // GDN_C

    def body(q_ref, k_ref, v_ref, g_ref, b_ref, s_ref, do_ref, dq_ref, dk_ref, dv_ref, dg_ref, db_ref, dstate):
        @pl.when(pl.program_id(0) == 0)
        def _():
            dstate[...] = jnp.zeros_like(dstate)

        _, vjp = jax.vjp(_gdn_chunk, _head_slices(q_ref, GDN_DK), _head_slices(k_ref, GDN_DK), _head_slices(v_ref, GDN_DV),
                         _head_slices(g_ref, GDN_DK), _head_slices(b_ref, GDN_DK), [s_ref[0, h] for h in range(GDN_H)])
        dq, dk, dv, dg, db, ds = vjp((_head_slices(do_ref, GDN_DV), [dstate[h] for h in range(GDN_H)]))
        for h in range(GDN_H):
            kk, vv = slice(h * GDN_DK, (h + 1) * GDN_DK), slice(h * GDN_DV, (h + 1) * GDN_DV)
            dq_ref[:, kk] = dq[h]
            dk_ref[:, kk] = dk[h]
            dv_ref[:, vv] = dv[h]
            dg_ref[:, kk] = dg[h]
            db_ref[:, kk] = db[h]
            dstate[h] = ds[h]

    r = lambda n: N - 1 - n
    qk = lambda c: pl.BlockSpec((GDN_C, GDN_QK), lambda n: (r(n), c))
    vs = lambda c: pl.BlockSpec((GDN_C, GDN_V), lambda n: (r(n), c))
    return _pc(body, grid=(N,),
               in_specs=[qk(0), qk(1), vs(1), qk(0), qk(0),
                         pl.BlockSpec((1, GDN_H, GDN_DK, GDN_DV), lambda n: (r(n), 0, 0, 0)), vs(0)],
               out_specs=[qk(0), qk(0), vs(0), qk(0), qk(0)],
               out_shape=[_sds((S_, GDN_QK), F32), _sds((S_, GDN_QK), F32), _sds((S_, GDN_V), F32),
                          _sds((S_, GDN_QK), F32), _sds((S_, GDN_QK), F32)],
               scratch_shapes=[pltpu.VMEM((GDN_H, GDN_DK, GDN_DV), F32)],
               compiler_params=_cparams(("arbitrary",)), name=name)(qkv, qkv, qkv, g_b, beta_b, states, do)


def _rope_tables(pos_ref, inv_ref, cm_ref, sg_ref):
    ang = pos_ref[...] * inv_ref[...]
    return jnp.cos(ang) * cm_ref[...], jnp.sin(ang) * sg_ref[...]


def mla_prep_fwd(qpad, kv, proj, pos, rope_consts, name):
    S_ = qpad.shape[0]
    ts = 256
    W = 2 * LANES

    def body(q_ref, kv_ref, kr_ref, pos_ref, inv_ref, cm_ref, sg_ref, qh_ref, kh_ref, vh_ref):
        cs, sn = _rope_tables(pos_ref, inv_ref, cm_ref, sg_ref)
        rope = lambda r: r * cs + pltpu.roll(r, LANES // 2, 1) * sn
        krr = rope(kr_ref[...]).astype(BF16)
        for h in range(MLA_H):
            qh_ref[h, :, 0:LANES] = (q_ref[:, h * W:h * W + LANES] * MLA_SCALE).astype(BF16)
            qh_ref[h, :, LANES:W] = (rope(q_ref[:, h * W + LANES:(h + 1) * W]) * MLA_SCALE).astype(BF16)
            kh_ref[h, :, 0:LANES] = kv_ref[:, h * W:h * W + LANES].astype(BF16)
            kh_ref[h, :, LANES:W] = krr
            vh_ref[h] = kv_ref[:, h * W + LANES:(h + 1) * W].astype(BF16)

    one = pl.BlockSpec((1, LANES), lambda i: (0, 0))
    return _pc(body, grid=(S_ // ts,),
               in_specs=[pl.BlockSpec((ts, MLA_H * W), lambda i: (i, 0)), pl.BlockSpec((ts, MLA_H * W), lambda i: (i, 0)),
                         pl.BlockSpec((ts, LANES), lambda i: (i, 1536 // LANES)), pl.BlockSpec((ts, 1), lambda i: (i, 0)),
                         one, one, one],
               out_specs=[pl.BlockSpec((MLA_H, ts, W), lambda i: (0, i, 0)), pl.BlockSpec((MLA_H, ts, W), lambda i: (0, i, 0)),
                          pl.BlockSpec((MLA_H, ts, LANES), lambda i: (0, i, 0))],
               out_shape=[_sds((MLA_H, S_, W), BF16), _sds((MLA_H, S_, W), BF16), _sds((MLA_H, S_, LANES), BF16)],
               compiler_params=_cparams(("parallel",)), name=name)(qpad, kv, proj, pos, *rope_consts)


def mla_prep_bwd(dqh, dkh, dvh, pos, rope_consts, into, name):
    S_ = dqh.shape[1]
    ts = 256
    W = 2 * LANES

    def body(dq_ref, dk_ref, dv_ref, pos_ref, inv_ref, cm_ref, sg_ref, _, dqp_ref, dkv_ref, dkr_ref):
        cs, sn = _rope_tables(pos_ref, inv_ref, cm_ref, sg_ref)
        rope_t = lambda g: g * cs + pltpu.roll(g * sn, LANES // 2, 1)
        acc = jnp.zeros((ts, LANES), F32)
        for h in range(MLA_H):
            dqp_ref[:, h * W:h * W + LANES] = dq_ref[h, :, 0:LANES] * MLA_SCALE
            dqp_ref[:, h * W + LANES:(h + 1) * W] = rope_t(dq_ref[h, :, LANES:W]) * MLA_SCALE
            dkv_ref[:, h * W:h * W + LANES] = dk_ref[h, :, 0:LANES]
            dkv_ref[:, h * W + LANES:(h + 1) * W] = dv_ref[h]
            acc = acc + dk_ref[h, :, LANES:W]
        dkr_ref[...] = rope_t(acc)

    one = pl.BlockSpec((1, LANES), lambda i: (0, 0))
    return _pc(body, grid=(S_ // ts,),
               in_specs=[pl.BlockSpec((MLA_H, ts, W), lambda i: (0, i, 0)), pl.BlockSpec((MLA_H, ts, W), lambda i: (0, i, 0)),
                         pl.BlockSpec((MLA_H, ts, LANES), lambda i: (0, i, 0)), pl.BlockSpec((ts, 1), lambda i: (i, 0)),
                         one, one, one, pl.BlockSpec(memory_space=pl.ANY)],
               out_specs=[pl.BlockSpec((ts, MLA_H * W), lambda i: (i, 0)), pl.BlockSpec((ts, MLA_H * W), lambda i: (i, 0)),
                          pl.BlockSpec((ts, LANES), lambda i: (i, 1536 // LANES))],
               out_shape=[_sds((S_, MLA_H * W), F32), _sds((S_, MLA_H * W), F32), _sds(into.shape, F32)],
               input_output_aliases={7: 2}, compiler_params=_cparams(("parallel",)), name=name)(dqh, dkh, dvh, pos, *rope_consts, into)


NEG = -1e30


FLASH_TILE = 1024


def _scores(q, k, diagonal):
    s = lax.dot_general(q, k, (_NT, ((), ())), preferred_element_type=F32)
    if not diagonal:
        return s
    t = s.shape[0]
    return jnp.where(lax.broadcasted_iota(I32, (t, t), 1) <= lax.broadcasted_iota(I32, (t, t), 0), s, NEG)


def flash_fwd(qh, kh, vh, name):
    H, S_, W = qh.shape
    t = _tile(S_, FLASH_TILE)
    n = S_ // t

    def body(q_ref, k_ref, v_ref, o_ref, lse_ref, m_s, l_s, acc):
        qi, kj = pl.program_id(1), pl.program_id(2)

        @pl.when(kj == 0)
        def _():
            m_s[...] = jnp.full_like(m_s, NEG)
            l_s[...] = jnp.zeros_like(l_s)
            acc[...] = jnp.zeros_like(acc)

        def step(diagonal):
            s = _scores(q_ref[...], k_ref[...], diagonal)
            m_old = m_s[...]
            m_new = jnp.maximum(m_old, jnp.max(s, axis=-1, keepdims=True))
            alpha = jnp.exp(m_old - m_new)
            p = jnp.exp(s - m_new[:, :1])
            l_s[...] = alpha * l_s[...] + jnp.sum(p, axis=-1, keepdims=True)
            acc[...] = alpha * acc[...] + lax.dot_general(p.astype(BF16), v_ref[...], (_NN, ((), ())), preferred_element_type=F32)
            m_s[...] = m_new

        pl.when(kj < qi)(lambda: step(False))
        pl.when(kj == qi)(lambda: step(True))

        @pl.when(kj == n - 1)
        def _():
            o_ref[...] = acc[...] / l_s[...]
            lse_ref[...] = m_s[...] + jnp.log(l_s[...])

    return _pc(body, grid=(H, n, n),
               in_specs=[pl.BlockSpec((None, t, W), lambda h, i, j: (h, i, 0)),
                         pl.BlockSpec((None, t, W), lambda h, i, j: (h, jnp.minimum(i, j), 0)),
                         pl.BlockSpec((None, t, LANES), lambda h, i, j: (h, jnp.minimum(i, j), 0))],
               out_specs=[pl.BlockSpec((t, LANES), lambda h, i, j: (i, h)), pl.BlockSpec((None, t, LANES), lambda h, i, j: (h, i, 0))],
               out_shape=[_sds((S_, H * LANES), F32), _sds((H, S_, LANES), F32)],
               scratch_shapes=[pltpu.VMEM((t, LANES), F32)] * 3,
               compiler_params=_cparams(("parallel", "parallel", "arbitrary")), name=name)(qh, kh, vh)


def flash_bwd(qh, kh, vh, o, lse, do, name):
    H, S_, W = qh.shape
    t = _tile(S_, FLASH_TILE)
    n = S_ // t

    def body(q_ref, k_ref, v_ref, o_ref, lse_ref, do_ref, dq_ref, dk_ref, dv_ref):
        kj, qi = pl.program_id(1), pl.program_id(2)

        @pl.when(jnp.logical_and(kj == 0, qi == 0))
        def _():
            dq_ref[...] = jnp.zeros_like(dq_ref)

        @pl.when(qi == 0)
        def _():
            dk_ref[...] = jnp.zeros_like(dk_ref)
            dv_ref[...] = jnp.zeros_like(dv_ref)

        def step(diagonal):
            q, k, v = q_ref[...], k_ref[...], v_ref[...]
            do_ = do_ref[...]
            p = jnp.exp(_scores(q, k, diagonal) - lse_ref[:, :1])
            dob = do_.astype(BF16)
            dv_ref[...] += lax.dot_general(p.astype(BF16), dob, (_TN, ((), ())), preferred_element_type=F32)
            dp = lax.dot_general(dob, v, (_NT, ((), ())), preferred_element_type=F32)
            delta = jnp.sum(do_ * o_ref[...], axis=-1, keepdims=True)
            ds = (p * (dp - delta)).astype(BF16)
            dk_ref[...] += lax.dot_general(ds, q, (_TN, ((), ())), preferred_element_type=F32)
            rows = pl.ds(pl.multiple_of(qi * t, t), t)
            dq_ref[rows, :] += lax.dot_general(ds, k, (_NN, ((), ())), preferred_element_type=F32)

        pl.when(qi > kj)(lambda: step(False))
        pl.when(qi == kj)(lambda: step(True))

    qrow = lambda h, j, i: jnp.maximum(i, j)
    return _pc(body, grid=(H, n, n),
               in_specs=[pl.BlockSpec((None, t, W), lambda h, j, i: (h, qrow(h, j, i), 0)),
                         pl.BlockSpec((None, t, W), lambda h, j, i: (h, j, 0)),
                         pl.BlockSpec((None, t, LANES), lambda h, j, i: (h, j, 0)),
                         pl.BlockSpec((t, LANES), lambda h, j, i: (qrow(h, j, i), h)),
                         pl.BlockSpec((None, t, LANES), lambda h, j, i: (h, qrow(h, j, i), 0)),
                         pl.BlockSpec((t, LANES), lambda h, j, i: (qrow(h, j, i), h))],
               out_specs=[pl.BlockSpec((None, S_, W), lambda h, j, i: (h, 0, 0)),
                          pl.BlockSpec((None, t, W), lambda h, j, i: (h, j, 0)),
                          pl.BlockSpec((None, t, LANES), lambda h, j, i: (h, j, 0))],
               out_shape=[_sds((H, S_, W), F32), _sds((H, S_, W), F32), _sds((H, S_, LANES), F32)],
               compiler_params=_cparams(("parallel", "arbitrary", "arbitrary")), name=name)(qh, kh, vh, o, lse, do)


def loss_head(x, target, g, name):
    S_ = x.shape[0]
    ts = 256

    def body(x_ref, t_ref, g_ref, l_ref, dx_ref, dg_ref):
        @pl.when(pl.program_id(0) == 0)
        def _():
            l_ref[...] = jnp.zeros_like(l_ref)
            dg_ref[...] = jnp.zeros_like(dg_ref)

        y, vjp = jax.vjp(_rms, x_ref[...], g_ref[...])
        err = y - t_ref[...]
        l_ref[...] += 0.5 * jnp.sum(jnp.sum(err * err, axis=-1, keepdims=True), axis=0, keepdims=True) / D
        dx, dg = vjp(err / D)
        dx_ref[...] = dx
        dg_ref[...] += dg

    row = pl.BlockSpec((ts, D), lambda i: (i, 0))
    return _pc(body, grid=(S_ // ts,), in_specs=[row, row, pl.BlockSpec((1, D), lambda i: (0, 0))],
               out_specs=[pl.BlockSpec((1, LANES), lambda i: (0, 0)), row, pl.BlockSpec((1, D), lambda i: (0, 0))],
               out_shape=[_sds((1, LANES), F32), _sds((S_, D), F32), _sds((1, D), F32)],
               compiler_params=_cparams(("arbitrary",)), name=name)(x, target, g)


def adamw(w, parts, m, v, name):
    R, C = w.shape
    rows = [p.shape[1] for p in parts[0]]
    tr = R
    for cand in (512, 256, 128, 64, 32, 16, 8):
        if all(r % cand == 0 for r in rows) and cand * C * 4 <= 1024 * 1024:
            tr = cand
            break
    c1 = 1.0 - ADAM_B1 ** ADAM_STEP
    c2 = 1.0 - ADAM_B2 ** ADAM_STEP
    starts = [sum(rows[:k]) // tr for k in range(len(rows))]
    flat = [p for part in parts for p in part]

    def body(*refs):
        w_ref, m_ref, v_ref = refs[0], refs[1 + len(flat)], refs[2 + len(flat)]
        g_ref, d_ref, nm_ref, nv_ref = refs[3 + len(flat):]
        i = pl.program_id(0)
        gg, at = None, 1
        for part in parts:
            val = None
            for k in range(len(part)):
                p_ref = refs[at]
                at += 1
                s = p_ref[0].astype(F32)
                for n in range(1, p_ref.shape[0]):
                    s = s + p_ref[n].astype(F32)
                val = s if val is None else jnp.where(i >= starts[k], s, val)
            gg = val if gg is None else gg + val
        m2 = ADAM_B1 * m_ref[...] + (1.0 - ADAM_B1) * gg
        v2 = ADAM_B2 * v_ref[...] + (1.0 - ADAM_B2) * (gg * gg)
        g_ref[...] = gg
        d_ref[...] = -ADAM_LR * ((m2 / c1) / (jnp.sqrt(v2 / c2) + ADAM_EPS) + ADAM_WD * w_ref[...])
        nm_ref[...] = m2
        nv_ref[...] = v2

    blk = pl.BlockSpec((tr, C), lambda i: (i, 0))
    piece = lambda p, k: pl.BlockSpec((p.shape[0], tr, C), lambda i: (0, jnp.clip(i - starts[k], 0, rows[k] // tr - 1), 0))
    pblk = [piece(p, k) for part in parts for k, p in enumerate(part)]
    return _pc(body, grid=(R // tr,), in_specs=[blk] + pblk + [blk, blk], out_specs=[blk] * 4, out_shape=[_sds((R, C), F32)] * 4,
               compiler_params=_cparams(("parallel",)), name=name)(w, *flat, m, v)


def sum_slots(own, recv, skip, name):
    n, R, C = recv.shape
    tr = _tile(R, 512) if R % LANES == 0 else R
    has_own = own is not None

    def body(*refs):
        skip_ref = refs[0]
        r_ref, o_ref = refs[-2], refs[-1]
        acc = refs[1][...] if has_own else jnp.zeros(o_ref.shape, F32)
        for s in range(n):
            acc = acc + jnp.where(skip_ref[0] == s, 0.0, r_ref[s].astype(F32))
        o_ref[...] = acc

    row = pl.BlockSpec((tr, C), lambda i, sk: (i, 0))
    gs = pltpu.PrefetchScalarGridSpec(
        num_scalar_prefetch=1, grid=(R // tr,),
        in_specs=([row] if has_own else []) + [pl.BlockSpec((n, tr, C), lambda i, sk: (0, i, 0))], out_specs=row)
    ins = ([own] if has_own else []) + [recv]
    return _pc(body, grid_spec=gs, out_shape=_sds((R, C), F32), compiler_params=_cparams(("parallel",)), name=name)(skip, *ins)


def _chip_peers():
    x, y, c = lax.axis_index("x"), lax.axis_index("y"), lax.axis_index("c")
    return (x, y, c), [(1 - x, y, c), (x, 1 - y, c), (1 - x, 1 - y, c)]


def _chip_index(p):
    return 2 * p[0] + p[1]


def _win(ref, axis, chip, size):
    if axis is None:
        return ref.at[chip]
    idx = [slice(None)] * len(ref.shape)
    idx[axis] = pl.ds(pl.multiple_of(chip * size, size), size)
    return ref.at[tuple(idx)]


def _remote(src, dst, send_sem, recv_sem, peer):
    return pltpu.make_async_remote_copy(src_ref=src, dst_ref=dst, send_sem=send_sem, recv_sem=recv_sem, device_id=peer,
                                        device_id_type=MESH)


HBM_SPEC = pl.BlockSpec(memory_space=pltpu.HBM)
SEM_SPEC = pl.BlockSpec(memory_space=pltpu.SEMAPHORE)
ANY_SPEC = pl.BlockSpec(memory_space=pl.ANY)
DATAFLOW = pltpu.SideEffectType.DATAFLOW_SIDE_EFFECTING


def gather_piece(i, l, o, axis, size):
    return (i, lambda r, chip: r.at[l], o, lambda r, chip: _win(r, axis, chip, size))


def scatter_piece(i, o, axis, size):
    return (i, lambda r, chip: _win(r, axis, chip, size), o, lambda r, chip: r.at[chip])


def _copies(pieces, in_refs, out_refs, send, recv):
    me, peers = _chip_peers()
    mine = _chip_index(me)
    local, remote = [], []
    for n, (i, src, o, dst) in enumerate(pieces):
        d = dst(out_refs[o], mine)
        local.append((src(in_refs[i], mine), d))
        remote += [_remote(src(in_refs[i], _chip_index(p)), d, send.at[3 * n + k], recv.at[3 * n + k], p)
                   for k, p in enumerate(peers)]
    return local, remote


def exchange_start(pieces, ins, out_shapes, after, name):
    n_in, n_out, ncp = len(ins), len(out_shapes), len(pieces)

    def body(*refs):
        in_refs, land_refs = refs[:n_in], refs[n_in:n_in + n_out]
        send, recv = refs[n_in + n_out + 1], refs[n_in + n_out + 2]
        token, local_sems = refs[-2], refs[-1]
        local, remote = _copies(pieces, in_refs, land_refs, send, recv)
        own = [pltpu.make_async_copy(s, d, local_sems.at[n]) for n, (s, d) in enumerate(local)]
        for cp in own + remote:
            cp.start()
        for cp in own:
            cp.wait()
        token[...] = jnp.zeros_like(token)

    hbm = lambda a: pltpu.with_memory_space_constraint(a, pltpu.HBM)
    lands = [hbm(lax.empty(s.shape, s.dtype)) for s in out_shapes]
    sem = pltpu.SemaphoreType.DMA((3 * ncp,))
    thru = [pltpu.HBM(a.shape, a.dtype) for a in ins] + [pltpu.HBM(s.shape, s.dtype) for s in out_shapes]
    res = _pc(body, in_specs=[HBM_SPEC] * (n_in + n_out) + [ANY_SPEC],
              out_specs=[SEM_SPEC, SEM_SPEC] + [HBM_SPEC] * (n_in + n_out) + [pl.BlockSpec(memory_space=pltpu.VMEM)],
              out_shape=[sem, sem] + thru + [_sds((8, LANES), F32)],
              input_output_aliases={i: 2 + i for i in range(n_in + n_out)},
              scratch_shapes=[pltpu.SemaphoreType.DMA((ncp,))],
              compiler_params=pltpu.CompilerParams(has_side_effects=DATAFLOW), name=name)(*[hbm(a) for a in ins], *lands, after)
    return (res[0], res[1]), list(res[2:2 + n_in]), list(res[2 + n_in:2 + n_in + n_out]), res[-1]


def exchange_wait(pieces, sems, ins, lands, after, name):
    n_in, n_out = len(ins), len(lands)

    def body(*refs):
        in_refs, land_refs = refs[:n_in], refs[n_in:n_in + n_out]
        send, recv = refs[n_in + n_out], refs[n_in + n_out + 1]
        _, remote = _copies(pieces, in_refs, land_refs, send, recv)
        for cp in remote:
            cp.wait_send()
            cp.wait_recv()

    thru = [pltpu.HBM(a.shape, a.dtype) for a in ins] + [pltpu.HBM(a.shape, a.dtype) for a in lands]
    res = _pc(body, in_specs=[HBM_SPEC] * (n_in + n_out) + [SEM_SPEC, SEM_SPEC, ANY_SPEC], out_specs=[HBM_SPEC] * (n_in + n_out),
              out_shape=thru, input_output_aliases={i: i for i in range(n_in + n_out)},
              compiler_params=pltpu.CompilerParams(has_side_effects=DATAFLOW), name=name)(*ins, *lands, sems[0], sems[1], after)
    return list(res[n_in:])


def swap_cores(bufs, name):
    nb = len(bufs)

    def body(*refs):
        in_refs, out_refs = refs[:nb], refs[nb:2 * nb]
        send, recv = refs[2 * nb:]
        x, y, c = lax.axis_index("x"), lax.axis_index("y"), lax.axis_index("c")
        copies = [_remote(in_refs[b], out_refs[b], send.at[b], recv.at[b], (x, y, 1 - c)) for b in range(nb)]
        for cp in copies:
            cp.start()
        for cp in copies:
            cp.wait()

    anyspec = pl.BlockSpec(memory_space=pl.ANY)
    return _pc(body, in_specs=[anyspec] * nb, out_specs=[anyspec] * nb, out_shape=[_sds(b.shape, b.dtype) for b in bufs],
               scratch_shapes=[pltpu.SemaphoreType.DMA((nb,)), pltpu.SemaphoreType.DMA((nb,))], name=name)(*bufs)


def exchange_all(buf, name):
    def body(in_ref, out_ref, send, recv, local):
        x, y, c = lax.axis_index("x"), lax.axis_index("y"), lax.axis_index("c")
        mine = 4 * x + 2 * y + c
        loc = pltpu.make_async_copy(in_ref, out_ref.at[mine], local)
        loc.start()
        copies = [loc]
        for k in range(1, 8):
            peer = (x ^ (k >> 2), y ^ ((k >> 1) & 1), c ^ (k & 1))
            cp = pltpu.make_async_remote_copy(src_ref=in_ref, dst_ref=out_ref.at[mine], send_sem=send.at[k - 1],
                                              recv_sem=recv.at[k - 1], device_id=peer, device_id_type=MESH)
            cp.start()
            copies.append(cp)
        for cp in copies:
            cp.wait()

    anyspec = pl.BlockSpec(memory_space=pl.ANY)
    return _pc(body, in_specs=[anyspec], out_specs=anyspec, out_shape=_sds((8,) + buf.shape, buf.dtype),
               scratch_shapes=[pltpu.SemaphoreType.DMA((7,)), pltpu.SemaphoreType.DMA((7,)), pltpu.SemaphoreType.DMA],
               name=name)(buf)


def _norm_fwd(x, g, name):
    return rowwise(f_rms, [(x, D, 0, 0)], [(g, D, 0, 0)], [(D, 0, BF16)], ts=512, name=name)[0]


def _norm_bwd(x, g, dh, dres, name):
    (dx,), (dg,) = rowwise_bwd(f_rms, [(x, D, 0, 0)], [(g, D, 0, 0)], [(dh, D, 0, 0)], need=[True],
                               adds={0: (dres, D, 0, 0)}, ts=256, name=name)
    return dx, dg


def pool_fwd(x, W, tag):
    h = _norm_fwd(x, W["ng"], tag + "_norm")
    proj = mm(h, W["w_in"], name=tag + "_in")
    p = pool_time_fwd(proj, tag + "_win")
    pg = gmm("nn", p, W["w_grp"], G=4, name=tag + "_grp")
    y = rowwise(f_pool_gate, [(pg, POOL_GROUP, 0, 1), (proj, POOL_GROUP, 4, 1)], [(W["scale"], POOL_GROUP, 0, 1)],
                [(POOL_GROUP, 1, BF16)], ncol=4, ts=512, name=tag + "_gate")[0]
    xn = mm(y, W["w_out"], add=x, name=tag + "_out")
    return xn, (x, h, proj, p, pg, y)


def pool_bwd(dxn, W, saved, tag):
    x, h, proj, p, pg, y = saved
    dy = mm(dxn, W["w_out"], tb=True, name=tag + "_dy")
    g = {"w_out": mm(y, dxn, ta=True, out_dtype=BF16, name=tag + "_dwout")}
    (dpg, dproj), (g["scale"],) = rowwise_bwd(
        f_pool_gate, [(pg, POOL_GROUP, 0, 1), (proj, POOL_GROUP, 4, 1)], [(W["scale"], POOL_GROUP, 0, 1)],
        [(dy, POOL_GROUP, 0, 1)], need=[True, True], place={1: (2 * POOL_WIDTH, 4)}, ncol=4, ts=512, name=tag + "_dgate")
    dp = gmm("nt", dpg, W["w_grp"], G=4, name=tag + "_dp")
    g["w_grp"] = gmm("tn", p, dpg, G=4, out_dtype=BF16, name=tag + "_dwgrp")
    dproj = pool_time_bwd(dp, dproj, tag + "_dwin")
    dh = mm(dproj, W["w_in"], tb=True, name=tag + "_dh")
    g["w_in"] = mm(h, dproj, ta=True, out_dtype=BF16, name=tag + "_dw_in")
    dx, g["ng"] = _norm_bwd(x, W["ng"], dh, dxn, tag + "_dnorm")
    return dx, g


def gdn_fwd(x, W, tag):
    h = _norm_fwd(x, W["ng"], tag + "_norm")
    proj = mm(h, W["w_in"], name=tag + "_in")
    qkv = gdn_conv_fwd(proj, W["conv"], tag + "_conv")
    g_b, beta_b = rowwise(f_gdn_gates, [(proj, LANES, 6144 // LANES, 0)], [(W["a_log"], LANES, 0, 0), (W["dt_bias"], LANES, 0, 0)],
                          [(GDN_QK, 0, F32), (GDN_QK, 0, F32)], ts=512, name=tag + "_gates")
    o, states = gdn_chunk_fwd(qkv, g_b, beta_b, tag + "_chunk")
    og = rowwise(f_gdn_out, [(o, GDN_DV, 0, 1), (proj, GDN_DV, 4096 // GDN_DV, 1)], [(W["norm_g"], GDN_DV, 0, 0)],
                 [(GDN_DV, 1, BF16)], ncol=GDN_H, ts=512, name=tag + "_onorm")[0]
    xn = mm(og, W["w_out"], add=x, name=tag + "_out")
    return xn, (x, h, proj, qkv, g_b, beta_b, o, states, og)


def gdn_bwd(dxn, W, saved, tag):
    x, h, proj, qkv, g_b, beta_b, o, states, og = saved
    dog = mm(dxn, W["w_out"], tb=True, name=tag + "_dog")
    g = {"w_out": mm(og, dxn, ta=True, out_dtype=BF16, name=tag + "_dwout")}
    (do, dproj), (g["norm_g"],) = rowwise_bwd(
        f_gdn_out, [(o, GDN_DV, 0, 1), (proj, GDN_DV, 4096 // GDN_DV, 1)], [(W["norm_g"], GDN_DV, 0, 0)],
        [(dog, GDN_DV, 0, 1)], need=[True, True], place={1: (GDN_IN_PAD, 4096 // GDN_DV)}, ncol=GDN_H, ts=512, name=tag + "_donorm")
    dq, dk, dv, dg_b, dbeta_b = gdn_chunk_bwd(qkv, g_b, beta_b, states, do, tag + "_dchunk")
    (dproj,), (g["a_log"], g["dt_bias"]) = rowwise_bwd(
        f_gdn_gates, [(proj, LANES, 6144 // LANES, 0)], [(W["a_log"], LANES, 0, 0), (W["dt_bias"], LANES, 0, 0)],
        [(dg_b, GDN_QK, 0, 0), (dbeta_b, GDN_QK, 0, 0)], need=[True], place={0: (dproj, 6144 // LANES)}, ts=256, name=tag + "_dgates")
    dproj, g["conv"] = gdn_conv_bwd(proj, W["conv"], dq, dk, dv, dproj, tag + "_dconv")
    dh = mm(dproj, W["w_in"], tb=True, name=tag + "_dh")
    g["w_in"] = mm(h, dproj, ta=True, out_dtype=BF16, name=tag + "_dw_in")
    dx, g["ng"] = _norm_bwd(x, W["ng"], dh, dxn, tag + "_dnorm")
    return dx, g


def mla_fwd(x, pos, W, tag):
    h = _norm_fwd(x, W["ng"], tag + "_norm")
    proj = mm(h, W["w_in"], name=tag + "_in")
    hq = rowwise(f_rms, [(proj, MLA_Q_LORA, 0, 0)], [(W["q_g"], MLA_Q_LORA, 0, 0)], [(MLA_Q_LORA, 0, BF16)], ts=512, name=tag + "_qnorm")[0]
    hkv = rowwise(f_rms, [(proj, MLA_KV_LORA, 2, 0)], [(W["kv_g"], MLA_KV_LORA, 0, 0)], [(MLA_KV_LORA, 0, BF16)], ts=512, name=tag + "_kvnorm")[0]
    qpad = mm(hq, W["w_uq"], name=tag + "_uq")
    kv = mm(hkv, W["w_ukv"], name=tag + "_ukv")
    qh, kh, vh = mla_prep_fwd(qpad, kv, proj, pos, W["rope"], tag + "_prep")
    o, lse = flash_fwd(qh, kh, vh, tag + "_attn")
    og = rowwise(f_ogate, [(o, 512, 0, 1), (proj, 512, 4, 1)], [], [(512, 1, BF16)], ncol=4, ts=512, name=tag + "_ogate")[0]
    xn = mm(og, W["w_out"], add=x, name=tag + "_out")
    return xn, (x, h, proj, hq, hkv, qh, kh, vh, o, lse, og)


def mla_bwd(dxn, pos, W, saved, tag):
    x, h, proj, hq, hkv, qh, kh, vh, o, lse, og = saved
    dog = mm(dxn, W["w_out"], tb=True, name=tag + "_dog")
    g = {"w_out": mm(og, dxn, ta=True, out_dtype=BF16, name=tag + "_dwout")}
    dproj = jnp.zeros(proj.shape, F32)
    (do, dproj), _ = rowwise_bwd(f_ogate, [(o, 512, 0, 1), (proj, 512, 4, 1)], [], [(dog, 512, 0, 1)], need=[True, True],
                                 place={1: (dproj, 4)}, ncol=4, ts=512, name=tag + "_dogate")
    dqh, dkh, dvh = flash_bwd(qh, kh, vh, o, lse, do, tag + "_dattn")
    dqpad, dkv, dproj = mla_prep_bwd(dqh, dkh, dvh, pos, W["rope"], dproj, tag + "_dprep")
    dhq = mm(dqpad, W["w_uq"], tb=True, name=tag + "_dhq")
    g["w_uq"] = mm(hq, dqpad, ta=True, out_dtype=BF16, name=tag + "_dwuq")
    dhkv = mm(dkv, W["w_ukv"], tb=True, name=tag + "_dhkv")
    g["w_ukv"] = mm(hkv, dkv, ta=True, out_dtype=BF16, name=tag + "_dwukv")
    (dproj,), (g["q_g"],) = rowwise_bwd(f_rms, [(proj, MLA_Q_LORA, 0, 0)], [(W["q_g"], MLA_Q_LORA, 0, 0)], [(dhq, MLA_Q_LORA, 0, 0)],
                                        need=[True], place={0: (dproj, 0)}, ts=256, name=tag + "_dqnorm")
    (dproj,), (g["kv_g"],) = rowwise_bwd(f_rms, [(proj, MLA_KV_LORA, 2, 0)], [(W["kv_g"], MLA_KV_LORA, 0, 0)], [(dhkv, MLA_KV_LORA, 0, 0)],
                                         need=[True], place={0: (dproj, 2)}, ts=256, name=tag + "_dkvnorm")
    dh = mm(dproj, W["w_in"], tb=True, name=tag + "_dh")
    g["w_in"] = mm(h, dproj, ta=True, out_dtype=BF16, name=tag + "_dw_in")
    dx, g["ng"] = _norm_bwd(x, W["ng"], dh, dxn, tag + "_dnorm")
    return dx, g


def _pad_cols(a, n):
    return jnp.pad(a, ((0, 0), (0, n - a.shape[1])))


def _mla_w_in_layout(w):
    z = lambda n: jnp.zeros((w.shape[0], n), w.dtype)
    kr = w[:, 1280:1344]
    return jnp.concatenate([w[:, :768], z(256), w[:, 768:1280], kr[:, :32], z(32), kr[:, 32:], z(32), z(384), w[:, 1344:]], axis=1)


def _mla_w_in_unlayout(g):
    return jnp.concatenate([g[:, :768], g[:, 1024:1536], g[:, 1536:1568], g[:, 1600:1632], g[:, 2048:]], axis=1)


def _mla_w_uq_layout(w):
    w3 = w.reshape(w.shape[0], MLA_H, MLA_NOPE + MLA_ROPE)
    z = jnp.zeros((w.shape[0], MLA_H, 32), w.dtype)
    return jnp.concatenate([w3[..., :128], w3[..., 128:160], z, w3[..., 160:192], z], axis=-1).reshape(w.shape[0], MLA_H * 256)


def _mla_w_uq_unlayout(g):
    g3 = g.reshape(g.shape[0], MLA_H, 256)
    return jnp.concatenate([g3[..., :128], g3[..., 128:160], g3[..., 192:224]], axis=-1).reshape(g.shape[0], MLA_H * 192)


def _rope_consts():
    half = MLA_ROPE // 2
    inv = ROPE_THETA ** (-jnp.arange(half, dtype=F32) / half)
    z = jnp.zeros((half,), F32)
    o = jnp.ones((half,), F32)
    row = lambda *p: jnp.concatenate(p).reshape(1, LANES)
    return row(inv, z, inv, z), row(o, z, o, z), row(-o, z, o, z)


BIG = ["pool_w_in", "pool_w_grp", "pool_w_out", "gdn_w_in", "gdn_w_out", "mla_w_in", "mla_w_uq", "mla_w_ukv", "mla_w_out"]
BIG_LAYOUT = {"pool_w_in": (1, 1024, (1024, 4096)), "pool_w_grp": (1, 128, (4, 512, 512)), "pool_w_out": (0, 512, (2048, 1024)),
              "gdn_w_in": (None, None, (4, 1024, 1540)), "gdn_w_out": (0, 512, (2048, 1024)),
              "mla_w_in": (None, None, (4, 1024, 848)), "mla_w_uq": (1, 768, (768, 3072)), "mla_w_ukv": (1, 1024, (512, 4096)),
              "mla_w_out": (0, 512, (2048, 1024))}
SMALL_SHARDED = ["pool_scale", "gdn_conv", "mla_q_norm_g", "mla_kv_norm_g"]
SMALL_AXIS = {"pool_scale": 1, "gdn_conv": 2, "mla_q_norm_g": 1, "mla_kv_norm_g": 1}
REPLICATED = ["norm_g", "gdn_a_log", "gdn_dt_bias", "gdn_norm_g", "final_g"]
PACK_C = 1024


def _pack(parts, dtype, row_mult):
    flat = jnp.concatenate([p.reshape(-1).astype(dtype) for p in parts])
    rows = -(-flat.shape[0] // PACK_C)
    rows = -(-rows // row_mult) * row_mult
    return jnp.pad(flat, (0, rows * PACK_C - flat.shape[0])).reshape(rows, PACK_C)


def _unpack(buf, shapes):
    lead = buf.shape[:-2]
    flat = buf.reshape(lead + (-1,))
    out, off = [], 0
    for s in shapes:
        n = int(np.prod(s))
        out.append(flat[..., off:off + n].reshape(lead + tuple(s)))
        off += n
    return out


def _unshard(g4, axis):
    a = jnp.moveaxis(g4, 0, axis)
    s = a.shape
    return a.reshape(s[:axis] + (s[axis] * s[axis + 1],) + s[axis + 2:])


def _to_shards(a, axis):
    s = a.shape
    return jnp.moveaxis(a.reshape(s[:axis] + (4, s[axis] // 4) + s[axis + 1:]), axis, 0)


def layer_weights(full, small, rep, layer):
    ng = rep["norm_g"][layer:layer + 1]
    side_by_side = lambda a4: jnp.moveaxis(a4, 0, 1).reshape(a4.shape[1], 4 * a4.shape[2])
    if layer in (0, 3):
        j = layer // 3
        return dict(ng=ng, w_in=full[("pool_w_in", j)], w_grp=full[("pool_w_grp", j)], scale=small["pool_scale"][j:j + 1],
                    w_out=full[("pool_w_out", j)])
    if layer == 1:
        return dict(ng=ng, w_in=_pad_cols(side_by_side(full[("gdn_w_in", 0)]), GDN_IN_PAD),
                    conv=jnp.pad(small["gdn_conv"][0], ((0, 4), (0, 0))), a_log=_pad_cols(rep["gdn_a_log"], LANES),
                    dt_bias=_pad_cols(rep["gdn_dt_bias"], LANES), norm_g=rep["gdn_norm_g"], w_out=full[("gdn_w_out", 0)])
    return dict(ng=ng, w_in=_mla_w_in_layout(side_by_side(full[("mla_w_in", 0)])), q_g=small["mla_q_norm_g"],
                kv_g=small["mla_kv_norm_g"], w_uq=_mla_w_uq_layout(full[("mla_w_uq", 0)]), w_ukv=full[("mla_w_ukv", 0)],
                w_out=full[("mla_w_out", 0)], rope=_rope_consts())


def big_grad_pieces(gl):
    g0, g1, g2, g3 = gl
    slots = lambda a: jnp.moveaxis(a.reshape(a.shape[0], 4, a.shape[1] // 4), 1, 0)
    out = {}
    for l, g in ((0, g0), (1, g3)):
        if g is not None:
            out.update({("pool_w_in", l): g["w_in"], ("pool_w_grp", l): g["w_grp"], ("pool_w_out", l): g["w_out"]})
    if g1 is not None:
        out.update({("gdn_w_in", 0): slots(g1["w_in"][:, :GDN_IN]), ("gdn_w_out", 0): g1["w_out"]})
    if g2 is not None:
        out.update({("mla_w_in", 0): slots(_mla_w_in_unlayout(g2["w_in"])), ("mla_w_uq", 0): _mla_w_uq_unlayout(g2["w_uq"]),
                    ("mla_w_ukv", 0): g2["w_ukv"], ("mla_w_out", 0): g2["w_out"]})
    return out


def small_grads(gl, dfinal):
    g0, g1, g2, g3 = gl
    return {"norm_g": jnp.concatenate([g0["ng"], g1["ng"], g2["ng"], g3["ng"]], axis=0),
            "pool_scale": jnp.concatenate([g0["scale"], g3["scale"]], axis=0), "gdn_conv": g1["conv"][None, :4],
            "gdn_a_log": g1["a_log"][:, :GDN_H], "gdn_dt_bias": g1["dt_bias"][:, :GDN_H], "gdn_norm_g": g1["norm_g"],
            "mla_q_norm_g": g2["q_g"], "mla_kv_norm_g": g2["kv_g"], "final_g": dfinal.reshape(D)}


NAMES = ["norm_g", "pool_w_in", "pool_w_grp", "pool_scale", "pool_w_out", "gdn_w_in", "gdn_conv", "gdn_a_log", "gdn_dt_bias",
         "gdn_norm_g", "gdn_w_out", "mla_w_in", "mla_q_norm_g", "mla_w_uq", "mla_kv_norm_g", "mla_w_ukv", "mla_w_out", "final_g"]


def kernel(x, positions, norm_g, pool_w_in, pool_w_grp, pool_scale, pool_w_out, gdn_w_in, gdn_conv, gdn_a_log, gdn_dt_bias, gdn_norm_g, gdn_w_out, mla_w_in, mla_q_norm_g, mla_w_uq, mla_kv_norm_g, mla_w_ukv, mla_w_out, final_g, loss_target, m_norm_g, m_pool_w_in, m_pool_w_grp, m_pool_scale, m_pool_w_out, m_gdn_w_in, m_gdn_conv, m_gdn_a_log, m_gdn_dt_bias, m_gdn_norm_g, m_gdn_w_out, m_mla_w_in, m_mla_q_norm_g, m_mla_w_uq, m_mla_kv_norm_g, m_mla_w_ukv, m_mla_w_out, m_final_g, v_norm_g, v_pool_w_in, v_pool_w_grp, v_pool_scale, v_pool_w_out, v_gdn_w_in, v_gdn_conv, v_gdn_a_log, v_gdn_dt_bias, v_gdn_norm_g, v_gdn_w_out, v_mla_w_in, v_mla_q_norm_g, v_mla_w_uq, v_mla_kv_norm_g, v_mla_w_ukv, v_mla_w_out, v_final_g):
    args = locals()
    w = {n: args[n] for n in NAMES}
    m = {n: args["m_" + n] for n in NAMES}
    v = {n: args["v_" + n] for n in NAMES}
    my_chip = (2 * lax.axis_index("x") + lax.axis_index("y")).astype(I32)
    S_ = x.shape[1]
    x0, pos, target = x[0], positions.reshape(S_, 1).astype(F32), loss_target[0]
    rep = {n: w[n] for n in REPLICATED}

    shard = {n: w[n].astype(BF16) for n in BIG}
    small_shapes = [w[n].shape for n in SMALL_SHARDED]
    shard["small"] = _pack([w[n] for n in SMALL_SHARDED], F32, 8)[None]
    layout = dict(BIG_LAYOUT, small=(None, None, (4,) + shard["small"].shape[1:]))

    def gather_start(group, after, tag):
        pieces = [gather_piece(i, l, i, layout[n][0], layout[n][1]) for i, (n, l) in enumerate(group)]
        shapes = [_sds(layout[n][2], shard[n].dtype) for n, l in group]
        sems, ins, lands, token = exchange_start(pieces, [shard[n] for n, l in group], shapes, after, tag + "_start")
        return (pieces, sems, ins, lands), token

    def finish(handle, after, tag):
        return exchange_wait(*handle, after, tag + "_wait")

    tied = lambda a, token: a + token[0:1, 0:1]
    group_a = [("pool_w_in", 0), ("pool_w_grp", 0), ("pool_w_out", 0), ("small", 0)]
    group_b = [("gdn_w_in", 0), ("gdn_w_out", 0)]
    group_c = [("mla_w_in", 0), ("mla_w_uq", 0), ("mla_w_ukv", 0), ("mla_w_out", 0), ("pool_w_in", 1), ("pool_w_grp", 1), ("pool_w_out", 1)]
    full = {}
    h_a, t_a = gather_start(group_a, x0, "gather_a")
    got_a = finish(h_a, t_a, "gather_a")
    full.update({k: a for k, a in zip(group_a, got_a)})
    small = {n: _unshard(a, SMALL_AXIS[n]) for n, a in zip(SMALL_SHARDED, _unpack(full[("small", 0)], small_shapes))}
    h_b, t_b = gather_start(group_b, got_a[0], "gather_b")
    W0 = layer_weights(full, small, rep, 0)
    x1, s0 = pool_fwd(x0, dict(W0, ng=tied(W0["ng"], t_b)), "l0")
    got_b = finish(h_b, x1, "gather_b")
    full.update({k: a for k, a in zip(group_b, got_b)})
    h_c, t_c = gather_start(group_c, got_b[0], "gather_c")
    W1 = layer_weights(full, small, rep, 1)
    x2, s1 = gdn_fwd(x1, dict(W1, ng=tied(W1["ng"], t_c)), "l1")
    got_c = finish(h_c, x2, "gather_c")
    full.update({k: a for k, a in zip(group_c, got_c)})
    W2, W3 = layer_weights(full, small, rep, 2), layer_weights(full, small, rep, 3)
    x3, s2 = mla_fwd(x2, pos, W2, "l2")
    x4, s3 = pool_fwd(x3, W3, "l3")
    loss_part, dx4, dfinal = loss_head(x4, target, final_g.reshape(1, D), "loss_head")

    def scatter_start(pieces_of, after, tag):
        keys = list(pieces_of)
        pieces = [scatter_piece(i, i, BIG_LAYOUT[n][0], BIG_LAYOUT[n][1]) for i, (n, l) in enumerate(keys)]
        shapes = [_sds((4,) + tuple(w[n].shape[1:]), BF16) for n, l in keys]
        sems, ins, lands, token = exchange_start(pieces, [pieces_of[k] for k in keys], shapes, after, tag + "_start")
        return keys, (pieces, sems, ins, lands), token

    dx3, g3 = pool_bwd(dx4, W3, s3, "l3")
    k3, h3, t3 = scatter_start(big_grad_pieces((None, None, None, g3)), dx3, "scatter_l3")
    dx2, g2 = mla_bwd(tied(dx3, t3), pos, W2, s2, "l2")
    k2, h2, t2 = scatter_start(big_grad_pieces((None, None, g2, None)), dx2, "scatter_l2")
    dx1, g1 = gdn_bwd(tied(dx2, t2), W1, s1, "l1")
    k1, h1, t1 = scatter_start(big_grad_pieces((None, g1, None, None)), dx1, "scatter_l1")
    dx0, g0 = pool_bwd(tied(dx1, t1), W0, s0, "l0")
    k0, h0, t0 = scatter_start(big_grad_pieces((g0, None, None, None)), dx0, "scatter_l0")
    recv = {}
    for keys, handle, tag in ((k3, h3, "scatter_l3"), (k2, h2, "scatter_l2"), (k1, h1, "scatter_l1"), (k0, h0, "scatter_l0")):
        recv.update(zip(keys, finish(handle, t0, tag)))
    keys = list(recv)
    sib = dict(zip(keys, swap_cores([recv[k] for k in keys], "swap_cores")))

    sg = small_grads((g0, g1, g2, g3), dfinal)
    small_names = SMALL_SHARDED + REPLICATED
    small_buf = _pack([sg[n] for n in small_names] + [loss_part], F32, 8)
    small_sum = sum_slots(None, exchange_all(small_buf, "gather_small"), jnp.full((1,), -1, I32), "sum_small")
    full_small = _unpack(small_sum, [sg[n].shape for n in small_names] + [(1, LANES)])
    loss = full_small[-1][0, 0]
    small_part = {}
    for n, a in zip(small_names, full_small[:-1]):
        if n in SMALL_AXIS:
            a = lax.dynamic_index_in_dim(_to_shards(a, SMALL_AXIS[n]), my_chip, axis=0, keepdims=False)
        small_part[n] = a

    outs = []
    for n in NAMES:
        shp = w[n].shape
        two = (int(np.prod(shp[:-1])), shp[-1]) if len(shp) > 1 else (1, shp[0])
        if n in BIG_LAYOUT:
            layers = shp[0]
            rows = lambda a: a.reshape(4, two[0] // layers, two[1])
            parts = [[rows(recv[(n, l)]) for l in range(layers)], [rows(sib[(n, l)]) for l in range(layers)]]
        else:
            parts = [[small_part[n].reshape((1,) + two)]]
        res = adamw(w[n].reshape(two), parts, m[n].reshape(two), v[n].reshape(two), "adamw_" + n)
        outs.append([r.reshape(shp) for r in res])
    return (loss, dx0[None], *[o[0] for o in outs], *[o[1] for o in outs], *[o[2] for o in outs], *[o[3] for o in outs])
```

```python
import functools
import math

import jax
import jax.numpy as jnp
import numpy as np
from jax import lax
from jax.experimental import pallas as pl
from jax.experimental.pallas import tpu as pltpu

F32 = jnp.float32
BF16 = jnp.bfloat16
I32 = jnp.int32

D = 1024
EPS = 1e-6
POOL_WIDTH = 2048
POOL_GROUP = 512
GDN_H, GDN_DK, GDN_DV, GDN_C = 8, 128, 256, 64
GDN_QK, GDN_V, GDN_CONV_CH, GDN_IN = 1024, 2048, 4096, 6160
GDN_IN_PAD = 6272
MLA_H, MLA_NOPE, MLA_ROPE, MLA_V = 16, 128, 64, 128
MLA_Q_LORA, MLA_KV_LORA, MLA_WIDTH, MLA_IN = 768, 512, 2048, 3392
MLA_IN_PAD = 4096
MLA_SCALE = (MLA_NOPE + MLA_ROPE) ** -0.5
ROPE_THETA = 10000.0
ADAM_LR, ADAM_B1, ADAM_B2, ADAM_EPS, ADAM_WD, ADAM_STEP = 0.001, 0.9, 0.999, 1e-08, 0.01, 10

VMEM_LIMIT_V7X = 56 * 1024 * 1024
LANES = 128
MESH = pl.DeviceIdType.MESH


def _pc(body, **kw):
    return pl.pallas_call(body, **kw)


def _cparams(sem):
    return pltpu.CompilerParams(dimension_semantics=sem, vmem_limit_bytes=VMEM_LIMIT_V7X)


def _tile(n, cap):
    t = (cap // LANES) * LANES
    while t >= LANES:
        if n % t == 0:
            return t
        t -= LANES
    return n


def _sds(shape, dt):
    return jax.ShapeDtypeStruct(shape, dt)


def mm(a, b, *, ta=False, tb=False, add=None, out_dtype=F32, name):
    if ta:
        K, M = a.shape
    else:
        M, K = a.shape
    if tb:
        N, K2 = b.shape
    else:
        K2, N = b.shape
    assert K == K2, (a.shape, b.shape, ta, tb)
    tm, tn, tk = _tile(M, 1024), _tile(N, 1024), _tile(K, 1024)
    nk = K // tk
    a_spec = pl.BlockSpec((tk, tm), lambda i, j, k: (k, i)) if ta else pl.BlockSpec((tm, tk), lambda i, j, k: (i, k))
    b_spec = pl.BlockSpec((tn, tk), lambda i, j, k: (j, k)) if tb else pl.BlockSpec((tk, tn), lambda i, j, k: (k, j))
    o_spec = pl.BlockSpec((tm, tn), lambda i, j, k: (i, j))
    dn = (((0 if ta else 1,), (1 if tb else 0,)), ((), ()))
    has_add = add is not None

    def body(*refs):
        a_ref, b_ref = refs[0], refs[1]
        o_ref, acc = refs[-2], refs[-1]
        k = pl.program_id(2)

        @pl.when(k == 0)
        def _():
            acc[...] = jnp.zeros_like(acc)

        acc[...] += lax.dot_general(a_ref[...].astype(BF16), b_ref[...].astype(BF16), dn, preferred_element_type=F32)

        @pl.when(k == nk - 1)
        def _():
            r = acc[...]
            if has_add:
                r = r + refs[2][...]
            o_ref[...] = r.astype(out_dtype)

    ins = [a, b] + ([add] if has_add else [])
    specs = [a_spec, b_spec] + ([o_spec] if has_add else [])
    return _pc(body, grid=(M // tm, N // tn, nk), in_specs=specs, out_specs=o_spec, out_shape=_sds((M, N), out_dtype),
               scratch_shapes=[pltpu.VMEM((tm, tn), F32)], compiler_params=_cparams(("parallel", "parallel", "arbitrary")),
               name=name)(*ins)


def gmm(kind, a, b, *, G, name, out_dtype=F32):
    S_ = a.shape[0]
    Ka = a.shape[1] // G
    if kind == "tn":
        N = b.shape[1] // G
        tk = _tile(S_, 512)
        nk = S_ // tk

        def body(a_ref, b_ref, o_ref, acc):
            k = pl.program_id(1)

            @pl.when(k == 0)
            def _():
                acc[...] = jnp.zeros_like(acc)

            acc[...] += lax.dot_general(a_ref[...].astype(BF16), b_ref[...].astype(BF16), (((0,), (0,)), ((), ())),
                                        preferred_element_type=F32)

            @pl.when(k == nk - 1)
            def _():
                o_ref[...] = acc[...].astype(out_dtype)

        return _pc(body, grid=(G, nk),
                   in_specs=[pl.BlockSpec((tk, Ka), lambda g, k: (k, g)), pl.BlockSpec((tk, N), lambda g, k: (k, g))],
                   out_specs=pl.BlockSpec((None, Ka, N), lambda g, k: (g, 0, 0)), out_shape=_sds((G, Ka, N), out_dtype),
                   scratch_shapes=[pltpu.VMEM((Ka, N), F32)], compiler_params=_cparams(("parallel", "arbitrary")), name=name)(a, b)
    N = b.shape[2] if kind == "nn" else b.shape[1]
    tm = _tile(S_, 1024)
    dn = (((1,), (0 if kind == "nn" else 1,)), ((), ()))

    def body(a_ref, b_ref, o_ref):
        o_ref[...] = lax.dot_general(a_ref[...].astype(BF16), b_ref[...].astype(BF16), dn, preferred_element_type=F32)

    bshape = (None,) + tuple(b.shape[1:])
    return _pc(body, grid=(G, S_ // tm),
               in_specs=[pl.BlockSpec((tm, Ka), lambda g, i: (i, g)), pl.BlockSpec(bshape, lambda g, i: (g, 0, 0))],
               out_specs=pl.BlockSpec((tm, N), lambda g, i: (i, g)), out_shape=_sds((S_, G * N), F32),
               compiler_params=_cparams(("parallel", "parallel")), name=name)(a, b)


def _rw_spec(ts, w, c, s):
    return pl.BlockSpec((ts, w), lambda j, i: (i, c + j * s))


def _rw_pspec(p, w, c, s):
    return pl.BlockSpec((p.shape[0], w), lambda j, i: (0, c + j * s))


def rowwise(f, tiles, params, outs, *, ncol=1, ts, name):
    S_ = tiles[0][0].shape[0]
    nin = len(tiles) + len(params)

    def body(*refs):
        res = f(pl.program_id(0), *[r[...] for r in refs[:nin]])
        for r, o in zip(refs[nin:], res):
            r[...] = o.astype(r.dtype)

    return _pc(body, grid=(ncol, S_ // ts),
               in_specs=[_rw_spec(ts, w, c, s) for (_, w, c, s) in tiles] + [_rw_pspec(*p) for p in params],
               out_specs=[_rw_spec(ts, w, 0, s) for (w, s, _) in outs],
               out_shape=[_sds((S_, w * (ncol if s else 1)), dt) for (w, s, dt) in outs],
               compiler_params=_cparams(("parallel", "parallel")), name=name)(*[t[0] for t in tiles], *[p[0] for p in params])


def rowwise_bwd(f, tiles, params, cots, *, need, adds=None, place=None, ncol=1, ts, name):
    S_ = tiles[0][0].shape[0]
    adds = adds or {}
    place = place or {}
    nt, npar, nc = len(tiles), len(params), len(cots)
    add_keys = sorted(adds)
    need_idx = [k for k in range(nt) if need[k]]
    into_keys = [k for k in need_idx if k in place and not isinstance(place[k][0], int)]
    n_extra = len(add_keys) + len(into_keys)

    def body(*refs):
        j, i = pl.program_id(0), pl.program_id(1)
        vals = [r[...] for r in refs[:nt + npar]]
        cvals = tuple(r[...] for r in refs[nt + npar:nt + npar + nc])
        add_refs = refs[nt + npar + nc:nt + npar + nc + len(add_keys)]
        out_refs = refs[nt + npar + nc + n_extra:]
        _, vjp = jax.vjp(lambda *v: tuple(f(j, *v)), *vals)
        grads = vjp(cvals)
        for n, k in enumerate(need_idx):
            g = grads[k]
            if k in adds:
                g = g + add_refs[add_keys.index(k)][...]
            out_refs[n][...] = g
        for n in range(npar):
            ref = out_refs[len(need_idx) + n]
            first = (i == 0) if params[n][3] else jnp.logical_and(i == 0, j == 0)

            @pl.when(first)
            def _():
                ref[...] = jnp.zeros_like(ref)

            ref[...] += grads[nt + n]

    in_specs = ([_rw_spec(ts, w, c, s) for (_, w, c, s) in tiles] + [_rw_pspec(*p) for p in params]
                + [_rw_spec(ts, w, c, s) for (_, w, c, s) in cots] + [_rw_spec(ts, *adds[k][1:]) for k in add_keys]
                + [pl.BlockSpec(memory_space=pl.ANY) for _ in into_keys])
    out_specs, out_shape, aliases = [], [], {}
    for n, k in enumerate(need_idx):
        w, s = tiles[k][1], tiles[k][3]
        if k in place:
            dst, c0 = place[k]
            total = dst if isinstance(dst, int) else dst.shape[1]
            out_specs.append(_rw_spec(ts, w, c0, s))
            out_shape.append(_sds((S_, total), F32))
            if k in into_keys:
                aliases[nt + npar + nc + len(add_keys) + into_keys.index(k)] = n
        else:
            out_specs.append(_rw_spec(ts, w, 0, s))
            out_shape.append(_sds((S_, w * (ncol if s else 1)), F32))
    out_specs += [_rw_pspec(p[0], p[1], p[2], p[3]) for p in params]
    out_shape += [_sds(p[0].shape, F32) for p in params]
    res = _pc(body, grid=(ncol, S_ // ts), in_specs=in_specs, out_specs=out_specs, out_shape=out_shape,
              input_output_aliases=aliases, compiler_params=_cparams(("arbitrary", "arbitrary")), name=name)(
        *[t[0] for t in tiles], *[p[0] for p in params], *[c[0] for c in cots], *[adds[k][0] for k in add_keys],
        *[place[k][0] for k in into_keys])
    return list(res[:len(need_idx)]), list(res[len(need_idx):])


def _rms(x, g):
    r = lax.rsqrt(jnp.mean(x * x, axis=-1, keepdims=True) + EPS)
    return x * r * g


def _silu(x):
    return x * jax.nn.sigmoid(x)


@jax.custom_vjp
def _softplus(x):
    return jnp.maximum(x, 0.0) + jnp.log1p(jnp.exp(-jnp.abs(x)))


_softplus.defvjp(lambda x: (_softplus(x), x), lambda x, d: (d * jax.nn.sigmoid(x),))


def f_rms(j, x, g):
    return (_rms(x, g),)


def f_pool_gate(j, pg, gate, scale):
    return (pg * scale * _silu(gate),)


def f_ogate(j, o, gate):
    return (o * _silu(gate),)


def f_gdn_out(j, o, gate, g):
    return (_rms(o, g) * _silu(gate),)


def f_gdn_gates(j, ba, alog, dtb):
    lane = lax.broadcasted_iota(I32, (1, LANES), 1)
    gs, bs = [], []
    for h in range(GDN_H):
        eb = (lane == h).astype(F32)
        ea = (lane == GDN_H + h).astype(F32)
        b = jnp.sum(ba * eb, -1, keepdims=True)
        a = jnp.sum(ba * ea, -1, keepdims=True)
        al = jnp.sum(alog * eb, -1, keepdims=True)
        dt = jnp.sum(dtb * eb, -1, keepdims=True)
        g = -jnp.exp(al) * _softplus(a + dt)
        gs.append(jnp.broadcast_to(g, ba.shape))
        bs.append(jnp.broadcast_to(jax.nn.sigmoid(b), ba.shape))
    return jnp.concatenate(gs, 1), jnp.concatenate(bs, 1)


def _shift_dn(x, k):
    rows = lax.broadcasted_iota(I32, x.shape, 0)
    return jnp.where(rows < k, 0.0, pltpu.roll(x, k, 0))


def _shift_up(x, k):
    n = x.shape[0]
    rows = lax.broadcasted_iota(I32, x.shape, 0)
    return jnp.where(rows >= n - k, 0.0, pltpu.roll(x, n - k, 0))


def _pool_window(j):
    g = lax.div(j, POOL_GROUP // LANES)
    return jnp.where(g == 0, 2.0, jnp.where(g == 1, 4.0, jnp.where(g == 2, 8.0, 16.0))), g


def _pick(g, a2, a4, a8, a16):
    return jnp.where(g == 0, a2, jnp.where(g == 1, a4, jnp.where(g == 2, a8, a16)))


def pool_time_fwd(proj, name):
    S_ = proj.shape[0]

    def body(u_ref, p_ref):
        u = u_ref[...]
        w, g = _pool_window(pl.program_id(0))
        s2 = u + _shift_dn(u, 1)
        s4 = s2 + _shift_dn(s2, 2)
        s8 = s4 + _shift_dn(s4, 4)
        s16 = s8 + _shift_dn(s8, 8)
        t1 = (lax.broadcasted_iota(I32, u.shape, 0) + 1).astype(F32)
        p_ref[...] = (_pick(g, s2, s4, s8, s16) / jnp.minimum(t1, w) - u).astype(p_ref.dtype)

    return _pc(body, grid=(POOL_WIDTH // LANES,), in_specs=[pl.BlockSpec((S_, LANES), lambda j: (0, j))],
               out_specs=pl.BlockSpec((S_, LANES), lambda j: (0, j)), out_shape=_sds((S_, POOL_WIDTH), BF16),
               compiler_params=_cparams(("parallel",)), name=name)(proj)


def pool_time_bwd(dp, into, name):
    S_ = dp.shape[0]

    def body(dp_ref, _, du_ref):
        d = dp_ref[...]
        w, g = _pool_window(pl.program_id(0))
        t1 = (lax.broadcasted_iota(I32, d.shape, 0) + 1).astype(F32)
        q = d / jnp.minimum(t1, w)
        r2 = q + _shift_up(q, 1)
        r4 = r2 + _shift_up(r2, 2)
        r8 = r4 + _shift_up(r4, 4)
        r16 = r8 + _shift_up(r8, 8)
        du_ref[...] = _pick(g, r2, r4, r8, r16) - d

    return _pc(body, grid=(POOL_WIDTH // LANES,),
               in_specs=[pl.BlockSpec((S_, LANES), lambda j: (0, j)), pl.BlockSpec(memory_space=pl.ANY)],
               out_specs=pl.BlockSpec((S_, LANES), lambda j: (0, j)), out_shape=_sds(into.shape, F32),
               input_output_aliases={1: 0}, compiler_params=_cparams(("parallel",)), name=name)(dp, into)


def _conv_post(j, a):
    n = a * lax.rsqrt(jnp.sum(a * a, axis=-1, keepdims=True) + EPS)
    nq = GDN_QK // LANES
    return jnp.where(j < nq, n * (GDN_DK ** -0.5), jnp.where(j < 2 * nq, n, a))


def _conv_pre(u, w):
    return w[3:4] * u + w[2:3] * _shift_dn(u, 1) + w[1:2] * _shift_dn(u, 2) + w[0:1] * _shift_dn(u, 3)


def gdn_conv_fwd(proj, conv_w, name):
    S_ = proj.shape[0]

    def body(u_ref, w_ref, o_ref):
        o_ref[...] = _conv_post(pl.program_id(0), _silu(_conv_pre(u_ref[...], w_ref[...])))

    return _pc(body, grid=(GDN_CONV_CH // LANES,),
               in_specs=[pl.BlockSpec((S_, LANES), lambda j: (0, j)), pl.BlockSpec((8, LANES), lambda j: (0, j))],
               out_specs=pl.BlockSpec((S_, LANES), lambda j: (0, j)), out_shape=_sds((S_, GDN_CONV_CH), F32),
               compiler_params=_cparams(("parallel",)), name=name)(proj, conv_w)


def gdn_conv_bwd(proj, conv_w, dq, dk, dv, into, name):
    S_ = proj.shape[0]
    nq = GDN_QK // LANES

    def body(u_ref, w_ref, dq_ref, dk_ref, dv_ref, _, du_ref, dw_ref):
        j = pl.program_id(0)
        u, w = u_ref[...], w_ref[...]
        c = _conv_pre(u, w)
        sig = jax.nn.sigmoid(c)
        dout = jnp.where(j < nq, dq_ref[...], jnp.where(j < 2 * nq, dk_ref[...], dv_ref[...]))
        _, vjp = jax.vjp(lambda a: _conv_post(j, a), c * sig)
        dc = vjp(dout)[0] * (sig * (1.0 + c * (1.0 - sig)))
        du_ref[...] = w[3:4] * dc + w[2:3] * _shift_up(dc, 1) + w[1:2] * _shift_up(dc, 2) + w[0:1] * _shift_up(dc, 3)
        rows = lax.broadcasted_iota(I32, (8, LANES), 0)
        dw = jnp.zeros((8, LANES), F32)
        for k in range(4):
            us = u if k == 3 else _shift_dn(u, 3 - k)
            dw = dw + jnp.where(rows == k, jnp.sum(dc * us, axis=0, keepdims=True), 0.0)
        dw_ref[...] = dw

    blk = lambda f: pl.BlockSpec((S_, LANES), f)
    return _pc(body, grid=(GDN_CONV_CH // LANES,),
               in_specs=[blk(lambda j: (0, j)), pl.BlockSpec((8, LANES), lambda j: (0, j)),
                         blk(lambda j: (0, jnp.minimum(j, nq - 1))), blk(lambda j: (0, jnp.clip(j - nq, 0, nq - 1))),
                         blk(lambda j: (0, jnp.clip(j - 2 * nq, 0, 2 * nq - 1))), pl.BlockSpec(memory_space=pl.ANY)],
               out_specs=[blk(lambda j: (0, j)), pl.BlockSpec((8, LANES), lambda j: (0, j))],
               out_shape=[_sds(into.shape, F32), _sds((8, GDN_CONV_CH), F32)], input_output_aliases={5: 0},
               compiler_params=_cparams(("parallel",)), name=name)(proj, conv_w, dq, dk, dv, into)


_NN, _NT, _TN = ((1,), (0,)), ((1,), (1,)), ((0,), (0,))


def _split(x, n):
    parts = []
    for _ in range(n):
        h = x.astype(BF16)
        parts.append(h)
        x = x - h.astype(F32)
    return parts


def _dot(a, b, dn, mode):
    d = lambda p, q: lax.dot_general(p, q, (dn, ((), ())), preferred_element_type=F32)
    if mode == "lo":
        return d(a.astype(BF16), b.astype(BF16))
    if mode == "x3":
        (ah, al), (bh, bl) = _split(a, 2), _split(b, 2)
        return d(ah, bh) + (d(ah, bl) + d(al, bh))
    b0, b1, b2 = _split(b, 3)
    ab = a.astype(BF16)
    return d(ab, b0) + (d(ab, b1) + d(ab, b2))


def _make_dots(mode):
    @jax.custom_vjp
    def nn(a, b):
        return _dot(a, b, _NN, mode)

    @jax.custom_vjp
    def nt(a, b):
        return _dot(a, b, _NT, mode)

    @jax.custom_vjp
    def tn(a, b):
        return _dot(a, b, _TN, mode)

    nn.defvjp(lambda a, b: (nn(a, b), (a, b)), lambda r, d: (nt(d, r[1]), tn(r[0], d)))
    nt.defvjp(lambda a, b: (nt(a, b), (a, b)), lambda r, d: (nn(d, r[1]), tn(d, r[0])))
    tn.defvjp(lambda a, b: (tn(a, b), (a, b)), lambda r, d: (nt(r[1], d), nn(r[0], d)))
    return nn, nt, tn


_nn_hi, _nt_hi, _tn_hi = _make_dots("x3")
_nn_lo, _nt_lo, _tn_lo = _make_dots("lo")


@jax.custom_vjp
def _nn_const(a, b):
    return _dot(a, b, _NN, "xl")


_nn_const.defvjp(lambda a, b: (_nn_const(a, b), a), lambda a, d: (jnp.zeros_like(a), _dot(a, d, _TN, "xl")))


def _each(f, *lists):
    return [f(*xs) for xs in zip(*lists)]


def _gdn_chunk(q, k, v, gb, bb, state):
    C = GDN_C
    e0 = (lax.broadcasted_iota(I32, (1, LANES), 1) == 0).astype(F32)
    ri = lax.broadcasted_iota(I32, (C, C), 0)
    ci = lax.broadcasted_iota(I32, (C, C), 1)
    causal, strict = ri >= ci, ri > ci
    tri, eye, ones = causal.astype(F32), (ri == ci).astype(F32), jnp.ones((C, C), F32)
    last = lax.broadcasted_iota(I32, (C, LANES), 0) == C - 1
    g1 = _each(lambda a: jnp.sum(a * e0, -1, keepdims=True), gb)
    b1 = _each(lambda a: jnp.sum(a * e0, -1, keepdims=True), bb)
    gc_c = _each(lambda g: _nn_const(tri, jnp.broadcast_to(g, (C, C))), g1)
    gc_d = _each(lambda g: _nn_const(tri, jnp.broadcast_to(g, (C, LANES))), g1)
    gr_c = _each(lambda g: _nn_const(ones, eye * g), gc_c)
    decay = _each(lambda a, r: jnp.where(causal, jnp.exp(jnp.where(causal, a - r, 0.0)), 0.0), gc_c, gr_c)
    kb = _each(lambda a, b: a * b, k, b1)
    vb = _each(lambda a, b: a * b, v, b1)
    x = _each(lambda a, b, d: -jnp.where(strict, _nt_lo(a, b) * d, 0.0), kb, k, decay)
    ainv, p = _each(lambda a: eye + a, x), x
    for _ in range(5):
        p = _each(lambda a: _nn_hi(a, a), p)
        ainv = _each(lambda a, b: a + _nn_hi(a, b), ainv, p)
    u = _each(_nn_hi, ainv, vb)
    w = _each(lambda a, b, g: _nn_hi(a, b * jnp.exp(g)), ainv, kb, gc_d)
    attn = _each(lambda a, b, d: jnp.where(causal, _nt_lo(a, b) * d, 0.0), q, k, decay)
    v_new = _each(lambda a, b, s: a - _nn_lo(b, s), u, w, state)
    o = _each(lambda a, g, s, t, vn: _nn_lo(a * jnp.exp(g), s) + _nn_lo(t, vn), q, gc_d, state, attn, v_new)
    gl = _each(lambda g: jnp.sum(jnp.where(last, g, 0.0), axis=0, keepdims=True), gc_d)
    new_state = _each(lambda s, g, a, gd, vn: s * jnp.exp(jnp.sum(g * e0, -1, keepdims=True)) + _tn_lo(a * jnp.exp(g - gd), vn),
                      state, gl, k, gc_d, v_new)
    return o, new_state


def _head_slices(ref, width):
    return [ref[:, h * width:(h + 1) * width] for h in range(GDN_H)]


def gdn_chunk_fwd(qkv, g_b, beta_b, name):
    S_ = qkv.shape[0]
    N = S_ // GDN_C

    def body(q_ref, k_ref, v_ref, g_ref, b_ref, o_ref, s_ref, state):
        @pl.when(pl.program_id(0) == 0)
        def _():
            state[...] = jnp.zeros_like(state)

        st = [state[h] for h in range(GDN_H)]
        s_ref[0] = state[...]
        o, st2 = _gdn_chunk(_head_slices(q_ref, GDN_DK), _head_slices(k_ref, GDN_DK), _head_slices(v_ref, GDN_DV),
                            _head_slices(g_ref, GDN_DK), _head_slices(b_ref, GDN_DK), st)
        for h in range(GDN_H):
            o_ref[:, h * GDN_DV:(h + 1) * GDN_DV] = o[h]
            state[h] = st2[h]

    return _pc(body, grid=(N,),
               in_specs=[pl.BlockSpec((GDN_C, GDN_QK), lambda n: (n, 0)), pl.BlockSpec((GDN_C, GDN_QK), lambda n: (n, 1)),
                         pl.BlockSpec((GDN_C, GDN_V), lambda n: (n, 1)), pl.BlockSpec((GDN_C, GDN_QK), lambda n: (n, 0)),
                         pl.BlockSpec((GDN_C, GDN_QK), lambda n: (n, 0))],
               out_specs=[pl.BlockSpec((GDN_C, GDN_V), lambda n: (n, 0)),
                          pl.BlockSpec((1, GDN_H, GDN_DK, GDN_DV), lambda n: (n, 0, 0, 0))],
               out_shape=[_sds((S_, GDN_V), F32), _sds((N, GDN_H, GDN_DK, GDN_DV), F32)],
               scratch_shapes=[pltpu.VMEM((GDN_H, GDN_DK, GDN_DV), F32)],
               compiler_params=_cparams(("arbitrary",)), name=name)(qkv, qkv, qkv, g_b, beta_b)


def gdn_chunk_bwd(qkv, g_b, beta_b, states, do, name):
    S_ = qkv.shape[0]
    N = S_ // GDN_C

    def body(q_ref, k_ref, v_ref, g_ref, b_ref, s_ref, do_ref, dq_ref, dk_ref, dv_ref, dg_ref, db_ref, dstate):
        @pl.when(pl.program_id(0) == 0)
        def _():
            dstate[...] = jnp.zeros_like(dstate)

        _, vjp = jax.vjp(_gdn_chunk, _head_slices(q_ref, GDN_DK), _head_slices(k_ref, GDN_DK), _head_slices(v_ref, GDN_DV),
                         _head_slices(g_ref, GDN_DK), _head_slices(b_ref, GDN_DK), [s_ref[0, h] for h in range(GDN_H)])
        dq, dk, dv, dg, db, ds = vjp((_head_slices(do_ref, GDN_DV), [dstate[h] for h in range(GDN_H)]))
        for h in range(GDN_H):
            kk, vv = slice(h * GDN_DK, (h + 1) * GDN_DK), slice(h * GDN_DV, (h + 1) * GDN_DV)
            dq_ref[:, kk] = dq[h]
            dk_ref[:, kk] = dk[h]
            dv_ref[:, vv] = dv[h]
            dg_ref[:, kk] = dg[h]
            db_ref[:, kk] = db[h]
            dstate[h] = ds[h]

    r = lambda n: N - 1 - n
    qk = lambda c: pl.BlockSpec((GDN_C, GDN_QK), lambda n: (r(n), c))
    vs = lambda c: pl.BlockSpec((GDN_C, GDN_V), lambda n: (r(n), c))
    return _pc(body, grid=(N,),
               in_specs=[qk(0), qk(1), vs(1), qk(0), qk(0),
                         pl.BlockSpec((1, GDN_H, GDN_DK, GDN_DV), lambda n: (r(n), 0, 0, 0)), vs(0)],
               out_specs=[qk(0), qk(0), vs(0), qk(0), qk(0)],
               out_shape=[_sds((S_, GDN_QK), F32), _sds((S_, GDN_QK), F32), _sds((S_, GDN_V), F32),
                          _sds((S_, GDN_QK), F32), _sds((S_, GDN_QK), F32)],
               scratch_shapes=[pltpu.VMEM((GDN_H, GDN_DK, GDN_DV), F32)],
               compiler_params=_cparams(("arbitrary",)), name=name)(qkv, qkv, qkv, g_b, beta_b, states, do)


def _rope_tables(pos_ref, inv_ref, cm_ref, sg_ref):
    ang = pos_ref[...] * inv_ref[...]
    return jnp.cos(ang) * cm_ref[...], jnp.sin(ang) * sg_ref[...]


def mla_prep_fwd(qpad, kv, proj, pos, rope_consts, name):
    S_ = qpad.shape[0]
    ts = 256
    W = 2 * LANES

    def body(q_ref, kv_ref, kr_ref, pos_ref, inv_ref, cm_ref, sg_ref, qh_ref, kh_ref, vh_ref):
        cs, sn = _rope_tables(pos_ref, inv_ref, cm_ref, sg_ref)
        rope = lambda r: r * cs + pltpu.roll(r, LANES // 2, 1) * sn
        krr = rope(kr_ref[...]).astype(BF16)
        for h in range(MLA_H):
            qh_ref[h, :, 0:LANES] = (q_ref[:, h * W:h * W + LANES] * MLA_SCALE).astype(BF16)
            qh_ref[h, :, LANES:W] = (rope(q_ref[:, h * W + LANES:(h + 1) * W]) * MLA_SCALE).astype(BF16)
            kh_ref[h, :, 0:LANES] = kv_ref[:, h * W:h * W + LANES].astype(BF16)
            kh_ref[h, :, LANES:W] = krr
            vh_ref[h] = kv_ref[:, h * W + LANES:(h + 1) * W].astype(BF16)

    one = pl.BlockSpec((1, LANES), lambda i: (0, 0))
    return _pc(body, grid=(S_ // ts,),
               in_specs=[pl.BlockSpec((ts, MLA_H * W), lambda i: (i, 0)), pl.BlockSpec((ts, MLA_H * W), lambda i: (i, 0)),
                         pl.BlockSpec((ts, LANES), lambda i: (i, 1536 // LANES)), pl.BlockSpec((ts, 1), lambda i: (i, 0)),
                         one, one, one],
               out_specs=[pl.BlockSpec((MLA_H, ts, W), lambda i: (0, i, 0)), pl.BlockSpec((MLA_H, ts, W), lambda i: (0, i, 0)),
                          pl.BlockSpec((MLA_H, ts, LANES), lambda i: (0, i, 0))],
               out_shape=[_sds((MLA_H, S_, W), BF16), _sds((MLA_H, S_, W), BF16), _sds((MLA_H, S_, LANES), BF16)],
               compiler_params=_cparams(("parallel",)), name=name)(qpad, kv, proj, pos, *rope_consts)


def mla_prep_bwd(dqh, dkh, dvh, pos, rope_consts, into, name):
    S_ = dqh.shape[1]
    ts = 256
    W = 2 * LANES

    def body(dq_ref, dk_ref, dv_ref, pos_ref, inv_ref, cm_ref, sg_ref, _, dqp_ref, dkv_ref, dkr_ref):
        cs, sn = _rope_tables(pos_ref, inv_ref, cm_ref, sg_ref)
        rope_t = lambda g: g * cs + pltpu.roll(g * sn, LANES // 2, 1)
        acc = jnp.zeros((ts, LANES), F32)
        for h in range(MLA_H):
            dqp_ref[:, h * W:h * W + LANES] = dq_ref[h, :, 0:LANES] * MLA_SCALE
            dqp_ref[:, h * W + LANES:(h + 1) * W] = rope_t(dq_ref[h, :, LANES:W]) * MLA_SCALE
            dkv_ref[:, h * W:h * W + LANES] = dk_ref[h, :, 0:LANES]
            dkv_ref[:, h * W + LANES:(h + 1) * W] = dv_ref[h]
            acc = acc + dk_ref[h, :, LANES:W]
        dkr_ref[...] = rope_t(acc)

    one = pl.BlockSpec((1, LANES), lambda i: (0, 0))
    return _pc(body, grid=(S_ // ts,),
               in_specs=[pl.BlockSpec((MLA_H, ts, W), lambda i: (0, i, 0)), pl.BlockSpec((MLA_H, ts, W), lambda i: (0, i, 0)),
                         pl.BlockSpec((MLA_H, ts, LANES), lambda i: (0, i, 0)), pl.BlockSpec((ts, 1), lambda i: (i, 0)),
                         one, one, one, pl.BlockSpec(memory_space=pl.ANY)],
               out_specs=[pl.BlockSpec((ts, MLA_H * W), lambda i: (i, 0)), pl.BlockSpec((ts, MLA_H * W), lambda i: (i, 0)),
                          pl.BlockSpec((ts, LANES), lambda i: (i, 1536 // LANES))],
               out_shape=[_sds((S_, MLA_H * W), F32), _sds((S_, MLA_H * W), F32), _sds(into.shape, F32)],
               input_output_aliases={7: 2}, compiler_params=_cparams(("parallel",)), name=name)(dqh, dkh, dvh, pos, *rope_consts, into)


NEG = -1e30


FLASH_TILE = 1024


def _scores(q, k, diagonal):
    s = lax.dot_general(q, k, (_NT, ((), ())), preferred_element_type=F32)
    if not diagonal:
        return s
    t = s.shape[0]
    return jnp.where(lax.broadcasted_iota(I32, (t, t), 1) <= lax.broadcasted_iota(I32, (t, t), 0), s, NEG)


def flash_fwd(qh, kh, vh, name):
    H, S_, W = qh.shape
    t = _tile(S_, FLASH_TILE)
    n = S_ // t

    def body(q_ref, k_ref, v_ref, o_ref, lse_ref, m_s, l_s, acc):
        qi, kj = pl.program_id(1), pl.program_id(2)

        @pl.when(kj == 0)
        def _():
            m_s[...] = jnp.full_like(m_s, NEG)
            l_s[...] = jnp.zeros_like(l_s)
            acc[...] = jnp.zeros_like(acc)

        def step(diagonal):
            s = _scores(q_ref[...], k_ref[...], diagonal)
            m_old = m_s[...]
            m_new = jnp.maximum(m_old, jnp.max(s, axis=-1, keepdims=True))
            alpha = jnp.exp(m_old - m_new)
            p = jnp.exp(s - m_new[:, :1])
            l_s[...] = alpha * l_s[...] + jnp.sum(p, axis=-1, keepdims=True)
            acc[...] = alpha * acc[...] + lax.dot_general(p.astype(BF16), v_ref[...], (_NN, ((), ())), preferred_element_type=F32)
            m_s[...] = m_new

        pl.when(kj < qi)(lambda: step(False))
        pl.when(kj == qi)(lambda: step(True))

        @pl.when(kj == n - 1)
        def _():
            o_ref[...] = acc[...] / l_s[...]
            lse_ref[...] = m_s[...] + jnp.log(l_s[...])

    return _pc(body, grid=(H, n, n),
               in_specs=[pl.BlockSpec((None, t, W), lambda h, i, j: (h, i, 0)),
                         pl.BlockSpec((None, t, W), lambda h, i, j: (h, jnp.minimum(i, j), 0)),
                         pl.BlockSpec((None, t, LANES), lambda h, i, j: (h, jnp.minimum(i, j), 0))],
               out_specs=[pl.BlockSpec((t, LANES), lambda h, i, j: (i, h)), pl.BlockSpec((None, t, LANES), lambda h, i, j: (h, i, 0))],
               out_shape=[_sds((S_, H * LANES), F32), _sds((H, S_, LANES), F32)],
               scratch_shapes=[pltpu.VMEM((t, LANES), F32)] * 3,
               compiler_params=_cparams(("parallel", "parallel", "arbitrary")), name=name)(qh, kh, vh)


def flash_bwd(qh, kh, vh, o, lse, do, name):
    H, S_, W = qh.shape
    t = _tile(S_, FLASH_TILE)
    n = S_ // t

    def body(q_ref, k_ref, v_ref, o_ref, lse_ref, do_ref, dq_ref, dk_ref, dv_ref):
        kj, qi = pl.program_id(1), pl.program_id(2)

        @pl.when(jnp.logical_and(kj == 0, qi == 0))
        def _():
            dq_ref[...] = jnp.zeros_like(dq_ref)

        @pl.when(qi == 0)
        def _():
            dk_ref[...] = jnp.zeros_like(dk_ref)
            dv_ref[...] = jnp.zeros_like(dv_ref)

        def step(diagonal):
            q, k, v = q_ref[...], k_ref[...], v_ref[...]
            do_ = do_ref[...]
            p = jnp.exp(_scores(q, k, diagonal) - lse_ref[:, :1])
            dob = do_.astype(BF16)
            dv_ref[...] += lax.dot_general(p.astype(BF16), dob, (_TN, ((), ())), preferred_element_type=F32)
            dp = lax.dot_general(dob, v, (_NT, ((), ())), preferred_element_type=F32)
            delta = jnp.sum(do_ * o_ref[...], axis=-1, keepdims=True)
            ds = (p * (dp - delta)).astype(BF16)
            dk_ref[...] += lax.dot_general(ds, q, (_TN, ((), ())), preferred_element_type=F32)
            rows = pl.ds(pl.multiple_of(qi * t, t), t)
            dq_ref[rows, :] += lax.dot_general(ds, k, (_NN, ((), ())), preferred_element_type=F32)

        pl.when(qi > kj)(lambda: step(False))
        pl.when(qi == kj)(lambda: step(True))

    qrow = lambda h, j, i: jnp.maximum(i, j)
    return _pc(body, grid=(H, n, n),
               in_specs=[pl.BlockSpec((None, t, W), lambda h, j, i: (h, qrow(h, j, i), 0)),
                         pl.BlockSpec((None, t, W), lambda h, j, i: (h, j, 0)),
                         pl.BlockSpec((None, t, LANES), lambda h, j, i: (h, j, 0)),
                         pl.BlockSpec((t, LANES), lambda h, j, i: (qrow(h, j, i), h)),
                         pl.BlockSpec((None, t, LANES), lambda h, j, i: (h, qrow(h, j, i), 0)),
                         pl.BlockSpec((t, LANES), lambda h, j, i: (qrow(h, j, i), h))],
               out_specs=[pl.BlockSpec((None, S_, W), lambda h, j, i: (h, 0, 0)),
                          pl.BlockSpec((None, t, W), lambda h, j, i: (h, j, 0)),
                          pl.BlockSpec((None, t, LANES), lambda h, j, i: (h, j, 0))],
               out_shape=[_sds((H, S_, W), F32), _sds((H, S_, W), F32), _sds((H, S_, LANES), F32)],
               compiler_params=_cparams(("parallel", "arbitrary", "arbitrary")), name=name)(qh, kh, vh, o, lse, do)


def loss_head(x, target, g, name):
    S_ = x.shape[0]
    ts = 256

    def body(x_ref, t_ref, g_ref, l_ref, dx_ref, dg_ref):
        @pl.when(pl.program_id(0) == 0)
        def _():
            l_ref[...] = jnp.zeros_like(l_ref)
            dg_ref[...] = jnp.zeros_like(dg_ref)

        y, vjp = jax.vjp(_rms, x_ref[...], g_ref[...])
        err = y - t_ref[...]
        l_ref[...] += 0.5 * jnp.sum(jnp.sum(err * err, axis=-1, keepdims=True), axis=0, keepdims=True) / D
        dx, dg = vjp(err / D)
        dx_ref[...] = dx
        dg_ref[...] += dg

    row = pl.BlockSpec((ts, D), lambda i: (i, 0))
    return _pc(body, grid=(S_ // ts,), in_specs=[row, row, pl.BlockSpec((1, D), lambda i: (0, 0))],
               out_specs=[pl.BlockSpec((1, LANES), lambda i: (0, 0)), row, pl.BlockSpec((1, D), lambda i: (0, 0))],
               out_shape=[_sds((1, LANES), F32), _sds((S_, D), F32), _sds((1, D), F32)],
               compiler_params=_cparams(("arbitrary",)), name=name)(x, target, g)


def adamw(w, parts, m, v, name):
    R, C = w.shape
    rows = [p.shape[1] for p in parts[0]]
    tr = R
    for cand in (512, 256, 128, 64, 32, 16, 8):
        if all(r % cand == 0 for r in rows) and cand * C * 4 <= 1024 * 1024:
            tr = cand
            break
    c1 = 1.0 - ADAM_B1 ** ADAM_STEP
    c2 = 1.0 - ADAM_B2 ** ADAM_STEP
    starts = [sum(rows[:k]) // tr for k in range(len(rows))]
    flat = [p for part in parts for p in part]

    def body(*refs):
        w_ref, m_ref, v_ref = refs[0], refs[1 + len(flat)], refs[2 + len(flat)]
        g_ref, d_ref, nm_ref, nv_ref = refs[3 + len(flat):]
        i = pl.program_id(0)
        gg, at = None, 1
        for part in parts:
            val = None
            for k in range(len(part)):
                p_ref = refs[at]
                at += 1
                s = p_ref[0].astype(F32)
                for n in range(1, p_ref.shape[0]):
                    s = s + p_ref[n].astype(F32)
                val = s if val is None else jnp.where(i >= starts[k], s, val)
            gg = val if gg is None else gg + val
        m2 = ADAM_B1 * m_ref[...] + (1.0 - ADAM_B1) * gg
        v2 = ADAM_B2 * v_ref[...] + (1.0 - ADAM_B2) * (gg * gg)
        g_ref[...] = gg
        d_ref[...] = -ADAM_LR * ((m2 / c1) / (jnp.sqrt(v2 / c2) + ADAM_EPS) + ADAM_WD * w_ref[...])
        nm_ref[...] = m2
        nv_ref[...] = v2

    blk = pl.BlockSpec((tr, C), lambda i: (i, 0))
    piece = lambda p, k: pl.BlockSpec((p.shape[0], tr, C), lambda i: (0, jnp.clip(i - starts[k], 0, rows[k] // tr - 1), 0))
    pblk = [piece(p, k) for part in parts for k, p in enumerate(part)]
    return _pc(body, grid=(R // tr,), in_specs=[blk] + pblk + [blk, blk], out_specs=[blk] * 4, out_shape=[_sds((R, C), F32)] * 4,
               compiler_params=_cparams(("parallel",)), name=name)(w, *flat, m, v)


def sum_slots(own, recv, skip, name):
    n, R, C = recv.shape
    tr = _tile(R, 512) if R % LANES == 0 else R
    has_own = own is not None

    def body(*refs):
        skip_ref = refs[0]
        r_ref, o_ref = refs[-2], refs[-1]
        acc = refs[1][...] if has_own else jnp.zeros(o_ref.shape, F32)
        for s in range(n):
            acc = acc + jnp.where(skip_ref[0] == s, 0.0, r_ref[s].astype(F32))
        o_ref[...] = acc

    row = pl.BlockSpec((tr, C), lambda i, sk: (i, 0))
    gs = pltpu.PrefetchScalarGridSpec(
        num_scalar_prefetch=1, grid=(R // tr,),
        in_specs=([row] if has_own else []) + [pl.BlockSpec((n, tr, C), lambda i, sk: (0, i, 0))], out_specs=row)
    ins = ([own] if has_own else []) + [recv]
    return _pc(body, grid_spec=gs, out_shape=_sds((R, C), F32), compiler_params=_cparams(("parallel",)), name=name)(skip, *ins)


def _chip_peers():
    x, y, c = lax.axis_index("x"), lax.axis_index("y"), lax.axis_index("c")
    return (x, y, c), [(1 - x, y, c), (x, 1 - y, c), (1 - x, 1 - y, c)]


def _chip_index(p):
    return 2 * p[0] + p[1]


def _win(ref, axis, chip, size):
    if axis is None:
        return ref.at[chip]
    idx = [slice(None)] * len(ref.shape)
    idx[axis] = pl.ds(pl.multiple_of(chip * size, size), size)
    return ref.at[tuple(idx)]


def _remote(src, dst, send_sem, recv_sem, peer):
    return pltpu.make_async_remote_copy(src_ref=src, dst_ref=dst, send_sem=send_sem, recv_sem=recv_sem, device_id=peer,
                                        device_id_type=MESH)


HBM_SPEC = pl.BlockSpec(memory_space=pltpu.HBM)
SEM_SPEC = pl.BlockSpec(memory_space=pltpu.SEMAPHORE)
ANY_SPEC = pl.BlockSpec(memory_space=pl.ANY)
DATAFLOW = pltpu.SideEffectType.DATAFLOW_SIDE_EFFECTING


def gather_piece(i, l, o, axis, size):
    return (i, lambda r, chip: r.at[l], o, lambda r, chip: _win(r, axis, chip, size))


def scatter_piece(i, o, axis, size):
    return (i, lambda r, chip: _win(r, axis, chip, size), o, lambda r, chip: r.at[chip])


def _copies(pieces, in_refs, out_refs, send, recv):
    me, peers = _chip_peers()
    mine = _chip_index(me)
    local, remote = [], []
    for n, (i, src, o, dst) in enumerate(pieces):
        d = dst(out_refs[o], mine)
        local.append((src(in_refs[i], mine), d))
        remote += [_remote(src(in_refs[i], _chip_index(p)), d, send.at[3 * n + k], recv.at[3 * n + k], p)
                   for k, p in enumerate(peers)]
    return local, remote


def exchange_start(pieces, ins, out_shapes, after, name):
    n_in, n_out, ncp = len(ins), len(out_shapes), len(pieces)

    def body(*refs):
        in_refs, land_refs = refs[:n_in], refs[n_in:n_in + n_out]
        send, recv = refs[n_in + n_out + 1], refs[n_in + n_out + 2]
        token, local_sems = refs[-2], refs[-1]
        local, remote = _copies(pieces, in_refs, land_refs, send, recv)
        own = [pltpu.make_async_copy(s, d, local_sems.at[n]) for n, (s, d) in enumerate(local)]
        for cp in own:
            cp.start()
        for cp in own:
            cp.wait()
        for cp in remote:
            cp.start()
        token[...] = jnp.zeros_like(token)

    hbm = lambda a: pltpu.with_memory_space_constraint(a, pltpu.HBM)
    lands = [hbm(lax.empty(s.shape, s.dtype)) for s in out_shapes]
    sem = pltpu.SemaphoreType.DMA((3 * ncp,))
    thru = [pltpu.HBM(a.shape, a.dtype) for a in ins] + [pltpu.HBM(s.shape, s.dtype) for s in out_shapes]
    res = _pc(body, in_specs=[HBM_SPEC] * (n_in + n_out) + [ANY_SPEC],
              out_specs=[SEM_SPEC, SEM_SPEC] + [HBM_SPEC] * (n_in + n_out) + [pl.BlockSpec(memory_space=pltpu.VMEM)],
              out_shape=[sem, sem] + thru + [_sds((8, LANES), F32)],
              input_output_aliases={i: 2 + i for i in range(n_in + n_out)},
              scratch_shapes=[pltpu.SemaphoreType.DMA((ncp,))],
              compiler_params=pltpu.CompilerParams(has_side_effects=DATAFLOW), name=name)(*[hbm(a) for a in ins], *lands, after)
    return (res[0], res[1]), list(res[2:2 + n_in]), list(res[2 + n_in:2 + n_in + n_out]), res[-1]


def exchange_wait(pieces, sems, ins, lands, after, name):
    n_in, n_out = len(ins), len(lands)

    def body(*refs):
        in_refs, land_refs = refs[:n_in], refs[n_in:n_in + n_out]
        send, recv = refs[n_in + n_out], refs[n_in + n_out + 1]
        _, remote = _copies(pieces, in_refs, land_refs, send, recv)
        for cp in remote:
            cp.wait_send()
            cp.wait_recv()

    thru = [pltpu.HBM(a.shape, a.dtype) for a in ins] + [pltpu.HBM(a.shape, a.dtype) for a in lands]
    res = _pc(body, in_specs=[HBM_SPEC] * (n_in + n_out) + [SEM_SPEC, SEM_SPEC, ANY_SPEC], out_specs=[HBM_SPEC] * (n_in + n_out),
              out_shape=thru, input_output_aliases={i: i for i in range(n_in + n_out)},
              compiler_params=pltpu.CompilerParams(has_side_effects=DATAFLOW), name=name)(*ins, *lands, sems[0], sems[1], after)
    return list(res[n_in:])


def swap_cores(bufs, name):
    nb = len(bufs)

    def body(*refs):
        in_refs, out_refs = refs[:nb], refs[nb:2 * nb]
        send, recv = refs[2 * nb:]
        x, y, c = lax.axis_index("x"), lax.axis_index("y"), lax.axis_index("c")
        copies = [_remote(in_refs[b], out_refs[b], send.at[b], recv.at[b], (x, y, 1 - c)) for b in range(nb)]
        for cp in copies:
            cp.start()
        for cp in copies:
            cp.wait()

    anyspec = pl.BlockSpec(memory_space=pl.ANY)
    return _pc(body, in_specs=[anyspec] * nb, out_specs=[anyspec] * nb, out_shape=[_sds(b.shape, b.dtype) for b in bufs],
               scratch_shapes=[pltpu.SemaphoreType.DMA((nb,)), pltpu.SemaphoreType.DMA((nb,))], name=name)(*bufs)


def exchange_all(buf, name):
    def body(in_ref, out_ref, send, recv, local):
        x, y, c = lax.axis_index("x"), lax.axis_index("y"), lax.axis_index("c")
        mine = 4 * x + 2 * y + c
        loc = pltpu.make_async_copy(in_ref, out_ref.at[mine], local)
        loc.start()
        copies = [loc]
        for k in range(1, 8):
            peer = (x ^ (k >> 2), y ^ ((k >> 1) & 1), c ^ (k & 1))
            cp = pltpu.make_async_remote_copy(src_ref=in_ref, dst_ref=out_ref.at[mine], send_sem=send.at[k - 1],
                                              recv_sem=recv.at[k - 1], device_id=peer, device_id_type=MESH)
            cp.start()
            copies.append(cp)
        for cp in copies:
            cp.wait()

    anyspec = pl.BlockSpec(memory_space=pl.ANY)
    return _pc(body, in_specs=[anyspec], out_specs=anyspec, out_shape=_sds((8,) + buf.shape, buf.dtype),
               scratch_shapes=[pltpu.SemaphoreType.DMA((7,)), pltpu.SemaphoreType.DMA((7,)), pltpu.SemaphoreType.DMA],
               name=name)(buf)


def _norm_fwd(x, g, name):
    return rowwise(f_rms, [(x, D, 0, 0)], [(g, D, 0, 0)], [(D, 0, BF16)], ts=512, name=name)[0]


def _norm_bwd(x, g, dh, dres, name):
    (dx,), (dg,) = rowwise_bwd(f_rms, [(x, D, 0, 0)], [(g, D, 0, 0)], [(dh, D, 0, 0)], need=[True],
                               adds={0: (dres, D, 0, 0)}, ts=256, name=name)
    return dx, dg


def pool_fwd(x, W, tag):
    h = _norm_fwd(x, W["ng"], tag + "_norm")
    proj = mm(h, W["w_in"], name=tag + "_in")
    p = pool_time_fwd(proj, tag + "_win")
    pg = gmm("nn", p, W["w_grp"], G=4, name=tag + "_grp")
    y = rowwise(f_pool_gate, [(pg, POOL_GROUP, 0, 1), (proj, POOL_GROUP, 4, 1)], [(W["scale"], POOL_GROUP, 0, 1)],
                [(POOL_GROUP, 1, BF16)], ncol=4, ts=512, name=tag + "_gate")[0]
    xn = mm(y, W["w_out"], add=x, name=tag + "_out")
    return xn, (x, h, proj, p, pg, y)


def pool_bwd(dxn, W, saved, tag):
    x, h, proj, p, pg, y = saved
    dy = mm(dxn, W["w_out"], tb=True, name=tag + "_dy")
    g = {"w_out": mm(y, dxn, ta=True, out_dtype=BF16, name=tag + "_dwout")}
    (dpg, dproj), (g["scale"],) = rowwise_bwd(
        f_pool_gate, [(pg, POOL_GROUP, 0, 1), (proj, POOL_GROUP, 4, 1)], [(W["scale"], POOL_GROUP, 0, 1)],
        [(dy, POOL_GROUP, 0, 1)], need=[True, True], place={1: (2 * POOL_WIDTH, 4)}, ncol=4, ts=512, name=tag + "_dgate")
    dp = gmm("nt", dpg, W["w_grp"], G=4, name=tag + "_dp")
    g["w_grp"] = gmm("tn", p, dpg, G=4, out_dtype=BF16, name=tag + "_dwgrp")
    dproj = pool_time_bwd(dp, dproj, tag + "_dwin")
    dh = mm(dproj, W["w_in"], tb=True, name=tag + "_dh")
    g["w_in"] = mm(h, dproj, ta=True, out_dtype=BF16, name=tag + "_dw_in")
    dx, g["ng"] = _norm_bwd(x, W["ng"], dh, dxn, tag + "_dnorm")
    return dx, g


def gdn_fwd(x, W, tag):
    h = _norm_fwd(x, W["ng"], tag + "_norm")
    proj = mm(h, W["w_in"], name=tag + "_in")
    qkv = gdn_conv_fwd(proj, W["conv"], tag + "_conv")
    g_b, beta_b = rowwise(f_gdn_gates, [(proj, LANES, 6144 // LANES, 0)], [(W["a_log"], LANES, 0, 0), (W["dt_bias"], LANES, 0, 0)],
                          [(GDN_QK, 0, F32), (GDN_QK, 0, F32)], ts=512, name=tag + "_gates")
    o, states = gdn_chunk_fwd(qkv, g_b, beta_b, tag + "_chunk")
    og = rowwise(f_gdn_out, [(o, GDN_DV, 0, 1), (proj, GDN_DV, 4096 // GDN_DV, 1)], [(W["norm_g"], GDN_DV, 0, 0)],
                 [(GDN_DV, 1, BF16)], ncol=GDN_H, ts=512, name=tag + "_onorm")[0]
    xn = mm(og, W["w_out"], add=x, name=tag + "_out")
    return xn, (x, h, proj, qkv, g_b, beta_b, o, states, og)


def gdn_bwd(dxn, W, saved, tag):
    x, h, proj, qkv, g_b, beta_b, o, states, og = saved
    dog = mm(dxn, W["w_out"], tb=True, name=tag + "_dog")
    g = {"w_out": mm(og, dxn, ta=True, out_dtype=BF16, name=tag + "_dwout")}
    (do, dproj), (g["norm_g"],) = rowwise_bwd(
        f_gdn_out, [(o, GDN_DV, 0, 1), (proj, GDN_DV, 4096 // GDN_DV, 1)], [(W["norm_g"], GDN_DV, 0, 0)],
        [(dog, GDN_DV, 0, 1)], need=[True, True], place={1: (GDN_IN_PAD, 4096 // GDN_DV)}, ncol=GDN_H, ts=512, name=tag + "_donorm")
    dq, dk, dv, dg_b, dbeta_b = gdn_chunk_bwd(qkv, g_b, beta_b, states, do, tag + "_dchunk")
    (dproj,), (g["a_log"], g["dt_bias"]) = rowwise_bwd(
        f_gdn_gates, [(proj, LANES, 6144 // LANES, 0)], [(W["a_log"], LANES, 0, 0), (W["dt_bias"], LANES, 0, 0)],
        [(dg_b, GDN_QK, 0, 0), (dbeta_b, GDN_QK, 0, 0)], need=[True], place={0: (dproj, 6144 // LANES)}, ts=256, name=tag + "_dgates")
    dproj, g["conv"] = gdn_conv_bwd(proj, W["conv"], dq, dk, dv, dproj, tag + "_dconv")
    dh = mm(dproj, W["w_in"], tb=True, name=tag + "_dh")
    g["w_in"] = mm(h, dproj, ta=True, out_dtype=BF16, name=tag + "_dw_in")
    dx, g["ng"] = _norm_bwd(x, W["ng"], dh, dxn, tag + "_dnorm")
    return dx, g


def mla_fwd(x, pos, W, tag):
    h = _norm_fwd(x, W["ng"], tag + "_norm")
    proj = mm(h, W["w_in"], name=tag + "_in")
    hq = rowwise(f_rms, [(proj, MLA_Q_LORA, 0, 0)], [(W["q_g"], MLA_Q_LORA, 0, 0)], [(MLA_Q_LORA, 0, BF16)], ts=512, name=tag + "_qnorm")[0]
    hkv = rowwise(f_rms, [(proj, MLA_KV_LORA, 2, 0)], [(W["kv_g"], MLA_KV_LORA, 0, 0)], [(MLA_KV_LORA, 0, BF16)], ts=512, name=tag + "_kvnorm")[0]
    qpad = mm(hq, W["w_uq"], name=tag + "_uq")
    kv = mm(hkv, W["w_ukv"], name=tag + "_ukv")
    qh, kh, vh = mla_prep_fwd(qpad, kv, proj, pos, W["rope"], tag + "_prep")
    o, lse = flash_fwd(qh, kh, vh, tag + "_attn")
    og = rowwise(f_ogate, [(o, 512, 0, 1), (proj, 512, 4, 1)], [], [(512, 1, BF16)], ncol=4, ts=512, name=tag + "_ogate")[0]
    xn = mm(og, W["w_out"], add=x, name=tag + "_out")
    return xn, (x, h, proj, hq, hkv, qh, kh, vh, o, lse, og)


def mla_bwd(dxn, pos, W, saved, tag):
    x, h, proj, hq, hkv, qh, kh, vh, o, lse, og = saved
    dog = mm(dxn, W["w_out"], tb=True, name=tag + "_dog")
    g = {"w_out": mm(og, dxn, ta=True, out_dtype=BF16, name=tag + "_dwout")}
    dproj = jnp.zeros(proj.shape, F32)
    (do, dproj), _ = rowwise_bwd(f_ogate, [(o, 512, 0, 1), (proj, 512, 4, 1)], [], [(dog, 512, 0, 1)], need=[True, True],
                                 place={1: (dproj, 4)}, ncol=4, ts=512, name=tag + "_dogate")
    dqh, dkh, dvh = flash_bwd(qh, kh, vh, o, lse, do, tag + "_dattn")
    dqpad, dkv, dproj = mla_prep_bwd(dqh, dkh, dvh, pos, W["rope"], dproj, tag + "_dprep")
    dhq = mm(dqpad, W["w_uq"], tb=True, name=tag + "_dhq")
    g["w_uq"] = mm(hq, dqpad, ta=True, out_dtype=BF16, name=tag + "_dwuq")
    dhkv = mm(dkv, W["w_ukv"], tb=True, name=tag + "_dhkv")
    g["w_ukv"] = mm(hkv, dkv, ta=True, out_dtype=BF16, name=tag + "_dwukv")
    (dproj,), (g["q_g"],) = rowwise_bwd(f_rms, [(proj, MLA_Q_LORA, 0, 0)], [(W["q_g"], MLA_Q_LORA, 0, 0)], [(dhq, MLA_Q_LORA, 0, 0)],
                                        need=[True], place={0: (dproj, 0)}, ts=256, name=tag + "_dqnorm")
    (dproj,), (g["kv_g"],) = rowwise_bwd(f_rms, [(proj, MLA_KV_LORA, 2, 0)], [(W["kv_g"], MLA_KV_LORA, 0, 0)], [(dhkv, MLA_KV_LORA, 0, 0)],
                                         need=[True], place={0: (dproj, 2)}, ts=256, name=tag + "_dkvnorm")
    dh = mm(dproj, W["w_in"], tb=True, name=tag + "_dh")
    g["w_in"] = mm(h, dproj, ta=True, out_dtype=BF16, name=tag + "_dw_in")
    dx, g["ng"] = _norm_bwd(x, W["ng"], dh, dxn, tag + "_dnorm")
    return dx, g


def _pad_cols(a, n):
    return jnp.pad(a, ((0, 0), (0, n - a.shape[1])))


def _mla_w_in_layout(w):
    z = lambda n: jnp.zeros((w.shape[0], n), w.dtype)
    kr = w[:, 1280:1344]
    return jnp.concatenate([w[:, :768], z(256), w[:, 768:1280], kr[:, :32], z(32), kr[:, 32:], z(32), z(384), w[:, 1344:]], axis=1)


def _mla_w_in_unlayout(g):
    return jnp.concatenate([g[:, :768], g[:, 1024:1536], g[:, 1536:1568], g[:, 1600:1632], g[:, 2048:]], axis=1)


def _mla_w_uq_layout(w):
    w3 = w.reshape(w.shape[0], MLA_H, MLA_NOPE + MLA_ROPE)
    z = jnp.zeros((w.shape[0], MLA_H, 32), w.dtype)
    return jnp.concatenate([w3[..., :128], w3[..., 128:160], z, w3[..., 160:192], z], axis=-1).reshape(w.shape[0], MLA_H * 256)


def _mla_w_uq_unlayout(g):
    g3 = g.reshape(g.shape[0], MLA_H, 256)
    return jnp.concatenate([g3[..., :128], g3[..., 128:160], g3[..., 192:224]], axis=-1).reshape(g.shape[0], MLA_H * 192)


def _rope_consts():
    half = MLA_ROPE // 2
    inv = ROPE_THETA ** (-jnp.arange(half, dtype=F32) / half)
    z = jnp.zeros((half,), F32)
    o = jnp.ones((half,), F32)
    row = lambda *p: jnp.concatenate(p).reshape(1, LANES)
    return row(inv, z, inv, z), row(o, z, o, z), row(-o, z, o, z)


BIG = ["pool_w_in", "pool_w_grp", "pool_w_out", "gdn_w_in", "gdn_w_out", "mla_w_in", "mla_w_uq", "mla_w_ukv", "mla_w_out"]
BIG_LAYOUT = {"pool_w_in": (1, 1024, (1024, 4096)), "pool_w_grp": (1, 128, (4, 512, 512)), "pool_w_out": (0, 512, (2048, 1024)),
              "gdn_w_in": (None, None, (4, 1024, 1540)), "gdn_w_out": (0, 512, (2048, 1024)),
              "mla_w_in": (None, None, (4, 1024, 848)), "mla_w_uq": (1, 768, (768, 3072)), "mla_w_ukv": (1, 1024, (512, 4096)),
              "mla_w_out": (0, 512, (2048, 1024))}
SMALL_SHARDED = ["pool_scale", "gdn_conv", "mla_q_norm_g", "mla_kv_norm_g"]
SMALL_AXIS = {"pool_scale": 1, "gdn_conv": 2, "mla_q_norm_g": 1, "mla_kv_norm_g": 1}
REPLICATED = ["norm_g", "gdn_a_log", "gdn_dt_bias", "gdn_norm_g", "final_g"]
PACK_C = 1024


def _pack(parts, dtype, row_mult):
    flat = jnp.concatenate([p.reshape(-1).astype(dtype) for p in parts])
    rows = -(-flat.shape[0] // PACK_C)
    rows = -(-rows // row_mult) * row_mult
    return jnp.pad(flat, (0, rows * PACK_C - flat.shape[0])).reshape(rows, PACK_C)


def _unpack(buf, shapes):
    lead = buf.shape[:-2]
    flat = buf.reshape(lead + (-1,))
    out, off = [], 0
    for s in shapes:
        n = int(np.prod(s))
        out.append(flat[..., off:off + n].reshape(lead + tuple(s)))
        off += n
    return out


def _unshard(g4, axis):
    a = jnp.moveaxis(g4, 0, axis)
    s = a.shape
    return a.reshape(s[:axis] + (s[axis] * s[axis + 1],) + s[axis + 2:])


def _to_shards(a, axis):
    s = a.shape
    return jnp.moveaxis(a.reshape(s[:axis] + (4, s[axis] // 4) + s[axis + 1:]), axis, 0)


def layer_weights(full, small, rep, layer):
    ng = rep["norm_g"][layer:layer + 1]
    side_by_side = lambda a4: jnp.moveaxis(a4, 0, 1).reshape(a4.shape[1], 4 * a4.shape[2])
    if layer in (0, 3):
        j = layer // 3
        return dict(ng=ng, w_in=full[("pool_w_in", j)], w_grp=full[("pool_w_grp", j)], scale=small["pool_scale"][j:j + 1],
                    w_out=full[("pool_w_out", j)])
    if layer == 1:
        return dict(ng=ng, w_in=_pad_cols(side_by_side(full[("gdn_w_in", 0)]), GDN_IN_PAD),
                    conv=jnp.pad(small["gdn_conv"][0], ((0, 4), (0, 0))), a_log=_pad_cols(rep["gdn_a_log"], LANES),
                    dt_bias=_pad_cols(rep["gdn_dt_bias"], LANES), norm_g=rep["gdn_norm_g"], w_out=full[("gdn_w_out", 0)])
    return dict(ng=ng, w_in=_mla_w_in_layout(side_by_side(full[("mla_w_in", 0)])), q_g=small["mla_q_norm_g"],
                kv_g=small["mla_kv_norm_g"], w_uq=_mla_w_uq_layout(full[("mla_w_uq", 0)]), w_ukv=full[("mla_w_ukv", 0)],
                w_out=full[("mla_w_out", 0)], rope=_rope_consts())


def big_grad_pieces(gl):
    g0, g1, g2, g3 = gl
    slots = lambda a: jnp.moveaxis(a.reshape(a.shape[0], 4, a.shape[1] // 4), 1, 0)
    out = {}
    for l, g in ((0, g0), (1, g3)):
        if g is not None:
            out.update({("pool_w_in", l): g["w_in"], ("pool_w_grp", l): g["w_grp"], ("pool_w_out", l): g["w_out"]})
    if g1 is not None:
        out.update({("gdn_w_in", 0): slots(g1["w_in"][:, :GDN_IN]), ("gdn_w_out", 0): g1["w_out"]})
    if g2 is not None:
        out.update({("mla_w_in", 0): slots(_mla_w_in_unlayout(g2["w_in"])), ("mla_w_uq", 0): _mla_w_uq_unlayout(g2["w_uq"]),
                    ("mla_w_ukv", 0): g2["w_ukv"], ("mla_w_out", 0): g2["w_out"]})
    return out


def small_grads(gl, dfinal):
    g0, g1, g2, g3 = gl
    return {"norm_g": jnp.concatenate([g0["ng"], g1["ng"], g2["ng"], g3["ng"]], axis=0),
            "pool_scale": jnp.concatenate([g0["scale"], g3["scale"]], axis=0), "gdn_conv": g1["conv"][None, :4],
            "gdn_a_log": g1["a_log"][:, :GDN_H], "gdn_dt_bias": g1["dt_bias"][:, :GDN_H], "gdn_norm_g": g1["norm_g"],
            "mla_q_norm_g": g2["q_g"], "mla_kv_norm_g": g2["kv_g"], "final_g": dfinal.reshape(D)}


NAMES = ["norm_g", "pool_w_in", "pool_w_grp", "pool_scale", "pool_w_out", "gdn_w_in", "gdn_conv", "gdn_a_log", "gdn_dt_bias",
         "gdn_norm_g", "gdn_w_out", "mla_w_in", "mla_q_norm_g", "mla_w_uq", "mla_kv_norm_g", "mla_w_ukv", "mla_w_out", "final_g"]


def kernel(x, positions, norm_g, pool_w_in, pool_w_grp, pool_scale, pool_w_out, gdn_w_in, gdn_conv, gdn_a_log, gdn_dt_bias, gdn_norm_g, gdn_w_out, mla_w_in, mla_q_norm_g, mla_w_uq, mla_kv_norm_g, mla_w_ukv, mla_w_out, final_g, loss_target, m_norm_g, m_pool_w_in, m_pool_w_grp, m_pool_scale, m_pool_w_out, m_gdn_w_in, m_gdn_conv, m_gdn_a_log, m_gdn_dt_bias, m_gdn_norm_g, m_gdn_w_out, m_mla_w_in, m_mla_q_norm_g, m_mla_w_uq, m_mla_kv_norm_g, m_mla_w_ukv, m_mla_w_out, m_final_g, v_norm_g, v_pool_w_in, v_pool_w_grp, v_pool_scale, v_pool_w_out, v_gdn_w_in, v_gdn_conv, v_gdn_a_log, v_gdn_dt_bias, v_gdn_norm_g, v_gdn_w_out, v_mla_w_in, v_mla_q_norm_g, v_mla_w_uq, v_mla_kv_norm_g, v_mla_w_ukv, v_mla_w_out, v_final_g):
    args = locals()
    w = {n: args[n] for n in NAMES}
    m = {n: args["m_" + n] for n in NAMES}
    v = {n: args["v_" + n] for n in NAMES}
    my_chip = (2 * lax.axis_index("x") + lax.axis_index("y")).astype(I32)
    S_ = x.shape[1]
    x0, pos, target = x[0], positions.reshape(S_, 1).astype(F32), loss_target[0]
    rep = {n: w[n] for n in REPLICATED}

    shard = {n: w[n].astype(BF16) for n in BIG}
    small_shapes = [w[n].shape for n in SMALL_SHARDED]
    shard["small"] = _pack([w[n] for n in SMALL_SHARDED], F32, 8)[None]
    layout = dict(BIG_LAYOUT, small=(None, None, (4,) + shard["small"].shape[1:]))

    def gather_start(group, after, tag):
        pieces = [gather_piece(i, l, i, layout[n][0], layout[n][1]) for i, (n, l) in enumerate(group)]
        shapes = [_sds(layout[n][2], shard[n].dtype) for n, l in group]
        sems, ins, lands, token = exchange_start(pieces, [shard[n] for n, l in group], shapes, after, tag + "_start")
        return (pieces, sems, ins, lands), token

    def finish(handle, after, tag):
        return exchange_wait(*handle, after, tag + "_wait")

    tied = lambda a, token: a + token[0:1, 0:1]
    group_a = [("pool_w_in", 0), ("pool_w_grp", 0), ("pool_w_out", 0), ("small", 0)]
    group_b = [("gdn_w_in", 0), ("gdn_w_out", 0)]
    group_c = [("mla_w_in", 0), ("mla_w_uq", 0), ("mla_w_ukv", 0), ("mla_w_out", 0), ("pool_w_in", 1), ("pool_w_grp", 1), ("pool_w_out", 1)]
    full = {}
    h_a, t_a = gather_start(group_a, x0, "gather_a")
    got_a = finish(h_a, t_a, "gather_a")
    full.update({k: a for k, a in zip(group_a, got_a)})
    small = {n: _unshard(a, SMALL_AXIS[n]) for n, a in zip(SMALL_SHARDED, _unpack(full[("small", 0)], small_shapes))}
    h_b, t_b = gather_start(group_b, got_a[0], "gather_b")
    W0 = layer_weights(full, small, rep, 0)
    x1, s0 = pool_fwd(x0, dict(W0, ng=tied(W0["ng"], t_b)), "l0")
    got_b = finish(h_b, x1, "gather_b")
    full.update({k: a for k, a in zip(group_b, got_b)})
    h_c, t_c = gather_start(group_c, got_b[0], "gather_c")
    W1 = layer_weights(full, small, rep, 1)
    x2, s1 = gdn_fwd(x1, dict(W1, ng=tied(W1["ng"], t_c)), "l1")
    got_c = finish(h_c, x2, "gather_c")
    full.update({k: a for k, a in zip(group_c, got_c)})
    W2, W3 = layer_weights(full, small, rep, 2), layer_weights(full, small, rep, 3)
    x3, s2 = mla_fwd(x2, pos, W2, "l2")
    x4, s3 = pool_fwd(x3, W3, "l3")
    loss_part, dx4, dfinal = loss_head(x4, target, final_g.reshape(1, D), "loss_head")

    def scatter_start(pieces_of, after, tag):
        keys = list(pieces_of)
        pieces = [scatter_piece(i, i, BIG_LAYOUT[n][0], BIG_LAYOUT[n][1]) for i, (n, l) in enumerate(keys)]
        shapes = [_sds((4,) + tuple(w[n].shape[1:]), BF16) for n, l in keys]
        sems, ins, lands, token = exchange_start(pieces, [pieces_of[k] for k in keys], shapes, after, tag + "_start")
        return keys, (pieces, sems, ins, lands), token

    dx3, g3 = pool_bwd(dx4, W3, s3, "l3")
    k3, h3, t3 = scatter_start(big_grad_pieces((None, None, None, g3)), dx3, "scatter_l3")
    dx2, g2 = mla_bwd(tied(dx3, t3), pos, W2, s2, "l2")
    k2, h2, t2 = scatter_start(big_grad_pieces((None, None, g2, None)), dx2, "scatter_l2")
    dx1, g1 = gdn_bwd(tied(dx2, t2), W1, s1, "l1")
    k1, h1, t1 = scatter_start(big_grad_pieces((None, g1, None, None)), dx1, "scatter_l1")
    dx0, g0 = pool_bwd(tied(dx1, t1), W0, s0, "l0")
    k0, h0, t0 = scatter_start(big_grad_pieces((g0, None, None, None)), dx0, "scatter_l0")
    recv = {}
    for keys, handle, tag in ((k3, h3, "scatter_l3"), (k2, h2, "scatter_l2"), (k1, h1, "scatter_l1"), (k0, h0, "scatter_l0")):
        recv.update(zip(keys, finish(handle, t0, tag)))
    keys = list(recv)
    sib = dict(zip(keys, swap_cores([recv[k] for k in keys], "swap_cores")))

    sg = small_grads((g0, g1, g2, g3), dfinal)
    small_names = SMALL_SHARDED + REPLICATED
    small_buf = _pack([sg[n] for n in small_names] + [loss_part], F32, 8)
    small_sum = sum_slots(None, exchange_all(small_buf, "gather_small"), jnp.full((1,), -1, I32), "sum_small")
    full_small = _unpack(small_sum, [sg[n].shape for n in small_names] + [(1, LANES)])
    loss = full_small[-1][0, 0]
    small_part = {}
    for n, a in zip(small_names, full_small[:-1]):
        if n in SMALL_AXIS:
            a = lax.dynamic_index_in_dim(_to_shards(a, SMALL_AXIS[n]), my_chip, axis=0, keepdims=False)
        small_part[n] = a

    outs = []
    for n in NAMES:
        shp = w[n].shape
        two = (int(np.prod(shp[:-1])), shp[-1]) if len(shp) > 1 else (1, shp[0])
        if n in BIG_LAYOUT:
            layers = shp[0]
            rows = lambda a: a.reshape(4, two[0] // layers, two[1])
            parts = [[rows(recv[(n, l)]) for l in range(layers)], [rows(sib[(n, l)]) for l in range(layers)]]
        else:
            parts = [[small_part[n].reshape((1,) + two)]]
        res = adamw(w[n].reshape(two), parts, m[n].reshape(two), v[n].reshape(two), "adamw_" + n)
        outs.append([r.reshape(shp) for r in res])
    return (loss, dx0[None], *[o[0] for o in outs], *[o[1] for o in outs], *[o[2] for o in outs], *[o[3] for o in outs])
```

```python
import functools
import math

import jax
import jax.numpy as jnp
import numpy as np
from jax import lax
from jax.experimental import pallas as pl
from jax.experimental.pallas import tpu as pltpu

F32 = jnp.float32
BF16 = jnp.bfloat16
I32 = jnp.int32

D = 1024
EPS = 1e-6
POOL_WIDTH = 2048
POOL_GROUP = 512
GDN_H, GDN_DK, GDN_DV, GDN_C = 8, 128, 256, 64
GDN_QK, GDN_V, GDN_CONV_CH, GDN_IN = 1024, 2048, 4096, 6160
GDN_IN_PAD = 6272
MLA_H, MLA_NOPE, MLA_ROPE, MLA_V = 16, 128, 64, 128
MLA_Q_LORA, MLA_KV_LORA, MLA_WIDTH, MLA_IN = 768, 512, 2048, 3392
MLA_IN_PAD = 4096
MLA_SCALE = (MLA_NOPE + MLA_ROPE) ** -0.5
ROPE_THETA = 10000.0
ADAM_LR, ADAM_B1, ADAM_B2, ADAM_EPS, ADAM_WD, ADAM_STEP = 0.001, 0.9, 0.999, 1e-08, 0.01, 10

VMEM_LIMIT_V7X = 56 * 1024 * 1024
LANES = 128
MESH = pl.DeviceIdType.MESH


def _pc(body, **kw):
    return pl.pallas_call(body, **kw)


def _cparams(sem):
    return pltpu.CompilerParams(dimension_semantics=sem, vmem_limit_bytes=VMEM_LIMIT_V7X)


def _tile(n, cap):
    t = (cap // LANES) * LANES
    while t >= LANES:
        if n % t == 0:
            return t
        t -= LANES
    return n


def _sds(shape, dt):
    return jax.ShapeDtypeStruct(shape, dt)


def mm(a, b, *, ta=False, tb=False, add=None, out_dtype=F32, name):
    if ta:
        K, M = a.shape
    else:
        M, K = a.shape
    if tb:
        N, K2 = b.shape
    else:
        K2, N = b.shape
    assert K == K2, (a.shape, b.shape, ta, tb)
    tm, tn, tk = _tile(M, 1024), _tile(N, 1024), _tile(K, 1024)
    nk = K // tk
    a_spec = pl.BlockSpec((tk, tm), lambda i, j, k: (k, i)) if ta else pl.BlockSpec((tm, tk), lambda i, j, k: (i, k))
    b_spec = pl.BlockSpec((tn, tk), lambda i, j, k: (j, k)) if tb else pl.BlockSpec((tk, tn), lambda i, j, k: (k, j))
    o_spec = pl.BlockSpec((tm, tn), lambda i, j, k: (i, j))
    dn = (((0 if ta else 1,), (1 if tb else 0,)), ((), ()))
    has_add = add is not None

    def body(*refs):
        a_ref, b_ref = refs[0], refs[1]
        o_ref, acc = refs[-2], refs[-1]
        k = pl.program_id(2)

        @pl.when(k == 0)
        def _():
            acc[...] = jnp.zeros_like(acc)

        acc[...] += lax.dot_general(a_ref[...].astype(BF16), b_ref[...].astype(BF16), dn, preferred_element_type=F32)

        @pl.when(k == nk - 1)
        def _():
            r = acc[...]
            if has_add:
                r = r + refs[2][...]
            o_ref[...] = r.astype(out_dtype)

    ins = [a, b] + ([add] if has_add else [])
    specs = [a_spec, b_spec] + ([o_spec] if has_add else [])
    return _pc(body, grid=(M // tm, N // tn, nk), in_specs=specs, out_specs=o_spec, out_shape=_sds((M, N), out_dtype),
               scratch_shapes=[pltpu.VMEM((tm, tn), F32)], compiler_params=_cparams(("parallel", "parallel", "arbitrary")),
               name=name)(*ins)


def gmm(kind, a, b, *, G, name, out_dtype=F32):
    S_ = a.shape[0]
    Ka = a.shape[1] // G
    if kind == "tn":
        N = b.shape[1] // G
        tk = _tile(S_, 512)
        nk = S_ // tk

        def body(a_ref, b_ref, o_ref, acc):
            k = pl.program_id(1)

            @pl.when(k == 0)
            def _():
                acc[...] = jnp.zeros_like(acc)

            acc[...] += lax.dot_general(a_ref[...].astype(BF16), b_ref[...].astype(BF16), (((0,), (0,)), ((), ())),
                                        preferred_element_type=F32)

            @pl.when(k == nk - 1)
            def _():
                o_ref[...] = acc[...].astype(out_dtype)

        return _pc(body, grid=(G, nk),
                   in_specs=[pl.BlockSpec((tk, Ka), lambda g, k: (k, g)), pl.BlockSpec((tk, N), lambda g, k: (k, g))],
                   out_specs=pl.BlockSpec((None, Ka, N), lambda g, k: (g, 0, 0)), out_shape=_sds((G, Ka, N), out_dtype),
                   scratch_shapes=[pltpu.VMEM((Ka, N), F32)], compiler_params=_cparams(("parallel", "arbitrary")), name=name)(a, b)
    N = b.shape[2] if kind == "nn" else b.shape[1]
    tm = _tile(S_, 1024)
    dn = (((1,), (0 if kind == "nn" else 1,)), ((), ()))

    def body(a_ref, b_ref, o_ref):
        o_ref[...] = lax.dot_general(a_ref[...].astype(BF16), b_ref[...].astype(BF16), dn, preferred_element_type=F32)

    bshape = (None,) + tuple(b.shape[1:])
    return _pc(body, grid=(G, S_ // tm),
               in_specs=[pl.BlockSpec((tm, Ka), lambda g, i: (i, g)), pl.BlockSpec(bshape, lambda g, i: (g, 0, 0))],
               out_specs=pl.BlockSpec((tm, N), lambda g, i: (i, g)), out_shape=_sds((S_, G * N), F32),
               compiler_params=_cparams(("parallel", "parallel")), name=name)(a, b)


def _rw_spec(ts, w, c, s):
    return pl.BlockSpec((ts, w), lambda j, i: (i, c + j * s))


def _rw_pspec(p, w, c, s):
    return pl.BlockSpec((p.shape[0], w), lambda j, i: (0, c + j * s))


def rowwise(f, tiles, params, outs, *, ncol=1, ts, name):
    S_ = tiles[0][0].shape[0]
    nin = len(tiles) + len(params)

    def body(*refs):
        res = f(pl.program_id(0), *[r[...] for r in refs[:nin]])
        for r, o in zip(refs[nin:], res):
            r[...] = o.astype(r.dtype)

    return _pc(body, grid=(ncol, S_ // ts),
               in_specs=[_rw_spec(ts, w, c, s) for (_, w, c, s) in tiles] + [_rw_pspec(*p) for p in params],
               out_specs=[_rw_spec(ts, w, 0, s) for (w, s, _) in outs],
               out_shape=[_sds((S_, w * (ncol if s else 1)), dt) for (w, s, dt) in outs],
               compiler_params=_cparams(("parallel", "parallel")), name=name)(*[t[0] for t in tiles], *[p[0] for p in params])


def rowwise_bwd(f, tiles, params, cots, *, need, adds=None, place=None, ncol=1, ts, name):
    S_ = tiles[0][0].shape[0]
    adds = adds or {}
    place = place or {}
    nt, npar, nc = len(tiles), len(params), len(cots)
    add_keys = sorted(adds)
    need_idx = [k for k in range(nt) if need[k]]
    into_keys = [k for k in need_idx if k in place and not isinstance(place[k][0], int)]
    n_extra = len(add_keys) + len(into_keys)

    def body(*refs):
        j, i = pl.program_id(0), pl.program_id(1)
        vals = [r[...] for r in refs[:nt + npar]]
        cvals = tuple(r[...] for r in refs[nt + npar:nt + npar + nc])
        add_refs = refs[nt + npar + nc:nt + npar + nc + len(add_keys)]
        out_refs = refs[nt + npar + nc + n_extra:]
        _, vjp = jax.vjp(lambda *v: tuple(f(j, *v)), *vals)
        grads = vjp(cvals)
        for n, k in enumerate(need_idx):
            g = grads[k]
            if k in adds:
                g = g + add_refs[add_keys.index(k)][...]
            out_refs[n][...] = g
        for n in range(npar):
            ref = out_refs[len(need_idx) + n]
            first = (i == 0) if params[n][3] else jnp.logical_and(i == 0, j == 0)

            @pl.when(first)
            def _():
                ref[...] = jnp.zeros_like(ref)

            ref[...] += grads[nt + n]

    in_specs = ([_rw_spec(ts, w, c, s) for (_, w, c, s) in tiles] + [_rw_pspec(*p) for p in params]
                + [_rw_spec(ts, w, c, s) for (_, w, c, s) in cots] + [_rw_spec(ts, *adds[k][1:]) for k in add_keys]
                + [pl.BlockSpec(memory_space=pl.ANY) for _ in into_keys])
    out_specs, out_shape, aliases = [], [], {}
    for n, k in enumerate(need_idx):
        w, s = tiles[k][1], tiles[k][3]
        if k in place:
            dst, c0 = place[k]
            total = dst if isinstance(dst, int) else dst.shape[1]
            out_specs.append(_rw_spec(ts, w, c0, s))
            out_shape.append(_sds((S_, total), F32))
            if k in into_keys:
                aliases[nt + npar + nc + len(add_keys) + into_keys.index(k)] = n
        else:
            out_specs.append(_rw_spec(ts, w, 0, s))
            out_shape.append(_sds((S_, w * (ncol if s else 1)), F32))
    out_specs += [_rw_pspec(p[0], p[1], p[2], p[3]) for p in params]
    out_shape += [_sds(p[0].shape, F32) for p in params]
    res = _pc(body, grid=(ncol, S_ // ts), in_specs=in_specs, out_specs=out_specs, out_shape=out_shape,
              input_output_aliases=aliases, compiler_params=_cparams(("arbitrary", "arbitrary")), name=name)(
        *[t[0] for t in tiles], *[p[0] for p in params], *[c[0] for c in cots], *[adds[k][0] for k in add_keys],
        *[place[k][0] for k in into_keys])
    return list(res[:len(need_idx)]), list(res[len(need_idx):])


def _rms(x, g):
    r = lax.rsqrt(jnp.mean(x * x, axis=-1, keepdims=True) + EPS)
    return x * r * g


def _silu(x):
    return x * jax.nn.sigmoid(x)


@jax.custom_vjp
def _softplus(x):
    return jnp.maximum(x, 0.0) + jnp.log1p(jnp.exp(-jnp.abs(x)))


_softplus.defvjp(lambda x: (_softplus(x), x), lambda x, d: (d * jax.nn.sigmoid(x),))


def f_rms(j, x, g):
    return (_rms(x, g),)


def f_pool_gate(j, pg, gate, scale):
    return (pg * scale * _silu(gate),)


def f_ogate(j, o, gate):
    return (o * _silu(gate),)


def f_gdn_out(j, o, gate, g):
    return (_rms(o, g) * _silu(gate),)


def f_gdn_gates(j, ba, alog, dtb):
    lane = lax.broadcasted_iota(I32, (1, LANES), 1)
    gs, bs = [], []
    for h in range(GDN_H):
        eb = (lane == h).astype(F32)
        ea = (lane == GDN_H + h).astype(F32)
        b = jnp.sum(ba * eb, -1, keepdims=True)
        a = jnp.sum(ba * ea, -1, keepdims=True)
        al = jnp.sum(alog * eb, -1, keepdims=True)
        dt = jnp.sum(dtb * eb, -1, keepdims=True)
        g = -jnp.exp(al) * _softplus(a + dt)
        gs.append(jnp.broadcast_to(g, ba.shape))
        bs.append(jnp.broadcast_to(jax.nn.sigmoid(b), ba.shape))
    return jnp.concatenate(gs, 1), jnp.concatenate(bs, 1)


def _shift_dn(x, k):
    rows = lax.broadcasted_iota(I32, x.shape, 0)
    return jnp.where(rows < k, 0.0, pltpu.roll(x, k, 0))


def _shift_up(x, k):
    n = x.shape[0]
    rows = lax.broadcasted_iota(I32, x.shape, 0)
    return jnp.where(rows >= n - k, 0.0, pltpu.roll(x, n - k, 0))


def _pool_window(j):
    g = lax.div(j, POOL_GROUP // LANES)
    return jnp.where(g == 0, 2.0, jnp.where(g == 1, 4.0, jnp.where(g == 2, 8.0, 16.0))), g


def _pick(g, a2, a4, a8, a16):
    return jnp.where(g == 0, a2, jnp.where(g == 1, a4, jnp.where(g == 2, a8, a16)))


def pool_time_fwd(proj, name):
    S_ = proj.shape[0]

    def body(u_ref, p_ref):
        u = u_ref[...]
        w, g = _pool_window(pl.program_id(0))
        s2 = u + _shift_dn(u, 1)
        s4 = s2 + _shift_dn(s2, 2)
        s8 = s4 + _shift_dn(s4, 4)
        s16 = s8 + _shift_dn(s8, 8)
        t1 = (lax.broadcasted_iota(I32, u.shape, 0) + 1).astype(F32)
        p_ref[...] = (_pick(g, s2, s4, s8, s16) / jnp.minimum(t1, w) - u).astype(p_ref.dtype)

    return _pc(body, grid=(POOL_WIDTH // LANES,), in_specs=[pl.BlockSpec((S_, LANES), lambda j: (0, j))],
               out_specs=pl.BlockSpec((S_, LANES), lambda j: (0, j)), out_shape=_sds((S_, POOL_WIDTH), BF16),
               compiler_params=_cparams(("parallel",)), name=name)(proj)


def pool_time_bwd(dp, into, name):
    S_ = dp.shape[0]

    def body(dp_ref, _, du_ref):
        d = dp_ref[...]
        w, g = _pool_window(pl.program_id(0))
        t1 = (lax.broadcasted_iota(I32, d.shape, 0) + 1).astype(F32)
        q = d / jnp.minimum(t1, w)
        r2 = q + _shift_up(q, 1)
        r4 = r2 + _shift_up(r2, 2)
        r8 = r4 + _shift_up(r4, 4)
        r16 = r8 + _shift_up(r8, 8)
        du_ref[...] = _pick(g, r2, r4, r8, r16) - d

    return _pc(body, grid=(POOL_WIDTH // LANES,),
               in_specs=[pl.BlockSpec((S_, LANES), lambda j: (0, j)), pl.BlockSpec(memory_space=pl.ANY)],
               out_specs=pl.BlockSpec((S_, LANES), lambda j: (0, j)), out_shape=_sds(into.shape, F32),
               input_output_aliases={1: 0}, compiler_params=_cparams(("parallel",)), name=name)(dp, into)


def _conv_post(j, a):
    n = a * lax.rsqrt(jnp.sum(a * a, axis=-1, keepdims=True) + EPS)
    nq = GDN_QK // LANES
    return jnp.where(j < nq, n * (GDN_DK ** -0.5), jnp.where(j < 2 * nq, n, a))


def _conv_pre(u, w):
    return w[3:4] * u + w[2:3] * _shift_dn(u, 1) + w[1:2] * _shift_dn(u, 2) + w[0:1] * _shift_dn(u, 3)


def gdn_conv_fwd(proj, conv_w, name):
    S_ = proj.shape[0]

    def body(u_ref, w_ref, o_ref):
        o_ref[...] = _conv_post(pl.program_id(0), _silu(_conv_pre(u_ref[...], w_ref[...])))

    return _pc(body, grid=(GDN_CONV_CH // LANES,),
               in_specs=[pl.BlockSpec((S_, LANES), lambda j: (0, j)), pl.BlockSpec((8, LANES), lambda j: (0, j))],
               out_specs=pl.BlockSpec((S_, LANES), lambda j: (0, j)), out_shape=_sds((S_, GDN_CONV_CH), F32),
               compiler_params=_cparams(("parallel",)), name=name)(proj, conv_w)


def gdn_conv_bwd(proj, conv_w, dq, dk, dv, into, name):
    S_ = proj.shape[0]
    nq = GDN_QK // LANES

    def body(u_ref, w_ref, dq_ref, dk_ref, dv_ref, _, du_ref, dw_ref):
        j = pl.program_id(0)
        u, w = u_ref[...], w_ref[...]
        c = _conv_pre(u, w)
        sig = jax.nn.sigmoid(c)
        dout = jnp.where(j < nq, dq_ref[...], jnp.where(j < 2 * nq, dk_ref[...], dv_ref[...]))
        _, vjp = jax.vjp(lambda a: _conv_post(j, a), c * sig)
        dc = vjp(dout)[0] * (sig * (1.0 + c * (1.0 - sig)))
        du_ref[...] = w[3:4] * dc + w[2:3] * _shift_up(dc, 1) + w[1:2] * _shift_up(dc, 2) + w[0:1] * _shift_up(dc, 3)
        rows = lax.broadcasted_iota(I32, (8, LANES), 0)
        dw = jnp.zeros((8, LANES), F32)
        for k in range(4):
            us = u if k == 3 else _shift_dn(u, 3 - k)
            dw = dw + jnp.where(rows == k, jnp.sum(dc * us, axis=0, keepdims=True), 0.0)
        dw_ref[...] = dw

    blk = lambda f: pl.BlockSpec((S_, LANES), f)
    return _pc(body, grid=(GDN_CONV_CH // LANES,),
               in_specs=[blk(lambda j: (0, j)), pl.BlockSpec((8, LANES), lambda j: (0, j)),
                         blk(lambda j: (0, jnp.minimum(j, nq - 1))), blk(lambda j: (0, jnp.clip(j - nq, 0, nq - 1))),
                         blk(lambda j: (0, jnp.clip(j - 2 * nq, 0, 2 * nq - 1))), pl.BlockSpec(memory_space=pl.ANY)],
               out_specs=[blk(lambda j: (0, j)), pl.BlockSpec((8, LANES), lambda j: (0, j))],
               out_shape=[_sds(into.shape, F32), _sds((8, GDN_CONV_CH), F32)], input_output_aliases={5: 0},
               compiler_params=_cparams(("parallel",)), name=name)(proj, conv_w, dq, dk, dv, into)


_NN, _NT, _TN = ((1,), (0,)), ((1,), (1,)), ((0,), (0,))


def _split(x, n):
    parts = []
    for _ in range(n):
        h = x.astype(BF16)
        parts.append(h)
        x = x - h.astype(F32)
    return parts


def _dot(a, b, dn, mode):
    d = lambda p, q: lax.dot_general(p, q, (dn, ((), ())), preferred_element_type=F32)
    if mode == "lo":
        return d(a.astype(BF16), b.astype(BF16))
    if mode == "x3":
        (ah, al), (bh, bl) = _split(a, 2), _split(b, 2)
        return d(ah, bh) + (d(ah, bl) + d(al, bh))
    b0, b1, b2 = _split(b, 3)
    ab = a.astype(BF16)
    return d(ab, b0) + (d(ab, b1) + d(ab, b2))


def _make_dots(mode):
    @jax.custom_vjp
    def nn(a, b):
        return _dot(a, b, _NN, mode)

    @jax.custom_vjp
    def nt(a, b):
        return _dot(a, b, _NT, mode)

    @jax.custom_vjp
    def tn(a, b):
        return _dot(a, b, _TN, mode)

    nn.defvjp(lambda a, b: (nn(a, b), (a, b)), lambda r, d: (nt(d, r[1]), tn(r[0], d)))
    nt.defvjp(lambda a, b: (nt(a, b), (a, b)), lambda r, d: (nn(d, r[1]), tn(d, r[0])))
    tn.defvjp(lambda a, b: (tn(a, b), (a, b)), lambda r, d: (nt(r[1], d), nn(r[0], d)))
    return nn, nt, tn


_nn_hi, _nt_hi, _tn_hi = _make_dots("x3")
_nn_lo, _nt_lo, _tn_lo = _make_dots("lo")


@jax.custom_vjp
def _nn_const(a, b):
    return _dot(a, b, _NN, "xl")


_nn_const.defvjp(lambda a, b: (_nn_const(a, b), a), lambda a, d: (jnp.zeros_like(a), _dot(a, d, _TN, "xl")))


def _each(f, *lists):
    return [f(*xs) for xs in zip(*lists)]


def _gdn_chunk(q, k, v, gb, bb, state):
    C = GDN_C
    e0 = (lax.broadcasted_iota(I32, (1, LANES), 1) == 0).astype(F32)
    ri = lax.broadcasted_iota(I32, (C, C), 0)
    ci = lax.broadcasted_iota(I32, (C, C), 1)
    causal, strict = ri >= ci, ri > ci
    tri, eye, ones = causal.astype(F32), (ri == ci).astype(F32), jnp.ones((C, C), F32)
    last = lax.broadcasted_iota(I32, (C, LANES), 0) == C - 1
    g1 = _each(lambda a: jnp.sum(a * e0, -1, keepdims=True), gb)
    b1 = _each(lambda a: jnp.sum(a * e0, -1, keepdims=True), bb)
    gc_c = _each(lambda g: _nn_const(tri, jnp.broadcast_to(g, (C, C))), g1)
    gc_d = _each(lambda g: _nn_const(tri, jnp.broadcast_to(g, (C, LANES))), g1)
    gr_c = _each(lambda g: _nn_const(ones, eye * g), gc_c)
    decay = _each(lambda a, r: jnp.where(causal, jnp.exp(jnp.where(causal, a - r, 0.0)), 0.0), gc_c, gr_c)
    kb = _each(lambda a, b: a * b, k, b1)
    vb = _each(lambda a, b: a * b, v, b1)
    x = _each(lambda a, b, d: -jnp.where(strict, _nt_lo(a, b) * d, 0.0), kb, k, decay)
    ainv, p = _each(lambda a: eye + a, x), x
    for _ in range(5):
        p = _each(lambda a: _nn_hi(a, a), p)
        ainv = _each(lambda a, b: a + _nn_hi(a, b), ainv, p)
    u = _each(_nn_hi, ainv, vb)
    w = _each(lambda a, b, g: _nn_hi(a, b * jnp.exp(g)), ainv, kb, gc_d)
    attn = _each(lambda a, b, d: jnp.where(causal, _nt_lo(a, b) * d, 0.0), q, k, decay)
    v_new = _each(lambda a, b, s: a - _nn_lo(b, s), u, w, state)
    o = _each(lambda a, g, s, t, vn: _nn_lo(a * jnp.exp(g), s) + _nn_lo(t, vn), q, gc_d, state, attn, v_new)
    gl = _each(lambda g: jnp.sum(jnp.where(last, g, 0.0), axis=0, keepdims=True), gc_d)
    new_state = _each(lambda s, g, a, gd, vn: s * jnp.exp(jnp.sum(g * e0, -1, keepdims=True)) + _tn_lo(a * jnp.exp(g - gd), vn),
                      state, gl, k, gc_d, v_new)
    return o, new_state


def _head_slices(ref, width):
    return [ref[:, h * width:(h + 1) * width] for h in range(GDN_H)]


def gdn_chunk_fwd(qkv, g_b, beta_b, name):
    S_ = qkv.shape[0]
    N = S_ // GDN_C

    def body(q_ref, k_ref, v_ref, g_ref, b_ref, o_ref, s_ref, state):
        @pl.when(pl.program_id(0) == 0)
        def _():
            state[...] = jnp.zeros_like(state)

        st = [state[h] for h in range(GDN_H)]
        s_ref[0] = state[...]
        o, st2 = _gdn_chunk(_head_slices(q_ref, GDN_DK), _head_slices(k_ref, GDN_DK), _head_slices(v_ref, GDN_DV),
                            _head_slices(g_ref, GDN_DK), _head_slices(b_ref, GDN_DK), st)
        for h in range(GDN_H):
            o_ref[:, h * GDN_DV:(h + 1) * GDN_DV] = o[h]
            state[h] = st2[h]

    return _pc(body, grid=(N,),
               in_specs=[pl.BlockSpec((GDN_C, GDN_QK), lambda n: (n, 0)), pl.BlockSpec((GDN_C, GDN_QK), lambda n: (n, 1)),
                         pl.BlockSpec((GDN_C, GDN_V), lambda n: (n, 1)), pl.BlockSpec((GDN_C, GDN_QK), lambda n: (n, 0)),
                         pl.BlockSpec((GDN_C, GDN_QK), lambda n: (n, 0))],
               out_specs=[pl.BlockSpec((GDN_C, GDN_V), lambda n: (n, 0)),
                          pl.BlockSpec((1, GDN_H, GDN_DK, GDN_DV), lambda n: (n, 0, 0, 0))],
               out_shape=[_sds((S_, GDN_V), F32), _sds((N, GDN_H, GDN_DK, GDN_DV), F32)],
               scratch_shapes=[pltpu.VMEM((GDN_H, GDN_DK, GDN_DV), F32)],
               compiler_params=_cparams(("arbitrary",)), name=name)(qkv, qkv, qkv, g_b, beta_b)


def gdn_chunk_bwd(qkv, g_b, beta_b, states, do, name):
    S_ = qkv.shape[0]
    N = S_ // GDN_C

    def body(q_ref, k_ref, v_ref, g_ref, b_ref, s_ref, do_ref, dq_ref, dk_ref, dv_ref, dg_ref, db_ref, dstate):
        @pl.when(pl.program_id(0) == 0)
        def _():
            dstate[...] = jnp.zeros_like(dstate)

        _, vjp = jax.vjp(_gdn_chunk, _head_slices(q_ref, GDN_DK), _head_slices(k_ref, GDN_DK), _head_slices(v_ref, GDN_DV),
                         _head_slices(g_ref, GDN_DK), _head_slices(b_ref, GDN_DK), [s_ref[0, h] for h in range(GDN_H)])
        dq, dk, dv, dg, db, ds = vjp((_head_slices(do_ref, GDN_DV), [dstate[h] for h in range(GDN_H)]))
        for h in range(GDN_H):
            kk, vv = slice(h * GDN_DK, (h + 1) * GDN_DK), slice(h * GDN_DV, (h + 1) * GDN_DV)
            dq_ref[:, kk] = dq[h]
            dk_ref[:, kk] = dk[h]
            dv_ref[:, vv] = dv[h]
            dg_ref[:, kk] = dg[h]
            db_ref[:, kk] = db[h]
            dstate[h] = ds[h]

    r = lambda n: N - 1 - n
    qk = lambda c: pl.BlockSpec((GDN_C, GDN_QK), lambda n: (r(n), c))
    vs = lambda c: pl.BlockSpec((GDN_C, GDN_V), lambda n: (r(n), c))
    return _pc(body, grid=(N,),
               in_specs=[qk(0), qk(1), vs(1), qk(0), qk(0),
                         pl.BlockSpec((1, GDN_H, GDN_DK, GDN_DV), lambda n: (r(n), 0, 0, 0)), vs(0)],
               out_specs=[qk(0), qk(0), vs(0), qk(0), qk(0)],
               out_shape=[_sds((S_, GDN_QK), F32), _sds((S_, GDN_QK), F32), _sds((S_, GDN_V), F32),
                          _sds((S_, GDN_QK), F32), _sds((S_, GDN_QK), F32)],
               scratch_shapes=[pltpu.VMEM((GDN_H, GDN_DK, GDN_DV), F32)],
               compiler_params=_cparams(("arbitrary",)), name=name)(qkv, qkv, qkv, g_b, beta_b, states, do)


def _rope_tables(pos_ref, inv_ref, cm_ref, sg_ref):
    ang = pos_ref[...] * inv_ref[...]
    return jnp.cos(ang) * cm_ref[...], jnp.sin(ang) * sg_ref[...]


def mla_prep_fwd(qpad, kv, proj, pos, rope_consts, name):
    S_ = qpad.shape[0]
    ts = 256
    W = 2 * LANES

    def body(q_ref, kv_ref, kr_ref, pos_ref, inv_ref, cm_ref, sg_ref, qh_ref, kh_ref, vh_ref):
        cs, sn = _rope_tables(pos_ref, inv_ref, cm_ref, sg_ref)
        rope = lambda r: r * cs + pltpu.roll(r, LANES // 2, 1) * sn
        krr = rope(kr_ref[...]).astype(BF16)
        for h in range(MLA_H):
            qh_ref[h, :, 0:LANES] = (q_ref[:, h * W:h * W + LANES] * MLA_SCALE).astype(BF16)
            qh_ref[h, :, LANES:W] = (rope(q_ref[:, h * W + LANES:(h + 1) * W]) * MLA_SCALE).astype(BF16)
            kh_ref[h, :, 0:LANES] = kv_ref[:, h * W:h * W + LANES].astype(BF16)
            kh_ref[h, :, LANES:W] = krr
            vh_ref[h] = kv_ref[:, h * W + LANES:(h + 1) * W].astype(BF16)

    one = pl.BlockSpec((1, LANES), lambda i: (0, 0))
    return _pc(body, grid=(S_ // ts,),
               in_specs=[pl.BlockSpec((ts, MLA_H * W), lambda i: (i, 0)), pl.BlockSpec((ts, MLA_H * W), lambda i: (i, 0)),
                         pl.BlockSpec((ts, LANES), lambda i: (i, 1536 // LANES)), pl.BlockSpec((ts, 1), lambda i: (i, 0)),
                         one, one, one],
               out_specs=[pl.BlockSpec((MLA_H, ts, W), lambda i: (0, i, 0)), pl.BlockSpec((MLA_H, ts, W), lambda i: (0, i, 0)),
                          pl.BlockSpec((MLA_H, ts, LANES), lambda i: (0, i, 0))],
               out_shape=[_sds((MLA_H, S_, W), BF16), _sds((MLA_H, S_, W), BF16), _sds((MLA_H, S_, LANES), BF16)],
               compiler_params=_cparams(("parallel",)), name=name)(qpad, kv, proj, pos, *rope_consts)


def mla_prep_bwd(dqh, dkh, dvh, pos, rope_consts, into, name):
    S_ = dqh.shape[1]
    ts = 256
    W = 2 * LANES

    def body(dq_ref, dk_ref, dv_ref, pos_ref, inv_ref, cm_ref, sg_ref, _, dqp_ref, dkv_ref, dkr_ref):
        cs, sn = _rope_tables(pos_ref, inv_ref, cm_ref, sg_ref)
        rope_t = lambda g: g * cs + pltpu.roll(g * sn, LANES // 2, 1)
        acc = jnp.zeros((ts, LANES), F32)
        for h in range(MLA_H):
            dqp_ref[:, h * W:h * W + LANES] = dq_ref[h, :, 0:LANES] * MLA_SCALE
            dqp_ref[:, h * W + LANES:(h + 1) * W] = rope_t(dq_ref[h, :, LANES:W]) * MLA_SCALE
            dkv_ref[:, h * W:h * W + LANES] = dk_ref[h, :, 0:LANES]
            dkv_ref[:, h * W + LANES:(h + 1) * W] = dv_ref[h]
            acc = acc + dk_ref[h, :, LANES:W]
        dkr_ref[...] = rope_t(acc)

    one = pl.BlockSpec((1, LANES), lambda i: (0, 0))
    return _pc(body, grid=(S_ // ts,),
               in_specs=[pl.BlockSpec((MLA_H, ts, W), lambda i: (0, i, 0)), pl.BlockSpec((MLA_H, ts, W), lambda i: (0, i, 0)),
                         pl.BlockSpec((MLA_H, ts, LANES), lambda i: (0, i, 0)), pl.BlockSpec((ts, 1), lambda i: (i, 0)),
                         one, one, one, pl.BlockSpec(memory_space=pl.ANY)],
               out_specs=[pl.BlockSpec((ts, MLA_H * W), lambda i: (i, 0)), pl.BlockSpec((ts, MLA_H * W), lambda i: (i, 0)),
                          pl.BlockSpec((ts, LANES), lambda i: (i, 1536 // LANES))],
               out_shape=[_sds((S_, MLA_H * W), F32), _sds((S_, MLA_H * W), F32), _sds(into.shape, F32)],
               input_output_aliases={7: 2}, compiler_params=_cparams(("parallel",)), name=name)(dqh, dkh, dvh, pos, *rope_consts, into)


NEG = -1e30


FLASH_TILE = 1024


def _scores(q, k, diagonal):
    s = lax.dot_general(q, k, (_NT, ((), ())), preferred_element_type=F32)
    if not diagonal:
        return s
    t = s.shape[0]
    return jnp.where(lax.broadcasted_iota(I32, (t, t), 1) <= lax.broadcasted_iota(I32, (t, t), 0), s, NEG)


def flash_fwd(qh, kh, vh, name):
    H, S_, W = qh.shape
    t = _tile(S_, FLASH_TILE)
    n = S_ // t

    def body(q_ref, k_ref, v_ref, o_ref, lse_ref, m_s, l_s, acc):
        qi, kj = pl.program_id(1), pl.program_id(2)

        @pl.when(kj == 0)
        def _():
            m_s[...] = jnp.full_like(m_s, NEG)
            l_s[...] = jnp.zeros_like(l_s)
            acc[...] = jnp.zeros_like(acc)

        def step(diagonal):
            s = _scores(q_ref[...], k_ref[...], diagonal)
            m_old = m_s[...]
            m_new = jnp.maximum(m_old, jnp.max(s, axis=-1, keepdims=True))
            alpha = jnp.exp(m_old - m_new)
            p = jnp.exp(s - m_new[:, :1])
            l_s[...] = alpha * l_s[...] + jnp.sum(p, axis=-1, keepdims=True)
            acc[...] = alpha * acc[...] + lax.dot_general(p.astype(BF16), v_ref[...], (_NN, ((), ())), preferred_element_type=F32)
            m_s[...] = m_new

        pl.when(kj < qi)(lambda: step(False))
        pl.when(kj == qi)(lambda: step(True))

        @pl.when(kj == n - 1)
        def _():
            o_ref[...] = acc[...] / l_s[...]
            lse_ref[...] = m_s[...] + jnp.log(l_s[...])

    return _pc(body, grid=(H, n, n),
               in_specs=[pl.BlockSpec((None, t, W), lambda h, i, j: (h, i, 0)),
                         pl.BlockSpec((None, t, W), lambda h, i, j: (h, jnp.minimum(i, j), 0)),
                         pl.BlockSpec((None, t, LANES), lambda h, i, j: (h, jnp.minimum(i, j), 0))],
               out_specs=[pl.BlockSpec((t, LANES), lambda h, i, j: (i, h)), pl.BlockSpec((None, t, LANES), lambda h, i, j: (h, i, 0))],
               out_shape=[_sds((S_, H * LANES), F32), _sds((H, S_, LANES), F32)],
               scratch_shapes=[pltpu.VMEM((t, LANES), F32)] * 3,
               compiler_params=_cparams(("parallel", "parallel", "arbitrary")), name=name)(qh, kh, vh)


def flash_bwd(qh, kh, vh, o, lse, do, name):
    H, S_, W = qh.shape
    t = _tile(S_, FLASH_TILE)
    n = S_ // t

    def body(q_ref, k_ref, v_ref, o_ref, lse_ref, do_ref, dq_ref, dk_ref, dv_ref):
        kj, qi = pl.program_id(1), pl.program_id(2)

        @pl.when(jnp.logical_and(kj == 0, qi == 0))
        def _():
            dq_ref[...] = jnp.zeros_like(dq_ref)

        @pl.when(qi == 0)
        def _():
            dk_ref[...] = jnp.zeros_like(dk_ref)
            dv_ref[...] = jnp.zeros_like(dv_ref)

        def step(diagonal):
            q, k, v = q_ref[...], k_ref[...], v_ref[...]
            do_ = do_ref[...]
            p = jnp.exp(_scores(q, k, diagonal) - lse_ref[:, :1])
            dob = do_.astype(BF16)
            dv_ref[...] += lax.dot_general(p.astype(BF16), dob, (_TN, ((), ())), preferred_element_type=F32)
            dp = lax.dot_general(dob, v, (_NT, ((), ())), preferred_element_type=F32)
            delta = jnp.sum(do_ * o_ref[...], axis=-1, keepdims=True)
            ds = (p * (dp - delta)).astype(BF16)
            dk_ref[...] += lax.dot_general(ds, q, (_TN, ((), ())), preferred_element_type=F32)
            rows = pl.ds(pl.multiple_of(qi * t, t), t)
            dq_ref[rows, :] += lax.dot_general(ds, k, (_NN, ((), ())), preferred_element_type=F32)

        pl.when(qi > kj)(lambda: step(False))
        pl.when(qi == kj)(lambda: step(True))

    qrow = lambda h, j, i: jnp.maximum(i, j)
    return _pc(body, grid=(H, n, n),
               in_specs=[pl.BlockSpec((None, t, W), lambda h, j, i: (h, qrow(h, j, i), 0)),
                         pl.BlockSpec((None, t, W), lambda h, j, i: (h, j, 0)),
                         pl.BlockSpec((None, t, LANES), lambda h, j, i: (h, j, 0)),
                         pl.BlockSpec((t, LANES), lambda h, j, i: (qrow(h, j, i), h)),
                         pl.BlockSpec((None, t, LANES), lambda h, j, i: (h, qrow(h, j, i), 0)),
                         pl.BlockSpec((t, LANES), lambda h, j, i: (qrow(h, j, i), h))],
               out_specs=[pl.BlockSpec((None, S_, W), lambda h, j, i: (h, 0, 0)),
                          pl.BlockSpec((None, t, W), lambda h, j, i: (h, j, 0)),
                          pl.BlockSpec((None, t, LANES), lambda h, j, i: (h, j, 0))],
               out_shape=[_sds((H, S_, W), F32), _sds((H, S_, W), F32), _sds((H, S_, LANES), F32)],
               compiler_params=_cparams(("parallel", "arbitrary", "arbitrary")), name=name)(qh, kh, vh, o, lse, do)


def loss_head(x, target, g, name):
    S_ = x.shape[0]
    ts = 256

    def body(x_ref, t_ref, g_ref, l_ref, dx_ref, dg_ref):
        @pl.when(pl.program_id(0) == 0)
        def _():
            l_ref[...] = jnp.zeros_like(l_ref)
            dg_ref[...] = jnp.zeros_like(dg_ref)

        y, vjp = jax.vjp(_rms, x_ref[...], g_ref[...])
        err = y - t_ref[...]
        l_ref[...] += 0.5 * jnp.sum(jnp.sum(err * err, axis=-1, keepdims=True), axis=0, keepdims=True) / D
        dx, dg = vjp(err / D)
        dx_ref[...] = dx
        dg_ref[...] += dg

    row = pl.BlockSpec((ts, D), lambda i: (i, 0))
    return _pc(body, grid=(S_ // ts,), in_specs=[row, row, pl.BlockSpec((1, D), lambda i: (0, 0))],
               out_specs=[pl.BlockSpec((1, LANES), lambda i: (0, 0)), row, pl.BlockSpec((1, D), lambda i: (0, 0))],
               out_shape=[_sds((1, LANES), F32), _sds((S_, D), F32), _sds((1, D), F32)],
               compiler_params=_cparams(("arbitrary",)), name=name)(x, target, g)


def adamw(w, parts, m, v, name):
    R, C = w.shape
    rows = [p.shape[1] for p in parts[0]]
    tr = R
    for cand in (512, 256, 128, 64, 32, 16, 8):
        if all(r % cand == 0 for r in rows) and cand * C * 4 <= 1024 * 1024:
            tr = cand
            break
    c1 = 1.0 - ADAM_B1 ** ADAM_STEP
    c2 = 1.0 - ADAM_B2 ** ADAM_STEP
    starts = [sum(rows[:k]) // tr for k in range(len(rows))]
    flat = [p for part in parts for p in part]

    def body(*refs):
        w_ref, m_ref, v_ref = refs[0], refs[1 + len(flat)], refs[2 + len(flat)]
        g_ref, d_ref, nm_ref, nv_ref = refs[3 + len(flat):]
        i = pl.program_id(0)
        gg, at = None, 1
        for part in parts:
            val = None
            for k in range(len(part)):
                p_ref = refs[at]
                at += 1
                s = p_ref[0].astype(F32)
                for n in range(1, p_ref.shape[0]):
                    s = s + p_ref[n].astype(F32)
                val = s if val is None else jnp.where(i >= starts[k], s, val)
            gg = val if gg is None else gg + val
        m2 = ADAM_B1 * m_ref[...] + (1.0 - ADAM_B1) * gg
        v2 = ADAM_B2 * v_ref[...] + (1.0 - ADAM_B2) * (gg * gg)
        g_ref[...] = gg
        d_ref[...] = -ADAM_LR * ((m2 / c1) / (jnp.sqrt(v2 / c2) + ADAM_EPS) + ADAM_WD * w_ref[...])
        nm_ref[...] = m2
        nv_ref[...] = v2

    blk = pl.BlockSpec((tr, C), lambda i: (i, 0))
    piece = lambda p, k: pl.BlockSpec((p.shape[0], tr, C), lambda i: (0, jnp.clip(i - starts[k], 0, rows[k] // tr - 1), 0))
    pblk = [piece(p, k) for part in parts for k, p in enumerate(part)]
    return _pc(body, grid=(R // tr,), in_specs=[blk] + pblk + [blk, blk], out_specs=[blk] * 4, out_shape=[_sds((R, C), F32)] * 4,
               compiler_params=_cparams(("parallel",)), name=name)(w, *flat, m, v)


def sum_slots(own, recv, skip, name):
    n, R, C = recv.shape
    tr = _tile(R, 512) if R % LANES == 0 else R
    has_own = own is not None

    def body(*refs):
        skip_ref = refs[0]
        r_ref, o_ref = refs[-2], refs[-1]
        acc = refs[1][...] if has_own else jnp.zeros(o_ref.shape, F32)
        for s in range(n):
            acc = acc + jnp.where(skip_ref[0] == s, 0.0, r_ref[s].astype(F32))
        o_ref[...] = acc

    row = pl.BlockSpec((tr, C), lambda i, sk: (i, 0))
    gs = pltpu.PrefetchScalarGridSpec(
        num_scalar_prefetch=1, grid=(R // tr,),
        in_specs=([row] if has_own else []) + [pl.BlockSpec((n, tr, C), lambda i, sk: (0, i, 0))], out_specs=row)
    ins = ([own] if has_own else []) + [recv]
    return _pc(body, grid_spec=gs, out_shape=_sds((R, C), F32), compiler_params=_cparams(("parallel",)), name=name)(skip, *ins)


def _chip_peers():
    x, y, c = lax.axis_index("x"), lax.axis_index("y"), lax.axis_index("c")
    return (x, y, c), [(1 - x, y, c), (x, 1 - y, c), (1 - x, 1 - y, c)]


def _chip_index(p):
    return 2 * p[0] + p[1]


def _win(ref, axis, chip, size):
    if axis is None:
        return ref.at[chip]
    idx = [slice(None)] * len(ref.shape)
    idx[axis] = pl.ds(pl.multiple_of(chip * size, size), size)
    return ref.at[tuple(idx)]


def _remote(src, dst, send_sem, recv_sem, peer):
    return pltpu.make_async_remote_copy(src_ref=src, dst_ref=dst, send_sem=send_sem, recv_sem=recv_sem, device_id=peer,
                                        device_id_type=MESH)


HBM_SPEC = pl.BlockSpec(memory_space=pltpu.HBM)
SEM_SPEC = pl.BlockSpec(memory_space=pltpu.SEMAPHORE)
ANY_SPEC = pl.BlockSpec(memory_space=pl.ANY)
DATAFLOW = pltpu.SideEffectType.DATAFLOW_SIDE_EFFECTING


def gather_piece(i, l, o, axis, size):
    return (i, lambda r, chip: r.at[l], o, lambda r, chip: _win(r, axis, chip, size))


def scatter_piece(i, o, axis, size):
    return (i, lambda r, chip: _win(r, axis, chip, size), o, lambda r, chip: r.at[chip])


def _copies(pieces, in_refs, out_refs, send, recv):
    me, peers = _chip_peers()
    mine = _chip_index(me)
    remote = []
    for n, (i, src, o, dst) in enumerate(pieces):
        d = dst(out_refs[o], mine)
        remote += [_remote(src(in_refs[i], _chip_index(p)), d, send.at[3 * n + k], recv.at[3 * n + k], p)
                   for k, p in enumerate(peers)]
    return remote


def own_window(a, axis, size, chip):
    if axis is None:
        return lax.dynamic_index_in_dim(a, chip, 0, keepdims=False)
    return lax.dynamic_slice_in_dim(a, chip * size, size, axis=axis)


def place_own(land, own, axis, size, chip):
    if axis is None:
        return lax.dynamic_update_slice_in_dim(land, own[None], chip, axis=0)
    return lax.dynamic_update_slice_in_dim(land, own, chip * size, axis=axis)


def exchange_start(pieces, ins, out_shapes, after, name):
    n_in, n_out, ncp = len(ins), len(out_shapes), len(pieces)

    def body(*refs):
        in_refs, land_refs = refs[:n_in], refs[n_in:n_in + n_out]
        send, recv = refs[n_in + n_out + 1], refs[n_in + n_out + 2]
        token = refs[-1]
        for cp in _copies(pieces, in_refs, land_refs, send, recv):
            cp.start()
        token[...] = jnp.zeros_like(token)

    hbm = lambda a: pltpu.with_memory_space_constraint(a, pltpu.HBM)
    lands = [hbm(lax.empty(s.shape, s.dtype)) for s in out_shapes]
    sem = pltpu.SemaphoreType.DMA((3 * ncp,))
    thru = [pltpu.HBM(a.shape, a.dtype) for a in ins] + [pltpu.HBM(s.shape, s.dtype) for s in out_shapes]
    res = _pc(body, in_specs=[HBM_SPEC] * (n_in + n_out) + [ANY_SPEC],
              out_specs=[SEM_SPEC, SEM_SPEC] + [HBM_SPEC] * (n_in + n_out) + [pl.BlockSpec(memory_space=pltpu.VMEM)],
              out_shape=[sem, sem] + thru + [_sds((8, LANES), F32)],
              input_output_aliases={i: 2 + i for i in range(n_in + n_out)},
              compiler_params=pltpu.CompilerParams(has_side_effects=DATAFLOW), name=name)(*[hbm(a) for a in ins], *lands, after)
    return (res[0], res[1]), list(res[2:2 + n_in]), list(res[2 + n_in:2 + n_in + n_out]), res[-1]


def exchange_wait(pieces, sems, ins, lands, after, name):
    n_in, n_out = len(ins), len(lands)

    def body(*refs):
        in_refs, land_refs = refs[:n_in], refs[n_in:n_in + n_out]
        send, recv = refs[n_in + n_out], refs[n_in + n_out + 1]
        for cp in _copies(pieces, in_refs, land_refs, send, recv):
            cp.wait_send()
            cp.wait_recv()

    thru = [pltpu.HBM(a.shape, a.dtype) for a in ins] + [pltpu.HBM(a.shape, a.dtype) for a in lands]
    res = _pc(body, in_specs=[HBM_SPEC] * (n_in + n_out) + [SEM_SPEC, SEM_SPEC, ANY_SPEC], out_specs=[HBM_SPEC] * (n_in + n_out),
              out_shape=thru, input_output_aliases={i: i for i in range(n_in + n_out)},
              compiler_params=pltpu.CompilerParams(has_side_effects=DATAFLOW), name=name)(*ins, *lands, sems[0], sems[1], after)
    return list(res[:n_in]), list(res[n_in:])


def swap_cores(bufs, name):
    nb = len(bufs)

    def body(*refs):
        in_refs, out_refs = refs[:nb], refs[nb:2 * nb]
        send, recv = refs[2 * nb:]
        x, y, c = lax.axis_index("x"), lax.axis_index("y"), lax.axis_index("c")
        copies = [_remote(in_refs[b], out_refs[b], send.at[b], recv.at[b], (x, y, 1 - c)) for b in range(nb)]
        for cp in copies:
            cp.start()
        for cp in copies:
            cp.wait()

    anyspec = pl.BlockSpec(memory_space=pl.ANY)
    return _pc(body, in_specs=[anyspec] * nb, out_specs=[anyspec] * nb, out_shape=[_sds(b.shape, b.dtype) for b in bufs],
               scratch_shapes=[pltpu.SemaphoreType.DMA((nb,)), pltpu.SemaphoreType.DMA((nb,))], name=name)(*bufs)


def exchange_all(buf, name):
    def body(in_ref, out_ref, send, recv, local):
        x, y, c = lax.axis_index("x"), lax.axis_index("y"), lax.axis_index("c")
        mine = 4 * x + 2 * y + c
        loc = pltpu.make_async_copy(in_ref, out_ref.at[mine], local)
        loc.start()
        copies = [loc]
        for k in range(1, 8):
            peer = (x ^ (k >> 2), y ^ ((k >> 1) & 1), c ^ (k & 1))
            cp = pltpu.make_async_remote_copy(src_ref=in_ref, dst_ref=out_ref.at[mine], send_sem=send.at[k - 1],
                                              recv_sem=recv.at[k - 1], device_id=peer, device_id_type=MESH)
            cp.start()
            copies.append(cp)
        for cp in copies:
            cp.wait()

    anyspec = pl.BlockSpec(memory_space=pl.ANY)
    return _pc(body, in_specs=[anyspec], out_specs=anyspec, out_shape=_sds((8,) + buf.shape, buf.dtype),
               scratch_shapes=[pltpu.SemaphoreType.DMA((7,)), pltpu.SemaphoreType.DMA((7,)), pltpu.SemaphoreType.DMA],
               name=name)(buf)


def _norm_fwd(x, g, name):
    return rowwise(f_rms, [(x, D, 0, 0)], [(g, D, 0, 0)], [(D, 0, BF16)], ts=512, name=name)[0]


def _norm_bwd(x, g, dh, dres, name):
    (dx,), (dg,) = rowwise_bwd(f_rms, [(x, D, 0, 0)], [(g, D, 0, 0)], [(dh, D, 0, 0)], need=[True],
                               adds={0: (dres, D, 0, 0)}, ts=256, name=name)
    return dx, dg


def pool_fwd(x, W, tag):
    h = _norm_fwd(x, W["ng"], tag + "_norm")
    proj = mm(h, W["w_in"], name=tag + "_in")
    p = pool_time_fwd(proj, tag + "_win")
    pg = gmm("nn", p, W["w_grp"], G=4, name=tag + "_grp")
    y = rowwise(f_pool_gate, [(pg, POOL_GROUP, 0, 1), (proj, POOL_GROUP, 4, 1)], [(W["scale"], POOL_GROUP, 0, 1)],
                [(POOL_GROUP, 1, BF16)], ncol=4, ts=512, name=tag + "_gate")[0]
    xn = mm(y, W["w_out"], add=x, name=tag + "_out")
    return xn, (x, h, proj, p, pg, y)


def pool_bwd(dxn, W, saved, tag):
    x, h, proj, p, pg, y = saved
    dy = mm(dxn, W["w_out"], tb=True, name=tag + "_dy")
    g = {"w_out": mm(y, dxn, ta=True, out_dtype=BF16, name=tag + "_dwout")}
    (dpg, dproj), (g["scale"],) = rowwise_bwd(
        f_pool_gate, [(pg, POOL_GROUP, 0, 1), (proj, POOL_GROUP, 4, 1)], [(W["scale"], POOL_GROUP, 0, 1)],
        [(dy, POOL_GROUP, 0, 1)], need=[True, True], place={1: (2 * POOL_WIDTH, 4)}, ncol=4, ts=512, name=tag + "_dgate")
    dp = gmm("nt", dpg, W["w_grp"], G=4, name=tag + "_dp")
    g["w_grp"] = gmm("tn", p, dpg, G=4, out_dtype=BF16, name=tag + "_dwgrp")
    dproj = pool_time_bwd(dp, dproj, tag + "_dwin")
    dh = mm(dproj, W["w_in"], tb=True, name=tag + "_dh")
    g["w_in"] = mm(h, dproj, ta=True, out_dtype=BF16, name=tag + "_dw_in")
    dx, g["ng"] = _norm_bwd(x, W["ng"], dh, dxn, tag + "_dnorm")
    return dx, g


def gdn_fwd(x, W, tag):
    h = _norm_fwd(x, W["ng"], tag + "_norm")
    proj = mm(h, W["w_in"], name=tag + "_in")
    qkv = gdn_conv_fwd(proj, W["conv"], tag + "_conv")
    g_b, beta_b = rowwise(f_gdn_gates, [(proj, LANES, 6144 // LANES, 0)], [(W["a_log"], LANES, 0, 0), (W["dt_bias"], LANES, 0, 0)],
                          [(GDN_QK, 0, F32), (GDN_QK, 0, F32)], ts=512, name=tag + "_gates")
    o, states = gdn_chunk_fwd(qkv, g_b, beta_b, tag + "_chunk")
    og = rowwise(f_gdn_out, [(o, GDN_DV, 0, 1), (proj, GDN_DV, 4096 // GDN_DV, 1)], [(W["norm_g"], GDN_DV, 0, 0)],
                 [(GDN_DV, 1, BF16)], ncol=GDN_H, ts=512, name=tag + "_onorm")[0]
    xn = mm(og, W["w_out"], add=x, name=tag + "_out")
    return xn, (x, h, proj, qkv, g_b, beta_b, o, states, og)


def gdn_bwd(dxn, W, saved, tag):
    x, h, proj, qkv, g_b, beta_b, o, states, og = saved
    dog = mm(dxn, W["w_out"], tb=True, name=tag + "_dog")
    g = {"w_out": mm(og, dxn, ta=True, out_dtype=BF16, name=tag + "_dwout")}
    (do, dproj), (g["norm_g"],) = rowwise_bwd(
        f_gdn_out, [(o, GDN_DV, 0, 1), (proj, GDN_DV, 4096 // GDN_DV, 1)], [(W["norm_g"], GDN_DV, 0, 0)],
        [(dog, GDN_DV, 0, 1)], need=[True, True], place={1: (GDN_IN_PAD, 4096 // GDN_DV)}, ncol=GDN_H, ts=512, name=tag + "_donorm")
    dq, dk, dv, dg_b, dbeta_b = gdn_chunk_bwd(qkv, g_b, beta_b, states, do, tag + "_dchunk")
    (dproj,), (g["a_log"], g["dt_bias"]) = rowwise_bwd(
        f_gdn_gates, [(proj, LANES, 6144 // LANES, 0)], [(W["a_log"], LANES, 0, 0), (W["dt_bias"], LANES, 0, 0)],
        [(dg_b, GDN_QK, 0, 0), (dbeta_b, GDN_QK, 0, 0)], need=[True], place={0: (dproj, 6144 // LANES)}, ts=256, name=tag + "_dgates")
    dproj, g["conv"] = gdn_conv_bwd(proj, W["conv"], dq, dk, dv, dproj, tag + "_dconv")
    dh = mm(dproj, W["w_in"], tb=True, name=tag + "_dh")
    g["w_in"] = mm(h, dproj, ta=True, out_dtype=BF16, name=tag + "_dw_in")
    dx, g["ng"] = _norm_bwd(x, W["ng"], dh, dxn, tag + "_dnorm")
    return dx, g


def mla_fwd(x, pos, W, tag):
    h = _norm_fwd(x, W["ng"], tag + "_norm")
    proj = mm(h, W["w_in"], name=tag + "_in")
    hq = rowwise(f_rms, [(proj, MLA_Q_LORA, 0, 0)], [(W["q_g"], MLA_Q_LORA, 0, 0)], [(MLA_Q_LORA, 0, BF16)], ts=512, name=tag + "_qnorm")[0]
    hkv = rowwise(f_rms, [(proj, MLA_KV_LORA, 2, 0)], [(W["kv_g"], MLA_KV_LORA, 0, 0)], [(MLA_KV_LORA, 0, BF16)], ts=512, name=tag + "_kvnorm")[0]
    qpad = mm(hq, W["w_uq"], name=tag + "_uq")
    kv = mm(hkv, W["w_ukv"], name=tag + "_ukv")
    qh, kh, vh = mla_prep_fwd(qpad, kv, proj, pos, W["rope"], tag + "_prep")
    o, lse = flash_fwd(qh, kh, vh, tag + "_attn")
    og = rowwise(f_ogate, [(o, 512, 0, 1), (proj, 512, 4, 1)], [], [(512, 1, BF16)], ncol=4, ts=512, name=tag + "_ogate")[0]
    xn = mm(og, W["w_out"], add=x, name=tag + "_out")
    return xn, (x, h, proj, hq, hkv, qh, kh, vh, o, lse, og)


def mla_bwd(dxn, pos, W, saved, tag):
    x, h, proj, hq, hkv, qh, kh, vh, o, lse, og = saved
    dog = mm(dxn, W["w_out"], tb=True, name=tag + "_dog")
    g = {"w_out": mm(og, dxn, ta=True, out_dtype=BF16, name=tag + "_dwout")}
    dproj = jnp.zeros(proj.shape, F32)
    (do, dproj), _ = rowwise_bwd(f_ogate, [(o, 512, 0, 1), (proj, 512, 4, 1)], [], [(dog, 512, 0, 1)], need=[True, True],
                                 place={1: (dproj, 4)}, ncol=4, ts=512, name=tag + "_dogate")
    dqh, dkh, dvh = flash_bwd(qh, kh, vh, o, lse, do, tag + "_dattn")
    dqpad, dkv, dproj = mla_prep_bwd(dqh, dkh, dvh, pos, W["rope"], dproj, tag + "_dprep")
    dhq = mm(dqpad, W["w_uq"], tb=True, name=tag + "_dhq")
    g["w_uq"] = mm(hq, dqpad, ta=True, out_dtype=BF16, name=tag + "_dwuq")
    dhkv = mm(dkv, W["w_ukv"], tb=True, name=tag + "_dhkv")
    g["w_ukv"] = mm(hkv, dkv, ta=True, out_dtype=BF16, name=tag + "_dwukv")
    (dproj,), (g["q_g"],) = rowwise_bwd(f_rms, [(proj, MLA_Q_LORA, 0, 0)], [(W["q_g"], MLA_Q_LORA, 0, 0)], [(dhq, MLA_Q_LORA, 0, 0)],
                                        need=[True], place={0: (dproj, 0)}, ts=256, name=tag + "_dqnorm")
    (dproj,), (g["kv_g"],) = rowwise_bwd(f_rms, [(proj, MLA_KV_LORA, 2, 0)], [(W["kv_g"], MLA_KV_LORA, 0, 0)], [(dhkv, MLA_KV_LORA, 0, 0)],
                                         need=[True], place={0: (dproj, 2)}, ts=256, name=tag + "_dkvnorm")
    dh = mm(dproj, W["w_in"], tb=True, name=tag + "_dh")
    g["w_in"] = mm(h, dproj, ta=True, out_dtype=BF16, name=tag + "_dw_in")
    dx, g["ng"] = _norm_bwd(x, W["ng"], dh, dxn, tag + "_dnorm")
    return dx, g


def _pad_cols(a, n):
    return jnp.pad(a, ((0, 0), (0, n - a.shape[1])))


def _mla_w_in_layout(w):
    z = lambda n: jnp.zeros((w.shape[0], n), w.dtype)
    kr = w[:, 1280:1344]
    return jnp.concatenate([w[:, :768], z(256), w[:, 768:1280], kr[:, :32], z(32), kr[:, 32:], z(32), z(384), w[:, 1344:]], axis=1)


def _mla_w_in_unlayout(g):
    return jnp.concatenate([g[:, :768], g[:, 1024:1536], g[:, 1536:1568], g[:, 1600:1632], g[:, 2048:]], axis=1)


def _mla_w_uq_layout(w):
    w3 = w.reshape(w.shape[0], MLA_H, MLA_NOPE + MLA_ROPE)
    z = jnp.zeros((w.shape[0], MLA_H, 32), w.dtype)
    return jnp.concatenate([w3[..., :128], w3[..., 128:160], z, w3[..., 160:192], z], axis=-1).reshape(w.shape[0], MLA_H * 256)


def _mla_w_uq_unlayout(g):
    g3 = g.reshape(g.shape[0], MLA_H, 256)
    return jnp.concatenate([g3[..., :128], g3[..., 128:160], g3[..., 192:224]], axis=-1).reshape(g.shape[0], MLA_H * 192)


def _rope_consts():
    half = MLA_ROPE // 2
    inv = ROPE_THETA ** (-jnp.arange(half, dtype=F32) / half)
    z = jnp.zeros((half,), F32)
    o = jnp.ones((half,), F32)
    row = lambda *p: jnp.concatenate(p).reshape(1, LANES)
    return row(inv, z, inv, z), row(o, z, o, z), row(-o, z, o, z)


BIG = ["pool_w_in", "pool_w_grp", "pool_w_out", "gdn_w_in", "gdn_w_out", "mla_w_in", "mla_w_uq", "mla_w_ukv", "mla_w_out"]
BIG_LAYOUT = {"pool_w_in": (1, 1024, (1024, 4096)), "pool_w_grp": (1, 128, (4, 512, 512)), "pool_w_out": (0, 512, (2048, 1024)),
              "gdn_w_in": (None, None, (4, 1024, 1540)), "gdn_w_out": (0, 512, (2048, 1024)),
              "mla_w_in": (None, None, (4, 1024, 848)), "mla_w_uq": (1, 768, (768, 3072)), "mla_w_ukv": (1, 1024, (512, 4096)),
              "mla_w_out": (0, 512, (2048, 1024))}
SMALL_SHARDED = ["pool_scale", "gdn_conv", "mla_q_norm_g", "mla_kv_norm_g"]
SMALL_AXIS = {"pool_scale": 1, "gdn_conv": 2, "mla_q_norm_g": 1, "mla_kv_norm_g": 1}
REPLICATED = ["norm_g", "gdn_a_log", "gdn_dt_bias", "gdn_norm_g", "final_g"]
PACK_C = 1024


def _pack(parts, dtype, row_mult):
    flat = jnp.concatenate([p.reshape(-1).astype(dtype) for p in parts])
    rows = -(-flat.shape[0] // PACK_C)
    rows = -(-rows // row_mult) * row_mult
    return jnp.pad(flat, (0, rows * PACK_C - flat.shape[0])).reshape(rows, PACK_C)


def _unpack(buf, shapes):
    lead = buf.shape[:-2]
    flat = buf.reshape(lead + (-1,))
    out, off = [], 0
    for s in shapes:
        n = int(np.prod(s))
        out.append(flat[..., off:off + n].reshape(lead + tuple(s)))
        off += n
    return out


def _unshard(g4, axis):
    a = jnp.moveaxis(g4, 0, axis)
    s = a.shape
    return a.reshape(s[:axis] + (s[axis] * s[axis + 1],) + s[axis + 2:])


def _to_shards(a, axis):
    s = a.shape
    return jnp.moveaxis(a.reshape(s[:axis] + (4, s[axis] // 4) + s[axis + 1:]), axis, 0)


def layer_weights(full, small, rep, layer):
    ng = rep["norm_g"][layer:layer + 1]
    side_by_side = lambda a4: jnp.moveaxis(a4, 0, 1).reshape(a4.shape[1], 4 * a4.shape[2])
    if layer in (0, 3):
        j = layer // 3
        return dict(ng=ng, w_in=full[("pool_w_in", j)], w_grp=full[("pool_w_grp", j)], scale=small["pool_scale"][j:j + 1],
                    w_out=full[("pool_w_out", j)])
    if layer == 1:
        return dict(ng=ng, w_in=_pad_cols(side_by_side(full[("gdn_w_in", 0)]), GDN_IN_PAD),
                    conv=jnp.pad(small["gdn_conv"][0], ((0, 4), (0, 0))), a_log=_pad_cols(rep["gdn_a_log"], LANES),
                    dt_bias=_pad_cols(rep["gdn_dt_bias"], LANES), norm_g=rep["gdn_norm_g"], w_out=full[("gdn_w_out", 0)])
    return dict(ng=ng, w_in=_mla_w_in_layout(side_by_side(full[("mla_w_in", 0)])), q_g=small["mla_q_norm_g"],
                kv_g=small["mla_kv_norm_g"], w_uq=_mla_w_uq_layout(full[("mla_w_uq", 0)]), w_ukv=full[("mla_w_ukv", 0)],
                w_out=full[("mla_w_out", 0)], rope=_rope_consts())


def big_grad_pieces(gl):
    g0, g1, g2, g3 = gl
    slots = lambda a: jnp.moveaxis(a.reshape(a.shape[0], 4, a.shape[1] // 4), 1, 0)
    out = {}
    for l, g in ((0, g0), (1, g3)):
        if g is not None:
            out.update({("pool_w_in", l): g["w_in"], ("pool_w_grp", l): g["w_grp"], ("pool_w_out", l): g["w_out"]})
    if g1 is not None:
        out.update({("gdn_w_in", 0): slots(g1["w_in"][:, :GDN_IN]), ("gdn_w_out", 0): g1["w_out"]})
    if g2 is not None:
        out.update({("mla_w_in", 0): slots(_mla_w_in_unlayout(g2["w_in"])), ("mla_w_uq", 0): _mla_w_uq_unlayout(g2["w_uq"]),
                    ("mla_w_ukv", 0): g2["w_ukv"], ("mla_w_out", 0): g2["w_out"]})
    return out


def small_grads(gl, dfinal):
    g0, g1, g2, g3 = gl
    return {"norm_g": jnp.concatenate([g0["ng"], g1["ng"], g2["ng"], g3["ng"]], axis=0),
            "pool_scale": jnp.concatenate([g0["scale"], g3["scale"]], axis=0), "gdn_conv": g1["conv"][None, :4],
            "gdn_a_log": g1["a_log"][:, :GDN_H], "gdn_dt_bias": g1["dt_bias"][:, :GDN_H], "gdn_norm_g": g1["norm_g"],
            "mla_q_norm_g": g2["q_g"], "mla_kv_norm_g": g2["kv_g"], "final_g": dfinal.reshape(D)}


NAMES = ["norm_g", "pool_w_in", "pool_w_grp", "pool_scale", "pool_w_out", "gdn_w_in", "gdn_conv", "gdn_a_log", "gdn_dt_bias",
         "gdn_norm_g", "gdn_w_out", "mla_w_in", "mla_q_norm_g", "mla_w_uq", "mla_kv_norm_g", "mla_w_ukv", "mla_w_out", "final_g"]


def kernel(x, positions, norm_g, pool_w_in, pool_w_grp, pool_scale, pool_w_out, gdn_w_in, gdn_conv, gdn_a_log, gdn_dt_bias, gdn_norm_g, gdn_w_out, mla_w_in, mla_q_norm_g, mla_w_uq, mla_kv_norm_g, mla_w_ukv, mla_w_out, final_g, loss_target, m_norm_g, m_pool_w_in, m_pool_w_grp, m_pool_scale, m_pool_w_out, m_gdn_w_in, m_gdn_conv, m_gdn_a_log, m_gdn_dt_bias, m_gdn_norm_g, m_gdn_w_out, m_mla_w_in, m_mla_q_norm_g, m_mla_w_uq, m_mla_kv_norm_g, m_mla_w_ukv, m_mla_w_out, m_final_g, v_norm_g, v_pool_w_in, v_pool_w_grp, v_pool_scale, v_pool_w_out, v_gdn_w_in, v_gdn_conv, v_gdn_a_log, v_gdn_dt_bias, v_gdn_norm_g, v_gdn_w_out, v_mla_w_in, v_mla_q_norm_g, v_mla_w_uq, v_mla_kv_norm_g, v_mla_w_ukv, v_mla_w_out, v_final_g):
    args = locals()
    w = {n: args[n] for n in NAMES}
    m = {n: args["m_" + n] for n in NAMES}
    v = {n: args["v_" + n] for n in NAMES}
    my_chip = (2 * lax.axis_index("x") + lax.axis_index("y")).astype(I32)
    S_ = x.shape[1]
    x0, pos, target = x[0], positions.reshape(S_, 1).astype(F32), loss_target[0]
    rep = {n: w[n] for n in REPLICATED}

    shard = {(n, l): w[n][l:l + 1].astype(BF16) for n in BIG for l in range(w[n].shape[0])}
    small_shapes = [w[n].shape for n in SMALL_SHARDED]
    shard[("small", 0)] = _pack([w[n] for n in SMALL_SHARDED], F32, 8)[None]
    layout = dict(BIG_LAYOUT, small=(None, None, (4,) + shard[("small", 0)].shape[1:]))

    def gather_start(group, after, tag):
        pieces = [gather_piece(i, 0, i, layout[n][0], layout[n][1]) for i, (n, l) in enumerate(group)]
        shapes = [_sds(layout[n][2], shard[(n, l)].dtype) for n, l in group]
        sems, ins, lands, token = exchange_start(pieces, [shard[k] for k in group], shapes, after, tag + "_start")
        return (pieces, sems, ins, lands), token

    def finish(handle, after, tag):
        return exchange_wait(*handle, after, tag + "_wait")

    def gathered(group, handle, after, tag):
        srcs, lands = finish(handle, after, tag)
        return {(n, l): place_own(a, s[0], layout[n][0], layout[n][1], my_chip) for (n, l), s, a in zip(group, srcs, lands)}

    tied = lambda a, token: a + token[0:1, 0:1]
    group_a = [("pool_w_in", 0), ("pool_w_grp", 0), ("pool_w_out", 0), ("small", 0)]
    group_b = [("gdn_w_in", 0), ("gdn_w_out", 0)]
    group_c = [("mla_w_in", 0), ("mla_w_uq", 0), ("mla_w_ukv", 0), ("mla_w_out", 0), ("pool_w_in", 1), ("pool_w_grp", 1), ("pool_w_out", 1)]
    full = {}
    h_a, t_a = gather_start(group_a, x0, "gather_a")
    full.update(gathered(group_a, h_a, t_a, "gather_a"))
    small = {n: _unshard(a, SMALL_AXIS[n]) for n, a in zip(SMALL_SHARDED, _unpack(full[("small", 0)], small_shapes))}
    h_b, t_b = gather_start(group_b, full[group_a[0]], "gather_b")
    W0 = layer_weights(full, small, rep, 0)
    x1, s0 = pool_fwd(x0, dict(W0, ng=tied(W0["ng"], t_b)), "l0")
    full.update(gathered(group_b, h_b, x1, "gather_b"))
    h_c, t_c = gather_start(group_c, full[group_b[0]], "gather_c")
    W1 = layer_weights(full, small, rep, 1)
    x2, s1 = gdn_fwd(x1, dict(W1, ng=tied(W1["ng"], t_c)), "l1")
    full.update(gathered(group_c, h_c, x2, "gather_c"))
    W2, W3 = layer_weights(full, small, rep, 2), layer_weights(full, small, rep, 3)
    x3, s2 = mla_fwd(x2, pos, W2, "l2")
    x4, s3 = pool_fwd(x3, W3, "l3")
    loss_part, dx4, dfinal = loss_head(x4, target, final_g.reshape(1, D), "loss_head")

    def scatter_start(pieces_of, after, tag):
        keys = list(pieces_of)
        pieces = [scatter_piece(i, i, BIG_LAYOUT[n][0], BIG_LAYOUT[n][1]) for i, (n, l) in enumerate(keys)]
        shapes = [_sds((4,) + tuple(w[n].shape[1:]), BF16) for n, l in keys]
        sems, ins, lands, token = exchange_start(pieces, [pieces_of[k] for k in keys], shapes, after, tag + "_start")
        return keys, (pieces, sems, ins, lands), token

    def scattered(keys, handle, after, tag):
        srcs, lands = finish(handle, after, tag)
        return {(n, l): place_own(a, own_window(g, BIG_LAYOUT[n][0], BIG_LAYOUT[n][1], my_chip), None, None, my_chip)
                for (n, l), g, a in zip(keys, srcs, lands)}

    dx3, g3 = pool_bwd(dx4, W3, s3, "l3")
    k3, h3, t3 = scatter_start(big_grad_pieces((None, None, None, g3)), dx3, "scatter_l3")
    dx2, g2 = mla_bwd(tied(dx3, t3), pos, W2, s2, "l2")
    k2, h2, t2 = scatter_start(big_grad_pieces((None, None, g2, None)), dx2, "scatter_l2")
    dx1, g1 = gdn_bwd(tied(dx2, t2), W1, s1, "l1")
    k1, h1, t1 = scatter_start(big_grad_pieces((None, g1, None, None)), dx1, "scatter_l1")
    dx0, g0 = pool_bwd(tied(dx1, t1), W0, s0, "l0")
    k0, h0, t0 = scatter_start(big_grad_pieces((g0, None, None, None)), dx0, "scatter_l0")
    recv = {}
    for keys, handle, tag in ((k3, h3, "scatter_l3"), (k2, h2, "scatter_l2"), (k1, h1, "scatter_l1"), (k0, h0, "scatter_l0")):
        recv.update(scattered(keys, handle, t0, tag))
    keys = list(recv)
    sib = dict(zip(keys, swap_cores([recv[k] for k in keys], "swap_cores")))

    sg = small_grads((g0, g1, g2, g3), dfinal)
    small_names = SMALL_SHARDED + REPLICATED
    small_buf = _pack([sg[n] for n in small_names] + [loss_part], F32, 8)
    small_sum = sum_slots(None, exchange_all(small_buf, "gather_small"), jnp.full((1,), -1, I32), "sum_small")
    full_small = _unpack(small_sum, [sg[n].shape for n in small_names] + [(1, LANES)])
    loss = full_small[-1][0, 0]
    small_part = {}
    for n, a in zip(small_names, full_small[:-1]):
        if n in SMALL_AXIS:
            a = lax.dynamic_index_in_dim(_to_shards(a, SMALL_AXIS[n]), my_chip, axis=0, keepdims=False)
        small_part[n] = a

    outs = []
    for n in NAMES:
        shp = w[n].shape
        two = (int(np.prod(shp[:-1])), shp[-1]) if len(shp) > 1 else (1, shp[0])
        if n in BIG_LAYOUT:
            layers = shp[0]
            rows = lambda a: a.reshape(4, two[0] // layers, two[1])
            parts = [[rows(recv[(n, l)]) for l in range(layers)], [rows(sib[(n, l)]) for l in range(layers)]]
        else:
            parts = [[small_part[n].reshape((1,) + two)]]
        res = adamw(w[n].reshape(two), parts, m[n].reshape(two), v[n].reshape(two), "adamw_" + n)
        outs.append([r.reshape(shp) for r in res])
    return (loss, dx0[None], *[o[0] for o in outs], *[o[1] for o in outs], *[o[2] for o in outs], *[o[3] for o in outs])
```

```python
import functools
import math

import jax
import jax.numpy as jnp
import numpy as np
from jax import lax
from jax.experimental import pallas as pl
from jax.experimental.pallas import tpu as pltpu

F32 = jnp.float32
BF16 = jnp.bfloat16
I32 = jnp.int32

D = 1024
EPS = 1e-6
POOL_WIDTH = 2048
POOL_GROUP = 512
GDN_H, GDN_DK, GDN_DV, GDN_C = 8, 128, 256, 64
GDN_QK, GDN_V, GDN_CONV_CH, GDN_IN = 1024, 2048, 4096, 6160
GDN_IN_PAD = 6272
MLA_H, MLA_NOPE, MLA_ROPE, MLA_V = 16, 128, 64, 128
MLA_Q_LORA, MLA_KV_LORA, MLA_WIDTH, MLA_IN = 768, 512, 2048, 3392
MLA_IN_PAD = 4096
MLA_SCALE = (MLA_NOPE + MLA_ROPE) ** -0.5
ROPE_THETA = 10000.0
ADAM_LR, ADAM_B1, ADAM_B2, ADAM_EPS, ADAM_WD, ADAM_STEP = 0.001, 0.9, 0.999, 1e-08, 0.01, 10

VMEM_LIMIT_V7X = 56 * 1024 * 1024
LANES = 128
MESH = pl.DeviceIdType.MESH


def _pc(body, **kw):
    return pl.pallas_call(body, **kw)


def _cparams(sem):
    return pltpu.CompilerParams(dimension_semantics=sem, vmem_limit_bytes=VMEM_LIMIT_V7X)


def _tile(n, cap):
    t = (cap // LANES) * LANES
    while t >= LANES:
        if n % t == 0:
            return t
        t -= LANES
    return n


def _sds(shape, dt):
    return jax.ShapeDtypeStruct(shape, dt)


def mm(a, b, *, ta=False, tb=False, add=None, after=None, out_dtype=F32, name):
    if ta:
        K, M = a.shape
    else:
        M, K = a.shape
    if tb:
        N, K2 = b.shape
    else:
        K2, N = b.shape
    assert K == K2, (a.shape, b.shape, ta, tb)
    tm, tn, tk = _tile(M, 1024), _tile(N, 1024), _tile(K, 1024)
    nk = K // tk
    a_spec = pl.BlockSpec((tk, tm), lambda i, j, k: (k, i)) if ta else pl.BlockSpec((tm, tk), lambda i, j, k: (i, k))
    b_spec = pl.BlockSpec((tn, tk), lambda i, j, k: (j, k)) if tb else pl.BlockSpec((tk, tn), lambda i, j, k: (k, j))
    o_spec = pl.BlockSpec((tm, tn), lambda i, j, k: (i, j))
    dn = (((0 if ta else 1,), (1 if tb else 0,)), ((), ()))
    has_add = add is not None

    def body(*refs):
        a_ref, b_ref = refs[0], refs[1]
        o_ref, acc = refs[-2], refs[-1]
        k = pl.program_id(2)

        @pl.when(k == 0)
        def _():
            acc[...] = jnp.zeros_like(acc)

        acc[...] += lax.dot_general(a_ref[...].astype(BF16), b_ref[...].astype(BF16), dn, preferred_element_type=F32)

        @pl.when(k == nk - 1)
        def _():
            r = acc[...]
            if has_add:
                r = r + refs[2][...]
            o_ref[...] = r.astype(out_dtype)

    ins = [a, b] + ([add] if has_add else []) + ([after] if after is not None else [])
    specs = [a_spec, b_spec] + ([o_spec] if has_add else []) + ([pl.BlockSpec(memory_space=pl.ANY)] if after is not None else [])
    return _pc(body, grid=(M // tm, N // tn, nk), in_specs=specs, out_specs=o_spec, out_shape=_sds((M, N), out_dtype),
               scratch_shapes=[pltpu.VMEM((tm, tn), F32)], compiler_params=_cparams(("parallel", "parallel", "arbitrary")),
               name=name)(*ins)


def gmm(kind, a, b, *, G, name, out_dtype=F32):
    S_ = a.shape[0]
    Ka = a.shape[1] // G
    if kind == "tn":
        N = b.shape[1] // G
        tk = _tile(S_, 512)
        nk = S_ // tk

        def body(a_ref, b_ref, o_ref, acc):
            k = pl.program_id(1)

            @pl.when(k == 0)
            def _():
                acc[...] = jnp.zeros_like(acc)

            acc[...] += lax.dot_general(a_ref[...].astype(BF16), b_ref[...].astype(BF16), (((0,), (0,)), ((), ())),
                                        preferred_element_type=F32)

            @pl.when(k == nk - 1)
            def _():
                o_ref[...] = acc[...].astype(out_dtype)

        return _pc(body, grid=(G, nk),
                   in_specs=[pl.BlockSpec((tk, Ka), lambda g, k: (k, g)), pl.BlockSpec((tk, N), lambda g, k: (k, g))],
                   out_specs=pl.BlockSpec((None, Ka, N), lambda g, k: (g, 0, 0)), out_shape=_sds((G, Ka, N), out_dtype),
                   scratch_shapes=[pltpu.VMEM((Ka, N), F32)], compiler_params=_cparams(("parallel", "arbitrary")), name=name)(a, b)
    N = b.shape[2] if kind == "nn" else b.shape[1]
    tm = _tile(S_, 1024)
    dn = (((1,), (0 if kind == "nn" else 1,)), ((), ()))

    def body(a_ref, b_ref, o_ref):
        o_ref[...] = lax.dot_general(a_ref[...].astype(BF16), b_ref[...].astype(BF16), dn, preferred_element_type=F32)

    bshape = (None,) + tuple(b.shape[1:])
    return _pc(body, grid=(G, S_ // tm),
               in_specs=[pl.BlockSpec((tm, Ka), lambda g, i: (i, g)), pl.BlockSpec(bshape, lambda g, i: (g, 0, 0))],
               out_specs=pl.BlockSpec((tm, N), lambda g, i: (i, g)), out_shape=_sds((S_, G * N), F32),
               compiler_params=_cparams(("parallel", "parallel")), name=name)(a, b)


def _rw_spec(ts, w, c, s):
    return pl.BlockSpec((ts, w), lambda j, i: (i, c + j * s))


def _rw_pspec(p, w, c, s):
    return pl.BlockSpec((p.shape[0], w), lambda j, i: (0, c + j * s))


def rowwise(f, tiles, params, outs, *, ncol=1, ts, name):
    S_ = tiles[0][0].shape[0]
    nin = len(tiles) + len(params)

    def body(*refs):
        res = f(pl.program_id(0), *[r[...] for r in refs[:nin]])
        for r, o in zip(refs[nin:], res):
            r[...] = o.astype(r.dtype)

    return _pc(body, grid=(ncol, S_ // ts),
               in_specs=[_rw_spec(ts, w, c, s) for (_, w, c, s) in tiles] + [_rw_pspec(*p) for p in params],
               out_specs=[_rw_spec(ts, w, 0, s) for (w, s, _) in outs],
               out_shape=[_sds((S_, w * (ncol if s else 1)), dt) for (w, s, dt) in outs],
               compiler_params=_cparams(("parallel", "parallel")), name=name)(*[t[0] for t in tiles], *[p[0] for p in params])


def rowwise_bwd(f, tiles, params, cots, *, need, adds=None, place=None, narrow=(), ncol=1, ts, name):
    S_ = tiles[0][0].shape[0]
    adds = adds or {}
    place = place or {}
    nt, npar, nc = len(tiles), len(params), len(cots)
    add_keys = sorted(adds)
    need_idx = [k for k in range(nt) if need[k]]
    into_keys = [k for k in need_idx if k in place and not isinstance(place[k][0], int)]
    n_extra = len(add_keys) + len(into_keys)

    def body(*refs):
        j, i = pl.program_id(0), pl.program_id(1)
        vals = [r[...] for r in refs[:nt + npar]]
        cvals = tuple(r[...] for r in refs[nt + npar:nt + npar + nc])
        add_refs = refs[nt + npar + nc:nt + npar + nc + len(add_keys)]
        out_refs = refs[nt + npar + nc + n_extra:]
        _, vjp = jax.vjp(lambda *v: tuple(f(j, *v)), *vals)
        grads = vjp(cvals)
        for n, k in enumerate(need_idx):
            g = grads[k]
            if k in adds:
                g = g + add_refs[add_keys.index(k)][...]
            out_refs[n][...] = g.astype(out_refs[n].dtype)
        for n in range(npar):
            ref = out_refs[len(need_idx) + n]
            first = (i == 0) if params[n][3] else jnp.logical_and(i == 0, j == 0)

            @pl.when(first)
            def _():
                ref[...] = jnp.zeros_like(ref)

            ref[...] += grads[nt + n]

    in_specs = ([_rw_spec(ts, w, c, s) for (_, w, c, s) in tiles] + [_rw_pspec(*p) for p in params]
                + [_rw_spec(ts, w, c, s) for (_, w, c, s) in cots] + [_rw_spec(ts, *adds[k][1:]) for k in add_keys]
                + [pl.BlockSpec(memory_space=pl.ANY) for _ in into_keys])
    out_specs, out_shape, aliases = [], [], {}
    for n, k in enumerate(need_idx):
        w, s = tiles[k][1], tiles[k][3]
        if k in place:
            dst, c0 = place[k]
            total = dst if isinstance(dst, int) else dst.shape[1]
            out_specs.append(_rw_spec(ts, w, c0, s))
            out_shape.append(_sds((S_, total), (BF16 if k in narrow else F32) if isinstance(dst, int) else dst.dtype))
            if k in into_keys:
                aliases[nt + npar + nc + len(add_keys) + into_keys.index(k)] = n
        else:
            out_specs.append(_rw_spec(ts, w, 0, s))
            out_shape.append(_sds((S_, w * (ncol if s else 1)), BF16 if k in narrow else F32))
    out_specs += [_rw_pspec(p[0], p[1], p[2], p[3]) for p in params]
    out_shape += [_sds(p[0].shape, F32) for p in params]
    res = _pc(body, grid=(ncol, S_ // ts), in_specs=in_specs, out_specs=out_specs, out_shape=out_shape,
              input_output_aliases=aliases, compiler_params=_cparams(("arbitrary", "arbitrary")), name=name)(
        *[t[0] for t in tiles], *[p[0] for p in params], *[c[0] for c in cots], *[adds[k][0] for k in add_keys],
        *[place[k][0] for k in into_keys])
    return list(res[:len(need_idx)]), list(res[len(need_idx):])


def _rms(x, g):
    r = lax.rsqrt(jnp.mean(x * x, axis=-1, keepdims=True) + EPS)
    return x * r * g


def _silu(x):
    return x * jax.nn.sigmoid(x)


@jax.custom_vjp
def _softplus(x):
    return jnp.maximum(x, 0.0) + jnp.log1p(jnp.exp(-jnp.abs(x)))


_softplus.defvjp(lambda x: (_softplus(x), x), lambda x, d: (d * jax.nn.sigmoid(x),))


def f_rms(j, x, g):
    return (_rms(x, g),)


def f_pool_gate(j, pg, gate, scale):
    return (pg * scale * _silu(gate),)


def f_ogate(j, o, gate):
    return (o * _silu(gate),)


def f_gdn_out(j, o, gate, g):
    return (_rms(o, g) * _silu(gate),)


def f_gdn_gates(j, ba, alog, dtb):
    lane = lax.broadcasted_iota(I32, (1, LANES), 1)
    gs, bs = [], []
    for h in range(GDN_H):
        eb = (lane == h).astype(F32)
        ea = (lane == GDN_H + h).astype(F32)
        b = jnp.sum(ba * eb, -1, keepdims=True)
        a = jnp.sum(ba * ea, -1, keepdims=True)
        al = jnp.sum(alog * eb, -1, keepdims=True)
        dt = jnp.sum(dtb * eb, -1, keepdims=True)
        g = -jnp.exp(al) * _softplus(a + dt)
        gs.append(jnp.broadcast_to(g, ba.shape))
        bs.append(jnp.broadcast_to(jax.nn.sigmoid(b), ba.shape))
    return jnp.concatenate(gs, 1), jnp.concatenate(bs, 1)


def _shift_dn(x, k):
    rows = lax.broadcasted_iota(I32, x.shape, 0)
    return jnp.where(rows < k, 0.0, pltpu.roll(x, k, 0))


def _shift_up(x, k):
    n = x.shape[0]
    rows = lax.broadcasted_iota(I32, x.shape, 0)
    return jnp.where(rows >= n - k, 0.0, pltpu.roll(x, n - k, 0))


def _pool_window(j):
    g = lax.div(j, POOL_GROUP // LANES)
    return jnp.where(g == 0, 2.0, jnp.where(g == 1, 4.0, jnp.where(g == 2, 8.0, 16.0))), g


def _pick(g, a2, a4, a8, a16):
    return jnp.where(g == 0, a2, jnp.where(g == 1, a4, jnp.where(g == 2, a8, a16)))


def pool_time_fwd(proj, name):
    S_ = proj.shape[0]

    def body(u_ref, p_ref):
        u = u_ref[...]
        w, g = _pool_window(pl.program_id(0))
        s2 = u + _shift_dn(u, 1)
        s4 = s2 + _shift_dn(s2, 2)
        s8 = s4 + _shift_dn(s4, 4)
        s16 = s8 + _shift_dn(s8, 8)
        t1 = (lax.broadcasted_iota(I32, u.shape, 0) + 1).astype(F32)
        p_ref[...] = (_pick(g, s2, s4, s8, s16) / jnp.minimum(t1, w) - u).astype(p_ref.dtype)

    return _pc(body, grid=(POOL_WIDTH // LANES,), in_specs=[pl.BlockSpec((S_, LANES), lambda j: (0, j))],
               out_specs=pl.BlockSpec((S_, LANES), lambda j: (0, j)), out_shape=_sds((S_, POOL_WIDTH), BF16),
               compiler_params=_cparams(("parallel",)), name=name)(proj)


def pool_time_bwd(dp, into, name):
    S_ = dp.shape[0]

    def body(dp_ref, _, du_ref):
        d = dp_ref[...]
        w, g = _pool_window(pl.program_id(0))
        t1 = (lax.broadcasted_iota(I32, d.shape, 0) + 1).astype(F32)
        q = d / jnp.minimum(t1, w)
        r2 = q + _shift_up(q, 1)
        r4 = r2 + _shift_up(r2, 2)
        r8 = r4 + _shift_up(r4, 4)
        r16 = r8 + _shift_up(r8, 8)
        du_ref[...] = (_pick(g, r2, r4, r8, r16) - d).astype(du_ref.dtype)

    return _pc(body, grid=(POOL_WIDTH // LANES,),
               in_specs=[pl.BlockSpec((S_, LANES), lambda j: (0, j)), pl.BlockSpec(memory_space=pl.ANY)],
               out_specs=pl.BlockSpec((S_, LANES), lambda j: (0, j)), out_shape=_sds(into.shape, into.dtype),
               input_output_aliases={1: 0}, compiler_params=_cparams(("parallel",)), name=name)(dp, into)


def _conv_post(j, a):
    n = a * lax.rsqrt(jnp.sum(a * a, axis=-1, keepdims=True) + EPS)
    nq = GDN_QK // LANES
    return jnp.where(j < nq, n * (GDN_DK ** -0.5), jnp.where(j < 2 * nq, n, a))


def _conv_pre(u, w):
    return w[3:4] * u + w[2:3] * _shift_dn(u, 1) + w[1:2] * _shift_dn(u, 2) + w[0:1] * _shift_dn(u, 3)


def gdn_conv_fwd(proj, conv_w, name):
    S_ = proj.shape[0]

    def body(u_ref, w_ref, o_ref):
        o_ref[...] = _conv_post(pl.program_id(0), _silu(_conv_pre(u_ref[...], w_ref[...])))

    return _pc(body, grid=(GDN_CONV_CH // LANES,),
               in_specs=[pl.BlockSpec((S_, LANES), lambda j: (0, j)), pl.BlockSpec((8, LANES), lambda j: (0, j))],
               out_specs=pl.BlockSpec((S_, LANES), lambda j: (0, j)), out_shape=_sds((S_, GDN_CONV_CH), F32),
               compiler_params=_cparams(("parallel",)), name=name)(proj, conv_w)


def gdn_conv_bwd(proj, conv_w, dq, dk, dv, into, name):
    S_ = proj.shape[0]
    nq = GDN_QK // LANES

    def body(u_ref, w_ref, dq_ref, dk_ref, dv_ref, _, du_ref, dw_ref):
        j = pl.program_id(0)
        u, w = u_ref[...], w_ref[...]
        c = _conv_pre(u, w)
        sig = jax.nn.sigmoid(c)
        dout = jnp.where(j < nq, dq_ref[...], jnp.where(j < 2 * nq, dk_ref[...], dv_ref[...]))
        _, vjp = jax.vjp(lambda a: _conv_post(j, a), c * sig)
        dc = vjp(dout)[0] * (sig * (1.0 + c * (1.0 - sig)))
        du = w[3:4] * dc + w[2:3] * _shift_up(dc, 1) + w[1:2] * _shift_up(dc, 2) + w[0:1] * _shift_up(dc, 3)
        du_ref[...] = du.astype(du_ref.dtype)
        rows = lax.broadcasted_iota(I32, (8, LANES), 0)
        dw = jnp.zeros((8, LANES), F32)
        for k in range(4):
            us = u if k == 3 else _shift_dn(u, 3 - k)
            dw = dw + jnp.where(rows == k, jnp.sum(dc * us, axis=0, keepdims=True), 0.0)
        dw_ref[...] = dw

    blk = lambda f: pl.BlockSpec((S_, LANES), f)
    return _pc(body, grid=(GDN_CONV_CH // LANES,),
               in_specs=[blk(lambda j: (0, j)), pl.BlockSpec((8, LANES), lambda j: (0, j)),
                         blk(lambda j: (0, jnp.minimum(j, nq - 1))), blk(lambda j: (0, jnp.clip(j - nq, 0, nq - 1))),
                         blk(lambda j: (0, jnp.clip(j - 2 * nq, 0, 2 * nq - 1))), pl.BlockSpec(memory_space=pl.ANY)],
               out_specs=[blk(lambda j: (0, j)), pl.BlockSpec((8, LANES), lambda j: (0, j))],
               out_shape=[_sds(into.shape, into.dtype), _sds((8, GDN_CONV_CH), F32)], input_output_aliases={5: 0},
               compiler_params=_cparams(("parallel",)), name=name)(proj, conv_w, dq, dk, dv, into)


_NN, _NT, _TN = ((1,), (0,)), ((1,), (1,)), ((0,), (0,))


def _split(x, n):
    parts = []
    for _ in range(n):
        h = x.astype(BF16)
        parts.append(h)
        x = x - h.astype(F32)
    return parts


def _dot(a, b, dn, mode):
    d = lambda p, q: lax.dot_general(p, q, (dn, ((), ())), preferred_element_type=F32)
    if mode == "lo":
        return d(a.astype(BF16), b.astype(BF16))
    if mode == "x3":
        (ah, al), (bh, bl) = _split(a, 2), _split(b, 2)
        return d(ah, bh) + (d(ah, bl) + d(al, bh))
    b0, b1, b2 = _split(b, 3)
    ab = a.astype(BF16)
    return d(ab, b0) + (d(ab, b1) + d(ab, b2))


def _make_dots(mode):
    @jax.custom_vjp
    def nn(a, b):
        return _dot(a, b, _NN, mode)

    @jax.custom_vjp
    def nt(a, b):
        return _dot(a, b, _NT, mode)

    @jax.custom_vjp
    def tn(a, b):
        return _dot(a, b, _TN, mode)

    nn.defvjp(lambda a, b: (nn(a, b), (a, b)), lambda r, d: (nt(d, r[1]), tn(r[0], d)))
    nt.defvjp(lambda a, b: (nt(a, b), (a, b)), lambda r, d: (nn(d, r[1]), tn(d, r[0])))
    tn.defvjp(lambda a, b: (tn(a, b), (a, b)), lambda r, d: (nt(r[1], d), nn(r[0], d)))
    return nn, nt, tn


_nn_hi, _nt_hi, _tn_hi = _make_dots("x3")
_nn_lo, _nt_lo, _tn_lo = _make_dots("lo")


@jax.custom_vjp
def _nn_const(a, b):
    return _dot(a, b, _NN, "xl")


_nn_const.defvjp(lambda a, b: (_nn_const(a, b), a), lambda a, d: (jnp.zeros_like(a), _dot(a, d, _TN, "xl")))


def _each(f, *lists):
    return [f(*xs) for xs in zip(*lists)]


def _gdn_chunk(q, k, v, gb, bb, state):
    C = GDN_C
    e0 = (lax.broadcasted_iota(I32, (1, LANES), 1) == 0).astype(F32)
    ri = lax.broadcasted_iota(I32, (C, C), 0)
    ci = lax.broadcasted_iota(I32, (C, C), 1)
    causal, strict = ri >= ci, ri > ci
    tri, eye, ones = causal.astype(F32), (ri == ci).astype(F32), jnp.ones((C, C), F32)
    last = lax.broadcasted_iota(I32, (C, LANES), 0) == C - 1
    g1 = _each(lambda a: jnp.sum(a * e0, -1, keepdims=True), gb)
    b1 = _each(lambda a: jnp.sum(a * e0, -1, keepdims=True), bb)
    gc_c = _each(lambda g: _nn_const(tri, jnp.broadcast_to(g, (C, C))), g1)
    gc_d = _each(lambda g: _nn_const(tri, jnp.broadcast_to(g, (C, LANES))), g1)
    gr_c = _each(lambda g: _nn_const(ones, eye * g), gc_c)
    decay = _each(lambda a, r: jnp.where(causal, jnp.exp(jnp.where(causal, a - r, 0.0)), 0.0), gc_c, gr_c)
    kb = _each(lambda a, b: a * b, k, b1)
    vb = _each(lambda a, b: a * b, v, b1)
    x = _each(lambda a, b, d: -jnp.where(strict, _nt_lo(a, b) * d, 0.0), kb, k, decay)
    ainv, p = _each(lambda a: eye + a, x), x
    for _ in range(5):
        p = _each(lambda a: _nn_hi(a, a), p)
        ainv = _each(lambda a, b: a + _nn_hi(a, b), ainv, p)
    u = _each(_nn_hi, ainv, vb)
    w = _each(lambda a, b, g: _nn_hi(a, b * jnp.exp(g)), ainv, kb, gc_d)
    attn = _each(lambda a, b, d: jnp.where(causal, _nt_lo(a, b) * d, 0.0), q, k, decay)
    v_new = _each(lambda a, b, s: a - _nn_lo(b, s), u, w, state)
    o = _each(lambda a, g, s, t, vn: _nn_lo(a * jnp.exp(g), s) + _nn_lo(t, vn), q, gc_d, state, attn, v_new)
    gl = _each(lambda g: jnp.sum(jnp.where(last, g, 0.0), axis=0, keepdims=True), gc_d)
    new_state = _each(lambda s, g, a, gd, vn: s * jnp.exp(jnp.sum(g * e0, -1, keepdims=True)) + _tn_lo(a * jnp.exp(g - gd), vn),
                      state, gl, k, gc_d, v_new)
    return o, new_state


def _head_slices(ref, width):
    return [ref[:, h * width:(h + 1) * width] for h in range(GDN_H)]


def gdn_chunk_fwd(qkv, g_b, beta_b, name):
    S_ = qkv.shape[0]
    N = S_ // GDN_C

    def body(q_ref, k_ref, v_ref, g_ref, b_ref, o_ref, s_ref, state):
        @pl.when(pl.program_id(0) == 0)
        def _():
            state[...] = jnp.zeros_like(state)

        st = [state[h] for h in range(GDN_H)]
        s_ref[0] = state[...]
        o, st2 = _gdn_chunk(_head_slices(q_ref, GDN_DK), _head_slices(k_ref, GDN_DK), _head_slices(v_ref, GDN_DV),
                            _head_slices(g_ref, GDN_DK), _head_slices(b_ref, GDN_DK), st)
        for h in range(GDN_H):
            o_ref[:, h * GDN_DV:(h + 1) * GDN_DV] = o[h]
            state[h] = st2[h]

    return _pc(body, grid=(N,),
               in_specs=[pl.BlockSpec((GDN_C, GDN_QK), lambda n: (n, 0)), pl.BlockSpec((GDN_C, GDN_QK), lambda n: (n, 1)),
                         pl.BlockSpec((GDN_C, GDN_V), lambda n: (n, 1)), pl.BlockSpec((GDN_C, GDN_QK), lambda n: (n, 0)),
                         pl.BlockSpec((GDN_C, GDN_QK), lambda n: (n, 0))],
               out_specs=[pl.BlockSpec((GDN_C, GDN_V), lambda n: (n, 0)),
                          pl.BlockSpec((1, GDN_H, GDN_DK, GDN_DV), lambda n: (n, 0, 0, 0))],
               out_shape=[_sds((S_, GDN_V), F32), _sds((N, GDN_H, GDN_DK, GDN_DV), F32)],
               scratch_shapes=[pltpu.VMEM((GDN_H, GDN_DK, GDN_DV), F32)],
               compiler_params=_cparams(("arbitrary",)), name=name)(qkv, qkv, qkv, g_b, beta_b)


def gdn_chunk_bwd(qkv, g_b, beta_b, states, do, name):
    S_ = qkv.shape[0]
    N = S_ // GDN_C

    def body(q_ref, k_ref, v_ref, g_ref, b_ref, s_ref, do_ref, dq_ref, dk_ref, dv_ref, dg_ref, db_ref, dstate):
        @pl.when(pl.program_id(0) == 0)
        def _():
            dstate[...] = jnp.zeros_like(dstate)

        _, vjp = jax.vjp(_gdn_chunk, _head_slices(q_ref, GDN_DK), _head_slices(k_ref, GDN_DK), _head_slices(v_ref, GDN_DV),
                         _head_slices(g_ref, GDN_DK), _head_slices(b_ref, GDN_DK), [s_ref[0, h] for h in range(GDN_H)])
        dq, dk, dv, dg, db, ds = vjp((_head_slices(do_ref, GDN_DV), [dstate[h] for h in range(GDN_H)]))
        for h in range(GDN_H):
            kk, vv = slice(h * GDN_DK, (h + 1) * GDN_DK), slice(h * GDN_DV, (h + 1) * GDN_DV)
            dq_ref[:, kk] = dq[h]
            dk_ref[:, kk] = dk[h]
            dv_ref[:, vv] = dv[h]
            dg_ref[:, kk] = dg[h]
            db_ref[:, kk] = db[h]
            dstate[h] = ds[h]

    r = lambda n: N - 1 - n
    qk = lambda c: pl.BlockSpec((GDN_C, GDN_QK), lambda n: (r(n), c))
    vs = lambda c: pl.BlockSpec((GDN_C, GDN_V), lambda n: (r(n), c))
    return _pc(body, grid=(N,),
               in_specs=[qk(0), qk(1), vs(1), qk(0), qk(0),
                         pl.BlockSpec((1, GDN_H, GDN_DK, GDN_DV), lambda n: (r(n), 0, 0, 0)), vs(0)],
               out_specs=[qk(0), qk(0), vs(0), qk(0), qk(0)],
               out_shape=[_sds((S_, GDN_QK), F32), _sds((S_, GDN_QK), F32), _sds((S_, GDN_V), F32),
                          _sds((S_, GDN_QK), F32), _sds((S_, GDN_QK), F32)],
               scratch_shapes=[pltpu.VMEM((GDN_H, GDN_DK, GDN_DV), F32)],
               compiler_params=_cparams(("arbitrary",)), name=name)(qkv, qkv, qkv, g_b, beta_b, states, do)


def _rope_tables(pos_ref, inv_ref, cm_ref, sg_ref):
    ang = pos_ref[...] * inv_ref[...]
    return jnp.cos(ang) * cm_ref[...], jnp.sin(ang) * sg_ref[...]


def mla_prep_fwd(qpad, kv, proj, pos, rope_consts, name):
    S_ = qpad.shape[0]
    ts = 256
    W = 2 * LANES

    def body(q_ref, kv_ref, kr_ref, pos_ref, inv_ref, cm_ref, sg_ref, qh_ref, kh_ref, vh_ref):
        cs, sn = _rope_tables(pos_ref, inv_ref, cm_ref, sg_ref)
        rope = lambda r: r * cs + pltpu.roll(r, LANES // 2, 1) * sn
        krr = rope(kr_ref[...]).astype(BF16)
        for h in range(MLA_H):
            qh_ref[h, :, 0:LANES] = (q_ref[:, h * W:h * W + LANES] * MLA_SCALE).astype(BF16)
            qh_ref[h, :, LANES:W] = (rope(q_ref[:, h * W + LANES:(h + 1) * W]) * MLA_SCALE).astype(BF16)
            kh_ref[h, :, 0:LANES] = kv_ref[:, h * W:h * W + LANES].astype(BF16)
            kh_ref[h, :, LANES:W] = krr
            vh_ref[h] = kv_ref[:, h * W + LANES:(h + 1) * W].astype(BF16)

    one = pl.BlockSpec((1, LANES), lambda i: (0, 0))
    return _pc(body, grid=(S_ // ts,),
               in_specs=[pl.BlockSpec((ts, MLA_H * W), lambda i: (i, 0)), pl.BlockSpec((ts, MLA_H * W), lambda i: (i, 0)),
                         pl.BlockSpec((ts, LANES), lambda i: (i, 1536 // LANES)), pl.BlockSpec((ts, 1), lambda i: (i, 0)),
                         one, one, one],
               out_specs=[pl.BlockSpec((MLA_H, ts, W), lambda i: (0, i, 0)), pl.BlockSpec((MLA_H, ts, W), lambda i: (0, i, 0)),
                          pl.BlockSpec((MLA_H, ts, LANES), lambda i: (0, i, 0))],
               out_shape=[_sds((MLA_H, S_, W), BF16), _sds((MLA_H, S_, W), BF16), _sds((MLA_H, S_, LANES), BF16)],
               compiler_params=_cparams(("parallel",)), name=name)(qpad, kv, proj, pos, *rope_consts)


def mla_prep_bwd(dqh, dkh, dvh, pos, rope_consts, into, name):
    S_ = dqh.shape[1]
    ts = 256
    W = 2 * LANES

    def body(dq_ref, dk_ref, dv_ref, pos_ref, inv_ref, cm_ref, sg_ref, _, dqp_ref, dkv_ref, dkr_ref):
        cs, sn = _rope_tables(pos_ref, inv_ref, cm_ref, sg_ref)
        rope_t = lambda g: g * cs + pltpu.roll(g * sn, LANES // 2, 1)
        acc = jnp.zeros((ts, LANES), F32)
        for h in range(MLA_H):
            dqp_ref[:, h * W:h * W + LANES] = (dq_ref[h, :, 0:LANES].astype(F32) * MLA_SCALE).astype(BF16)
            dqp_ref[:, h * W + LANES:(h + 1) * W] = (rope_t(dq_ref[h, :, LANES:W].astype(F32)) * MLA_SCALE).astype(BF16)
            dkv_ref[:, h * W:h * W + LANES] = dk_ref[h, :, 0:LANES]
            dkv_ref[:, h * W + LANES:(h + 1) * W] = dv_ref[h]
            acc = acc + dk_ref[h, :, LANES:W].astype(F32)
        dkr_ref[...] = rope_t(acc).astype(dkr_ref.dtype)

    one = pl.BlockSpec((1, LANES), lambda i: (0, 0))
    return _pc(body, grid=(S_ // ts,),
               in_specs=[pl.BlockSpec((MLA_H, ts, W), lambda i: (0, i, 0)), pl.BlockSpec((MLA_H, ts, W), lambda i: (0, i, 0)),
                         pl.BlockSpec((MLA_H, ts, LANES), lambda i: (0, i, 0)), pl.BlockSpec((ts, 1), lambda i: (i, 0)),
                         one, one, one, pl.BlockSpec(memory_space=pl.ANY)],
               out_specs=[pl.BlockSpec((ts, MLA_H * W), lambda i: (i, 0)), pl.BlockSpec((ts, MLA_H * W), lambda i: (i, 0)),
                          pl.BlockSpec((ts, LANES), lambda i: (i, 1536 // LANES))],
               out_shape=[_sds((S_, MLA_H * W), BF16), _sds((S_, MLA_H * W), BF16), _sds(into.shape, into.dtype)],
               input_output_aliases={7: 2}, compiler_params=_cparams(("parallel",)), name=name)(dqh, dkh, dvh, pos, *rope_consts, into)


NEG = -1e30


FLASH_TILE = 1024


def _scores(q, k, diagonal):
    s = lax.dot_general(q, k, (_NT, ((), ())), preferred_element_type=F32)
    if not diagonal:
        return s
    t = s.shape[0]
    return jnp.where(lax.broadcasted_iota(I32, (t, t), 1) <= lax.broadcasted_iota(I32, (t, t), 0), s, NEG)


def flash_fwd(qh, kh, vh, name):
    H, S_, W = qh.shape
    t = _tile(S_, FLASH_TILE)
    n = S_ // t

    def body(q_ref, k_ref, v_ref, o_ref, lse_ref, m_s, l_s, acc):
        qi, kj = pl.program_id(1), pl.program_id(2)

        @pl.when(kj == 0)
        def _():
            m_s[...] = jnp.full_like(m_s, NEG)
            l_s[...] = jnp.zeros_like(l_s)
            acc[...] = jnp.zeros_like(acc)

        def step(diagonal):
            s = _scores(q_ref[...], k_ref[...], diagonal)
            m_old = m_s[...]
            m_new = jnp.maximum(m_old, jnp.max(s, axis=-1, keepdims=True))
            alpha = jnp.exp(m_old - m_new)
            p = jnp.exp(s - m_new[:, :1])
            l_s[...] = alpha * l_s[...] + jnp.sum(p, axis=-1, keepdims=True)
            acc[...] = alpha * acc[...] + lax.dot_general(p.astype(BF16), v_ref[...], (_NN, ((), ())), preferred_element_type=F32)
            m_s[...] = m_new

        pl.when(kj < qi)(lambda: step(False))
        pl.when(kj == qi)(lambda: step(True))

        @pl.when(kj == n - 1)
        def _():
            o_ref[...] = acc[...] / l_s[...]
            lse_ref[...] = m_s[...] + jnp.log(l_s[...])

    return _pc(body, grid=(H, n, n),
               in_specs=[pl.BlockSpec((None, t, W), lambda h, i, j: (h, i, 0)),
                         pl.BlockSpec((None, t, W), lambda h, i, j: (h, jnp.minimum(i, j), 0)),
                         pl.BlockSpec((None, t, LANES), lambda h, i, j: (h, jnp.minimum(i, j), 0))],
               out_specs=[pl.BlockSpec((t, LANES), lambda h, i, j: (i, h)), pl.BlockSpec((None, t, LANES), lambda h, i, j: (h, i, 0))],
               out_shape=[_sds((S_, H * LANES), F32), _sds((H, S_, LANES), F32)],
               scratch_shapes=[pltpu.VMEM((t, LANES), F32)] * 3,
               compiler_params=_cparams(("parallel", "parallel", "arbitrary")), name=name)(qh, kh, vh)


def flash_bwd(qh, kh, vh, o, lse, do, name):
    H, S_, W = qh.shape
    t = _tile(S_, FLASH_TILE)
    n = S_ // t

    def body(q_ref, k_ref, v_ref, o_ref, lse_ref, do_ref, dq_ref, dk_ref, dv_ref, dq_acc, dk_acc, dv_acc):
        kj, qi = pl.program_id(1), pl.program_id(2)

        @pl.when(jnp.logical_and(kj == 0, qi == 0))
        def _():
            dq_acc[...] = jnp.zeros_like(dq_acc)

        @pl.when(qi == 0)
        def _():
            dk_acc[...] = jnp.zeros_like(dk_acc)
            dv_acc[...] = jnp.zeros_like(dv_acc)

        def step(diagonal):
            q, k, v = q_ref[...], k_ref[...], v_ref[...]
            do_ = do_ref[...]
            p = jnp.exp(_scores(q, k, diagonal) - lse_ref[:, :1])
            dob = do_.astype(BF16)
            dv_acc[...] += lax.dot_general(p.astype(BF16), dob, (_TN, ((), ())), preferred_element_type=F32)
            dp = lax.dot_general(dob, v, (_NT, ((), ())), preferred_element_type=F32)
            delta = jnp.sum(do_ * o_ref[...], axis=-1, keepdims=True)
            ds = (p * (dp - delta)).astype(BF16)
            dk_acc[...] += lax.dot_general(ds, q, (_TN, ((), ())), preferred_element_type=F32)
            rows = pl.ds(pl.multiple_of(qi * t, t), t)
            dq_acc[rows, :] += lax.dot_general(ds, k, (_NN, ((), ())), preferred_element_type=F32)

        pl.when(qi > kj)(lambda: step(False))
        pl.when(qi == kj)(lambda: step(True))

        @pl.when(qi == n - 1)
        def _():
            dk_ref[...] = dk_acc[...].astype(BF16)
            dv_ref[...] = dv_acc[...].astype(BF16)

        @pl.when(jnp.logical_and(kj == n - 1, qi == n - 1))
        def _():
            dq_ref[...] = dq_acc[...].astype(BF16)

    qrow = lambda h, j, i: jnp.maximum(i, j)
    return _pc(body, grid=(H, n, n),
               in_specs=[pl.BlockSpec((None, t, W), lambda h, j, i: (h, qrow(h, j, i), 0)),
                         pl.BlockSpec((None, t, W), lambda h, j, i: (h, j, 0)),
                         pl.BlockSpec((None, t, LANES), lambda h, j, i: (h, j, 0)),
                         pl.BlockSpec((t, LANES), lambda h, j, i: (qrow(h, j, i), h)),
                         pl.BlockSpec((None, t, LANES), lambda h, j, i: (h, qrow(h, j, i), 0)),
                         pl.BlockSpec((t, LANES), lambda h, j, i: (qrow(h, j, i), h))],
               out_specs=[pl.BlockSpec((None, S_, W), lambda h, j, i: (h, 0, 0)),
                          pl.BlockSpec((None, t, W), lambda h, j, i: (h, j, 0)),
                          pl.BlockSpec((None, t, LANES), lambda h, j, i: (h, j, 0))],
               out_shape=[_sds((H, S_, W), BF16), _sds((H, S_, W), BF16), _sds((H, S_, LANES), BF16)],
               scratch_shapes=[pltpu.VMEM((S_, W), F32), pltpu.VMEM((t, W), F32), pltpu.VMEM((t, LANES), F32)],
               compiler_params=_cparams(("parallel", "arbitrary", "arbitrary")), name=name)(qh, kh, vh, o, lse, do)


def loss_head(x, target, g, name):
    S_ = x.shape[0]
    ts = 256

    def body(x_ref, t_ref, g_ref, l_ref, dx_ref, dg_ref):
        @pl.when(pl.program_id(0) == 0)
        def _():
            l_ref[...] = jnp.zeros_like(l_ref)
            dg_ref[...] = jnp.zeros_like(dg_ref)

        y, vjp = jax.vjp(_rms, x_ref[...], g_ref[...])
        err = y - t_ref[...]
        l_ref[...] += 0.5 * jnp.sum(jnp.sum(err * err, axis=-1, keepdims=True), axis=0, keepdims=True) / D
        dx, dg = vjp(err / D)
        dx_ref[...] = dx
        dg_ref[...] += dg

    row = pl.BlockSpec((ts, D), lambda i: (i, 0))
    return _pc(body, grid=(S_ // ts,), in_specs=[row, row, pl.BlockSpec((1, D), lambda i: (0, 0))],
               out_specs=[pl.BlockSpec((1, LANES), lambda i: (0, 0)), row, pl.BlockSpec((1, D), lambda i: (0, 0))],
               out_shape=[_sds((1, LANES), F32), _sds((S_, D), F32), _sds((1, D), F32)],
               compiler_params=_cparams(("arbitrary",)), name=name)(x, target, g)


def adamw(w, parts, m, v, name):
    R, C = w.shape
    rows = [p.shape[1] for p in parts[0]]
    tr = R
    for cand in (512, 256, 128, 64, 32, 16, 8):
        if all(r % cand == 0 for r in rows) and cand * C * 4 <= 1024 * 1024:
            tr = cand
            break
    c1 = 1.0 - ADAM_B1 ** ADAM_STEP
    c2 = 1.0 - ADAM_B2 ** ADAM_STEP
    starts = [sum(rows[:k]) // tr for k in range(len(rows))]
    flat = [p for part in parts for p in part]

    def body(*refs):
        w_ref, m_ref, v_ref = refs[0], refs[1 + len(flat)], refs[2 + len(flat)]
        g_ref, d_ref, nm_ref, nv_ref = refs[3 + len(flat):]
        i = pl.program_id(0)
        gg, at = None, 1
        for part in parts:
            val = None
            for k in range(len(part)):
                p_ref = refs[at]
                at += 1
                s = p_ref[0].astype(F32)
                for n in range(1, p_ref.shape[0]):
                    s = s + p_ref[n].astype(F32)
                val = s if val is None else jnp.where(i >= starts[k], s, val)
            gg = val if gg is None else gg + val
        m2 = ADAM_B1 * m_ref[...] + (1.0 - ADAM_B1) * gg
        v2 = ADAM_B2 * v_ref[...] + (1.0 - ADAM_B2) * (gg * gg)
        g_ref[...] = gg
        d_ref[...] = -ADAM_LR * ((m2 / c1) / (jnp.sqrt(v2 / c2) + ADAM_EPS) + ADAM_WD * w_ref[...])
        nm_ref[...] = m2
        nv_ref[...] = v2

    blk = pl.BlockSpec((tr, C), lambda i: (i, 0))
    piece = lambda p, k: pl.BlockSpec((p.shape[0], tr, C), lambda i: (0, jnp.clip(i - starts[k], 0, rows[k] // tr - 1), 0))
    pblk = [piece(p, k) for part in parts for k, p in enumerate(part)]
    return _pc(body, grid=(R // tr,), in_specs=[blk] + pblk + [blk, blk], out_specs=[blk] * 4, out_shape=[_sds((R, C), F32)] * 4,
               compiler_params=_cparams(("parallel",)), name=name)(w, *flat, m, v)


def sum_slots(own, recv, skip, name):
    n, R, C = recv.shape
    tr = _tile(R, 512) if R % LANES == 0 else R
    has_own = own is not None

    def body(*refs):
        skip_ref = refs[0]
        r_ref, o_ref = refs[-2], refs[-1]
        acc = refs[1][...] if has_own else jnp.zeros(o_ref.shape, F32)
        for s in range(n):
            acc = acc + jnp.where(skip_ref[0] == s, 0.0, r_ref[s].astype(F32))
        o_ref[...] = acc

    row = pl.BlockSpec((tr, C), lambda i, sk: (i, 0))
    gs = pltpu.PrefetchScalarGridSpec(
        num_scalar_prefetch=1, grid=(R // tr,),
        in_specs=([row] if has_own else []) + [pl.BlockSpec((n, tr, C), lambda i, sk: (0, i, 0))], out_specs=row)
    ins = ([own] if has_own else []) + [recv]
    return _pc(body, grid_spec=gs, out_shape=_sds((R, C), F32), compiler_params=_cparams(("parallel",)), name=name)(skip, *ins)


def _chip_peers():
    x, y, c = lax.axis_index("x"), lax.axis_index("y"), lax.axis_index("c")
    return (x, y, c), [(1 - x, y, c), (x, 1 - y, c), (1 - x, 1 - y, c)]


def _chip_index(p):
    return 2 * p[0] + p[1]


def _win(ref, axis, chip, size):
    if axis is None:
        return ref.at[chip]
    idx = [slice(None)] * len(ref.shape)
    idx[axis] = pl.ds(pl.multiple_of(chip * size, size), size)
    return ref.at[tuple(idx)]


def _remote(src, dst, send_sem, recv_sem, peer):
    return pltpu.make_async_remote_copy(src_ref=src, dst_ref=dst, send_sem=send_sem, recv_sem=recv_sem, device_id=peer,
                                        device_id_type=MESH)


HBM_SPEC = pl.BlockSpec(memory_space=pltpu.HBM)
SEM_SPEC = pl.BlockSpec(memory_space=pltpu.SEMAPHORE)
ANY_SPEC = pl.BlockSpec(memory_space=pl.ANY)
DATAFLOW = pltpu.SideEffectType.DATAFLOW_SIDE_EFFECTING


def gather_piece(i, l, o, axis, size):
    return (i, lambda r, chip: r.at[l], o, lambda r, chip: _win(r, axis, chip, size))


def scatter_piece(i, o, axis, size):
    return (i, lambda r, chip: _win(r, axis, chip, size), o, lambda r, chip: r.at[chip])


def _copies(pieces, in_refs, out_refs, send, recv):
    me, peers = _chip_peers()
    mine = _chip_index(me)
    remote = []
    for n, (i, src, o, dst) in enumerate(pieces):
        d = dst(out_refs[o], mine)
        remote += [_remote(src(in_refs[i], _chip_index(p)), d, send.at[3 * n + k], recv.at[3 * n + k], p)
                   for k, p in enumerate(peers)]
    return remote


def own_window(a, axis, size, chip):
    if axis is None:
        return lax.dynamic_index_in_dim(a, chip, 0, keepdims=False)
    return lax.dynamic_slice_in_dim(a, chip * size, size, axis=axis)


def place_own(land, own, axis, size, chip):
    if axis is None:
        return lax.dynamic_update_slice_in_dim(land, own[None], chip, axis=0)
    return lax.dynamic_update_slice_in_dim(land, own, chip * size, axis=axis)


def exchange_start(pieces, ins, out_shapes, after, name):
    n_in, n_out, ncp = len(ins), len(out_shapes), len(pieces)

    def body(*refs):
        in_refs, land_refs = refs[:n_in], refs[n_in:n_in + n_out]
        send, recv = refs[n_in + n_out + 1], refs[n_in + n_out + 2]
        token = refs[-1]
        for cp in _copies(pieces, in_refs, land_refs, send, recv):
            cp.start()
        token[...] = jnp.zeros_like(token)

    hbm = lambda a: pltpu.with_memory_space_constraint(a, pltpu.HBM)
    lands = [hbm(lax.empty(s.shape, s.dtype)) for s in out_shapes]
    sem = pltpu.SemaphoreType.DMA((3 * ncp,))
    thru = [pltpu.HBM(a.shape, a.dtype) for a in ins] + [pltpu.HBM(s.shape, s.dtype) for s in out_shapes]
    res = _pc(body, in_specs=[HBM_SPEC] * (n_in + n_out) + [ANY_SPEC],
              out_specs=[SEM_SPEC, SEM_SPEC] + [HBM_SPEC] * (n_in + n_out) + [pl.BlockSpec(memory_space=pltpu.VMEM)],
              out_shape=[sem, sem] + thru + [_sds((8, LANES), F32)],
              input_output_aliases={i: 2 + i for i in range(n_in + n_out)},
              compiler_params=pltpu.CompilerParams(has_side_effects=DATAFLOW), name=name)(*[hbm(a) for a in ins], *lands, after)
    return (res[0], res[1]), list(res[2:2 + n_in]), list(res[2 + n_in:2 + n_in + n_out]), res[-1]


def exchange_wait(pieces, sems, ins, lands, after, name):
    n_in, n_out = len(ins), len(lands)

    def body(*refs):
        in_refs, land_refs = refs[:n_in], refs[n_in:n_in + n_out]
        send, recv = refs[n_in + n_out], refs[n_in + n_out + 1]
        for cp in _copies(pieces, in_refs, land_refs, send, recv):
            cp.wait_send()
            cp.wait_recv()

    thru = [pltpu.HBM(a.shape, a.dtype) for a in ins] + [pltpu.HBM(a.shape, a.dtype) for a in lands]
    res = _pc(body, in_specs=[HBM_SPEC] * (n_in + n_out) + [SEM_SPEC, SEM_SPEC, ANY_SPEC], out_specs=[HBM_SPEC] * (n_in + n_out),
              out_shape=thru, input_output_aliases={i: i for i in range(n_in + n_out)},
              compiler_params=pltpu.CompilerParams(has_side_effects=DATAFLOW), name=name)(*ins, *lands, sems[0], sems[1], after)
    return list(res[:n_in]), list(res[n_in:])


def swap_cores(bufs, name):
    nb = len(bufs)

    def body(*refs):
        in_refs, out_refs = refs[:nb], refs[nb:2 * nb]
        send, recv = refs[2 * nb:]
        x, y, c = lax.axis_index("x"), lax.axis_index("y"), lax.axis_index("c")
        copies = [_remote(in_refs[b], out_refs[b], send.at[b], recv.at[b], (x, y, 1 - c)) for b in range(nb)]
        for cp in copies:
            cp.start()
        for cp in copies:
            cp.wait()

    anyspec = pl.BlockSpec(memory_space=pl.ANY)
    return _pc(body, in_specs=[anyspec] * nb, out_specs=[anyspec] * nb, out_shape=[_sds(b.shape, b.dtype) for b in bufs],
               scratch_shapes=[pltpu.SemaphoreType.DMA((nb,)), pltpu.SemaphoreType.DMA((nb,))], name=name)(*bufs)


def exchange_all(buf, name):
    def body(in_ref, out_ref, send, recv, local):
        x, y, c = lax.axis_index("x"), lax.axis_index("y"), lax.axis_index("c")
        mine = 4 * x + 2 * y + c
        loc = pltpu.make_async_copy(in_ref, out_ref.at[mine], local)
        loc.start()
        copies = [loc]
        for k in range(1, 8):
            peer = (x ^ (k >> 2), y ^ ((k >> 1) & 1), c ^ (k & 1))
            cp = pltpu.make_async_remote_copy(src_ref=in_ref, dst_ref=out_ref.at[mine], send_sem=send.at[k - 1],
                                              recv_sem=recv.at[k - 1], device_id=peer, device_id_type=MESH)
            cp.start()
            copies.append(cp)
        for cp in copies:
            cp.wait()

    anyspec = pl.BlockSpec(memory_space=pl.ANY)
    return _pc(body, in_specs=[anyspec], out_specs=anyspec, out_shape=_sds((8,) + buf.shape, buf.dtype),
               scratch_shapes=[pltpu.SemaphoreType.DMA((7,)), pltpu.SemaphoreType.DMA((7,)), pltpu.SemaphoreType.DMA],
               name=name)(buf)


def _norm_fwd(x, g, name):
    return rowwise(f_rms, [(x, D, 0, 0)], [(g, D, 0, 0)], [(D, 0, BF16)], ts=512, name=name)[0]


def _norm_bwd(x, g, dh, dres, name):
    (dx,), (dg,) = rowwise_bwd(f_rms, [(x, D, 0, 0)], [(g, D, 0, 0)], [(dh, D, 0, 0)], need=[True],
                               adds={0: (dres, D, 0, 0)}, ts=256, name=name)
    return dx, dg


def pool_fwd(x, W, tag):
    h = _norm_fwd(x, W["ng"], tag + "_norm")
    proj = mm(h, W["w_in"], name=tag + "_in")
    p = pool_time_fwd(proj, tag + "_win")
    pg = gmm("nn", p, W["w_grp"], G=4, name=tag + "_grp")
    y = rowwise(f_pool_gate, [(pg, POOL_GROUP, 0, 1), (proj, POOL_GROUP, 4, 1)], [(W["scale"], POOL_GROUP, 0, 1)],
                [(POOL_GROUP, 1, BF16)], ncol=4, ts=512, name=tag + "_gate")[0]
    xn = mm(y, W["w_out"], add=x, name=tag + "_out")
    return xn, (x, h, proj, p, pg, y)


def pool_bwd(dxn, W, saved, tag, after=None):
    x, h, proj, p, pg, y = saved
    dy = mm(dxn, W["w_out"], tb=True, after=after, name=tag + "_dy")
    g = {"w_out": mm(y, dxn, ta=True, out_dtype=BF16, name=tag + "_dwout")}
    (dpg, dproj), (g["scale"],) = rowwise_bwd(
        f_pool_gate, [(pg, POOL_GROUP, 0, 1), (proj, POOL_GROUP, 4, 1)], [(W["scale"], POOL_GROUP, 0, 1)],
        [(dy, POOL_GROUP, 0, 1)], need=[True, True], place={1: (2 * POOL_WIDTH, 4)}, narrow=(0, 1), ncol=4, ts=512, name=tag + "_dgate")
    dp = gmm("nt", dpg, W["w_grp"], G=4, name=tag + "_dp")
    g["w_grp"] = gmm("tn", p, dpg, G=4, out_dtype=BF16, name=tag + "_dwgrp")
    dproj = pool_time_bwd(dp, dproj, tag + "_dwin")
    dh = mm(dproj, W["w_in"], tb=True, name=tag + "_dh")
    g["w_in"] = mm(h, dproj, ta=True, out_dtype=BF16, name=tag + "_dw_in")
    dx, g["ng"] = _norm_bwd(x, W["ng"], dh, dxn, tag + "_dnorm")
    return dx, g


def gdn_fwd(x, W, tag):
    h = _norm_fwd(x, W["ng"], tag + "_norm")
    proj = mm(h, W["w_in"], name=tag + "_in")
    qkv = gdn_conv_fwd(proj, W["conv"], tag + "_conv")
    g_b, beta_b = rowwise(f_gdn_gates, [(proj, LANES, 6144 // LANES, 0)], [(W["a_log"], LANES, 0, 0), (W["dt_bias"], LANES, 0, 0)],
                          [(GDN_QK, 0, F32), (GDN_QK, 0, F32)], ts=512, name=tag + "_gates")
    o, states = gdn_chunk_fwd(qkv, g_b, beta_b, tag + "_chunk")
    og = rowwise(f_gdn_out, [(o, GDN_DV, 0, 1), (proj, GDN_DV, 4096 // GDN_DV, 1)], [(W["norm_g"], GDN_DV, 0, 0)],
                 [(GDN_DV, 1, BF16)], ncol=GDN_H, ts=512, name=tag + "_onorm")[0]
    xn = mm(og, W["w_out"], add=x, name=tag + "_out")
    return xn, (x, h, proj, qkv, g_b, beta_b, o, states, og)


def gdn_bwd(dxn, W, saved, tag, after=None):
    x, h, proj, qkv, g_b, beta_b, o, states, og = saved
    dog = mm(dxn, W["w_out"], tb=True, after=after, name=tag + "_dog")
    g = {"w_out": mm(og, dxn, ta=True, out_dtype=BF16, name=tag + "_dwout")}
    (do, dproj), (g["norm_g"],) = rowwise_bwd(
        f_gdn_out, [(o, GDN_DV, 0, 1), (proj, GDN_DV, 4096 // GDN_DV, 1)], [(W["norm_g"], GDN_DV, 0, 0)],
        [(dog, GDN_DV, 0, 1)], need=[True, True], place={1: (GDN_IN_PAD, 4096 // GDN_DV)}, narrow=(1,), ncol=GDN_H, ts=512, name=tag + "_donorm")
    dq, dk, dv, dg_b, dbeta_b = gdn_chunk_bwd(qkv, g_b, beta_b, states, do, tag + "_dchunk")
    (dproj,), (g["a_log"], g["dt_bias"]) = rowwise_bwd(
        f_gdn_gates, [(proj, LANES, 6144 // LANES, 0)], [(W["a_log"], LANES, 0, 0), (W["dt_bias"], LANES, 0, 0)],
        [(dg_b, GDN_QK, 0, 0), (dbeta_b, GDN_QK, 0, 0)], need=[True], place={0: (dproj, 6144 // LANES)}, ts=256, name=tag + "_dgates")
    dproj, g["conv"] = gdn_conv_bwd(proj, W["conv"], dq, dk, dv, dproj, tag + "_dconv")
    dh = mm(dproj, W["w_in"], tb=True, name=tag + "_dh")
    g["w_in"] = mm(h, dproj, ta=True, out_dtype=BF16, name=tag + "_dw_in")
    dx, g["ng"] = _norm_bwd(x, W["ng"], dh, dxn, tag + "_dnorm")
    return dx, g


def mla_fwd(x, pos, W, tag):
    h = _norm_fwd(x, W["ng"], tag + "_norm")
    proj = mm(h, W["w_in"], name=tag + "_in")
    hq = rowwise(f_rms, [(proj, MLA_Q_LORA, 0, 0)], [(W["q_g"], MLA_Q_LORA, 0, 0)], [(MLA_Q_LORA, 0, BF16)], ts=512, name=tag + "_qnorm")[0]
    hkv = rowwise(f_rms, [(proj, MLA_KV_LORA, 2, 0)], [(W["kv_g"], MLA_KV_LORA, 0, 0)], [(MLA_KV_LORA, 0, BF16)], ts=512, name=tag + "_kvnorm")[0]
    qpad = mm(hq, W["w_uq"], name=tag + "_uq")
    kv = mm(hkv, W["w_ukv"], name=tag + "_ukv")
    qh, kh, vh = mla_prep_fwd(qpad, kv, proj, pos, W["rope"], tag + "_prep")
    o, lse = flash_fwd(qh, kh, vh, tag + "_attn")
    og = rowwise(f_ogate, [(o, 512, 0, 1), (proj, 512, 4, 1)], [], [(512, 1, BF16)], ncol=4, ts=512, name=tag + "_ogate")[0]
    xn = mm(og, W["w_out"], add=x, name=tag + "_out")
    return xn, (x, h, proj, hq, hkv, qh, kh, vh, o, lse, og)


def mla_bwd(dxn, pos, W, saved, tag, after=None):
    x, h, proj, hq, hkv, qh, kh, vh, o, lse, og = saved
    dog = mm(dxn, W["w_out"], tb=True, after=after, name=tag + "_dog")
    g = {"w_out": mm(og, dxn, ta=True, out_dtype=BF16, name=tag + "_dwout")}
    dproj = jnp.zeros(proj.shape, BF16)
    (do, dproj), _ = rowwise_bwd(f_ogate, [(o, 512, 0, 1), (proj, 512, 4, 1)], [], [(dog, 512, 0, 1)], need=[True, True],
                                 place={1: (dproj, 4)}, ncol=4, ts=512, name=tag + "_dogate")
    dqh, dkh, dvh = flash_bwd(qh, kh, vh, o, lse, do, tag + "_dattn")
    dqpad, dkv, dproj = mla_prep_bwd(dqh, dkh, dvh, pos, W["rope"], dproj, tag + "_dprep")
    dhq = mm(dqpad, W["w_uq"], tb=True, name=tag + "_dhq")
    g["w_uq"] = mm(hq, dqpad, ta=True, out_dtype=BF16, name=tag + "_dwuq")
    dhkv = mm(dkv, W["w_ukv"], tb=True, name=tag + "_dhkv")
    g["w_ukv"] = mm(hkv, dkv, ta=True, out_dtype=BF16, name=tag + "_dwukv")
    (dproj,), (g["q_g"],) = rowwise_bwd(f_rms, [(proj, MLA_Q_LORA, 0, 0)], [(W["q_g"], MLA_Q_LORA, 0, 0)], [(dhq, MLA_Q_LORA, 0, 0)],
                                        need=[True], place={0: (dproj, 0)}, ts=256, name=tag + "_dqnorm")
    (dproj,), (g["kv_g"],) = rowwise_bwd(f_rms, [(proj, MLA_KV_LORA, 2, 0)], [(W["kv_g"], MLA_KV_LORA, 0, 0)], [(dhkv, MLA_KV_LORA, 0, 0)],
                                         need=[True], place={0: (dproj, 2)}, ts=256, name=tag + "_dkvnorm")
    dh = mm(dproj, W["w_in"], tb=True, name=tag + "_dh")
    g["w_in"] = mm(h, dproj, ta=True, out_dtype=BF16, name=tag + "_dw_in")
    dx, g["ng"] = _norm_bwd(x, W["ng"], dh, dxn, tag + "_dnorm")
    return dx, g


def _pad_cols(a, n):
    return jnp.pad(a, ((0, 0), (0, n - a.shape[1])))


def _mla_w_in_layout(w):
    z = lambda n: jnp.zeros((w.shape[0], n), w.dtype)
    kr = w[:, 1280:1344]
    return jnp.concatenate([w[:, :768], z(256), w[:, 768:1280], kr[:, :32], z(32), kr[:, 32:], z(32), z(384), w[:, 1344:]], axis=1)


def _mla_w_in_unlayout(g):
    return jnp.concatenate([g[:, :768], g[:, 1024:1536], g[:, 1536:1568], g[:, 1600:1632], g[:, 2048:]], axis=1)


def _mla_w_uq_layout(w):
    w3 = w.reshape(w.shape[0], MLA_H, MLA_NOPE + MLA_ROPE)
    z = jnp.zeros((w.shape[0], MLA_H, 32), w.dtype)
    return jnp.concatenate([w3[..., :128], w3[..., 128:160], z, w3[..., 160:192], z], axis=-1).reshape(w.shape[0], MLA_H * 256)


def _mla_w_uq_unlayout(g):
    g3 = g.reshape(g.shape[0], MLA_H, 256)
    return jnp.concatenate([g3[..., :128], g3[..., 128:160], g3[..., 192:224]], axis=-1).reshape(g.shape[0], MLA_H * 192)


def _rope_consts():
    half = MLA_ROPE // 2
    inv = ROPE_THETA ** (-jnp.arange(half, dtype=F32) / half)
    z = jnp.zeros((half,), F32)
    o = jnp.ones((half,), F32)
    row = lambda *p: jnp.concatenate(p).reshape(1, LANES)
    return row(inv, z, inv, z), row(o, z, o, z), row(-o, z, o, z)


BIG = ["pool_w_in", "pool_w_grp", "pool_w_out", "gdn_w_in", "gdn_w_out", "mla_w_in", "mla_w_uq", "mla_w_ukv", "mla_w_out"]
BIG_LAYOUT = {"pool_w_in": (1, 1024, (1024, 4096)), "pool_w_grp": (1, 128, (4, 512, 512)), "pool_w_out": (0, 512, (2048, 1024)),
              "gdn_w_in": (None, None, (4, 1024, 1540)), "gdn_w_out": (0, 512, (2048, 1024)),
              "mla_w_in": (None, None, (4, 1024, 848)), "mla_w_uq": (1, 768, (768, 3072)), "mla_w_ukv": (1, 1024, (512, 4096)),
              "mla_w_out": (0, 512, (2048, 1024))}
SMALL_SHARDED = ["pool_scale", "gdn_conv", "mla_q_norm_g", "mla_kv_norm_g"]
SMALL_AXIS = {"pool_scale": 1, "gdn_conv": 2, "mla_q_norm_g": 1, "mla_kv_norm_g": 1}
REPLICATED = ["norm_g", "gdn_a_log", "gdn_dt_bias", "gdn_norm_g", "final_g"]
PACK_C = 1024


def _pack(parts, dtype, row_mult):
    flat = jnp.concatenate([p.reshape(-1).astype(dtype) for p in parts])
    rows = -(-flat.shape[0] // PACK_C)
    rows = -(-rows // row_mult) * row_mult
    return jnp.pad(flat, (0, rows * PACK_C - flat.shape[0])).reshape(rows, PACK_C)


def _unpack(buf, shapes):
    lead = buf.shape[:-2]
    flat = buf.reshape(lead + (-1,))
    out, off = [], 0
    for s in shapes:
        n = int(np.prod(s))
        out.append(flat[..., off:off + n].reshape(lead + tuple(s)))
        off += n
    return out


def _unshard(g4, axis):
    a = jnp.moveaxis(g4, 0, axis)
    s = a.shape
    return a.reshape(s[:axis] + (s[axis] * s[axis + 1],) + s[axis + 2:])


def _to_shards(a, axis):
    s = a.shape
    return jnp.moveaxis(a.reshape(s[:axis] + (4, s[axis] // 4) + s[axis + 1:]), axis, 0)


def layer_weights(full, small, rep, layer):
    ng = rep["norm_g"][layer:layer + 1]
    side_by_side = lambda a4: jnp.moveaxis(a4, 0, 1).reshape(a4.shape[1], 4 * a4.shape[2])
    if layer in (0, 3):
        j = layer // 3
        return dict(ng=ng, w_in=full[("pool_w_in", j)], w_grp=full[("pool_w_grp", j)], scale=small["pool_scale"][j:j + 1],
                    w_out=full[("pool_w_out", j)])
    if layer == 1:
        return dict(ng=ng, w_in=_pad_cols(side_by_side(full[("gdn_w_in", 0)]), GDN_IN_PAD),
                    conv=jnp.pad(small["gdn_conv"][0], ((0, 4), (0, 0))), a_log=_pad_cols(rep["gdn_a_log"], LANES),
                    dt_bias=_pad_cols(rep["gdn_dt_bias"], LANES), norm_g=rep["gdn_norm_g"], w_out=full[("gdn_w_out", 0)])
    return dict(ng=ng, w_in=_mla_w_in_layout(side_by_side(full[("mla_w_in", 0)])), q_g=small["mla_q_norm_g"],
                kv_g=small["mla_kv_norm_g"], w_uq=_mla_w_uq_layout(full[("mla_w_uq", 0)]), w_ukv=full[("mla_w_ukv", 0)],
                w_out=full[("mla_w_out", 0)], rope=_rope_consts())


def big_grad_pieces(gl):
    g0, g1, g2, g3 = gl
    slots = lambda a: jnp.moveaxis(a.reshape(a.shape[0], 4, a.shape[1] // 4), 1, 0)
    out = {}
    for l, g in ((0, g0), (1, g3)):
        if g is not None:
            out.update({("pool_w_in", l): g["w_in"], ("pool_w_grp", l): g["w_grp"], ("pool_w_out", l): g["w_out"]})
    if g1 is not None:
        out.update({("gdn_w_in", 0): slots(g1["w_in"][:, :GDN_IN]), ("gdn_w_out", 0): g1["w_out"]})
    if g2 is not None:
        out.update({("mla_w_in", 0): slots(_mla_w_in_unlayout(g2["w_in"])), ("mla_w_uq", 0): _mla_w_uq_unlayout(g2["w_uq"]),
                    ("mla_w_ukv", 0): g2["w_ukv"], ("mla_w_out", 0): g2["w_out"]})
    return out


def small_grads(gl, dfinal):
    g0, g1, g2, g3 = gl
    return {"norm_g": jnp.concatenate([g0["ng"], g1["ng"], g2["ng"], g3["ng"]], axis=0),
            "pool_scale": jnp.concatenate([g0["scale"], g3["scale"]], axis=0), "gdn_conv": g1["conv"][None, :4],
            "gdn_a_log": g1["a_log"][:, :GDN_H], "gdn_dt_bias": g1["dt_bias"][:, :GDN_H], "gdn_norm_g": g1["norm_g"],
            "mla_q_norm_g": g2["q_g"], "mla_kv_norm_g": g2["kv_g"], "final_g": dfinal.reshape(D)}


NAMES = ["norm_g", "pool_w_in", "pool_w_grp", "pool_scale", "pool_w_out", "gdn_w_in", "gdn_conv", "gdn_a_log", "gdn_dt_bias",
         "gdn_norm_g", "gdn_w_out", "mla_w_in", "mla_q_norm_g", "mla_w_uq", "mla_kv_norm_g", "mla_w_ukv", "mla_w_out", "final_g"]


def kernel(x, positions, norm_g, pool_w_in, pool_w_grp, pool_scale, pool_w_out, gdn_w_in, gdn_conv, gdn_a_log, gdn_dt_bias, gdn_norm_g, gdn_w_out, mla_w_in, mla_q_norm_g, mla_w_uq, mla_kv_norm_g, mla_w_ukv, mla_w_out, final_g, loss_target, m_norm_g, m_pool_w_in, m_pool_w_grp, m_pool_scale, m_pool_w_out, m_gdn_w_in, m_gdn_conv, m_gdn_a_log, m_gdn_dt_bias, m_gdn_norm_g, m_gdn_w_out, m_mla_w_in, m_mla_q_norm_g, m_mla_w_uq, m_mla_kv_norm_g, m_mla_w_ukv, m_mla_w_out, m_final_g, v_norm_g, v_pool_w_in, v_pool_w_grp, v_pool_scale, v_pool_w_out, v_gdn_w_in, v_gdn_conv, v_gdn_a_log, v_gdn_dt_bias, v_gdn_norm_g, v_gdn_w_out, v_mla_w_in, v_mla_q_norm_g, v_mla_w_uq, v_mla_kv_norm_g, v_mla_w_ukv, v_mla_w_out, v_final_g):
    args = locals()
    w = {n: args[n] for n in NAMES}
    m = {n: args["m_" + n] for n in NAMES}
    v = {n: args["v_" + n] for n in NAMES}
    my_chip = (2 * lax.axis_index("x") + lax.axis_index("y")).astype(I32)
    S_ = x.shape[1]
    x0, pos, target = x[0], positions.reshape(S_, 1).astype(F32), loss_target[0]
    rep = {n: w[n] for n in REPLICATED}

    shard = {(n, l): w[n][l:l + 1].astype(BF16) for n in BIG for l in range(w[n].shape[0])}
    small_shapes = [w[n].shape for n in SMALL_SHARDED]
    shard[("small", 0)] = _pack([w[n] for n in SMALL_SHARDED], F32, 8)[None]
    layout = dict(BIG_LAYOUT, small=(None, None, (4,) + shard[("small", 0)].shape[1:]))

    def gather_start(group, after, tag):
        pieces = [gather_piece(i, 0, i, layout[n][0], layout[n][1]) for i, (n, l) in enumerate(group)]
        shapes = [_sds(layout[n][2], shard[(n, l)].dtype) for n, l in group]
        sems, ins, lands, token = exchange_start(pieces, [shard[k] for k in group], shapes, after, tag + "_start")
        return (pieces, sems, ins, lands), token

    def finish(handle, after, tag):
        return exchange_wait(*handle, after, tag + "_wait")

    def gathered(group, handle, after, tag):
        srcs, lands = finish(handle, after, tag)
        return {(n, l): place_own(a, s[0], layout[n][0], layout[n][1], my_chip) for (n, l), s, a in zip(group, srcs, lands)}

    tied = lambda a, token: a + token[0:1, 0:1]
    group_a = [("pool_w_in", 0), ("pool_w_grp", 0), ("pool_w_out", 0), ("small", 0)]
    group_b = [("gdn_w_in", 0), ("gdn_w_out", 0)]
    group_c = [("mla_w_in", 0), ("mla_w_uq", 0), ("mla_w_ukv", 0), ("mla_w_out", 0), ("pool_w_in", 1), ("pool_w_grp", 1), ("pool_w_out", 1)]
    full = {}
    h_a, t_a = gather_start(group_a, x0, "gather_a")
    full.update(gathered(group_a, h_a, t_a, "gather_a"))
    small = {n: _unshard(a, SMALL_AXIS[n]) for n, a in zip(SMALL_SHARDED, _unpack(full[("small", 0)], small_shapes))}
    h_b, t_b = gather_start(group_b, full[group_a[0]], "gather_b")
    W0 = layer_weights(full, small, rep, 0)
    x1, s0 = pool_fwd(x0, dict(W0, ng=tied(W0["ng"], t_b)), "l0")
    full.update(gathered(group_b, h_b, x1, "gather_b"))
    h_c, t_c = gather_start(group_c, full[group_b[0]], "gather_c")
    W1 = layer_weights(full, small, rep, 1)
    x2, s1 = gdn_fwd(x1, dict(W1, ng=tied(W1["ng"], t_c)), "l1")
    full.update(gathered(group_c, h_c, x2, "gather_c"))
    W2, W3 = layer_weights(full, small, rep, 2), layer_weights(full, small, rep, 3)
    x3, s2 = mla_fwd(x2, pos, W2, "l2")
    x4, s3 = pool_fwd(x3, W3, "l3")
    loss_part, dx4, dfinal = loss_head(x4, target, final_g.reshape(1, D), "loss_head")

    def scatter_start(pieces_of, after, tag):
        keys = list(pieces_of)
        pieces = [scatter_piece(i, i, BIG_LAYOUT[n][0], BIG_LAYOUT[n][1]) for i, (n, l) in enumerate(keys)]
        shapes = [_sds((4,) + tuple(w[n].shape[1:]), BF16) for n, l in keys]
        sems, ins, lands, token = exchange_start(pieces, [pieces_of[k] for k in keys], shapes, after, tag + "_start")
        return keys, (pieces, sems, ins, lands), token

    def scattered(keys, handle, after, tag):
        srcs, lands = finish(handle, after, tag)
        return {(n, l): place_own(a, own_window(g, BIG_LAYOUT[n][0], BIG_LAYOUT[n][1], my_chip), None, None, my_chip)
                for (n, l), g, a in zip(keys, srcs, lands)}

    dx3, g3 = pool_bwd(dx4, W3, s3, "l3")
    k3, h3, t3 = scatter_start(big_grad_pieces((None, None, None, g3)), dx3, "scatter_l3")
    dx2, g2 = mla_bwd(dx3, pos, W2, s2, "l2", after=t3)
    k2, h2, t2 = scatter_start(big_grad_pieces((None, None, g2, None)), dx2, "scatter_l2")
    dx1, g1 = gdn_bwd(dx2, W1, s1, "l1", after=t2)
    k1, h1, t1 = scatter_start(big_grad_pieces((None, g1, None, None)), dx1, "scatter_l1")
    dx0, g0 = pool_bwd(dx1, W0, s0, "l0", after=t1)
    k0, h0, t0 = scatter_start(big_grad_pieces((g0, None, None, None)), dx0, "scatter_l0")
    recv = {}
    for keys, handle, tag in ((k3, h3, "scatter_l3"), (k2, h2, "scatter_l2"), (k1, h1, "scatter_l1"), (k0, h0, "scatter_l0")):
        recv.update(scattered(keys, handle, t0, tag))
    keys = list(recv)
    sib = dict(zip(keys, swap_cores([recv[k] for k in keys], "swap_cores")))

    sg = small_grads((g0, g1, g2, g3), dfinal)
    small_names = SMALL_SHARDED + REPLICATED
    small_buf = _pack([sg[n] for n in small_names] + [loss_part], F32, 8)
    small_sum = sum_slots(None, exchange_all(small_buf, "gather_small"), jnp.full((1,), -1, I32), "sum_small")
    full_small = _unpack(small_sum, [sg[n].shape for n in small_names] + [(1, LANES)])
    loss = full_small[-1][0, 0]
    small_part = {}
    for n, a in zip(small_names, full_small[:-1]):
        if n in SMALL_AXIS:
            a = lax.dynamic_index_in_dim(_to_shards(a, SMALL_AXIS[n]), my_chip, axis=0, keepdims=False)
        small_part[n] = a

    outs = []
    for n in NAMES:
        shp = w[n].shape
        two = (int(np.prod(shp[:-1])), shp[-1]) if len(shp) > 1 else (1, shp[0])
        if n in BIG_LAYOUT:
            layers = shp[0]
            rows = lambda a: a.reshape(4, two[0] // layers, two[1])
            parts = [[rows(recv[(n, l)]) for l in range(layers)], [rows(sib[(n, l)]) for l in range(layers)]]
        else:
            parts = [[small_part[n].reshape((1,) + two)]]
        res = adamw(w[n].reshape(two), parts, m[n].reshape(two), v[n].reshape(two), "adamw_" + n)
        outs.append([r.reshape(shp) for r in res])
    return (loss, dx0[None], *[o[0] for o in outs], *[o[1] for o in outs], *[o[2] for o in outs], *[o[3] for o in outs])
```

```python
import functools
import math

import jax
import jax.numpy as jnp
import numpy as np
from jax import lax
from jax.experimental import pallas as pl
from jax.experimental.pallas import tpu as pltpu

F32 = jnp.float32
BF16 = jnp.bfloat16
I32 = jnp.int32

D = 1024
EPS = 1e-6
POOL_WIDTH = 2048
POOL_GROUP = 512
GDN_H, GDN_DK, GDN_DV, GDN_C = 8, 128, 256, 64
GDN_QK, GDN_V, GDN_CONV_CH, GDN_IN = 1024, 2048, 4096, 6160
GDN_IN_PAD = 6272
MLA_H, MLA_NOPE, MLA_ROPE, MLA_V = 16, 128, 64, 128
MLA_Q_LORA, MLA_KV_LORA, MLA_WIDTH, MLA_IN = 768, 512, 2048, 3392
MLA_IN_PAD = 4096
MLA_SCALE = (MLA_NOPE + MLA_ROPE) ** -0.5
ROPE_THETA = 10000.0
ADAM_LR, ADAM_B1, ADAM_B2, ADAM_EPS, ADAM_WD, ADAM_STEP = 0.001, 0.9, 0.999, 1e-08, 0.01, 10

VMEM_LIMIT_V7X = 56 * 1024 * 1024
LANES = 128
MESH = pl.DeviceIdType.MESH


def _pc(body, **kw):
    return pl.pallas_call(body, **kw)


def _cparams(sem):
    return pltpu.CompilerParams(dimension_semantics=sem, vmem_limit_bytes=VMEM_LIMIT_V7X)


def _tile(n, cap):
    t = (cap // LANES) * LANES
    while t >= LANES:
        if n % t == 0:
            return t
        t -= LANES
    return n


def _sds(shape, dt):
    return jax.ShapeDtypeStruct(shape, dt)


def mm(a, b, *, ta=False, tb=False, add=None, after=None, out_dtype=F32, name):
    if ta:
        K, M = a.shape
    else:
        M, K = a.shape
    if tb:
        N, K2 = b.shape
    else:
        K2, N = b.shape
    assert K == K2, (a.shape, b.shape, ta, tb)
    tm, tn, tk = _tile(M, 1024), _tile(N, 1024), _tile(K, 1024)
    nk = K // tk
    a_spec = pl.BlockSpec((tk, tm), lambda i, j, k: (k, i)) if ta else pl.BlockSpec((tm, tk), lambda i, j, k: (i, k))
    b_spec = pl.BlockSpec((tn, tk), lambda i, j, k: (j, k)) if tb else pl.BlockSpec((tk, tn), lambda i, j, k: (k, j))
    o_spec = pl.BlockSpec((tm, tn), lambda i, j, k: (i, j))
    dn = (((0 if ta else 1,), (1 if tb else 0,)), ((), ()))
    has_add = add is not None

    def body(*refs):
        a_ref, b_ref = refs[0], refs[1]
        part = lax.dot_general(a_ref[...].astype(BF16), b_ref[...].astype(BF16), dn, preferred_element_type=F32)
        if nk == 1:
            refs[-1][...] = (part + refs[2][...] if has_add else part).astype(out_dtype)
            return
        o_ref, acc = refs[-2], refs[-1]
        k = pl.program_id(2)

        @pl.when(k == 0)
        def _():
            acc[...] = part

        @pl.when(k > 0)
        def _():
            acc[...] += part

        @pl.when(k == nk - 1)
        def _():
            r = acc[...]
            if has_add:
                r = r + refs[2][...]
            o_ref[...] = r.astype(out_dtype)

    ins = [a, b] + ([add] if has_add else []) + ([after] if after is not None else [])
    specs = [a_spec, b_spec] + ([o_spec] if has_add else []) + ([pl.BlockSpec(memory_space=pl.ANY)] if after is not None else [])
    return _pc(body, grid=(M // tm, N // tn, nk), in_specs=specs, out_specs=o_spec, out_shape=_sds((M, N), out_dtype),
               scratch_shapes=[pltpu.VMEM((tm, tn), F32)] if nk > 1 else [], compiler_params=_cparams(("parallel", "parallel", "arbitrary")),
               name=name)(*ins)


def gmm(kind, a, b, *, G, name, out_dtype=F32):
    S_ = a.shape[0]
    Ka = a.shape[1] // G
    if kind == "tn":
        N = b.shape[1] // G
        tk = _tile(S_, 512)
        nk = S_ // tk

        def body(a_ref, b_ref, o_ref, acc):
            k = pl.program_id(1)

            @pl.when(k == 0)
            def _():
                acc[...] = jnp.zeros_like(acc)

            acc[...] += lax.dot_general(a_ref[...].astype(BF16), b_ref[...].astype(BF16), (((0,), (0,)), ((), ())),
                                        preferred_element_type=F32)

            @pl.when(k == nk - 1)
            def _():
                o_ref[...] = acc[...].astype(out_dtype)

        return _pc(body, grid=(G, nk),
                   in_specs=[pl.BlockSpec((tk, Ka), lambda g, k: (k, g)), pl.BlockSpec((tk, N), lambda g, k: (k, g))],
                   out_specs=pl.BlockSpec((None, Ka, N), lambda g, k: (g, 0, 0)), out_shape=_sds((G, Ka, N), out_dtype),
                   scratch_shapes=[pltpu.VMEM((Ka, N), F32)], compiler_params=_cparams(("parallel", "arbitrary")), name=name)(a, b)
    N = b.shape[2] if kind == "nn" else b.shape[1]
    tm = _tile(S_, 1024)
    dn = (((1,), (0 if kind == "nn" else 1,)), ((), ()))

    def body(a_ref, b_ref, o_ref):
        o_ref[...] = lax.dot_general(a_ref[...].astype(BF16), b_ref[...].astype(BF16), dn, preferred_element_type=F32)

    bshape = (None,) + tuple(b.shape[1:])
    return _pc(body, grid=(G, S_ // tm),
               in_specs=[pl.BlockSpec((tm, Ka), lambda g, i: (i, g)), pl.BlockSpec(bshape, lambda g, i: (g, 0, 0))],
               out_specs=pl.BlockSpec((tm, N), lambda g, i: (i, g)), out_shape=_sds((S_, G * N), F32),
               compiler_params=_cparams(("parallel", "parallel")), name=name)(a, b)


def _rw_spec(ts, w, c, s):
    return pl.BlockSpec((ts, w), lambda j, i: (i, c + j * s))


def _rw_pspec(p, w, c, s):
    return pl.BlockSpec((p.shape[0], w), lambda j, i: (0, c + j * s))


def rowwise(f, tiles, params, outs, *, ncol=1, ts, name):
    S_ = tiles[0][0].shape[0]
    nin = len(tiles) + len(params)

    def body(*refs):
        res = f(pl.program_id(0), *[r[...] for r in refs[:nin]])
        for r, o in zip(refs[nin:], res):
            r[...] = o.astype(r.dtype)

    return _pc(body, grid=(ncol, S_ // ts),
               in_specs=[_rw_spec(ts, w, c, s) for (_, w, c, s) in tiles] + [_rw_pspec(*p) for p in params],
               out_specs=[_rw_spec(ts, w, 0, s) for (w, s, _) in outs],
               out_shape=[_sds((S_, w * (ncol if s else 1)), dt) for (w, s, dt) in outs],
               compiler_params=_cparams(("parallel", "parallel")), name=name)(*[t[0] for t in tiles], *[p[0] for p in params])


def rowwise_bwd(f, tiles, params, cots, *, need, adds=None, place=None, narrow=(), ncol=1, ts, name):
    S_ = tiles[0][0].shape[0]
    adds = adds or {}
    place = place or {}
    nt, npar, nc = len(tiles), len(params), len(cots)
    add_keys = sorted(adds)
    need_idx = [k for k in range(nt) if need[k]]
    into_keys = [k for k in need_idx if k in place and not isinstance(place[k][0], int)]
    n_extra = len(add_keys) + len(into_keys)

    def body(*refs):
        j, i = pl.program_id(0), pl.program_id(1)
        vals = [r[...] for r in refs[:nt + npar]]
        cvals = tuple(r[...] for r in refs[nt + npar:nt + npar + nc])
        add_refs = refs[nt + npar + nc:nt + npar + nc + len(add_keys)]
        out_refs = refs[nt + npar + nc + n_extra:]
        _, vjp = jax.vjp(lambda *v: tuple(f(j, *v)), *vals)
        grads = vjp(cvals)
        for n, k in enumerate(need_idx):
            g = grads[k]
            if k in adds:
                g = g + add_refs[add_keys.index(k)][...]
            out_refs[n][...] = g.astype(out_refs[n].dtype)
        for n in range(npar):
            ref = out_refs[len(need_idx) + n]
            first = (i == 0) if params[n][3] else jnp.logical_and(i == 0, j == 0)

            @pl.when(first)
            def _():
                ref[...] = jnp.zeros_like(ref)

            ref[...] += grads[nt + n]

    in_specs = ([_rw_spec(ts, w, c, s) for (_, w, c, s) in tiles] + [_rw_pspec(*p) for p in params]
                + [_rw_spec(ts, w, c, s) for (_, w, c, s) in cots] + [_rw_spec(ts, *adds[k][1:]) for k in add_keys]
                + [pl.BlockSpec(memory_space=pl.ANY) for _ in into_keys])
    out_specs, out_shape, aliases = [], [], {}
    for n, k in enumerate(need_idx):
        w, s = tiles[k][1], tiles[k][3]
        if k in place:
            dst, c0 = place[k]
            total = dst if isinstance(dst, int) else dst.shape[1]
            out_specs.append(_rw_spec(ts, w, c0, s))
            out_shape.append(_sds((S_, total), (BF16 if k in narrow else F32) if isinstance(dst, int) else dst.dtype))
            if k in into_keys:
                aliases[nt + npar + nc + len(add_keys) + into_keys.index(k)] = n
        else:
            out_specs.append(_rw_spec(ts, w, 0, s))
            out_shape.append(_sds((S_, w * (ncol if s else 1)), BF16 if k in narrow else F32))
    out_specs += [_rw_pspec(p[0], p[1], p[2], p[3]) for p in params]
    out_shape += [_sds(p[0].shape, F32) for p in params]
    res = _pc(body, grid=(ncol, S_ // ts), in_specs=in_specs, out_specs=out_specs, out_shape=out_shape,
              input_output_aliases=aliases, compiler_params=_cparams(("arbitrary", "arbitrary")), name=name)(
        *[t[0] for t in tiles], *[p[0] for p in params], *[c[0] for c in cots], *[adds[k][0] for k in add_keys],
        *[place[k][0] for k in into_keys])
    return list(res[:len(need_idx)]), list(res[len(need_idx):])


def _rms(x, g):
    r = lax.rsqrt(jnp.mean(x * x, axis=-1, keepdims=True) + EPS)
    return x * r * g


def _silu(x):
    return x * jax.nn.sigmoid(x)


@jax.custom_vjp
def _softplus(x):
    return jnp.maximum(x, 0.0) + jnp.log1p(jnp.exp(-jnp.abs(x)))


_softplus.defvjp(lambda x: (_softplus(x), x), lambda x, d: (d * jax.nn.sigmoid(x),))


def f_rms(j, x, g):
    return (_rms(x, g),)


def f_pool_gate(j, pg, gate, scale):
    return (pg * scale * _silu(gate),)


def f_ogate(j, o, gate):
    return (o * _silu(gate),)


def f_gdn_out(j, o, gate, g):
    return (_rms(o, g) * _silu(gate),)


def f_gdn_gates(j, ba, alog, dtb):
    lane = lax.broadcasted_iota(I32, (1, LANES), 1)
    gs, bs = [], []
    for h in range(GDN_H):
        eb = (lane == h).astype(F32)
        ea = (lane == GDN_H + h).astype(F32)
        b = jnp.sum(ba * eb, -1, keepdims=True)
        a = jnp.sum(ba * ea, -1, keepdims=True)
        al = jnp.sum(alog * eb, -1, keepdims=True)
        dt = jnp.sum(dtb * eb, -1, keepdims=True)
        g = -jnp.exp(al) * _softplus(a + dt)
        gs.append(jnp.broadcast_to(g, ba.shape))
        bs.append(jnp.broadcast_to(jax.nn.sigmoid(b), ba.shape))
    return jnp.concatenate(gs, 1), jnp.concatenate(bs, 1)


def _shift_dn(x, k):
    rows = lax.broadcasted_iota(I32, x.shape, 0)
    return jnp.where(rows < k, 0.0, pltpu.roll(x, k, 0))


def _shift_up(x, k):
    n = x.shape[0]
    rows = lax.broadcasted_iota(I32, x.shape, 0)
    return jnp.where(rows >= n - k, 0.0, pltpu.roll(x, n - k, 0))


def _pool_window(j):
    g = lax.div(j, POOL_GROUP // LANES)
    return jnp.where(g == 0, 2.0, jnp.where(g == 1, 4.0, jnp.where(g == 2, 8.0, 16.0))), g


def _pick(g, a2, a4, a8, a16):
    return jnp.where(g == 0, a2, jnp.where(g == 1, a4, jnp.where(g == 2, a8, a16)))


def pool_time_fwd(proj, name):
    S_ = proj.shape[0]

    def body(u_ref, p_ref):
        u = u_ref[...]
        w, g = _pool_window(pl.program_id(0))
        s2 = u + _shift_dn(u, 1)
        s4 = s2 + _shift_dn(s2, 2)
        s8 = s4 + _shift_dn(s4, 4)
        s16 = s8 + _shift_dn(s8, 8)
        t1 = (lax.broadcasted_iota(I32, u.shape, 0) + 1).astype(F32)
        p_ref[...] = (_pick(g, s2, s4, s8, s16) / jnp.minimum(t1, w) - u).astype(p_ref.dtype)

    return _pc(body, grid=(POOL_WIDTH // LANES,), in_specs=[pl.BlockSpec((S_, LANES), lambda j: (0, j))],
               out_specs=pl.BlockSpec((S_, LANES), lambda j: (0, j)), out_shape=_sds((S_, POOL_WIDTH), BF16),
               compiler_params=_cparams(("parallel",)), name=name)(proj)


def pool_time_bwd(dp, into, name):
    S_ = dp.shape[0]

    def body(dp_ref, _, du_ref):
        d = dp_ref[...]
        w, g = _pool_window(pl.program_id(0))
        t1 = (lax.broadcasted_iota(I32, d.shape, 0) + 1).astype(F32)
        q = d / jnp.minimum(t1, w)
        r2 = q + _shift_up(q, 1)
        r4 = r2 + _shift_up(r2, 2)
        r8 = r4 + _shift_up(r4, 4)
        r16 = r8 + _shift_up(r8, 8)
        du_ref[...] = (_pick(g, r2, r4, r8, r16) - d).astype(du_ref.dtype)

    return _pc(body, grid=(POOL_WIDTH // LANES,),
               in_specs=[pl.BlockSpec((S_, LANES), lambda j: (0, j)), pl.BlockSpec(memory_space=pl.ANY)],
               out_specs=pl.BlockSpec((S_, LANES), lambda j: (0, j)), out_shape=_sds(into.shape, into.dtype),
               input_output_aliases={1: 0}, compiler_params=_cparams(("parallel",)), name=name)(dp, into)


def _conv_post(j, a):
    n = a * lax.rsqrt(jnp.sum(a * a, axis=-1, keepdims=True) + EPS)
    nq = GDN_QK // LANES
    return jnp.where(j < nq, n * (GDN_DK ** -0.5), jnp.where(j < 2 * nq, n, a))


def _conv_pre(u, w):
    return w[3:4] * u + w[2:3] * _shift_dn(u, 1) + w[1:2] * _shift_dn(u, 2) + w[0:1] * _shift_dn(u, 3)


def gdn_conv_fwd(proj, conv_w, name):
    S_ = proj.shape[0]

    def body(u_ref, w_ref, o_ref):
        o_ref[...] = _conv_post(pl.program_id(0), _silu(_conv_pre(u_ref[...], w_ref[...])))

    return _pc(body, grid=(GDN_CONV_CH // LANES,),
               in_specs=[pl.BlockSpec((S_, LANES), lambda j: (0, j)), pl.BlockSpec((8, LANES), lambda j: (0, j))],
               out_specs=pl.BlockSpec((S_, LANES), lambda j: (0, j)), out_shape=_sds((S_, GDN_CONV_CH), F32),
               compiler_params=_cparams(("parallel",)), name=name)(proj, conv_w)


def gdn_conv_bwd(proj, conv_w, dq, dk, dv, into, name):
    S_ = proj.shape[0]
    nq = GDN_QK // LANES

    def body(u_ref, w_ref, dq_ref, dk_ref, dv_ref, _, du_ref, dw_ref):
        j = pl.program_id(0)
        u, w = u_ref[...], w_ref[...]
        c = _conv_pre(u, w)
        sig = jax.nn.sigmoid(c)
        dout = jnp.where(j < nq, dq_ref[...], jnp.where(j < 2 * nq, dk_ref[...], dv_ref[...]))
        _, vjp = jax.vjp(lambda a: _conv_post(j, a), c * sig)
        dc = vjp(dout)[0] * (sig * (1.0 + c * (1.0 - sig)))
        du = w[3:4] * dc + w[2:3] * _shift_up(dc, 1) + w[1:2] * _shift_up(dc, 2) + w[0:1] * _shift_up(dc, 3)
        du_ref[...] = du.astype(du_ref.dtype)
        rows = lax.broadcasted_iota(I32, (8, LANES), 0)
        dw = jnp.zeros((8, LANES), F32)
        for k in range(4):
            us = u if k == 3 else _shift_dn(u, 3 - k)
            dw = dw + jnp.where(rows == k, jnp.sum(dc * us, axis=0, keepdims=True), 0.0)
        dw_ref[...] = dw

    blk = lambda f: pl.BlockSpec((S_, LANES), f)
    return _pc(body, grid=(GDN_CONV_CH // LANES,),
               in_specs=[blk(lambda j: (0, j)), pl.BlockSpec((8, LANES), lambda j: (0, j)),
                         blk(lambda j: (0, jnp.minimum(j, nq - 1))), blk(lambda j: (0, jnp.clip(j - nq, 0, nq - 1))),
                         blk(lambda j: (0, jnp.clip(j - 2 * nq, 0, 2 * nq - 1))), pl.BlockSpec(memory_space=pl.ANY)],
               out_specs=[blk(lambda j: (0, j)), pl.BlockSpec((8, LANES), lambda j: (0, j))],
               out_shape=[_sds(into.shape, into.dtype), _sds((8, GDN_CONV_CH), F32)], input_output_aliases={5: 0},
               compiler_params=_cparams(("parallel",)), name=name)(proj, conv_w, dq, dk, dv, into)


_NN, _NT, _TN = ((1,), (0,)), ((1,), (1,)), ((0,), (0,))


def _split(x, n):
    parts = []
    for _ in range(n):
        h = x.astype(BF16)
        parts.append(h)
        x = x - h.astype(F32)
    return parts


def _dot(a, b, dn, mode):
    d = lambda p, q: lax.dot_general(p, q, (dn, ((), ())), preferred_element_type=F32)
    if mode == "lo":
        return d(a.astype(BF16), b.astype(BF16))
    if mode == "x3":
        (ah, al), (bh, bl) = _split(a, 2), _split(b, 2)
        return d(ah, bh) + (d(ah, bl) + d(al, bh))
    b0, b1, b2 = _split(b, 3)
    ab = a.astype(BF16)
    return d(ab, b0) + (d(ab, b1) + d(ab, b2))


def _make_dots(mode):
    @jax.custom_vjp
    def nn(a, b):
        return _dot(a, b, _NN, mode)

    @jax.custom_vjp
    def nt(a, b):
        return _dot(a, b, _NT, mode)

    @jax.custom_vjp
    def tn(a, b):
        return _dot(a, b, _TN, mode)

    nn.defvjp(lambda a, b: (nn(a, b), (a, b)), lambda r, d: (nt(d, r[1]), tn(r[0], d)))
    nt.defvjp(lambda a, b: (nt(a, b), (a, b)), lambda r, d: (nn(d, r[1]), tn(d, r[0])))
    tn.defvjp(lambda a, b: (tn(a, b), (a, b)), lambda r, d: (nt(r[1], d), nn(r[0], d)))
    return nn, nt, tn


_nn_hi, _nt_hi, _tn_hi = _make_dots("x3")
_nn_lo, _nt_lo, _tn_lo = _make_dots("lo")


@jax.custom_vjp
def _nn_const(a, b):
    return _dot(a, b, _NN, "xl")


_nn_const.defvjp(lambda a, b: (_nn_const(a, b), a), lambda a, d: (jnp.zeros_like(a), _dot(a, d, _TN, "xl")))


def _each(f, *lists):
    return [f(*xs) for xs in zip(*lists)]


@jax.custom_vjp
def _unit_inverses(xs):
    C = xs[0].shape[0]
    eye = (lax.broadcasted_iota(I32, (C, C), 0) == lax.broadcasted_iota(I32, (C, C), 1)).astype(F32)
    ainv, p = [eye + a for a in xs], xs
    for _ in range(int(math.log2(C)) - 1):
        p = _each(lambda a: _dot(a, a, _NN, "x3"), p)
        ainv = _each(lambda a, b: a + _dot(a, b, _NN, "x3"), ainv, p)
    return ainv


def _unit_inverses_bwd(ainv, d):
    left = _each(lambda a, g: _dot(a, g, _TN, "x3"), ainv, d)
    return (_each(lambda t, a: _dot(t, a, _NT, "x3"), left, ainv),)


_unit_inverses.defvjp(lambda xs: (lambda a: (a, a))(_unit_inverses(xs)), _unit_inverses_bwd)


def _gdn_chunk(q, k, v, gb, bb, state):
    C = GDN_C
    e0 = (lax.broadcasted_iota(I32, (1, LANES), 1) == 0).astype(F32)
    ri = lax.broadcasted_iota(I32, (C, C), 0)
    ci = lax.broadcasted_iota(I32, (C, C), 1)
    causal, strict = ri >= ci, ri > ci
    tri, eye, ones = causal.astype(F32), (ri == ci).astype(F32), jnp.ones((C, C), F32)
    last = lax.broadcasted_iota(I32, (C, LANES), 0) == C - 1
    g1 = _each(lambda a: jnp.sum(a * e0, -1, keepdims=True), gb)
    b1 = _each(lambda a: jnp.sum(a * e0, -1, keepdims=True), bb)
    gc_c = _each(lambda g: _nn_const(tri, jnp.broadcast_to(g, (C, C))), g1)
    gc_d = _each(lambda g: _nn_const(tri, jnp.broadcast_to(g, (C, LANES))), g1)
    gr_c = _each(lambda g: _nn_const(ones, eye * g), gc_c)
    decay = _each(lambda a, r: jnp.where(causal, jnp.exp(jnp.where(causal, a - r, 0.0)), 0.0), gc_c, gr_c)
    kb = _each(lambda a, b: a * b, k, b1)
    vb = _each(lambda a, b: a * b, v, b1)
    x = _each(lambda a, b, d: -jnp.where(strict, _nt_lo(a, b) * d, 0.0), kb, k, decay)
    ainv = _unit_inverses(x)
    u = _each(_nn_hi, ainv, vb)
    w = _each(lambda a, b, g: _nn_hi(a, b * jnp.exp(g)), ainv, kb, gc_d)
    attn = _each(lambda a, b, d: jnp.where(causal, _nt_lo(a, b) * d, 0.0), q, k, decay)
    v_new = _each(lambda a, b, s: a - _nn_lo(b, s), u, w, state)
    o = _each(lambda a, g, s, t, vn: _nn_lo(a * jnp.exp(g), s) + _nn_lo(t, vn), q, gc_d, state, attn, v_new)
    gl = _each(lambda g: jnp.sum(jnp.where(last, g, 0.0), axis=0, keepdims=True), gc_d)
    new_state = _each(lambda s, g, a, gd, vn: s * jnp.exp(jnp.sum(g * e0, -1, keepdims=True)) + _tn_lo(a * jnp.exp(g - gd), vn),
                      state, gl, k, gc_d, v_new)
    return o, new_state


def _head_slices(ref, width):
    return [ref[:, h * width:(h + 1) * width] for h in range(GDN_H)]


def gdn_chunk_fwd(qkv, g_b, beta_b, name):
    S_ = qkv.shape[0]
    N = S_ // GDN_C

    def body(q_ref, k_ref, v_ref, g_ref, b_ref, o_ref, s_ref, state):
        @pl.when(pl.program_id(0) == 0)
        def _():
            state[...] = jnp.zeros_like(state)

        st = [state[h] for h in range(GDN_H)]
        s_ref[0] = state[...]
        o, st2 = _gdn_chunk(_head_slices(q_ref, GDN_DK), _head_slices(k_ref, GDN_DK), _head_slices(v_ref, GDN_DV),
                            _head_slices(g_ref, GDN_DK), _head_slices(b_ref, GDN_DK), st)
        for h in range(GDN_H):
            o_ref[:, h * GDN_DV:(h + 1) * GDN_DV] = o[h]
            state[h] = st2[h]

    return _pc(body, grid=(N,),
               in_specs=[pl.BlockSpec((GDN_C, GDN_QK), lambda n: (n, 0)), pl.BlockSpec((GDN_C, GDN_QK), lambda n: (n, 1)),
                         pl.BlockSpec((GDN_C, GDN_V), lambda n: (n, 1)), pl.BlockSpec((GDN_C, GDN_QK), lambda n: (n, 0)),
                         pl.BlockSpec((GDN_C, GDN_QK), lambda n: (n, 0))],
               out_specs=[pl.BlockSpec((GDN_C, GDN_V), lambda n: (n, 0)),
                          pl.BlockSpec((1, GDN_H, GDN_DK, GDN_DV), lambda n: (n, 0, 0, 0))],
               out_shape=[_sds((S_, GDN_V), F32), _sds((N, GDN_H, GDN_DK, GDN_DV), F32)],
               scratch_shapes=[pltpu.VMEM((GDN_H, GDN_DK, GDN_DV), F32)],
               compiler_params=_cparams(("arbitrary",)), name=name)(qkv, qkv, qkv, g_b, beta_b)


def gdn_chunk_bwd(qkv, g_b, beta_b, states, do, name):
    S_ = qkv.shape[0]
    N = S_ // GDN_C

    def body(q_ref, k_ref, v_ref, g_ref, b_ref, s_ref, do_ref, dq_ref, dk_ref, dv_ref, dg_ref, db_ref, dstate):
        @pl.when(pl.program_id(0) == 0)
        def _():
            dstate[...] = jnp.zeros_like(dstate)

        _, vjp = jax.vjp(_gdn_chunk, _head_slices(q_ref, GDN_DK), _head_slices(k_ref, GDN_DK), _head_slices(v_ref, GDN_DV),
                         _head_slices(g_ref, GDN_DK), _head_slices(b_ref, GDN_DK), [s_ref[0, h] for h in range(GDN_H)])
        dq, dk, dv, dg, db, ds = vjp((_head_slices(do_ref, GDN_DV), [dstate[h] for h in range(GDN_H)]))
        for h in range(GDN_H):
            kk, vv = slice(h * GDN_DK, (h + 1) * GDN_DK), slice(h * GDN_DV, (h + 1) * GDN_DV)
            dq_ref[:, kk] = dq[h]
            dk_ref[:, kk] = dk[h]
            dv_ref[:, vv] = dv[h]
            dg_ref[:, kk] = dg[h]
            db_ref[:, kk] = db[h]
            dstate[h] = ds[h]

    r = lambda n: N - 1 - n
    qk = lambda c: pl.BlockSpec((GDN_C, GDN_QK), lambda n: (r(n), c))
    vs = lambda c: pl.BlockSpec((GDN_C, GDN_V), lambda n: (r(n), c))
    return _pc(body, grid=(N,),
               in_specs=[qk(0), qk(1), vs(1), qk(0), qk(0),
                         pl.BlockSpec((1, GDN_H, GDN_DK, GDN_DV), lambda n: (r(n), 0, 0, 0)), vs(0)],
               out_specs=[qk(0), qk(0), vs(0), qk(0), qk(0)],
               out_shape=[_sds((S_, GDN_QK), F32), _sds((S_, GDN_QK), F32), _sds((S_, GDN_V), F32),
                          _sds((S_, GDN_QK), F32), _sds((S_, GDN_QK), F32)],
               scratch_shapes=[pltpu.VMEM((GDN_H, GDN_DK, GDN_DV), F32)],
               compiler_params=_cparams(("arbitrary",)), name=name)(qkv, qkv, qkv, g_b, beta_b, states, do)


def _rope_tables(pos_ref, inv_ref, cm_ref, sg_ref):
    ang = pos_ref[...] * inv_ref[...]
    return jnp.cos(ang) * cm_ref[...], jnp.sin(ang) * sg_ref[...]


def mla_prep_fwd(qpad, kv, proj, pos, rope_consts, name):
    S_ = qpad.shape[0]
    ts = 256
    W = 2 * LANES

    def body(q_ref, kv_ref, kr_ref, pos_ref, inv_ref, cm_ref, sg_ref, qh_ref, kh_ref, vh_ref):
        cs, sn = _rope_tables(pos_ref, inv_ref, cm_ref, sg_ref)
        rope = lambda r: r * cs + pltpu.roll(r, LANES // 2, 1) * sn
        krr = rope(kr_ref[...]).astype(BF16)
        for h in range(MLA_H):
            qh_ref[h, :, 0:LANES] = (q_ref[:, h * W:h * W + LANES] * MLA_SCALE).astype(BF16)
            qh_ref[h, :, LANES:W] = (rope(q_ref[:, h * W + LANES:(h + 1) * W]) * MLA_SCALE).astype(BF16)
            kh_ref[h, :, 0:LANES] = kv_ref[:, h * W:h * W + LANES].astype(BF16)
            kh_ref[h, :, LANES:W] = krr
            vh_ref[h] = kv_ref[:, h * W + LANES:(h + 1) * W].astype(BF16)

    one = pl.BlockSpec((1, LANES), lambda i: (0, 0))
    return _pc(body, grid=(S_ // ts,),
               in_specs=[pl.BlockSpec((ts, MLA_H * W), lambda i: (i, 0)), pl.BlockSpec((ts, MLA_H * W), lambda i: (i, 0)),
                         pl.BlockSpec((ts, LANES), lambda i: (i, 1536 // LANES)), pl.BlockSpec((ts, 1), lambda i: (i, 0)),
                         one, one, one],
               out_specs=[pl.BlockSpec((MLA_H, ts, W), lambda i: (0, i, 0)), pl.BlockSpec((MLA_H, ts, W), lambda i: (0, i, 0)),
                          pl.BlockSpec((MLA_H, ts, LANES), lambda i: (0, i, 0))],
               out_shape=[_sds((MLA_H, S_, W), BF16), _sds((MLA_H, S_, W), BF16), _sds((MLA_H, S_, LANES), BF16)],
               compiler_params=_cparams(("parallel",)), name=name)(qpad, kv, proj, pos, *rope_consts)


def mla_prep_bwd(dqh, dkh, dvh, pos, rope_consts, into, name):
    S_ = dqh.shape[1]
    ts = 256
    W = 2 * LANES

    def body(dq_ref, dk_ref, dv_ref, pos_ref, inv_ref, cm_ref, sg_ref, _, dqp_ref, dkv_ref, dkr_ref):
        cs, sn = _rope_tables(pos_ref, inv_ref, cm_ref, sg_ref)
        rope_t = lambda g: g * cs + pltpu.roll(g * sn, LANES // 2, 1)
        acc = jnp.zeros((ts, LANES), F32)
        for h in range(MLA_H):
            dqp_ref[:, h * W:h * W + LANES] = (dq_ref[h, :, 0:LANES].astype(F32) * MLA_SCALE).astype(BF16)
            dqp_ref[:, h * W + LANES:(h + 1) * W] = (rope_t(dq_ref[h, :, LANES:W].astype(F32)) * MLA_SCALE).astype(BF16)
            dkv_ref[:, h * W:h * W + LANES] = dk_ref[h, :, 0:LANES]
            dkv_ref[:, h * W + LANES:(h + 1) * W] = dv_ref[h]
            acc = acc + dk_ref[h, :, LANES:W].astype(F32)
        dkr_ref[...] = rope_t(acc).astype(dkr_ref.dtype)

    one = pl.BlockSpec((1, LANES), lambda i: (0, 0))
    return _pc(body, grid=(S_ // ts,),
               in_specs=[pl.BlockSpec((MLA_H, ts, W), lambda i: (0, i, 0)), pl.BlockSpec((MLA_H, ts, W), lambda i: (0, i, 0)),
                         pl.BlockSpec((MLA_H, ts, LANES), lambda i: (0, i, 0)), pl.BlockSpec((ts, 1), lambda i: (i, 0)),
                         one, one, one, pl.BlockSpec(memory_space=pl.ANY)],
               out_specs=[pl.BlockSpec((ts, MLA_H * W), lambda i: (i, 0)), pl.BlockSpec((ts, MLA_H * W), lambda i: (i, 0)),
                          pl.BlockSpec((ts, LANES), lambda i: (i, 1536 // LANES))],
               out_shape=[_sds((S_, MLA_H * W), BF16), _sds((S_, MLA_H * W), BF16), _sds(into.shape, into.dtype)],
               input_output_aliases={7: 2}, compiler_params=_cparams(("parallel",)), name=name)(dqh, dkh, dvh, pos, *rope_consts, into)


NEG = -1e30


FLASH_TILE = 1024


def _scores(q, k, diagonal):
    s = lax.dot_general(q, k, (_NT, ((), ())), preferred_element_type=F32)
    if not diagonal:
        return s
    t = s.shape[0]
    return jnp.where(lax.broadcasted_iota(I32, (t, t), 1) <= lax.broadcasted_iota(I32, (t, t), 0), s, NEG)


def flash_fwd(qh, kh, vh, name):
    H, S_, W = qh.shape
    t = _tile(S_, FLASH_TILE)
    n = S_ // t

    def body(q_ref, k_ref, v_ref, o_ref, lse_ref, m_s, l_s, acc):
        qi, kj = pl.program_id(1), pl.program_id(2)

        @pl.when(kj == 0)
        def _():
            m_s[...] = jnp.full_like(m_s, NEG)
            l_s[...] = jnp.zeros_like(l_s)
            acc[...] = jnp.zeros_like(acc)

        def step(diagonal):
            s = _scores(q_ref[...], k_ref[...], diagonal)
            m_old = m_s[...]
            m_new = jnp.maximum(m_old, jnp.max(s, axis=-1, keepdims=True))
            alpha = jnp.exp(m_old - m_new)
            p = jnp.exp(s - m_new[:, :1])
            l_s[...] = alpha * l_s[...] + jnp.sum(p, axis=-1, keepdims=True)
            acc[...] = alpha * acc[...] + lax.dot_general(p.astype(BF16), v_ref[...], (_NN, ((), ())), preferred_element_type=F32)
            m_s[...] = m_new

        pl.when(kj < qi)(lambda: step(False))
        pl.when(kj == qi)(lambda: step(True))

        @pl.when(kj == n - 1)
        def _():
            o_ref[...] = acc[...] / l_s[...]
            lse_ref[...] = m_s[...] + jnp.log(l_s[...])

    return _pc(body, grid=(H, n, n),
               in_specs=[pl.BlockSpec((None, t, W), lambda h, i, j: (h, i, 0)),
                         pl.BlockSpec((None, t, W), lambda h, i, j: (h, jnp.minimum(i, j), 0)),
                         pl.BlockSpec((None, t, LANES), lambda h, i, j: (h, jnp.minimum(i, j), 0))],
               out_specs=[pl.BlockSpec((t, LANES), lambda h, i, j: (i, h)), pl.BlockSpec((None, t, LANES), lambda h, i, j: (h, i, 0))],
               out_shape=[_sds((S_, H * LANES), F32), _sds((H, S_, LANES), F32)],
               scratch_shapes=[pltpu.VMEM((t, LANES), F32)] * 3,
               compiler_params=_cparams(("parallel", "parallel", "arbitrary")), name=name)(qh, kh, vh)


def flash_bwd(qh, kh, vh, o, lse, do, name):
    H, S_, W = qh.shape
    t = _tile(S_, FLASH_TILE)
    n = S_ // t

    def body(q_ref, k_ref, v_ref, o_ref, lse_ref, do_ref, dq_ref, dk_ref, dv_ref, dq_acc, dk_acc, dv_acc):
        kj, qi = pl.program_id(1), pl.program_id(2)

        @pl.when(jnp.logical_and(kj == 0, qi == 0))
        def _():
            dq_acc[...] = jnp.zeros_like(dq_acc)

        @pl.when(qi == 0)
        def _():
            dk_acc[...] = jnp.zeros_like(dk_acc)
            dv_acc[...] = jnp.zeros_like(dv_acc)

        def step(diagonal):
            q, k, v = q_ref[...], k_ref[...], v_ref[...]
            do_ = do_ref[...]
            p = jnp.exp(_scores(q, k, diagonal) - lse_ref[:, :1])
            dob = do_.astype(BF16)
            dv_acc[...] += lax.dot_general(p.astype(BF16), dob, (_TN, ((), ())), preferred_element_type=F32)
            dp = lax.dot_general(dob, v, (_NT, ((), ())), preferred_element_type=F32)
            delta = jnp.sum(do_ * o_ref[...], axis=-1, keepdims=True)
            ds = (p * (dp - delta)).astype(BF16)
            dk_acc[...] += lax.dot_general(ds, q, (_TN, ((), ())), preferred_element_type=F32)
            rows = pl.ds(pl.multiple_of(qi * t, t), t)
            dq_acc[rows, :] += lax.dot_general(ds, k, (_NN, ((), ())), preferred_element_type=F32)

        pl.when(qi > kj)(lambda: step(False))
        pl.when(qi == kj)(lambda: step(True))

        @pl.when(qi == n - 1)
        def _():
            dk_ref[...] = dk_acc[...].astype(BF16)
            dv_ref[...] = dv_acc[...].astype(BF16)

        @pl.when(jnp.logical_and(kj == n - 1, qi == n - 1))
        def _():
            dq_ref[...] = dq_acc[...].astype(BF16)

    qrow = lambda h, j, i: jnp.maximum(i, j)
    return _pc(body, grid=(H, n, n),
               in_specs=[pl.BlockSpec((None, t, W), lambda h, j, i: (h, qrow(h, j, i), 0)),
                         pl.BlockSpec((None, t, W), lambda h, j, i: (h, j, 0)),
                         pl.BlockSpec((None, t, LANES), lambda h, j, i: (h, j, 0)),
                         pl.BlockSpec((t, LANES), lambda h, j, i: (qrow(h, j, i), h)),
                         pl.BlockSpec((None, t, LANES), lambda h, j, i: (h, qrow(h, j, i), 0)),
                         pl.BlockSpec((t, LANES), lambda h, j, i: (qrow(h, j, i), h))],
               out_specs=[pl.BlockSpec((None, S_, W), lambda h, j, i: (h, 0, 0)),
                          pl.BlockSpec((None, t, W), lambda h, j, i: (h, j, 0)),
                          pl.BlockSpec((None, t, LANES), lambda h, j, i: (h, j, 0))],
               out_shape=[_sds((H, S_, W), BF16), _sds((H, S_, W), BF16), _sds((H, S_, LANES), BF16)],
               scratch_shapes=[pltpu.VMEM((S_, W), F32), pltpu.VMEM((t, W), F32), pltpu.VMEM((t, LANES), F32)],
               compiler_params=_cparams(("parallel", "arbitrary", "arbitrary")), name=name)(qh, kh, vh, o, lse, do)


def loss_head(x, target, g, name):
    S_ = x.shape[0]
    ts = 256

    def body(x_ref, t_ref, g_ref, l_ref, dx_ref, dg_ref):
        @pl.when(pl.program_id(0) == 0)
        def _():
            l_ref[...] = jnp.zeros_like(l_ref)
            dg_ref[...] = jnp.zeros_like(dg_ref)

        y, vjp = jax.vjp(_rms, x_ref[...], g_ref[...])
        err = y - t_ref[...]
        l_ref[...] += 0.5 * jnp.sum(jnp.sum(err * err, axis=-1, keepdims=True), axis=0, keepdims=True) / D
        dx, dg = vjp(err / D)
        dx_ref[...] = dx
        dg_ref[...] += dg

    row = pl.BlockSpec((ts, D), lambda i: (i, 0))
    return _pc(body, grid=(S_ // ts,), in_specs=[row, row, pl.BlockSpec((1, D), lambda i: (0, 0))],
               out_specs=[pl.BlockSpec((1, LANES), lambda i: (0, 0)), row, pl.BlockSpec((1, D), lambda i: (0, 0))],
               out_shape=[_sds((1, LANES), F32), _sds((S_, D), F32), _sds((1, D), F32)],
               compiler_params=_cparams(("arbitrary",)), name=name)(x, target, g)


def adamw(w, parts, m, v, name):
    R, C = w.shape
    rows = [p.shape[1] for p in parts[0]]
    tr = R
    for cand in (512, 256, 128, 64, 32, 16, 8):
        if all(r % cand == 0 for r in rows) and cand * C * 4 <= 1024 * 1024:
            tr = cand
            break
    c1 = 1.0 - ADAM_B1 ** ADAM_STEP
    c2 = 1.0 - ADAM_B2 ** ADAM_STEP
    starts = [sum(rows[:k]) // tr for k in range(len(rows))]
    flat = [p for part in parts for p in part]

    def body(*refs):
        w_ref, m_ref, v_ref = refs[0], refs[1 + len(flat)], refs[2 + len(flat)]
        g_ref, d_ref, nm_ref, nv_ref = refs[3 + len(flat):]
        i = pl.program_id(0)
        gg, at = None, 1
        for part in parts:
            val = None
            for k in range(len(part)):
                p_ref = refs[at]
                at += 1
                s = p_ref[0].astype(F32)
                for n in range(1, p_ref.shape[0]):
                    s = s + p_ref[n].astype(F32)
                val = s if val is None else jnp.where(i >= starts[k], s, val)
            gg = val if gg is None else gg + val
        m2 = ADAM_B1 * m_ref[...] + (1.0 - ADAM_B1) * gg
        v2 = ADAM_B2 * v_ref[...] + (1.0 - ADAM_B2) * (gg * gg)
        g_ref[...] = gg
        d_ref[...] = -ADAM_LR * ((m2 / c1) / (jnp.sqrt(v2 / c2) + ADAM_EPS) + ADAM_WD * w_ref[...])
        nm_ref[...] = m2
        nv_ref[...] = v2

    blk = pl.BlockSpec((tr, C), lambda i: (i, 0))
    piece = lambda p, k: pl.BlockSpec((p.shape[0], tr, C), lambda i: (0, jnp.clip(i - starts[k], 0, rows[k] // tr - 1), 0))
    pblk = [piece(p, k) for part in parts for k, p in enumerate(part)]
    return _pc(body, grid=(R // tr,), in_specs=[blk] + pblk + [blk, blk], out_specs=[blk] * 4, out_shape=[_sds((R, C), F32)] * 4,
               compiler_params=_cparams(("parallel",)), name=name)(w, *flat, m, v)


def sum_slots(own, recv, skip, name):
    n, R, C = recv.shape
    tr = _tile(R, 512) if R % LANES == 0 else R
    has_own = own is not None

    def body(*refs):
        skip_ref = refs[0]
        r_ref, o_ref = refs[-2], refs[-1]
        acc = refs[1][...] if has_own else jnp.zeros(o_ref.shape, F32)
        for s in range(n):
            acc = acc + jnp.where(skip_ref[0] == s, 0.0, r_ref[s].astype(F32))
        o_ref[...] = acc

    row = pl.BlockSpec((tr, C), lambda i, sk: (i, 0))
    gs = pltpu.PrefetchScalarGridSpec(
        num_scalar_prefetch=1, grid=(R // tr,),
        in_specs=([row] if has_own else []) + [pl.BlockSpec((n, tr, C), lambda i, sk: (0, i, 0))], out_specs=row)
    ins = ([own] if has_own else []) + [recv]
    return _pc(body, grid_spec=gs, out_shape=_sds((R, C), F32), compiler_params=_cparams(("parallel",)), name=name)(skip, *ins)


def _chip_peers():
    x, y, c = lax.axis_index("x"), lax.axis_index("y"), lax.axis_index("c")
    return (x, y, c), [(1 - x, y, c), (x, 1 - y, c), (1 - x, 1 - y, c)]


def _chip_index(p):
    return 2 * p[0] + p[1]


def _win(ref, axis, chip, size):
    if axis is None:
        return ref.at[chip]
    idx = [slice(None)] * len(ref.shape)
    idx[axis] = pl.ds(pl.multiple_of(chip * size, size), size)
    return ref.at[tuple(idx)]


def _remote(src, dst, send_sem, recv_sem, peer):
    return pltpu.make_async_remote_copy(src_ref=src, dst_ref=dst, send_sem=send_sem, recv_sem=recv_sem, device_id=peer,
                                        device_id_type=MESH)


HBM_SPEC = pl.BlockSpec(memory_space=pltpu.HBM)
SEM_SPEC = pl.BlockSpec(memory_space=pltpu.SEMAPHORE)
ANY_SPEC = pl.BlockSpec(memory_space=pl.ANY)
DATAFLOW = pltpu.SideEffectType.DATAFLOW_SIDE_EFFECTING


def gather_piece(i, l, o, axis, size):
    return (i, lambda r, chip: r.at[l], o, lambda r, chip: _win(r, axis, chip, size))


def scatter_piece(i, o, axis, size):
    return (i, lambda r, chip: _win(r, axis, chip, size), o, lambda r, chip: r.at[chip])


def _copies(pieces, in_refs, out_refs, send, recv):
    me, peers = _chip_peers()
    mine = _chip_index(me)
    remote = []
    for n, (i, src, o, dst) in enumerate(pieces):
        d = dst(out_refs[o], mine)
        remote += [_remote(src(in_refs[i], _chip_index(p)), d, send.at[3 * n + k], recv.at[3 * n + k], p)
                   for k, p in enumerate(peers)]
    return remote


def own_window(a, axis, size, chip):
    if axis is None:
        return lax.dynamic_index_in_dim(a, chip, 0, keepdims=False)
    return lax.dynamic_slice_in_dim(a, chip * size, size, axis=axis)


def place_own(land, own, axis, size, chip):
    if axis is None:
        return lax.dynamic_update_slice_in_dim(land, own[None], chip, axis=0)
    return lax.dynamic_update_slice_in_dim(land, own, chip * size, axis=axis)


def exchange_start(pieces, ins, out_shapes, after, name):
    n_in, n_out, ncp = len(ins), len(out_shapes), len(pieces)

    def body(*refs):
        in_refs, land_refs = refs[:n_in], refs[n_in:n_in + n_out]
        send, recv = refs[n_in + n_out + 1], refs[n_in + n_out + 2]
        token = refs[-1]
        for cp in _copies(pieces, in_refs, land_refs, send, recv):
            cp.start()
        token[...] = jnp.zeros_like(token)

    hbm = lambda a: pltpu.with_memory_space_constraint(a, pltpu.HBM)
    lands = [hbm(lax.empty(s.shape, s.dtype)) for s in out_shapes]
    sem = pltpu.SemaphoreType.DMA((3 * ncp,))
    thru = [pltpu.HBM(a.shape, a.dtype) for a in ins] + [pltpu.HBM(s.shape, s.dtype) for s in out_shapes]
    res = _pc(body, in_specs=[HBM_SPEC] * (n_in + n_out) + [ANY_SPEC],
              out_specs=[SEM_SPEC, SEM_SPEC] + [HBM_SPEC] * (n_in + n_out) + [pl.BlockSpec(memory_space=pltpu.VMEM)],
              out_shape=[sem, sem] + thru + [_sds((8, LANES), F32)],
              input_output_aliases={i: 2 + i for i in range(n_in + n_out)},
              compiler_params=pltpu.CompilerParams(has_side_effects=DATAFLOW), name=name)(*[hbm(a) for a in ins], *lands, after)
    return (res[0], res[1]), list(res[2:2 + n_in]), list(res[2 + n_in:2 + n_in + n_out]), res[-1]


def exchange_wait(pieces, sems, ins, lands, after, name):
    n_in, n_out = len(ins), len(lands)

    def body(*refs):
        in_refs, land_refs = refs[:n_in], refs[n_in:n_in + n_out]
        send, recv = refs[n_in + n_out], refs[n_in + n_out + 1]
        for cp in _copies(pieces, in_refs, land_refs, send, recv):
            cp.wait_send()
            cp.wait_recv()

    thru = [pltpu.HBM(a.shape, a.dtype) for a in ins] + [pltpu.HBM(a.shape, a.dtype) for a in lands]
    res = _pc(body, in_specs=[HBM_SPEC] * (n_in + n_out) + [SEM_SPEC, SEM_SPEC, ANY_SPEC], out_specs=[HBM_SPEC] * (n_in + n_out),
              out_shape=thru, input_output_aliases={i: i for i in range(n_in + n_out)},
              compiler_params=pltpu.CompilerParams(has_side_effects=DATAFLOW), name=name)(*ins, *lands, sems[0], sems[1], after)
    return list(res[:n_in]), list(res[n_in:])


def swap_cores(bufs, name):
    nb = len(bufs)

    def body(*refs):
        in_refs, out_refs = refs[:nb], refs[nb:2 * nb]
        send, recv = refs[2 * nb:]
        x, y, c = lax.axis_index("x"), lax.axis_index("y"), lax.axis_index("c")
        copies = [_remote(in_refs[b], out_refs[b], send.at[b], recv.at[b], (x, y, 1 - c)) for b in range(nb)]
        for cp in copies:
            cp.start()
        for cp in copies:
            cp.wait()

    anyspec = pl.BlockSpec(memory_space=pl.ANY)
    return _pc(body, in_specs=[anyspec] * nb, out_specs=[anyspec] * nb, out_shape=[_sds(b.shape, b.dtype) for b in bufs],
               scratch_shapes=[pltpu.SemaphoreType.DMA((nb,)), pltpu.SemaphoreType.DMA((nb,))], name=name)(*bufs)


def exchange_all(buf, name):
    def body(in_ref, out_ref, send, recv, local):
        x, y, c = lax.axis_index("x"), lax.axis_index("y"), lax.axis_index("c")
        mine = 4 * x + 2 * y + c
        loc = pltpu.make_async_copy(in_ref, out_ref.at[mine], local)
        loc.start()
        copies = [loc]
        for k in range(1, 8):
            peer = (x ^ (k >> 2), y ^ ((k >> 1) & 1), c ^ (k & 1))
            cp = pltpu.make_async_remote_copy(src_ref=in_ref, dst_ref=out_ref.at[mine], send_sem=send.at[k - 1],
                                              recv_sem=recv.at[k - 1], device_id=peer, device_id_type=MESH)
            cp.start()
            copies.append(cp)
        for cp in copies:
            cp.wait()

    anyspec = pl.BlockSpec(memory_space=pl.ANY)
    return _pc(body, in_specs=[anyspec], out_specs=anyspec, out_shape=_sds((8,) + buf.shape, buf.dtype),
               scratch_shapes=[pltpu.SemaphoreType.DMA((7,)), pltpu.SemaphoreType.DMA((7,)), pltpu.SemaphoreType.DMA],
               name=name)(buf)


def _norm_fwd(x, g, name):
    return rowwise(f_rms, [(x, D, 0, 0)], [(g, D, 0, 0)], [(D, 0, BF16)], ts=512, name=name)[0]


def _norm_bwd(x, g, dh, dres, name):
    (dx,), (dg,) = rowwise_bwd(f_rms, [(x, D, 0, 0)], [(g, D, 0, 0)], [(dh, D, 0, 0)], need=[True],
                               adds={0: (dres, D, 0, 0)}, ts=256, name=name)
    return dx, dg


def pool_fwd(x, W, tag):
    h = _norm_fwd(x, W["ng"], tag + "_norm")
    proj = mm(h, W["w_in"], name=tag + "_in")
    p = pool_time_fwd(proj, tag + "_win")
    pg = gmm("nn", p, W["w_grp"], G=4, name=tag + "_grp")
    y = rowwise(f_pool_gate, [(pg, POOL_GROUP, 0, 1), (proj, POOL_GROUP, 4, 1)], [(W["scale"], POOL_GROUP, 0, 1)],
                [(POOL_GROUP, 1, BF16)], ncol=4, ts=512, name=tag + "_gate")[0]
    xn = mm(y, W["w_out"], add=x, name=tag + "_out")
    return xn, (x, h, proj, p, pg, y)


def pool_bwd(dxn, W, saved, tag, after=None):
    x, h, proj, p, pg, y = saved
    dy = mm(dxn, W["w_out"], tb=True, after=after, name=tag + "_dy")
    g = {"w_out": mm(y, dxn, ta=True, out_dtype=BF16, name=tag + "_dwout")}
    (dpg, dproj), (g["scale"],) = rowwise_bwd(
        f_pool_gate, [(pg, POOL_GROUP, 0, 1), (proj, POOL_GROUP, 4, 1)], [(W["scale"], POOL_GROUP, 0, 1)],
        [(dy, POOL_GROUP, 0, 1)], need=[True, True], place={1: (2 * POOL_WIDTH, 4)}, narrow=(0, 1), ncol=4, ts=512, name=tag + "_dgate")
    dp = gmm("nt", dpg, W["w_grp"], G=4, name=tag + "_dp")
    g["w_grp"] = gmm("tn", p, dpg, G=4, out_dtype=BF16, name=tag + "_dwgrp")
    dproj = pool_time_bwd(dp, dproj, tag + "_dwin")
    dh = mm(dproj, W["w_in"], tb=True, name=tag + "_dh")
    g["w_in"] = mm(h, dproj, ta=True, out_dtype=BF16, name=tag + "_dw_in")
    dx, g["ng"] = _norm_bwd(x, W["ng"], dh, dxn, tag + "_dnorm")
    return dx, g


def gdn_fwd(x, W, tag):
    h = _norm_fwd(x, W["ng"], tag + "_norm")
    proj = mm(h, W["w_in"], name=tag + "_in")
    qkv = gdn_conv_fwd(proj, W["conv"], tag + "_conv")
    g_b, beta_b = rowwise(f_gdn_gates, [(proj, LANES, 6144 // LANES, 0)], [(W["a_log"], LANES, 0, 0), (W["dt_bias"], LANES, 0, 0)],
                          [(GDN_QK, 0, F32), (GDN_QK, 0, F32)], ts=512, name=tag + "_gates")
    o, states = gdn_chunk_fwd(qkv, g_b, beta_b, tag + "_chunk")
    og = rowwise(f_gdn_out, [(o, GDN_DV, 0, 1), (proj, GDN_DV, 4096 // GDN_DV, 1)], [(W["norm_g"], GDN_DV, 0, 0)],
                 [(GDN_DV, 1, BF16)], ncol=GDN_H, ts=512, name=tag + "_onorm")[0]
    xn = mm(og, W["w_out"], add=x, name=tag + "_out")
    return xn, (x, h, proj, qkv, g_b, beta_b, o, states, og)


def gdn_bwd(dxn, W, saved, tag, after=None):
    x, h, proj, qkv, g_b, beta_b, o, states, og = saved
    dog = mm(dxn, W["w_out"], tb=True, after=after, name=tag + "_dog")
    g = {"w_out": mm(og, dxn, ta=True, out_dtype=BF16, name=tag + "_dwout")}
    (do, dproj), (g["norm_g"],) = rowwise_bwd(
        f_gdn_out, [(o, GDN_DV, 0, 1), (proj, GDN_DV, 4096 // GDN_DV, 1)], [(W["norm_g"], GDN_DV, 0, 0)],
        [(dog, GDN_DV, 0, 1)], need=[True, True], place={1: (GDN_IN_PAD, 4096 // GDN_DV)}, narrow=(1,), ncol=GDN_H, ts=512, name=tag + "_donorm")
    dq, dk, dv, dg_b, dbeta_b = gdn_chunk_bwd(qkv, g_b, beta_b, states, do, tag + "_dchunk")
    (dproj,), (g["a_log"], g["dt_bias"]) = rowwise_bwd(
        f_gdn_gates, [(proj, LANES, 6144 // LANES, 0)], [(W["a_log"], LANES, 0, 0), (W["dt_bias"], LANES, 0, 0)],
        [(dg_b, GDN_QK, 0, 0), (dbeta_b, GDN_QK, 0, 0)], need=[True], place={0: (dproj, 6144 // LANES)}, ts=256, name=tag + "_dgates")
    dproj, g["conv"] = gdn_conv_bwd(proj, W["conv"], dq, dk, dv, dproj, tag + "_dconv")
    dh = mm(dproj, W["w_in"], tb=True, name=tag + "_dh")
    g["w_in"] = mm(h, dproj, ta=True, out_dtype=BF16, name=tag + "_dw_in")
    dx, g["ng"] = _norm_bwd(x, W["ng"], dh, dxn, tag + "_dnorm")
    return dx, g


def mla_fwd(x, pos, W, tag):
    h = _norm_fwd(x, W["ng"], tag + "_norm")
    proj = mm(h, W["w_in"], name=tag + "_in")
    hq = rowwise(f_rms, [(proj, MLA_Q_LORA, 0, 0)], [(W["q_g"], MLA_Q_LORA, 0, 0)], [(MLA_Q_LORA, 0, BF16)], ts=512, name=tag + "_qnorm")[0]
    hkv = rowwise(f_rms, [(proj, MLA_KV_LORA, 2, 0)], [(W["kv_g"], MLA_KV_LORA, 0, 0)], [(MLA_KV_LORA, 0, BF16)], ts=512, name=tag + "_kvnorm")[0]
    qpad = mm(hq, W["w_uq"], name=tag + "_uq")
    kv = mm(hkv, W["w_ukv"], name=tag + "_ukv")
    qh, kh, vh = mla_prep_fwd(qpad, kv, proj, pos, W["rope"], tag + "_prep")
    o, lse = flash_fwd(qh, kh, vh, tag + "_attn")
    og = rowwise(f_ogate, [(o, 512, 0, 1), (proj, 512, 4, 1)], [], [(512, 1, BF16)], ncol=4, ts=512, name=tag + "_ogate")[0]
    xn = mm(og, W["w_out"], add=x, name=tag + "_out")
    return xn, (x, h, proj, hq, hkv, qh, kh, vh, o, lse, og)


def mla_bwd(dxn, pos, W, saved, tag, after=None):
    x, h, proj, hq, hkv, qh, kh, vh, o, lse, og = saved
    dog = mm(dxn, W["w_out"], tb=True, after=after, name=tag + "_dog")
    g = {"w_out": mm(og, dxn, ta=True, out_dtype=BF16, name=tag + "_dwout")}
    dproj = jnp.zeros(proj.shape, BF16)
    (do, dproj), _ = rowwise_bwd(f_ogate, [(o, 512, 0, 1), (proj, 512, 4, 1)], [], [(dog, 512, 0, 1)], need=[True, True],
                                 place={1: (dproj, 4)}, ncol=4, ts=512, name=tag + "_dogate")
    dqh, dkh, dvh = flash_bwd(qh, kh, vh, o, lse, do, tag + "_dattn")
    dqpad, dkv, dproj = mla_prep_bwd(dqh, dkh, dvh, pos, W["rope"], dproj, tag + "_dprep")
    dhq = mm(dqpad, W["w_uq"], tb=True, name=tag + "_dhq")
    g["w_uq"] = mm(hq, dqpad, ta=True, out_dtype=BF16, name=tag + "_dwuq")
    dhkv = mm(dkv, W["w_ukv"], tb=True, name=tag + "_dhkv")
    g["w_ukv"] = mm(hkv, dkv, ta=True, out_dtype=BF16, name=tag + "_dwukv")
    (dproj,), (g["q_g"],) = rowwise_bwd(f_rms, [(proj, MLA_Q_LORA, 0, 0)], [(W["q_g"], MLA_Q_LORA, 0, 0)], [(dhq, MLA_Q_LORA, 0, 0)],
                                        need=[True], place={0: (dproj, 0)}, ts=256, name=tag + "_dqnorm")
    (dproj,), (g["kv_g"],) = rowwise_bwd(f_rms, [(proj, MLA_KV_LORA, 2, 0)], [(W["kv_g"], MLA_KV_LORA, 0, 0)], [(dhkv, MLA_KV_LORA, 0, 0)],
                                         need=[True], place={0: (dproj, 2)}, ts=256, name=tag + "_dkvnorm")
    dh = mm(dproj, W["w_in"], tb=True, name=tag + "_dh")
    g["w_in"] = mm(h, dproj, ta=True, out_dtype=BF16, name=tag + "_dw_in")
    dx, g["ng"] = _norm_bwd(x, W["ng"], dh, dxn, tag + "_dnorm")
    return dx, g


def _pad_cols(a, n):
    return jnp.pad(a, ((0, 0), (0, n - a.shape[1])))


def _mla_w_in_layout(w):
    z = lambda n: jnp.zeros((w.shape[0], n), w.dtype)
    kr = w[:, 1280:1344]
    return jnp.concatenate([w[:, :768], z(256), w[:, 768:1280], kr[:, :32], z(32), kr[:, 32:], z(32), z(384), w[:, 1344:]], axis=1)


def _mla_w_in_unlayout(g):
    return jnp.concatenate([g[:, :768], g[:, 1024:1536], g[:, 1536:1568], g[:, 1600:1632], g[:, 2048:]], axis=1)


def _mla_w_uq_layout(w):
    w3 = w.reshape(w.shape[0], MLA_H, MLA_NOPE + MLA_ROPE)
    z = jnp.zeros((w.shape[0], MLA_H, 32), w.dtype)
    return jnp.concatenate([w3[..., :128], w3[..., 128:160], z, w3[..., 160:192], z], axis=-1).reshape(w.shape[0], MLA_H * 256)


def _mla_w_uq_unlayout(g):
    g3 = g.reshape(g.shape[0], MLA_H, 256)
    return jnp.concatenate([g3[..., :128], g3[..., 128:160], g3[..., 192:224]], axis=-1).reshape(g.shape[0], MLA_H * 192)


def _rope_consts():
    half = MLA_ROPE // 2
    inv = ROPE_THETA ** (-jnp.arange(half, dtype=F32) / half)
    z = jnp.zeros((half,), F32)
    o = jnp.ones((half,), F32)
    row = lambda *p: jnp.concatenate(p).reshape(1, LANES)
    return row(inv, z, inv, z), row(o, z, o, z), row(-o, z, o, z)


BIG = ["pool_w_in", "pool_w_grp", "pool_w_out", "gdn_w_in", "gdn_w_out", "mla_w_in", "mla_w_uq", "mla_w_ukv", "mla_w_out"]
BIG_LAYOUT = {"pool_w_in": (1, 1024, (1024, 4096)), "pool_w_grp": (1, 128, (4, 512, 512)), "pool_w_out": (0, 512, (2048, 1024)),
              "gdn_w_in": (None, None, (4, 1024, 1540)), "gdn_w_out": (0, 512, (2048, 1024)),
              "mla_w_in": (None, None, (4, 1024, 848)), "mla_w_uq": (1, 768, (768, 3072)), "mla_w_ukv": (1, 1024, (512, 4096)),
              "mla_w_out": (0, 512, (2048, 1024))}
SMALL_SHARDED = ["pool_scale", "gdn_conv", "mla_q_norm_g", "mla_kv_norm_g"]
SMALL_AXIS = {"pool_scale": 1, "gdn_conv": 2, "mla_q_norm_g": 1, "mla_kv_norm_g": 1}
REPLICATED = ["norm_g", "gdn_a_log", "gdn_dt_bias", "gdn_norm_g", "final_g"]
PACK_C = 1024


def _pack(parts, dtype, row_mult):
    flat = jnp.concatenate([p.reshape(-1).astype(dtype) for p in parts])
    rows = -(-flat.shape[0] // PACK_C)
    rows = -(-rows // row_mult) * row_mult
    return jnp.pad(flat, (0, rows * PACK_C - flat.shape[0])).reshape(rows, PACK_C)


def _unpack(buf, shapes):
    lead = buf.shape[:-2]
    flat = buf.reshape(lead + (-1,))
    out, off = [], 0
    for s in shapes:
        n = int(np.prod(s))
        out.append(flat[..., off:off + n].reshape(lead + tuple(s)))
        off += n
    return out


def _unshard(g4, axis):
    a = jnp.moveaxis(g4, 0, axis)
    s = a.shape
    return a.reshape(s[:axis] + (s[axis] * s[axis + 1],) + s[axis + 2:])


def _to_shards(a, axis):
    s = a.shape
    return jnp.moveaxis(a.reshape(s[:axis] + (4, s[axis] // 4) + s[axis + 1:]), axis, 0)


def layer_weights(full, small, rep, layer):
    ng = rep["norm_g"][layer:layer + 1]
    side_by_side = lambda a4: jnp.moveaxis(a4, 0, 1).reshape(a4.shape[1], 4 * a4.shape[2])
    if layer in (0, 3):
        j = layer // 3
        return dict(ng=ng, w_in=full[("pool_w_in", j)], w_grp=full[("pool_w_grp", j)], scale=small["pool_scale"][j:j + 1],
                    w_out=full[("pool_w_out", j)])
    if layer == 1:
        return dict(ng=ng, w_in=_pad_cols(side_by_side(full[("gdn_w_in", 0)]), GDN_IN_PAD),
                    conv=jnp.pad(small["gdn_conv"][0], ((0, 4), (0, 0))), a_log=_pad_cols(rep["gdn_a_log"], LANES),
                    dt_bias=_pad_cols(rep["gdn_dt_bias"], LANES), norm_g=rep["gdn_norm_g"], w_out=full[("gdn_w_out", 0)])
    return dict(ng=ng, w_in=_mla_w_in_layout(side_by_side(full[("mla_w_in", 0)])), q_g=small["mla_q_norm_g"],
                kv_g=small["mla_kv_norm_g"], w_uq=_mla_w_uq_layout(full[("mla_w_uq", 0)]), w_ukv=full[("mla_w_ukv", 0)],
                w_out=full[("mla_w_out", 0)], rope=_rope_consts())


def big_grad_pieces(gl):
    g0, g1, g2, g3 = gl
    slots = lambda a: jnp.moveaxis(a.reshape(a.shape[0], 4, a.shape[1] // 4), 1, 0)
    out = {}
    for l, g in ((0, g0), (1, g3)):
        if g is not None:
            out.update({("pool_w_in", l): g["w_in"], ("pool_w_grp", l): g["w_grp"], ("pool_w_out", l): g["w_out"]})
    if g1 is not None:
        out.update({("gdn_w_in", 0): slots(g1["w_in"][:, :GDN_IN]), ("gdn_w_out", 0): g1["w_out"]})
    if g2 is not None:
        out.update({("mla_w_in", 0): slots(_mla_w_in_unlayout(g2["w_in"])), ("mla_w_uq", 0): _mla_w_uq_unlayout(g2["w_uq"]),
                    ("mla_w_ukv", 0): g2["w_ukv"], ("mla_w_out", 0): g2["w_out"]})
    return out


def small_grads(gl, dfinal):
    g0, g1, g2, g3 = gl
    return {"norm_g": jnp.concatenate([g0["ng"], g1["ng"], g2["ng"], g3["ng"]], axis=0),
            "pool_scale": jnp.concatenate([g0["scale"], g3["scale"]], axis=0), "gdn_conv": g1["conv"][None, :4],
            "gdn_a_log": g1["a_log"][:, :GDN_H], "gdn_dt_bias": g1["dt_bias"][:, :GDN_H], "gdn_norm_g": g1["norm_g"],
            "mla_q_norm_g": g2["q_g"], "mla_kv_norm_g": g2["kv_g"], "final_g": dfinal.reshape(D)}


NAMES = ["norm_g", "pool_w_in", "pool_w_grp", "pool_scale", "pool_w_out", "gdn_w_in", "gdn_conv", "gdn_a_log", "gdn_dt_bias",
         "gdn_norm_g", "gdn_w_out", "mla_w_in", "mla_q_norm_g", "mla_w_uq", "mla_kv_norm_g", "mla_w_ukv", "mla_w_out", "final_g"]


def kernel(x, positions, norm_g, pool_w_in, pool_w_grp, pool_scale, pool_w_out, gdn_w_in, gdn_conv, gdn_a_log, gdn_dt_bias, gdn_norm_g, gdn_w_out, mla_w_in, mla_q_norm_g, mla_w_uq, mla_kv_norm_g, mla_w_ukv, mla_w_out, final_g, loss_target, m_norm_g, m_pool_w_in, m_pool_w_grp, m_pool_scale, m_pool_w_out, m_gdn_w_in, m_gdn_conv, m_gdn_a_log, m_gdn_dt_bias, m_gdn_norm_g, m_gdn_w_out, m_mla_w_in, m_mla_q_norm_g, m_mla_w_uq, m_mla_kv_norm_g, m_mla_w_ukv, m_mla_w_out, m_final_g, v_norm_g, v_pool_w_in, v_pool_w_grp, v_pool_scale, v_pool_w_out, v_gdn_w_in, v_gdn_conv, v_gdn_a_log, v_gdn_dt_bias, v_gdn_norm_g, v_gdn_w_out, v_mla_w_in, v_mla_q_norm_g, v_mla_w_uq, v_mla_kv_norm_g, v_mla_w_ukv, v_mla_w_out, v_final_g):
    args = locals()
    w = {n: args[n] for n in NAMES}
    m = {n: args["m_" + n] for n in NAMES}
    v = {n: args["v_" + n] for n in NAMES}
    my_chip = (2 * lax.axis_index("x") + lax.axis_index("y")).astype(I32)
    S_ = x.shape[1]
    x0, pos, target = x[0], positions.reshape(S_, 1).astype(F32), loss_target[0]
    rep = {n: w[n] for n in REPLICATED}

    shard = {(n, l): w[n][l:l + 1].astype(BF16) for n in BIG for l in range(w[n].shape[0])}
    small_shapes = [w[n].shape for n in SMALL_SHARDED]
    shard[("small", 0)] = _pack([w[n] for n in SMALL_SHARDED], F32, 8)[None]
    layout = dict(BIG_LAYOUT, small=(None, None, (4,) + shard[("small", 0)].shape[1:]))

    def gather_start(group, after, tag):
        pieces = [gather_piece(i, 0, i, layout[n][0], layout[n][1]) for i, (n, l) in enumerate(group)]
        shapes = [_sds(layout[n][2], shard[(n, l)].dtype) for n, l in group]
        sems, ins, lands, token = exchange_start(pieces, [shard[k] for k in group], shapes, after, tag + "_start")
        return (pieces, sems, ins, lands), token

    def finish(handle, after, tag):
        return exchange_wait(*handle, after, tag + "_wait")

    def gathered(group, handle, after, tag):
        srcs, lands = finish(handle, after, tag)
        return {(n, l): place_own(a, s[0], layout[n][0], layout[n][1], my_chip) for (n, l), s, a in zip(group, srcs, lands)}

    tied = lambda a, token: a + token[0:1, 0:1]
    group_a = [("pool_w_in", 0), ("pool_w_grp", 0), ("pool_w_out", 0), ("small", 0)]
    group_b = [("gdn_w_in", 0), ("gdn_w_out", 0)]
    group_c = [("mla_w_in", 0), ("mla_w_uq", 0), ("mla_w_ukv", 0), ("mla_w_out", 0), ("pool_w_in", 1), ("pool_w_grp", 1), ("pool_w_out", 1)]
    full = {}
    h_a, t_a = gather_start(group_a, x0, "gather_a")
    full.update(gathered(group_a, h_a, t_a, "gather_a"))
    small = {n: _unshard(a, SMALL_AXIS[n]) for n, a in zip(SMALL_SHARDED, _unpack(full[("small", 0)], small_shapes))}
    h_b, t_b = gather_start(group_b, full[group_a[0]], "gather_b")
    W0 = layer_weights(full, small, rep, 0)
    x1, s0 = pool_fwd(x0, dict(W0, ng=tied(W0["ng"], t_b)), "l0")
    full.update(gathered(group_b, h_b, x1, "gather_b"))
    h_c, t_c = gather_start(group_c, full[group_b[0]], "gather_c")
    W1 = layer_weights(full, small, rep, 1)
    x2, s1 = gdn_fwd(x1, dict(W1, ng=tied(W1["ng"], t_c)), "l1")
    full.update(gathered(group_c, h_c, x2, "gather_c"))
    W2, W3 = layer_weights(full, small, rep, 2), layer_weights(full, small, rep, 3)
    x3, s2 = mla_fwd(x2, pos, W2, "l2")
    x4, s3 = pool_fwd(x3, W3, "l3")
    loss_part, dx4, dfinal = loss_head(x4, target, final_g.reshape(1, D), "loss_head")

    def scatter_start(pieces_of, after, tag):
        keys = list(pieces_of)
        pieces = [scatter_piece(i, i, BIG_LAYOUT[n][0], BIG_LAYOUT[n][1]) for i, (n, l) in enumerate(keys)]
        shapes = [_sds((4,) + tuple(w[n].shape[1:]), BF16) for n, l in keys]
        sems, ins, lands, token = exchange_start(pieces, [pieces_of[k] for k in keys], shapes, after, tag + "_start")
        return keys, (pieces, sems, ins, lands), token

    def scattered(keys, handle, after, tag):
        srcs, lands = finish(handle, after, tag)
        return {(n, l): place_own(a, own_window(g, BIG_LAYOUT[n][0], BIG_LAYOUT[n][1], my_chip), None, None, my_chip)
                for (n, l), g, a in zip(keys, srcs, lands)}

    dx3, g3 = pool_bwd(dx4, W3, s3, "l3")
    k3, h3, t3 = scatter_start(big_grad_pieces((None, None, None, g3)), dx3, "scatter_l3")
    dx2, g2 = mla_bwd(dx3, pos, W2, s2, "l2", after=t3)
    k2, h2, t2 = scatter_start(big_grad_pieces((None, None, g2, None)), dx2, "scatter_l2")
    dx1, g1 = gdn_bwd(dx2, W1, s1, "l1", after=t2)
    k1, h1, t1 = scatter_start(big_grad_pieces((None, g1, None, None)), dx1, "scatter_l1")
    dx0, g0 = pool_bwd(dx1, W0, s0, "l0", after=t1)
    k0, h0, t0 = scatter_start(big_grad_pieces((g0, None, None, None)), dx0, "scatter_l0")
    recv = {}
    for keys, handle, tag in ((k3, h3, "scatter_l3"), (k2, h2, "scatter_l2"), (k1, h1, "scatter_l1"), (k0, h0, "scatter_l0")):
        recv.update(scattered(keys, handle, t0, tag))
    keys = list(recv)
    sib = dict(zip(keys, swap_cores([recv[k] for k in keys], "swap_cores")))

    sg = small_grads((g0, g1, g2, g3), dfinal)
    small_names = SMALL_SHARDED + REPLICATED
    small_buf = _pack([sg[n] for n in small_names] + [loss_part], F32, 8)
    small_sum = sum_slots(None, exchange_all(small_buf, "gather_small"), jnp.full((1,), -1, I32), "sum_small")
    full_small = _unpack(small_sum, [sg[n].shape for n in small_names] + [(1, LANES)])
    loss = full_small[-1][0, 0]
    small_part = {}
    for n, a in zip(small_names, full_small[:-1]):
        if n in SMALL_AXIS:
            a = lax.dynamic_index_in_dim(_to_shards(a, SMALL_AXIS[n]), my_chip, axis=0, keepdims=False)
        small_part[n] = a

    outs = []
    for n in NAMES:
        shp = w[n].shape
        two = (int(np.prod(shp[:-1])), shp[-1]) if len(shp) > 1 else (1, shp[0])
        if n in BIG_LAYOUT:
            layers = shp[0]
            rows = lambda a: a.reshape(4, two[0] // layers, two[1])
            parts = [[rows(recv[(n, l)]) for l in range(layers)], [rows(sib[(n, l)]) for l in range(layers)]]
        else:
            parts = [[small_part[n].reshape((1,) + two)]]
        res = adamw(w[n].reshape(two), parts, m[n].reshape(two), v[n].reshape(two), "adamw_" + n)
        outs.append([r.reshape(shp) for r in res])
    return (loss, dx0[None], *[o[0] for o in outs], *[o[1] for o in outs], *[o[2] for o in outs], *[o[3] for o in outs])
```

```python
import functools
import math

import jax
import jax.numpy as jnp
import numpy as np
from jax import lax
from jax.experimental import pallas as pl
from jax.experimental.pallas import tpu as pltpu

F32 = jnp.float32
BF16 = jnp.bfloat16
I32 = jnp.int32

D = 1024
EPS = 1e-6
POOL_WIDTH = 2048
POOL_GROUP = 512
GDN_H, GDN_DK, GDN_DV, GDN_C = 8, 128, 256, 64
GDN_QK, GDN_V, GDN_CONV_CH, GDN_IN = 1024, 2048, 4096, 6160
GDN_IN_PAD = 6272
MLA_H, MLA_NOPE, MLA_ROPE, MLA_V = 16, 128, 64, 128
MLA_Q_LORA, MLA_KV_LORA, MLA_WIDTH, MLA_IN = 768, 512, 2048, 3392
MLA_IN_PAD = 4096
MLA_SCALE = (MLA_NOPE + MLA_ROPE) ** -0.5
ROPE_THETA = 10000.0
ADAM_LR, ADAM_B1, ADAM_B2, ADAM_EPS, ADAM_WD, ADAM_STEP = 0.001, 0.9, 0.999, 1e-08, 0.01, 10

VMEM_LIMIT_V7X = 56 * 1024 * 1024
LANES = 128
MESH = pl.DeviceIdType.MESH


def _pc(body, **kw):
    return pl.pallas_call(body, **kw)


def _cparams(sem):
    return pltpu.CompilerParams(dimension_semantics=sem, vmem_limit_bytes=VMEM_LIMIT_V7X)


def _tile(n, cap):
    t = (cap // LANES) * LANES
    while t >= LANES:
        if n % t == 0:
            return t
        t -= LANES
    return n


def _sds(shape, dt):
    return jax.ShapeDtypeStruct(shape, dt)


def mm(a, b, *, ta=False, tb=False, add=None, after=None, out_dtype=F32, name):
    if ta:
        K, M = a.shape
    else:
        M, K = a.shape
    if tb:
        N, K2 = b.shape
    else:
        K2, N = b.shape
    assert K == K2, (a.shape, b.shape, ta, tb)
    tm, tn, tk = _tile(M, 1024), _tile(N, 1024), _tile(K, 1024)
    nk = K // tk
    a_spec = pl.BlockSpec((tk, tm), lambda i, j, k: (k, i)) if ta else pl.BlockSpec((tm, tk), lambda i, j, k: (i, k))
    b_spec = pl.BlockSpec((tn, tk), lambda i, j, k: (j, k)) if tb else pl.BlockSpec((tk, tn), lambda i, j, k: (k, j))
    o_spec = pl.BlockSpec((tm, tn), lambda i, j, k: (i, j))
    dn = (((0 if ta else 1,), (1 if tb else 0,)), ((), ()))
    has_add = add is not None

    def body(*refs):
        a_ref, b_ref = refs[0], refs[1]
        part = lax.dot_general(a_ref[...].astype(BF16), b_ref[...].astype(BF16), dn, preferred_element_type=F32)
        if nk == 1:
            refs[-1][...] = (part + refs[2][...] if has_add else part).astype(out_dtype)
            return
        o_ref, acc = refs[-2], refs[-1]
        k = pl.program_id(2)

        @pl.when(k == 0)
        def _():
            acc[...] = part

        @pl.when(k > 0)
        def _():
            acc[...] += part

        @pl.when(k == nk - 1)
        def _():
            r = acc[...]
            if has_add:
                r = r + refs[2][...]
            o_ref[...] = r.astype(out_dtype)

    ins = [a, b] + ([add] if has_add else []) + ([after] if after is not None else [])
    specs = [a_spec, b_spec] + ([o_spec] if has_add else []) + ([pl.BlockSpec(memory_space=pl.ANY)] if after is not None else [])
    return _pc(body, grid=(M // tm, N // tn, nk), in_specs=specs, out_specs=o_spec, out_shape=_sds((M, N), out_dtype),
               scratch_shapes=[pltpu.VMEM((tm, tn), F32)] if nk > 1 else [], compiler_params=_cparams(("parallel", "parallel", "arbitrary")),
               name=name)(*ins)


def gmm(kind, a, b, *, G, name, out_dtype=F32):
    S_ = a.shape[0]
    Ka = a.shape[1] // G
    if kind == "tn":
        N = b.shape[1] // G
        tk = _tile(S_, 512)
        nk = S_ // tk

        def body(a_ref, b_ref, o_ref, acc):
            k = pl.program_id(1)

            @pl.when(k == 0)
            def _():
                acc[...] = jnp.zeros_like(acc)

            acc[...] += lax.dot_general(a_ref[...].astype(BF16), b_ref[...].astype(BF16), (((0,), (0,)), ((), ())),
                                        preferred_element_type=F32)

            @pl.when(k == nk - 1)
            def _():
                o_ref[...] = acc[...].astype(out_dtype)

        return _pc(body, grid=(G, nk),
                   in_specs=[pl.BlockSpec((tk, Ka), lambda g, k: (k, g)), pl.BlockSpec((tk, N), lambda g, k: (k, g))],
                   out_specs=pl.BlockSpec((None, Ka, N), lambda g, k: (g, 0, 0)), out_shape=_sds((G, Ka, N), out_dtype),
                   scratch_shapes=[pltpu.VMEM((Ka, N), F32)], compiler_params=_cparams(("parallel", "arbitrary")), name=name)(a, b)
    N = b.shape[2] if kind == "nn" else b.shape[1]
    tm = _tile(S_, 1024)
    dn = (((1,), (0 if kind == "nn" else 1,)), ((), ()))

    def body(a_ref, b_ref, o_ref):
        o_ref[...] = lax.dot_general(a_ref[...].astype(BF16), b_ref[...].astype(BF16), dn, preferred_element_type=F32)

    bshape = (None,) + tuple(b.shape[1:])
    return _pc(body, grid=(G, S_ // tm),
               in_specs=[pl.BlockSpec((tm, Ka), lambda g, i: (i, g)), pl.BlockSpec(bshape, lambda g, i: (g, 0, 0))],
               out_specs=pl.BlockSpec((tm, N), lambda g, i: (i, g)), out_shape=_sds((S_, G * N), F32),
               compiler_params=_cparams(("parallel", "parallel")), name=name)(a, b)


def _rw_spec(ts, w, c, s):
    return pl.BlockSpec((ts, w), lambda j, i: (i, c + j * s))


def _rw_pspec(p, w, c, s):
    return pl.BlockSpec((p.shape[0], w), lambda j, i: (0, c + j * s))


def rowwise(f, tiles, params, outs, *, ncol=1, ts, name):
    S_ = tiles[0][0].shape[0]
    nin = len(tiles) + len(params)

    def body(*refs):
        res = f(pl.program_id(0), *[r[...] for r in refs[:nin]])
        for r, o in zip(refs[nin:], res):
            r[...] = o.astype(r.dtype)

    return _pc(body, grid=(ncol, S_ // ts),
               in_specs=[_rw_spec(ts, w, c, s) for (_, w, c, s) in tiles] + [_rw_pspec(*p) for p in params],
               out_specs=[_rw_spec(ts, w, 0, s) for (w, s, _) in outs],
               out_shape=[_sds((S_, w * (ncol if s else 1)), dt) for (w, s, dt) in outs],
               compiler_params=_cparams(("parallel", "parallel")), name=name)(*[t[0] for t in tiles], *[p[0] for p in params])


def rowwise_bwd(f, tiles, params, cots, *, need, adds=None, place=None, narrow=(), ncol=1, ts, name):
    S_ = tiles[0][0].shape[0]
    adds = adds or {}
    place = place or {}
    nt, npar, nc = len(tiles), len(params), len(cots)
    add_keys = sorted(adds)
    need_idx = [k for k in range(nt) if need[k]]
    into_keys = [k for k in need_idx if k in place and not isinstance(place[k][0], int)]
    n_extra = len(add_keys) + len(into_keys)

    def body(*refs):
        j, i = pl.program_id(0), pl.program_id(1)
        vals = [r[...] for r in refs[:nt + npar]]
        cvals = tuple(r[...] for r in refs[nt + npar:nt + npar + nc])
        add_refs = refs[nt + npar + nc:nt + npar + nc + len(add_keys)]
        out_refs = refs[nt + npar + nc + n_extra:]
        _, vjp = jax.vjp(lambda *v: tuple(f(j, *v)), *vals)
        grads = vjp(cvals)
        for n, k in enumerate(need_idx):
            g = grads[k]
            if k in adds:
                g = g + add_refs[add_keys.index(k)][...]
            out_refs[n][...] = g.astype(out_refs[n].dtype)
        for n in range(npar):
            ref = out_refs[len(need_idx) + n]
            first = (i == 0) if params[n][3] else jnp.logical_and(i == 0, j == 0)

            @pl.when(first)
            def _():
                ref[...] = jnp.zeros_like(ref)

            ref[...] += grads[nt + n]

    in_specs = ([_rw_spec(ts, w, c, s) for (_, w, c, s) in tiles] + [_rw_pspec(*p) for p in params]
                + [_rw_spec(ts, w, c, s) for (_, w, c, s) in cots] + [_rw_spec(ts, *adds[k][1:]) for k in add_keys]
                + [pl.BlockSpec(memory_space=pl.ANY) for _ in into_keys])
    out_specs, out_shape, aliases = [], [], {}
    for n, k in enumerate(need_idx):
        w, s = tiles[k][1], tiles[k][3]
        if k in place:
            dst, c0 = place[k]
            total = dst if isinstance(dst, int) else dst.shape[1]
            out_specs.append(_rw_spec(ts, w, c0, s))
            out_shape.append(_sds((S_, total), (BF16 if k in narrow else F32) if isinstance(dst, int) else dst.dtype))
            if k in into_keys:
                aliases[nt + npar + nc + len(add_keys) + into_keys.index(k)] = n
        else:
            out_specs.append(_rw_spec(ts, w, 0, s))
            out_shape.append(_sds((S_, w * (ncol if s else 1)), BF16 if k in narrow else F32))
    out_specs += [_rw_pspec(p[0], p[1], p[2], p[3]) for p in params]
    out_shape += [_sds(p[0].shape, F32) for p in params]
    res = _pc(body, grid=(ncol, S_ // ts), in_specs=in_specs, out_specs=out_specs, out_shape=out_shape,
              input_output_aliases=aliases, compiler_params=_cparams(("arbitrary", "arbitrary")), name=name)(
        *[t[0] for t in tiles], *[p[0] for p in params], *[c[0] for c in cots], *[adds[k][0] for k in add_keys],
        *[place[k][0] for k in into_keys])
    return list(res[:len(need_idx)]), list(res[len(need_idx):])


def _rms(x, g):
    r = lax.rsqrt(jnp.mean(x * x, axis=-1, keepdims=True) + EPS)
    return x * r * g


def _silu(x):
    return x * jax.nn.sigmoid(x)


@jax.custom_vjp
def _softplus(x):
    return jnp.maximum(x, 0.0) + jnp.log1p(jnp.exp(-jnp.abs(x)))


_softplus.defvjp(lambda x: (_softplus(x), x), lambda x, d: (d * jax.nn.sigmoid(x),))


def f_rms(j, x, g):
    return (_rms(x, g),)


def f_pool_gate(j, pg, gate, scale):
    return (pg * scale * _silu(gate),)


def f_ogate(j, o, gate):
    return (o * _silu(gate),)


def f_gdn_out(j, o, gate, g):
    return (_rms(o, g) * _silu(gate),)


def f_gdn_gates(j, ba, alog, dtb):
    lane = lax.broadcasted_iota(I32, (1, LANES), 1)
    gs, bs = [], []
    for h in range(GDN_H):
        eb = (lane == h).astype(F32)
        ea = (lane == GDN_H + h).astype(F32)
        b = jnp.sum(ba * eb, -1, keepdims=True)
        a = jnp.sum(ba * ea, -1, keepdims=True)
        al = jnp.sum(alog * eb, -1, keepdims=True)
        dt = jnp.sum(dtb * eb, -1, keepdims=True)
        g = -jnp.exp(al) * _softplus(a + dt)
        gs.append(jnp.broadcast_to(g, ba.shape))
        bs.append(jnp.broadcast_to(jax.nn.sigmoid(b), ba.shape))
    return jnp.concatenate(gs, 1), jnp.concatenate(bs, 1)


def _shift_dn(x, k):
    rows = lax.broadcasted_iota(I32, x.shape, 0)
    return jnp.where(rows < k, 0.0, pltpu.roll(x, k, 0))


def _shift_up(x, k):
    n = x.shape[0]
    rows = lax.broadcasted_iota(I32, x.shape, 0)
    return jnp.where(rows >= n - k, 0.0, pltpu.roll(x, n - k, 0))


def _pool_window(j):
    g = lax.div(j, POOL_GROUP // LANES)
    return jnp.where(g == 0, 2.0, jnp.where(g == 1, 4.0, jnp.where(g == 2, 8.0, 16.0))), g


def _pick(g, a2, a4, a8, a16):
    return jnp.where(g == 0, a2, jnp.where(g == 1, a4, jnp.where(g == 2, a8, a16)))


def pool_time_fwd(proj, name):
    S_ = proj.shape[0]

    def body(u_ref, p_ref):
        u = u_ref[...]
        w, g = _pool_window(pl.program_id(0))
        s2 = u + _shift_dn(u, 1)
        s4 = s2 + _shift_dn(s2, 2)
        s8 = s4 + _shift_dn(s4, 4)
        s16 = s8 + _shift_dn(s8, 8)
        t1 = (lax.broadcasted_iota(I32, u.shape, 0) + 1).astype(F32)
        p_ref[...] = (_pick(g, s2, s4, s8, s16) / jnp.minimum(t1, w) - u).astype(p_ref.dtype)

    return _pc(body, grid=(POOL_WIDTH // LANES,), in_specs=[pl.BlockSpec((S_, LANES), lambda j: (0, j))],
               out_specs=pl.BlockSpec((S_, LANES), lambda j: (0, j)), out_shape=_sds((S_, POOL_WIDTH), BF16),
               compiler_params=_cparams(("parallel",)), name=name)(proj)


def pool_time_bwd(dp, into, name):
    S_ = dp.shape[0]

    def body(dp_ref, _, du_ref):
        d = dp_ref[...]
        w, g = _pool_window(pl.program_id(0))
        t1 = (lax.broadcasted_iota(I32, d.shape, 0) + 1).astype(F32)
        q = d / jnp.minimum(t1, w)
        r2 = q + _shift_up(q, 1)
        r4 = r2 + _shift_up(r2, 2)
        r8 = r4 + _shift_up(r4, 4)
        r16 = r8 + _shift_up(r8, 8)
        du_ref[...] = (_pick(g, r2, r4, r8, r16) - d).astype(du_ref.dtype)

    return _pc(body, grid=(POOL_WIDTH // LANES,),
               in_specs=[pl.BlockSpec((S_, LANES), lambda j: (0, j)), pl.BlockSpec(memory_space=pl.ANY)],
               out_specs=pl.BlockSpec((S_, LANES), lambda j: (0, j)), out_shape=_sds(into.shape, into.dtype),
               input_output_aliases={1: 0}, compiler_params=_cparams(("parallel",)), name=name)(dp, into)


def _conv_post(j, a):
    n = a * lax.rsqrt(jnp.sum(a * a, axis=-1, keepdims=True) + EPS)
    nq = GDN_QK // LANES
    return jnp.where(j < nq, n * (GDN_DK ** -0.5), jnp.where(j < 2 * nq, n, a))


def _conv_pre(u, w):
    return w[3:4] * u + w[2:3] * _shift_dn(u, 1) + w[1:2] * _shift_dn(u, 2) + w[0:1] * _shift_dn(u, 3)


def gdn_conv_fwd(proj, conv_w, name):
    S_ = proj.shape[0]

    def body(u_ref, w_ref, o_ref):
        o_ref[...] = _conv_post(pl.program_id(0), _silu(_conv_pre(u_ref[...], w_ref[...])))

    return _pc(body, grid=(GDN_CONV_CH // LANES,),
               in_specs=[pl.BlockSpec((S_, LANES), lambda j: (0, j)), pl.BlockSpec((8, LANES), lambda j: (0, j))],
               out_specs=pl.BlockSpec((S_, LANES), lambda j: (0, j)), out_shape=_sds((S_, GDN_CONV_CH), F32),
               compiler_params=_cparams(("parallel",)), name=name)(proj, conv_w)


def gdn_conv_bwd(proj, conv_w, dq, dk, dv, into, name):
    S_ = proj.shape[0]
    nq = GDN_QK // LANES

    def body(u_ref, w_ref, dq_ref, dk_ref, dv_ref, _, du_ref, dw_ref):
        j = pl.program_id(0)
        u, w = u_ref[...], w_ref[...]
        c = _conv_pre(u, w)
        sig = jax.nn.sigmoid(c)
        dout = jnp.where(j < nq, dq_ref[...], jnp.where(j < 2 * nq, dk_ref[...], dv_ref[...]))
        _, vjp = jax.vjp(lambda a: _conv_post(j, a), c * sig)
        dc = vjp(dout)[0] * (sig * (1.0 + c * (1.0 - sig)))
        du = w[3:4] * dc + w[2:3] * _shift_up(dc, 1) + w[1:2] * _shift_up(dc, 2) + w[0:1] * _shift_up(dc, 3)
        du_ref[...] = du.astype(du_ref.dtype)
        rows = lax.broadcasted_iota(I32, (8, LANES), 0)
        dw = jnp.zeros((8, LANES), F32)
        for k in range(4):
            us = u if k == 3 else _shift_dn(u, 3 - k)
            dw = dw + jnp.where(rows == k, jnp.sum(dc * us, axis=0, keepdims=True), 0.0)
        dw_ref[...] = dw

    blk = lambda f: pl.BlockSpec((S_, LANES), f)
    return _pc(body, grid=(GDN_CONV_CH // LANES,),
               in_specs=[blk(lambda j: (0, j)), pl.BlockSpec((8, LANES), lambda j: (0, j)),
                         blk(lambda j: (0, jnp.minimum(j, nq - 1))), blk(lambda j: (0, jnp.clip(j - nq, 0, nq - 1))),
                         blk(lambda j: (0, jnp.clip(j - 2 * nq, 0, 2 * nq - 1))), pl.BlockSpec(memory_space=pl.ANY)],
               out_specs=[blk(lambda j: (0, j)), pl.BlockSpec((8, LANES), lambda j: (0, j))],
               out_shape=[_sds(into.shape, into.dtype), _sds((8, GDN_CONV_CH), F32)], input_output_aliases={5: 0},
               compiler_params=_cparams(("parallel",)), name=name)(proj, conv_w, dq, dk, dv, into)


_NN, _NT, _TN = ((1,), (0,)), ((1,), (1,)), ((0,), (0,))


def _split(x, n):
    parts = []
    for _ in range(n):
        h = x.astype(BF16)
        parts.append(h)
        x = x - h.astype(F32)
    return parts


def _dot(a, b, dn, mode):
    d = lambda p, q: lax.dot_general(p, q, (dn, ((), ())), preferred_element_type=F32)
    if mode == "lo":
        return d(a.astype(BF16), b.astype(BF16))
    if mode == "x3":
        (ah, al), (bh, bl) = _split(a, 2), _split(b, 2)
        return d(ah, bh) + (d(ah, bl) + d(al, bh))
    b0, b1, b2 = _split(b, 3)
    ab = a.astype(BF16)
    return d(ab, b0) + (d(ab, b1) + d(ab, b2))


def _make_dots(mode):
    @jax.custom_vjp
    def nn(a, b):
        return _dot(a, b, _NN, mode)

    @jax.custom_vjp
    def nt(a, b):
        return _dot(a, b, _NT, mode)

    @jax.custom_vjp
    def tn(a, b):
        return _dot(a, b, _TN, mode)

    nn.defvjp(lambda a, b: (nn(a, b), (a, b)), lambda r, d: (nt(d, r[1]), tn(r[0], d)))
    nt.defvjp(lambda a, b: (nt(a, b), (a, b)), lambda r, d: (nn(d, r[1]), tn(d, r[0])))
    tn.defvjp(lambda a, b: (tn(a, b), (a, b)), lambda r, d: (nt(r[1], d), nn(r[0], d)))
    return nn, nt, tn


_nn_hi, _nt_hi, _tn_hi = _make_dots("x3")
_nn_lo, _nt_lo, _tn_lo = _make_dots("lo")


@jax.custom_vjp
def _nn_const(a, b):
    return _dot(a, b, _NN, "xl")


_nn_const.defvjp(lambda a, b: (_nn_const(a, b), a), lambda a, d: (jnp.zeros_like(a), _dot(a, d, _TN, "xl")))


def _each(f, *lists):
    return [f(*xs) for xs in zip(*lists)]


@jax.custom_vjp
def _unit_inverses(xs):
    C = xs[0].shape[0]
    eye = (lax.broadcasted_iota(I32, (C, C), 0) == lax.broadcasted_iota(I32, (C, C), 1)).astype(F32)
    ainv, p = [eye + a for a in xs], xs
    for _ in range(int(math.log2(C)) - 1):
        p = _each(lambda a: _dot(a, a, _NN, "x3"), p)
        ainv = _each(lambda a, b: a + _dot(a, b, _NN, "x3"), ainv, p)
    return ainv


def _unit_inverses_bwd(ainv, d):
    left = _each(lambda a, g: _dot(a, g, _TN, "x3"), ainv, d)
    return (_each(lambda t, a: _dot(t, a, _NT, "x3"), left, ainv),)


_unit_inverses.defvjp(lambda xs: (lambda a: (a, a))(_unit_inverses(xs)), _unit_inverses_bwd)


def _gdn_chunk(q, k, v, gb, bb, state):
    C = GDN_C
    e0 = (lax.broadcasted_iota(I32, (1, LANES), 1) == 0).astype(F32)
    ri = lax.broadcasted_iota(I32, (C, C), 0)
    ci = lax.broadcasted_iota(I32, (C, C), 1)
    causal, strict = ri >= ci, ri > ci
    tri, eye, ones = causal.astype(F32), (ri == ci).astype(F32), jnp.ones((C, C), F32)
    last = lax.broadcasted_iota(I32, (C, LANES), 0) == C - 1
    g1 = _each(lambda a: jnp.sum(a * e0, -1, keepdims=True), gb)
    b1 = _each(lambda a: jnp.sum(a * e0, -1, keepdims=True), bb)
    gc_c = _each(lambda g: _nn_const(tri, jnp.broadcast_to(g, (C, C))), g1)
    gc_d = _each(lambda g: _nn_const(tri, jnp.broadcast_to(g, (C, LANES))), g1)
    gr_c = _each(lambda g: _nn_const(ones, eye * g), gc_c)
    decay = _each(lambda a, r: jnp.where(causal, jnp.exp(jnp.where(causal, a - r, 0.0)), 0.0), gc_c, gr_c)
    kb = _each(lambda a, b: a * b, k, b1)
    vb = _each(lambda a, b: a * b, v, b1)
    x = _each(lambda a, b, d: -jnp.where(strict, _nt_lo(a, b) * d, 0.0), kb, k, decay)
    ainv = _unit_inverses(x)
    u = _each(_nn_hi, ainv, vb)
    w = _each(lambda a, b, g: _nn_hi(a, b * jnp.exp(g)), ainv, kb, gc_d)
    attn = _each(lambda a, b, d: jnp.where(causal, _nt_lo(a, b) * d, 0.0), q, k, decay)
    v_new = _each(lambda a, b, s: a - _nn_lo(b, s), u, w, state)
    o = _each(lambda a, g, s, t, vn: _nn_lo(a * jnp.exp(g), s) + _nn_lo(t, vn), q, gc_d, state, attn, v_new)
    gl = _each(lambda g: jnp.sum(jnp.where(last, g, 0.0), axis=0, keepdims=True), gc_d)
    new_state = _each(lambda s, g, a, gd, vn: s * jnp.exp(jnp.sum(g * e0, -1, keepdims=True)) + _tn_lo(a * jnp.exp(g - gd), vn),
                      state, gl, k, gc_d, v_new)
    return o, new_state


def _head_slices(ref, width):
    return [ref[:, h * width:(h + 1) * width] for h in range(GDN_H)]


def gdn_chunk_fwd(qkv, g_b, beta_b, name):
    S_ = qkv.shape[0]
    N = S_ // GDN_C

    def body(q_ref, k_ref, v_ref, g_ref, b_ref, o_ref, s_ref, state):
        @pl.when(pl.program_id(0) == 0)
        def _():
            state[...] = jnp.zeros_like(state)

        st = [state[h] for h in range(GDN_H)]
        s_ref[0] = state[...]
        o, st2 = _gdn_chunk(_head_slices(q_ref, GDN_DK), _head_slices(k_ref, GDN_DK), _head_slices(v_ref, GDN_DV),
                            _head_slices(g_ref, GDN_DK), _head_slices(b_ref, GDN_DK), st)
        for h in range(GDN_H):
            o_ref[:, h * GDN_DV:(h + 1) * GDN_DV] = o[h]
            state[h] = st2[h]

    return _pc(body, grid=(N,),
               in_specs=[pl.BlockSpec((GDN_C, GDN_QK), lambda n: (n, 0)), pl.BlockSpec((GDN_C, GDN_QK), lambda n: (n, 1)),
                         pl.BlockSpec((GDN_C, GDN_V), lambda n: (n, 1)), pl.BlockSpec((GDN_C, GDN_QK), lambda n: (n, 0)),
                         pl.BlockSpec((GDN_C, GDN_QK), lambda n: (n, 0))],
               out_specs=[pl.BlockSpec((GDN_C, GDN_V), lambda n: (n, 0)),
                          pl.BlockSpec((1, GDN_H, GDN_DK, GDN_DV), lambda n: (n, 0, 0, 0))],
               out_shape=[_sds((S_, GDN_V), F32), _sds((N, GDN_H, GDN_DK, GDN_DV), F32)],
               scratch_shapes=[pltpu.VMEM((GDN_H, GDN_DK, GDN_DV), F32)],
               compiler_params=_cparams(("arbitrary",)), name=name)(qkv, qkv, qkv, g_b, beta_b)


def gdn_chunk_bwd(qkv, g_b, beta_b, states, do, name):
    S_ = qkv.shape[0]
    N = S_ // GDN_C

    def body(q_ref, k_ref, v_ref, g_ref, b_ref, s_ref, do_ref, dq_ref, dk_ref, dv_ref, dg_ref, db_ref, dstate):
        @pl.when(pl.program_id(0) == 0)
        def _():
            dstate[...] = jnp.zeros_like(dstate)

        _, vjp = jax.vjp(_gdn_chunk, _head_slices(q_ref, GDN_DK), _head_slices(k_ref, GDN_DK), _head_slices(v_ref, GDN_DV),
                         _head_slices(g_ref, GDN_DK), _head_slices(b_ref, GDN_DK), [s_ref[0, h] for h in range(GDN_H)])
        dq, dk, dv, dg, db, ds = vjp((_head_slices(do_ref, GDN_DV), [dstate[h] for h in range(GDN_H)]))
        for h in range(GDN_H):
            kk, vv = slice(h * GDN_DK, (h + 1) * GDN_DK), slice(h * GDN_DV, (h + 1) * GDN_DV)
            dq_ref[:, kk] = dq[h]
            dk_ref[:, kk] = dk[h]
            dv_ref[:, vv] = dv[h]
            dg_ref[:, kk] = dg[h]
            db_ref[:, kk] = db[h]
            dstate[h] = ds[h]

    r = lambda n: N - 1 - n
    qk = lambda c: pl.BlockSpec((GDN_C, GDN_QK), lambda n: (r(n), c))
    vs = lambda c: pl.BlockSpec((GDN_C, GDN_V), lambda n: (r(n), c))
    return _pc(body, grid=(N,),
               in_specs=[qk(0), qk(1), vs(1), qk(0), qk(0),
                         pl.BlockSpec((1, GDN_H, GDN_DK, GDN_DV), lambda n: (r(n), 0, 0, 0)), vs(0)],
               out_specs=[qk(0), qk(0), vs(0), qk(0), qk(0)],
               out_shape=[_sds((S_, GDN_QK), F32), _sds((S_, GDN_QK), F32), _sds((S_, GDN_V), F32),
                          _sds((S_, GDN_QK), F32), _sds((S_, GDN_QK), F32)],
               scratch_shapes=[pltpu.VMEM((GDN_H, GDN_DK, GDN_DV), F32)],
               compiler_params=_cparams(("arbitrary",)), name=name)(qkv, qkv, qkv, g_b, beta_b, states, do)


def _rope_tables(pos_ref, inv_ref, cm_ref, sg_ref):
    ang = pos_ref[...] * inv_ref[...]
    return jnp.cos(ang) * cm_ref[...], jnp.sin(ang) * sg_ref[...]


def mla_prep_fwd(qpad, kv, proj, pos, rope_consts, name):
    S_ = qpad.shape[0]
    ts = 256
    W = 2 * LANES

    def body(q_ref, kv_ref, kr_ref, pos_ref, inv_ref, cm_ref, sg_ref, qh_ref, kh_ref, vh_ref):
        cs, sn = _rope_tables(pos_ref, inv_ref, cm_ref, sg_ref)
        rope = lambda r: r * cs + pltpu.roll(r, LANES // 2, 1) * sn
        krr = rope(kr_ref[...]).astype(BF16)
        for h in range(MLA_H):
            qh_ref[h, :, 0:LANES] = (q_ref[:, h * W:h * W + LANES] * MLA_SCALE).astype(BF16)
            qh_ref[h, :, LANES:W] = (rope(q_ref[:, h * W + LANES:(h + 1) * W]) * MLA_SCALE).astype(BF16)
            kh_ref[h, :, 0:LANES] = kv_ref[:, h * W:h * W + LANES].astype(BF16)
            kh_ref[h, :, LANES:W] = krr
            vh_ref[h] = kv_ref[:, h * W + LANES:(h + 1) * W].astype(BF16)

    one = pl.BlockSpec((1, LANES), lambda i: (0, 0))
    return _pc(body, grid=(S_ // ts,),
               in_specs=[pl.BlockSpec((ts, MLA_H * W), lambda i: (i, 0)), pl.BlockSpec((ts, MLA_H * W), lambda i: (i, 0)),
                         pl.BlockSpec((ts, LANES), lambda i: (i, 1536 // LANES)), pl.BlockSpec((ts, 1), lambda i: (i, 0)),
                         one, one, one],
               out_specs=[pl.BlockSpec((MLA_H, ts, W), lambda i: (0, i, 0)), pl.BlockSpec((MLA_H, ts, W), lambda i: (0, i, 0)),
                          pl.BlockSpec((MLA_H, ts, LANES), lambda i: (0, i, 0))],
               out_shape=[_sds((MLA_H, S_, W), BF16), _sds((MLA_H, S_, W), BF16), _sds((MLA_H, S_, LANES), BF16)],
               compiler_params=_cparams(("parallel",)), name=name)(qpad, kv, proj, pos, *rope_consts)


def mla_prep_bwd(dqh, dkh, dvh, pos, rope_consts, into, name):
    S_ = dqh.shape[1]
    ts = 256
    W = 2 * LANES

    def body(dq_ref, dk_ref, dv_ref, pos_ref, inv_ref, cm_ref, sg_ref, _, dqp_ref, dkv_ref, dkr_ref):
        cs, sn = _rope_tables(pos_ref, inv_ref, cm_ref, sg_ref)
        rope_t = lambda g: g * cs + pltpu.roll(g * sn, LANES // 2, 1)
        acc = jnp.zeros((ts, LANES), F32)
        for h in range(MLA_H):
            dqp_ref[:, h * W:h * W + LANES] = (dq_ref[h, :, 0:LANES].astype(F32) * MLA_SCALE).astype(BF16)
            dqp_ref[:, h * W + LANES:(h + 1) * W] = (rope_t(dq_ref[h, :, LANES:W].astype(F32)) * MLA_SCALE).astype(BF16)
            dkv_ref[:, h * W:h * W + LANES] = dk_ref[h, :, 0:LANES]
            dkv_ref[:, h * W + LANES:(h + 1) * W] = dv_ref[h]
            acc = acc + dk_ref[h, :, LANES:W].astype(F32)
        dkr_ref[...] = rope_t(acc).astype(dkr_ref.dtype)

    one = pl.BlockSpec((1, LANES), lambda i: (0, 0))
    return _pc(body, grid=(S_ // ts,),
               in_specs=[pl.BlockSpec((MLA_H, ts, W), lambda i: (0, i, 0)), pl.BlockSpec((MLA_H, ts, W), lambda i: (0, i, 0)),
                         pl.BlockSpec((MLA_H, ts, LANES), lambda i: (0, i, 0)), pl.BlockSpec((ts, 1), lambda i: (i, 0)),
                         one, one, one, pl.BlockSpec(memory_space=pl.ANY)],
               out_specs=[pl.BlockSpec((ts, MLA_H * W), lambda i: (i, 0)), pl.BlockSpec((ts, MLA_H * W), lambda i: (i, 0)),
                          pl.BlockSpec((ts, LANES), lambda i: (i, 1536 // LANES))],
               out_shape=[_sds((S_, MLA_H * W), BF16), _sds((S_, MLA_H * W), BF16), _sds(into.shape, into.dtype)],
               input_output_aliases={7: 2}, compiler_params=_cparams(("parallel",)), name=name)(dqh, dkh, dvh, pos, *rope_consts, into)


NEG = -1e30


FLASH_TILE = 1024
FLASH_SUB = 512


def _scores(q, k, diagonal):
    s = lax.dot_general(q, k, (_NT, ((), ())), preferred_element_type=F32)
    if not diagonal:
        return s
    return jnp.where(lax.broadcasted_iota(I32, s.shape, 1) <= lax.broadcasted_iota(I32, s.shape, 0), s, NEG)


def _sub_blocks(t, diagonal):
    sub = min(FLASH_SUB, t) if diagonal else t
    return [(c * sub if diagonal else 0, slice(c * sub, (c + 1) * sub)) for c in range(t // sub)]


FLASH_HEADS = 2


def flash_fwd(qh, kh, vh, name):
    H, S_, W = qh.shape
    t = _tile(S_, FLASH_TILE)
    n = S_ // t
    G = FLASH_HEADS
    heads = list(range(G))

    def body(q_ref, k_ref, v_ref, o_ref, lse_ref, m_s, l_s, acc):
        qi, kj = pl.program_id(1), pl.program_id(2)

        @pl.when(kj == 0)
        def _():
            m_s[...] = jnp.full_like(m_s, NEG)
            l_s[...] = jnp.zeros_like(l_s)
            acc[...] = jnp.zeros_like(acc)

        def step(diagonal):
            s = _each(lambda a: _scores(q_ref[a], k_ref[a], diagonal), heads)
            m_old = _each(lambda a: m_s[a], heads)
            m_new = _each(lambda mo, sa: jnp.maximum(mo, jnp.max(sa, axis=-1, keepdims=True)), m_old, s)
            alpha = _each(lambda mo, mn: jnp.exp(mo - mn), m_old, m_new)
            p = _each(lambda sa, mn: jnp.exp(sa - mn[:, :1]), s, m_new)
            pv = _each(lambda pa, a: lax.dot_general(pa.astype(BF16), v_ref[a], (_NN, ((), ())), preferred_element_type=F32), p, heads)
            for a in heads:
                l_s[a] = alpha[a] * l_s[a] + jnp.sum(p[a], axis=-1, keepdims=True)
                acc[a] = alpha[a] * acc[a] + pv[a]
                m_s[a] = m_new[a]

        pl.when(kj < qi)(lambda: step(False))
        pl.when(kj == qi)(lambda: step(True))

        @pl.when(kj == n - 1)
        def _():
            for a in heads:
                o_ref[:, a * LANES:(a + 1) * LANES] = acc[a] / l_s[a]
                lse_ref[a] = m_s[a] + jnp.log(l_s[a])

    return _pc(body, grid=(H // G, n, n),
               in_specs=[pl.BlockSpec((G, t, W), lambda h, i, j: (h, i, 0)),
                         pl.BlockSpec((G, t, W), lambda h, i, j: (h, jnp.minimum(i, j), 0)),
                         pl.BlockSpec((G, t, LANES), lambda h, i, j: (h, jnp.minimum(i, j), 0))],
               out_specs=[pl.BlockSpec((t, G * LANES), lambda h, i, j: (i, h)), pl.BlockSpec((G, t, LANES), lambda h, i, j: (h, i, 0))],
               out_shape=[_sds((S_, H * LANES), F32), _sds((H, S_, LANES), F32)],
               scratch_shapes=[pltpu.VMEM((G, t, LANES), F32)] * 3,
               compiler_params=_cparams(("parallel", "parallel", "arbitrary")), name=name)(qh, kh, vh)


def flash_bwd(qh, kh, vh, o, lse, do, name):
    H, S_, W = qh.shape
    t = _tile(S_, FLASH_TILE)
    n = S_ // t

    def body(q_ref, k_ref, v_ref, o_ref, lse_ref, do_ref, dq_ref, dk_ref, dv_ref, dq_acc, dk_acc, dv_acc):
        kj, qi = pl.program_id(1), pl.program_id(2)

        @pl.when(jnp.logical_and(kj == 0, qi == 0))
        def _():
            dq_acc[...] = jnp.zeros_like(dq_acc)

        @pl.when(qi == 0)
        def _():
            dk_acc[...] = jnp.zeros_like(dk_acc)
            dv_acc[...] = jnp.zeros_like(dv_acc)

        def step(diagonal):
            do_ = do_ref[...]
            dob = do_.astype(BF16)
            delta = jnp.sum(do_ * o_ref[...], axis=-1, keepdims=True)
            for r0, keys in _sub_blocks(t, diagonal):
                q, k, v = q_ref[r0:, :], k_ref[keys, :], v_ref[keys, :]
                p = jnp.exp(_scores(q, k, diagonal) - lse_ref[r0:, :1])
                dv_acc[keys, :] += lax.dot_general(p.astype(BF16), dob[r0:], (_TN, ((), ())), preferred_element_type=F32)
                dp = lax.dot_general(dob[r0:], v, (_NT, ((), ())), preferred_element_type=F32)
                ds = (p * (dp - delta[r0:])).astype(BF16)
                dk_acc[keys, :] += lax.dot_general(ds, q, (_TN, ((), ())), preferred_element_type=F32)
                rows = pl.ds(pl.multiple_of(qi * t, t) + r0, t - r0)
                dq_acc[rows, :] += lax.dot_general(ds, k, (_NN, ((), ())), preferred_element_type=F32)

        pl.when(qi > kj)(lambda: step(False))
        pl.when(qi == kj)(lambda: step(True))

        @pl.when(qi == n - 1)
        def _():
            dk_ref[...] = dk_acc[...].astype(BF16)
            dv_ref[...] = dv_acc[...].astype(BF16)

        @pl.when(jnp.logical_and(kj == n - 1, qi == n - 1))
        def _():
            dq_ref[...] = dq_acc[...].astype(BF16)

    qrow = lambda h, j, i: jnp.maximum(i, j)
    return _pc(body, grid=(H, n, n),
               in_specs=[pl.BlockSpec((None, t, W), lambda h, j, i: (h, qrow(h, j, i), 0)),
                         pl.BlockSpec((None, t, W), lambda h, j, i: (h, j, 0)),
                         pl.BlockSpec((None, t, LANES), lambda h, j, i: (h, j, 0)),
                         pl.BlockSpec((t, LANES), lambda h, j, i: (qrow(h, j, i), h)),
                         pl.BlockSpec((None, t, LANES), lambda h, j, i: (h, qrow(h, j, i), 0)),
                         pl.BlockSpec((t, LANES), lambda h, j, i: (qrow(h, j, i), h))],
               out_specs=[pl.BlockSpec((None, S_, W), lambda h, j, i: (h, 0, 0)),
                          pl.BlockSpec((None, t, W), lambda h, j, i: (h, j, 0)),
                          pl.BlockSpec((None, t, LANES), lambda h, j, i: (h, j, 0))],
               out_shape=[_sds((H, S_, W), BF16), _sds((H, S_, W), BF16), _sds((H, S_, LANES), BF16)],
               scratch_shapes=[pltpu.VMEM((S_, W), F32), pltpu.VMEM((t, W), F32), pltpu.VMEM((t, LANES), F32)],
               compiler_params=_cparams(("parallel", "arbitrary", "arbitrary")), name=name)(qh, kh, vh, o, lse, do)


def loss_head(x, target, g, name):
    S_ = x.shape[0]
    ts = 256

    def body(x_ref, t_ref, g_ref, l_ref, dx_ref, dg_ref):
        @pl.when(pl.program_id(0) == 0)
        def _():
            l_ref[...] = jnp.zeros_like(l_ref)
            dg_ref[...] = jnp.zeros_like(dg_ref)

        y, vjp = jax.vjp(_rms, x_ref[...], g_ref[...])
        err = y - t_ref[...]
        l_ref[...] += 0.5 * jnp.sum(jnp.sum(err * err, axis=-1, keepdims=True), axis=0, keepdims=True) / D
        dx, dg = vjp(err / D)
        dx_ref[...] = dx
        dg_ref[...] += dg

    row = pl.BlockSpec((ts, D), lambda i: (i, 0))
    return _pc(body, grid=(S_ // ts,), in_specs=[row, row, pl.BlockSpec((1, D), lambda i: (0, 0))],
               out_specs=[pl.BlockSpec((1, LANES), lambda i: (0, 0)), row, pl.BlockSpec((1, D), lambda i: (0, 0))],
               out_shape=[_sds((1, LANES), F32), _sds((S_, D), F32), _sds((1, D), F32)],
               compiler_params=_cparams(("arbitrary",)), name=name)(x, target, g)


def adamw(w, parts, m, v, name):
    R, C = w.shape
    rows = [p.shape[1] for p in parts[0]]
    tr = R
    for cand in (512, 256, 128, 64, 32, 16, 8):
        if all(r % cand == 0 for r in rows) and cand * C * 4 <= 1024 * 1024:
            tr = cand
            break
    c1 = 1.0 - ADAM_B1 ** ADAM_STEP
    c2 = 1.0 - ADAM_B2 ** ADAM_STEP
    starts = [sum(rows[:k]) // tr for k in range(len(rows))]
    flat = [p for part in parts for p in part]

    def body(*refs):
        w_ref, m_ref, v_ref = refs[0], refs[1 + len(flat)], refs[2 + len(flat)]
        g_ref, d_ref, nm_ref, nv_ref = refs[3 + len(flat):]
        i = pl.program_id(0)
        gg, at = None, 1
        for part in parts:
            val = None
            for k in range(len(part)):
                p_ref = refs[at]
                at += 1
                s = p_ref[0].astype(F32)
                for n in range(1, p_ref.shape[0]):
                    s = s + p_ref[n].astype(F32)
                val = s if val is None else jnp.where(i >= starts[k], s, val)
            gg = val if gg is None else gg + val
        m2 = ADAM_B1 * m_ref[...] + (1.0 - ADAM_B1) * gg
        v2 = ADAM_B2 * v_ref[...] + (1.0 - ADAM_B2) * (gg * gg)
        g_ref[...] = gg
        d_ref[...] = -ADAM_LR * ((m2 / c1) / (jnp.sqrt(v2 / c2) + ADAM_EPS) + ADAM_WD * w_ref[...])
        nm_ref[...] = m2
        nv_ref[...] = v2

    blk = pl.BlockSpec((tr, C), lambda i: (i, 0))
    piece = lambda p, k: pl.BlockSpec((p.shape[0], tr, C), lambda i: (0, jnp.clip(i - starts[k], 0, rows[k] // tr - 1), 0))
    pblk = [piece(p, k) for part in parts for k, p in enumerate(part)]
    return _pc(body, grid=(R // tr,), in_specs=[blk] + pblk + [blk, blk], out_specs=[blk] * 4, out_shape=[_sds((R, C), F32)] * 4,
               compiler_params=_cparams(("parallel",)), name=name)(w, *flat, m, v)


def sum_slots(own, recv, skip, name):
    n, R, C = recv.shape
    tr = _tile(R, 512) if R % LANES == 0 else R
    has_own = own is not None

    def body(*refs):
        skip_ref = refs[0]
        r_ref, o_ref = refs[-2], refs[-1]
        acc = refs[1][...] if has_own else jnp.zeros(o_ref.shape, F32)
        for s in range(n):
            acc = acc + jnp.where(skip_ref[0] == s, 0.0, r_ref[s].astype(F32))
        o_ref[...] = acc

    row = pl.BlockSpec((tr, C), lambda i, sk: (i, 0))
    gs = pltpu.PrefetchScalarGridSpec(
        num_scalar_prefetch=1, grid=(R // tr,),
        in_specs=([row] if has_own else []) + [pl.BlockSpec((n, tr, C), lambda i, sk: (0, i, 0))], out_specs=row)
    ins = ([own] if has_own else []) + [recv]
    return _pc(body, grid_spec=gs, out_shape=_sds((R, C), F32), compiler_params=_cparams(("parallel",)), name=name)(skip, *ins)


def _chip_peers():
    x, y, c = lax.axis_index("x"), lax.axis_index("y"), lax.axis_index("c")
    return (x, y, c), [(1 - x, y, c), (x, 1 - y, c), (1 - x, 1 - y, c)]


def _chip_index(p):
    return 2 * p[0] + p[1]


def _win(ref, axis, chip, size):
    if axis is None:
        return ref.at[chip]
    idx = [slice(None)] * len(ref.shape)
    idx[axis] = pl.ds(pl.multiple_of(chip * size, size), size)
    return ref.at[tuple(idx)]


def _remote(src, dst, send_sem, recv_sem, peer):
    return pltpu.make_async_remote_copy(src_ref=src, dst_ref=dst, send_sem=send_sem, recv_sem=recv_sem, device_id=peer,
                                        device_id_type=MESH)


HBM_SPEC = pl.BlockSpec(memory_space=pltpu.HBM)
SEM_SPEC = pl.BlockSpec(memory_space=pltpu.SEMAPHORE)
ANY_SPEC = pl.BlockSpec(memory_space=pl.ANY)
DATAFLOW = pltpu.SideEffectType.DATAFLOW_SIDE_EFFECTING


def gather_piece(i, l, o, axis, size):
    return (i, lambda r, chip: r.at[l], o, lambda r, chip: _win(r, axis, chip, size))


def scatter_piece(i, o, axis, size):
    return (i, lambda r, chip: _win(r, axis, chip, size), o, lambda r, chip: r.at[chip])


def _copies(pieces, in_refs, out_refs, send, recv):
    me, peers = _chip_peers()
    mine = _chip_index(me)
    remote = []
    for n, (i, src, o, dst) in enumerate(pieces):
        d = dst(out_refs[o], mine)
        remote += [_remote(src(in_refs[i], _chip_index(p)), d, send.at[3 * n + k], recv.at[3 * n + k], p)
                   for k, p in enumerate(peers)]
    return remote


def own_window(a, axis, size, chip):
    if axis is None:
        return lax.dynamic_index_in_dim(a, chip, 0, keepdims=False)
    return lax.dynamic_slice_in_dim(a, chip * size, size, axis=axis)


def place_own(land, own, axis, size, chip):
    if axis is None:
        return lax.dynamic_update_slice_in_dim(land, own[None], chip, axis=0)
    return lax.dynamic_update_slice_in_dim(land, own, chip * size, axis=axis)


def exchange_start(pieces, ins, out_shapes, after, name):
    n_in, n_out, ncp = len(ins), len(out_shapes), len(pieces)

    def body(*refs):
        in_refs, land_refs = refs[:n_in], refs[n_in:n_in + n_out]
        send, recv = refs[n_in + n_out + 1], refs[n_in + n_out + 2]
        token = refs[-1]
        for cp in _copies(pieces, in_refs, land_refs, send, recv):
            cp.start()
        token[...] = jnp.zeros_like(token)

    hbm = lambda a: pltpu.with_memory_space_constraint(a, pltpu.HBM)
    lands = [hbm(lax.empty(s.shape, s.dtype)) for s in out_shapes]
    sem = pltpu.SemaphoreType.DMA((3 * ncp,))
    thru = [pltpu.HBM(a.shape, a.dtype) for a in ins] + [pltpu.HBM(s.shape, s.dtype) for s in out_shapes]
    res = _pc(body, in_specs=[HBM_SPEC] * (n_in + n_out) + [ANY_SPEC],
              out_specs=[SEM_SPEC, SEM_SPEC] + [HBM_SPEC] * (n_in + n_out) + [pl.BlockSpec(memory_space=pltpu.VMEM)],
              out_shape=[sem, sem] + thru + [_sds((8, LANES), F32)],
              input_output_aliases={i: 2 + i for i in range(n_in + n_out)},
              compiler_params=pltpu.CompilerParams(has_side_effects=DATAFLOW), name=name)(*[hbm(a) for a in ins], *lands, after)
    return (res[0], res[1]), list(res[2:2 + n_in]), list(res[2 + n_in:2 + n_in + n_out]), res[-1]


def exchange_wait(pieces, sems, ins, lands, after, name):
    n_in, n_out = len(ins), len(lands)

    def body(*refs):
        in_refs, land_refs = refs[:n_in], refs[n_in:n_in + n_out]
        send, recv = refs[n_in + n_out], refs[n_in + n_out + 1]
        for cp in _copies(pieces, in_refs, land_refs, send, recv):
            cp.wait_send()
            cp.wait_recv()

    thru = [pltpu.HBM(a.shape, a.dtype) for a in ins] + [pltpu.HBM(a.shape, a.dtype) for a in lands]
    res = _pc(body, in_specs=[HBM_SPEC] * (n_in + n_out) + [SEM_SPEC, SEM_SPEC, ANY_SPEC], out_specs=[HBM_SPEC] * (n_in + n_out),
              out_shape=thru, input_output_aliases={i: i for i in range(n_in + n_out)},
              compiler_params=pltpu.CompilerParams(has_side_effects=DATAFLOW), name=name)(*ins, *lands, sems[0], sems[1], after)
    return list(res[:n_in]), list(res[n_in:])


def swap_cores(bufs, name):
    nb = len(bufs)

    def body(*refs):
        in_refs, out_refs = refs[:nb], refs[nb:2 * nb]
        send, recv = refs[2 * nb:]
        x, y, c = lax.axis_index("x"), lax.axis_index("y"), lax.axis_index("c")
        copies = [_remote(in_refs[b], out_refs[b], send.at[b], recv.at[b], (x, y, 1 - c)) for b in range(nb)]
        for cp in copies:
            cp.start()
        for cp in copies:
            cp.wait()

    anyspec = pl.BlockSpec(memory_space=pl.ANY)
    return _pc(body, in_specs=[anyspec] * nb, out_specs=[anyspec] * nb, out_shape=[_sds(b.shape, b.dtype) for b in bufs],
               scratch_shapes=[pltpu.SemaphoreType.DMA((nb,)), pltpu.SemaphoreType.DMA((nb,))], name=name)(*bufs)


def exchange_all(buf, name):
    def body(in_ref, out_ref, send, recv, local):
        x, y, c = lax.axis_index("x"), lax.axis_index("y"), lax.axis_index("c")
        mine = 4 * x + 2 * y + c
        loc = pltpu.make_async_copy(in_ref, out_ref.at[mine], local)
        loc.start()
        copies = [loc]
        for k in range(1, 8):
            peer = (x ^ (k >> 2), y ^ ((k >> 1) & 1), c ^ (k & 1))
            cp = pltpu.make_async_remote_copy(src_ref=in_ref, dst_ref=out_ref.at[mine], send_sem=send.at[k - 1],
                                              recv_sem=recv.at[k - 1], device_id=peer, device_id_type=MESH)
            cp.start()
            copies.append(cp)
        for cp in copies:
            cp.wait()

    anyspec = pl.BlockSpec(memory_space=pl.ANY)
    return _pc(body, in_specs=[anyspec], out_specs=anyspec, out_shape=_sds((8,) + buf.shape, buf.dtype),
               scratch_shapes=[pltpu.SemaphoreType.DMA((7,)), pltpu.SemaphoreType.DMA((7,)), pltpu.SemaphoreType.DMA],
               name=name)(buf)


def _norm_fwd(x, g, name):
    return rowwise(f_rms, [(x, D, 0, 0)], [(g, D, 0, 0)], [(D, 0, BF16)], ts=512, name=name)[0]


def _norm_bwd(x, g, dh, dres, name):
    (dx,), (dg,) = rowwise_bwd(f_rms, [(x, D, 0, 0)], [(g, D, 0, 0)], [(dh, D, 0, 0)], need=[True],
                               adds={0: (dres, D, 0, 0)}, ts=256, name=name)
    return dx, dg


def pool_fwd(x, W, tag):
    h = _norm_fwd(x, W["ng"], tag + "_norm")
    proj = mm(h, W["w_in"], name=tag + "_in")
    p = pool_time_fwd(proj, tag + "_win")
    pg = gmm("nn", p, W["w_grp"], G=4, name=tag + "_grp")
    y = rowwise(f_pool_gate, [(pg, POOL_GROUP, 0, 1), (proj, POOL_GROUP, 4, 1)], [(W["scale"], POOL_GROUP, 0, 1)],
                [(POOL_GROUP, 1, BF16)], ncol=4, ts=512, name=tag + "_gate")[0]
    xn = mm(y, W["w_out"], add=x, name=tag + "_out")
    return xn, (x, h, proj, p, pg, y)


def pool_bwd(dxn, W, saved, tag, after=None):
    x, h, proj, p, pg, y = saved
    dy = mm(dxn, W["w_out"], tb=True, after=after, name=tag + "_dy")
    g = {"w_out": mm(y, dxn, ta=True, out_dtype=BF16, name=tag + "_dwout")}
    (dpg, dproj), (g["scale"],) = rowwise_bwd(
        f_pool_gate, [(pg, POOL_GROUP, 0, 1), (proj, POOL_GROUP, 4, 1)], [(W["scale"], POOL_GROUP, 0, 1)],
        [(dy, POOL_GROUP, 0, 1)], need=[True, True], place={1: (2 * POOL_WIDTH, 4)}, narrow=(0, 1), ncol=4, ts=512, name=tag + "_dgate")
    dp = gmm("nt", dpg, W["w_grp"], G=4, name=tag + "_dp")
    g["w_grp"] = gmm("tn", p, dpg, G=4, out_dtype=BF16, name=tag + "_dwgrp")
    dproj = pool_time_bwd(dp, dproj, tag + "_dwin")
    dh = mm(dproj, W["w_in"], tb=True, name=tag + "_dh")
    g["w_in"] = mm(h, dproj, ta=True, out_dtype=BF16, name=tag + "_dw_in")
    dx, g["ng"] = _norm_bwd(x, W["ng"], dh, dxn, tag + "_dnorm")
    return dx, g


def gdn_fwd(x, W, tag):
    h = _norm_fwd(x, W["ng"], tag + "_norm")
    proj = mm(h, W["w_in"], name=tag + "_in")
    qkv = gdn_conv_fwd(proj, W["conv"], tag + "_conv")
    g_b, beta_b = rowwise(f_gdn_gates, [(proj, LANES, 6144 // LANES, 0)], [(W["a_log"], LANES, 0, 0), (W["dt_bias"], LANES, 0, 0)],
                          [(GDN_QK, 0, F32), (GDN_QK, 0, F32)], ts=512, name=tag + "_gates")
    o, states = gdn_chunk_fwd(qkv, g_b, beta_b, tag + "_chunk")
    og = rowwise(f_gdn_out, [(o, GDN_DV, 0, 1), (proj, GDN_DV, 4096 // GDN_DV, 1)], [(W["norm_g"], GDN_DV, 0, 0)],
                 [(GDN_DV, 1, BF16)], ncol=GDN_H, ts=512, name=tag + "_onorm")[0]
    xn = mm(og, W["w_out"], add=x, name=tag + "_out")
    return xn, (x, h, proj, qkv, g_b, beta_b, o, states, og)


def gdn_bwd(dxn, W, saved, tag, after=None):
    x, h, proj, qkv, g_b, beta_b, o, states, og = saved
    dog = mm(dxn, W["w_out"], tb=True, after=after, name=tag + "_dog")
    g = {"w_out": mm(og, dxn, ta=True, out_dtype=BF16, name=tag + "_dwout")}
    (do, dproj), (g["norm_g"],) = rowwise_bwd(
        f_gdn_out, [(o, GDN_DV, 0, 1), (proj, GDN_DV, 4096 // GDN_DV, 1)], [(W["norm_g"], GDN_DV, 0, 0)],
        [(dog, GDN_DV, 0, 1)], need=[True, True], place={1: (GDN_IN_PAD, 4096 // GDN_DV)}, narrow=(1,), ncol=GDN_H, ts=512, name=tag + "_donorm")
    dq, dk, dv, dg_b, dbeta_b = gdn_chunk_bwd(qkv, g_b, beta_b, states, do, tag + "_dchunk")
    (dproj,), (g["a_log"], g["dt_bias"]) = rowwise_bwd(
        f_gdn_gates, [(proj, LANES, 6144 // LANES, 0)], [(W["a_log"], LANES, 0, 0), (W["dt_bias"], LANES, 0, 0)],
        [(dg_b, GDN_QK, 0, 0), (dbeta_b, GDN_QK, 0, 0)], need=[True], place={0: (dproj, 6144 // LANES)}, ts=256, name=tag + "_dgates")
    dproj, g["conv"] = gdn_conv_bwd(proj, W["conv"], dq, dk, dv, dproj, tag + "_dconv")
    dh = mm(dproj, W["w_in"], tb=True, name=tag + "_dh")
    g["w_in"] = mm(h, dproj, ta=True, out_dtype=BF16, name=tag + "_dw_in")
    dx, g["ng"] = _norm_bwd(x, W["ng"], dh, dxn, tag + "_dnorm")
    return dx, g


def mla_fwd(x, pos, W, tag):
    h = _norm_fwd(x, W["ng"], tag + "_norm")
    proj = mm(h, W["w_in"], name=tag + "_in")
    hq = rowwise(f_rms, [(proj, MLA_Q_LORA, 0, 0)], [(W["q_g"], MLA_Q_LORA, 0, 0)], [(MLA_Q_LORA, 0, BF16)], ts=512, name=tag + "_qnorm")[0]
    hkv = rowwise(f_rms, [(proj, MLA_KV_LORA, 2, 0)], [(W["kv_g"], MLA_KV_LORA, 0, 0)], [(MLA_KV_LORA, 0, BF16)], ts=512, name=tag + "_kvnorm")[0]
    qpad = mm(hq, W["w_uq"], name=tag + "_uq")
    kv = mm(hkv, W["w_ukv"], name=tag + "_ukv")
    qh, kh, vh = mla_prep_fwd(qpad, kv, proj, pos, W["rope"], tag + "_prep")
    o, lse = flash_fwd(qh, kh, vh, tag + "_attn")
    og = rowwise(f_ogate, [(o, 512, 0, 1), (proj, 512, 4, 1)], [], [(512, 1, BF16)], ncol=4, ts=512, name=tag + "_ogate")[0]
    xn = mm(og, W["w_out"], add=x, name=tag + "_out")
    return xn, (x, h, proj, hq, hkv, qh, kh, vh, o, lse, og)


def mla_bwd(dxn, pos, W, saved, tag, after=None):
    x, h, proj, hq, hkv, qh, kh, vh, o, lse, og = saved
    dog = mm(dxn, W["w_out"], tb=True, after=after, name=tag + "_dog")
    g = {"w_out": mm(og, dxn, ta=True, out_dtype=BF16, name=tag + "_dwout")}
    dproj = jnp.zeros(proj.shape, BF16)
    (do, dproj), _ = rowwise_bwd(f_ogate, [(o, 512, 0, 1), (proj, 512, 4, 1)], [], [(dog, 512, 0, 1)], need=[True, True],
                                 place={1: (dproj, 4)}, ncol=4, ts=512, name=tag + "_dogate")
    dqh, dkh, dvh = flash_bwd(qh, kh, vh, o, lse, do, tag + "_dattn")
    dqpad, dkv, dproj = mla_prep_bwd(dqh, dkh, dvh, pos, W["rope"], dproj, tag + "_dprep")
    dhq = mm(dqpad, W["w_uq"], tb=True, name=tag + "_dhq")
    g["w_uq"] = mm(hq, dqpad, ta=True, out_dtype=BF16, name=tag + "_dwuq")
    dhkv = mm(dkv, W["w_ukv"], tb=True, name=tag + "_dhkv")
    g["w_ukv"] = mm(hkv, dkv, ta=True, out_dtype=BF16, name=tag + "_dwukv")
    (dproj,), (g["q_g"],) = rowwise_bwd(f_rms, [(proj, MLA_Q_LORA, 0, 0)], [(W["q_g"], MLA_Q_LORA, 0, 0)], [(dhq, MLA_Q_LORA, 0, 0)],
                                        need=[True], place={0: (dproj, 0)}, ts=256, name=tag + "_dqnorm")
    (dproj,), (g["kv_g"],) = rowwise_bwd(f_rms, [(proj, MLA_KV_LORA, 2, 0)], [(W["kv_g"], MLA_KV_LORA, 0, 0)], [(dhkv, MLA_KV_LORA, 0, 0)],
                                         need=[True], place={0: (dproj, 2)}, ts=256, name=tag + "_dkvnorm")
    dh = mm(dproj, W["w_in"], tb=True, name=tag + "_dh")
    g["w_in"] = mm(h, dproj, ta=True, out_dtype=BF16, name=tag + "_dw_in")
    dx, g["ng"] = _norm_bwd(x, W["ng"], dh, dxn, tag + "_dnorm")
    return dx, g


def _pad_cols(a, n):
    return jnp.pad(a, ((0, 0), (0, n - a.shape[1])))


def _mla_w_in_layout(w):
    z = lambda n: jnp.zeros((w.shape[0], n), w.dtype)
    kr = w[:, 1280:1344]
    return jnp.concatenate([w[:, :768], z(256), w[:, 768:1280], kr[:, :32], z(32), kr[:, 32:], z(32), z(384), w[:, 1344:]], axis=1)


def _mla_w_in_unlayout(g):
    return jnp.concatenate([g[:, :768], g[:, 1024:1536], g[:, 1536:1568], g[:, 1600:1632], g[:, 2048:]], axis=1)


def _mla_w_uq_layout(w):
    w3 = w.reshape(w.shape[0], MLA_H, MLA_NOPE + MLA_ROPE)
    z = jnp.zeros((w.shape[0], MLA_H, 32), w.dtype)
    return jnp.concatenate([w3[..., :128], w3[..., 128:160], z, w3[..., 160:192], z], axis=-1).reshape(w.shape[0], MLA_H * 256)


def _mla_w_uq_unlayout(g):
    g3 = g.reshape(g.shape[0], MLA_H, 256)
    return jnp.concatenate([g3[..., :128], g3[..., 128:160], g3[..., 192:224]], axis=-1).reshape(g.shape[0], MLA_H * 192)


def _rope_consts():
    half = MLA_ROPE // 2
    inv = ROPE_THETA ** (-jnp.arange(half, dtype=F32) / half)
    z = jnp.zeros((half,), F32)
    o = jnp.ones((half,), F32)
    row = lambda *p: jnp.concatenate(p).reshape(1, LANES)
    return row(inv, z, inv, z), row(o, z, o, z), row(-o, z, o, z)


BIG = ["pool_w_in", "pool_w_grp", "pool_w_out", "gdn_w_in", "gdn_w_out", "mla_w_in", "mla_w_uq", "mla_w_ukv", "mla_w_out"]
BIG_LAYOUT = {"pool_w_in": (1, 1024, (1024, 4096)), "pool_w_grp": (1, 128, (4, 512, 512)), "pool_w_out": (0, 512, (2048, 1024)),
              "gdn_w_in": (None, None, (4, 1024, 1540)), "gdn_w_out": (0, 512, (2048, 1024)),
              "mla_w_in": (None, None, (4, 1024, 848)), "mla_w_uq": (1, 768, (768, 3072)), "mla_w_ukv": (1, 1024, (512, 4096)),
              "mla_w_out": (0, 512, (2048, 1024))}
SMALL_SHARDED = ["pool_scale", "gdn_conv", "mla_q_norm_g", "mla_kv_norm_g"]
SMALL_AXIS = {"pool_scale": 1, "gdn_conv": 2, "mla_q_norm_g": 1, "mla_kv_norm_g": 1}
REPLICATED = ["norm_g", "gdn_a_log", "gdn_dt_bias", "gdn_norm_g", "final_g"]
PACK_C = 1024


def _pack(parts, dtype, row_mult):
    flat = jnp.concatenate([p.reshape(-1).astype(dtype) for p in parts])
    rows = -(-flat.shape[0] // PACK_C)
    rows = -(-rows // row_mult) * row_mult
    return jnp.pad(flat, (0, rows * PACK_C - flat.shape[0])).reshape(rows, PACK_C)


def _unpack(buf, shapes):
    lead = buf.shape[:-2]
    flat = buf.reshape(lead + (-1,))
    out, off = [], 0
    for s in shapes:
        n = int(np.prod(s))
        out.append(flat[..., off:off + n].reshape(lead + tuple(s)))
        off += n
    return out


def _unshard(g4, axis):
    a = jnp.moveaxis(g4, 0, axis)
    s = a.shape
    return a.reshape(s[:axis] + (s[axis] * s[axis + 1],) + s[axis + 2:])


def _to_shards(a, axis):
    s = a.shape
    return jnp.moveaxis(a.reshape(s[:axis] + (4, s[axis] // 4) + s[axis + 1:]), axis, 0)


def layer_weights(full, small, rep, layer):
    ng = rep["norm_g"][layer:layer + 1]
    side_by_side = lambda a4: jnp.moveaxis(a4, 0, 1).reshape(a4.shape[1], 4 * a4.shape[2])
    if layer in (0, 3):
        j = layer // 3
        return dict(ng=ng, w_in=full[("pool_w_in", j)], w_grp=full[("pool_w_grp", j)], scale=small["pool_scale"][j:j + 1],
                    w_out=full[("pool_w_out", j)])
    if layer == 1:
        return dict(ng=ng, w_in=_pad_cols(side_by_side(full[("gdn_w_in", 0)]), GDN_IN_PAD),
                    conv=jnp.pad(small["gdn_conv"][0], ((0, 4), (0, 0))), a_log=_pad_cols(rep["gdn_a_log"], LANES),
                    dt_bias=_pad_cols(rep["gdn_dt_bias"], LANES), norm_g=rep["gdn_norm_g"], w_out=full[("gdn_w_out", 0)])
    return dict(ng=ng, w_in=_mla_w_in_layout(side_by_side(full[("mla_w_in", 0)])), q_g=small["mla_q_norm_g"],
                kv_g=small["mla_kv_norm_g"], w_uq=_mla_w_uq_layout(full[("mla_w_uq", 0)]), w_ukv=full[("mla_w_ukv", 0)],
                w_out=full[("mla_w_out", 0)], rope=_rope_consts())


def big_grad_pieces(gl):
    g0, g1, g2, g3 = gl
    slots = lambda a: jnp.moveaxis(a.reshape(a.shape[0], 4, a.shape[1] // 4), 1, 0)
    out = {}
    for l, g in ((0, g0), (1, g3)):
        if g is not None:
            out.update({("pool_w_in", l): g["w_in"], ("pool_w_grp", l): g["w_grp"], ("pool_w_out", l): g["w_out"]})
    if g1 is not None:
        out.update({("gdn_w_in", 0): slots(g1["w_in"][:, :GDN_IN]), ("gdn_w_out", 0): g1["w_out"]})
    if g2 is not None:
        out.update({("mla_w_in", 0): slots(_mla_w_in_unlayout(g2["w_in"])), ("mla_w_uq", 0): _mla_w_uq_unlayout(g2["w_uq"]),
                    ("mla_w_ukv", 0): g2["w_ukv"], ("mla_w_out", 0): g2["w_out"]})
    return out


def small_grads(gl, dfinal):
    g0, g1, g2, g3 = gl
    return {"norm_g": jnp.concatenate([g0["ng"], g1["ng"], g2["ng"], g3["ng"]], axis=0),
            "pool_scale": jnp.concatenate([g0["scale"], g3["scale"]], axis=0), "gdn_conv": g1["conv"][None, :4],
            "gdn_a_log": g1["a_log"][:, :GDN_H], "gdn_dt_bias": g1["dt_bias"][:, :GDN_H], "gdn_norm_g": g1["norm_g"],
            "mla_q_norm_g": g2["q_g"], "mla_kv_norm_g": g2["kv_g"], "final_g": dfinal.reshape(D)}


NAMES = ["norm_g", "pool_w_in", "pool_w_grp", "pool_scale", "pool_w_out", "gdn_w_in", "gdn_conv", "gdn_a_log", "gdn_dt_bias",
         "gdn_norm_g", "gdn_w_out", "mla_w_in", "mla_q_norm_g", "mla_w_uq", "mla_kv_norm_g", "mla_w_ukv", "mla_w_out", "final_g"]


def kernel(x, positions, norm_g, pool_w_in, pool_w_grp, pool_scale, pool_w_out, gdn_w_in, gdn_conv, gdn_a_log, gdn_dt_bias, gdn_norm_g, gdn_w_out, mla_w_in, mla_q_norm_g, mla_w_uq, mla_kv_norm_g, mla_w_ukv, mla_w_out, final_g, loss_target, m_norm_g, m_pool_w_in, m_pool_w_grp, m_pool_scale, m_pool_w_out, m_gdn_w_in, m_gdn_conv, m_gdn_a_log, m_gdn_dt_bias, m_gdn_norm_g, m_gdn_w_out, m_mla_w_in, m_mla_q_norm_g, m_mla_w_uq, m_mla_kv_norm_g, m_mla_w_ukv, m_mla_w_out, m_final_g, v_norm_g, v_pool_w_in, v_pool_w_grp, v_pool_scale, v_pool_w_out, v_gdn_w_in, v_gdn_conv, v_gdn_a_log, v_gdn_dt_bias, v_gdn_norm_g, v_gdn_w_out, v_mla_w_in, v_mla_q_norm_g, v_mla_w_uq, v_mla_kv_norm_g, v_mla_w_ukv, v_mla_w_out, v_final_g):
    args = locals()
    w = {n: args[n] for n in NAMES}
    m = {n: args["m_" + n] for n in NAMES}
    v = {n: args["v_" + n] for n in NAMES}
    my_chip = (2 * lax.axis_index("x") + lax.axis_index("y")).astype(I32)
    S_ = x.shape[1]
    x0, pos, target = x[0], positions.reshape(S_, 1).astype(F32), loss_target[0]
    rep = {n: w[n] for n in REPLICATED}

    shard = {(n, l): w[n][l:l + 1].astype(BF16) for n in BIG for l in range(w[n].shape[0])}
    small_shapes = [w[n].shape for n in SMALL_SHARDED]
    shard[("small", 0)] = _pack([w[n] for n in SMALL_SHARDED], F32, 8)[None]
    layout = dict(BIG_LAYOUT, small=(None, None, (4,) + shard[("small", 0)].shape[1:]))

    def gather_start(group, after, tag):
        pieces = [gather_piece(i, 0, i, layout[n][0], layout[n][1]) for i, (n, l) in enumerate(group)]
        shapes = [_sds(layout[n][2], shard[(n, l)].dtype) for n, l in group]
        sems, ins, lands, token = exchange_start(pieces, [shard[k] for k in group], shapes, after, tag + "_start")
        return (pieces, sems, ins, lands), token

    def finish(handle, after, tag):
        return exchange_wait(*handle, after, tag + "_wait")

    def gathered(group, handle, after, tag):
        srcs, lands = finish(handle, after, tag)
        return {(n, l): place_own(a, s[0], layout[n][0], layout[n][1], my_chip) for (n, l), s, a in zip(group, srcs, lands)}

    tied = lambda a, token: a + token[0:1, 0:1]
    group_a = [("pool_w_in", 0), ("pool_w_grp", 0), ("pool_w_out", 0), ("small", 0)]
    group_b = [("gdn_w_in", 0), ("gdn_w_out", 0)]
    group_c = [("mla_w_in", 0), ("mla_w_uq", 0), ("mla_w_ukv", 0), ("mla_w_out", 0), ("pool_w_in", 1), ("pool_w_grp", 1), ("pool_w_out", 1)]
    full = {}
    h_a, t_a = gather_start(group_a, x0, "gather_a")
    full.update(gathered(group_a, h_a, t_a, "gather_a"))
    small = {n: _unshard(a, SMALL_AXIS[n]) for n, a in zip(SMALL_SHARDED, _unpack(full[("small", 0)], small_shapes))}
    h_b, t_b = gather_start(group_b, full[group_a[0]], "gather_b")
    W0 = layer_weights(full, small, rep, 0)
    x1, s0 = pool_fwd(x0, dict(W0, ng=tied(W0["ng"], t_b)), "l0")
    full.update(gathered(group_b, h_b, x1, "gather_b"))
    h_c, t_c = gather_start(group_c, full[group_b[0]], "gather_c")
    W1 = layer_weights(full, small, rep, 1)
    x2, s1 = gdn_fwd(x1, dict(W1, ng=tied(W1["ng"], t_c)), "l1")
    full.update(gathered(group_c, h_c, x2, "gather_c"))
    W2, W3 = layer_weights(full, small, rep, 2), layer_weights(full, small, rep, 3)
    x3, s2 = mla_fwd(x2, pos, W2, "l2")
    x4, s3 = pool_fwd(x3, W3, "l3")
    loss_part, dx4, dfinal = loss_head(x4, target, final_g.reshape(1, D), "loss_head")

    def scatter_start(pieces_of, after, tag):
        keys = list(pieces_of)
        pieces = [scatter_piece(i, i, BIG_LAYOUT[n][0], BIG_LAYOUT[n][1]) for i, (n, l) in enumerate(keys)]
        shapes = [_sds((4,) + tuple(w[n].shape[1:]), BF16) for n, l in keys]
        sems, ins, lands, token = exchange_start(pieces, [pieces_of[k] for k in keys], shapes, after, tag + "_start")
        return keys, (pieces, sems, ins, lands), token

    def scattered(keys, handle, after, tag):
        srcs, lands = finish(handle, after, tag)
        return {(n, l): place_own(a, own_window(g, BIG_LAYOUT[n][0], BIG_LAYOUT[n][1], my_chip), None, None, my_chip)
                for (n, l), g, a in zip(keys, srcs, lands)}

    dx3, g3 = pool_bwd(dx4, W3, s3, "l3")
    k3, h3, t3 = scatter_start(big_grad_pieces((None, None, None, g3)), dx3, "scatter_l3")
    dx2, g2 = mla_bwd(dx3, pos, W2, s2, "l2", after=t3)
    k2, h2, t2 = scatter_start(big_grad_pieces((None, None, g2, None)), dx2, "scatter_l2")
    dx1, g1 = gdn_bwd(dx2, W1, s1, "l1", after=t2)
    k1, h1, t1 = scatter_start(big_grad_pieces((None, g1, None, None)), dx1, "scatter_l1")
    dx0, g0 = pool_bwd(dx1, W0, s0, "l0", after=t1)
    k0, h0, t0 = scatter_start(big_grad_pieces((g0, None, None, None)), dx0, "scatter_l0")
    recv = {}
    for keys, handle, tag in ((k3, h3, "scatter_l3"), (k2, h2, "scatter_l2"), (k1, h1, "scatter_l1"), (k0, h0, "scatter_l0")):
        recv.update(scattered(keys, handle, t0, tag))
    keys = list(recv)
    sib = dict(zip(keys, swap_cores([recv[k] for k in keys], "swap_cores")))

    sg = small_grads((g0, g1, g2, g3), dfinal)
    small_names = SMALL_SHARDED + REPLICATED
    small_buf = _pack([sg[n] for n in small_names] + [loss_part], F32, 8)
    small_sum = sum_slots(None, exchange_all(small_buf, "gather_small"), jnp.full((1,), -1, I32), "sum_small")
    full_small = _unpack(small_sum, [sg[n].shape for n in small_names] + [(1, LANES)])
    loss = full_small[-1][0, 0]
    small_part = {}
    for n, a in zip(small_names, full_small[:-1]):
        if n in SMALL_AXIS:
            a = lax.dynamic_index_in_dim(_to_shards(a, SMALL_AXIS[n]), my_chip, axis=0, keepdims=False)
        small_part[n] = a

    outs = []
    for n in NAMES:
        shp = w[n].shape
        two = (int(np.prod(shp[:-1])), shp[-1]) if len(shp) > 1 else (1, shp[0])
        if n in BIG_LAYOUT:
            layers = shp[0]
            rows = lambda a: a.reshape(4, two[0] // layers, two[1])
            parts = [[rows(recv[(n, l)]) for l in range(layers)], [rows(sib[(n, l)]) for l in range(layers)]]
        else:
            parts = [[small_part[n].reshape((1,) + two)]]
        res = adamw(w[n].reshape(two), parts, m[n].reshape(two), v[n].reshape(two), "adamw_" + n)
        outs.append([r.reshape(shp) for r in res])
    return (loss, dx0[None], *[o[0] for o in outs], *[o[1] for o in outs], *[o[2] for o in outs], *[o[3] for o in outs])
```

```python
import functools
import math

import jax
import jax.numpy as jnp
import numpy as np
from jax import lax
from jax.experimental import pallas as pl
from jax.experimental.pallas import tpu as pltpu

F32 = jnp.float32
BF16 = jnp.bfloat16
I32 = jnp.int32

D = 1024
EPS = 1e-6
POOL_WIDTH = 2048
POOL_GROUP = 512
GDN_H, GDN_DK, GDN_DV, GDN_C = 8, 128, 256, 64
GDN_QK, GDN_V, GDN_CONV_CH, GDN_IN = 1024, 2048, 4096, 6160
GDN_IN_PAD = 6272
MLA_H, MLA_NOPE, MLA_ROPE, MLA_V = 16, 128, 64, 128
MLA_Q_LORA, MLA_KV_LORA, MLA_WIDTH, MLA_IN = 768, 512, 2048, 3392
MLA_IN_PAD = 4096
MLA_SCALE = (MLA_NOPE + MLA_ROPE) ** -0.5
ROPE_THETA = 10000.0
ADAM_LR, ADAM_B1, ADAM_B2, ADAM_EPS, ADAM_WD, ADAM_STEP = 0.001, 0.9, 0.999, 1e-08, 0.01, 10

VMEM_LIMIT_V7X = 56 * 1024 * 1024
LANES = 128
MESH = pl.DeviceIdType.MESH


def _pc(body, **kw):
    return pl.pallas_call(body, **kw)


def _cparams(sem):
    return pltpu.CompilerParams(dimension_semantics=sem, vmem_limit_bytes=VMEM_LIMIT_V7X)


def _tile(n, cap):
    t = (cap // LANES) * LANES
    while t >= LANES:
        if n % t == 0:
            return t
        t -= LANES
    return n


def _sds(shape, dt):
    return jax.ShapeDtypeStruct(shape, dt)


def mm(a, b, *, ta=False, tb=False, add=None, after=None, out_dtype=F32, name):
    if ta:
        K, M = a.shape
    else:
        M, K = a.shape
    if tb:
        N, K2 = b.shape
    else:
        K2, N = b.shape
    assert K == K2, (a.shape, b.shape, ta, tb)
    tm, tn, tk = _tile(M, 1024), _tile(N, 1024), _tile(K, 1024)
    nk = K // tk
    a_spec = pl.BlockSpec((tk, tm), lambda i, j, k: (k, i)) if ta else pl.BlockSpec((tm, tk), lambda i, j, k: (i, k))
    b_spec = pl.BlockSpec((tn, tk), lambda i, j, k: (j, k)) if tb else pl.BlockSpec((tk, tn), lambda i, j, k: (k, j))
    o_spec = pl.BlockSpec((tm, tn), lambda i, j, k: (i, j))
    dn = (((0 if ta else 1,), (1 if tb else 0,)), ((), ()))
    has_add = add is not None

    def body(*refs):
        a_ref, b_ref = refs[0], refs[1]
        part = lax.dot_general(a_ref[...].astype(BF16), b_ref[...].astype(BF16), dn, preferred_element_type=F32)
        if nk == 1:
            refs[-1][...] = (part + refs[2][...] if has_add else part).astype(out_dtype)
            return
        o_ref, acc = refs[-2], refs[-1]
        k = pl.program_id(2)

        @pl.when(k == 0)
        def _():
            acc[...] = part

        @pl.when(k > 0)
        def _():
            acc[...] += part

        @pl.when(k == nk - 1)
        def _():
            r = acc[...]
            if has_add:
                r = r + refs[2][...]
            o_ref[...] = r.astype(out_dtype)

    ins = [a, b] + ([add] if has_add else []) + ([after] if after is not None else [])
    specs = [a_spec, b_spec] + ([o_spec] if has_add else []) + ([pl.BlockSpec(memory_space=pl.ANY)] if after is not None else [])
    return _pc(body, grid=(M // tm, N // tn, nk), in_specs=specs, out_specs=o_spec, out_shape=_sds((M, N), out_dtype),
               scratch_shapes=[pltpu.VMEM((tm, tn), F32)] if nk > 1 else [], compiler_params=_cparams(("parallel", "parallel", "arbitrary")),
               name=name)(*ins)


def gmm(kind, a, b, *, G, name, out_dtype=F32):
    S_ = a.shape[0]
    Ka = a.shape[1] // G
    if kind == "tn":
        N = b.shape[1] // G
        tk = _tile(S_, 512)
        nk = S_ // tk

        def body(a_ref, b_ref, o_ref, acc):
            k = pl.program_id(1)

            @pl.when(k == 0)
            def _():
                acc[...] = jnp.zeros_like(acc)

            acc[...] += lax.dot_general(a_ref[...].astype(BF16), b_ref[...].astype(BF16), (((0,), (0,)), ((), ())),
                                        preferred_element_type=F32)

            @pl.when(k == nk - 1)
            def _():
                o_ref[...] = acc[...].astype(out_dtype)

        return _pc(body, grid=(G, nk),
                   in_specs=[pl.BlockSpec((tk, Ka), lambda g, k: (k, g)), pl.BlockSpec((tk, N), lambda g, k: (k, g))],
                   out_specs=pl.BlockSpec((None, Ka, N), lambda g, k: (g, 0, 0)), out_shape=_sds((G, Ka, N), out_dtype),
                   scratch_shapes=[pltpu.VMEM((Ka, N), F32)], compiler_params=_cparams(("parallel", "arbitrary")), name=name)(a, b)
    N = b.shape[2] if kind == "nn" else b.shape[1]
    tm = _tile(S_, 1024)
    dn = (((1,), (0 if kind == "nn" else 1,)), ((), ()))

    def body(a_ref, b_ref, o_ref):
        o_ref[...] = lax.dot_general(a_ref[...].astype(BF16), b_ref[...].astype(BF16), dn, preferred_element_type=F32)

    bshape = (None,) + tuple(b.shape[1:])
    return _pc(body, grid=(G, S_ // tm),
               in_specs=[pl.BlockSpec((tm, Ka), lambda g, i: (i, g)), pl.BlockSpec(bshape, lambda g, i: (g, 0, 0))],
               out_specs=pl.BlockSpec((tm, N), lambda g, i: (i, g)), out_shape=_sds((S_, G * N), F32),
               compiler_params=_cparams(("parallel", "parallel")), name=name)(a, b)


def _rw_spec(ts, w, c, s):
    return pl.BlockSpec((ts, w), lambda j, i: (i, c + j * s))


def _rw_pspec(p, w, c, s):
    return pl.BlockSpec((p.shape[0], w), lambda j, i: (0, c + j * s))


def rowwise(f, tiles, params, outs, *, ncol=1, ts, name):
    S_ = tiles[0][0].shape[0]
    nin = len(tiles) + len(params)

    def body(*refs):
        res = f(pl.program_id(0), *[r[...] for r in refs[:nin]])
        for r, o in zip(refs[nin:], res):
            r[...] = o.astype(r.dtype)

    return _pc(body, grid=(ncol, S_ // ts),
               in_specs=[_rw_spec(ts, w, c, s) for (_, w, c, s) in tiles] + [_rw_pspec(*p) for p in params],
               out_specs=[_rw_spec(ts, w, 0, s) for (w, s, _) in outs],
               out_shape=[_sds((S_, w * (ncol if s else 1)), dt) for (w, s, dt) in outs],
               compiler_params=_cparams(("parallel", "parallel")), name=name)(*[t[0] for t in tiles], *[p[0] for p in params])


def rowwise_bwd(f, tiles, params, cots, *, need, adds=None, place=None, narrow=(), ncol=1, ts, name):
    S_ = tiles[0][0].shape[0]
    adds = adds or {}
    place = place or {}
    nt, npar, nc = len(tiles), len(params), len(cots)
    add_keys = sorted(adds)
    need_idx = [k for k in range(nt) if need[k]]
    into_keys = [k for k in need_idx if k in place and not isinstance(place[k][0], int)]
    n_extra = len(add_keys) + len(into_keys)

    def body(*refs):
        j, i = pl.program_id(0), pl.program_id(1)
        vals = [r[...] for r in refs[:nt + npar]]
        cvals = tuple(r[...] for r in refs[nt + npar:nt + npar + nc])
        add_refs = refs[nt + npar + nc:nt + npar + nc + len(add_keys)]
        out_refs = refs[nt + npar + nc + n_extra:]
        _, vjp = jax.vjp(lambda *v: tuple(f(j, *v)), *vals)
        grads = vjp(cvals)
        for n, k in enumerate(need_idx):
            g = grads[k]
            if k in adds:
                g = g + add_refs[add_keys.index(k)][...]
            out_refs[n][...] = g.astype(out_refs[n].dtype)
        for n in range(npar):
            ref = out_refs[len(need_idx) + n]
            first = (i == 0) if params[n][3] else jnp.logical_and(i == 0, j == 0)

            @pl.when(first)
            def _():
                ref[...] = jnp.zeros_like(ref)

            ref[...] += grads[nt + n]

    in_specs = ([_rw_spec(ts, w, c, s) for (_, w, c, s) in tiles] + [_rw_pspec(*p) for p in params]
                + [_rw_spec(ts, w, c, s) for (_, w, c, s) in cots] + [_rw_spec(ts, *adds[k][1:]) for k in add_keys]
                + [pl.BlockSpec(memory_space=pl.ANY) for _ in into_keys])
    out_specs, out_shape, aliases = [], [], {}
    for n, k in enumerate(need_idx):
        w, s = tiles[k][1], tiles[k][3]
        if k in place:
            dst, c0 = place[k]
            total = dst if isinstance(dst, int) else dst.shape[1]
            out_specs.append(_rw_spec(ts, w, c0, s))
            out_shape.append(_sds((S_, total), (BF16 if k in narrow else F32) if isinstance(dst, int) else dst.dtype))
            if k in into_keys:
                aliases[nt + npar + nc + len(add_keys) + into_keys.index(k)] = n
        else:
            out_specs.append(_rw_spec(ts, w, 0, s))
            out_shape.append(_sds((S_, w * (ncol if s else 1)), BF16 if k in narrow else F32))
    out_specs += [_rw_pspec(p[0], p[1], p[2], p[3]) for p in params]
    out_shape += [_sds(p[0].shape, F32) for p in params]
    res = _pc(body, grid=(ncol, S_ // ts), in_specs=in_specs, out_specs=out_specs, out_shape=out_shape,
              input_output_aliases=aliases, compiler_params=_cparams(("arbitrary", "arbitrary")), name=name)(
        *[t[0] for t in tiles], *[p[0] for p in params], *[c[0] for c in cots], *[adds[k][0] for k in add_keys],
        *[place[k][0] for k in into_keys])
    return list(res[:len(need_idx)]), list(res[len(need_idx):])


def _rms(x, g):
    r = lax.rsqrt(jnp.mean(x * x, axis=-1, keepdims=True) + EPS)
    return x * r * g


def _silu(x):
    return x * jax.nn.sigmoid(x)


@jax.custom_vjp
def _softplus(x):
    return jnp.maximum(x, 0.0) + jnp.log1p(jnp.exp(-jnp.abs(x)))


_softplus.defvjp(lambda x: (_softplus(x), x), lambda x, d: (d * jax.nn.sigmoid(x),))


def f_rms(j, x, g):
    return (_rms(x, g),)


def f_pool_gate(j, pg, gate, scale):
    return (pg * scale * _silu(gate),)


def f_ogate(j, o, gate):
    return (o * _silu(gate),)


def f_gdn_out(j, o, gate, g):
    return (_rms(o, g) * _silu(gate),)


def f_gdn_gates(j, ba, alog, dtb):
    lane = lax.broadcasted_iota(I32, (1, LANES), 1)
    gs, bs = [], []
    for h in range(GDN_H):
        eb = (lane == h).astype(F32)
        ea = (lane == GDN_H + h).astype(F32)
        b = jnp.sum(ba * eb, -1, keepdims=True)
        a = jnp.sum(ba * ea, -1, keepdims=True)
        al = jnp.sum(alog * eb, -1, keepdims=True)
        dt = jnp.sum(dtb * eb, -1, keepdims=True)
        g = -jnp.exp(al) * _softplus(a + dt)
        gs.append(jnp.broadcast_to(g, ba.shape))
        bs.append(jnp.broadcast_to(jax.nn.sigmoid(b), ba.shape))
    return jnp.concatenate(gs, 1), jnp.concatenate(bs, 1)


def _shift_dn(x, k):
    rows = lax.broadcasted_iota(I32, x.shape, 0)
    return jnp.where(rows < k, 0.0, pltpu.roll(x, k, 0))


def _shift_up(x, k):
    n = x.shape[0]
    rows = lax.broadcasted_iota(I32, x.shape, 0)
    return jnp.where(rows >= n - k, 0.0, pltpu.roll(x, n - k, 0))


def _pool_window(j):
    g = lax.div(j, POOL_GROUP // LANES)
    return jnp.where(g == 0, 2.0, jnp.where(g == 1, 4.0, jnp.where(g == 2, 8.0, 16.0))), g


def _pick(g, a2, a4, a8, a16):
    return jnp.where(g == 0, a2, jnp.where(g == 1, a4, jnp.where(g == 2, a8, a16)))


def pool_time_fwd(proj, name):
    S_ = proj.shape[0]

    def body(u_ref, p_ref):
        u = u_ref[...]
        w, g = _pool_window(pl.program_id(0))
        s2 = u + _shift_dn(u, 1)
        s4 = s2 + _shift_dn(s2, 2)
        s8 = s4 + _shift_dn(s4, 4)
        s16 = s8 + _shift_dn(s8, 8)
        t1 = (lax.broadcasted_iota(I32, u.shape, 0) + 1).astype(F32)
        p_ref[...] = (_pick(g, s2, s4, s8, s16) / jnp.minimum(t1, w) - u).astype(p_ref.dtype)

    return _pc(body, grid=(POOL_WIDTH // LANES,), in_specs=[pl.BlockSpec((S_, LANES), lambda j: (0, j))],
               out_specs=pl.BlockSpec((S_, LANES), lambda j: (0, j)), out_shape=_sds((S_, POOL_WIDTH), BF16),
               compiler_params=_cparams(("parallel",)), name=name)(proj)


def pool_time_bwd(dp, into, name):
    S_ = dp.shape[0]

    def body(dp_ref, _, du_ref):
        d = dp_ref[...]
        w, g = _pool_window(pl.program_id(0))
        t1 = (lax.broadcasted_iota(I32, d.shape, 0) + 1).astype(F32)
        q = d / jnp.minimum(t1, w)
        r2 = q + _shift_up(q, 1)
        r4 = r2 + _shift_up(r2, 2)
        r8 = r4 + _shift_up(r4, 4)
        r16 = r8 + _shift_up(r8, 8)
        du_ref[...] = (_pick(g, r2, r4, r8, r16) - d).astype(du_ref.dtype)

    return _pc(body, grid=(POOL_WIDTH // LANES,),
               in_specs=[pl.BlockSpec((S_, LANES), lambda j: (0, j)), pl.BlockSpec(memory_space=pl.ANY)],
               out_specs=pl.BlockSpec((S_, LANES), lambda j: (0, j)), out_shape=_sds(into.shape, into.dtype),
               input_output_aliases={1: 0}, compiler_params=_cparams(("parallel",)), name=name)(dp, into)


def _conv_post(j, a):
    n = a * lax.rsqrt(jnp.sum(a * a, axis=-1, keepdims=True) + EPS)
    nq = GDN_QK // LANES
    return jnp.where(j < nq, n * (GDN_DK ** -0.5), jnp.where(j < 2 * nq, n, a))


def _conv_pre(u, w):
    return w[3:4] * u + w[2:3] * _shift_dn(u, 1) + w[1:2] * _shift_dn(u, 2) + w[0:1] * _shift_dn(u, 3)


def gdn_conv_fwd(proj, conv_w, name):
    S_ = proj.shape[0]

    def body(u_ref, w_ref, o_ref):
        o_ref[...] = _conv_post(pl.program_id(0), _silu(_conv_pre(u_ref[...], w_ref[...])))

    return _pc(body, grid=(GDN_CONV_CH // LANES,),
               in_specs=[pl.BlockSpec((S_, LANES), lambda j: (0, j)), pl.BlockSpec((8, LANES), lambda j: (0, j))],
               out_specs=pl.BlockSpec((S_, LANES), lambda j: (0, j)), out_shape=_sds((S_, GDN_CONV_CH), F32),
               compiler_params=_cparams(("parallel",)), name=name)(proj, conv_w)


def gdn_conv_bwd(proj, conv_w, dq, dk, dv, into, name):
    S_ = proj.shape[0]
    nq = GDN_QK // LANES

    def body(u_ref, w_ref, dq_ref, dk_ref, dv_ref, _, du_ref, dw_ref):
        j = pl.program_id(0)
        u, w = u_ref[...], w_ref[...]
        c = _conv_pre(u, w)
        sig = jax.nn.sigmoid(c)
        dout = jnp.where(j < nq, dq_ref[...], jnp.where(j < 2 * nq, dk_ref[...], dv_ref[...]))
        _, vjp = jax.vjp(lambda a: _conv_post(j, a), c * sig)
        dc = vjp(dout)[0] * (sig * (1.0 + c * (1.0 - sig)))
        du = w[3:4] * dc + w[2:3] * _shift_up(dc, 1) + w[1:2] * _shift_up(dc, 2) + w[0:1] * _shift_up(dc, 3)
        du_ref[...] = du.astype(du_ref.dtype)
        rows = lax.broadcasted_iota(I32, (8, LANES), 0)
        dw = jnp.zeros((8, LANES), F32)
        for k in range(4):
            us = u if k == 3 else _shift_dn(u, 3 - k)
            dw = dw + jnp.where(rows == k, jnp.sum(dc * us, axis=0, keepdims=True), 0.0)
        dw_ref[...] = dw

    blk = lambda f: pl.BlockSpec((S_, LANES), f)
    return _pc(body, grid=(GDN_CONV_CH // LANES,),
               in_specs=[blk(lambda j: (0, j)), pl.BlockSpec((8, LANES), lambda j: (0, j)),
                         blk(lambda j: (0, jnp.minimum(j, nq - 1))), blk(lambda j: (0, jnp.clip(j - nq, 0, nq - 1))),
                         blk(lambda j: (0, jnp.clip(j - 2 * nq, 0, 2 * nq - 1))), pl.BlockSpec(memory_space=pl.ANY)],
               out_specs=[blk(lambda j: (0, j)), pl.BlockSpec((8, LANES), lambda j: (0, j))],
               out_shape=[_sds(into.shape, into.dtype), _sds((8, GDN_CONV_CH), F32)], input_output_aliases={5: 0},
               compiler_params=_cparams(("parallel",)), name=name)(proj, conv_w, dq, dk, dv, into)


_NN, _NT, _TN = ((1,), (0,)), ((1,), (1,)), ((0,), (0,))


def _split(x, n):
    parts = []
    for _ in range(n):
        h = x.astype(BF16)
        parts.append(h)
        x = x - h.astype(F32)
    return parts


def _dot(a, b, dn, mode):
    d = lambda p, q: lax.dot_general(p, q, (dn, ((), ())), preferred_element_type=F32)
    if mode == "lo":
        return d(a.astype(BF16), b.astype(BF16))
    if mode == "x3":
        (ah, al), (bh, bl) = _split(a, 2), _split(b, 2)
        return d(ah, bh) + (d(ah, bl) + d(al, bh))
    b0, b1, b2 = _split(b, 3)
    ab = a.astype(BF16)
    return d(ab, b0) + (d(ab, b1) + d(ab, b2))


def _make_dots(mode):
    @jax.custom_vjp
    def nn(a, b):
        return _dot(a, b, _NN, mode)

    @jax.custom_vjp
    def nt(a, b):
        return _dot(a, b, _NT, mode)

    @jax.custom_vjp
    def tn(a, b):
        return _dot(a, b, _TN, mode)

    nn.defvjp(lambda a, b: (nn(a, b), (a, b)), lambda r, d: (nt(d, r[1]), tn(r[0], d)))
    nt.defvjp(lambda a, b: (nt(a, b), (a, b)), lambda r, d: (nn(d, r[1]), tn(d, r[0])))
    tn.defvjp(lambda a, b: (tn(a, b), (a, b)), lambda r, d: (nt(r[1], d), nn(r[0], d)))
    return nn, nt, tn


_nn_hi, _nt_hi, _tn_hi = _make_dots("x3")
_nn_lo, _nt_lo, _tn_lo = _make_dots("lo")


@jax.custom_vjp
def _nn_const(a, b):
    return _dot(a, b, _NN, "xl")


_nn_const.defvjp(lambda a, b: (_nn_const(a, b), a), lambda a, d: (jnp.zeros_like(a), _dot(a, d, _TN, "xl")))


def _each(f, *lists):
    return [f(*xs) for xs in zip(*lists)]


@jax.custom_vjp
def _unit_inverses(xs):
    C = xs[0].shape[0]
    eye = (lax.broadcasted_iota(I32, (C, C), 0) == lax.broadcasted_iota(I32, (C, C), 1)).astype(F32)
    ainv, p = [eye + a for a in xs], xs
    for _ in range(int(math.log2(C)) - 1):
        p = _each(lambda a: _dot(a, a, _NN, "x3"), p)
        ainv = _each(lambda a, b: a + _dot(a, b, _NN, "x3"), ainv, p)
    return ainv


def _unit_inverses_bwd(ainv, d):
    left = _each(lambda a, g: _dot(a, g, _TN, "x3"), ainv, d)
    return (_each(lambda t, a: _dot(t, a, _NT, "x3"), left, ainv),)


_unit_inverses.defvjp(lambda xs: (lambda a: (a, a))(_unit_inverses(xs)), _unit_inverses_bwd)


def _gdn_chunk(q, k, v, gb, bb, state):
    C = GDN_C
    e0 = (lax.broadcasted_iota(I32, (1, LANES), 1) == 0).astype(F32)
    ri = lax.broadcasted_iota(I32, (C, C), 0)
    ci = lax.broadcasted_iota(I32, (C, C), 1)
    causal, strict = ri >= ci, ri > ci
    tri, eye, ones = causal.astype(F32), (ri == ci).astype(F32), jnp.ones((C, C), F32)
    last = lax.broadcasted_iota(I32, (C, LANES), 0) == C - 1
    g1 = _each(lambda a: jnp.sum(a * e0, -1, keepdims=True), gb)
    b1 = _each(lambda a: jnp.sum(a * e0, -1, keepdims=True), bb)
    gc_c = _each(lambda g: _nn_const(tri, jnp.broadcast_to(g, (C, C))), g1)
    gc_d = _each(lambda g: _nn_const(tri, jnp.broadcast_to(g, (C, LANES))), g1)
    gr_c = _each(lambda g: _nn_const(ones, eye * g), gc_c)
    decay = _each(lambda a, r: jnp.where(causal, jnp.exp(jnp.where(causal, a - r, 0.0)), 0.0), gc_c, gr_c)
    kb = _each(lambda a, b: a * b, k, b1)
    vb = _each(lambda a, b: a * b, v, b1)
    x = _each(lambda a, b, d: -jnp.where(strict, _nt_lo(a, b) * d, 0.0), kb, k, decay)
    ainv = _unit_inverses(x)
    u = _each(_nn_hi, ainv, vb)
    w = _each(lambda a, b, g: _nn_hi(a, b * jnp.exp(g)), ainv, kb, gc_d)
    attn = _each(lambda a, b, d: jnp.where(causal, _nt_lo(a, b) * d, 0.0), q, k, decay)
    v_new = _each(lambda a, b, s: a - _nn_lo(b, s), u, w, state)
    o = _each(lambda a, g, s, t, vn: _nn_lo(a * jnp.exp(g), s) + _nn_lo(t, vn), q, gc_d, state, attn, v_new)
    gl = _each(lambda g: jnp.sum(jnp.where(last, g, 0.0), axis=0, keepdims=True), gc_d)
    new_state = _each(lambda s, g, a, gd, vn: s * jnp.exp(jnp.sum(g * e0, -1, keepdims=True)) + _tn_lo(a * jnp.exp(g - gd), vn),
                      state, gl, k, gc_d, v_new)
    return o, new_state


def _head_slices(ref, width):
    return [ref[:, h * width:(h + 1) * width] for h in range(GDN_H)]


def gdn_chunk_fwd(qkv, g_b, beta_b, name):
    S_ = qkv.shape[0]
    N = S_ // GDN_C

    def body(q_ref, k_ref, v_ref, g_ref, b_ref, o_ref, s_ref, state):
        @pl.when(pl.program_id(0) == 0)
        def _():
            state[...] = jnp.zeros_like(state)

        st = [state[h] for h in range(GDN_H)]
        s_ref[0] = state[...]
        o, st2 = _gdn_chunk(_head_slices(q_ref, GDN_DK), _head_slices(k_ref, GDN_DK), _head_slices(v_ref, GDN_DV),
                            _head_slices(g_ref, GDN_DK), _head_slices(b_ref, GDN_DK), st)
        for h in range(GDN_H):
            o_ref[:, h * GDN_DV:(h + 1) * GDN_DV] = o[h]
            state[h] = st2[h]

    return _pc(body, grid=(N,),
               in_specs=[pl.BlockSpec((GDN_C, GDN_QK), lambda n: (n, 0)), pl.BlockSpec((GDN_C, GDN_QK), lambda n: (n, 1)),
                         pl.BlockSpec((GDN_C, GDN_V), lambda n: (n, 1)), pl.BlockSpec((GDN_C, GDN_QK), lambda n: (n, 0)),
                         pl.BlockSpec((GDN_C, GDN_QK), lambda n: (n, 0))],
               out_specs=[pl.BlockSpec((GDN_C, GDN_V), lambda n: (n, 0)),
                          pl.BlockSpec((1, GDN_H, GDN_DK, GDN_DV), lambda n: (n, 0, 0, 0))],
               out_shape=[_sds((S_, GDN_V), F32), _sds((N, GDN_H, GDN_DK, GDN_DV), F32)],
               scratch_shapes=[pltpu.VMEM((GDN_H, GDN_DK, GDN_DV), F32)],
               compiler_params=_cparams(("arbitrary",)), name=name)(qkv, qkv, qkv, g_b, beta_b)


def gdn_chunk_bwd(qkv, g_b, beta_b, states, do, name):
    S_ = qkv.shape[0]
    N = S_ // GDN_C

    def body(q_ref, k_ref, v_ref, g_ref, b_ref, s_ref, do_ref, dq_ref, dk_ref, dv_ref, dg_ref, db_ref, dstate):
        @pl.when(pl.program_id(0) == 0)
        def _():
            dstate[...] = jnp.zeros_like(dstate)

        _, vjp = jax.vjp(_gdn_chunk, _head_slices(q_ref, GDN_DK), _head_slices(k_ref, GDN_DK), _head_slices(v_ref, GDN_DV),
                         _head_slices(g_ref, GDN_DK), _head_slices(b_ref, GDN_DK), [s_ref[0, h] for h in range(GDN_H)])
        dq, dk, dv, dg, db, ds = vjp((_head_slices(do_ref, GDN_DV), [dstate[h] for h in range(GDN_H)]))
        for h in range(GDN_H):
            kk, vv = slice(h * GDN_DK, (h + 1) * GDN_DK), slice(h * GDN_DV, (h + 1) * GDN_DV)
            dq_ref[:, kk] = dq[h]
            dk_ref[:, kk] = dk[h]
            dv_ref[:, vv] = dv[h]
            dg_ref[:, kk] = dg[h]
            db_ref[:, kk] = db[h]
            dstate[h] = ds[h]

    r = lambda n: N - 1 - n
    qk = lambda c: pl.BlockSpec((GDN_C, GDN_QK), lambda n: (r(n), c))
    vs = lambda c: pl.BlockSpec((GDN_C, GDN_V), lambda n: (r(n), c))
    return _pc(body, grid=(N,),
               in_specs=[qk(0), qk(1), vs(1), qk(0), qk(0),
                         pl.BlockSpec((1, GDN_H, GDN_DK, GDN_DV), lambda n: (r(n), 0, 0, 0)), vs(0)],
               out_specs=[qk(0), qk(0), vs(0), qk(0), qk(0)],
               out_shape=[_sds((S_, GDN_QK), F32), _sds((S_, GDN_QK), F32), _sds((S_, GDN_V), F32),
                          _sds((S_, GDN_QK), F32), _sds((S_, GDN_QK), F32)],
               scratch_shapes=[pltpu.VMEM((GDN_H, GDN_DK, GDN_DV), F32)],
               compiler_params=_cparams(("arbitrary",)), name=name)(qkv, qkv, qkv, g_b, beta_b, states, do)


def _rope_tables(pos_ref, inv_ref, cm_ref, sg_ref):
    ang = pos_ref[...] * inv_ref[...]
    return jnp.cos(ang) * cm_ref[...], jnp.sin(ang) * sg_ref[...]


def mla_prep_fwd(qpad, kv, proj, pos, rope_consts, name):
    S_ = qpad.shape[0]
    ts = 256
    W = 2 * LANES

    def body(q_ref, kv_ref, kr_ref, pos_ref, inv_ref, cm_ref, sg_ref, qh_ref, kh_ref, vh_ref):
        cs, sn = _rope_tables(pos_ref, inv_ref, cm_ref, sg_ref)
        rope = lambda r: r * cs + pltpu.roll(r, LANES // 2, 1) * sn
        krr = rope(kr_ref[...]).astype(BF16)
        for h in range(MLA_H):
            qh_ref[h, :, 0:LANES] = (q_ref[:, h * W:h * W + LANES] * MLA_SCALE).astype(BF16)
            qh_ref[h, :, LANES:W] = (rope(q_ref[:, h * W + LANES:(h + 1) * W]) * MLA_SCALE).astype(BF16)
            kh_ref[h, :, 0:LANES] = kv_ref[:, h * W:h * W + LANES].astype(BF16)
            kh_ref[h, :, LANES:W] = krr
            vh_ref[h] = kv_ref[:, h * W + LANES:(h + 1) * W].astype(BF16)

    one = pl.BlockSpec((1, LANES), lambda i: (0, 0))
    return _pc(body, grid=(S_ // ts,),
               in_specs=[pl.BlockSpec((ts, MLA_H * W), lambda i: (i, 0)), pl.BlockSpec((ts, MLA_H * W), lambda i: (i, 0)),
                         pl.BlockSpec((ts, LANES), lambda i: (i, 1536 // LANES)), pl.BlockSpec((ts, 1), lambda i: (i, 0)),
                         one, one, one],
               out_specs=[pl.BlockSpec((MLA_H, ts, W), lambda i: (0, i, 0)), pl.BlockSpec((MLA_H, ts, W), lambda i: (0, i, 0)),
                          pl.BlockSpec((MLA_H, ts, LANES), lambda i: (0, i, 0))],
               out_shape=[_sds((MLA_H, S_, W), BF16), _sds((MLA_H, S_, W), BF16), _sds((MLA_H, S_, LANES), BF16)],
               compiler_params=_cparams(("parallel",)), name=name)(qpad, kv, proj, pos, *rope_consts)


def mla_prep_bwd(dqh, dkh, dvh, pos, rope_consts, into, name):
    S_ = dqh.shape[1]
    ts = 256
    W = 2 * LANES

    def body(dq_ref, dk_ref, dv_ref, pos_ref, inv_ref, cm_ref, sg_ref, _, dqp_ref, dkv_ref, dkr_ref):
        cs, sn = _rope_tables(pos_ref, inv_ref, cm_ref, sg_ref)
        rope_t = lambda g: g * cs + pltpu.roll(g * sn, LANES // 2, 1)
        acc = jnp.zeros((ts, LANES), F32)
        for h in range(MLA_H):
            dqp_ref[:, h * W:h * W + LANES] = (dq_ref[h, :, 0:LANES].astype(F32) * MLA_SCALE).astype(BF16)
            dqp_ref[:, h * W + LANES:(h + 1) * W] = (rope_t(dq_ref[h, :, LANES:W].astype(F32)) * MLA_SCALE).astype(BF16)
            dkv_ref[:, h * W:h * W + LANES] = dk_ref[h, :, 0:LANES]
            dkv_ref[:, h * W + LANES:(h + 1) * W] = dv_ref[h]
            acc = acc + dk_ref[h, :, LANES:W].astype(F32)
        dkr_ref[...] = rope_t(acc).astype(dkr_ref.dtype)

    one = pl.BlockSpec((1, LANES), lambda i: (0, 0))
    return _pc(body, grid=(S_ // ts,),
               in_specs=[pl.BlockSpec((MLA_H, ts, W), lambda i: (0, i, 0)), pl.BlockSpec((MLA_H, ts, W), lambda i: (0, i, 0)),
                         pl.BlockSpec((MLA_H, ts, LANES), lambda i: (0, i, 0)), pl.BlockSpec((ts, 1), lambda i: (i, 0)),
                         one, one, one, pl.BlockSpec(memory_space=pl.ANY)],
               out_specs=[pl.BlockSpec((ts, MLA_H * W), lambda i: (i, 0)), pl.BlockSpec((ts, MLA_H * W), lambda i: (i, 0)),
                          pl.BlockSpec((ts, LANES), lambda i: (i, 1536 // LANES))],
               out_shape=[_sds((S_, MLA_H * W), BF16), _sds((S_, MLA_H * W), BF16), _sds(into.shape, into.dtype)],
               input_output_aliases={7: 2}, compiler_params=_cparams(("parallel",)), name=name)(dqh, dkh, dvh, pos, *rope_consts, into)


NEG = -1e30


FLASH_TILE = 1024
FLASH_SUB = 512


def _scores(q, k, diagonal):
    s = lax.dot_general(q, k, (_NT, ((), ())), preferred_element_type=F32)
    if not diagonal:
        return s
    return jnp.where(lax.broadcasted_iota(I32, s.shape, 1) <= lax.broadcasted_iota(I32, s.shape, 0), s, NEG)


def _sub_blocks(t, diagonal):
    sub = min(FLASH_SUB, t) if diagonal else t
    return [(c * sub if diagonal else 0, slice(c * sub, (c + 1) * sub)) for c in range(t // sub)]


FLASH_HEADS = 2


def flash_fwd(qh, kh, vh, name):
    H, S_, W = qh.shape
    t = _tile(S_, FLASH_TILE)
    n = S_ // t
    G = FLASH_HEADS
    heads = list(range(G))

    def body(q_ref, k_ref, v_ref, o_ref, lse_ref, m_s, l_s, acc):
        qi, kj = pl.program_id(1), pl.program_id(2)

        @pl.when(kj == 0)
        def _():
            m_s[...] = jnp.full_like(m_s, NEG)
            l_s[...] = jnp.zeros_like(l_s)
            acc[...] = jnp.zeros_like(acc)

        def step(diagonal):
            s = _each(lambda a: _scores(q_ref[a], k_ref[a], diagonal), heads)
            m_old = _each(lambda a: m_s[a], heads)
            m_new = _each(lambda mo, sa: jnp.maximum(mo, jnp.max(sa, axis=-1, keepdims=True)), m_old, s)
            alpha = _each(lambda mo, mn: jnp.exp(mo - mn), m_old, m_new)
            p = _each(lambda sa, mn: jnp.exp(sa - mn[:, :1]), s, m_new)
            pv = _each(lambda pa, a: lax.dot_general(pa.astype(BF16), v_ref[a], (_NN, ((), ())), preferred_element_type=F32), p, heads)
            for a in heads:
                l_s[a] = alpha[a] * l_s[a] + jnp.sum(p[a], axis=-1, keepdims=True)
                acc[a] = alpha[a] * acc[a] + pv[a]
                m_s[a] = m_new[a]

        pl.when(kj < qi)(lambda: step(False))
        pl.when(kj == qi)(lambda: step(True))

        @pl.when(kj == n - 1)
        def _():
            for a in heads:
                o_ref[:, a * LANES:(a + 1) * LANES] = acc[a] / l_s[a]
                lse_ref[a] = m_s[a] + jnp.log(l_s[a])

    return _pc(body, grid=(H // G, n, n),
               in_specs=[pl.BlockSpec((G, t, W), lambda h, i, j: (h, i, 0)),
                         pl.BlockSpec((G, t, W), lambda h, i, j: (h, jnp.minimum(i, j), 0)),
                         pl.BlockSpec((G, t, LANES), lambda h, i, j: (h, jnp.minimum(i, j), 0))],
               out_specs=[pl.BlockSpec((t, G * LANES), lambda h, i, j: (i, h)), pl.BlockSpec((G, t, LANES), lambda h, i, j: (h, i, 0))],
               out_shape=[_sds((S_, H * LANES), F32), _sds((H, S_, LANES), F32)],
               scratch_shapes=[pltpu.VMEM((G, t, LANES), F32)] * 3,
               compiler_params=_cparams(("parallel", "parallel", "arbitrary")), name=name)(qh, kh, vh)


def flash_bwd(qh, kh, vh, o, lse, do, name):
    H, S_, W = qh.shape
    t = _tile(S_, FLASH_TILE)
    n = S_ // t

    def body(q_ref, k_ref, v_ref, o_ref, lse_ref, do_ref, dq_ref, dk_ref, dv_ref, dq_acc, dk_acc, dv_acc):
        kj, qi = pl.program_id(1), pl.program_id(2)

        @pl.when(jnp.logical_and(kj == 0, qi == 0))
        def _():
            dq_acc[...] = jnp.zeros_like(dq_acc)

        @pl.when(qi == 0)
        def _():
            dk_acc[...] = jnp.zeros_like(dk_acc)
            dv_acc[...] = jnp.zeros_like(dv_acc)

        def step(diagonal):
            do_ = do_ref[...]
            dob = do_.astype(BF16)
            delta = jnp.sum(do_ * o_ref[...], axis=-1, keepdims=True)
            for r0, keys in _sub_blocks(t, diagonal):
                q, k, v = q_ref[r0:, :], k_ref[keys, :], v_ref[keys, :]
                p = jnp.exp(_scores(q, k, diagonal) - lse_ref[r0:, :1])
                dv_acc[keys, :] += lax.dot_general(p.astype(BF16), dob[r0:], (_TN, ((), ())), preferred_element_type=F32)
                dp = lax.dot_general(dob[r0:], v, (_NT, ((), ())), preferred_element_type=F32)
                ds = (p * (dp - delta[r0:])).astype(BF16)
                dk_acc[keys, :] += lax.dot_general(ds, q, (_TN, ((), ())), preferred_element_type=F32)
                rows = pl.ds(pl.multiple_of(qi * t, t) + r0, t - r0)
                dq_acc[rows, :] += lax.dot_general(ds, k, (_NN, ((), ())), preferred_element_type=F32)

        pl.when(qi > kj)(lambda: step(False))
        pl.when(qi == kj)(lambda: step(True))

        @pl.when(qi == n - 1)
        def _():
            dk_ref[...] = dk_acc[...].astype(BF16)
            dv_ref[...] = dv_acc[...].astype(BF16)

        @pl.when(jnp.logical_and(kj == n - 1, qi == n - 1))
        def _():
            dq_ref[...] = dq_acc[...].astype(BF16)

    qrow = lambda h, j, i: jnp.maximum(i, j)
    return _pc(body, grid=(H, n, n),
               in_specs=[pl.BlockSpec((None, t, W), lambda h, j, i: (h, qrow(h, j, i), 0)),
                         pl.BlockSpec((None, t, W), lambda h, j, i: (h, j, 0)),
                         pl.BlockSpec((None, t, LANES), lambda h, j, i: (h, j, 0)),
                         pl.BlockSpec((t, LANES), lambda h, j, i: (qrow(h, j, i), h)),
                         pl.BlockSpec((None, t, LANES), lambda h, j, i: (h, qrow(h, j, i), 0)),
                         pl.BlockSpec((t, LANES), lambda h, j, i: (qrow(h, j, i), h))],
               out_specs=[pl.BlockSpec((None, S_, W), lambda h, j, i: (h, 0, 0)),
                          pl.BlockSpec((None, t, W), lambda h, j, i: (h, j, 0)),
                          pl.BlockSpec((None, t, LANES), lambda h, j, i: (h, j, 0))],
               out_shape=[_sds((H, S_, W), BF16), _sds((H, S_, W), BF16), _sds((H, S_, LANES), BF16)],
               scratch_shapes=[pltpu.VMEM((S_, W), F32), pltpu.VMEM((t, W), F32), pltpu.VMEM((t, LANES), F32)],
               compiler_params=_cparams(("parallel", "arbitrary", "arbitrary")), name=name)(qh, kh, vh, o, lse, do)


def loss_head(x, target, g, name):
    S_ = x.shape[0]
    ts = 256

    def body(x_ref, t_ref, g_ref, l_ref, dx_ref, dg_ref):
        @pl.when(pl.program_id(0) == 0)
        def _():
            l_ref[...] = jnp.zeros_like(l_ref)
            dg_ref[...] = jnp.zeros_like(dg_ref)

        y, vjp = jax.vjp(_rms, x_ref[...], g_ref[...])
        err = y - t_ref[...]
        l_ref[...] += 0.5 * jnp.sum(jnp.sum(err * err, axis=-1, keepdims=True), axis=0, keepdims=True) / D
        dx, dg = vjp(err / D)
        dx_ref[...] = dx
        dg_ref[...] += dg

    row = pl.BlockSpec((ts, D), lambda i: (i, 0))
    return _pc(body, grid=(S_ // ts,), in_specs=[row, row, pl.BlockSpec((1, D), lambda i: (0, 0))],
               out_specs=[pl.BlockSpec((1, LANES), lambda i: (0, 0)), row, pl.BlockSpec((1, D), lambda i: (0, 0))],
               out_shape=[_sds((1, LANES), F32), _sds((S_, D), F32), _sds((1, D), F32)],
               compiler_params=_cparams(("arbitrary",)), name=name)(x, target, g)


def adamw(w, parts, m, v, name):
    R, C = w.shape
    rows = [p.shape[1] for p in parts[0]]
    tr = R
    for cand in (512, 256, 128, 64, 32, 16, 8):
        if all(r % cand == 0 for r in rows) and cand * C * 4 <= 1024 * 1024:
            tr = cand
            break
    c1 = 1.0 - ADAM_B1 ** ADAM_STEP
    c2 = 1.0 - ADAM_B2 ** ADAM_STEP
    starts = [sum(rows[:k]) // tr for k in range(len(rows))]
    flat = [p for part in parts for p in part]

    def body(*refs):
        w_ref, m_ref, v_ref = refs[0], refs[1 + len(flat)], refs[2 + len(flat)]
        g_ref, d_ref, nm_ref, nv_ref = refs[3 + len(flat):]
        i = pl.program_id(0)
        gg, at = None, 1
        for part in parts:
            val = None
            for k in range(len(part)):
                p_ref = refs[at]
                at += 1
                s = p_ref[0].astype(F32)
                for n in range(1, p_ref.shape[0]):
                    s = s + p_ref[n].astype(F32)
                val = s if val is None else jnp.where(i >= starts[k], s, val)
            gg = val if gg is None else gg + val
        m2 = ADAM_B1 * m_ref[...] + (1.0 - ADAM_B1) * gg
        v2 = ADAM_B2 * v_ref[...] + (1.0 - ADAM_B2) * (gg * gg)
        g_ref[...] = gg
        d_ref[...] = -ADAM_LR * ((m2 / c1) / (jnp.sqrt(v2 / c2) + ADAM_EPS) + ADAM_WD * w_ref[...])
        nm_ref[...] = m2
        nv_ref[...] = v2

    blk = pl.BlockSpec((tr, C), lambda i: (i, 0))
    piece = lambda p, k: pl.BlockSpec((p.shape[0], tr, C), lambda i: (0, jnp.clip(i - starts[k], 0, rows[k] // tr - 1), 0))
    pblk = [piece(p, k) for part in parts for k, p in enumerate(part)]
    return _pc(body, grid=(R // tr,), in_specs=[blk] + pblk + [blk, blk], out_specs=[blk] * 4, out_shape=[_sds((R, C), F32)] * 4,
               compiler_params=_cparams(("parallel",)), name=name)(w, *flat, m, v)


def sum_slots(own, recv, skip, name):
    n, R, C = recv.shape
    tr = _tile(R, 512) if R % LANES == 0 else R
    has_own = own is not None

    def body(*refs):
        skip_ref = refs[0]
        r_ref, o_ref = refs[-2], refs[-1]
        acc = refs[1][...] if has_own else jnp.zeros(o_ref.shape, F32)
        for s in range(n):
            acc = acc + jnp.where(skip_ref[0] == s, 0.0, r_ref[s].astype(F32))
        o_ref[...] = acc

    row = pl.BlockSpec((tr, C), lambda i, sk: (i, 0))
    gs = pltpu.PrefetchScalarGridSpec(
        num_scalar_prefetch=1, grid=(R // tr,),
        in_specs=([row] if has_own else []) + [pl.BlockSpec((n, tr, C), lambda i, sk: (0, i, 0))], out_specs=row)
    ins = ([own] if has_own else []) + [recv]
    return _pc(body, grid_spec=gs, out_shape=_sds((R, C), F32), compiler_params=_cparams(("parallel",)), name=name)(skip, *ins)


def _chip_peers():
    x, y, c = lax.axis_index("x"), lax.axis_index("y"), lax.axis_index("c")
    return (x, y, c), [(1 - x, y, c), (x, 1 - y, c), (1 - x, 1 - y, c)]


def _chip_index(p):
    return 2 * p[0] + p[1]


def _win(ref, axis, chip, size):
    if axis is None:
        return ref.at[chip]
    idx = [slice(None)] * len(ref.shape)
    idx[axis] = pl.ds(pl.multiple_of(chip * size, size), size)
    return ref.at[tuple(idx)]


def _remote(src, dst, send_sem, recv_sem, peer):
    return pltpu.make_async_remote_copy(src_ref=src, dst_ref=dst, send_sem=send_sem, recv_sem=recv_sem, device_id=peer,
                                        device_id_type=MESH)


HBM_SPEC = pl.BlockSpec(memory_space=pltpu.HBM)
SEM_SPEC = pl.BlockSpec(memory_space=pltpu.SEMAPHORE)
ANY_SPEC = pl.BlockSpec(memory_space=pl.ANY)
DATAFLOW = pltpu.SideEffectType.DATAFLOW_SIDE_EFFECTING


def gather_piece(i, l, o, axis, size):
    return (i, lambda r, chip: r.at[l], o, lambda r, chip: _win(r, axis, chip, size))


def scatter_piece(i, o, axis, size):
    return (i, lambda r, chip: _win(r, axis, chip, size), o, lambda r, chip: r.at[chip])


def whole_piece(i):
    return (i, lambda r, chip: r, i, lambda r, chip: r)


def _copies(pieces, in_refs, out_refs, send, recv, sibling):
    me, peers = _chip_peers()
    if sibling:
        peers = [(me[0], me[1], 1 - me[2])]
    mine = _chip_index(me)
    remote = []
    for n, (i, src, o, dst) in enumerate(pieces):
        d = dst(out_refs[o], mine)
        remote += [_remote(src(in_refs[i], _chip_index(p)), d, send.at[len(peers) * n + k], recv.at[len(peers) * n + k], p)
                   for k, p in enumerate(peers)]
    return remote


def own_window(a, axis, size, chip):
    if axis is None:
        return lax.dynamic_index_in_dim(a, chip, 0, keepdims=False)
    return lax.dynamic_slice_in_dim(a, chip * size, size, axis=axis)


def place_own(land, own, axis, size, chip):
    if axis is None:
        return lax.dynamic_update_slice_in_dim(land, own[None], chip, axis=0)
    return lax.dynamic_update_slice_in_dim(land, own, chip * size, axis=axis)


def exchange_start(pieces, ins, out_shapes, after, name, sibling=False):
    n_in, n_out, ncp = len(ins), len(out_shapes), len(pieces)

    def body(*refs):
        in_refs, land_refs = refs[:n_in], refs[n_in:n_in + n_out]
        send, recv = refs[n_in + n_out + 1], refs[n_in + n_out + 2]
        token = refs[-1]
        for cp in _copies(pieces, in_refs, land_refs, send, recv, sibling):
            cp.start()
        token[...] = jnp.zeros_like(token)

    hbm = lambda a: pltpu.with_memory_space_constraint(a, pltpu.HBM)
    lands = [hbm(lax.empty(s.shape, s.dtype)) for s in out_shapes]
    sem = pltpu.SemaphoreType.DMA(((1 if sibling else 3) * ncp,))
    thru = [pltpu.HBM(a.shape, a.dtype) for a in ins] + [pltpu.HBM(s.shape, s.dtype) for s in out_shapes]
    res = _pc(body, in_specs=[HBM_SPEC] * (n_in + n_out) + [ANY_SPEC],
              out_specs=[SEM_SPEC, SEM_SPEC] + [HBM_SPEC] * (n_in + n_out) + [pl.BlockSpec(memory_space=pltpu.VMEM)],
              out_shape=[sem, sem] + thru + [_sds((8, LANES), F32)],
              input_output_aliases={i: 2 + i for i in range(n_in + n_out)},
              compiler_params=pltpu.CompilerParams(has_side_effects=DATAFLOW), name=name)(*[hbm(a) for a in ins], *lands, after)
    return (res[0], res[1]), list(res[2:2 + n_in]), list(res[2 + n_in:2 + n_in + n_out]), res[-1]


def exchange_wait(pieces, sems, ins, lands, after, name, sibling=False):
    n_in, n_out = len(ins), len(lands)

    def body(*refs):
        in_refs, land_refs = refs[:n_in], refs[n_in:n_in + n_out]
        send, recv = refs[n_in + n_out], refs[n_in + n_out + 1]
        for cp in _copies(pieces, in_refs, land_refs, send, recv, sibling):
            cp.wait_send()
            cp.wait_recv()

    thru = [pltpu.HBM(a.shape, a.dtype) for a in ins] + [pltpu.HBM(a.shape, a.dtype) for a in lands]
    res = _pc(body, in_specs=[HBM_SPEC] * (n_in + n_out) + [SEM_SPEC, SEM_SPEC, ANY_SPEC], out_specs=[HBM_SPEC] * (n_in + n_out),
              out_shape=thru, input_output_aliases={i: i for i in range(n_in + n_out)},
              compiler_params=pltpu.CompilerParams(has_side_effects=DATAFLOW), name=name)(*ins, *lands, sems[0], sems[1], after)
    return list(res[:n_in]), list(res[n_in:])


def exchange_all(buf, name):
    def body(in_ref, out_ref, send, recv, local):
        x, y, c = lax.axis_index("x"), lax.axis_index("y"), lax.axis_index("c")
        mine = 4 * x + 2 * y + c
        loc = pltpu.make_async_copy(in_ref, out_ref.at[mine], local)
        loc.start()
        copies = [loc]
        for k in range(1, 8):
            peer = (x ^ (k >> 2), y ^ ((k >> 1) & 1), c ^ (k & 1))
            cp = pltpu.make_async_remote_copy(src_ref=in_ref, dst_ref=out_ref.at[mine], send_sem=send.at[k - 1],
                                              recv_sem=recv.at[k - 1], device_id=peer, device_id_type=MESH)
            cp.start()
            copies.append(cp)
        for cp in copies:
            cp.wait()

    anyspec = pl.BlockSpec(memory_space=pl.ANY)
    return _pc(body, in_specs=[anyspec], out_specs=anyspec, out_shape=_sds((8,) + buf.shape, buf.dtype),
               scratch_shapes=[pltpu.SemaphoreType.DMA((7,)), pltpu.SemaphoreType.DMA((7,)), pltpu.SemaphoreType.DMA],
               name=name)(buf)


def _norm_fwd(x, g, name):
    return rowwise(f_rms, [(x, D, 0, 0)], [(g, D, 0, 0)], [(D, 0, BF16)], ts=512, name=name)[0]


def _norm_bwd(x, g, dh, dres, name):
    (dx,), (dg,) = rowwise_bwd(f_rms, [(x, D, 0, 0)], [(g, D, 0, 0)], [(dh, D, 0, 0)], need=[True],
                               adds={0: (dres, D, 0, 0)}, ts=256, name=name)
    return dx, dg


def pool_fwd(x, W, tag):
    h = _norm_fwd(x, W["ng"], tag + "_norm")
    proj = mm(h, W["w_in"], name=tag + "_in")
    p = pool_time_fwd(proj, tag + "_win")
    pg = gmm("nn", p, W["w_grp"], G=4, name=tag + "_grp")
    y = rowwise(f_pool_gate, [(pg, POOL_GROUP, 0, 1), (proj, POOL_GROUP, 4, 1)], [(W["scale"], POOL_GROUP, 0, 1)],
                [(POOL_GROUP, 1, BF16)], ncol=4, ts=512, name=tag + "_gate")[0]
    xn = mm(y, W["w_out"], add=x, name=tag + "_out")
    return xn, (x, h, proj, p, pg, y)


def pool_bwd(dxn, W, saved, tag, after=None, emit=None):
    x, h, proj, p, pg, y = saved
    emit = emit or (lambda grads: None)
    dy = mm(dxn, W["w_out"], tb=True, after=after, name=tag + "_dy")
    g = {}
    (dpg, dproj), (g["scale"],) = rowwise_bwd(
        f_pool_gate, [(pg, POOL_GROUP, 0, 1), (proj, POOL_GROUP, 4, 1)], [(W["scale"], POOL_GROUP, 0, 1)],
        [(dy, POOL_GROUP, 0, 1)], need=[True, True], place={1: (2 * POOL_WIDTH, 4)}, narrow=(0, 1), ncol=4, ts=512, name=tag + "_dgate")
    dp = gmm("nt", dpg, W["w_grp"], G=4, name=tag + "_dp")
    dproj = pool_time_bwd(dp, dproj, tag + "_dwin")
    g["w_in"] = mm(h, dproj, ta=True, out_dtype=BF16, name=tag + "_dw_in")
    t1 = emit({"w_in": g["w_in"]})
    g["w_out"] = mm(y, dxn, ta=True, after=t1, out_dtype=BF16, name=tag + "_dwout")
    g["w_grp"] = gmm("tn", p, dpg, G=4, out_dtype=BF16, name=tag + "_dwgrp")
    t2 = emit({"w_out": g["w_out"], "w_grp": g["w_grp"]})
    dh = mm(dproj, W["w_in"], tb=True, after=t2, name=tag + "_dh")
    dx, g["ng"] = _norm_bwd(x, W["ng"], dh, dxn, tag + "_dnorm")
    return dx, g


def gdn_fwd(x, W, tag):
    h = _norm_fwd(x, W["ng"], tag + "_norm")
    proj = mm(h, W["w_in"], name=tag + "_in")
    qkv = gdn_conv_fwd(proj, W["conv"], tag + "_conv")
    g_b, beta_b = rowwise(f_gdn_gates, [(proj, LANES, 6144 // LANES, 0)], [(W["a_log"], LANES, 0, 0), (W["dt_bias"], LANES, 0, 0)],
                          [(GDN_QK, 0, F32), (GDN_QK, 0, F32)], ts=512, name=tag + "_gates")
    o, states = gdn_chunk_fwd(qkv, g_b, beta_b, tag + "_chunk")
    og = rowwise(f_gdn_out, [(o, GDN_DV, 0, 1), (proj, GDN_DV, 4096 // GDN_DV, 1)], [(W["norm_g"], GDN_DV, 0, 0)],
                 [(GDN_DV, 1, BF16)], ncol=GDN_H, ts=512, name=tag + "_onorm")[0]
    xn = mm(og, W["w_out"], add=x, name=tag + "_out")
    return xn, (x, h, proj, qkv, g_b, beta_b, o, states, og)


def gdn_bwd(dxn, W, saved, tag, after=None):
    x, h, proj, qkv, g_b, beta_b, o, states, og = saved
    dog = mm(dxn, W["w_out"], tb=True, after=after, name=tag + "_dog")
    g = {"w_out": mm(og, dxn, ta=True, out_dtype=BF16, name=tag + "_dwout")}
    (do, dproj), (g["norm_g"],) = rowwise_bwd(
        f_gdn_out, [(o, GDN_DV, 0, 1), (proj, GDN_DV, 4096 // GDN_DV, 1)], [(W["norm_g"], GDN_DV, 0, 0)],
        [(dog, GDN_DV, 0, 1)], need=[True, True], place={1: (GDN_IN_PAD, 4096 // GDN_DV)}, narrow=(1,), ncol=GDN_H, ts=512, name=tag + "_donorm")
    dq, dk, dv, dg_b, dbeta_b = gdn_chunk_bwd(qkv, g_b, beta_b, states, do, tag + "_dchunk")
    (dproj,), (g["a_log"], g["dt_bias"]) = rowwise_bwd(
        f_gdn_gates, [(proj, LANES, 6144 // LANES, 0)], [(W["a_log"], LANES, 0, 0), (W["dt_bias"], LANES, 0, 0)],
        [(dg_b, GDN_QK, 0, 0), (dbeta_b, GDN_QK, 0, 0)], need=[True], place={0: (dproj, 6144 // LANES)}, ts=256, name=tag + "_dgates")
    dproj, g["conv"] = gdn_conv_bwd(proj, W["conv"], dq, dk, dv, dproj, tag + "_dconv")
    dh = mm(dproj, W["w_in"], tb=True, name=tag + "_dh")
    g["w_in"] = mm(h, dproj, ta=True, out_dtype=BF16, name=tag + "_dw_in")
    dx, g["ng"] = _norm_bwd(x, W["ng"], dh, dxn, tag + "_dnorm")
    return dx, g


def mla_fwd(x, pos, W, tag):
    h = _norm_fwd(x, W["ng"], tag + "_norm")
    proj = mm(h, W["w_in"], name=tag + "_in")
    hq = rowwise(f_rms, [(proj, MLA_Q_LORA, 0, 0)], [(W["q_g"], MLA_Q_LORA, 0, 0)], [(MLA_Q_LORA, 0, BF16)], ts=512, name=tag + "_qnorm")[0]
    hkv = rowwise(f_rms, [(proj, MLA_KV_LORA, 2, 0)], [(W["kv_g"], MLA_KV_LORA, 0, 0)], [(MLA_KV_LORA, 0, BF16)], ts=512, name=tag + "_kvnorm")[0]
    qpad = mm(hq, W["w_uq"], name=tag + "_uq")
    kv = mm(hkv, W["w_ukv"], name=tag + "_ukv")
    qh, kh, vh = mla_prep_fwd(qpad, kv, proj, pos, W["rope"], tag + "_prep")
    o, lse = flash_fwd(qh, kh, vh, tag + "_attn")
    og = rowwise(f_ogate, [(o, 512, 0, 1), (proj, 512, 4, 1)], [], [(512, 1, BF16)], ncol=4, ts=512, name=tag + "_ogate")[0]
    xn = mm(og, W["w_out"], add=x, name=tag + "_out")
    return xn, (x, h, proj, hq, hkv, qh, kh, vh, o, lse, og)


def mla_bwd(dxn, pos, W, saved, tag, after=None):
    x, h, proj, hq, hkv, qh, kh, vh, o, lse, og = saved
    dog = mm(dxn, W["w_out"], tb=True, after=after, name=tag + "_dog")
    g = {"w_out": mm(og, dxn, ta=True, out_dtype=BF16, name=tag + "_dwout")}
    dproj = jnp.zeros(proj.shape, BF16)
    (do, dproj), _ = rowwise_bwd(f_ogate, [(o, 512, 0, 1), (proj, 512, 4, 1)], [], [(dog, 512, 0, 1)], need=[True, True],
                                 place={1: (dproj, 4)}, ncol=4, ts=512, name=tag + "_dogate")
    dqh, dkh, dvh = flash_bwd(qh, kh, vh, o, lse, do, tag + "_dattn")
    dqpad, dkv, dproj = mla_prep_bwd(dqh, dkh, dvh, pos, W["rope"], dproj, tag + "_dprep")
    dhq = mm(dqpad, W["w_uq"], tb=True, name=tag + "_dhq")
    g["w_uq"] = mm(hq, dqpad, ta=True, out_dtype=BF16, name=tag + "_dwuq")
    dhkv = mm(dkv, W["w_ukv"], tb=True, name=tag + "_dhkv")
    g["w_ukv"] = mm(hkv, dkv, ta=True, out_dtype=BF16, name=tag + "_dwukv")
    (dproj,), (g["q_g"],) = rowwise_bwd(f_rms, [(proj, MLA_Q_LORA, 0, 0)], [(W["q_g"], MLA_Q_LORA, 0, 0)], [(dhq, MLA_Q_LORA, 0, 0)],
                                        need=[True], place={0: (dproj, 0)}, ts=256, name=tag + "_dqnorm")
    (dproj,), (g["kv_g"],) = rowwise_bwd(f_rms, [(proj, MLA_KV_LORA, 2, 0)], [(W["kv_g"], MLA_KV_LORA, 0, 0)], [(dhkv, MLA_KV_LORA, 0, 0)],
                                         need=[True], place={0: (dproj, 2)}, ts=256, name=tag + "_dkvnorm")
    dh = mm(dproj, W["w_in"], tb=True, name=tag + "_dh")
    g["w_in"] = mm(h, dproj, ta=True, out_dtype=BF16, name=tag + "_dw_in")
    dx, g["ng"] = _norm_bwd(x, W["ng"], dh, dxn, tag + "_dnorm")
    return dx, g


def _pad_cols(a, n):
    return jnp.pad(a, ((0, 0), (0, n - a.shape[1])))


def _mla_w_in_layout(w):
    z = lambda n: jnp.zeros((w.shape[0], n), w.dtype)
    kr = w[:, 1280:1344]
    return jnp.concatenate([w[:, :768], z(256), w[:, 768:1280], kr[:, :32], z(32), kr[:, 32:], z(32), z(384), w[:, 1344:]], axis=1)


def _mla_w_in_unlayout(g):
    return jnp.concatenate([g[:, :768], g[:, 1024:1536], g[:, 1536:1568], g[:, 1600:1632], g[:, 2048:]], axis=1)


def _mla_w_uq_layout(w):
    w3 = w.reshape(w.shape[0], MLA_H, MLA_NOPE + MLA_ROPE)
    z = jnp.zeros((w.shape[0], MLA_H, 32), w.dtype)
    return jnp.concatenate([w3[..., :128], w3[..., 128:160], z, w3[..., 160:192], z], axis=-1).reshape(w.shape[0], MLA_H * 256)


def _mla_w_uq_unlayout(g):
    g3 = g.reshape(g.shape[0], MLA_H, 256)
    return jnp.concatenate([g3[..., :128], g3[..., 128:160], g3[..., 192:224]], axis=-1).reshape(g.shape[0], MLA_H * 192)


def _rope_consts():
    half = MLA_ROPE // 2
    inv = ROPE_THETA ** (-jnp.arange(half, dtype=F32) / half)
    z = jnp.zeros((half,), F32)
    o = jnp.ones((half,), F32)
    row = lambda *p: jnp.concatenate(p).reshape(1, LANES)
    return row(inv, z, inv, z), row(o, z, o, z), row(-o, z, o, z)


BIG = ["pool_w_in", "pool_w_grp", "pool_w_out", "gdn_w_in", "gdn_w_out", "mla_w_in", "mla_w_uq", "mla_w_ukv", "mla_w_out"]
BIG_LAYOUT = {"pool_w_in": (1, 1024, (1024, 4096)), "pool_w_grp": (1, 128, (4, 512, 512)), "pool_w_out": (0, 512, (2048, 1024)),
              "gdn_w_in": (None, None, (4, 1024, 1540)), "gdn_w_out": (0, 512, (2048, 1024)),
              "mla_w_in": (None, None, (4, 1024, 848)), "mla_w_uq": (1, 768, (768, 3072)), "mla_w_ukv": (1, 1024, (512, 4096)),
              "mla_w_out": (0, 512, (2048, 1024))}
SMALL_SHARDED = ["pool_scale", "gdn_conv", "mla_q_norm_g", "mla_kv_norm_g"]
SMALL_AXIS = {"pool_scale": 1, "gdn_conv": 2, "mla_q_norm_g": 1, "mla_kv_norm_g": 1}
REPLICATED = ["norm_g", "gdn_a_log", "gdn_dt_bias", "gdn_norm_g", "final_g"]
PACK_C = 1024


def _pack(parts, dtype, row_mult):
    flat = jnp.concatenate([p.reshape(-1).astype(dtype) for p in parts])
    rows = -(-flat.shape[0] // PACK_C)
    rows = -(-rows // row_mult) * row_mult
    return jnp.pad(flat, (0, rows * PACK_C - flat.shape[0])).reshape(rows, PACK_C)


def _unpack(buf, shapes):
    lead = buf.shape[:-2]
    flat = buf.reshape(lead + (-1,))
    out, off = [], 0
    for s in shapes:
        n = int(np.prod(s))
        out.append(flat[..., off:off + n].reshape(lead + tuple(s)))
        off += n
    return out


def _unshard(g4, axis):
    a = jnp.moveaxis(g4, 0, axis)
    s = a.shape
    return a.reshape(s[:axis] + (s[axis] * s[axis + 1],) + s[axis + 2:])


def _to_shards(a, axis):
    s = a.shape
    return jnp.moveaxis(a.reshape(s[:axis] + (4, s[axis] // 4) + s[axis + 1:]), axis, 0)


def layer_weights(full, small, rep, layer):
    ng = rep["norm_g"][layer:layer + 1]
    side_by_side = lambda a4: jnp.moveaxis(a4, 0, 1).reshape(a4.shape[1], 4 * a4.shape[2])
    if layer in (0, 3):
        j = layer // 3
        return dict(ng=ng, w_in=full[("pool_w_in", j)], w_grp=full[("pool_w_grp", j)], scale=small["pool_scale"][j:j + 1],
                    w_out=full[("pool_w_out", j)])
    if layer == 1:
        return dict(ng=ng, w_in=_pad_cols(side_by_side(full[("gdn_w_in", 0)]), GDN_IN_PAD),
                    conv=jnp.pad(small["gdn_conv"][0], ((0, 4), (0, 0))), a_log=_pad_cols(rep["gdn_a_log"], LANES),
                    dt_bias=_pad_cols(rep["gdn_dt_bias"], LANES), norm_g=rep["gdn_norm_g"], w_out=full[("gdn_w_out", 0)])
    return dict(ng=ng, w_in=_mla_w_in_layout(side_by_side(full[("mla_w_in", 0)])), q_g=small["mla_q_norm_g"],
                kv_g=small["mla_kv_norm_g"], w_uq=_mla_w_uq_layout(full[("mla_w_uq", 0)]), w_ukv=full[("mla_w_ukv", 0)],
                w_out=full[("mla_w_out", 0)], rope=_rope_consts())


def big_grad_pieces(gl):
    g0, g1, g2, g3 = gl
    slots = lambda a: jnp.moveaxis(a.reshape(a.shape[0], 4, a.shape[1] // 4), 1, 0)
    out = {}
    for l, g in ((0, g0), (1, g3)):
        if g is not None:
            out.update({("pool_w_in", l): g["w_in"], ("pool_w_grp", l): g["w_grp"], ("pool_w_out", l): g["w_out"]})
    if g1 is not None:
        out.update({("gdn_w_in", 0): slots(g1["w_in"][:, :GDN_IN]), ("gdn_w_out", 0): g1["w_out"]})
    if g2 is not None:
        out.update({("mla_w_in", 0): slots(_mla_w_in_unlayout(g2["w_in"])), ("mla_w_uq", 0): _mla_w_uq_unlayout(g2["w_uq"]),
                    ("mla_w_ukv", 0): g2["w_ukv"], ("mla_w_out", 0): g2["w_out"]})
    return out


def small_grads(gl, dfinal):
    g0, g1, g2, g3 = gl
    return {"norm_g": jnp.concatenate([g0["ng"], g1["ng"], g2["ng"], g3["ng"]], axis=0),
            "pool_scale": jnp.concatenate([g0["scale"], g3["scale"]], axis=0), "gdn_conv": g1["conv"][None, :4],
            "gdn_a_log": g1["a_log"][:, :GDN_H], "gdn_dt_bias": g1["dt_bias"][:, :GDN_H], "gdn_norm_g": g1["norm_g"],
            "mla_q_norm_g": g2["q_g"], "mla_kv_norm_g": g2["kv_g"], "final_g": dfinal.reshape(D)}


NAMES = ["norm_g", "pool_w_in", "pool_w_grp", "pool_scale", "pool_w_out", "gdn_w_in", "gdn_conv", "gdn_a_log", "gdn_dt_bias",
         "gdn_norm_g", "gdn_w_out", "mla_w_in", "mla_q_norm_g", "mla_w_uq", "mla_kv_norm_g", "mla_w_ukv", "mla_w_out", "final_g"]


def kernel(x, positions, norm_g, pool_w_in, pool_w_grp, pool_scale, pool_w_out, gdn_w_in, gdn_conv, gdn_a_log, gdn_dt_bias, gdn_norm_g, gdn_w_out, mla_w_in, mla_q_norm_g, mla_w_uq, mla_kv_norm_g, mla_w_ukv, mla_w_out, final_g, loss_target, m_norm_g, m_pool_w_in, m_pool_w_grp, m_pool_scale, m_pool_w_out, m_gdn_w_in, m_gdn_conv, m_gdn_a_log, m_gdn_dt_bias, m_gdn_norm_g, m_gdn_w_out, m_mla_w_in, m_mla_q_norm_g, m_mla_w_uq, m_mla_kv_norm_g, m_mla_w_ukv, m_mla_w_out, m_final_g, v_norm_g, v_pool_w_in, v_pool_w_grp, v_pool_scale, v_pool_w_out, v_gdn_w_in, v_gdn_conv, v_gdn_a_log, v_gdn_dt_bias, v_gdn_norm_g, v_gdn_w_out, v_mla_w_in, v_mla_q_norm_g, v_mla_w_uq, v_mla_kv_norm_g, v_mla_w_ukv, v_mla_w_out, v_final_g):
    args = locals()
    w = {n: args[n] for n in NAMES}
    m = {n: args["m_" + n] for n in NAMES}
    v = {n: args["v_" + n] for n in NAMES}
    my_chip = (2 * lax.axis_index("x") + lax.axis_index("y")).astype(I32)
    S_ = x.shape[1]
    x0, pos, target = x[0], positions.reshape(S_, 1).astype(F32), loss_target[0]
    rep = {n: w[n] for n in REPLICATED}

    shard = {(n, l): w[n][l:l + 1].astype(BF16) for n in BIG for l in range(w[n].shape[0])}
    small_shapes = [w[n].shape for n in SMALL_SHARDED]
    shard[("small", 0)] = _pack([w[n] for n in SMALL_SHARDED], F32, 8)[None]
    layout = dict(BIG_LAYOUT, small=(None, None, (4,) + shard[("small", 0)].shape[1:]))

    def gather_start(group, after, tag):
        pieces = [gather_piece(i, 0, i, layout[n][0], layout[n][1]) for i, (n, l) in enumerate(group)]
        shapes = [_sds(layout[n][2], shard[(n, l)].dtype) for n, l in group]
        sems, ins, lands, token = exchange_start(pieces, [shard[k] for k in group], shapes, after, tag + "_start")
        return (pieces, sems, ins, lands), token

    def finish(handle, after, tag):
        return exchange_wait(*handle, after, tag + "_wait")

    def gathered(group, handle, after, tag):
        srcs, lands = finish(handle, after, tag)
        return {(n, l): place_own(a, s[0], layout[n][0], layout[n][1], my_chip) for (n, l), s, a in zip(group, srcs, lands)}

    tied = lambda a, token: a + token[0:1, 0:1]
    group_a = [("pool_w_in", 0), ("pool_w_grp", 0), ("pool_w_out", 0), ("small", 0)]
    group_b = [("gdn_w_in", 0), ("gdn_w_out", 0)]
    group_c = [("mla_w_in", 0), ("mla_w_uq", 0), ("mla_w_ukv", 0), ("mla_w_out", 0), ("pool_w_in", 1), ("pool_w_grp", 1), ("pool_w_out", 1)]
    full = {}
    h_a, t_a = gather_start(group_a, x0, "gather_a")
    full.update(gathered(group_a, h_a, t_a, "gather_a"))
    small = {n: _unshard(a, SMALL_AXIS[n]) for n, a in zip(SMALL_SHARDED, _unpack(full[("small", 0)], small_shapes))}
    h_b, t_b = gather_start(group_b, full[group_a[0]], "gather_b")
    W0 = layer_weights(full, small, rep, 0)
    x1, s0 = pool_fwd(x0, dict(W0, ng=tied(W0["ng"], t_b)), "l0")
    full.update(gathered(group_b, h_b, x1, "gather_b"))
    h_c, t_c = gather_start(group_c, full[group_b[0]], "gather_c")
    W1 = layer_weights(full, small, rep, 1)
    x2, s1 = gdn_fwd(x1, dict(W1, ng=tied(W1["ng"], t_c)), "l1")
    full.update(gathered(group_c, h_c, x2, "gather_c"))
    W2, W3 = layer_weights(full, small, rep, 2), layer_weights(full, small, rep, 3)
    x3, s2 = mla_fwd(x2, pos, W2, "l2")
    x4, s3 = pool_fwd(x3, W3, "l3")
    loss_part, dx4, dfinal = loss_head(x4, target, final_g.reshape(1, D), "loss_head")

    def scatter_start(pieces_of, after, tag):
        keys = list(pieces_of)
        pieces = [scatter_piece(i, i, BIG_LAYOUT[n][0], BIG_LAYOUT[n][1]) for i, (n, l) in enumerate(keys)]
        shapes = [_sds((4,) + tuple(w[n].shape[1:]), BF16) for n, l in keys]
        sems, ins, lands, token = exchange_start(pieces, [pieces_of[k] for k in keys], shapes, after, tag + "_start")
        return keys, (pieces, sems, ins, lands), token

    def scattered(keys, handle, after, tag):
        srcs, lands = finish(handle, after, tag)
        return {(n, l): place_own(a, own_window(g, BIG_LAYOUT[n][0], BIG_LAYOUT[n][1], my_chip), None, None, my_chip)
                for (n, l), g, a in zip(keys, srcs, lands)}

    dx3, g3 = pool_bwd(dx4, W3, s3, "l3")
    k3, h3, t3 = scatter_start(big_grad_pieces((None, None, None, g3)), dx3, "scatter_l3")
    dx2, g2 = mla_bwd(dx3, pos, W2, s2, "l2", after=t3)
    k2, h2, t2 = scatter_start(big_grad_pieces((None, None, g2, None)), dx2, "scatter_l2")
    dx1, g1 = gdn_bwd(dx2, W1, s1, "l1", after=t2)
    k1, h1, t1 = scatter_start(big_grad_pieces((None, g1, None, None)), dx1, "scatter_l1")
    def swap_start(part, tag):
        keys = list(part)
        ins = [part[k] for k in keys]
        pieces = [whole_piece(i) for i in range(len(keys))]
        sems, ins, lands, _ = exchange_start(pieces, ins, [_sds(a.shape, a.dtype) for a in ins], ins[0], tag + "_start", sibling=True)
        return keys, (pieces, sems, ins, lands), tag

    last, swaps = [], []

    def emit_l0(grads):
        first = not last
        now = next(iter(grads.values()))
        early = [(k3, h3, "scatter_l3"), (k2, h2, "scatter_l2")] if first else [(k1, h1, "scatter_l1")]
        landed = {}
        for keys, handle, tag in early:
            landed.update(scattered(keys, handle, now, tag))
        swaps.append(swap_start(landed, "swap_a" if first else "swap_b"))
        tag = "scatter_l0a" if first else "scatter_l0b"
        keys, handle, token = scatter_start({("pool_" + k, 0): a for k, a in grads.items()}, now, tag)
        last.append((keys, handle, tag))
        return token

    dx0, g0 = pool_bwd(dx1, W0, s0, "l0", after=t1, emit=emit_l0)
    landed = {}
    for keys, handle, tag in last:
        landed.update(scattered(keys, handle, dx0, tag))
    swaps.append(swap_start(landed, "swap_c"))
    recv, sib = {}, {}
    for keys, handle, tag in swaps:
        mine_, theirs = exchange_wait(*handle, dx0, tag + "_wait", sibling=True)
        recv.update(zip(keys, mine_))
        sib.update(zip(keys, theirs))

    sg = small_grads((g0, g1, g2, g3), dfinal)
    small_names = SMALL_SHARDED + REPLICATED
    small_buf = _pack([sg[n] for n in small_names] + [loss_part], F32, 8)
    small_sum = sum_slots(None, exchange_all(small_buf, "gather_small"), jnp.full((1,), -1, I32), "sum_small")
    full_small = _unpack(small_sum, [sg[n].shape for n in small_names] + [(1, LANES)])
    loss = full_small[-1][0, 0]
    small_part = {}
    for n, a in zip(small_names, full_small[:-1]):
        if n in SMALL_AXIS:
            a = lax.dynamic_index_in_dim(_to_shards(a, SMALL_AXIS[n]), my_chip, axis=0, keepdims=False)
        small_part[n] = a

    outs = []
    for n in NAMES:
        shp = w[n].shape
        two = (int(np.prod(shp[:-1])), shp[-1]) if len(shp) > 1 else (1, shp[0])
        if n in BIG_LAYOUT:
            layers = shp[0]
            rows = lambda a: a.reshape(4, two[0] // layers, two[1])
            parts = [[rows(recv[(n, l)]) for l in range(layers)], [rows(sib[(n, l)]) for l in range(layers)]]
        else:
            parts = [[small_part[n].reshape((1,) + two)]]
        res = adamw(w[n].reshape(two), parts, m[n].reshape(two), v[n].reshape(two), "adamw_" + n)
        outs.append([r.reshape(shp) for r in res])
    return (loss, dx0[None], *[o[0] for o in outs], *[o[1] for o in outs], *[o[2] for o in outs], *[o[3] for o in outs])
```

```python
import functools
import math

import jax
import jax.numpy as jnp
import numpy as np
from jax import lax
from jax.experimental import pallas as pl
from jax.experimental.pallas import tpu as pltpu

F32 = jnp.float32
BF16 = jnp.bfloat16
I32 = jnp.int32

D = 1024
EPS = 1e-6
POOL_WIDTH = 2048
POOL_GROUP = 512
GDN_H, GDN_DK, GDN_DV, GDN_C = 8, 128, 256, 64
GDN_QK, GDN_V, GDN_CONV_CH, GDN_IN = 1024, 2048, 4096, 6160
GDN_IN_PAD = 6272
MLA_H, MLA_NOPE, MLA_ROPE, MLA_V = 16, 128, 64, 128
MLA_Q_LORA, MLA_KV_LORA, MLA_WIDTH, MLA_IN = 768, 512, 2048, 3392
MLA_IN_PAD = 4096
MLA_SCALE = (MLA_NOPE + MLA_ROPE) ** -0.5
ROPE_THETA = 10000.0
ADAM_LR, ADAM_B1, ADAM_B2, ADAM_EPS, ADAM_WD, ADAM_STEP = 0.001, 0.9, 0.999, 1e-08, 0.01, 10

VMEM_LIMIT_V7X = 56 * 1024 * 1024
LANES = 128
MESH = pl.DeviceIdType.MESH


def _pc(body, **kw):
    return pl.pallas_call(body, **kw)


def _cparams(sem):
    return pltpu.CompilerParams(dimension_semantics=sem, vmem_limit_bytes=VMEM_LIMIT_V7X)


def _tile(n, cap):
    t = (cap // LANES) * LANES
    while t >= LANES:
        if n % t == 0:
            return t
        t -= LANES
    return n


def _sds(shape, dt):
    return jax.ShapeDtypeStruct(shape, dt)


def mm(a, b, *, ta=False, tb=False, add=None, after=None, out_dtype=F32, name):
    if ta:
        K, M = a.shape
    else:
        M, K = a.shape
    if tb:
        N, K2 = b.shape
    else:
        K2, N = b.shape
    assert K == K2, (a.shape, b.shape, ta, tb)
    tm, tn, tk = _tile(M, 1024), _tile(N, 1024), _tile(K, 1024)
    nk = K // tk
    a_spec = pl.BlockSpec((tk, tm), lambda i, j, k: (k, i)) if ta else pl.BlockSpec((tm, tk), lambda i, j, k: (i, k))
    b_spec = pl.BlockSpec((tn, tk), lambda i, j, k: (j, k)) if tb else pl.BlockSpec((tk, tn), lambda i, j, k: (k, j))
    o_spec = pl.BlockSpec((tm, tn), lambda i, j, k: (i, j))
    dn = (((0 if ta else 1,), (1 if tb else 0,)), ((), ()))
    has_add = add is not None

    def body(*refs):
        a_ref, b_ref = refs[0], refs[1]
        part = lax.dot_general(a_ref[...].astype(BF16), b_ref[...].astype(BF16), dn, preferred_element_type=F32)
        if nk == 1:
            refs[-1][...] = (part + refs[2][...] if has_add else part).astype(out_dtype)
            return
        o_ref, acc = refs[-2], refs[-1]
        k = pl.program_id(2)

        @pl.when(k == 0)
        def _():
            acc[...] = part

        @pl.when(k > 0)
        def _():
            acc[...] += part

        @pl.when(k == nk - 1)
        def _():
            r = acc[...]
            if has_add:
                r = r + refs[2][...]
            o_ref[...] = r.astype(out_dtype)

    ins = [a, b] + ([add] if has_add else []) + ([after] if after is not None else [])
    specs = [a_spec, b_spec] + ([o_spec] if has_add else []) + ([pl.BlockSpec(memory_space=pl.ANY)] if after is not None else [])
    return _pc(body, grid=(M // tm, N // tn, nk), in_specs=specs, out_specs=o_spec, out_shape=_sds((M, N), out_dtype),
               scratch_shapes=[pltpu.VMEM((tm, tn), F32)] if nk > 1 else [], compiler_params=_cparams(("parallel", "parallel", "arbitrary")),
               name=name)(*ins)


def gmm(kind, a, b, *, G, name, out_dtype=F32):
    S_ = a.shape[0]
    Ka = a.shape[1] // G
    if kind == "tn":
        N = b.shape[1] // G
        tk = _tile(S_, 512)
        nk = S_ // tk

        def body(a_ref, b_ref, o_ref, acc):
            k = pl.program_id(1)

            @pl.when(k == 0)
            def _():
                acc[...] = jnp.zeros_like(acc)

            acc[...] += lax.dot_general(a_ref[...].astype(BF16), b_ref[...].astype(BF16), (((0,), (0,)), ((), ())),
                                        preferred_element_type=F32)

            @pl.when(k == nk - 1)
            def _():
                o_ref[...] = acc[...].astype(out_dtype)

        return _pc(body, grid=(G, nk),
                   in_specs=[pl.BlockSpec((tk, Ka), lambda g, k: (k, g)), pl.BlockSpec((tk, N), lambda g, k: (k, g))],
                   out_specs=pl.BlockSpec((None, Ka, N), lambda g, k: (g, 0, 0)), out_shape=_sds((G, Ka, N), out_dtype),
                   scratch_shapes=[pltpu.VMEM((Ka, N), F32)], compiler_params=_cparams(("parallel", "arbitrary")), name=name)(a, b)
    N = b.shape[2] if kind == "nn" else b.shape[1]
    tm = _tile(S_, 1024)
    dn = (((1,), (0 if kind == "nn" else 1,)), ((), ()))

    def body(a_ref, b_ref, o_ref):
        o_ref[...] = lax.dot_general(a_ref[...].astype(BF16), b_ref[...].astype(BF16), dn, preferred_element_type=F32)

    bshape = (None,) + tuple(b.shape[1:])
    return _pc(body, grid=(G, S_ // tm),
               in_specs=[pl.BlockSpec((tm, Ka), lambda g, i: (i, g)), pl.BlockSpec(bshape, lambda g, i: (g, 0, 0))],
               out_specs=pl.BlockSpec((tm, N), lambda g, i: (i, g)), out_shape=_sds((S_, G * N), F32),
               compiler_params=_cparams(("parallel", "parallel")), name=name)(a, b)


def _rw_spec(ts, w, c, s):
    return pl.BlockSpec((ts, w), lambda j, i: (i, c + j * s))


def _rw_pspec(p, w, c, s):
    return pl.BlockSpec((p.shape[0], w), lambda j, i: (0, c + j * s))


def rowwise(f, tiles, params, outs, *, ncol=1, ts, name):
    S_ = tiles[0][0].shape[0]
    nin = len(tiles) + len(params)

    def body(*refs):
        res = f(pl.program_id(0), *[r[...] for r in refs[:nin]])
        for r, o in zip(refs[nin:], res):
            r[...] = o.astype(r.dtype)

    return _pc(body, grid=(ncol, S_ // ts),
               in_specs=[_rw_spec(ts, w, c, s) for (_, w, c, s) in tiles] + [_rw_pspec(*p) for p in params],
               out_specs=[_rw_spec(ts, w, 0, s) for (w, s, _) in outs],
               out_shape=[_sds((S_, w * (ncol if s else 1)), dt) for (w, s, dt) in outs],
               compiler_params=_cparams(("parallel", "parallel")), name=name)(*[t[0] for t in tiles], *[p[0] for p in params])


def rowwise_bwd(f, tiles, params, cots, *, need, adds=None, place=None, narrow=(), ncol=1, ts, name):
    S_ = tiles[0][0].shape[0]
    adds = adds or {}
    place = place or {}
    nt, npar, nc = len(tiles), len(params), len(cots)
    add_keys = sorted(adds)
    need_idx = [k for k in range(nt) if need[k]]
    into_keys = [k for k in need_idx if k in place and not isinstance(place[k][0], int)]
    n_extra = len(add_keys) + len(into_keys)

    def body(*refs):
        j, i = pl.program_id(0), pl.program_id(1)
        vals = [r[...] for r in refs[:nt + npar]]
        cvals = tuple(r[...] for r in refs[nt + npar:nt + npar + nc])
        add_refs = refs[nt + npar + nc:nt + npar + nc + len(add_keys)]
        out_refs = refs[nt + npar + nc + n_extra:]
        _, vjp = jax.vjp(lambda *v: tuple(f(j, *v)), *vals)
        grads = vjp(cvals)
        for n, k in enumerate(need_idx):
            g = grads[k]
            if k in adds:
                g = g + add_refs[add_keys.index(k)][...]
            out_refs[n][...] = g.astype(out_refs[n].dtype)
        for n in range(npar):
            ref = out_refs[len(need_idx) + n]
            first = (i == 0) if params[n][3] else jnp.logical_and(i == 0, j == 0)

            @pl.when(first)
            def _():
                ref[...] = jnp.zeros_like(ref)

            ref[...] += grads[nt + n]

    in_specs = ([_rw_spec(ts, w, c, s) for (_, w, c, s) in tiles] + [_rw_pspec(*p) for p in params]
                + [_rw_spec(ts, w, c, s) for (_, w, c, s) in cots] + [_rw_spec(ts, *adds[k][1:]) for k in add_keys]
                + [pl.BlockSpec(memory_space=pl.ANY) for _ in into_keys])
    out_specs, out_shape, aliases = [], [], {}
    for n, k in enumerate(need_idx):
        w, s = tiles[k][1], tiles[k][3]
        if k in place:
            dst, c0 = place[k]
            total = dst if isinstance(dst, int) else dst.shape[1]
            out_specs.append(_rw_spec(ts, w, c0, s))
            out_shape.append(_sds((S_, total), (BF16 if k in narrow else F32) if isinstance(dst, int) else dst.dtype))
            if k in into_keys:
                aliases[nt + npar + nc + len(add_keys) + into_keys.index(k)] = n
        else:
            out_specs.append(_rw_spec(ts, w, 0, s))
            out_shape.append(_sds((S_, w * (ncol if s else 1)), BF16 if k in narrow else F32))
    out_specs += [_rw_pspec(p[0], p[1], p[2], p[3]) for p in params]
    out_shape += [_sds(p[0].shape, F32) for p in params]
    res = _pc(body, grid=(ncol, S_ // ts), in_specs=in_specs, out_specs=out_specs, out_shape=out_shape,
              input_output_aliases=aliases, compiler_params=_cparams(("arbitrary", "arbitrary")), name=name)(
        *[t[0] for t in tiles], *[p[0] for p in params], *[c[0] for c in cots], *[adds[k][0] for k in add_keys],
        *[place[k][0] for k in into_keys])
    return list(res[:len(need_idx)]), list(res[len(need_idx):])


def _rms(x, g):
    r = lax.rsqrt(jnp.mean(x * x, axis=-1, keepdims=True) + EPS)
    return x * r * g


def _silu(x):
    return x * jax.nn.sigmoid(x)


@jax.custom_vjp
def _softplus(x):
    return jnp.maximum(x, 0.0) + jnp.log1p(jnp.exp(-jnp.abs(x)))


_softplus.defvjp(lambda x: (_softplus(x), x), lambda x, d: (d * jax.nn.sigmoid(x),))


def f_rms(j, x, g):
    return (_rms(x, g),)


def f_pool_gate(j, pg, gate, scale):
    return (pg * scale * _silu(gate),)


def f_ogate(j, o, gate):
    return (o * _silu(gate),)


def f_gdn_out(j, o, gate, g):
    return (_rms(o, g) * _silu(gate),)


def f_gdn_gates(j, ba, alog, dtb):
    lane = lax.broadcasted_iota(I32, (1, LANES), 1)
    gs, bs = [], []
    for h in range(GDN_H):
        eb = (lane == h).astype(F32)
        ea = (lane == GDN_H + h).astype(F32)
        b = jnp.sum(ba * eb, -1, keepdims=True)
        a = jnp.sum(ba * ea, -1, keepdims=True)
        al = jnp.sum(alog * eb, -1, keepdims=True)
        dt = jnp.sum(dtb * eb, -1, keepdims=True)
        g = -jnp.exp(al) * _softplus(a + dt)
        gs.append(jnp.broadcast_to(g, ba.shape))
        bs.append(jnp.broadcast_to(jax.nn.sigmoid(b), ba.shape))
    return jnp.concatenate(gs, 1), jnp.concatenate(bs, 1)


def _shift_dn(x, k):
    rows = lax.broadcasted_iota(I32, x.shape, 0)
    return jnp.where(rows < k, 0.0, pltpu.roll(x, k, 0))


def _shift_up(x, k):
    n = x.shape[0]
    rows = lax.broadcasted_iota(I32, x.shape, 0)
    return jnp.where(rows >= n - k, 0.0, pltpu.roll(x, n - k, 0))


def _pool_window(j):
    g = lax.div(j, POOL_GROUP // LANES)
    return jnp.where(g == 0, 2.0, jnp.where(g == 1, 4.0, jnp.where(g == 2, 8.0, 16.0))), g


def _pick(g, a2, a4, a8, a16):
    return jnp.where(g == 0, a2, jnp.where(g == 1, a4, jnp.where(g == 2, a8, a16)))


def pool_time_fwd(proj, name):
    S_ = proj.shape[0]

    def body(u_ref, p_ref):
        u = u_ref[...]
        w, g = _pool_window(pl.program_id(0))
        s2 = u + _shift_dn(u, 1)
        s4 = s2 + _shift_dn(s2, 2)
        s8 = s4 + _shift_dn(s4, 4)
        s16 = s8 + _shift_dn(s8, 8)
        t1 = (lax.broadcasted_iota(I32, u.shape, 0) + 1).astype(F32)
        p_ref[...] = (_pick(g, s2, s4, s8, s16) / jnp.minimum(t1, w) - u).astype(p_ref.dtype)

    return _pc(body, grid=(POOL_WIDTH // LANES,), in_specs=[pl.BlockSpec((S_, LANES), lambda j: (0, j))],
               out_specs=pl.BlockSpec((S_, LANES), lambda j: (0, j)), out_shape=_sds((S_, POOL_WIDTH), BF16),
               compiler_params=_cparams(("parallel",)), name=name)(proj)


def pool_time_bwd(dp, into, name):
    S_ = dp.shape[0]

    def body(dp_ref, _, du_ref):
        d = dp_ref[...]
        w, g = _pool_window(pl.program_id(0))
        t1 = (lax.broadcasted_iota(I32, d.shape, 0) + 1).astype(F32)
        q = d / jnp.minimum(t1, w)
        r2 = q + _shift_up(q, 1)
        r4 = r2 + _shift_up(r2, 2)
        r8 = r4 + _shift_up(r4, 4)
        r16 = r8 + _shift_up(r8, 8)
        du_ref[...] = (_pick(g, r2, r4, r8, r16) - d).astype(du_ref.dtype)

    return _pc(body, grid=(POOL_WIDTH // LANES,),
               in_specs=[pl.BlockSpec((S_, LANES), lambda j: (0, j)), pl.BlockSpec(memory_space=pl.ANY)],
               out_specs=pl.BlockSpec((S_, LANES), lambda j: (0, j)), out_shape=_sds(into.shape, into.dtype),
               input_output_aliases={1: 0}, compiler_params=_cparams(("parallel",)), name=name)(dp, into)


def _conv_post(j, a):
    n = a * lax.rsqrt(jnp.sum(a * a, axis=-1, keepdims=True) + EPS)
    nq = GDN_QK // LANES
    return jnp.where(j < nq, n * (GDN_DK ** -0.5), jnp.where(j < 2 * nq, n, a))


def _conv_pre(u, w):
    return w[3:4] * u + w[2:3] * _shift_dn(u, 1) + w[1:2] * _shift_dn(u, 2) + w[0:1] * _shift_dn(u, 3)


def gdn_conv_fwd(proj, conv_w, name):
    S_ = proj.shape[0]

    def body(u_ref, w_ref, o_ref):
        o_ref[...] = _conv_post(pl.program_id(0), _silu(_conv_pre(u_ref[...], w_ref[...])))

    return _pc(body, grid=(GDN_CONV_CH // LANES,),
               in_specs=[pl.BlockSpec((S_, LANES), lambda j: (0, j)), pl.BlockSpec((8, LANES), lambda j: (0, j))],
               out_specs=pl.BlockSpec((S_, LANES), lambda j: (0, j)), out_shape=_sds((S_, GDN_CONV_CH), F32),
               compiler_params=_cparams(("parallel",)), name=name)(proj, conv_w)


def gdn_conv_bwd(proj, conv_w, dq, dk, dv, into, name):
    S_ = proj.shape[0]
    nq = GDN_QK // LANES

    def body(u_ref, w_ref, dq_ref, dk_ref, dv_ref, _, du_ref, dw_ref):
        j = pl.program_id(0)
        u, w = u_ref[...], w_ref[...]
        c = _conv_pre(u, w)
        sig = jax.nn.sigmoid(c)
        dout = jnp.where(j < nq, dq_ref[...], jnp.where(j < 2 * nq, dk_ref[...], dv_ref[...]))
        _, vjp = jax.vjp(lambda a: _conv_post(j, a), c * sig)
        dc = vjp(dout)[0] * (sig * (1.0 + c * (1.0 - sig)))
        du = w[3:4] * dc + w[2:3] * _shift_up(dc, 1) + w[1:2] * _shift_up(dc, 2) + w[0:1] * _shift_up(dc, 3)
        du_ref[...] = du.astype(du_ref.dtype)
        rows = lax.broadcasted_iota(I32, (8, LANES), 0)
        dw = jnp.zeros((8, LANES), F32)
        for k in range(4):
            us = u if k == 3 else _shift_dn(u, 3 - k)
            dw = dw + jnp.where(rows == k, jnp.sum(dc * us, axis=0, keepdims=True), 0.0)
        dw_ref[...] = dw

    blk = lambda f: pl.BlockSpec((S_, LANES), f)
    return _pc(body, grid=(GDN_CONV_CH // LANES,),
               in_specs=[blk(lambda j: (0, j)), pl.BlockSpec((8, LANES), lambda j: (0, j)),
                         blk(lambda j: (0, jnp.minimum(j, nq - 1))), blk(lambda j: (0, jnp.clip(j - nq, 0, nq - 1))),
                         blk(lambda j: (0, jnp.clip(j - 2 * nq, 0, 2 * nq - 1))), pl.BlockSpec(memory_space=pl.ANY)],
               out_specs=[blk(lambda j: (0, j)), pl.BlockSpec((8, LANES), lambda j: (0, j))],
               out_shape=[_sds(into.shape, into.dtype), _sds((8, GDN_CONV_CH), F32)], input_output_aliases={5: 0},
               compiler_params=_cparams(("parallel",)), name=name)(proj, conv_w, dq, dk, dv, into)


_NN, _NT, _TN = ((1,), (0,)), ((1,), (1,)), ((0,), (0,))


def _split(x, n):
    parts = []
    for _ in range(n):
        h = x.astype(BF16)
        parts.append(h)
        x = x - h.astype(F32)
    return parts


def _dot(a, b, dn, mode):
    d = lambda p, q: lax.dot_general(p, q, (dn, ((), ())), preferred_element_type=F32)
    if mode == "lo":
        return d(a.astype(BF16), b.astype(BF16))
    if mode == "x3":
        (ah, al), (bh, bl) = _split(a, 2), _split(b, 2)
        return d(ah, bh) + (d(ah, bl) + d(al, bh))
    b0, b1, b2 = _split(b, 3)
    ab = a.astype(BF16)
    return d(ab, b0) + (d(ab, b1) + d(ab, b2))


def _make_dots(mode):
    @jax.custom_vjp
    def nn(a, b):
        return _dot(a, b, _NN, mode)

    @jax.custom_vjp
    def nt(a, b):
        return _dot(a, b, _NT, mode)

    @jax.custom_vjp
    def tn(a, b):
        return _dot(a, b, _TN, mode)

    nn.defvjp(lambda a, b: (nn(a, b), (a, b)), lambda r, d: (nt(d, r[1]), tn(r[0], d)))
    nt.defvjp(lambda a, b: (nt(a, b), (a, b)), lambda r, d: (nn(d, r[1]), tn(d, r[0])))
    tn.defvjp(lambda a, b: (tn(a, b), (a, b)), lambda r, d: (nt(r[1], d), nn(r[0], d)))
    return nn, nt, tn


_nn_hi, _nt_hi, _tn_hi = _make_dots("x3")
_nn_lo, _nt_lo, _tn_lo = _make_dots("lo")


@jax.custom_vjp
def _nn_const(a, b):
    return _dot(a, b, _NN, "xl")


_nn_const.defvjp(lambda a, b: (_nn_const(a, b), a), lambda a, d: (jnp.zeros_like(a), _dot(a, d, _TN, "xl")))


def _each(f, *lists):
    return [f(*xs) for xs in zip(*lists)]


@jax.custom_vjp
def _unit_inverses(xs):
    C = xs[0].shape[0]
    eye = (lax.broadcasted_iota(I32, (C, C), 0) == lax.broadcasted_iota(I32, (C, C), 1)).astype(F32)
    ainv, p = [eye + a for a in xs], xs
    for _ in range(int(math.log2(C)) - 1):
        p = _each(lambda a: _dot(a, a, _NN, "x3"), p)
        ainv = _each(lambda a, b: a + _dot(a, b, _NN, "x3"), ainv, p)
    return ainv


def _unit_inverses_bwd(ainv, d):
    left = _each(lambda a, g: _dot(a, g, _TN, "x3"), ainv, d)
    return (_each(lambda t, a: _dot(t, a, _NT, "x3"), left, ainv),)


_unit_inverses.defvjp(lambda xs: (lambda a: (a, a))(_unit_inverses(xs)), _unit_inverses_bwd)


def _gdn_chunk(q, k, v, gb, bb, state):
    C = GDN_C
    e0 = (lax.broadcasted_iota(I32, (1, LANES), 1) == 0).astype(F32)
    ri = lax.broadcasted_iota(I32, (C, C), 0)
    ci = lax.broadcasted_iota(I32, (C, C), 1)
    causal, strict = ri >= ci, ri > ci
    tri, eye, ones = causal.astype(F32), (ri == ci).astype(F32), jnp.ones((C, C), F32)
    last = lax.broadcasted_iota(I32, (C, LANES), 0) == C - 1
    g1 = _each(lambda a: jnp.sum(a * e0, -1, keepdims=True), gb)
    b1 = _each(lambda a: jnp.sum(a * e0, -1, keepdims=True), bb)
    gc_c = _each(lambda g: _nn_const(tri, jnp.broadcast_to(g, (C, C))), g1)
    gc_d = _each(lambda g: _nn_const(tri, jnp.broadcast_to(g, (C, LANES))), g1)
    gr_c = _each(lambda g: _nn_const(ones, eye * g), gc_c)
    decay = _each(lambda a, r: jnp.where(causal, jnp.exp(jnp.where(causal, a - r, 0.0)), 0.0), gc_c, gr_c)
    kb = _each(lambda a, b: a * b, k, b1)
    vb = _each(lambda a, b: a * b, v, b1)
    x = _each(lambda a, b, d: -jnp.where(strict, _nt_lo(a, b) * d, 0.0), kb, k, decay)
    ainv = _unit_inverses(x)
    u = _each(_nn_hi, ainv, vb)
    w = _each(lambda a, b, g: _nn_hi(a, b * jnp.exp(g)), ainv, kb, gc_d)
    attn = _each(lambda a, b, d: jnp.where(causal, _nt_lo(a, b) * d, 0.0), q, k, decay)
    v_new = _each(lambda a, b, s: a - _nn_lo(b, s), u, w, state)
    o = _each(lambda a, g, s, t, vn: _nn_lo(a * jnp.exp(g), s) + _nn_lo(t, vn), q, gc_d, state, attn, v_new)
    gl = _each(lambda g: jnp.sum(jnp.where(last, g, 0.0), axis=0, keepdims=True), gc_d)
    new_state = _each(lambda s, g, a, gd, vn: s * jnp.exp(jnp.sum(g * e0, -1, keepdims=True)) + _tn_lo(a * jnp.exp(g - gd), vn),
                      state, gl, k, gc_d, v_new)
    return o, new_state


def _head_slices(ref, width):
    return [ref[:, h * width:(h + 1) * width] for h in range(GDN_H)]


def gdn_chunk_fwd(qkv, g_b, beta_b, name):
    S_ = qkv.shape[0]
    N = S_ // GDN_C

    def body(q_ref, k_ref, v_ref, g_ref, b_ref, o_ref, s_ref, state):
        @pl.when(pl.program_id(0) == 0)
        def _():
            state[...] = jnp.zeros_like(state)

        st = [state[h] for h in range(GDN_H)]
        s_ref[0] = state[...]
        o, st2 = _gdn_chunk(_head_slices(q_ref, GDN_DK), _head_slices(k_ref, GDN_DK), _head_slices(v_ref, GDN_DV),
                            _head_slices(g_ref, GDN_DK), _head_slices(b_ref, GDN_DK), st)
        for h in range(GDN_H):
            o_ref[:, h * GDN_DV:(h + 1) * GDN_DV] = o[h]
            state[h] = st2[h]

    return _pc(body, grid=(N,),
               in_specs=[pl.BlockSpec((GDN_C, GDN_QK), lambda n: (n, 0)), pl.BlockSpec((GDN_C, GDN_QK), lambda n: (n, 1)),
                         pl.BlockSpec((GDN_C, GDN_V), lambda n: (n, 1)), pl.BlockSpec((GDN_C, GDN_QK), lambda n: (n, 0)),
                         pl.BlockSpec((GDN_C, GDN_QK), lambda n: (n, 0))],
               out_specs=[pl.BlockSpec((GDN_C, GDN_V), lambda n: (n, 0)),
                          pl.BlockSpec((1, GDN_H, GDN_DK, GDN_DV), lambda n: (n, 0, 0, 0))],
               out_shape=[_sds((S_, GDN_V), F32), _sds((N, GDN_H, GDN_DK, GDN_DV), F32)],
               scratch_shapes=[pltpu.VMEM((GDN_H, GDN_DK, GDN_DV), F32)],
               compiler_params=_cparams(("arbitrary",)), name=name)(qkv, qkv, qkv, g_b, beta_b)


def gdn_chunk_bwd(qkv, g_b, beta_b, states, do, name):
    S_ = qkv.shape[0]
    N = S_ // GDN_C

    def body(q_ref, k_ref, v_ref, g_ref, b_ref, s_ref, do_ref, dq_ref, dk_ref, dv_ref, dg_ref, db_ref, dstate):
        @pl.when(pl.program_id(0) == 0)
        def _():
            dstate[...] = jnp.zeros_like(dstate)

        _, vjp = jax.vjp(_gdn_chunk, _head_slices(q_ref, GDN_DK), _head_slices(k_ref, GDN_DK), _head_slices(v_ref, GDN_DV),
                         _head_slices(g_ref, GDN_DK), _head_slices(b_ref, GDN_DK), [s_ref[0, h] for h in range(GDN_H)])
        dq, dk, dv, dg, db, ds = vjp((_head_slices(do_ref, GDN_DV), [dstate[h] for h in range(GDN_H)]))
        for h in range(GDN_H):
            kk, vv = slice(h * GDN_DK, (h + 1) * GDN_DK), slice(h * GDN_DV, (h + 1) * GDN_DV)
            dq_ref[:, kk] = dq[h]
            dk_ref[:, kk] = dk[h]
            dv_ref[:, vv] = dv[h]
            dg_ref[:, kk] = dg[h]
            db_ref[:, kk] = db[h]
            dstate[h] = ds[h]

    r = lambda n: N - 1 - n
    qk = lambda c: pl.BlockSpec((GDN_C, GDN_QK), lambda n: (r(n), c))
    vs = lambda c: pl.BlockSpec((GDN_C, GDN_V), lambda n: (r(n), c))
    return _pc(body, grid=(N,),
               in_specs=[qk(0), qk(1), vs(1), qk(0), qk(0),
                         pl.BlockSpec((1, GDN_H, GDN_DK, GDN_DV), lambda n: (r(n), 0, 0, 0)), vs(0)],
               out_specs=[qk(0), qk(0), vs(0), qk(0), qk(0)],
               out_shape=[_sds((S_, GDN_QK), F32), _sds((S_, GDN_QK), F32), _sds((S_, GDN_V), F32),
                          _sds((S_, GDN_QK), F32), _sds((S_, GDN_QK), F32)],
               scratch_shapes=[pltpu.VMEM((GDN_H, GDN_DK, GDN_DV), F32)],
               compiler_params=_cparams(("arbitrary",)), name=name)(qkv, qkv, qkv, g_b, beta_b, states, do)


def _rope_tables(pos_ref, inv_ref, cm_ref, sg_ref):
    ang = pos_ref[...] * inv_ref[...]
    return jnp.cos(ang) * cm_ref[...], jnp.sin(ang) * sg_ref[...]


def mla_prep_fwd(qpad, kv, proj, pos, rope_consts, name):
    S_ = qpad.shape[0]
    ts = 256
    W = 2 * LANES

    def body(q_ref, kv_ref, kr_ref, pos_ref, inv_ref, cm_ref, sg_ref, qh_ref, kh_ref, vh_ref):
        cs, sn = _rope_tables(pos_ref, inv_ref, cm_ref, sg_ref)
        rope = lambda r: r * cs + pltpu.roll(r, LANES // 2, 1) * sn
        krr = rope(kr_ref[...]).astype(BF16)
        for h in range(MLA_H):
            qh_ref[h, :, 0:LANES] = (q_ref[:, h * W:h * W + LANES] * MLA_SCALE).astype(BF16)
            qh_ref[h, :, LANES:W] = (rope(q_ref[:, h * W + LANES:(h + 1) * W]) * MLA_SCALE).astype(BF16)
            kh_ref[h, :, 0:LANES] = kv_ref[:, h * W:h * W + LANES].astype(BF16)
            kh_ref[h, :, LANES:W] = krr
            vh_ref[h] = kv_ref[:, h * W + LANES:(h + 1) * W].astype(BF16)

    one = pl.BlockSpec((1, LANES), lambda i: (0, 0))
    return _pc(body, grid=(S_ // ts,),
               in_specs=[pl.BlockSpec((ts, MLA_H * W), lambda i: (i, 0)), pl.BlockSpec((ts, MLA_H * W), lambda i: (i, 0)),
                         pl.BlockSpec((ts, LANES), lambda i: (i, 1536 // LANES)), pl.BlockSpec((ts, 1), lambda i: (i, 0)),
                         one, one, one],
               out_specs=[pl.BlockSpec((MLA_H, ts, W), lambda i: (0, i, 0)), pl.BlockSpec((MLA_H, ts, W), lambda i: (0, i, 0)),
                          pl.BlockSpec((MLA_H, ts, LANES), lambda i: (0, i, 0))],
               out_shape=[_sds((MLA_H, S_, W), BF16), _sds((MLA_H, S_, W), BF16), _sds((MLA_H, S_, LANES), BF16)],
               compiler_params=_cparams(("parallel",)), name=name)(qpad, kv, proj, pos, *rope_consts)


def mla_prep_bwd(dqh, dkh, dvh, pos, rope_consts, into, name):
    S_ = dqh.shape[1]
    ts = 256
    W = 2 * LANES

    def body(dq_ref, dk_ref, dv_ref, pos_ref, inv_ref, cm_ref, sg_ref, _, dqp_ref, dkv_ref, dkr_ref):
        cs, sn = _rope_tables(pos_ref, inv_ref, cm_ref, sg_ref)
        rope_t = lambda g: g * cs + pltpu.roll(g * sn, LANES // 2, 1)
        acc = jnp.zeros((ts, LANES), F32)
        for h in range(MLA_H):
            dqp_ref[:, h * W:h * W + LANES] = (dq_ref[h, :, 0:LANES].astype(F32) * MLA_SCALE).astype(BF16)
            dqp_ref[:, h * W + LANES:(h + 1) * W] = (rope_t(dq_ref[h, :, LANES:W].astype(F32)) * MLA_SCALE).astype(BF16)
            dkv_ref[:, h * W:h * W + LANES] = dk_ref[h, :, 0:LANES]
            dkv_ref[:, h * W + LANES:(h + 1) * W] = dv_ref[h]
            acc = acc + dk_ref[h, :, LANES:W].astype(F32)
        dkr_ref[...] = rope_t(acc).astype(dkr_ref.dtype)

    one = pl.BlockSpec((1, LANES), lambda i: (0, 0))
    return _pc(body, grid=(S_ // ts,),
               in_specs=[pl.BlockSpec((MLA_H, ts, W), lambda i: (0, i, 0)), pl.BlockSpec((MLA_H, ts, W), lambda i: (0, i, 0)),
                         pl.BlockSpec((MLA_H, ts, LANES), lambda i: (0, i, 0)), pl.BlockSpec((ts, 1), lambda i: (i, 0)),
                         one, one, one, pl.BlockSpec(memory_space=pl.ANY)],
               out_specs=[pl.BlockSpec((ts, MLA_H * W), lambda i: (i, 0)), pl.BlockSpec((ts, MLA_H * W), lambda i: (i, 0)),
                          pl.BlockSpec((ts, LANES), lambda i: (i, 1536 // LANES))],
               out_shape=[_sds((S_, MLA_H * W), BF16), _sds((S_, MLA_H * W), BF16), _sds(into.shape, into.dtype)],
               input_output_aliases={7: 2}, compiler_params=_cparams(("parallel",)), name=name)(dqh, dkh, dvh, pos, *rope_consts, into)


NEG = -1e30


FLASH_TILE = 1024
FLASH_SUB = 512


def _scores(q, k, diagonal):
    s = lax.dot_general(q, k, (_NT, ((), ())), preferred_element_type=F32)
    if not diagonal:
        return s
    return jnp.where(lax.broadcasted_iota(I32, s.shape, 1) <= lax.broadcasted_iota(I32, s.shape, 0), s, NEG)


def _sub_blocks(t, diagonal):
    sub = min(FLASH_SUB, t) if diagonal else t
    return [(c * sub if diagonal else 0, slice(c * sub, (c + 1) * sub)) for c in range(t // sub)]


FLASH_HEADS = 2


def flash_fwd(qh, kh, vh, name):
    H, S_, W = qh.shape
    t = _tile(S_, FLASH_TILE)
    n = S_ // t
    G = FLASH_HEADS
    heads = list(range(G))

    def body(q_ref, k_ref, v_ref, o_ref, lse_ref, m_s, l_s, acc):
        qi, kj = pl.program_id(1), pl.program_id(2)

        @pl.when(kj == 0)
        def _():
            m_s[...] = jnp.full_like(m_s, NEG)
            l_s[...] = jnp.zeros_like(l_s)
            acc[...] = jnp.zeros_like(acc)

        def step(diagonal):
            s = _each(lambda a: _scores(q_ref[a], k_ref[a], diagonal), heads)
            m_old = _each(lambda a: m_s[a], heads)
            m_new = _each(lambda mo, sa: jnp.maximum(mo, jnp.max(sa, axis=-1, keepdims=True)), m_old, s)
            alpha = _each(lambda mo, mn: jnp.exp(mo - mn), m_old, m_new)
            p = _each(lambda sa, mn: jnp.exp(sa - mn[:, :1]), s, m_new)
            pv = _each(lambda pa, a: lax.dot_general(pa.astype(BF16), v_ref[a], (_NN, ((), ())), preferred_element_type=F32), p, heads)
            for a in heads:
                l_s[a] = alpha[a] * l_s[a] + jnp.sum(p[a], axis=-1, keepdims=True)
                acc[a] = alpha[a] * acc[a] + pv[a]
                m_s[a] = m_new[a]

        pl.when(kj < qi)(lambda: step(False))
        pl.when(kj == qi)(lambda: step(True))

        @pl.when(kj == n - 1)
        def _():
            for a in heads:
                o_ref[:, a * LANES:(a + 1) * LANES] = acc[a] / l_s[a]
                lse_ref[a] = m_s[a] + jnp.log(l_s[a])

    return _pc(body, grid=(H // G, n, n),
               in_specs=[pl.BlockSpec((G, t, W), lambda h, i, j: (h, i, 0)),
                         pl.BlockSpec((G, t, W), lambda h, i, j: (h, jnp.minimum(i, j), 0)),
                         pl.BlockSpec((G, t, LANES), lambda h, i, j: (h, jnp.minimum(i, j), 0))],
               out_specs=[pl.BlockSpec((t, G * LANES), lambda h, i, j: (i, h)), pl.BlockSpec((G, t, LANES), lambda h, i, j: (h, i, 0))],
               out_shape=[_sds((S_, H * LANES), F32), _sds((H, S_, LANES), F32)],
               scratch_shapes=[pltpu.VMEM((G, t, LANES), F32)] * 3,
               compiler_params=_cparams(("parallel", "parallel", "arbitrary")), name=name)(qh, kh, vh)


def flash_bwd(qh, kh, vh, o, lse, do, name):
    H, S_, W = qh.shape
    t = _tile(S_, FLASH_TILE)
    n = S_ // t

    def body(q_ref, k_ref, v_ref, o_ref, lse_ref, do_ref, dq_ref, dk_ref, dv_ref, dq_acc, dk_acc, dv_acc):
        kj, qi = pl.program_id(1), pl.program_id(2)

        @pl.when(jnp.logical_and(kj == 0, qi == 0))
        def _():
            dq_acc[...] = jnp.zeros_like(dq_acc)

        @pl.when(qi == 0)
        def _():
            dk_acc[...] = jnp.zeros_like(dk_acc)
            dv_acc[...] = jnp.zeros_like(dv_acc)

        def step(diagonal):
            do_ = do_ref[...]
            dob = do_.astype(BF16)
            delta = jnp.sum(do_ * o_ref[...], axis=-1, keepdims=True)
            for r0, keys in _sub_blocks(t, diagonal):
                q, k, v = q_ref[r0:, :], k_ref[keys, :], v_ref[keys, :]
                p = jnp.exp(_scores(q, k, diagonal) - lse_ref[r0:, :1])
                dv_acc[keys, :] += lax.dot_general(p.astype(BF16), dob[r0:], (_TN, ((), ())), preferred_element_type=F32)
                dp = lax.dot_general(dob[r0:], v, (_NT, ((), ())), preferred_element_type=F32)
                ds = (p * (dp - delta[r0:])).astype(BF16)
                dk_acc[keys, :] += lax.dot_general(ds, q, (_TN, ((), ())), preferred_element_type=F32)
                rows = pl.ds(pl.multiple_of(qi * t, t) + r0, t - r0)
                dq_acc[rows, :] += lax.dot_general(ds, k, (_NN, ((), ())), preferred_element_type=F32)

        pl.when(qi > kj)(lambda: step(False))
        pl.when(qi == kj)(lambda: step(True))

        @pl.when(qi == n - 1)
        def _():
            dk_ref[...] = dk_acc[...].astype(BF16)
            dv_ref[...] = dv_acc[...].astype(BF16)

        @pl.when(jnp.logical_and(kj == n - 1, qi == n - 1))
        def _():
            dq_ref[...] = dq_acc[...].astype(BF16)

    qrow = lambda h, j, i: jnp.maximum(i, j)
    return _pc(body, grid=(H, n, n),
               in_specs=[pl.BlockSpec((None, t, W), lambda h, j, i: (h, qrow(h, j, i), 0)),
                         pl.BlockSpec((None, t, W), lambda h, j, i: (h, j, 0)),
                         pl.BlockSpec((None, t, LANES), lambda h, j, i: (h, j, 0)),
                         pl.BlockSpec((t, LANES), lambda h, j, i: (qrow(h, j, i), h)),
                         pl.BlockSpec((None, t, LANES), lambda h, j, i: (h, qrow(h, j, i), 0)),
                         pl.BlockSpec((t, LANES), lambda h, j, i: (qrow(h, j, i), h))],
               out_specs=[pl.BlockSpec((None, S_, W), lambda h, j, i: (h, 0, 0)),
                          pl.BlockSpec((None, t, W), lambda h, j, i: (h, j, 0)),
                          pl.BlockSpec((None, t, LANES), lambda h, j, i: (h, j, 0))],
               out_shape=[_sds((H, S_, W), BF16), _sds((H, S_, W), BF16), _sds((H, S_, LANES), BF16)],
               scratch_shapes=[pltpu.VMEM((S_, W), F32), pltpu.VMEM((t, W), F32), pltpu.VMEM((t, LANES), F32)],
               compiler_params=_cparams(("parallel", "arbitrary", "arbitrary")), name=name)(qh, kh, vh, o, lse, do)


def loss_head(x, target, g, name):
    S_ = x.shape[0]
    ts = 256

    def body(x_ref, t_ref, g_ref, l_ref, dx_ref, dg_ref):
        @pl.when(pl.program_id(0) == 0)
        def _():
            l_ref[...] = jnp.zeros_like(l_ref)
            dg_ref[...] = jnp.zeros_like(dg_ref)

        y, vjp = jax.vjp(_rms, x_ref[...], g_ref[...])
        err = y - t_ref[...]
        l_ref[...] += 0.5 * jnp.sum(jnp.sum(err * err, axis=-1, keepdims=True), axis=0, keepdims=True) / D
        dx, dg = vjp(err / D)
        dx_ref[...] = dx
        dg_ref[...] += dg

    row = pl.BlockSpec((ts, D), lambda i: (i, 0))
    return _pc(body, grid=(S_ // ts,), in_specs=[row, row, pl.BlockSpec((1, D), lambda i: (0, 0))],
               out_specs=[pl.BlockSpec((1, LANES), lambda i: (0, 0)), row, pl.BlockSpec((1, D), lambda i: (0, 0))],
               out_shape=[_sds((1, LANES), F32), _sds((S_, D), F32), _sds((1, D), F32)],
               compiler_params=_cparams(("arbitrary",)), name=name)(x, target, g)


def adamw(w, parts, m, v, name):
    R, C = w.shape
    rows = [p.shape[1] for p in parts[0]]
    tr = R
    for cand in (512, 256, 128, 64, 32, 16, 8):
        if all(r % cand == 0 for r in rows) and cand * C * 4 <= 1024 * 1024:
            tr = cand
            break
    c1 = 1.0 - ADAM_B1 ** ADAM_STEP
    c2 = 1.0 - ADAM_B2 ** ADAM_STEP
    starts = [sum(rows[:k]) // tr for k in range(len(rows))]
    flat = [p for part in parts for p in part]

    def body(*refs):
        w_ref, m_ref, v_ref = refs[0], refs[1 + len(flat)], refs[2 + len(flat)]
        g_ref, d_ref, nm_ref, nv_ref = refs[3 + len(flat):]
        i = pl.program_id(0)
        gg, at = None, 1
        for part in parts:
            val = None
            for k in range(len(part)):
                p_ref = refs[at]
                at += 1
                s = p_ref[0].astype(F32)
                for n in range(1, p_ref.shape[0]):
                    s = s + p_ref[n].astype(F32)
                val = s if val is None else jnp.where(i >= starts[k], s, val)
            gg = val if gg is None else gg + val
        m2 = ADAM_B1 * m_ref[...] + (1.0 - ADAM_B1) * gg
        v2 = ADAM_B2 * v_ref[...] + (1.0 - ADAM_B2) * (gg * gg)
        g_ref[...] = gg
        d_ref[...] = -ADAM_LR * ((m2 / c1) / (jnp.sqrt(v2 / c2) + ADAM_EPS) + ADAM_WD * w_ref[...])
        nm_ref[...] = m2
        nv_ref[...] = v2

    blk = pl.BlockSpec((tr, C), lambda i: (i, 0))
    piece = lambda p, k: pl.BlockSpec((p.shape[0], tr, C), lambda i: (0, jnp.clip(i - starts[k], 0, rows[k] // tr - 1), 0))
    pblk = [piece(p, k) for part in parts for k, p in enumerate(part)]
    return _pc(body, grid=(R // tr,), in_specs=[blk] + pblk + [blk, blk], out_specs=[blk] * 4, out_shape=[_sds((R, C), F32)] * 4,
               compiler_params=_cparams(("parallel",)), name=name)(w, *flat, m, v)


def sum_slots(own, recv, skip, name):
    n, R, C = recv.shape
    tr = _tile(R, 512) if R % LANES == 0 else R
    has_own = own is not None

    def body(*refs):
        skip_ref = refs[0]
        r_ref, o_ref = refs[-2], refs[-1]
        acc = refs[1][...] if has_own else jnp.zeros(o_ref.shape, F32)
        for s in range(n):
            acc = acc + jnp.where(skip_ref[0] == s, 0.0, r_ref[s].astype(F32))
        o_ref[...] = acc

    row = pl.BlockSpec((tr, C), lambda i, sk: (i, 0))
    gs = pltpu.PrefetchScalarGridSpec(
        num_scalar_prefetch=1, grid=(R // tr,),
        in_specs=([row] if has_own else []) + [pl.BlockSpec((n, tr, C), lambda i, sk: (0, i, 0))], out_specs=row)
    ins = ([own] if has_own else []) + [recv]
    return _pc(body, grid_spec=gs, out_shape=_sds((R, C), F32), compiler_params=_cparams(("parallel",)), name=name)(skip, *ins)


def _chip_peers():
    x, y, c = lax.axis_index("x"), lax.axis_index("y"), lax.axis_index("c")
    return (x, y, c), [(1 - x, y, c), (x, 1 - y, c), (1 - x, 1 - y, c)]


def _chip_index(p):
    return 2 * p[0] + p[1]


def _win(ref, axis, chip, size):
    if axis is None:
        return ref.at[chip]
    idx = [slice(None)] * len(ref.shape)
    idx[axis] = pl.ds(pl.multiple_of(chip * size, size), size)
    return ref.at[tuple(idx)]


def _remote(src, dst, send_sem, recv_sem, peer):
    return pltpu.make_async_remote_copy(src_ref=src, dst_ref=dst, send_sem=send_sem, recv_sem=recv_sem, device_id=peer,
                                        device_id_type=MESH)


HBM_SPEC = pl.BlockSpec(memory_space=pltpu.HBM)
SEM_SPEC = pl.BlockSpec(memory_space=pltpu.SEMAPHORE)
ANY_SPEC = pl.BlockSpec(memory_space=pl.ANY)
DATAFLOW = pltpu.SideEffectType.DATAFLOW_SIDE_EFFECTING


def gather_piece(i, l, o, axis, size):
    return (i, lambda r, chip: r.at[l], o, lambda r, chip: _win(r, axis, chip, size))


def scatter_piece(i, o, axis, size):
    return (i, lambda r, chip: _win(r, axis, chip, size), o, lambda r, chip: r.at[chip])


def whole_piece(i):
    return (i, lambda r, chip: r, i, lambda r, chip: r)


def _copies(pieces, in_refs, out_refs, send, recv, sibling):
    me, peers = _chip_peers()
    if sibling:
        peers = [(me[0], me[1], 1 - me[2])]
    mine = _chip_index(me)
    remote = []
    for n, (i, src, o, dst) in enumerate(pieces):
        d = dst(out_refs[o], mine)
        remote += [_remote(src(in_refs[i], _chip_index(p)), d, send.at[len(peers) * n + k], recv.at[len(peers) * n + k], p)
                   for k, p in enumerate(peers)]
    return remote


def own_window(a, axis, size, chip):
    if axis is None:
        return lax.dynamic_index_in_dim(a, chip, 0, keepdims=False)
    return lax.dynamic_slice_in_dim(a, chip * size, size, axis=axis)


def place_own(land, own, axis, size, chip):
    if axis is None:
        return lax.dynamic_update_slice_in_dim(land, own[None], chip, axis=0)
    return lax.dynamic_update_slice_in_dim(land, own, chip * size, axis=axis)


def exchange_start(pieces, ins, out_shapes, after, name, sibling=False):
    n_in, n_out, ncp = len(ins), len(out_shapes), len(pieces)

    def body(*refs):
        in_refs, land_refs = refs[:n_in], refs[n_in:n_in + n_out]
        send, recv = refs[n_in + n_out + 1], refs[n_in + n_out + 2]
        token = refs[-1]
        for cp in _copies(pieces, in_refs, land_refs, send, recv, sibling):
            cp.start()
        token[...] = jnp.zeros_like(token)

    hbm = lambda a: pltpu.with_memory_space_constraint(a, pltpu.HBM)
    lands = [hbm(lax.empty(s.shape, s.dtype)) for s in out_shapes]
    sem = pltpu.SemaphoreType.DMA(((1 if sibling else 3) * ncp,))
    thru = [pltpu.HBM(a.shape, a.dtype) for a in ins] + [pltpu.HBM(s.shape, s.dtype) for s in out_shapes]
    res = _pc(body, in_specs=[HBM_SPEC] * (n_in + n_out) + [ANY_SPEC],
              out_specs=[SEM_SPEC, SEM_SPEC] + [HBM_SPEC] * (n_in + n_out) + [pl.BlockSpec(memory_space=pltpu.VMEM)],
              out_shape=[sem, sem] + thru + [_sds((8, LANES), F32)],
              input_output_aliases={i: 2 + i for i in range(n_in + n_out)},
              compiler_params=pltpu.CompilerParams(has_side_effects=DATAFLOW), name=name)(*[hbm(a) for a in ins], *lands, after)
    return (res[0], res[1]), list(res[2:2 + n_in]), list(res[2 + n_in:2 + n_in + n_out]), res[-1]


def exchange_wait(pieces, sems, ins, lands, after, name, sibling=False):
    n_in, n_out = len(ins), len(lands)

    def body(*refs):
        in_refs, land_refs = refs[:n_in], refs[n_in:n_in + n_out]
        send, recv = refs[n_in + n_out], refs[n_in + n_out + 1]
        for cp in _copies(pieces, in_refs, land_refs, send, recv, sibling):
            cp.wait_send()
            cp.wait_recv()

    thru = [pltpu.HBM(a.shape, a.dtype) for a in ins] + [pltpu.HBM(a.shape, a.dtype) for a in lands]
    res = _pc(body, in_specs=[HBM_SPEC] * (n_in + n_out) + [SEM_SPEC, SEM_SPEC, ANY_SPEC], out_specs=[HBM_SPEC] * (n_in + n_out),
              out_shape=thru, input_output_aliases={i: i for i in range(n_in + n_out)},
              compiler_params=pltpu.CompilerParams(has_side_effects=DATAFLOW), name=name)(*ins, *lands, sems[0], sems[1], after)
    return list(res[:n_in]), list(res[n_in:])


def exchange_all(buf, name):
    def body(in_ref, out_ref, send, recv, local):
        x, y, c = lax.axis_index("x"), lax.axis_index("y"), lax.axis_index("c")
        mine = 4 * x + 2 * y + c
        loc = pltpu.make_async_copy(in_ref, out_ref.at[mine], local)
        loc.start()
        copies = [loc]
        for k in range(1, 8):
            peer = (x ^ (k >> 2), y ^ ((k >> 1) & 1), c ^ (k & 1))
            cp = pltpu.make_async_remote_copy(src_ref=in_ref, dst_ref=out_ref.at[mine], send_sem=send.at[k - 1],
                                              recv_sem=recv.at[k - 1], device_id=peer, device_id_type=MESH)
            cp.start()
            copies.append(cp)
        for cp in copies:
            cp.wait()

    anyspec = pl.BlockSpec(memory_space=pl.ANY)
    return _pc(body, in_specs=[anyspec], out_specs=anyspec, out_shape=_sds((8,) + buf.shape, buf.dtype),
               scratch_shapes=[pltpu.SemaphoreType.DMA((7,)), pltpu.SemaphoreType.DMA((7,)), pltpu.SemaphoreType.DMA],
               name=name)(buf)


def _norm_fwd(x, g, name):
    return rowwise(f_rms, [(x, D, 0, 0)], [(g, D, 0, 0)], [(D, 0, BF16)], ts=512, name=name)[0]


def _norm_bwd(x, g, dh, dres, name):
    (dx,), (dg,) = rowwise_bwd(f_rms, [(x, D, 0, 0)], [(g, D, 0, 0)], [(dh, D, 0, 0)], need=[True],
                               adds={0: (dres, D, 0, 0)}, ts=256, name=name)
    return dx, dg


def pool_fwd(x, W, tag):
    h = _norm_fwd(x, W["ng"], tag + "_norm")
    proj = mm(h, W["w_in"], name=tag + "_in")
    p = pool_time_fwd(proj, tag + "_win")
    pg = gmm("nn", p, W["w_grp"], G=4, name=tag + "_grp")
    y = rowwise(f_pool_gate, [(pg, POOL_GROUP, 0, 1), (proj, POOL_GROUP, 4, 1)], [(W["scale"], POOL_GROUP, 0, 1)],
                [(POOL_GROUP, 1, BF16)], ncol=4, ts=512, name=tag + "_gate")[0]
    xn = mm(y, W["w_out"], add=x, name=tag + "_out")
    return xn, (x, h, proj, p, pg, y)


def pool_bwd(dxn, W, saved, tag, after=None, emit=None):
    x, h, proj, p, pg, y = saved
    emit = emit or (lambda grads: None)
    dy = mm(dxn, W["w_out"], tb=True, after=after, name=tag + "_dy")
    g = {}
    (dpg, dproj), (g["scale"],) = rowwise_bwd(
        f_pool_gate, [(pg, POOL_GROUP, 0, 1), (proj, POOL_GROUP, 4, 1)], [(W["scale"], POOL_GROUP, 0, 1)],
        [(dy, POOL_GROUP, 0, 1)], need=[True, True], place={1: (2 * POOL_WIDTH, 4)}, narrow=(0, 1), ncol=4, ts=512, name=tag + "_dgate")
    dp = gmm("nt", dpg, W["w_grp"], G=4, name=tag + "_dp")
    dproj = pool_time_bwd(dp, dproj, tag + "_dwin")
    g["w_in"] = mm(h, dproj, ta=True, out_dtype=BF16, name=tag + "_dw_in")
    t1 = emit({"w_in": g["w_in"]})
    g["w_out"] = mm(y, dxn, ta=True, after=t1, out_dtype=BF16, name=tag + "_dwout")
    g["w_grp"] = gmm("tn", p, dpg, G=4, out_dtype=BF16, name=tag + "_dwgrp")
    t2 = emit({"w_out": g["w_out"], "w_grp": g["w_grp"]})
    dh = mm(dproj, W["w_in"], tb=True, after=t2, name=tag + "_dh")
    dx, g["ng"] = _norm_bwd(x, W["ng"], dh, dxn, tag + "_dnorm")
    return dx, g


def gdn_fwd(x, W, tag):
    h = _norm_fwd(x, W["ng"], tag + "_norm")
    proj = mm(h, W["w_in"], name=tag + "_in")
    qkv = gdn_conv_fwd(proj, W["conv"], tag + "_conv")
    g_b, beta_b = rowwise(f_gdn_gates, [(proj, LANES, 6144 // LANES, 0)], [(W["a_log"], LANES, 0, 0), (W["dt_bias"], LANES, 0, 0)],
                          [(GDN_QK, 0, F32), (GDN_QK, 0, F32)], ts=512, name=tag + "_gates")
    o, states = gdn_chunk_fwd(qkv, g_b, beta_b, tag + "_chunk")
    og = rowwise(f_gdn_out, [(o, GDN_DV, 0, 1), (proj, GDN_DV, 4096 // GDN_DV, 1)], [(W["norm_g"], GDN_DV, 0, 0)],
                 [(GDN_DV, 1, BF16)], ncol=GDN_H, ts=512, name=tag + "_onorm")[0]
    xn = mm(og, W["w_out"], add=x, name=tag + "_out")
    return xn, (x, h, proj, qkv, g_b, beta_b, o, states, og)


def gdn_bwd(dxn, W, saved, tag, after=None):
    x, h, proj, qkv, g_b, beta_b, o, states, og = saved
    dog = mm(dxn, W["w_out"], tb=True, after=after, name=tag + "_dog")
    g = {"w_out": mm(og, dxn, ta=True, out_dtype=BF16, name=tag + "_dwout")}
    (do, dproj), (g["norm_g"],) = rowwise_bwd(
        f_gdn_out, [(o, GDN_DV, 0, 1), (proj, GDN_DV, 4096 // GDN_DV, 1)], [(W["norm_g"], GDN_DV, 0, 0)],
        [(dog, GDN_DV, 0, 1)], need=[True, True], place={1: (GDN_IN_PAD, 4096 // GDN_DV)}, narrow=(1,), ncol=GDN_H, ts=512, name=tag + "_donorm")
    dq, dk, dv, dg_b, dbeta_b = gdn_chunk_bwd(qkv, g_b, beta_b, states, do, tag + "_dchunk")
    (dproj,), (g["a_log"], g["dt_bias"]) = rowwise_bwd(
        f_gdn_gates, [(proj, LANES, 6144 // LANES, 0)], [(W["a_log"], LANES, 0, 0), (W["dt_bias"], LANES, 0, 0)],
        [(dg_b, GDN_QK, 0, 0), (dbeta_b, GDN_QK, 0, 0)], need=[True], place={0: (dproj, 6144 // LANES)}, ts=256, name=tag + "_dgates")
    dproj, g["conv"] = gdn_conv_bwd(proj, W["conv"], dq, dk, dv, dproj, tag + "_dconv")
    dh = mm(dproj, W["w_in"], tb=True, name=tag + "_dh")
    g["w_in"] = mm(h, dproj, ta=True, out_dtype=BF16, name=tag + "_dw_in")
    dx, g["ng"] = _norm_bwd(x, W["ng"], dh, dxn, tag + "_dnorm")
    return dx, g


def mla_fwd(x, pos, W, tag):
    h = _norm_fwd(x, W["ng"], tag + "_norm")
    proj = mm(h, W["w_in"], name=tag + "_in")
    hq = rowwise(f_rms, [(proj, MLA_Q_LORA, 0, 0)], [(W["q_g"], MLA_Q_LORA, 0, 0)], [(MLA_Q_LORA, 0, BF16)], ts=512, name=tag + "_qnorm")[0]
    hkv = rowwise(f_rms, [(proj, MLA_KV_LORA, 2, 0)], [(W["kv_g"], MLA_KV_LORA, 0, 0)], [(MLA_KV_LORA, 0, BF16)], ts=512, name=tag + "_kvnorm")[0]
    qpad = mm(hq, W["w_uq"], name=tag + "_uq")
    kv = mm(hkv, W["w_ukv"], name=tag + "_ukv")
    qh, kh, vh = mla_prep_fwd(qpad, kv, proj, pos, W["rope"], tag + "_prep")
    o, lse = flash_fwd(qh, kh, vh, tag + "_attn")
    og = rowwise(f_ogate, [(o, 512, 0, 1), (proj, 512, 4, 1)], [], [(512, 1, BF16)], ncol=4, ts=512, name=tag + "_ogate")[0]
    xn = mm(og, W["w_out"], add=x, name=tag + "_out")
    return xn, (x, h, proj, hq, hkv, qh, kh, vh, o, lse, og)


def mla_bwd(dxn, pos, W, saved, tag, after=None):
    x, h, proj, hq, hkv, qh, kh, vh, o, lse, og = saved
    dog = mm(dxn, W["w_out"], tb=True, after=after, name=tag + "_dog")
    g = {"w_out": mm(og, dxn, ta=True, out_dtype=BF16, name=tag + "_dwout")}
    dproj = jnp.zeros(proj.shape, BF16)
    (do, dproj), _ = rowwise_bwd(f_ogate, [(o, 512, 0, 1), (proj, 512, 4, 1)], [], [(dog, 512, 0, 1)], need=[True, True],
                                 place={1: (dproj, 4)}, ncol=4, ts=512, name=tag + "_dogate")
    dqh, dkh, dvh = flash_bwd(qh, kh, vh, o, lse, do, tag + "_dattn")
    dqpad, dkv, dproj = mla_prep_bwd(dqh, dkh, dvh, pos, W["rope"], dproj, tag + "_dprep")
    dhq = mm(dqpad, W["w_uq"], tb=True, name=tag + "_dhq")
    g["w_uq"] = mm(hq, dqpad, ta=True, out_dtype=BF16, name=tag + "_dwuq")
    dhkv = mm(dkv, W["w_ukv"], tb=True, name=tag + "_dhkv")
    g["w_ukv"] = mm(hkv, dkv, ta=True, out_dtype=BF16, name=tag + "_dwukv")
    (dproj,), (g["q_g"],) = rowwise_bwd(f_rms, [(proj, MLA_Q_LORA, 0, 0)], [(W["q_g"], MLA_Q_LORA, 0, 0)], [(dhq, MLA_Q_LORA, 0, 0)],
                                        need=[True], place={0: (dproj, 0)}, ts=256, name=tag + "_dqnorm")
    (dproj,), (g["kv_g"],) = rowwise_bwd(f_rms, [(proj, MLA_KV_LORA, 2, 0)], [(W["kv_g"], MLA_KV_LORA, 0, 0)], [(dhkv, MLA_KV_LORA, 0, 0)],
                                         need=[True], place={0: (dproj, 2)}, ts=256, name=tag + "_dkvnorm")
    dh = mm(dproj, W["w_in"], tb=True, name=tag + "_dh")
    g["w_in"] = mm(h, dproj, ta=True, out_dtype=BF16, name=tag + "_dw_in")
    dx, g["ng"] = _norm_bwd(x, W["ng"], dh, dxn, tag + "_dnorm")
    return dx, g


def _pad_cols(a, n):
    return jnp.pad(a, ((0, 0), (0, n - a.shape[1])))


def _mla_w_in_layout(w):
    z = lambda n: jnp.zeros((w.shape[0], n), w.dtype)
    kr = w[:, 1280:1344]
    return jnp.concatenate([w[:, :768], z(256), w[:, 768:1280], kr[:, :32], z(32), kr[:, 32:], z(32), z(384), w[:, 1344:]], axis=1)


def _mla_w_in_unlayout(g):
    return jnp.concatenate([g[:, :768], g[:, 1024:1536], g[:, 1536:1568], g[:, 1600:1632], g[:, 2048:]], axis=1)


def _mla_w_uq_layout(w):
    w3 = w.reshape(w.shape[0], MLA_H, MLA_NOPE + MLA_ROPE)
    z = jnp.zeros((w.shape[0], MLA_H, 32), w.dtype)
    return jnp.concatenate([w3[..., :128], w3[..., 128:160], z, w3[..., 160:192], z], axis=-1).reshape(w.shape[0], MLA_H * 256)


def _mla_w_uq_unlayout(g):
    g3 = g.reshape(g.shape[0], MLA_H, 256)
    return jnp.concatenate([g3[..., :128], g3[..., 128:160], g3[..., 192:224]], axis=-1).reshape(g.shape[0], MLA_H * 192)


def _rope_consts():
    half = MLA_ROPE // 2
    inv = ROPE_THETA ** (-jnp.arange(half, dtype=F32) / half)
    z = jnp.zeros((half,), F32)
    o = jnp.ones((half,), F32)
    row = lambda *p: jnp.concatenate(p).reshape(1, LANES)
    return row(inv, z, inv, z), row(o, z, o, z), row(-o, z, o, z)


BIG = ["pool_w_in", "pool_w_grp", "pool_w_out", "gdn_w_in", "gdn_w_out", "mla_w_in", "mla_w_uq", "mla_w_ukv", "mla_w_out"]
BIG_LAYOUT = {"pool_w_in": (1, 1024, (1024, 4096)), "pool_w_grp": (1, 128, (4, 512, 512)), "pool_w_out": (0, 512, (2048, 1024)),
              "gdn_w_in": (None, None, (4, 1024, 1540)), "gdn_w_out": (0, 512, (2048, 1024)),
              "mla_w_in": (None, None, (4, 1024, 848)), "mla_w_uq": (1, 768, (768, 3072)), "mla_w_ukv": (1, 1024, (512, 4096)),
              "mla_w_out": (0, 512, (2048, 1024))}
SMALL_SHARDED = ["pool_scale", "gdn_conv", "mla_q_norm_g", "mla_kv_norm_g"]
SMALL_AXIS = {"pool_scale": 1, "gdn_conv": 2, "mla_q_norm_g": 1, "mla_kv_norm_g": 1}
REPLICATED = ["norm_g", "gdn_a_log", "gdn_dt_bias", "gdn_norm_g", "final_g"]
PACK_C = 1024


def _pack(parts, dtype, row_mult):
    flat = jnp.concatenate([p.reshape(-1).astype(dtype) for p in parts])
    rows = -(-flat.shape[0] // PACK_C)
    rows = -(-rows // row_mult) * row_mult
    return jnp.pad(flat, (0, rows * PACK_C - flat.shape[0])).reshape(rows, PACK_C)


def _unpack(buf, shapes):
    lead = buf.shape[:-2]
    flat = buf.reshape(lead + (-1,))
    out, off = [], 0
    for s in shapes:
        n = int(np.prod(s))
        out.append(flat[..., off:off + n].reshape(lead + tuple(s)))
        off += n
    return out


def _unshard(g4, axis):
    a = jnp.moveaxis(g4, 0, axis)
    s = a.shape
    return a.reshape(s[:axis] + (s[axis] * s[axis + 1],) + s[axis + 2:])


def _to_shards(a, axis):
    s = a.shape
    return jnp.moveaxis(a.reshape(s[:axis] + (4, s[axis] // 4) + s[axis + 1:]), axis, 0)


def layer_weights(full, small, rep, layer):
    ng = rep["norm_g"][layer:layer + 1]
    side_by_side = lambda a4: jnp.moveaxis(a4, 0, 1).reshape(a4.shape[1], 4 * a4.shape[2])
    if layer in (0, 3):
        j = layer // 3
        return dict(ng=ng, w_in=full[("pool_w_in", j)], w_grp=full[("pool_w_grp", j)], scale=small["pool_scale"][j:j + 1],
                    w_out=full[("pool_w_out", j)])
    if layer == 1:
        return dict(ng=ng, w_in=_pad_cols(side_by_side(full[("gdn_w_in", 0)]), GDN_IN_PAD),
                    conv=jnp.pad(small["gdn_conv"][0], ((0, 4), (0, 0))), a_log=_pad_cols(rep["gdn_a_log"], LANES),
                    dt_bias=_pad_cols(rep["gdn_dt_bias"], LANES), norm_g=rep["gdn_norm_g"], w_out=full[("gdn_w_out", 0)])
    return dict(ng=ng, w_in=_mla_w_in_layout(side_by_side(full[("mla_w_in", 0)])), q_g=small["mla_q_norm_g"],
                kv_g=small["mla_kv_norm_g"], w_uq=_mla_w_uq_layout(full[("mla_w_uq", 0)]), w_ukv=full[("mla_w_ukv", 0)],
                w_out=full[("mla_w_out", 0)], rope=_rope_consts())


def big_grad_pieces(gl):
    g0, g1, g2, g3 = gl
    slots = lambda a: jnp.moveaxis(a.reshape(a.shape[0], 4, a.shape[1] // 4), 1, 0)
    out = {}
    for l, g in ((0, g0), (1, g3)):
        if g is not None:
            out.update({("pool_w_in", l): g["w_in"], ("pool_w_grp", l): g["w_grp"], ("pool_w_out", l): g["w_out"]})
    if g1 is not None:
        out.update({("gdn_w_in", 0): slots(g1["w_in"][:, :GDN_IN]), ("gdn_w_out", 0): g1["w_out"]})
    if g2 is not None:
        out.update({("mla_w_in", 0): slots(_mla_w_in_unlayout(g2["w_in"])), ("mla_w_uq", 0): _mla_w_uq_unlayout(g2["w_uq"]),
                    ("mla_w_ukv", 0): g2["w_ukv"], ("mla_w_out", 0): g2["w_out"]})
    return out


def small_grads(gl, dfinal):
    g0, g1, g2, g3 = gl
    return {"norm_g": jnp.concatenate([g0["ng"], g1["ng"], g2["ng"], g3["ng"]], axis=0),
            "pool_scale": jnp.concatenate([g0["scale"], g3["scale"]], axis=0), "gdn_conv": g1["conv"][None, :4],
            "gdn_a_log": g1["a_log"][:, :GDN_H], "gdn_dt_bias": g1["dt_bias"][:, :GDN_H], "gdn_norm_g": g1["norm_g"],
            "mla_q_norm_g": g2["q_g"], "mla_kv_norm_g": g2["kv_g"], "final_g": dfinal.reshape(D)}


NAMES = ["norm_g", "pool_w_in", "pool_w_grp", "pool_scale", "pool_w_out", "gdn_w_in", "gdn_conv", "gdn_a_log", "gdn_dt_bias",
         "gdn_norm_g", "gdn_w_out", "mla_w_in", "mla_q_norm_g", "mla_w_uq", "mla_kv_norm_g", "mla_w_ukv", "mla_w_out", "final_g"]


def kernel(x, positions, norm_g, pool_w_in, pool_w_grp, pool_scale, pool_w_out, gdn_w_in, gdn_conv, gdn_a_log, gdn_dt_bias, gdn_norm_g, gdn_w_out, mla_w_in, mla_q_norm_g, mla_w_uq, mla_kv_norm_g, mla_w_ukv, mla_w_out, final_g, loss_target, m_norm_g, m_pool_w_in, m_pool_w_grp, m_pool_scale, m_pool_w_out, m_gdn_w_in, m_gdn_conv, m_gdn_a_log, m_gdn_dt_bias, m_gdn_norm_g, m_gdn_w_out, m_mla_w_in, m_mla_q_norm_g, m_mla_w_uq, m_mla_kv_norm_g, m_mla_w_ukv, m_mla_w_out, m_final_g, v_norm_g, v_pool_w_in, v_pool_w_grp, v_pool_scale, v_pool_w_out, v_gdn_w_in, v_gdn_conv, v_gdn_a_log, v_gdn_dt_bias, v_gdn_norm_g, v_gdn_w_out, v_mla_w_in, v_mla_q_norm_g, v_mla_w_uq, v_mla_kv_norm_g, v_mla_w_ukv, v_mla_w_out, v_final_g):
    args = locals()
    w = {n: args[n] for n in NAMES}
    m = {n: args["m_" + n] for n in NAMES}
    v = {n: args["v_" + n] for n in NAMES}
    my_chip = (2 * lax.axis_index("x") + lax.axis_index("y")).astype(I32)
    S_ = x.shape[1]
    x0, pos, target = x[0], positions.reshape(S_, 1).astype(F32), loss_target[0]
    rep = {n: w[n] for n in REPLICATED}

    shard = {(n, l): w[n][l:l + 1].astype(BF16) for n in BIG for l in range(w[n].shape[0])}
    small_shapes = [w[n].shape for n in SMALL_SHARDED]
    shard[("small", 0)] = _pack([w[n] for n in SMALL_SHARDED], F32, 8)[None]
    layout = dict(BIG_LAYOUT, small=(None, None, (4,) + shard[("small", 0)].shape[1:]))

    def gather_start(group, after, tag):
        pieces = [gather_piece(i, 0, i, layout[n][0], layout[n][1]) for i, (n, l) in enumerate(group)]
        shapes = [_sds(layout[n][2], shard[(n, l)].dtype) for n, l in group]
        sems, ins, lands, token = exchange_start(pieces, [shard[k] for k in group], shapes, after, tag + "_start")
        return (pieces, sems, ins, lands), token

    def finish(handle, after, tag):
        return exchange_wait(*handle, after, tag + "_wait")

    def gathered(group, handle, after, tag):
        srcs, lands = finish(handle, after, tag)
        return {(n, l): place_own(a, s[0], layout[n][0], layout[n][1], my_chip) for (n, l), s, a in zip(group, srcs, lands)}

    tied = lambda a, token: a + token[0:1, 0:1]
    group_a = [("pool_w_in", 0), ("pool_w_grp", 0), ("pool_w_out", 0), ("small", 0)]
    group_b = [("gdn_w_in", 0), ("gdn_w_out", 0)]
    group_c = [("mla_w_in", 0), ("mla_w_uq", 0), ("mla_w_ukv", 0), ("mla_w_out", 0), ("pool_w_in", 1), ("pool_w_grp", 1), ("pool_w_out", 1)]
    full = {}
    h_a, t_a = gather_start(group_a, x0, "gather_a")
    full.update(gathered(group_a, h_a, t_a, "gather_a"))
    small = {n: _unshard(a, SMALL_AXIS[n]) for n, a in zip(SMALL_SHARDED, _unpack(full[("small", 0)], small_shapes))}
    h_b, t_b = gather_start(group_b, full[group_a[0]], "gather_b")
    W0 = layer_weights(full, small, rep, 0)
    x1, s0 = pool_fwd(x0, dict(W0, ng=tied(W0["ng"], t_b)), "l0")
    full.update(gathered(group_b, h_b, x1, "gather_b"))
    h_c, t_c = gather_start(group_c, full[group_b[0]], "gather_c")
    W1 = layer_weights(full, small, rep, 1)
    x2, s1 = gdn_fwd(x1, dict(W1, ng=tied(W1["ng"], t_c)), "l1")
    full.update(gathered(group_c, h_c, x2, "gather_c"))
    W2, W3 = layer_weights(full, small, rep, 2), layer_weights(full, small, rep, 3)
    x3, s2 = mla_fwd(x2, pos, W2, "l2")
    x4, s3 = pool_fwd(x3, W3, "l3")
    loss_part, dx4, dfinal = loss_head(x4, target, final_g.reshape(1, D), "loss_head")

    def scatter_start(pieces_of, after, tag):
        keys = list(pieces_of)
        pieces = [scatter_piece(i, i, BIG_LAYOUT[n][0], BIG_LAYOUT[n][1]) for i, (n, l) in enumerate(keys)]
        shapes = [_sds((4,) + tuple(w[n].shape[1:]), BF16) for n, l in keys]
        sems, ins, lands, token = exchange_start(pieces, [pieces_of[k] for k in keys], shapes, after, tag + "_start")
        return keys, (pieces, sems, ins, lands), token

    def scattered(keys, handle, after, tag):
        srcs, lands = finish(handle, after, tag)
        return {(n, l): place_own(a, own_window(g, BIG_LAYOUT[n][0], BIG_LAYOUT[n][1], my_chip), None, None, my_chip)
                for (n, l), g, a in zip(keys, srcs, lands)}

    dx3, g3 = pool_bwd(dx4, W3, s3, "l3")
    k3, h3, t3 = scatter_start(big_grad_pieces((None, None, None, g3)), dx3, "scatter_l3")
    dx2, g2 = mla_bwd(dx3, pos, W2, s2, "l2", after=t3)
    k2, h2, t2 = scatter_start(big_grad_pieces((None, None, g2, None)), dx2, "scatter_l2")
    dx1, g1 = gdn_bwd(dx2, W1, s1, "l1", after=t2)
    k1, h1, t1 = scatter_start(big_grad_pieces((None, g1, None, None)), dx1, "scatter_l1")
    def swap_start(part, tag):
        keys = list(part)
        ins = [part[k] for k in keys]
        pieces = [whole_piece(i) for i in range(len(keys))]
        sems, ins, lands, token = exchange_start(pieces, ins, [_sds(a.shape, a.dtype) for a in ins], ins[0], tag + "_start", sibling=True)
        swaps.append((keys, (pieces, sems, ins, lands), tag))
        return token

    last, swaps = [], []

    def emit_l0(grads):
        first = not last
        now = next(iter(grads.values()))
        early = [(k3, h3, "scatter_l3"), (k2, h2, "scatter_l2")] if first else [(k1, h1, "scatter_l1")]
        landed = {}
        for keys, handle, tag in early:
            landed.update(scattered(keys, handle, now, tag))
        swapping = swap_start(landed, "swap_a" if first else "swap_b")
        tag = "scatter_l0a" if first else "scatter_l0b"
        keys, handle, token = scatter_start({("pool_" + k, 0): a for k, a in grads.items()}, swapping, tag)
        last.append((keys, handle, tag))
        return token

    dx0, g0 = pool_bwd(dx1, W0, s0, "l0", after=t1, emit=emit_l0)
    landed = {}
    for keys, handle, tag in last:
        landed.update(scattered(keys, handle, dx0, tag))
    swap_start(landed, "swap_c")
    recv, sib = {}, {}
    for keys, handle, tag in swaps:
        mine_, theirs = exchange_wait(*handle, dx0, tag + "_wait", sibling=True)
        recv.update(zip(keys, mine_))
        sib.update(zip(keys, theirs))

    sg = small_grads((g0, g1, g2, g3), dfinal)
    small_names = SMALL_SHARDED + REPLICATED
    small_buf = _pack([sg[n] for n in small_names] + [loss_part], F32, 8)
    small_sum = sum_slots(None, exchange_all(small_buf, "gather_small"), jnp.full((1,), -1, I32), "sum_small")
    full_small = _unpack(small_sum, [sg[n].shape for n in small_names] + [(1, LANES)])
    loss = full_small[-1][0, 0]
    small_part = {}
    for n, a in zip(small_names, full_small[:-1]):
        if n in SMALL_AXIS:
            a = lax.dynamic_index_in_dim(_to_shards(a, SMALL_AXIS[n]), my_chip, axis=0, keepdims=False)
        small_part[n] = a

    outs = []
    for n in NAMES:
        shp = w[n].shape
        two = (int(np.prod(shp[:-1])), shp[-1]) if len(shp) > 1 else (1, shp[0])
        if n in BIG_LAYOUT:
            layers = shp[0]
            rows = lambda a: a.reshape(4, two[0] // layers, two[1])
            parts = [[rows(recv[(n, l)]) for l in range(layers)], [rows(sib[(n, l)]) for l in range(layers)]]
        else:
            parts = [[small_part[n].reshape((1,) + two)]]
        res = adamw(w[n].reshape(two), parts, m[n].reshape(two), v[n].reshape(two), "adamw_" + n)
        outs.append([r.reshape(shp) for r in res])
    return (loss, dx0[None], *[o[0] for o in outs], *[o[1] for o in outs], *[o[2] for o in outs], *[o[3] for o in outs])
```

```python
import functools
import math

import jax
import jax.numpy as jnp
import numpy as np
from jax import lax
from jax.experimental import pallas as pl
from jax.experimental.pallas import tpu as pltpu

F32 = jnp.float32
BF16 = jnp.bfloat16
I32 = jnp.int32

D = 1024
EPS = 1e-6
POOL_WIDTH = 2048
POOL_GROUP = 512
GDN_H, GDN_DK, GDN_DV, GDN_C = 8, 128, 256, 64
GDN_QK, GDN_V, GDN_CONV_CH, GDN_IN = 1024, 2048, 4096, 6160
GDN_IN_PAD = 6272
MLA_H, MLA_NOPE, MLA_ROPE, MLA_V = 16, 128, 64, 128
MLA_Q_LORA, MLA_KV_LORA, MLA_WIDTH, MLA_IN = 768, 512, 2048, 3392
MLA_IN_PAD = 4096
MLA_SCALE = (MLA_NOPE + MLA_ROPE) ** -0.5
ROPE_THETA = 10000.0
ADAM_LR, ADAM_B1, ADAM_B2, ADAM_EPS, ADAM_WD, ADAM_STEP = 0.001, 0.9, 0.999, 1e-08, 0.01, 10

VMEM_LIMIT_V7X = 56 * 1024 * 1024
LANES = 128
MESH = pl.DeviceIdType.MESH


def _pc(body, **kw):
    return pl.pallas_call(body, **kw)


def _cparams(sem):
    return pltpu.CompilerParams(dimension_semantics=sem, vmem_limit_bytes=VMEM_LIMIT_V7X)


def _tile(n, cap):
    t = (cap // LANES) * LANES
    while t >= LANES:
        if n % t == 0:
            return t
        t -= LANES
    return n


def _sds(shape, dt):
    return jax.ShapeDtypeStruct(shape, dt)


def mm(a, b, *, ta=False, tb=False, add=None, after=None, out_dtype=F32, name):
    if ta:
        K, M = a.shape
    else:
        M, K = a.shape
    if tb:
        N, K2 = b.shape
    else:
        K2, N = b.shape
    assert K == K2, (a.shape, b.shape, ta, tb)
    tm, tn, tk = _tile(M, 1024), _tile(N, 1024), _tile(K, 1024)
    nk = K // tk
    a_spec = pl.BlockSpec((tk, tm), lambda i, j, k: (k, i)) if ta else pl.BlockSpec((tm, tk), lambda i, j, k: (i, k))
    b_spec = pl.BlockSpec((tn, tk), lambda i, j, k: (j, k)) if tb else pl.BlockSpec((tk, tn), lambda i, j, k: (k, j))
    o_spec = pl.BlockSpec((tm, tn), lambda i, j, k: (i, j))
    dn = (((0 if ta else 1,), (1 if tb else 0,)), ((), ()))
    has_add = add is not None

    def body(*refs):
        a_ref, b_ref = refs[0], refs[1]
        part = lax.dot_general(a_ref[...].astype(BF16), b_ref[...].astype(BF16), dn, preferred_element_type=F32)
        if nk == 1:
            refs[-1][...] = (part + refs[2][...] if has_add else part).astype(out_dtype)
            return
        o_ref, acc = refs[-2], refs[-1]
        k = pl.program_id(2)

        @pl.when(k == 0)
        def _():
            acc[...] = part

        @pl.when(k > 0)
        def _():
            acc[...] += part

        @pl.when(k == nk - 1)
        def _():
            r = acc[...]
            if has_add:
                r = r + refs[2][...]
            o_ref[...] = r.astype(out_dtype)

    ins = [a, b] + ([add] if has_add else []) + ([after] if after is not None else [])
    specs = [a_spec, b_spec] + ([o_spec] if has_add else []) + ([pl.BlockSpec(memory_space=pl.ANY)] if after is not None else [])
    return _pc(body, grid=(M // tm, N // tn, nk), in_specs=specs, out_specs=o_spec, out_shape=_sds((M, N), out_dtype),
               scratch_shapes=[pltpu.VMEM((tm, tn), F32)] if nk > 1 else [], compiler_params=_cparams(("parallel", "parallel", "arbitrary")),
               name=name)(*ins)


def gmm(kind, a, b, *, G, name, out_dtype=F32):
    S_ = a.shape[0]
    Ka = a.shape[1] // G
    if kind == "tn":
        N = b.shape[1] // G
        tk = _tile(S_, 512)
        nk = S_ // tk

        def body(a_ref, b_ref, o_ref, acc):
            k = pl.program_id(1)

            @pl.when(k == 0)
            def _():
                acc[...] = jnp.zeros_like(acc)

            acc[...] += lax.dot_general(a_ref[...].astype(BF16), b_ref[...].astype(BF16), (((0,), (0,)), ((), ())),
                                        preferred_element_type=F32)

            @pl.when(k == nk - 1)
            def _():
                o_ref[...] = acc[...].astype(out_dtype)

        return _pc(body, grid=(G, nk),
                   in_specs=[pl.BlockSpec((tk, Ka), lambda g, k: (k, g)), pl.BlockSpec((tk, N), lambda g, k: (k, g))],
                   out_specs=pl.BlockSpec((None, Ka, N), lambda g, k: (g, 0, 0)), out_shape=_sds((G, Ka, N), out_dtype),
                   scratch_shapes=[pltpu.VMEM((Ka, N), F32)], compiler_params=_cparams(("parallel", "arbitrary")), name=name)(a, b)
    N = b.shape[2] if kind == "nn" else b.shape[1]
    tm = _tile(S_, 1024)
    dn = (((1,), (0 if kind == "nn" else 1,)), ((), ()))

    def body(a_ref, b_ref, o_ref):
        o_ref[...] = lax.dot_general(a_ref[...].astype(BF16), b_ref[...].astype(BF16), dn, preferred_element_type=F32)

    bshape = (None,) + tuple(b.shape[1:])
    return _pc(body, grid=(G, S_ // tm),
               in_specs=[pl.BlockSpec((tm, Ka), lambda g, i: (i, g)), pl.BlockSpec(bshape, lambda g, i: (g, 0, 0))],
               out_specs=pl.BlockSpec((tm, N), lambda g, i: (i, g)), out_shape=_sds((S_, G * N), F32),
               compiler_params=_cparams(("parallel", "parallel")), name=name)(a, b)


def _rw_spec(ts, w, c, s):
    return pl.BlockSpec((ts, w), lambda j, i: (i, c + j * s))


def _rw_pspec(p, w, c, s):
    return pl.BlockSpec((p.shape[0], w), lambda j, i: (0, c + j * s))


def rowwise(f, tiles, params, outs, *, ncol=1, ts, name):
    S_ = tiles[0][0].shape[0]
    nin = len(tiles) + len(params)

    def body(*refs):
        res = f(pl.program_id(0), *[r[...] for r in refs[:nin]])
        for r, o in zip(refs[nin:], res):
            r[...] = o.astype(r.dtype)

    return _pc(body, grid=(ncol, S_ // ts),
               in_specs=[_rw_spec(ts, w, c, s) for (_, w, c, s) in tiles] + [_rw_pspec(*p) for p in params],
               out_specs=[_rw_spec(ts, w, 0, s) for (w, s, _) in outs],
               out_shape=[_sds((S_, w * (ncol if s else 1)), dt) for (w, s, dt) in outs],
               compiler_params=_cparams(("parallel", "parallel")), name=name)(*[t[0] for t in tiles], *[p[0] for p in params])


def rowwise_bwd(f, tiles, params, cots, *, need, adds=None, place=None, narrow=(), ncol=1, ts, name):
    S_ = tiles[0][0].shape[0]
    adds = adds or {}
    place = place or {}
    nt, npar, nc = len(tiles), len(params), len(cots)
    add_keys = sorted(adds)
    need_idx = [k for k in range(nt) if need[k]]
    into_keys = [k for k in need_idx if k in place and not isinstance(place[k][0], int)]
    n_extra = len(add_keys) + len(into_keys)

    def body(*refs):
        j, i = pl.program_id(0), pl.program_id(1)
        vals = [r[...] for r in refs[:nt + npar]]
        cvals = tuple(r[...] for r in refs[nt + npar:nt + npar + nc])
        add_refs = refs[nt + npar + nc:nt + npar + nc + len(add_keys)]
        out_refs = refs[nt + npar + nc + n_extra:]
        _, vjp = jax.vjp(lambda *v: tuple(f(j, *v)), *vals)
        grads = vjp(cvals)
        for n, k in enumerate(need_idx):
            g = grads[k]
            if k in adds:
                g = g + add_refs[add_keys.index(k)][...]
            out_refs[n][...] = g.astype(out_refs[n].dtype)
        for n in range(npar):
            ref = out_refs[len(need_idx) + n]
            first = (i == 0) if params[n][3] else jnp.logical_and(i == 0, j == 0)

            @pl.when(first)
            def _():
                ref[...] = jnp.zeros_like(ref)

            ref[...] += grads[nt + n]

    in_specs = ([_rw_spec(ts, w, c, s) for (_, w, c, s) in tiles] + [_rw_pspec(*p) for p in params]
                + [_rw_spec(ts, w, c, s) for (_, w, c, s) in cots] + [_rw_spec(ts, *adds[k][1:]) for k in add_keys]
                + [pl.BlockSpec(memory_space=pl.ANY) for _ in into_keys])
    out_specs, out_shape, aliases = [], [], {}
    for n, k in enumerate(need_idx):
        w, s = tiles[k][1], tiles[k][3]
        if k in place:
            dst, c0 = place[k]
            total = dst if isinstance(dst, int) else dst.shape[1]
            out_specs.append(_rw_spec(ts, w, c0, s))
            out_shape.append(_sds((S_, total), (BF16 if k in narrow else F32) if isinstance(dst, int) else dst.dtype))
            if k in into_keys:
                aliases[nt + npar + nc + len(add_keys) + into_keys.index(k)] = n
        else:
            out_specs.append(_rw_spec(ts, w, 0, s))
            out_shape.append(_sds((S_, w * (ncol if s else 1)), BF16 if k in narrow else F32))
    out_specs += [_rw_pspec(p[0], p[1], p[2], p[3]) for p in params]
    out_shape += [_sds(p[0].shape, F32) for p in params]
    res = _pc(body, grid=(ncol, S_ // ts), in_specs=in_specs, out_specs=out_specs, out_shape=out_shape,
              input_output_aliases=aliases, compiler_params=_cparams(("arbitrary", "arbitrary")), name=name)(
        *[t[0] for t in tiles], *[p[0] for p in params], *[c[0] for c in cots], *[adds[k][0] for k in add_keys],
        *[place[k][0] for k in into_keys])
    return list(res[:len(need_idx)]), list(res[len(need_idx):])


def _rms(x, g):
    r = lax.rsqrt(jnp.mean(x * x, axis=-1, keepdims=True) + EPS)
    return x * r * g


def _silu(x):
    return x * jax.nn.sigmoid(x)


@jax.custom_vjp
def _softplus(x):
    return jnp.maximum(x, 0.0) + jnp.log1p(jnp.exp(-jnp.abs(x)))


_softplus.defvjp(lambda x: (_softplus(x), x), lambda x, d: (d * jax.nn.sigmoid(x),))


def f_rms(j, x, g):
    return (_rms(x, g),)


def f_pool_gate(j, pg, gate, scale):
    return (pg * scale * _silu(gate),)


def f_ogate(j, o, gate):
    return (o * _silu(gate),)


def f_gdn_out(j, o, gate, g):
    return (_rms(o, g) * _silu(gate),)


def f_gdn_gates(j, ba, alog, dtb):
    lane = lax.broadcasted_iota(I32, (1, LANES), 1)
    gs, bs = [], []
    for h in range(GDN_H):
        eb = (lane == h).astype(F32)
        ea = (lane == GDN_H + h).astype(F32)
        b = jnp.sum(ba * eb, -1, keepdims=True)
        a = jnp.sum(ba * ea, -1, keepdims=True)
        al = jnp.sum(alog * eb, -1, keepdims=True)
        dt = jnp.sum(dtb * eb, -1, keepdims=True)
        g = -jnp.exp(al) * _softplus(a + dt)
        gs.append(jnp.broadcast_to(g, ba.shape))
        bs.append(jnp.broadcast_to(jax.nn.sigmoid(b), ba.shape))
    return jnp.concatenate(gs, 1), jnp.concatenate(bs, 1)


def _shift_dn(x, k):
    rows = lax.broadcasted_iota(I32, x.shape, 0)
    return jnp.where(rows < k, 0.0, pltpu.roll(x, k, 0))


def _shift_up(x, k):
    n = x.shape[0]
    rows = lax.broadcasted_iota(I32, x.shape, 0)
    return jnp.where(rows >= n - k, 0.0, pltpu.roll(x, n - k, 0))


def _pool_window(j):
    g = lax.div(j, POOL_GROUP // LANES)
    return jnp.where(g == 0, 2.0, jnp.where(g == 1, 4.0, jnp.where(g == 2, 8.0, 16.0))), g


def _pick(g, a2, a4, a8, a16):
    return jnp.where(g == 0, a2, jnp.where(g == 1, a4, jnp.where(g == 2, a8, a16)))


def pool_time_fwd(proj, name):
    S_ = proj.shape[0]

    def body(u_ref, p_ref):
        u = u_ref[...]
        w, g = _pool_window(pl.program_id(0))
        s2 = u + _shift_dn(u, 1)
        s4 = s2 + _shift_dn(s2, 2)
        s8 = s4 + _shift_dn(s4, 4)
        s16 = s8 + _shift_dn(s8, 8)
        t1 = (lax.broadcasted_iota(I32, u.shape, 0) + 1).astype(F32)
        p_ref[...] = (_pick(g, s2, s4, s8, s16) / jnp.minimum(t1, w) - u).astype(p_ref.dtype)

    return _pc(body, grid=(POOL_WIDTH // LANES,), in_specs=[pl.BlockSpec((S_, LANES), lambda j: (0, j))],
               out_specs=pl.BlockSpec((S_, LANES), lambda j: (0, j)), out_shape=_sds((S_, POOL_WIDTH), BF16),
               compiler_params=_cparams(("parallel",)), name=name)(proj)


def pool_time_bwd(dp, into, name):
    S_ = dp.shape[0]

    def body(dp_ref, _, du_ref):
        d = dp_ref[...]
        w, g = _pool_window(pl.program_id(0))
        t1 = (lax.broadcasted_iota(I32, d.shape, 0) + 1).astype(F32)
        q = d / jnp.minimum(t1, w)
        r2 = q + _shift_up(q, 1)
        r4 = r2 + _shift_up(r2, 2)
        r8 = r4 + _shift_up(r4, 4)
        r16 = r8 + _shift_up(r8, 8)
        du_ref[...] = (_pick(g, r2, r4, r8, r16) - d).astype(du_ref.dtype)

    return _pc(body, grid=(POOL_WIDTH // LANES,),
               in_specs=[pl.BlockSpec((S_, LANES), lambda j: (0, j)), pl.BlockSpec(memory_space=pl.ANY)],
               out_specs=pl.BlockSpec((S_, LANES), lambda j: (0, j)), out_shape=_sds(into.shape, into.dtype),
               input_output_aliases={1: 0}, compiler_params=_cparams(("parallel",)), name=name)(dp, into)


def _conv_post(j, a):
    n = a * lax.rsqrt(jnp.sum(a * a, axis=-1, keepdims=True) + EPS)
    nq = GDN_QK // LANES
    return jnp.where(j < nq, n * (GDN_DK ** -0.5), jnp.where(j < 2 * nq, n, a))


def _conv_taps(u):
    return [_shift_dn(u, 3), _shift_dn(u, 2), _shift_dn(u, 1), u]


def _conv_pre(taps, w):
    return w[0:1] * taps[0] + w[1:2] * taps[1] + w[2:3] * taps[2] + w[3:4] * taps[3]


def gdn_conv_fwd(proj, conv_w, name):
    S_ = proj.shape[0]

    def body(u_ref, w_ref, o_ref):
        o_ref[...] = _conv_post(pl.program_id(0), _silu(_conv_pre(_conv_taps(u_ref[...]), w_ref[...])))

    return _pc(body, grid=(GDN_CONV_CH // LANES,),
               in_specs=[pl.BlockSpec((S_, LANES), lambda j: (0, j)), pl.BlockSpec((8, LANES), lambda j: (0, j))],
               out_specs=pl.BlockSpec((S_, LANES), lambda j: (0, j)), out_shape=_sds((S_, GDN_CONV_CH), F32),
               compiler_params=_cparams(("parallel",)), name=name)(proj, conv_w)


def gdn_conv_bwd(proj, conv_w, dq, dk, dv, into, name):
    S_ = proj.shape[0]
    nq = GDN_QK // LANES

    def body(u_ref, w_ref, dq_ref, dk_ref, dv_ref, _, du_ref, dw_ref):
        j = pl.program_id(0)
        u, w = u_ref[...], w_ref[...]
        taps = _conv_taps(u)
        c = _conv_pre(taps, w)
        sig = jax.nn.sigmoid(c)
        dout = jnp.where(j < nq, dq_ref[...], jnp.where(j < 2 * nq, dk_ref[...], dv_ref[...]))
        _, vjp = jax.vjp(lambda a: _conv_post(j, a), c * sig)
        dc = vjp(dout)[0] * (sig * (1.0 + c * (1.0 - sig)))
        du = w[3:4] * dc + w[2:3] * _shift_up(dc, 1) + w[1:2] * _shift_up(dc, 2) + w[0:1] * _shift_up(dc, 3)
        du_ref[...] = du.astype(du_ref.dtype)
        rows = lax.broadcasted_iota(I32, (8, LANES), 0)
        dw = jnp.zeros((8, LANES), F32)
        for k in range(4):
            dw = dw + jnp.where(rows == k, jnp.sum(dc * taps[k], axis=0, keepdims=True), 0.0)
        dw_ref[...] = dw

    blk = lambda f: pl.BlockSpec((S_, LANES), f)
    return _pc(body, grid=(GDN_CONV_CH // LANES,),
               in_specs=[blk(lambda j: (0, j)), pl.BlockSpec((8, LANES), lambda j: (0, j)),
                         blk(lambda j: (0, jnp.minimum(j, nq - 1))), blk(lambda j: (0, jnp.clip(j - nq, 0, nq - 1))),
                         blk(lambda j: (0, jnp.clip(j - 2 * nq, 0, 2 * nq - 1))), pl.BlockSpec(memory_space=pl.ANY)],
               out_specs=[blk(lambda j: (0, j)), pl.BlockSpec((8, LANES), lambda j: (0, j))],
               out_shape=[_sds(into.shape, into.dtype), _sds((8, GDN_CONV_CH), F32)], input_output_aliases={5: 0},
               compiler_params=_cparams(("parallel",)), name=name)(proj, conv_w, dq, dk, dv, into)


_NN, _NT, _TN = ((1,), (0,)), ((1,), (1,)), ((0,), (0,))


def _split(x, n):
    parts = []
    for _ in range(n):
        h = x.astype(BF16)
        parts.append(h)
        x = x - h.astype(F32)
    return parts


def _dot(a, b, dn, mode):
    d = lambda p, q: lax.dot_general(p, q, (dn, ((), ())), preferred_element_type=F32)
    if mode == "lo":
        return d(a.astype(BF16), b.astype(BF16))
    if mode == "x3":
        (ah, al), (bh, bl) = _split(a, 2), _split(b, 2)
        return d(ah, bh) + (d(ah, bl) + d(al, bh))
    b0, b1, b2 = _split(b, 3)
    ab = a.astype(BF16)
    return d(ab, b0) + (d(ab, b1) + d(ab, b2))


def _make_dots(mode):
    @jax.custom_vjp
    def nn(a, b):
        return _dot(a, b, _NN, mode)

    @jax.custom_vjp
    def nt(a, b):
        return _dot(a, b, _NT, mode)

    @jax.custom_vjp
    def tn(a, b):
        return _dot(a, b, _TN, mode)

    nn.defvjp(lambda a, b: (nn(a, b), (a, b)), lambda r, d: (nt(d, r[1]), tn(r[0], d)))
    nt.defvjp(lambda a, b: (nt(a, b), (a, b)), lambda r, d: (nn(d, r[1]), tn(d, r[0])))
    tn.defvjp(lambda a, b: (tn(a, b), (a, b)), lambda r, d: (nt(r[1], d), nn(r[0], d)))
    return nn, nt, tn


_nn_hi, _nt_hi, _tn_hi = _make_dots("x3")
_nn_lo, _nt_lo, _tn_lo = _make_dots("lo")


@jax.custom_vjp
def _nn_const(a, b):
    return _dot(a, b, _NN, "xl")


_nn_const.defvjp(lambda a, b: (_nn_const(a, b), a), lambda a, d: (jnp.zeros_like(a), _dot(a, d, _TN, "xl")))


def _each(f, *lists):
    return [f(*xs) for xs in zip(*lists)]


@jax.custom_vjp
def _unit_inverses(xs):
    C = xs[0].shape[0]
    eye = (lax.broadcasted_iota(I32, (C, C), 0) == lax.broadcasted_iota(I32, (C, C), 1)).astype(F32)
    ainv, p = [eye + a for a in xs], xs
    for _ in range(int(math.log2(C)) - 1):
        p = _each(lambda a: _dot(a, a, _NN, "x3"), p)
        ainv = _each(lambda a, b: a + _dot(a, b, _NN, "x3"), ainv, p)
    return ainv


def _unit_inverses_bwd(ainv, d):
    left = _each(lambda a, g: _dot(a, g, _TN, "x3"), ainv, d)
    return (_each(lambda t, a: _dot(t, a, _NT, "x3"), left, ainv),)


_unit_inverses.defvjp(lambda xs: (lambda a: (a, a))(_unit_inverses(xs)), _unit_inverses_bwd)


def _gdn_chunk(q, k, v, gb, bb, state):
    C = GDN_C
    e0 = (lax.broadcasted_iota(I32, (1, LANES), 1) == 0).astype(F32)
    ri = lax.broadcasted_iota(I32, (C, C), 0)
    ci = lax.broadcasted_iota(I32, (C, C), 1)
    causal, strict = ri >= ci, ri > ci
    tri, eye, ones = causal.astype(F32), (ri == ci).astype(F32), jnp.ones((C, C), F32)
    last = lax.broadcasted_iota(I32, (C, LANES), 0) == C - 1
    g1 = _each(lambda a: jnp.sum(a * e0, -1, keepdims=True), gb)
    b1 = _each(lambda a: jnp.sum(a * e0, -1, keepdims=True), bb)
    gc_c = _each(lambda g: _nn_const(tri, jnp.broadcast_to(g, (C, C))), g1)
    gc_d = _each(lambda g: _nn_const(tri, jnp.broadcast_to(g, (C, LANES))), g1)
    gr_c = _each(lambda g: _nn_const(ones, eye * g), gc_c)
    decay = _each(lambda a, r: jnp.where(causal, jnp.exp(jnp.where(causal, a - r, 0.0)), 0.0), gc_c, gr_c)
    kb = _each(lambda a, b: a * b, k, b1)
    vb = _each(lambda a, b: a * b, v, b1)
    x = _each(lambda a, b, d: -jnp.where(strict, _nt_lo(a, b) * d, 0.0), kb, k, decay)
    ainv = _unit_inverses(x)
    u = _each(_nn_hi, ainv, vb)
    w = _each(lambda a, b, g: _nn_hi(a, b * jnp.exp(g)), ainv, kb, gc_d)
    attn = _each(lambda a, b, d: jnp.where(causal, _nt_lo(a, b) * d, 0.0), q, k, decay)
    v_new = _each(lambda a, b, s: a - _nn_lo(b, s), u, w, state)
    o = _each(lambda a, g, s, t, vn: _nn_lo(a * jnp.exp(g), s) + _nn_lo(t, vn), q, gc_d, state, attn, v_new)
    gl = _each(lambda g: jnp.sum(jnp.where(last, g, 0.0), axis=0, keepdims=True), gc_d)
    new_state = _each(lambda s, g, a, gd, vn: s * jnp.exp(jnp.sum(g * e0, -1, keepdims=True)) + _tn_lo(a * jnp.exp(g - gd), vn),
                      state, gl, k, gc_d, v_new)
    return o, new_state


def _head_slices(ref, width):
    return [ref[:, h * width:(h + 1) * width] for h in range(GDN_H)]


def gdn_chunk_fwd(qkv, g_b, beta_b, name):
    S_ = qkv.shape[0]
    N = S_ // GDN_C

    def body(q_ref, k_ref, v_ref, g_ref, b_ref, o_ref, s_ref, state):
        @pl.when(pl.program_id(0) == 0)
        def _():
            state[...] = jnp.zeros_like(state)

        st = [state[h] for h in range(GDN_H)]
        s_ref[0] = state[...]
        o, st2 = _gdn_chunk(_head_slices(q_ref, GDN_DK), _head_slices(k_ref, GDN_DK), _head_slices(v_ref, GDN_DV),
                            _head_slices(g_ref, GDN_DK), _head_slices(b_ref, GDN_DK), st)
        for h in range(GDN_H):
            o_ref[:, h * GDN_DV:(h + 1) * GDN_DV] = o[h]
            state[h] = st2[h]

    return _pc(body, grid=(N,),
               in_specs=[pl.BlockSpec((GDN_C, GDN_QK), lambda n: (n, 0)), pl.BlockSpec((GDN_C, GDN_QK), lambda n: (n, 1)),
                         pl.BlockSpec((GDN_C, GDN_V), lambda n: (n, 1)), pl.BlockSpec((GDN_C, GDN_QK), lambda n: (n, 0)),
                         pl.BlockSpec((GDN_C, GDN_QK), lambda n: (n, 0))],
               out_specs=[pl.BlockSpec((GDN_C, GDN_V), lambda n: (n, 0)),
                          pl.BlockSpec((1, GDN_H, GDN_DK, GDN_DV), lambda n: (n, 0, 0, 0))],
               out_shape=[_sds((S_, GDN_V), F32), _sds((N, GDN_H, GDN_DK, GDN_DV), F32)],
               scratch_shapes=[pltpu.VMEM((GDN_H, GDN_DK, GDN_DV), F32)],
               compiler_params=_cparams(("arbitrary",)), name=name)(qkv, qkv, qkv, g_b, beta_b)


def gdn_chunk_bwd(qkv, g_b, beta_b, states, do, name):
    S_ = qkv.shape[0]
    N = S_ // GDN_C

    def body(q_ref, k_ref, v_ref, g_ref, b_ref, s_ref, do_ref, dq_ref, dk_ref, dv_ref, dg_ref, db_ref, dstate):
        @pl.when(pl.program_id(0) == 0)
        def _():
            dstate[...] = jnp.zeros_like(dstate)

        _, vjp = jax.vjp(_gdn_chunk, _head_slices(q_ref, GDN_DK), _head_slices(k_ref, GDN_DK), _head_slices(v_ref, GDN_DV),
                         _head_slices(g_ref, GDN_DK), _head_slices(b_ref, GDN_DK), [s_ref[0, h] for h in range(GDN_H)])
        dq, dk, dv, dg, db, ds = vjp((_head_slices(do_ref, GDN_DV), [dstate[h] for h in range(GDN_H)]))
        for h in range(GDN_H):
            kk, vv = slice(h * GDN_DK, (h + 1) * GDN_DK), slice(h * GDN_DV, (h + 1) * GDN_DV)
            dq_ref[:, kk] = dq[h]
            dk_ref[:, kk] = dk[h]
            dv_ref[:, vv] = dv[h]
            dg_ref[:, kk] = dg[h]
            db_ref[:, kk] = db[h]
            dstate[h] = ds[h]

    r = lambda n: N - 1 - n
    qk = lambda c: pl.BlockSpec((GDN_C, GDN_QK), lambda n: (r(n), c))
    vs = lambda c: pl.BlockSpec((GDN_C, GDN_V), lambda n: (r(n), c))
    return _pc(body, grid=(N,),
               in_specs=[qk(0), qk(1), vs(1), qk(0), qk(0),
                         pl.BlockSpec((1, GDN_H, GDN_DK, GDN_DV), lambda n: (r(n), 0, 0, 0)), vs(0)],
               out_specs=[qk(0), qk(0), vs(0), qk(0), qk(0)],
               out_shape=[_sds((S_, GDN_QK), F32), _sds((S_, GDN_QK), F32), _sds((S_, GDN_V), F32),
                          _sds((S_, GDN_QK), F32), _sds((S_, GDN_QK), F32)],
               scratch_shapes=[pltpu.VMEM((GDN_H, GDN_DK, GDN_DV), F32)],
               compiler_params=_cparams(("arbitrary",)), name=name)(qkv, qkv, qkv, g_b, beta_b, states, do)


def _rope_tables(pos_ref, inv_ref, cm_ref, sg_ref):
    ang = pos_ref[...] * inv_ref[...]
    return jnp.cos(ang) * cm_ref[...], jnp.sin(ang) * sg_ref[...]


def mla_prep_fwd(qpad, kv, proj, pos, rope_consts, name):
    S_ = qpad.shape[0]
    ts = 256
    W = 2 * LANES

    def body(q_ref, kv_ref, kr_ref, pos_ref, inv_ref, cm_ref, sg_ref, qh_ref, kh_ref, vh_ref):
        cs, sn = _rope_tables(pos_ref, inv_ref, cm_ref, sg_ref)
        rope = lambda r: r * cs + pltpu.roll(r, LANES // 2, 1) * sn
        krr = rope(kr_ref[...]).astype(BF16)
        for h in range(MLA_H):
            qh_ref[h, :, 0:LANES] = (q_ref[:, h * W:h * W + LANES] * MLA_SCALE).astype(BF16)
            qh_ref[h, :, LANES:W] = (rope(q_ref[:, h * W + LANES:(h + 1) * W]) * MLA_SCALE).astype(BF16)
            kh_ref[h, :, 0:LANES] = kv_ref[:, h * W:h * W + LANES].astype(BF16)
            kh_ref[h, :, LANES:W] = krr
            vh_ref[h] = kv_ref[:, h * W + LANES:(h + 1) * W].astype(BF16)

    one = pl.BlockSpec((1, LANES), lambda i: (0, 0))
    return _pc(body, grid=(S_ // ts,),
               in_specs=[pl.BlockSpec((ts, MLA_H * W), lambda i: (i, 0)), pl.BlockSpec((ts, MLA_H * W), lambda i: (i, 0)),
                         pl.BlockSpec((ts, LANES), lambda i: (i, 1536 // LANES)), pl.BlockSpec((ts, 1), lambda i: (i, 0)),
                         one, one, one],
               out_specs=[pl.BlockSpec((MLA_H, ts, W), lambda i: (0, i, 0)), pl.BlockSpec((MLA_H, ts, W), lambda i: (0, i, 0)),
                          pl.BlockSpec((MLA_H, ts, LANES), lambda i: (0, i, 0))],
               out_shape=[_sds((MLA_H, S_, W), BF16), _sds((MLA_H, S_, W), BF16), _sds((MLA_H, S_, LANES), BF16)],
               compiler_params=_cparams(("parallel",)), name=name)(qpad, kv, proj, pos, *rope_consts)


def mla_prep_bwd(dqh, dkh, dvh, pos, rope_consts, into, name):
    S_ = dqh.shape[1]
    ts = 256
    W = 2 * LANES

    def body(dq_ref, dk_ref, dv_ref, pos_ref, inv_ref, cm_ref, sg_ref, _, dqp_ref, dkv_ref, dkr_ref):
        cs, sn = _rope_tables(pos_ref, inv_ref, cm_ref, sg_ref)
        rope_t = lambda g: g * cs + pltpu.roll(g * sn, LANES // 2, 1)
        acc = jnp.zeros((ts, LANES), F32)
        for h in range(MLA_H):
            dqp_ref[:, h * W:h * W + LANES] = (dq_ref[h, :, 0:LANES].astype(F32) * MLA_SCALE).astype(BF16)
            dqp_ref[:, h * W + LANES:(h + 1) * W] = (rope_t(dq_ref[h, :, LANES:W].astype(F32)) * MLA_SCALE).astype(BF16)
            dkv_ref[:, h * W:h * W + LANES] = dk_ref[h, :, 0:LANES]
            dkv_ref[:, h * W + LANES:(h + 1) * W] = dv_ref[h]
            acc = acc + dk_ref[h, :, LANES:W].astype(F32)
        dkr_ref[...] = rope_t(acc).astype(dkr_ref.dtype)

    one = pl.BlockSpec((1, LANES), lambda i: (0, 0))
    return _pc(body, grid=(S_ // ts,),
               in_specs=[pl.BlockSpec((MLA_H, ts, W), lambda i: (0, i, 0)), pl.BlockSpec((MLA_H, ts, W), lambda i: (0, i, 0)),
                         pl.BlockSpec((MLA_H, ts, LANES), lambda i: (0, i, 0)), pl.BlockSpec((ts, 1), lambda i: (i, 0)),
                         one, one, one, pl.BlockSpec(memory_space=pl.ANY)],
               out_specs=[pl.BlockSpec((ts, MLA_H * W), lambda i: (i, 0)), pl.BlockSpec((ts, MLA_H * W), lambda i: (i, 0)),
                          pl.BlockSpec((ts, LANES), lambda i: (i, 1536 // LANES))],
               out_shape=[_sds((S_, MLA_H * W), BF16), _sds((S_, MLA_H * W), BF16), _sds(into.shape, into.dtype)],
               input_output_aliases={7: 2}, compiler_params=_cparams(("parallel",)), name=name)(dqh, dkh, dvh, pos, *rope_consts, into)


NEG = -1e30


FLASH_TILE = 1024
FLASH_SUB = 512


def _scores(q, k, diagonal):
    s = lax.dot_general(q, k, (_NT, ((), ())), preferred_element_type=F32)
    if not diagonal:
        return s
    return jnp.where(lax.broadcasted_iota(I32, s.shape, 1) <= lax.broadcasted_iota(I32, s.shape, 0), s, NEG)


def _sub_blocks(t, diagonal):
    sub = min(FLASH_SUB, t) if diagonal else t
    return [(c * sub if diagonal else 0, slice(c * sub, (c + 1) * sub)) for c in range(t // sub)]


FLASH_HEADS = 2


def flash_fwd(qh, kh, vh, name):
    H, S_, W = qh.shape
    t = _tile(S_, FLASH_TILE)
    n = S_ // t
    G = FLASH_HEADS
    heads = list(range(G))

    def body(q_ref, k_ref, v_ref, o_ref, lse_ref, m_s, l_s, acc):
        qi, kj = pl.program_id(1), pl.program_id(2)

        @pl.when(kj == 0)
        def _():
            m_s[...] = jnp.full_like(m_s, NEG)
            l_s[...] = jnp.zeros_like(l_s)
            acc[...] = jnp.zeros_like(acc)

        def step(diagonal):
            s = _each(lambda a: _scores(q_ref[a], k_ref[a], diagonal), heads)
            m_old = _each(lambda a: m_s[a], heads)
            m_new = _each(lambda mo, sa: jnp.maximum(mo, jnp.max(sa, axis=-1, keepdims=True)), m_old, s)
            alpha = _each(lambda mo, mn: jnp.exp(mo - mn), m_old, m_new)
            p = _each(lambda sa, mn: jnp.exp(sa - mn[:, :1]), s, m_new)
            pv = _each(lambda pa, a: lax.dot_general(pa.astype(BF16), v_ref[a], (_NN, ((), ())), preferred_element_type=F32), p, heads)
            for a in heads:
                l_s[a] = alpha[a] * l_s[a] + jnp.sum(p[a], axis=-1, keepdims=True)
                acc[a] = alpha[a] * acc[a] + pv[a]
                m_s[a] = m_new[a]

        pl.when(kj < qi)(lambda: step(False))
        pl.when(kj == qi)(lambda: step(True))

        @pl.when(kj == n - 1)
        def _():
            for a in heads:
                o_ref[:, a * LANES:(a + 1) * LANES] = acc[a] / l_s[a]
                lse_ref[a] = m_s[a] + jnp.log(l_s[a])

    return _pc(body, grid=(H // G, n, n),
               in_specs=[pl.BlockSpec((G, t, W), lambda h, i, j: (h, i, 0)),
                         pl.BlockSpec((G, t, W), lambda h, i, j: (h, jnp.minimum(i, j), 0)),
                         pl.BlockSpec((G, t, LANES), lambda h, i, j: (h, jnp.minimum(i, j), 0))],
               out_specs=[pl.BlockSpec((t, G * LANES), lambda h, i, j: (i, h)), pl.BlockSpec((G, t, LANES), lambda h, i, j: (h, i, 0))],
               out_shape=[_sds((S_, H * LANES), F32), _sds((H, S_, LANES), F32)],
               scratch_shapes=[pltpu.VMEM((G, t, LANES), F32)] * 3,
               compiler_params=_cparams(("parallel", "parallel", "arbitrary")), name=name)(qh, kh, vh)


def flash_bwd(qh, kh, vh, o, lse, do, name):
    H, S_, W = qh.shape
    t = _tile(S_, FLASH_TILE)
    n = S_ // t

    def body(q_ref, k_ref, v_ref, o_ref, lse_ref, do_ref, dq_ref, dk_ref, dv_ref, dq_acc, dk_acc, dv_acc):
        kj, qi = pl.program_id(1), pl.program_id(2)

        @pl.when(jnp.logical_and(kj == 0, qi == 0))
        def _():
            dq_acc[...] = jnp.zeros_like(dq_acc)

        @pl.when(qi == 0)
        def _():
            dk_acc[...] = jnp.zeros_like(dk_acc)
            dv_acc[...] = jnp.zeros_like(dv_acc)

        def step(diagonal):
            do_ = do_ref[...]
            dob = do_.astype(BF16)
            delta = jnp.sum(do_ * o_ref[...], axis=-1, keepdims=True)
            for r0, keys in _sub_blocks(t, diagonal):
                q, k, v = q_ref[r0:, :], k_ref[keys, :], v_ref[keys, :]
                p = jnp.exp(_scores(q, k, diagonal) - lse_ref[r0:, :1])
                dv_acc[keys, :] += lax.dot_general(p.astype(BF16), dob[r0:], (_TN, ((), ())), preferred_element_type=F32)
                dp = lax.dot_general(dob[r0:], v, (_NT, ((), ())), preferred_element_type=F32)
                ds = (p * (dp - delta[r0:])).astype(BF16)
                dk_acc[keys, :] += lax.dot_general(ds, q, (_TN, ((), ())), preferred_element_type=F32)
                rows = pl.ds(pl.multiple_of(qi * t, t) + r0, t - r0)
                dq_acc[rows, :] += lax.dot_general(ds, k, (_NN, ((), ())), preferred_element_type=F32)

        pl.when(qi > kj)(lambda: step(False))
        pl.when(qi == kj)(lambda: step(True))

        @pl.when(qi == n - 1)
        def _():
            dk_ref[...] = dk_acc[...].astype(BF16)
            dv_ref[...] = dv_acc[...].astype(BF16)

        @pl.when(jnp.logical_and(kj == n - 1, qi == n - 1))
        def _():
            dq_ref[...] = dq_acc[...].astype(BF16)

    qrow = lambda h, j, i: jnp.maximum(i, j)
    return _pc(body, grid=(H, n, n),
               in_specs=[pl.BlockSpec((None, t, W), lambda h, j, i: (h, qrow(h, j, i), 0)),
                         pl.BlockSpec((None, t, W), lambda h, j, i: (h, j, 0)),
                         pl.BlockSpec((None, t, LANES), lambda h, j, i: (h, j, 0)),
                         pl.BlockSpec((t, LANES), lambda h, j, i: (qrow(h, j, i), h)),
                         pl.BlockSpec((None, t, LANES), lambda h, j, i: (h, qrow(h, j, i), 0)),
                         pl.BlockSpec((t, LANES), lambda h, j, i: (qrow(h, j, i), h))],
               out_specs=[pl.BlockSpec((None, S_, W), lambda h, j, i: (h, 0, 0)),
                          pl.BlockSpec((None, t, W), lambda h, j, i: (h, j, 0)),
                          pl.BlockSpec((None, t, LANES), lambda h, j, i: (h, j, 0))],
               out_shape=[_sds((H, S_, W), BF16), _sds((H, S_, W), BF16), _sds((H, S_, LANES), BF16)],
               scratch_shapes=[pltpu.VMEM((S_, W), F32), pltpu.VMEM((t, W), F32), pltpu.VMEM((t, LANES), F32)],
               compiler_params=_cparams(("parallel", "arbitrary", "arbitrary")), name=name)(qh, kh, vh, o, lse, do)


def loss_head(x, target, g, name):
    S_ = x.shape[0]
    ts = 256

    def body(x_ref, t_ref, g_ref, l_ref, dx_ref, dg_ref):
        @pl.when(pl.program_id(0) == 0)
        def _():
            l_ref[...] = jnp.zeros_like(l_ref)
            dg_ref[...] = jnp.zeros_like(dg_ref)

        y, vjp = jax.vjp(_rms, x_ref[...], g_ref[...])
        err = y - t_ref[...]
        l_ref[...] += 0.5 * jnp.sum(jnp.sum(err * err, axis=-1, keepdims=True), axis=0, keepdims=True) / D
        dx, dg = vjp(err / D)
        dx_ref[...] = dx
        dg_ref[...] += dg

    row = pl.BlockSpec((ts, D), lambda i: (i, 0))
    return _pc(body, grid=(S_ // ts,), in_specs=[row, row, pl.BlockSpec((1, D), lambda i: (0, 0))],
               out_specs=[pl.BlockSpec((1, LANES), lambda i: (0, 0)), row, pl.BlockSpec((1, D), lambda i: (0, 0))],
               out_shape=[_sds((1, LANES), F32), _sds((S_, D), F32), _sds((1, D), F32)],
               compiler_params=_cparams(("arbitrary",)), name=name)(x, target, g)


def adamw(w, parts, m, v, name):
    R, C = w.shape
    rows = [p.shape[1] for p in parts[0]]
    tr = R
    for cand in (512, 256, 128, 64, 32, 16, 8):
        if all(r % cand == 0 for r in rows) and cand * C * 4 <= 1024 * 1024:
            tr = cand
            break
    c1 = 1.0 - ADAM_B1 ** ADAM_STEP
    c2 = 1.0 - ADAM_B2 ** ADAM_STEP
    starts = [sum(rows[:k]) // tr for k in range(len(rows))]
    flat = [p for part in parts for p in part]

    def body(*refs):
        w_ref, m_ref, v_ref = refs[0], refs[1 + len(flat)], refs[2 + len(flat)]
        g_ref, d_ref, nm_ref, nv_ref = refs[3 + len(flat):]
        i = pl.program_id(0)
        gg, at = None, 1
        for part in parts:
            val = None
            for k in range(len(part)):
                p_ref = refs[at]
                at += 1
                s = p_ref[0].astype(F32)
                for n in range(1, p_ref.shape[0]):
                    s = s + p_ref[n].astype(F32)
                val = s if val is None else jnp.where(i >= starts[k], s, val)
            gg = val if gg is None else gg + val
        m2 = ADAM_B1 * m_ref[...] + (1.0 - ADAM_B1) * gg
        v2 = ADAM_B2 * v_ref[...] + (1.0 - ADAM_B2) * (gg * gg)
        g_ref[...] = gg
        d_ref[...] = -ADAM_LR * ((m2 / c1) / (jnp.sqrt(v2 / c2) + ADAM_EPS) + ADAM_WD * w_ref[...])
        nm_ref[...] = m2
        nv_ref[...] = v2

    blk = pl.BlockSpec((tr, C), lambda i: (i, 0))
    piece = lambda p, k: pl.BlockSpec((p.shape[0], tr, C), lambda i: (0, jnp.clip(i - starts[k], 0, rows[k] // tr - 1), 0))
    pblk = [piece(p, k) for part in parts for k, p in enumerate(part)]
    return _pc(body, grid=(R // tr,), in_specs=[blk] + pblk + [blk, blk], out_specs=[blk] * 4, out_shape=[_sds((R, C), F32)] * 4,
               compiler_params=_cparams(("parallel",)), name=name)(w, *flat, m, v)


def sum_slots(own, recv, skip, name):
    n, R, C = recv.shape
    tr = _tile(R, 512) if R % LANES == 0 else R
    has_own = own is not None

    def body(*refs):
        skip_ref = refs[0]
        r_ref, o_ref = refs[-2], refs[-1]
        acc = refs[1][...] if has_own else jnp.zeros(o_ref.shape, F32)
        for s in range(n):
            acc = acc + jnp.where(skip_ref[0] == s, 0.0, r_ref[s].astype(F32))
        o_ref[...] = acc

    row = pl.BlockSpec((tr, C), lambda i, sk: (i, 0))
    gs = pltpu.PrefetchScalarGridSpec(
        num_scalar_prefetch=1, grid=(R // tr,),
        in_specs=([row] if has_own else []) + [pl.BlockSpec((n, tr, C), lambda i, sk: (0, i, 0))], out_specs=row)
    ins = ([own] if has_own else []) + [recv]
    return _pc(body, grid_spec=gs, out_shape=_sds((R, C), F32), compiler_params=_cparams(("parallel",)), name=name)(skip, *ins)


def _chip_peers():
    x, y, c = lax.axis_index("x"), lax.axis_index("y"), lax.axis_index("c")
    return (x, y, c), [(1 - x, y, c), (x, 1 - y, c), (1 - x, 1 - y, c)]


def _chip_index(p):
    return 2 * p[0] + p[1]


def _win(ref, axis, chip, size):
    if axis is None:
        return ref.at[chip]
    idx = [slice(None)] * len(ref.shape)
    idx[axis] = pl.ds(pl.multiple_of(chip * size, size), size)
    return ref.at[tuple(idx)]


def _remote(src, dst, send_sem, recv_sem, peer):
    return pltpu.make_async_remote_copy(src_ref=src, dst_ref=dst, send_sem=send_sem, recv_sem=recv_sem, device_id=peer,
                                        device_id_type=MESH)


HBM_SPEC = pl.BlockSpec(memory_space=pltpu.HBM)
SEM_SPEC = pl.BlockSpec(memory_space=pltpu.SEMAPHORE)
ANY_SPEC = pl.BlockSpec(memory_space=pl.ANY)
DATAFLOW = pltpu.SideEffectType.DATAFLOW_SIDE_EFFECTING


def gather_piece(i, l, o, axis, size):
    return (i, lambda r, chip: r.at[l], o, lambda r, chip: _win(r, axis, chip, size))


def scatter_piece(i, o, axis, size):
    return (i, lambda r, chip: _win(r, axis, chip, size), o, lambda r, chip: r.at[chip])


def whole_piece(i):
    return (i, lambda r, chip: r, i, lambda r, chip: r)


def _copies(pieces, in_refs, out_refs, send, recv, sibling):
    me, peers = _chip_peers()
    if sibling:
        peers = [(me[0], me[1], 1 - me[2])]
    mine = _chip_index(me)
    remote = []
    for n, (i, src, o, dst) in enumerate(pieces):
        d = dst(out_refs[o], mine)
        remote += [_remote(src(in_refs[i], _chip_index(p)), d, send.at[len(peers) * n + k], recv.at[len(peers) * n + k], p)
                   for k, p in enumerate(peers)]
    return remote


def own_window(a, axis, size, chip):
    if axis is None:
        return lax.dynamic_index_in_dim(a, chip, 0, keepdims=False)
    return lax.dynamic_slice_in_dim(a, chip * size, size, axis=axis)


def place_own(land, own, axis, size, chip):
    if axis is None:
        return lax.dynamic_update_slice_in_dim(land, own[None], chip, axis=0)
    return lax.dynamic_update_slice_in_dim(land, own, chip * size, axis=axis)


def exchange_start(pieces, ins, out_shapes, after, name, sibling=False):
    n_in, n_out, ncp = len(ins), len(out_shapes), len(pieces)

    def body(*refs):
        in_refs, land_refs = refs[:n_in], refs[n_in:n_in + n_out]
        send, recv = refs[n_in + n_out + 1], refs[n_in + n_out + 2]
        token = refs[-1]
        for cp in _copies(pieces, in_refs, land_refs, send, recv, sibling):
            cp.start()
        token[...] = jnp.zeros_like(token)

    hbm = lambda a: pltpu.with_memory_space_constraint(a, pltpu.HBM)
    lands = [hbm(lax.empty(s.shape, s.dtype)) for s in out_shapes]
    sem = pltpu.SemaphoreType.DMA(((1 if sibling else 3) * ncp,))
    thru = [pltpu.HBM(a.shape, a.dtype) for a in ins] + [pltpu.HBM(s.shape, s.dtype) for s in out_shapes]
    res = _pc(body, in_specs=[HBM_SPEC] * (n_in + n_out) + [ANY_SPEC],
              out_specs=[SEM_SPEC, SEM_SPEC] + [HBM_SPEC] * (n_in + n_out) + [pl.BlockSpec(memory_space=pltpu.VMEM)],
              out_shape=[sem, sem] + thru + [_sds((8, LANES), F32)],
              input_output_aliases={i: 2 + i for i in range(n_in + n_out)},
              compiler_params=pltpu.CompilerParams(has_side_effects=DATAFLOW), name=name)(*[hbm(a) for a in ins], *lands, after)
    return (res[0], res[1]), list(res[2:2 + n_in]), list(res[2 + n_in:2 + n_in + n_out]), res[-1]


def exchange_wait(pieces, sems, ins, lands, after, name, sibling=False):
    n_in, n_out = len(ins), len(lands)

    def body(*refs):
        in_refs, land_refs = refs[:n_in], refs[n_in:n_in + n_out]
        send, recv = refs[n_in + n_out], refs[n_in + n_out + 1]
        for cp in _copies(pieces, in_refs, land_refs, send, recv, sibling):
            cp.wait_send()
            cp.wait_recv()

    thru = [pltpu.HBM(a.shape, a.dtype) for a in ins] + [pltpu.HBM(a.shape, a.dtype) for a in lands]
    res = _pc(body, in_specs=[HBM_SPEC] * (n_in + n_out) + [SEM_SPEC, SEM_SPEC, ANY_SPEC], out_specs=[HBM_SPEC] * (n_in + n_out),
              out_shape=thru, input_output_aliases={i: i for i in range(n_in + n_out)},
              compiler_params=pltpu.CompilerParams(has_side_effects=DATAFLOW), name=name)(*ins, *lands, sems[0], sems[1], after)
    return list(res[:n_in]), list(res[n_in:])


def exchange_all(buf, name):
    def body(in_ref, out_ref, send, recv, local):
        x, y, c = lax.axis_index("x"), lax.axis_index("y"), lax.axis_index("c")
        mine = 4 * x + 2 * y + c
        loc = pltpu.make_async_copy(in_ref, out_ref.at[mine], local)
        loc.start()
        copies = [loc]
        for k in range(1, 8):
            peer = (x ^ (k >> 2), y ^ ((k >> 1) & 1), c ^ (k & 1))
            cp = pltpu.make_async_remote_copy(src_ref=in_ref, dst_ref=out_ref.at[mine], send_sem=send.at[k - 1],
                                              recv_sem=recv.at[k - 1], device_id=peer, device_id_type=MESH)
            cp.start()
            copies.append(cp)
        for cp in copies:
            cp.wait()

    anyspec = pl.BlockSpec(memory_space=pl.ANY)
    return _pc(body, in_specs=[anyspec], out_specs=anyspec, out_shape=_sds((8,) + buf.shape, buf.dtype),
               scratch_shapes=[pltpu.SemaphoreType.DMA((7,)), pltpu.SemaphoreType.DMA((7,)), pltpu.SemaphoreType.DMA],
               name=name)(buf)


def _norm_fwd(x, g, name):
    return rowwise(f_rms, [(x, D, 0, 0)], [(g, D, 0, 0)], [(D, 0, BF16)], ts=512, name=name)[0]


def _norm_bwd(x, g, dh, dres, name):
    (dx,), (dg,) = rowwise_bwd(f_rms, [(x, D, 0, 0)], [(g, D, 0, 0)], [(dh, D, 0, 0)], need=[True],
                               adds={0: (dres, D, 0, 0)}, ts=256, name=name)
    return dx, dg


def pool_fwd(x, W, tag, late=None):
    h = _norm_fwd(x, W["ng"], tag + "_norm")
    proj = mm(h, W["w_in"], name=tag + "_in")
    if late is not None:
        W = dict(W, **late(proj))
    p = pool_time_fwd(proj, tag + "_win")
    pg = gmm("nn", p, W["w_grp"], G=4, name=tag + "_grp")
    y = rowwise(f_pool_gate, [(pg, POOL_GROUP, 0, 1), (proj, POOL_GROUP, 4, 1)], [(W["scale"], POOL_GROUP, 0, 1)],
                [(POOL_GROUP, 1, BF16)], ncol=4, ts=512, name=tag + "_gate")[0]
    xn = mm(y, W["w_out"], add=x, name=tag + "_out")
    return xn, (x, h, proj, p, pg, y)


def pool_bwd(dxn, W, saved, tag, after=None, emit=None):
    x, h, proj, p, pg, y = saved
    emit = emit or (lambda grads: None)
    dy = mm(dxn, W["w_out"], tb=True, after=after, name=tag + "_dy")
    g = {}
    (dpg, dproj), (g["scale"],) = rowwise_bwd(
        f_pool_gate, [(pg, POOL_GROUP, 0, 1), (proj, POOL_GROUP, 4, 1)], [(W["scale"], POOL_GROUP, 0, 1)],
        [(dy, POOL_GROUP, 0, 1)], need=[True, True], place={1: (2 * POOL_WIDTH, 4)}, narrow=(0, 1), ncol=4, ts=512, name=tag + "_dgate")
    dp = gmm("nt", dpg, W["w_grp"], G=4, name=tag + "_dp")
    dproj = pool_time_bwd(dp, dproj, tag + "_dwin")
    g["w_in"] = mm(h, dproj, ta=True, out_dtype=BF16, name=tag + "_dw_in")
    t1 = emit({"w_in": g["w_in"]})
    g["w_out"] = mm(y, dxn, ta=True, after=t1, out_dtype=BF16, name=tag + "_dwout")
    g["w_grp"] = gmm("tn", p, dpg, G=4, out_dtype=BF16, name=tag + "_dwgrp")
    t2 = emit({"w_out": g["w_out"], "w_grp": g["w_grp"]})
    dh = mm(dproj, W["w_in"], tb=True, after=t2, name=tag + "_dh")
    dx, g["ng"] = _norm_bwd(x, W["ng"], dh, dxn, tag + "_dnorm")
    return dx, g


def gdn_fwd(x, W, tag):
    h = _norm_fwd(x, W["ng"], tag + "_norm")
    proj = mm(h, W["w_in"], name=tag + "_in")
    qkv = gdn_conv_fwd(proj, W["conv"], tag + "_conv")
    g_b, beta_b = rowwise(f_gdn_gates, [(proj, LANES, 6144 // LANES, 0)], [(W["a_log"], LANES, 0, 0), (W["dt_bias"], LANES, 0, 0)],
                          [(GDN_QK, 0, F32), (GDN_QK, 0, F32)], ts=512, name=tag + "_gates")
    o, states = gdn_chunk_fwd(qkv, g_b, beta_b, tag + "_chunk")
    og = rowwise(f_gdn_out, [(o, GDN_DV, 0, 1), (proj, GDN_DV, 4096 // GDN_DV, 1)], [(W["norm_g"], GDN_DV, 0, 0)],
                 [(GDN_DV, 1, BF16)], ncol=GDN_H, ts=512, name=tag + "_onorm")[0]
    xn = mm(og, W["w_out"], add=x, name=tag + "_out")
    return xn, (x, h, proj, qkv, g_b, beta_b, o, states, og)


def gdn_bwd(dxn, W, saved, tag, after=None):
    x, h, proj, qkv, g_b, beta_b, o, states, og = saved
    dog = mm(dxn, W["w_out"], tb=True, after=after, name=tag + "_dog")
    g = {"w_out": mm(og, dxn, ta=True, out_dtype=BF16, name=tag + "_dwout")}
    (do, dproj), (g["norm_g"],) = rowwise_bwd(
        f_gdn_out, [(o, GDN_DV, 0, 1), (proj, GDN_DV, 4096 // GDN_DV, 1)], [(W["norm_g"], GDN_DV, 0, 0)],
        [(dog, GDN_DV, 0, 1)], need=[True, True], place={1: (GDN_IN_PAD, 4096 // GDN_DV)}, narrow=(1,), ncol=GDN_H, ts=512, name=tag + "_donorm")
    dq, dk, dv, dg_b, dbeta_b = gdn_chunk_bwd(qkv, g_b, beta_b, states, do, tag + "_dchunk")
    (dproj,), (g["a_log"], g["dt_bias"]) = rowwise_bwd(
        f_gdn_gates, [(proj, LANES, 6144 // LANES, 0)], [(W["a_log"], LANES, 0, 0), (W["dt_bias"], LANES, 0, 0)],
        [(dg_b, GDN_QK, 0, 0), (dbeta_b, GDN_QK, 0, 0)], need=[True], place={0: (dproj, 6144 // LANES)}, ts=256, name=tag + "_dgates")
    dproj, g["conv"] = gdn_conv_bwd(proj, W["conv"], dq, dk, dv, dproj, tag + "_dconv")
    dh = mm(dproj, W["w_in"], tb=True, name=tag + "_dh")
    g["w_in"] = mm(h, dproj, ta=True, out_dtype=BF16, name=tag + "_dw_in")
    dx, g["ng"] = _norm_bwd(x, W["ng"], dh, dxn, tag + "_dnorm")
    return dx, g


def mla_fwd(x, pos, W, tag):
    h = _norm_fwd(x, W["ng"], tag + "_norm")
    proj = mm(h, W["w_in"], name=tag + "_in")
    hq = rowwise(f_rms, [(proj, MLA_Q_LORA, 0, 0)], [(W["q_g"], MLA_Q_LORA, 0, 0)], [(MLA_Q_LORA, 0, BF16)], ts=512, name=tag + "_qnorm")[0]
    hkv = rowwise(f_rms, [(proj, MLA_KV_LORA, 2, 0)], [(W["kv_g"], MLA_KV_LORA, 0, 0)], [(MLA_KV_LORA, 0, BF16)], ts=512, name=tag + "_kvnorm")[0]
    qpad = mm(hq, W["w_uq"], name=tag + "_uq")
    kv = mm(hkv, W["w_ukv"], name=tag + "_ukv")
    qh, kh, vh = mla_prep_fwd(qpad, kv, proj, pos, W["rope"], tag + "_prep")
    o, lse = flash_fwd(qh, kh, vh, tag + "_attn")
    og = rowwise(f_ogate, [(o, 512, 0, 1), (proj, 512, 4, 1)], [], [(512, 1, BF16)], ncol=4, ts=512, name=tag + "_ogate")[0]
    xn = mm(og, W["w_out"], add=x, name=tag + "_out")
    return xn, (x, h, proj, hq, hkv, qh, kh, vh, o, lse, og)


def mla_bwd(dxn, pos, W, saved, tag, after=None):
    x, h, proj, hq, hkv, qh, kh, vh, o, lse, og = saved
    dog = mm(dxn, W["w_out"], tb=True, after=after, name=tag + "_dog")
    g = {"w_out": mm(og, dxn, ta=True, out_dtype=BF16, name=tag + "_dwout")}
    dproj = jnp.zeros(proj.shape, BF16)
    (do, dproj), _ = rowwise_bwd(f_ogate, [(o, 512, 0, 1), (proj, 512, 4, 1)], [], [(dog, 512, 0, 1)], need=[True, True],
                                 place={1: (dproj, 4)}, ncol=4, ts=512, name=tag + "_dogate")
    dqh, dkh, dvh = flash_bwd(qh, kh, vh, o, lse, do, tag + "_dattn")
    dqpad, dkv, dproj = mla_prep_bwd(dqh, dkh, dvh, pos, W["rope"], dproj, tag + "_dprep")
    dhq = mm(dqpad, W["w_uq"], tb=True, name=tag + "_dhq")
    g["w_uq"] = mm(hq, dqpad, ta=True, out_dtype=BF16, name=tag + "_dwuq")
    dhkv = mm(dkv, W["w_ukv"], tb=True, name=tag + "_dhkv")
    g["w_ukv"] = mm(hkv, dkv, ta=True, out_dtype=BF16, name=tag + "_dwukv")
    (dproj,), (g["q_g"],) = rowwise_bwd(f_rms, [(proj, MLA_Q_LORA, 0, 0)], [(W["q_g"], MLA_Q_LORA, 0, 0)], [(dhq, MLA_Q_LORA, 0, 0)],
                                        need=[True], place={0: (dproj, 0)}, ts=256, name=tag + "_dqnorm")
    (dproj,), (g["kv_g"],) = rowwise_bwd(f_rms, [(proj, MLA_KV_LORA, 2, 0)], [(W["kv_g"], MLA_KV_LORA, 0, 0)], [(dhkv, MLA_KV_LORA, 0, 0)],
                                         need=[True], place={0: (dproj, 2)}, ts=256, name=tag + "_dkvnorm")
    dh = mm(dproj, W["w_in"], tb=True, name=tag + "_dh")
    g["w_in"] = mm(h, dproj, ta=True, out_dtype=BF16, name=tag + "_dw_in")
    dx, g["ng"] = _norm_bwd(x, W["ng"], dh, dxn, tag + "_dnorm")
    return dx, g


def _pad_cols(a, n):
    return jnp.pad(a, ((0, 0), (0, n - a.shape[1])))


def _mla_w_in_layout(w):
    z = lambda n: jnp.zeros((w.shape[0], n), w.dtype)
    kr = w[:, 1280:1344]
    return jnp.concatenate([w[:, :768], z(256), w[:, 768:1280], kr[:, :32], z(32), kr[:, 32:], z(32), z(384), w[:, 1344:]], axis=1)


def _mla_w_in_unlayout(g):
    return jnp.concatenate([g[:, :768], g[:, 1024:1536], g[:, 1536:1568], g[:, 1600:1632], g[:, 2048:]], axis=1)


def _mla_w_uq_layout(w):
    w3 = w.reshape(w.shape[0], MLA_H, MLA_NOPE + MLA_ROPE)
    z = jnp.zeros((w.shape[0], MLA_H, 32), w.dtype)
    return jnp.concatenate([w3[..., :128], w3[..., 128:160], z, w3[..., 160:192], z], axis=-1).reshape(w.shape[0], MLA_H * 256)


def _mla_w_uq_unlayout(g):
    g3 = g.reshape(g.shape[0], MLA_H, 256)
    return jnp.concatenate([g3[..., :128], g3[..., 128:160], g3[..., 192:224]], axis=-1).reshape(g.shape[0], MLA_H * 192)


def _rope_consts():
    half = MLA_ROPE // 2
    inv = ROPE_THETA ** (-jnp.arange(half, dtype=F32) / half)
    z = jnp.zeros((half,), F32)
    o = jnp.ones((half,), F32)
    row = lambda *p: jnp.concatenate(p).reshape(1, LANES)
    return row(inv, z, inv, z), row(o, z, o, z), row(-o, z, o, z)


BIG = ["pool_w_in", "pool_w_grp", "pool_w_out", "gdn_w_in", "gdn_w_out", "mla_w_in", "mla_w_uq", "mla_w_ukv", "mla_w_out"]
BIG_LAYOUT = {"pool_w_in": (1, 1024, (1024, 4096)), "pool_w_grp": (1, 128, (4, 512, 512)), "pool_w_out": (0, 512, (2048, 1024)),
              "gdn_w_in": (None, None, (4, 1024, 1540)), "gdn_w_out": (0, 512, (2048, 1024)),
              "mla_w_in": (None, None, (4, 1024, 848)), "mla_w_uq": (1, 768, (768, 3072)), "mla_w_ukv": (1, 1024, (512, 4096)),
              "mla_w_out": (0, 512, (2048, 1024))}
SMALL_SHARDED = ["pool_scale", "gdn_conv", "mla_q_norm_g", "mla_kv_norm_g"]
SMALL_AXIS = {"pool_scale": 1, "gdn_conv": 2, "mla_q_norm_g": 1, "mla_kv_norm_g": 1}
REPLICATED = ["norm_g", "gdn_a_log", "gdn_dt_bias", "gdn_norm_g", "final_g"]
PACK_C = 1024


def _pack(parts, dtype, row_mult):
    flat = jnp.concatenate([p.reshape(-1).astype(dtype) for p in parts])
    rows = -(-flat.shape[0] // PACK_C)
    rows = -(-rows // row_mult) * row_mult
    return jnp.pad(flat, (0, rows * PACK_C - flat.shape[0])).reshape(rows, PACK_C)


def _unpack(buf, shapes):
    lead = buf.shape[:-2]
    flat = buf.reshape(lead + (-1,))
    out, off = [], 0
    for s in shapes:
        n = int(np.prod(s))
        out.append(flat[..., off:off + n].reshape(lead + tuple(s)))
        off += n
    return out


def _unshard(g4, axis):
    a = jnp.moveaxis(g4, 0, axis)
    s = a.shape
    return a.reshape(s[:axis] + (s[axis] * s[axis + 1],) + s[axis + 2:])


def _to_shards(a, axis):
    s = a.shape
    return jnp.moveaxis(a.reshape(s[:axis] + (4, s[axis] // 4) + s[axis + 1:]), axis, 0)


def layer_weights(full, small, rep, layer):
    ng = rep["norm_g"][layer:layer + 1]
    side_by_side = lambda a4: jnp.moveaxis(a4, 0, 1).reshape(a4.shape[1], 4 * a4.shape[2])
    if layer in (0, 3):
        j = layer // 3
        return dict(ng=ng, w_in=full[("pool_w_in", j)], w_grp=full[("pool_w_grp", j)], scale=small["pool_scale"][j:j + 1],
                    w_out=full[("pool_w_out", j)])
    if layer == 1:
        return dict(ng=ng, w_in=_pad_cols(side_by_side(full[("gdn_w_in", 0)]), GDN_IN_PAD),
                    conv=jnp.pad(small["gdn_conv"][0], ((0, 4), (0, 0))), a_log=_pad_cols(rep["gdn_a_log"], LANES),
                    dt_bias=_pad_cols(rep["gdn_dt_bias"], LANES), norm_g=rep["gdn_norm_g"], w_out=full[("gdn_w_out", 0)])
    return dict(ng=ng, w_in=_mla_w_in_layout(side_by_side(full[("mla_w_in", 0)])), q_g=small["mla_q_norm_g"],
                kv_g=small["mla_kv_norm_g"], w_uq=_mla_w_uq_layout(full[("mla_w_uq", 0)]), w_ukv=full[("mla_w_ukv", 0)],
                w_out=full[("mla_w_out", 0)], rope=_rope_consts())


def big_grad_pieces(gl):
    g0, g1, g2, g3 = gl
    slots = lambda a: jnp.moveaxis(a.reshape(a.shape[0], 4, a.shape[1] // 4), 1, 0)
    out = {}
    for l, g in ((0, g0), (1, g3)):
        if g is not None:
            out.update({("pool_w_in", l): g["w_in"], ("pool_w_grp", l): g["w_grp"], ("pool_w_out", l): g["w_out"]})
    if g1 is not None:
        out.update({("gdn_w_in", 0): slots(g1["w_in"][:, :GDN_IN]), ("gdn_w_out", 0): g1["w_out"]})
    if g2 is not None:
        out.update({("mla_w_in", 0): slots(_mla_w_in_unlayout(g2["w_in"])), ("mla_w_uq", 0): _mla_w_uq_unlayout(g2["w_uq"]),
                    ("mla_w_ukv", 0): g2["w_ukv"], ("mla_w_out", 0): g2["w_out"]})
    return out


def small_grads(gl, dfinal):
    g0, g1, g2, g3 = gl
    return {"norm_g": jnp.concatenate([g0["ng"], g1["ng"], g2["ng"], g3["ng"]], axis=0),
            "pool_scale": jnp.concatenate([g0["scale"], g3["scale"]], axis=0), "gdn_conv": g1["conv"][None, :4],
            "gdn_a_log": g1["a_log"][:, :GDN_H], "gdn_dt_bias": g1["dt_bias"][:, :GDN_H], "gdn_norm_g": g1["norm_g"],
            "mla_q_norm_g": g2["q_g"], "mla_kv_norm_g": g2["kv_g"], "final_g": dfinal.reshape(D)}


NAMES = ["norm_g", "pool_w_in", "pool_w_grp", "pool_scale", "pool_w_out", "gdn_w_in", "gdn_conv", "gdn_a_log", "gdn_dt_bias",
         "gdn_norm_g", "gdn_w_out", "mla_w_in", "mla_q_norm_g", "mla_w_uq", "mla_kv_norm_g", "mla_w_ukv", "mla_w_out", "final_g"]


def kernel(x, positions, norm_g, pool_w_in, pool_w_grp, pool_scale, pool_w_out, gdn_w_in, gdn_conv, gdn_a_log, gdn_dt_bias, gdn_norm_g, gdn_w_out, mla_w_in, mla_q_norm_g, mla_w_uq, mla_kv_norm_g, mla_w_ukv, mla_w_out, final_g, loss_target, m_norm_g, m_pool_w_in, m_pool_w_grp, m_pool_scale, m_pool_w_out, m_gdn_w_in, m_gdn_conv, m_gdn_a_log, m_gdn_dt_bias, m_gdn_norm_g, m_gdn_w_out, m_mla_w_in, m_mla_q_norm_g, m_mla_w_uq, m_mla_kv_norm_g, m_mla_w_ukv, m_mla_w_out, m_final_g, v_norm_g, v_pool_w_in, v_pool_w_grp, v_pool_scale, v_pool_w_out, v_gdn_w_in, v_gdn_conv, v_gdn_a_log, v_gdn_dt_bias, v_gdn_norm_g, v_gdn_w_out, v_mla_w_in, v_mla_q_norm_g, v_mla_w_uq, v_mla_kv_norm_g, v_mla_w_ukv, v_mla_w_out, v_final_g):
    args = locals()
    w = {n: args[n] for n in NAMES}
    m = {n: args["m_" + n] for n in NAMES}
    v = {n: args["v_" + n] for n in NAMES}
    my_chip = (2 * lax.axis_index("x") + lax.axis_index("y")).astype(I32)
    S_ = x.shape[1]
    x0, pos, target = x[0], positions.reshape(S_, 1).astype(F32), loss_target[0]
    rep = {n: w[n] for n in REPLICATED}

    shard = {(n, l): w[n][l:l + 1].astype(BF16) for n in BIG for l in range(w[n].shape[0])}
    small_shapes = [w[n].shape for n in SMALL_SHARDED]
    shard[("small", 0)] = _pack([w[n] for n in SMALL_SHARDED], F32, 8)[None]
    layout = dict(BIG_LAYOUT, small=(None, None, (4,) + shard[("small", 0)].shape[1:]))

    def gather_start(group, after, tag):
        pieces = [gather_piece(i, 0, i, layout[n][0], layout[n][1]) for i, (n, l) in enumerate(group)]
        shapes = [_sds(layout[n][2], shard[(n, l)].dtype) for n, l in group]
        sems, ins, lands, token = exchange_start(pieces, [shard[k] for k in group], shapes, after, tag + "_start")
        return (pieces, sems, ins, lands), token

    def finish(handle, after, tag):
        return exchange_wait(*handle, after, tag + "_wait")

    def gathered(group, handle, after, tag):
        srcs, lands = finish(handle, after, tag)
        return {(n, l): place_own(a, s[0], layout[n][0], layout[n][1], my_chip) for (n, l), s, a in zip(group, srcs, lands)}

    group_a = [("small", 0), ("pool_w_in", 0)]
    group_a2 = [("pool_w_grp", 0), ("pool_w_out", 0)]
    group_b = [("gdn_w_in", 0), ("gdn_w_out", 0)]
    group_c = [("mla_w_in", 0), ("mla_w_uq", 0), ("mla_w_ukv", 0), ("mla_w_out", 0), ("pool_w_in", 1), ("pool_w_grp", 1), ("pool_w_out", 1)]
    full = {}
    h_a, t_a = gather_start(group_a, x0, "gather_a")
    h_a2, t_a2 = gather_start(group_a2, t_a, "gather_a2")
    h_b, t_b = gather_start(group_b, t_a2, "gather_b")
    h_c, t_c = gather_start(group_c, t_b, "gather_c")
    full.update(gathered(group_a, h_a, t_c, "gather_a"))
    small = {n: _unshard(a, SMALL_AXIS[n]) for n, a in zip(SMALL_SHARDED, _unpack(full[("small", 0)], small_shapes))}

    def late_l0(proj):
        full.update(gathered(group_a2, h_a2, proj, "gather_a2"))
        return dict(w_grp=full[("pool_w_grp", 0)], w_out=full[("pool_w_out", 0)])

    x1, s0 = pool_fwd(x0, dict(ng=rep["norm_g"][0:1], w_in=full[("pool_w_in", 0)], scale=small["pool_scale"][0:1]), "l0", late=late_l0)
    W0 = layer_weights(full, small, rep, 0)
    full.update(gathered(group_b, h_b, x1, "gather_b"))
    W1 = layer_weights(full, small, rep, 1)
    x2, s1 = gdn_fwd(x1, W1, "l1")
    full.update(gathered(group_c, h_c, x2, "gather_c"))
    W2, W3 = layer_weights(full, small, rep, 2), layer_weights(full, small, rep, 3)
    x3, s2 = mla_fwd(x2, pos, W2, "l2")
    x4, s3 = pool_fwd(x3, W3, "l3")
    loss_part, dx4, dfinal = loss_head(x4, target, final_g.reshape(1, D), "loss_head")

    def scatter_start(pieces_of, after, tag):
        keys = list(pieces_of)
        pieces = [scatter_piece(i, i, BIG_LAYOUT[n][0], BIG_LAYOUT[n][1]) for i, (n, l) in enumerate(keys)]
        shapes = [_sds((4,) + tuple(w[n].shape[1:]), BF16) for n, l in keys]
        sems, ins, lands, token = exchange_start(pieces, [pieces_of[k] for k in keys], shapes, after, tag + "_start")
        return keys, (pieces, sems, ins, lands), token

    def scattered(keys, handle, after, tag):
        srcs, lands = finish(handle, after, tag)
        return {(n, l): place_own(a, own_window(g, BIG_LAYOUT[n][0], BIG_LAYOUT[n][1], my_chip), None, None, my_chip)
                for (n, l), g, a in zip(keys, srcs, lands)}

    dx3, g3 = pool_bwd(dx4, W3, s3, "l3")
    k3, h3, t3 = scatter_start(big_grad_pieces((None, None, None, g3)), dx3, "scatter_l3")
    dx2, g2 = mla_bwd(dx3, pos, W2, s2, "l2", after=t3)
    k2, h2, t2 = scatter_start(big_grad_pieces((None, None, g2, None)), dx2, "scatter_l2")
    dx1, g1 = gdn_bwd(dx2, W1, s1, "l1", after=t2)
    k1, h1, t1 = scatter_start(big_grad_pieces((None, g1, None, None)), dx1, "scatter_l1")
    def swap_start(part, tag):
        keys = list(part)
        ins = [part[k] for k in keys]
        pieces = [whole_piece(i) for i in range(len(keys))]
        sems, ins, lands, token = exchange_start(pieces, ins, [_sds(a.shape, a.dtype) for a in ins], ins[0], tag + "_start", sibling=True)
        swaps.append((keys, (pieces, sems, ins, lands), tag))
        return token

    last, swaps = [], []

    def emit_l0(grads):
        first = not last
        now = next(iter(grads.values()))
        early = [(k3, h3, "scatter_l3"), (k2, h2, "scatter_l2")] if first else [(k1, h1, "scatter_l1")]
        landed = {}
        for keys, handle, tag in early:
            landed.update(scattered(keys, handle, now, tag))
        swapping = swap_start(landed, "swap_a" if first else "swap_b")
        tag = "scatter_l0a" if first else "scatter_l0b"
        keys, handle, token = scatter_start({("pool_" + k, 0): a for k, a in grads.items()}, swapping, tag)
        last.append((keys, handle, tag))
        return token

    dx0, g0 = pool_bwd(dx1, W0, s0, "l0", after=t1, emit=emit_l0)
    landed = {}
    for keys, handle, tag in last:
        landed.update(scattered(keys, handle, dx0, tag))
    swap_start(landed, "swap_c")
    recv, sib = {}, {}
    for keys, handle, tag in swaps:
        mine_, theirs = exchange_wait(*handle, dx0, tag + "_wait", sibling=True)
        recv.update(zip(keys, mine_))
        sib.update(zip(keys, theirs))

    sg = small_grads((g0, g1, g2, g3), dfinal)
    small_names = SMALL_SHARDED + REPLICATED
    small_buf = _pack([sg[n] for n in small_names] + [loss_part], F32, 8)
    small_sum = sum_slots(None, exchange_all(small_buf, "gather_small"), jnp.full((1,), -1, I32), "sum_small")
    full_small = _unpack(small_sum, [sg[n].shape for n in small_names] + [(1, LANES)])
    loss = full_small[-1][0, 0]
    small_part = {}
    for n, a in zip(small_names, full_small[:-1]):
        if n in SMALL_AXIS:
            a = lax.dynamic_index_in_dim(_to_shards(a, SMALL_AXIS[n]), my_chip, axis=0, keepdims=False)
        small_part[n] = a

    outs = []
    for n in NAMES:
        shp = w[n].shape
        two = (int(np.prod(shp[:-1])), shp[-1]) if len(shp) > 1 else (1, shp[0])
        if n in BIG_LAYOUT:
            layers = shp[0]
            rows = lambda a: a.reshape(4, two[0] // layers, two[1])
            parts = [[rows(recv[(n, l)]) for l in range(layers)], [rows(sib[(n, l)]) for l in range(layers)]]
        else:
            parts = [[small_part[n].reshape((1,) + two)]]
        res = adamw(w[n].reshape(two), parts, m[n].reshape(two), v[n].reshape(two), "adamw_" + n)
        outs.append([r.reshape(shp) for r in res])
    return (loss, dx0[None], *[o[0] for o in outs], *[o[1] for o in outs], *[o[2] for o in outs], *[o[3] for o in outs])
```

```python
import functools
import math

import jax
import jax.numpy as jnp
import numpy as np
from jax import lax
from jax.experimental import pallas as pl
from jax.experimental.pallas import tpu as pltpu

F32 = jnp.float32
BF16 = jnp.bfloat16
I32 = jnp.int32

D = 1024
EPS = 1e-6
POOL_WIDTH = 2048
POOL_GROUP = 512
GDN_H, GDN_DK, GDN_DV, GDN_C = 8, 128, 256, 64
GDN_QK, GDN_V, GDN_CONV_CH, GDN_IN = 1024, 2048, 4096, 6160
GDN_IN_PAD = 6272
MLA_H, MLA_NOPE, MLA_ROPE, MLA_V = 16, 128, 64, 128
MLA_Q_LORA, MLA_KV_LORA, MLA_WIDTH, MLA_IN = 768, 512, 2048, 3392
MLA_IN_PAD = 4096
MLA_SCALE = (MLA_NOPE + MLA_ROPE) ** -0.5
ROPE_THETA = 10000.0
ADAM_LR, ADAM_B1, ADAM_B2, ADAM_EPS, ADAM_WD, ADAM_STEP = 0.001, 0.9, 0.999, 1e-08, 0.01, 10

VMEM_LIMIT_V7X = 56 * 1024 * 1024
LANES = 128
MESH = pl.DeviceIdType.MESH


def _pc(body, **kw):
    return pl.pallas_call(body, **kw)


def _cparams(sem):
    return pltpu.CompilerParams(dimension_semantics=sem, vmem_limit_bytes=VMEM_LIMIT_V7X)


def _tile(n, cap):
    t = (cap // LANES) * LANES
    while t >= LANES:
        if n % t == 0:
            return t
        t -= LANES
    return n


def _sds(shape, dt):
    return jax.ShapeDtypeStruct(shape, dt)


def mm(a, b, *, ta=False, tb=False, add=None, after=None, out_dtype=F32, name):
    if ta:
        K, M = a.shape
    else:
        M, K = a.shape
    if tb:
        N, K2 = b.shape
    else:
        K2, N = b.shape
    assert K == K2, (a.shape, b.shape, ta, tb)
    tm, tn, tk = _tile(M, 1024), _tile(N, 1024), _tile(K, 1024)
    nk = K // tk
    a_spec = pl.BlockSpec((tk, tm), lambda i, j, k: (k, i)) if ta else pl.BlockSpec((tm, tk), lambda i, j, k: (i, k))
    b_spec = pl.BlockSpec((tn, tk), lambda i, j, k: (j, k)) if tb else pl.BlockSpec((tk, tn), lambda i, j, k: (k, j))
    o_spec = pl.BlockSpec((tm, tn), lambda i, j, k: (i, j))
    dn = (((0 if ta else 1,), (1 if tb else 0,)), ((), ()))
    has_add = add is not None

    def body(*refs):
        a_ref, b_ref = refs[0], refs[1]
        part = lax.dot_general(a_ref[...].astype(BF16), b_ref[...].astype(BF16), dn, preferred_element_type=F32)
        if nk == 1:
            refs[-1][...] = (part + refs[2][...] if has_add else part).astype(out_dtype)
            return
        o_ref, acc = refs[-2], refs[-1]
        k = pl.program_id(2)

        @pl.when(k == 0)
        def _():
            acc[...] = part

        @pl.when(k > 0)
        def _():
            acc[...] += part

        @pl.when(k == nk - 1)
        def _():
            r = acc[...]
            if has_add:
                r = r + refs[2][...]
            o_ref[...] = r.astype(out_dtype)

    ins = [a, b] + ([add] if has_add else []) + ([after] if after is not None else [])
    specs = [a_spec, b_spec] + ([o_spec] if has_add else []) + ([pl.BlockSpec(memory_space=pl.ANY)] if after is not None else [])
    return _pc(body, grid=(M // tm, N // tn, nk), in_specs=specs, out_specs=o_spec, out_shape=_sds((M, N), out_dtype),
               scratch_shapes=[pltpu.VMEM((tm, tn), F32)] if nk > 1 else [], compiler_params=_cparams(("parallel", "parallel", "arbitrary")),
               name=name)(*ins)


def gmm(kind, a, b, *, G, name, out_dtype=F32):
    S_ = a.shape[0]
    Ka = a.shape[1] // G
    if kind == "tn":
        N = b.shape[1] // G
        tk = _tile(S_, 512)
        nk = S_ // tk

        def body(a_ref, b_ref, o_ref, acc):
            k = pl.program_id(1)

            @pl.when(k == 0)
            def _():
                acc[...] = jnp.zeros_like(acc)

            acc[...] += lax.dot_general(a_ref[...].astype(BF16), b_ref[...].astype(BF16), (((0,), (0,)), ((), ())),
                                        preferred_element_type=F32)

            @pl.when(k == nk - 1)
            def _():
                o_ref[...] = acc[...].astype(out_dtype)

        return _pc(body, grid=(G, nk),
                   in_specs=[pl.BlockSpec((tk, Ka), lambda g, k: (k, g)), pl.BlockSpec((tk, N), lambda g, k: (k, g))],
                   out_specs=pl.BlockSpec((None, Ka, N), lambda g, k: (g, 0, 0)), out_shape=_sds((G, Ka, N), out_dtype),
                   scratch_shapes=[pltpu.VMEM((Ka, N), F32)], compiler_params=_cparams(("parallel", "arbitrary")), name=name)(a, b)
    N = b.shape[2] if kind == "nn" else b.shape[1]
    tm = _tile(S_, 1024)
    dn = (((1,), (0 if kind == "nn" else 1,)), ((), ()))

    def body(a_ref, b_ref, o_ref):
        o_ref[...] = lax.dot_general(a_ref[...].astype(BF16), b_ref[...].astype(BF16), dn, preferred_element_type=F32)

    bshape = (None,) + tuple(b.shape[1:])
    return _pc(body, grid=(G, S_ // tm),
               in_specs=[pl.BlockSpec((tm, Ka), lambda g, i: (i, g)), pl.BlockSpec(bshape, lambda g, i: (g, 0, 0))],
               out_specs=pl.BlockSpec((tm, N), lambda g, i: (i, g)), out_shape=_sds((S_, G * N), F32),
               compiler_params=_cparams(("parallel", "parallel")), name=name)(a, b)


def _rw_spec(ts, w, c, s):
    return pl.BlockSpec((ts, w), lambda j, i: (i, c + j * s))


def _rw_pspec(p, w, c, s):
    return pl.BlockSpec((p.shape[0], w), lambda j, i: (0, c + j * s))


def rowwise(f, tiles, params, outs, *, ncol=1, ts, name):
    S_ = tiles[0][0].shape[0]
    nin = len(tiles) + len(params)

    def body(*refs):
        res = f(pl.program_id(0), *[r[...] for r in refs[:nin]])
        for r, o in zip(refs[nin:], res):
            r[...] = o.astype(r.dtype)

    return _pc(body, grid=(ncol, S_ // ts),
               in_specs=[_rw_spec(ts, w, c, s) for (_, w, c, s) in tiles] + [_rw_pspec(*p) for p in params],
               out_specs=[_rw_spec(ts, w, 0, s) for (w, s, _) in outs],
               out_shape=[_sds((S_, w * (ncol if s else 1)), dt) for (w, s, dt) in outs],
               compiler_params=_cparams(("parallel", "parallel")), name=name)(*[t[0] for t in tiles], *[p[0] for p in params])


def rowwise_bwd(f, tiles, params, cots, *, need, adds=None, place=None, narrow=(), ncol=1, ts, name):
    S_ = tiles[0][0].shape[0]
    adds = adds or {}
    place = place or {}
    nt, npar, nc = len(tiles), len(params), len(cots)
    add_keys = sorted(adds)
    need_idx = [k for k in range(nt) if need[k]]
    into_keys = [k for k in need_idx if k in place and not isinstance(place[k][0], int)]
    n_extra = len(add_keys) + len(into_keys)

    def body(*refs):
        j, i = pl.program_id(0), pl.program_id(1)
        vals = [r[...] for r in refs[:nt + npar]]
        cvals = tuple(r[...] for r in refs[nt + npar:nt + npar + nc])
        add_refs = refs[nt + npar + nc:nt + npar + nc + len(add_keys)]
        out_refs = refs[nt + npar + nc + n_extra:]
        _, vjp = jax.vjp(lambda *v: tuple(f(j, *v)), *vals)
        grads = vjp(cvals)
        for n, k in enumerate(need_idx):
            g = grads[k]
            if k in adds:
                g = g + add_refs[add_keys.index(k)][...]
            out_refs[n][...] = g.astype(out_refs[n].dtype)
        for n in range(npar):
            ref = out_refs[len(need_idx) + n]
            first = (i == 0) if params[n][3] else jnp.logical_and(i == 0, j == 0)

            @pl.when(first)
            def _():
                ref[...] = jnp.zeros_like(ref)

            ref[...] += grads[nt + n]

    in_specs = ([_rw_spec(ts, w, c, s) for (_, w, c, s) in tiles] + [_rw_pspec(*p) for p in params]
                + [_rw_spec(ts, w, c, s) for (_, w, c, s) in cots] + [_rw_spec(ts, *adds[k][1:]) for k in add_keys]
                + [pl.BlockSpec(memory_space=pl.ANY) for _ in into_keys])
    out_specs, out_shape, aliases = [], [], {}
    for n, k in enumerate(need_idx):
        w, s = tiles[k][1], tiles[k][3]
        if k in place:
            dst, c0 = place[k]
            total = dst if isinstance(dst, int) else dst.shape[1]
            out_specs.append(_rw_spec(ts, w, c0, s))
            out_shape.append(_sds((S_, total), (BF16 if k in narrow else F32) if isinstance(dst, int) else dst.dtype))
            if k in into_keys:
                aliases[nt + npar + nc + len(add_keys) + into_keys.index(k)] = n
        else:
            out_specs.append(_rw_spec(ts, w, 0, s))
            out_shape.append(_sds((S_, w * (ncol if s else 1)), BF16 if k in narrow else F32))
    out_specs += [_rw_pspec(p[0], p[1], p[2], p[3]) for p in params]
    out_shape += [_sds(p[0].shape, F32) for p in params]
    res = _pc(body, grid=(ncol, S_ // ts), in_specs=in_specs, out_specs=out_specs, out_shape=out_shape,
              input_output_aliases=aliases, compiler_params=_cparams(("arbitrary", "arbitrary")), name=name)(
        *[t[0] for t in tiles], *[p[0] for p in params], *[c[0] for c in cots], *[adds[k][0] for k in add_keys],
        *[place[k][0] for k in into_keys])
    return list(res[:len(need_idx)]), list(res[len(need_idx):])


def _rms(x, g):
    r = lax.rsqrt(jnp.mean(x * x, axis=-1, keepdims=True) + EPS)
    return x * r * g


def _silu(x):
    return x * jax.nn.sigmoid(x)


@jax.custom_vjp
def _softplus(x):
    return jnp.maximum(x, 0.0) + jnp.log1p(jnp.exp(-jnp.abs(x)))


_softplus.defvjp(lambda x: (_softplus(x), x), lambda x, d: (d * jax.nn.sigmoid(x),))


def f_rms(j, x, g):
    return (_rms(x, g),)


def f_pool_gate(j, pg, gate, scale):
    return (pg * scale * _silu(gate),)


def f_ogate(j, o, gate):
    return (o * _silu(gate),)


def f_gdn_out(j, o, gate, g):
    return (_rms(o, g) * _silu(gate),)


def f_gdn_gates(j, ba, alog, dtb):
    lane = lax.broadcasted_iota(I32, (1, LANES), 1)
    gs, bs = [], []
    for h in range(GDN_H):
        eb = (lane == h).astype(F32)
        ea = (lane == GDN_H + h).astype(F32)
        b = jnp.sum(ba * eb, -1, keepdims=True)
        a = jnp.sum(ba * ea, -1, keepdims=True)
        al = jnp.sum(alog * eb, -1, keepdims=True)
        dt = jnp.sum(dtb * eb, -1, keepdims=True)
        g = -jnp.exp(al) * _softplus(a + dt)
        gs.append(jnp.broadcast_to(g, ba.shape))
        bs.append(jnp.broadcast_to(jax.nn.sigmoid(b), ba.shape))
    return jnp.concatenate(gs, 1), jnp.concatenate(bs, 1)


def _shift_dn(x, k):
    rows = lax.broadcasted_iota(I32, x.shape, 0)
    return jnp.where(rows < k, 0.0, pltpu.roll(x, k, 0))


def _shift_up(x, k):
    n = x.shape[0]
    rows = lax.broadcasted_iota(I32, x.shape, 0)
    return jnp.where(rows >= n - k, 0.0, pltpu.roll(x, n - k, 0))


def _pool_window(j):
    g = lax.div(j, POOL_GROUP // LANES)
    return jnp.where(g == 0, 2.0, jnp.where(g == 1, 4.0, jnp.where(g == 2, 8.0, 16.0))), g


def _pick(g, a2, a4, a8, a16):
    return jnp.where(g == 0, a2, jnp.where(g == 1, a4, jnp.where(g == 2, a8, a16)))


def pool_time_fwd(proj, name):
    S_ = proj.shape[0]

    def body(u_ref, p_ref):
        u = u_ref[...]
        w, g = _pool_window(pl.program_id(0))
        s2 = u + _shift_dn(u, 1)
        s4 = s2 + _shift_dn(s2, 2)
        s8 = s4 + _shift_dn(s4, 4)
        s16 = s8 + _shift_dn(s8, 8)
        t1 = (lax.broadcasted_iota(I32, u.shape, 0) + 1).astype(F32)
        p_ref[...] = (_pick(g, s2, s4, s8, s16) / jnp.minimum(t1, w) - u).astype(p_ref.dtype)

    return _pc(body, grid=(POOL_WIDTH // LANES,), in_specs=[pl.BlockSpec((S_, LANES), lambda j: (0, j))],
               out_specs=pl.BlockSpec((S_, LANES), lambda j: (0, j)), out_shape=_sds((S_, POOL_WIDTH), BF16),
               compiler_params=_cparams(("parallel",)), name=name)(proj)


def pool_time_bwd(dp, into, name):
    S_ = dp.shape[0]

    def body(dp_ref, _, du_ref):
        d = dp_ref[...]
        w, g = _pool_window(pl.program_id(0))
        t1 = (lax.broadcasted_iota(I32, d.shape, 0) + 1).astype(F32)
        q = d / jnp.minimum(t1, w)
        r2 = q + _shift_up(q, 1)
        r4 = r2 + _shift_up(r2, 2)
        r8 = r4 + _shift_up(r4, 4)
        r16 = r8 + _shift_up(r8, 8)
        du_ref[...] = (_pick(g, r2, r4, r8, r16) - d).astype(du_ref.dtype)

    return _pc(body, grid=(POOL_WIDTH // LANES,),
               in_specs=[pl.BlockSpec((S_, LANES), lambda j: (0, j)), pl.BlockSpec(memory_space=pl.ANY)],
               out_specs=pl.BlockSpec((S_, LANES), lambda j: (0, j)), out_shape=_sds(into.shape, into.dtype),
               input_output_aliases={1: 0}, compiler_params=_cparams(("parallel",)), name=name)(dp, into)


def _conv_post(j, a):
    n = a * lax.rsqrt(jnp.sum(a * a, axis=-1, keepdims=True) + EPS)
    nq = GDN_QK // LANES
    return jnp.where(j < nq, n * (GDN_DK ** -0.5), jnp.where(j < 2 * nq, n, a))


def _conv_taps(u):
    return [_shift_dn(u, 3), _shift_dn(u, 2), _shift_dn(u, 1), u]


def _conv_pre(taps, w):
    return w[0:1] * taps[0] + w[1:2] * taps[1] + w[2:3] * taps[2] + w[3:4] * taps[3]


def gdn_conv_fwd(proj, conv_w, name):
    S_ = proj.shape[0]

    def body(u_ref, w_ref, o_ref):
        o_ref[...] = _conv_post(pl.program_id(0), _silu(_conv_pre(_conv_taps(u_ref[...]), w_ref[...])))

    return _pc(body, grid=(GDN_CONV_CH // LANES,),
               in_specs=[pl.BlockSpec((S_, LANES), lambda j: (0, j)), pl.BlockSpec((8, LANES), lambda j: (0, j))],
               out_specs=pl.BlockSpec((S_, LANES), lambda j: (0, j)), out_shape=_sds((S_, GDN_CONV_CH), F32),
               compiler_params=_cparams(("parallel",)), name=name)(proj, conv_w)


def gdn_conv_bwd(proj, conv_w, dq, dk, dv, into, name):
    S_ = proj.shape[0]
    nq = GDN_QK // LANES

    def body(u_ref, w_ref, dq_ref, dk_ref, dv_ref, _, du_ref, dw_ref):
        j = pl.program_id(0)
        u, w = u_ref[...], w_ref[...]
        taps = _conv_taps(u)
        c = _conv_pre(taps, w)
        sig = jax.nn.sigmoid(c)
        dout = jnp.where(j < nq, dq_ref[...], jnp.where(j < 2 * nq, dk_ref[...], dv_ref[...]))
        _, vjp = jax.vjp(lambda a: _conv_post(j, a), c * sig)
        dc = vjp(dout)[0] * (sig * (1.0 + c * (1.0 - sig)))
        du = w[3:4] * dc + w[2:3] * _shift_up(dc, 1) + w[1:2] * _shift_up(dc, 2) + w[0:1] * _shift_up(dc, 3)
        du_ref[...] = du.astype(du_ref.dtype)
        rows = lax.broadcasted_iota(I32, (8, LANES), 0)
        dw = jnp.zeros((8, LANES), F32)
        for k in range(4):
            dw = dw + jnp.where(rows == k, jnp.sum(dc * taps[k], axis=0, keepdims=True), 0.0)
        dw_ref[...] = dw

    blk = lambda f: pl.BlockSpec((S_, LANES), f)
    return _pc(body, grid=(GDN_CONV_CH // LANES,),
               in_specs=[blk(lambda j: (0, j)), pl.BlockSpec((8, LANES), lambda j: (0, j)),
                         blk(lambda j: (0, jnp.minimum(j, nq - 1))), blk(lambda j: (0, jnp.clip(j - nq, 0, nq - 1))),
                         blk(lambda j: (0, jnp.clip(j - 2 * nq, 0, 2 * nq - 1))), pl.BlockSpec(memory_space=pl.ANY)],
               out_specs=[blk(lambda j: (0, j)), pl.BlockSpec((8, LANES), lambda j: (0, j))],
               out_shape=[_sds(into.shape, into.dtype), _sds((8, GDN_CONV_CH), F32)], input_output_aliases={5: 0},
               compiler_params=_cparams(("parallel",)), name=name)(proj, conv_w, dq, dk, dv, into)


_NN, _NT, _TN = ((1,), (0,)), ((1,), (1,)), ((0,), (0,))


def _split(x, n):
    parts = []
    for _ in range(n):
        h = x.astype(BF16)
        parts.append(h)
        x = x - h.astype(F32)
    return parts


def _dot(a, b, dn, mode):
    d = lambda p, q: lax.dot_general(p, q, (dn, ((), ())), preferred_element_type=F32)
    if mode == "lo":
        return d(a.astype(BF16), b.astype(BF16))
    if mode == "x3":
        (ah, al), (bh, bl) = _split(a, 2), _split(b, 2)
        return d(ah, bh) + (d(ah, bl) + d(al, bh))
    b0, b1, b2 = _split(b, 3)
    ab = a.astype(BF16)
    return d(ab, b0) + (d(ab, b1) + d(ab, b2))


def _make_dots(mode):
    @jax.custom_vjp
    def nn(a, b):
        return _dot(a, b, _NN, mode)

    @jax.custom_vjp
    def nt(a, b):
        return _dot(a, b, _NT, mode)

    @jax.custom_vjp
    def tn(a, b):
        return _dot(a, b, _TN, mode)

    nn.defvjp(lambda a, b: (nn(a, b), (a, b)), lambda r, d: (nt(d, r[1]), tn(r[0], d)))
    nt.defvjp(lambda a, b: (nt(a, b), (a, b)), lambda r, d: (nn(d, r[1]), tn(d, r[0])))
    tn.defvjp(lambda a, b: (tn(a, b), (a, b)), lambda r, d: (nt(r[1], d), nn(r[0], d)))
    return nn, nt, tn


_nn_hi, _nt_hi, _tn_hi = _make_dots("x3")
_nn_lo, _nt_lo, _tn_lo = _make_dots("lo")


@jax.custom_vjp
def _nn_const(a, b):
    return _dot(a, b, _NN, "xl")


_nn_const.defvjp(lambda a, b: (_nn_const(a, b), a), lambda a, d: (jnp.zeros_like(a), _dot(a, d, _TN, "xl")))


def _each(f, *lists):
    return [f(*xs) for xs in zip(*lists)]


@jax.custom_vjp
def _unit_inverses(xs):
    C = xs[0].shape[0]
    eye = (lax.broadcasted_iota(I32, (C, C), 0) == lax.broadcasted_iota(I32, (C, C), 1)).astype(F32)
    ainv, p = [eye + a for a in xs], xs
    for _ in range(int(math.log2(C)) - 1):
        p = _each(lambda a: _dot(a, a, _NN, "x3"), p)
        ainv = _each(lambda a, b: a + _dot(a, b, _NN, "x3"), ainv, p)
    return ainv


def _unit_inverses_bwd(ainv, d):
    left = _each(lambda a, g: _dot(a, g, _TN, "x3"), ainv, d)
    return (_each(lambda t, a: _dot(t, a, _NT, "x3"), left, ainv),)


_unit_inverses.defvjp(lambda xs: (lambda a: (a, a))(_unit_inverses(xs)), _unit_inverses_bwd)


def _gdn_chunk(q, k, v, gb, bb, state):
    C = GDN_C
    e0 = (lax.broadcasted_iota(I32, (1, LANES), 1) == 0).astype(F32)
    ri = lax.broadcasted_iota(I32, (C, C), 0)
    ci = lax.broadcasted_iota(I32, (C, C), 1)
    causal, strict = ri >= ci, ri > ci
    tri, eye, ones = causal.astype(F32), (ri == ci).astype(F32), jnp.ones((C, C), F32)
    last = lax.broadcasted_iota(I32, (C, LANES), 0) == C - 1
    g1 = _each(lambda a: jnp.sum(a * e0, -1, keepdims=True), gb)
    b1 = _each(lambda a: jnp.sum(a * e0, -1, keepdims=True), bb)
    gc_c = _each(lambda g: _nn_const(tri, jnp.broadcast_to(g, (C, C))), g1)
    gc_d = _each(lambda g: _nn_const(tri, jnp.broadcast_to(g, (C, LANES))), g1)
    gr_c = _each(lambda g: _nn_const(ones, eye * g), gc_c)
    decay = _each(lambda a, r: jnp.where(causal, jnp.exp(jnp.where(causal, a - r, 0.0)), 0.0), gc_c, gr_c)
    kb = _each(lambda a, b: a * b, k, b1)
    vb = _each(lambda a, b: a * b, v, b1)
    x = _each(lambda a, b, d: -jnp.where(strict, _nt_lo(a, b) * d, 0.0), kb, k, decay)
    ainv = _unit_inverses(x)
    u = _each(_nn_hi, ainv, vb)
    w = _each(lambda a, b, g: _nn_hi(a, b * jnp.exp(g)), ainv, kb, gc_d)
    attn = _each(lambda a, b, d: jnp.where(causal, _nt_lo(a, b) * d, 0.0), q, k, decay)
    v_new = _each(lambda a, b, s: a - _nn_lo(b, s), u, w, state)
    o = _each(lambda a, g, s, t, vn: _nn_lo(a * jnp.exp(g), s) + _nn_lo(t, vn), q, gc_d, state, attn, v_new)
    gl = _each(lambda g: jnp.sum(jnp.where(last, g, 0.0), axis=0, keepdims=True), gc_d)
    new_state = _each(lambda s, g, a, gd, vn: s * jnp.exp(jnp.sum(g * e0, -1, keepdims=True)) + _tn_lo(a * jnp.exp(g - gd), vn),
                      state, gl, k, gc_d, v_new)
    return o, new_state


def _head_slices(ref, width):
    return [ref[:, h * width:(h + 1) * width] for h in range(GDN_H)]


def gdn_chunk_fwd(qkv, g_b, beta_b, name):
    S_ = qkv.shape[0]
    N = S_ // GDN_C

    def body(q_ref, k_ref, v_ref, g_ref, b_ref, o_ref, s_ref, state):
        @pl.when(pl.program_id(0) == 0)
        def _():
            state[...] = jnp.zeros_like(state)

        st = [state[h] for h in range(GDN_H)]
        s_ref[0] = state[...]
        o, st2 = _gdn_chunk(_head_slices(q_ref, GDN_DK), _head_slices(k_ref, GDN_DK), _head_slices(v_ref, GDN_DV),
                            _head_slices(g_ref, GDN_DK), _head_slices(b_ref, GDN_DK), st)
        for h in range(GDN_H):
            o_ref[:, h * GDN_DV:(h + 1) * GDN_DV] = o[h]
            state[h] = st2[h]

    return _pc(body, grid=(N,),
               in_specs=[pl.BlockSpec((GDN_C, GDN_QK), lambda n: (n, 0)), pl.BlockSpec((GDN_C, GDN_QK), lambda n: (n, 1)),
                         pl.BlockSpec((GDN_C, GDN_V), lambda n: (n, 1)), pl.BlockSpec((GDN_C, GDN_QK), lambda n: (n, 0)),
                         pl.BlockSpec((GDN_C, GDN_QK), lambda n: (n, 0))],
               out_specs=[pl.BlockSpec((GDN_C, GDN_V), lambda n: (n, 0)),
                          pl.BlockSpec((1, GDN_H, GDN_DK, GDN_DV), lambda n: (n, 0, 0, 0))],
               out_shape=[_sds((S_, GDN_V), F32), _sds((N, GDN_H, GDN_DK, GDN_DV), F32)],
               scratch_shapes=[pltpu.VMEM((GDN_H, GDN_DK, GDN_DV), F32)],
               compiler_params=_cparams(("arbitrary",)), name=name)(qkv, qkv, qkv, g_b, beta_b)


def gdn_chunk_bwd(qkv, g_b, beta_b, states, do, name):
    S_ = qkv.shape[0]
    N = S_ // GDN_C

    def body(q_ref, k_ref, v_ref, g_ref, b_ref, s_ref, do_ref, dq_ref, dk_ref, dv_ref, dg_ref, db_ref, dstate):
        @pl.when(pl.program_id(0) == 0)
        def _():
            dstate[...] = jnp.zeros_like(dstate)

        _, vjp = jax.vjp(_gdn_chunk, _head_slices(q_ref, GDN_DK), _head_slices(k_ref, GDN_DK), _head_slices(v_ref, GDN_DV),
                         _head_slices(g_ref, GDN_DK), _head_slices(b_ref, GDN_DK), [s_ref[0, h] for h in range(GDN_H)])
        dq, dk, dv, dg, db, ds = vjp((_head_slices(do_ref, GDN_DV), [dstate[h] for h in range(GDN_H)]))
        for h in range(GDN_H):
            kk, vv = slice(h * GDN_DK, (h + 1) * GDN_DK), slice(h * GDN_DV, (h + 1) * GDN_DV)
            dq_ref[:, kk] = dq[h]
            dk_ref[:, kk] = dk[h]
            dv_ref[:, vv] = dv[h]
            dg_ref[:, kk] = dg[h]
            db_ref[:, kk] = db[h]
            dstate[h] = ds[h]

    r = lambda n: N - 1 - n
    qk = lambda c: pl.BlockSpec((GDN_C, GDN_QK), lambda n: (r(n), c))
    vs = lambda c: pl.BlockSpec((GDN_C, GDN_V), lambda n: (r(n), c))
    return _pc(body, grid=(N,),
               in_specs=[qk(0), qk(1), vs(1), qk(0), qk(0),
                         pl.BlockSpec((1, GDN_H, GDN_DK, GDN_DV), lambda n: (r(n), 0, 0, 0)), vs(0)],
               out_specs=[qk(0), qk(0), vs(0), qk(0), qk(0)],
               out_shape=[_sds((S_, GDN_QK), F32), _sds((S_, GDN_QK), F32), _sds((S_, GDN_V), F32),
                          _sds((S_, GDN_QK), F32), _sds((S_, GDN_QK), F32)],
               scratch_shapes=[pltpu.VMEM((GDN_H, GDN_DK, GDN_DV), F32)],
               compiler_params=_cparams(("arbitrary",)), name=name)(qkv, qkv, qkv, g_b, beta_b, states, do)


def _rope_tables(pos_ref, inv_ref, cm_ref, sg_ref):
    ang = pos_ref[...] * inv_ref[...]
    return jnp.cos(ang) * cm_ref[...], jnp.sin(ang) * sg_ref[...]


def mla_prep_fwd(qpad, kv, proj, pos, rope_consts, name):
    S_ = qpad.shape[0]
    ts = 256
    W = 2 * LANES

    def body(q_ref, kv_ref, kr_ref, pos_ref, inv_ref, cm_ref, sg_ref, qh_ref, kh_ref, vh_ref):
        cs, sn = _rope_tables(pos_ref, inv_ref, cm_ref, sg_ref)
        rope = lambda r: r * cs + pltpu.roll(r, LANES // 2, 1) * sn
        krr = rope(kr_ref[...]).astype(BF16)
        for h in range(MLA_H):
            qh_ref[h, :, 0:LANES] = (q_ref[:, h * W:h * W + LANES] * MLA_SCALE).astype(BF16)
            qh_ref[h, :, LANES:W] = (rope(q_ref[:, h * W + LANES:(h + 1) * W]) * MLA_SCALE).astype(BF16)
            kh_ref[h, :, 0:LANES] = kv_ref[:, h * W:h * W + LANES].astype(BF16)
            kh_ref[h, :, LANES:W] = krr
            vh_ref[h] = kv_ref[:, h * W + LANES:(h + 1) * W].astype(BF16)

    one = pl.BlockSpec((1, LANES), lambda i: (0, 0))
    return _pc(body, grid=(S_ // ts,),
               in_specs=[pl.BlockSpec((ts, MLA_H * W), lambda i: (i, 0)), pl.BlockSpec((ts, MLA_H * W), lambda i: (i, 0)),
                         pl.BlockSpec((ts, LANES), lambda i: (i, 1536 // LANES)), pl.BlockSpec((ts, 1), lambda i: (i, 0)),
                         one, one, one],
               out_specs=[pl.BlockSpec((MLA_H, ts, W), lambda i: (0, i, 0)), pl.BlockSpec((MLA_H, ts, W), lambda i: (0, i, 0)),
                          pl.BlockSpec((MLA_H, ts, LANES), lambda i: (0, i, 0))],
               out_shape=[_sds((MLA_H, S_, W), BF16), _sds((MLA_H, S_, W), BF16), _sds((MLA_H, S_, LANES), BF16)],
               compiler_params=_cparams(("parallel",)), name=name)(qpad, kv, proj, pos, *rope_consts)


def mla_prep_bwd(dqh, dkh, dvh, pos, rope_consts, into, name):
    S_ = dqh.shape[1]
    ts = 256
    W = 2 * LANES

    def body(dq_ref, dk_ref, dv_ref, pos_ref, inv_ref, cm_ref, sg_ref, _, dqp_ref, dkv_ref, dkr_ref):
        cs, sn = _rope_tables(pos_ref, inv_ref, cm_ref, sg_ref)
        rope_t = lambda g: g * cs + pltpu.roll(g * sn, LANES // 2, 1)
        acc = jnp.zeros((ts, LANES), F32)
        for h in range(MLA_H):
            dqp_ref[:, h * W:h * W + LANES] = (dq_ref[h, :, 0:LANES].astype(F32) * MLA_SCALE).astype(BF16)
            dqp_ref[:, h * W + LANES:(h + 1) * W] = (rope_t(dq_ref[h, :, LANES:W].astype(F32)) * MLA_SCALE).astype(BF16)
            dkv_ref[:, h * W:h * W + LANES] = dk_ref[h, :, 0:LANES]
            dkv_ref[:, h * W + LANES:(h + 1) * W] = dv_ref[h]
            acc = acc + dk_ref[h, :, LANES:W].astype(F32)
        dkr_ref[...] = rope_t(acc).astype(dkr_ref.dtype)

    one = pl.BlockSpec((1, LANES), lambda i: (0, 0))
    return _pc(body, grid=(S_ // ts,),
               in_specs=[pl.BlockSpec((MLA_H, ts, W), lambda i: (0, i, 0)), pl.BlockSpec((MLA_H, ts, W), lambda i: (0, i, 0)),
                         pl.BlockSpec((MLA_H, ts, LANES), lambda i: (0, i, 0)), pl.BlockSpec((ts, 1), lambda i: (i, 0)),
                         one, one, one, pl.BlockSpec(memory_space=pl.ANY)],
               out_specs=[pl.BlockSpec((ts, MLA_H * W), lambda i: (i, 0)), pl.BlockSpec((ts, MLA_H * W), lambda i: (i, 0)),
                          pl.BlockSpec((ts, LANES), lambda i: (i, 1536 // LANES))],
               out_shape=[_sds((S_, MLA_H * W), BF16), _sds((S_, MLA_H * W), BF16), _sds(into.shape, into.dtype)],
               input_output_aliases={7: 2}, compiler_params=_cparams(("parallel",)), name=name)(dqh, dkh, dvh, pos, *rope_consts, into)


NEG = -1e30


FLASH_TILE = 1024
FLASH_SUB = 512


def _scores(q, k, diagonal):
    s = lax.dot_general(q, k, (_NT, ((), ())), preferred_element_type=F32)
    if not diagonal:
        return s
    return jnp.where(lax.broadcasted_iota(I32, s.shape, 1) <= lax.broadcasted_iota(I32, s.shape, 0), s, NEG)


def _sub_blocks(t, diagonal):
    sub = min(FLASH_SUB, t) if diagonal else t
    return [(c * sub if diagonal else 0, slice(c * sub, (c + 1) * sub)) for c in range(t // sub)]


FLASH_HEADS = 2


def flash_fwd(qh, kh, vh, name):
    H, S_, W = qh.shape
    t = _tile(S_, FLASH_TILE)
    n = S_ // t
    G = FLASH_HEADS
    heads = list(range(G))

    def body(q_ref, k_ref, v_ref, o_ref, lse_ref, m_s, l_s, acc):
        qi, kj = pl.program_id(1), pl.program_id(2)

        @pl.when(kj == 0)
        def _():
            m_s[...] = jnp.full_like(m_s, NEG)
            l_s[...] = jnp.zeros_like(l_s)
            acc[...] = jnp.zeros_like(acc)

        def step(diagonal):
            s = _each(lambda a: _scores(q_ref[a], k_ref[a], diagonal), heads)
            m_old = _each(lambda a: m_s[a], heads)
            m_new = _each(lambda mo, sa: jnp.maximum(mo, jnp.max(sa, axis=-1, keepdims=True)), m_old, s)
            alpha = _each(lambda mo, mn: jnp.exp(mo - mn), m_old, m_new)
            p = _each(lambda sa, mn: jnp.exp(sa - mn[:, :1]), s, m_new)
            pv = _each(lambda pa, a: lax.dot_general(pa.astype(BF16), v_ref[a], (_NN, ((), ())), preferred_element_type=F32), p, heads)
            for a in heads:
                l_s[a] = alpha[a] * l_s[a] + jnp.sum(p[a], axis=-1, keepdims=True)
                acc[a] = alpha[a] * acc[a] + pv[a]
                m_s[a] = m_new[a]

        pl.when(kj < qi)(lambda: step(False))
        pl.when(kj == qi)(lambda: step(True))

        @pl.when(kj == n - 1)
        def _():
            for a in heads:
                o_ref[:, a * LANES:(a + 1) * LANES] = acc[a] / l_s[a]
                lse_ref[a] = m_s[a] + jnp.log(l_s[a])

    return _pc(body, grid=(H // G, n, n),
               in_specs=[pl.BlockSpec((G, t, W), lambda h, i, j: (h, i, 0)),
                         pl.BlockSpec((G, t, W), lambda h, i, j: (h, jnp.minimum(i, j), 0)),
                         pl.BlockSpec((G, t, LANES), lambda h, i, j: (h, jnp.minimum(i, j), 0))],
               out_specs=[pl.BlockSpec((t, G * LANES), lambda h, i, j: (i, h)), pl.BlockSpec((G, t, LANES), lambda h, i, j: (h, i, 0))],
               out_shape=[_sds((S_, H * LANES), F32), _sds((H, S_, LANES), F32)],
               scratch_shapes=[pltpu.VMEM((G, t, LANES), F32)] * 3,
               compiler_params=_cparams(("parallel", "parallel", "arbitrary")), name=name)(qh, kh, vh)


def flash_bwd(qh, kh, vh, o, lse, do, name):
    H, S_, W = qh.shape
    t = _tile(S_, FLASH_TILE)
    n = S_ // t

    def body(q_ref, k_ref, v_ref, o_ref, lse_ref, do_ref, dq_ref, dk_ref, dv_ref, dq_acc, dk_acc, dv_acc):
        kj, qi = pl.program_id(1), pl.program_id(2)

        @pl.when(jnp.logical_and(kj == 0, qi == 0))
        def _():
            dq_acc[...] = jnp.zeros_like(dq_acc)

        @pl.when(qi == 0)
        def _():
            dk_acc[...] = jnp.zeros_like(dk_acc)
            dv_acc[...] = jnp.zeros_like(dv_acc)

        def step(diagonal):
            do_ = do_ref[...]
            dob = do_.astype(BF16)
            delta = jnp.sum(do_ * o_ref[...], axis=-1, keepdims=True)
            for r0, keys in _sub_blocks(t, diagonal):
                q, k, v = q_ref[r0:, :], k_ref[keys, :], v_ref[keys, :]
                p = jnp.exp(_scores(q, k, diagonal) - lse_ref[r0:, :1])
                dv_acc[keys, :] += lax.dot_general(p.astype(BF16), dob[r0:], (_TN, ((), ())), preferred_element_type=F32)
                dp = lax.dot_general(dob[r0:], v, (_NT, ((), ())), preferred_element_type=F32)
                ds = (p * (dp - delta[r0:])).astype(BF16)
                dk_acc[keys, :] += lax.dot_general(ds, q, (_TN, ((), ())), preferred_element_type=F32)
                rows = pl.ds(pl.multiple_of(qi * t, t) + r0, t - r0)
                dq_acc[rows, :] += lax.dot_general(ds, k, (_NN, ((), ())), preferred_element_type=F32)

        pl.when(qi > kj)(lambda: step(False))
        pl.when(qi == kj)(lambda: step(True))

        @pl.when(qi == n - 1)
        def _():
            dk_ref[...] = dk_acc[...].astype(BF16)
            dv_ref[...] = dv_acc[...].astype(BF16)

        @pl.when(jnp.logical_and(kj == n - 1, qi == n - 1))
        def _():
            dq_ref[...] = dq_acc[...].astype(BF16)

    qrow = lambda h, j, i: jnp.maximum(i, j)
    return _pc(body, grid=(H, n, n),
               in_specs=[pl.BlockSpec((None, t, W), lambda h, j, i: (h, qrow(h, j, i), 0)),
                         pl.BlockSpec((None, t, W), lambda h, j, i: (h, j, 0)),
                         pl.BlockSpec((None, t, LANES), lambda h, j, i: (h, j, 0)),
                         pl.BlockSpec((t, LANES), lambda h, j, i: (qrow(h, j, i), h)),
                         pl.BlockSpec((None, t, LANES), lambda h, j, i: (h, qrow(h, j, i), 0)),
                         pl.BlockSpec((t, LANES), lambda h, j, i: (qrow(h, j, i), h))],
               out_specs=[pl.BlockSpec((None, S_, W), lambda h, j, i: (h, 0, 0)),
                          pl.BlockSpec((None, t, W), lambda h, j, i: (h, j, 0)),
                          pl.BlockSpec((None, t, LANES), lambda h, j, i: (h, j, 0))],
               out_shape=[_sds((H, S_, W), BF16), _sds((H, S_, W), BF16), _sds((H, S_, LANES), BF16)],
               scratch_shapes=[pltpu.VMEM((S_, W), F32), pltpu.VMEM((t, W), F32), pltpu.VMEM((t, LANES), F32)],
               compiler_params=_cparams(("parallel", "arbitrary", "arbitrary")), name=name)(qh, kh, vh, o, lse, do)


def loss_head(x, target, g, name):
    S_ = x.shape[0]
    ts = 256

    def body(x_ref, t_ref, g_ref, l_ref, dx_ref, dg_ref):
        @pl.when(pl.program_id(0) == 0)
        def _():
            l_ref[...] = jnp.zeros_like(l_ref)
            dg_ref[...] = jnp.zeros_like(dg_ref)

        y, vjp = jax.vjp(_rms, x_ref[...], g_ref[...])
        err = y - t_ref[...]
        l_ref[...] += 0.5 * jnp.sum(jnp.sum(err * err, axis=-1, keepdims=True), axis=0, keepdims=True) / D
        dx, dg = vjp(err / D)
        dx_ref[...] = dx
        dg_ref[...] += dg

    row = pl.BlockSpec((ts, D), lambda i: (i, 0))
    return _pc(body, grid=(S_ // ts,), in_specs=[row, row, pl.BlockSpec((1, D), lambda i: (0, 0))],
               out_specs=[pl.BlockSpec((1, LANES), lambda i: (0, 0)), row, pl.BlockSpec((1, D), lambda i: (0, 0))],
               out_shape=[_sds((1, LANES), F32), _sds((S_, D), F32), _sds((1, D), F32)],
               compiler_params=_cparams(("arbitrary",)), name=name)(x, target, g)


def adamw(w, parts, m, v, name):
    R, C = w.shape
    rows = [p.shape[1] for p in parts[0]]
    tr = R
    for cand in (512, 256, 128, 64, 32, 16, 8):
        if all(r % cand == 0 for r in rows) and cand * C * 4 <= 1024 * 1024:
            tr = cand
            break
    c1 = 1.0 - ADAM_B1 ** ADAM_STEP
    c2 = 1.0 - ADAM_B2 ** ADAM_STEP
    starts = [sum(rows[:k]) // tr for k in range(len(rows))]
    flat = [p for part in parts for p in part]

    def body(*refs):
        w_ref, m_ref, v_ref = refs[0], refs[1 + len(flat)], refs[2 + len(flat)]
        g_ref, d_ref, nm_ref, nv_ref = refs[3 + len(flat):]
        i = pl.program_id(0)
        gg, at = None, 1
        for part in parts:
            val = None
            for k in range(len(part)):
                p_ref = refs[at]
                at += 1
                s = p_ref[0].astype(F32)
                for n in range(1, p_ref.shape[0]):
                    s = s + p_ref[n].astype(F32)
                val = s if val is None else jnp.where(i >= starts[k], s, val)
            gg = val if gg is None else gg + val
        m2 = ADAM_B1 * m_ref[...] + (1.0 - ADAM_B1) * gg
        v2 = ADAM_B2 * v_ref[...] + (1.0 - ADAM_B2) * (gg * gg)
        g_ref[...] = gg
        d_ref[...] = -ADAM_LR * ((m2 / c1) / (jnp.sqrt(v2 / c2) + ADAM_EPS) + ADAM_WD * w_ref[...])
        nm_ref[...] = m2
        nv_ref[...] = v2

    blk = pl.BlockSpec((tr, C), lambda i: (i, 0))
    piece = lambda p, k: pl.BlockSpec((p.shape[0], tr, C), lambda i: (0, jnp.clip(i - starts[k], 0, rows[k] // tr - 1), 0))
    pblk = [piece(p, k) for part in parts for k, p in enumerate(part)]
    return _pc(body, grid=(R // tr,), in_specs=[blk] + pblk + [blk, blk], out_specs=[blk] * 4, out_shape=[_sds((R, C), F32)] * 4,
               compiler_params=_cparams(("parallel",)), name=name)(w, *flat, m, v)


def sum_slots(own, recv, skip, name):
    n, R, C = recv.shape
    tr = _tile(R, 512) if R % LANES == 0 else R
    has_own = own is not None

    def body(*refs):
        skip_ref = refs[0]
        r_ref, o_ref = refs[-2], refs[-1]
        acc = refs[1][...] if has_own else jnp.zeros(o_ref.shape, F32)
        for s in range(n):
            acc = acc + jnp.where(skip_ref[0] == s, 0.0, r_ref[s].astype(F32))
        o_ref[...] = acc

    row = pl.BlockSpec((tr, C), lambda i, sk: (i, 0))
    gs = pltpu.PrefetchScalarGridSpec(
        num_scalar_prefetch=1, grid=(R // tr,),
        in_specs=([row] if has_own else []) + [pl.BlockSpec((n, tr, C), lambda i, sk: (0, i, 0))], out_specs=row)
    ins = ([own] if has_own else []) + [recv]
    return _pc(body, grid_spec=gs, out_shape=_sds((R, C), F32), compiler_params=_cparams(("parallel",)), name=name)(skip, *ins)


def _chip_peers():
    x, y, c = lax.axis_index("x"), lax.axis_index("y"), lax.axis_index("c")
    return (x, y, c), [(1 - x, y, c), (x, 1 - y, c), (1 - x, 1 - y, c)]


def _chip_index(p):
    return 2 * p[0] + p[1]


def _win(ref, axis, chip, size):
    if axis is None:
        return ref.at[chip]
    idx = [slice(None)] * len(ref.shape)
    idx[axis] = pl.ds(pl.multiple_of(chip * size, size), size)
    return ref.at[tuple(idx)]


def _remote(src, dst, send_sem, recv_sem, peer):
    return pltpu.make_async_remote_copy(src_ref=src, dst_ref=dst, send_sem=send_sem, recv_sem=recv_sem, device_id=peer,
                                        device_id_type=MESH)


HBM_SPEC = pl.BlockSpec(memory_space=pltpu.HBM)
SEM_SPEC = pl.BlockSpec(memory_space=pltpu.SEMAPHORE)
ANY_SPEC = pl.BlockSpec(memory_space=pl.ANY)
DATAFLOW = pltpu.SideEffectType.DATAFLOW_SIDE_EFFECTING


def gather_piece(i, l, o, axis, size):
    return (i, lambda r, chip: r.at[l], o, lambda r, chip: _win(r, axis, chip, size))


def scatter_piece(i, o, axis, size):
    return (i, lambda r, chip: _win(r, axis, chip, size), o, lambda r, chip: r.at[chip])


def whole_piece(i):
    return (i, lambda r, chip: r, i, lambda r, chip: r)


def _copies(pieces, in_refs, out_refs, send, recv, sibling):
    me, peers = _chip_peers()
    if sibling:
        peers = [(me[0], me[1], 1 - me[2])]
    mine = _chip_index(me)
    remote = []
    for n, (i, src, o, dst) in enumerate(pieces):
        d = dst(out_refs[o], mine)
        remote += [_remote(src(in_refs[i], _chip_index(p)), d, send.at[len(peers) * n + k], recv.at[len(peers) * n + k], p)
                   for k, p in enumerate(peers)]
    return remote


def own_window(a, axis, size, chip):
    if axis is None:
        return lax.dynamic_index_in_dim(a, chip, 0, keepdims=False)
    return lax.dynamic_slice_in_dim(a, chip * size, size, axis=axis)


def place_own(land, own, axis, size, chip):
    if axis is None:
        return lax.dynamic_update_slice_in_dim(land, own[None], chip, axis=0)
    return lax.dynamic_update_slice_in_dim(land, own, chip * size, axis=axis)


def exchange_start(pieces, ins, out_shapes, after, name, sibling=False):
    n_in, n_out, ncp = len(ins), len(out_shapes), len(pieces)

    def body(*refs):
        in_refs, land_refs = refs[:n_in], refs[n_in:n_in + n_out]
        send, recv = refs[n_in + n_out + 1], refs[n_in + n_out + 2]
        token = refs[-1]
        for cp in _copies(pieces, in_refs, land_refs, send, recv, sibling):
            cp.start()
        token[...] = jnp.zeros_like(token)

    hbm = lambda a: pltpu.with_memory_space_constraint(a, pltpu.HBM)
    lands = [hbm(lax.empty(s.shape, s.dtype)) for s in out_shapes]
    sem = pltpu.SemaphoreType.DMA(((1 if sibling else 3) * ncp,))
    thru = [pltpu.HBM(a.shape, a.dtype) for a in ins] + [pltpu.HBM(s.shape, s.dtype) for s in out_shapes]
    res = _pc(body, in_specs=[HBM_SPEC] * (n_in + n_out) + [ANY_SPEC],
              out_specs=[SEM_SPEC, SEM_SPEC] + [HBM_SPEC] * (n_in + n_out) + [pl.BlockSpec(memory_space=pltpu.VMEM)],
              out_shape=[sem, sem] + thru + [_sds((8, LANES), F32)],
              input_output_aliases={i: 2 + i for i in range(n_in + n_out)},
              compiler_params=pltpu.CompilerParams(has_side_effects=DATAFLOW), name=name)(*[hbm(a) for a in ins], *lands, after)
    return (res[0], res[1]), list(res[2:2 + n_in]), list(res[2 + n_in:2 + n_in + n_out]), res[-1]


def exchange_wait(pieces, sems, ins, lands, after, name, sibling=False):
    n_in, n_out = len(ins), len(lands)

    def body(*refs):
        in_refs, land_refs = refs[:n_in], refs[n_in:n_in + n_out]
        send, recv = refs[n_in + n_out], refs[n_in + n_out + 1]
        for cp in _copies(pieces, in_refs, land_refs, send, recv, sibling):
            cp.wait_send()
            cp.wait_recv()

    thru = [pltpu.HBM(a.shape, a.dtype) for a in ins] + [pltpu.HBM(a.shape, a.dtype) for a in lands]
    res = _pc(body, in_specs=[HBM_SPEC] * (n_in + n_out) + [SEM_SPEC, SEM_SPEC, ANY_SPEC], out_specs=[HBM_SPEC] * (n_in + n_out),
              out_shape=thru, input_output_aliases={i: i for i in range(n_in + n_out)},
              compiler_params=pltpu.CompilerParams(has_side_effects=DATAFLOW), name=name)(*ins, *lands, sems[0], sems[1], after)
    return list(res[:n_in]), list(res[n_in:])


def exchange_all(buf, name):
    def body(in_ref, out_ref, send, recv, local):
        x, y, c = lax.axis_index("x"), lax.axis_index("y"), lax.axis_index("c")
        mine = 4 * x + 2 * y + c
        loc = pltpu.make_async_copy(in_ref, out_ref.at[mine], local)
        loc.start()
        copies = [loc]
        for k in range(1, 8):
            peer = (x ^ (k >> 2), y ^ ((k >> 1) & 1), c ^ (k & 1))
            cp = pltpu.make_async_remote_copy(src_ref=in_ref, dst_ref=out_ref.at[mine], send_sem=send.at[k - 1],
                                              recv_sem=recv.at[k - 1], device_id=peer, device_id_type=MESH)
            cp.start()
            copies.append(cp)
        for cp in copies:
            cp.wait()

    anyspec = pl.BlockSpec(memory_space=pl.ANY)
    return _pc(body, in_specs=[anyspec], out_specs=anyspec, out_shape=_sds((8,) + buf.shape, buf.dtype),
               scratch_shapes=[pltpu.SemaphoreType.DMA((7,)), pltpu.SemaphoreType.DMA((7,)), pltpu.SemaphoreType.DMA],
               name=name)(buf)


def _norm_fwd(x, g, name):
    return rowwise(f_rms, [(x, D, 0, 0)], [(g, D, 0, 0)], [(D, 0, BF16)], ts=512, name=name)[0]


def _norm_bwd(x, g, dh, dres, name):
    (dx,), (dg,) = rowwise_bwd(f_rms, [(x, D, 0, 0)], [(g, D, 0, 0)], [(dh, D, 0, 0)], need=[True],
                               adds={0: (dres, D, 0, 0)}, ts=256, name=name)
    return dx, dg


def pool_fwd(x, W, tag, late=None):
    h = _norm_fwd(x, W["ng"], tag + "_norm")
    proj = mm(h, W["w_in"], name=tag + "_in")
    if late is not None:
        W = dict(W, **late(proj))
    p = pool_time_fwd(proj, tag + "_win")
    pg = gmm("nn", p, W["w_grp"], G=4, name=tag + "_grp")
    y = rowwise(f_pool_gate, [(pg, POOL_GROUP, 0, 1), (proj, POOL_GROUP, 4, 1)], [(W["scale"], POOL_GROUP, 0, 1)],
                [(POOL_GROUP, 1, BF16)], ncol=4, ts=512, name=tag + "_gate")[0]
    xn = mm(y, W["w_out"], add=x, name=tag + "_out")
    return xn, (x, h, proj, p, pg, y)


def pool_bwd(dxn, W, saved, tag, after=None, emit=None):
    x, h, proj, p, pg, y = saved
    emit = emit or (lambda grads: None)
    dy = mm(dxn, W["w_out"], tb=True, after=after, name=tag + "_dy")
    g = {}
    (dpg, dproj), (g["scale"],) = rowwise_bwd(
        f_pool_gate, [(pg, POOL_GROUP, 0, 1), (proj, POOL_GROUP, 4, 1)], [(W["scale"], POOL_GROUP, 0, 1)],
        [(dy, POOL_GROUP, 0, 1)], need=[True, True], place={1: (2 * POOL_WIDTH, 4)}, narrow=(0, 1), ncol=4, ts=512, name=tag + "_dgate")
    dp = gmm("nt", dpg, W["w_grp"], G=4, name=tag + "_dp")
    dproj = pool_time_bwd(dp, dproj, tag + "_dwin")
    g["w_in"] = mm(h, dproj, ta=True, out_dtype=BF16, name=tag + "_dw_in")
    t1 = emit({"w_in": g["w_in"]})
    g["w_out"] = mm(y, dxn, ta=True, after=t1, out_dtype=BF16, name=tag + "_dwout")
    g["w_grp"] = gmm("tn", p, dpg, G=4, out_dtype=BF16, name=tag + "_dwgrp")
    t2 = emit({"w_out": g["w_out"], "w_grp": g["w_grp"]})
    dh = mm(dproj, W["w_in"], tb=True, after=t2, name=tag + "_dh")
    dx, g["ng"] = _norm_bwd(x, W["ng"], dh, dxn, tag + "_dnorm")
    return dx, g


def gdn_fwd(x, W, tag, late=None):
    h = _norm_fwd(x, W["ng"], tag + "_norm")
    proj = mm(h, W["w_in"], name=tag + "_in")
    qkv = gdn_conv_fwd(proj, W["conv"], tag + "_conv")
    g_b, beta_b = rowwise(f_gdn_gates, [(proj, LANES, 6144 // LANES, 0)], [(W["a_log"], LANES, 0, 0), (W["dt_bias"], LANES, 0, 0)],
                          [(GDN_QK, 0, F32), (GDN_QK, 0, F32)], ts=512, name=tag + "_gates")
    o, states = gdn_chunk_fwd(qkv, g_b, beta_b, tag + "_chunk")
    og = rowwise(f_gdn_out, [(o, GDN_DV, 0, 1), (proj, GDN_DV, 4096 // GDN_DV, 1)], [(W["norm_g"], GDN_DV, 0, 0)],
                 [(GDN_DV, 1, BF16)], ncol=GDN_H, ts=512, name=tag + "_onorm")[0]
    if late is not None:
        W = dict(W, **late(og))
    xn = mm(og, W["w_out"], add=x, name=tag + "_out")
    return xn, (x, h, proj, qkv, g_b, beta_b, o, states, og)


def gdn_bwd(dxn, W, saved, tag, after=None):
    x, h, proj, qkv, g_b, beta_b, o, states, og = saved
    dog = mm(dxn, W["w_out"], tb=True, after=after, name=tag + "_dog")
    g = {"w_out": mm(og, dxn, ta=True, out_dtype=BF16, name=tag + "_dwout")}
    (do, dproj), (g["norm_g"],) = rowwise_bwd(
        f_gdn_out, [(o, GDN_DV, 0, 1), (proj, GDN_DV, 4096 // GDN_DV, 1)], [(W["norm_g"], GDN_DV, 0, 0)],
        [(dog, GDN_DV, 0, 1)], need=[True, True], place={1: (GDN_IN_PAD, 4096 // GDN_DV)}, narrow=(1,), ncol=GDN_H, ts=512, name=tag + "_donorm")
    dq, dk, dv, dg_b, dbeta_b = gdn_chunk_bwd(qkv, g_b, beta_b, states, do, tag + "_dchunk")
    (dproj,), (g["a_log"], g["dt_bias"]) = rowwise_bwd(
        f_gdn_gates, [(proj, LANES, 6144 // LANES, 0)], [(W["a_log"], LANES, 0, 0), (W["dt_bias"], LANES, 0, 0)],
        [(dg_b, GDN_QK, 0, 0), (dbeta_b, GDN_QK, 0, 0)], need=[True], place={0: (dproj, 6144 // LANES)}, ts=256, name=tag + "_dgates")
    dproj, g["conv"] = gdn_conv_bwd(proj, W["conv"], dq, dk, dv, dproj, tag + "_dconv")
    dh = mm(dproj, W["w_in"], tb=True, name=tag + "_dh")
    g["w_in"] = mm(h, dproj, ta=True, out_dtype=BF16, name=tag + "_dw_in")
    dx, g["ng"] = _norm_bwd(x, W["ng"], dh, dxn, tag + "_dnorm")
    return dx, g


def mla_fwd(x, pos, W, tag):
    h = _norm_fwd(x, W["ng"], tag + "_norm")
    proj = mm(h, W["w_in"], name=tag + "_in")
    hq = rowwise(f_rms, [(proj, MLA_Q_LORA, 0, 0)], [(W["q_g"], MLA_Q_LORA, 0, 0)], [(MLA_Q_LORA, 0, BF16)], ts=512, name=tag + "_qnorm")[0]
    hkv = rowwise(f_rms, [(proj, MLA_KV_LORA, 2, 0)], [(W["kv_g"], MLA_KV_LORA, 0, 0)], [(MLA_KV_LORA, 0, BF16)], ts=512, name=tag + "_kvnorm")[0]
    qpad = mm(hq, W["w_uq"], name=tag + "_uq")
    kv = mm(hkv, W["w_ukv"], name=tag + "_ukv")
    qh, kh, vh = mla_prep_fwd(qpad, kv, proj, pos, W["rope"], tag + "_prep")
    o, lse = flash_fwd(qh, kh, vh, tag + "_attn")
    og = rowwise(f_ogate, [(o, 512, 0, 1), (proj, 512, 4, 1)], [], [(512, 1, BF16)], ncol=4, ts=512, name=tag + "_ogate")[0]
    xn = mm(og, W["w_out"], add=x, name=tag + "_out")
    return xn, (x, h, proj, hq, hkv, qh, kh, vh, o, lse, og)


def mla_bwd(dxn, pos, W, saved, tag, after=None):
    x, h, proj, hq, hkv, qh, kh, vh, o, lse, og = saved
    dog = mm(dxn, W["w_out"], tb=True, after=after, name=tag + "_dog")
    g = {"w_out": mm(og, dxn, ta=True, out_dtype=BF16, name=tag + "_dwout")}
    dproj = jnp.zeros(proj.shape, BF16)
    (do, dproj), _ = rowwise_bwd(f_ogate, [(o, 512, 0, 1), (proj, 512, 4, 1)], [], [(dog, 512, 0, 1)], need=[True, True],
                                 place={1: (dproj, 4)}, ncol=4, ts=512, name=tag + "_dogate")
    dqh, dkh, dvh = flash_bwd(qh, kh, vh, o, lse, do, tag + "_dattn")
    dqpad, dkv, dproj = mla_prep_bwd(dqh, dkh, dvh, pos, W["rope"], dproj, tag + "_dprep")
    dhq = mm(dqpad, W["w_uq"], tb=True, name=tag + "_dhq")
    g["w_uq"] = mm(hq, dqpad, ta=True, out_dtype=BF16, name=tag + "_dwuq")
    dhkv = mm(dkv, W["w_ukv"], tb=True, name=tag + "_dhkv")
    g["w_ukv"] = mm(hkv, dkv, ta=True, out_dtype=BF16, name=tag + "_dwukv")
    (dproj,), (g["q_g"],) = rowwise_bwd(f_rms, [(proj, MLA_Q_LORA, 0, 0)], [(W["q_g"], MLA_Q_LORA, 0, 0)], [(dhq, MLA_Q_LORA, 0, 0)],
                                        need=[True], place={0: (dproj, 0)}, ts=256, name=tag + "_dqnorm")
    (dproj,), (g["kv_g"],) = rowwise_bwd(f_rms, [(proj, MLA_KV_LORA, 2, 0)], [(W["kv_g"], MLA_KV_LORA, 0, 0)], [(dhkv, MLA_KV_LORA, 0, 0)],
                                         need=[True], place={0: (dproj, 2)}, ts=256, name=tag + "_dkvnorm")
    dh = mm(dproj, W["w_in"], tb=True, name=tag + "_dh")
    g["w_in"] = mm(h, dproj, ta=True, out_dtype=BF16, name=tag + "_dw_in")
    dx, g["ng"] = _norm_bwd(x, W["ng"], dh, dxn, tag + "_dnorm")
    return dx, g


def _pad_cols(a, n):
    return jnp.pad(a, ((0, 0), (0, n - a.shape[1])))


def _mla_w_in_layout(w):
    z = lambda n: jnp.zeros((w.shape[0], n), w.dtype)
    kr = w[:, 1280:1344]
    return jnp.concatenate([w[:, :768], z(256), w[:, 768:1280], kr[:, :32], z(32), kr[:, 32:], z(32), z(384), w[:, 1344:]], axis=1)


def _mla_w_in_unlayout(g):
    return jnp.concatenate([g[:, :768], g[:, 1024:1536], g[:, 1536:1568], g[:, 1600:1632], g[:, 2048:]], axis=1)


def _mla_w_uq_layout(w):
    w3 = w.reshape(w.shape[0], MLA_H, MLA_NOPE + MLA_ROPE)
    z = jnp.zeros((w.shape[0], MLA_H, 32), w.dtype)
    return jnp.concatenate([w3[..., :128], w3[..., 128:160], z, w3[..., 160:192], z], axis=-1).reshape(w.shape[0], MLA_H * 256)


def _mla_w_uq_unlayout(g):
    g3 = g.reshape(g.shape[0], MLA_H, 256)
    return jnp.concatenate([g3[..., :128], g3[..., 128:160], g3[..., 192:224]], axis=-1).reshape(g.shape[0], MLA_H * 192)


def _rope_consts():
    half = MLA_ROPE // 2
    inv = ROPE_THETA ** (-jnp.arange(half, dtype=F32) / half)
    z = jnp.zeros((half,), F32)
    o = jnp.ones((half,), F32)
    row = lambda *p: jnp.concatenate(p).reshape(1, LANES)
    return row(inv, z, inv, z), row(o, z, o, z), row(-o, z, o, z)


BIG = ["pool_w_in", "pool_w_grp", "pool_w_out", "gdn_w_in", "gdn_w_out", "mla_w_in", "mla_w_uq", "mla_w_ukv", "mla_w_out"]
BIG_LAYOUT = {"pool_w_in": (1, 1024, (1024, 4096)), "pool_w_grp": (1, 128, (4, 512, 512)), "pool_w_out": (0, 512, (2048, 1024)),
              "gdn_w_in": (None, None, (4, 1024, 1540)), "gdn_w_out": (0, 512, (2048, 1024)),
              "mla_w_in": (None, None, (4, 1024, 848)), "mla_w_uq": (1, 768, (768, 3072)), "mla_w_ukv": (1, 1024, (512, 4096)),
              "mla_w_out": (0, 512, (2048, 1024))}
SMALL_SHARDED = ["pool_scale", "gdn_conv", "mla_q_norm_g", "mla_kv_norm_g"]
SMALL_AXIS = {"pool_scale": 1, "gdn_conv": 2, "mla_q_norm_g": 1, "mla_kv_norm_g": 1}
REPLICATED = ["norm_g", "gdn_a_log", "gdn_dt_bias", "gdn_norm_g", "final_g"]
PACK_C = 1024


def _pack(parts, dtype, row_mult):
    flat = jnp.concatenate([p.reshape(-1).astype(dtype) for p in parts])
    rows = -(-flat.shape[0] // PACK_C)
    rows = -(-rows // row_mult) * row_mult
    return jnp.pad(flat, (0, rows * PACK_C - flat.shape[0])).reshape(rows, PACK_C)


def _unpack(buf, shapes):
    lead = buf.shape[:-2]
    flat = buf.reshape(lead + (-1,))
    out, off = [], 0
    for s in shapes:
        n = int(np.prod(s))
        out.append(flat[..., off:off + n].reshape(lead + tuple(s)))
        off += n
    return out


def _unshard(g4, axis):
    a = jnp.moveaxis(g4, 0, axis)
    s = a.shape
    return a.reshape(s[:axis] + (s[axis] * s[axis + 1],) + s[axis + 2:])


def _to_shards(a, axis):
    s = a.shape
    return jnp.moveaxis(a.reshape(s[:axis] + (4, s[axis] // 4) + s[axis + 1:]), axis, 0)


def layer_weights(full, small, rep, layer):
    ng = rep["norm_g"][layer:layer + 1]
    side_by_side = lambda a4: jnp.moveaxis(a4, 0, 1).reshape(a4.shape[1], 4 * a4.shape[2])
    if layer in (0, 3):
        j = layer // 3
        return dict(ng=ng, w_in=full[("pool_w_in", j)], w_grp=full[("pool_w_grp", j)], scale=small["pool_scale"][j:j + 1],
                    w_out=full[("pool_w_out", j)])
    if layer == 1:
        return dict(ng=ng, w_in=_pad_cols(side_by_side(full[("gdn_w_in", 0)]), GDN_IN_PAD),
                    conv=jnp.pad(small["gdn_conv"][0], ((0, 4), (0, 0))), a_log=_pad_cols(rep["gdn_a_log"], LANES),
                    dt_bias=_pad_cols(rep["gdn_dt_bias"], LANES), norm_g=rep["gdn_norm_g"], w_out=full.get(("gdn_w_out", 0)))
    return dict(ng=ng, w_in=_mla_w_in_layout(side_by_side(full[("mla_w_in", 0)])), q_g=small["mla_q_norm_g"],
                kv_g=small["mla_kv_norm_g"], w_uq=_mla_w_uq_layout(full[("mla_w_uq", 0)]), w_ukv=full[("mla_w_ukv", 0)],
                w_out=full[("mla_w_out", 0)], rope=_rope_consts())


def big_grad_pieces(gl):
    g0, g1, g2, g3 = gl
    slots = lambda a: jnp.moveaxis(a.reshape(a.shape[0], 4, a.shape[1] // 4), 1, 0)
    out = {}
    for l, g in ((0, g0), (1, g3)):
        if g is not None:
            out.update({("pool_w_in", l): g["w_in"], ("pool_w_grp", l): g["w_grp"], ("pool_w_out", l): g["w_out"]})
    if g1 is not None:
        out.update({("gdn_w_in", 0): slots(g1["w_in"][:, :GDN_IN]), ("gdn_w_out", 0): g1["w_out"]})
    if g2 is not None:
        out.update({("mla_w_in", 0): slots(_mla_w_in_unlayout(g2["w_in"])), ("mla_w_uq", 0): _mla_w_uq_unlayout(g2["w_uq"]),
                    ("mla_w_ukv", 0): g2["w_ukv"], ("mla_w_out", 0): g2["w_out"]})
    return out


def small_grads(gl, dfinal):
    g0, g1, g2, g3 = gl
    return {"norm_g": jnp.concatenate([g0["ng"], g1["ng"], g2["ng"], g3["ng"]], axis=0),
            "pool_scale": jnp.concatenate([g0["scale"], g3["scale"]], axis=0), "gdn_conv": g1["conv"][None, :4],
            "gdn_a_log": g1["a_log"][:, :GDN_H], "gdn_dt_bias": g1["dt_bias"][:, :GDN_H], "gdn_norm_g": g1["norm_g"],
            "mla_q_norm_g": g2["q_g"], "mla_kv_norm_g": g2["kv_g"], "final_g": dfinal.reshape(D)}


NAMES = ["norm_g", "pool_w_in", "pool_w_grp", "pool_scale", "pool_w_out", "gdn_w_in", "gdn_conv", "gdn_a_log", "gdn_dt_bias",
         "gdn_norm_g", "gdn_w_out", "mla_w_in", "mla_q_norm_g", "mla_w_uq", "mla_kv_norm_g", "mla_w_ukv", "mla_w_out", "final_g"]


def kernel(x, positions, norm_g, pool_w_in, pool_w_grp, pool_scale, pool_w_out, gdn_w_in, gdn_conv, gdn_a_log, gdn_dt_bias, gdn_norm_g, gdn_w_out, mla_w_in, mla_q_norm_g, mla_w_uq, mla_kv_norm_g, mla_w_ukv, mla_w_out, final_g, loss_target, m_norm_g, m_pool_w_in, m_pool_w_grp, m_pool_scale, m_pool_w_out, m_gdn_w_in, m_gdn_conv, m_gdn_a_log, m_gdn_dt_bias, m_gdn_norm_g, m_gdn_w_out, m_mla_w_in, m_mla_q_norm_g, m_mla_w_uq, m_mla_kv_norm_g, m_mla_w_ukv, m_mla_w_out, m_final_g, v_norm_g, v_pool_w_in, v_pool_w_grp, v_pool_scale, v_pool_w_out, v_gdn_w_in, v_gdn_conv, v_gdn_a_log, v_gdn_dt_bias, v_gdn_norm_g, v_gdn_w_out, v_mla_w_in, v_mla_q_norm_g, v_mla_w_uq, v_mla_kv_norm_g, v_mla_w_ukv, v_mla_w_out, v_final_g):
    args = locals()
    w = {n: args[n] for n in NAMES}
    m = {n: args["m_" + n] for n in NAMES}
    v = {n: args["v_" + n] for n in NAMES}
    my_chip = (2 * lax.axis_index("x") + lax.axis_index("y")).astype(I32)
    S_ = x.shape[1]
    x0, pos, target = x[0], positions.reshape(S_, 1).astype(F32), loss_target[0]
    rep = {n: w[n] for n in REPLICATED}

    shard = {(n, l): w[n][l:l + 1].astype(BF16) for n in BIG for l in range(w[n].shape[0])}
    small_shapes = [w[n].shape for n in SMALL_SHARDED]
    shard[("small", 0)] = _pack([w[n] for n in SMALL_SHARDED], F32, 8)[None]
    layout = dict(BIG_LAYOUT, small=(None, None, (4,) + shard[("small", 0)].shape[1:]))

    def gather_start(group, after, tag):
        pieces = [gather_piece(i, 0, i, layout[n][0], layout[n][1]) for i, (n, l) in enumerate(group)]
        shapes = [_sds(layout[n][2], shard[(n, l)].dtype) for n, l in group]
        sems, ins, lands, token = exchange_start(pieces, [shard[k] for k in group], shapes, after, tag + "_start")
        return (pieces, sems, ins, lands), token

    def finish(handle, after, tag):
        return exchange_wait(*handle, after, tag + "_wait")

    def gathered(group, handle, after, tag):
        srcs, lands = finish(handle, after, tag)
        return {(n, l): place_own(a, s[0], layout[n][0], layout[n][1], my_chip) for (n, l), s, a in zip(group, srcs, lands)}

    group_a = [("small", 0), ("pool_w_in", 0)]
    group_a2 = [("pool_w_grp", 0), ("pool_w_out", 0)]
    group_b = [("gdn_w_in", 0)]
    group_c = [("gdn_w_out", 0), ("mla_w_in", 0), ("mla_w_uq", 0), ("mla_w_ukv", 0), ("mla_w_out", 0), ("pool_w_in", 1),
               ("pool_w_grp", 1), ("pool_w_out", 1)]
    full = {}
    h_a, t_a = gather_start(group_a, x0, "gather_a")
    h_a2, t_a2 = gather_start(group_a2, t_a, "gather_a2")
    h_b, t_b = gather_start(group_b, t_a2, "gather_b")
    h_c, t_c = gather_start(group_c, t_b, "gather_c")
    full.update(gathered(group_a, h_a, t_c, "gather_a"))
    small = {n: _unshard(a, SMALL_AXIS[n]) for n, a in zip(SMALL_SHARDED, _unpack(full[("small", 0)], small_shapes))}

    def late_l0(proj):
        full.update(gathered(group_a2, h_a2, proj, "gather_a2"))
        return dict(w_grp=full[("pool_w_grp", 0)], w_out=full[("pool_w_out", 0)])

    x1, s0 = pool_fwd(x0, dict(ng=rep["norm_g"][0:1], w_in=full[("pool_w_in", 0)], scale=small["pool_scale"][0:1]), "l0", late=late_l0)
    W0 = layer_weights(full, small, rep, 0)
    full.update(gathered(group_b, h_b, x1, "gather_b"))

    def late_l1(og):
        full.update(gathered(group_c, h_c, og, "gather_c"))
        return dict(w_out=full[("gdn_w_out", 0)])

    x2, s1 = gdn_fwd(x1, layer_weights(full, small, rep, 1), "l1", late=late_l1)
    W1, W2, W3 = (layer_weights(full, small, rep, i) for i in (1, 2, 3))
    x3, s2 = mla_fwd(x2, pos, W2, "l2")
    x4, s3 = pool_fwd(x3, W3, "l3")
    loss_part, dx4, dfinal = loss_head(x4, target, final_g.reshape(1, D), "loss_head")

    def scatter_start(pieces_of, after, tag):
        keys = list(pieces_of)
        pieces = [scatter_piece(i, i, BIG_LAYOUT[n][0], BIG_LAYOUT[n][1]) for i, (n, l) in enumerate(keys)]
        shapes = [_sds((4,) + tuple(w[n].shape[1:]), BF16) for n, l in keys]
        sems, ins, lands, token = exchange_start(pieces, [pieces_of[k] for k in keys], shapes, after, tag + "_start")
        return keys, (pieces, sems, ins, lands), token

    def scattered(keys, handle, after, tag):
        srcs, lands = finish(handle, after, tag)
        return {(n, l): place_own(a, own_window(g, BIG_LAYOUT[n][0], BIG_LAYOUT[n][1], my_chip), None, None, my_chip)
                for (n, l), g, a in zip(keys, srcs, lands)}

    dx3, g3 = pool_bwd(dx4, W3, s3, "l3")
    k3, h3, t3 = scatter_start(big_grad_pieces((None, None, None, g3)), dx3, "scatter_l3")
    dx2, g2 = mla_bwd(dx3, pos, W2, s2, "l2", after=t3)
    k2, h2, t2 = scatter_start(big_grad_pieces((None, None, g2, None)), dx2, "scatter_l2")
    dx1, g1 = gdn_bwd(dx2, W1, s1, "l1", after=t2)
    k1, h1, t1 = scatter_start(big_grad_pieces((None, g1, None, None)), dx1, "scatter_l1")
    def swap_start(part, tag):
        keys = list(part)
        ins = [part[k] for k in keys]
        pieces = [whole_piece(i) for i in range(len(keys))]
        sems, ins, lands, token = exchange_start(pieces, ins, [_sds(a.shape, a.dtype) for a in ins], ins[0], tag + "_start", sibling=True)
        swaps.append((keys, (pieces, sems, ins, lands), tag))
        return token

    last, swaps = [], []

    def emit_l0(grads):
        first = not last
        now = next(iter(grads.values()))
        early = [(k3, h3, "scatter_l3"), (k2, h2, "scatter_l2")] if first else [(k1, h1, "scatter_l1")]
        landed = {}
        for keys, handle, tag in early:
            landed.update(scattered(keys, handle, now, tag))
        swapping = swap_start(landed, "swap_a" if first else "swap_b")
        tag = "scatter_l0a" if first else "scatter_l0b"
        keys, handle, token = scatter_start({("pool_" + k, 0): a for k, a in grads.items()}, swapping, tag)
        last.append((keys, handle, tag))
        return token

    dx0, g0 = pool_bwd(dx1, W0, s0, "l0", after=t1, emit=emit_l0)
    landed = {}
    for keys, handle, tag in last:
        landed.update(scattered(keys, handle, dx0, tag))
    swap_start(landed, "swap_c")
    recv, sib = {}, {}
    for keys, handle, tag in swaps:
        mine_, theirs = exchange_wait(*handle, dx0, tag + "_wait", sibling=True)
        recv.update(zip(keys, mine_))
        sib.update(zip(keys, theirs))

    sg = small_grads((g0, g1, g2, g3), dfinal)
    small_names = SMALL_SHARDED + REPLICATED
    small_buf = _pack([sg[n] for n in small_names] + [loss_part], F32, 8)
    small_sum = sum_slots(None, exchange_all(small_buf, "gather_small"), jnp.full((1,), -1, I32), "sum_small")
    full_small = _unpack(small_sum, [sg[n].shape for n in small_names] + [(1, LANES)])
    loss = full_small[-1][0, 0]
    small_part = {}
    for n, a in zip(small_names, full_small[:-1]):
        if n in SMALL_AXIS:
            a = lax.dynamic_index_in_dim(_to_shards(a, SMALL_AXIS[n]), my_chip, axis=0, keepdims=False)
        small_part[n] = a

    outs = []
    for n in NAMES:
        shp = w[n].shape
        two = (int(np.prod(shp[:-1])), shp[-1]) if len(shp) > 1 else (1, shp[0])
        if n in BIG_LAYOUT:
            layers = shp[0]
            rows = lambda a: a.reshape(4, two[0] // layers, two[1])
            parts = [[rows(recv[(n, l)]) for l in range(layers)], [rows(sib[(n, l)]) for l in range(layers)]]
        else:
            parts = [[small_part[n].reshape((1,) + two)]]
        res = adamw(w[n].reshape(two), parts, m[n].reshape(two), v[n].reshape(two), "adamw_" + n)
        outs.append([r.reshape(shp) for r in res])
    return (loss, dx0[None], *[o[0] for o in outs], *[o[1] for o in outs], *[o[2] for o in outs], *[o[3] for o in outs])
```

```python
import functools
import math

import jax
import jax.numpy as jnp
import numpy as np
from jax import lax
from jax.experimental import pallas as pl
from jax.experimental.pallas import tpu as pltpu

F32 = jnp.float32
BF16 = jnp.bfloat16
I32 = jnp.int32

D = 1024
EPS = 1e-6
POOL_WIDTH = 2048
POOL_GROUP = 512
GDN_H, GDN_DK, GDN_DV, GDN_C = 8, 128, 256, 64
GDN_QK, GDN_V, GDN_CONV_CH, GDN_IN = 1024, 2048, 4096, 6160
GDN_IN_PAD = 6272
MLA_H, MLA_NOPE, MLA_ROPE, MLA_V = 16, 128, 64, 128
MLA_Q_LORA, MLA_KV_LORA, MLA_WIDTH, MLA_IN = 768, 512, 2048, 3392
MLA_IN_PAD = 4096
MLA_SCALE = (MLA_NOPE + MLA_ROPE) ** -0.5
ROPE_THETA = 10000.0
ADAM_LR, ADAM_B1, ADAM_B2, ADAM_EPS, ADAM_WD, ADAM_STEP = 0.001, 0.9, 0.999, 1e-08, 0.01, 10

VMEM_LIMIT_V7X = 56 * 1024 * 1024
LANES = 128
MESH = pl.DeviceIdType.MESH


def _pc(body, **kw):
    return pl.pallas_call(body, **kw)


def _cparams(sem):
    return pltpu.CompilerParams(dimension_semantics=sem, vmem_limit_bytes=VMEM_LIMIT_V7X)


def _tile(n, cap):
    t = (cap // LANES) * LANES
    while t >= LANES:
        if n % t == 0:
            return t
        t -= LANES
    return n


def _sds(shape, dt):
    return jax.ShapeDtypeStruct(shape, dt)


def mm(a, b, *, ta=False, tb=False, add=None, after=None, out_dtype=F32, name):
    if ta:
        K, M = a.shape
    else:
        M, K = a.shape
    if tb:
        N, K2 = b.shape
    else:
        K2, N = b.shape
    assert K == K2, (a.shape, b.shape, ta, tb)
    tm, tn, tk = _tile(M, 1024), _tile(N, 1024), _tile(K, 1024)
    nk = K // tk
    a_spec = pl.BlockSpec((tk, tm), lambda i, j, k: (k, i)) if ta else pl.BlockSpec((tm, tk), lambda i, j, k: (i, k))
    b_spec = pl.BlockSpec((tn, tk), lambda i, j, k: (j, k)) if tb else pl.BlockSpec((tk, tn), lambda i, j, k: (k, j))
    o_spec = pl.BlockSpec((tm, tn), lambda i, j, k: (i, j))
    dn = (((0 if ta else 1,), (1 if tb else 0,)), ((), ()))
    has_add = add is not None

    def body(*refs):
        a_ref, b_ref = refs[0], refs[1]
        part = lax.dot_general(a_ref[...].astype(BF16), b_ref[...].astype(BF16), dn, preferred_element_type=F32)
        if nk == 1:
            refs[-1][...] = (part + refs[2][...] if has_add else part).astype(out_dtype)
            return
        o_ref, acc = refs[-2], refs[-1]
        k = pl.program_id(2)

        @pl.when(k == 0)
        def _():
            acc[...] = part

        @pl.when(k > 0)
        def _():
            acc[...] += part

        @pl.when(k == nk - 1)
        def _():
            r = acc[...]
            if has_add:
                r = r + refs[2][...]
            o_ref[...] = r.astype(out_dtype)

    ins = [a, b] + ([add] if has_add else []) + ([after] if after is not None else [])
    specs = [a_spec, b_spec] + ([o_spec] if has_add else []) + ([pl.BlockSpec(memory_space=pl.ANY)] if after is not None else [])
    return _pc(body, grid=(M // tm, N // tn, nk), in_specs=specs, out_specs=o_spec, out_shape=_sds((M, N), out_dtype),
               scratch_shapes=[pltpu.VMEM((tm, tn), F32)] if nk > 1 else [], compiler_params=_cparams(("parallel", "parallel", "arbitrary")),
               name=name)(*ins)


def gmm(kind, a, b, *, G, name, out_dtype=F32):
    S_ = a.shape[0]
    Ka = a.shape[1] // G
    if kind == "tn":
        N = b.shape[1] // G
        tk = _tile(S_, 512)
        nk = S_ // tk

        def body(a_ref, b_ref, o_ref, acc):
            k = pl.program_id(1)

            @pl.when(k == 0)
            def _():
                acc[...] = jnp.zeros_like(acc)

            acc[...] += lax.dot_general(a_ref[...].astype(BF16), b_ref[...].astype(BF16), (((0,), (0,)), ((), ())),
                                        preferred_element_type=F32)

            @pl.when(k == nk - 1)
            def _():
                o_ref[...] = acc[...].astype(out_dtype)

        return _pc(body, grid=(G, nk),
                   in_specs=[pl.BlockSpec((tk, Ka), lambda g, k: (k, g)), pl.BlockSpec((tk, N), lambda g, k: (k, g))],
                   out_specs=pl.BlockSpec((None, Ka, N), lambda g, k: (g, 0, 0)), out_shape=_sds((G, Ka, N), out_dtype),
                   scratch_shapes=[pltpu.VMEM((Ka, N), F32)], compiler_params=_cparams(("parallel", "arbitrary")), name=name)(a, b)
    N = b.shape[2] if kind == "nn" else b.shape[1]
    tm = _tile(S_, 1024)
    dn = (((1,), (0 if kind == "nn" else 1,)), ((), ()))

    def body(a_ref, b_ref, o_ref):
        o_ref[...] = lax.dot_general(a_ref[...].astype(BF16), b_ref[...].astype(BF16), dn, preferred_element_type=F32).astype(out_dtype)

    bshape = (None,) + tuple(b.shape[1:])
    return _pc(body, grid=(G, S_ // tm),
               in_specs=[pl.BlockSpec((tm, Ka), lambda g, i: (i, g)), pl.BlockSpec(bshape, lambda g, i: (g, 0, 0))],
               out_specs=pl.BlockSpec((tm, N), lambda g, i: (i, g)), out_shape=_sds((S_, G * N), out_dtype),
               compiler_params=_cparams(("parallel", "parallel")), name=name)(a, b)


def _rw_spec(ts, w, c, s):
    return pl.BlockSpec((ts, w), lambda j, i: (i, c + j * s))


def _rw_pspec(p, w, c, s):
    return pl.BlockSpec((p.shape[0], w), lambda j, i: (0, c + j * s))


def rowwise(f, tiles, params, outs, *, ncol=1, ts, name):
    S_ = tiles[0][0].shape[0]
    nin = len(tiles) + len(params)

    def body(*refs):
        res = f(pl.program_id(0), *[r[...].astype(F32) for r in refs[:nin]])
        for r, o in zip(refs[nin:], res):
            r[...] = o.astype(r.dtype)

    return _pc(body, grid=(ncol, S_ // ts),
               in_specs=[_rw_spec(ts, w, c, s) for (_, w, c, s) in tiles] + [_rw_pspec(*p) for p in params],
               out_specs=[_rw_spec(ts, w, 0, s) for (w, s, _) in outs],
               out_shape=[_sds((S_, w * (ncol if s else 1)), dt) for (w, s, dt) in outs],
               compiler_params=_cparams(("parallel", "parallel")), name=name)(*[t[0] for t in tiles], *[p[0] for p in params])


def rowwise_bwd(f, tiles, params, cots, *, need, adds=None, place=None, narrow=(), ncol=1, ts, name):
    S_ = tiles[0][0].shape[0]
    adds = adds or {}
    place = place or {}
    nt, npar, nc = len(tiles), len(params), len(cots)
    add_keys = sorted(adds)
    need_idx = [k for k in range(nt) if need[k]]
    into_keys = [k for k in need_idx if k in place and not isinstance(place[k][0], int)]
    n_extra = len(add_keys) + len(into_keys)

    def body(*refs):
        j, i = pl.program_id(0), pl.program_id(1)
        vals = [r[...].astype(F32) for r in refs[:nt + npar]]
        cvals = tuple(r[...].astype(F32) for r in refs[nt + npar:nt + npar + nc])
        add_refs = refs[nt + npar + nc:nt + npar + nc + len(add_keys)]
        out_refs = refs[nt + npar + nc + n_extra:]
        _, vjp = jax.vjp(lambda *v: tuple(f(j, *v)), *vals)
        grads = vjp(cvals)
        for n, k in enumerate(need_idx):
            g = grads[k]
            if k in adds:
                g = g + add_refs[add_keys.index(k)][...]
            out_refs[n][...] = g.astype(out_refs[n].dtype)
        for n in range(npar):
            ref = out_refs[len(need_idx) + n]
            first = (i == 0) if params[n][3] else jnp.logical_and(i == 0, j == 0)

            @pl.when(first)
            def _():
                ref[...] = jnp.zeros_like(ref)

            ref[...] += grads[nt + n]

    in_specs = ([_rw_spec(ts, w, c, s) for (_, w, c, s) in tiles] + [_rw_pspec(*p) for p in params]
                + [_rw_spec(ts, w, c, s) for (_, w, c, s) in cots] + [_rw_spec(ts, *adds[k][1:]) for k in add_keys]
                + [pl.BlockSpec(memory_space=pl.ANY) for _ in into_keys])
    out_specs, out_shape, aliases = [], [], {}
    for n, k in enumerate(need_idx):
        w, s = tiles[k][1], tiles[k][3]
        if k in place:
            dst, c0 = place[k]
            total = dst if isinstance(dst, int) else dst.shape[1]
            out_specs.append(_rw_spec(ts, w, c0, s))
            out_shape.append(_sds((S_, total), (BF16 if k in narrow else F32) if isinstance(dst, int) else dst.dtype))
            if k in into_keys:
                aliases[nt + npar + nc + len(add_keys) + into_keys.index(k)] = n
        else:
            out_specs.append(_rw_spec(ts, w, 0, s))
            out_shape.append(_sds((S_, w * (ncol if s else 1)), BF16 if k in narrow else F32))
    out_specs += [_rw_pspec(p[0], p[1], p[2], p[3]) for p in params]
    out_shape += [_sds(p[0].shape, F32) for p in params]
    res = _pc(body, grid=(ncol, S_ // ts), in_specs=in_specs, out_specs=out_specs, out_shape=out_shape,
              input_output_aliases=aliases, compiler_params=_cparams(("arbitrary", "arbitrary")), name=name)(
        *[t[0] for t in tiles], *[p[0] for p in params], *[c[0] for c in cots], *[adds[k][0] for k in add_keys],
        *[place[k][0] for k in into_keys])
    return list(res[:len(need_idx)]), list(res[len(need_idx):])


def _rms(x, g):
    r = lax.rsqrt(jnp.mean(x * x, axis=-1, keepdims=True) + EPS)
    return x * r * g


def _silu(x):
    return x * jax.nn.sigmoid(x)


@jax.custom_vjp
def _softplus(x):
    return jnp.maximum(x, 0.0) + jnp.log1p(jnp.exp(-jnp.abs(x)))


_softplus.defvjp(lambda x: (_softplus(x), x), lambda x, d: (d * jax.nn.sigmoid(x),))


def f_rms(j, x, g):
    return (_rms(x, g),)


def f_pool_gate(j, pg, gate, scale):
    return (pg * scale * _silu(gate),)


def f_ogate(j, o, gate):
    return (o * _silu(gate),)


def f_gdn_out(j, o, gate, g):
    return (_rms(o, g) * _silu(gate),)


def f_gdn_gates(j, ba, alog, dtb):
    lane = lax.broadcasted_iota(I32, (1, LANES), 1)
    gs, bs = [], []
    for h in range(GDN_H):
        eb = (lane == h).astype(F32)
        ea = (lane == GDN_H + h).astype(F32)
        b = jnp.sum(ba * eb, -1, keepdims=True)
        a = jnp.sum(ba * ea, -1, keepdims=True)
        al = jnp.sum(alog * eb, -1, keepdims=True)
        dt = jnp.sum(dtb * eb, -1, keepdims=True)
        g = -jnp.exp(al) * _softplus(a + dt)
        gs.append(jnp.broadcast_to(g, ba.shape))
        bs.append(jnp.broadcast_to(jax.nn.sigmoid(b), ba.shape))
    return jnp.concatenate(gs, 1), jnp.concatenate(bs, 1)


def _shift_dn(x, k):
    rows = lax.broadcasted_iota(I32, x.shape, 0)
    return jnp.where(rows < k, 0.0, pltpu.roll(x, k, 0))


def _shift_up(x, k):
    n = x.shape[0]
    rows = lax.broadcasted_iota(I32, x.shape, 0)
    return jnp.where(rows >= n - k, 0.0, pltpu.roll(x, n - k, 0))


def _pool_window(j):
    g = lax.div(j, POOL_GROUP // LANES)
    return jnp.where(g == 0, 2.0, jnp.where(g == 1, 4.0, jnp.where(g == 2, 8.0, 16.0))), g


def _pick(g, a2, a4, a8, a16):
    return jnp.where(g == 0, a2, jnp.where(g == 1, a4, jnp.where(g == 2, a8, a16)))


def pool_time_fwd(proj, name):
    S_ = proj.shape[0]

    def body(u_ref, p_ref):
        u = u_ref[...].astype(F32)
        w, g = _pool_window(pl.program_id(0))
        s2 = u + _shift_dn(u, 1)
        s4 = s2 + _shift_dn(s2, 2)
        s8 = s4 + _shift_dn(s4, 4)
        s16 = s8 + _shift_dn(s8, 8)
        t1 = (lax.broadcasted_iota(I32, u.shape, 0) + 1).astype(F32)
        p_ref[...] = (_pick(g, s2, s4, s8, s16) / jnp.minimum(t1, w) - u).astype(p_ref.dtype)

    return _pc(body, grid=(POOL_WIDTH // LANES,), in_specs=[pl.BlockSpec((S_, LANES), lambda j: (0, j))],
               out_specs=pl.BlockSpec((S_, LANES), lambda j: (0, j)), out_shape=_sds((S_, POOL_WIDTH), BF16),
               compiler_params=_cparams(("parallel",)), name=name)(proj)


def pool_time_bwd(dp, into, name):
    S_ = dp.shape[0]

    def body(dp_ref, _, du_ref):
        d = dp_ref[...].astype(F32)
        w, g = _pool_window(pl.program_id(0))
        t1 = (lax.broadcasted_iota(I32, d.shape, 0) + 1).astype(F32)
        q = d / jnp.minimum(t1, w)
        r2 = q + _shift_up(q, 1)
        r4 = r2 + _shift_up(r2, 2)
        r8 = r4 + _shift_up(r4, 4)
        r16 = r8 + _shift_up(r8, 8)
        du_ref[...] = (_pick(g, r2, r4, r8, r16) - d).astype(du_ref.dtype)

    return _pc(body, grid=(POOL_WIDTH // LANES,),
               in_specs=[pl.BlockSpec((S_, LANES), lambda j: (0, j)), pl.BlockSpec(memory_space=pl.ANY)],
               out_specs=pl.BlockSpec((S_, LANES), lambda j: (0, j)), out_shape=_sds(into.shape, into.dtype),
               input_output_aliases={1: 0}, compiler_params=_cparams(("parallel",)), name=name)(dp, into)


def _conv_post(j, a):
    n = a * lax.rsqrt(jnp.sum(a * a, axis=-1, keepdims=True) + EPS)
    nq = GDN_QK // LANES
    return jnp.where(j < nq, n * (GDN_DK ** -0.5), jnp.where(j < 2 * nq, n, a))


def _conv_taps(u):
    return [_shift_dn(u, 3), _shift_dn(u, 2), _shift_dn(u, 1), u]


def _conv_pre(taps, w):
    return w[0:1] * taps[0] + w[1:2] * taps[1] + w[2:3] * taps[2] + w[3:4] * taps[3]


def gdn_conv_fwd(proj, conv_w, name):
    S_ = proj.shape[0]

    def body(u_ref, w_ref, o_ref):
        o_ref[...] = _conv_post(pl.program_id(0), _silu(_conv_pre(_conv_taps(u_ref[...]), w_ref[...])))

    return _pc(body, grid=(GDN_CONV_CH // LANES,),
               in_specs=[pl.BlockSpec((S_, LANES), lambda j: (0, j)), pl.BlockSpec((8, LANES), lambda j: (0, j))],
               out_specs=pl.BlockSpec((S_, LANES), lambda j: (0, j)), out_shape=_sds((S_, GDN_CONV_CH), F32),
               compiler_params=_cparams(("parallel",)), name=name)(proj, conv_w)


def gdn_conv_bwd(proj, conv_w, dq, dk, dv, into, name):
    S_ = proj.shape[0]
    nq = GDN_QK // LANES

    def body(u_ref, w_ref, dq_ref, dk_ref, dv_ref, _, du_ref, dw_ref):
        j = pl.program_id(0)
        u, w = u_ref[...], w_ref[...]
        taps = _conv_taps(u)
        c = _conv_pre(taps, w)
        sig = jax.nn.sigmoid(c)
        dout = jnp.where(j < nq, dq_ref[...], jnp.where(j < 2 * nq, dk_ref[...], dv_ref[...]))
        _, vjp = jax.vjp(lambda a: _conv_post(j, a), c * sig)
        dc = vjp(dout)[0] * (sig * (1.0 + c * (1.0 - sig)))
        du = w[3:4] * dc + w[2:3] * _shift_up(dc, 1) + w[1:2] * _shift_up(dc, 2) + w[0:1] * _shift_up(dc, 3)
        du_ref[...] = du.astype(du_ref.dtype)
        rows = lax.broadcasted_iota(I32, (8, LANES), 0)
        dw = jnp.zeros((8, LANES), F32)
        for k in range(4):
            dw = dw + jnp.where(rows == k, jnp.sum(dc * taps[k], axis=0, keepdims=True), 0.0)
        dw_ref[...] = dw

    blk = lambda f: pl.BlockSpec((S_, LANES), f)
    return _pc(body, grid=(GDN_CONV_CH // LANES,),
               in_specs=[blk(lambda j: (0, j)), pl.BlockSpec((8, LANES), lambda j: (0, j)),
                         blk(lambda j: (0, jnp.minimum(j, nq - 1))), blk(lambda j: (0, jnp.clip(j - nq, 0, nq - 1))),
                         blk(lambda j: (0, jnp.clip(j - 2 * nq, 0, 2 * nq - 1))), pl.BlockSpec(memory_space=pl.ANY)],
               out_specs=[blk(lambda j: (0, j)), pl.BlockSpec((8, LANES), lambda j: (0, j))],
               out_shape=[_sds(into.shape, into.dtype), _sds((8, GDN_CONV_CH), F32)], input_output_aliases={5: 0},
               compiler_params=_cparams(("parallel",)), name=name)(proj, conv_w, dq, dk, dv, into)


_NN, _NT, _TN = ((1,), (0,)), ((1,), (1,)), ((0,), (0,))


def _split(x, n):
    parts = []
    for _ in range(n):
        h = x.astype(BF16)
        parts.append(h)
        x = x - h.astype(F32)
    return parts


def _dot(a, b, dn, mode):
    d = lambda p, q: lax.dot_general(p, q, (dn, ((), ())), preferred_element_type=F32)
    if mode == "lo":
        return d(a.astype(BF16), b.astype(BF16))
    if mode == "x3":
        (ah, al), (bh, bl) = _split(a, 2), _split(b, 2)
        return d(ah, bh) + (d(ah, bl) + d(al, bh))
    b0, b1, b2 = _split(b, 3)
    ab = a.astype(BF16)
    return d(ab, b0) + (d(ab, b1) + d(ab, b2))


def _make_dots(mode):
    @jax.custom_vjp
    def nn(a, b):
        return _dot(a, b, _NN, mode)

    @jax.custom_vjp
    def nt(a, b):
        return _dot(a, b, _NT, mode)

    @jax.custom_vjp
    def tn(a, b):
        return _dot(a, b, _TN, mode)

    nn.defvjp(lambda a, b: (nn(a, b), (a, b)), lambda r, d: (nt(d, r[1]), tn(r[0], d)))
    nt.defvjp(lambda a, b: (nt(a, b), (a, b)), lambda r, d: (nn(d, r[1]), tn(d, r[0])))
    tn.defvjp(lambda a, b: (tn(a, b), (a, b)), lambda r, d: (nt(r[1], d), nn(r[0], d)))
    return nn, nt, tn


_nn_hi, _nt_hi, _tn_hi = _make_dots("x3")
_nn_lo, _nt_lo, _tn_lo = _make_dots("lo")


@jax.custom_vjp
def _nn_const(a, b):
    return _dot(a, b, _NN, "xl")


_nn_const.defvjp(lambda a, b: (_nn_const(a, b), a), lambda a, d: (jnp.zeros_like(a), _dot(a, d, _TN, "xl")))


def _each(f, *lists):
    return [f(*xs) for xs in zip(*lists)]


@jax.custom_vjp
def _unit_inverses(xs):
    C = xs[0].shape[0]
    eye = (lax.broadcasted_iota(I32, (C, C), 0) == lax.broadcasted_iota(I32, (C, C), 1)).astype(F32)
    ainv, p = [eye + a for a in xs], xs
    for _ in range(int(math.log2(C)) - 1):
        p = _each(lambda a: _dot(a, a, _NN, "x3"), p)
        ainv = _each(lambda a, b: a + _dot(a, b, _NN, "x3"), ainv, p)
    return ainv


def _unit_inverses_bwd(ainv, d):
    left = _each(lambda a, g: _dot(a, g, _TN, "x3"), ainv, d)
    return (_each(lambda t, a: _dot(t, a, _NT, "x3"), left, ainv),)


_unit_inverses.defvjp(lambda xs: (lambda a: (a, a))(_unit_inverses(xs)), _unit_inverses_bwd)


def _gdn_chunk(q, k, v, gb, bb, state):
    C = GDN_C
    e0 = (lax.broadcasted_iota(I32, (1, LANES), 1) == 0).astype(F32)
    ri = lax.broadcasted_iota(I32, (C, C), 0)
    ci = lax.broadcasted_iota(I32, (C, C), 1)
    causal, strict = ri >= ci, ri > ci
    tri, eye, ones = causal.astype(F32), (ri == ci).astype(F32), jnp.ones((C, C), F32)
    last = lax.broadcasted_iota(I32, (C, LANES), 0) == C - 1
    g1 = _each(lambda a: jnp.sum(a * e0, -1, keepdims=True), gb)
    b1 = _each(lambda a: jnp.sum(a * e0, -1, keepdims=True), bb)
    gc_c = _each(lambda g: _nn_const(tri, jnp.broadcast_to(g, (C, C))), g1)
    gc_d = _each(lambda g: _nn_const(tri, jnp.broadcast_to(g, (C, LANES))), g1)
    gr_c = _each(lambda g: _nn_const(ones, eye * g), gc_c)
    decay = _each(lambda a, r: jnp.where(causal, jnp.exp(jnp.where(causal, a - r, 0.0)), 0.0), gc_c, gr_c)
    kb = _each(lambda a, b: a * b, k, b1)
    vb = _each(lambda a, b: a * b, v, b1)
    x = _each(lambda a, b, d: -jnp.where(strict, _nt_lo(a, b) * d, 0.0), kb, k, decay)
    ainv = _unit_inverses(x)
    u = _each(_nn_hi, ainv, vb)
    w = _each(lambda a, b, g: _nn_hi(a, b * jnp.exp(g)), ainv, kb, gc_d)
    attn = _each(lambda a, b, d: jnp.where(causal, _nt_lo(a, b) * d, 0.0), q, k, decay)
    v_new = _each(lambda a, b, s: a - _nn_lo(b, s), u, w, state)
    o = _each(lambda a, g, s, t, vn: _nn_lo(a * jnp.exp(g), s) + _nn_lo(t, vn), q, gc_d, state, attn, v_new)
    gl = _each(lambda g: jnp.sum(jnp.where(last, g, 0.0), axis=0, keepdims=True), gc_d)
    new_state = _each(lambda s, g, a, gd, vn: s * jnp.exp(jnp.sum(g * e0, -1, keepdims=True)) + _tn_lo(a * jnp.exp(g - gd), vn),
                      state, gl, k, gc_d, v_new)
    return o, new_state


def _head_slices(ref, width):
    return [ref[:, h * width:(h + 1) * width] for h in range(GDN_H)]


def gdn_chunk_fwd(qkv, g_b, beta_b, name):
    S_ = qkv.shape[0]
    N = S_ // GDN_C

    def body(q_ref, k_ref, v_ref, g_ref, b_ref, o_ref, s_ref, state):
        @pl.when(pl.program_id(0) == 0)
        def _():
            state[...] = jnp.zeros_like(state)

        st = [state[h] for h in range(GDN_H)]
        s_ref[0] = state[...]
        o, st2 = _gdn_chunk(_head_slices(q_ref, GDN_DK), _head_slices(k_ref, GDN_DK), _head_slices(v_ref, GDN_DV),
                            _head_slices(g_ref, GDN_DK), _head_slices(b_ref, GDN_DK), st)
        for h in range(GDN_H):
            o_ref[:, h * GDN_DV:(h + 1) * GDN_DV] = o[h]
            state[h] = st2[h]

    return _pc(body, grid=(N,),
               in_specs=[pl.BlockSpec((GDN_C, GDN_QK), lambda n: (n, 0)), pl.BlockSpec((GDN_C, GDN_QK), lambda n: (n, 1)),
                         pl.BlockSpec((GDN_C, GDN_V), lambda n: (n, 1)), pl.BlockSpec((GDN_C, GDN_QK), lambda n: (n, 0)),
                         pl.BlockSpec((GDN_C, GDN_QK), lambda n: (n, 0))],
               out_specs=[pl.BlockSpec((GDN_C, GDN_V), lambda n: (n, 0)),
                          pl.BlockSpec((1, GDN_H, GDN_DK, GDN_DV), lambda n: (n, 0, 0, 0))],
               out_shape=[_sds((S_, GDN_V), F32), _sds((N, GDN_H, GDN_DK, GDN_DV), F32)],
               scratch_shapes=[pltpu.VMEM((GDN_H, GDN_DK, GDN_DV), F32)],
               compiler_params=_cparams(("arbitrary",)), name=name)(qkv, qkv, qkv, g_b, beta_b)


def gdn_chunk_bwd(qkv, g_b, beta_b, states, do, name):
    S_ = qkv.shape[0]
    N = S_ // GDN_C

    def body(q_ref, k_ref, v_ref, g_ref, b_ref, s_ref, do_ref, dq_ref, dk_ref, dv_ref, dg_ref, db_ref, dstate):
        @pl.when(pl.program_id(0) == 0)
        def _():
            dstate[...] = jnp.zeros_like(dstate)

        _, vjp = jax.vjp(_gdn_chunk, _head_slices(q_ref, GDN_DK), _head_slices(k_ref, GDN_DK), _head_slices(v_ref, GDN_DV),
                         _head_slices(g_ref, GDN_DK), _head_slices(b_ref, GDN_DK), [s_ref[0, h] for h in range(GDN_H)])
        dq, dk, dv, dg, db, ds = vjp((_head_slices(do_ref, GDN_DV), [dstate[h] for h in range(GDN_H)]))
        for h in range(GDN_H):
            kk, vv = slice(h * GDN_DK, (h + 1) * GDN_DK), slice(h * GDN_DV, (h + 1) * GDN_DV)
            dq_ref[:, kk] = dq[h]
            dk_ref[:, kk] = dk[h]
            dv_ref[:, vv] = dv[h]
            dg_ref[:, kk] = dg[h]
            db_ref[:, kk] = db[h]
            dstate[h] = ds[h]

    r = lambda n: N - 1 - n
    qk = lambda c: pl.BlockSpec((GDN_C, GDN_QK), lambda n: (r(n), c))
    vs = lambda c: pl.BlockSpec((GDN_C, GDN_V), lambda n: (r(n), c))
    return _pc(body, grid=(N,),
               in_specs=[qk(0), qk(1), vs(1), qk(0), qk(0),
                         pl.BlockSpec((1, GDN_H, GDN_DK, GDN_DV), lambda n: (r(n), 0, 0, 0)), vs(0)],
               out_specs=[qk(0), qk(0), vs(0), qk(0), qk(0)],
               out_shape=[_sds((S_, GDN_QK), F32), _sds((S_, GDN_QK), F32), _sds((S_, GDN_V), F32),
                          _sds((S_, GDN_QK), F32), _sds((S_, GDN_QK), F32)],
               scratch_shapes=[pltpu.VMEM((GDN_H, GDN_DK, GDN_DV), F32)],
               compiler_params=_cparams(("arbitrary",)), name=name)(qkv, qkv, qkv, g_b, beta_b, states, do)


def _rope_tables(pos_ref, inv_ref, cm_ref, sg_ref):
    ang = pos_ref[...] * inv_ref[...]
    return jnp.cos(ang) * cm_ref[...], jnp.sin(ang) * sg_ref[...]


def mla_prep_fwd(qpad, kv, proj, pos, rope_consts, name):
    S_ = qpad.shape[0]
    ts = 256
    W = 2 * LANES

    def body(q_ref, kv_ref, kr_ref, pos_ref, inv_ref, cm_ref, sg_ref, qh_ref, kh_ref, vh_ref):
        cs, sn = _rope_tables(pos_ref, inv_ref, cm_ref, sg_ref)
        rope = lambda r: r * cs + pltpu.roll(r, LANES // 2, 1) * sn
        krr = rope(kr_ref[...].astype(F32)).astype(BF16)
        for h in range(MLA_H):
            qh_ref[h, :, 0:LANES] = (q_ref[:, h * W:h * W + LANES].astype(F32) * MLA_SCALE).astype(BF16)
            qh_ref[h, :, LANES:W] = (rope(q_ref[:, h * W + LANES:(h + 1) * W].astype(F32)) * MLA_SCALE).astype(BF16)
            kh_ref[h, :, 0:LANES] = kv_ref[:, h * W:h * W + LANES].astype(BF16)
            kh_ref[h, :, LANES:W] = krr
            vh_ref[h] = kv_ref[:, h * W + LANES:(h + 1) * W].astype(BF16)

    one = pl.BlockSpec((1, LANES), lambda i: (0, 0))
    return _pc(body, grid=(S_ // ts,),
               in_specs=[pl.BlockSpec((ts, MLA_H * W), lambda i: (i, 0)), pl.BlockSpec((ts, MLA_H * W), lambda i: (i, 0)),
                         pl.BlockSpec((ts, LANES), lambda i: (i, 1536 // LANES)), pl.BlockSpec((ts, 1), lambda i: (i, 0)),
                         one, one, one],
               out_specs=[pl.BlockSpec((MLA_H, ts, W), lambda i: (0, i, 0)), pl.BlockSpec((MLA_H, ts, W), lambda i: (0, i, 0)),
                          pl.BlockSpec((MLA_H, ts, LANES), lambda i: (0, i, 0))],
               out_shape=[_sds((MLA_H, S_, W), BF16), _sds((MLA_H, S_, W), BF16), _sds((MLA_H, S_, LANES), BF16)],
               compiler_params=_cparams(("parallel",)), name=name)(qpad, kv, proj, pos, *rope_consts)


def mla_prep_bwd(dqh, dkh, dvh, pos, rope_consts, into, name):
    S_ = dqh.shape[1]
    ts = 256
    W = 2 * LANES

    def body(dq_ref, dk_ref, dv_ref, pos_ref, inv_ref, cm_ref, sg_ref, _, dqp_ref, dkv_ref, dkr_ref):
        cs, sn = _rope_tables(pos_ref, inv_ref, cm_ref, sg_ref)
        rope_t = lambda g: g * cs + pltpu.roll(g * sn, LANES // 2, 1)
        acc = jnp.zeros((ts, LANES), F32)
        for h in range(MLA_H):
            dqp_ref[:, h * W:h * W + LANES] = (dq_ref[h, :, 0:LANES].astype(F32) * MLA_SCALE).astype(BF16)
            dqp_ref[:, h * W + LANES:(h + 1) * W] = (rope_t(dq_ref[h, :, LANES:W].astype(F32)) * MLA_SCALE).astype(BF16)
            dkv_ref[:, h * W:h * W + LANES] = dk_ref[h, :, 0:LANES]
            dkv_ref[:, h * W + LANES:(h + 1) * W] = dv_ref[h]
            acc = acc + dk_ref[h, :, LANES:W].astype(F32)
        dkr_ref[...] = rope_t(acc).astype(dkr_ref.dtype)

    one = pl.BlockSpec((1, LANES), lambda i: (0, 0))
    return _pc(body, grid=(S_ // ts,),
               in_specs=[pl.BlockSpec((MLA_H, ts, W), lambda i: (0, i, 0)), pl.BlockSpec((MLA_H, ts, W), lambda i: (0, i, 0)),
                         pl.BlockSpec((MLA_H, ts, LANES), lambda i: (0, i, 0)), pl.BlockSpec((ts, 1), lambda i: (i, 0)),
                         one, one, one, pl.BlockSpec(memory_space=pl.ANY)],
               out_specs=[pl.BlockSpec((ts, MLA_H * W), lambda i: (i, 0)), pl.BlockSpec((ts, MLA_H * W), lambda i: (i, 0)),
                          pl.BlockSpec((ts, LANES), lambda i: (i, 1536 // LANES))],
               out_shape=[_sds((S_, MLA_H * W), BF16), _sds((S_, MLA_H * W), BF16), _sds(into.shape, into.dtype)],
               input_output_aliases={7: 2}, compiler_params=_cparams(("parallel",)), name=name)(dqh, dkh, dvh, pos, *rope_consts, into)


NEG = -1e30


FLASH_TILE = 1024
FLASH_SUB = 512


def _scores(q, k, diagonal):
    s = lax.dot_general(q, k, (_NT, ((), ())), preferred_element_type=F32)
    if not diagonal:
        return s
    return jnp.where(lax.broadcasted_iota(I32, s.shape, 1) <= lax.broadcasted_iota(I32, s.shape, 0), s, NEG)


def _sub_blocks(t, diagonal):
    sub = min(FLASH_SUB, t) if diagonal else t
    return [(c * sub if diagonal else 0, slice(c * sub, (c + 1) * sub)) for c in range(t // sub)]


FLASH_HEADS = 2


def flash_fwd(qh, kh, vh, name):
    H, S_, W = qh.shape
    t = _tile(S_, FLASH_TILE)
    n = S_ // t
    G = FLASH_HEADS
    heads = list(range(G))

    def body(q_ref, k_ref, v_ref, o_ref, lse_ref, m_s, l_s, acc):
        qi, kj = pl.program_id(1), pl.program_id(2)

        @pl.when(kj == 0)
        def _():
            m_s[...] = jnp.full_like(m_s, NEG)
            l_s[...] = jnp.zeros_like(l_s)
            acc[...] = jnp.zeros_like(acc)

        def step(diagonal):
            s = _each(lambda a: _scores(q_ref[a], k_ref[a], diagonal), heads)
            m_old = _each(lambda a: m_s[a], heads)
            m_new = _each(lambda mo, sa: jnp.maximum(mo, jnp.max(sa, axis=-1, keepdims=True)), m_old, s)
            alpha = _each(lambda mo, mn: jnp.exp(mo - mn), m_old, m_new)
            p = _each(lambda sa, mn: jnp.exp(sa - mn[:, :1]), s, m_new)
            pv = _each(lambda pa, a: lax.dot_general(pa.astype(BF16), v_ref[a], (_NN, ((), ())), preferred_element_type=F32), p, heads)
            for a in heads:
                l_s[a] = alpha[a] * l_s[a] + jnp.sum(p[a], axis=-1, keepdims=True)
                acc[a] = alpha[a] * acc[a] + pv[a]
                m_s[a] = m_new[a]

        pl.when(kj < qi)(lambda: step(False))
        pl.when(kj == qi)(lambda: step(True))

        @pl.when(kj == n - 1)
        def _():
            for a in heads:
                o_ref[:, a * LANES:(a + 1) * LANES] = acc[a] / l_s[a]
                lse_ref[a] = m_s[a] + jnp.log(l_s[a])

    return _pc(body, grid=(H // G, n, n),
               in_specs=[pl.BlockSpec((G, t, W), lambda h, i, j: (h, i, 0)),
                         pl.BlockSpec((G, t, W), lambda h, i, j: (h, jnp.minimum(i, j), 0)),
                         pl.BlockSpec((G, t, LANES), lambda h, i, j: (h, jnp.minimum(i, j), 0))],
               out_specs=[pl.BlockSpec((t, G * LANES), lambda h, i, j: (i, h)), pl.BlockSpec((G, t, LANES), lambda h, i, j: (h, i, 0))],
               out_shape=[_sds((S_, H * LANES), F32), _sds((H, S_, LANES), F32)],
               scratch_shapes=[pltpu.VMEM((G, t, LANES), F32)] * 3,
               compiler_params=_cparams(("parallel", "parallel", "arbitrary")), name=name)(qh, kh, vh)


def flash_bwd(qh, kh, vh, o, lse, do, name):
    H, S_, W = qh.shape
    t = _tile(S_, FLASH_TILE)
    n = S_ // t

    def body(q_ref, k_ref, v_ref, o_ref, lse_ref, do_ref, dq_ref, dk_ref, dv_ref, dq_acc, dk_acc, dv_acc):
        kj, qi = pl.program_id(1), pl.program_id(2)

        @pl.when(jnp.logical_and(kj == 0, qi == 0))
        def _():
            dq_acc[...] = jnp.zeros_like(dq_acc)

        @pl.when(qi == 0)
        def _():
            dk_acc[...] = jnp.zeros_like(dk_acc)
            dv_acc[...] = jnp.zeros_like(dv_acc)

        def step(diagonal):
            do_ = do_ref[...]
            dob = do_.astype(BF16)
            delta = jnp.sum(do_ * o_ref[...], axis=-1, keepdims=True)
            for r0, keys in _sub_blocks(t, diagonal):
                q, k, v = q_ref[r0:, :], k_ref[keys, :], v_ref[keys, :]
                p = jnp.exp(_scores(q, k, diagonal) - lse_ref[r0:, :1])
                dv_acc[keys, :] += lax.dot_general(p.astype(BF16), dob[r0:], (_TN, ((), ())), preferred_element_type=F32)
                dp = lax.dot_general(dob[r0:], v, (_NT, ((), ())), preferred_element_type=F32)
                ds = (p * (dp - delta[r0:])).astype(BF16)
                dk_acc[keys, :] += lax.dot_general(ds, q, (_TN, ((), ())), preferred_element_type=F32)
                rows = pl.ds(pl.multiple_of(qi * t, t) + r0, t - r0)
                dq_acc[rows, :] += lax.dot_general(ds, k, (_NN, ((), ())), preferred_element_type=F32)

        pl.when(qi > kj)(lambda: step(False))
        pl.when(qi == kj)(lambda: step(True))

        @pl.when(qi == n - 1)
        def _():
            dk_ref[...] = dk_acc[...].astype(BF16)
            dv_ref[...] = dv_acc[...].astype(BF16)

        @pl.when(jnp.logical_and(kj == n - 1, qi == n - 1))
        def _():
            dq_ref[...] = dq_acc[...].astype(BF16)

    qrow = lambda h, j, i: jnp.maximum(i, j)
    return _pc(body, grid=(H, n, n),
               in_specs=[pl.BlockSpec((None, t, W), lambda h, j, i: (h, qrow(h, j, i), 0)),
                         pl.BlockSpec((None, t, W), lambda h, j, i: (h, j, 0)),
                         pl.BlockSpec((None, t, LANES), lambda h, j, i: (h, j, 0)),
                         pl.BlockSpec((t, LANES), lambda h, j, i: (qrow(h, j, i), h)),
                         pl.BlockSpec((None, t, LANES), lambda h, j, i: (h, qrow(h, j, i), 0)),
                         pl.BlockSpec((t, LANES), lambda h, j, i: (qrow(h, j, i), h))],
               out_specs=[pl.BlockSpec((None, S_, W), lambda h, j, i: (h, 0, 0)),
                          pl.BlockSpec((None, t, W), lambda h, j, i: (h, j, 0)),
                          pl.BlockSpec((None, t, LANES), lambda h, j, i: (h, j, 0))],
               out_shape=[_sds((H, S_, W), BF16), _sds((H, S_, W), BF16), _sds((H, S_, LANES), BF16)],
               scratch_shapes=[pltpu.VMEM((S_, W), F32), pltpu.VMEM((t, W), F32), pltpu.VMEM((t, LANES), F32)],
               compiler_params=_cparams(("parallel", "arbitrary", "arbitrary")), name=name)(qh, kh, vh, o, lse, do)


def loss_head(x, target, g, name):
    S_ = x.shape[0]
    ts = 256

    def body(x_ref, t_ref, g_ref, l_ref, dx_ref, dg_ref):
        @pl.when(pl.program_id(0) == 0)
        def _():
            l_ref[...] = jnp.zeros_like(l_ref)
            dg_ref[...] = jnp.zeros_like(dg_ref)

        y, vjp = jax.vjp(_rms, x_ref[...], g_ref[...])
        err = y - t_ref[...]
        l_ref[...] += 0.5 * jnp.sum(jnp.sum(err * err, axis=-1, keepdims=True), axis=0, keepdims=True) / D
        dx, dg = vjp(err / D)
        dx_ref[...] = dx
        dg_ref[...] += dg

    row = pl.BlockSpec((ts, D), lambda i: (i, 0))
    return _pc(body, grid=(S_ // ts,), in_specs=[row, row, pl.BlockSpec((1, D), lambda i: (0, 0))],
               out_specs=[pl.BlockSpec((1, LANES), lambda i: (0, 0)), row, pl.BlockSpec((1, D), lambda i: (0, 0))],
               out_shape=[_sds((1, LANES), F32), _sds((S_, D), F32), _sds((1, D), F32)],
               compiler_params=_cparams(("arbitrary",)), name=name)(x, target, g)


def adamw(w, parts, m, v, name):
    R, C = w.shape
    rows = [p.shape[1] for p in parts[0]]
    tr = R
    for cand in (512, 256, 128, 64, 32, 16, 8):
        if all(r % cand == 0 for r in rows) and cand * C * 4 <= 1024 * 1024:
            tr = cand
            break
    c1 = 1.0 - ADAM_B1 ** ADAM_STEP
    c2 = 1.0 - ADAM_B2 ** ADAM_STEP
    starts = [sum(rows[:k]) // tr for k in range(len(rows))]
    flat = [p for part in parts for p in part]

    def body(*refs):
        w_ref, m_ref, v_ref = refs[0], refs[1 + len(flat)], refs[2 + len(flat)]
        g_ref, d_ref, nm_ref, nv_ref = refs[3 + len(flat):]
        i = pl.program_id(0)
        gg, at = None, 1
        for part in parts:
            val = None
            for k in range(len(part)):
                p_ref = refs[at]
                at += 1
                s = p_ref[0].astype(F32)
                for n in range(1, p_ref.shape[0]):
                    s = s + p_ref[n].astype(F32)
                val = s if val is None else jnp.where(i >= starts[k], s, val)
            gg = val if gg is None else gg + val
        m2 = ADAM_B1 * m_ref[...] + (1.0 - ADAM_B1) * gg
        v2 = ADAM_B2 * v_ref[...] + (1.0 - ADAM_B2) * (gg * gg)
        g_ref[...] = gg
        d_ref[...] = -ADAM_LR * ((m2 / c1) / (jnp.sqrt(v2 / c2) + ADAM_EPS) + ADAM_WD * w_ref[...])
        nm_ref[...] = m2
        nv_ref[...] = v2

    blk = pl.BlockSpec((tr, C), lambda i: (i, 0))
    piece = lambda p, k: pl.BlockSpec((p.shape[0], tr, C), lambda i: (0, jnp.clip(i - starts[k], 0, rows[k] // tr - 1), 0))
    pblk = [piece(p, k) for part in parts for k, p in enumerate(part)]
    return _pc(body, grid=(R // tr,), in_specs=[blk] + pblk + [blk, blk], out_specs=[blk] * 4, out_shape=[_sds((R, C), F32)] * 4,
               compiler_params=_cparams(("parallel",)), name=name)(w, *flat, m, v)


def sum_slots(own, recv, skip, name):
    n, R, C = recv.shape
    tr = _tile(R, 512) if R % LANES == 0 else R
    has_own = own is not None

    def body(*refs):
        skip_ref = refs[0]
        r_ref, o_ref = refs[-2], refs[-1]
        acc = refs[1][...] if has_own else jnp.zeros(o_ref.shape, F32)
        for s in range(n):
            acc = acc + jnp.where(skip_ref[0] == s, 0.0, r_ref[s].astype(F32))
        o_ref[...] = acc

    row = pl.BlockSpec((tr, C), lambda i, sk: (i, 0))
    gs = pltpu.PrefetchScalarGridSpec(
        num_scalar_prefetch=1, grid=(R // tr,),
        in_specs=([row] if has_own else []) + [pl.BlockSpec((n, tr, C), lambda i, sk: (0, i, 0))], out_specs=row)
    ins = ([own] if has_own else []) + [recv]
    return _pc(body, grid_spec=gs, out_shape=_sds((R, C), F32), compiler_params=_cparams(("parallel",)), name=name)(skip, *ins)


def _chip_peers():
    x, y, c = lax.axis_index("x"), lax.axis_index("y"), lax.axis_index("c")
    return (x, y, c), [(1 - x, y, c), (x, 1 - y, c), (1 - x, 1 - y, c)]


def _chip_index(p):
    return 2 * p[0] + p[1]


def _win(ref, axis, chip, size):
    if axis is None:
        return ref.at[chip]
    idx = [slice(None)] * len(ref.shape)
    idx[axis] = pl.ds(pl.multiple_of(chip * size, size), size)
    return ref.at[tuple(idx)]


def _remote(src, dst, send_sem, recv_sem, peer):
    return pltpu.make_async_remote_copy(src_ref=src, dst_ref=dst, send_sem=send_sem, recv_sem=recv_sem, device_id=peer,
                                        device_id_type=MESH)


HBM_SPEC = pl.BlockSpec(memory_space=pltpu.HBM)
SEM_SPEC = pl.BlockSpec(memory_space=pltpu.SEMAPHORE)
ANY_SPEC = pl.BlockSpec(memory_space=pl.ANY)
DATAFLOW = pltpu.SideEffectType.DATAFLOW_SIDE_EFFECTING


def gather_piece(i, l, o, axis, size):
    return (i, lambda r, chip: r.at[l], o, lambda r, chip: _win(r, axis, chip, size))


def scatter_piece(i, o, axis, size):
    return (i, lambda r, chip: _win(r, axis, chip, size), o, lambda r, chip: r.at[chip])


def whole_piece(i):
    return (i, lambda r, chip: r, i, lambda r, chip: r)


def _copies(pieces, in_refs, out_refs, send, recv, sibling):
    me, peers = _chip_peers()
    if sibling:
        peers = [(me[0], me[1], 1 - me[2])]
    mine = _chip_index(me)
    remote = []
    for n, (i, src, o, dst) in enumerate(pieces):
        d = dst(out_refs[o], mine)
        remote += [_remote(src(in_refs[i], _chip_index(p)), d, send.at[len(peers) * n + k], recv.at[len(peers) * n + k], p)
                   for k, p in enumerate(peers)]
    return remote


def own_window(a, axis, size, chip):
    if axis is None:
        return lax.dynamic_index_in_dim(a, chip, 0, keepdims=False)
    return lax.dynamic_slice_in_dim(a, chip * size, size, axis=axis)


def place_own(land, own, axis, size, chip):
    if axis is None:
        return lax.dynamic_update_slice_in_dim(land, own[None], chip, axis=0)
    return lax.dynamic_update_slice_in_dim(land, own, chip * size, axis=axis)


def exchange_start(pieces, ins, out_shapes, after, name, sibling=False):
    n_in, n_out, ncp = len(ins), len(out_shapes), len(pieces)

    def body(*refs):
        in_refs, land_refs = refs[:n_in], refs[n_in:n_in + n_out]
        send, recv = refs[n_in + n_out + 1], refs[n_in + n_out + 2]
        token = refs[-1]
        for cp in _copies(pieces, in_refs, land_refs, send, recv, sibling):
            cp.start()
        token[...] = jnp.zeros_like(token)

    hbm = lambda a: pltpu.with_memory_space_constraint(a, pltpu.HBM)
    lands = [hbm(lax.empty(s.shape, s.dtype)) for s in out_shapes]
    sem = pltpu.SemaphoreType.DMA(((1 if sibling else 3) * ncp,))
    thru = [pltpu.HBM(a.shape, a.dtype) for a in ins] + [pltpu.HBM(s.shape, s.dtype) for s in out_shapes]
    res = _pc(body, in_specs=[HBM_SPEC] * (n_in + n_out) + [ANY_SPEC],
              out_specs=[SEM_SPEC, SEM_SPEC] + [HBM_SPEC] * (n_in + n_out) + [pl.BlockSpec(memory_space=pltpu.VMEM)],
              out_shape=[sem, sem] + thru + [_sds((8, LANES), F32)],
              input_output_aliases={i: 2 + i for i in range(n_in + n_out)},
              compiler_params=pltpu.CompilerParams(has_side_effects=DATAFLOW), name=name)(*[hbm(a) for a in ins], *lands, after)
    return (res[0], res[1]), list(res[2:2 + n_in]), list(res[2 + n_in:2 + n_in + n_out]), res[-1]


def exchange_wait(pieces, sems, ins, lands, after, name, sibling=False):
    n_in, n_out = len(ins), len(lands)

    def body(*refs):
        in_refs, land_refs = refs[:n_in], refs[n_in:n_in + n_out]
        send, recv = refs[n_in + n_out], refs[n_in + n_out + 1]
        for cp in _copies(pieces, in_refs, land_refs, send, recv, sibling):
            cp.wait_send()
            cp.wait_recv()

    thru = [pltpu.HBM(a.shape, a.dtype) for a in ins] + [pltpu.HBM(a.shape, a.dtype) for a in lands]
    res = _pc(body, in_specs=[HBM_SPEC] * (n_in + n_out) + [SEM_SPEC, SEM_SPEC, ANY_SPEC], out_specs=[HBM_SPEC] * (n_in + n_out),
              out_shape=thru, input_output_aliases={i: i for i in range(n_in + n_out)},
              compiler_params=pltpu.CompilerParams(has_side_effects=DATAFLOW), name=name)(*ins, *lands, sems[0], sems[1], after)
    return list(res[:n_in]), list(res[n_in:])


def exchange_all(buf, name):
    def body(in_ref, out_ref, send, recv, local):
        x, y, c = lax.axis_index("x"), lax.axis_index("y"), lax.axis_index("c")
        mine = 4 * x + 2 * y + c
        loc = pltpu.make_async_copy(in_ref, out_ref.at[mine], local)
        loc.start()
        copies = [loc]
        for k in range(1, 8):
            peer = (x ^ (k >> 2), y ^ ((k >> 1) & 1), c ^ (k & 1))
            cp = pltpu.make_async_remote_copy(src_ref=in_ref, dst_ref=out_ref.at[mine], send_sem=send.at[k - 1],
                                              recv_sem=recv.at[k - 1], device_id=peer, device_id_type=MESH)
            cp.start()
            copies.append(cp)
        for cp in copies:
            cp.wait()

    anyspec = pl.BlockSpec(memory_space=pl.ANY)
    return _pc(body, in_specs=[anyspec], out_specs=anyspec, out_shape=_sds((8,) + buf.shape, buf.dtype),
               scratch_shapes=[pltpu.SemaphoreType.DMA((7,)), pltpu.SemaphoreType.DMA((7,)), pltpu.SemaphoreType.DMA],
               name=name)(buf)


def _norm_fwd(x, g, name):
    return rowwise(f_rms, [(x, D, 0, 0)], [(g, D, 0, 0)], [(D, 0, BF16)], ts=512, name=name)[0]


def _norm_bwd(x, g, dh, dres, name):
    (dx,), (dg,) = rowwise_bwd(f_rms, [(x, D, 0, 0)], [(g, D, 0, 0)], [(dh, D, 0, 0)], need=[True],
                               adds={0: (dres, D, 0, 0)}, ts=256, name=name)
    return dx, dg


def pool_fwd(x, W, tag, late=None):
    h = _norm_fwd(x, W["ng"], tag + "_norm")
    proj = mm(h, W["w_in"], out_dtype=BF16, name=tag + "_in")
    if late is not None:
        W = dict(W, **late(proj))
    p = pool_time_fwd(proj, tag + "_win")
    pg = gmm("nn", p, W["w_grp"], G=4, out_dtype=BF16, name=tag + "_grp")
    y = rowwise(f_pool_gate, [(pg, POOL_GROUP, 0, 1), (proj, POOL_GROUP, 4, 1)], [(W["scale"], POOL_GROUP, 0, 1)],
                [(POOL_GROUP, 1, BF16)], ncol=4, ts=512, name=tag + "_gate")[0]
    xn = mm(y, W["w_out"], add=x, name=tag + "_out")
    return xn, (x, h, proj, p, pg, y)


def pool_bwd(dxn, W, saved, tag, after=None, emit=None):
    x, h, proj, p, pg, y = saved
    emit = emit or (lambda grads: None)
    dy = mm(dxn, W["w_out"], tb=True, after=after, out_dtype=BF16, name=tag + "_dy")
    g = {}
    (dpg, dproj), (g["scale"],) = rowwise_bwd(
        f_pool_gate, [(pg, POOL_GROUP, 0, 1), (proj, POOL_GROUP, 4, 1)], [(W["scale"], POOL_GROUP, 0, 1)],
        [(dy, POOL_GROUP, 0, 1)], need=[True, True], place={1: (2 * POOL_WIDTH, 4)}, narrow=(0, 1), ncol=4, ts=512, name=tag + "_dgate")
    dp = gmm("nt", dpg, W["w_grp"], G=4, out_dtype=BF16, name=tag + "_dp")
    dproj = pool_time_bwd(dp, dproj, tag + "_dwin")
    g["w_in"] = mm(h, dproj, ta=True, out_dtype=BF16, name=tag + "_dw_in")
    t1 = emit({"w_in": g["w_in"]})
    g["w_out"] = mm(y, dxn, ta=True, after=t1, out_dtype=BF16, name=tag + "_dwout")
    g["w_grp"] = gmm("tn", p, dpg, G=4, out_dtype=BF16, name=tag + "_dwgrp")
    t2 = emit({"w_out": g["w_out"], "w_grp": g["w_grp"]})
    dh = mm(dproj, W["w_in"], tb=True, after=t2, name=tag + "_dh")
    dx, g["ng"] = _norm_bwd(x, W["ng"], dh, dxn, tag + "_dnorm")
    return dx, g


def gdn_fwd(x, W, tag, late=None):
    h = _norm_fwd(x, W["ng"], tag + "_norm")
    proj = mm(h, W["w_in"], name=tag + "_in")
    qkv = gdn_conv_fwd(proj, W["conv"], tag + "_conv")
    g_b, beta_b = rowwise(f_gdn_gates, [(proj, LANES, 6144 // LANES, 0)], [(W["a_log"], LANES, 0, 0), (W["dt_bias"], LANES, 0, 0)],
                          [(GDN_QK, 0, F32), (GDN_QK, 0, F32)], ts=512, name=tag + "_gates")
    o, states = gdn_chunk_fwd(qkv, g_b, beta_b, tag + "_chunk")
    og = rowwise(f_gdn_out, [(o, GDN_DV, 0, 1), (proj, GDN_DV, 4096 // GDN_DV, 1)], [(W["norm_g"], GDN_DV, 0, 0)],
                 [(GDN_DV, 1, BF16)], ncol=GDN_H, ts=512, name=tag + "_onorm")[0]
    if late is not None:
        W = dict(W, **late(og))
    xn = mm(og, W["w_out"], add=x, name=tag + "_out")
    return xn, (x, h, proj, qkv, g_b, beta_b, o, states, og)


def gdn_bwd(dxn, W, saved, tag, after=None):
    x, h, proj, qkv, g_b, beta_b, o, states, og = saved
    dog = mm(dxn, W["w_out"], tb=True, after=after, out_dtype=BF16, name=tag + "_dog")
    g = {"w_out": mm(og, dxn, ta=True, out_dtype=BF16, name=tag + "_dwout")}
    (do, dproj), (g["norm_g"],) = rowwise_bwd(
        f_gdn_out, [(o, GDN_DV, 0, 1), (proj, GDN_DV, 4096 // GDN_DV, 1)], [(W["norm_g"], GDN_DV, 0, 0)],
        [(dog, GDN_DV, 0, 1)], need=[True, True], place={1: (GDN_IN_PAD, 4096 // GDN_DV)}, narrow=(1,), ncol=GDN_H, ts=512, name=tag + "_donorm")
    dq, dk, dv, dg_b, dbeta_b = gdn_chunk_bwd(qkv, g_b, beta_b, states, do, tag + "_dchunk")
    (dproj,), (g["a_log"], g["dt_bias"]) = rowwise_bwd(
        f_gdn_gates, [(proj, LANES, 6144 // LANES, 0)], [(W["a_log"], LANES, 0, 0), (W["dt_bias"], LANES, 0, 0)],
        [(dg_b, GDN_QK, 0, 0), (dbeta_b, GDN_QK, 0, 0)], need=[True], place={0: (dproj, 6144 // LANES)}, ts=256, name=tag + "_dgates")
    dproj, g["conv"] = gdn_conv_bwd(proj, W["conv"], dq, dk, dv, dproj, tag + "_dconv")
    dh = mm(dproj, W["w_in"], tb=True, name=tag + "_dh")
    g["w_in"] = mm(h, dproj, ta=True, out_dtype=BF16, name=tag + "_dw_in")
    dx, g["ng"] = _norm_bwd(x, W["ng"], dh, dxn, tag + "_dnorm")
    return dx, g


def mla_fwd(x, pos, W, tag):
    h = _norm_fwd(x, W["ng"], tag + "_norm")
    proj = mm(h, W["w_in"], out_dtype=BF16, name=tag + "_in")
    hq = rowwise(f_rms, [(proj, MLA_Q_LORA, 0, 0)], [(W["q_g"], MLA_Q_LORA, 0, 0)], [(MLA_Q_LORA, 0, BF16)], ts=512, name=tag + "_qnorm")[0]
    hkv = rowwise(f_rms, [(proj, MLA_KV_LORA, 2, 0)], [(W["kv_g"], MLA_KV_LORA, 0, 0)], [(MLA_KV_LORA, 0, BF16)], ts=512, name=tag + "_kvnorm")[0]
    qpad = mm(hq, W["w_uq"], out_dtype=BF16, name=tag + "_uq")
    kv = mm(hkv, W["w_ukv"], out_dtype=BF16, name=tag + "_ukv")
    qh, kh, vh = mla_prep_fwd(qpad, kv, proj, pos, W["rope"], tag + "_prep")
    o, lse = flash_fwd(qh, kh, vh, tag + "_attn")
    og = rowwise(f_ogate, [(o, 512, 0, 1), (proj, 512, 4, 1)], [], [(512, 1, BF16)], ncol=4, ts=512, name=tag + "_ogate")[0]
    xn = mm(og, W["w_out"], add=x, name=tag + "_out")
    return xn, (x, h, proj, hq, hkv, qh, kh, vh, o, lse, og)


def mla_bwd(dxn, pos, W, saved, tag, after=None):
    x, h, proj, hq, hkv, qh, kh, vh, o, lse, og = saved
    dog = mm(dxn, W["w_out"], tb=True, after=after, out_dtype=BF16, name=tag + "_dog")
    g = {"w_out": mm(og, dxn, ta=True, out_dtype=BF16, name=tag + "_dwout")}
    dproj = jnp.zeros(proj.shape, BF16)
    (do, dproj), _ = rowwise_bwd(f_ogate, [(o, 512, 0, 1), (proj, 512, 4, 1)], [], [(dog, 512, 0, 1)], need=[True, True],
                                 place={1: (dproj, 4)}, ncol=4, ts=512, name=tag + "_dogate")
    dqh, dkh, dvh = flash_bwd(qh, kh, vh, o, lse, do, tag + "_dattn")
    dqpad, dkv, dproj = mla_prep_bwd(dqh, dkh, dvh, pos, W["rope"], dproj, tag + "_dprep")
    dhq = mm(dqpad, W["w_uq"], tb=True, name=tag + "_dhq")
    g["w_uq"] = mm(hq, dqpad, ta=True, out_dtype=BF16, name=tag + "_dwuq")
    dhkv = mm(dkv, W["w_ukv"], tb=True, name=tag + "_dhkv")
    g["w_ukv"] = mm(hkv, dkv, ta=True, out_dtype=BF16, name=tag + "_dwukv")
    (dproj,), (g["q_g"],) = rowwise_bwd(f_rms, [(proj, MLA_Q_LORA, 0, 0)], [(W["q_g"], MLA_Q_LORA, 0, 0)], [(dhq, MLA_Q_LORA, 0, 0)],
                                        need=[True], place={0: (dproj, 0)}, ts=256, name=tag + "_dqnorm")
    (dproj,), (g["kv_g"],) = rowwise_bwd(f_rms, [(proj, MLA_KV_LORA, 2, 0)], [(W["kv_g"], MLA_KV_LORA, 0, 0)], [(dhkv, MLA_KV_LORA, 0, 0)],
                                         need=[True], place={0: (dproj, 2)}, ts=256, name=tag + "_dkvnorm")
    dh = mm(dproj, W["w_in"], tb=True, name=tag + "_dh")
    g["w_in"] = mm(h, dproj, ta=True, out_dtype=BF16, name=tag + "_dw_in")
    dx, g["ng"] = _norm_bwd(x, W["ng"], dh, dxn, tag + "_dnorm")
    return dx, g


def _pad_cols(a, n):
    return jnp.pad(a, ((0, 0), (0, n - a.shape[1])))


def _mla_w_in_layout(w):
    z = lambda n: jnp.zeros((w.shape[0], n), w.dtype)
    kr = w[:, 1280:1344]
    return jnp.concatenate([w[:, :768], z(256), w[:, 768:1280], kr[:, :32], z(32), kr[:, 32:], z(32), z(384), w[:, 1344:]], axis=1)


def _mla_w_in_unlayout(g):
    return jnp.concatenate([g[:, :768], g[:, 1024:1536], g[:, 1536:1568], g[:, 1600:1632], g[:, 2048:]], axis=1)


def _mla_w_uq_layout(w):
    w3 = w.reshape(w.shape[0], MLA_H, MLA_NOPE + MLA_ROPE)
    z = jnp.zeros((w.shape[0], MLA_H, 32), w.dtype)
    return jnp.concatenate([w3[..., :128], w3[..., 128:160], z, w3[..., 160:192], z], axis=-1).reshape(w.shape[0], MLA_H * 256)


def _mla_w_uq_unlayout(g):
    g3 = g.reshape(g.shape[0], MLA_H, 256)
    return jnp.concatenate([g3[..., :128], g3[..., 128:160], g3[..., 192:224]], axis=-1).reshape(g.shape[0], MLA_H * 192)


def _rope_consts():
    half = MLA_ROPE // 2
    inv = ROPE_THETA ** (-jnp.arange(half, dtype=F32) / half)
    z = jnp.zeros((half,), F32)
    o = jnp.ones((half,), F32)
    row = lambda *p: jnp.concatenate(p).reshape(1, LANES)
    return row(inv, z, inv, z), row(o, z, o, z), row(-o, z, o, z)


BIG = ["pool_w_in", "pool_w_grp", "pool_w_out", "gdn_w_in", "gdn_w_out", "mla_w_in", "mla_w_uq", "mla_w_ukv", "mla_w_out"]
BIG_LAYOUT = {"pool_w_in": (1, 1024, (1024, 4096)), "pool_w_grp": (1, 128, (4, 512, 512)), "pool_w_out": (0, 512, (2048, 1024)),
              "gdn_w_in": (None, None, (4, 1024, 1540)), "gdn_w_out": (0, 512, (2048, 1024)),
              "mla_w_in": (None, None, (4, 1024, 848)), "mla_w_uq": (1, 768, (768, 3072)), "mla_w_ukv": (1, 1024, (512, 4096)),
              "mla_w_out": (0, 512, (2048, 1024))}
SMALL_SHARDED = ["pool_scale", "gdn_conv", "mla_q_norm_g", "mla_kv_norm_g"]
SMALL_AXIS = {"pool_scale": 1, "gdn_conv": 2, "mla_q_norm_g": 1, "mla_kv_norm_g": 1}
REPLICATED = ["norm_g", "gdn_a_log", "gdn_dt_bias", "gdn_norm_g", "final_g"]
PACK_C = 1024


def _pack(parts, dtype, row_mult):
    flat = jnp.concatenate([p.reshape(-1).astype(dtype) for p in parts])
    rows = -(-flat.shape[0] // PACK_C)
    rows = -(-rows // row_mult) * row_mult
    return jnp.pad(flat, (0, rows * PACK_C - flat.shape[0])).reshape(rows, PACK_C)


def _unpack(buf, shapes):
    lead = buf.shape[:-2]
    flat = buf.reshape(lead + (-1,))
    out, off = [], 0
    for s in shapes:
        n = int(np.prod(s))
        out.append(flat[..., off:off + n].reshape(lead + tuple(s)))
        off += n
    return out


def _unshard(g4, axis):
    a = jnp.moveaxis(g4, 0, axis)
    s = a.shape
    return a.reshape(s[:axis] + (s[axis] * s[axis + 1],) + s[axis + 2:])


def _to_shards(a, axis):
    s = a.shape
    return jnp.moveaxis(a.reshape(s[:axis] + (4, s[axis] // 4) + s[axis + 1:]), axis, 0)


def layer_weights(full, small, rep, layer):
    ng = rep["norm_g"][layer:layer + 1]
    side_by_side = lambda a4: jnp.moveaxis(a4, 0, 1).reshape(a4.shape[1], 4 * a4.shape[2])
    if layer in (0, 3):
        j = layer // 3
        return dict(ng=ng, w_in=full[("pool_w_in", j)], w_grp=full[("pool_w_grp", j)], scale=small["pool_scale"][j:j + 1],
                    w_out=full[("pool_w_out", j)])
    if layer == 1:
        return dict(ng=ng, w_in=_pad_cols(side_by_side(full[("gdn_w_in", 0)]), GDN_IN_PAD),
                    conv=jnp.pad(small["gdn_conv"][0], ((0, 4), (0, 0))), a_log=_pad_cols(rep["gdn_a_log"], LANES),
                    dt_bias=_pad_cols(rep["gdn_dt_bias"], LANES), norm_g=rep["gdn_norm_g"], w_out=full.get(("gdn_w_out", 0)))
    return dict(ng=ng, w_in=_mla_w_in_layout(side_by_side(full[("mla_w_in", 0)])), q_g=small["mla_q_norm_g"],
                kv_g=small["mla_kv_norm_g"], w_uq=_mla_w_uq_layout(full[("mla_w_uq", 0)]), w_ukv=full[("mla_w_ukv", 0)],
                w_out=full[("mla_w_out", 0)], rope=_rope_consts())


def big_grad_pieces(gl):
    g0, g1, g2, g3 = gl
    slots = lambda a: jnp.moveaxis(a.reshape(a.shape[0], 4, a.shape[1] // 4), 1, 0)
    out = {}
    for l, g in ((0, g0), (1, g3)):
        if g is not None:
            out.update({("pool_w_in", l): g["w_in"], ("pool_w_grp", l): g["w_grp"], ("pool_w_out", l): g["w_out"]})
    if g1 is not None:
        out.update({("gdn_w_in", 0): slots(g1["w_in"][:, :GDN_IN]), ("gdn_w_out", 0): g1["w_out"]})
    if g2 is not None:
        out.update({("mla_w_in", 0): slots(_mla_w_in_unlayout(g2["w_in"])), ("mla_w_uq", 0): _mla_w_uq_unlayout(g2["w_uq"]),
                    ("mla_w_ukv", 0): g2["w_ukv"], ("mla_w_out", 0): g2["w_out"]})
    return out


def small_grads(gl, dfinal):
    g0, g1, g2, g3 = gl
    return {"norm_g": jnp.concatenate([g0["ng"], g1["ng"], g2["ng"], g3["ng"]], axis=0),
            "pool_scale": jnp.concatenate([g0["scale"], g3["scale"]], axis=0), "gdn_conv": g1["conv"][None, :4],
            "gdn_a_log": g1["a_log"][:, :GDN_H], "gdn_dt_bias": g1["dt_bias"][:, :GDN_H], "gdn_norm_g": g1["norm_g"],
            "mla_q_norm_g": g2["q_g"], "mla_kv_norm_g": g2["kv_g"], "final_g": dfinal.reshape(D)}


NAMES = ["norm_g", "pool_w_in", "pool_w_grp", "pool_scale", "pool_w_out", "gdn_w_in", "gdn_conv", "gdn_a_log", "gdn_dt_bias",
         "gdn_norm_g", "gdn_w_out", "mla_w_in", "mla_q_norm_g", "mla_w_uq", "mla_kv_norm_g", "mla_w_ukv", "mla_w_out", "final_g"]


def kernel(x, positions, norm_g, pool_w_in, pool_w_grp, pool_scale, pool_w_out, gdn_w_in, gdn_conv, gdn_a_log, gdn_dt_bias, gdn_norm_g, gdn_w_out, mla_w_in, mla_q_norm_g, mla_w_uq, mla_kv_norm_g, mla_w_ukv, mla_w_out, final_g, loss_target, m_norm_g, m_pool_w_in, m_pool_w_grp, m_pool_scale, m_pool_w_out, m_gdn_w_in, m_gdn_conv, m_gdn_a_log, m_gdn_dt_bias, m_gdn_norm_g, m_gdn_w_out, m_mla_w_in, m_mla_q_norm_g, m_mla_w_uq, m_mla_kv_norm_g, m_mla_w_ukv, m_mla_w_out, m_final_g, v_norm_g, v_pool_w_in, v_pool_w_grp, v_pool_scale, v_pool_w_out, v_gdn_w_in, v_gdn_conv, v_gdn_a_log, v_gdn_dt_bias, v_gdn_norm_g, v_gdn_w_out, v_mla_w_in, v_mla_q_norm_g, v_mla_w_uq, v_mla_kv_norm_g, v_mla_w_ukv, v_mla_w_out, v_final_g):
    args = locals()
    w = {n: args[n] for n in NAMES}
    m = {n: args["m_" + n] for n in NAMES}
    v = {n: args["v_" + n] for n in NAMES}
    my_chip = (2 * lax.axis_index("x") + lax.axis_index("y")).astype(I32)
    S_ = x.shape[1]
    x0, pos, target = x[0], positions.reshape(S_, 1).astype(F32), loss_target[0]
    rep = {n: w[n] for n in REPLICATED}

    shard = {(n, l): w[n][l:l + 1].astype(BF16) for n in BIG for l in range(w[n].shape[0])}
    small_shapes = [w[n].shape for n in SMALL_SHARDED]
    shard[("small", 0)] = _pack([w[n] for n in SMALL_SHARDED], F32, 8)[None]
    layout = dict(BIG_LAYOUT, small=(None, None, (4,) + shard[("small", 0)].shape[1:]))

    def gather_start(group, after, tag):
        pieces = [gather_piece(i, 0, i, layout[n][0], layout[n][1]) for i, (n, l) in enumerate(group)]
        shapes = [_sds(layout[n][2], shard[(n, l)].dtype) for n, l in group]
        sems, ins, lands, token = exchange_start(pieces, [shard[k] for k in group], shapes, after, tag + "_start")
        return (pieces, sems, ins, lands), token

    def finish(handle, after, tag):
        return exchange_wait(*handle, after, tag + "_wait")

    def gathered(group, handle, after, tag):
        srcs, lands = finish(handle, after, tag)
        return {(n, l): place_own(a, s[0], layout[n][0], layout[n][1], my_chip) for (n, l), s, a in zip(group, srcs, lands)}

    group_a = [("small", 0), ("pool_w_in", 0)]
    group_a2 = [("pool_w_grp", 0), ("pool_w_out", 0)]
    group_b = [("gdn_w_in", 0)]
    group_c = [("gdn_w_out", 0), ("mla_w_in", 0), ("mla_w_uq", 0), ("mla_w_ukv", 0), ("mla_w_out", 0), ("pool_w_in", 1),
               ("pool_w_grp", 1), ("pool_w_out", 1)]
    full = {}
    h_a, t_a = gather_start(group_a, x0, "gather_a")
    h_a2, t_a2 = gather_start(group_a2, t_a, "gather_a2")
    h_b, t_b = gather_start(group_b, t_a2, "gather_b")
    h_c, t_c = gather_start(group_c, t_b, "gather_c")
    full.update(gathered(group_a, h_a, t_c, "gather_a"))
    small = {n: _unshard(a, SMALL_AXIS[n]) for n, a in zip(SMALL_SHARDED, _unpack(full[("small", 0)], small_shapes))}

    def late_l0(proj):
        full.update(gathered(group_a2, h_a2, proj, "gather_a2"))
        return dict(w_grp=full[("pool_w_grp", 0)], w_out=full[("pool_w_out", 0)])

    x1, s0 = pool_fwd(x0, dict(ng=rep["norm_g"][0:1], w_in=full[("pool_w_in", 0)], scale=small["pool_scale"][0:1]), "l0", late=late_l0)
    W0 = layer_weights(full, small, rep, 0)
    full.update(gathered(group_b, h_b, x1, "gather_b"))

    def late_l1(og):
        full.update(gathered(group_c, h_c, og, "gather_c"))
        return dict(w_out=full[("gdn_w_out", 0)])

    x2, s1 = gdn_fwd(x1, layer_weights(full, small, rep, 1), "l1", late=late_l1)
    W1, W2, W3 = (layer_weights(full, small, rep, i) for i in (1, 2, 3))
    x3, s2 = mla_fwd(x2, pos, W2, "l2")
    x4, s3 = pool_fwd(x3, W3, "l3")
    loss_part, dx4, dfinal = loss_head(x4, target, final_g.reshape(1, D), "loss_head")

    def scatter_start(pieces_of, after, tag):
        keys = list(pieces_of)
        pieces = [scatter_piece(i, i, BIG_LAYOUT[n][0], BIG_LAYOUT[n][1]) for i, (n, l) in enumerate(keys)]
        shapes = [_sds((4,) + tuple(w[n].shape[1:]), BF16) for n, l in keys]
        sems, ins, lands, token = exchange_start(pieces, [pieces_of[k] for k in keys], shapes, after, tag + "_start")
        return keys, (pieces, sems, ins, lands), token

    def scattered(keys, handle, after, tag):
        srcs, lands = finish(handle, after, tag)
        return {(n, l): place_own(a, own_window(g, BIG_LAYOUT[n][0], BIG_LAYOUT[n][1], my_chip), None, None, my_chip)
                for (n, l), g, a in zip(keys, srcs, lands)}

    dx3, g3 = pool_bwd(dx4, W3, s3, "l3")
    k3, h3, t3 = scatter_start(big_grad_pieces((None, None, None, g3)), dx3, "scatter_l3")
    dx2, g2 = mla_bwd(dx3, pos, W2, s2, "l2", after=t3)
    k2, h2, t2 = scatter_start(big_grad_pieces((None, None, g2, None)), dx2, "scatter_l2")
    dx1, g1 = gdn_bwd(dx2, W1, s1, "l1", after=t2)
    k1, h1, t1 = scatter_start(big_grad_pieces((None, g1, None, None)), dx1, "scatter_l1")
    def swap_start(part, tag):
        keys = list(part)
        ins = [part[k] for k in keys]
        pieces = [whole_piece(i) for i in range(len(keys))]
        sems, ins, lands, token = exchange_start(pieces, ins, [_sds(a.shape, a.dtype) for a in ins], ins[0], tag + "_start", sibling=True)
        swaps.append((keys, (pieces, sems, ins, lands), tag))
        return token

    last, swaps = [], []

    def emit_l0(grads):
        first = not last
        now = next(iter(grads.values()))
        early = [(k3, h3, "scatter_l3"), (k2, h2, "scatter_l2")] if first else [(k1, h1, "scatter_l1")]
        landed = {}
        for keys, handle, tag in early:
            landed.update(scattered(keys, handle, now, tag))
        swapping = swap_start(landed, "swap_a" if first else "swap_b")
        tag = "scatter_l0a" if first else "scatter_l0b"
        keys, handle, token = scatter_start({("pool_" + k, 0): a for k, a in grads.items()}, swapping, tag)
        last.append((keys, handle, tag))
        return token

    dx0, g0 = pool_bwd(dx1, W0, s0, "l0", after=t1, emit=emit_l0)
    landed = {}
    for keys, handle, tag in last:
        landed.update(scattered(keys, handle, dx0, tag))
    swap_start(landed, "swap_c")
    recv, sib = {}, {}
    for keys, handle, tag in swaps:
        mine_, theirs = exchange_wait(*handle, dx0, tag + "_wait", sibling=True)
        recv.update(zip(keys, mine_))
        sib.update(zip(keys, theirs))

    sg = small_grads((g0, g1, g2, g3), dfinal)
    small_names = SMALL_SHARDED + REPLICATED
    small_buf = _pack([sg[n] for n in small_names] + [loss_part], F32, 8)
    small_sum = sum_slots(None, exchange_all(small_buf, "gather_small"), jnp.full((1,), -1, I32), "sum_small")
    full_small = _unpack(small_sum, [sg[n].shape for n in small_names] + [(1, LANES)])
    loss = full_small[-1][0, 0]
    small_part = {}
    for n, a in zip(small_names, full_small[:-1]):
        if n in SMALL_AXIS:
            a = lax.dynamic_index_in_dim(_to_shards(a, SMALL_AXIS[n]), my_chip, axis=0, keepdims=False)
        small_part[n] = a

    outs = []
    for n in NAMES:
        shp = w[n].shape
        two = (int(np.prod(shp[:-1])), shp[-1]) if len(shp) > 1 else (1, shp[0])
        if n in BIG_LAYOUT:
            layers = shp[0]
            rows = lambda a: a.reshape(4, two[0] // layers, two[1])
            parts = [[rows(recv[(n, l)]) for l in range(layers)], [rows(sib[(n, l)]) for l in range(layers)]]
        else:
            parts = [[small_part[n].reshape((1,) + two)]]
        res = adamw(w[n].reshape(two), parts, m[n].reshape(two), v[n].reshape(two), "adamw_" + n)
        outs.append([r.reshape(shp) for r in res])
    return (loss, dx0[None], *[o[0] for o in outs], *[o[1] for o in outs], *[o[2] for o in outs], *[o[3] for o in outs])
```

```python
import math

import jax
import jax.numpy as jnp
import numpy as np
from jax import lax
from jax.experimental import pallas as pl
from jax.experimental.pallas import tpu as pltpu

F32 = jnp.float32
BF16 = jnp.bfloat16
I32 = jnp.int32

D = 1024
EPS = 1e-6
POOL_WIDTH = 2048
POOL_GROUP = 512
GDN_H, GDN_DK, GDN_DV, GDN_C = 8, 128, 256, 64
GDN_QK, GDN_V, GDN_CONV_CH, GDN_IN = 1024, 2048, 4096, 6160
GDN_IN_PAD = 6272
MLA_H, MLA_NOPE, MLA_ROPE, MLA_V = 16, 128, 64, 128
MLA_Q_LORA, MLA_KV_LORA, MLA_WIDTH, MLA_IN = 768, 512, 2048, 3392
MLA_IN_PAD = 4096
MLA_SCALE = (MLA_NOPE + MLA_ROPE) ** -0.5
ROPE_THETA = 10000.0
ADAM_LR, ADAM_B1, ADAM_B2, ADAM_EPS, ADAM_WD, ADAM_STEP = 0.001, 0.9, 0.999, 1e-08, 0.01, 10

VMEM_LIMIT_V7X = 56 * 1024 * 1024
LANES = 128
MESH = pl.DeviceIdType.MESH


def _pc(body, **kw):
    return pl.pallas_call(body, **kw)


def _cparams(sem):
    return pltpu.CompilerParams(dimension_semantics=sem, vmem_limit_bytes=VMEM_LIMIT_V7X)


def _tile(n, cap):
    t = (cap // LANES) * LANES
    while t >= LANES:
        if n % t == 0:
            return t
        t -= LANES
    return n


def _sds(shape, dt):
    return jax.ShapeDtypeStruct(shape, dt)


def mm(a, b, *, ta=False, tb=False, add=None, after=None, out_dtype=F32, name):
    if ta:
        K, M = a.shape
    else:
        M, K = a.shape
    if tb:
        N, K2 = b.shape
    else:
        K2, N = b.shape
    assert K == K2, (a.shape, b.shape, ta, tb)
    tm, tn, tk = _tile(M, 1024), _tile(N, 1024), _tile(K, 1024)
    nk = K // tk
    a_spec = pl.BlockSpec((tk, tm), lambda i, j, k: (k, i)) if ta else pl.BlockSpec((tm, tk), lambda i, j, k: (i, k))
    b_spec = pl.BlockSpec((tn, tk), lambda i, j, k: (j, k)) if tb else pl.BlockSpec((tk, tn), lambda i, j, k: (k, j))
    o_spec = pl.BlockSpec((tm, tn), lambda i, j, k: (i, j))
    dn = (((0 if ta else 1,), (1 if tb else 0,)), ((), ()))
    has_add = add is not None

    def body(*refs):
        a_ref, b_ref = refs[0], refs[1]
        part = lax.dot_general(a_ref[...].astype(BF16), b_ref[...].astype(BF16), dn, preferred_element_type=F32)
        if nk == 1:
            refs[-1][...] = (part + refs[2][...] if has_add else part).astype(out_dtype)
            return
        o_ref, acc = refs[-2], refs[-1]
        k = pl.program_id(2)

        @pl.when(k == 0)
        def _():
            acc[...] = part

        @pl.when(k > 0)
        def _():
            acc[...] += part

        @pl.when(k == nk - 1)
        def _():
            r = acc[...]
            if has_add:
                r = r + refs[2][...]
            o_ref[...] = r.astype(out_dtype)

    ins = [a, b] + ([add] if has_add else []) + ([after] if after is not None else [])
    specs = [a_spec, b_spec] + ([o_spec] if has_add else []) + ([pl.BlockSpec(memory_space=pl.ANY)] if after is not None else [])
    return _pc(body, grid=(M // tm, N // tn, nk), in_specs=specs, out_specs=o_spec, out_shape=_sds((M, N), out_dtype),
               scratch_shapes=[pltpu.VMEM((tm, tn), F32)] if nk > 1 else [], compiler_params=_cparams(("parallel", "parallel", "arbitrary")),
               name=name)(*ins)


def gmm(kind, a, b, *, G, name, out_dtype=F32):
    S_ = a.shape[0]
    Ka = a.shape[1] // G
    if kind == "tn":
        N = b.shape[1] // G
        tk = _tile(S_, 512)
        nk = S_ // tk

        def body(a_ref, b_ref, o_ref, acc):
            k = pl.program_id(1)

            @pl.when(k == 0)
            def _():
                acc[...] = jnp.zeros_like(acc)

            acc[...] += lax.dot_general(a_ref[...].astype(BF16), b_ref[...].astype(BF16), (((0,), (0,)), ((), ())),
                                        preferred_element_type=F32)

            @pl.when(k == nk - 1)
            def _():
                o_ref[...] = acc[...].astype(out_dtype)

        return _pc(body, grid=(G, nk),
                   in_specs=[pl.BlockSpec((tk, Ka), lambda g, k: (k, g)), pl.BlockSpec((tk, N), lambda g, k: (k, g))],
                   out_specs=pl.BlockSpec((None, Ka, N), lambda g, k: (g, 0, 0)), out_shape=_sds((G, Ka, N), out_dtype),
                   scratch_shapes=[pltpu.VMEM((Ka, N), F32)], compiler_params=_cparams(("parallel", "arbitrary")), name=name)(a, b)
    N = b.shape[2] if kind == "nn" else b.shape[1]
    tm = _tile(S_, 1024)
    dn = (((1,), (0 if kind == "nn" else 1,)), ((), ()))

    def body(a_ref, b_ref, o_ref):
        o_ref[...] = lax.dot_general(a_ref[...].astype(BF16), b_ref[...].astype(BF16), dn, preferred_element_type=F32).astype(out_dtype)

    bshape = (None,) + tuple(b.shape[1:])
    return _pc(body, grid=(G, S_ // tm),
               in_specs=[pl.BlockSpec((tm, Ka), lambda g, i: (i, g)), pl.BlockSpec(bshape, lambda g, i: (g, 0, 0))],
               out_specs=pl.BlockSpec((tm, N), lambda g, i: (i, g)), out_shape=_sds((S_, G * N), out_dtype),
               compiler_params=_cparams(("parallel", "parallel")), name=name)(a, b)


def _rw_spec(ts, w, c, s):
    return pl.BlockSpec((ts, w), lambda j, i: (i, c + j * s))


def _rw_pspec(p, w, c, s):
    return pl.BlockSpec((p.shape[0], w), lambda j, i: (0, c + j * s))


def rowwise(f, tiles, params, outs, *, ncol=1, ts, name):
    S_ = tiles[0][0].shape[0]
    nin = len(tiles) + len(params)

    def body(*refs):
        res = f(pl.program_id(0), *[r[...].astype(F32) for r in refs[:nin]])
        for r, o in zip(refs[nin:], res):
            r[...] = o.astype(r.dtype)

    return _pc(body, grid=(ncol, S_ // ts),
               in_specs=[_rw_spec(ts, w, c, s) for (_, w, c, s) in tiles] + [_rw_pspec(*p) for p in params],
               out_specs=[_rw_spec(ts, w, 0, s) for (w, s, _) in outs],
               out_shape=[_sds((S_, w * (ncol if s else 1)), dt) for (w, s, dt) in outs],
               compiler_params=_cparams(("parallel", "parallel")), name=name)(*[t[0] for t in tiles], *[p[0] for p in params])


def rowwise_bwd(f, tiles, params, cots, *, need, adds=None, place=None, narrow=(), ncol=1, ts, name):
    S_ = tiles[0][0].shape[0]
    adds = adds or {}
    place = place or {}
    nt, npar, nc = len(tiles), len(params), len(cots)
    add_keys = sorted(adds)
    need_idx = [k for k in range(nt) if need[k]]
    into_keys = [k for k in need_idx if k in place and not isinstance(place[k][0], int)]
    n_extra = len(add_keys) + len(into_keys)

    def body(*refs):
        j, i = pl.program_id(0), pl.program_id(1)
        vals = [r[...].astype(F32) for r in refs[:nt + npar]]
        cvals = tuple(r[...].astype(F32) for r in refs[nt + npar:nt + npar + nc])
        add_refs = refs[nt + npar + nc:nt + npar + nc + len(add_keys)]
        out_refs = refs[nt + npar + nc + n_extra:]
        _, vjp = jax.vjp(lambda *v: tuple(f(j, *v)), *vals)
        grads = vjp(cvals)
        for n, k in enumerate(need_idx):
            g = grads[k]
            if k in adds:
                g = g + add_refs[add_keys.index(k)][...]
            out_refs[n][...] = g.astype(out_refs[n].dtype)
        for n in range(npar):
            ref = out_refs[len(need_idx) + n]
            first = (i == 0) if params[n][3] else jnp.logical_and(i == 0, j == 0)

            @pl.when(first)
            def _():
                ref[...] = jnp.zeros_like(ref)

            ref[...] += grads[nt + n]

    in_specs = ([_rw_spec(ts, w, c, s) for (_, w, c, s) in tiles] + [_rw_pspec(*p) for p in params]
                + [_rw_spec(ts, w, c, s) for (_, w, c, s) in cots] + [_rw_spec(ts, *adds[k][1:]) for k in add_keys]
                + [pl.BlockSpec(memory_space=pl.ANY) for _ in into_keys])
    out_specs, out_shape, aliases = [], [], {}
    for n, k in enumerate(need_idx):
        w, s = tiles[k][1], tiles[k][3]
        if k in place:
            dst, c0 = place[k]
            total = dst if isinstance(dst, int) else dst.shape[1]
            out_specs.append(_rw_spec(ts, w, c0, s))
            out_shape.append(_sds((S_, total), (BF16 if k in narrow else F32) if isinstance(dst, int) else dst.dtype))
            if k in into_keys:
                aliases[nt + npar + nc + len(add_keys) + into_keys.index(k)] = n
        else:
            out_specs.append(_rw_spec(ts, w, 0, s))
            out_shape.append(_sds((S_, w * (ncol if s else 1)), BF16 if k in narrow else F32))
    out_specs += [_rw_pspec(p[0], p[1], p[2], p[3]) for p in params]
    out_shape += [_sds(p[0].shape, F32) for p in params]
    res = _pc(body, grid=(ncol, S_ // ts), in_specs=in_specs, out_specs=out_specs, out_shape=out_shape,
              input_output_aliases=aliases, compiler_params=_cparams(("arbitrary", "arbitrary")), name=name)(
        *[t[0] for t in tiles], *[p[0] for p in params], *[c[0] for c in cots], *[adds[k][0] for k in add_keys],
        *[place[k][0] for k in into_keys])
    return list(res[:len(need_idx)]), list(res[len(need_idx):])


def _rms(x, g):
    r = lax.rsqrt(jnp.mean(x * x, axis=-1, keepdims=True) + EPS)
    return x * r * g


def _silu(x):
    return x * jax.nn.sigmoid(x)


@jax.custom_vjp
def _softplus(x):
    return jnp.maximum(x, 0.0) + jnp.log1p(jnp.exp(-jnp.abs(x)))


_softplus.defvjp(lambda x: (_softplus(x), x), lambda x, d: (d * jax.nn.sigmoid(x),))


def f_rms(j, x, g):
    return (_rms(x, g),)


def f_pool_gate(j, pg, gate, scale):
    return (pg * scale * _silu(gate),)


def f_ogate(j, o, gate):
    return (o * _silu(gate),)


def f_gdn_out(j, o, gate, g):
    return (_rms(o, g) * _silu(gate),)


def f_gdn_gates(j, ba, alog, dtb):
    lane = lax.broadcasted_iota(I32, (1, LANES), 1)
    gs, bs = [], []
    for h in range(GDN_H):
        eb = (lane == h).astype(F32)
        ea = (lane == GDN_H + h).astype(F32)
        b = jnp.sum(ba * eb, -1, keepdims=True)
        a = jnp.sum(ba * ea, -1, keepdims=True)
        al = jnp.sum(alog * eb, -1, keepdims=True)
        dt = jnp.sum(dtb * eb, -1, keepdims=True)
        g = -jnp.exp(al) * _softplus(a + dt)
        gs.append(jnp.broadcast_to(g, ba.shape))
        bs.append(jnp.broadcast_to(jax.nn.sigmoid(b), ba.shape))
    return jnp.concatenate(gs, 1), jnp.concatenate(bs, 1)


def _shift_dn(x, k):
    rows = lax.broadcasted_iota(I32, x.shape, 0)
    return jnp.where(rows < k, 0.0, pltpu.roll(x, k, 0))


def _shift_up(x, k):
    n = x.shape[0]
    rows = lax.broadcasted_iota(I32, x.shape, 0)
    return jnp.where(rows >= n - k, 0.0, pltpu.roll(x, n - k, 0))


def _pool_window(j):
    g = lax.div(j, POOL_GROUP // LANES)
    return jnp.where(g == 0, 2.0, jnp.where(g == 1, 4.0, jnp.where(g == 2, 8.0, 16.0))), g


def _pick(g, a2, a4, a8, a16):
    return jnp.where(g == 0, a2, jnp.where(g == 1, a4, jnp.where(g == 2, a8, a16)))


def pool_time_fwd(proj, name):
    S_ = proj.shape[0]

    def body(u_ref, p_ref):
        u = u_ref[...].astype(F32)
        w, g = _pool_window(pl.program_id(0))
        s2 = u + _shift_dn(u, 1)
        s4 = s2 + _shift_dn(s2, 2)
        s8 = s4 + _shift_dn(s4, 4)
        s16 = s8 + _shift_dn(s8, 8)
        t1 = (lax.broadcasted_iota(I32, u.shape, 0) + 1).astype(F32)
        p_ref[...] = (_pick(g, s2, s4, s8, s16) / jnp.minimum(t1, w) - u).astype(p_ref.dtype)

    return _pc(body, grid=(POOL_WIDTH // LANES,), in_specs=[pl.BlockSpec((S_, LANES), lambda j: (0, j))],
               out_specs=pl.BlockSpec((S_, LANES), lambda j: (0, j)), out_shape=_sds((S_, POOL_WIDTH), BF16),
               compiler_params=_cparams(("parallel",)), name=name)(proj)


def pool_time_bwd(dp, into, name):
    S_ = dp.shape[0]

    def body(dp_ref, _, du_ref):
        d = dp_ref[...].astype(F32)
        w, g = _pool_window(pl.program_id(0))
        t1 = (lax.broadcasted_iota(I32, d.shape, 0) + 1).astype(F32)
        q = d / jnp.minimum(t1, w)
        r2 = q + _shift_up(q, 1)
        r4 = r2 + _shift_up(r2, 2)
        r8 = r4 + _shift_up(r4, 4)
        r16 = r8 + _shift_up(r8, 8)
        du_ref[...] = (_pick(g, r2, r4, r8, r16) - d).astype(du_ref.dtype)

    return _pc(body, grid=(POOL_WIDTH // LANES,),
               in_specs=[pl.BlockSpec((S_, LANES), lambda j: (0, j)), pl.BlockSpec(memory_space=pl.ANY)],
               out_specs=pl.BlockSpec((S_, LANES), lambda j: (0, j)), out_shape=_sds(into.shape, into.dtype),
               input_output_aliases={1: 0}, compiler_params=_cparams(("parallel",)), name=name)(dp, into)


def _conv_head_norm(j, a):
    n = a * lax.rsqrt(jnp.sum(a * a, axis=-1, keepdims=True) + EPS)
    return n * jnp.where(j < GDN_QK // LANES, GDN_DK ** -0.5, 1.0)


def _conv_taps(u):
    return [_shift_dn(u, 3), _shift_dn(u, 2), _shift_dn(u, 1), u]


def _conv_pre(taps, w):
    return w[0:1] * taps[0] + w[1:2] * taps[1] + w[2:3] * taps[2] + w[3:4] * taps[3]


def gdn_conv_fwd(proj, conv_w, name):
    S_ = proj.shape[0]

    def body(u_ref, w_ref, o_ref):
        j = pl.program_id(0)
        a = _silu(_conv_pre(_conv_taps(u_ref[...]), w_ref[...]))

        @pl.when(j < 2 * GDN_QK // LANES)
        def _():
            o_ref[...] = _conv_head_norm(j, a)

        @pl.when(j >= 2 * GDN_QK // LANES)
        def _():
            o_ref[...] = a

    return _pc(body, grid=(GDN_CONV_CH // LANES,),
               in_specs=[pl.BlockSpec((S_, LANES), lambda j: (0, j)), pl.BlockSpec((8, LANES), lambda j: (0, j))],
               out_specs=pl.BlockSpec((S_, LANES), lambda j: (0, j)), out_shape=_sds((S_, GDN_CONV_CH), F32),
               compiler_params=_cparams(("parallel",)), name=name)(proj, conv_w)


def gdn_conv_bwd(proj, conv_w, dq, dk, dv, into, name):
    S_ = proj.shape[0]
    nq = GDN_QK // LANES

    def body(u_ref, w_ref, dq_ref, dk_ref, dv_ref, _, du_ref, dw_ref):
        j = pl.program_id(0)
        u, w = u_ref[...], w_ref[...]
        taps = _conv_taps(u)
        c = _conv_pre(taps, w)
        sig = jax.nn.sigmoid(c)

        def finish(da):
            dc = da * (sig * (1.0 + c * (1.0 - sig)))
            du = w[3:4] * dc + w[2:3] * _shift_up(dc, 1) + w[1:2] * _shift_up(dc, 2) + w[0:1] * _shift_up(dc, 3)
            du_ref[...] = du.astype(du_ref.dtype)
            rows = lax.broadcasted_iota(I32, (8, LANES), 0)
            dw = jnp.zeros((8, LANES), F32)
            for k in range(4):
                dw = dw + jnp.where(rows == k, jnp.sum(dc * taps[k], axis=0, keepdims=True), 0.0)
            dw_ref[...] = dw

        @pl.when(j < 2 * nq)
        def _():
            _, vjp = jax.vjp(lambda a: _conv_head_norm(j, a), c * sig)
            finish(vjp(jnp.where(j < nq, dq_ref[...], dk_ref[...]))[0])

        @pl.when(j >= 2 * nq)
        def _():
            finish(dv_ref[...])

    blk = lambda f: pl.BlockSpec((S_, LANES), f)
    return _pc(body, grid=(GDN_CONV_CH // LANES,),
               in_specs=[blk(lambda j: (0, j)), pl.BlockSpec((8, LANES), lambda j: (0, j)),
                         blk(lambda j: (0, jnp.minimum(j, nq - 1))), blk(lambda j: (0, jnp.clip(j - nq, 0, nq - 1))),
                         blk(lambda j: (0, jnp.clip(j - 2 * nq, 0, 2 * nq - 1))), pl.BlockSpec(memory_space=pl.ANY)],
               out_specs=[blk(lambda j: (0, j)), pl.BlockSpec((8, LANES), lambda j: (0, j))],
               out_shape=[_sds(into.shape, into.dtype), _sds((8, GDN_CONV_CH), F32)], input_output_aliases={5: 0},
               compiler_params=_cparams(("parallel",)), name=name)(proj, conv_w, dq, dk, dv, into)


_NN, _NT, _TN = ((1,), (0,)), ((1,), (1,)), ((0,), (0,))


def _split(x, n):
    parts = []
    for _ in range(n):
        h = x.astype(BF16)
        parts.append(h)
        x = x - h.astype(F32)
    return parts


def _dot(a, b, dn, mode):
    d = lambda p, q: lax.dot_general(p, q, (dn, ((), ())), preferred_element_type=F32)
    if mode == "lo":
        return d(a.astype(BF16), b.astype(BF16))
    if mode == "x3":
        (ah, al), (bh, bl) = _split(a, 2), _split(b, 2)
        return d(ah, bh) + (d(ah, bl) + d(al, bh))
    b0, b1, b2 = _split(b, 3)
    ab = a.astype(BF16)
    return d(ab, b0) + (d(ab, b1) + d(ab, b2))


def _make_dots(mode):
    @jax.custom_vjp
    def nn(a, b):
        return _dot(a, b, _NN, mode)

    @jax.custom_vjp
    def nt(a, b):
        return _dot(a, b, _NT, mode)

    @jax.custom_vjp
    def tn(a, b):
        return _dot(a, b, _TN, mode)

    nn.defvjp(lambda a, b: (nn(a, b), (a, b)), lambda r, d: (nt(d, r[1]), tn(r[0], d)))
    nt.defvjp(lambda a, b: (nt(a, b), (a, b)), lambda r, d: (nn(d, r[1]), tn(d, r[0])))
    tn.defvjp(lambda a, b: (tn(a, b), (a, b)), lambda r, d: (nt(r[1], d), nn(r[0], d)))
    return nn, nt, tn


_nn_hi, _nt_hi, _tn_hi = _make_dots("x3")
_nn_lo, _nt_lo, _tn_lo = _make_dots("lo")


@jax.custom_vjp
def _nn_const(a, b):
    return _dot(a, b, _NN, "xl")


_nn_const.defvjp(lambda a, b: (_nn_const(a, b), a), lambda a, d: (jnp.zeros_like(a), _dot(a, d, _TN, "xl")))


def _each(f, *lists):
    return [f(*xs) for xs in zip(*lists)]


@jax.custom_vjp
def _unit_inverses(xs):
    C = xs[0].shape[0]
    eye = (lax.broadcasted_iota(I32, (C, C), 0) == lax.broadcasted_iota(I32, (C, C), 1)).astype(F32)
    ainv, p = [eye + a for a in xs], xs
    for _ in range(int(math.log2(C)) - 1):
        p = _each(lambda a: _dot(a, a, _NN, "x3"), p)
        ainv = _each(lambda a, b: a + _dot(a, b, _NN, "x3"), ainv, p)
    return ainv


def _unit_inverses_bwd(ainv, d):
    left = _each(lambda a, g: _dot(a, g, _TN, "x3"), ainv, d)
    return (_each(lambda t, a: _dot(t, a, _NT, "x3"), left, ainv),)


_unit_inverses.defvjp(lambda xs: (lambda a: (a, a))(_unit_inverses(xs)), _unit_inverses_bwd)


def _gdn_chunk(q, k, v, gb, bb, state):
    C = GDN_C
    e0 = (lax.broadcasted_iota(I32, (1, LANES), 1) == 0).astype(F32)
    ri = lax.broadcasted_iota(I32, (C, C), 0)
    ci = lax.broadcasted_iota(I32, (C, C), 1)
    causal, strict = ri >= ci, ri > ci
    tri, eye, ones = causal.astype(F32), (ri == ci).astype(F32), jnp.ones((C, C), F32)
    last = lax.broadcasted_iota(I32, (C, LANES), 0) == C - 1
    g1 = _each(lambda a: jnp.sum(a * e0, -1, keepdims=True), gb)
    b1 = _each(lambda a: jnp.sum(a * e0, -1, keepdims=True), bb)
    gc_c = _each(lambda g: _nn_const(tri, jnp.broadcast_to(g, (C, C))), g1)
    gc_d = _each(lambda g: _nn_const(tri, jnp.broadcast_to(g, (C, LANES))), g1)
    gr_c = _each(lambda g: _nn_const(ones, eye * g), gc_c)
    decay = _each(lambda a, r: jnp.where(causal, jnp.exp(jnp.where(causal, a - r, 0.0)), 0.0), gc_c, gr_c)
    kb = _each(lambda a, b: a * b, k, b1)
    vb = _each(lambda a, b: a * b, v, b1)
    x = _each(lambda a, b, d: -jnp.where(strict, _nt_lo(a, b) * d, 0.0), kb, k, decay)
    ainv = _unit_inverses(x)
    u = _each(_nn_hi, ainv, vb)
    w = _each(lambda a, b, g: _nn_hi(a, b * jnp.exp(g)), ainv, kb, gc_d)
    attn = _each(lambda a, b, d: jnp.where(causal, _nt_lo(a, b) * d, 0.0), q, k, decay)
    v_new = _each(lambda a, b, s: a - _nn_lo(b, s), u, w, state)
    o = _each(lambda a, g, s, t, vn: _nn_lo(a * jnp.exp(g), s) + _nn_lo(t, vn), q, gc_d, state, attn, v_new)
    gl = _each(lambda g: jnp.sum(jnp.where(last, g, 0.0), axis=0, keepdims=True), gc_d)
    new_state = _each(lambda s, g, a, gd, vn: s * jnp.exp(jnp.sum(g * e0, -1, keepdims=True)) + _tn_lo(a * jnp.exp(g - gd), vn),
                      state, gl, k, gc_d, v_new)
    return o, new_state


def _head_slices(ref, width):
    return [ref[:, h * width:(h + 1) * width] for h in range(GDN_H)]


def gdn_chunk_fwd(qkv, g_b, beta_b, name):
    S_ = qkv.shape[0]
    N = S_ // GDN_C

    def body(q_ref, k_ref, v_ref, g_ref, b_ref, o_ref, s_ref, state):
        @pl.when(pl.program_id(0) == 0)
        def _():
            state[...] = jnp.zeros_like(state)

        st = [state[h] for h in range(GDN_H)]
        s_ref[0] = state[...]
        o, st2 = _gdn_chunk(_head_slices(q_ref, GDN_DK), _head_slices(k_ref, GDN_DK), _head_slices(v_ref, GDN_DV),
                            _head_slices(g_ref, GDN_DK), _head_slices(b_ref, GDN_DK), st)
        for h in range(GDN_H):
            o_ref[:, h * GDN_DV:(h + 1) * GDN_DV] = o[h]
            state[h] = st2[h]

    return _pc(body, grid=(N,),
               in_specs=[pl.BlockSpec((GDN_C, GDN_QK), lambda n: (n, 0)), pl.BlockSpec((GDN_C, GDN_QK), lambda n: (n, 1)),
                         pl.BlockSpec((GDN_C, GDN_V), lambda n: (n, 1)), pl.BlockSpec((GDN_C, GDN_QK), lambda n: (n, 0)),
                         pl.BlockSpec((GDN_C, GDN_QK), lambda n: (n, 0))],
               out_specs=[pl.BlockSpec((GDN_C, GDN_V), lambda n: (n, 0)),
                          pl.BlockSpec((1, GDN_H, GDN_DK, GDN_DV), lambda n: (n, 0, 0, 0))],
               out_shape=[_sds((S_, GDN_V), F32), _sds((N, GDN_H, GDN_DK, GDN_DV), F32)],
               scratch_shapes=[pltpu.VMEM((GDN_H, GDN_DK, GDN_DV), F32)],
               compiler_params=_cparams(("arbitrary",)), name=name)(qkv, qkv, qkv, g_b, beta_b)


def gdn_chunk_bwd(qkv, g_b, beta_b, states, do, name):
    S_ = qkv.shape[0]
    N = S_ // GDN_C

    def body(q_ref, k_ref, v_ref, g_ref, b_ref, s_ref, do_ref, dq_ref, dk_ref, dv_ref, dg_ref, db_ref, dstate):
        @pl.when(pl.program_id(0) == 0)
        def _():
            dstate[...] = jnp.zeros_like(dstate)

        _, vjp = jax.vjp(_gdn_chunk, _head_slices(q_ref, GDN_DK), _head_slices(k_ref, GDN_DK), _head_slices(v_ref, GDN_DV),
                         _head_slices(g_ref, GDN_DK), _head_slices(b_ref, GDN_DK), [s_ref[0, h] for h in range(GDN_H)])
        dq, dk, dv, dg, db, ds = vjp((_head_slices(do_ref, GDN_DV), [dstate[h] for h in range(GDN_H)]))
        for h in range(GDN_H):
            kk, vv = slice(h * GDN_DK, (h + 1) * GDN_DK), slice(h * GDN_DV, (h + 1) * GDN_DV)
            dq_ref[:, kk] = dq[h]
            dk_ref[:, kk] = dk[h]
            dv_ref[:, vv] = dv[h]
            dg_ref[:, kk] = dg[h]
            db_ref[:, kk] = db[h]
            dstate[h] = ds[h]

    r = lambda n: N - 1 - n
    qk = lambda c: pl.BlockSpec((GDN_C, GDN_QK), lambda n: (r(n), c))
    vs = lambda c: pl.BlockSpec((GDN_C, GDN_V), lambda n: (r(n), c))
    return _pc(body, grid=(N,),
               in_specs=[qk(0), qk(1), vs(1), qk(0), qk(0),
                         pl.BlockSpec((1, GDN_H, GDN_DK, GDN_DV), lambda n: (r(n), 0, 0, 0)), vs(0)],
               out_specs=[qk(0), qk(0), vs(0), qk(0), qk(0)],
               out_shape=[_sds((S_, GDN_QK), F32), _sds((S_, GDN_QK), F32), _sds((S_, GDN_V), F32),
                          _sds((S_, GDN_QK), F32), _sds((S_, GDN_QK), F32)],
               scratch_shapes=[pltpu.VMEM((GDN_H, GDN_DK, GDN_DV), F32)],
               compiler_params=_cparams(("arbitrary",)), name=name)(qkv, qkv, qkv, g_b, beta_b, states, do)


def _rope_tables(pos_ref, inv_ref, cm_ref, sg_ref):
    ang = pos_ref[...] * inv_ref[...]
    return jnp.cos(ang) * cm_ref[...], jnp.sin(ang) * sg_ref[...]


def mla_prep_fwd(qpad, kv, proj, pos, rope_consts, name):
    S_ = qpad.shape[0]
    ts = 256
    W = 2 * LANES

    def body(q_ref, kv_ref, kr_ref, pos_ref, inv_ref, cm_ref, sg_ref, qh_ref, kh_ref, vh_ref):
        cs, sn = _rope_tables(pos_ref, inv_ref, cm_ref, sg_ref)
        rope = lambda r: r * cs + pltpu.roll(r, LANES // 2, 1) * sn
        krr = rope(kr_ref[...].astype(F32)).astype(BF16)
        for h in range(MLA_H):
            qh_ref[h, :, 0:LANES] = (q_ref[:, h * W:h * W + LANES].astype(F32) * MLA_SCALE).astype(BF16)
            qh_ref[h, :, LANES:W] = (rope(q_ref[:, h * W + LANES:(h + 1) * W].astype(F32)) * MLA_SCALE).astype(BF16)
            kh_ref[h, :, 0:LANES] = kv_ref[:, h * W:h * W + LANES].astype(BF16)
            kh_ref[h, :, LANES:W] = krr
            vh_ref[h] = kv_ref[:, h * W + LANES:(h + 1) * W].astype(BF16)

    one = pl.BlockSpec((1, LANES), lambda i: (0, 0))
    return _pc(body, grid=(S_ // ts,),
               in_specs=[pl.BlockSpec((ts, MLA_H * W), lambda i: (i, 0)), pl.BlockSpec((ts, MLA_H * W), lambda i: (i, 0)),
                         pl.BlockSpec((ts, LANES), lambda i: (i, 1536 // LANES)), pl.BlockSpec((ts, 1), lambda i: (i, 0)),
                         one, one, one],
               out_specs=[pl.BlockSpec((MLA_H, ts, W), lambda i: (0, i, 0)), pl.BlockSpec((MLA_H, ts, W), lambda i: (0, i, 0)),
                          pl.BlockSpec((MLA_H, ts, LANES), lambda i: (0, i, 0))],
               out_shape=[_sds((MLA_H, S_, W), BF16), _sds((MLA_H, S_, W), BF16), _sds((MLA_H, S_, LANES), BF16)],
               compiler_params=_cparams(("parallel",)), name=name)(qpad, kv, proj, pos, *rope_consts)


def mla_prep_bwd(dqh, dkh, dvh, pos, rope_consts, into, name):
    S_ = dqh.shape[1]
    ts = 256
    W = 2 * LANES

    def body(dq_ref, dk_ref, dv_ref, pos_ref, inv_ref, cm_ref, sg_ref, _, dqp_ref, dkv_ref, dkr_ref):
        cs, sn = _rope_tables(pos_ref, inv_ref, cm_ref, sg_ref)
        rope_t = lambda g: g * cs + pltpu.roll(g * sn, LANES // 2, 1)
        acc = jnp.zeros((ts, LANES), F32)
        for h in range(MLA_H):
            dqp_ref[:, h * W:h * W + LANES] = (dq_ref[h, :, 0:LANES].astype(F32) * MLA_SCALE).astype(BF16)
            dqp_ref[:, h * W + LANES:(h + 1) * W] = (rope_t(dq_ref[h, :, LANES:W].astype(F32)) * MLA_SCALE).astype(BF16)
            dkv_ref[:, h * W:h * W + LANES] = dk_ref[h, :, 0:LANES]
            dkv_ref[:, h * W + LANES:(h + 1) * W] = dv_ref[h]
            acc = acc + dk_ref[h, :, LANES:W].astype(F32)
        dkr_ref[...] = rope_t(acc).astype(dkr_ref.dtype)

    one = pl.BlockSpec((1, LANES), lambda i: (0, 0))
    return _pc(body, grid=(S_ // ts,),
               in_specs=[pl.BlockSpec((MLA_H, ts, W), lambda i: (0, i, 0)), pl.BlockSpec((MLA_H, ts, W), lambda i: (0, i, 0)),
                         pl.BlockSpec((MLA_H, ts, LANES), lambda i: (0, i, 0)), pl.BlockSpec((ts, 1), lambda i: (i, 0)),
                         one, one, one, pl.BlockSpec(memory_space=pl.ANY)],
               out_specs=[pl.BlockSpec((ts, MLA_H * W), lambda i: (i, 0)), pl.BlockSpec((ts, MLA_H * W), lambda i: (i, 0)),
                          pl.BlockSpec((ts, LANES), lambda i: (i, 1536 // LANES))],
               out_shape=[_sds((S_, MLA_H * W), BF16), _sds((S_, MLA_H * W), BF16), _sds(into.shape, into.dtype)],
               input_output_aliases={7: 2}, compiler_params=_cparams(("parallel",)), name=name)(dqh, dkh, dvh, pos, *rope_consts, into)


NEG = -1e30


FLASH_TILE = 1024
FLASH_SUB = 512


def _scores(q, k, diagonal):
    s = lax.dot_general(q, k, (_NT, ((), ())), preferred_element_type=F32)
    if not diagonal:
        return s
    return jnp.where(lax.broadcasted_iota(I32, s.shape, 1) <= lax.broadcasted_iota(I32, s.shape, 0), s, NEG)


def _sub_blocks(t, diagonal):
    sub = min(FLASH_SUB, t) if diagonal else t
    return [(c * sub if diagonal else 0, slice(c * sub, (c + 1) * sub)) for c in range(t // sub)]


FLASH_HEADS = 2


def flash_fwd(qh, kh, vh, name):
    H, S_, W = qh.shape
    t = _tile(S_, FLASH_TILE)
    n = S_ // t
    G = FLASH_HEADS
    heads = list(range(G))

    def body(q_ref, k_ref, v_ref, o_ref, lse_ref, m_s, l_s, acc):
        qi, kj = pl.program_id(1), pl.program_id(2)

        @pl.when(kj == 0)
        def _():
            m_s[...] = jnp.full_like(m_s, NEG)
            l_s[...] = jnp.zeros_like(l_s)
            acc[...] = jnp.zeros_like(acc)

        def step(diagonal):
            s = _each(lambda a: _scores(q_ref[a], k_ref[a], diagonal), heads)
            m_old = _each(lambda a: m_s[a], heads)
            m_new = _each(lambda mo, sa: jnp.maximum(mo, jnp.max(sa, axis=-1, keepdims=True)), m_old, s)
            alpha = _each(lambda mo, mn: jnp.exp(mo - mn), m_old, m_new)
            p = _each(lambda sa, mn: jnp.exp(sa - mn[:, :1]), s, m_new)
            pv = _each(lambda pa, a: lax.dot_general(pa.astype(BF16), v_ref[a], (_NN, ((), ())), preferred_element_type=F32), p, heads)
            for a in heads:
                l_s[a] = alpha[a] * l_s[a] + jnp.sum(p[a], axis=-1, keepdims=True)
                acc[a] = alpha[a] * acc[a] + pv[a]
                m_s[a] = m_new[a]

        pl.when(kj < qi)(lambda: step(False))
        pl.when(kj == qi)(lambda: step(True))

        @pl.when(kj == n - 1)
        def _():
            for a in heads:
                o_ref[:, a * LANES:(a + 1) * LANES] = (acc[a] / l_s[a]).astype(o_ref.dtype)
                lse_ref[a] = m_s[a] + jnp.log(l_s[a])

    return _pc(body, grid=(H // G, n, n),
               in_specs=[pl.BlockSpec((G, t, W), lambda h, i, j: (h, i, 0)),
                         pl.BlockSpec((G, t, W), lambda h, i, j: (h, jnp.minimum(i, j), 0)),
                         pl.BlockSpec((G, t, LANES), lambda h, i, j: (h, jnp.minimum(i, j), 0))],
               out_specs=[pl.BlockSpec((t, G * LANES), lambda h, i, j: (i, h)), pl.BlockSpec((G, t, LANES), lambda h, i, j: (h, i, 0))],
               out_shape=[_sds((S_, H * LANES), BF16), _sds((H, S_, LANES), F32)],
               scratch_shapes=[pltpu.VMEM((G, t, LANES), F32)] * 3,
               compiler_params=_cparams(("parallel", "parallel", "arbitrary")), name=name)(qh, kh, vh)


def flash_bwd(qh, kh, vh, o, lse, do, name):
    H, S_, W = qh.shape
    t = _tile(S_, FLASH_TILE)
    n = S_ // t

    def body(q_ref, k_ref, v_ref, o_ref, lse_ref, do_ref, dq_ref, dk_ref, dv_ref, dq_acc, dk_acc, dv_acc):
        kj, qi = pl.program_id(1), pl.program_id(2)

        @pl.when(jnp.logical_and(kj == 0, qi == 0))
        def _():
            dq_acc[...] = jnp.zeros_like(dq_acc)

        @pl.when(qi == 0)
        def _():
            dk_acc[...] = jnp.zeros_like(dk_acc)
            dv_acc[...] = jnp.zeros_like(dv_acc)

        def step(diagonal):
            do_ = do_ref[...]
            dob = do_.astype(BF16)
            delta = jnp.sum(do_.astype(F32) * o_ref[...].astype(F32), axis=-1, keepdims=True)
            for r0, keys in _sub_blocks(t, diagonal):
                q, k, v = q_ref[r0:, :], k_ref[keys, :], v_ref[keys, :]
                p = jnp.exp(_scores(q, k, diagonal) - lse_ref[r0:, :1])
                dv_acc[keys, :] += lax.dot_general(p.astype(BF16), dob[r0:], (_TN, ((), ())), preferred_element_type=F32)
                dp = lax.dot_general(dob[r0:], v, (_NT, ((), ())), preferred_element_type=F32)
                ds = (p * (dp - delta[r0:])).astype(BF16)
                dk_acc[keys, :] += lax.dot_general(ds, q, (_TN, ((), ())), preferred_element_type=F32)
                rows = pl.ds(pl.multiple_of(qi * t, t) + r0, t - r0)
                dq_acc[rows, :] += lax.dot_general(ds, k, (_NN, ((), ())), preferred_element_type=F32)

        pl.when(qi > kj)(lambda: step(False))
        pl.when(qi == kj)(lambda: step(True))

        @pl.when(qi == n - 1)
        def _():
            dk_ref[...] = dk_acc[...].astype(BF16)
            dv_ref[...] = dv_acc[...].astype(BF16)

        @pl.when(jnp.logical_and(kj == n - 1, qi == n - 1))
        def _():
            dq_ref[...] = dq_acc[...].astype(BF16)

    qrow = lambda h, j, i: jnp.maximum(i, j)
    return _pc(body, grid=(H, n, n),
               in_specs=[pl.BlockSpec((None, t, W), lambda h, j, i: (h, qrow(h, j, i), 0)),
                         pl.BlockSpec((None, t, W), lambda h, j, i: (h, j, 0)),
                         pl.BlockSpec((None, t, LANES), lambda h, j, i: (h, j, 0)),
                         pl.BlockSpec((t, LANES), lambda h, j, i: (qrow(h, j, i), h)),
                         pl.BlockSpec((None, t, LANES), lambda h, j, i: (h, qrow(h, j, i), 0)),
                         pl.BlockSpec((t, LANES), lambda h, j, i: (qrow(h, j, i), h))],
               out_specs=[pl.BlockSpec((None, S_, W), lambda h, j, i: (h, 0, 0)),
                          pl.BlockSpec((None, t, W), lambda h, j, i: (h, j, 0)),
                          pl.BlockSpec((None, t, LANES), lambda h, j, i: (h, j, 0))],
               out_shape=[_sds((H, S_, W), BF16), _sds((H, S_, W), BF16), _sds((H, S_, LANES), BF16)],
               scratch_shapes=[pltpu.VMEM((S_, W), F32), pltpu.VMEM((t, W), F32), pltpu.VMEM((t, LANES), F32)],
               compiler_params=_cparams(("parallel", "arbitrary", "arbitrary")), name=name)(qh, kh, vh, o, lse, do)


def loss_head(x, target, g, name):
    S_ = x.shape[0]
    ts = 256

    def body(x_ref, t_ref, g_ref, l_ref, dx_ref, dg_ref):
        @pl.when(pl.program_id(0) == 0)
        def _():
            l_ref[...] = jnp.zeros_like(l_ref)
            dg_ref[...] = jnp.zeros_like(dg_ref)

        y, vjp = jax.vjp(_rms, x_ref[...], g_ref[...])
        err = y - t_ref[...]
        l_ref[...] += 0.5 * jnp.sum(jnp.sum(err * err, axis=-1, keepdims=True), axis=0, keepdims=True) / D
        dx, dg = vjp(err / D)
        dx_ref[...] = dx
        dg_ref[...] += dg

    row = pl.BlockSpec((ts, D), lambda i: (i, 0))
    return _pc(body, grid=(S_ // ts,), in_specs=[row, row, pl.BlockSpec((1, D), lambda i: (0, 0))],
               out_specs=[pl.BlockSpec((1, LANES), lambda i: (0, 0)), row, pl.BlockSpec((1, D), lambda i: (0, 0))],
               out_shape=[_sds((1, LANES), F32), _sds((S_, D), F32), _sds((1, D), F32)],
               compiler_params=_cparams(("arbitrary",)), name=name)(x, target, g)


def adamw(w, parts, m, v, name):
    R, C = w.shape
    rows = [p.shape[1] for p in parts[0]]
    tr = R
    for cand in (512, 256, 128, 64, 32, 16, 8):
        if all(r % cand == 0 for r in rows) and cand * C * 4 <= 1024 * 1024:
            tr = cand
            break
    c1 = 1.0 - ADAM_B1 ** ADAM_STEP
    c2 = 1.0 - ADAM_B2 ** ADAM_STEP
    starts = [sum(rows[:k]) // tr for k in range(len(rows))]
    flat = [p for part in parts for p in part]

    def body(*refs):
        w_ref, m_ref, v_ref = refs[0], refs[1 + len(flat)], refs[2 + len(flat)]
        g_ref, d_ref, nm_ref, nv_ref = refs[3 + len(flat):]
        i = pl.program_id(0)
        gg, at = None, 1
        for part in parts:
            val = None
            for k in range(len(part)):
                p_ref = refs[at]
                at += 1
                s = p_ref[0].astype(F32)
                for n in range(1, p_ref.shape[0]):
                    s = s + p_ref[n].astype(F32)
                val = s if val is None else jnp.where(i >= starts[k], s, val)
            gg = val if gg is None else gg + val
        m2 = ADAM_B1 * m_ref[...] + (1.0 - ADAM_B1) * gg
        v2 = ADAM_B2 * v_ref[...] + (1.0 - ADAM_B2) * (gg * gg)
        g_ref[...] = gg
        d_ref[...] = -ADAM_LR * ((m2 / c1) / (jnp.sqrt(v2 / c2) + ADAM_EPS) + ADAM_WD * w_ref[...])
        nm_ref[...] = m2
        nv_ref[...] = v2

    blk = pl.BlockSpec((tr, C), lambda i: (i, 0))
    piece = lambda p, k: pl.BlockSpec((p.shape[0], tr, C), lambda i: (0, jnp.clip(i - starts[k], 0, rows[k] // tr - 1), 0))
    pblk = [piece(p, k) for part in parts for k, p in enumerate(part)]
    return _pc(body, grid=(R // tr,), in_specs=[blk] + pblk + [blk, blk], out_specs=[blk] * 4, out_shape=[_sds((R, C), F32)] * 4,
               compiler_params=_cparams(("parallel",)), name=name)(w, *flat, m, v)


def sum_slots(recv, name):
    n, R, C = recv.shape

    def body(r_ref, o_ref):
        acc = r_ref[0]
        for s in range(1, n):
            acc = acc + r_ref[s]
        o_ref[...] = acc

    return _pc(body, grid=(1,), in_specs=[pl.BlockSpec((n, R, C), lambda i: (0, 0, 0))],
               out_specs=pl.BlockSpec((R, C), lambda i: (0, 0)), out_shape=_sds((R, C), F32),
               compiler_params=_cparams(("arbitrary",)), name=name)(recv)


def _chip_peers():
    x, y, c = lax.axis_index("x"), lax.axis_index("y"), lax.axis_index("c")
    return (x, y, c), [(1 - x, y, c), (x, 1 - y, c), (1 - x, 1 - y, c)]


def _chip_index(p):
    return 2 * p[0] + p[1]


def _win(ref, axis, chip, size):
    if axis is None:
        return ref.at[chip]
    idx = [slice(None)] * len(ref.shape)
    idx[axis] = pl.ds(pl.multiple_of(chip * size, size), size)
    return ref.at[tuple(idx)]


def _remote(src, dst, send_sem, recv_sem, peer):
    return pltpu.make_async_remote_copy(src_ref=src, dst_ref=dst, send_sem=send_sem, recv_sem=recv_sem, device_id=peer,
                                        device_id_type=MESH)


HBM_SPEC = pl.BlockSpec(memory_space=pltpu.HBM)
SEM_SPEC = pl.BlockSpec(memory_space=pltpu.SEMAPHORE)
ANY_SPEC = pl.BlockSpec(memory_space=pl.ANY)
DATAFLOW = pltpu.SideEffectType.DATAFLOW_SIDE_EFFECTING


def gather_piece(i, l, o, axis, size):
    return (i, lambda r, chip: r.at[l], o, lambda r, chip: _win(r, axis, chip, size))


def scatter_piece(i, o, axis, size):
    return (i, lambda r, chip: _win(r, axis, chip, size), o, lambda r, chip: r.at[chip])


def whole_piece(i):
    return (i, lambda r, chip: r, i, lambda r, chip: r)


def _copies(pieces, in_refs, out_refs, send, recv, sibling):
    me, peers = _chip_peers()
    if sibling:
        peers = [(me[0], me[1], 1 - me[2])]
    mine = _chip_index(me)
    remote = []
    for n, (i, src, o, dst) in enumerate(pieces):
        d = dst(out_refs[o], mine)
        remote += [_remote(src(in_refs[i], _chip_index(p)), d, send.at[len(peers) * n + k], recv.at[len(peers) * n + k], p)
                   for k, p in enumerate(peers)]
    return remote


def own_window(a, axis, size, chip):
    if axis is None:
        return lax.dynamic_index_in_dim(a, chip, 0, keepdims=False)
    return lax.dynamic_slice_in_dim(a, chip * size, size, axis=axis)


def place_own(land, own, axis, size, chip):
    if axis is None:
        return lax.dynamic_update_slice_in_dim(land, own[None], chip, axis=0)
    return lax.dynamic_update_slice_in_dim(land, own, chip * size, axis=axis)


def exchange_start(pieces, ins, out_shapes, after, name, sibling=False):
    n_in, n_out, ncp = len(ins), len(out_shapes), len(pieces)

    def body(*refs):
        in_refs, land_refs = refs[:n_in], refs[n_in:n_in + n_out]
        send, recv = refs[n_in + n_out + 1], refs[n_in + n_out + 2]
        token = refs[-1]
        for cp in _copies(pieces, in_refs, land_refs, send, recv, sibling):
            cp.start()
        token[...] = jnp.zeros_like(token)

    hbm = lambda a: pltpu.with_memory_space_constraint(a, pltpu.HBM)
    lands = [hbm(lax.empty(s.shape, s.dtype)) for s in out_shapes]
    sem = pltpu.SemaphoreType.DMA(((1 if sibling else 3) * ncp,))
    thru = [pltpu.HBM(a.shape, a.dtype) for a in ins] + [pltpu.HBM(s.shape, s.dtype) for s in out_shapes]
    res = _pc(body, in_specs=[HBM_SPEC] * (n_in + n_out) + [ANY_SPEC],
              out_specs=[SEM_SPEC, SEM_SPEC] + [HBM_SPEC] * (n_in + n_out) + [pl.BlockSpec(memory_space=pltpu.VMEM)],
              out_shape=[sem, sem] + thru + [_sds((8, LANES), F32)],
              input_output_aliases={i: 2 + i for i in range(n_in + n_out)},
              compiler_params=pltpu.CompilerParams(has_side_effects=DATAFLOW), name=name)(*[hbm(a) for a in ins], *lands, after)
    return (res[0], res[1]), list(res[2:2 + n_in]), list(res[2 + n_in:2 + n_in + n_out]), res[-1]


def exchange_wait(pieces, sems, ins, lands, after, name, sibling=False):
    n_in, n_out = len(ins), len(lands)

    def body(*refs):
        in_refs, land_refs = refs[:n_in], refs[n_in:n_in + n_out]
        send, recv = refs[n_in + n_out], refs[n_in + n_out + 1]
        for cp in _copies(pieces, in_refs, land_refs, send, recv, sibling):
            cp.wait_send()
            cp.wait_recv()

    thru = [pltpu.HBM(a.shape, a.dtype) for a in ins] + [pltpu.HBM(a.shape, a.dtype) for a in lands]
    res = _pc(body, in_specs=[HBM_SPEC] * (n_in + n_out) + [SEM_SPEC, SEM_SPEC, ANY_SPEC], out_specs=[HBM_SPEC] * (n_in + n_out),
              out_shape=thru, input_output_aliases={i: i for i in range(n_in + n_out)},
              compiler_params=pltpu.CompilerParams(has_side_effects=DATAFLOW), name=name)(*ins, *lands, sems[0], sems[1], after)
    return list(res[:n_in]), list(res[n_in:])


def exchange_all(buf, name):
    def body(in_ref, out_ref, send, recv, local):
        x, y, c = lax.axis_index("x"), lax.axis_index("y"), lax.axis_index("c")
        mine = 4 * x + 2 * y + c
        loc = pltpu.make_async_copy(in_ref, out_ref.at[mine], local)
        loc.start()
        copies = [loc]
        for k in range(1, 8):
            peer = (x ^ (k >> 2), y ^ ((k >> 1) & 1), c ^ (k & 1))
            cp = pltpu.make_async_remote_copy(src_ref=in_ref, dst_ref=out_ref.at[mine], send_sem=send.at[k - 1],
                                              recv_sem=recv.at[k - 1], device_id=peer, device_id_type=MESH)
            cp.start()
            copies.append(cp)
        for cp in copies:
            cp.wait()

    anyspec = pl.BlockSpec(memory_space=pl.ANY)
    return _pc(body, in_specs=[anyspec], out_specs=anyspec, out_shape=_sds((8,) + buf.shape, buf.dtype),
               scratch_shapes=[pltpu.SemaphoreType.DMA((7,)), pltpu.SemaphoreType.DMA((7,)), pltpu.SemaphoreType.DMA],
               name=name)(buf)


def _norm_fwd(x, g, name):
    return rowwise(f_rms, [(x, D, 0, 0)], [(g, D, 0, 0)], [(D, 0, BF16)], ts=512, name=name)[0]


def _norm_bwd(x, g, dh, dres, name):
    (dx,), (dg,) = rowwise_bwd(f_rms, [(x, D, 0, 0)], [(g, D, 0, 0)], [(dh, D, 0, 0)], need=[True],
                               adds={0: (dres, D, 0, 0)}, ts=256, name=name)
    return dx, dg


def pool_fwd(x, W, tag, late=None):
    h = _norm_fwd(x, W["ng"], tag + "_norm")
    proj = mm(h, W["w_in"], out_dtype=BF16, name=tag + "_in")
    if late is not None:
        W = dict(W, **late(proj))
    p = pool_time_fwd(proj, tag + "_win")
    pg = gmm("nn", p, W["w_grp"], G=4, out_dtype=BF16, name=tag + "_grp")
    y = rowwise(f_pool_gate, [(pg, POOL_GROUP, 0, 1), (proj, POOL_GROUP, 4, 1)], [(W["scale"], POOL_GROUP, 0, 1)],
                [(POOL_GROUP, 1, BF16)], ncol=4, ts=512, name=tag + "_gate")[0]
    xn = mm(y, W["w_out"], add=x, name=tag + "_out")
    return xn, (x, h, proj, p, pg, y)


def pool_bwd(dxn, W, saved, tag, after=None, emit=None):
    x, h, proj, p, pg, y = saved
    emit = emit or (lambda grads: None)
    dy = mm(dxn, W["w_out"], tb=True, after=after, out_dtype=BF16, name=tag + "_dy")
    g = {}
    (dpg, dproj), (g["scale"],) = rowwise_bwd(
        f_pool_gate, [(pg, POOL_GROUP, 0, 1), (proj, POOL_GROUP, 4, 1)], [(W["scale"], POOL_GROUP, 0, 1)],
        [(dy, POOL_GROUP, 0, 1)], need=[True, True], place={1: (2 * POOL_WIDTH, 4)}, narrow=(0, 1), ncol=4, ts=512, name=tag + "_dgate")
    dp = gmm("nt", dpg, W["w_grp"], G=4, out_dtype=BF16, name=tag + "_dp")
    dproj = pool_time_bwd(dp, dproj, tag + "_dwin")
    g["w_in"] = mm(h, dproj, ta=True, out_dtype=BF16, name=tag + "_dw_in")
    t1 = emit({"w_in": g["w_in"]})
    g["w_out"] = mm(y, dxn, ta=True, after=t1, out_dtype=BF16, name=tag + "_dwout")
    g["w_grp"] = gmm("tn", p, dpg, G=4, out_dtype=BF16, name=tag + "_dwgrp")
    t2 = emit({"w_out": g["w_out"], "w_grp": g["w_grp"]})
    dh = mm(dproj, W["w_in"], tb=True, after=t2, name=tag + "_dh")
    dx, g["ng"] = _norm_bwd(x, W["ng"], dh, dxn, tag + "_dnorm")
    return dx, g


def gdn_fwd(x, W, tag, late=None):
    h = _norm_fwd(x, W["ng"], tag + "_norm")
    proj = mm(h, W["w_in"], name=tag + "_in")
    qkv = gdn_conv_fwd(proj, W["conv"], tag + "_conv")
    g_b, beta_b = rowwise(f_gdn_gates, [(proj, LANES, 6144 // LANES, 0)], [(W["a_log"], LANES, 0, 0), (W["dt_bias"], LANES, 0, 0)],
                          [(GDN_QK, 0, F32), (GDN_QK, 0, F32)], ts=512, name=tag + "_gates")
    o, states = gdn_chunk_fwd(qkv, g_b, beta_b, tag + "_chunk")
    og = rowwise(f_gdn_out, [(o, GDN_DV, 0, 1), (proj, GDN_DV, 4096 // GDN_DV, 1)], [(W["norm_g"], GDN_DV, 0, 0)],
                 [(GDN_DV, 1, BF16)], ncol=GDN_H, ts=512, name=tag + "_onorm")[0]
    if late is not None:
        W = dict(W, **late(og))
    xn = mm(og, W["w_out"], add=x, name=tag + "_out")
    return xn, (x, h, proj, qkv, g_b, beta_b, o, states, og)


def gdn_bwd(dxn, W, saved, tag, after=None):
    x, h, proj, qkv, g_b, beta_b, o, states, og = saved
    dog = mm(dxn, W["w_out"], tb=True, after=after, out_dtype=BF16, name=tag + "_dog")
    g = {"w_out": mm(og, dxn, ta=True, out_dtype=BF16, name=tag + "_dwout")}
    (do, dproj), (g["norm_g"],) = rowwise_bwd(
        f_gdn_out, [(o, GDN_DV, 0, 1), (proj, GDN_DV, 4096 // GDN_DV, 1)], [(W["norm_g"], GDN_DV, 0, 0)],
        [(dog, GDN_DV, 0, 1)], need=[True, True], place={1: (GDN_IN_PAD, 4096 // GDN_DV)}, narrow=(1,), ncol=GDN_H, ts=512, name=tag + "_donorm")
    dq, dk, dv, dg_b, dbeta_b = gdn_chunk_bwd(qkv, g_b, beta_b, states, do, tag + "_dchunk")
    (dproj,), (g["a_log"], g["dt_bias"]) = rowwise_bwd(
        f_gdn_gates, [(proj, LANES, 6144 // LANES, 0)], [(W["a_log"], LANES, 0, 0), (W["dt_bias"], LANES, 0, 0)],
        [(dg_b, GDN_QK, 0, 0), (dbeta_b, GDN_QK, 0, 0)], need=[True], place={0: (dproj, 6144 // LANES)}, ts=256, name=tag + "_dgates")
    dproj, g["conv"] = gdn_conv_bwd(proj, W["conv"], dq, dk, dv, dproj, tag + "_dconv")
    dh = mm(dproj, W["w_in"], tb=True, name=tag + "_dh")
    g["w_in"] = mm(h, dproj, ta=True, out_dtype=BF16, name=tag + "_dw_in")
    dx, g["ng"] = _norm_bwd(x, W["ng"], dh, dxn, tag + "_dnorm")
    return dx, g


def mla_fwd(x, pos, W, tag):
    h = _norm_fwd(x, W["ng"], tag + "_norm")
    proj = mm(h, W["w_in"], out_dtype=BF16, name=tag + "_in")
    hq = rowwise(f_rms, [(proj, MLA_Q_LORA, 0, 0)], [(W["q_g"], MLA_Q_LORA, 0, 0)], [(MLA_Q_LORA, 0, BF16)], ts=512, name=tag + "_qnorm")[0]
    hkv = rowwise(f_rms, [(proj, MLA_KV_LORA, 2, 0)], [(W["kv_g"], MLA_KV_LORA, 0, 0)], [(MLA_KV_LORA, 0, BF16)], ts=512, name=tag + "_kvnorm")[0]
    qpad = mm(hq, W["w_uq"], out_dtype=BF16, name=tag + "_uq")
    kv = mm(hkv, W["w_ukv"], out_dtype=BF16, name=tag + "_ukv")
    qh, kh, vh = mla_prep_fwd(qpad, kv, proj, pos, W["rope"], tag + "_prep")
    o, lse = flash_fwd(qh, kh, vh, tag + "_attn")
    og = rowwise(f_ogate, [(o, 512, 0, 1), (proj, 512, 4, 1)], [], [(512, 1, BF16)], ncol=4, ts=512, name=tag + "_ogate")[0]
    xn = mm(og, W["w_out"], add=x, name=tag + "_out")
    return xn, (x, h, proj, hq, hkv, qh, kh, vh, o, lse, og)


def mla_bwd(dxn, pos, W, saved, tag, after=None):
    x, h, proj, hq, hkv, qh, kh, vh, o, lse, og = saved
    dog = mm(dxn, W["w_out"], tb=True, after=after, out_dtype=BF16, name=tag + "_dog")
    g = {"w_out": mm(og, dxn, ta=True, out_dtype=BF16, name=tag + "_dwout")}
    dproj = jnp.zeros(proj.shape, BF16)
    (do, dproj), _ = rowwise_bwd(f_ogate, [(o, 512, 0, 1), (proj, 512, 4, 1)], [], [(dog, 512, 0, 1)], need=[True, True],
                                 place={1: (dproj, 4)}, narrow=(0,), ncol=4, ts=512, name=tag + "_dogate")
    dqh, dkh, dvh = flash_bwd(qh, kh, vh, o, lse, do, tag + "_dattn")
    dqpad, dkv, dproj = mla_prep_bwd(dqh, dkh, dvh, pos, W["rope"], dproj, tag + "_dprep")
    dhq = mm(dqpad, W["w_uq"], tb=True, name=tag + "_dhq")
    g["w_uq"] = mm(hq, dqpad, ta=True, out_dtype=BF16, name=tag + "_dwuq")
    dhkv = mm(dkv, W["w_ukv"], tb=True, name=tag + "_dhkv")
    g["w_ukv"] = mm(hkv, dkv, ta=True, out_dtype=BF16, name=tag + "_dwukv")
    (dproj,), (g["q_g"],) = rowwise_bwd(f_rms, [(proj, MLA_Q_LORA, 0, 0)], [(W["q_g"], MLA_Q_LORA, 0, 0)], [(dhq, MLA_Q_LORA, 0, 0)],
                                        need=[True], place={0: (dproj, 0)}, ts=256, name=tag + "_dqnorm")
    (dproj,), (g["kv_g"],) = rowwise_bwd(f_rms, [(proj, MLA_KV_LORA, 2, 0)], [(W["kv_g"], MLA_KV_LORA, 0, 0)], [(dhkv, MLA_KV_LORA, 0, 0)],
                                         need=[True], place={0: (dproj, 2)}, ts=256, name=tag + "_dkvnorm")
    dh = mm(dproj, W["w_in"], tb=True, name=tag + "_dh")
    g["w_in"] = mm(h, dproj, ta=True, out_dtype=BF16, name=tag + "_dw_in")
    dx, g["ng"] = _norm_bwd(x, W["ng"], dh, dxn, tag + "_dnorm")
    return dx, g


def _pad_cols(a, n):
    return jnp.pad(a, ((0, 0), (0, n - a.shape[1])))


def _mla_w_in_layout(w):
    z = lambda n: jnp.zeros((w.shape[0], n), w.dtype)
    kr = w[:, 1280:1344]
    return jnp.concatenate([w[:, :768], z(256), w[:, 768:1280], kr[:, :32], z(32), kr[:, 32:], z(32), z(384), w[:, 1344:]], axis=1)


def _mla_w_in_unlayout(g):
    return jnp.concatenate([g[:, :768], g[:, 1024:1536], g[:, 1536:1568], g[:, 1600:1632], g[:, 2048:]], axis=1)


def _mla_w_uq_layout(w):
    w3 = w.reshape(w.shape[0], MLA_H, MLA_NOPE + MLA_ROPE)
    z = jnp.zeros((w.shape[0], MLA_H, 32), w.dtype)
    return jnp.concatenate([w3[..., :128], w3[..., 128:160], z, w3[..., 160:192], z], axis=-1).reshape(w.shape[0], MLA_H * 256)


def _mla_w_uq_unlayout(g):
    g3 = g.reshape(g.shape[0], MLA_H, 256)
    return jnp.concatenate([g3[..., :128], g3[..., 128:160], g3[..., 192:224]], axis=-1).reshape(g.shape[0], MLA_H * 192)


def _rope_consts():
    half = MLA_ROPE // 2
    inv = ROPE_THETA ** (-jnp.arange(half, dtype=F32) / half)
    z = jnp.zeros((half,), F32)
    o = jnp.ones((half,), F32)
    row = lambda *p: jnp.concatenate(p).reshape(1, LANES)
    return row(inv, z, inv, z), row(o, z, o, z), row(-o, z, o, z)


BIG = ["pool_w_in", "pool_w_grp", "pool_w_out", "gdn_w_in", "gdn_w_out", "mla_w_in", "mla_w_uq", "mla_w_ukv", "mla_w_out"]
BIG_LAYOUT = {"pool_w_in": (1, 1024, (1024, 4096)), "pool_w_grp": (1, 128, (4, 512, 512)), "pool_w_out": (0, 512, (2048, 1024)),
              "gdn_w_in": (None, None, (4, 1024, 1540)), "gdn_w_out": (0, 512, (2048, 1024)),
              "mla_w_in": (None, None, (4, 1024, 848)), "mla_w_uq": (1, 768, (768, 3072)), "mla_w_ukv": (1, 1024, (512, 4096)),
              "mla_w_out": (0, 512, (2048, 1024))}
SMALL_SHARDED = ["pool_scale", "gdn_conv", "mla_q_norm_g", "mla_kv_norm_g"]
SMALL_AXIS = {"pool_scale": 1, "gdn_conv": 2, "mla_q_norm_g": 1, "mla_kv_norm_g": 1}
REPLICATED = ["norm_g", "gdn_a_log", "gdn_dt_bias", "gdn_norm_g", "final_g"]
PACK_C = 1024


def _pack(parts, dtype, row_mult):
    flat = jnp.concatenate([p.reshape(-1).astype(dtype) for p in parts])
    rows = -(-flat.shape[0] // PACK_C)
    rows = -(-rows // row_mult) * row_mult
    return jnp.pad(flat, (0, rows * PACK_C - flat.shape[0])).reshape(rows, PACK_C)


def _unpack(buf, shapes):
    lead = buf.shape[:-2]
    flat = buf.reshape(lead + (-1,))
    out, off = [], 0
    for s in shapes:
        n = int(np.prod(s))
        out.append(flat[..., off:off + n].reshape(lead + tuple(s)))
        off += n
    return out


def _unshard(g4, axis):
    a = jnp.moveaxis(g4, 0, axis)
    s = a.shape
    return a.reshape(s[:axis] + (s[axis] * s[axis + 1],) + s[axis + 2:])


def _to_shards(a, axis):
    s = a.shape
    return jnp.moveaxis(a.reshape(s[:axis] + (4, s[axis] // 4) + s[axis + 1:]), axis, 0)


def layer_weights(full, small, rep, layer):
    ng = rep["norm_g"][layer:layer + 1]
    side_by_side = lambda a4: jnp.moveaxis(a4, 0, 1).reshape(a4.shape[1], 4 * a4.shape[2])
    if layer in (0, 3):
        j = layer // 3
        return dict(ng=ng, w_in=full[("pool_w_in", j)], w_grp=full[("pool_w_grp", j)], scale=small["pool_scale"][j:j + 1],
                    w_out=full[("pool_w_out", j)])
    if layer == 1:
        return dict(ng=ng, w_in=_pad_cols(side_by_side(full[("gdn_w_in", 0)]), GDN_IN_PAD),
                    conv=jnp.pad(small["gdn_conv"][0], ((0, 4), (0, 0))), a_log=_pad_cols(rep["gdn_a_log"], LANES),
                    dt_bias=_pad_cols(rep["gdn_dt_bias"], LANES), norm_g=rep["gdn_norm_g"], w_out=full.get(("gdn_w_out", 0)))
    return dict(ng=ng, w_in=_mla_w_in_layout(side_by_side(full[("mla_w_in", 0)])), q_g=small["mla_q_norm_g"],
                kv_g=small["mla_kv_norm_g"], w_uq=_mla_w_uq_layout(full[("mla_w_uq", 0)]), w_ukv=full[("mla_w_ukv", 0)],
                w_out=full[("mla_w_out", 0)], rope=_rope_consts())


def big_grad_pieces(gl):
    g0, g1, g2, g3 = gl
    slots = lambda a: jnp.moveaxis(a.reshape(a.shape[0], 4, a.shape[1] // 4), 1, 0)
    out = {}
    for l, g in ((0, g0), (1, g3)):
        if g is not None:
            out.update({("pool_w_in", l): g["w_in"], ("pool_w_grp", l): g["w_grp"], ("pool_w_out", l): g["w_out"]})
    if g1 is not None:
        out.update({("gdn_w_in", 0): slots(g1["w_in"][:, :GDN_IN]), ("gdn_w_out", 0): g1["w_out"]})
    if g2 is not None:
        out.update({("mla_w_in", 0): slots(_mla_w_in_unlayout(g2["w_in"])), ("mla_w_uq", 0): _mla_w_uq_unlayout(g2["w_uq"]),
                    ("mla_w_ukv", 0): g2["w_ukv"], ("mla_w_out", 0): g2["w_out"]})
    return out


def small_grads(gl, dfinal):
    g0, g1, g2, g3 = gl
    return {"norm_g": jnp.concatenate([g0["ng"], g1["ng"], g2["ng"], g3["ng"]], axis=0),
            "pool_scale": jnp.concatenate([g0["scale"], g3["scale"]], axis=0), "gdn_conv": g1["conv"][None, :4],
            "gdn_a_log": g1["a_log"][:, :GDN_H], "gdn_dt_bias": g1["dt_bias"][:, :GDN_H], "gdn_norm_g": g1["norm_g"],
            "mla_q_norm_g": g2["q_g"], "mla_kv_norm_g": g2["kv_g"], "final_g": dfinal.reshape(D)}


NAMES = ["norm_g", "pool_w_in", "pool_w_grp", "pool_scale", "pool_w_out", "gdn_w_in", "gdn_conv", "gdn_a_log", "gdn_dt_bias",
         "gdn_norm_g", "gdn_w_out", "mla_w_in", "mla_q_norm_g", "mla_w_uq", "mla_kv_norm_g", "mla_w_ukv", "mla_w_out", "final_g"]


def kernel(x, positions, norm_g, pool_w_in, pool_w_grp, pool_scale, pool_w_out, gdn_w_in, gdn_conv, gdn_a_log, gdn_dt_bias, gdn_norm_g, gdn_w_out, mla_w_in, mla_q_norm_g, mla_w_uq, mla_kv_norm_g, mla_w_ukv, mla_w_out, final_g, loss_target, m_norm_g, m_pool_w_in, m_pool_w_grp, m_pool_scale, m_pool_w_out, m_gdn_w_in, m_gdn_conv, m_gdn_a_log, m_gdn_dt_bias, m_gdn_norm_g, m_gdn_w_out, m_mla_w_in, m_mla_q_norm_g, m_mla_w_uq, m_mla_kv_norm_g, m_mla_w_ukv, m_mla_w_out, m_final_g, v_norm_g, v_pool_w_in, v_pool_w_grp, v_pool_scale, v_pool_w_out, v_gdn_w_in, v_gdn_conv, v_gdn_a_log, v_gdn_dt_bias, v_gdn_norm_g, v_gdn_w_out, v_mla_w_in, v_mla_q_norm_g, v_mla_w_uq, v_mla_kv_norm_g, v_mla_w_ukv, v_mla_w_out, v_final_g):
    args = locals()
    w = {n: args[n] for n in NAMES}
    m = {n: args["m_" + n] for n in NAMES}
    v = {n: args["v_" + n] for n in NAMES}
    my_chip = (2 * lax.axis_index("x") + lax.axis_index("y")).astype(I32)
    S_ = x.shape[1]
    x0, pos, target = x[0], positions.reshape(S_, 1).astype(F32), loss_target[0]
    rep = {n: w[n] for n in REPLICATED}

    shard = {(n, l): w[n][l:l + 1].astype(BF16) for n in BIG for l in range(w[n].shape[0])}
    small_shapes = [w[n].shape for n in SMALL_SHARDED]
    shard[("small", 0)] = _pack([w[n] for n in SMALL_SHARDED], F32, 8)[None]
    layout = dict(BIG_LAYOUT, small=(None, None, (4,) + shard[("small", 0)].shape[1:]))

    def gather_start(group, after, tag):
        pieces = [gather_piece(i, 0, i, layout[n][0], layout[n][1]) for i, (n, l) in enumerate(group)]
        shapes = [_sds(layout[n][2], shard[(n, l)].dtype) for n, l in group]
        sems, ins, lands, token = exchange_start(pieces, [shard[k] for k in group], shapes, after, tag + "_start")
        return (pieces, sems, ins, lands), token

    def finish(handle, after, tag):
        return exchange_wait(*handle, after, tag + "_wait")

    def gathered(group, handle, after, tag):
        srcs, lands = finish(handle, after, tag)
        return {(n, l): place_own(a, s[0], layout[n][0], layout[n][1], my_chip) for (n, l), s, a in zip(group, srcs, lands)}

    group_a = [("small", 0), ("pool_w_in", 0)]
    group_a2 = [("pool_w_grp", 0), ("pool_w_out", 0)]
    group_b = [("gdn_w_in", 0)]
    group_c = [("gdn_w_out", 0), ("mla_w_in", 0), ("mla_w_uq", 0), ("mla_w_ukv", 0), ("mla_w_out", 0), ("pool_w_in", 1),
               ("pool_w_grp", 1), ("pool_w_out", 1)]
    full = {}
    h_a, t_a = gather_start(group_a, x0, "gather_a")
    h_a2, t_a2 = gather_start(group_a2, t_a, "gather_a2")
    h_b, t_b = gather_start(group_b, t_a2, "gather_b")
    h_c, t_c = gather_start(group_c, t_b, "gather_c")
    full.update(gathered(group_a, h_a, t_c, "gather_a"))
    small = {n: _unshard(a, SMALL_AXIS[n]) for n, a in zip(SMALL_SHARDED, _unpack(full[("small", 0)], small_shapes))}

    def late_l0(proj):
        full.update(gathered(group_a2, h_a2, proj, "gather_a2"))
        return dict(w_grp=full[("pool_w_grp", 0)], w_out=full[("pool_w_out", 0)])

    x1, s0 = pool_fwd(x0, dict(ng=rep["norm_g"][0:1], w_in=full[("pool_w_in", 0)], scale=small["pool_scale"][0:1]), "l0", late=late_l0)
    W0 = layer_weights(full, small, rep, 0)
    full.update(gathered(group_b, h_b, x1, "gather_b"))

    def late_l1(og):
        full.update(gathered(group_c, h_c, og, "gather_c"))
        return dict(w_out=full[("gdn_w_out", 0)])

    x2, s1 = gdn_fwd(x1, layer_weights(full, small, rep, 1), "l1", late=late_l1)
    W1, W2, W3 = (layer_weights(full, small, rep, i) for i in (1, 2, 3))
    x3, s2 = mla_fwd(x2, pos, W2, "l2")
    x4, s3 = pool_fwd(x3, W3, "l3")
    loss_part, dx4, dfinal = loss_head(x4, target, final_g.reshape(1, D), "loss_head")

    def scatter_start(pieces_of, after, tag):
        keys = list(pieces_of)
        pieces = [scatter_piece(i, i, BIG_LAYOUT[n][0], BIG_LAYOUT[n][1]) for i, (n, l) in enumerate(keys)]
        shapes = [_sds((4,) + tuple(w[n].shape[1:]), BF16) for n, l in keys]
        sems, ins, lands, token = exchange_start(pieces, [pieces_of[k] for k in keys], shapes, after, tag + "_start")
        return keys, (pieces, sems, ins, lands), token

    def scattered(keys, handle, after, tag):
        srcs, lands = finish(handle, after, tag)
        return {(n, l): place_own(a, own_window(g, BIG_LAYOUT[n][0], BIG_LAYOUT[n][1], my_chip), None, None, my_chip)
                for (n, l), g, a in zip(keys, srcs, lands)}

    dx3, g3 = pool_bwd(dx4, W3, s3, "l3")
    k3, h3, t3 = scatter_start(big_grad_pieces((None, None, None, g3)), dx3, "scatter_l3")
    dx2, g2 = mla_bwd(dx3, pos, W2, s2, "l2", after=t3)
    k2, h2, t2 = scatter_start(big_grad_pieces((None, None, g2, None)), dx2, "scatter_l2")
    dx1, g1 = gdn_bwd(dx2, W1, s1, "l1", after=t2)
    k1, h1, t1 = scatter_start(big_grad_pieces((None, g1, None, None)), dx1, "scatter_l1")
    def swap_start(part, tag):
        keys = list(part)
        ins = [part[k] for k in keys]
        pieces = [whole_piece(i) for i in range(len(keys))]
        sems, ins, lands, token = exchange_start(pieces, ins, [_sds(a.shape, a.dtype) for a in ins], ins[0], tag + "_start", sibling=True)
        swaps.append((keys, (pieces, sems, ins, lands), tag))
        return token

    last, swaps = [], []

    def emit_l0(grads):
        first = not last
        now = next(iter(grads.values()))
        early = [(k3, h3, "scatter_l3"), (k2, h2, "scatter_l2")] if first else [(k1, h1, "scatter_l1")]
        landed = {}
        for keys, handle, tag in early:
            landed.update(scattered(keys, handle, now, tag))
        swapping = swap_start(landed, "swap_a" if first else "swap_b")
        tag = "scatter_l0a" if first else "scatter_l0b"
        keys, handle, token = scatter_start({("pool_" + k, 0): a for k, a in grads.items()}, swapping, tag)
        last.append((keys, handle, tag))
        return token

    dx0, g0 = pool_bwd(dx1, W0, s0, "l0", after=t1, emit=emit_l0)
    landed = {}
    for keys, handle, tag in last:
        landed.update(scattered(keys, handle, dx0, tag))
    swap_start(landed, "swap_c")
    recv, sib = {}, {}
    for keys, handle, tag in swaps:
        mine_, theirs = exchange_wait(*handle, dx0, tag + "_wait", sibling=True)
        recv.update(zip(keys, mine_))
        sib.update(zip(keys, theirs))

    sg = small_grads((g0, g1, g2, g3), dfinal)
    small_names = SMALL_SHARDED + REPLICATED
    small_buf = _pack([sg[n] for n in small_names] + [loss_part], F32, 8)
    small_sum = sum_slots(exchange_all(small_buf, "gather_small"), "sum_small")
    full_small = _unpack(small_sum, [sg[n].shape for n in small_names] + [(1, LANES)])
    loss = full_small[-1][0, 0]
    small_part = {}
    for n, a in zip(small_names, full_small[:-1]):
        if n in SMALL_AXIS:
            a = lax.dynamic_index_in_dim(_to_shards(a, SMALL_AXIS[n]), my_chip, axis=0, keepdims=False)
        small_part[n] = a

    outs = []
    for n in NAMES:
        shp = w[n].shape
        two = (int(np.prod(shp[:-1])), shp[-1]) if len(shp) > 1 else (1, shp[0])
        if n in BIG_LAYOUT:
            layers = shp[0]
            rows = lambda a: a.reshape(4, two[0] // layers, two[1])
            parts = [[rows(recv[(n, l)]) for l in range(layers)], [rows(sib[(n, l)]) for l in range(layers)]]
        else:
            parts = [[small_part[n].reshape((1,) + two)]]
        res = adamw(w[n].reshape(two), parts, m[n].reshape(two), v[n].reshape(two), "adamw_" + n)
        outs.append([r.reshape(shp) for r in res])
    return (loss, dx0[None], *[o[0] for o in outs], *[o[1] for o in outs], *[o[2] for o in outs], *[o[3] for o in outs])
```

```python
import math

import jax
import jax.numpy as jnp
import numpy as np
from jax import lax
from jax.experimental import pallas as pl
from jax.experimental.pallas import tpu as pltpu

F32 = jnp.float32
BF16 = jnp.bfloat16
I32 = jnp.int32

D = 1024
EPS = 1e-6
POOL_WIDTH = 2048
POOL_GROUP = 512
GDN_H, GDN_DK, GDN_DV, GDN_C = 8, 128, 256, 64
GDN_QK, GDN_V, GDN_CONV_CH, GDN_IN = 1024, 2048, 4096, 6160
GDN_IN_PAD = 6272
MLA_H, MLA_NOPE, MLA_ROPE, MLA_V = 16, 128, 64, 128
MLA_Q_LORA, MLA_KV_LORA, MLA_WIDTH, MLA_IN = 768, 512, 2048, 3392
MLA_IN_PAD = 4096
MLA_SCALE = (MLA_NOPE + MLA_ROPE) ** -0.5
ROPE_THETA = 10000.0
ADAM_LR, ADAM_B1, ADAM_B2, ADAM_EPS, ADAM_WD, ADAM_STEP = 0.001, 0.9, 0.999, 1e-08, 0.01, 10

VMEM_LIMIT_V7X = 56 * 1024 * 1024
LANES = 128
MESH = pl.DeviceIdType.MESH


def _pc(body, **kw):
    return pl.pallas_call(body, **kw)


def _cparams(sem):
    return pltpu.CompilerParams(dimension_semantics=sem, vmem_limit_bytes=VMEM_LIMIT_V7X)


def _tile(n, cap):
    t = (cap // LANES) * LANES
    while t >= LANES:
        if n % t == 0:
            return t
        t -= LANES
    return n


def _sds(shape, dt):
    return jax.ShapeDtypeStruct(shape, dt)


def mm(a, b, *, ta=False, tb=False, add=None, after=None, out_dtype=F32, name):
    if ta:
        K, M = a.shape
    else:
        M, K = a.shape
    if tb:
        N, K2 = b.shape
    else:
        K2, N = b.shape
    assert K == K2, (a.shape, b.shape, ta, tb)
    tm, tn, tk = _tile(M, 1024), _tile(N, 1024), _tile(K, 1024)
    nk = K // tk
    a_spec = pl.BlockSpec((tk, tm), lambda i, j, k: (k, i)) if ta else pl.BlockSpec((tm, tk), lambda i, j, k: (i, k))
    b_spec = pl.BlockSpec((tn, tk), lambda i, j, k: (j, k)) if tb else pl.BlockSpec((tk, tn), lambda i, j, k: (k, j))
    o_spec = pl.BlockSpec((tm, tn), lambda i, j, k: (i, j))
    dn = (((0 if ta else 1,), (1 if tb else 0,)), ((), ()))
    has_add = add is not None

    def body(*refs):
        a_ref, b_ref = refs[0], refs[1]
        part = lax.dot_general(a_ref[...].astype(BF16), b_ref[...].astype(BF16), dn, preferred_element_type=F32)
        if nk == 1:
            refs[-1][...] = (part + refs[2][...] if has_add else part).astype(out_dtype)
            return
        o_ref, acc = refs[-2], refs[-1]
        k = pl.program_id(2)

        @pl.when(k == 0)
        def _():
            acc[...] = part

        @pl.when(k > 0)
        def _():
            acc[...] += part

        @pl.when(k == nk - 1)
        def _():
            r = acc[...]
            if has_add:
                r = r + refs[2][...]
            o_ref[...] = r.astype(out_dtype)

    ins = [a, b] + ([add] if has_add else []) + ([after] if after is not None else [])
    specs = [a_spec, b_spec] + ([o_spec] if has_add else []) + ([pl.BlockSpec(memory_space=pl.ANY)] if after is not None else [])
    return _pc(body, grid=(M // tm, N // tn, nk), in_specs=specs, out_specs=o_spec, out_shape=_sds((M, N), out_dtype),
               scratch_shapes=[pltpu.VMEM((tm, tn), F32)] if nk > 1 else [], compiler_params=_cparams(("parallel", "parallel", "arbitrary")),
               name=name)(*ins)


def gmm(kind, a, b, *, G, name, out_dtype=F32):
    S_ = a.shape[0]
    Ka = a.shape[1] // G
    if kind == "tn":
        N = b.shape[1] // G
        tk = _tile(S_, 512)
        nk = S_ // tk

        def body(a_ref, b_ref, o_ref, acc):
            k = pl.program_id(1)

            @pl.when(k == 0)
            def _():
                acc[...] = jnp.zeros_like(acc)

            acc[...] += lax.dot_general(a_ref[...].astype(BF16), b_ref[...].astype(BF16), (((0,), (0,)), ((), ())),
                                        preferred_element_type=F32)

            @pl.when(k == nk - 1)
            def _():
                o_ref[...] = acc[...].astype(out_dtype)

        return _pc(body, grid=(G, nk),
                   in_specs=[pl.BlockSpec((tk, Ka), lambda g, k: (k, g)), pl.BlockSpec((tk, N), lambda g, k: (k, g))],
                   out_specs=pl.BlockSpec((None, Ka, N), lambda g, k: (g, 0, 0)), out_shape=_sds((G, Ka, N), out_dtype),
                   scratch_shapes=[pltpu.VMEM((Ka, N), F32)], compiler_params=_cparams(("parallel", "arbitrary")), name=name)(a, b)
    N = b.shape[2] if kind == "nn" else b.shape[1]
    tm = _tile(S_, 1024)
    dn = (((1,), (0 if kind == "nn" else 1,)), ((), ()))

    def body(a_ref, b_ref, o_ref):
        o_ref[...] = lax.dot_general(a_ref[...].astype(BF16), b_ref[...].astype(BF16), dn, preferred_element_type=F32).astype(out_dtype)

    bshape = (None,) + tuple(b.shape[1:])
    return _pc(body, grid=(G, S_ // tm),
               in_specs=[pl.BlockSpec((tm, Ka), lambda g, i: (i, g)), pl.BlockSpec(bshape, lambda g, i: (g, 0, 0))],
               out_specs=pl.BlockSpec((tm, N), lambda g, i: (i, g)), out_shape=_sds((S_, G * N), out_dtype),
               compiler_params=_cparams(("parallel", "parallel")), name=name)(a, b)


def _rw_spec(ts, w, c, s):
    return pl.BlockSpec((ts, w), lambda j, i: (i, c + j * s))


def _rw_pspec(p, w, c, s):
    return pl.BlockSpec((p.shape[0], w), lambda j, i: (0, c + j * s))


def rowwise(f, tiles, params, outs, *, ncol=1, ts, name):
    S_ = tiles[0][0].shape[0]
    nin = len(tiles) + len(params)

    def body(*refs):
        res = f(pl.program_id(0), *[r[...].astype(F32) for r in refs[:nin]])
        for r, o in zip(refs[nin:], res):
            r[...] = o.astype(r.dtype)

    return _pc(body, grid=(ncol, S_ // ts),
               in_specs=[_rw_spec(ts, w, c, s) for (_, w, c, s) in tiles] + [_rw_pspec(*p) for p in params],
               out_specs=[_rw_spec(ts, w, 0, s) for (w, s, _) in outs],
               out_shape=[_sds((S_, w * (ncol if s else 1)), dt) for (w, s, dt) in outs],
               compiler_params=_cparams(("parallel", "parallel")), name=name)(*[t[0] for t in tiles], *[p[0] for p in params])


def rowwise_bwd(f, tiles, params, cots, *, need, adds=None, place=None, narrow=(), ncol=1, ts, name):
    S_ = tiles[0][0].shape[0]
    adds = adds or {}
    place = place or {}
    nt, npar, nc = len(tiles), len(params), len(cots)
    add_keys = sorted(adds)
    need_idx = [k for k in range(nt) if need[k]]
    into_keys = [k for k in need_idx if k in place and not isinstance(place[k][0], int)]
    n_extra = len(add_keys) + len(into_keys)

    def body(*refs):
        j, i = pl.program_id(0), pl.program_id(1)
        vals = [r[...].astype(F32) for r in refs[:nt + npar]]
        cvals = tuple(r[...].astype(F32) for r in refs[nt + npar:nt + npar + nc])
        add_refs = refs[nt + npar + nc:nt + npar + nc + len(add_keys)]
        out_refs = refs[nt + npar + nc + n_extra:]
        _, vjp = jax.vjp(lambda *v: tuple(f(j, *v)), *vals)
        grads = vjp(cvals)
        for n, k in enumerate(need_idx):
            g = grads[k]
            if k in adds:
                g = g + add_refs[add_keys.index(k)][...]
            out_refs[n][...] = g.astype(out_refs[n].dtype)
        for n in range(npar):
            ref = out_refs[len(need_idx) + n]
            first = (i == 0) if params[n][3] else jnp.logical_and(i == 0, j == 0)

            @pl.when(first)
            def _():
                ref[...] = jnp.zeros_like(ref)

            ref[...] += grads[nt + n]

    in_specs = ([_rw_spec(ts, w, c, s) for (_, w, c, s) in tiles] + [_rw_pspec(*p) for p in params]
                + [_rw_spec(ts, w, c, s) for (_, w, c, s) in cots] + [_rw_spec(ts, *adds[k][1:]) for k in add_keys]
                + [pl.BlockSpec(memory_space=pl.ANY) for _ in into_keys])
    out_specs, out_shape, aliases = [], [], {}
    for n, k in enumerate(need_idx):
        w, s = tiles[k][1], tiles[k][3]
        if k in place:
            dst, c0 = place[k]
            total = dst if isinstance(dst, int) else dst.shape[1]
            out_specs.append(_rw_spec(ts, w, c0, s))
            out_shape.append(_sds((S_, total), (BF16 if k in narrow else F32) if isinstance(dst, int) else dst.dtype))
            if k in into_keys:
                aliases[nt + npar + nc + len(add_keys) + into_keys.index(k)] = n
        else:
            out_specs.append(_rw_spec(ts, w, 0, s))
            out_shape.append(_sds((S_, w * (ncol if s else 1)), BF16 if k in narrow else F32))
    out_specs += [_rw_pspec(p[0], p[1], p[2], p[3]) for p in params]
    out_shape += [_sds(p[0].shape, F32) for p in params]
    res = _pc(body, grid=(ncol, S_ // ts), in_specs=in_specs, out_specs=out_specs, out_shape=out_shape,
              input_output_aliases=aliases, compiler_params=_cparams(("arbitrary", "arbitrary")), name=name)(
        *[t[0] for t in tiles], *[p[0] for p in params], *[c[0] for c in cots], *[adds[k][0] for k in add_keys],
        *[place[k][0] for k in into_keys])
    return list(res[:len(need_idx)]), list(res[len(need_idx):])


def _rms(x, g):
    r = lax.rsqrt(jnp.mean(x * x, axis=-1, keepdims=True) + EPS)
    return x * r * g


def _silu(x):
    return x * jax.nn.sigmoid(x)


@jax.custom_vjp
def _softplus(x):
    return jnp.maximum(x, 0.0) + jnp.log1p(jnp.exp(-jnp.abs(x)))


_softplus.defvjp(lambda x: (_softplus(x), x), lambda x, d: (d * jax.nn.sigmoid(x),))


def f_rms(j, x, g):
    return (_rms(x, g),)


def f_pool_gate(j, pg, gate, scale):
    return (pg * scale * _silu(gate),)


def f_ogate(j, o, gate):
    return (o * _silu(gate),)


def f_gdn_out(j, o, gate, g):
    return (_rms(o, g) * _silu(gate),)


def f_gdn_gates(j, ba, alog, dtb):
    lane = lax.broadcasted_iota(I32, (1, LANES), 1)
    gs, bs = [], []
    for h in range(GDN_H):
        eb = (lane == h).astype(F32)
        ea = (lane == GDN_H + h).astype(F32)
        b = jnp.sum(ba * eb, -1, keepdims=True)
        a = jnp.sum(ba * ea, -1, keepdims=True)
        al = jnp.sum(alog * eb, -1, keepdims=True)
        dt = jnp.sum(dtb * eb, -1, keepdims=True)
        g = -jnp.exp(al) * _softplus(a + dt)
        gs.append(jnp.broadcast_to(g, ba.shape))
        bs.append(jnp.broadcast_to(jax.nn.sigmoid(b), ba.shape))
    return jnp.concatenate(gs, 1), jnp.concatenate(bs, 1)


def _shift_dn(x, k):
    rows = lax.broadcasted_iota(I32, x.shape, 0)
    return jnp.where(rows < k, 0.0, pltpu.roll(x, k, 0))


def _shift_up(x, k):
    n = x.shape[0]
    rows = lax.broadcasted_iota(I32, x.shape, 0)
    return jnp.where(rows >= n - k, 0.0, pltpu.roll(x, n - k, 0))


def _pool_window(j):
    g = lax.div(j, POOL_GROUP // LANES)
    return jnp.where(g == 0, 2.0, jnp.where(g == 1, 4.0, jnp.where(g == 2, 8.0, 16.0))), g


def _pick(g, a2, a4, a8, a16):
    return jnp.where(g == 0, a2, jnp.where(g == 1, a4, jnp.where(g == 2, a8, a16)))


def pool_time_fwd(proj, name):
    S_ = proj.shape[0]

    def body(u_ref, p_ref):
        u = u_ref[...].astype(F32)
        w, g = _pool_window(pl.program_id(0))
        s2 = u + _shift_dn(u, 1)
        s4 = s2 + _shift_dn(s2, 2)
        s8 = s4 + _shift_dn(s4, 4)
        s16 = s8 + _shift_dn(s8, 8)
        t1 = (lax.broadcasted_iota(I32, u.shape, 0) + 1).astype(F32)
        p_ref[...] = (_pick(g, s2, s4, s8, s16) / jnp.minimum(t1, w) - u).astype(p_ref.dtype)

    return _pc(body, grid=(POOL_WIDTH // LANES,), in_specs=[pl.BlockSpec((S_, LANES), lambda j: (0, j))],
               out_specs=pl.BlockSpec((S_, LANES), lambda j: (0, j)), out_shape=_sds((S_, POOL_WIDTH), BF16),
               compiler_params=_cparams(("parallel",)), name=name)(proj)


def pool_time_bwd(dp, into, name):
    S_ = dp.shape[0]

    def body(dp_ref, _, du_ref):
        d = dp_ref[...].astype(F32)
        w, g = _pool_window(pl.program_id(0))
        t1 = (lax.broadcasted_iota(I32, d.shape, 0) + 1).astype(F32)
        q = d / jnp.minimum(t1, w)
        r2 = q + _shift_up(q, 1)
        r4 = r2 + _shift_up(r2, 2)
        r8 = r4 + _shift_up(r4, 4)
        r16 = r8 + _shift_up(r8, 8)
        du_ref[...] = (_pick(g, r2, r4, r8, r16) - d).astype(du_ref.dtype)

    return _pc(body, grid=(POOL_WIDTH // LANES,),
               in_specs=[pl.BlockSpec((S_, LANES), lambda j: (0, j)), pl.BlockSpec(memory_space=pl.ANY)],
               out_specs=pl.BlockSpec((S_, LANES), lambda j: (0, j)), out_shape=_sds(into.shape, into.dtype),
               input_output_aliases={1: 0}, compiler_params=_cparams(("parallel",)), name=name)(dp, into)


def _conv_post(j, a):
    n = a * lax.rsqrt(jnp.sum(a * a, axis=-1, keepdims=True) + EPS)
    nq = GDN_QK // LANES
    return jnp.where(j < nq, n * (GDN_DK ** -0.5), jnp.where(j < 2 * nq, n, a))


def _conv_taps(u):
    return [_shift_dn(u, 3), _shift_dn(u, 2), _shift_dn(u, 1), u]


def _conv_pre(taps, w):
    return w[0:1] * taps[0] + w[1:2] * taps[1] + w[2:3] * taps[2] + w[3:4] * taps[3]


def gdn_conv_fwd(proj, conv_w, name):
    S_ = proj.shape[0]

    def body(u_ref, w_ref, o_ref):
        o_ref[...] = _conv_post(pl.program_id(0), _silu(_conv_pre(_conv_taps(u_ref[...]), w_ref[...])))

    return _pc(body, grid=(GDN_CONV_CH // LANES,),
               in_specs=[pl.BlockSpec((S_, LANES), lambda j: (0, j)), pl.BlockSpec((8, LANES), lambda j: (0, j))],
               out_specs=pl.BlockSpec((S_, LANES), lambda j: (0, j)), out_shape=_sds((S_, GDN_CONV_CH), F32),
               compiler_params=_cparams(("parallel",)), name=name)(proj, conv_w)


def gdn_conv_bwd(proj, conv_w, dq, dk, dv, into, name):
    S_ = proj.shape[0]
    nq = GDN_QK // LANES

    def body(u_ref, w_ref, dq_ref, dk_ref, dv_ref, _, du_ref, dw_ref):
        j = pl.program_id(0)
        u, w = u_ref[...], w_ref[...]
        taps = _conv_taps(u)
        c = _conv_pre(taps, w)
        sig = jax.nn.sigmoid(c)
        dout = jnp.where(j < nq, dq_ref[...], jnp.where(j < 2 * nq, dk_ref[...], dv_ref[...]))
        _, vjp = jax.vjp(lambda a: _conv_post(j, a), c * sig)
        dc = vjp(dout)[0] * (sig * (1.0 + c * (1.0 - sig)))
        du = w[3:4] * dc + w[2:3] * _shift_up(dc, 1) + w[1:2] * _shift_up(dc, 2) + w[0:1] * _shift_up(dc, 3)
        du_ref[...] = du.astype(du_ref.dtype)
        rows = lax.broadcasted_iota(I32, (8, LANES), 0)
        dw = jnp.zeros((8, LANES), F32)
        for k in range(4):
            dw = dw + jnp.where(rows == k, jnp.sum(dc * taps[k], axis=0, keepdims=True), 0.0)
        dw_ref[...] = dw

    blk = lambda f: pl.BlockSpec((S_, LANES), f)
    return _pc(body, grid=(GDN_CONV_CH // LANES,),
               in_specs=[blk(lambda j: (0, j)), pl.BlockSpec((8, LANES), lambda j: (0, j)),
                         blk(lambda j: (0, jnp.minimum(j, nq - 1))), blk(lambda j: (0, jnp.clip(j - nq, 0, nq - 1))),
                         blk(lambda j: (0, jnp.clip(j - 2 * nq, 0, 2 * nq - 1))), pl.BlockSpec(memory_space=pl.ANY)],
               out_specs=[blk(lambda j: (0, j)), pl.BlockSpec((8, LANES), lambda j: (0, j))],
               out_shape=[_sds(into.shape, into.dtype), _sds((8, GDN_CONV_CH), F32)], input_output_aliases={5: 0},
               compiler_params=_cparams(("parallel",)), name=name)(proj, conv_w, dq, dk, dv, into)


_NN, _NT, _TN = ((1,), (0,)), ((1,), (1,)), ((0,), (0,))


def _split(x, n):
    parts = []
    for _ in range(n):
        h = x.astype(BF16)
        parts.append(h)
        x = x - h.astype(F32)
    return parts


def _dot(a, b, dn, mode):
    d = lambda p, q: lax.dot_general(p, q, (dn, ((), ())), preferred_element_type=F32)
    if mode == "lo":
        return d(a.astype(BF16), b.astype(BF16))
    if mode == "x3":
        (ah, al), (bh, bl) = _split(a, 2), _split(b, 2)
        return d(ah, bh) + (d(ah, bl) + d(al, bh))
    b0, b1, b2 = _split(b, 3)
    ab = a.astype(BF16)
    return d(ab, b0) + (d(ab, b1) + d(ab, b2))


def _make_dots(mode):
    @jax.custom_vjp
    def nn(a, b):
        return _dot(a, b, _NN, mode)

    @jax.custom_vjp
    def nt(a, b):
        return _dot(a, b, _NT, mode)

    @jax.custom_vjp
    def tn(a, b):
        return _dot(a, b, _TN, mode)

    nn.defvjp(lambda a, b: (nn(a, b), (a, b)), lambda r, d: (nt(d, r[1]), tn(r[0], d)))
    nt.defvjp(lambda a, b: (nt(a, b), (a, b)), lambda r, d: (nn(d, r[1]), tn(d, r[0])))
    tn.defvjp(lambda a, b: (tn(a, b), (a, b)), lambda r, d: (nt(r[1], d), nn(r[0], d)))
    return nn, nt, tn


_nn_hi, _nt_hi, _tn_hi = _make_dots("x3")
_nn_lo, _nt_lo, _tn_lo = _make_dots("lo")


@jax.custom_vjp
def _nn_const(a, b):
    return _dot(a, b, _NN, "xl")


_nn_const.defvjp(lambda a, b: (_nn_const(a, b), a), lambda a, d: (jnp.zeros_like(a), _dot(a, d, _TN, "xl")))


def _each(f, *lists):
    return [f(*xs) for xs in zip(*lists)]


@jax.custom_vjp
def _unit_inverses(xs):
    C = xs[0].shape[0]
    eye = (lax.broadcasted_iota(I32, (C, C), 0) == lax.broadcasted_iota(I32, (C, C), 1)).astype(F32)
    ainv, p = [eye + a for a in xs], xs
    for _ in range(int(math.log2(C)) - 1):
        p = _each(lambda a: _dot(a, a, _NN, "x3"), p)
        ainv = _each(lambda a, b: a + _dot(a, b, _NN, "x3"), ainv, p)
    return ainv


def _unit_inverses_bwd(ainv, d):
    left = _each(lambda a, g: _dot(a, g, _TN, "x3"), ainv, d)
    return (_each(lambda t, a: _dot(t, a, _NT, "x3"), left, ainv),)


_unit_inverses.defvjp(lambda xs: (lambda a: (a, a))(_unit_inverses(xs)), _unit_inverses_bwd)


def _gdn_chunk(q, k, v, gb, bb, state):
    C = GDN_C
    e0 = (lax.broadcasted_iota(I32, (1, LANES), 1) == 0).astype(F32)
    ri = lax.broadcasted_iota(I32, (C, C), 0)
    ci = lax.broadcasted_iota(I32, (C, C), 1)
    causal, strict = ri >= ci, ri > ci
    tri, eye, ones = causal.astype(F32), (ri == ci).astype(F32), jnp.ones((C, C), F32)
    last = lax.broadcasted_iota(I32, (C, LANES), 0) == C - 1
    g1 = _each(lambda a: jnp.sum(a * e0, -1, keepdims=True), gb)
    b1 = _each(lambda a: jnp.sum(a * e0, -1, keepdims=True), bb)
    gc_c = _each(lambda g: _nn_const(tri, jnp.broadcast_to(g, (C, C))), g1)
    gc_d = _each(lambda g: _nn_const(tri, jnp.broadcast_to(g, (C, LANES))), g1)
    gr_c = _each(lambda g: _nn_const(ones, eye * g), gc_c)
    decay = _each(lambda a, r: jnp.where(causal, jnp.exp(jnp.where(causal, a - r, 0.0)), 0.0), gc_c, gr_c)
    kb = _each(lambda a, b: a * b, k, b1)
    vb = _each(lambda a, b: a * b, v, b1)
    x = _each(lambda a, b, d: -jnp.where(strict, _nt_lo(a, b) * d, 0.0), kb, k, decay)
    ainv = _unit_inverses(x)
    u = _each(_nn_hi, ainv, vb)
    w = _each(lambda a, b, g: _nn_hi(a, b * jnp.exp(g)), ainv, kb, gc_d)
    attn = _each(lambda a, b, d: jnp.where(causal, _nt_lo(a, b) * d, 0.0), q, k, decay)
    v_new = _each(lambda a, b, s: a - _nn_lo(b, s), u, w, state)
    o = _each(lambda a, g, s, t, vn: _nn_lo(a * jnp.exp(g), s) + _nn_lo(t, vn), q, gc_d, state, attn, v_new)
    gl = _each(lambda g: jnp.sum(jnp.where(last, g, 0.0), axis=0, keepdims=True), gc_d)
    new_state = _each(lambda s, g, a, gd, vn: s * jnp.exp(jnp.sum(g * e0, -1, keepdims=True)) + _tn_lo(a * jnp.exp(g - gd), vn),
                      state, gl, k, gc_d, v_new)
    return o, new_state


def _head_slices(ref, width):
    return [ref[:, h * width:(h + 1) * width] for h in range(GDN_H)]


def gdn_chunk_fwd(qkv, g_b, beta_b, name):
    S_ = qkv.shape[0]
    N = S_ // GDN_C

    def body(q_ref, k_ref, v_ref, g_ref, b_ref, o_ref, s_ref, state):
        @pl.when(pl.program_id(0) == 0)
        def _():
            state[...] = jnp.zeros_like(state)

        st = [state[h] for h in range(GDN_H)]
        s_ref[0] = state[...]
        o, st2 = _gdn_chunk(_head_slices(q_ref, GDN_DK), _head_slices(k_ref, GDN_DK), _head_slices(v_ref, GDN_DV),
                            _head_slices(g_ref, GDN_DK), _head_slices(b_ref, GDN_DK), st)
        for h in range(GDN_H):
            o_ref[:, h * GDN_DV:(h + 1) * GDN_DV] = o[h]
            state[h] = st2[h]

    return _pc(body, grid=(N,),
               in_specs=[pl.BlockSpec((GDN_C, GDN_QK), lambda n: (n, 0)), pl.BlockSpec((GDN_C, GDN_QK), lambda n: (n, 1)),
                         pl.BlockSpec((GDN_C, GDN_V), lambda n: (n, 1)), pl.BlockSpec((GDN_C, GDN_QK), lambda n: (n, 0)),
                         pl.BlockSpec((GDN_C, GDN_QK), lambda n: (n, 0))],
               out_specs=[pl.BlockSpec((GDN_C, GDN_V), lambda n: (n, 0)),
                          pl.BlockSpec((1, GDN_H, GDN_DK, GDN_DV), lambda n: (n, 0, 0, 0))],
               out_shape=[_sds((S_, GDN_V), F32), _sds((N, GDN_H, GDN_DK, GDN_DV), F32)],
               scratch_shapes=[pltpu.VMEM((GDN_H, GDN_DK, GDN_DV), F32)],
               compiler_params=_cparams(("arbitrary",)), name=name)(qkv, qkv, qkv, g_b, beta_b)


def gdn_chunk_bwd(qkv, g_b, beta_b, states, do, name):
    S_ = qkv.shape[0]
    N = S_ // GDN_C

    def body(q_ref, k_ref, v_ref, g_ref, b_ref, s_ref, do_ref, dq_ref, dk_ref, dv_ref, dg_ref, db_ref, dstate):
        @pl.when(pl.program_id(0) == 0)
        def _():
            dstate[...] = jnp.zeros_like(dstate)

        _, vjp = jax.vjp(_gdn_chunk, _head_slices(q_ref, GDN_DK), _head_slices(k_ref, GDN_DK), _head_slices(v_ref, GDN_DV),
                         _head_slices(g_ref, GDN_DK), _head_slices(b_ref, GDN_DK), [s_ref[0, h] for h in range(GDN_H)])
        dq, dk, dv, dg, db, ds = vjp((_head_slices(do_ref, GDN_DV), [dstate[h] for h in range(GDN_H)]))
        for h in range(GDN_H):
            kk, vv = slice(h * GDN_DK, (h + 1) * GDN_DK), slice(h * GDN_DV, (h + 1) * GDN_DV)
            dq_ref[:, kk] = dq[h]
            dk_ref[:, kk] = dk[h]
            dv_ref[:, vv] = dv[h]
            dg_ref[:, kk] = dg[h]
            db_ref[:, kk] = db[h]
            dstate[h] = ds[h]

    r = lambda n: N - 1 - n
    qk = lambda c: pl.BlockSpec((GDN_C, GDN_QK), lambda n: (r(n), c))
    vs = lambda c: pl.BlockSpec((GDN_C, GDN_V), lambda n: (r(n), c))
    return _pc(body, grid=(N,),
               in_specs=[qk(0), qk(1), vs(1), qk(0), qk(0),
                         pl.BlockSpec((1, GDN_H, GDN_DK, GDN_DV), lambda n: (r(n), 0, 0, 0)), vs(0)],
               out_specs=[qk(0), qk(0), vs(0), qk(0), qk(0)],
               out_shape=[_sds((S_, GDN_QK), F32), _sds((S_, GDN_QK), F32), _sds((S_, GDN_V), F32),
                          _sds((S_, GDN_QK), F32), _sds((S_, GDN_QK), F32)],
               scratch_shapes=[pltpu.VMEM((GDN_H, GDN_DK, GDN_DV), F32)],
               compiler_params=_cparams(("arbitrary",)), name=name)(qkv, qkv, qkv, g_b, beta_b, states, do)


def _rope_tables(pos_ref, inv_ref, cm_ref, sg_ref):
    ang = pos_ref[...] * inv_ref[...]
    return jnp.cos(ang) * cm_ref[...], jnp.sin(ang) * sg_ref[...]


def mla_prep_fwd(qpad, kv, proj, pos, rope_consts, name):
    S_ = qpad.shape[0]
    ts = 256
    W = 2 * LANES

    def body(q_ref, kv_ref, kr_ref, pos_ref, inv_ref, cm_ref, sg_ref, qh_ref, kh_ref, vh_ref):
        cs, sn = _rope_tables(pos_ref, inv_ref, cm_ref, sg_ref)
        rope = lambda r: r * cs + pltpu.roll(r, LANES // 2, 1) * sn
        krr = rope(kr_ref[...].astype(F32)).astype(BF16)
        for h in range(MLA_H):
            qh_ref[h, :, 0:LANES] = (q_ref[:, h * W:h * W + LANES].astype(F32) * MLA_SCALE).astype(BF16)
            qh_ref[h, :, LANES:W] = (rope(q_ref[:, h * W + LANES:(h + 1) * W].astype(F32)) * MLA_SCALE).astype(BF16)
            kh_ref[h, :, 0:LANES] = kv_ref[:, h * W:h * W + LANES].astype(BF16)
            kh_ref[h, :, LANES:W] = krr
            vh_ref[h] = kv_ref[:, h * W + LANES:(h + 1) * W].astype(BF16)

    one = pl.BlockSpec((1, LANES), lambda i: (0, 0))
    return _pc(body, grid=(S_ // ts,),
               in_specs=[pl.BlockSpec((ts, MLA_H * W), lambda i: (i, 0)), pl.BlockSpec((ts, MLA_H * W), lambda i: (i, 0)),
                         pl.BlockSpec((ts, LANES), lambda i: (i, 1536 // LANES)), pl.BlockSpec((ts, 1), lambda i: (i, 0)),
                         one, one, one],
               out_specs=[pl.BlockSpec((MLA_H, ts, W), lambda i: (0, i, 0)), pl.BlockSpec((MLA_H, ts, W), lambda i: (0, i, 0)),
                          pl.BlockSpec((MLA_H, ts, LANES), lambda i: (0, i, 0))],
               out_shape=[_sds((MLA_H, S_, W), BF16), _sds((MLA_H, S_, W), BF16), _sds((MLA_H, S_, LANES), BF16)],
               compiler_params=_cparams(("parallel",)), name=name)(qpad, kv, proj, pos, *rope_consts)


def mla_prep_bwd(dqh, dkh, dvh, pos, rope_consts, into, name):
    S_ = dqh.shape[1]
    ts = 256
    W = 2 * LANES

    def body(dq_ref, dk_ref, dv_ref, pos_ref, inv_ref, cm_ref, sg_ref, _, dqp_ref, dkv_ref, dkr_ref):
        cs, sn = _rope_tables(pos_ref, inv_ref, cm_ref, sg_ref)
        rope_t = lambda g: g * cs + pltpu.roll(g * sn, LANES // 2, 1)
        acc = jnp.zeros((ts, LANES), F32)
        for h in range(MLA_H):
            dqp_ref[:, h * W:h * W + LANES] = (dq_ref[h, :, 0:LANES].astype(F32) * MLA_SCALE).astype(BF16)
            dqp_ref[:, h * W + LANES:(h + 1) * W] = (rope_t(dq_ref[h, :, LANES:W].astype(F32)) * MLA_SCALE).astype(BF16)
            dkv_ref[:, h * W:h * W + LANES] = dk_ref[h, :, 0:LANES]
            dkv_ref[:, h * W + LANES:(h + 1) * W] = dv_ref[h]
            acc = acc + dk_ref[h, :, LANES:W].astype(F32)
        dkr_ref[...] = rope_t(acc).astype(dkr_ref.dtype)

    one = pl.BlockSpec((1, LANES), lambda i: (0, 0))
    return _pc(body, grid=(S_ // ts,),
               in_specs=[pl.BlockSpec((MLA_H, ts, W), lambda i: (0, i, 0)), pl.BlockSpec((MLA_H, ts, W), lambda i: (0, i, 0)),
                         pl.BlockSpec((MLA_H, ts, LANES), lambda i: (0, i, 0)), pl.BlockSpec((ts, 1), lambda i: (i, 0)),
                         one, one, one, pl.BlockSpec(memory_space=pl.ANY)],
               out_specs=[pl.BlockSpec((ts, MLA_H * W), lambda i: (i, 0)), pl.BlockSpec((ts, MLA_H * W), lambda i: (i, 0)),
                          pl.BlockSpec((ts, LANES), lambda i: (i, 1536 // LANES))],
               out_shape=[_sds((S_, MLA_H * W), BF16), _sds((S_, MLA_H * W), BF16), _sds(into.shape, into.dtype)],
               input_output_aliases={7: 2}, compiler_params=_cparams(("parallel",)), name=name)(dqh, dkh, dvh, pos, *rope_consts, into)


NEG = -1e30


FLASH_TILE = 1024
FLASH_SUB = 512


def _scores(q, k, diagonal):
    s = lax.dot_general(q, k, (_NT, ((), ())), preferred_element_type=F32)
    if not diagonal:
        return s
    return jnp.where(lax.broadcasted_iota(I32, s.shape, 1) <= lax.broadcasted_iota(I32, s.shape, 0), s, NEG)


def _sub_blocks(t, diagonal):
    sub = min(FLASH_SUB, t) if diagonal else t
    return [(c * sub if diagonal else 0, slice(c * sub, (c + 1) * sub)) for c in range(t // sub)]


FLASH_HEADS = 2


def flash_fwd(qh, kh, vh, name):
    H, S_, W = qh.shape
    t = _tile(S_, FLASH_TILE)
    n = S_ // t
    G = FLASH_HEADS
    heads = list(range(G))

    def body(q_ref, k_ref, v_ref, o_ref, lse_ref, m_s, l_s, acc):
        qi, kj = pl.program_id(1), pl.program_id(2)

        @pl.when(kj == 0)
        def _():
            m_s[...] = jnp.full_like(m_s, NEG)
            l_s[...] = jnp.zeros_like(l_s)
            acc[...] = jnp.zeros_like(acc)

        def step(diagonal):
            s = _each(lambda a: _scores(q_ref[a], k_ref[a], diagonal), heads)
            m_old = _each(lambda a: m_s[a], heads)
            m_new = _each(lambda mo, sa: jnp.maximum(mo, jnp.max(sa, axis=-1, keepdims=True)), m_old, s)
            alpha = _each(lambda mo, mn: jnp.exp(mo - mn), m_old, m_new)
            p = _each(lambda sa, mn: jnp.exp(sa - mn[:, :1]), s, m_new)
            pv = _each(lambda pa, a: lax.dot_general(pa.astype(BF16), v_ref[a], (_NN, ((), ())), preferred_element_type=F32), p, heads)
            for a in heads:
                l_s[a] = alpha[a] * l_s[a] + jnp.sum(p[a], axis=-1, keepdims=True)
                acc[a] = alpha[a] * acc[a] + pv[a]
                m_s[a] = m_new[a]

        pl.when(kj < qi)(lambda: step(False))
        pl.when(kj == qi)(lambda: step(True))

        @pl.when(kj == n - 1)
        def _():
            for a in heads:
                o_ref[:, a * LANES:(a + 1) * LANES] = (acc[a] / l_s[a]).astype(o_ref.dtype)
                lse_ref[a] = m_s[a] + jnp.log(l_s[a])

    return _pc(body, grid=(H // G, n, n),
               in_specs=[pl.BlockSpec((G, t, W), lambda h, i, j: (h, i, 0)),
                         pl.BlockSpec((G, t, W), lambda h, i, j: (h, jnp.minimum(i, j), 0)),
                         pl.BlockSpec((G, t, LANES), lambda h, i, j: (h, jnp.minimum(i, j), 0))],
               out_specs=[pl.BlockSpec((t, G * LANES), lambda h, i, j: (i, h)), pl.BlockSpec((G, t, LANES), lambda h, i, j: (h, i, 0))],
               out_shape=[_sds((S_, H * LANES), BF16), _sds((H, S_, LANES), F32)],
               scratch_shapes=[pltpu.VMEM((G, t, LANES), F32)] * 3,
               compiler_params=_cparams(("parallel", "parallel", "arbitrary")), name=name)(qh, kh, vh)


def flash_bwd(qh, kh, vh, o, lse, do, name):
    H, S_, W = qh.shape
    t = _tile(S_, FLASH_TILE)
    n = S_ // t

    def body(q_ref, k_ref, v_ref, o_ref, lse_ref, do_ref, dq_ref, dk_ref, dv_ref, dq_acc, dk_acc, dv_acc):
        kj, qi = pl.program_id(1), pl.program_id(2)

        @pl.when(jnp.logical_and(kj == 0, qi == 0))
        def _():
            dq_acc[...] = jnp.zeros_like(dq_acc)

        @pl.when(qi == 0)
        def _():
            dk_acc[...] = jnp.zeros_like(dk_acc)
            dv_acc[...] = jnp.zeros_like(dv_acc)

        def step(diagonal):
            do_ = do_ref[...]
            dob = do_.astype(BF16)
            delta = jnp.sum(do_.astype(F32) * o_ref[...].astype(F32), axis=-1, keepdims=True)
            for r0, keys in _sub_blocks(t, diagonal):
                q, k, v = q_ref[r0:, :], k_ref[keys, :], v_ref[keys, :]
                p = jnp.exp(_scores(q, k, diagonal) - lse_ref[r0:, :1])
                dv_acc[keys, :] += lax.dot_general(p.astype(BF16), dob[r0:], (_TN, ((), ())), preferred_element_type=F32)
                dp = lax.dot_general(dob[r0:], v, (_NT, ((), ())), preferred_element_type=F32)
                ds = (p * (dp - delta[r0:])).astype(BF16)
                dk_acc[keys, :] += lax.dot_general(ds, q, (_TN, ((), ())), preferred_element_type=F32)
                rows = pl.ds(pl.multiple_of(qi * t, t) + r0, t - r0)
                dq_acc[rows, :] += lax.dot_general(ds, k, (_NN, ((), ())), preferred_element_type=F32)

        pl.when(qi > kj)(lambda: step(False))
        pl.when(qi == kj)(lambda: step(True))

        @pl.when(qi == n - 1)
        def _():
            dk_ref[...] = dk_acc[...].astype(BF16)
            dv_ref[...] = dv_acc[...].astype(BF16)

        @pl.when(jnp.logical_and(kj == n - 1, qi == n - 1))
        def _():
            dq_ref[...] = dq_acc[...].astype(BF16)

    qrow = lambda h, j, i: jnp.maximum(i, j)
    return _pc(body, grid=(H, n, n),
               in_specs=[pl.BlockSpec((None, t, W), lambda h, j, i: (h, qrow(h, j, i), 0)),
                         pl.BlockSpec((None, t, W), lambda h, j, i: (h, j, 0)),
                         pl.BlockSpec((None, t, LANES), lambda h, j, i: (h, j, 0)),
                         pl.BlockSpec((t, LANES), lambda h, j, i: (qrow(h, j, i), h)),
                         pl.BlockSpec((None, t, LANES), lambda h, j, i: (h, qrow(h, j, i), 0)),
                         pl.BlockSpec((t, LANES), lambda h, j, i: (qrow(h, j, i), h))],
               out_specs=[pl.BlockSpec((None, S_, W), lambda h, j, i: (h, 0, 0)),
                          pl.BlockSpec((None, t, W), lambda h, j, i: (h, j, 0)),
                          pl.BlockSpec((None, t, LANES), lambda h, j, i: (h, j, 0))],
               out_shape=[_sds((H, S_, W), BF16), _sds((H, S_, W), BF16), _sds((H, S_, LANES), BF16)],
               scratch_shapes=[pltpu.VMEM((S_, W), F32), pltpu.VMEM((t, W), F32), pltpu.VMEM((t, LANES), F32)],
               compiler_params=_cparams(("parallel", "arbitrary", "arbitrary")), name=name)(qh, kh, vh, o, lse, do)


def loss_head(x, target, g, name):
    S_ = x.shape[0]
    ts = 256

    def body(x_ref, t_ref, g_ref, l_ref, dx_ref, dg_ref):
        @pl.when(pl.program_id(0) == 0)
        def _():
            l_ref[...] = jnp.zeros_like(l_ref)
            dg_ref[...] = jnp.zeros_like(dg_ref)

        y, vjp = jax.vjp(_rms, x_ref[...], g_ref[...])
        err = y - t_ref[...]
        l_ref[...] += 0.5 * jnp.sum(jnp.sum(err * err, axis=-1, keepdims=True), axis=0, keepdims=True) / D
        dx, dg = vjp(err / D)
        dx_ref[...] = dx
        dg_ref[...] += dg

    row = pl.BlockSpec((ts, D), lambda i: (i, 0))
    return _pc(body, grid=(S_ // ts,), in_specs=[row, row, pl.BlockSpec((1, D), lambda i: (0, 0))],
               out_specs=[pl.BlockSpec((1, LANES), lambda i: (0, 0)), row, pl.BlockSpec((1, D), lambda i: (0, 0))],
               out_shape=[_sds((1, LANES), F32), _sds((S_, D), F32), _sds((1, D), F32)],
               compiler_params=_cparams(("arbitrary",)), name=name)(x, target, g)


def adamw(w, parts, m, v, name):
    R, C = w.shape
    rows = [p.shape[1] for p in parts[0]]
    tr = R
    for cand in (512, 256, 128, 64, 32, 16, 8):
        if all(r % cand == 0 for r in rows) and cand * C * 4 <= 1024 * 1024:
            tr = cand
            break
    c1 = 1.0 - ADAM_B1 ** ADAM_STEP
    c2 = 1.0 - ADAM_B2 ** ADAM_STEP
    starts = [sum(rows[:k]) // tr for k in range(len(rows))]
    flat = [p for part in parts for p in part]

    def body(*refs):
        w_ref, m_ref, v_ref = refs[0], refs[1 + len(flat)], refs[2 + len(flat)]
        g_ref, d_ref, nm_ref, nv_ref = refs[3 + len(flat):]
        i = pl.program_id(0)
        gg, at = None, 1
        for part in parts:
            val = None
            for k in range(len(part)):
                p_ref = refs[at]
                at += 1
                s = p_ref[0].astype(F32)
                for n in range(1, p_ref.shape[0]):
                    s = s + p_ref[n].astype(F32)
                val = s if val is None else jnp.where(i >= starts[k], s, val)
            gg = val if gg is None else gg + val
        m2 = ADAM_B1 * m_ref[...] + (1.0 - ADAM_B1) * gg
        v2 = ADAM_B2 * v_ref[...] + (1.0 - ADAM_B2) * (gg * gg)
        g_ref[...] = gg
        d_ref[...] = -ADAM_LR * ((m2 / c1) / (jnp.sqrt(v2 / c2) + ADAM_EPS) + ADAM_WD * w_ref[...])
        nm_ref[...] = m2
        nv_ref[...] = v2

    blk = pl.BlockSpec((tr, C), lambda i: (i, 0))
    piece = lambda p, k: pl.BlockSpec((p.shape[0], tr, C), lambda i: (0, jnp.clip(i - starts[k], 0, rows[k] // tr - 1), 0))
    pblk = [piece(p, k) for part in parts for k, p in enumerate(part)]
    return _pc(body, grid=(R // tr,), in_specs=[blk] + pblk + [blk, blk], out_specs=[blk] * 4, out_shape=[_sds((R, C), F32)] * 4,
               compiler_params=_cparams(("parallel",)), name=name)(w, *flat, m, v)


def sum_slots(recv, name):
    n, R, C = recv.shape

    def body(r_ref, o_ref):
        acc = r_ref[0]
        for s in range(1, n):
            acc = acc + r_ref[s]
        o_ref[...] = acc

    return _pc(body, grid=(1,), in_specs=[pl.BlockSpec((n, R, C), lambda i: (0, 0, 0))],
               out_specs=pl.BlockSpec((R, C), lambda i: (0, 0)), out_shape=_sds((R, C), F32),
               compiler_params=_cparams(("arbitrary",)), name=name)(recv)


def _chip_peers():
    x, y, c = lax.axis_index("x"), lax.axis_index("y"), lax.axis_index("c")
    return (x, y, c), [(1 - x, y, c), (x, 1 - y, c), (1 - x, 1 - y, c)]


def _chip_index(p):
    return 2 * p[0] + p[1]


def _win(ref, axis, chip, size):
    if axis is None:
        return ref.at[chip]
    idx = [slice(None)] * len(ref.shape)
    idx[axis] = pl.ds(pl.multiple_of(chip * size, size), size)
    return ref.at[tuple(idx)]


def _remote(src, dst, send_sem, recv_sem, peer):
    return pltpu.make_async_remote_copy(src_ref=src, dst_ref=dst, send_sem=send_sem, recv_sem=recv_sem, device_id=peer,
                                        device_id_type=MESH)


HBM_SPEC = pl.BlockSpec(memory_space=pltpu.HBM)
SEM_SPEC = pl.BlockSpec(memory_space=pltpu.SEMAPHORE)
ANY_SPEC = pl.BlockSpec(memory_space=pl.ANY)
DATAFLOW = pltpu.SideEffectType.DATAFLOW_SIDE_EFFECTING


def gather_piece(i, l, o, axis, size):
    return (i, lambda r, chip: r.at[l], o, lambda r, chip: _win(r, axis, chip, size))


def scatter_piece(i, o, axis, size):
    return (i, lambda r, chip: _win(r, axis, chip, size), o, lambda r, chip: r.at[chip])


def whole_piece(i):
    return (i, lambda r, chip: r, i, lambda r, chip: r)


def _copies(pieces, in_refs, out_refs, send, recv, sibling):
    me, peers = _chip_peers()
    if sibling:
        peers = [(me[0], me[1], 1 - me[2])]
    mine = _chip_index(me)
    remote = []
    for n, (i, src, o, dst) in enumerate(pieces):
        d = dst(out_refs[o], mine)
        remote += [_remote(src(in_refs[i], _chip_index(p)), d, send.at[len(peers) * n + k], recv.at[len(peers) * n + k], p)
                   for k, p in enumerate(peers)]
    return remote


def own_window(a, axis, size, chip):
    if axis is None:
        return lax.dynamic_index_in_dim(a, chip, 0, keepdims=False)
    return lax.dynamic_slice_in_dim(a, chip * size, size, axis=axis)


def place_own(land, own, axis, size, chip):
    if axis is None:
        return lax.dynamic_update_slice_in_dim(land, own[None], chip, axis=0)
    return lax.dynamic_update_slice_in_dim(land, own, chip * size, axis=axis)


def exchange_start(pieces, ins, out_shapes, after, name, sibling=False):
    n_in, n_out, ncp = len(ins), len(out_shapes), len(pieces)

    def body(*refs):
        in_refs, land_refs = refs[:n_in], refs[n_in:n_in + n_out]
        send, recv = refs[n_in + n_out + 1], refs[n_in + n_out + 2]
        token = refs[-1]
        for cp in _copies(pieces, in_refs, land_refs, send, recv, sibling):
            cp.start()
        token[...] = jnp.zeros_like(token)

    hbm = lambda a: pltpu.with_memory_space_constraint(a, pltpu.HBM)
    lands = [hbm(lax.empty(s.shape, s.dtype)) for s in out_shapes]
    sem = pltpu.SemaphoreType.DMA(((1 if sibling else 3) * ncp,))
    thru = [pltpu.HBM(a.shape, a.dtype) for a in ins] + [pltpu.HBM(s.shape, s.dtype) for s in out_shapes]
    res = _pc(body, in_specs=[HBM_SPEC] * (n_in + n_out) + [ANY_SPEC],
              out_specs=[SEM_SPEC, SEM_SPEC] + [HBM_SPEC] * (n_in + n_out) + [pl.BlockSpec(memory_space=pltpu.VMEM)],
              out_shape=[sem, sem] + thru + [_sds((8, LANES), F32)],
              input_output_aliases={i: 2 + i for i in range(n_in + n_out)},
              compiler_params=pltpu.CompilerParams(has_side_effects=DATAFLOW), name=name)(*[hbm(a) for a in ins], *lands, after)
    return (res[0], res[1]), list(res[2:2 + n_in]), list(res[2 + n_in:2 + n_in + n_out]), res[-1]


def exchange_wait(pieces, sems, ins, lands, after, name, sibling=False):
    n_in, n_out = len(ins), len(lands)

    def body(*refs):
        in_refs, land_refs = refs[:n_in], refs[n_in:n_in + n_out]
        send, recv = refs[n_in + n_out], refs[n_in + n_out + 1]
        for cp in _copies(pieces, in_refs, land_refs, send, recv, sibling):
            cp.wait_send()
            cp.wait_recv()

    thru = [pltpu.HBM(a.shape, a.dtype) for a in ins] + [pltpu.HBM(a.shape, a.dtype) for a in lands]
    res = _pc(body, in_specs=[HBM_SPEC] * (n_in + n_out) + [SEM_SPEC, SEM_SPEC, ANY_SPEC], out_specs=[HBM_SPEC] * (n_in + n_out),
              out_shape=thru, input_output_aliases={i: i for i in range(n_in + n_out)},
              compiler_params=pltpu.CompilerParams(has_side_effects=DATAFLOW), name=name)(*ins, *lands, sems[0], sems[1], after)
    return list(res[:n_in]), list(res[n_in:])


def exchange_all(buf, name):
    def body(in_ref, out_ref, send, recv, local):
        x, y, c = lax.axis_index("x"), lax.axis_index("y"), lax.axis_index("c")
        mine = 4 * x + 2 * y + c
        loc = pltpu.make_async_copy(in_ref, out_ref.at[mine], local)
        loc.start()
        copies = [loc]
        for k in range(1, 8):
            peer = (x ^ (k >> 2), y ^ ((k >> 1) & 1), c ^ (k & 1))
            cp = pltpu.make_async_remote_copy(src_ref=in_ref, dst_ref=out_ref.at[mine], send_sem=send.at[k - 1],
                                              recv_sem=recv.at[k - 1], device_id=peer, device_id_type=MESH)
            cp.start()
            copies.append(cp)
        for cp in copies:
            cp.wait()

    anyspec = pl.BlockSpec(memory_space=pl.ANY)
    return _pc(body, in_specs=[anyspec], out_specs=anyspec, out_shape=_sds((8,) + buf.shape, buf.dtype),
               scratch_shapes=[pltpu.SemaphoreType.DMA((7,)), pltpu.SemaphoreType.DMA((7,)), pltpu.SemaphoreType.DMA],
               name=name)(buf)


def _norm_fwd(x, g, name):
    return rowwise(f_rms, [(x, D, 0, 0)], [(g, D, 0, 0)], [(D, 0, BF16)], ts=512, name=name)[0]


def _norm_bwd(x, g, dh, dres, name):
    (dx,), (dg,) = rowwise_bwd(f_rms, [(x, D, 0, 0)], [(g, D, 0, 0)], [(dh, D, 0, 0)], need=[True],
                               adds={0: (dres, D, 0, 0)}, ts=256, name=name)
    return dx, dg


def pool_fwd(x, W, tag, late=None):
    h = _norm_fwd(x, W["ng"], tag + "_norm")
    proj = mm(h, W["w_in"], out_dtype=BF16, name=tag + "_in")
    if late is not None:
        W = dict(W, **late(proj))
    p = pool_time_fwd(proj, tag + "_win")
    pg = gmm("nn", p, W["w_grp"], G=4, out_dtype=BF16, name=tag + "_grp")
    y = rowwise(f_pool_gate, [(pg, POOL_GROUP, 0, 1), (proj, POOL_GROUP, 4, 1)], [(W["scale"], POOL_GROUP, 0, 1)],
                [(POOL_GROUP, 1, BF16)], ncol=4, ts=512, name=tag + "_gate")[0]
    xn = mm(y, W["w_out"], add=x, name=tag + "_out")
    return xn, (x, h, proj, p, pg, y)


def pool_bwd(dxn, W, saved, tag, after=None, emit=None):
    x, h, proj, p, pg, y = saved
    emit = emit or (lambda grads: None)
    dy = mm(dxn, W["w_out"], tb=True, after=after, out_dtype=BF16, name=tag + "_dy")
    g = {}
    (dpg, dproj), (g["scale"],) = rowwise_bwd(
        f_pool_gate, [(pg, POOL_GROUP, 0, 1), (proj, POOL_GROUP, 4, 1)], [(W["scale"], POOL_GROUP, 0, 1)],
        [(dy, POOL_GROUP, 0, 1)], need=[True, True], place={1: (2 * POOL_WIDTH, 4)}, narrow=(0, 1), ncol=4, ts=512, name=tag + "_dgate")
    dp = gmm("nt", dpg, W["w_grp"], G=4, out_dtype=BF16, name=tag + "_dp")
    dproj = pool_time_bwd(dp, dproj, tag + "_dwin")
    g["w_in"] = mm(h, dproj, ta=True, out_dtype=BF16, name=tag + "_dw_in")
    t1 = emit({"w_in": g["w_in"]})
    g["w_out"] = mm(y, dxn, ta=True, after=t1, out_dtype=BF16, name=tag + "_dwout")
    g["w_grp"] = gmm("tn", p, dpg, G=4, out_dtype=BF16, name=tag + "_dwgrp")
    t2 = emit({"w_out": g["w_out"], "w_grp": g["w_grp"]})
    dh = mm(dproj, W["w_in"], tb=True, after=t2, name=tag + "_dh")
    dx, g["ng"] = _norm_bwd(x, W["ng"], dh, dxn, tag + "_dnorm")
    return dx, g


def gdn_fwd(x, W, tag, late=None):
    h = _norm_fwd(x, W["ng"], tag + "_norm")
    proj = mm(h, W["w_in"], name=tag + "_in")
    qkv = gdn_conv_fwd(proj, W["conv"], tag + "_conv")
    g_b, beta_b = rowwise(f_gdn_gates, [(proj, LANES, 6144 // LANES, 0)], [(W["a_log"], LANES, 0, 0), (W["dt_bias"], LANES, 0, 0)],
                          [(GDN_QK, 0, F32), (GDN_QK, 0, F32)], ts=512, name=tag + "_gates")
    o, states = gdn_chunk_fwd(qkv, g_b, beta_b, tag + "_chunk")
    og = rowwise(f_gdn_out, [(o, GDN_DV, 0, 1), (proj, GDN_DV, 4096 // GDN_DV, 1)], [(W["norm_g"], GDN_DV, 0, 0)],
                 [(GDN_DV, 1, BF16)], ncol=GDN_H, ts=512, name=tag + "_onorm")[0]
    if late is not None:
        W = dict(W, **late(og))
    xn = mm(og, W["w_out"], add=x, name=tag + "_out")
    return xn, (x, h, proj, qkv, g_b, beta_b, o, states, og)


def gdn_bwd(dxn, W, saved, tag, after=None):
    x, h, proj, qkv, g_b, beta_b, o, states, og = saved
    dog = mm(dxn, W["w_out"], tb=True, after=after, out_dtype=BF16, name=tag + "_dog")
    g = {"w_out": mm(og, dxn, ta=True, out_dtype=BF16, name=tag + "_dwout")}
    (do, dproj), (g["norm_g"],) = rowwise_bwd(
        f_gdn_out, [(o, GDN_DV, 0, 1), (proj, GDN_DV, 4096 // GDN_DV, 1)], [(W["norm_g"], GDN_DV, 0, 0)],
        [(dog, GDN_DV, 0, 1)], need=[True, True], place={1: (GDN_IN_PAD, 4096 // GDN_DV)}, narrow=(1,), ncol=GDN_H, ts=512, name=tag + "_donorm")
    dq, dk, dv, dg_b, dbeta_b = gdn_chunk_bwd(qkv, g_b, beta_b, states, do, tag + "_dchunk")
    (dproj,), (g["a_log"], g["dt_bias"]) = rowwise_bwd(
        f_gdn_gates, [(proj, LANES, 6144 // LANES, 0)], [(W["a_log"], LANES, 0, 0), (W["dt_bias"], LANES, 0, 0)],
        [(dg_b, GDN_QK, 0, 0), (dbeta_b, GDN_QK, 0, 0)], need=[True], place={0: (dproj, 6144 // LANES)}, ts=256, name=tag + "_dgates")
    dproj, g["conv"] = gdn_conv_bwd(proj, W["conv"], dq, dk, dv, dproj, tag + "_dconv")
    dh = mm(dproj, W["w_in"], tb=True, name=tag + "_dh")
    g["w_in"] = mm(h, dproj, ta=True, out_dtype=BF16, name=tag + "_dw_in")
    dx, g["ng"] = _norm_bwd(x, W["ng"], dh, dxn, tag + "_dnorm")
    return dx, g


def mla_fwd(x, pos, W, tag):
    h = _norm_fwd(x, W["ng"], tag + "_norm")
    proj = mm(h, W["w_in"], out_dtype=BF16, name=tag + "_in")
    hq = rowwise(f_rms, [(proj, MLA_Q_LORA, 0, 0)], [(W["q_g"], MLA_Q_LORA, 0, 0)], [(MLA_Q_LORA, 0, BF16)], ts=512, name=tag + "_qnorm")[0]
    hkv = rowwise(f_rms, [(proj, MLA_KV_LORA, 2, 0)], [(W["kv_g"], MLA_KV_LORA, 0, 0)], [(MLA_KV_LORA, 0, BF16)], ts=512, name=tag + "_kvnorm")[0]
    qpad = mm(hq, W["w_uq"], out_dtype=BF16, name=tag + "_uq")
    kv = mm(hkv, W["w_ukv"], out_dtype=BF16, name=tag + "_ukv")
    qh, kh, vh = mla_prep_fwd(qpad, kv, proj, pos, W["rope"], tag + "_prep")
    o, lse = flash_fwd(qh, kh, vh, tag + "_attn")
    og = rowwise(f_ogate, [(o, 512, 0, 1), (proj, 512, 4, 1)], [], [(512, 1, BF16)], ncol=4, ts=512, name=tag + "_ogate")[0]
    xn = mm(og, W["w_out"], add=x, name=tag + "_out")
    return xn, (x, h, proj, hq, hkv, qh, kh, vh, o, lse, og)


def mla_bwd(dxn, pos, W, saved, tag, after=None):
    x, h, proj, hq, hkv, qh, kh, vh, o, lse, og = saved
    dog = mm(dxn, W["w_out"], tb=True, after=after, out_dtype=BF16, name=tag + "_dog")
    g = {"w_out": mm(og, dxn, ta=True, out_dtype=BF16, name=tag + "_dwout")}
    dproj = jnp.zeros(proj.shape, BF16)
    (do, dproj), _ = rowwise_bwd(f_ogate, [(o, 512, 0, 1), (proj, 512, 4, 1)], [], [(dog, 512, 0, 1)], need=[True, True],
                                 place={1: (dproj, 4)}, narrow=(0,), ncol=4, ts=512, name=tag + "_dogate")
    dqh, dkh, dvh = flash_bwd(qh, kh, vh, o, lse, do, tag + "_dattn")
    dqpad, dkv, dproj = mla_prep_bwd(dqh, dkh, dvh, pos, W["rope"], dproj, tag + "_dprep")
    dhq = mm(dqpad, W["w_uq"], tb=True, name=tag + "_dhq")
    g["w_uq"] = mm(hq, dqpad, ta=True, out_dtype=BF16, name=tag + "_dwuq")
    dhkv = mm(dkv, W["w_ukv"], tb=True, name=tag + "_dhkv")
    g["w_ukv"] = mm(hkv, dkv, ta=True, out_dtype=BF16, name=tag + "_dwukv")
    (dproj,), (g["q_g"],) = rowwise_bwd(f_rms, [(proj, MLA_Q_LORA, 0, 0)], [(W["q_g"], MLA_Q_LORA, 0, 0)], [(dhq, MLA_Q_LORA, 0, 0)],
                                        need=[True], place={0: (dproj, 0)}, ts=256, name=tag + "_dqnorm")
    (dproj,), (g["kv_g"],) = rowwise_bwd(f_rms, [(proj, MLA_KV_LORA, 2, 0)], [(W["kv_g"], MLA_KV_LORA, 0, 0)], [(dhkv, MLA_KV_LORA, 0, 0)],
                                         need=[True], place={0: (dproj, 2)}, ts=256, name=tag + "_dkvnorm")
    dh = mm(dproj, W["w_in"], tb=True, name=tag + "_dh")
    g["w_in"] = mm(h, dproj, ta=True, out_dtype=BF16, name=tag + "_dw_in")
    dx, g["ng"] = _norm_bwd(x, W["ng"], dh, dxn, tag + "_dnorm")
    return dx, g


def _pad_cols(a, n):
    return jnp.pad(a, ((0, 0), (0, n - a.shape[1])))


def _mla_w_in_layout(w):
    z = lambda n: jnp.zeros((w.shape[0], n), w.dtype)
    kr = w[:, 1280:1344]
    return jnp.concatenate([w[:, :768], z(256), w[:, 768:1280], kr[:, :32], z(32), kr[:, 32:], z(32), z(384), w[:, 1344:]], axis=1)


def _mla_w_in_unlayout(g):
    return jnp.concatenate([g[:, :768], g[:, 1024:1536], g[:, 1536:1568], g[:, 1600:1632], g[:, 2048:]], axis=1)


def _mla_w_uq_layout(w):
    w3 = w.reshape(w.shape[0], MLA_H, MLA_NOPE + MLA_ROPE)
    z = jnp.zeros((w.shape[0], MLA_H, 32), w.dtype)
    return jnp.concatenate([w3[..., :128], w3[..., 128:160], z, w3[..., 160:192], z], axis=-1).reshape(w.shape[0], MLA_H * 256)


def _mla_w_uq_unlayout(g):
    g3 = g.reshape(g.shape[0], MLA_H, 256)
    return jnp.concatenate([g3[..., :128], g3[..., 128:160], g3[..., 192:224]], axis=-1).reshape(g.shape[0], MLA_H * 192)


def _rope_consts():
    half = MLA_ROPE // 2
    inv = ROPE_THETA ** (-jnp.arange(half, dtype=F32) / half)
    z = jnp.zeros((half,), F32)
    o = jnp.ones((half,), F32)
    row = lambda *p: jnp.concatenate(p).reshape(1, LANES)
    return row(inv, z, inv, z), row(o, z, o, z), row(-o, z, o, z)


BIG = ["pool_w_in", "pool_w_grp", "pool_w_out", "gdn_w_in", "gdn_w_out", "mla_w_in", "mla_w_uq", "mla_w_ukv", "mla_w_out"]
BIG_LAYOUT = {"pool_w_in": (1, 1024, (1024, 4096)), "pool_w_grp": (1, 128, (4, 512, 512)), "pool_w_out": (0, 512, (2048, 1024)),
              "gdn_w_in": (None, None, (4, 1024, 1540)), "gdn_w_out": (0, 512, (2048, 1024)),
              "mla_w_in": (None, None, (4, 1024, 848)), "mla_w_uq": (1, 768, (768, 3072)), "mla_w_ukv": (1, 1024, (512, 4096)),
              "mla_w_out": (0, 512, (2048, 1024))}
SMALL_SHARDED = ["pool_scale", "gdn_conv", "mla_q_norm_g", "mla_kv_norm_g"]
SMALL_AXIS = {"pool_scale": 1, "gdn_conv": 2, "mla_q_norm_g": 1, "mla_kv_norm_g": 1}
REPLICATED = ["norm_g", "gdn_a_log", "gdn_dt_bias", "gdn_norm_g", "final_g"]
PACK_C = 1024


def _pack(parts, dtype, row_mult):
    flat = jnp.concatenate([p.reshape(-1).astype(dtype) for p in parts])
    rows = -(-flat.shape[0] // PACK_C)
    rows = -(-rows // row_mult) * row_mult
    return jnp.pad(flat, (0, rows * PACK_C - flat.shape[0])).reshape(rows, PACK_C)


def _unpack(buf, shapes):
    lead = buf.shape[:-2]
    flat = buf.reshape(lead + (-1,))
    out, off = [], 0
    for s in shapes:
        n = int(np.prod(s))
        out.append(flat[..., off:off + n].reshape(lead + tuple(s)))
        off += n
    return out


def _unshard(g4, axis):
    a = jnp.moveaxis(g4, 0, axis)
    s = a.shape
    return a.reshape(s[:axis] + (s[axis] * s[axis + 1],) + s[axis + 2:])


def _to_shards(a, axis):
    s = a.shape
    return jnp.moveaxis(a.reshape(s[:axis] + (4, s[axis] // 4) + s[axis + 1:]), axis, 0)


def layer_weights(full, small, rep, layer):
    ng = rep["norm_g"][layer:layer + 1]
    side_by_side = lambda a4: jnp.moveaxis(a4, 0, 1).reshape(a4.shape[1], 4 * a4.shape[2])
    if layer in (0, 3):
        j = layer // 3
        return dict(ng=ng, w_in=full[("pool_w_in", j)], w_grp=full[("pool_w_grp", j)], scale=small["pool_scale"][j:j + 1],
                    w_out=full[("pool_w_out", j)])
    if layer == 1:
        return dict(ng=ng, w_in=_pad_cols(side_by_side(full[("gdn_w_in", 0)]), GDN_IN_PAD),
                    conv=jnp.pad(small["gdn_conv"][0], ((0, 4), (0, 0))), a_log=_pad_cols(rep["gdn_a_log"], LANES),
                    dt_bias=_pad_cols(rep["gdn_dt_bias"], LANES), norm_g=rep["gdn_norm_g"], w_out=full.get(("gdn_w_out", 0)))
    return dict(ng=ng, w_in=_mla_w_in_layout(side_by_side(full[("mla_w_in", 0)])), q_g=small["mla_q_norm_g"],
                kv_g=small["mla_kv_norm_g"], w_uq=_mla_w_uq_layout(full[("mla_w_uq", 0)]), w_ukv=full[("mla_w_ukv", 0)],
                w_out=full[("mla_w_out", 0)], rope=_rope_consts())


def big_grad_pieces(gl):
    g0, g1, g2, g3 = gl
    slots = lambda a: jnp.moveaxis(a.reshape(a.shape[0], 4, a.shape[1] // 4), 1, 0)
    out = {}
    for l, g in ((0, g0), (1, g3)):
        if g is not None:
            out.update({("pool_w_in", l): g["w_in"], ("pool_w_grp", l): g["w_grp"], ("pool_w_out", l): g["w_out"]})
    if g1 is not None:
        out.update({("gdn_w_in", 0): slots(g1["w_in"][:, :GDN_IN]), ("gdn_w_out", 0): g1["w_out"]})
    if g2 is not None:
        out.update({("mla_w_in", 0): slots(_mla_w_in_unlayout(g2["w_in"])), ("mla_w_uq", 0): _mla_w_uq_unlayout(g2["w_uq"]),
                    ("mla_w_ukv", 0): g2["w_ukv"], ("mla_w_out", 0): g2["w_out"]})
    return out


def small_grads(gl, dfinal):
    g0, g1, g2, g3 = gl
    return {"norm_g": jnp.concatenate([g0["ng"], g1["ng"], g2["ng"], g3["ng"]], axis=0),
            "pool_scale": jnp.concatenate([g0["scale"], g3["scale"]], axis=0), "gdn_conv": g1["conv"][None, :4],
            "gdn_a_log": g1["a_log"][:, :GDN_H], "gdn_dt_bias": g1["dt_bias"][:, :GDN_H], "gdn_norm_g": g1["norm_g"],
            "mla_q_norm_g": g2["q_g"], "mla_kv_norm_g": g2["kv_g"], "final_g": dfinal.reshape(D)}


NAMES = ["norm_g", "pool_w_in", "pool_w_grp", "pool_scale", "pool_w_out", "gdn_w_in", "gdn_conv", "gdn_a_log", "gdn_dt_bias",
         "gdn_norm_g", "gdn_w_out", "mla_w_in", "mla_q_norm_g", "mla_w_uq", "mla_kv_norm_g", "mla_w_ukv", "mla_w_out", "final_g"]


def kernel(x, positions, norm_g, pool_w_in, pool_w_grp, pool_scale, pool_w_out, gdn_w_in, gdn_conv, gdn_a_log, gdn_dt_bias, gdn_norm_g, gdn_w_out, mla_w_in, mla_q_norm_g, mla_w_uq, mla_kv_norm_g, mla_w_ukv, mla_w_out, final_g, loss_target, m_norm_g, m_pool_w_in, m_pool_w_grp, m_pool_scale, m_pool_w_out, m_gdn_w_in, m_gdn_conv, m_gdn_a_log, m_gdn_dt_bias, m_gdn_norm_g, m_gdn_w_out, m_mla_w_in, m_mla_q_norm_g, m_mla_w_uq, m_mla_kv_norm_g, m_mla_w_ukv, m_mla_w_out, m_final_g, v_norm_g, v_pool_w_in, v_pool_w_grp, v_pool_scale, v_pool_w_out, v_gdn_w_in, v_gdn_conv, v_gdn_a_log, v_gdn_dt_bias, v_gdn_norm_g, v_gdn_w_out, v_mla_w_in, v_mla_q_norm_g, v_mla_w_uq, v_mla_kv_norm_g, v_mla_w_ukv, v_mla_w_out, v_final_g):
    args = locals()
    w = {n: args[n] for n in NAMES}
    m = {n: args["m_" + n] for n in NAMES}
    v = {n: args["v_" + n] for n in NAMES}
    my_chip = (2 * lax.axis_index("x") + lax.axis_index("y")).astype(I32)
    S_ = x.shape[1]
    x0, pos, target = x[0], positions.reshape(S_, 1).astype(F32), loss_target[0]
    rep = {n: w[n] for n in REPLICATED}

    shard = {(n, l): w[n][l:l + 1].astype(BF16) for n in BIG for l in range(w[n].shape[0])}
    small_shapes = [w[n].shape for n in SMALL_SHARDED]
    shard[("small", 0)] = _pack([w[n] for n in SMALL_SHARDED], F32, 8)[None]
    layout = dict(BIG_LAYOUT, small=(None, None, (4,) + shard[("small", 0)].shape[1:]))

    def gather_start(group, after, tag):
        pieces = [gather_piece(i, 0, i, layout[n][0], layout[n][1]) for i, (n, l) in enumerate(group)]
        shapes = [_sds(layout[n][2], shard[(n, l)].dtype) for n, l in group]
        sems, ins, lands, token = exchange_start(pieces, [shard[k] for k in group], shapes, after, tag + "_start")
        return (pieces, sems, ins, lands), token

    def finish(handle, after, tag):
        return exchange_wait(*handle, after, tag + "_wait")

    def gathered(group, handle, after, tag):
        srcs, lands = finish(handle, after, tag)
        return {(n, l): place_own(a, s[0], layout[n][0], layout[n][1], my_chip) for (n, l), s, a in zip(group, srcs, lands)}

    group_a = [("small", 0), ("pool_w_in", 0)]
    group_a2 = [("pool_w_grp", 0), ("pool_w_out", 0)]
    group_b = [("gdn_w_in", 0)]
    group_c = [("gdn_w_out", 0), ("mla_w_in", 0), ("mla_w_uq", 0), ("mla_w_ukv", 0), ("mla_w_out", 0), ("pool_w_in", 1),
               ("pool_w_grp", 1), ("pool_w_out", 1)]
    full = {}
    h_a, t_a = gather_start(group_a, x0, "gather_a")
    h_a2, t_a2 = gather_start(group_a2, t_a, "gather_a2")
    h_b, t_b = gather_start(group_b, t_a2, "gather_b")
    h_c, t_c = gather_start(group_c, t_b, "gather_c")
    full.update(gathered(group_a, h_a, t_c, "gather_a"))
    small = {n: _unshard(a, SMALL_AXIS[n]) for n, a in zip(SMALL_SHARDED, _unpack(full[("small", 0)], small_shapes))}

    def late_l0(proj):
        full.update(gathered(group_a2, h_a2, proj, "gather_a2"))
        return dict(w_grp=full[("pool_w_grp", 0)], w_out=full[("pool_w_out", 0)])

    x1, s0 = pool_fwd(x0, dict(ng=rep["norm_g"][0:1], w_in=full[("pool_w_in", 0)], scale=small["pool_scale"][0:1]), "l0", late=late_l0)
    W0 = layer_weights(full, small, rep, 0)
    full.update(gathered(group_b, h_b, x1, "gather_b"))

    def late_l1(og):
        full.update(gathered(group_c, h_c, og, "gather_c"))
        return dict(w_out=full[("gdn_w_out", 0)])

    x2, s1 = gdn_fwd(x1, layer_weights(full, small, rep, 1), "l1", late=late_l1)
    W1, W2, W3 = (layer_weights(full, small, rep, i) for i in (1, 2, 3))
    x3, s2 = mla_fwd(x2, pos, W2, "l2")
    x4, s3 = pool_fwd(x3, W3, "l3")
    loss_part, dx4, dfinal = loss_head(x4, target, final_g.reshape(1, D), "loss_head")

    def scatter_start(pieces_of, after, tag):
        keys = list(pieces_of)
        pieces = [scatter_piece(i, i, BIG_LAYOUT[n][0], BIG_LAYOUT[n][1]) for i, (n, l) in enumerate(keys)]
        shapes = [_sds((4,) + tuple(w[n].shape[1:]), BF16) for n, l in keys]
        sems, ins, lands, token = exchange_start(pieces, [pieces_of[k] for k in keys], shapes, after, tag + "_start")
        return keys, (pieces, sems, ins, lands), token

    def scattered(keys, handle, after, tag):
        srcs, lands = finish(handle, after, tag)
        return {(n, l): place_own(a, own_window(g, BIG_LAYOUT[n][0], BIG_LAYOUT[n][1], my_chip), None, None, my_chip)
                for (n, l), g, a in zip(keys, srcs, lands)}

    dx3, g3 = pool_bwd(dx4, W3, s3, "l3")
    k3, h3, t3 = scatter_start(big_grad_pieces((None, None, None, g3)), dx3, "scatter_l3")
    dx2, g2 = mla_bwd(dx3, pos, W2, s2, "l2", after=t3)
    k2, h2, t2 = scatter_start(big_grad_pieces((None, None, g2, None)), dx2, "scatter_l2")
    dx1, g1 = gdn_bwd(dx2, W1, s1, "l1", after=t2)
    k1, h1, t1 = scatter_start(big_grad_pieces((None, g1, None, None)), dx1, "scatter_l1")
    def swap_start(part, tag):
        keys = list(part)
        ins = [part[k] for k in keys]
        pieces = [whole_piece(i) for i in range(len(keys))]
        sems, ins, lands, token = exchange_start(pieces, ins, [_sds(a.shape, a.dtype) for a in ins], ins[0], tag + "_start", sibling=True)
        swaps.append((keys, (pieces, sems, ins, lands), tag))
        return token

    last, swaps = [], []

    def emit_l0(grads):
        first = not last
        now = next(iter(grads.values()))
        early = [(k3, h3, "scatter_l3"), (k2, h2, "scatter_l2")] if first else [(k1, h1, "scatter_l1")]
        landed = {}
        for keys, handle, tag in early:
            landed.update(scattered(keys, handle, now, tag))
        swapping = swap_start(landed, "swap_a" if first else "swap_b")
        tag = "scatter_l0a" if first else "scatter_l0b"
        keys, handle, token = scatter_start({("pool_" + k, 0): a for k, a in grads.items()}, swapping, tag)
        last.append((keys, handle, tag))
        return token

    dx0, g0 = pool_bwd(dx1, W0, s0, "l0", after=t1, emit=emit_l0)
    landed = {}
    for keys, handle, tag in last:
        landed.update(scattered(keys, handle, dx0, tag))
    swap_start(landed, "swap_c")
    recv, sib = {}, {}
    for keys, handle, tag in swaps:
        mine_, theirs = exchange_wait(*handle, dx0, tag + "_wait", sibling=True)
        recv.update(zip(keys, mine_))
        sib.update(zip(keys, theirs))

    sg = small_grads((g0, g1, g2, g3), dfinal)
    small_names = SMALL_SHARDED + REPLICATED
    small_buf = _pack([sg[n] for n in small_names] + [loss_part], F32, 8)
    small_sum = sum_slots(exchange_all(small_buf, "gather_small"), "sum_small")
    full_small = _unpack(small_sum, [sg[n].shape for n in small_names] + [(1, LANES)])
    loss = full_small[-1][0, 0]
    small_part = {}
    for n, a in zip(small_names, full_small[:-1]):
        if n in SMALL_AXIS:
            a = lax.dynamic_index_in_dim(_to_shards(a, SMALL_AXIS[n]), my_chip, axis=0, keepdims=False)
        small_part[n] = a

    outs = []
    for n in NAMES:
        shp = w[n].shape
        two = (int(np.prod(shp[:-1])), shp[-1]) if len(shp) > 1 else (1, shp[0])
        if n in BIG_LAYOUT:
            layers = shp[0]
            rows = lambda a: a.reshape(4, two[0] // layers, two[1])
            parts = [[rows(recv[(n, l)]) for l in range(layers)], [rows(sib[(n, l)]) for l in range(layers)]]
        else:
            parts = [[small_part[n].reshape((1,) + two)]]
        res = adamw(w[n].reshape(two), parts, m[n].reshape(two), v[n].reshape(two), "adamw_" + n)
        outs.append([r.reshape(shp) for r in res])
    return (loss, dx0[None], *[o[0] for o in outs], *[o[1] for o in outs], *[o[2] for o in outs], *[o[3] for o in outs])
```

```python
import math

import jax
import jax.numpy as jnp
import numpy as np
from jax import lax
from jax.experimental import pallas as pl
from jax.experimental.pallas import tpu as pltpu

F32 = jnp.float32
BF16 = jnp.bfloat16
I32 = jnp.int32

D = 1024
EPS = 1e-6
POOL_WIDTH = 2048
POOL_GROUP = 512
GDN_H, GDN_DK, GDN_DV, GDN_C = 8, 128, 256, 64
GDN_QK, GDN_V, GDN_CONV_CH, GDN_IN = 1024, 2048, 4096, 6160
GDN_IN_PAD = 6272
MLA_H, MLA_NOPE, MLA_ROPE, MLA_V = 16, 128, 64, 128
MLA_Q_LORA, MLA_KV_LORA, MLA_WIDTH, MLA_IN = 768, 512, 2048, 3392
MLA_IN_PAD = 4096
MLA_SCALE = (MLA_NOPE + MLA_ROPE) ** -0.5
ROPE_THETA = 10000.0
ADAM_LR, ADAM_B1, ADAM_B2, ADAM_EPS, ADAM_WD, ADAM_STEP = 0.001, 0.9, 0.999, 1e-08, 0.01, 10

VMEM_LIMIT_V7X = 56 * 1024 * 1024
LANES = 128
MESH = pl.DeviceIdType.MESH


def _pc(body, **kw):
    return pl.pallas_call(body, **kw)


def _cparams(sem):
    return pltpu.CompilerParams(dimension_semantics=sem, vmem_limit_bytes=VMEM_LIMIT_V7X)


def _tile(n, cap):
    t = (cap // LANES) * LANES
    while t >= LANES:
        if n % t == 0:
            return t
        t -= LANES
    return n


def _sds(shape, dt):
    return jax.ShapeDtypeStruct(shape, dt)


def mm(a, b, *, ta=False, tb=False, add=None, after=None, out_dtype=F32, name):
    if ta:
        K, M = a.shape
    else:
        M, K = a.shape
    if tb:
        N, K2 = b.shape
    else:
        K2, N = b.shape
    assert K == K2, (a.shape, b.shape, ta, tb)
    tm, tn, tk = _tile(M, 1024), _tile(N, 1024), _tile(K, 1024)
    nk = K // tk
    a_spec = pl.BlockSpec((tk, tm), lambda i, j, k: (k, i)) if ta else pl.BlockSpec((tm, tk), lambda i, j, k: (i, k))
    b_spec = pl.BlockSpec((tn, tk), lambda i, j, k: (j, k)) if tb else pl.BlockSpec((tk, tn), lambda i, j, k: (k, j))
    o_spec = pl.BlockSpec((tm, tn), lambda i, j, k: (i, j))
    dn = (((0 if ta else 1,), (1 if tb else 0,)), ((), ()))
    has_add = add is not None

    def body(*refs):
        a_ref, b_ref = refs[0], refs[1]
        part = lax.dot_general(a_ref[...].astype(BF16), b_ref[...].astype(BF16), dn, preferred_element_type=F32)
        if nk == 1:
            refs[-1][...] = (part + refs[2][...] if has_add else part).astype(out_dtype)
            return
        o_ref, acc = refs[-2], refs[-1]
        k = pl.program_id(2)

        @pl.when(k == 0)
        def _():
            acc[...] = part

        @pl.when(k > 0)
        def _():
            acc[...] += part

        @pl.when(k == nk - 1)
        def _():
            r = acc[...]
            if has_add:
                r = r + refs[2][...]
            o_ref[...] = r.astype(out_dtype)

    ins = [a, b] + ([add] if has_add else []) + ([after] if after is not None else [])
    specs = [a_spec, b_spec] + ([o_spec] if has_add else []) + ([pl.BlockSpec(memory_space=pl.ANY)] if after is not None else [])
    return _pc(body, grid=(M // tm, N // tn, nk), in_specs=specs, out_specs=o_spec, out_shape=_sds((M, N), out_dtype),
               scratch_shapes=[pltpu.VMEM((tm, tn), F32)] if nk > 1 else [], compiler_params=_cparams(("parallel", "parallel", "arbitrary")),
               name=name)(*ins)


def gmm(kind, a, b, *, G, name, out_dtype=F32):
    S_ = a.shape[0]
    Ka = a.shape[1] // G
    if kind == "tn":
        N = b.shape[1] // G
        tk = _tile(S_, 512)
        nk = S_ // tk

        def body(a_ref, b_ref, o_ref, acc):
            k = pl.program_id(1)

            @pl.when(k == 0)
            def _():
                acc[...] = jnp.zeros_like(acc)

            acc[...] += lax.dot_general(a_ref[...].astype(BF16), b_ref[...].astype(BF16), (((0,), (0,)), ((), ())),
                                        preferred_element_type=F32)

            @pl.when(k == nk - 1)
            def _():
                o_ref[...] = acc[...].astype(out_dtype)

        return _pc(body, grid=(G, nk),
                   in_specs=[pl.BlockSpec((tk, Ka), lambda g, k: (k, g)), pl.BlockSpec((tk, N), lambda g, k: (k, g))],
                   out_specs=pl.BlockSpec((None, Ka, N), lambda g, k: (g, 0, 0)), out_shape=_sds((G, Ka, N), out_dtype),
                   scratch_shapes=[pltpu.VMEM((Ka, N), F32)], compiler_params=_cparams(("parallel", "arbitrary")), name=name)(a, b)
    N = b.shape[2] if kind == "nn" else b.shape[1]
    tm = _tile(S_, 1024)
    dn = (((1,), (0 if kind == "nn" else 1,)), ((), ()))

    def body(a_ref, b_ref, o_ref):
        o_ref[...] = lax.dot_general(a_ref[...].astype(BF16), b_ref[...].astype(BF16), dn, preferred_element_type=F32).astype(out_dtype)

    bshape = (None,) + tuple(b.shape[1:])
    return _pc(body, grid=(G, S_ // tm),
               in_specs=[pl.BlockSpec((tm, Ka), lambda g, i: (i, g)), pl.BlockSpec(bshape, lambda g, i: (g, 0, 0))],
               out_specs=pl.BlockSpec((tm, N), lambda g, i: (i, g)), out_shape=_sds((S_, G * N), out_dtype),
               compiler_params=_cparams(("parallel", "parallel")), name=name)(a, b)


def _rw_spec(ts, w, c, s):
    return pl.BlockSpec((ts, w), lambda j, i: (i, c + j * s))


def _rw_pspec(p, w, c, s):
    return pl.BlockSpec((p.shape[0], w), lambda j, i: (0, c + j * s))


def rowwise(f, tiles, params, outs, *, ncol=1, ts, name):
    S_ = tiles[0][0].shape[0]
    nin = len(tiles) + len(params)

    def body(*refs):
        res = f(pl.program_id(0), *[r[...].astype(F32) for r in refs[:nin]])
        for r, o in zip(refs[nin:], res):
            r[...] = o.astype(r.dtype)

    return _pc(body, grid=(ncol, S_ // ts),
               in_specs=[_rw_spec(ts, w, c, s) for (_, w, c, s) in tiles] + [_rw_pspec(*p) for p in params],
               out_specs=[_rw_spec(ts, w, 0, s) for (w, s, _) in outs],
               out_shape=[_sds((S_, w * (ncol if s else 1)), dt) for (w, s, dt) in outs],
               compiler_params=_cparams(("parallel", "parallel")), name=name)(*[t[0] for t in tiles], *[p[0] for p in params])


def rowwise_bwd(f, tiles, params, cots, *, need, adds=None, place=None, narrow=(), ncol=1, ts, name):
    S_ = tiles[0][0].shape[0]
    adds = adds or {}
    place = place or {}
    nt, npar, nc = len(tiles), len(params), len(cots)
    add_keys = sorted(adds)
    need_idx = [k for k in range(nt) if need[k]]
    into_keys = [k for k in need_idx if k in place and not isinstance(place[k][0], int)]
    n_extra = len(add_keys) + len(into_keys)

    def body(*refs):
        j, i = pl.program_id(0), pl.program_id(1)
        vals = [r[...].astype(F32) for r in refs[:nt + npar]]
        cvals = tuple(r[...].astype(F32) for r in refs[nt + npar:nt + npar + nc])
        add_refs = refs[nt + npar + nc:nt + npar + nc + len(add_keys)]
        out_refs = refs[nt + npar + nc + n_extra:]
        _, vjp = jax.vjp(lambda *v: tuple(f(j, *v)), *vals)
        grads = vjp(cvals)
        for n, k in enumerate(need_idx):
            g = grads[k]
            if k in adds:
                g = g + add_refs[add_keys.index(k)][...]
            out_refs[n][...] = g.astype(out_refs[n].dtype)
        for n in range(npar):
            ref = out_refs[len(need_idx) + n]
            first = (i == 0) if params[n][3] else jnp.logical_and(i == 0, j == 0)

            @pl.when(first)
            def _():
                ref[...] = jnp.zeros_like(ref)

            ref[...] += grads[nt + n]

    in_specs = ([_rw_spec(ts, w, c, s) for (_, w, c, s) in tiles] + [_rw_pspec(*p) for p in params]
                + [_rw_spec(ts, w, c, s) for (_, w, c, s) in cots] + [_rw_spec(ts, *adds[k][1:]) for k in add_keys]
                + [pl.BlockSpec(memory_space=pl.ANY) for _ in into_keys])
    out_specs, out_shape, aliases = [], [], {}
    for n, k in enumerate(need_idx):
        w, s = tiles[k][1], tiles[k][3]
        if k in place:
            dst, c0 = place[k]
            total = dst if isinstance(dst, int) else dst.shape[1]
            out_specs.append(_rw_spec(ts, w, c0, s))
            out_shape.append(_sds((S_, total), (BF16 if k in narrow else F32) if isinstance(dst, int) else dst.dtype))
            if k in into_keys:
                aliases[nt + npar + nc + len(add_keys) + into_keys.index(k)] = n
        else:
            out_specs.append(_rw_spec(ts, w, 0, s))
            out_shape.append(_sds((S_, w * (ncol if s else 1)), BF16 if k in narrow else F32))
    out_specs += [_rw_pspec(p[0], p[1], p[2], p[3]) for p in params]
    out_shape += [_sds(p[0].shape, F32) for p in params]
    res = _pc(body, grid=(ncol, S_ // ts), in_specs=in_specs, out_specs=out_specs, out_shape=out_shape,
              input_output_aliases=aliases, compiler_params=_cparams(("arbitrary", "arbitrary")), name=name)(
        *[t[0] for t in tiles], *[p[0] for p in params], *[c[0] for c in cots], *[adds[k][0] for k in add_keys],
        *[place[k][0] for k in into_keys])
    return list(res[:len(need_idx)]), list(res[len(need_idx):])


def _rms(x, g):
    r = lax.rsqrt(jnp.mean(x * x, axis=-1, keepdims=True) + EPS)
    return x * r * g


def _silu(x):
    return x * jax.nn.sigmoid(x)


@jax.custom_vjp
def _softplus(x):
    return jnp.maximum(x, 0.0) + jnp.log1p(jnp.exp(-jnp.abs(x)))


_softplus.defvjp(lambda x: (_softplus(x), x), lambda x, d: (d * jax.nn.sigmoid(x),))


def f_rms(j, x, g):
    return (_rms(x, g),)


def f_pool_gate(j, pg, gate, scale):
    return (pg * scale * _silu(gate),)


def f_ogate(j, o, gate):
    return (o * _silu(gate),)


def f_gdn_out(j, o, gate, g):
    return (_rms(o, g) * _silu(gate),)


def f_gdn_gates(j, ba, alog, dtb):
    lane = lax.broadcasted_iota(I32, (1, LANES), 1)
    gs, bs = [], []
    for h in range(GDN_H):
        eb = (lane == h).astype(F32)
        ea = (lane == GDN_H + h).astype(F32)
        b = jnp.sum(ba * eb, -1, keepdims=True)
        a = jnp.sum(ba * ea, -1, keepdims=True)
        al = jnp.sum(alog * eb, -1, keepdims=True)
        dt = jnp.sum(dtb * eb, -1, keepdims=True)
        g = -jnp.exp(al) * _softplus(a + dt)
        gs.append(jnp.broadcast_to(g, ba.shape))
        bs.append(jnp.broadcast_to(jax.nn.sigmoid(b), ba.shape))
    return jnp.concatenate(gs, 1), jnp.concatenate(bs, 1)


def _shift_dn(x, k):
    rows = lax.broadcasted_iota(I32, x.shape, 0)
    return jnp.where(rows < k, 0.0, pltpu.roll(x, k, 0))


def _shift_up(x, k):
    n = x.shape[0]
    rows = lax.broadcasted_iota(I32, x.shape, 0)
    return jnp.where(rows >= n - k, 0.0, pltpu.roll(x, n - k, 0))


def _pool_window(j):
    g = lax.div(j, POOL_GROUP // LANES)
    return jnp.where(g == 0, 2.0, jnp.where(g == 1, 4.0, jnp.where(g == 2, 8.0, 16.0))), g


def _pick(g, a2, a4, a8, a16):
    return jnp.where(g == 0, a2, jnp.where(g == 1, a4, jnp.where(g == 2, a8, a16)))


def pool_time_fwd(proj, name):
    S_ = proj.shape[0]

    def body(u_ref, p_ref):
        u = u_ref[...].astype(F32)
        w, g = _pool_window(pl.program_id(0))
        s2 = u + _shift_dn(u, 1)
        s4 = s2 + _shift_dn(s2, 2)
        s8 = s4 + _shift_dn(s4, 4)
        s16 = s8 + _shift_dn(s8, 8)
        t1 = (lax.broadcasted_iota(I32, u.shape, 0) + 1).astype(F32)
        p_ref[...] = (_pick(g, s2, s4, s8, s16) / jnp.minimum(t1, w) - u).astype(p_ref.dtype)

    return _pc(body, grid=(POOL_WIDTH // LANES,), in_specs=[pl.BlockSpec((S_, LANES), lambda j: (0, j))],
               out_specs=pl.BlockSpec((S_, LANES), lambda j: (0, j)), out_shape=_sds((S_, POOL_WIDTH), BF16),
               compiler_params=_cparams(("parallel",)), name=name)(proj)


def pool_time_bwd(dp, into, name):
    S_ = dp.shape[0]

    def body(dp_ref, _, du_ref):
        d = dp_ref[...].astype(F32)
        w, g = _pool_window(pl.program_id(0))
        t1 = (lax.broadcasted_iota(I32, d.shape, 0) + 1).astype(F32)
        q = d / jnp.minimum(t1, w)
        r2 = q + _shift_up(q, 1)
        r4 = r2 + _shift_up(r2, 2)
        r8 = r4 + _shift_up(r4, 4)
        r16 = r8 + _shift_up(r8, 8)
        du_ref[...] = (_pick(g, r2, r4, r8, r16) - d).astype(du_ref.dtype)

    return _pc(body, grid=(POOL_WIDTH // LANES,),
               in_specs=[pl.BlockSpec((S_, LANES), lambda j: (0, j)), pl.BlockSpec(memory_space=pl.ANY)],
               out_specs=pl.BlockSpec((S_, LANES), lambda j: (0, j)), out_shape=_sds(into.shape, into.dtype),
               input_output_aliases={1: 0}, compiler_params=_cparams(("parallel",)), name=name)(dp, into)


def _conv_post(j, a):
    n = a * lax.rsqrt(jnp.sum(a * a, axis=-1, keepdims=True) + EPS)
    nq = GDN_QK // LANES
    return jnp.where(j < nq, n * (GDN_DK ** -0.5), jnp.where(j < 2 * nq, n, a))


def _conv_taps(u):
    return [_shift_dn(u, 3), _shift_dn(u, 2), _shift_dn(u, 1), u]


def _conv_pre(taps, w):
    return w[0:1] * taps[0] + w[1:2] * taps[1] + w[2:3] * taps[2] + w[3:4] * taps[3]


def gdn_conv_fwd(proj, conv_w, name):
    S_ = proj.shape[0]

    def body(u_ref, w_ref, o_ref):
        o_ref[...] = _conv_post(pl.program_id(0), _silu(_conv_pre(_conv_taps(u_ref[...]), w_ref[...])))

    return _pc(body, grid=(GDN_CONV_CH // LANES,),
               in_specs=[pl.BlockSpec((S_, LANES), lambda j: (0, j)), pl.BlockSpec((8, LANES), lambda j: (0, j))],
               out_specs=pl.BlockSpec((S_, LANES), lambda j: (0, j)), out_shape=_sds((S_, GDN_CONV_CH), F32),
               compiler_params=_cparams(("parallel",)), name=name)(proj, conv_w)


def gdn_conv_bwd(proj, conv_w, dq, dk, dv, into, name):
    S_ = proj.shape[0]
    nq = GDN_QK // LANES

    def body(u_ref, w_ref, dq_ref, dk_ref, dv_ref, _, du_ref, dw_ref):
        j = pl.program_id(0)
        u, w = u_ref[...], w_ref[...]
        taps = _conv_taps(u)
        c = _conv_pre(taps, w)
        sig = jax.nn.sigmoid(c)
        dout = jnp.where(j < nq, dq_ref[...], jnp.where(j < 2 * nq, dk_ref[...], dv_ref[...]))
        _, vjp = jax.vjp(lambda a: _conv_post(j, a), c * sig)
        dc = vjp(dout)[0] * (sig * (1.0 + c * (1.0 - sig)))
        du = w[3:4] * dc + w[2:3] * _shift_up(dc, 1) + w[1:2] * _shift_up(dc, 2) + w[0:1] * _shift_up(dc, 3)
        du_ref[...] = du.astype(du_ref.dtype)
        rows = lax.broadcasted_iota(I32, (8, LANES), 0)
        dw = jnp.zeros((8, LANES), F32)
        for k in range(4):
            dw = dw + jnp.where(rows == k, jnp.sum(dc * taps[k], axis=0, keepdims=True), 0.0)
        dw_ref[...] = dw

    blk = lambda f: pl.BlockSpec((S_, LANES), f)
    return _pc(body, grid=(GDN_CONV_CH // LANES,),
               in_specs=[blk(lambda j: (0, j)), pl.BlockSpec((8, LANES), lambda j: (0, j)),
                         blk(lambda j: (0, jnp.minimum(j, nq - 1))), blk(lambda j: (0, jnp.clip(j - nq, 0, nq - 1))),
                         blk(lambda j: (0, jnp.clip(j - 2 * nq, 0, 2 * nq - 1))), pl.BlockSpec(memory_space=pl.ANY)],
               out_specs=[blk(lambda j: (0, j)), pl.BlockSpec((8, LANES), lambda j: (0, j))],
               out_shape=[_sds(into.shape, into.dtype), _sds((8, GDN_CONV_CH), F32)], input_output_aliases={5: 0},
               compiler_params=_cparams(("parallel",)), name=name)(proj, conv_w, dq, dk, dv, into)


_NN, _NT, _TN = ((1,), (0,)), ((1,), (1,)), ((0,), (0,))


def _split(x, n):
    parts = []
    for _ in range(n):
        h = x.astype(BF16)
        parts.append(h)
        x = x - h.astype(F32)
    return parts


def _dot(a, b, dn, mode):
    d = lambda p, q: lax.dot_general(p, q, (dn, ((), ())), preferred_element_type=F32)
    if mode == "lo":
        return d(a.astype(BF16), b.astype(BF16))
    if mode == "x3":
        (ah, al), (bh, bl) = _split(a, 2), _split(b, 2)
        return d(ah, bh) + (d(ah, bl) + d(al, bh))
    b0, b1, b2 = _split(b, 3)
    ab = a.astype(BF16)
    return d(ab, b0) + (d(ab, b1) + d(ab, b2))


def _make_dots(mode):
    @jax.custom_vjp
    def nn(a, b):
        return _dot(a, b, _NN, mode)

    @jax.custom_vjp
    def nt(a, b):
        return _dot(a, b, _NT, mode)

    @jax.custom_vjp
    def tn(a, b):
        return _dot(a, b, _TN, mode)

    nn.defvjp(lambda a, b: (nn(a, b), (a, b)), lambda r, d: (nt(d, r[1]), tn(r[0], d)))
    nt.defvjp(lambda a, b: (nt(a, b), (a, b)), lambda r, d: (nn(d, r[1]), tn(d, r[0])))
    tn.defvjp(lambda a, b: (tn(a, b), (a, b)), lambda r, d: (nt(r[1], d), nn(r[0], d)))
    return nn, nt, tn


_nn_hi, _nt_hi, _tn_hi = _make_dots("x3")
_nn_lo, _nt_lo, _tn_lo = _make_dots("lo")


@jax.custom_vjp
def _nn_const(a, b):
    return _dot(a, b, _NN, "xl")


_nn_const.defvjp(lambda a, b: (_nn_const(a, b), a), lambda a, d: (jnp.zeros_like(a), _dot(a, d, _TN, "xl")))


def _each(f, *lists):
    return [f(*xs) for xs in zip(*lists)]


@jax.custom_vjp
def _unit_inverses(xs):
    C = xs[0].shape[0]
    eye = (lax.broadcasted_iota(I32, (C, C), 0) == lax.broadcasted_iota(I32, (C, C), 1)).astype(F32)
    ainv, p = [eye + a for a in xs], xs
    for _ in range(int(math.log2(C)) - 1):
        p = _each(lambda a: _dot(a, a, _NN, "x3"), p)
        ainv = _each(lambda a, b: a + _dot(a, b, _NN, "x3"), ainv, p)
    return ainv


def _unit_inverses_bwd(ainv, d):
    left = _each(lambda a, g: _dot(a, g, _TN, "x3"), ainv, d)
    return (_each(lambda t, a: _dot(t, a, _NT, "x3"), left, ainv),)


_unit_inverses.defvjp(lambda xs: (lambda a: (a, a))(_unit_inverses(xs)), _unit_inverses_bwd)


def _gdn_chunk(q, k, v, gb, bb, state):
    C = GDN_C
    e0 = (lax.broadcasted_iota(I32, (1, LANES), 1) == 0).astype(F32)
    ri = lax.broadcasted_iota(I32, (C, C), 0)
    ci = lax.broadcasted_iota(I32, (C, C), 1)
    causal, strict = ri >= ci, ri > ci
    tri, eye, ones = causal.astype(F32), (ri == ci).astype(F32), jnp.ones((C, C), F32)
    last = lax.broadcasted_iota(I32, (C, LANES), 0) == C - 1
    g1 = _each(lambda a: jnp.sum(a * e0, -1, keepdims=True), gb)
    b1 = _each(lambda a: jnp.sum(a * e0, -1, keepdims=True), bb)
    gc_c = _each(lambda g: _nn_const(tri, jnp.broadcast_to(g, (C, C))), g1)
    gc_d = _each(lambda g: _nn_const(tri, jnp.broadcast_to(g, (C, LANES))), g1)
    gr_c = _each(lambda g: _nn_const(ones, eye * g), gc_c)
    decay = _each(lambda a, r: jnp.where(causal, jnp.exp(jnp.where(causal, a - r, 0.0)), 0.0), gc_c, gr_c)
    kb = _each(lambda a, b: a * b, k, b1)
    vb = _each(lambda a, b: a * b, v, b1)
    x = _each(lambda a, b, d: -jnp.where(strict, _nt_lo(a, b) * d, 0.0), kb, k, decay)
    ainv = _unit_inverses(x)
    u = _each(_nn_hi, ainv, vb)
    w = _each(lambda a, b, g: _nn_hi(a, b * jnp.exp(g)), ainv, kb, gc_d)
    attn = _each(lambda a, b, d: jnp.where(causal, _nt_lo(a, b) * d, 0.0), q, k, decay)
    v_new = _each(lambda a, b, s: a - _nn_lo(b, s), u, w, state)
    o = _each(lambda a, g, s, t, vn: _nn_lo(a * jnp.exp(g), s) + _nn_lo(t, vn), q, gc_d, state, attn, v_new)
    gl = _each(lambda g: jnp.sum(jnp.where(last, g, 0.0), axis=0, keepdims=True), gc_d)
    new_state = _each(lambda s, g, a, gd, vn: s * jnp.exp(jnp.sum(g * e0, -1, keepdims=True)) + _tn_lo(a * jnp.exp(g - gd), vn),
                      state, gl, k, gc_d, v_new)
    return o, new_state


def _head_slices(ref, width):
    return [ref[:, h * width:(h + 1) * width] for h in range(GDN_H)]


def gdn_chunk_fwd(qkv, g_b, beta_b, name):
    S_ = qkv.shape[0]
    N = S_ // GDN_C

    def body(q_ref, k_ref, v_ref, g_ref, b_ref, o_ref, s_ref, state):
        @pl.when(pl.program_id(0) == 0)
        def _():
            state[...] = jnp.zeros_like(state)

        st = [state[h] for h in range(GDN_H)]
        s_ref[0] = state[...]
        o, st2 = _gdn_chunk(_head_slices(q_ref, GDN_DK), _head_slices(k_ref, GDN_DK), _head_slices(v_ref, GDN_DV),
                            _head_slices(g_ref, GDN_DK), _head_slices(b_ref, GDN_DK), st)
        for h in range(GDN_H):
            o_ref[:, h * GDN_DV:(h + 1) * GDN_DV] = o[h]
            state[h] = st2[h]

    return _pc(body, grid=(N,),
               in_specs=[pl.BlockSpec((GDN_C, GDN_QK), lambda n: (n, 0)), pl.BlockSpec((GDN_C, GDN_QK), lambda n: (n, 1)),
                         pl.BlockSpec((GDN_C, GDN_V), lambda n: (n, 1)), pl.BlockSpec((GDN_C, GDN_QK), lambda n: (n, 0)),
                         pl.BlockSpec((GDN_C, GDN_QK), lambda n: (n, 0))],
               out_specs=[pl.BlockSpec((GDN_C, GDN_V), lambda n: (n, 0)),
                          pl.BlockSpec((1, GDN_H, GDN_DK, GDN_DV), lambda n: (n, 0, 0, 0))],
               out_shape=[_sds((S_, GDN_V), F32), _sds((N, GDN_H, GDN_DK, GDN_DV), F32)],
               scratch_shapes=[pltpu.VMEM((GDN_H, GDN_DK, GDN_DV), F32)],
               compiler_params=_cparams(("arbitrary",)), name=name)(qkv, qkv, qkv, g_b, beta_b)


def gdn_chunk_bwd(qkv, g_b, beta_b, states, do, name):
    S_ = qkv.shape[0]
    N = S_ // GDN_C

    def body(q_ref, k_ref, v_ref, g_ref, b_ref, s_ref, do_ref, dq_ref, dk_ref, dv_ref, dg_ref, db_ref, dstate):
        @pl.when(pl.program_id(0) == 0)
        def _():
            dstate[...] = jnp.zeros_like(dstate)

        _, vjp = jax.vjp(_gdn_chunk, _head_slices(q_ref, GDN_DK), _head_slices(k_ref, GDN_DK), _head_slices(v_ref, GDN_DV),
                         _head_slices(g_ref, GDN_DK), _head_slices(b_ref, GDN_DK), [s_ref[0, h] for h in range(GDN_H)])
        dq, dk, dv, dg, db, ds = vjp((_head_slices(do_ref, GDN_DV), [dstate[h] for h in range(GDN_H)]))
        for h in range(GDN_H):
            kk, vv = slice(h * GDN_DK, (h + 1) * GDN_DK), slice(h * GDN_DV, (h + 1) * GDN_DV)
            dq_ref[:, kk] = dq[h]
            dk_ref[:, kk] = dk[h]
            dv_ref[:, vv] = dv[h]
            dg_ref[:, kk] = dg[h]
            db_ref[:, kk] = db[h]
            dstate[h] = ds[h]

    r = lambda n: N - 1 - n
    qk = lambda c: pl.BlockSpec((GDN_C, GDN_QK), lambda n: (r(n), c))
    vs = lambda c: pl.BlockSpec((GDN_C, GDN_V), lambda n: (r(n), c))
    return _pc(body, grid=(N,),
               in_specs=[qk(0), qk(1), vs(1), qk(0), qk(0),
                         pl.BlockSpec((1, GDN_H, GDN_DK, GDN_DV), lambda n: (r(n), 0, 0, 0)), vs(0)],
               out_specs=[qk(0), qk(0), vs(0), qk(0), qk(0)],
               out_shape=[_sds((S_, GDN_QK), F32), _sds((S_, GDN_QK), F32), _sds((S_, GDN_V), F32),
                          _sds((S_, GDN_QK), F32), _sds((S_, GDN_QK), F32)],
               scratch_shapes=[pltpu.VMEM((GDN_H, GDN_DK, GDN_DV), F32)],
               compiler_params=_cparams(("arbitrary",)), name=name)(qkv, qkv, qkv, g_b, beta_b, states, do)


def _rope_tables(pos_ref, inv_ref, cm_ref, sg_ref):
    ang = pos_ref[...] * inv_ref[...]
    return jnp.cos(ang) * cm_ref[...], jnp.sin(ang) * sg_ref[...]


def mla_prep_fwd(qpad, kv, proj, pos, rope_consts, name):
    S_ = qpad.shape[0]
    ts = 256
    W = 2 * LANES

    def body(q_ref, kv_ref, kr_ref, pos_ref, inv_ref, cm_ref, sg_ref, qh_ref, kh_ref, vh_ref):
        cs, sn = _rope_tables(pos_ref, inv_ref, cm_ref, sg_ref)
        rope = lambda r: r * cs + pltpu.roll(r, LANES // 2, 1) * sn
        krr = rope(kr_ref[...].astype(F32)).astype(BF16)
        for h in range(MLA_H):
            qh_ref[h, :, 0:LANES] = (q_ref[:, h * W:h * W + LANES].astype(F32) * MLA_SCALE).astype(BF16)
            qh_ref[h, :, LANES:W] = (rope(q_ref[:, h * W + LANES:(h + 1) * W].astype(F32)) * MLA_SCALE).astype(BF16)
            kh_ref[h, :, 0:LANES] = kv_ref[:, h * W:h * W + LANES].astype(BF16)
            kh_ref[h, :, LANES:W] = krr
            vh_ref[h] = kv_ref[:, h * W + LANES:(h + 1) * W].astype(BF16)

    one = pl.BlockSpec((1, LANES), lambda i: (0, 0))
    return _pc(body, grid=(S_ // ts,),
               in_specs=[pl.BlockSpec((ts, MLA_H * W), lambda i: (i, 0)), pl.BlockSpec((ts, MLA_H * W), lambda i: (i, 0)),
                         pl.BlockSpec((ts, LANES), lambda i: (i, 1536 // LANES)), pl.BlockSpec((ts, 1), lambda i: (i, 0)),
                         one, one, one],
               out_specs=[pl.BlockSpec((MLA_H, ts, W), lambda i: (0, i, 0)), pl.BlockSpec((MLA_H, ts, W), lambda i: (0, i, 0)),
                          pl.BlockSpec((MLA_H, ts, LANES), lambda i: (0, i, 0))],
               out_shape=[_sds((MLA_H, S_, W), BF16), _sds((MLA_H, S_, W), BF16), _sds((MLA_H, S_, LANES), BF16)],
               compiler_params=_cparams(("parallel",)), name=name)(qpad, kv, proj, pos, *rope_consts)


def mla_prep_bwd(dqh, dkh, dvh, pos, rope_consts, into, name):
    S_ = dqh.shape[1]
    ts = 256
    W = 2 * LANES

    def body(dq_ref, dk_ref, dv_ref, pos_ref, inv_ref, cm_ref, sg_ref, _, dqp_ref, dkv_ref, dkr_ref):
        cs, sn = _rope_tables(pos_ref, inv_ref, cm_ref, sg_ref)
        rope_t = lambda g: g * cs + pltpu.roll(g * sn, LANES // 2, 1)
        acc = jnp.zeros((ts, LANES), F32)
        for h in range(MLA_H):
            dqp_ref[:, h * W:h * W + LANES] = (dq_ref[h, :, 0:LANES].astype(F32) * MLA_SCALE).astype(BF16)
            dqp_ref[:, h * W + LANES:(h + 1) * W] = (rope_t(dq_ref[h, :, LANES:W].astype(F32)) * MLA_SCALE).astype(BF16)
            dkv_ref[:, h * W:h * W + LANES] = dk_ref[h, :, 0:LANES]
            dkv_ref[:, h * W + LANES:(h + 1) * W] = dv_ref[h]
            acc = acc + dk_ref[h, :, LANES:W].astype(F32)
        dkr_ref[...] = rope_t(acc).astype(dkr_ref.dtype)

    one = pl.BlockSpec((1, LANES), lambda i: (0, 0))
    return _pc(body, grid=(S_ // ts,),
               in_specs=[pl.BlockSpec((MLA_H, ts, W), lambda i: (0, i, 0)), pl.BlockSpec((MLA_H, ts, W), lambda i: (0, i, 0)),
                         pl.BlockSpec((MLA_H, ts, LANES), lambda i: (0, i, 0)), pl.BlockSpec((ts, 1), lambda i: (i, 0)),
                         one, one, one, pl.BlockSpec(memory_space=pl.ANY)],
               out_specs=[pl.BlockSpec((ts, MLA_H * W), lambda i: (i, 0)), pl.BlockSpec((ts, MLA_H * W), lambda i: (i, 0)),
                          pl.BlockSpec((ts, LANES), lambda i: (i, 1536 // LANES))],
               out_shape=[_sds((S_, MLA_H * W), BF16), _sds((S_, MLA_H * W), BF16), _sds(into.shape, into.dtype)],
               input_output_aliases={7: 2}, compiler_params=_cparams(("parallel",)), name=name)(dqh, dkh, dvh, pos, *rope_consts, into)


NEG = -1e30


FLASH_TILE = 1024
FLASH_SUB = 512


def _scores(q, k, diagonal):
    s = lax.dot_general(q, k, (_NT, ((), ())), preferred_element_type=F32)
    if not diagonal:
        return s
    return jnp.where(lax.broadcasted_iota(I32, s.shape, 1) <= lax.broadcasted_iota(I32, s.shape, 0), s, NEG)


def _sub_blocks(t, diagonal):
    sub = min(FLASH_SUB, t) if diagonal else t
    return [(c * sub if diagonal else 0, slice(c * sub, (c + 1) * sub)) for c in range(t // sub)]


FLASH_HEADS = 2


def flash_fwd(qh, kh, vh, name):
    H, S_, W = qh.shape
    t = _tile(S_, FLASH_TILE)
    n = S_ // t
    G = FLASH_HEADS
    heads = list(range(G))

    def body(q_ref, k_ref, v_ref, o_ref, lse_ref, m_s, l_s, acc):
        qi, kj = pl.program_id(1), pl.program_id(2)

        @pl.when(kj == 0)
        def _():
            m_s[...] = jnp.full_like(m_s, NEG)
            l_s[...] = jnp.zeros_like(l_s)
            acc[...] = jnp.zeros_like(acc)

        def step(diagonal):
            s = _each(lambda a: _scores(q_ref[a], k_ref[a], diagonal), heads)
            m_old = _each(lambda a: m_s[a], heads)
            m_new = _each(lambda mo, sa: jnp.maximum(mo, jnp.max(sa, axis=-1, keepdims=True)), m_old, s)
            alpha = _each(lambda mo, mn: jnp.exp(mo - mn), m_old, m_new)
            p = _each(lambda sa, mn: jnp.exp(sa - mn[:, :1]), s, m_new)
            pv = _each(lambda pa, a: lax.dot_general(pa.astype(BF16), v_ref[a], (_NN, ((), ())), preferred_element_type=F32), p, heads)
            for a in heads:
                l_s[a] = alpha[a] * l_s[a] + jnp.sum(p[a], axis=-1, keepdims=True)
                acc[a] = alpha[a] * acc[a] + pv[a]
                m_s[a] = m_new[a]

        pl.when(kj < qi)(lambda: step(False))
        pl.when(kj == qi)(lambda: step(True))

        @pl.when(kj == n - 1)
        def _():
            for a in heads:
                o_ref[:, a * LANES:(a + 1) * LANES] = (acc[a] / l_s[a]).astype(o_ref.dtype)
                lse_ref[a] = m_s[a] + jnp.log(l_s[a])

    return _pc(body, grid=(H // G, n, n),
               in_specs=[pl.BlockSpec((G, t, W), lambda h, i, j: (h, i, 0)),
                         pl.BlockSpec((G, t, W), lambda h, i, j: (h, jnp.minimum(i, j), 0)),
                         pl.BlockSpec((G, t, LANES), lambda h, i, j: (h, jnp.minimum(i, j), 0))],
               out_specs=[pl.BlockSpec((t, G * LANES), lambda h, i, j: (i, h)), pl.BlockSpec((G, t, LANES), lambda h, i, j: (h, i, 0))],
               out_shape=[_sds((S_, H * LANES), BF16), _sds((H, S_, LANES), F32)],
               scratch_shapes=[pltpu.VMEM((G, t, LANES), F32)] * 3,
               compiler_params=_cparams(("parallel", "parallel", "arbitrary")), name=name)(qh, kh, vh)


def flash_bwd(qh, kh, vh, o, lse, do, name):
    H, S_, W = qh.shape
    t = _tile(S_, FLASH_TILE)
    n = S_ // t

    def body(q_ref, k_ref, v_ref, o_ref, lse_ref, do_ref, dq_ref, dk_ref, dv_ref, dq_acc, dk_acc, dv_acc):
        kj, qi = pl.program_id(1), pl.program_id(2)

        @pl.when(jnp.logical_and(kj == 0, qi == 0))
        def _():
            dq_acc[...] = jnp.zeros_like(dq_acc)

        @pl.when(qi == 0)
        def _():
            dk_acc[...] = jnp.zeros_like(dk_acc)
            dv_acc[...] = jnp.zeros_like(dv_acc)

        def step(diagonal):
            do_ = do_ref[...]
            dob = do_.astype(BF16)
            delta = jnp.sum(do_.astype(F32) * o_ref[...].astype(F32), axis=-1, keepdims=True)
            for r0, keys in _sub_blocks(t, diagonal):
                q, k, v = q_ref[r0:, :], k_ref[keys, :], v_ref[keys, :]
                p = jnp.exp(_scores(q, k, diagonal) - lse_ref[r0:, :1])
                dv_acc[keys, :] += lax.dot_general(p.astype(BF16), dob[r0:], (_TN, ((), ())), preferred_element_type=F32)
                dp = lax.dot_general(dob[r0:], v, (_NT, ((), ())), preferred_element_type=F32)
                ds = (p * (dp - delta[r0:])).astype(BF16)
                dk_acc[keys, :] += lax.dot_general(ds, q, (_TN, ((), ())), preferred_element_type=F32)
                rows = pl.ds(pl.multiple_of(qi * t, t) + r0, t - r0)
                dq_acc[rows, :] += lax.dot_general(ds, k, (_NN, ((), ())), preferred_element_type=F32)

        pl.when(qi > kj)(lambda: step(False))
        pl.when(qi == kj)(lambda: step(True))

        @pl.when(qi == n - 1)
        def _():
            dk_ref[...] = dk_acc[...].astype(BF16)
            dv_ref[...] = dv_acc[...].astype(BF16)

        @pl.when(jnp.logical_and(kj == n - 1, qi == n - 1))
        def _():
            dq_ref[...] = dq_acc[...].astype(BF16)

    qrow = lambda h, j, i: jnp.maximum(i, j)
    return _pc(body, grid=(H, n, n),
               in_specs=[pl.BlockSpec((None, t, W), lambda h, j, i: (h, qrow(h, j, i), 0)),
                         pl.BlockSpec((None, t, W), lambda h, j, i: (h, j, 0)),
                         pl.BlockSpec((None, t, LANES), lambda h, j, i: (h, j, 0)),
                         pl.BlockSpec((t, LANES), lambda h, j, i: (qrow(h, j, i), h)),
                         pl.BlockSpec((None, t, LANES), lambda h, j, i: (h, qrow(h, j, i), 0)),
                         pl.BlockSpec((t, LANES), lambda h, j, i: (qrow(h, j, i), h))],
               out_specs=[pl.BlockSpec((None, S_, W), lambda h, j, i: (h, 0, 0)),
                          pl.BlockSpec((None, t, W), lambda h, j, i: (h, j, 0)),
                          pl.BlockSpec((None, t, LANES), lambda h, j, i: (h, j, 0))],
               out_shape=[_sds((H, S_, W), BF16), _sds((H, S_, W), BF16), _sds((H, S_, LANES), BF16)],
               scratch_shapes=[pltpu.VMEM((S_, W), F32), pltpu.VMEM((t, W), F32), pltpu.VMEM((t, LANES), F32)],
               compiler_params=_cparams(("parallel", "arbitrary", "arbitrary")), name=name)(qh, kh, vh, o, lse, do)


def loss_head(x, target, g, name):
    S_ = x.shape[0]
    ts = 256

    def body(x_ref, t_ref, g_ref, l_ref, dx_ref, dg_ref):
        @pl.when(pl.program_id(0) == 0)
        def _():
            l_ref[...] = jnp.zeros_like(l_ref)
            dg_ref[...] = jnp.zeros_like(dg_ref)

        y, vjp = jax.vjp(_rms, x_ref[...], g_ref[...])
        err = y - t_ref[...]
        l_ref[...] += 0.5 * jnp.sum(jnp.sum(err * err, axis=-1, keepdims=True), axis=0, keepdims=True) / D
        dx, dg = vjp(err / D)
        dx_ref[...] = dx
        dg_ref[...] += dg

    row = pl.BlockSpec((ts, D), lambda i: (i, 0))
    return _pc(body, grid=(S_ // ts,), in_specs=[row, row, pl.BlockSpec((1, D), lambda i: (0, 0))],
               out_specs=[pl.BlockSpec((1, LANES), lambda i: (0, 0)), row, pl.BlockSpec((1, D), lambda i: (0, 0))],
               out_shape=[_sds((1, LANES), F32), _sds((S_, D), F32), _sds((1, D), F32)],
               compiler_params=_cparams(("arbitrary",)), name=name)(x, target, g)


def adamw(w, parts, m, v, name):
    R, C = w.shape
    rows = [p.shape[1] for p in parts[0]]
    tr = R
    for cand in (512, 256, 128, 64, 32, 16, 8):
        if all(r % cand == 0 for r in rows) and cand * C * 4 <= 1024 * 1024:
            tr = cand
            break
    c1 = 1.0 - ADAM_B1 ** ADAM_STEP
    c2 = 1.0 - ADAM_B2 ** ADAM_STEP
    starts = [sum(rows[:k]) // tr for k in range(len(rows))]
    flat = [p for part in parts for p in part]

    def body(*refs):
        w_ref, m_ref, v_ref = refs[0], refs[1 + len(flat)], refs[2 + len(flat)]
        g_ref, d_ref, nm_ref, nv_ref = refs[3 + len(flat):]
        i = pl.program_id(0)
        gg, at = None, 1
        for part in parts:
            val = None
            for k in range(len(part)):
                p_ref = refs[at]
                at += 1
                s = p_ref[0].astype(F32)
                for n in range(1, p_ref.shape[0]):
                    s = s + p_ref[n].astype(F32)
                val = s if val is None else jnp.where(i >= starts[k], s, val)
            gg = val if gg is None else gg + val
        m2 = ADAM_B1 * m_ref[...] + (1.0 - ADAM_B1) * gg
        v2 = ADAM_B2 * v_ref[...] + (1.0 - ADAM_B2) * (gg * gg)
        g_ref[...] = gg
        d_ref[...] = -ADAM_LR * ((m2 / c1) / (jnp.sqrt(v2 / c2) + ADAM_EPS) + ADAM_WD * w_ref[...])
        nm_ref[...] = m2
        nv_ref[...] = v2

    blk = pl.BlockSpec((tr, C), lambda i: (i, 0))
    piece = lambda p, k: pl.BlockSpec((p.shape[0], tr, C), lambda i: (0, jnp.clip(i - starts[k], 0, rows[k] // tr - 1), 0))
    pblk = [piece(p, k) for part in parts for k, p in enumerate(part)]
    return _pc(body, grid=(R // tr,), in_specs=[blk] + pblk + [blk, blk], out_specs=[blk] * 4, out_shape=[_sds((R, C), F32)] * 4,
               compiler_params=_cparams(("parallel",)), name=name)(w, *flat, m, v)


def sum_slots(recv, name):
    n, R, C = recv.shape

    def body(r_ref, o_ref):
        acc = r_ref[0]
        for s in range(1, n):
            acc = acc + r_ref[s]
        o_ref[...] = acc

    return _pc(body, grid=(1,), in_specs=[pl.BlockSpec((n, R, C), lambda i: (0, 0, 0))],
               out_specs=pl.BlockSpec((R, C), lambda i: (0, 0)), out_shape=_sds((R, C), F32),
               compiler_params=_cparams(("arbitrary",)), name=name)(recv)


def _chip_peers():
    x, y, c = lax.axis_index("x"), lax.axis_index("y"), lax.axis_index("c")
    return (x, y, c), [(1 - x, y, c), (x, 1 - y, c), (1 - x, 1 - y, c)]


def _chip_index(p):
    return 2 * p[0] + p[1]


def _win(ref, axis, chip, size):
    if axis is None:
        return ref.at[chip]
    idx = [slice(None)] * len(ref.shape)
    idx[axis] = pl.ds(pl.multiple_of(chip * size, size), size)
    return ref.at[tuple(idx)]


def _remote(src, dst, send_sem, recv_sem, peer):
    return pltpu.make_async_remote_copy(src_ref=src, dst_ref=dst, send_sem=send_sem, recv_sem=recv_sem, device_id=peer,
                                        device_id_type=MESH)


HBM_SPEC = pl.BlockSpec(memory_space=pltpu.HBM)
SEM_SPEC = pl.BlockSpec(memory_space=pltpu.SEMAPHORE)
ANY_SPEC = pl.BlockSpec(memory_space=pl.ANY)
DATAFLOW = pltpu.SideEffectType.DATAFLOW_SIDE_EFFECTING


def gather_piece(i, l, o, axis, size):
    return (i, lambda r, chip: r.at[l], o, lambda r, chip: _win(r, axis, chip, size))


def scatter_piece(i, o, axis, size):
    return (i, lambda r, chip: _win(r, axis, chip, size), o, lambda r, chip: r.at[chip])


def whole_piece(i):
    return (i, lambda r, chip: r, i, lambda r, chip: r)


def _copies(pieces, in_refs, out_refs, send, recv, sibling):
    me, peers = _chip_peers()
    if sibling:
        peers = [(me[0], me[1], 1 - me[2])]
    mine = _chip_index(me)
    remote = []
    for n, (i, src, o, dst) in enumerate(pieces):
        d = dst(out_refs[o], mine)
        remote += [_remote(src(in_refs[i], _chip_index(p)), d, send.at[len(peers) * n + k], recv.at[len(peers) * n + k], p)
                   for k, p in enumerate(peers)]
    return remote


def own_window(a, axis, size, chip):
    if axis is None:
        return lax.dynamic_index_in_dim(a, chip, 0, keepdims=False)
    return lax.dynamic_slice_in_dim(a, chip * size, size, axis=axis)


def place_own(land, own, axis, size, chip):
    if axis is None:
        return lax.dynamic_update_slice_in_dim(land, own[None], chip, axis=0)
    return lax.dynamic_update_slice_in_dim(land, own, chip * size, axis=axis)


def exchange_start(pieces, ins, out_shapes, after, name, sibling=False):
    n_in, n_out, ncp = len(ins), len(out_shapes), len(pieces)

    def body(*refs):
        in_refs, land_refs = refs[:n_in], refs[n_in:n_in + n_out]
        send, recv = refs[n_in + n_out + 1], refs[n_in + n_out + 2]
        token = refs[-1]
        for cp in _copies(pieces, in_refs, land_refs, send, recv, sibling):
            cp.start()
        token[...] = jnp.zeros_like(token)

    hbm = lambda a: pltpu.with_memory_space_constraint(a, pltpu.HBM)
    lands = [hbm(lax.empty(s.shape, s.dtype)) for s in out_shapes]
    sem = pltpu.SemaphoreType.DMA(((1 if sibling else 3) * ncp,))
    thru = [pltpu.HBM(a.shape, a.dtype) for a in ins] + [pltpu.HBM(s.shape, s.dtype) for s in out_shapes]
    res = _pc(body, in_specs=[HBM_SPEC] * (n_in + n_out) + [ANY_SPEC],
              out_specs=[SEM_SPEC, SEM_SPEC] + [HBM_SPEC] * (n_in + n_out) + [pl.BlockSpec(memory_space=pltpu.VMEM)],
              out_shape=[sem, sem] + thru + [_sds((8, LANES), F32)],
              input_output_aliases={i: 2 + i for i in range(n_in + n_out)},
              compiler_params=pltpu.CompilerParams(has_side_effects=DATAFLOW), name=name)(*[hbm(a) for a in ins], *lands, after)
    return (res[0], res[1]), list(res[2:2 + n_in]), list(res[2 + n_in:2 + n_in + n_out]), res[-1]


def exchange_wait(pieces, sems, ins, lands, after, name, sibling=False):
    n_in, n_out = len(ins), len(lands)

    def body(*refs):
        in_refs, land_refs = refs[:n_in], refs[n_in:n_in + n_out]
        send, recv = refs[n_in + n_out], refs[n_in + n_out + 1]
        for cp in _copies(pieces, in_refs, land_refs, send, recv, sibling):
            cp.wait_send()
            cp.wait_recv()

    thru = [pltpu.HBM(a.shape, a.dtype) for a in ins] + [pltpu.HBM(a.shape, a.dtype) for a in lands]
    res = _pc(body, in_specs=[HBM_SPEC] * (n_in + n_out) + [SEM_SPEC, SEM_SPEC, ANY_SPEC], out_specs=[HBM_SPEC] * (n_in + n_out),
              out_shape=thru, input_output_aliases={i: i for i in range(n_in + n_out)},
              compiler_params=pltpu.CompilerParams(has_side_effects=DATAFLOW), name=name)(*ins, *lands, sems[0], sems[1], after)
    return list(res[:n_in]), list(res[n_in:])


def exchange_all(buf, name):
    def body(in_ref, out_ref, send, recv, local):
        x, y, c = lax.axis_index("x"), lax.axis_index("y"), lax.axis_index("c")
        mine = 4 * x + 2 * y + c
        loc = pltpu.make_async_copy(in_ref, out_ref.at[mine], local)
        loc.start()
        copies = [loc]
        for k in range(1, 8):
            peer = (x ^ (k >> 2), y ^ ((k >> 1) & 1), c ^ (k & 1))
            cp = pltpu.make_async_remote_copy(src_ref=in_ref, dst_ref=out_ref.at[mine], send_sem=send.at[k - 1],
                                              recv_sem=recv.at[k - 1], device_id=peer, device_id_type=MESH)
            cp.start()
            copies.append(cp)
        for cp in copies:
            cp.wait()

    anyspec = pl.BlockSpec(memory_space=pl.ANY)
    return _pc(body, in_specs=[anyspec], out_specs=anyspec, out_shape=_sds((8,) + buf.shape, buf.dtype),
               scratch_shapes=[pltpu.SemaphoreType.DMA((7,)), pltpu.SemaphoreType.DMA((7,)), pltpu.SemaphoreType.DMA],
               name=name)(buf)


def _norm_fwd(x, g, name):
    return rowwise(f_rms, [(x, D, 0, 0)], [(g, D, 0, 0)], [(D, 0, BF16)], ts=512, name=name)[0]


def _norm_bwd(x, g, dh, dres, name):
    (dx,), (dg,) = rowwise_bwd(f_rms, [(x, D, 0, 0)], [(g, D, 0, 0)], [(dh, D, 0, 0)], need=[True],
                               adds={0: (dres, D, 0, 0)}, ts=512, name=name)
    return dx, dg


def pool_fwd(x, W, tag, late=None):
    h = _norm_fwd(x, W["ng"], tag + "_norm")
    proj = mm(h, W["w_in"], out_dtype=BF16, name=tag + "_in")
    if late is not None:
        W = dict(W, **late(proj))
    p = pool_time_fwd(proj, tag + "_win")
    pg = gmm("nn", p, W["w_grp"], G=4, out_dtype=BF16, name=tag + "_grp")
    y = rowwise(f_pool_gate, [(pg, POOL_GROUP, 0, 1), (proj, POOL_GROUP, 4, 1)], [(W["scale"], POOL_GROUP, 0, 1)],
                [(POOL_GROUP, 1, BF16)], ncol=4, ts=1024, name=tag + "_gate")[0]
    xn = mm(y, W["w_out"], add=x, name=tag + "_out")
    return xn, (x, h, proj, p, pg, y)


def pool_bwd(dxn, W, saved, tag, after=None, emit=None):
    x, h, proj, p, pg, y = saved
    emit = emit or (lambda grads: None)
    dy = mm(dxn, W["w_out"], tb=True, after=after, out_dtype=BF16, name=tag + "_dy")
    g = {}
    (dpg, dproj), (g["scale"],) = rowwise_bwd(
        f_pool_gate, [(pg, POOL_GROUP, 0, 1), (proj, POOL_GROUP, 4, 1)], [(W["scale"], POOL_GROUP, 0, 1)],
        [(dy, POOL_GROUP, 0, 1)], need=[True, True], place={1: (2 * POOL_WIDTH, 4)}, narrow=(0, 1), ncol=4, ts=1024, name=tag + "_dgate")
    dp = gmm("nt", dpg, W["w_grp"], G=4, out_dtype=BF16, name=tag + "_dp")
    dproj = pool_time_bwd(dp, dproj, tag + "_dwin")
    g["w_in"] = mm(h, dproj, ta=True, out_dtype=BF16, name=tag + "_dw_in")
    t1 = emit({"w_in": g["w_in"]})
    g["w_out"] = mm(y, dxn, ta=True, after=t1, out_dtype=BF16, name=tag + "_dwout")
    g["w_grp"] = gmm("tn", p, dpg, G=4, out_dtype=BF16, name=tag + "_dwgrp")
    t2 = emit({"w_out": g["w_out"], "w_grp": g["w_grp"]})
    dh = mm(dproj, W["w_in"], tb=True, after=t2, name=tag + "_dh")
    dx, g["ng"] = _norm_bwd(x, W["ng"], dh, dxn, tag + "_dnorm")
    return dx, g


def gdn_fwd(x, W, tag, late=None):
    h = _norm_fwd(x, W["ng"], tag + "_norm")
    proj = mm(h, W["w_in"], name=tag + "_in")
    qkv = gdn_conv_fwd(proj, W["conv"], tag + "_conv")
    g_b, beta_b = rowwise(f_gdn_gates, [(proj, LANES, 6144 // LANES, 0)], [(W["a_log"], LANES, 0, 0), (W["dt_bias"], LANES, 0, 0)],
                          [(GDN_QK, 0, F32), (GDN_QK, 0, F32)], ts=512, name=tag + "_gates")
    o, states = gdn_chunk_fwd(qkv, g_b, beta_b, tag + "_chunk")
    og = rowwise(f_gdn_out, [(o, GDN_DV, 0, 1), (proj, GDN_DV, 4096 // GDN_DV, 1)], [(W["norm_g"], GDN_DV, 0, 0)],
                 [(GDN_DV, 1, BF16)], ncol=GDN_H, ts=2048, name=tag + "_onorm")[0]
    if late is not None:
        W = dict(W, **late(og))
    xn = mm(og, W["w_out"], add=x, name=tag + "_out")
    return xn, (x, h, proj, qkv, g_b, beta_b, o, states, og)


def gdn_bwd(dxn, W, saved, tag, after=None):
    x, h, proj, qkv, g_b, beta_b, o, states, og = saved
    dog = mm(dxn, W["w_out"], tb=True, after=after, out_dtype=BF16, name=tag + "_dog")
    g = {"w_out": mm(og, dxn, ta=True, out_dtype=BF16, name=tag + "_dwout")}
    (do, dproj), (g["norm_g"],) = rowwise_bwd(
        f_gdn_out, [(o, GDN_DV, 0, 1), (proj, GDN_DV, 4096 // GDN_DV, 1)], [(W["norm_g"], GDN_DV, 0, 0)],
        [(dog, GDN_DV, 0, 1)], need=[True, True], place={1: (GDN_IN_PAD, 4096 // GDN_DV)}, narrow=(1,), ncol=GDN_H, ts=2048, name=tag + "_donorm")
    dq, dk, dv, dg_b, dbeta_b = gdn_chunk_bwd(qkv, g_b, beta_b, states, do, tag + "_dchunk")
    (dproj,), (g["a_log"], g["dt_bias"]) = rowwise_bwd(
        f_gdn_gates, [(proj, LANES, 6144 // LANES, 0)], [(W["a_log"], LANES, 0, 0), (W["dt_bias"], LANES, 0, 0)],
        [(dg_b, GDN_QK, 0, 0), (dbeta_b, GDN_QK, 0, 0)], need=[True], place={0: (dproj, 6144 // LANES)}, ts=512, name=tag + "_dgates")
    dproj, g["conv"] = gdn_conv_bwd(proj, W["conv"], dq, dk, dv, dproj, tag + "_dconv")
    dh = mm(dproj, W["w_in"], tb=True, name=tag + "_dh")
    g["w_in"] = mm(h, dproj, ta=True, out_dtype=BF16, name=tag + "_dw_in")
    dx, g["ng"] = _norm_bwd(x, W["ng"], dh, dxn, tag + "_dnorm")
    return dx, g


def mla_fwd(x, pos, W, tag):
    h = _norm_fwd(x, W["ng"], tag + "_norm")
    proj = mm(h, W["w_in"], out_dtype=BF16, name=tag + "_in")
    hq = rowwise(f_rms, [(proj, MLA_Q_LORA, 0, 0)], [(W["q_g"], MLA_Q_LORA, 0, 0)], [(MLA_Q_LORA, 0, BF16)], ts=512, name=tag + "_qnorm")[0]
    hkv = rowwise(f_rms, [(proj, MLA_KV_LORA, 2, 0)], [(W["kv_g"], MLA_KV_LORA, 0, 0)], [(MLA_KV_LORA, 0, BF16)], ts=512, name=tag + "_kvnorm")[0]
    qpad = mm(hq, W["w_uq"], out_dtype=BF16, name=tag + "_uq")
    kv = mm(hkv, W["w_ukv"], out_dtype=BF16, name=tag + "_ukv")
    qh, kh, vh = mla_prep_fwd(qpad, kv, proj, pos, W["rope"], tag + "_prep")
    o, lse = flash_fwd(qh, kh, vh, tag + "_attn")
    og = rowwise(f_ogate, [(o, 512, 0, 1), (proj, 512, 4, 1)], [], [(512, 1, BF16)], ncol=4, ts=1024, name=tag + "_ogate")[0]
    xn = mm(og, W["w_out"], add=x, name=tag + "_out")
    return xn, (x, h, proj, hq, hkv, qh, kh, vh, o, lse, og)


def mla_bwd(dxn, pos, W, saved, tag, after=None):
    x, h, proj, hq, hkv, qh, kh, vh, o, lse, og = saved
    dog = mm(dxn, W["w_out"], tb=True, after=after, out_dtype=BF16, name=tag + "_dog")
    g = {"w_out": mm(og, dxn, ta=True, out_dtype=BF16, name=tag + "_dwout")}
    dproj = jnp.zeros(proj.shape, BF16)
    (do, dproj), _ = rowwise_bwd(f_ogate, [(o, 512, 0, 1), (proj, 512, 4, 1)], [], [(dog, 512, 0, 1)], need=[True, True],
                                 place={1: (dproj, 4)}, narrow=(0,), ncol=4, ts=1024, name=tag + "_dogate")
    dqh, dkh, dvh = flash_bwd(qh, kh, vh, o, lse, do, tag + "_dattn")
    dqpad, dkv, dproj = mla_prep_bwd(dqh, dkh, dvh, pos, W["rope"], dproj, tag + "_dprep")
    dhq = mm(dqpad, W["w_uq"], tb=True, name=tag + "_dhq")
    g["w_uq"] = mm(hq, dqpad, ta=True, out_dtype=BF16, name=tag + "_dwuq")
    dhkv = mm(dkv, W["w_ukv"], tb=True, name=tag + "_dhkv")
    g["w_ukv"] = mm(hkv, dkv, ta=True, out_dtype=BF16, name=tag + "_dwukv")
    (dproj,), (g["q_g"],) = rowwise_bwd(f_rms, [(proj, MLA_Q_LORA, 0, 0)], [(W["q_g"], MLA_Q_LORA, 0, 0)], [(dhq, MLA_Q_LORA, 0, 0)],
                                        need=[True], place={0: (dproj, 0)}, ts=512, name=tag + "_dqnorm")
    (dproj,), (g["kv_g"],) = rowwise_bwd(f_rms, [(proj, MLA_KV_LORA, 2, 0)], [(W["kv_g"], MLA_KV_LORA, 0, 0)], [(dhkv, MLA_KV_LORA, 0, 0)],
                                         need=[True], place={0: (dproj, 2)}, ts=512, name=tag + "_dkvnorm")
    dh = mm(dproj, W["w_in"], tb=True, name=tag + "_dh")
    g["w_in"] = mm(h, dproj, ta=True, out_dtype=BF16, name=tag + "_dw_in")
    dx, g["ng"] = _norm_bwd(x, W["ng"], dh, dxn, tag + "_dnorm")
    return dx, g


def _pad_cols(a, n):
    return jnp.pad(a, ((0, 0), (0, n - a.shape[1])))


def _mla_w_in_layout(w):
    z = lambda n: jnp.zeros((w.shape[0], n), w.dtype)
    kr = w[:, 1280:1344]
    return jnp.concatenate([w[:, :768], z(256), w[:, 768:1280], kr[:, :32], z(32), kr[:, 32:], z(32), z(384), w[:, 1344:]], axis=1)


def _mla_w_in_unlayout(g):
    return jnp.concatenate([g[:, :768], g[:, 1024:1536], g[:, 1536:1568], g[:, 1600:1632], g[:, 2048:]], axis=1)


def _mla_w_uq_layout(w):
    w3 = w.reshape(w.shape[0], MLA_H, MLA_NOPE + MLA_ROPE)
    z = jnp.zeros((w.shape[0], MLA_H, 32), w.dtype)
    return jnp.concatenate([w3[..., :128], w3[..., 128:160], z, w3[..., 160:192], z], axis=-1).reshape(w.shape[0], MLA_H * 256)


def _mla_w_uq_unlayout(g):
    g3 = g.reshape(g.shape[0], MLA_H, 256)
    return jnp.concatenate([g3[..., :128], g3[..., 128:160], g3[..., 192:224]], axis=-1).reshape(g.shape[0], MLA_H * 192)


def _rope_consts():
    half = MLA_ROPE // 2
    inv = ROPE_THETA ** (-jnp.arange(half, dtype=F32) / half)
    z = jnp.zeros((half,), F32)
    o = jnp.ones((half,), F32)
    row = lambda *p: jnp.concatenate(p).reshape(1, LANES)
    return row(inv, z, inv, z), row(o, z, o, z), row(-o, z, o, z)


BIG = ["pool_w_in", "pool_w_grp", "pool_w_out", "gdn_w_in", "gdn_w_out", "mla_w_in", "mla_w_uq", "mla_w_ukv", "mla_w_out"]
BIG_LAYOUT = {"pool_w_in": (1, 1024, (1024, 4096)), "pool_w_grp": (1, 128, (4, 512, 512)), "pool_w_out": (0, 512, (2048, 1024)),
              "gdn_w_in": (None, None, (4, 1024, 1540)), "gdn_w_out": (0, 512, (2048, 1024)),
              "mla_w_in": (None, None, (4, 1024, 848)), "mla_w_uq": (1, 768, (768, 3072)), "mla_w_ukv": (1, 1024, (512, 4096)),
              "mla_w_out": (0, 512, (2048, 1024))}
SMALL_SHARDED = ["pool_scale", "gdn_conv", "mla_q_norm_g", "mla_kv_norm_g"]
SMALL_AXIS = {"pool_scale": 1, "gdn_conv": 2, "mla_q_norm_g": 1, "mla_kv_norm_g": 1}
REPLICATED = ["norm_g", "gdn_a_log", "gdn_dt_bias", "gdn_norm_g", "final_g"]
PACK_C = 1024


def _pack(parts, dtype, row_mult):
    flat = jnp.concatenate([p.reshape(-1).astype(dtype) for p in parts])
    rows = -(-flat.shape[0] // PACK_C)
    rows = -(-rows // row_mult) * row_mult
    return jnp.pad(flat, (0, rows * PACK_C - flat.shape[0])).reshape(rows, PACK_C)


def _unpack(buf, shapes):
    lead = buf.shape[:-2]
    flat = buf.reshape(lead + (-1,))
    out, off = [], 0
    for s in shapes:
        n = int(np.prod(s))
        out.append(flat[..., off:off + n].reshape(lead + tuple(s)))
        off += n
    return out


def _unshard(g4, axis):
    a = jnp.moveaxis(g4, 0, axis)
    s = a.shape
    return a.reshape(s[:axis] + (s[axis] * s[axis + 1],) + s[axis + 2:])


def _to_shards(a, axis):
    s = a.shape
    return jnp.moveaxis(a.reshape(s[:axis] + (4, s[axis] // 4) + s[axis + 1:]), axis, 0)


def layer_weights(full, small, rep, layer):
    ng = rep["norm_g"][layer:layer + 1]
    side_by_side = lambda a4: jnp.moveaxis(a4, 0, 1).reshape(a4.shape[1], 4 * a4.shape[2])
    if layer in (0, 3):
        j = layer // 3
        return dict(ng=ng, w_in=full[("pool_w_in", j)], w_grp=full[("pool_w_grp", j)], scale=small["pool_scale"][j:j + 1],
                    w_out=full[("pool_w_out", j)])
    if layer == 1:
        return dict(ng=ng, w_in=_pad_cols(side_by_side(full[("gdn_w_in", 0)]), GDN_IN_PAD),
                    conv=jnp.pad(small["gdn_conv"][0], ((0, 4), (0, 0))), a_log=_pad_cols(rep["gdn_a_log"], LANES),
                    dt_bias=_pad_cols(rep["gdn_dt_bias"], LANES), norm_g=rep["gdn_norm_g"], w_out=full.get(("gdn_w_out", 0)))
    return dict(ng=ng, w_in=_mla_w_in_layout(side_by_side(full[("mla_w_in", 0)])), q_g=small["mla_q_norm_g"],
                kv_g=small["mla_kv_norm_g"], w_uq=_mla_w_uq_layout(full[("mla_w_uq", 0)]), w_ukv=full[("mla_w_ukv", 0)],
                w_out=full[("mla_w_out", 0)], rope=_rope_consts())


def big_grad_pieces(gl):
    g0, g1, g2, g3 = gl
    slots = lambda a: jnp.moveaxis(a.reshape(a.shape[0], 4, a.shape[1] // 4), 1, 0)
    out = {}
    for l, g in ((0, g0), (1, g3)):
        if g is not None:
            out.update({("pool_w_in", l): g["w_in"], ("pool_w_grp", l): g["w_grp"], ("pool_w_out", l): g["w_out"]})
    if g1 is not None:
        out.update({("gdn_w_in", 0): slots(g1["w_in"][:, :GDN_IN]), ("gdn_w_out", 0): g1["w_out"]})
    if g2 is not None:
        out.update({("mla_w_in", 0): slots(_mla_w_in_unlayout(g2["w_in"])), ("mla_w_uq", 0): _mla_w_uq_unlayout(g2["w_uq"]),
                    ("mla_w_ukv", 0): g2["w_ukv"], ("mla_w_out", 0): g2["w_out"]})
    return out


def small_grads(gl, dfinal):
    g0, g1, g2, g3 = gl
    return {"norm_g": jnp.concatenate([g0["ng"], g1["ng"], g2["ng"], g3["ng"]], axis=0),
            "pool_scale": jnp.concatenate([g0["scale"], g3["scale"]], axis=0), "gdn_conv": g1["conv"][None, :4],
            "gdn_a_log": g1["a_log"][:, :GDN_H], "gdn_dt_bias": g1["dt_bias"][:, :GDN_H], "gdn_norm_g": g1["norm_g"],
            "mla_q_norm_g": g2["q_g"], "mla_kv_norm_g": g2["kv_g"], "final_g": dfinal.reshape(D)}


NAMES = ["norm_g", "pool_w_in", "pool_w_grp", "pool_scale", "pool_w_out", "gdn_w_in", "gdn_conv", "gdn_a_log", "gdn_dt_bias",
         "gdn_norm_g", "gdn_w_out", "mla_w_in", "mla_q_norm_g", "mla_w_uq", "mla_kv_norm_g", "mla_w_ukv", "mla_w_out", "final_g"]


def kernel(x, positions, norm_g, pool_w_in, pool_w_grp, pool_scale, pool_w_out, gdn_w_in, gdn_conv, gdn_a_log, gdn_dt_bias, gdn_norm_g, gdn_w_out, mla_w_in, mla_q_norm_g, mla_w_uq, mla_kv_norm_g, mla_w_ukv, mla_w_out, final_g, loss_target, m_norm_g, m_pool_w_in, m_pool_w_grp, m_pool_scale, m_pool_w_out, m_gdn_w_in, m_gdn_conv, m_gdn_a_log, m_gdn_dt_bias, m_gdn_norm_g, m_gdn_w_out, m_mla_w_in, m_mla_q_norm_g, m_mla_w_uq, m_mla_kv_norm_g, m_mla_w_ukv, m_mla_w_out, m_final_g, v_norm_g, v_pool_w_in, v_pool_w_grp, v_pool_scale, v_pool_w_out, v_gdn_w_in, v_gdn_conv, v_gdn_a_log, v_gdn_dt_bias, v_gdn_norm_g, v_gdn_w_out, v_mla_w_in, v_mla_q_norm_g, v_mla_w_uq, v_mla_kv_norm_g, v_mla_w_ukv, v_mla_w_out, v_final_g):
    args = locals()
    w = {n: args[n] for n in NAMES}
    m = {n: args["m_" + n] for n in NAMES}
    v = {n: args["v_" + n] for n in NAMES}
    my_chip = (2 * lax.axis_index("x") + lax.axis_index("y")).astype(I32)
    S_ = x.shape[1]
    x0, pos, target = x[0], positions.reshape(S_, 1).astype(F32), loss_target[0]
    rep = {n: w[n] for n in REPLICATED}

    shard = {(n, l): w[n][l:l + 1].astype(BF16) for n in BIG for l in range(w[n].shape[0])}
    small_shapes = [w[n].shape for n in SMALL_SHARDED]
    shard[("small", 0)] = _pack([w[n] for n in SMALL_SHARDED], F32, 8)[None]
    layout = dict(BIG_LAYOUT, small=(None, None, (4,) + shard[("small", 0)].shape[1:]))

    def gather_start(group, after, tag):
        pieces = [gather_piece(i, 0, i, layout[n][0], layout[n][1]) for i, (n, l) in enumerate(group)]
        shapes = [_sds(layout[n][2], shard[(n, l)].dtype) for n, l in group]
        sems, ins, lands, token = exchange_start(pieces, [shard[k] for k in group], shapes, after, tag + "_start")
        return (pieces, sems, ins, lands), token

    def finish(handle, after, tag):
        return exchange_wait(*handle, after, tag + "_wait")

    def gathered(group, handle, after, tag):
        srcs, lands = finish(handle, after, tag)
        return {(n, l): place_own(a, s[0], layout[n][0], layout[n][1], my_chip) for (n, l), s, a in zip(group, srcs, lands)}

    group_a = [("small", 0), ("pool_w_in", 0)]
    group_a2 = [("pool_w_grp", 0), ("pool_w_out", 0)]
    group_b = [("gdn_w_in", 0)]
    group_c = [("gdn_w_out", 0), ("mla_w_in", 0), ("mla_w_uq", 0), ("mla_w_ukv", 0), ("mla_w_out", 0), ("pool_w_in", 1),
               ("pool_w_grp", 1), ("pool_w_out", 1)]
    full = {}
    h_a, t_a = gather_start(group_a, x0, "gather_a")
    h_a2, t_a2 = gather_start(group_a2, t_a, "gather_a2")
    h_b, t_b = gather_start(group_b, t_a2, "gather_b")
    h_c, t_c = gather_start(group_c, t_b, "gather_c")
    full.update(gathered(group_a, h_a, t_c, "gather_a"))
    small = {n: _unshard(a, SMALL_AXIS[n]) for n, a in zip(SMALL_SHARDED, _unpack(full[("small", 0)], small_shapes))}

    def late_l0(proj):
        full.update(gathered(group_a2, h_a2, proj, "gather_a2"))
        return dict(w_grp=full[("pool_w_grp", 0)], w_out=full[("pool_w_out", 0)])

    x1, s0 = pool_fwd(x0, dict(ng=rep["norm_g"][0:1], w_in=full[("pool_w_in", 0)], scale=small["pool_scale"][0:1]), "l0", late=late_l0)
    W0 = layer_weights(full, small, rep, 0)
    full.update(gathered(group_b, h_b, x1, "gather_b"))

    def late_l1(og):
        full.update(gathered(group_c, h_c, og, "gather_c"))
        return dict(w_out=full[("gdn_w_out", 0)])

    x2, s1 = gdn_fwd(x1, layer_weights(full, small, rep, 1), "l1", late=late_l1)
    W1, W2, W3 = (layer_weights(full, small, rep, i) for i in (1, 2, 3))
    x3, s2 = mla_fwd(x2, pos, W2, "l2")
    x4, s3 = pool_fwd(x3, W3, "l3")
    loss_part, dx4, dfinal = loss_head(x4, target, final_g.reshape(1, D), "loss_head")

    def scatter_start(pieces_of, after, tag):
        keys = list(pieces_of)
        pieces = [scatter_piece(i, i, BIG_LAYOUT[n][0], BIG_LAYOUT[n][1]) for i, (n, l) in enumerate(keys)]
        shapes = [_sds((4,) + tuple(w[n].shape[1:]), BF16) for n, l in keys]
        sems, ins, lands, token = exchange_start(pieces, [pieces_of[k] for k in keys], shapes, after, tag + "_start")
        return keys, (pieces, sems, ins, lands), token

    def scattered(keys, handle, after, tag):
        srcs, lands = finish(handle, after, tag)
        return {(n, l): place_own(a, own_window(g, BIG_LAYOUT[n][0], BIG_LAYOUT[n][1], my_chip), None, None, my_chip)
                for (n, l), g, a in zip(keys, srcs, lands)}

    dx3, g3 = pool_bwd(dx4, W3, s3, "l3")
    k3, h3, t3 = scatter_start(big_grad_pieces((None, None, None, g3)), dx3, "scatter_l3")
    dx2, g2 = mla_bwd(dx3, pos, W2, s2, "l2", after=t3)
    k2, h2, t2 = scatter_start(big_grad_pieces((None, None, g2, None)), dx2, "scatter_l2")
    dx1, g1 = gdn_bwd(dx2, W1, s1, "l1", after=t2)
    k1, h1, t1 = scatter_start(big_grad_pieces((None, g1, None, None)), dx1, "scatter_l1")
    def swap_start(part, tag):
        keys = list(part)
        ins = [part[k] for k in keys]
        pieces = [whole_piece(i) for i in range(len(keys))]
        sems, ins, lands, token = exchange_start(pieces, ins, [_sds(a.shape, a.dtype) for a in ins], ins[0], tag + "_start", sibling=True)
        swaps.append((keys, (pieces, sems, ins, lands), tag))
        return token

    last, swaps = [], []

    def emit_l0(grads):
        first = not last
        now = next(iter(grads.values()))
        early = [(k3, h3, "scatter_l3"), (k2, h2, "scatter_l2")] if first else [(k1, h1, "scatter_l1")]
        landed = {}
        for keys, handle, tag in early:
            landed.update(scattered(keys, handle, now, tag))
        swapping = swap_start(landed, "swap_a" if first else "swap_b")
        tag = "scatter_l0a" if first else "scatter_l0b"
        keys, handle, token = scatter_start({("pool_" + k, 0): a for k, a in grads.items()}, swapping, tag)
        last.append((keys, handle, tag))
        return token

    dx0, g0 = pool_bwd(dx1, W0, s0, "l0", after=t1, emit=emit_l0)
    landed = {}
    for keys, handle, tag in last:
        landed.update(scattered(keys, handle, dx0, tag))
    swap_start(landed, "swap_c")
    recv, sib = {}, {}
    for keys, handle, tag in swaps:
        mine_, theirs = exchange_wait(*handle, dx0, tag + "_wait", sibling=True)
        recv.update(zip(keys, mine_))
        sib.update(zip(keys, theirs))

    sg = small_grads((g0, g1, g2, g3), dfinal)
    small_names = SMALL_SHARDED + REPLICATED
    small_buf = _pack([sg[n] for n in small_names] + [loss_part], F32, 8)
    small_sum = sum_slots(exchange_all(small_buf, "gather_small"), "sum_small")
    full_small = _unpack(small_sum, [sg[n].shape for n in small_names] + [(1, LANES)])
    loss = full_small[-1][0, 0]
    small_part = {}
    for n, a in zip(small_names, full_small[:-1]):
        if n in SMALL_AXIS:
            a = lax.dynamic_index_in_dim(_to_shards(a, SMALL_AXIS[n]), my_chip, axis=0, keepdims=False)
        small_part[n] = a

    outs = []
    for n in NAMES:
        shp = w[n].shape
        two = (int(np.prod(shp[:-1])), shp[-1]) if len(shp) > 1 else (1, shp[0])
        if n in BIG_LAYOUT:
            layers = shp[0]
            rows = lambda a: a.reshape(4, two[0] // layers, two[1])
            parts = [[rows(recv[(n, l)]) for l in range(layers)], [rows(sib[(n, l)]) for l in range(layers)]]
        else:
            parts = [[small_part[n].reshape((1,) + two)]]
        res = adamw(w[n].reshape(two), parts, m[n].reshape(two), v[n].reshape(two), "adamw_" + n)
        outs.append([r.reshape(shp) for r in res])
    return (loss, dx0[None], *[o[0] for o in outs], *[o[1] for o in outs], *[o[2] for o in outs], *[o[3] for o in outs])
```

```python
import math

import jax
import jax.numpy as jnp
import numpy as np
from jax import lax
from jax.experimental import pallas as pl
from jax.experimental.pallas import tpu as pltpu

F32 = jnp.float32
BF16 = jnp.bfloat16
I32 = jnp.int32

D = 1024
EPS = 1e-6
POOL_WIDTH = 2048
POOL_GROUP = 512
GDN_H, GDN_DK, GDN_DV, GDN_C = 8, 128, 256, 64
GDN_QK, GDN_V, GDN_CONV_CH, GDN_IN = 1024, 2048, 4096, 6160
GDN_IN_PAD = 6272
MLA_H, MLA_NOPE, MLA_ROPE, MLA_V = 16, 128, 64, 128
MLA_Q_LORA, MLA_KV_LORA, MLA_WIDTH, MLA_IN = 768, 512, 2048, 3392
MLA_IN_PAD = 4096
MLA_SCALE = (MLA_NOPE + MLA_ROPE) ** -0.5
ROPE_THETA = 10000.0
ADAM_LR, ADAM_B1, ADAM_B2, ADAM_EPS, ADAM_WD, ADAM_STEP = 0.001, 0.9, 0.999, 1e-08, 0.01, 10

VMEM_LIMIT_V7X = 56 * 1024 * 1024
LANES = 128
MESH = pl.DeviceIdType.MESH


def _pc(body, **kw):
    return pl.pallas_call(body, **kw)


def _cparams(sem):
    return pltpu.CompilerParams(dimension_semantics=sem, vmem_limit_bytes=VMEM_LIMIT_V7X)


def _tile(n, cap):
    t = (cap // LANES) * LANES
    while t >= LANES:
        if n % t == 0:
            return t
        t -= LANES
    return n


def _sds(shape, dt):
    return jax.ShapeDtypeStruct(shape, dt)


def mm(a, b, *, ta=False, tb=False, add=None, after=None, out_dtype=F32, name):
    if ta:
        K, M = a.shape
    else:
        M, K = a.shape
    if tb:
        N, K2 = b.shape
    else:
        K2, N = b.shape
    assert K == K2, (a.shape, b.shape, ta, tb)
    tm, tn, tk = _tile(M, 1024), _tile(N, 1024), _tile(K, 1024)
    nk = K // tk
    a_spec = pl.BlockSpec((tk, tm), lambda i, j, k: (k, i)) if ta else pl.BlockSpec((tm, tk), lambda i, j, k: (i, k))
    b_spec = pl.BlockSpec((tn, tk), lambda i, j, k: (j, k)) if tb else pl.BlockSpec((tk, tn), lambda i, j, k: (k, j))
    o_spec = pl.BlockSpec((tm, tn), lambda i, j, k: (i, j))
    dn = (((0 if ta else 1,), (1 if tb else 0,)), ((), ()))
    has_add = add is not None

    def body(*refs):
        a_ref, b_ref = refs[0], refs[1]
        part = lax.dot_general(a_ref[...].astype(BF16), b_ref[...].astype(BF16), dn, preferred_element_type=F32)
        if nk == 1:
            refs[-1][...] = (part + refs[2][...] if has_add else part).astype(out_dtype)
            return
        o_ref, acc = refs[-2], refs[-1]
        k = pl.program_id(2)

        @pl.when(k == 0)
        def _():
            acc[...] = part

        @pl.when(k > 0)
        def _():
            acc[...] += part

        @pl.when(k == nk - 1)
        def _():
            r = acc[...]
            if has_add:
                r = r + refs[2][...]
            o_ref[...] = r.astype(out_dtype)

    ins = [a, b] + ([add] if has_add else []) + ([after] if after is not None else [])
    specs = [a_spec, b_spec] + ([o_spec] if has_add else []) + ([pl.BlockSpec(memory_space=pl.ANY)] if after is not None else [])
    return _pc(body, grid=(M // tm, N // tn, nk), in_specs=specs, out_specs=o_spec, out_shape=_sds((M, N), out_dtype),
               scratch_shapes=[pltpu.VMEM((tm, tn), F32)] if nk > 1 else [], compiler_params=_cparams(("parallel", "parallel", "arbitrary")),
               name=name)(*ins)


def gmm(kind, a, b, *, G, name, out_dtype=F32):
    S_ = a.shape[0]
    Ka = a.shape[1] // G
    if kind == "tn":
        N = b.shape[1] // G
        tk = _tile(S_, 2048)
        nk = S_ // tk

        def body(a_ref, b_ref, o_ref, acc):
            k = pl.program_id(1)

            @pl.when(k == 0)
            def _():
                acc[...] = jnp.zeros_like(acc)

            acc[...] += lax.dot_general(a_ref[...].astype(BF16), b_ref[...].astype(BF16), (((0,), (0,)), ((), ())),
                                        preferred_element_type=F32)

            @pl.when(k == nk - 1)
            def _():
                o_ref[...] = acc[...].astype(out_dtype)

        return _pc(body, grid=(G, nk),
                   in_specs=[pl.BlockSpec((tk, Ka), lambda g, k: (k, g)), pl.BlockSpec((tk, N), lambda g, k: (k, g))],
                   out_specs=pl.BlockSpec((None, Ka, N), lambda g, k: (g, 0, 0)), out_shape=_sds((G, Ka, N), out_dtype),
                   scratch_shapes=[pltpu.VMEM((Ka, N), F32)], compiler_params=_cparams(("parallel", "arbitrary")), name=name)(a, b)
    N = b.shape[2] if kind == "nn" else b.shape[1]
    tm = _tile(S_, 4096)
    dn = (((1,), (0 if kind == "nn" else 1,)), ((), ()))

    def body(a_ref, b_ref, o_ref):
        o_ref[...] = lax.dot_general(a_ref[...].astype(BF16), b_ref[...].astype(BF16), dn, preferred_element_type=F32).astype(out_dtype)

    bshape = (None,) + tuple(b.shape[1:])
    return _pc(body, grid=(G, S_ // tm),
               in_specs=[pl.BlockSpec((tm, Ka), lambda g, i: (i, g)), pl.BlockSpec(bshape, lambda g, i: (g, 0, 0))],
               out_specs=pl.BlockSpec((tm, N), lambda g, i: (i, g)), out_shape=_sds((S_, G * N), out_dtype),
               compiler_params=_cparams(("parallel", "parallel")), name=name)(a, b)


def _rw_spec(ts, w, c, s):
    return pl.BlockSpec((ts, w), lambda j, i: (i, c + j * s))


def _rw_pspec(p, w, c, s):
    return pl.BlockSpec((p.shape[0], w), lambda j, i: (0, c + j * s))


def rowwise(f, tiles, params, outs, *, ncol=1, ts, name):
    S_ = tiles[0][0].shape[0]
    nin = len(tiles) + len(params)

    def body(*refs):
        res = f(pl.program_id(0), *[r[...].astype(F32) for r in refs[:nin]])
        for r, o in zip(refs[nin:], res):
            r[...] = o.astype(r.dtype)

    return _pc(body, grid=(ncol, S_ // ts),
               in_specs=[_rw_spec(ts, w, c, s) for (_, w, c, s) in tiles] + [_rw_pspec(*p) for p in params],
               out_specs=[_rw_spec(ts, w, 0, s) for (w, s, _) in outs],
               out_shape=[_sds((S_, w * (ncol if s else 1)), dt) for (w, s, dt) in outs],
               compiler_params=_cparams(("parallel", "parallel")), name=name)(*[t[0] for t in tiles], *[p[0] for p in params])


def rowwise_bwd(f, tiles, params, cots, *, need, adds=None, place=None, narrow=(), ncol=1, ts, name):
    S_ = tiles[0][0].shape[0]
    adds = adds or {}
    place = place or {}
    nt, npar, nc = len(tiles), len(params), len(cots)
    add_keys = sorted(adds)
    need_idx = [k for k in range(nt) if need[k]]
    into_keys = [k for k in need_idx if k in place and not isinstance(place[k][0], int)]
    n_extra = len(add_keys) + len(into_keys)

    def body(*refs):
        j, i = pl.program_id(0), pl.program_id(1)
        vals = [r[...].astype(F32) for r in refs[:nt + npar]]
        cvals = tuple(r[...].astype(F32) for r in refs[nt + npar:nt + npar + nc])
        add_refs = refs[nt + npar + nc:nt + npar + nc + len(add_keys)]
        out_refs = refs[nt + npar + nc + n_extra:]
        _, vjp = jax.vjp(lambda *v: tuple(f(j, *v)), *vals)
        grads = vjp(cvals)
        for n, k in enumerate(need_idx):
            g = grads[k]
            if k in adds:
                g = g + add_refs[add_keys.index(k)][...]
            out_refs[n][...] = g.astype(out_refs[n].dtype)
        for n in range(npar):
            ref = out_refs[len(need_idx) + n]
            first = (i == 0) if params[n][3] else jnp.logical_and(i == 0, j == 0)

            @pl.when(first)
            def _():
                ref[...] = jnp.zeros_like(ref)

            ref[...] += grads[nt + n]

    in_specs = ([_rw_spec(ts, w, c, s) for (_, w, c, s) in tiles] + [_rw_pspec(*p) for p in params]
                + [_rw_spec(ts, w, c, s) for (_, w, c, s) in cots] + [_rw_spec(ts, *adds[k][1:]) for k in add_keys]
                + [pl.BlockSpec(memory_space=pl.ANY) for _ in into_keys])
    out_specs, out_shape, aliases = [], [], {}
    for n, k in enumerate(need_idx):
        w, s = tiles[k][1], tiles[k][3]
        if k in place:
            dst, c0 = place[k]
            total = dst if isinstance(dst, int) else dst.shape[1]
            out_specs.append(_rw_spec(ts, w, c0, s))
            out_shape.append(_sds((S_, total), (BF16 if k in narrow else F32) if isinstance(dst, int) else dst.dtype))
            if k in into_keys:
                aliases[nt + npar + nc + len(add_keys) + into_keys.index(k)] = n
        else:
            out_specs.append(_rw_spec(ts, w, 0, s))
            out_shape.append(_sds((S_, w * (ncol if s else 1)), BF16 if k in narrow else F32))
    out_specs += [_rw_pspec(p[0], p[1], p[2], p[3]) for p in params]
    out_shape += [_sds(p[0].shape, F32) for p in params]
    res = _pc(body, grid=(ncol, S_ // ts), in_specs=in_specs, out_specs=out_specs, out_shape=out_shape,
              input_output_aliases=aliases, compiler_params=_cparams(("arbitrary", "arbitrary")), name=name)(
        *[t[0] for t in tiles], *[p[0] for p in params], *[c[0] for c in cots], *[adds[k][0] for k in add_keys],
        *[place[k][0] for k in into_keys])
    return list(res[:len(need_idx)]), list(res[len(need_idx):])


def _rms(x, g):
    r = lax.rsqrt(jnp.mean(x * x, axis=-1, keepdims=True) + EPS)
    return x * r * g


def _silu(x):
    return x * jax.nn.sigmoid(x)


@jax.custom_vjp
def _softplus(x):
    return jnp.maximum(x, 0.0) + jnp.log1p(jnp.exp(-jnp.abs(x)))


_softplus.defvjp(lambda x: (_softplus(x), x), lambda x, d: (d * jax.nn.sigmoid(x),))


def f_rms(j, x, g):
    return (_rms(x, g),)


def f_pool_gate(j, pg, gate, scale):
    return (pg * scale * _silu(gate),)


def f_ogate(j, o, gate):
    return (o * _silu(gate),)


def f_gdn_out(j, o, gate, g):
    return (_rms(o, g) * _silu(gate),)


def f_gdn_gates(j, ba, alog, dtb):
    lane = lax.broadcasted_iota(I32, (1, LANES), 1)
    gs, bs = [], []
    for h in range(GDN_H):
        eb = (lane == h).astype(F32)
        ea = (lane == GDN_H + h).astype(F32)
        b = jnp.sum(ba * eb, -1, keepdims=True)
        a = jnp.sum(ba * ea, -1, keepdims=True)
        al = jnp.sum(alog * eb, -1, keepdims=True)
        dt = jnp.sum(dtb * eb, -1, keepdims=True)
        g = -jnp.exp(al) * _softplus(a + dt)
        gs.append(jnp.broadcast_to(g, ba.shape))
        bs.append(jnp.broadcast_to(jax.nn.sigmoid(b), ba.shape))
    return jnp.concatenate(gs, 1), jnp.concatenate(bs, 1)


def _shift_dn(x, k):
    rows = lax.broadcasted_iota(I32, x.shape, 0)
    return jnp.where(rows < k, 0.0, pltpu.roll(x, k, 0))


def _shift_up(x, k):
    n = x.shape[0]
    rows = lax.broadcasted_iota(I32, x.shape, 0)
    return jnp.where(rows >= n - k, 0.0, pltpu.roll(x, n - k, 0))


def _pool_window(j):
    g = lax.div(j, POOL_GROUP // LANES)
    return jnp.where(g == 0, 2.0, jnp.where(g == 1, 4.0, jnp.where(g == 2, 8.0, 16.0))), g


def _pick(g, a2, a4, a8, a16):
    return jnp.where(g == 0, a2, jnp.where(g == 1, a4, jnp.where(g == 2, a8, a16)))


def pool_time_fwd(proj, name):
    S_ = proj.shape[0]

    def body(u_ref, p_ref):
        u = u_ref[...].astype(F32)
        w, g = _pool_window(pl.program_id(0))
        s2 = u + _shift_dn(u, 1)
        s4 = s2 + _shift_dn(s2, 2)
        s8 = s4 + _shift_dn(s4, 4)
        s16 = s8 + _shift_dn(s8, 8)
        t1 = (lax.broadcasted_iota(I32, u.shape, 0) + 1).astype(F32)
        p_ref[...] = (_pick(g, s2, s4, s8, s16) / jnp.minimum(t1, w) - u).astype(p_ref.dtype)

    return _pc(body, grid=(POOL_WIDTH // LANES,), in_specs=[pl.BlockSpec((S_, LANES), lambda j: (0, j))],
               out_specs=pl.BlockSpec((S_, LANES), lambda j: (0, j)), out_shape=_sds((S_, POOL_WIDTH), BF16),
               compiler_params=_cparams(("parallel",)), name=name)(proj)


def pool_time_bwd(dp, into, name):
    S_ = dp.shape[0]

    def body(dp_ref, _, du_ref):
        d = dp_ref[...].astype(F32)
        w, g = _pool_window(pl.program_id(0))
        t1 = (lax.broadcasted_iota(I32, d.shape, 0) + 1).astype(F32)
        q = d / jnp.minimum(t1, w)
        r2 = q + _shift_up(q, 1)
        r4 = r2 + _shift_up(r2, 2)
        r8 = r4 + _shift_up(r4, 4)
        r16 = r8 + _shift_up(r8, 8)
        du_ref[...] = (_pick(g, r2, r4, r8, r16) - d).astype(du_ref.dtype)

    return _pc(body, grid=(POOL_WIDTH // LANES,),
               in_specs=[pl.BlockSpec((S_, LANES), lambda j: (0, j)), pl.BlockSpec(memory_space=pl.ANY)],
               out_specs=pl.BlockSpec((S_, LANES), lambda j: (0, j)), out_shape=_sds(into.shape, into.dtype),
               input_output_aliases={1: 0}, compiler_params=_cparams(("parallel",)), name=name)(dp, into)


def _conv_post(j, a):
    n = a * lax.rsqrt(jnp.sum(a * a, axis=-1, keepdims=True) + EPS)
    nq = GDN_QK // LANES
    return jnp.where(j < nq, n * (GDN_DK ** -0.5), jnp.where(j < 2 * nq, n, a))


def _conv_taps(u):
    return [_shift_dn(u, 3), _shift_dn(u, 2), _shift_dn(u, 1), u]


def _conv_pre(taps, w):
    return w[0:1] * taps[0] + w[1:2] * taps[1] + w[2:3] * taps[2] + w[3:4] * taps[3]


def gdn_conv_fwd(proj, conv_w, name):
    S_ = proj.shape[0]

    def body(u_ref, w_ref, o_ref):
        o_ref[...] = _conv_post(pl.program_id(0), _silu(_conv_pre(_conv_taps(u_ref[...]), w_ref[...])))

    return _pc(body, grid=(GDN_CONV_CH // LANES,),
               in_specs=[pl.BlockSpec((S_, LANES), lambda j: (0, j)), pl.BlockSpec((8, LANES), lambda j: (0, j))],
               out_specs=pl.BlockSpec((S_, LANES), lambda j: (0, j)), out_shape=_sds((S_, GDN_CONV_CH), F32),
               compiler_params=_cparams(("parallel",)), name=name)(proj, conv_w)


def gdn_conv_bwd(proj, conv_w, dq, dk, dv, into, name):
    S_ = proj.shape[0]
    nq = GDN_QK // LANES

    def body(u_ref, w_ref, dq_ref, dk_ref, dv_ref, _, du_ref, dw_ref):
        j = pl.program_id(0)
        u, w = u_ref[...], w_ref[...]
        taps = _conv_taps(u)
        c = _conv_pre(taps, w)
        sig = jax.nn.sigmoid(c)
        dout = jnp.where(j < nq, dq_ref[...], jnp.where(j < 2 * nq, dk_ref[...], dv_ref[...]))
        _, vjp = jax.vjp(lambda a: _conv_post(j, a), c * sig)
        dc = vjp(dout)[0] * (sig * (1.0 + c * (1.0 - sig)))
        du = w[3:4] * dc + w[2:3] * _shift_up(dc, 1) + w[1:2] * _shift_up(dc, 2) + w[0:1] * _shift_up(dc, 3)
        du_ref[...] = du.astype(du_ref.dtype)
        rows = lax.broadcasted_iota(I32, (8, LANES), 0)
        dw = jnp.zeros((8, LANES), F32)
        for k in range(4):
            dw = dw + jnp.where(rows == k, jnp.sum(dc * taps[k], axis=0, keepdims=True), 0.0)
        dw_ref[...] = dw

    blk = lambda f: pl.BlockSpec((S_, LANES), f)
    return _pc(body, grid=(GDN_CONV_CH // LANES,),
               in_specs=[blk(lambda j: (0, j)), pl.BlockSpec((8, LANES), lambda j: (0, j)),
                         blk(lambda j: (0, jnp.minimum(j, nq - 1))), blk(lambda j: (0, jnp.clip(j - nq, 0, nq - 1))),
                         blk(lambda j: (0, jnp.clip(j - 2 * nq, 0, 2 * nq - 1))), pl.BlockSpec(memory_space=pl.ANY)],
               out_specs=[blk(lambda j: (0, j)), pl.BlockSpec((8, LANES), lambda j: (0, j))],
               out_shape=[_sds(into.shape, into.dtype), _sds((8, GDN_CONV_CH), F32)], input_output_aliases={5: 0},
               compiler_params=_cparams(("parallel",)), name=name)(proj, conv_w, dq, dk, dv, into)


_NN, _NT, _TN = ((1,), (0,)), ((1,), (1,)), ((0,), (0,))


def _split(x, n):
    parts = []
    for _ in range(n):
        h = x.astype(BF16)
        parts.append(h)
        x = x - h.astype(F32)
    return parts


def _dot(a, b, dn, mode):
    d = lambda p, q: lax.dot_general(p, q, (dn, ((), ())), preferred_element_type=F32)
    if mode == "lo":
        return d(a.astype(BF16), b.astype(BF16))
    if mode == "x3":
        (ah, al), (bh, bl) = _split(a, 2), _split(b, 2)
        return d(ah, bh) + (d(ah, bl) + d(al, bh))
    b0, b1, b2 = _split(b, 3)
    ab = a.astype(BF16)
    return d(ab, b0) + (d(ab, b1) + d(ab, b2))


def _make_dots(mode):
    @jax.custom_vjp
    def nn(a, b):
        return _dot(a, b, _NN, mode)

    @jax.custom_vjp
    def nt(a, b):
        return _dot(a, b, _NT, mode)

    @jax.custom_vjp
    def tn(a, b):
        return _dot(a, b, _TN, mode)

    nn.defvjp(lambda a, b: (nn(a, b), (a, b)), lambda r, d: (nt(d, r[1]), tn(r[0], d)))
    nt.defvjp(lambda a, b: (nt(a, b), (a, b)), lambda r, d: (nn(d, r[1]), tn(d, r[0])))
    tn.defvjp(lambda a, b: (tn(a, b), (a, b)), lambda r, d: (nt(r[1], d), nn(r[0], d)))
    return nn, nt, tn


_nn_hi, _nt_hi, _tn_hi = _make_dots("x3")
_nn_lo, _nt_lo, _tn_lo = _make_dots("lo")


@jax.custom_vjp
def _nn_const(a, b):
    return _dot(a, b, _NN, "xl")


_nn_const.defvjp(lambda a, b: (_nn_const(a, b), a), lambda a, d: (jnp.zeros_like(a), _dot(a, d, _TN, "xl")))


def _each(f, *lists):
    return [f(*xs) for xs in zip(*lists)]


@jax.custom_vjp
def _unit_inverses(xs):
    C = xs[0].shape[0]
    eye = (lax.broadcasted_iota(I32, (C, C), 0) == lax.broadcasted_iota(I32, (C, C), 1)).astype(F32)
    ainv, p = [eye + a for a in xs], xs
    for _ in range(int(math.log2(C)) - 1):
        p = _each(lambda a: _dot(a, a, _NN, "x3"), p)
        ainv = _each(lambda a, b: a + _dot(a, b, _NN, "x3"), ainv, p)
    return ainv


def _unit_inverses_bwd(ainv, d):
    left = _each(lambda a, g: _dot(a, g, _TN, "x3"), ainv, d)
    return (_each(lambda t, a: _dot(t, a, _NT, "x3"), left, ainv),)


_unit_inverses.defvjp(lambda xs: (lambda a: (a, a))(_unit_inverses(xs)), _unit_inverses_bwd)


def _gdn_chunk(q, k, v, gb, bb, state):
    C = GDN_C
    e0 = (lax.broadcasted_iota(I32, (1, LANES), 1) == 0).astype(F32)
    ri = lax.broadcasted_iota(I32, (C, C), 0)
    ci = lax.broadcasted_iota(I32, (C, C), 1)
    causal, strict = ri >= ci, ri > ci
    tri, eye, ones = causal.astype(F32), (ri == ci).astype(F32), jnp.ones((C, C), F32)
    last = lax.broadcasted_iota(I32, (C, LANES), 0) == C - 1
    g1 = _each(lambda a: jnp.sum(a * e0, -1, keepdims=True), gb)
    b1 = _each(lambda a: jnp.sum(a * e0, -1, keepdims=True), bb)
    gc_c = _each(lambda g: _nn_const(tri, jnp.broadcast_to(g, (C, C))), g1)
    gc_d = _each(lambda g: _nn_const(tri, jnp.broadcast_to(g, (C, LANES))), g1)
    gr_c = _each(lambda g: _nn_const(ones, eye * g), gc_c)
    decay = _each(lambda a, r: jnp.where(causal, jnp.exp(jnp.where(causal, a - r, 0.0)), 0.0), gc_c, gr_c)
    kb = _each(lambda a, b: a * b, k, b1)
    vb = _each(lambda a, b: a * b, v, b1)
    x = _each(lambda a, b, d: -jnp.where(strict, _nt_lo(a, b) * d, 0.0), kb, k, decay)
    ainv = _unit_inverses(x)
    u = _each(_nn_hi, ainv, vb)
    w = _each(lambda a, b, g: _nn_hi(a, b * jnp.exp(g)), ainv, kb, gc_d)
    attn = _each(lambda a, b, d: jnp.where(causal, _nt_lo(a, b) * d, 0.0), q, k, decay)
    v_new = _each(lambda a, b, s: a - _nn_lo(b, s), u, w, state)
    o = _each(lambda a, g, s, t, vn: _nn_lo(a * jnp.exp(g), s) + _nn_lo(t, vn), q, gc_d, state, attn, v_new)
    gl = _each(lambda g: jnp.sum(jnp.where(last, g, 0.0), axis=0, keepdims=True), gc_d)
    new_state = _each(lambda s, g, a, gd, vn: s * jnp.exp(jnp.sum(g * e0, -1, keepdims=True)) + _tn_lo(a * jnp.exp(g - gd), vn),
                      state, gl, k, gc_d, v_new)
    return o, new_state


def _head_slices(ref, width):
    return [ref[:, h * width:(h + 1) * width] for h in range(GDN_H)]


def gdn_chunk_fwd(qkv, g_b, beta_b, name):
    S_ = qkv.shape[0]
    N = S_ // GDN_C

    def body(q_ref, k_ref, v_ref, g_ref, b_ref, o_ref, s_ref, state):
        @pl.when(pl.program_id(0) == 0)
        def _():
            state[...] = jnp.zeros_like(state)

        st = [state[h] for h in range(GDN_H)]
        s_ref[0] = state[...]
        o, st2 = _gdn_chunk(_head_slices(q_ref, GDN_DK), _head_slices(k_ref, GDN_DK), _head_slices(v_ref, GDN_DV),
                            _head_slices(g_ref, GDN_DK), _head_slices(b_ref, GDN_DK), st)
        for h in range(GDN_H):
            o_ref[:, h * GDN_DV:(h + 1) * GDN_DV] = o[h]
            state[h] = st2[h]

    return _pc(body, grid=(N,),
               in_specs=[pl.BlockSpec((GDN_C, GDN_QK), lambda n: (n, 0)), pl.BlockSpec((GDN_C, GDN_QK), lambda n: (n, 1)),
                         pl.BlockSpec((GDN_C, GDN_V), lambda n: (n, 1)), pl.BlockSpec((GDN_C, GDN_QK), lambda n: (n, 0)),
                         pl.BlockSpec((GDN_C, GDN_QK), lambda n: (n, 0))],
               out_specs=[pl.BlockSpec((GDN_C, GDN_V), lambda n: (n, 0)),
                          pl.BlockSpec((1, GDN_H, GDN_DK, GDN_DV), lambda n: (n, 0, 0, 0))],
               out_shape=[_sds((S_, GDN_V), F32), _sds((N, GDN_H, GDN_DK, GDN_DV), F32)],
               scratch_shapes=[pltpu.VMEM((GDN_H, GDN_DK, GDN_DV), F32)],
               compiler_params=_cparams(("arbitrary",)), name=name)(qkv, qkv, qkv, g_b, beta_b)


def gdn_chunk_bwd(qkv, g_b, beta_b, states, do, name):
    S_ = qkv.shape[0]
    N = S_ // GDN_C

    def body(q_ref, k_ref, v_ref, g_ref, b_ref, s_ref, do_ref, dq_ref, dk_ref, dv_ref, dg_ref, db_ref, dstate):
        @pl.when(pl.program_id(0) == 0)
        def _():
            dstate[...] = jnp.zeros_like(dstate)

        _, vjp = jax.vjp(_gdn_chunk, _head_slices(q_ref, GDN_DK), _head_slices(k_ref, GDN_DK), _head_slices(v_ref, GDN_DV),
                         _head_slices(g_ref, GDN_DK), _head_slices(b_ref, GDN_DK), [s_ref[0, h] for h in range(GDN_H)])
        dq, dk, dv, dg, db, ds = vjp((_head_slices(do_ref, GDN_DV), [dstate[h] for h in range(GDN_H)]))
        for h in range(GDN_H):
            kk, vv = slice(h * GDN_DK, (h + 1) * GDN_DK), slice(h * GDN_DV, (h + 1) * GDN_DV)
            dq_ref[:, kk] = dq[h]
            dk_ref[:, kk] = dk[h]
            dv_ref[:, vv] = dv[h]
            dg_ref[:, kk] = dg[h]
            db_ref[:, kk] = db[h]
            dstate[h] = ds[h]

    r = lambda n: N - 1 - n
    qk = lambda c: pl.BlockSpec((GDN_C, GDN_QK), lambda n: (r(n), c))
    vs = lambda c: pl.BlockSpec((GDN_C, GDN_V), lambda n: (r(n), c))
    return _pc(body, grid=(N,),
               in_specs=[qk(0), qk(1), vs(1), qk(0), qk(0),
                         pl.BlockSpec((1, GDN_H, GDN_DK, GDN_DV), lambda n: (r(n), 0, 0, 0)), vs(0)],
               out_specs=[qk(0), qk(0), vs(0), qk(0), qk(0)],
               out_shape=[_sds((S_, GDN_QK), F32), _sds((S_, GDN_QK), F32), _sds((S_, GDN_V), F32),
                          _sds((S_, GDN_QK), F32), _sds((S_, GDN_QK), F32)],
               scratch_shapes=[pltpu.VMEM((GDN_H, GDN_DK, GDN_DV), F32)],
               compiler_params=_cparams(("arbitrary",)), name=name)(qkv, qkv, qkv, g_b, beta_b, states, do)


def _rope_tables(pos_ref, inv_ref, cm_ref, sg_ref):
    ang = pos_ref[...] * inv_ref[...]
    return jnp.cos(ang) * cm_ref[...], jnp.sin(ang) * sg_ref[...]


def mla_prep_fwd(qpad, kv, proj, pos, rope_consts, name):
    S_ = qpad.shape[0]
    ts = 256
    W = 2 * LANES

    def body(q_ref, kv_ref, kr_ref, pos_ref, inv_ref, cm_ref, sg_ref, qh_ref, kh_ref, vh_ref):
        cs, sn = _rope_tables(pos_ref, inv_ref, cm_ref, sg_ref)
        rope = lambda r: r * cs + pltpu.roll(r, LANES // 2, 1) * sn
        krr = rope(kr_ref[...].astype(F32)).astype(BF16)
        for h in range(MLA_H):
            qh_ref[h, :, 0:LANES] = (q_ref[:, h * W:h * W + LANES].astype(F32) * MLA_SCALE).astype(BF16)
            qh_ref[h, :, LANES:W] = (rope(q_ref[:, h * W + LANES:(h + 1) * W].astype(F32)) * MLA_SCALE).astype(BF16)
            kh_ref[h, :, 0:LANES] = kv_ref[:, h * W:h * W + LANES].astype(BF16)
            kh_ref[h, :, LANES:W] = krr
            vh_ref[h] = kv_ref[:, h * W + LANES:(h + 1) * W].astype(BF16)

    one = pl.BlockSpec((1, LANES), lambda i: (0, 0))
    return _pc(body, grid=(S_ // ts,),
               in_specs=[pl.BlockSpec((ts, MLA_H * W), lambda i: (i, 0)), pl.BlockSpec((ts, MLA_H * W), lambda i: (i, 0)),
                         pl.BlockSpec((ts, LANES), lambda i: (i, 1536 // LANES)), pl.BlockSpec((ts, 1), lambda i: (i, 0)),
                         one, one, one],
               out_specs=[pl.BlockSpec((MLA_H, ts, W), lambda i: (0, i, 0)), pl.BlockSpec((MLA_H, ts, W), lambda i: (0, i, 0)),
                          pl.BlockSpec((MLA_H, ts, LANES), lambda i: (0, i, 0))],
               out_shape=[_sds((MLA_H, S_, W), BF16), _sds((MLA_H, S_, W), BF16), _sds((MLA_H, S_, LANES), BF16)],
               compiler_params=_cparams(("parallel",)), name=name)(qpad, kv, proj, pos, *rope_consts)


def mla_prep_bwd(dqh, dkh, dvh, pos, rope_consts, into, name):
    S_ = dqh.shape[1]
    ts = 256
    W = 2 * LANES

    def body(dq_ref, dk_ref, dv_ref, pos_ref, inv_ref, cm_ref, sg_ref, _, dqp_ref, dkv_ref, dkr_ref):
        cs, sn = _rope_tables(pos_ref, inv_ref, cm_ref, sg_ref)
        rope_t = lambda g: g * cs + pltpu.roll(g * sn, LANES // 2, 1)
        acc = jnp.zeros((ts, LANES), F32)
        for h in range(MLA_H):
            dqp_ref[:, h * W:h * W + LANES] = (dq_ref[h, :, 0:LANES].astype(F32) * MLA_SCALE).astype(BF16)
            dqp_ref[:, h * W + LANES:(h + 1) * W] = (rope_t(dq_ref[h, :, LANES:W].astype(F32)) * MLA_SCALE).astype(BF16)
            dkv_ref[:, h * W:h * W + LANES] = dk_ref[h, :, 0:LANES]
            dkv_ref[:, h * W + LANES:(h + 1) * W] = dv_ref[h]
            acc = acc + dk_ref[h, :, LANES:W].astype(F32)
        dkr_ref[...] = rope_t(acc).astype(dkr_ref.dtype)

    one = pl.BlockSpec((1, LANES), lambda i: (0, 0))
    return _pc(body, grid=(S_ // ts,),
               in_specs=[pl.BlockSpec((MLA_H, ts, W), lambda i: (0, i, 0)), pl.BlockSpec((MLA_H, ts, W), lambda i: (0, i, 0)),
                         pl.BlockSpec((MLA_H, ts, LANES), lambda i: (0, i, 0)), pl.BlockSpec((ts, 1), lambda i: (i, 0)),
                         one, one, one, pl.BlockSpec(memory_space=pl.ANY)],
               out_specs=[pl.BlockSpec((ts, MLA_H * W), lambda i: (i, 0)), pl.BlockSpec((ts, MLA_H * W), lambda i: (i, 0)),
                          pl.BlockSpec((ts, LANES), lambda i: (i, 1536 // LANES))],
               out_shape=[_sds((S_, MLA_H * W), BF16), _sds((S_, MLA_H * W), BF16), _sds(into.shape, into.dtype)],
               input_output_aliases={7: 2}, compiler_params=_cparams(("parallel",)), name=name)(dqh, dkh, dvh, pos, *rope_consts, into)


NEG = -1e30


FLASH_TILE = 1024
FLASH_SUB = 512


def _scores(q, k, diagonal):
    s = lax.dot_general(q, k, (_NT, ((), ())), preferred_element_type=F32)
    if not diagonal:
        return s
    return jnp.where(lax.broadcasted_iota(I32, s.shape, 1) <= lax.broadcasted_iota(I32, s.shape, 0), s, NEG)


def _sub_blocks(t, diagonal):
    sub = min(FLASH_SUB, t) if diagonal else t
    return [(c * sub if diagonal else 0, slice(c * sub, (c + 1) * sub)) for c in range(t // sub)]


FLASH_HEADS = 2


def flash_fwd(qh, kh, vh, name):
    H, S_, W = qh.shape
    t = _tile(S_, FLASH_TILE)
    n = S_ // t
    G = FLASH_HEADS
    heads = list(range(G))

    def body(q_ref, k_ref, v_ref, o_ref, lse_ref, m_s, l_s, acc):
        qi, kj = pl.program_id(1), pl.program_id(2)

        @pl.when(kj == 0)
        def _():
            m_s[...] = jnp.full_like(m_s, NEG)
            l_s[...] = jnp.zeros_like(l_s)
            acc[...] = jnp.zeros_like(acc)

        def step(diagonal):
            s = _each(lambda a: _scores(q_ref[a], k_ref[a], diagonal), heads)
            m_old = _each(lambda a: m_s[a], heads)
            m_new = _each(lambda mo, sa: jnp.maximum(mo, jnp.max(sa, axis=-1, keepdims=True)), m_old, s)
            alpha = _each(lambda mo, mn: jnp.exp(mo - mn), m_old, m_new)
            p = _each(lambda sa, mn: jnp.exp(sa - mn[:, :1]), s, m_new)
            pv = _each(lambda pa, a: lax.dot_general(pa.astype(BF16), v_ref[a], (_NN, ((), ())), preferred_element_type=F32), p, heads)
            for a in heads:
                l_s[a] = alpha[a] * l_s[a] + jnp.sum(p[a], axis=-1, keepdims=True)
                acc[a] = alpha[a] * acc[a] + pv[a]
                m_s[a] = m_new[a]

        pl.when(kj < qi)(lambda: step(False))
        pl.when(kj == qi)(lambda: step(True))

        @pl.when(kj == n - 1)
        def _():
            for a in heads:
                o_ref[:, a * LANES:(a + 1) * LANES] = (acc[a] / l_s[a]).astype(o_ref.dtype)
                lse_ref[a] = m_s[a] + jnp.log(l_s[a])

    return _pc(body, grid=(H // G, n, n),
               in_specs=[pl.BlockSpec((G, t, W), lambda h, i, j: (h, i, 0)),
                         pl.BlockSpec((G, t, W), lambda h, i, j: (h, jnp.minimum(i, j), 0)),
                         pl.BlockSpec((G, t, LANES), lambda h, i, j: (h, jnp.minimum(i, j), 0))],
               out_specs=[pl.BlockSpec((t, G * LANES), lambda h, i, j: (i, h)), pl.BlockSpec((G, t, LANES), lambda h, i, j: (h, i, 0))],
               out_shape=[_sds((S_, H * LANES), BF16), _sds((H, S_, LANES), F32)],
               scratch_shapes=[pltpu.VMEM((G, t, LANES), F32)] * 3,
               compiler_params=_cparams(("parallel", "parallel", "arbitrary")), name=name)(qh, kh, vh)


def flash_bwd(qh, kh, vh, o, lse, do, name):
    H, S_, W = qh.shape
    t = _tile(S_, FLASH_TILE)
    n = S_ // t

    def body(q_ref, k_ref, v_ref, o_ref, lse_ref, do_ref, dq_ref, dk_ref, dv_ref, dq_acc, dk_acc, dv_acc):
        kj, qi = pl.program_id(1), pl.program_id(2)

        @pl.when(jnp.logical_and(kj == 0, qi == 0))
        def _():
            dq_acc[...] = jnp.zeros_like(dq_acc)

        @pl.when(qi == 0)
        def _():
            dk_acc[...] = jnp.zeros_like(dk_acc)
            dv_acc[...] = jnp.zeros_like(dv_acc)

        def step(diagonal):
            do_ = do_ref[...]
            dob = do_.astype(BF16)
            delta = jnp.sum(do_.astype(F32) * o_ref[...].astype(F32), axis=-1, keepdims=True)
            for r0, keys in _sub_blocks(t, diagonal):
                q, k, v = q_ref[r0:, :], k_ref[keys, :], v_ref[keys, :]
                p = jnp.exp(_scores(q, k, diagonal) - lse_ref[r0:, :1])
                dv_acc[keys, :] += lax.dot_general(p.astype(BF16), dob[r0:], (_TN, ((), ())), preferred_element_type=F32)
                dp = lax.dot_general(dob[r0:], v, (_NT, ((), ())), preferred_element_type=F32)
                ds = (p * (dp - delta[r0:])).astype(BF16)
                dk_acc[keys, :] += lax.dot_general(ds, q, (_TN, ((), ())), preferred_element_type=F32)
                rows = pl.ds(pl.multiple_of(qi * t, t) + r0, t - r0)
                dq_acc[rows, :] += lax.dot_general(ds, k, (_NN, ((), ())), preferred_element_type=F32)

        pl.when(qi > kj)(lambda: step(False))
        pl.when(qi == kj)(lambda: step(True))

        @pl.when(qi == n - 1)
        def _():
            dk_ref[...] = dk_acc[...].astype(BF16)
            dv_ref[...] = dv_acc[...].astype(BF16)

        @pl.when(jnp.logical_and(kj == n - 1, qi == n - 1))
        def _():
            dq_ref[...] = dq_acc[...].astype(BF16)

    qrow = lambda h, j, i: jnp.maximum(i, j)
    return _pc(body, grid=(H, n, n),
               in_specs=[pl.BlockSpec((None, t, W), lambda h, j, i: (h, qrow(h, j, i), 0)),
                         pl.BlockSpec((None, t, W), lambda h, j, i: (h, j, 0)),
                         pl.BlockSpec((None, t, LANES), lambda h, j, i: (h, j, 0)),
                         pl.BlockSpec((t, LANES), lambda h, j, i: (qrow(h, j, i), h)),
                         pl.BlockSpec((None, t, LANES), lambda h, j, i: (h, qrow(h, j, i), 0)),
                         pl.BlockSpec((t, LANES), lambda h, j, i: (qrow(h, j, i), h))],
               out_specs=[pl.BlockSpec((None, S_, W), lambda h, j, i: (h, 0, 0)),
                          pl.BlockSpec((None, t, W), lambda h, j, i: (h, j, 0)),
                          pl.BlockSpec((None, t, LANES), lambda h, j, i: (h, j, 0))],
               out_shape=[_sds((H, S_, W), BF16), _sds((H, S_, W), BF16), _sds((H, S_, LANES), BF16)],
               scratch_shapes=[pltpu.VMEM((S_, W), F32), pltpu.VMEM((t, W), F32), pltpu.VMEM((t, LANES), F32)],
               compiler_params=_cparams(("parallel", "arbitrary", "arbitrary")), name=name)(qh, kh, vh, o, lse, do)


def loss_head(x, target, g, name):
    S_ = x.shape[0]
    ts = 512

    def body(x_ref, t_ref, g_ref, l_ref, dx_ref, dg_ref):
        @pl.when(pl.program_id(0) == 0)
        def _():
            l_ref[...] = jnp.zeros_like(l_ref)
            dg_ref[...] = jnp.zeros_like(dg_ref)

        y, vjp = jax.vjp(_rms, x_ref[...], g_ref[...])
        err = y - t_ref[...]
        l_ref[...] += 0.5 * jnp.sum(jnp.sum(err * err, axis=-1, keepdims=True), axis=0, keepdims=True) / D
        dx, dg = vjp(err / D)
        dx_ref[...] = dx
        dg_ref[...] += dg

    row = pl.BlockSpec((ts, D), lambda i: (i, 0))
    return _pc(body, grid=(S_ // ts,), in_specs=[row, row, pl.BlockSpec((1, D), lambda i: (0, 0))],
               out_specs=[pl.BlockSpec((1, LANES), lambda i: (0, 0)), row, pl.BlockSpec((1, D), lambda i: (0, 0))],
               out_shape=[_sds((1, LANES), F32), _sds((S_, D), F32), _sds((1, D), F32)],
               compiler_params=_cparams(("arbitrary",)), name=name)(x, target, g)


def adamw(w, parts, m, v, name):
    R, C = w.shape
    rows = [p.shape[1] for p in parts[0]]
    tr = R
    for cand in (512, 256, 128, 64, 32, 16, 8):
        if all(r % cand == 0 for r in rows) and cand * C * 4 <= 1024 * 1024:
            tr = cand
            break
    c1 = 1.0 - ADAM_B1 ** ADAM_STEP
    c2 = 1.0 - ADAM_B2 ** ADAM_STEP
    starts = [sum(rows[:k]) // tr for k in range(len(rows))]
    flat = [p for part in parts for p in part]

    def body(*refs):
        w_ref, m_ref, v_ref = refs[0], refs[1 + len(flat)], refs[2 + len(flat)]
        g_ref, d_ref, nm_ref, nv_ref = refs[3 + len(flat):]
        i = pl.program_id(0)
        gg, at = None, 1
        for part in parts:
            val = None
            for k in range(len(part)):
                p_ref = refs[at]
                at += 1
                s = p_ref[0].astype(F32)
                for n in range(1, p_ref.shape[0]):
                    s = s + p_ref[n].astype(F32)
                val = s if val is None else jnp.where(i >= starts[k], s, val)
            gg = val if gg is None else gg + val
        m2 = ADAM_B1 * m_ref[...] + (1.0 - ADAM_B1) * gg
        v2 = ADAM_B2 * v_ref[...] + (1.0 - ADAM_B2) * (gg * gg)
        g_ref[...] = gg
        d_ref[...] = -ADAM_LR * ((m2 / c1) / (jnp.sqrt(v2 / c2) + ADAM_EPS) + ADAM_WD * w_ref[...])
        nm_ref[...] = m2
        nv_ref[...] = v2

    blk = pl.BlockSpec((tr, C), lambda i: (i, 0))
    piece = lambda p, k: pl.BlockSpec((p.shape[0], tr, C), lambda i: (0, jnp.clip(i - starts[k], 0, rows[k] // tr - 1), 0))
    pblk = [piece(p, k) for part in parts for k, p in enumerate(part)]
    return _pc(body, grid=(R // tr,), in_specs=[blk] + pblk + [blk, blk], out_specs=[blk] * 4, out_shape=[_sds((R, C), F32)] * 4,
               compiler_params=_cparams(("parallel",)), name=name)(w, *flat, m, v)


def sum_slots(recv, name):
    n, R, C = recv.shape

    def body(r_ref, o_ref):
        acc = r_ref[0]
        for s in range(1, n):
            acc = acc + r_ref[s]
        o_ref[...] = acc

    return _pc(body, grid=(1,), in_specs=[pl.BlockSpec((n, R, C), lambda i: (0, 0, 0))],
               out_specs=pl.BlockSpec((R, C), lambda i: (0, 0)), out_shape=_sds((R, C), F32),
               compiler_params=_cparams(("arbitrary",)), name=name)(recv)


def _chip_peers():
    x, y, c = lax.axis_index("x"), lax.axis_index("y"), lax.axis_index("c")
    return (x, y, c), [(1 - x, y, c), (x, 1 - y, c), (1 - x, 1 - y, c)]


def _chip_index(p):
    return 2 * p[0] + p[1]


def _win(ref, axis, chip, size):
    if axis is None:
        return ref.at[chip]
    idx = [slice(None)] * len(ref.shape)
    idx[axis] = pl.ds(pl.multiple_of(chip * size, size), size)
    return ref.at[tuple(idx)]


def _remote(src, dst, send_sem, recv_sem, peer):
    return pltpu.make_async_remote_copy(src_ref=src, dst_ref=dst, send_sem=send_sem, recv_sem=recv_sem, device_id=peer,
                                        device_id_type=MESH)


HBM_SPEC = pl.BlockSpec(memory_space=pltpu.HBM)
SEM_SPEC = pl.BlockSpec(memory_space=pltpu.SEMAPHORE)
ANY_SPEC = pl.BlockSpec(memory_space=pl.ANY)
DATAFLOW = pltpu.SideEffectType.DATAFLOW_SIDE_EFFECTING


def gather_piece(i, l, o, axis, size):
    return (i, lambda r, chip: r.at[l], o, lambda r, chip: _win(r, axis, chip, size))


def scatter_piece(i, o, axis, size):
    return (i, lambda r, chip: _win(r, axis, chip, size), o, lambda r, chip: r.at[chip])


def whole_piece(i):
    return (i, lambda r, chip: r, i, lambda r, chip: r)


def _copies(pieces, in_refs, out_refs, send, recv, sibling):
    me, peers = _chip_peers()
    if sibling:
        peers = [(me[0], me[1], 1 - me[2])]
    mine = _chip_index(me)
    remote = []
    for n, (i, src, o, dst) in enumerate(pieces):
        d = dst(out_refs[o], mine)
        remote += [_remote(src(in_refs[i], _chip_index(p)), d, send.at[len(peers) * n + k], recv.at[len(peers) * n + k], p)
                   for k, p in enumerate(peers)]
    return remote


def own_window(a, axis, size, chip):
    if axis is None:
        return lax.dynamic_index_in_dim(a, chip, 0, keepdims=False)
    return lax.dynamic_slice_in_dim(a, chip * size, size, axis=axis)


def place_own(land, own, axis, size, chip):
    if axis is None:
        return lax.dynamic_update_slice_in_dim(land, own[None], chip, axis=0)
    return lax.dynamic_update_slice_in_dim(land, own, chip * size, axis=axis)


def exchange_start(pieces, ins, out_shapes, after, name, sibling=False):
    n_in, n_out, ncp = len(ins), len(out_shapes), len(pieces)

    def body(*refs):
        in_refs, land_refs = refs[:n_in], refs[n_in:n_in + n_out]
        send, recv = refs[n_in + n_out + 1], refs[n_in + n_out + 2]
        token = refs[-1]
        for cp in _copies(pieces, in_refs, land_refs, send, recv, sibling):
            cp.start()
        token[...] = jnp.zeros_like(token)

    hbm = lambda a: pltpu.with_memory_space_constraint(a, pltpu.HBM)
    lands = [hbm(lax.empty(s.shape, s.dtype)) for s in out_shapes]
    sem = pltpu.SemaphoreType.DMA(((1 if sibling else 3) * ncp,))
    thru = [pltpu.HBM(a.shape, a.dtype) for a in ins] + [pltpu.HBM(s.shape, s.dtype) for s in out_shapes]
    res = _pc(body, in_specs=[HBM_SPEC] * (n_in + n_out) + [ANY_SPEC],
              out_specs=[SEM_SPEC, SEM_SPEC] + [HBM_SPEC] * (n_in + n_out) + [pl.BlockSpec(memory_space=pltpu.VMEM)],
              out_shape=[sem, sem] + thru + [_sds((8, LANES), F32)],
              input_output_aliases={i: 2 + i for i in range(n_in + n_out)},
              compiler_params=pltpu.CompilerParams(has_side_effects=DATAFLOW), name=name)(*[hbm(a) for a in ins], *lands, after)
    return (res[0], res[1]), list(res[2:2 + n_in]), list(res[2 + n_in:2 + n_in + n_out]), res[-1]


def exchange_wait(pieces, sems, ins, lands, after, name, sibling=False):
    n_in, n_out = len(ins), len(lands)

    def body(*refs):
        in_refs, land_refs = refs[:n_in], refs[n_in:n_in + n_out]
        send, recv = refs[n_in + n_out], refs[n_in + n_out + 1]
        for cp in _copies(pieces, in_refs, land_refs, send, recv, sibling):
            cp.wait_send()
            cp.wait_recv()

    thru = [pltpu.HBM(a.shape, a.dtype) for a in ins] + [pltpu.HBM(a.shape, a.dtype) for a in lands]
    res = _pc(body, in_specs=[HBM_SPEC] * (n_in + n_out) + [SEM_SPEC, SEM_SPEC, ANY_SPEC], out_specs=[HBM_SPEC] * (n_in + n_out),
              out_shape=thru, input_output_aliases={i: i for i in range(n_in + n_out)},
              compiler_params=pltpu.CompilerParams(has_side_effects=DATAFLOW), name=name)(*ins, *lands, sems[0], sems[1], after)
    return list(res[:n_in]), list(res[n_in:])


def exchange_all(buf, name):
    def body(in_ref, out_ref, send, recv, local):
        x, y, c = lax.axis_index("x"), lax.axis_index("y"), lax.axis_index("c")
        mine = 4 * x + 2 * y + c
        loc = pltpu.make_async_copy(in_ref, out_ref.at[mine], local)
        loc.start()
        copies = [loc]
        for k in range(1, 8):
            peer = (x ^ (k >> 2), y ^ ((k >> 1) & 1), c ^ (k & 1))
            cp = pltpu.make_async_remote_copy(src_ref=in_ref, dst_ref=out_ref.at[mine], send_sem=send.at[k - 1],
                                              recv_sem=recv.at[k - 1], device_id=peer, device_id_type=MESH)
            cp.start()
            copies.append(cp)
        for cp in copies:
            cp.wait()

    anyspec = pl.BlockSpec(memory_space=pl.ANY)
    return _pc(body, in_specs=[anyspec], out_specs=anyspec, out_shape=_sds((8,) + buf.shape, buf.dtype),
               scratch_shapes=[pltpu.SemaphoreType.DMA((7,)), pltpu.SemaphoreType.DMA((7,)), pltpu.SemaphoreType.DMA],
               name=name)(buf)


def _norm_fwd(x, g, name):
    return rowwise(f_rms, [(x, D, 0, 0)], [(g, D, 0, 0)], [(D, 0, BF16)], ts=1024, name=name)[0]


def _norm_bwd(x, g, dh, dres, name):
    (dx,), (dg,) = rowwise_bwd(f_rms, [(x, D, 0, 0)], [(g, D, 0, 0)], [(dh, D, 0, 0)], need=[True],
                               adds={0: (dres, D, 0, 0)}, ts=512, name=name)
    return dx, dg


def pool_fwd(x, W, tag, late=None):
    h = _norm_fwd(x, W["ng"], tag + "_norm")
    proj = mm(h, W["w_in"], out_dtype=BF16, name=tag + "_in")
    if late is not None:
        W = dict(W, **late(proj))
    p = pool_time_fwd(proj, tag + "_win")
    pg = gmm("nn", p, W["w_grp"], G=4, out_dtype=BF16, name=tag + "_grp")
    y = rowwise(f_pool_gate, [(pg, POOL_GROUP, 0, 1), (proj, POOL_GROUP, 4, 1)], [(W["scale"], POOL_GROUP, 0, 1)],
                [(POOL_GROUP, 1, BF16)], ncol=4, ts=1024, name=tag + "_gate")[0]
    xn = mm(y, W["w_out"], add=x, name=tag + "_out")
    return xn, (x, h, proj, p, pg, y)


def pool_bwd(dxn, W, saved, tag, after=None, emit=None):
    x, h, proj, p, pg, y = saved
    emit = emit or (lambda grads: None)
    dy = mm(dxn, W["w_out"], tb=True, after=after, out_dtype=BF16, name=tag + "_dy")
    g = {}
    (dpg, dproj), (g["scale"],) = rowwise_bwd(
        f_pool_gate, [(pg, POOL_GROUP, 0, 1), (proj, POOL_GROUP, 4, 1)], [(W["scale"], POOL_GROUP, 0, 1)],
        [(dy, POOL_GROUP, 0, 1)], need=[True, True], place={1: (2 * POOL_WIDTH, 4)}, narrow=(0, 1), ncol=4, ts=1024, name=tag + "_dgate")
    dp = gmm("nt", dpg, W["w_grp"], G=4, out_dtype=BF16, name=tag + "_dp")
    dproj = pool_time_bwd(dp, dproj, tag + "_dwin")
    g["w_in"] = mm(h, dproj, ta=True, out_dtype=BF16, name=tag + "_dw_in")
    t1 = emit({"w_in": g["w_in"]})
    g["w_out"] = mm(y, dxn, ta=True, after=t1, out_dtype=BF16, name=tag + "_dwout")
    g["w_grp"] = gmm("tn", p, dpg, G=4, out_dtype=BF16, name=tag + "_dwgrp")
    t2 = emit({"w_out": g["w_out"], "w_grp": g["w_grp"]})
    dh = mm(dproj, W["w_in"], tb=True, after=t2, name=tag + "_dh")
    dx, g["ng"] = _norm_bwd(x, W["ng"], dh, dxn, tag + "_dnorm")
    return dx, g


def gdn_fwd(x, W, tag, late=None):
    h = _norm_fwd(x, W["ng"], tag + "_norm")
    proj = mm(h, W["w_in"], name=tag + "_in")
    qkv = gdn_conv_fwd(proj, W["conv"], tag + "_conv")
    g_b, beta_b = rowwise(f_gdn_gates, [(proj, LANES, 6144 // LANES, 0)], [(W["a_log"], LANES, 0, 0), (W["dt_bias"], LANES, 0, 0)],
                          [(GDN_QK, 0, F32), (GDN_QK, 0, F32)], ts=1024, name=tag + "_gates")
    o, states = gdn_chunk_fwd(qkv, g_b, beta_b, tag + "_chunk")
    og = rowwise(f_gdn_out, [(o, GDN_DV, 0, 1), (proj, GDN_DV, 4096 // GDN_DV, 1)], [(W["norm_g"], GDN_DV, 0, 0)],
                 [(GDN_DV, 1, BF16)], ncol=GDN_H, ts=2048, name=tag + "_onorm")[0]
    if late is not None:
        W = dict(W, **late(og))
    xn = mm(og, W["w_out"], add=x, name=tag + "_out")
    return xn, (x, h, proj, qkv, g_b, beta_b, o, states, og)


def gdn_bwd(dxn, W, saved, tag, after=None):
    x, h, proj, qkv, g_b, beta_b, o, states, og = saved
    dog = mm(dxn, W["w_out"], tb=True, after=after, name=tag + "_dog")
    g = {"w_out": mm(og, dxn, ta=True, out_dtype=BF16, name=tag + "_dwout")}
    (do, dproj), (g["norm_g"],) = rowwise_bwd(
        f_gdn_out, [(o, GDN_DV, 0, 1), (proj, GDN_DV, 4096 // GDN_DV, 1)], [(W["norm_g"], GDN_DV, 0, 0)],
        [(dog, GDN_DV, 0, 1)], need=[True, True], place={1: (GDN_IN_PAD, 4096 // GDN_DV)}, narrow=(1,), ncol=GDN_H, ts=2048, name=tag + "_donorm")
    dq, dk, dv, dg_b, dbeta_b = gdn_chunk_bwd(qkv, g_b, beta_b, states, do, tag + "_dchunk")
    (dproj,), (g["a_log"], g["dt_bias"]) = rowwise_bwd(
        f_gdn_gates, [(proj, LANES, 6144 // LANES, 0)], [(W["a_log"], LANES, 0, 0), (W["dt_bias"], LANES, 0, 0)],
        [(dg_b, GDN_QK, 0, 0), (dbeta_b, GDN_QK, 0, 0)], need=[True], place={0: (dproj, 6144 // LANES)}, ts=512, name=tag + "_dgates")
    dproj, g["conv"] = gdn_conv_bwd(proj, W["conv"], dq, dk, dv, dproj, tag + "_dconv")
    dh = mm(dproj, W["w_in"], tb=True, name=tag + "_dh")
    g["w_in"] = mm(h, dproj, ta=True, out_dtype=BF16, name=tag + "_dw_in")
    dx, g["ng"] = _norm_bwd(x, W["ng"], dh, dxn, tag + "_dnorm")
    return dx, g


def mla_fwd(x, pos, W, tag):
    h = _norm_fwd(x, W["ng"], tag + "_norm")
    proj = mm(h, W["w_in"], out_dtype=BF16, name=tag + "_in")
    hq = rowwise(f_rms, [(proj, MLA_Q_LORA, 0, 0)], [(W["q_g"], MLA_Q_LORA, 0, 0)], [(MLA_Q_LORA, 0, BF16)], ts=1024, name=tag + "_qnorm")[0]
    hkv = rowwise(f_rms, [(proj, MLA_KV_LORA, 2, 0)], [(W["kv_g"], MLA_KV_LORA, 0, 0)], [(MLA_KV_LORA, 0, BF16)], ts=1024, name=tag + "_kvnorm")[0]
    qpad = mm(hq, W["w_uq"], out_dtype=BF16, name=tag + "_uq")
    kv = mm(hkv, W["w_ukv"], out_dtype=BF16, name=tag + "_ukv")
    qh, kh, vh = mla_prep_fwd(qpad, kv, proj, pos, W["rope"], tag + "_prep")
    o, lse = flash_fwd(qh, kh, vh, tag + "_attn")
    og = rowwise(f_ogate, [(o, 512, 0, 1), (proj, 512, 4, 1)], [], [(512, 1, BF16)], ncol=4, ts=1024, name=tag + "_ogate")[0]
    xn = mm(og, W["w_out"], add=x, name=tag + "_out")
    return xn, (x, h, proj, hq, hkv, qh, kh, vh, o, lse, og)


def mla_bwd(dxn, pos, W, saved, tag, after=None):
    x, h, proj, hq, hkv, qh, kh, vh, o, lse, og = saved
    dog = mm(dxn, W["w_out"], tb=True, after=after, out_dtype=BF16, name=tag + "_dog")
    g = {"w_out": mm(og, dxn, ta=True, out_dtype=BF16, name=tag + "_dwout")}
    dproj = jnp.zeros(proj.shape, BF16)
    (do, dproj), _ = rowwise_bwd(f_ogate, [(o, 512, 0, 1), (proj, 512, 4, 1)], [], [(dog, 512, 0, 1)], need=[True, True],
                                 place={1: (dproj, 4)}, narrow=(0,), ncol=4, ts=1024, name=tag + "_dogate")
    dqh, dkh, dvh = flash_bwd(qh, kh, vh, o, lse, do, tag + "_dattn")
    dqpad, dkv, dproj = mla_prep_bwd(dqh, dkh, dvh, pos, W["rope"], dproj, tag + "_dprep")
    dhq = mm(dqpad, W["w_uq"], tb=True, name=tag + "_dhq")
    g["w_uq"] = mm(hq, dqpad, ta=True, out_dtype=BF16, name=tag + "_dwuq")
    dhkv = mm(dkv, W["w_ukv"], tb=True, name=tag + "_dhkv")
    g["w_ukv"] = mm(hkv, dkv, ta=True, out_dtype=BF16, name=tag + "_dwukv")
    (dproj,), (g["q_g"],) = rowwise_bwd(f_rms, [(proj, MLA_Q_LORA, 0, 0)], [(W["q_g"], MLA_Q_LORA, 0, 0)], [(dhq, MLA_Q_LORA, 0, 0)],
                                        need=[True], place={0: (dproj, 0)}, ts=512, name=tag + "_dqnorm")
    (dproj,), (g["kv_g"],) = rowwise_bwd(f_rms, [(proj, MLA_KV_LORA, 2, 0)], [(W["kv_g"], MLA_KV_LORA, 0, 0)], [(dhkv, MLA_KV_LORA, 0, 0)],
                                         need=[True], place={0: (dproj, 2)}, ts=512, name=tag + "_dkvnorm")
    dh = mm(dproj, W["w_in"], tb=True, name=tag + "_dh")
    g["w_in"] = mm(h, dproj, ta=True, out_dtype=BF16, name=tag + "_dw_in")
    dx, g["ng"] = _norm_bwd(x, W["ng"], dh, dxn, tag + "_dnorm")
    return dx, g


def _pad_cols(a, n):
    return jnp.pad(a, ((0, 0), (0, n - a.shape[1])))


def _mla_w_in_layout(w):
    z = lambda n: jnp.zeros((w.shape[0], n), w.dtype)
    kr = w[:, 1280:1344]
    return jnp.concatenate([w[:, :768], z(256), w[:, 768:1280], kr[:, :32], z(32), kr[:, 32:], z(32), z(384), w[:, 1344:]], axis=1)


def _mla_w_in_unlayout(g):
    return jnp.concatenate([g[:, :768], g[:, 1024:1536], g[:, 1536:1568], g[:, 1600:1632], g[:, 2048:]], axis=1)


def _mla_w_uq_layout(w):
    w3 = w.reshape(w.shape[0], MLA_H, MLA_NOPE + MLA_ROPE)
    z = jnp.zeros((w.shape[0], MLA_H, 32), w.dtype)
    return jnp.concatenate([w3[..., :128], w3[..., 128:160], z, w3[..., 160:192], z], axis=-1).reshape(w.shape[0], MLA_H * 256)


def _mla_w_uq_unlayout(g):
    g3 = g.reshape(g.shape[0], MLA_H, 256)
    return jnp.concatenate([g3[..., :128], g3[..., 128:160], g3[..., 192:224]], axis=-1).reshape(g.shape[0], MLA_H * 192)


def _rope_consts():
    half = MLA_ROPE // 2
    inv = ROPE_THETA ** (-jnp.arange(half, dtype=F32) / half)
    z = jnp.zeros((half,), F32)
    o = jnp.ones((half,), F32)
    row = lambda *p: jnp.concatenate(p).reshape(1, LANES)
    return row(inv, z, inv, z), row(o, z, o, z), row(-o, z, o, z)


BIG = ["pool_w_in", "pool_w_grp", "pool_w_out", "gdn_w_in", "gdn_w_out", "mla_w_in", "mla_w_uq", "mla_w_ukv", "mla_w_out"]
BIG_LAYOUT = {"pool_w_in": (1, 1024, (1024, 4096)), "pool_w_grp": (1, 128, (4, 512, 512)), "pool_w_out": (0, 512, (2048, 1024)),
              "gdn_w_in": (None, None, (4, 1024, 1540)), "gdn_w_out": (0, 512, (2048, 1024)),
              "mla_w_in": (None, None, (4, 1024, 848)), "mla_w_uq": (1, 768, (768, 3072)), "mla_w_ukv": (1, 1024, (512, 4096)),
              "mla_w_out": (0, 512, (2048, 1024))}
SMALL_SHARDED = ["pool_scale", "gdn_conv", "mla_q_norm_g", "mla_kv_norm_g"]
SMALL_AXIS = {"pool_scale": 1, "gdn_conv": 2, "mla_q_norm_g": 1, "mla_kv_norm_g": 1}
REPLICATED = ["norm_g", "gdn_a_log", "gdn_dt_bias", "gdn_norm_g", "final_g"]
PACK_C = 1024


def _pack(parts, dtype, row_mult):
    flat = jnp.concatenate([p.reshape(-1).astype(dtype) for p in parts])
    rows = -(-flat.shape[0] // PACK_C)
    rows = -(-rows // row_mult) * row_mult
    return jnp.pad(flat, (0, rows * PACK_C - flat.shape[0])).reshape(rows, PACK_C)


def _unpack(buf, shapes):
    lead = buf.shape[:-2]
    flat = buf.reshape(lead + (-1,))
    out, off = [], 0
    for s in shapes:
        n = int(np.prod(s))
        out.append(flat[..., off:off + n].reshape(lead + tuple(s)))
        off += n
    return out


def _unshard(g4, axis):
    a = jnp.moveaxis(g4, 0, axis)
    s = a.shape
    return a.reshape(s[:axis] + (s[axis] * s[axis + 1],) + s[axis + 2:])


def _to_shards(a, axis):
    s = a.shape
    return jnp.moveaxis(a.reshape(s[:axis] + (4, s[axis] // 4) + s[axis + 1:]), axis, 0)


def layer_weights(full, small, rep, layer):
    ng = rep["norm_g"][layer:layer + 1]
    side_by_side = lambda a4: jnp.moveaxis(a4, 0, 1).reshape(a4.shape[1], 4 * a4.shape[2])
    if layer in (0, 3):
        j = layer // 3
        return dict(ng=ng, w_in=full[("pool_w_in", j)], w_grp=full[("pool_w_grp", j)], scale=small["pool_scale"][j:j + 1],
                    w_out=full[("pool_w_out", j)])
    if layer == 1:
        return dict(ng=ng, w_in=_pad_cols(side_by_side(full[("gdn_w_in", 0)]), GDN_IN_PAD),
                    conv=jnp.pad(small["gdn_conv"][0], ((0, 4), (0, 0))), a_log=_pad_cols(rep["gdn_a_log"], LANES),
                    dt_bias=_pad_cols(rep["gdn_dt_bias"], LANES), norm_g=rep["gdn_norm_g"], w_out=full.get(("gdn_w_out", 0)))
    return dict(ng=ng, w_in=_mla_w_in_layout(side_by_side(full[("mla_w_in", 0)])), q_g=small["mla_q_norm_g"],
                kv_g=small["mla_kv_norm_g"], w_uq=_mla_w_uq_layout(full[("mla_w_uq", 0)]), w_ukv=full[("mla_w_ukv", 0)],
                w_out=full[("mla_w_out", 0)], rope=_rope_consts())


def big_grad_pieces(gl):
    g0, g1, g2, g3 = gl
    slots = lambda a: jnp.moveaxis(a.reshape(a.shape[0], 4, a.shape[1] // 4), 1, 0)
    out = {}
    for l, g in ((0, g0), (1, g3)):
        if g is not None:
            out.update({("pool_w_in", l): g["w_in"], ("pool_w_grp", l): g["w_grp"], ("pool_w_out", l): g["w_out"]})
    if g1 is not None:
        out.update({("gdn_w_in", 0): slots(g1["w_in"][:, :GDN_IN]), ("gdn_w_out", 0): g1["w_out"]})
    if g2 is not None:
        out.update({("mla_w_in", 0): slots(_mla_w_in_unlayout(g2["w_in"])), ("mla_w_uq", 0): _mla_w_uq_unlayout(g2["w_uq"]),
                    ("mla_w_ukv", 0): g2["w_ukv"], ("mla_w_out", 0): g2["w_out"]})
    return out


def small_grads(gl, dfinal):
    g0, g1, g2, g3 = gl
    return {"norm_g": jnp.concatenate([g0["ng"], g1["ng"], g2["ng"], g3["ng"]], axis=0),
            "pool_scale": jnp.concatenate([g0["scale"], g3["scale"]], axis=0), "gdn_conv": g1["conv"][None, :4],
            "gdn_a_log": g1["a_log"][:, :GDN_H], "gdn_dt_bias": g1["dt_bias"][:, :GDN_H], "gdn_norm_g": g1["norm_g"],
            "mla_q_norm_g": g2["q_g"], "mla_kv_norm_g": g2["kv_g"], "final_g": dfinal.reshape(D)}


NAMES = ["norm_g", "pool_w_in", "pool_w_grp", "pool_scale", "pool_w_out", "gdn_w_in", "gdn_conv", "gdn_a_log", "gdn_dt_bias",
         "gdn_norm_g", "gdn_w_out", "mla_w_in", "mla_q_norm_g", "mla_w_uq", "mla_kv_norm_g", "mla_w_ukv", "mla_w_out", "final_g"]


def kernel(x, positions, norm_g, pool_w_in, pool_w_grp, pool_scale, pool_w_out, gdn_w_in, gdn_conv, gdn_a_log, gdn_dt_bias, gdn_norm_g, gdn_w_out, mla_w_in, mla_q_norm_g, mla_w_uq, mla_kv_norm_g, mla_w_ukv, mla_w_out, final_g, loss_target, m_norm_g, m_pool_w_in, m_pool_w_grp, m_pool_scale, m_pool_w_out, m_gdn_w_in, m_gdn_conv, m_gdn_a_log, m_gdn_dt_bias, m_gdn_norm_g, m_gdn_w_out, m_mla_w_in, m_mla_q_norm_g, m_mla_w_uq, m_mla_kv_norm_g, m_mla_w_ukv, m_mla_w_out, m_final_g, v_norm_g, v_pool_w_in, v_pool_w_grp, v_pool_scale, v_pool_w_out, v_gdn_w_in, v_gdn_conv, v_gdn_a_log, v_gdn_dt_bias, v_gdn_norm_g, v_gdn_w_out, v_mla_w_in, v_mla_q_norm_g, v_mla_w_uq, v_mla_kv_norm_g, v_mla_w_ukv, v_mla_w_out, v_final_g):
    args = locals()
    w = {n: args[n] for n in NAMES}
    m = {n: args["m_" + n] for n in NAMES}
    v = {n: args["v_" + n] for n in NAMES}
    my_chip = (2 * lax.axis_index("x") + lax.axis_index("y")).astype(I32)
    S_ = x.shape[1]
    x0, pos, target = x[0], positions.reshape(S_, 1).astype(F32), loss_target[0]
    rep = {n: w[n] for n in REPLICATED}

    shard = {(n, l): w[n][l:l + 1].astype(BF16) for n in BIG for l in range(w[n].shape[0])}
    small_shapes = [w[n].shape for n in SMALL_SHARDED]
    shard[("small", 0)] = _pack([w[n] for n in SMALL_SHARDED], F32, 8)[None]
    layout = dict(BIG_LAYOUT, small=(None, None, (4,) + shard[("small", 0)].shape[1:]))

    def gather_start(group, after, tag):
        pieces = [gather_piece(i, 0, i, layout[n][0], layout[n][1]) for i, (n, l) in enumerate(group)]
        shapes = [_sds(layout[n][2], shard[(n, l)].dtype) for n, l in group]
        sems, ins, lands, token = exchange_start(pieces, [shard[k] for k in group], shapes, after, tag + "_start")
        return (pieces, sems, ins, lands), token

    def finish(handle, after, tag):
        return exchange_wait(*handle, after, tag + "_wait")

    def gathered(group, handle, after, tag):
        srcs, lands = finish(handle, after, tag)
        return {(n, l): place_own(a, s[0], layout[n][0], layout[n][1], my_chip) for (n, l), s, a in zip(group, srcs, lands)}

    group_a = [("small", 0), ("pool_w_in", 0)]
    group_a2 = [("pool_w_grp", 0), ("pool_w_out", 0)]
    group_b = [("gdn_w_in", 0)]
    group_c = [("gdn_w_out", 0), ("mla_w_in", 0), ("mla_w_uq", 0), ("mla_w_ukv", 0), ("mla_w_out", 0), ("pool_w_in", 1),
               ("pool_w_grp", 1), ("pool_w_out", 1)]
    full = {}
    h_a, t_a = gather_start(group_a, x0, "gather_a")
    h_a2, t_a2 = gather_start(group_a2, t_a, "gather_a2")
    h_b, t_b = gather_start(group_b, t_a2, "gather_b")
    h_c, t_c = gather_start(group_c, t_b, "gather_c")
    full.update(gathered(group_a, h_a, t_c, "gather_a"))
    small = {n: _unshard(a, SMALL_AXIS[n]) for n, a in zip(SMALL_SHARDED, _unpack(full[("small", 0)], small_shapes))}

    def late_l0(proj):
        full.update(gathered(group_a2, h_a2, proj, "gather_a2"))
        return dict(w_grp=full[("pool_w_grp", 0)], w_out=full[("pool_w_out", 0)])

    x1, s0 = pool_fwd(x0, dict(ng=rep["norm_g"][0:1], w_in=full[("pool_w_in", 0)], scale=small["pool_scale"][0:1]), "l0", late=late_l0)
    W0 = layer_weights(full, small, rep, 0)
    full.update(gathered(group_b, h_b, x1, "gather_b"))

    def late_l1(og):
        full.update(gathered(group_c, h_c, og, "gather_c"))
        return dict(w_out=full[("gdn_w_out", 0)])

    x2, s1 = gdn_fwd(x1, layer_weights(full, small, rep, 1), "l1", late=late_l1)
    W1, W2, W3 = (layer_weights(full, small, rep, i) for i in (1, 2, 3))
    x3, s2 = mla_fwd(x2, pos, W2, "l2")
    x4, s3 = pool_fwd(x3, W3, "l3")
    loss_part, dx4, dfinal = loss_head(x4, target, final_g.reshape(1, D), "loss_head")

    def scatter_start(pieces_of, after, tag):
        keys = list(pieces_of)
        pieces = [scatter_piece(i, i, BIG_LAYOUT[n][0], BIG_LAYOUT[n][1]) for i, (n, l) in enumerate(keys)]
        shapes = [_sds((4,) + tuple(w[n].shape[1:]), BF16) for n, l in keys]
        sems, ins, lands, token = exchange_start(pieces, [pieces_of[k] for k in keys], shapes, after, tag + "_start")
        return keys, (pieces, sems, ins, lands), token

    def scattered(keys, handle, after, tag):
        srcs, lands = finish(handle, after, tag)
        return {(n, l): place_own(a, own_window(g, BIG_LAYOUT[n][0], BIG_LAYOUT[n][1], my_chip), None, None, my_chip)
                for (n, l), g, a in zip(keys, srcs, lands)}

    dx3, g3 = pool_bwd(dx4, W3, s3, "l3")
    k3, h3, t3 = scatter_start(big_grad_pieces((None, None, None, g3)), dx3, "scatter_l3")
    dx2, g2 = mla_bwd(dx3, pos, W2, s2, "l2", after=t3)
    k2, h2, t2 = scatter_start(big_grad_pieces((None, None, g2, None)), dx2, "scatter_l2")
    dx1, g1 = gdn_bwd(dx2, W1, s1, "l1", after=t2)
    k1, h1, t1 = scatter_start(big_grad_pieces((None, g1, None, None)), dx1, "scatter_l1")
    def swap_start(part, tag):
        keys = list(part)
        ins = [part[k] for k in keys]
        pieces = [whole_piece(i) for i in range(len(keys))]
        sems, ins, lands, token = exchange_start(pieces, ins, [_sds(a.shape, a.dtype) for a in ins], ins[0], tag + "_start", sibling=True)
        swaps.append((keys, (pieces, sems, ins, lands), tag))
        return token

    last, swaps = [], []

    def emit_l0(grads):
        first = not last
        now = next(iter(grads.values()))
        early = [(k3, h3, "scatter_l3"), (k2, h2, "scatter_l2")] if first else [(k1, h1, "scatter_l1")]
        landed = {}
        for keys, handle, tag in early:
            landed.update(scattered(keys, handle, now, tag))
        swapping = swap_start(landed, "swap_a" if first else "swap_b")
        tag = "scatter_l0a" if first else "scatter_l0b"
        keys, handle, token = scatter_start({("pool_" + k, 0): a for k, a in grads.items()}, swapping, tag)
        last.append((keys, handle, tag))
        return token

    dx0, g0 = pool_bwd(dx1, W0, s0, "l0", after=t1, emit=emit_l0)
    landed = {}
    for keys, handle, tag in last:
        landed.update(scattered(keys, handle, dx0, tag))
    swap_start(landed, "swap_c")
    recv, sib = {}, {}
    for keys, handle, tag in swaps:
        mine_, theirs = exchange_wait(*handle, dx0, tag + "_wait", sibling=True)
        recv.update(zip(keys, mine_))
        sib.update(zip(keys, theirs))

    sg = small_grads((g0, g1, g2, g3), dfinal)
    small_names = SMALL_SHARDED + REPLICATED
    small_buf = _pack([sg[n] for n in small_names] + [loss_part], F32, 8)
    small_sum = sum_slots(exchange_all(small_buf, "gather_small"), "sum_small")
    full_small = _unpack(small_sum, [sg[n].shape for n in small_names] + [(1, LANES)])
    loss = full_small[-1][0, 0]
    small_part = {}
    for n, a in zip(small_names, full_small[:-1]):
        if n in SMALL_AXIS:
            a = lax.dynamic_index_in_dim(_to_shards(a, SMALL_AXIS[n]), my_chip, axis=0, keepdims=False)
        small_part[n] = a

    outs = []
    for n in NAMES:
        shp = w[n].shape
        two = (int(np.prod(shp[:-1])), shp[-1]) if len(shp) > 1 else (1, shp[0])
        if n in BIG_LAYOUT:
            layers = shp[0]
            rows = lambda a: a.reshape(4, two[0] // layers, two[1])
            parts = [[rows(recv[(n, l)]) for l in range(layers)], [rows(sib[(n, l)]) for l in range(layers)]]
        else:
            parts = [[small_part[n].reshape((1,) + two)]]
        res = adamw(w[n].reshape(two), parts, m[n].reshape(two), v[n].reshape(two), "adamw_" + n)
        outs.append([r.reshape(shp) for r in res])
    return (loss, dx0[None], *[o[0] for o in outs], *[o[1] for o in outs], *[o[2] for o in outs], *[o[3] for o in outs])
```

```python
import math

import jax
import jax.numpy as jnp
import numpy as np
from jax import lax
from jax.experimental import pallas as pl
from jax.experimental.pallas import tpu as pltpu

F32 = jnp.float32
BF16 = jnp.bfloat16
I32 = jnp.int32

D = 1024
EPS = 1e-6
POOL_WIDTH = 2048
POOL_GROUP = 512
GDN_H, GDN_DK, GDN_DV, GDN_C = 8, 128, 256, 64
GDN_QK, GDN_V, GDN_CONV_CH, GDN_IN = 1024, 2048, 4096, 6160
GDN_IN_PAD = 6272
MLA_H, MLA_NOPE, MLA_ROPE, MLA_V = 16, 128, 64, 128
MLA_Q_LORA, MLA_KV_LORA, MLA_WIDTH, MLA_IN = 768, 512, 2048, 3392
MLA_IN_PAD = 4096
MLA_SCALE = (MLA_NOPE + MLA_ROPE) ** -0.5
ROPE_THETA = 10000.0
ADAM_LR, ADAM_B1, ADAM_B2, ADAM_EPS, ADAM_WD, ADAM_STEP = 0.001, 0.9, 0.999, 1e-08, 0.01, 10

VMEM_LIMIT_V7X = 56 * 1024 * 1024
LANES = 128
MESH = pl.DeviceIdType.MESH


def _pc(body, **kw):
    return pl.pallas_call(body, **kw)


def _cparams(sem):
    return pltpu.CompilerParams(dimension_semantics=sem, vmem_limit_bytes=VMEM_LIMIT_V7X)


def _tile(n, cap):
    t = (cap // LANES) * LANES
    while t >= LANES:
        if n % t == 0:
            return t
        t -= LANES
    return n


def _sds(shape, dt):
    return jax.ShapeDtypeStruct(shape, dt)


def mm(a, b, *, ta=False, tb=False, add=None, after=None, out_dtype=F32, name):
    if ta:
        K, M = a.shape
    else:
        M, K = a.shape
    if tb:
        N, K2 = b.shape
    else:
        K2, N = b.shape
    assert K == K2, (a.shape, b.shape, ta, tb)
    tm, tn, tk = _tile(M, 1024), _tile(N, 1024), _tile(K, 1024)
    nk = K // tk
    a_spec = pl.BlockSpec((tk, tm), lambda i, j, k: (k, i)) if ta else pl.BlockSpec((tm, tk), lambda i, j, k: (i, k))
    b_spec = pl.BlockSpec((tn, tk), lambda i, j, k: (j, k)) if tb else pl.BlockSpec((tk, tn), lambda i, j, k: (k, j))
    o_spec = pl.BlockSpec((tm, tn), lambda i, j, k: (i, j))
    dn = (((0 if ta else 1,), (1 if tb else 0,)), ((), ()))
    has_add = add is not None

    def body(*refs):
        a_ref, b_ref = refs[0], refs[1]
        part = lax.dot_general(a_ref[...].astype(BF16), b_ref[...].astype(BF16), dn, preferred_element_type=F32)
        if nk == 1:
            refs[-1][...] = (part + refs[2][...] if has_add else part).astype(out_dtype)
            return
        o_ref, acc = refs[-2], refs[-1]
        k = pl.program_id(2)

        @pl.when(k == 0)
        def _():
            acc[...] = part

        @pl.when(k > 0)
        def _():
            acc[...] += part

        @pl.when(k == nk - 1)
        def _():
            r = acc[...]
            if has_add:
                r = r + refs[2][...]
            o_ref[...] = r.astype(out_dtype)

    ins = [a, b] + ([add] if has_add else []) + ([after] if after is not None else [])
    specs = [a_spec, b_spec] + ([o_spec] if has_add else []) + ([pl.BlockSpec(memory_space=pl.ANY)] if after is not None else [])
    return _pc(body, grid=(M // tm, N // tn, nk), in_specs=specs, out_specs=o_spec, out_shape=_sds((M, N), out_dtype),
               scratch_shapes=[pltpu.VMEM((tm, tn), F32)] if nk > 1 else [], compiler_params=_cparams(("parallel", "parallel", "arbitrary")),
               name=name)(*ins)


def gmm(kind, a, b, *, G, name, out_dtype=F32):
    S_ = a.shape[0]
    Ka = a.shape[1] // G
    if kind == "tn":
        N = b.shape[1] // G
        tk = _tile(S_, 2048)
        nk = S_ // tk

        def body(a_ref, b_ref, o_ref, acc):
            k = pl.program_id(1)

            @pl.when(k == 0)
            def _():
                acc[...] = jnp.zeros_like(acc)

            acc[...] += lax.dot_general(a_ref[...].astype(BF16), b_ref[...].astype(BF16), (((0,), (0,)), ((), ())),
                                        preferred_element_type=F32)

            @pl.when(k == nk - 1)
            def _():
                o_ref[...] = acc[...].astype(out_dtype)

        return _pc(body, grid=(G, nk),
                   in_specs=[pl.BlockSpec((tk, Ka), lambda g, k: (k, g)), pl.BlockSpec((tk, N), lambda g, k: (k, g))],
                   out_specs=pl.BlockSpec((None, Ka, N), lambda g, k: (g, 0, 0)), out_shape=_sds((G, Ka, N), out_dtype),
                   scratch_shapes=[pltpu.VMEM((Ka, N), F32)], compiler_params=_cparams(("parallel", "arbitrary")), name=name)(a, b)
    N = b.shape[2] if kind == "nn" else b.shape[1]
    tm = _tile(S_, 4096)
    dn = (((1,), (0 if kind == "nn" else 1,)), ((), ()))

    def body(a_ref, b_ref, o_ref):
        o_ref[...] = lax.dot_general(a_ref[...].astype(BF16), b_ref[...].astype(BF16), dn, preferred_element_type=F32).astype(out_dtype)

    bshape = (None,) + tuple(b.shape[1:])
    return _pc(body, grid=(G, S_ // tm),
               in_specs=[pl.BlockSpec((tm, Ka), lambda g, i: (i, g)), pl.BlockSpec(bshape, lambda g, i: (g, 0, 0))],
               out_specs=pl.BlockSpec((tm, N), lambda g, i: (i, g)), out_shape=_sds((S_, G * N), out_dtype),
               compiler_params=_cparams(("parallel", "parallel")), name=name)(a, b)


def _rw_spec(ts, w, c, s):
    return pl.BlockSpec((ts, w), lambda j, i: (i, c + j * s))


def _rw_pspec(p, w, c, s):
    return pl.BlockSpec((p.shape[0], w), lambda j, i: (0, c + j * s))


def rowwise(f, tiles, params, outs, *, ncol=1, ts, name):
    S_ = tiles[0][0].shape[0]
    nin = len(tiles) + len(params)

    def body(*refs):
        res = f(pl.program_id(0), *[r[...].astype(F32) for r in refs[:nin]])
        for r, o in zip(refs[nin:], res):
            r[...] = o.astype(r.dtype)

    return _pc(body, grid=(ncol, S_ // ts),
               in_specs=[_rw_spec(ts, w, c, s) for (_, w, c, s) in tiles] + [_rw_pspec(*p) for p in params],
               out_specs=[_rw_spec(ts, w, 0, s) for (w, s, _) in outs],
               out_shape=[_sds((S_, w * (ncol if s else 1)), dt) for (w, s, dt) in outs],
               compiler_params=_cparams(("parallel", "parallel")), name=name)(*[t[0] for t in tiles], *[p[0] for p in params])


def rowwise_bwd(f, tiles, params, cots, *, need, adds=None, place=None, narrow=(), ncol=1, ts, name):
    S_ = tiles[0][0].shape[0]
    adds = adds or {}
    place = place or {}
    nt, npar, nc = len(tiles), len(params), len(cots)
    add_keys = sorted(adds)
    need_idx = [k for k in range(nt) if need[k]]
    into_keys = [k for k in need_idx if k in place and not isinstance(place[k][0], int)]
    n_extra = len(add_keys) + len(into_keys)

    def body(*refs):
        j, i = pl.program_id(0), pl.program_id(1)
        vals = [r[...].astype(F32) for r in refs[:nt + npar]]
        cvals = tuple(r[...].astype(F32) for r in refs[nt + npar:nt + npar + nc])
        add_refs = refs[nt + npar + nc:nt + npar + nc + len(add_keys)]
        out_refs = refs[nt + npar + nc + n_extra:]
        _, vjp = jax.vjp(lambda *v: tuple(f(j, *v)), *vals)
        grads = vjp(cvals)
        for n, k in enumerate(need_idx):
            g = grads[k]
            if k in adds:
                g = g + add_refs[add_keys.index(k)][...]
            out_refs[n][...] = g.astype(out_refs[n].dtype)
        for n in range(npar):
            ref = out_refs[len(need_idx) + n]
            first = (i == 0) if params[n][3] else jnp.logical_and(i == 0, j == 0)

            @pl.when(first)
            def _():
                ref[...] = jnp.zeros_like(ref)

            ref[...] += grads[nt + n]

    in_specs = ([_rw_spec(ts, w, c, s) for (_, w, c, s) in tiles] + [_rw_pspec(*p) for p in params]
                + [_rw_spec(ts, w, c, s) for (_, w, c, s) in cots] + [_rw_spec(ts, *adds[k][1:]) for k in add_keys]
                + [pl.BlockSpec(memory_space=pl.ANY) for _ in into_keys])
    out_specs, out_shape, aliases = [], [], {}
    for n, k in enumerate(need_idx):
        w, s = tiles[k][1], tiles[k][3]
        if k in place:
            dst, c0 = place[k]
            total = dst if isinstance(dst, int) else dst.shape[1]
            out_specs.append(_rw_spec(ts, w, c0, s))
            out_shape.append(_sds((S_, total), (BF16 if k in narrow else F32) if isinstance(dst, int) else dst.dtype))
            if k in into_keys:
                aliases[nt + npar + nc + len(add_keys) + into_keys.index(k)] = n
        else:
            out_specs.append(_rw_spec(ts, w, 0, s))
            out_shape.append(_sds((S_, w * (ncol if s else 1)), BF16 if k in narrow else F32))
    out_specs += [_rw_pspec(p[0], p[1], p[2], p[3]) for p in params]
    out_shape += [_sds(p[0].shape, F32) for p in params]
    res = _pc(body, grid=(ncol, S_ // ts), in_specs=in_specs, out_specs=out_specs, out_shape=out_shape,
              input_output_aliases=aliases, compiler_params=_cparams(("arbitrary", "arbitrary")), name=name)(
        *[t[0] for t in tiles], *[p[0] for p in params], *[c[0] for c in cots], *[adds[k][0] for k in add_keys],
        *[place[k][0] for k in into_keys])
    return list(res[:len(need_idx)]), list(res[len(need_idx):])


def _rms(x, g):
    r = lax.rsqrt(jnp.mean(x * x, axis=-1, keepdims=True) + EPS)
    return x * r * g


def _silu(x):
    return x * jax.nn.sigmoid(x)


@jax.custom_vjp
def _softplus(x):
    return jnp.maximum(x, 0.0) + jnp.log1p(jnp.exp(-jnp.abs(x)))


_softplus.defvjp(lambda x: (_softplus(x), x), lambda x, d: (d * jax.nn.sigmoid(x),))


def f_rms(j, x, g):
    return (_rms(x, g),)


def f_pool_gate(j, pg, gate, scale):
    return (pg * scale * _silu(gate),)


def f_ogate(j, o, gate):
    return (o * _silu(gate),)


def f_gdn_out(j, o, gate, g):
    return (_rms(o, g) * _silu(gate),)


def f_gdn_gates(j, ba, alog, dtb):
    lane = lax.broadcasted_iota(I32, (1, LANES), 1)
    gs, bs = [], []
    for h in range(GDN_H):
        eb = (lane == h).astype(F32)
        ea = (lane == GDN_H + h).astype(F32)
        b = jnp.sum(ba * eb, -1, keepdims=True)
        a = jnp.sum(ba * ea, -1, keepdims=True)
        al = jnp.sum(alog * eb, -1, keepdims=True)
        dt = jnp.sum(dtb * eb, -1, keepdims=True)
        g = -jnp.exp(al) * _softplus(a + dt)
        gs.append(jnp.broadcast_to(g, ba.shape))
        bs.append(jnp.broadcast_to(jax.nn.sigmoid(b), ba.shape))
    return jnp.concatenate(gs, 1), jnp.concatenate(bs, 1)


def _shift_dn(x, k):
    rows = lax.broadcasted_iota(I32, x.shape, 0)
    return jnp.where(rows < k, 0.0, pltpu.roll(x, k, 0))


def _shift_up(x, k):
    n = x.shape[0]
    rows = lax.broadcasted_iota(I32, x.shape, 0)
    return jnp.where(rows >= n - k, 0.0, pltpu.roll(x, n - k, 0))


def _pool_window(j):
    g = lax.div(j, POOL_GROUP // LANES)
    return jnp.where(g == 0, 2.0, jnp.where(g == 1, 4.0, jnp.where(g == 2, 8.0, 16.0))), g


def _pick(g, a2, a4, a8, a16):
    return jnp.where(g == 0, a2, jnp.where(g == 1, a4, jnp.where(g == 2, a8, a16)))


def pool_time_fwd(proj, name):
    S_ = proj.shape[0]

    def body(u_ref, p_ref):
        u = u_ref[...].astype(F32)
        w, g = _pool_window(pl.program_id(0))
        s2 = u + _shift_dn(u, 1)
        s4 = s2 + _shift_dn(s2, 2)
        s8 = s4 + _shift_dn(s4, 4)
        s16 = s8 + _shift_dn(s8, 8)
        t1 = (lax.broadcasted_iota(I32, u.shape, 0) + 1).astype(F32)
        p_ref[...] = (_pick(g, s2, s4, s8, s16) / jnp.minimum(t1, w) - u).astype(p_ref.dtype)

    return _pc(body, grid=(POOL_WIDTH // LANES,), in_specs=[pl.BlockSpec((S_, LANES), lambda j: (0, j))],
               out_specs=pl.BlockSpec((S_, LANES), lambda j: (0, j)), out_shape=_sds((S_, POOL_WIDTH), BF16),
               compiler_params=_cparams(("parallel",)), name=name)(proj)


def pool_time_bwd(dp, into, name):
    S_ = dp.shape[0]

    def body(dp_ref, _, du_ref):
        d = dp_ref[...].astype(F32)
        w, g = _pool_window(pl.program_id(0))
        t1 = (lax.broadcasted_iota(I32, d.shape, 0) + 1).astype(F32)
        q = d / jnp.minimum(t1, w)
        r2 = q + _shift_up(q, 1)
        r4 = r2 + _shift_up(r2, 2)
        r8 = r4 + _shift_up(r4, 4)
        r16 = r8 + _shift_up(r8, 8)
        du_ref[...] = (_pick(g, r2, r4, r8, r16) - d).astype(du_ref.dtype)

    return _pc(body, grid=(POOL_WIDTH // LANES,),
               in_specs=[pl.BlockSpec((S_, LANES), lambda j: (0, j)), pl.BlockSpec(memory_space=pl.ANY)],
               out_specs=pl.BlockSpec((S_, LANES), lambda j: (0, j)), out_shape=_sds(into.shape, into.dtype),
               input_output_aliases={1: 0}, compiler_params=_cparams(("parallel",)), name=name)(dp, into)


def _conv_post(j, a):
    n = a * lax.rsqrt(jnp.sum(a * a, axis=-1, keepdims=True) + EPS)
    nq = GDN_QK // LANES
    return jnp.where(j < nq, n * (GDN_DK ** -0.5), jnp.where(j < 2 * nq, n, a))


def _conv_taps(u):
    return [_shift_dn(u, 3), _shift_dn(u, 2), _shift_dn(u, 1), u]


def _conv_pre(taps, w):
    return w[0:1] * taps[0] + w[1:2] * taps[1] + w[2:3] * taps[2] + w[3:4] * taps[3]


def gdn_conv_fwd(proj, conv_w, name):
    S_ = proj.shape[0]

    def body(u_ref, w_ref, o_ref):
        o_ref[...] = _conv_post(pl.program_id(0), _silu(_conv_pre(_conv_taps(u_ref[...]), w_ref[...])))

    return _pc(body, grid=(GDN_CONV_CH // LANES,),
               in_specs=[pl.BlockSpec((S_, LANES), lambda j: (0, j)), pl.BlockSpec((8, LANES), lambda j: (0, j))],
               out_specs=pl.BlockSpec((S_, LANES), lambda j: (0, j)), out_shape=_sds((S_, GDN_CONV_CH), F32),
               compiler_params=_cparams(("parallel",)), name=name)(proj, conv_w)


def gdn_conv_bwd(proj, conv_w, dq, dk, dv, into, name):
    S_ = proj.shape[0]
    nq = GDN_QK // LANES

    def body(u_ref, w_ref, dq_ref, dk_ref, dv_ref, _, du_ref, dw_ref):
        j = pl.program_id(0)
        u, w = u_ref[...], w_ref[...]
        taps = _conv_taps(u)
        c = _conv_pre(taps, w)
        sig = jax.nn.sigmoid(c)
        dout = jnp.where(j < nq, dq_ref[...], jnp.where(j < 2 * nq, dk_ref[...], dv_ref[...]))
        _, vjp = jax.vjp(lambda a: _conv_post(j, a), c * sig)
        dc = vjp(dout)[0] * (sig * (1.0 + c * (1.0 - sig)))
        du = w[3:4] * dc + w[2:3] * _shift_up(dc, 1) + w[1:2] * _shift_up(dc, 2) + w[0:1] * _shift_up(dc, 3)
        du_ref[...] = du.astype(du_ref.dtype)
        rows = lax.broadcasted_iota(I32, (8, LANES), 0)
        dw = jnp.zeros((8, LANES), F32)
        for k in range(4):
            dw = dw + jnp.where(rows == k, jnp.sum(dc * taps[k], axis=0, keepdims=True), 0.0)
        dw_ref[...] = dw

    blk = lambda f: pl.BlockSpec((S_, LANES), f)
    return _pc(body, grid=(GDN_CONV_CH // LANES,),
               in_specs=[blk(lambda j: (0, j)), pl.BlockSpec((8, LANES), lambda j: (0, j)),
                         blk(lambda j: (0, jnp.minimum(j, nq - 1))), blk(lambda j: (0, jnp.clip(j - nq, 0, nq - 1))),
                         blk(lambda j: (0, jnp.clip(j - 2 * nq, 0, 2 * nq - 1))), pl.BlockSpec(memory_space=pl.ANY)],
               out_specs=[blk(lambda j: (0, j)), pl.BlockSpec((8, LANES), lambda j: (0, j))],
               out_shape=[_sds(into.shape, into.dtype), _sds((8, GDN_CONV_CH), F32)], input_output_aliases={5: 0},
               compiler_params=_cparams(("parallel",)), name=name)(proj, conv_w, dq, dk, dv, into)


_NN, _NT, _TN = ((1,), (0,)), ((1,), (1,)), ((0,), (0,))


def _split(x, n):
    parts = []
    for _ in range(n):
        h = x.astype(BF16)
        parts.append(h)
        x = x - h.astype(F32)
    return parts


def _dot(a, b, dn, mode):
    d = lambda p, q: lax.dot_general(p, q, (dn, ((), ())), preferred_element_type=F32)
    if mode == "lo":
        return d(a.astype(BF16), b.astype(BF16))
    if mode == "x3":
        (ah, al), (bh, bl) = _split(a, 2), _split(b, 2)
        return d(ah, bh) + (d(ah, bl) + d(al, bh))
    b0, b1, b2 = _split(b, 3)
    ab = a.astype(BF16)
    return d(ab, b0) + (d(ab, b1) + d(ab, b2))


def _make_dots(mode):
    @jax.custom_vjp
    def nn(a, b):
        return _dot(a, b, _NN, mode)

    @jax.custom_vjp
    def nt(a, b):
        return _dot(a, b, _NT, mode)

    @jax.custom_vjp
    def tn(a, b):
        return _dot(a, b, _TN, mode)

    nn.defvjp(lambda a, b: (nn(a, b), (a, b)), lambda r, d: (nt(d, r[1]), tn(r[0], d)))
    nt.defvjp(lambda a, b: (nt(a, b), (a, b)), lambda r, d: (nn(d, r[1]), tn(d, r[0])))
    tn.defvjp(lambda a, b: (tn(a, b), (a, b)), lambda r, d: (nt(r[1], d), nn(r[0], d)))
    return nn, nt, tn


_nn_hi, _nt_hi, _tn_hi = _make_dots("x3")
_nn_lo, _nt_lo, _tn_lo = _make_dots("lo")


@jax.custom_vjp
def _nn_const(a, b):
    return _dot(a, b, _NN, "xl")


_nn_const.defvjp(lambda a, b: (_nn_const(a, b), a), lambda a, d: (jnp.zeros_like(a), _dot(a, d, _TN, "xl")))


def _each(f, *lists):
    return [f(*xs) for xs in zip(*lists)]


@jax.custom_vjp
def _unit_inverses(xs):
    C = xs[0].shape[0]
    eye = (lax.broadcasted_iota(I32, (C, C), 0) == lax.broadcasted_iota(I32, (C, C), 1)).astype(F32)
    ainv, p = [eye + a for a in xs], xs
    for _ in range(int(math.log2(C)) - 1):
        p = _each(lambda a: _dot(a, a, _NN, "x3"), p)
        ainv = _each(lambda a, b: a + _dot(a, b, _NN, "x3"), ainv, p)
    return ainv


def _unit_inverses_bwd(ainv, d):
    left = _each(lambda a, g: _dot(a, g, _TN, "x3"), ainv, d)
    return (_each(lambda t, a: _dot(t, a, _NT, "x3"), left, ainv),)


_unit_inverses.defvjp(lambda xs: (lambda a: (a, a))(_unit_inverses(xs)), _unit_inverses_bwd)


def _gdn_chunk(q, k, v, gb, bb, state):
    C = GDN_C
    e0 = (lax.broadcasted_iota(I32, (1, LANES), 1) == 0).astype(F32)
    ri = lax.broadcasted_iota(I32, (C, C), 0)
    ci = lax.broadcasted_iota(I32, (C, C), 1)
    causal, strict = ri >= ci, ri > ci
    tri, eye, ones = causal.astype(F32), (ri == ci).astype(F32), jnp.ones((C, C), F32)
    last = lax.broadcasted_iota(I32, (C, LANES), 0) == C - 1
    g1 = _each(lambda a: jnp.sum(a * e0, -1, keepdims=True), gb)
    b1 = _each(lambda a: jnp.sum(a * e0, -1, keepdims=True), bb)
    gc_c = _each(lambda g: _nn_const(tri, jnp.broadcast_to(g, (C, C))), g1)
    gc_d = _each(lambda g: _nn_const(tri, jnp.broadcast_to(g, (C, LANES))), g1)
    gr_c = _each(lambda g: _nn_const(ones, eye * g), gc_c)
    decay = _each(lambda a, r: jnp.where(causal, jnp.exp(jnp.where(causal, a - r, 0.0)), 0.0), gc_c, gr_c)
    kb = _each(lambda a, b: a * b, k, b1)
    vb = _each(lambda a, b: a * b, v, b1)
    x = _each(lambda a, b, d: -jnp.where(strict, _nt_lo(a, b) * d, 0.0), kb, k, decay)
    ainv = _unit_inverses(x)
    u = _each(_nn_hi, ainv, vb)
    w = _each(lambda a, b, g: _nn_hi(a, b * jnp.exp(g)), ainv, kb, gc_d)
    attn = _each(lambda a, b, d: jnp.where(causal, _nt_lo(a, b) * d, 0.0), q, k, decay)
    v_new = _each(lambda a, b, s: a - _nn_lo(b, s), u, w, state)
    o = _each(lambda a, g, s, t, vn: _nn_lo(a * jnp.exp(g), s) + _nn_lo(t, vn), q, gc_d, state, attn, v_new)
    gl = _each(lambda g: jnp.sum(jnp.where(last, g, 0.0), axis=0, keepdims=True), gc_d)
    new_state = _each(lambda s, g, a, gd, vn: s * jnp.exp(jnp.sum(g * e0, -1, keepdims=True)) + _tn_lo(a * jnp.exp(g - gd), vn),
                      state, gl, k, gc_d, v_new)
    return o, new_state


def _head_slices(ref, width):
    return [ref[:, h * width:(h + 1) * width] for h in range(GDN_H)]


def gdn_chunk_fwd(qkv, g_b, beta_b, name):
    S_ = qkv.shape[0]
    N = S_ // GDN_C

    def body(q_ref, k_ref, v_ref, g_ref, b_ref, o_ref, s_ref, state):
        @pl.when(pl.program_id(0) == 0)
        def _():
            state[...] = jnp.zeros_like(state)

        st = [state[h] for h in range(GDN_H)]
        s_ref[0] = state[...]
        o, st2 = _gdn_chunk(_head_slices(q_ref, GDN_DK), _head_slices(k_ref, GDN_DK), _head_slices(v_ref, GDN_DV),
                            _head_slices(g_ref, GDN_DK), _head_slices(b_ref, GDN_DK), st)
        for h in range(GDN_H):
            o_ref[:, h * GDN_DV:(h + 1) * GDN_DV] = o[h]
            state[h] = st2[h]

    return _pc(body, grid=(N,),
               in_specs=[pl.BlockSpec((GDN_C, GDN_QK), lambda n: (n, 0)), pl.BlockSpec((GDN_C, GDN_QK), lambda n: (n, 1)),
                         pl.BlockSpec((GDN_C, GDN_V), lambda n: (n, 1)), pl.BlockSpec((GDN_C, GDN_QK), lambda n: (n, 0)),
                         pl.BlockSpec((GDN_C, GDN_QK), lambda n: (n, 0))],
               out_specs=[pl.BlockSpec((GDN_C, GDN_V), lambda n: (n, 0)),
                          pl.BlockSpec((1, GDN_H, GDN_DK, GDN_DV), lambda n: (n, 0, 0, 0))],
               out_shape=[_sds((S_, GDN_V), F32), _sds((N, GDN_H, GDN_DK, GDN_DV), F32)],
               scratch_shapes=[pltpu.VMEM((GDN_H, GDN_DK, GDN_DV), F32)],
               compiler_params=_cparams(("arbitrary",)), name=name)(qkv, qkv, qkv, g_b, beta_b)


def gdn_chunk_bwd(qkv, g_b, beta_b, states, do, name):
    S_ = qkv.shape[0]
    N = S_ // GDN_C

    def body(q_ref, k_ref, v_ref, g_ref, b_ref, s_ref, do_ref, dq_ref, dk_ref, dv_ref, dg_ref, db_ref, dstate):
        @pl.when(pl.program_id(0) == 0)
        def _():
            dstate[...] = jnp.zeros_like(dstate)

        _, vjp = jax.vjp(_gdn_chunk, _head_slices(q_ref, GDN_DK), _head_slices(k_ref, GDN_DK), _head_slices(v_ref, GDN_DV),
                         _head_slices(g_ref, GDN_DK), _head_slices(b_ref, GDN_DK), [s_ref[0, h] for h in range(GDN_H)])
        dq, dk, dv, dg, db, ds = vjp((_head_slices(do_ref, GDN_DV), [dstate[h] for h in range(GDN_H)]))
        for h in range(GDN_H):
            kk, vv = slice(h * GDN_DK, (h + 1) * GDN_DK), slice(h * GDN_DV, (h + 1) * GDN_DV)
            dq_ref[:, kk] = dq[h]
            dk_ref[:, kk] = dk[h]
            dv_ref[:, vv] = dv[h]
            dg_ref[:, kk] = dg[h]
            db_ref[:, kk] = db[h]
            dstate[h] = ds[h]

    r = lambda n: N - 1 - n
    qk = lambda c: pl.BlockSpec((GDN_C, GDN_QK), lambda n: (r(n), c))
    vs = lambda c: pl.BlockSpec((GDN_C, GDN_V), lambda n: (r(n), c))
    return _pc(body, grid=(N,),
               in_specs=[qk(0), qk(1), vs(1), qk(0), qk(0),
                         pl.BlockSpec((1, GDN_H, GDN_DK, GDN_DV), lambda n: (r(n), 0, 0, 0)), vs(0)],
               out_specs=[qk(0), qk(0), vs(0), qk(0), qk(0)],
               out_shape=[_sds((S_, GDN_QK), F32), _sds((S_, GDN_QK), F32), _sds((S_, GDN_V), F32),
                          _sds((S_, GDN_QK), F32), _sds((S_, GDN_QK), F32)],
               scratch_shapes=[pltpu.VMEM((GDN_H, GDN_DK, GDN_DV), F32)],
               compiler_params=_cparams(("arbitrary",)), name=name)(qkv, qkv, qkv, g_b, beta_b, states, do)


def _rope_tables(pos_ref, inv_ref, cm_ref, sg_ref):
    ang = pos_ref[...] * inv_ref[...]
    return jnp.cos(ang) * cm_ref[...], jnp.sin(ang) * sg_ref[...]


def mla_prep_fwd(qpad, kv, proj, pos, rope_consts, name):
    S_ = qpad.shape[0]
    ts = 512
    W = 2 * LANES

    def body(q_ref, kv_ref, kr_ref, pos_ref, inv_ref, cm_ref, sg_ref, qh_ref, kh_ref, vh_ref):
        cs, sn = _rope_tables(pos_ref, inv_ref, cm_ref, sg_ref)
        rope = lambda r: r * cs + pltpu.roll(r, LANES // 2, 1) * sn
        krr = rope(kr_ref[...].astype(F32)).astype(BF16)
        for h in range(MLA_H):
            qh_ref[h, :, 0:LANES] = (q_ref[:, h * W:h * W + LANES].astype(F32) * MLA_SCALE).astype(BF16)
            qh_ref[h, :, LANES:W] = (rope(q_ref[:, h * W + LANES:(h + 1) * W].astype(F32)) * MLA_SCALE).astype(BF16)
            kh_ref[h, :, 0:LANES] = kv_ref[:, h * W:h * W + LANES].astype(BF16)
            kh_ref[h, :, LANES:W] = krr
            vh_ref[h] = kv_ref[:, h * W + LANES:(h + 1) * W].astype(BF16)

    one = pl.BlockSpec((1, LANES), lambda i: (0, 0))
    return _pc(body, grid=(S_ // ts,),
               in_specs=[pl.BlockSpec((ts, MLA_H * W), lambda i: (i, 0)), pl.BlockSpec((ts, MLA_H * W), lambda i: (i, 0)),
                         pl.BlockSpec((ts, LANES), lambda i: (i, 1536 // LANES)), pl.BlockSpec((ts, 1), lambda i: (i, 0)),
                         one, one, one],
               out_specs=[pl.BlockSpec((MLA_H, ts, W), lambda i: (0, i, 0)), pl.BlockSpec((MLA_H, ts, W), lambda i: (0, i, 0)),
                          pl.BlockSpec((MLA_H, ts, LANES), lambda i: (0, i, 0))],
               out_shape=[_sds((MLA_H, S_, W), BF16), _sds((MLA_H, S_, W), BF16), _sds((MLA_H, S_, LANES), BF16)],
               compiler_params=_cparams(("parallel",)), name=name)(qpad, kv, proj, pos, *rope_consts)


def mla_prep_bwd(dqh, dkh, dvh, pos, rope_consts, into, name):
    S_ = dqh.shape[1]
    ts = 512
    W = 2 * LANES

    def body(dq_ref, dk_ref, dv_ref, pos_ref, inv_ref, cm_ref, sg_ref, _, dqp_ref, dkv_ref, dkr_ref):
        cs, sn = _rope_tables(pos_ref, inv_ref, cm_ref, sg_ref)
        rope_t = lambda g: g * cs + pltpu.roll(g * sn, LANES // 2, 1)
        acc = jnp.zeros((ts, LANES), F32)
        for h in range(MLA_H):
            dqp_ref[:, h * W:h * W + LANES] = (dq_ref[h, :, 0:LANES].astype(F32) * MLA_SCALE).astype(BF16)
            dqp_ref[:, h * W + LANES:(h + 1) * W] = (rope_t(dq_ref[h, :, LANES:W].astype(F32)) * MLA_SCALE).astype(BF16)
            dkv_ref[:, h * W:h * W + LANES] = dk_ref[h, :, 0:LANES]
            dkv_ref[:, h * W + LANES:(h + 1) * W] = dv_ref[h]
            acc = acc + dk_ref[h, :, LANES:W].astype(F32)
        dkr_ref[...] = rope_t(acc).astype(dkr_ref.dtype)

    one = pl.BlockSpec((1, LANES), lambda i: (0, 0))
    return _pc(body, grid=(S_ // ts,),
               in_specs=[pl.BlockSpec((MLA_H, ts, W), lambda i: (0, i, 0)), pl.BlockSpec((MLA_H, ts, W), lambda i: (0, i, 0)),
                         pl.BlockSpec((MLA_H, ts, LANES), lambda i: (0, i, 0)), pl.BlockSpec((ts, 1), lambda i: (i, 0)),
                         one, one, one, pl.BlockSpec(memory_space=pl.ANY)],
               out_specs=[pl.BlockSpec((ts, MLA_H * W), lambda i: (i, 0)), pl.BlockSpec((ts, MLA_H * W), lambda i: (i, 0)),
                          pl.BlockSpec((ts, LANES), lambda i: (i, 1536 // LANES))],
               out_shape=[_sds((S_, MLA_H * W), BF16), _sds((S_, MLA_H * W), BF16), _sds(into.shape, into.dtype)],
               input_output_aliases={7: 2}, compiler_params=_cparams(("parallel",)), name=name)(dqh, dkh, dvh, pos, *rope_consts, into)


NEG = -1e30


FLASH_TILE = 1024
FLASH_SUB = 512


def _scores(q, k, diagonal):
    s = lax.dot_general(q, k, (_NT, ((), ())), preferred_element_type=F32)
    if not diagonal:
        return s
    return jnp.where(lax.broadcasted_iota(I32, s.shape, 1) <= lax.broadcasted_iota(I32, s.shape, 0), s, NEG)


def _sub_blocks(t, diagonal):
    sub = min(FLASH_SUB, t) if diagonal else t
    return [(c * sub if diagonal else 0, slice(c * sub, (c + 1) * sub)) for c in range(t // sub)]


FLASH_HEADS = 2


def flash_fwd(qh, kh, vh, name):
    H, S_, W = qh.shape
    t = _tile(S_, FLASH_TILE)
    n = S_ // t
    G = FLASH_HEADS
    heads = list(range(G))

    def body(q_ref, k_ref, v_ref, o_ref, lse_ref, m_s, l_s, acc):
        qi, kj = pl.program_id(1), pl.program_id(2)

        @pl.when(kj == 0)
        def _():
            m_s[...] = jnp.full_like(m_s, NEG)
            l_s[...] = jnp.zeros_like(l_s)
            acc[...] = jnp.zeros_like(acc)

        def step(diagonal):
            s = _each(lambda a: _scores(q_ref[a], k_ref[a], diagonal), heads)
            m_old = _each(lambda a: m_s[a], heads)
            m_new = _each(lambda mo, sa: jnp.maximum(mo, jnp.max(sa, axis=-1, keepdims=True)), m_old, s)
            alpha = _each(lambda mo, mn: jnp.exp(mo - mn), m_old, m_new)
            p = _each(lambda sa, mn: jnp.exp(sa - mn[:, :1]), s, m_new)
            pv = _each(lambda pa, a: lax.dot_general(pa.astype(BF16), v_ref[a], (_NN, ((), ())), preferred_element_type=F32), p, heads)
            for a in heads:
                l_s[a] = alpha[a] * l_s[a] + jnp.sum(p[a], axis=-1, keepdims=True)
                acc[a] = alpha[a] * acc[a] + pv[a]
                m_s[a] = m_new[a]

        pl.when(kj < qi)(lambda: step(False))
        pl.when(kj == qi)(lambda: step(True))

        @pl.when(kj == n - 1)
        def _():
            for a in heads:
                o_ref[:, a * LANES:(a + 1) * LANES] = (acc[a] / l_s[a]).astype(o_ref.dtype)
                lse_ref[a] = m_s[a] + jnp.log(l_s[a])

    return _pc(body, grid=(H // G, n, n),
               in_specs=[pl.BlockSpec((G, t, W), lambda h, i, j: (h, i, 0)),
                         pl.BlockSpec((G, t, W), lambda h, i, j: (h, jnp.minimum(i, j), 0)),
                         pl.BlockSpec((G, t, LANES), lambda h, i, j: (h, jnp.minimum(i, j), 0))],
               out_specs=[pl.BlockSpec((t, G * LANES), lambda h, i, j: (i, h)), pl.BlockSpec((G, t, LANES), lambda h, i, j: (h, i, 0))],
               out_shape=[_sds((S_, H * LANES), BF16), _sds((H, S_, LANES), F32)],
               scratch_shapes=[pltpu.VMEM((G, t, LANES), F32)] * 3,
               compiler_params=_cparams(("parallel", "parallel", "arbitrary")), name=name)(qh, kh, vh)


def flash_bwd(qh, kh, vh, o, lse, do, name):
    H, S_, W = qh.shape
    t = _tile(S_, FLASH_TILE)
    n = S_ // t

    def body(q_ref, k_ref, v_ref, o_ref, lse_ref, do_ref, dq_ref, dk_ref, dv_ref, dq_acc, dk_acc, dv_acc):
        kj, qi = pl.program_id(1), pl.program_id(2)

        @pl.when(jnp.logical_and(kj == 0, qi == 0))
        def _():
            dq_acc[...] = jnp.zeros_like(dq_acc)

        @pl.when(qi == 0)
        def _():
            dk_acc[...] = jnp.zeros_like(dk_acc)
            dv_acc[...] = jnp.zeros_like(dv_acc)

        def step(diagonal):
            do_ = do_ref[...]
            dob = do_.astype(BF16)
            delta = jnp.sum(do_.astype(F32) * o_ref[...].astype(F32), axis=-1, keepdims=True)
            for r0, keys in _sub_blocks(t, diagonal):
                q, k, v = q_ref[r0:, :], k_ref[keys, :], v_ref[keys, :]
                p = jnp.exp(_scores(q, k, diagonal) - lse_ref[r0:, :1])
                dv_acc[keys, :] += lax.dot_general(p.astype(BF16), dob[r0:], (_TN, ((), ())), preferred_element_type=F32)
                dp = lax.dot_general(dob[r0:], v, (_NT, ((), ())), preferred_element_type=F32)
                ds = (p * (dp - delta[r0:])).astype(BF16)
                dk_acc[keys, :] += lax.dot_general(ds, q, (_TN, ((), ())), preferred_element_type=F32)
                rows = pl.ds(pl.multiple_of(qi * t, t) + r0, t - r0)
                dq_acc[rows, :] += lax.dot_general(ds, k, (_NN, ((), ())), preferred_element_type=F32)

        pl.when(qi > kj)(lambda: step(False))
        pl.when(qi == kj)(lambda: step(True))

        @pl.when(qi == n - 1)
        def _():
            dk_ref[...] = dk_acc[...].astype(BF16)
            dv_ref[...] = dv_acc[...].astype(BF16)

        @pl.when(jnp.logical_and(kj == n - 1, qi == n - 1))
        def _():
            dq_ref[...] = dq_acc[...].astype(BF16)

    qrow = lambda h, j, i: jnp.maximum(i, j)
    return _pc(body, grid=(H, n, n),
               in_specs=[pl.BlockSpec((None, t, W), lambda h, j, i: (h, qrow(h, j, i), 0)),
                         pl.BlockSpec((None, t, W), lambda h, j, i: (h, j, 0)),
                         pl.BlockSpec((None, t, LANES), lambda h, j, i: (h, j, 0)),
                         pl.BlockSpec((t, LANES), lambda h, j, i: (qrow(h, j, i), h)),
                         pl.BlockSpec((None, t, LANES), lambda h, j, i: (h, qrow(h, j, i), 0)),
                         pl.BlockSpec((t, LANES), lambda h, j, i: (qrow(h, j, i), h))],
               out_specs=[pl.BlockSpec((None, S_, W), lambda h, j, i: (h, 0, 0)),
                          pl.BlockSpec((None, t, W), lambda h, j, i: (h, j, 0)),
                          pl.BlockSpec((None, t, LANES), lambda h, j, i: (h, j, 0))],
               out_shape=[_sds((H, S_, W), BF16), _sds((H, S_, W), BF16), _sds((H, S_, LANES), BF16)],
               scratch_shapes=[pltpu.VMEM((S_, W), F32), pltpu.VMEM((t, W), F32), pltpu.VMEM((t, LANES), F32)],
               compiler_params=_cparams(("parallel", "arbitrary", "arbitrary")), name=name)(qh, kh, vh, o, lse, do)


def loss_head(x, target, g, name):
    S_ = x.shape[0]
    ts = 512

    def body(x_ref, t_ref, g_ref, l_ref, dx_ref, dg_ref):
        @pl.when(pl.program_id(0) == 0)
        def _():
            l_ref[...] = jnp.zeros_like(l_ref)
            dg_ref[...] = jnp.zeros_like(dg_ref)

        y, vjp = jax.vjp(_rms, x_ref[...], g_ref[...])
        err = y - t_ref[...]
        l_ref[...] += 0.5 * jnp.sum(jnp.sum(err * err, axis=-1, keepdims=True), axis=0, keepdims=True) / D
        dx, dg = vjp(err / D)
        dx_ref[...] = dx
        dg_ref[...] += dg

    row = pl.BlockSpec((ts, D), lambda i: (i, 0))
    return _pc(body, grid=(S_ // ts,), in_specs=[row, row, pl.BlockSpec((1, D), lambda i: (0, 0))],
               out_specs=[pl.BlockSpec((1, LANES), lambda i: (0, 0)), row, pl.BlockSpec((1, D), lambda i: (0, 0))],
               out_shape=[_sds((1, LANES), F32), _sds((S_, D), F32), _sds((1, D), F32)],
               compiler_params=_cparams(("arbitrary",)), name=name)(x, target, g)


def adamw(w, parts, m, v, name):
    R, C = w.shape
    rows = [p.shape[1] for p in parts[0]]
    tr = R
    for cand in (512, 256, 128, 64, 32, 16, 8):
        if all(r % cand == 0 for r in rows) and cand * C * 4 * len(rows) <= 2 * 1024 * 1024:
            tr = cand
            break
    c1 = 1.0 - ADAM_B1 ** ADAM_STEP
    c2 = 1.0 - ADAM_B2 ** ADAM_STEP
    starts = [sum(rows[:k]) // tr for k in range(len(rows))]
    flat = [p for part in parts for p in part]

    def body(*refs):
        w_ref, m_ref, v_ref = refs[0], refs[1 + len(flat)], refs[2 + len(flat)]
        g_ref, d_ref, nm_ref, nv_ref = refs[3 + len(flat):]
        i = pl.program_id(0)
        gg, at = None, 1
        for part in parts:
            val = None
            for k in range(len(part)):
                p_ref = refs[at]
                at += 1
                s = p_ref[0].astype(F32)
                for n in range(1, p_ref.shape[0]):
                    s = s + p_ref[n].astype(F32)
                val = s if val is None else jnp.where(i >= starts[k], s, val)
            gg = val if gg is None else gg + val
        m2 = ADAM_B1 * m_ref[...] + (1.0 - ADAM_B1) * gg
        v2 = ADAM_B2 * v_ref[...] + (1.0 - ADAM_B2) * (gg * gg)
        g_ref[...] = gg
        d_ref[...] = -ADAM_LR * ((m2 / c1) / (jnp.sqrt(v2 / c2) + ADAM_EPS) + ADAM_WD * w_ref[...])
        nm_ref[...] = m2
        nv_ref[...] = v2

    blk = pl.BlockSpec((tr, C), lambda i: (i, 0))
    piece = lambda p, k: pl.BlockSpec((p.shape[0], tr, C), lambda i: (0, jnp.clip(i - starts[k], 0, rows[k] // tr - 1), 0))
    pblk = [piece(p, k) for part in parts for k, p in enumerate(part)]
    return _pc(body, grid=(R // tr,), in_specs=[blk] + pblk + [blk, blk], out_specs=[blk] * 4, out_shape=[_sds((R, C), F32)] * 4,
               compiler_params=_cparams(("parallel",)), name=name)(w, *flat, m, v)


def sum_slots(recv, name):
    n, R, C = recv.shape

    def body(r_ref, o_ref):
        acc = r_ref[0]
        for s in range(1, n):
            acc = acc + r_ref[s]
        o_ref[...] = acc

    return _pc(body, grid=(1,), in_specs=[pl.BlockSpec((n, R, C), lambda i: (0, 0, 0))],
               out_specs=pl.BlockSpec((R, C), lambda i: (0, 0)), out_shape=_sds((R, C), F32),
               compiler_params=_cparams(("arbitrary",)), name=name)(recv)


def _chip_peers():
    x, y, c = lax.axis_index("x"), lax.axis_index("y"), lax.axis_index("c")
    return (x, y, c), [(1 - x, y, c), (x, 1 - y, c), (1 - x, 1 - y, c)]


def _chip_index(p):
    return 2 * p[0] + p[1]


def _win(ref, axis, chip, size):
    if axis is None:
        return ref.at[chip]
    idx = [slice(None)] * len(ref.shape)
    idx[axis] = pl.ds(pl.multiple_of(chip * size, size), size)
    return ref.at[tuple(idx)]


def _remote(src, dst, send_sem, recv_sem, peer):
    return pltpu.make_async_remote_copy(src_ref=src, dst_ref=dst, send_sem=send_sem, recv_sem=recv_sem, device_id=peer,
                                        device_id_type=MESH)


HBM_SPEC = pl.BlockSpec(memory_space=pltpu.HBM)
SEM_SPEC = pl.BlockSpec(memory_space=pltpu.SEMAPHORE)
ANY_SPEC = pl.BlockSpec(memory_space=pl.ANY)
DATAFLOW = pltpu.SideEffectType.DATAFLOW_SIDE_EFFECTING


def gather_piece(i, l, o, axis, size):
    return (i, lambda r, chip: r.at[l], o, lambda r, chip: _win(r, axis, chip, size))


def scatter_piece(i, o, axis, size):
    return (i, lambda r, chip: _win(r, axis, chip, size), o, lambda r, chip: r.at[chip])


def whole_piece(i):
    return (i, lambda r, chip: r, i, lambda r, chip: r)


def _copies(pieces, in_refs, out_refs, send, recv, sibling):
    me, peers = _chip_peers()
    if sibling:
        peers = [(me[0], me[1], 1 - me[2])]
    mine = _chip_index(me)
    remote = []
    for n, (i, src, o, dst) in enumerate(pieces):
        d = dst(out_refs[o], mine)
        remote += [_remote(src(in_refs[i], _chip_index(p)), d, send.at[len(peers) * n + k], recv.at[len(peers) * n + k], p)
                   for k, p in enumerate(peers)]
    return remote


def own_window(a, axis, size, chip):
    if axis is None:
        return lax.dynamic_index_in_dim(a, chip, 0, keepdims=False)
    return lax.dynamic_slice_in_dim(a, chip * size, size, axis=axis)


def place_own(land, own, axis, size, chip):
    if axis is None:
        return lax.dynamic_update_slice_in_dim(land, own[None], chip, axis=0)
    return lax.dynamic_update_slice_in_dim(land, own, chip * size, axis=axis)


def exchange_start(pieces, ins, out_shapes, after, name, sibling=False):
    n_in, n_out, ncp = len(ins), len(out_shapes), len(pieces)

    def body(*refs):
        in_refs, land_refs = refs[:n_in], refs[n_in:n_in + n_out]
        send, recv = refs[n_in + n_out + 1], refs[n_in + n_out + 2]
        token = refs[-1]
        for cp in _copies(pieces, in_refs, land_refs, send, recv, sibling):
            cp.start()
        token[...] = jnp.zeros_like(token)

    hbm = lambda a: pltpu.with_memory_space_constraint(a, pltpu.HBM)
    lands = [hbm(lax.empty(s.shape, s.dtype)) for s in out_shapes]
    sem = pltpu.SemaphoreType.DMA(((1 if sibling else 3) * ncp,))
    thru = [pltpu.HBM(a.shape, a.dtype) for a in ins] + [pltpu.HBM(s.shape, s.dtype) for s in out_shapes]
    res = _pc(body, in_specs=[HBM_SPEC] * (n_in + n_out) + [ANY_SPEC],
              out_specs=[SEM_SPEC, SEM_SPEC] + [HBM_SPEC] * (n_in + n_out) + [pl.BlockSpec(memory_space=pltpu.VMEM)],
              out_shape=[sem, sem] + thru + [_sds((8, LANES), F32)],
              input_output_aliases={i: 2 + i for i in range(n_in + n_out)},
              compiler_params=pltpu.CompilerParams(has_side_effects=DATAFLOW), name=name)(*[hbm(a) for a in ins], *lands, after)
    return (res[0], res[1]), list(res[2:2 + n_in]), list(res[2 + n_in:2 + n_in + n_out]), res[-1]


def exchange_wait(pieces, sems, ins, lands, after, name, sibling=False):
    n_in, n_out = len(ins), len(lands)

    def body(*refs):
        in_refs, land_refs = refs[:n_in], refs[n_in:n_in + n_out]
        send, recv = refs[n_in + n_out], refs[n_in + n_out + 1]
        for cp in _copies(pieces, in_refs, land_refs, send, recv, sibling):
            cp.wait_send()
            cp.wait_recv()

    thru = [pltpu.HBM(a.shape, a.dtype) for a in ins] + [pltpu.HBM(a.shape, a.dtype) for a in lands]
    res = _pc(body, in_specs=[HBM_SPEC] * (n_in + n_out) + [SEM_SPEC, SEM_SPEC, ANY_SPEC], out_specs=[HBM_SPEC] * (n_in + n_out),
              out_shape=thru, input_output_aliases={i: i for i in range(n_in + n_out)},
              compiler_params=pltpu.CompilerParams(has_side_effects=DATAFLOW), name=name)(*ins, *lands, sems[0], sems[1], after)
    return list(res[:n_in]), list(res[n_in:])


def exchange_all(buf, name):
    def body(in_ref, out_ref, send, recv, local):
        x, y, c = lax.axis_index("x"), lax.axis_index("y"), lax.axis_index("c")
        mine = 4 * x + 2 * y + c
        loc = pltpu.make_async_copy(in_ref, out_ref.at[mine], local)
        loc.start()
        copies = [loc]
        for k in range(1, 8):
            peer = (x ^ (k >> 2), y ^ ((k >> 1) & 1), c ^ (k & 1))
            cp = pltpu.make_async_remote_copy(src_ref=in_ref, dst_ref=out_ref.at[mine], send_sem=send.at[k - 1],
                                              recv_sem=recv.at[k - 1], device_id=peer, device_id_type=MESH)
            cp.start()
            copies.append(cp)
        for cp in copies:
            cp.wait()

    anyspec = pl.BlockSpec(memory_space=pl.ANY)
    return _pc(body, in_specs=[anyspec], out_specs=anyspec, out_shape=_sds((8,) + buf.shape, buf.dtype),
               scratch_shapes=[pltpu.SemaphoreType.DMA((7,)), pltpu.SemaphoreType.DMA((7,)), pltpu.SemaphoreType.DMA],
               name=name)(buf)


def _norm_fwd(x, g, name):
    return rowwise(f_rms, [(x, D, 0, 0)], [(g, D, 0, 0)], [(D, 0, BF16)], ts=1024, name=name)[0]


def _norm_bwd(x, g, dh, dres, name):
    (dx,), (dg,) = rowwise_bwd(f_rms, [(x, D, 0, 0)], [(g, D, 0, 0)], [(dh, D, 0, 0)], need=[True],
                               adds={0: (dres, D, 0, 0)}, ts=1024, name=name)
    return dx, dg


def pool_fwd(x, W, tag, late=None):
    h = _norm_fwd(x, W["ng"], tag + "_norm")
    proj = mm(h, W["w_in"], out_dtype=BF16, name=tag + "_in")
    if late is not None:
        W = dict(W, **late(proj))
    p = pool_time_fwd(proj, tag + "_win")
    pg = gmm("nn", p, W["w_grp"], G=4, out_dtype=BF16, name=tag + "_grp")
    y = rowwise(f_pool_gate, [(pg, POOL_GROUP, 0, 1), (proj, POOL_GROUP, 4, 1)], [(W["scale"], POOL_GROUP, 0, 1)],
                [(POOL_GROUP, 1, BF16)], ncol=4, ts=1024, name=tag + "_gate")[0]
    xn = mm(y, W["w_out"], add=x, name=tag + "_out")
    return xn, (x, h, proj, p, pg, y)


def pool_bwd(dxn, W, saved, tag, after=None, emit=None):
    x, h, proj, p, pg, y = saved
    emit = emit or (lambda grads: None)
    dy = mm(dxn, W["w_out"], tb=True, after=after, out_dtype=BF16, name=tag + "_dy")
    g = {}
    (dpg, dproj), (g["scale"],) = rowwise_bwd(
        f_pool_gate, [(pg, POOL_GROUP, 0, 1), (proj, POOL_GROUP, 4, 1)], [(W["scale"], POOL_GROUP, 0, 1)],
        [(dy, POOL_GROUP, 0, 1)], need=[True, True], place={1: (2 * POOL_WIDTH, 4)}, narrow=(0, 1), ncol=4, ts=1024, name=tag + "_dgate")
    dp = gmm("nt", dpg, W["w_grp"], G=4, out_dtype=BF16, name=tag + "_dp")
    dproj = pool_time_bwd(dp, dproj, tag + "_dwin")
    g["w_in"] = mm(h, dproj, ta=True, out_dtype=BF16, name=tag + "_dw_in")
    t1 = emit({"w_in": g["w_in"]})
    g["w_out"] = mm(y, dxn, ta=True, after=t1, out_dtype=BF16, name=tag + "_dwout")
    g["w_grp"] = gmm("tn", p, dpg, G=4, out_dtype=BF16, name=tag + "_dwgrp")
    t2 = emit({"w_out": g["w_out"], "w_grp": g["w_grp"]})
    dh = mm(dproj, W["w_in"], tb=True, after=t2, name=tag + "_dh")
    dx, g["ng"] = _norm_bwd(x, W["ng"], dh, dxn, tag + "_dnorm")
    return dx, g


def gdn_fwd(x, W, tag, late=None):
    h = _norm_fwd(x, W["ng"], tag + "_norm")
    proj = mm(h, W["w_in"], name=tag + "_in")
    qkv = gdn_conv_fwd(proj, W["conv"], tag + "_conv")
    g_b, beta_b = rowwise(f_gdn_gates, [(proj, LANES, 6144 // LANES, 0)], [(W["a_log"], LANES, 0, 0), (W["dt_bias"], LANES, 0, 0)],
                          [(GDN_QK, 0, F32), (GDN_QK, 0, F32)], ts=1024, name=tag + "_gates")
    o, states = gdn_chunk_fwd(qkv, g_b, beta_b, tag + "_chunk")
    og = rowwise(f_gdn_out, [(o, GDN_DV, 0, 1), (proj, GDN_DV, 4096 // GDN_DV, 1)], [(W["norm_g"], GDN_DV, 0, 0)],
                 [(GDN_DV, 1, BF16)], ncol=GDN_H, ts=2048, name=tag + "_onorm")[0]
    if late is not None:
        W = dict(W, **late(og))
    xn = mm(og, W["w_out"], add=x, name=tag + "_out")
    return xn, (x, h, proj, qkv, g_b, beta_b, o, states, og)


def gdn_bwd(dxn, W, saved, tag, after=None):
    x, h, proj, qkv, g_b, beta_b, o, states, og = saved
    dog = mm(dxn, W["w_out"], tb=True, after=after, name=tag + "_dog")
    g = {"w_out": mm(og, dxn, ta=True, out_dtype=BF16, name=tag + "_dwout")}
    (do, dproj), (g["norm_g"],) = rowwise_bwd(
        f_gdn_out, [(o, GDN_DV, 0, 1), (proj, GDN_DV, 4096 // GDN_DV, 1)], [(W["norm_g"], GDN_DV, 0, 0)],
        [(dog, GDN_DV, 0, 1)], need=[True, True], place={1: (GDN_IN_PAD, 4096 // GDN_DV)}, narrow=(1,), ncol=GDN_H, ts=2048, name=tag + "_donorm")
    dq, dk, dv, dg_b, dbeta_b = gdn_chunk_bwd(qkv, g_b, beta_b, states, do, tag + "_dchunk")
    (dproj,), (g["a_log"], g["dt_bias"]) = rowwise_bwd(
        f_gdn_gates, [(proj, LANES, 6144 // LANES, 0)], [(W["a_log"], LANES, 0, 0), (W["dt_bias"], LANES, 0, 0)],
        [(dg_b, GDN_QK, 0, 0), (dbeta_b, GDN_QK, 0, 0)], need=[True], place={0: (dproj, 6144 // LANES)}, ts=1024, name=tag + "_dgates")
    dproj, g["conv"] = gdn_conv_bwd(proj, W["conv"], dq, dk, dv, dproj, tag + "_dconv")
    dh = mm(dproj, W["w_in"], tb=True, name=tag + "_dh")
    g["w_in"] = mm(h, dproj, ta=True, out_dtype=BF16, name=tag + "_dw_in")
    dx, g["ng"] = _norm_bwd(x, W["ng"], dh, dxn, tag + "_dnorm")
    return dx, g


def mla_fwd(x, pos, W, tag):
    h = _norm_fwd(x, W["ng"], tag + "_norm")
    proj = mm(h, W["w_in"], out_dtype=BF16, name=tag + "_in")
    hq = rowwise(f_rms, [(proj, MLA_Q_LORA, 0, 0)], [(W["q_g"], MLA_Q_LORA, 0, 0)], [(MLA_Q_LORA, 0, BF16)], ts=1024, name=tag + "_qnorm")[0]
    hkv = rowwise(f_rms, [(proj, MLA_KV_LORA, 2, 0)], [(W["kv_g"], MLA_KV_LORA, 0, 0)], [(MLA_KV_LORA, 0, BF16)], ts=1024, name=tag + "_kvnorm")[0]
    qpad = mm(hq, W["w_uq"], out_dtype=BF16, name=tag + "_uq")
    kv = mm(hkv, W["w_ukv"], out_dtype=BF16, name=tag + "_ukv")
    qh, kh, vh = mla_prep_fwd(qpad, kv, proj, pos, W["rope"], tag + "_prep")
    o, lse = flash_fwd(qh, kh, vh, tag + "_attn")
    og = rowwise(f_ogate, [(o, 512, 0, 1), (proj, 512, 4, 1)], [], [(512, 1, BF16)], ncol=4, ts=1024, name=tag + "_ogate")[0]
    xn = mm(og, W["w_out"], add=x, name=tag + "_out")
    return xn, (x, h, proj, hq, hkv, qh, kh, vh, o, lse, og)


def mla_bwd(dxn, pos, W, saved, tag, after=None):
    x, h, proj, hq, hkv, qh, kh, vh, o, lse, og = saved
    dog = mm(dxn, W["w_out"], tb=True, after=after, out_dtype=BF16, name=tag + "_dog")
    g = {"w_out": mm(og, dxn, ta=True, out_dtype=BF16, name=tag + "_dwout")}
    dproj = jnp.zeros(proj.shape, BF16)
    (do, dproj), _ = rowwise_bwd(f_ogate, [(o, 512, 0, 1), (proj, 512, 4, 1)], [], [(dog, 512, 0, 1)], need=[True, True],
                                 place={1: (dproj, 4)}, narrow=(0,), ncol=4, ts=1024, name=tag + "_dogate")
    dqh, dkh, dvh = flash_bwd(qh, kh, vh, o, lse, do, tag + "_dattn")
    dqpad, dkv, dproj = mla_prep_bwd(dqh, dkh, dvh, pos, W["rope"], dproj, tag + "_dprep")
    dhq = mm(dqpad, W["w_uq"], tb=True, name=tag + "_dhq")
    g["w_uq"] = mm(hq, dqpad, ta=True, out_dtype=BF16, name=tag + "_dwuq")
    dhkv = mm(dkv, W["w_ukv"], tb=True, name=tag + "_dhkv")
    g["w_ukv"] = mm(hkv, dkv, ta=True, out_dtype=BF16, name=tag + "_dwukv")
    (dproj,), (g["q_g"],) = rowwise_bwd(f_rms, [(proj, MLA_Q_LORA, 0, 0)], [(W["q_g"], MLA_Q_LORA, 0, 0)], [(dhq, MLA_Q_LORA, 0, 0)],
                                        need=[True], place={0: (dproj, 0)}, ts=512, name=tag + "_dqnorm")
    (dproj,), (g["kv_g"],) = rowwise_bwd(f_rms, [(proj, MLA_KV_LORA, 2, 0)], [(W["kv_g"], MLA_KV_LORA, 0, 0)], [(dhkv, MLA_KV_LORA, 0, 0)],
                                         need=[True], place={0: (dproj, 2)}, ts=512, name=tag + "_dkvnorm")
    dh = mm(dproj, W["w_in"], tb=True, name=tag + "_dh")
    g["w_in"] = mm(h, dproj, ta=True, out_dtype=BF16, name=tag + "_dw_in")
    dx, g["ng"] = _norm_bwd(x, W["ng"], dh, dxn, tag + "_dnorm")
    return dx, g


def _pad_cols(a, n):
    return jnp.pad(a, ((0, 0), (0, n - a.shape[1])))


def _mla_w_in_layout(w):
    z = lambda n: jnp.zeros((w.shape[0], n), w.dtype)
    kr = w[:, 1280:1344]
    return jnp.concatenate([w[:, :768], z(256), w[:, 768:1280], kr[:, :32], z(32), kr[:, 32:], z(32), z(384), w[:, 1344:]], axis=1)


def _mla_w_in_unlayout(g):
    return jnp.concatenate([g[:, :768], g[:, 1024:1536], g[:, 1536:1568], g[:, 1600:1632], g[:, 2048:]], axis=1)


def _mla_w_uq_layout(w):
    w3 = w.reshape(w.shape[0], MLA_H, MLA_NOPE + MLA_ROPE)
    z = jnp.zeros((w.shape[0], MLA_H, 32), w.dtype)
    return jnp.concatenate([w3[..., :128], w3[..., 128:160], z, w3[..., 160:192], z], axis=-1).reshape(w.shape[0], MLA_H * 256)


def _mla_w_uq_unlayout(g):
    g3 = g.reshape(g.shape[0], MLA_H, 256)
    return jnp.concatenate([g3[..., :128], g3[..., 128:160], g3[..., 192:224]], axis=-1).reshape(g.shape[0], MLA_H * 192)


def _rope_consts():
    half = MLA_ROPE // 2
    inv = ROPE_THETA ** (-jnp.arange(half, dtype=F32) / half)
    z = jnp.zeros((half,), F32)
    o = jnp.ones((half,), F32)
    row = lambda *p: jnp.concatenate(p).reshape(1, LANES)
    return row(inv, z, inv, z), row(o, z, o, z), row(-o, z, o, z)


BIG = ["pool_w_in", "pool_w_grp", "pool_w_out", "gdn_w_in", "gdn_w_out", "mla_w_in", "mla_w_uq", "mla_w_ukv", "mla_w_out"]
BIG_LAYOUT = {"pool_w_in": (1, 1024, (1024, 4096)), "pool_w_grp": (1, 128, (4, 512, 512)), "pool_w_out": (0, 512, (2048, 1024)),
              "gdn_w_in": (None, None, (4, 1024, 1540)), "gdn_w_out": (0, 512, (2048, 1024)),
              "mla_w_in": (None, None, (4, 1024, 848)), "mla_w_uq": (1, 768, (768, 3072)), "mla_w_ukv": (1, 1024, (512, 4096)),
              "mla_w_out": (0, 512, (2048, 1024))}
SMALL_SHARDED = ["pool_scale", "gdn_conv", "mla_q_norm_g", "mla_kv_norm_g"]
SMALL_AXIS = {"pool_scale": 1, "gdn_conv": 2, "mla_q_norm_g": 1, "mla_kv_norm_g": 1}
REPLICATED = ["norm_g", "gdn_a_log", "gdn_dt_bias", "gdn_norm_g", "final_g"]
PACK_C = 1024


def _pack(parts, dtype, row_mult):
    flat = jnp.concatenate([p.reshape(-1).astype(dtype) for p in parts])
    rows = -(-flat.shape[0] // PACK_C)
    rows = -(-rows // row_mult) * row_mult
    return jnp.pad(flat, (0, rows * PACK_C - flat.shape[0])).reshape(rows, PACK_C)


def _unpack(buf, shapes):
    lead = buf.shape[:-2]
    flat = buf.reshape(lead + (-1,))
    out, off = [], 0
    for s in shapes:
        n = int(np.prod(s))
        out.append(flat[..., off:off + n].reshape(lead + tuple(s)))
        off += n
    return out


def _unshard(g4, axis):
    a = jnp.moveaxis(g4, 0, axis)
    s = a.shape
    return a.reshape(s[:axis] + (s[axis] * s[axis + 1],) + s[axis + 2:])


def _to_shards(a, axis):
    s = a.shape
    return jnp.moveaxis(a.reshape(s[:axis] + (4, s[axis] // 4) + s[axis + 1:]), axis, 0)


def layer_weights(full, small, rep, layer):
    ng = rep["norm_g"][layer:layer + 1]
    side_by_side = lambda a4: jnp.moveaxis(a4, 0, 1).reshape(a4.shape[1], 4 * a4.shape[2])
    if layer in (0, 3):
        j = layer // 3
        return dict(ng=ng, w_in=full[("pool_w_in", j)], w_grp=full[("pool_w_grp", j)], scale=small["pool_scale"][j:j + 1],
                    w_out=full[("pool_w_out", j)])
    if layer == 1:
        return dict(ng=ng, w_in=_pad_cols(side_by_side(full[("gdn_w_in", 0)]), GDN_IN_PAD),
                    conv=jnp.pad(small["gdn_conv"][0], ((0, 4), (0, 0))), a_log=_pad_cols(rep["gdn_a_log"], LANES),
                    dt_bias=_pad_cols(rep["gdn_dt_bias"], LANES), norm_g=rep["gdn_norm_g"], w_out=full.get(("gdn_w_out", 0)))
    return dict(ng=ng, w_in=_mla_w_in_layout(side_by_side(full[("mla_w_in", 0)])), q_g=small["mla_q_norm_g"],
                kv_g=small["mla_kv_norm_g"], w_uq=_mla_w_uq_layout(full[("mla_w_uq", 0)]), w_ukv=full[("mla_w_ukv", 0)],
                w_out=full[("mla_w_out", 0)], rope=_rope_consts())


def big_grad_pieces(gl):
    g0, g1, g2, g3 = gl
    slots = lambda a: jnp.moveaxis(a.reshape(a.shape[0], 4, a.shape[1] // 4), 1, 0)
    out = {}
    for l, g in ((0, g0), (1, g3)):
        if g is not None:
            out.update({("pool_w_in", l): g["w_in"], ("pool_w_grp", l): g["w_grp"], ("pool_w_out", l): g["w_out"]})
    if g1 is not None:
        out.update({("gdn_w_in", 0): slots(g1["w_in"][:, :GDN_IN]), ("gdn_w_out", 0): g1["w_out"]})
    if g2 is not None:
        out.update({("mla_w_in", 0): slots(_mla_w_in_unlayout(g2["w_in"])), ("mla_w_uq", 0): _mla_w_uq_unlayout(g2["w_uq"]),
                    ("mla_w_ukv", 0): g2["w_ukv"], ("mla_w_out", 0): g2["w_out"]})
    return out


def small_grads(gl, dfinal):
    g0, g1, g2, g3 = gl
    return {"norm_g": jnp.concatenate([g0["ng"], g1["ng"], g2["ng"], g3["ng"]], axis=0),
            "pool_scale": jnp.concatenate([g0["scale"], g3["scale"]], axis=0), "gdn_conv": g1["conv"][None, :4],
            "gdn_a_log": g1["a_log"][:, :GDN_H], "gdn_dt_bias": g1["dt_bias"][:, :GDN_H], "gdn_norm_g": g1["norm_g"],
            "mla_q_norm_g": g2["q_g"], "mla_kv_norm_g": g2["kv_g"], "final_g": dfinal.reshape(D)}


NAMES = ["norm_g", "pool_w_in", "pool_w_grp", "pool_scale", "pool_w_out", "gdn_w_in", "gdn_conv", "gdn_a_log", "gdn_dt_bias",
         "gdn_norm_g", "gdn_w_out", "mla_w_in", "mla_q_norm_g", "mla_w_uq", "mla_kv_norm_g", "mla_w_ukv", "mla_w_out", "final_g"]


def kernel(x, positions, norm_g, pool_w_in, pool_w_grp, pool_scale, pool_w_out, gdn_w_in, gdn_conv, gdn_a_log, gdn_dt_bias, gdn_norm_g, gdn_w_out, mla_w_in, mla_q_norm_g, mla_w_uq, mla_kv_norm_g, mla_w_ukv, mla_w_out, final_g, loss_target, m_norm_g, m_pool_w_in, m_pool_w_grp, m_pool_scale, m_pool_w_out, m_gdn_w_in, m_gdn_conv, m_gdn_a_log, m_gdn_dt_bias, m_gdn_norm_g, m_gdn_w_out, m_mla_w_in, m_mla_q_norm_g, m_mla_w_uq, m_mla_kv_norm_g, m_mla_w_ukv, m_mla_w_out, m_final_g, v_norm_g, v_pool_w_in, v_pool_w_grp, v_pool_scale, v_pool_w_out, v_gdn_w_in, v_gdn_conv, v_gdn_a_log, v_gdn_dt_bias, v_gdn_norm_g, v_gdn_w_out, v_mla_w_in, v_mla_q_norm_g, v_mla_w_uq, v_mla_kv_norm_g, v_mla_w_ukv, v_mla_w_out, v_final_g):
    args = locals()
    w = {n: args[n] for n in NAMES}
    m = {n: args["m_" + n] for n in NAMES}
    v = {n: args["v_" + n] for n in NAMES}
    my_chip = (2 * lax.axis_index("x") + lax.axis_index("y")).astype(I32)
    S_ = x.shape[1]
    x0, pos, target = x[0], positions.reshape(S_, 1).astype(F32), loss_target[0]
    rep = {n: w[n] for n in REPLICATED}

    shard = {(n, l): w[n][l:l + 1].astype(BF16) for n in BIG for l in range(w[n].shape[0])}
    small_shapes = [w[n].shape for n in SMALL_SHARDED]
    shard[("small", 0)] = _pack([w[n] for n in SMALL_SHARDED], F32, 8)[None]
    layout = dict(BIG_LAYOUT, small=(None, None, (4,) + shard[("small", 0)].shape[1:]))

    def gather_start(group, after, tag):
        pieces = [gather_piece(i, 0, i, layout[n][0], layout[n][1]) for i, (n, l) in enumerate(group)]
        shapes = [_sds(layout[n][2], shard[(n, l)].dtype) for n, l in group]
        sems, ins, lands, token = exchange_start(pieces, [shard[k] for k in group], shapes, after, tag + "_start")
        return (pieces, sems, ins, lands), token

    def finish(handle, after, tag):
        return exchange_wait(*handle, after, tag + "_wait")

    def gathered(group, handle, after, tag):
        srcs, lands = finish(handle, after, tag)
        return {(n, l): place_own(a, s[0], layout[n][0], layout[n][1], my_chip) for (n, l), s, a in zip(group, srcs, lands)}

    group_a = [("small", 0), ("pool_w_in", 0)]
    group_a2 = [("pool_w_grp", 0), ("pool_w_out", 0)]
    group_b = [("gdn_w_in", 0)]
    group_c = [("gdn_w_out", 0), ("mla_w_in", 0), ("mla_w_uq", 0), ("mla_w_ukv", 0), ("mla_w_out", 0), ("pool_w_in", 1),
               ("pool_w_grp", 1), ("pool_w_out", 1)]
    full = {}
    h_a, t_a = gather_start(group_a, x0, "gather_a")
    h_a2, t_a2 = gather_start(group_a2, t_a, "gather_a2")
    h_b, t_b = gather_start(group_b, t_a2, "gather_b")
    h_c, t_c = gather_start(group_c, t_b, "gather_c")
    full.update(gathered(group_a, h_a, t_c, "gather_a"))
    small = {n: _unshard(a, SMALL_AXIS[n]) for n, a in zip(SMALL_SHARDED, _unpack(full[("small", 0)], small_shapes))}

    def late_l0(proj):
        full.update(gathered(group_a2, h_a2, proj, "gather_a2"))
        return dict(w_grp=full[("pool_w_grp", 0)], w_out=full[("pool_w_out", 0)])

    x1, s0 = pool_fwd(x0, dict(ng=rep["norm_g"][0:1], w_in=full[("pool_w_in", 0)], scale=small["pool_scale"][0:1]), "l0", late=late_l0)
    W0 = layer_weights(full, small, rep, 0)
    full.update(gathered(group_b, h_b, x1, "gather_b"))

    def late_l1(og):
        full.update(gathered(group_c, h_c, og, "gather_c"))
        return dict(w_out=full[("gdn_w_out", 0)])

    x2, s1 = gdn_fwd(x1, layer_weights(full, small, rep, 1), "l1", late=late_l1)
    W1, W2, W3 = (layer_weights(full, small, rep, i) for i in (1, 2, 3))
    x3, s2 = mla_fwd(x2, pos, W2, "l2")
    x4, s3 = pool_fwd(x3, W3, "l3")
    loss_part, dx4, dfinal = loss_head(x4, target, final_g.reshape(1, D), "loss_head")

    def scatter_start(pieces_of, after, tag):
        keys = list(pieces_of)
        pieces = [scatter_piece(i, i, BIG_LAYOUT[n][0], BIG_LAYOUT[n][1]) for i, (n, l) in enumerate(keys)]
        shapes = [_sds((4,) + tuple(w[n].shape[1:]), BF16) for n, l in keys]
        sems, ins, lands, token = exchange_start(pieces, [pieces_of[k] for k in keys], shapes, after, tag + "_start")
        return keys, (pieces, sems, ins, lands), token

    def scattered(keys, handle, after, tag):
        srcs, lands = finish(handle, after, tag)
        return {(n, l): place_own(a, own_window(g, BIG_LAYOUT[n][0], BIG_LAYOUT[n][1], my_chip), None, None, my_chip)
                for (n, l), g, a in zip(keys, srcs, lands)}

    dx3, g3 = pool_bwd(dx4, W3, s3, "l3")
    k3, h3, t3 = scatter_start(big_grad_pieces((None, None, None, g3)), dx3, "scatter_l3")
    dx2, g2 = mla_bwd(dx3, pos, W2, s2, "l2", after=t3)
    k2, h2, t2 = scatter_start(big_grad_pieces((None, None, g2, None)), dx2, "scatter_l2")
    dx1, g1 = gdn_bwd(dx2, W1, s1, "l1", after=t2)
    k1, h1, t1 = scatter_start(big_grad_pieces((None, g1, None, None)), dx1, "scatter_l1")
    def swap_start(part, tag):
        keys = list(part)
        ins = [part[k] for k in keys]
        pieces = [whole_piece(i) for i in range(len(keys))]
        sems, ins, lands, token = exchange_start(pieces, ins, [_sds(a.shape, a.dtype) for a in ins], ins[0], tag + "_start", sibling=True)
        swaps.append((keys, (pieces, sems, ins, lands), tag))
        return token

    last, swaps = [], []

    def emit_l0(grads):
        first = not last
        now = next(iter(grads.values()))
        early = [(k3, h3, "scatter_l3"), (k2, h2, "scatter_l2")] if first else [(k1, h1, "scatter_l1")]
        landed = {}
        for keys, handle, tag in early:
            landed.update(scattered(keys, handle, now, tag))
        swapping = swap_start(landed, "swap_a" if first else "swap_b")
        tag = "scatter_l0a" if first else "scatter_l0b"
        keys, handle, token = scatter_start({("pool_" + k, 0): a for k, a in grads.items()}, swapping, tag)
        last.append((keys, handle, tag))
        return token

    dx0, g0 = pool_bwd(dx1, W0, s0, "l0", after=t1, emit=emit_l0)
    landed = {}
    for keys, handle, tag in last:
        landed.update(scattered(keys, handle, dx0, tag))
    swap_start(landed, "swap_c")
    recv, sib = {}, {}
    for keys, handle, tag in swaps:
        mine_, theirs = exchange_wait(*handle, dx0, tag + "_wait", sibling=True)
        recv.update(zip(keys, mine_))
        sib.update(zip(keys, theirs))

    sg = small_grads((g0, g1, g2, g3), dfinal)
    small_names = SMALL_SHARDED + REPLICATED
    small_buf = _pack([sg[n] for n in small_names] + [loss_part], F32, 8)
    small_sum = sum_slots(exchange_all(small_buf, "gather_small"), "sum_small")
    full_small = _unpack(small_sum, [sg[n].shape for n in small_names] + [(1, LANES)])
    loss = full_small[-1][0, 0]
    small_part = {}
    for n, a in zip(small_names, full_small[:-1]):
        if n in SMALL_AXIS:
            a = lax.dynamic_index_in_dim(_to_shards(a, SMALL_AXIS[n]), my_chip, axis=0, keepdims=False)
        small_part[n] = a

    outs = []
    for n in NAMES:
        shp = w[n].shape
        two = (int(np.prod(shp[:-1])), shp[-1]) if len(shp) > 1 else (1, shp[0])
        if n in BIG_LAYOUT:
            layers = shp[0]
            rows = lambda a: a.reshape(4, two[0] // layers, two[1])
            parts = [[rows(recv[(n, l)]) for l in range(layers)], [rows(sib[(n, l)]) for l in range(layers)]]
        else:
            parts = [[small_part[n].reshape((1,) + two)]]
        res = adamw(w[n].reshape(two), parts, m[n].reshape(two), v[n].reshape(two), "adamw_" + n)
        outs.append([r.reshape(shp) for r in res])
    return (loss, dx0[None], *[o[0] for o in outs], *[o[1] for o in outs], *[o[2] for o in outs], *[o[3] for o in outs])
```

```python
import math

import jax
import jax.numpy as jnp
import numpy as np
from jax import lax
from jax.experimental import pallas as pl
from jax.experimental.pallas import tpu as pltpu

F32 = jnp.float32
BF16 = jnp.bfloat16
I32 = jnp.int32

D = 1024
EPS = 1e-6
POOL_WIDTH = 2048
POOL_GROUP = 512
GDN_H, GDN_DK, GDN_DV, GDN_C = 8, 128, 256, 64
GDN_QK, GDN_V, GDN_CONV_CH, GDN_IN = 1024, 2048, 4096, 6160
GDN_IN_PAD = 6272
MLA_H, MLA_NOPE, MLA_ROPE, MLA_V = 16, 128, 64, 128
MLA_Q_LORA, MLA_KV_LORA, MLA_WIDTH, MLA_IN = 768, 512, 2048, 3392
MLA_IN_PAD = 4096
MLA_SCALE = (MLA_NOPE + MLA_ROPE) ** -0.5
ROPE_THETA = 10000.0
ADAM_LR, ADAM_B1, ADAM_B2, ADAM_EPS, ADAM_WD, ADAM_STEP = 0.001, 0.9, 0.999, 1e-08, 0.01, 10

VMEM_LIMIT_V7X = 56 * 1024 * 1024
LANES = 128
MESH = pl.DeviceIdType.MESH


def _pc(body, **kw):
    return pl.pallas_call(body, **kw)


def _cparams(sem):
    return pltpu.CompilerParams(dimension_semantics=sem, vmem_limit_bytes=VMEM_LIMIT_V7X)


def _tile(n, cap):
    t = (cap // LANES) * LANES
    while t >= LANES:
        if n % t == 0:
            return t
        t -= LANES
    return n


def _sds(shape, dt):
    return jax.ShapeDtypeStruct(shape, dt)


def mm(a, b, *, ta=False, tb=False, add=None, after=None, out_dtype=F32, name):
    if ta:
        K, M = a.shape
    else:
        M, K = a.shape
    if tb:
        N, K2 = b.shape
    else:
        K2, N = b.shape
    assert K == K2, (a.shape, b.shape, ta, tb)
    tm, tn, tk = _tile(M, 1024), _tile(N, 1024), _tile(K, 1024)
    nk = K // tk
    a_spec = pl.BlockSpec((tk, tm), lambda i, j, k: (k, i)) if ta else pl.BlockSpec((tm, tk), lambda i, j, k: (i, k))
    b_spec = pl.BlockSpec((tn, tk), lambda i, j, k: (j, k)) if tb else pl.BlockSpec((tk, tn), lambda i, j, k: (k, j))
    o_spec = pl.BlockSpec((tm, tn), lambda i, j, k: (i, j))
    dn = (((0 if ta else 1,), (1 if tb else 0,)), ((), ()))
    has_add = add is not None

    def body(*refs):
        a_ref, b_ref = refs[0], refs[1]
        part = lax.dot_general(a_ref[...].astype(BF16), b_ref[...].astype(BF16), dn, preferred_element_type=F32)
        if nk == 1:
            refs[-1][...] = (part + refs[2][...] if has_add else part).astype(out_dtype)
            return
        o_ref, acc = refs[-2], refs[-1]
        k = pl.program_id(2)

        @pl.when(k == 0)
        def _():
            acc[...] = part

        @pl.when(k > 0)
        def _():
            acc[...] += part

        @pl.when(k == nk - 1)
        def _():
            r = acc[...]
            if has_add:
                r = r + refs[2][...]
            o_ref[...] = r.astype(out_dtype)

    ins = [a, b] + ([add] if has_add else []) + ([after] if after is not None else [])
    specs = [a_spec, b_spec] + ([o_spec] if has_add else []) + ([pl.BlockSpec(memory_space=pl.ANY)] if after is not None else [])
    return _pc(body, grid=(M // tm, N // tn, nk), in_specs=specs, out_specs=o_spec, out_shape=_sds((M, N), out_dtype),
               scratch_shapes=[pltpu.VMEM((tm, tn), F32)] if nk > 1 else [], compiler_params=_cparams(("parallel", "parallel", "arbitrary")),
               name=name)(*ins)


def gmm(kind, a, b, *, G, name, out_dtype=F32):
    S_ = a.shape[0]
    Ka = a.shape[1] // G
    if kind == "tn":
        N = b.shape[1] // G
        tk = _tile(S_, 2048)
        nk = S_ // tk

        def body(a_ref, b_ref, o_ref, acc):
            k = pl.program_id(1)

            @pl.when(k == 0)
            def _():
                acc[...] = jnp.zeros_like(acc)

            acc[...] += lax.dot_general(a_ref[...].astype(BF16), b_ref[...].astype(BF16), (((0,), (0,)), ((), ())),
                                        preferred_element_type=F32)

            @pl.when(k == nk - 1)
            def _():
                o_ref[...] = acc[...].astype(out_dtype)

        return _pc(body, grid=(G, nk),
                   in_specs=[pl.BlockSpec((tk, Ka), lambda g, k: (k, g)), pl.BlockSpec((tk, N), lambda g, k: (k, g))],
                   out_specs=pl.BlockSpec((None, Ka, N), lambda g, k: (g, 0, 0)), out_shape=_sds((G, Ka, N), out_dtype),
                   scratch_shapes=[pltpu.VMEM((Ka, N), F32)], compiler_params=_cparams(("parallel", "arbitrary")), name=name)(a, b)
    N = b.shape[2] if kind == "nn" else b.shape[1]
    tm = _tile(S_, 4096)
    dn = (((1,), (0 if kind == "nn" else 1,)), ((), ()))

    def body(a_ref, b_ref, o_ref):
        o_ref[...] = lax.dot_general(a_ref[...].astype(BF16), b_ref[...].astype(BF16), dn, preferred_element_type=F32).astype(out_dtype)

    bshape = (None,) + tuple(b.shape[1:])
    return _pc(body, grid=(G, S_ // tm),
               in_specs=[pl.BlockSpec((tm, Ka), lambda g, i: (i, g)), pl.BlockSpec(bshape, lambda g, i: (g, 0, 0))],
               out_specs=pl.BlockSpec((tm, N), lambda g, i: (i, g)), out_shape=_sds((S_, G * N), out_dtype),
               compiler_params=_cparams(("parallel", "parallel")), name=name)(a, b)


def _rw_spec(ts, w, c, s):
    return pl.BlockSpec((ts, w), lambda j, i: (i, c + j * s))


def _rw_pspec(p, w, c, s):
    return pl.BlockSpec((p.shape[0], w), lambda j, i: (0, c + j * s))


def rowwise(f, tiles, params, outs, *, ncol=1, ts, name):
    S_ = tiles[0][0].shape[0]
    nin = len(tiles) + len(params)

    def body(*refs):
        res = f(pl.program_id(0), *[r[...].astype(F32) for r in refs[:nin]])
        for r, o in zip(refs[nin:], res):
            r[...] = o.astype(r.dtype)

    return _pc(body, grid=(ncol, S_ // ts),
               in_specs=[_rw_spec(ts, w, c, s) for (_, w, c, s) in tiles] + [_rw_pspec(*p) for p in params],
               out_specs=[_rw_spec(ts, w, 0, s) for (w, s, _) in outs],
               out_shape=[_sds((S_, w * (ncol if s else 1)), dt) for (w, s, dt) in outs],
               compiler_params=_cparams(("parallel", "parallel")), name=name)(*[t[0] for t in tiles], *[p[0] for p in params])


def rowwise_bwd(f, tiles, params, cots, *, need, adds=None, place=None, narrow=(), ncol=1, ts, name):
    S_ = tiles[0][0].shape[0]
    adds = adds or {}
    place = place or {}
    nt, npar, nc = len(tiles), len(params), len(cots)
    add_keys = sorted(adds)
    need_idx = [k for k in range(nt) if need[k]]
    into_keys = [k for k in need_idx if k in place and not isinstance(place[k][0], int)]
    n_extra = len(add_keys) + len(into_keys)

    def body(*refs):
        j, i = pl.program_id(0), pl.program_id(1)
        vals = [r[...].astype(F32) for r in refs[:nt + npar]]
        cvals = tuple(r[...].astype(F32) for r in refs[nt + npar:nt + npar + nc])
        add_refs = refs[nt + npar + nc:nt + npar + nc + len(add_keys)]
        out_refs = refs[nt + npar + nc + n_extra:]
        _, vjp = jax.vjp(lambda *v: tuple(f(j, *v)), *vals)
        grads = vjp(cvals)
        for n, k in enumerate(need_idx):
            g = grads[k]
            if k in adds:
                g = g + add_refs[add_keys.index(k)][...]
            out_refs[n][...] = g.astype(out_refs[n].dtype)
        for n in range(npar):
            ref = out_refs[len(need_idx) + n]
            first = (i == 0) if params[n][3] else jnp.logical_and(i == 0, j == 0)

            @pl.when(first)
            def _():
                ref[...] = jnp.zeros_like(ref)

            ref[...] += grads[nt + n]

    in_specs = ([_rw_spec(ts, w, c, s) for (_, w, c, s) in tiles] + [_rw_pspec(*p) for p in params]
                + [_rw_spec(ts, w, c, s) for (_, w, c, s) in cots] + [_rw_spec(ts, *adds[k][1:]) for k in add_keys]
                + [pl.BlockSpec(memory_space=pl.ANY) for _ in into_keys])
    out_specs, out_shape, aliases = [], [], {}
    for n, k in enumerate(need_idx):
        w, s = tiles[k][1], tiles[k][3]
        if k in place:
            dst, c0 = place[k]
            total = dst if isinstance(dst, int) else dst.shape[1]
            out_specs.append(_rw_spec(ts, w, c0, s))
            out_shape.append(_sds((S_, total), (BF16 if k in narrow else F32) if isinstance(dst, int) else dst.dtype))
            if k in into_keys:
                aliases[nt + npar + nc + len(add_keys) + into_keys.index(k)] = n
        else:
            out_specs.append(_rw_spec(ts, w, 0, s))
            out_shape.append(_sds((S_, w * (ncol if s else 1)), BF16 if k in narrow else F32))
    out_specs += [_rw_pspec(p[0], p[1], p[2], p[3]) for p in params]
    out_shape += [_sds(p[0].shape, F32) for p in params]
    res = _pc(body, grid=(ncol, S_ // ts), in_specs=in_specs, out_specs=out_specs, out_shape=out_shape,
              input_output_aliases=aliases, compiler_params=_cparams(("arbitrary", "arbitrary")), name=name)(
        *[t[0] for t in tiles], *[p[0] for p in params], *[c[0] for c in cots], *[adds[k][0] for k in add_keys],
        *[place[k][0] for k in into_keys])
    return list(res[:len(need_idx)]), list(res[len(need_idx):])


def _rms(x, g):
    r = lax.rsqrt(jnp.mean(x * x, axis=-1, keepdims=True) + EPS)
    return x * r * g


def _silu(x):
    return x * jax.nn.sigmoid(x)


@jax.custom_vjp
def _softplus(x):
    return jnp.maximum(x, 0.0) + jnp.log1p(jnp.exp(-jnp.abs(x)))


_softplus.defvjp(lambda x: (_softplus(x), x), lambda x, d: (d * jax.nn.sigmoid(x),))


def f_rms(j, x, g):
    return (_rms(x, g),)


def f_pool_gate(j, pg, gate, scale):
    return (pg * scale * _silu(gate),)


def f_ogate(j, o, gate):
    return (o * _silu(gate),)


def f_gdn_out(j, o, gate, g):
    return (_rms(o, g) * _silu(gate),)


def f_gdn_gates(j, ba, alog, dtb):
    lane = lax.broadcasted_iota(I32, (1, LANES), 1)
    gs, bs = [], []
    for h in range(GDN_H):
        eb = (lane == h).astype(F32)
        ea = (lane == GDN_H + h).astype(F32)
        b = jnp.sum(ba * eb, -1, keepdims=True)
        a = jnp.sum(ba * ea, -1, keepdims=True)
        al = jnp.sum(alog * eb, -1, keepdims=True)
        dt = jnp.sum(dtb * eb, -1, keepdims=True)
        g = -jnp.exp(al) * _softplus(a + dt)
        gs.append(jnp.broadcast_to(g, ba.shape))
        bs.append(jnp.broadcast_to(jax.nn.sigmoid(b), ba.shape))
    return jnp.concatenate(gs, 1), jnp.concatenate(bs, 1)


def _shift_dn(x, k):
    rows = lax.broadcasted_iota(I32, x.shape, 0)
    return jnp.where(rows < k, 0.0, pltpu.roll(x, k, 0))


def _shift_up(x, k):
    n = x.shape[0]
    rows = lax.broadcasted_iota(I32, x.shape, 0)
    return jnp.where(rows >= n - k, 0.0, pltpu.roll(x, n - k, 0))


def _pool_window(j):
    g = lax.div(j, POOL_GROUP // LANES)
    return jnp.where(g == 0, 2.0, jnp.where(g == 1, 4.0, jnp.where(g == 2, 8.0, 16.0))), g


def _pick(g, a2, a4, a8, a16):
    return jnp.where(g == 0, a2, jnp.where(g == 1, a4, jnp.where(g == 2, a8, a16)))


def pool_time_fwd(proj, name):
    S_ = proj.shape[0]

    def body(u_ref, p_ref):
        u = u_ref[...].astype(F32)
        w, g = _pool_window(pl.program_id(0))
        s2 = u + _shift_dn(u, 1)
        s4 = s2 + _shift_dn(s2, 2)
        s8 = s4 + _shift_dn(s4, 4)
        s16 = s8 + _shift_dn(s8, 8)
        t1 = (lax.broadcasted_iota(I32, u.shape, 0) + 1).astype(F32)
        p_ref[...] = (_pick(g, s2, s4, s8, s16) / jnp.minimum(t1, w) - u).astype(p_ref.dtype)

    return _pc(body, grid=(POOL_WIDTH // LANES,), in_specs=[pl.BlockSpec((S_, LANES), lambda j: (0, j))],
               out_specs=pl.BlockSpec((S_, LANES), lambda j: (0, j)), out_shape=_sds((S_, POOL_WIDTH), BF16),
               compiler_params=_cparams(("parallel",)), name=name)(proj)


def pool_time_bwd(dp, into, name):
    S_ = dp.shape[0]

    def body(dp_ref, _, du_ref):
        d = dp_ref[...].astype(F32)
        w, g = _pool_window(pl.program_id(0))
        t1 = (lax.broadcasted_iota(I32, d.shape, 0) + 1).astype(F32)
        q = d / jnp.minimum(t1, w)
        r2 = q + _shift_up(q, 1)
        r4 = r2 + _shift_up(r2, 2)
        r8 = r4 + _shift_up(r4, 4)
        r16 = r8 + _shift_up(r8, 8)
        du_ref[...] = (_pick(g, r2, r4, r8, r16) - d).astype(du_ref.dtype)

    return _pc(body, grid=(POOL_WIDTH // LANES,),
               in_specs=[pl.BlockSpec((S_, LANES), lambda j: (0, j)), pl.BlockSpec(memory_space=pl.ANY)],
               out_specs=pl.BlockSpec((S_, LANES), lambda j: (0, j)), out_shape=_sds(into.shape, into.dtype),
               input_output_aliases={1: 0}, compiler_params=_cparams(("parallel",)), name=name)(dp, into)


def _conv_post(j, a):
    n = a * lax.rsqrt(jnp.sum(a * a, axis=-1, keepdims=True) + EPS)
    nq = GDN_QK // LANES
    return jnp.where(j < nq, n * (GDN_DK ** -0.5), jnp.where(j < 2 * nq, n, a))


def _conv_taps(u):
    return [_shift_dn(u, 3), _shift_dn(u, 2), _shift_dn(u, 1), u]


def _conv_pre(taps, w):
    return w[0:1] * taps[0] + w[1:2] * taps[1] + w[2:3] * taps[2] + w[3:4] * taps[3]


def gdn_conv_fwd(proj, conv_w, name):
    S_ = proj.shape[0]

    def body(u_ref, w_ref, o_ref):
        o_ref[...] = _conv_post(pl.program_id(0), _silu(_conv_pre(_conv_taps(u_ref[...]), w_ref[...])))

    return _pc(body, grid=(GDN_CONV_CH // LANES,),
               in_specs=[pl.BlockSpec((S_, LANES), lambda j: (0, j)), pl.BlockSpec((8, LANES), lambda j: (0, j))],
               out_specs=pl.BlockSpec((S_, LANES), lambda j: (0, j)), out_shape=_sds((S_, GDN_CONV_CH), F32),
               compiler_params=_cparams(("parallel",)), name=name)(proj, conv_w)


def gdn_conv_bwd(proj, conv_w, dq, dk, dv, into, name):
    S_ = proj.shape[0]
    nq = GDN_QK // LANES

    def body(u_ref, w_ref, dq_ref, dk_ref, dv_ref, _, du_ref, dw_ref):
        j = pl.program_id(0)
        u, w = u_ref[...], w_ref[...]
        taps = _conv_taps(u)
        c = _conv_pre(taps, w)
        sig = jax.nn.sigmoid(c)
        dout = jnp.where(j < nq, dq_ref[...], jnp.where(j < 2 * nq, dk_ref[...], dv_ref[...]))
        _, vjp = jax.vjp(lambda a: _conv_post(j, a), c * sig)
        dc = vjp(dout)[0] * (sig * (1.0 + c * (1.0 - sig)))
        du = w[3:4] * dc + w[2:3] * _shift_up(dc, 1) + w[1:2] * _shift_up(dc, 2) + w[0:1] * _shift_up(dc, 3)
        du_ref[...] = du.astype(du_ref.dtype)
        rows = lax.broadcasted_iota(I32, (8, LANES), 0)
        dw = jnp.zeros((8, LANES), F32)
        for k in range(4):
            dw = dw + jnp.where(rows == k, jnp.sum(dc * taps[k], axis=0, keepdims=True), 0.0)
        dw_ref[...] = dw

    blk = lambda f: pl.BlockSpec((S_, LANES), f)
    return _pc(body, grid=(GDN_CONV_CH // LANES,),
               in_specs=[blk(lambda j: (0, j)), pl.BlockSpec((8, LANES), lambda j: (0, j)),
                         blk(lambda j: (0, jnp.minimum(j, nq - 1))), blk(lambda j: (0, jnp.clip(j - nq, 0, nq - 1))),
                         blk(lambda j: (0, jnp.clip(j - 2 * nq, 0, 2 * nq - 1))), pl.BlockSpec(memory_space=pl.ANY)],
               out_specs=[blk(lambda j: (0, j)), pl.BlockSpec((8, LANES), lambda j: (0, j))],
               out_shape=[_sds(into.shape, into.dtype), _sds((8, GDN_CONV_CH), F32)], input_output_aliases={5: 0},
               compiler_params=_cparams(("parallel",)), name=name)(proj, conv_w, dq, dk, dv, into)


_NN, _NT, _TN = ((1,), (0,)), ((1,), (1,)), ((0,), (0,))


def _split(x, n):
    parts = []
    for _ in range(n):
        h = x.astype(BF16)
        parts.append(h)
        x = x - h.astype(F32)
    return parts


def _dot(a, b, dn, mode):
    d = lambda p, q: lax.dot_general(p, q, (dn, ((), ())), preferred_element_type=F32)
    if mode == "lo":
        return d(a.astype(BF16), b.astype(BF16))
    if mode == "x3":
        (ah, al), (bh, bl) = _split(a, 2), _split(b, 2)
        return d(ah, bh) + (d(ah, bl) + d(al, bh))
    b0, b1, b2 = _split(b, 3)
    ab = a.astype(BF16)
    return d(ab, b0) + (d(ab, b1) + d(ab, b2))


def _make_dots(mode):
    @jax.custom_vjp
    def nn(a, b):
        return _dot(a, b, _NN, mode)

    @jax.custom_vjp
    def nt(a, b):
        return _dot(a, b, _NT, mode)

    @jax.custom_vjp
    def tn(a, b):
        return _dot(a, b, _TN, mode)

    nn.defvjp(lambda a, b: (nn(a, b), (a, b)), lambda r, d: (nt(d, r[1]), tn(r[0], d)))
    nt.defvjp(lambda a, b: (nt(a, b), (a, b)), lambda r, d: (nn(d, r[1]), tn(d, r[0])))
    tn.defvjp(lambda a, b: (tn(a, b), (a, b)), lambda r, d: (nt(r[1], d), nn(r[0], d)))
    return nn, nt, tn


_nn_hi, _nt_hi, _tn_hi = _make_dots("x3")
_nn_lo, _nt_lo, _tn_lo = _make_dots("lo")


@jax.custom_vjp
def _nn_const(a, b):
    return _dot(a, b, _NN, "xl")


_nn_const.defvjp(lambda a, b: (_nn_const(a, b), a), lambda a, d: (jnp.zeros_like(a), _dot(a, d, _TN, "xl")))


def _each(f, *lists):
    return [f(*xs) for xs in zip(*lists)]


@jax.custom_vjp
def _unit_inverses(xs):
    C = xs[0].shape[0]
    eye = (lax.broadcasted_iota(I32, (C, C), 0) == lax.broadcasted_iota(I32, (C, C), 1)).astype(F32)
    ainv, p = [eye + a for a in xs], xs
    for _ in range(int(math.log2(C)) - 1):
        p = _each(lambda a: _dot(a, a, _NN, "x3"), p)
        ainv = _each(lambda a, b: a + _dot(a, b, _NN, "x3"), ainv, p)
    return ainv


def _unit_inverses_bwd(ainv, d):
    left = _each(lambda a, g: _dot(a, g, _TN, "x3"), ainv, d)
    return (_each(lambda t, a: _dot(t, a, _NT, "x3"), left, ainv),)


_unit_inverses.defvjp(lambda xs: (lambda a: (a, a))(_unit_inverses(xs)), _unit_inverses_bwd)


def _gdn_chunk(q, k, v, gb, bb, state):
    C = GDN_C
    e0 = (lax.broadcasted_iota(I32, (1, LANES), 1) == 0).astype(F32)
    ri = lax.broadcasted_iota(I32, (C, C), 0)
    ci = lax.broadcasted_iota(I32, (C, C), 1)
    causal, strict = ri >= ci, ri > ci
    tri, eye, ones = causal.astype(F32), (ri == ci).astype(F32), jnp.ones((C, C), F32)
    last = lax.broadcasted_iota(I32, (C, LANES), 0) == C - 1
    g1 = _each(lambda a: jnp.sum(a * e0, -1, keepdims=True), gb)
    b1 = _each(lambda a: jnp.sum(a * e0, -1, keepdims=True), bb)
    gc_c = _each(lambda g: _nn_const(tri, jnp.broadcast_to(g, (C, C))), g1)
    gc_d = _each(lambda g: _nn_const(tri, jnp.broadcast_to(g, (C, LANES))), g1)
    gr_c = _each(lambda g: _nn_const(ones, eye * g), gc_c)
    decay = _each(lambda a, r: jnp.where(causal, jnp.exp(jnp.where(causal, a - r, 0.0)), 0.0), gc_c, gr_c)
    kb = _each(lambda a, b: a * b, k, b1)
    vb = _each(lambda a, b: a * b, v, b1)
    x = _each(lambda a, b, d: -jnp.where(strict, _nt_lo(a, b) * d, 0.0), kb, k, decay)
    ainv = _unit_inverses(x)
    u = _each(_nn_hi, ainv, vb)
    w = _each(lambda a, b, g: _nn_hi(a, b * jnp.exp(g)), ainv, kb, gc_d)
    attn = _each(lambda a, b, d: jnp.where(causal, _nt_lo(a, b) * d, 0.0), q, k, decay)
    v_new = _each(lambda a, b, s: a - _nn_lo(b, s), u, w, state)
    o = _each(lambda a, g, s, t, vn: _nn_lo(a * jnp.exp(g), s) + _nn_lo(t, vn), q, gc_d, state, attn, v_new)
    gl = _each(lambda g: jnp.sum(jnp.where(last, g, 0.0), axis=0, keepdims=True), gc_d)
    new_state = _each(lambda s, g, a, gd, vn: s * jnp.exp(jnp.sum(g * e0, -1, keepdims=True)) + _tn_lo(a * jnp.exp(g - gd), vn),
                      state, gl, k, gc_d, v_new)
    return o, new_state


def _head_slices(ref, width):
    return [ref[:, h * width:(h + 1) * width] for h in range(GDN_H)]


def gdn_chunk_fwd(qkv, g_b, beta_b, name):
    S_ = qkv.shape[0]
    N = S_ // GDN_C

    def body(q_ref, k_ref, v_ref, g_ref, b_ref, o_ref, s_ref, state):
        @pl.when(pl.program_id(0) == 0)
        def _():
            state[...] = jnp.zeros_like(state)

        st = [state[h] for h in range(GDN_H)]
        s_ref[0] = state[...]
        o, st2 = _gdn_chunk(_head_slices(q_ref, GDN_DK), _head_slices(k_ref, GDN_DK), _head_slices(v_ref, GDN_DV),
                            _head_slices(g_ref, GDN_DK), _head_slices(b_ref, GDN_DK), st)
        for h in range(GDN_H):
            o_ref[:, h * GDN_DV:(h + 1) * GDN_DV] = o[h]
            state[h] = st2[h]

    return _pc(body, grid=(N,),
               in_specs=[pl.BlockSpec((GDN_C, GDN_QK), lambda n: (n, 0)), pl.BlockSpec((GDN_C, GDN_QK), lambda n: (n, 1)),
                         pl.BlockSpec((GDN_C, GDN_V), lambda n: (n, 1)), pl.BlockSpec((GDN_C, GDN_QK), lambda n: (n, 0)),
                         pl.BlockSpec((GDN_C, GDN_QK), lambda n: (n, 0))],
               out_specs=[pl.BlockSpec((GDN_C, GDN_V), lambda n: (n, 0)),
                          pl.BlockSpec((1, GDN_H, GDN_DK, GDN_DV), lambda n: (n, 0, 0, 0))],
               out_shape=[_sds((S_, GDN_V), F32), _sds((N, GDN_H, GDN_DK, GDN_DV), F32)],
               scratch_shapes=[pltpu.VMEM((GDN_H, GDN_DK, GDN_DV), F32)],
               compiler_params=_cparams(("arbitrary",)), name=name)(qkv, qkv, qkv, g_b, beta_b)


def gdn_chunk_bwd(qkv, g_b, beta_b, states, do, name):
    S_ = qkv.shape[0]
    N = S_ // GDN_C

    def body(q_ref, k_ref, v_ref, g_ref, b_ref, s_ref, do_ref, dq_ref, dk_ref, dv_ref, dg_ref, db_ref, dstate):
        @pl.when(pl.program_id(0) == 0)
        def _():
            dstate[...] = jnp.zeros_like(dstate)

        _, vjp = jax.vjp(_gdn_chunk, _head_slices(q_ref, GDN_DK), _head_slices(k_ref, GDN_DK), _head_slices(v_ref, GDN_DV),
                         _head_slices(g_ref, GDN_DK), _head_slices(b_ref, GDN_DK), [s_ref[0, h] for h in range(GDN_H)])
        dq, dk, dv, dg, db, ds = vjp((_head_slices(do_ref, GDN_DV), [dstate[h] for h in range(GDN_H)]))
        for h in range(GDN_H):
            kk, vv = slice(h * GDN_DK, (h + 1) * GDN_DK), slice(h * GDN_DV, (h + 1) * GDN_DV)
            dq_ref[:, kk] = dq[h]
            dk_ref[:, kk] = dk[h]
            dv_ref[:, vv] = dv[h]
            dg_ref[:, kk] = dg[h]
            db_ref[:, kk] = db[h]
            dstate[h] = ds[h]

    r = lambda n: N - 1 - n
    qk = lambda c: pl.BlockSpec((GDN_C, GDN_QK), lambda n: (r(n), c))
    vs = lambda c: pl.BlockSpec((GDN_C, GDN_V), lambda n: (r(n), c))
    return _pc(body, grid=(N,),
               in_specs=[qk(0), qk(1), vs(1), qk(0), qk(0),
                         pl.BlockSpec((1, GDN_H, GDN_DK, GDN_DV), lambda n: (r(n), 0, 0, 0)), vs(0)],
               out_specs=[qk(0), qk(0), vs(0), qk(0), qk(0)],
               out_shape=[_sds((S_, GDN_QK), F32), _sds((S_, GDN_QK), F32), _sds((S_, GDN_V), F32),
                          _sds((S_, GDN_QK), F32), _sds((S_, GDN_QK), F32)],
               scratch_shapes=[pltpu.VMEM((GDN_H, GDN_DK, GDN_DV), F32)],
               compiler_params=_cparams(("arbitrary",)), name=name)(qkv, qkv, qkv, g_b, beta_b, states, do)


def _rope_tables(pos_ref, inv_ref, cm_ref, sg_ref):
    ang = pos_ref[...] * inv_ref[...]
    return jnp.cos(ang) * cm_ref[...], jnp.sin(ang) * sg_ref[...]


def mla_prep_fwd(qpad, kv, proj, pos, rope_consts, name):
    S_ = qpad.shape[0]
    ts = 512
    W = 2 * LANES

    def body(q_ref, kv_ref, kr_ref, pos_ref, inv_ref, cm_ref, sg_ref, qh_ref, kh_ref, vh_ref):
        cs, sn = _rope_tables(pos_ref, inv_ref, cm_ref, sg_ref)
        rope = lambda r: r * cs + pltpu.roll(r, LANES // 2, 1) * sn
        krr = rope(kr_ref[...].astype(F32)).astype(BF16)
        for h in range(MLA_H):
            qh_ref[h, :, 0:LANES] = (q_ref[:, h * W:h * W + LANES].astype(F32) * MLA_SCALE).astype(BF16)
            qh_ref[h, :, LANES:W] = (rope(q_ref[:, h * W + LANES:(h + 1) * W].astype(F32)) * MLA_SCALE).astype(BF16)
            kh_ref[h, :, 0:LANES] = kv_ref[:, h * W:h * W + LANES].astype(BF16)
            kh_ref[h, :, LANES:W] = krr
            vh_ref[h] = kv_ref[:, h * W + LANES:(h + 1) * W].astype(BF16)

    one = pl.BlockSpec((1, LANES), lambda i: (0, 0))
    return _pc(body, grid=(S_ // ts,),
               in_specs=[pl.BlockSpec((ts, MLA_H * W), lambda i: (i, 0)), pl.BlockSpec((ts, MLA_H * W), lambda i: (i, 0)),
                         pl.BlockSpec((ts, LANES), lambda i: (i, 1536 // LANES)), pl.BlockSpec((ts, 1), lambda i: (i, 0)),
                         one, one, one],
               out_specs=[pl.BlockSpec((MLA_H, ts, W), lambda i: (0, i, 0)), pl.BlockSpec((MLA_H, ts, W), lambda i: (0, i, 0)),
                          pl.BlockSpec((MLA_H, ts, LANES), lambda i: (0, i, 0))],
               out_shape=[_sds((MLA_H, S_, W), BF16), _sds((MLA_H, S_, W), BF16), _sds((MLA_H, S_, LANES), BF16)],
               compiler_params=_cparams(("parallel",)), name=name)(qpad, kv, proj, pos, *rope_consts)


def mla_prep_bwd(dqh, dkh, dvh, pos, rope_consts, into, name):
    S_ = dqh.shape[1]
    ts = 512
    W = 2 * LANES

    def body(dq_ref, dk_ref, dv_ref, pos_ref, inv_ref, cm_ref, sg_ref, _, dqp_ref, dkv_ref, dkr_ref):
        cs, sn = _rope_tables(pos_ref, inv_ref, cm_ref, sg_ref)
        rope_t = lambda g: g * cs + pltpu.roll(g * sn, LANES // 2, 1)
        acc = jnp.zeros((ts, LANES), F32)
        for h in range(MLA_H):
            dqp_ref[:, h * W:h * W + LANES] = (dq_ref[h, :, 0:LANES].astype(F32) * MLA_SCALE).astype(BF16)
            dqp_ref[:, h * W + LANES:(h + 1) * W] = (rope_t(dq_ref[h, :, LANES:W].astype(F32)) * MLA_SCALE).astype(BF16)
            dkv_ref[:, h * W:h * W + LANES] = dk_ref[h, :, 0:LANES]
            dkv_ref[:, h * W + LANES:(h + 1) * W] = dv_ref[h]
            acc = acc + dk_ref[h, :, LANES:W].astype(F32)
        dkr_ref[...] = rope_t(acc).astype(dkr_ref.dtype)

    one = pl.BlockSpec((1, LANES), lambda i: (0, 0))
    return _pc(body, grid=(S_ // ts,),
               in_specs=[pl.BlockSpec((MLA_H, ts, W), lambda i: (0, i, 0)), pl.BlockSpec((MLA_H, ts, W), lambda i: (0, i, 0)),
                         pl.BlockSpec((MLA_H, ts, LANES), lambda i: (0, i, 0)), pl.BlockSpec((ts, 1), lambda i: (i, 0)),
                         one, one, one, pl.BlockSpec(memory_space=pl.ANY)],
               out_specs=[pl.BlockSpec((ts, MLA_H * W), lambda i: (i, 0)), pl.BlockSpec((ts, MLA_H * W), lambda i: (i, 0)),
                          pl.BlockSpec((ts, LANES), lambda i: (i, 1536 // LANES))],
               out_shape=[_sds((S_, MLA_H * W), BF16), _sds((S_, MLA_H * W), BF16), _sds(into.shape, into.dtype)],
               input_output_aliases={7: 2}, compiler_params=_cparams(("parallel",)), name=name)(dqh, dkh, dvh, pos, *rope_consts, into)


NEG = -1e30


FLASH_TILE = 1024
FLASH_SUB = 512


def _scores(q, k, diagonal):
    s = lax.dot_general(q, k, (_NT, ((), ())), preferred_element_type=F32)
    if not diagonal:
        return s
    return jnp.where(lax.broadcasted_iota(I32, s.shape, 1) <= lax.broadcasted_iota(I32, s.shape, 0), s, NEG)


def _sub_blocks(t, diagonal):
    sub = min(FLASH_SUB, t) if diagonal else t
    return [(c * sub if diagonal else 0, slice(c * sub, (c + 1) * sub)) for c in range(t // sub)]


FLASH_HEADS = 2


def flash_fwd(qh, kh, vh, name):
    H, S_, W = qh.shape
    t = _tile(S_, FLASH_TILE)
    n = S_ // t
    G = FLASH_HEADS
    heads = list(range(G))

    def body(q_ref, k_ref, v_ref, o_ref, lse_ref, m_s, l_s, acc):
        qi, kj = pl.program_id(1), pl.program_id(2)

        @pl.when(kj == 0)
        def _():
            m_s[...] = jnp.full_like(m_s, NEG)
            l_s[...] = jnp.zeros_like(l_s)
            acc[...] = jnp.zeros_like(acc)

        def step(diagonal):
            s = _each(lambda a: _scores(q_ref[a], k_ref[a], diagonal), heads)
            m_old = _each(lambda a: m_s[a], heads)
            m_new = _each(lambda mo, sa: jnp.maximum(mo, jnp.max(sa, axis=-1, keepdims=True)), m_old, s)
            alpha = _each(lambda mo, mn: jnp.exp(mo - mn), m_old, m_new)
            p = _each(lambda sa, mn: jnp.exp(sa - mn[:, :1]), s, m_new)
            pv = _each(lambda pa, a: lax.dot_general(pa.astype(BF16), v_ref[a], (_NN, ((), ())), preferred_element_type=F32), p, heads)
            for a in heads:
                l_s[a] = alpha[a] * l_s[a] + jnp.sum(p[a], axis=-1, keepdims=True)
                acc[a] = alpha[a] * acc[a] + pv[a]
                m_s[a] = m_new[a]

        pl.when(kj < qi)(lambda: step(False))
        pl.when(kj == qi)(lambda: step(True))

        @pl.when(kj == n - 1)
        def _():
            for a in heads:
                o_ref[:, a * LANES:(a + 1) * LANES] = (acc[a] / l_s[a]).astype(o_ref.dtype)
                lse_ref[a] = m_s[a] + jnp.log(l_s[a])

    return _pc(body, grid=(H // G, n, n),
               in_specs=[pl.BlockSpec((G, t, W), lambda h, i, j: (h, i, 0)),
                         pl.BlockSpec((G, t, W), lambda h, i, j: (h, jnp.minimum(i, j), 0)),
                         pl.BlockSpec((G, t, LANES), lambda h, i, j: (h, jnp.minimum(i, j), 0))],
               out_specs=[pl.BlockSpec((t, G * LANES), lambda h, i, j: (i, h)), pl.BlockSpec((G, t, LANES), lambda h, i, j: (h, i, 0))],
               out_shape=[_sds((S_, H * LANES), BF16), _sds((H, S_, LANES), F32)],
               scratch_shapes=[pltpu.VMEM((G, t, LANES), F32)] * 3,
               compiler_params=_cparams(("parallel", "parallel", "arbitrary")), name=name)(qh, kh, vh)


def flash_bwd(qh, kh, vh, o, lse, do, name):
    H, S_, W = qh.shape
    t = _tile(S_, FLASH_TILE)
    n = S_ // t
    G = FLASH_HEADS
    heads = list(range(G))

    def body(q_ref, k_ref, v_ref, o_ref, lse_ref, do_ref, dq_ref, dk_ref, dv_ref, dq_acc, dk_acc, dv_acc):
        kj, qi = pl.program_id(1), pl.program_id(2)

        @pl.when(jnp.logical_and(kj == 0, qi == 0))
        def _():
            dq_acc[...] = jnp.zeros_like(dq_acc)

        @pl.when(qi == 0)
        def _():
            dk_acc[...] = jnp.zeros_like(dk_acc)
            dv_acc[...] = jnp.zeros_like(dv_acc)

        def step(diagonal):
            lanes = lambda a: slice(a * LANES, (a + 1) * LANES)
            do_ = _each(lambda a: do_ref[:, lanes(a)], heads)
            dob = _each(lambda d: d.astype(BF16), do_)
            delta = _each(lambda d, a: jnp.sum(d.astype(F32) * o_ref[:, lanes(a)].astype(F32), axis=-1, keepdims=True), do_, heads)
            for r0, keys in _sub_blocks(t, diagonal):
                p = _each(lambda a: jnp.exp(_scores(q_ref[a, r0:, :], k_ref[a, keys, :], diagonal) - lse_ref[a, r0:, :1]), heads)
                dp = _each(lambda d, a: lax.dot_general(d[r0:], v_ref[a, keys, :], (_NT, ((), ())), preferred_element_type=F32), dob, heads)
                ds = _each(lambda pa, dpa, de: (pa * (dpa - de[r0:])).astype(BF16), p, dp, delta)
                rows = pl.ds(pl.multiple_of(qi * t, t) + r0, t - r0)
                for a in heads:
                    dv_acc[a, keys, :] += lax.dot_general(p[a].astype(BF16), dob[a][r0:], (_TN, ((), ())), preferred_element_type=F32)
                    dk_acc[a, keys, :] += lax.dot_general(ds[a], q_ref[a, r0:, :], (_TN, ((), ())), preferred_element_type=F32)
                    dq_acc[a, rows, :] += lax.dot_general(ds[a], k_ref[a, keys, :], (_NN, ((), ())), preferred_element_type=F32)

        pl.when(qi > kj)(lambda: step(False))
        pl.when(qi == kj)(lambda: step(True))

        @pl.when(qi == n - 1)
        def _():
            dk_ref[...] = dk_acc[...].astype(BF16)
            dv_ref[...] = dv_acc[...].astype(BF16)

        @pl.when(jnp.logical_and(kj == n - 1, qi == n - 1))
        def _():
            dq_ref[...] = dq_acc[...].astype(BF16)

    qrow = lambda h, j, i: jnp.maximum(i, j)
    return _pc(body, grid=(H // G, n, n),
               in_specs=[pl.BlockSpec((G, t, W), lambda h, j, i: (h, qrow(h, j, i), 0)),
                         pl.BlockSpec((G, t, W), lambda h, j, i: (h, j, 0)),
                         pl.BlockSpec((G, t, LANES), lambda h, j, i: (h, j, 0)),
                         pl.BlockSpec((t, G * LANES), lambda h, j, i: (qrow(h, j, i), h)),
                         pl.BlockSpec((G, t, LANES), lambda h, j, i: (h, qrow(h, j, i), 0)),
                         pl.BlockSpec((t, G * LANES), lambda h, j, i: (qrow(h, j, i), h))],
               out_specs=[pl.BlockSpec((G, S_, W), lambda h, j, i: (h, 0, 0)),
                          pl.BlockSpec((G, t, W), lambda h, j, i: (h, j, 0)),
                          pl.BlockSpec((G, t, LANES), lambda h, j, i: (h, j, 0))],
               out_shape=[_sds((H, S_, W), BF16), _sds((H, S_, W), BF16), _sds((H, S_, LANES), BF16)],
               scratch_shapes=[pltpu.VMEM((G, S_, W), F32), pltpu.VMEM((G, t, W), F32), pltpu.VMEM((G, t, LANES), F32)],
               compiler_params=_cparams(("parallel", "arbitrary", "arbitrary")), name=name)(qh, kh, vh, o, lse, do)


def loss_head(x, target, g, name):
    S_ = x.shape[0]
    ts = 512

    def body(x_ref, t_ref, g_ref, l_ref, dx_ref, dg_ref):
        @pl.when(pl.program_id(0) == 0)
        def _():
            l_ref[...] = jnp.zeros_like(l_ref)
            dg_ref[...] = jnp.zeros_like(dg_ref)

        y, vjp = jax.vjp(_rms, x_ref[...], g_ref[...])
        err = y - t_ref[...]
        l_ref[...] += 0.5 * jnp.sum(jnp.sum(err * err, axis=-1, keepdims=True), axis=0, keepdims=True) / D
        dx, dg = vjp(err / D)
        dx_ref[...] = dx
        dg_ref[...] += dg

    row = pl.BlockSpec((ts, D), lambda i: (i, 0))
    return _pc(body, grid=(S_ // ts,), in_specs=[row, row, pl.BlockSpec((1, D), lambda i: (0, 0))],
               out_specs=[pl.BlockSpec((1, LANES), lambda i: (0, 0)), row, pl.BlockSpec((1, D), lambda i: (0, 0))],
               out_shape=[_sds((1, LANES), F32), _sds((S_, D), F32), _sds((1, D), F32)],
               compiler_params=_cparams(("arbitrary",)), name=name)(x, target, g)


def adamw(w, parts, m, v, name):
    R, C = w.shape
    rows = [p.shape[1] for p in parts[0]]
    tr = R
    for cand in (512, 256, 128, 64, 32, 16, 8):
        if all(r % cand == 0 for r in rows) and cand * C * 4 * len(rows) <= 2 * 1024 * 1024:
            tr = cand
            break
    c1 = 1.0 - ADAM_B1 ** ADAM_STEP
    c2 = 1.0 - ADAM_B2 ** ADAM_STEP
    starts = [sum(rows[:k]) // tr for k in range(len(rows))]
    flat = [p for part in parts for p in part]

    def body(*refs):
        w_ref, m_ref, v_ref = refs[0], refs[1 + len(flat)], refs[2 + len(flat)]
        g_ref, d_ref, nm_ref, nv_ref = refs[3 + len(flat):]
        i = pl.program_id(0)
        gg, at = None, 1
        for part in parts:
            val = None
            for k in range(len(part)):
                p_ref = refs[at]
                at += 1
                s = p_ref[0].astype(F32)
                for n in range(1, p_ref.shape[0]):
                    s = s + p_ref[n].astype(F32)
                val = s if val is None else jnp.where(i >= starts[k], s, val)
            gg = val if gg is None else gg + val
        m2 = ADAM_B1 * m_ref[...] + (1.0 - ADAM_B1) * gg
        v2 = ADAM_B2 * v_ref[...] + (1.0 - ADAM_B2) * (gg * gg)
        g_ref[...] = gg
        d_ref[...] = -ADAM_LR * ((m2 / c1) / (jnp.sqrt(v2 / c2) + ADAM_EPS) + ADAM_WD * w_ref[...])
        nm_ref[...] = m2
        nv_ref[...] = v2

    blk = pl.BlockSpec((tr, C), lambda i: (i, 0))
    piece = lambda p, k: pl.BlockSpec((p.shape[0], tr, C), lambda i: (0, jnp.clip(i - starts[k], 0, rows[k] // tr - 1), 0))
    pblk = [piece(p, k) for part in parts for k, p in enumerate(part)]
    return _pc(body, grid=(R // tr,), in_specs=[blk] + pblk + [blk, blk], out_specs=[blk] * 4, out_shape=[_sds((R, C), F32)] * 4,
               compiler_params=_cparams(("parallel",)), name=name)(w, *flat, m, v)


def sum_slots(recv, name):
    n, R, C = recv.shape

    def body(r_ref, o_ref):
        acc = r_ref[0]
        for s in range(1, n):
            acc = acc + r_ref[s]
        o_ref[...] = acc

    return _pc(body, grid=(1,), in_specs=[pl.BlockSpec((n, R, C), lambda i: (0, 0, 0))],
               out_specs=pl.BlockSpec((R, C), lambda i: (0, 0)), out_shape=_sds((R, C), F32),
               compiler_params=_cparams(("arbitrary",)), name=name)(recv)


def _chip_peers():
    x, y, c = lax.axis_index("x"), lax.axis_index("y"), lax.axis_index("c")
    return (x, y, c), [(1 - x, y, c), (x, 1 - y, c), (1 - x, 1 - y, c)]


def _chip_index(p):
    return 2 * p[0] + p[1]


def _win(ref, axis, chip, size):
    if axis is None:
        return ref.at[chip]
    idx = [slice(None)] * len(ref.shape)
    idx[axis] = pl.ds(pl.multiple_of(chip * size, size), size)
    return ref.at[tuple(idx)]


def _remote(src, dst, send_sem, recv_sem, peer):
    return pltpu.make_async_remote_copy(src_ref=src, dst_ref=dst, send_sem=send_sem, recv_sem=recv_sem, device_id=peer,
                                        device_id_type=MESH)


HBM_SPEC = pl.BlockSpec(memory_space=pltpu.HBM)
SEM_SPEC = pl.BlockSpec(memory_space=pltpu.SEMAPHORE)
ANY_SPEC = pl.BlockSpec(memory_space=pl.ANY)
DATAFLOW = pltpu.SideEffectType.DATAFLOW_SIDE_EFFECTING


def gather_piece(i, l, o, axis, size):
    return (i, lambda r, chip: r.at[l], o, lambda r, chip: _win(r, axis, chip, size))


def scatter_piece(i, o, axis, size):
    return (i, lambda r, chip: _win(r, axis, chip, size), o, lambda r, chip: r.at[chip])


def whole_piece(i):
    return (i, lambda r, chip: r, i, lambda r, chip: r)


def _copies(pieces, in_refs, out_refs, send, recv, sibling):
    me, peers = _chip_peers()
    if sibling:
        peers = [(me[0], me[1], 1 - me[2])]
    mine = _chip_index(me)
    remote = []
    for n, (i, src, o, dst) in enumerate(pieces):
        d = dst(out_refs[o], mine)
        remote += [_remote(src(in_refs[i], _chip_index(p)), d, send.at[len(peers) * n + k], recv.at[len(peers) * n + k], p)
                   for k, p in enumerate(peers)]
    return remote


def own_window(a, axis, size, chip):
    if axis is None:
        return lax.dynamic_index_in_dim(a, chip, 0, keepdims=False)
    return lax.dynamic_slice_in_dim(a, chip * size, size, axis=axis)


def place_own(land, own, axis, size, chip):
    if axis is None:
        return lax.dynamic_update_slice_in_dim(land, own[None], chip, axis=0)
    return lax.dynamic_update_slice_in_dim(land, own, chip * size, axis=axis)


def exchange_start(pieces, ins, out_shapes, after, name, sibling=False):
    n_in, n_out, ncp = len(ins), len(out_shapes), len(pieces)

    def body(*refs):
        in_refs, land_refs = refs[:n_in], refs[n_in:n_in + n_out]
        send, recv = refs[n_in + n_out + 1], refs[n_in + n_out + 2]
        token = refs[-1]
        for cp in _copies(pieces, in_refs, land_refs, send, recv, sibling):
            cp.start()
        token[...] = jnp.zeros_like(token)

    hbm = lambda a: pltpu.with_memory_space_constraint(a, pltpu.HBM)
    lands = [hbm(lax.empty(s.shape, s.dtype)) for s in out_shapes]
    sem = pltpu.SemaphoreType.DMA(((1 if sibling else 3) * ncp,))
    thru = [pltpu.HBM(a.shape, a.dtype) for a in ins] + [pltpu.HBM(s.shape, s.dtype) for s in out_shapes]
    res = _pc(body, in_specs=[HBM_SPEC] * (n_in + n_out) + [ANY_SPEC],
              out_specs=[SEM_SPEC, SEM_SPEC] + [HBM_SPEC] * (n_in + n_out) + [pl.BlockSpec(memory_space=pltpu.VMEM)],
              out_shape=[sem, sem] + thru + [_sds((8, LANES), F32)],
              input_output_aliases={i: 2 + i for i in range(n_in + n_out)},
              compiler_params=pltpu.CompilerParams(has_side_effects=DATAFLOW), name=name)(*[hbm(a) for a in ins], *lands, after)
    return (res[0], res[1]), list(res[2:2 + n_in]), list(res[2 + n_in:2 + n_in + n_out]), res[-1]


def exchange_wait(pieces, sems, ins, lands, after, name, sibling=False):
    n_in, n_out = len(ins), len(lands)

    def body(*refs):
        in_refs, land_refs = refs[:n_in], refs[n_in:n_in + n_out]
        send, recv = refs[n_in + n_out], refs[n_in + n_out + 1]
        for cp in _copies(pieces, in_refs, land_refs, send, recv, sibling):
            cp.wait_send()
            cp.wait_recv()

    thru = [pltpu.HBM(a.shape, a.dtype) for a in ins] + [pltpu.HBM(a.shape, a.dtype) for a in lands]
    res = _pc(body, in_specs=[HBM_SPEC] * (n_in + n_out) + [SEM_SPEC, SEM_SPEC, ANY_SPEC], out_specs=[HBM_SPEC] * (n_in + n_out),
              out_shape=thru, input_output_aliases={i: i for i in range(n_in + n_out)},
              compiler_params=pltpu.CompilerParams(has_side_effects=DATAFLOW), name=name)(*ins, *lands, sems[0], sems[1], after)
    return list(res[:n_in]), list(res[n_in:])


def exchange_all(buf, name):
    def body(in_ref, out_ref, send, recv, local):
        x, y, c = lax.axis_index("x"), lax.axis_index("y"), lax.axis_index("c")
        mine = 4 * x + 2 * y + c
        loc = pltpu.make_async_copy(in_ref, out_ref.at[mine], local)
        loc.start()
        copies = [loc]
        for k in range(1, 8):
            peer = (x ^ (k >> 2), y ^ ((k >> 1) & 1), c ^ (k & 1))
            cp = pltpu.make_async_remote_copy(src_ref=in_ref, dst_ref=out_ref.at[mine], send_sem=send.at[k - 1],
                                              recv_sem=recv.at[k - 1], device_id=peer, device_id_type=MESH)
            cp.start()
            copies.append(cp)
        for cp in copies:
            cp.wait()

    anyspec = pl.BlockSpec(memory_space=pl.ANY)
    return _pc(body, in_specs=[anyspec], out_specs=anyspec, out_shape=_sds((8,) + buf.shape, buf.dtype),
               scratch_shapes=[pltpu.SemaphoreType.DMA((7,)), pltpu.SemaphoreType.DMA((7,)), pltpu.SemaphoreType.DMA],
               name=name)(buf)


def _norm_fwd(x, g, name):
    return rowwise(f_rms, [(x, D, 0, 0)], [(g, D, 0, 0)], [(D, 0, BF16)], ts=1024, name=name)[0]


def _norm_bwd(x, g, dh, dres, name):
    (dx,), (dg,) = rowwise_bwd(f_rms, [(x, D, 0, 0)], [(g, D, 0, 0)], [(dh, D, 0, 0)], need=[True],
                               adds={0: (dres, D, 0, 0)}, ts=1024, name=name)
    return dx, dg


def pool_fwd(x, W, tag, late=None):
    h = _norm_fwd(x, W["ng"], tag + "_norm")
    proj = mm(h, W["w_in"], out_dtype=BF16, name=tag + "_in")
    if late is not None:
        W = dict(W, **late(proj))
    p = pool_time_fwd(proj, tag + "_win")
    pg = gmm("nn", p, W["w_grp"], G=4, out_dtype=BF16, name=tag + "_grp")
    y = rowwise(f_pool_gate, [(pg, POOL_GROUP, 0, 1), (proj, POOL_GROUP, 4, 1)], [(W["scale"], POOL_GROUP, 0, 1)],
                [(POOL_GROUP, 1, BF16)], ncol=4, ts=1024, name=tag + "_gate")[0]
    xn = mm(y, W["w_out"], add=x, name=tag + "_out")
    return xn, (x, h, proj, p, pg, y)


def pool_bwd(dxn, W, saved, tag, after=None, emit=None):
    x, h, proj, p, pg, y = saved
    emit = emit or (lambda grads: None)
    dy = mm(dxn, W["w_out"], tb=True, after=after, out_dtype=BF16, name=tag + "_dy")
    g = {}
    (dpg, dproj), (g["scale"],) = rowwise_bwd(
        f_pool_gate, [(pg, POOL_GROUP, 0, 1), (proj, POOL_GROUP, 4, 1)], [(W["scale"], POOL_GROUP, 0, 1)],
        [(dy, POOL_GROUP, 0, 1)], need=[True, True], place={1: (2 * POOL_WIDTH, 4)}, narrow=(0, 1), ncol=4, ts=1024, name=tag + "_dgate")
    dp = gmm("nt", dpg, W["w_grp"], G=4, out_dtype=BF16, name=tag + "_dp")
    dproj = pool_time_bwd(dp, dproj, tag + "_dwin")
    g["w_in"] = mm(h, dproj, ta=True, out_dtype=BF16, name=tag + "_dw_in")
    t1 = emit({"w_in": g["w_in"]})
    g["w_out"] = mm(y, dxn, ta=True, after=t1, out_dtype=BF16, name=tag + "_dwout")
    g["w_grp"] = gmm("tn", p, dpg, G=4, out_dtype=BF16, name=tag + "_dwgrp")
    t2 = emit({"w_out": g["w_out"], "w_grp": g["w_grp"]})
    dh = mm(dproj, W["w_in"], tb=True, after=t2, name=tag + "_dh")
    dx, g["ng"] = _norm_bwd(x, W["ng"], dh, dxn, tag + "_dnorm")
    return dx, g


def gdn_fwd(x, W, tag, late=None):
    h = _norm_fwd(x, W["ng"], tag + "_norm")
    proj = mm(h, W["w_in"], name=tag + "_in")
    qkv = gdn_conv_fwd(proj, W["conv"], tag + "_conv")
    g_b, beta_b = rowwise(f_gdn_gates, [(proj, LANES, 6144 // LANES, 0)], [(W["a_log"], LANES, 0, 0), (W["dt_bias"], LANES, 0, 0)],
                          [(GDN_QK, 0, F32), (GDN_QK, 0, F32)], ts=1024, name=tag + "_gates")
    o, states = gdn_chunk_fwd(qkv, g_b, beta_b, tag + "_chunk")
    og = rowwise(f_gdn_out, [(o, GDN_DV, 0, 1), (proj, GDN_DV, 4096 // GDN_DV, 1)], [(W["norm_g"], GDN_DV, 0, 0)],
                 [(GDN_DV, 1, BF16)], ncol=GDN_H, ts=2048, name=tag + "_onorm")[0]
    if late is not None:
        W = dict(W, **late(og))
    xn = mm(og, W["w_out"], add=x, name=tag + "_out")
    return xn, (x, h, proj, qkv, g_b, beta_b, o, states, og)


def gdn_bwd(dxn, W, saved, tag, after=None):
    x, h, proj, qkv, g_b, beta_b, o, states, og = saved
    dog = mm(dxn, W["w_out"], tb=True, after=after, name=tag + "_dog")
    g = {"w_out": mm(og, dxn, ta=True, out_dtype=BF16, name=tag + "_dwout")}
    (do, dproj), (g["norm_g"],) = rowwise_bwd(
        f_gdn_out, [(o, GDN_DV, 0, 1), (proj, GDN_DV, 4096 // GDN_DV, 1)], [(W["norm_g"], GDN_DV, 0, 0)],
        [(dog, GDN_DV, 0, 1)], need=[True, True], place={1: (GDN_IN_PAD, 4096 // GDN_DV)}, narrow=(1,), ncol=GDN_H, ts=2048, name=tag + "_donorm")
    dq, dk, dv, dg_b, dbeta_b = gdn_chunk_bwd(qkv, g_b, beta_b, states, do, tag + "_dchunk")
    (dproj,), (g["a_log"], g["dt_bias"]) = rowwise_bwd(
        f_gdn_gates, [(proj, LANES, 6144 // LANES, 0)], [(W["a_log"], LANES, 0, 0), (W["dt_bias"], LANES, 0, 0)],
        [(dg_b, GDN_QK, 0, 0), (dbeta_b, GDN_QK, 0, 0)], need=[True], place={0: (dproj, 6144 // LANES)}, ts=1024, name=tag + "_dgates")
    dproj, g["conv"] = gdn_conv_bwd(proj, W["conv"], dq, dk, dv, dproj, tag + "_dconv")
    dh = mm(dproj, W["w_in"], tb=True, name=tag + "_dh")
    g["w_in"] = mm(h, dproj, ta=True, out_dtype=BF16, name=tag + "_dw_in")
    dx, g["ng"] = _norm_bwd(x, W["ng"], dh, dxn, tag + "_dnorm")
    return dx, g


def mla_fwd(x, pos, W, tag):
    h = _norm_fwd(x, W["ng"], tag + "_norm")
    proj = mm(h, W["w_in"], out_dtype=BF16, name=tag + "_in")
    hq = rowwise(f_rms, [(proj, MLA_Q_LORA, 0, 0)], [(W["q_g"], MLA_Q_LORA, 0, 0)], [(MLA_Q_LORA, 0, BF16)], ts=1024, name=tag + "_qnorm")[0]
    hkv = rowwise(f_rms, [(proj, MLA_KV_LORA, 2, 0)], [(W["kv_g"], MLA_KV_LORA, 0, 0)], [(MLA_KV_LORA, 0, BF16)], ts=1024, name=tag + "_kvnorm")[0]
    qpad = mm(hq, W["w_uq"], out_dtype=BF16, name=tag + "_uq")
    kv = mm(hkv, W["w_ukv"], out_dtype=BF16, name=tag + "_ukv")
    qh, kh, vh = mla_prep_fwd(qpad, kv, proj, pos, W["rope"], tag + "_prep")
    o, lse = flash_fwd(qh, kh, vh, tag + "_attn")
    og = rowwise(f_ogate, [(o, 512, 0, 1), (proj, 512, 4, 1)], [], [(512, 1, BF16)], ncol=4, ts=1024, name=tag + "_ogate")[0]
    xn = mm(og, W["w_out"], add=x, name=tag + "_out")
    return xn, (x, h, proj, hq, hkv, qh, kh, vh, o, lse, og)


def mla_bwd(dxn, pos, W, saved, tag, after=None):
    x, h, proj, hq, hkv, qh, kh, vh, o, lse, og = saved
    dog = mm(dxn, W["w_out"], tb=True, after=after, out_dtype=BF16, name=tag + "_dog")
    g = {"w_out": mm(og, dxn, ta=True, out_dtype=BF16, name=tag + "_dwout")}
    dproj = jnp.zeros(proj.shape, BF16)
    (do, dproj), _ = rowwise_bwd(f_ogate, [(o, 512, 0, 1), (proj, 512, 4, 1)], [], [(dog, 512, 0, 1)], need=[True, True],
                                 place={1: (dproj, 4)}, narrow=(0,), ncol=4, ts=1024, name=tag + "_dogate")
    dqh, dkh, dvh = flash_bwd(qh, kh, vh, o, lse, do, tag + "_dattn")
    dqpad, dkv, dproj = mla_prep_bwd(dqh, dkh, dvh, pos, W["rope"], dproj, tag + "_dprep")
    dhq = mm(dqpad, W["w_uq"], tb=True, name=tag + "_dhq")
    g["w_uq"] = mm(hq, dqpad, ta=True, out_dtype=BF16, name=tag + "_dwuq")
    dhkv = mm(dkv, W["w_ukv"], tb=True, name=tag + "_dhkv")
    g["w_ukv"] = mm(hkv, dkv, ta=True, out_dtype=BF16, name=tag + "_dwukv")
    (dproj,), (g["q_g"],) = rowwise_bwd(f_rms, [(proj, MLA_Q_LORA, 0, 0)], [(W["q_g"], MLA_Q_LORA, 0, 0)], [(dhq, MLA_Q_LORA, 0, 0)],
                                        need=[True], place={0: (dproj, 0)}, ts=512, name=tag + "_dqnorm")
    (dproj,), (g["kv_g"],) = rowwise_bwd(f_rms, [(proj, MLA_KV_LORA, 2, 0)], [(W["kv_g"], MLA_KV_LORA, 0, 0)], [(dhkv, MLA_KV_LORA, 0, 0)],
                                         need=[True], place={0: (dproj, 2)}, ts=512, name=tag + "_dkvnorm")
    dh = mm(dproj, W["w_in"], tb=True, name=tag + "_dh")
    g["w_in"] = mm(h, dproj, ta=True, out_dtype=BF16, name=tag + "_dw_in")
    dx, g["ng"] = _norm_bwd(x, W["ng"], dh, dxn, tag + "_dnorm")
    return dx, g


def _pad_cols(a, n):
    return jnp.pad(a, ((0, 0), (0, n - a.shape[1])))


def _mla_w_in_layout(w):
    z = lambda n: jnp.zeros((w.shape[0], n), w.dtype)
    kr = w[:, 1280:1344]
    return jnp.concatenate([w[:, :768], z(256), w[:, 768:1280], kr[:, :32], z(32), kr[:, 32:], z(32), z(384), w[:, 1344:]], axis=1)


def _mla_w_in_unlayout(g):
    return jnp.concatenate([g[:, :768], g[:, 1024:1536], g[:, 1536:1568], g[:, 1600:1632], g[:, 2048:]], axis=1)


def _mla_w_uq_layout(w):
    w3 = w.reshape(w.shape[0], MLA_H, MLA_NOPE + MLA_ROPE)
    z = jnp.zeros((w.shape[0], MLA_H, 32), w.dtype)
    return jnp.concatenate([w3[..., :128], w3[..., 128:160], z, w3[..., 160:192], z], axis=-1).reshape(w.shape[0], MLA_H * 256)


def _mla_w_uq_unlayout(g):
    g3 = g.reshape(g.shape[0], MLA_H, 256)
    return jnp.concatenate([g3[..., :128], g3[..., 128:160], g3[..., 192:224]], axis=-1).reshape(g.shape[0], MLA_H * 192)


def _rope_consts():
    half = MLA_ROPE // 2
    inv = ROPE_THETA ** (-jnp.arange(half, dtype=F32) / half)
    z = jnp.zeros((half,), F32)
    o = jnp.ones((half,), F32)
    row = lambda *p: jnp.concatenate(p).reshape(1, LANES)
    return row(inv, z, inv, z), row(o, z, o, z), row(-o, z, o, z)


BIG = ["pool_w_in", "pool_w_grp", "pool_w_out", "gdn_w_in", "gdn_w_out", "mla_w_in", "mla_w_uq", "mla_w_ukv", "mla_w_out"]
BIG_LAYOUT = {"pool_w_in": (1, 1024, (1024, 4096)), "pool_w_grp": (1, 128, (4, 512, 512)), "pool_w_out": (0, 512, (2048, 1024)),
              "gdn_w_in": (None, None, (4, 1024, 1540)), "gdn_w_out": (0, 512, (2048, 1024)),
              "mla_w_in": (None, None, (4, 1024, 848)), "mla_w_uq": (1, 768, (768, 3072)), "mla_w_ukv": (1, 1024, (512, 4096)),
              "mla_w_out": (0, 512, (2048, 1024))}
SMALL_SHARDED = ["pool_scale", "gdn_conv", "mla_q_norm_g", "mla_kv_norm_g"]
SMALL_AXIS = {"pool_scale": 1, "gdn_conv": 2, "mla_q_norm_g": 1, "mla_kv_norm_g": 1}
REPLICATED = ["norm_g", "gdn_a_log", "gdn_dt_bias", "gdn_norm_g", "final_g"]
PACK_C = 1024


def _pack(parts, dtype, row_mult):
    flat = jnp.concatenate([p.reshape(-1).astype(dtype) for p in parts])
    rows = -(-flat.shape[0] // PACK_C)
    rows = -(-rows // row_mult) * row_mult
    return jnp.pad(flat, (0, rows * PACK_C - flat.shape[0])).reshape(rows, PACK_C)


def _unpack(buf, shapes):
    lead = buf.shape[:-2]
    flat = buf.reshape(lead + (-1,))
    out, off = [], 0
    for s in shapes:
        n = int(np.prod(s))
        out.append(flat[..., off:off + n].reshape(lead + tuple(s)))
        off += n
    return out


def _unshard(g4, axis):
    a = jnp.moveaxis(g4, 0, axis)
    s = a.shape
    return a.reshape(s[:axis] + (s[axis] * s[axis + 1],) + s[axis + 2:])


def _to_shards(a, axis):
    s = a.shape
    return jnp.moveaxis(a.reshape(s[:axis] + (4, s[axis] // 4) + s[axis + 1:]), axis, 0)


def layer_weights(full, small, rep, layer):
    ng = rep["norm_g"][layer:layer + 1]
    side_by_side = lambda a4: jnp.moveaxis(a4, 0, 1).reshape(a4.shape[1], 4 * a4.shape[2])
    if layer in (0, 3):
        j = layer // 3
        return dict(ng=ng, w_in=full[("pool_w_in", j)], w_grp=full[("pool_w_grp", j)], scale=small["pool_scale"][j:j + 1],
                    w_out=full[("pool_w_out", j)])
    if layer == 1:
        return dict(ng=ng, w_in=_pad_cols(side_by_side(full[("gdn_w_in", 0)]), GDN_IN_PAD),
                    conv=jnp.pad(small["gdn_conv"][0], ((0, 4), (0, 0))), a_log=_pad_cols(rep["gdn_a_log"], LANES),
                    dt_bias=_pad_cols(rep["gdn_dt_bias"], LANES), norm_g=rep["gdn_norm_g"], w_out=full.get(("gdn_w_out", 0)))
    return dict(ng=ng, w_in=_mla_w_in_layout(side_by_side(full[("mla_w_in", 0)])), q_g=small["mla_q_norm_g"],
                kv_g=small["mla_kv_norm_g"], w_uq=_mla_w_uq_layout(full[("mla_w_uq", 0)]), w_ukv=full[("mla_w_ukv", 0)],
                w_out=full[("mla_w_out", 0)], rope=_rope_consts())


def big_grad_pieces(gl):
    g0, g1, g2, g3 = gl
    slots = lambda a: jnp.moveaxis(a.reshape(a.shape[0], 4, a.shape[1] // 4), 1, 0)
    out = {}
    for l, g in ((0, g0), (1, g3)):
        if g is not None:
            out.update({("pool_w_in", l): g["w_in"], ("pool_w_grp", l): g["w_grp"], ("pool_w_out", l): g["w_out"]})
    if g1 is not None:
        out.update({("gdn_w_in", 0): slots(g1["w_in"][:, :GDN_IN]), ("gdn_w_out", 0): g1["w_out"]})
    if g2 is not None:
        out.update({("mla_w_in", 0): slots(_mla_w_in_unlayout(g2["w_in"])), ("mla_w_uq", 0): _mla_w_uq_unlayout(g2["w_uq"]),
                    ("mla_w_ukv", 0): g2["w_ukv"], ("mla_w_out", 0): g2["w_out"]})
    return out


def small_grads(gl, dfinal):
    g0, g1, g2, g3 = gl
    return {"norm_g": jnp.concatenate([g0["ng"], g1["ng"], g2["ng"], g3["ng"]], axis=0),
            "pool_scale": jnp.concatenate([g0["scale"], g3["scale"]], axis=0), "gdn_conv": g1["conv"][None, :4],
            "gdn_a_log": g1["a_log"][:, :GDN_H], "gdn_dt_bias": g1["dt_bias"][:, :GDN_H], "gdn_norm_g": g1["norm_g"],
            "mla_q_norm_g": g2["q_g"], "mla_kv_norm_g": g2["kv_g"], "final_g": dfinal.reshape(D)}


NAMES = ["norm_g", "pool_w_in", "pool_w_grp", "pool_scale", "pool_w_out", "gdn_w_in", "gdn_conv", "gdn_a_log", "gdn_dt_bias",
         "gdn_norm_g", "gdn_w_out", "mla_w_in", "mla_q_norm_g", "mla_w_uq", "mla_kv_norm_g", "mla_w_ukv", "mla_w_out", "final_g"]


def kernel(x, positions, norm_g, pool_w_in, pool_w_grp, pool_scale, pool_w_out, gdn_w_in, gdn_conv, gdn_a_log, gdn_dt_bias, gdn_norm_g, gdn_w_out, mla_w_in, mla_q_norm_g, mla_w_uq, mla_kv_norm_g, mla_w_ukv, mla_w_out, final_g, loss_target, m_norm_g, m_pool_w_in, m_pool_w_grp, m_pool_scale, m_pool_w_out, m_gdn_w_in, m_gdn_conv, m_gdn_a_log, m_gdn_dt_bias, m_gdn_norm_g, m_gdn_w_out, m_mla_w_in, m_mla_q_norm_g, m_mla_w_uq, m_mla_kv_norm_g, m_mla_w_ukv, m_mla_w_out, m_final_g, v_norm_g, v_pool_w_in, v_pool_w_grp, v_pool_scale, v_pool_w_out, v_gdn_w_in, v_gdn_conv, v_gdn_a_log, v_gdn_dt_bias, v_gdn_norm_g, v_gdn_w_out, v_mla_w_in, v_mla_q_norm_g, v_mla_w_uq, v_mla_kv_norm_g, v_mla_w_ukv, v_mla_w_out, v_final_g):
    args = locals()
    w = {n: args[n] for n in NAMES}
    m = {n: args["m_" + n] for n in NAMES}
    v = {n: args["v_" + n] for n in NAMES}
    my_chip = (2 * lax.axis_index("x") + lax.axis_index("y")).astype(I32)
    S_ = x.shape[1]
    x0, pos, target = x[0], positions.reshape(S_, 1).astype(F32), loss_target[0]
    rep = {n: w[n] for n in REPLICATED}

    shard = {(n, l): w[n][l:l + 1].astype(BF16) for n in BIG for l in range(w[n].shape[0])}
    small_shapes = [w[n].shape for n in SMALL_SHARDED]
    shard[("small", 0)] = _pack([w[n] for n in SMALL_SHARDED], F32, 8)[None]
    layout = dict(BIG_LAYOUT, small=(None, None, (4,) + shard[("small", 0)].shape[1:]))

    def gather_start(group, after, tag):
        pieces = [gather_piece(i, 0, i, layout[n][0], layout[n][1]) for i, (n, l) in enumerate(group)]
        shapes = [_sds(layout[n][2], shard[(n, l)].dtype) for n, l in group]
        sems, ins, lands, token = exchange_start(pieces, [shard[k] for k in group], shapes, after, tag + "_start")
        return (pieces, sems, ins, lands), token

    def finish(handle, after, tag):
        return exchange_wait(*handle, after, tag + "_wait")

    def gathered(group, handle, after, tag):
        srcs, lands = finish(handle, after, tag)
        return {(n, l): place_own(a, s[0], layout[n][0], layout[n][1], my_chip) for (n, l), s, a in zip(group, srcs, lands)}

    group_a = [("small", 0), ("pool_w_in", 0)]
    group_a2 = [("pool_w_grp", 0), ("pool_w_out", 0)]
    group_b = [("gdn_w_in", 0)]
    group_c = [("gdn_w_out", 0), ("mla_w_in", 0), ("mla_w_uq", 0), ("mla_w_ukv", 0), ("mla_w_out", 0), ("pool_w_in", 1),
               ("pool_w_grp", 1), ("pool_w_out", 1)]
    full = {}
    h_a, t_a = gather_start(group_a, x0, "gather_a")
    h_a2, t_a2 = gather_start(group_a2, t_a, "gather_a2")
    h_b, t_b = gather_start(group_b, t_a2, "gather_b")
    h_c, t_c = gather_start(group_c, t_b, "gather_c")
    full.update(gathered(group_a, h_a, t_c, "gather_a"))
    small = {n: _unshard(a, SMALL_AXIS[n]) for n, a in zip(SMALL_SHARDED, _unpack(full[("small", 0)], small_shapes))}

    def late_l0(proj):
        full.update(gathered(group_a2, h_a2, proj, "gather_a2"))
        return dict(w_grp=full[("pool_w_grp", 0)], w_out=full[("pool_w_out", 0)])

    x1, s0 = pool_fwd(x0, dict(ng=rep["norm_g"][0:1], w_in=full[("pool_w_in", 0)], scale=small["pool_scale"][0:1]), "l0", late=late_l0)
    W0 = layer_weights(full, small, rep, 0)
    full.update(gathered(group_b, h_b, x1, "gather_b"))

    def late_l1(og):
        full.update(gathered(group_c, h_c, og, "gather_c"))
        return dict(w_out=full[("gdn_w_out", 0)])

    x2, s1 = gdn_fwd(x1, layer_weights(full, small, rep, 1), "l1", late=late_l1)
    W1, W2, W3 = (layer_weights(full, small, rep, i) for i in (1, 2, 3))
    x3, s2 = mla_fwd(x2, pos, W2, "l2")
    x4, s3 = pool_fwd(x3, W3, "l3")
    loss_part, dx4, dfinal = loss_head(x4, target, final_g.reshape(1, D), "loss_head")

    def scatter_start(pieces_of, after, tag):
        keys = list(pieces_of)
        pieces = [scatter_piece(i, i, BIG_LAYOUT[n][0], BIG_LAYOUT[n][1]) for i, (n, l) in enumerate(keys)]
        shapes = [_sds((4,) + tuple(w[n].shape[1:]), BF16) for n, l in keys]
        sems, ins, lands, token = exchange_start(pieces, [pieces_of[k] for k in keys], shapes, after, tag + "_start")
        return keys, (pieces, sems, ins, lands), token

    def scattered(keys, handle, after, tag):
        srcs, lands = finish(handle, after, tag)
        return {(n, l): place_own(a, own_window(g, BIG_LAYOUT[n][0], BIG_LAYOUT[n][1], my_chip), None, None, my_chip)
                for (n, l), g, a in zip(keys, srcs, lands)}

    dx3, g3 = pool_bwd(dx4, W3, s3, "l3")
    k3, h3, t3 = scatter_start(big_grad_pieces((None, None, None, g3)), dx3, "scatter_l3")
    dx2, g2 = mla_bwd(dx3, pos, W2, s2, "l2", after=t3)
    k2, h2, t2 = scatter_start(big_grad_pieces((None, None, g2, None)), dx2, "scatter_l2")
    dx1, g1 = gdn_bwd(dx2, W1, s1, "l1", after=t2)
    k1, h1, t1 = scatter_start(big_grad_pieces((None, g1, None, None)), dx1, "scatter_l1")
    def swap_start(part, tag):
        keys = list(part)
        ins = [part[k] for k in keys]
        pieces = [whole_piece(i) for i in range(len(keys))]
        sems, ins, lands, token = exchange_start(pieces, ins, [_sds(a.shape, a.dtype) for a in ins], ins[0], tag + "_start", sibling=True)
        swaps.append((keys, (pieces, sems, ins, lands), tag))
        return token

    last, swaps = [], []

    def emit_l0(grads):
        first = not last
        now = next(iter(grads.values()))
        early = [(k3, h3, "scatter_l3"), (k2, h2, "scatter_l2")] if first else [(k1, h1, "scatter_l1")]
        landed = {}
        for keys, handle, tag in early:
            landed.update(scattered(keys, handle, now, tag))
        swapping = swap_start(landed, "swap_a" if first else "swap_b")
        tag = "scatter_l0a" if first else "scatter_l0b"
        keys, handle, token = scatter_start({("pool_" + k, 0): a for k, a in grads.items()}, swapping, tag)
        last.append((keys, handle, tag))
        return token

    dx0, g0 = pool_bwd(dx1, W0, s0, "l0", after=t1, emit=emit_l0)
    landed = {}
    for keys, handle, tag in last:
        landed.update(scattered(keys, handle, dx0, tag))
    swap_start(landed, "swap_c")
    recv, sib = {}, {}
    for keys, handle, tag in swaps:
        mine_, theirs = exchange_wait(*handle, dx0, tag + "_wait", sibling=True)
        recv.update(zip(keys, mine_))
        sib.update(zip(keys, theirs))

    sg = small_grads((g0, g1, g2, g3), dfinal)
    small_names = SMALL_SHARDED + REPLICATED
    small_buf = _pack([sg[n] for n in small_names] + [loss_part], F32, 8)
    small_sum = sum_slots(exchange_all(small_buf, "gather_small"), "sum_small")
    full_small = _unpack(small_sum, [sg[n].shape for n in small_names] + [(1, LANES)])
    loss = full_small[-1][0, 0]
    small_part = {}
    for n, a in zip(small_names, full_small[:-1]):
        if n in SMALL_AXIS:
            a = lax.dynamic_index_in_dim(_to_shards(a, SMALL_AXIS[n]), my_chip, axis=0, keepdims=False)
        small_part[n] = a

    outs = []
    for n in NAMES:
        shp = w[n].shape
        two = (int(np.prod(shp[:-1])), shp[-1]) if len(shp) > 1 else (1, shp[0])
        if n in BIG_LAYOUT:
            layers = shp[0]
            rows = lambda a: a.reshape(4, two[0] // layers, two[1])
            parts = [[rows(recv[(n, l)]) for l in range(layers)], [rows(sib[(n, l)]) for l in range(layers)]]
        else:
            parts = [[small_part[n].reshape((1,) + two)]]
        res = adamw(w[n].reshape(two), parts, m[n].reshape(two), v[n].reshape(two), "adamw_" + n)
        outs.append([r.reshape(shp) for r in res])
    return (loss, dx0[None], *[o[0] for o in outs], *[o[1] for o in outs], *[o[2] for o in outs], *[o[3] for o in outs])
```

```python
import math

import jax
import jax.numpy as jnp
import numpy as np
from jax import lax
from jax.experimental import pallas as pl
from jax.experimental.pallas import tpu as pltpu

F32 = jnp.float32
BF16 = jnp.bfloat16
I32 = jnp.int32

D = 1024
EPS = 1e-6
POOL_WIDTH = 2048
POOL_GROUP = 512
GDN_H, GDN_DK, GDN_DV, GDN_C = 8, 128, 256, 64
GDN_QK, GDN_V, GDN_CONV_CH, GDN_IN = 1024, 2048, 4096, 6160
GDN_IN_PAD = 6272
MLA_H, MLA_NOPE, MLA_ROPE, MLA_V = 16, 128, 64, 128
MLA_Q_LORA, MLA_KV_LORA, MLA_WIDTH, MLA_IN = 768, 512, 2048, 3392
MLA_IN_PAD = 4096
MLA_SCALE = (MLA_NOPE + MLA_ROPE) ** -0.5
ROPE_THETA = 10000.0
ADAM_LR, ADAM_B1, ADAM_B2, ADAM_EPS, ADAM_WD, ADAM_STEP = 0.001, 0.9, 0.999, 1e-08, 0.01, 10

VMEM_LIMIT_V7X = 56 * 1024 * 1024
LANES = 128
MESH = pl.DeviceIdType.MESH


def _pc(body, **kw):
    return pl.pallas_call(body, **kw)


def _cparams(sem):
    return pltpu.CompilerParams(dimension_semantics=sem, vmem_limit_bytes=VMEM_LIMIT_V7X)


def _tile(n, cap):
    t = (cap // LANES) * LANES
    while t >= LANES:
        if n % t == 0:
            return t
        t -= LANES
    return n


def _sds(shape, dt):
    return jax.ShapeDtypeStruct(shape, dt)


def mm(a, b, *, ta=False, tb=False, add=None, after=None, out_dtype=F32, name):
    if ta:
        K, M = a.shape
    else:
        M, K = a.shape
    if tb:
        N, K2 = b.shape
    else:
        K2, N = b.shape
    assert K == K2, (a.shape, b.shape, ta, tb)
    tm, tn, tk = _tile(M, 1024), _tile(N, 1024), _tile(K, 1024)
    nk = K // tk
    a_spec = pl.BlockSpec((tk, tm), lambda i, j, k: (k, i)) if ta else pl.BlockSpec((tm, tk), lambda i, j, k: (i, k))
    b_spec = pl.BlockSpec((tn, tk), lambda i, j, k: (j, k)) if tb else pl.BlockSpec((tk, tn), lambda i, j, k: (k, j))
    o_spec = pl.BlockSpec((tm, tn), lambda i, j, k: (i, j))
    dn = (((0 if ta else 1,), (1 if tb else 0,)), ((), ()))
    has_add = add is not None

    def body(*refs):
        a_ref, b_ref = refs[0], refs[1]
        part = lax.dot_general(a_ref[...].astype(BF16), b_ref[...].astype(BF16), dn, preferred_element_type=F32)
        if nk == 1:
            refs[-1][...] = (part + refs[2][...] if has_add else part).astype(out_dtype)
            return
        o_ref, acc = refs[-2], refs[-1]
        k = pl.program_id(2)

        @pl.when(k == 0)
        def _():
            acc[...] = part

        @pl.when(k > 0)
        def _():
            acc[...] += part

        @pl.when(k == nk - 1)
        def _():
            r = acc[...]
            if has_add:
                r = r + refs[2][...]
            o_ref[...] = r.astype(out_dtype)

    ins = [a, b] + ([add] if has_add else []) + ([after] if after is not None else [])
    specs = [a_spec, b_spec] + ([o_spec] if has_add else []) + ([pl.BlockSpec(memory_space=pl.ANY)] if after is not None else [])
    return _pc(body, grid=(M // tm, N // tn, nk), in_specs=specs, out_specs=o_spec, out_shape=_sds((M, N), out_dtype),
               scratch_shapes=[pltpu.VMEM((tm, tn), F32)] if nk > 1 else [], compiler_params=_cparams(("parallel", "parallel", "arbitrary")),
               name=name)(*ins)


def gmm(kind, a, b, *, G, name, out_dtype=F32):
    S_ = a.shape[0]
    Ka = a.shape[1] // G
    if kind == "tn":
        N = b.shape[1] // G
        tk = _tile(S_, 2048)
        nk = S_ // tk

        def body(a_ref, b_ref, o_ref, acc):
            k = pl.program_id(1)

            @pl.when(k == 0)
            def _():
                acc[...] = jnp.zeros_like(acc)

            acc[...] += lax.dot_general(a_ref[...].astype(BF16), b_ref[...].astype(BF16), (((0,), (0,)), ((), ())),
                                        preferred_element_type=F32)

            @pl.when(k == nk - 1)
            def _():
                o_ref[...] = acc[...].astype(out_dtype)

        return _pc(body, grid=(G, nk),
                   in_specs=[pl.BlockSpec((tk, Ka), lambda g, k: (k, g)), pl.BlockSpec((tk, N), lambda g, k: (k, g))],
                   out_specs=pl.BlockSpec((None, Ka, N), lambda g, k: (g, 0, 0)), out_shape=_sds((G, Ka, N), out_dtype),
                   scratch_shapes=[pltpu.VMEM((Ka, N), F32)], compiler_params=_cparams(("parallel", "arbitrary")), name=name)(a, b)
    N = b.shape[2] if kind == "nn" else b.shape[1]
    tm = _tile(S_, 4096)
    dn = (((1,), (0 if kind == "nn" else 1,)), ((), ()))

    def body(a_ref, b_ref, o_ref):
        o_ref[...] = lax.dot_general(a_ref[...].astype(BF16), b_ref[...].astype(BF16), dn, preferred_element_type=F32).astype(out_dtype)

    bshape = (None,) + tuple(b.shape[1:])
    return _pc(body, grid=(G, S_ // tm),
               in_specs=[pl.BlockSpec((tm, Ka), lambda g, i: (i, g)), pl.BlockSpec(bshape, lambda g, i: (g, 0, 0))],
               out_specs=pl.BlockSpec((tm, N), lambda g, i: (i, g)), out_shape=_sds((S_, G * N), out_dtype),
               compiler_params=_cparams(("parallel", "parallel")), name=name)(a, b)


def _rw_spec(ts, w, c, s):
    return pl.BlockSpec((ts, w), lambda j, i: (i, c + j * s))


def _rw_pspec(p, w, c, s):
    return pl.BlockSpec((p.shape[0], w), lambda j, i: (0, c + j * s))


def rowwise(f, tiles, params, outs, *, ncol=1, ts, name):
    S_ = tiles[0][0].shape[0]
    nin = len(tiles) + len(params)

    def body(*refs):
        res = f(pl.program_id(0), *[r[...].astype(F32) for r in refs[:nin]])
        for r, o in zip(refs[nin:], res):
            r[...] = o.astype(r.dtype)

    return _pc(body, grid=(ncol, S_ // ts),
               in_specs=[_rw_spec(ts, w, c, s) for (_, w, c, s) in tiles] + [_rw_pspec(*p) for p in params],
               out_specs=[_rw_spec(ts, w, 0, s) for (w, s, _) in outs],
               out_shape=[_sds((S_, w * (ncol if s else 1)), dt) for (w, s, dt) in outs],
               compiler_params=_cparams(("parallel", "parallel")), name=name)(*[t[0] for t in tiles], *[p[0] for p in params])


def rowwise_bwd(f, tiles, params, cots, *, need, adds=None, place=None, narrow=(), ncol=1, ts, name):
    S_ = tiles[0][0].shape[0]
    adds = adds or {}
    place = place or {}
    nt, npar, nc = len(tiles), len(params), len(cots)
    add_keys = sorted(adds)
    need_idx = [k for k in range(nt) if need[k]]
    into_keys = [k for k in need_idx if k in place and not isinstance(place[k][0], int)]
    n_extra = len(add_keys) + len(into_keys)

    def body(*refs):
        j, i = pl.program_id(0), pl.program_id(1)
        vals = [r[...].astype(F32) for r in refs[:nt + npar]]
        cvals = tuple(r[...].astype(F32) for r in refs[nt + npar:nt + npar + nc])
        add_refs = refs[nt + npar + nc:nt + npar + nc + len(add_keys)]
        out_refs = refs[nt + npar + nc + n_extra:]
        _, vjp = jax.vjp(lambda *v: tuple(f(j, *v)), *vals)
        grads = vjp(cvals)
        for n, k in enumerate(need_idx):
            g = grads[k]
            if k in adds:
                g = g + add_refs[add_keys.index(k)][...]
            out_refs[n][...] = g.astype(out_refs[n].dtype)
        for n in range(npar):
            ref = out_refs[len(need_idx) + n]
            first = (i == 0) if params[n][3] else jnp.logical_and(i == 0, j == 0)

            @pl.when(first)
            def _():
                ref[...] = jnp.zeros_like(ref)

            ref[...] += grads[nt + n]

    in_specs = ([_rw_spec(ts, w, c, s) for (_, w, c, s) in tiles] + [_rw_pspec(*p) for p in params]
                + [_rw_spec(ts, w, c, s) for (_, w, c, s) in cots] + [_rw_spec(ts, *adds[k][1:]) for k in add_keys]
                + [pl.BlockSpec(memory_space=pl.ANY) for _ in into_keys])
    out_specs, out_shape, aliases = [], [], {}
    for n, k in enumerate(need_idx):
        w, s = tiles[k][1], tiles[k][3]
        if k in place:
            dst, c0 = place[k]
            total = dst if isinstance(dst, int) else dst.shape[1]
            out_specs.append(_rw_spec(ts, w, c0, s))
            out_shape.append(_sds((S_, total), (BF16 if k in narrow else F32) if isinstance(dst, int) else dst.dtype))
            if k in into_keys:
                aliases[nt + npar + nc + len(add_keys) + into_keys.index(k)] = n
        else:
            out_specs.append(_rw_spec(ts, w, 0, s))
            out_shape.append(_sds((S_, w * (ncol if s else 1)), BF16 if k in narrow else F32))
    out_specs += [_rw_pspec(p[0], p[1], p[2], p[3]) for p in params]
    out_shape += [_sds(p[0].shape, F32) for p in params]
    res = _pc(body, grid=(ncol, S_ // ts), in_specs=in_specs, out_specs=out_specs, out_shape=out_shape,
              input_output_aliases=aliases, compiler_params=_cparams(("arbitrary", "arbitrary")), name=name)(
        *[t[0] for t in tiles], *[p[0] for p in params], *[c[0] for c in cots], *[adds[k][0] for k in add_keys],
        *[place[k][0] for k in into_keys])
    return list(res[:len(need_idx)]), list(res[len(need_idx):])


def _rms(x, g):
    r = lax.rsqrt(jnp.mean(x * x, axis=-1, keepdims=True) + EPS)
    return x * r * g


def _silu(x):
    return x * jax.nn.sigmoid(x)


@jax.custom_vjp
def _softplus(x):
    return jnp.maximum(x, 0.0) + jnp.log1p(jnp.exp(-jnp.abs(x)))


_softplus.defvjp(lambda x: (_softplus(x), x), lambda x, d: (d * jax.nn.sigmoid(x),))


def f_rms(j, x, g):
    return (_rms(x, g),)


def f_pool_gate(j, pg, gate, scale):
    return (pg * scale * _silu(gate),)


def f_ogate(j, o, gate):
    return (o * _silu(gate),)


def f_gdn_out(j, o, gate, g):
    return (_rms(o, g) * _silu(gate),)


def f_gdn_gates(j, ba, alog, dtb):
    lane = lax.broadcasted_iota(I32, (1, LANES), 1)
    gs, bs = [], []
    for h in range(GDN_H):
        eb = (lane == h).astype(F32)
        ea = (lane == GDN_H + h).astype(F32)
        b = jnp.sum(ba * eb, -1, keepdims=True)
        a = jnp.sum(ba * ea, -1, keepdims=True)
        al = jnp.sum(alog * eb, -1, keepdims=True)
        dt = jnp.sum(dtb * eb, -1, keepdims=True)
        g = -jnp.exp(al) * _softplus(a + dt)
        gs.append(jnp.broadcast_to(g, ba.shape))
        bs.append(jnp.broadcast_to(jax.nn.sigmoid(b), ba.shape))
    return jnp.concatenate(gs, 1), jnp.concatenate(bs, 1)


def _shift_dn(x, k):
    rows = lax.broadcasted_iota(I32, x.shape, 0)
    return jnp.where(rows < k, 0.0, pltpu.roll(x, k, 0))


def _shift_up(x, k):
    n = x.shape[0]
    rows = lax.broadcasted_iota(I32, x.shape, 0)
    return jnp.where(rows >= n - k, 0.0, pltpu.roll(x, n - k, 0))


def _pool_window(j):
    g = lax.div(j, POOL_GROUP // LANES)
    return jnp.where(g == 0, 2.0, jnp.where(g == 1, 4.0, jnp.where(g == 2, 8.0, 16.0))), g


def _pick(g, a2, a4, a8, a16):
    return jnp.where(g == 0, a2, jnp.where(g == 1, a4, jnp.where(g == 2, a8, a16)))


def pool_time_fwd(proj, name):
    S_ = proj.shape[0]

    def body(u_ref, p_ref):
        u = u_ref[...].astype(F32)
        w, g = _pool_window(pl.program_id(0))
        s2 = u + _shift_dn(u, 1)
        s4 = s2 + _shift_dn(s2, 2)
        s8 = s4 + _shift_dn(s4, 4)
        s16 = s8 + _shift_dn(s8, 8)
        t1 = (lax.broadcasted_iota(I32, u.shape, 0) + 1).astype(F32)
        p_ref[...] = (_pick(g, s2, s4, s8, s16) / jnp.minimum(t1, w) - u).astype(p_ref.dtype)

    return _pc(body, grid=(POOL_WIDTH // LANES,), in_specs=[pl.BlockSpec((S_, LANES), lambda j: (0, j))],
               out_specs=pl.BlockSpec((S_, LANES), lambda j: (0, j)), out_shape=_sds((S_, POOL_WIDTH), BF16),
               compiler_params=_cparams(("parallel",)), name=name)(proj)


def pool_time_bwd(dp, into, name):
    S_ = dp.shape[0]

    def body(dp_ref, _, du_ref):
        d = dp_ref[...].astype(F32)
        w, g = _pool_window(pl.program_id(0))
        t1 = (lax.broadcasted_iota(I32, d.shape, 0) + 1).astype(F32)
        q = d / jnp.minimum(t1, w)
        r2 = q + _shift_up(q, 1)
        r4 = r2 + _shift_up(r2, 2)
        r8 = r4 + _shift_up(r4, 4)
        r16 = r8 + _shift_up(r8, 8)
        du_ref[...] = (_pick(g, r2, r4, r8, r16) - d).astype(du_ref.dtype)

    return _pc(body, grid=(POOL_WIDTH // LANES,),
               in_specs=[pl.BlockSpec((S_, LANES), lambda j: (0, j)), pl.BlockSpec(memory_space=pl.ANY)],
               out_specs=pl.BlockSpec((S_, LANES), lambda j: (0, j)), out_shape=_sds(into.shape, into.dtype),
               input_output_aliases={1: 0}, compiler_params=_cparams(("parallel",)), name=name)(dp, into)


def _conv_post(j, a):
    n = a * lax.rsqrt(jnp.sum(a * a, axis=-1, keepdims=True) + EPS)
    nq = GDN_QK // LANES
    return jnp.where(j < nq, n * (GDN_DK ** -0.5), jnp.where(j < 2 * nq, n, a))


def _conv_taps(u):
    return [_shift_dn(u, 3), _shift_dn(u, 2), _shift_dn(u, 1), u]


def _conv_pre(taps, w):
    return w[0:1] * taps[0] + w[1:2] * taps[1] + w[2:3] * taps[2] + w[3:4] * taps[3]


def gdn_conv_fwd(proj, conv_w, name):
    S_ = proj.shape[0]

    def body(u_ref, w_ref, o_ref):
        o_ref[...] = _conv_post(pl.program_id(0), _silu(_conv_pre(_conv_taps(u_ref[...]), w_ref[...])))

    return _pc(body, grid=(GDN_CONV_CH // LANES,),
               in_specs=[pl.BlockSpec((S_, LANES), lambda j: (0, j)), pl.BlockSpec((8, LANES), lambda j: (0, j))],
               out_specs=pl.BlockSpec((S_, LANES), lambda j: (0, j)), out_shape=_sds((S_, GDN_CONV_CH), F32),
               compiler_params=_cparams(("parallel",)), name=name)(proj, conv_w)


def gdn_conv_bwd(proj, conv_w, dq, dk, dv, into, name):
    S_ = proj.shape[0]
    nq = GDN_QK // LANES

    def body(u_ref, w_ref, dq_ref, dk_ref, dv_ref, _, du_ref, dw_ref):
        j = pl.program_id(0)
        u, w = u_ref[...], w_ref[...]
        taps = _conv_taps(u)
        c = _conv_pre(taps, w)
        sig = jax.nn.sigmoid(c)
        dout = jnp.where(j < nq, dq_ref[...], jnp.where(j < 2 * nq, dk_ref[...], dv_ref[...]))
        _, vjp = jax.vjp(lambda a: _conv_post(j, a), c * sig)
        dc = vjp(dout)[0] * (sig * (1.0 + c * (1.0 - sig)))
        du = w[3:4] * dc + w[2:3] * _shift_up(dc, 1) + w[1:2] * _shift_up(dc, 2) + w[0:1] * _shift_up(dc, 3)
        du_ref[...] = du.astype(du_ref.dtype)
        rows = lax.broadcasted_iota(I32, (8, LANES), 0)
        dw = jnp.zeros((8, LANES), F32)
        for k in range(4):
            dw = dw + jnp.where(rows == k, jnp.sum(dc * taps[k], axis=0, keepdims=True), 0.0)
        dw_ref[...] = dw

    blk = lambda f: pl.BlockSpec((S_, LANES), f)
    return _pc(body, grid=(GDN_CONV_CH // LANES,),
               in_specs=[blk(lambda j: (0, j)), pl.BlockSpec((8, LANES), lambda j: (0, j)),
                         blk(lambda j: (0, jnp.minimum(j, nq - 1))), blk(lambda j: (0, jnp.clip(j - nq, 0, nq - 1))),
                         blk(lambda j: (0, jnp.clip(j - 2 * nq, 0, 2 * nq - 1))), pl.BlockSpec(memory_space=pl.ANY)],
               out_specs=[blk(lambda j: (0, j)), pl.BlockSpec((8, LANES), lambda j: (0, j))],
               out_shape=[_sds(into.shape, into.dtype), _sds((8, GDN_CONV_CH), F32)], input_output_aliases={5: 0},
               compiler_params=_cparams(("parallel",)), name=name)(proj, conv_w, dq, dk, dv, into)


_NN, _NT, _TN = ((1,), (0,)), ((1,), (1,)), ((0,), (0,))


def _split(x, n):
    parts = []
    for _ in range(n):
        h = x.astype(BF16)
        parts.append(h)
        x = x - h.astype(F32)
    return parts


def _dot(a, b, dn, mode):
    d = lambda p, q: lax.dot_general(p, q, (dn, ((), ())), preferred_element_type=F32)
    if mode == "lo":
        return d(a.astype(BF16), b.astype(BF16))
    if mode == "x3":
        (ah, al), (bh, bl) = _split(a, 2), _split(b, 2)
        return d(ah, bh) + (d(ah, bl) + d(al, bh))
    b0, b1, b2 = _split(b, 3)
    ab = a.astype(BF16)
    return d(ab, b0) + (d(ab, b1) + d(ab, b2))


def _make_dots(mode):
    @jax.custom_vjp
    def nn(a, b):
        return _dot(a, b, _NN, mode)

    @jax.custom_vjp
    def nt(a, b):
        return _dot(a, b, _NT, mode)

    @jax.custom_vjp
    def tn(a, b):
        return _dot(a, b, _TN, mode)

    nn.defvjp(lambda a, b: (nn(a, b), (a, b)), lambda r, d: (nt(d, r[1]), tn(r[0], d)))
    nt.defvjp(lambda a, b: (nt(a, b), (a, b)), lambda r, d: (nn(d, r[1]), tn(d, r[0])))
    tn.defvjp(lambda a, b: (tn(a, b), (a, b)), lambda r, d: (nt(r[1], d), nn(r[0], d)))
    return nn, nt, tn


_nn_hi, _nt_hi, _tn_hi = _make_dots("x3")
_nn_lo, _nt_lo, _tn_lo = _make_dots("lo")


@jax.custom_vjp
def _nn_const(a, b):
    return _dot(a, b, _NN, "xl")


_nn_const.defvjp(lambda a, b: (_nn_const(a, b), a), lambda a, d: (jnp.zeros_like(a), _dot(a, d, _TN, "xl")))


def _each(f, *lists):
    return [f(*xs) for xs in zip(*lists)]


@jax.custom_vjp
def _unit_inverses(xs):
    C = xs[0].shape[0]
    eye = (lax.broadcasted_iota(I32, (C, C), 0) == lax.broadcasted_iota(I32, (C, C), 1)).astype(F32)
    ainv, p = [eye + a for a in xs], xs
    for _ in range(int(math.log2(C)) - 1):
        p = _each(lambda a: _dot(a, a, _NN, "x3"), p)
        ainv = _each(lambda a, b: a + _dot(a, b, _NN, "x3"), ainv, p)
    return ainv


def _unit_inverses_bwd(ainv, d):
    left = _each(lambda a, g: _dot(a, g, _TN, "x3"), ainv, d)
    return (_each(lambda t, a: _dot(t, a, _NT, "x3"), left, ainv),)


_unit_inverses.defvjp(lambda xs: (lambda a: (a, a))(_unit_inverses(xs)), _unit_inverses_bwd)


def _gdn_chunk(q, k, v, gb, bb, state):
    C = GDN_C
    e0 = (lax.broadcasted_iota(I32, (1, LANES), 1) == 0).astype(F32)
    ri = lax.broadcasted_iota(I32, (C, C), 0)
    ci = lax.broadcasted_iota(I32, (C, C), 1)
    causal, strict = ri >= ci, ri > ci
    tri, eye, ones = causal.astype(F32), (ri == ci).astype(F32), jnp.ones((C, C), F32)
    last = lax.broadcasted_iota(I32, (C, LANES), 0) == C - 1
    g1 = _each(lambda a: jnp.sum(a * e0, -1, keepdims=True), gb)
    b1 = _each(lambda a: jnp.sum(a * e0, -1, keepdims=True), bb)
    gc_c = _each(lambda g: _nn_const(tri, jnp.broadcast_to(g, (C, C))), g1)
    gc_d = _each(lambda g: _nn_const(tri, jnp.broadcast_to(g, (C, LANES))), g1)
    gr_c = _each(lambda g: _nn_const(ones, eye * g), gc_c)
    decay = _each(lambda a, r: jnp.where(causal, jnp.exp(jnp.where(causal, a - r, 0.0)), 0.0), gc_c, gr_c)
    kb = _each(lambda a, b: a * b, k, b1)
    vb = _each(lambda a, b: a * b, v, b1)
    x = _each(lambda a, b, d: -jnp.where(strict, _nt_lo(a, b) * d, 0.0), kb, k, decay)
    ainv = _unit_inverses(x)
    u = _each(_nn_hi, ainv, vb)
    w = _each(lambda a, b, g: _nn_hi(a, b * jnp.exp(g)), ainv, kb, gc_d)
    attn = _each(lambda a, b, d: jnp.where(causal, _nt_lo(a, b) * d, 0.0), q, k, decay)
    v_new = _each(lambda a, b, s: a - _nn_lo(b, s), u, w, state)
    o = _each(lambda a, g, s, t, vn: _nn_lo(a * jnp.exp(g), s) + _nn_lo(t, vn), q, gc_d, state, attn, v_new)
    gl = _each(lambda g: jnp.sum(jnp.where(last, g, 0.0), axis=0, keepdims=True), gc_d)
    new_state = _each(lambda s, g, a, gd, vn: s * jnp.exp(jnp.sum(g * e0, -1, keepdims=True)) + _tn_lo(a * jnp.exp(g - gd), vn),
                      state, gl, k, gc_d, v_new)
    return o, new_state


def _head_slices(ref, width):
    return [ref[:, h * width:(h + 1) * width] for h in range(GDN_H)]


def gdn_chunk_fwd(qkv, g_b, beta_b, name):
    S_ = qkv.shape[0]
    N = S_ // GDN_C

    def body(q_ref, k_ref, v_ref, g_ref, b_ref, o_ref, s_ref, state):
        @pl.when(pl.program_id(0) == 0)
        def _():
            state[...] = jnp.zeros_like(state)

        st = [state[h] for h in range(GDN_H)]
        s_ref[0] = state[...]
        o, st2 = _gdn_chunk(_head_slices(q_ref, GDN_DK), _head_slices(k_ref, GDN_DK), _head_slices(v_ref, GDN_DV),
                            _head_slices(g_ref, GDN_DK), _head_slices(b_ref, GDN_DK), st)
        for h in range(GDN_H):
            o_ref[:, h * GDN_DV:(h + 1) * GDN_DV] = o[h]
            state[h] = st2[h]

    return _pc(body, grid=(N,),
               in_specs=[pl.BlockSpec((GDN_C, GDN_QK), lambda n: (n, 0)), pl.BlockSpec((GDN_C, GDN_QK), lambda n: (n, 1)),
                         pl.BlockSpec((GDN_C, GDN_V), lambda n: (n, 1)), pl.BlockSpec((GDN_C, GDN_QK), lambda n: (n, 0)),
                         pl.BlockSpec((GDN_C, GDN_QK), lambda n: (n, 0))],
               out_specs=[pl.BlockSpec((GDN_C, GDN_V), lambda n: (n, 0)),
                          pl.BlockSpec((1, GDN_H, GDN_DK, GDN_DV), lambda n: (n, 0, 0, 0))],
               out_shape=[_sds((S_, GDN_V), F32), _sds((N, GDN_H, GDN_DK, GDN_DV), F32)],
               scratch_shapes=[pltpu.VMEM((GDN_H, GDN_DK, GDN_DV), F32)],
               compiler_params=_cparams(("arbitrary",)), name=name)(qkv, qkv, qkv, g_b, beta_b)


def gdn_chunk_bwd(qkv, g_b, beta_b, states, do, name):
    S_ = qkv.shape[0]
    N = S_ // GDN_C

    def body(q_ref, k_ref, v_ref, g_ref, b_ref, s_ref, do_ref, dq_ref, dk_ref, dv_ref, dg_ref, db_ref, dstate):
        @pl.when(pl.program_id(0) == 0)
        def _():
            dstate[...] = jnp.zeros_like(dstate)

        _, vjp = jax.vjp(_gdn_chunk, _head_slices(q_ref, GDN_DK), _head_slices(k_ref, GDN_DK), _head_slices(v_ref, GDN_DV),
                         _head_slices(g_ref, GDN_DK), _head_slices(b_ref, GDN_DK), [s_ref[0, h] for h in range(GDN_H)])
        dq, dk, dv, dg, db, ds = vjp((_head_slices(do_ref, GDN_DV), [dstate[h] for h in range(GDN_H)]))
        for h in range(GDN_H):
            kk, vv = slice(h * GDN_DK, (h + 1) * GDN_DK), slice(h * GDN_DV, (h + 1) * GDN_DV)
            dq_ref[:, kk] = dq[h]
            dk_ref[:, kk] = dk[h]
            dv_ref[:, vv] = dv[h]
            dg_ref[:, kk] = dg[h]
            db_ref[:, kk] = db[h]
            dstate[h] = ds[h]

    r = lambda n: N - 1 - n
    qk = lambda c: pl.BlockSpec((GDN_C, GDN_QK), lambda n: (r(n), c))
    vs = lambda c: pl.BlockSpec((GDN_C, GDN_V), lambda n: (r(n), c))
    return _pc(body, grid=(N,),
               in_specs=[qk(0), qk(1), vs(1), qk(0), qk(0),
                         pl.BlockSpec((1, GDN_H, GDN_DK, GDN_DV), lambda n: (r(n), 0, 0, 0)), vs(0)],
               out_specs=[qk(0), qk(0), vs(0), qk(0), qk(0)],
               out_shape=[_sds((S_, GDN_QK), F32), _sds((S_, GDN_QK), F32), _sds((S_, GDN_V), F32),
                          _sds((S_, GDN_QK), F32), _sds((S_, GDN_QK), F32)],
               scratch_shapes=[pltpu.VMEM((GDN_H, GDN_DK, GDN_DV), F32)],
               compiler_params=_cparams(("arbitrary",)), name=name)(qkv, qkv, qkv, g_b, beta_b, states, do)


def _rope_tables(pos_ref, inv_ref, cm_ref, sg_ref):
    ang = pos_ref[...] * inv_ref[...]
    return jnp.cos(ang) * cm_ref[...], jnp.sin(ang) * sg_ref[...]


def mla_prep_fwd(qpad, kv, proj, pos, rope_consts, name):
    S_ = qpad.shape[0]
    ts = 512
    W = 2 * LANES

    def body(q_ref, kv_ref, kr_ref, pos_ref, inv_ref, cm_ref, sg_ref, qh_ref, kh_ref, vh_ref):
        cs, sn = _rope_tables(pos_ref, inv_ref, cm_ref, sg_ref)
        rope = lambda r: r * cs + pltpu.roll(r, LANES // 2, 1) * sn
        krr = rope(kr_ref[...].astype(F32)).astype(BF16)
        for h in range(MLA_H):
            qh_ref[h, :, 0:LANES] = (q_ref[:, h * W:h * W + LANES].astype(F32) * MLA_SCALE).astype(BF16)
            qh_ref[h, :, LANES:W] = (rope(q_ref[:, h * W + LANES:(h + 1) * W].astype(F32)) * MLA_SCALE).astype(BF16)
            kh_ref[h, :, 0:LANES] = kv_ref[:, h * W:h * W + LANES].astype(BF16)
            kh_ref[h, :, LANES:W] = krr
            vh_ref[h] = kv_ref[:, h * W + LANES:(h + 1) * W].astype(BF16)

    one = pl.BlockSpec((1, LANES), lambda i: (0, 0))
    return _pc(body, grid=(S_ // ts,),
               in_specs=[pl.BlockSpec((ts, MLA_H * W), lambda i: (i, 0)), pl.BlockSpec((ts, MLA_H * W), lambda i: (i, 0)),
                         pl.BlockSpec((ts, LANES), lambda i: (i, 1536 // LANES)), pl.BlockSpec((ts, 1), lambda i: (i, 0)),
                         one, one, one],
               out_specs=[pl.BlockSpec((MLA_H, ts, W), lambda i: (0, i, 0)), pl.BlockSpec((MLA_H, ts, W), lambda i: (0, i, 0)),
                          pl.BlockSpec((MLA_H, ts, LANES), lambda i: (0, i, 0))],
               out_shape=[_sds((MLA_H, S_, W), BF16), _sds((MLA_H, S_, W), BF16), _sds((MLA_H, S_, LANES), BF16)],
               compiler_params=_cparams(("parallel",)), name=name)(qpad, kv, proj, pos, *rope_consts)


def mla_prep_bwd(dqh, dkh, dvh, pos, rope_consts, into, name):
    S_ = dqh.shape[1]
    ts = 512
    W = 2 * LANES

    def body(dq_ref, dk_ref, dv_ref, pos_ref, inv_ref, cm_ref, sg_ref, _, dqp_ref, dkv_ref, dkr_ref):
        cs, sn = _rope_tables(pos_ref, inv_ref, cm_ref, sg_ref)
        rope_t = lambda g: g * cs + pltpu.roll(g * sn, LANES // 2, 1)
        acc = jnp.zeros((ts, LANES), F32)
        for h in range(MLA_H):
            dqp_ref[:, h * W:h * W + LANES] = (dq_ref[h, :, 0:LANES].astype(F32) * MLA_SCALE).astype(BF16)
            dqp_ref[:, h * W + LANES:(h + 1) * W] = (rope_t(dq_ref[h, :, LANES:W].astype(F32)) * MLA_SCALE).astype(BF16)
            dkv_ref[:, h * W:h * W + LANES] = dk_ref[h, :, 0:LANES]
            dkv_ref[:, h * W + LANES:(h + 1) * W] = dv_ref[h]
            acc = acc + dk_ref[h, :, LANES:W].astype(F32)
        dkr_ref[...] = rope_t(acc).astype(dkr_ref.dtype)

    one = pl.BlockSpec((1, LANES), lambda i: (0, 0))
    return _pc(body, grid=(S_ // ts,),
               in_specs=[pl.BlockSpec((MLA_H, ts, W), lambda i: (0, i, 0)), pl.BlockSpec((MLA_H, ts, W), lambda i: (0, i, 0)),
                         pl.BlockSpec((MLA_H, ts, LANES), lambda i: (0, i, 0)), pl.BlockSpec((ts, 1), lambda i: (i, 0)),
                         one, one, one, pl.BlockSpec(memory_space=pl.ANY)],
               out_specs=[pl.BlockSpec((ts, MLA_H * W), lambda i: (i, 0)), pl.BlockSpec((ts, MLA_H * W), lambda i: (i, 0)),
                          pl.BlockSpec((ts, LANES), lambda i: (i, 1536 // LANES))],
               out_shape=[_sds((S_, MLA_H * W), BF16), _sds((S_, MLA_H * W), BF16), _sds(into.shape, into.dtype)],
               input_output_aliases={7: 2}, compiler_params=_cparams(("parallel",)), name=name)(dqh, dkh, dvh, pos, *rope_consts, into)


NEG = -1e30


FLASH_TILE = 1024
FLASH_SUB = 512


def _scores(q, k, diagonal):
    s = lax.dot_general(q, k, (_NT, ((), ())), preferred_element_type=F32)
    if not diagonal:
        return s
    return jnp.where(lax.broadcasted_iota(I32, s.shape, 1) <= lax.broadcasted_iota(I32, s.shape, 0), s, NEG)


def _sub_blocks(t, diagonal):
    sub = min(FLASH_SUB, t) if diagonal else t
    return [(c * sub if diagonal else 0, slice(c * sub, (c + 1) * sub)) for c in range(t // sub)]


FLASH_HEADS = 2


def flash_fwd(qh, kh, vh, name):
    H, S_, W = qh.shape
    t = _tile(S_, FLASH_TILE)
    n = S_ // t
    G = FLASH_HEADS
    heads = list(range(G))

    def body(q_ref, k_ref, v_ref, o_ref, lse_ref, m_s, l_s, acc):
        qi, kj = pl.program_id(1), pl.program_id(2)

        @pl.when(kj == 0)
        def _():
            m_s[...] = jnp.full_like(m_s, NEG)
            l_s[...] = jnp.zeros_like(l_s)
            acc[...] = jnp.zeros_like(acc)

        def step(diagonal):
            s = _each(lambda a: _scores(q_ref[a], k_ref[a], diagonal), heads)
            m_old = _each(lambda a: m_s[a], heads)
            m_new = _each(lambda mo, sa: jnp.maximum(mo, jnp.max(sa, axis=-1, keepdims=True)), m_old, s)
            alpha = _each(lambda mo, mn: jnp.exp(mo - mn), m_old, m_new)
            p = _each(lambda sa, mn: jnp.exp(sa - mn[:, :1]), s, m_new)
            pv = _each(lambda pa, a: lax.dot_general(pa.astype(BF16), v_ref[a], (_NN, ((), ())), preferred_element_type=F32), p, heads)
            for a in heads:
                l_s[a] = alpha[a] * l_s[a] + jnp.sum(p[a], axis=-1, keepdims=True)
                acc[a] = alpha[a] * acc[a] + pv[a]
                m_s[a] = m_new[a]

        pl.when(kj < qi)(lambda: step(False))
        pl.when(kj == qi)(lambda: step(True))

        @pl.when(kj == n - 1)
        def _():
            for a in heads:
                o_ref[:, a * LANES:(a + 1) * LANES] = (acc[a] / l_s[a]).astype(o_ref.dtype)
                lse_ref[a] = m_s[a] + jnp.log(l_s[a])

    return _pc(body, grid=(H // G, n, n),
               in_specs=[pl.BlockSpec((G, t, W), lambda h, i, j: (h, i, 0)),
                         pl.BlockSpec((G, t, W), lambda h, i, j: (h, jnp.minimum(i, j), 0)),
                         pl.BlockSpec((G, t, LANES), lambda h, i, j: (h, jnp.minimum(i, j), 0))],
               out_specs=[pl.BlockSpec((t, G * LANES), lambda h, i, j: (i, h)), pl.BlockSpec((G, t, LANES), lambda h, i, j: (h, i, 0))],
               out_shape=[_sds((S_, H * LANES), BF16), _sds((H, S_, LANES), F32)],
               scratch_shapes=[pltpu.VMEM((G, t, LANES), F32)] * 3,
               compiler_params=_cparams(("parallel", "parallel", "arbitrary")), name=name)(qh, kh, vh)


def flash_bwd(qh, kh, vh, o, lse, do, name):
    H, S_, W = qh.shape
    t = _tile(S_, FLASH_TILE)
    n = S_ // t
    G = FLASH_HEADS
    heads = list(range(G))

    def body(q_ref, k_ref, v_ref, o_ref, lse_ref, do_ref, dq_ref, dk_ref, dv_ref, dq_acc, dk_acc, dv_acc):
        kj, qi = pl.program_id(1), pl.program_id(2)

        @pl.when(jnp.logical_and(kj == 0, qi == 0))
        def _():
            dq_acc[...] = jnp.zeros_like(dq_acc)

        @pl.when(qi == 0)
        def _():
            dk_acc[...] = jnp.zeros_like(dk_acc)
            dv_acc[...] = jnp.zeros_like(dv_acc)

        def step(diagonal):
            lanes = lambda a: slice(a * LANES, (a + 1) * LANES)
            do_ = _each(lambda a: do_ref[:, lanes(a)], heads)
            dob = _each(lambda d: d.astype(BF16), do_)
            delta = _each(lambda d, a: jnp.sum(d.astype(F32) * o_ref[:, lanes(a)].astype(F32), axis=-1, keepdims=True), do_, heads)
            for r0, keys in _sub_blocks(t, diagonal):
                p = _each(lambda a: jnp.exp(_scores(q_ref[a, r0:, :], k_ref[a, keys, :], diagonal) - lse_ref[a, r0:, :1]), heads)
                dp = _each(lambda d, a: lax.dot_general(d[r0:], v_ref[a, keys, :], (_NT, ((), ())), preferred_element_type=F32), dob, heads)
                ds = _each(lambda pa, dpa, de: (pa * (dpa - de[r0:])).astype(BF16), p, dp, delta)
                rows = pl.ds(pl.multiple_of(qi * t, t) + r0, t - r0)
                for a in heads:
                    dv_acc[a, keys, :] += lax.dot_general(p[a].astype(BF16), dob[a][r0:], (_TN, ((), ())), preferred_element_type=F32)
                    dk_acc[a, keys, :] += lax.dot_general(ds[a], q_ref[a, r0:, :], (_TN, ((), ())), preferred_element_type=F32)
                    dq_acc[a, rows, :] += lax.dot_general(ds[a], k_ref[a, keys, :], (_NN, ((), ())), preferred_element_type=F32)

        pl.when(qi > kj)(lambda: step(False))
        pl.when(qi == kj)(lambda: step(True))

        @pl.when(qi == n - 1)
        def _():
            dk_ref[...] = dk_acc[...].astype(BF16)
            dv_ref[...] = dv_acc[...].astype(BF16)

        @pl.when(jnp.logical_and(kj == n - 1, qi == n - 1))
        def _():
            dq_ref[...] = dq_acc[...].astype(BF16)

    qrow = lambda h, j, i: jnp.maximum(i, j)
    return _pc(body, grid=(H // G, n, n),
               in_specs=[pl.BlockSpec((G, t, W), lambda h, j, i: (h, qrow(h, j, i), 0)),
                         pl.BlockSpec((G, t, W), lambda h, j, i: (h, j, 0)),
                         pl.BlockSpec((G, t, LANES), lambda h, j, i: (h, j, 0)),
                         pl.BlockSpec((t, G * LANES), lambda h, j, i: (qrow(h, j, i), h)),
                         pl.BlockSpec((G, t, LANES), lambda h, j, i: (h, qrow(h, j, i), 0)),
                         pl.BlockSpec((t, G * LANES), lambda h, j, i: (qrow(h, j, i), h))],
               out_specs=[pl.BlockSpec((G, S_, W), lambda h, j, i: (h, 0, 0)),
                          pl.BlockSpec((G, t, W), lambda h, j, i: (h, j, 0)),
                          pl.BlockSpec((G, t, LANES), lambda h, j, i: (h, j, 0))],
               out_shape=[_sds((H, S_, W), BF16), _sds((H, S_, W), BF16), _sds((H, S_, LANES), BF16)],
               scratch_shapes=[pltpu.VMEM((G, S_, W), F32), pltpu.VMEM((G, t, W), F32), pltpu.VMEM((G, t, LANES), F32)],
               compiler_params=_cparams(("parallel", "arbitrary", "arbitrary")), name=name)(qh, kh, vh, o, lse, do)


def loss_head(x, target, g, name):
    S_ = x.shape[0]
    ts = 512

    def body(x_ref, t_ref, g_ref, l_ref, dx_ref, dg_ref):
        @pl.when(pl.program_id(0) == 0)
        def _():
            l_ref[...] = jnp.zeros_like(l_ref)
            dg_ref[...] = jnp.zeros_like(dg_ref)

        y, vjp = jax.vjp(_rms, x_ref[...], g_ref[...])
        err = y - t_ref[...]
        l_ref[...] += 0.5 * jnp.sum(jnp.sum(err * err, axis=-1, keepdims=True), axis=0, keepdims=True) / D
        dx, dg = vjp(err / D)
        dx_ref[...] = dx
        dg_ref[...] += dg

    row = pl.BlockSpec((ts, D), lambda i: (i, 0))
    return _pc(body, grid=(S_ // ts,), in_specs=[row, row, pl.BlockSpec((1, D), lambda i: (0, 0))],
               out_specs=[pl.BlockSpec((1, LANES), lambda i: (0, 0)), row, pl.BlockSpec((1, D), lambda i: (0, 0))],
               out_shape=[_sds((1, LANES), F32), _sds((S_, D), F32), _sds((1, D), F32)],
               compiler_params=_cparams(("arbitrary",)), name=name)(x, target, g)


def adamw(w, parts, m, v, name):
    R, C = w.shape
    rows = [p.shape[1] for p in parts[0]]
    tr = R
    for cand in (512, 256, 128, 64, 32, 16, 8):
        if all(r % cand == 0 for r in rows) and cand * C * 4 * len(rows) <= 2 * 1024 * 1024:
            tr = cand
            break
    c1 = 1.0 - ADAM_B1 ** ADAM_STEP
    c2 = 1.0 - ADAM_B2 ** ADAM_STEP
    starts = [sum(rows[:k]) // tr for k in range(len(rows))]
    flat = [p for part in parts for p in part]

    def body(*refs):
        w_ref, m_ref, v_ref = refs[0], refs[1 + len(flat)], refs[2 + len(flat)]
        g_ref, d_ref, nm_ref, nv_ref = refs[3 + len(flat):]
        i = pl.program_id(0)
        gg, at = None, 1
        for part in parts:
            val = None
            for k in range(len(part)):
                p_ref = refs[at]
                at += 1
                s = p_ref[0].astype(F32)
                for n in range(1, p_ref.shape[0]):
                    s = s + p_ref[n].astype(F32)
                val = s if val is None else jnp.where(i >= starts[k], s, val)
            gg = val if gg is None else gg + val
        m2 = ADAM_B1 * m_ref[...] + (1.0 - ADAM_B1) * gg
        v2 = ADAM_B2 * v_ref[...] + (1.0 - ADAM_B2) * (gg * gg)
        g_ref[...] = gg
        d_ref[...] = -ADAM_LR * ((m2 / c1) / (jnp.sqrt(v2 / c2) + ADAM_EPS) + ADAM_WD * w_ref[...])
        nm_ref[...] = m2
        nv_ref[...] = v2

    blk = pl.BlockSpec((tr, C), lambda i: (i, 0))
    piece = lambda p, k: pl.BlockSpec((p.shape[0], tr, C), lambda i: (0, jnp.clip(i - starts[k], 0, rows[k] // tr - 1), 0))
    pblk = [piece(p, k) for part in parts for k, p in enumerate(part)]
    return _pc(body, grid=(R // tr,), in_specs=[blk] + pblk + [blk, blk], out_specs=[blk] * 4, out_shape=[_sds((R, C), F32)] * 4,
               compiler_params=_cparams(("parallel",)), name=name)(w, *flat, m, v)


def sum_slots(recv, name):
    n, R, C = recv.shape

    def body(r_ref, o_ref):
        acc = r_ref[0]
        for s in range(1, n):
            acc = acc + r_ref[s]
        o_ref[...] = acc

    return _pc(body, grid=(1,), in_specs=[pl.BlockSpec((n, R, C), lambda i: (0, 0, 0))],
               out_specs=pl.BlockSpec((R, C), lambda i: (0, 0)), out_shape=_sds((R, C), F32),
               compiler_params=_cparams(("arbitrary",)), name=name)(recv)


def _chip_peers():
    x, y, c = lax.axis_index("x"), lax.axis_index("y"), lax.axis_index("c")
    return (x, y, c), [(1 - x, y, c), (x, 1 - y, c), (1 - x, 1 - y, c)]


def _chip_index(p):
    return 2 * p[0] + p[1]


def _win(ref, axis, chip, size):
    if axis is None:
        return ref.at[chip]
    idx = [slice(None)] * len(ref.shape)
    idx[axis] = pl.ds(pl.multiple_of(chip * size, size), size)
    return ref.at[tuple(idx)]


def _remote(src, dst, send_sem, recv_sem, peer):
    return pltpu.make_async_remote_copy(src_ref=src, dst_ref=dst, send_sem=send_sem, recv_sem=recv_sem, device_id=peer,
                                        device_id_type=MESH)


HBM_SPEC = pl.BlockSpec(memory_space=pltpu.HBM)
SEM_SPEC = pl.BlockSpec(memory_space=pltpu.SEMAPHORE)
ANY_SPEC = pl.BlockSpec(memory_space=pl.ANY)
DATAFLOW = pltpu.SideEffectType.DATAFLOW_SIDE_EFFECTING


def gather_piece(i, l, o, axis, size):
    return (i, lambda r, chip: r.at[l], o, lambda r, chip: _win(r, axis, chip, size))


def scatter_piece(i, o, axis, size):
    return (i, lambda r, chip: _win(r, axis, chip, size), o, lambda r, chip: r.at[chip])


def whole_piece(i):
    return (i, lambda r, chip: r, i, lambda r, chip: r)


def _copies(pieces, in_refs, out_refs, send, recv, sibling):
    me, peers = _chip_peers()
    if sibling:
        peers = [(me[0], me[1], 1 - me[2])]
    mine = _chip_index(me)
    remote = []
    for n, (i, src, o, dst) in enumerate(pieces):
        d = dst(out_refs[o], mine)
        remote += [_remote(src(in_refs[i], _chip_index(p)), d, send.at[len(peers) * n + k], recv.at[len(peers) * n + k], p)
                   for k, p in enumerate(peers)]
    return remote


def own_window(a, axis, size, chip):
    if axis is None:
        return lax.dynamic_index_in_dim(a, chip, 0, keepdims=False)
    return lax.dynamic_slice_in_dim(a, chip * size, size, axis=axis)


def place_own(land, own, axis, size, chip):
    if axis is None:
        return lax.dynamic_update_slice_in_dim(land, own[None], chip, axis=0)
    return lax.dynamic_update_slice_in_dim(land, own, chip * size, axis=axis)


def exchange_start(pieces, ins, out_shapes, after, name, sibling=False):
    n_in, n_out, ncp = len(ins), len(out_shapes), len(pieces)

    def body(*refs):
        in_refs, land_refs = refs[:n_in], refs[n_in:n_in + n_out]
        send, recv = refs[n_in + n_out + 1], refs[n_in + n_out + 2]
        token = refs[-1]
        for cp in _copies(pieces, in_refs, land_refs, send, recv, sibling):
            cp.start()
        token[...] = jnp.zeros_like(token)

    hbm = lambda a: pltpu.with_memory_space_constraint(a, pltpu.HBM)
    lands = [hbm(lax.empty(s.shape, s.dtype)) for s in out_shapes]
    sem = pltpu.SemaphoreType.DMA(((1 if sibling else 3) * ncp,))
    thru = [pltpu.HBM(a.shape, a.dtype) for a in ins] + [pltpu.HBM(s.shape, s.dtype) for s in out_shapes]
    res = _pc(body, in_specs=[HBM_SPEC] * (n_in + n_out) + [ANY_SPEC],
              out_specs=[SEM_SPEC, SEM_SPEC] + [HBM_SPEC] * (n_in + n_out) + [pl.BlockSpec(memory_space=pltpu.VMEM)],
              out_shape=[sem, sem] + thru + [_sds((8, LANES), F32)],
              input_output_aliases={i: 2 + i for i in range(n_in + n_out)},
              compiler_params=pltpu.CompilerParams(has_side_effects=DATAFLOW), name=name)(*[hbm(a) for a in ins], *lands, after)
    return (res[0], res[1]), list(res[2:2 + n_in]), list(res[2 + n_in:2 + n_in + n_out]), res[-1]


def exchange_wait(pieces, sems, ins, lands, after, name, sibling=False):
    n_in, n_out = len(ins), len(lands)

    def body(*refs):
        in_refs, land_refs = refs[:n_in], refs[n_in:n_in + n_out]
        send, recv = refs[n_in + n_out], refs[n_in + n_out + 1]
        for cp in _copies(pieces, in_refs, land_refs, send, recv, sibling):
            cp.wait_send()
            cp.wait_recv()

    thru = [pltpu.HBM(a.shape, a.dtype) for a in ins] + [pltpu.HBM(a.shape, a.dtype) for a in lands]
    res = _pc(body, in_specs=[HBM_SPEC] * (n_in + n_out) + [SEM_SPEC, SEM_SPEC, ANY_SPEC], out_specs=[HBM_SPEC] * (n_in + n_out),
              out_shape=thru, input_output_aliases={i: i for i in range(n_in + n_out)},
              compiler_params=pltpu.CompilerParams(has_side_effects=DATAFLOW), name=name)(*ins, *lands, sems[0], sems[1], after)
    return list(res[:n_in]), list(res[n_in:])


def exchange_all(buf, name):
    def body(in_ref, out_ref, send, recv, local):
        x, y, c = lax.axis_index("x"), lax.axis_index("y"), lax.axis_index("c")
        mine = 4 * x + 2 * y + c
        loc = pltpu.make_async_copy(in_ref, out_ref.at[mine], local)
        loc.start()
        copies = [loc]
        for k in range(1, 8):
            peer = (x ^ (k >> 2), y ^ ((k >> 1) & 1), c ^ (k & 1))
            cp = pltpu.make_async_remote_copy(src_ref=in_ref, dst_ref=out_ref.at[mine], send_sem=send.at[k - 1],
                                              recv_sem=recv.at[k - 1], device_id=peer, device_id_type=MESH)
            cp.start()
            copies.append(cp)
        for cp in copies:
            cp.wait()

    anyspec = pl.BlockSpec(memory_space=pl.ANY)
    return _pc(body, in_specs=[anyspec], out_specs=anyspec, out_shape=_sds((8,) + buf.shape, buf.dtype),
               scratch_shapes=[pltpu.SemaphoreType.DMA((7,)), pltpu.SemaphoreType.DMA((7,)), pltpu.SemaphoreType.DMA],
               name=name)(buf)


def _norm_fwd(x, g, name):
    return rowwise(f_rms, [(x, D, 0, 0)], [(g, D, 0, 0)], [(D, 0, BF16)], ts=1024, name=name)[0]


def _norm_bwd(x, g, dh, dres, name):
    (dx,), (dg,) = rowwise_bwd(f_rms, [(x, D, 0, 0)], [(g, D, 0, 0)], [(dh, D, 0, 0)], need=[True],
                               adds={0: (dres, D, 0, 0)}, ts=1024, name=name)
    return dx, dg


def pool_fwd(x, W, tag, late=None):
    h = _norm_fwd(x, W["ng"], tag + "_norm")
    proj = mm(h, W["w_in"], out_dtype=BF16, name=tag + "_in")
    if late is not None:
        W = dict(W, **late(proj))
    p = pool_time_fwd(proj, tag + "_win")
    pg = gmm("nn", p, W["w_grp"], G=4, out_dtype=BF16, name=tag + "_grp")
    y = rowwise(f_pool_gate, [(pg, POOL_GROUP, 0, 1), (proj, POOL_GROUP, 4, 1)], [(W["scale"], POOL_GROUP, 0, 1)],
                [(POOL_GROUP, 1, BF16)], ncol=4, ts=1024, name=tag + "_gate")[0]
    xn = mm(y, W["w_out"], add=x, name=tag + "_out")
    return xn, (x, h, proj, p, pg, y)


def pool_bwd(dxn, W, saved, tag, after=None, emit=None):
    x, h, proj, p, pg, y = saved
    emit = emit or (lambda grads: None)
    dy = mm(dxn, W["w_out"], tb=True, after=after, out_dtype=BF16, name=tag + "_dy")
    g = {}
    (dpg, dproj), (g["scale"],) = rowwise_bwd(
        f_pool_gate, [(pg, POOL_GROUP, 0, 1), (proj, POOL_GROUP, 4, 1)], [(W["scale"], POOL_GROUP, 0, 1)],
        [(dy, POOL_GROUP, 0, 1)], need=[True, True], place={1: (2 * POOL_WIDTH, 4)}, narrow=(0, 1), ncol=4, ts=1024, name=tag + "_dgate")
    dp = gmm("nt", dpg, W["w_grp"], G=4, out_dtype=BF16, name=tag + "_dp")
    dproj = pool_time_bwd(dp, dproj, tag + "_dwin")
    g["w_in"] = mm(h, dproj, ta=True, out_dtype=BF16, name=tag + "_dw_in")
    t1 = emit({"w_in": g["w_in"]})
    g["w_out"] = mm(y, dxn, ta=True, after=t1, out_dtype=BF16, name=tag + "_dwout")
    g["w_grp"] = gmm("tn", p, dpg, G=4, out_dtype=BF16, name=tag + "_dwgrp")
    t2 = emit({"w_out": g["w_out"], "w_grp": g["w_grp"]})
    dh = mm(dproj, W["w_in"], tb=True, after=t2, name=tag + "_dh")
    dx, g["ng"] = _norm_bwd(x, W["ng"], dh, dxn, tag + "_dnorm")
    return dx, g


def gdn_fwd(x, W, tag, late=None):
    h = _norm_fwd(x, W["ng"], tag + "_norm")
    proj = mm(h, W["w_in"], name=tag + "_in")
    qkv = gdn_conv_fwd(proj, W["conv"], tag + "_conv")
    g_b, beta_b = rowwise(f_gdn_gates, [(proj, LANES, 6144 // LANES, 0)], [(W["a_log"], LANES, 0, 0), (W["dt_bias"], LANES, 0, 0)],
                          [(GDN_QK, 0, F32), (GDN_QK, 0, F32)], ts=1024, name=tag + "_gates")
    o, states = gdn_chunk_fwd(qkv, g_b, beta_b, tag + "_chunk")
    og = rowwise(f_gdn_out, [(o, GDN_DV, 0, 1), (proj, GDN_DV, 4096 // GDN_DV, 1)], [(W["norm_g"], GDN_DV, 0, 0)],
                 [(GDN_DV, 1, BF16)], ncol=GDN_H, ts=2048, name=tag + "_onorm")[0]
    if late is not None:
        W = dict(W, **late(og))
    xn = mm(og, W["w_out"], add=x, name=tag + "_out")
    return xn, (x, h, proj, qkv, g_b, beta_b, o, states, og)


def gdn_bwd(dxn, W, saved, tag, after=None):
    x, h, proj, qkv, g_b, beta_b, o, states, og = saved
    dog = mm(dxn, W["w_out"], tb=True, after=after, name=tag + "_dog")
    g = {"w_out": mm(og, dxn, ta=True, out_dtype=BF16, name=tag + "_dwout")}
    (do, dproj), (g["norm_g"],) = rowwise_bwd(
        f_gdn_out, [(o, GDN_DV, 0, 1), (proj, GDN_DV, 4096 // GDN_DV, 1)], [(W["norm_g"], GDN_DV, 0, 0)],
        [(dog, GDN_DV, 0, 1)], need=[True, True], place={1: (GDN_IN_PAD, 4096 // GDN_DV)}, narrow=(1,), ncol=GDN_H, ts=2048, name=tag + "_donorm")
    dq, dk, dv, dg_b, dbeta_b = gdn_chunk_bwd(qkv, g_b, beta_b, states, do, tag + "_dchunk")
    (dproj,), (g["a_log"], g["dt_bias"]) = rowwise_bwd(
        f_gdn_gates, [(proj, LANES, 6144 // LANES, 0)], [(W["a_log"], LANES, 0, 0), (W["dt_bias"], LANES, 0, 0)],
        [(dg_b, GDN_QK, 0, 0), (dbeta_b, GDN_QK, 0, 0)], need=[True], place={0: (dproj, 6144 // LANES)}, ts=1024, name=tag + "_dgates")
    dproj, g["conv"] = gdn_conv_bwd(proj, W["conv"], dq, dk, dv, dproj, tag + "_dconv")
    dh = mm(dproj, W["w_in"], tb=True, name=tag + "_dh")
    g["w_in"] = mm(h, dproj, ta=True, out_dtype=BF16, name=tag + "_dw_in")
    dx, g["ng"] = _norm_bwd(x, W["ng"], dh, dxn, tag + "_dnorm")
    return dx, g


def mla_fwd(x, pos, W, tag):
    h = _norm_fwd(x, W["ng"], tag + "_norm")
    proj = mm(h, W["w_in"], out_dtype=BF16, name=tag + "_in")
    hq = rowwise(f_rms, [(proj, MLA_Q_LORA, 0, 0)], [(W["q_g"], MLA_Q_LORA, 0, 0)], [(MLA_Q_LORA, 0, BF16)], ts=1024, name=tag + "_qnorm")[0]
    hkv = rowwise(f_rms, [(proj, MLA_KV_LORA, 2, 0)], [(W["kv_g"], MLA_KV_LORA, 0, 0)], [(MLA_KV_LORA, 0, BF16)], ts=1024, name=tag + "_kvnorm")[0]
    qpad = mm(hq, W["w_uq"], out_dtype=BF16, name=tag + "_uq")
    kv = mm(hkv, W["w_ukv"], out_dtype=BF16, name=tag + "_ukv")
    qh, kh, vh = mla_prep_fwd(qpad, kv, proj, pos, W["rope"], tag + "_prep")
    o, lse = flash_fwd(qh, kh, vh, tag + "_attn")
    og = rowwise(f_ogate, [(o, 512, 0, 1), (proj, 512, 4, 1)], [], [(512, 1, BF16)], ncol=4, ts=1024, name=tag + "_ogate")[0]
    xn = mm(og, W["w_out"], add=x, name=tag + "_out")
    return xn, (x, h, proj, hq, hkv, qh, kh, vh, o, lse, og)


def mla_bwd(dxn, pos, W, saved, tag, after=None):
    x, h, proj, hq, hkv, qh, kh, vh, o, lse, og = saved
    dog = mm(dxn, W["w_out"], tb=True, after=after, out_dtype=BF16, name=tag + "_dog")
    g = {"w_out": mm(og, dxn, ta=True, out_dtype=BF16, name=tag + "_dwout")}
    dproj = jnp.zeros(proj.shape, BF16)
    (do, dproj), _ = rowwise_bwd(f_ogate, [(o, 512, 0, 1), (proj, 512, 4, 1)], [], [(dog, 512, 0, 1)], need=[True, True],
                                 place={1: (dproj, 4)}, narrow=(0,), ncol=4, ts=1024, name=tag + "_dogate")
    dqh, dkh, dvh = flash_bwd(qh, kh, vh, o, lse, do, tag + "_dattn")
    dqpad, dkv, dproj = mla_prep_bwd(dqh, dkh, dvh, pos, W["rope"], dproj, tag + "_dprep")
    dhq = mm(dqpad, W["w_uq"], tb=True, name=tag + "_dhq")
    g["w_uq"] = mm(hq, dqpad, ta=True, out_dtype=BF16, name=tag + "_dwuq")
    dhkv = mm(dkv, W["w_ukv"], tb=True, name=tag + "_dhkv")
    g["w_ukv"] = mm(hkv, dkv, ta=True, out_dtype=BF16, name=tag + "_dwukv")
    (dproj,), (g["q_g"],) = rowwise_bwd(f_rms, [(proj, MLA_Q_LORA, 0, 0)], [(W["q_g"], MLA_Q_LORA, 0, 0)], [(dhq, MLA_Q_LORA, 0, 0)],
                                        need=[True], place={0: (dproj, 0)}, ts=512, name=tag + "_dqnorm")
    (dproj,), (g["kv_g"],) = rowwise_bwd(f_rms, [(proj, MLA_KV_LORA, 2, 0)], [(W["kv_g"], MLA_KV_LORA, 0, 0)], [(dhkv, MLA_KV_LORA, 0, 0)],
                                         need=[True], place={0: (dproj, 2)}, ts=512, name=tag + "_dkvnorm")
    dh = mm(dproj, W["w_in"], tb=True, name=tag + "_dh")
    g["w_in"] = mm(h, dproj, ta=True, out_dtype=BF16, name=tag + "_dw_in")
    dx, g["ng"] = _norm_bwd(x, W["ng"], dh, dxn, tag + "_dnorm")
    return dx, g


def _pad_cols(a, n):
    return jnp.pad(a, ((0, 0), (0, n - a.shape[1])))


def _mla_w_in_layout(w):
    z = lambda n: jnp.zeros((w.shape[0], n), w.dtype)
    kr = w[:, 1280:1344]
    return jnp.concatenate([w[:, :768], z(256), w[:, 768:1280], kr[:, :32], z(32), kr[:, 32:], z(32), z(384), w[:, 1344:]], axis=1)


def _mla_w_in_unlayout(g):
    return jnp.concatenate([g[:, :768], g[:, 1024:1536], g[:, 1536:1568], g[:, 1600:1632], g[:, 2048:]], axis=1)


def _mla_w_uq_layout(w):
    w3 = w.reshape(w.shape[0], MLA_H, MLA_NOPE + MLA_ROPE)
    z = jnp.zeros((w.shape[0], MLA_H, 32), w.dtype)
    return jnp.concatenate([w3[..., :128], w3[..., 128:160], z, w3[..., 160:192], z], axis=-1).reshape(w.shape[0], MLA_H * 256)


def _mla_w_uq_unlayout(g):
    g3 = g.reshape(g.shape[0], MLA_H, 256)
    return jnp.concatenate([g3[..., :128], g3[..., 128:160], g3[..., 192:224]], axis=-1).reshape(g.shape[0], MLA_H * 192)


def _rope_consts():
    half = MLA_ROPE // 2
    inv = ROPE_THETA ** (-jnp.arange(half, dtype=F32) / half)
    z = jnp.zeros((half,), F32)
    o = jnp.ones((half,), F32)
    row = lambda *p: jnp.concatenate(p).reshape(1, LANES)
    return row(inv, z, inv, z), row(o, z, o, z), row(-o, z, o, z)


BIG = ["pool_w_in", "pool_w_grp", "pool_w_out", "gdn_w_in", "gdn_w_out", "mla_w_in", "mla_w_uq", "mla_w_ukv", "mla_w_out"]
BIG_LAYOUT = {"pool_w_in": (1, 1024, (1024, 4096)), "pool_w_grp": (1, 128, (4, 512, 512)), "pool_w_out": (0, 512, (2048, 1024)),
              "gdn_w_in": (None, None, (4, 1024, 1540)), "gdn_w_out": (0, 512, (2048, 1024)),
              "mla_w_in": (None, None, (4, 1024, 848)), "mla_w_uq": (1, 768, (768, 3072)), "mla_w_ukv": (1, 1024, (512, 4096)),
              "mla_w_out": (0, 512, (2048, 1024))}
SMALL_SHARDED = ["pool_scale", "gdn_conv", "mla_q_norm_g", "mla_kv_norm_g"]
SMALL_AXIS = {"pool_scale": 1, "gdn_conv": 2, "mla_q_norm_g": 1, "mla_kv_norm_g": 1}
REPLICATED = ["norm_g", "gdn_a_log", "gdn_dt_bias", "gdn_norm_g", "final_g"]
PACK_C = 1024


def _pack(parts, dtype, row_mult):
    flat = jnp.concatenate([p.reshape(-1).astype(dtype) for p in parts])
    rows = -(-flat.shape[0] // PACK_C)
    rows = -(-rows // row_mult) * row_mult
    return jnp.pad(flat, (0, rows * PACK_C - flat.shape[0])).reshape(rows, PACK_C)


def _unpack(buf, shapes):
    lead = buf.shape[:-2]
    flat = buf.reshape(lead + (-1,))
    out, off = [], 0
    for s in shapes:
        n = int(np.prod(s))
        out.append(flat[..., off:off + n].reshape(lead + tuple(s)))
        off += n
    return out


def _unshard(g4, axis):
    a = jnp.moveaxis(g4, 0, axis)
    s = a.shape
    return a.reshape(s[:axis] + (s[axis] * s[axis + 1],) + s[axis + 2:])


def _to_shards(a, axis):
    s = a.shape
    return jnp.moveaxis(a.reshape(s[:axis] + (4, s[axis] // 4) + s[axis + 1:]), axis, 0)


def layer_weights(full, small, rep, layer):
    ng = rep["norm_g"][layer:layer + 1]
    side_by_side = lambda a4: jnp.moveaxis(a4, 0, 1).reshape(a4.shape[1], 4 * a4.shape[2])
    if layer in (0, 3):
        j = layer // 3
        return dict(ng=ng, w_in=full[("pool_w_in", j)], w_grp=full[("pool_w_grp", j)], scale=small["pool_scale"][j:j + 1],
                    w_out=full[("pool_w_out", j)])
    if layer == 1:
        return dict(ng=ng, w_in=_pad_cols(side_by_side(full[("gdn_w_in", 0)]), GDN_IN_PAD),
                    conv=jnp.pad(small["gdn_conv"][0], ((0, 4), (0, 0))), a_log=_pad_cols(rep["gdn_a_log"], LANES),
                    dt_bias=_pad_cols(rep["gdn_dt_bias"], LANES), norm_g=rep["gdn_norm_g"], w_out=full.get(("gdn_w_out", 0)))
    return dict(ng=ng, w_in=_mla_w_in_layout(side_by_side(full[("mla_w_in", 0)])), q_g=small["mla_q_norm_g"],
                kv_g=small["mla_kv_norm_g"], w_uq=_mla_w_uq_layout(full[("mla_w_uq", 0)]), w_ukv=full[("mla_w_ukv", 0)],
                w_out=full[("mla_w_out", 0)], rope=_rope_consts())


def big_grad_pieces(gl):
    g0, g1, g2, g3 = gl
    slots = lambda a: jnp.moveaxis(a.reshape(a.shape[0], 4, a.shape[1] // 4), 1, 0)
    out = {}
    for l, g in ((0, g0), (1, g3)):
        if g is not None:
            out.update({("pool_w_in", l): g["w_in"], ("pool_w_grp", l): g["w_grp"], ("pool_w_out", l): g["w_out"]})
    if g1 is not None:
        out.update({("gdn_w_in", 0): slots(g1["w_in"][:, :GDN_IN]), ("gdn_w_out", 0): g1["w_out"]})
    if g2 is not None:
        out.update({("mla_w_in", 0): slots(_mla_w_in_unlayout(g2["w_in"])), ("mla_w_uq", 0): _mla_w_uq_unlayout(g2["w_uq"]),
                    ("mla_w_ukv", 0): g2["w_ukv"], ("mla_w_out", 0): g2["w_out"]})
    return out


def small_grads(gl, dfinal):
    g0, g1, g2, g3 = gl
    return {"norm_g": jnp.concatenate([g0["ng"], g1["ng"], g2["ng"], g3["ng"]], axis=0),
            "pool_scale": jnp.concatenate([g0["scale"], g3["scale"]], axis=0), "gdn_conv": g1["conv"][None, :4],
            "gdn_a_log": g1["a_log"][:, :GDN_H], "gdn_dt_bias": g1["dt_bias"][:, :GDN_H], "gdn_norm_g": g1["norm_g"],
            "mla_q_norm_g": g2["q_g"], "mla_kv_norm_g": g2["kv_g"], "final_g": dfinal.reshape(D)}


NAMES = ["norm_g", "pool_w_in", "pool_w_grp", "pool_scale", "pool_w_out", "gdn_w_in", "gdn_conv", "gdn_a_log", "gdn_dt_bias",
         "gdn_norm_g", "gdn_w_out", "mla_w_in", "mla_q_norm_g", "mla_w_uq", "mla_kv_norm_g", "mla_w_ukv", "mla_w_out", "final_g"]


def kernel(x, positions, norm_g, pool_w_in, pool_w_grp, pool_scale, pool_w_out, gdn_w_in, gdn_conv, gdn_a_log, gdn_dt_bias, gdn_norm_g, gdn_w_out, mla_w_in, mla_q_norm_g, mla_w_uq, mla_kv_norm_g, mla_w_ukv, mla_w_out, final_g, loss_target, m_norm_g, m_pool_w_in, m_pool_w_grp, m_pool_scale, m_pool_w_out, m_gdn_w_in, m_gdn_conv, m_gdn_a_log, m_gdn_dt_bias, m_gdn_norm_g, m_gdn_w_out, m_mla_w_in, m_mla_q_norm_g, m_mla_w_uq, m_mla_kv_norm_g, m_mla_w_ukv, m_mla_w_out, m_final_g, v_norm_g, v_pool_w_in, v_pool_w_grp, v_pool_scale, v_pool_w_out, v_gdn_w_in, v_gdn_conv, v_gdn_a_log, v_gdn_dt_bias, v_gdn_norm_g, v_gdn_w_out, v_mla_w_in, v_mla_q_norm_g, v_mla_w_uq, v_mla_kv_norm_g, v_mla_w_ukv, v_mla_w_out, v_final_g):
    args = locals()
    w = {n: args[n] for n in NAMES}
    m = {n: args["m_" + n] for n in NAMES}
    v = {n: args["v_" + n] for n in NAMES}
    my_chip = (2 * lax.axis_index("x") + lax.axis_index("y")).astype(I32)
    S_ = x.shape[1]
    x0, pos, target = x[0], positions.reshape(S_, 1).astype(F32), loss_target[0]
    rep = {n: w[n] for n in REPLICATED}

    small_shapes = [w[n].shape for n in SMALL_SHARDED]
    small_pack = _pack([w[n] for n in SMALL_SHARDED], F32, 8)[None]
    layout = dict(BIG_LAYOUT, small=(None, None, (4,) + small_pack.shape[1:]))

    def shard(key, token):
        n, l = key
        if n == "small":
            return small_pack
        a = w[n][l:l + 1]
        return (a if token is None else a + token[0, 0]).astype(BF16)

    def gather_start(group, after, tag, token=None):
        pieces = [gather_piece(i, 0, i, layout[n][0], layout[n][1]) for i, (n, l) in enumerate(group)]
        ins = [shard(k, token) for k in group]
        shapes = [_sds(layout[n][2], a.dtype) for (n, l), a in zip(group, ins)]
        sems, ins, lands, token = exchange_start(pieces, ins, shapes, after, tag + "_start")
        return (pieces, sems, ins, lands), token

    def finish(handle, after, tag):
        return exchange_wait(*handle, after, tag + "_wait")

    def gathered(group, handle, after, tag):
        srcs, lands = finish(handle, after, tag)
        return {(n, l): place_own(a, s[0], layout[n][0], layout[n][1], my_chip) for (n, l), s, a in zip(group, srcs, lands)}

    group_a = [("small", 0), ("pool_w_in", 0)]
    group_a2 = [("pool_w_grp", 0), ("pool_w_out", 0)]
    group_b = [("gdn_w_in", 0)]
    group_c = [("gdn_w_out", 0), ("mla_w_in", 0), ("mla_w_uq", 0), ("mla_w_ukv", 0), ("mla_w_out", 0), ("pool_w_in", 1),
               ("pool_w_grp", 1), ("pool_w_out", 1)]
    full = {}
    h_a, t_a = gather_start(group_a, x0, "gather_a")
    h_a2, t_a2 = gather_start(group_a2, t_a, "gather_a2", t_a)
    h_b, t_b = gather_start(group_b, t_a2, "gather_b", t_a2)
    h_c, t_c = gather_start(group_c, t_b, "gather_c", t_b)
    full.update(gathered(group_a, h_a, t_c, "gather_a"))
    small = {n: _unshard(a, SMALL_AXIS[n]) for n, a in zip(SMALL_SHARDED, _unpack(full[("small", 0)], small_shapes))}

    def late_l0(proj):
        full.update(gathered(group_a2, h_a2, proj, "gather_a2"))
        return dict(w_grp=full[("pool_w_grp", 0)], w_out=full[("pool_w_out", 0)])

    first = dict(ng=rep["norm_g"][0:1] + t_c[0:1, 0:1], w_in=full[("pool_w_in", 0)], scale=small["pool_scale"][0:1])
    x1, s0 = pool_fwd(x0, first, "l0", late=late_l0)
    W0 = layer_weights(full, small, rep, 0)
    full.update(gathered(group_b, h_b, x1, "gather_b"))

    def late_l1(og):
        full.update(gathered(group_c, h_c, og, "gather_c"))
        return dict(w_out=full[("gdn_w_out", 0)])

    x2, s1 = gdn_fwd(x1, layer_weights(full, small, rep, 1), "l1", late=late_l1)
    W1, W2, W3 = (layer_weights(full, small, rep, i) for i in (1, 2, 3))
    x3, s2 = mla_fwd(x2, pos, W2, "l2")
    x4, s3 = pool_fwd(x3, W3, "l3")
    loss_part, dx4, dfinal = loss_head(x4, target, final_g.reshape(1, D), "loss_head")

    def scatter_start(pieces_of, after, tag):
        keys = list(pieces_of)
        pieces = [scatter_piece(i, i, BIG_LAYOUT[n][0], BIG_LAYOUT[n][1]) for i, (n, l) in enumerate(keys)]
        shapes = [_sds((4,) + tuple(w[n].shape[1:]), BF16) for n, l in keys]
        sems, ins, lands, token = exchange_start(pieces, [pieces_of[k] for k in keys], shapes, after, tag + "_start")
        return keys, (pieces, sems, ins, lands), token

    def scattered(keys, handle, after, tag):
        srcs, lands = finish(handle, after, tag)
        return {(n, l): place_own(a, own_window(g, BIG_LAYOUT[n][0], BIG_LAYOUT[n][1], my_chip), None, None, my_chip)
                for (n, l), g, a in zip(keys, srcs, lands)}

    dx3, g3 = pool_bwd(dx4, W3, s3, "l3")
    k3, h3, t3 = scatter_start(big_grad_pieces((None, None, None, g3)), dx3, "scatter_l3")
    dx2, g2 = mla_bwd(dx3, pos, W2, s2, "l2", after=t3)
    k2, h2, t2 = scatter_start(big_grad_pieces((None, None, g2, None)), dx2, "scatter_l2")
    dx1, g1 = gdn_bwd(dx2, W1, s1, "l1", after=t2)
    k1, h1, t1 = scatter_start(big_grad_pieces((None, g1, None, None)), dx1, "scatter_l1")
    def swap_start(part, tag):
        keys = list(part)
        ins = [part[k] for k in keys]
        pieces = [whole_piece(i) for i in range(len(keys))]
        sems, ins, lands, token = exchange_start(pieces, ins, [_sds(a.shape, a.dtype) for a in ins], ins[0], tag + "_start", sibling=True)
        swaps.append((keys, (pieces, sems, ins, lands), tag))
        return token

    last, swaps = [], []

    def emit_l0(grads):
        first = not last
        now = next(iter(grads.values()))
        early = [(k3, h3, "scatter_l3"), (k2, h2, "scatter_l2")] if first else [(k1, h1, "scatter_l1")]
        landed = {}
        for keys, handle, tag in early:
            landed.update(scattered(keys, handle, now, tag))
        swapping = swap_start(landed, "swap_a" if first else "swap_b")
        tag = "scatter_l0a" if first else "scatter_l0b"
        keys, handle, token = scatter_start({("pool_" + k, 0): a for k, a in grads.items()}, swapping, tag)
        last.append((keys, handle, tag))
        return token

    dx0, g0 = pool_bwd(dx1, W0, s0, "l0", after=t1, emit=emit_l0)
    landed = {}
    for keys, handle, tag in last:
        landed.update(scattered(keys, handle, dx0, tag))
    swap_start(landed, "swap_c")
    recv, sib = {}, {}
    for keys, handle, tag in swaps:
        mine_, theirs = exchange_wait(*handle, dx0, tag + "_wait", sibling=True)
        recv.update(zip(keys, mine_))
        sib.update(zip(keys, theirs))

    sg = small_grads((g0, g1, g2, g3), dfinal)
    small_names = SMALL_SHARDED + REPLICATED
    small_buf = _pack([sg[n] for n in small_names] + [loss_part], F32, 8)
    small_sum = sum_slots(exchange_all(small_buf, "gather_small"), "sum_small")
    full_small = _unpack(small_sum, [sg[n].shape for n in small_names] + [(1, LANES)])
    loss = full_small[-1][0, 0]
    small_part = {}
    for n, a in zip(small_names, full_small[:-1]):
        if n in SMALL_AXIS:
            a = lax.dynamic_index_in_dim(_to_shards(a, SMALL_AXIS[n]), my_chip, axis=0, keepdims=False)
        small_part[n] = a

    outs = []
    for n in NAMES:
        shp = w[n].shape
        two = (int(np.prod(shp[:-1])), shp[-1]) if len(shp) > 1 else (1, shp[0])
        if n in BIG_LAYOUT:
            layers = shp[0]
            rows = lambda a: a.reshape(4, two[0] // layers, two[1])
            parts = [[rows(recv[(n, l)]) for l in range(layers)], [rows(sib[(n, l)]) for l in range(layers)]]
        else:
            parts = [[small_part[n].reshape((1,) + two)]]
        res = adamw(w[n].reshape(two), parts, m[n].reshape(two), v[n].reshape(two), "adamw_" + n)
        outs.append([r.reshape(shp) for r in res])
    return (loss, dx0[None], *[o[0] for o in outs], *[o[1] for o in outs], *[o[2] for o in outs], *[o[3] for o in outs])
```

```python
import math

import jax
import jax.numpy as jnp
import numpy as np
from jax import lax
from jax.experimental import pallas as pl
from jax.experimental.pallas import tpu as pltpu

F32 = jnp.float32
BF16 = jnp.bfloat16
I32 = jnp.int32

D = 1024
EPS = 1e-6
POOL_WIDTH = 2048
POOL_GROUP = 512
GDN_H, GDN_DK, GDN_DV, GDN_C = 8, 128, 256, 64
GDN_QK, GDN_V, GDN_CONV_CH, GDN_IN = 1024, 2048, 4096, 6160
GDN_IN_PAD = 6272
MLA_H, MLA_NOPE, MLA_ROPE, MLA_V = 16, 128, 64, 128
MLA_Q_LORA, MLA_KV_LORA, MLA_WIDTH, MLA_IN = 768, 512, 2048, 3392
MLA_IN_PAD = 4096
MLA_SCALE = (MLA_NOPE + MLA_ROPE) ** -0.5
ROPE_THETA = 10000.0
ADAM_LR, ADAM_B1, ADAM_B2, ADAM_EPS, ADAM_WD, ADAM_STEP = 0.001, 0.9, 0.999, 1e-08, 0.01, 10

VMEM_LIMIT_V7X = 56 * 1024 * 1024
LANES = 128
MESH = pl.DeviceIdType.MESH


def _pc(body, **kw):
    return pl.pallas_call(body, **kw)


def _cparams(sem):
    return pltpu.CompilerParams(dimension_semantics=sem, vmem_limit_bytes=VMEM_LIMIT_V7X)


def _tile(n, cap):
    t = (cap // LANES) * LANES
    while t >= LANES:
        if n % t == 0:
            return t
        t -= LANES
    return n


def _sds(shape, dt):
    return jax.ShapeDtypeStruct(shape, dt)


def mm(a, b, *, ta=False, tb=False, add=None, after=None, out_dtype=F32, name):
    if ta:
        K, M = a.shape
    else:
        M, K = a.shape
    if tb:
        N, K2 = b.shape
    else:
        K2, N = b.shape
    assert K == K2, (a.shape, b.shape, ta, tb)
    tm, tn, tk = _tile(M, 1024), _tile(N, 1024), _tile(K, 1024)
    nk = K // tk
    a_spec = pl.BlockSpec((tk, tm), lambda i, j, k: (k, i)) if ta else pl.BlockSpec((tm, tk), lambda i, j, k: (i, k))
    b_spec = pl.BlockSpec((tn, tk), lambda i, j, k: (j, k)) if tb else pl.BlockSpec((tk, tn), lambda i, j, k: (k, j))
    o_spec = pl.BlockSpec((tm, tn), lambda i, j, k: (i, j))
    dn = (((0 if ta else 1,), (1 if tb else 0,)), ((), ()))
    has_add = add is not None

    def body(*refs):
        a_ref, b_ref = refs[0], refs[1]
        part = lax.dot_general(a_ref[...].astype(BF16), b_ref[...].astype(BF16), dn, preferred_element_type=F32)
        if nk == 1:
            refs[-1][...] = (part + refs[2][...] if has_add else part).astype(out_dtype)
            return
        o_ref, acc = refs[-2], refs[-1]
        k = pl.program_id(2)

        @pl.when(k == 0)
        def _():
            acc[...] = part

        @pl.when(k > 0)
        def _():
            acc[...] += part

        @pl.when(k == nk - 1)
        def _():
            r = acc[...]
            if has_add:
                r = r + refs[2][...]
            o_ref[...] = r.astype(out_dtype)

    ins = [a, b] + ([add] if has_add else []) + ([after] if after is not None else [])
    specs = [a_spec, b_spec] + ([o_spec] if has_add else []) + ([pl.BlockSpec(memory_space=pl.ANY)] if after is not None else [])
    return _pc(body, grid=(M // tm, N // tn, nk), in_specs=specs, out_specs=o_spec, out_shape=_sds((M, N), out_dtype),
               scratch_shapes=[pltpu.VMEM((tm, tn), F32)] if nk > 1 else [], compiler_params=_cparams(("parallel", "parallel", "arbitrary")),
               name=name)(*ins)


def gmm(kind, a, b, *, G, name, out_dtype=F32):
    S_ = a.shape[0]
    Ka = a.shape[1] // G
    if kind == "tn":
        N = b.shape[1] // G
        tk = _tile(S_, 2048)
        nk = S_ // tk

        def body(a_ref, b_ref, o_ref, acc):
            k = pl.program_id(1)

            @pl.when(k == 0)
            def _():
                acc[...] = jnp.zeros_like(acc)

            acc[...] += lax.dot_general(a_ref[...].astype(BF16), b_ref[...].astype(BF16), (((0,), (0,)), ((), ())),
                                        preferred_element_type=F32)

            @pl.when(k == nk - 1)
            def _():
                o_ref[...] = acc[...].astype(out_dtype)

        return _pc(body, grid=(G, nk),
                   in_specs=[pl.BlockSpec((tk, Ka), lambda g, k: (k, g)), pl.BlockSpec((tk, N), lambda g, k: (k, g))],
                   out_specs=pl.BlockSpec((None, Ka, N), lambda g, k: (g, 0, 0)), out_shape=_sds((G, Ka, N), out_dtype),
                   scratch_shapes=[pltpu.VMEM((Ka, N), F32)], compiler_params=_cparams(("parallel", "arbitrary")), name=name)(a, b)
    N = b.shape[2] if kind == "nn" else b.shape[1]
    tm = _tile(S_, 4096)
    dn = (((1,), (0 if kind == "nn" else 1,)), ((), ()))

    def body(a_ref, b_ref, o_ref):
        o_ref[...] = lax.dot_general(a_ref[...].astype(BF16), b_ref[...].astype(BF16), dn, preferred_element_type=F32).astype(out_dtype)

    bshape = (None,) + tuple(b.shape[1:])
    return _pc(body, grid=(G, S_ // tm),
               in_specs=[pl.BlockSpec((tm, Ka), lambda g, i: (i, g)), pl.BlockSpec(bshape, lambda g, i: (g, 0, 0))],
               out_specs=pl.BlockSpec((tm, N), lambda g, i: (i, g)), out_shape=_sds((S_, G * N), out_dtype),
               compiler_params=_cparams(("parallel", "parallel")), name=name)(a, b)


def _rw_spec(ts, w, c, s):
    return pl.BlockSpec((ts, w), lambda j, i: (i, c + j * s))


def _rw_pspec(p, w, c, s):
    return pl.BlockSpec((p.shape[0], w), lambda j, i: (0, c + j * s))


def rowwise(f, tiles, params, outs, *, ncol=1, ts, name):
    S_ = tiles[0][0].shape[0]
    nin = len(tiles) + len(params)

    def body(*refs):
        res = f(pl.program_id(0), *[r[...].astype(F32) for r in refs[:nin]])
        for r, o in zip(refs[nin:], res):
            r[...] = o.astype(r.dtype)

    return _pc(body, grid=(ncol, S_ // ts),
               in_specs=[_rw_spec(ts, w, c, s) for (_, w, c, s) in tiles] + [_rw_pspec(*p) for p in params],
               out_specs=[_rw_spec(ts, w, 0, s) for (w, s, _) in outs],
               out_shape=[_sds((S_, w * (ncol if s else 1)), dt) for (w, s, dt) in outs],
               compiler_params=_cparams(("parallel", "parallel")), name=name)(*[t[0] for t in tiles], *[p[0] for p in params])


def rowwise_bwd(f, tiles, params, cots, *, need, adds=None, place=None, narrow=(), ncol=1, ts, name):
    S_ = tiles[0][0].shape[0]
    adds = adds or {}
    place = place or {}
    nt, npar, nc = len(tiles), len(params), len(cots)
    add_keys = sorted(adds)
    need_idx = [k for k in range(nt) if need[k]]
    into_keys = [k for k in need_idx if k in place and not isinstance(place[k][0], int)]
    n_extra = len(add_keys) + len(into_keys)

    def body(*refs):
        j, i = pl.program_id(0), pl.program_id(1)
        vals = [r[...].astype(F32) for r in refs[:nt + npar]]
        cvals = tuple(r[...].astype(F32) for r in refs[nt + npar:nt + npar + nc])
        add_refs = refs[nt + npar + nc:nt + npar + nc + len(add_keys)]
        out_refs = refs[nt + npar + nc + n_extra:]
        _, vjp = jax.vjp(lambda *v: tuple(f(j, *v)), *vals)
        grads = vjp(cvals)
        for n, k in enumerate(need_idx):
            g = grads[k]
            if k in adds:
                g = g + add_refs[add_keys.index(k)][...]
            out_refs[n][...] = g.astype(out_refs[n].dtype)
        for n in range(npar):
            ref = out_refs[len(need_idx) + n]
            first = (i == 0) if params[n][3] else jnp.logical_and(i == 0, j == 0)

            @pl.when(first)
            def _():
                ref[...] = jnp.zeros_like(ref)

            ref[...] += grads[nt + n]

    in_specs = ([_rw_spec(ts, w, c, s) for (_, w, c, s) in tiles] + [_rw_pspec(*p) for p in params]
                + [_rw_spec(ts, w, c, s) for (_, w, c, s) in cots] + [_rw_spec(ts, *adds[k][1:]) for k in add_keys]
                + [pl.BlockSpec(memory_space=pl.ANY) for _ in into_keys])
    out_specs, out_shape, aliases = [], [], {}
    for n, k in enumerate(need_idx):
        w, s = tiles[k][1], tiles[k][3]
        if k in place:
            dst, c0 = place[k]
            total = dst if isinstance(dst, int) else dst.shape[1]
            out_specs.append(_rw_spec(ts, w, c0, s))
            out_shape.append(_sds((S_, total), (BF16 if k in narrow else F32) if isinstance(dst, int) else dst.dtype))
            if k in into_keys:
                aliases[nt + npar + nc + len(add_keys) + into_keys.index(k)] = n
        else:
            out_specs.append(_rw_spec(ts, w, 0, s))
            out_shape.append(_sds((S_, w * (ncol if s else 1)), BF16 if k in narrow else F32))
    out_specs += [_rw_pspec(p[0], p[1], p[2], p[3]) for p in params]
    out_shape += [_sds(p[0].shape, F32) for p in params]
    res = _pc(body, grid=(ncol, S_ // ts), in_specs=in_specs, out_specs=out_specs, out_shape=out_shape,
              input_output_aliases=aliases, compiler_params=_cparams(("arbitrary", "arbitrary")), name=name)(
        *[t[0] for t in tiles], *[p[0] for p in params], *[c[0] for c in cots], *[adds[k][0] for k in add_keys],
        *[place[k][0] for k in into_keys])
    return list(res[:len(need_idx)]), list(res[len(need_idx):])


def _rms(x, g):
    r = lax.rsqrt(jnp.mean(x * x, axis=-1, keepdims=True) + EPS)
    return x * r * g


def _silu(x):
    return x * jax.nn.sigmoid(x)


@jax.custom_vjp
def _softplus(x):
    return jnp.maximum(x, 0.0) + jnp.log1p(jnp.exp(-jnp.abs(x)))


_softplus.defvjp(lambda x: (_softplus(x), x), lambda x, d: (d * jax.nn.sigmoid(x),))


def f_rms(j, x, g):
    return (_rms(x, g),)


def f_pool_gate(j, pg, gate, scale):
    return (pg * scale * _silu(gate),)


def f_ogate(j, o, gate):
    return (o * _silu(gate),)


def f_gdn_out(j, o, gate, g):
    return (_rms(o, g) * _silu(gate),)


def f_gdn_gates(j, ba, alog, dtb):
    lane = lax.broadcasted_iota(I32, (1, LANES), 1)
    gs, bs = [], []
    for h in range(GDN_H):
        eb = (lane == h).astype(F32)
        ea = (lane == GDN_H + h).astype(F32)
        b = jnp.sum(ba * eb, -1, keepdims=True)
        a = jnp.sum(ba * ea, -1, keepdims=True)
        al = jnp.sum(alog * eb, -1, keepdims=True)
        dt = jnp.sum(dtb * eb, -1, keepdims=True)
        g = -jnp.exp(al) * _softplus(a + dt)
        gs.append(jnp.broadcast_to(g, ba.shape))
        bs.append(jnp.broadcast_to(jax.nn.sigmoid(b), ba.shape))
    return jnp.concatenate(gs, 1), jnp.concatenate(bs, 1)


def _shift_dn(x, k):
    rows = lax.broadcasted_iota(I32, x.shape, 0)
    return jnp.where(rows < k, 0.0, pltpu.roll(x, k, 0))


def _shift_up(x, k):
    n = x.shape[0]
    rows = lax.broadcasted_iota(I32, x.shape, 0)
    return jnp.where(rows >= n - k, 0.0, pltpu.roll(x, n - k, 0))


def _pool_window(j):
    g = lax.div(j, POOL_GROUP // LANES)
    return jnp.where(g == 0, 2.0, jnp.where(g == 1, 4.0, jnp.where(g == 2, 8.0, 16.0))), g


def _pick(g, a2, a4, a8, a16):
    return jnp.where(g == 0, a2, jnp.where(g == 1, a4, jnp.where(g == 2, a8, a16)))


def pool_time_fwd(proj, name):
    S_ = proj.shape[0]

    def body(u_ref, p_ref):
        u = u_ref[...].astype(F32)
        w, g = _pool_window(pl.program_id(0))
        s2 = u + _shift_dn(u, 1)
        s4 = s2 + _shift_dn(s2, 2)
        s8 = s4 + _shift_dn(s4, 4)
        s16 = s8 + _shift_dn(s8, 8)
        t1 = (lax.broadcasted_iota(I32, u.shape, 0) + 1).astype(F32)
        p_ref[...] = (_pick(g, s2, s4, s8, s16) / jnp.minimum(t1, w) - u).astype(p_ref.dtype)

    return _pc(body, grid=(POOL_WIDTH // LANES,), in_specs=[pl.BlockSpec((S_, LANES), lambda j: (0, j))],
               out_specs=pl.BlockSpec((S_, LANES), lambda j: (0, j)), out_shape=_sds((S_, POOL_WIDTH), BF16),
               compiler_params=_cparams(("parallel",)), name=name)(proj)


def pool_time_bwd(dp, into, name):
    S_ = dp.shape[0]

    def body(dp_ref, _, du_ref):
        d = dp_ref[...].astype(F32)
        w, g = _pool_window(pl.program_id(0))
        t1 = (lax.broadcasted_iota(I32, d.shape, 0) + 1).astype(F32)
        q = d / jnp.minimum(t1, w)
        r2 = q + _shift_up(q, 1)
        r4 = r2 + _shift_up(r2, 2)
        r8 = r4 + _shift_up(r4, 4)
        r16 = r8 + _shift_up(r8, 8)
        du_ref[...] = (_pick(g, r2, r4, r8, r16) - d).astype(du_ref.dtype)

    return _pc(body, grid=(POOL_WIDTH // LANES,),
               in_specs=[pl.BlockSpec((S_, LANES), lambda j: (0, j)), pl.BlockSpec(memory_space=pl.ANY)],
               out_specs=pl.BlockSpec((S_, LANES), lambda j: (0, j)), out_shape=_sds(into.shape, into.dtype),
               input_output_aliases={1: 0}, compiler_params=_cparams(("parallel",)), name=name)(dp, into)


def _conv_post(j, a):
    n = a * lax.rsqrt(jnp.sum(a * a, axis=-1, keepdims=True) + EPS)
    nq = GDN_QK // LANES
    return jnp.where(j < nq, n * (GDN_DK ** -0.5), jnp.where(j < 2 * nq, n, a))


def _conv_taps(u):
    return [_shift_dn(u, 3), _shift_dn(u, 2), _shift_dn(u, 1), u]


def _conv_pre(taps, w):
    return w[0:1] * taps[0] + w[1:2] * taps[1] + w[2:3] * taps[2] + w[3:4] * taps[3]


def gdn_conv_fwd(proj, conv_w, name):
    S_ = proj.shape[0]

    def body(u_ref, w_ref, o_ref):
        o_ref[...] = _conv_post(pl.program_id(0), _silu(_conv_pre(_conv_taps(u_ref[...]), w_ref[...])))

    return _pc(body, grid=(GDN_CONV_CH // LANES,),
               in_specs=[pl.BlockSpec((S_, LANES), lambda j: (0, j)), pl.BlockSpec((8, LANES), lambda j: (0, j))],
               out_specs=pl.BlockSpec((S_, LANES), lambda j: (0, j)), out_shape=_sds((S_, GDN_CONV_CH), F32),
               compiler_params=_cparams(("parallel",)), name=name)(proj, conv_w)


def gdn_conv_bwd(proj, conv_w, dq, dk, dv, into, name):
    S_ = proj.shape[0]
    nq = GDN_QK // LANES

    def body(u_ref, w_ref, dq_ref, dk_ref, dv_ref, _, du_ref, dw_ref):
        j = pl.program_id(0)
        u, w = u_ref[...], w_ref[...]
        taps = _conv_taps(u)
        c = _conv_pre(taps, w)
        sig = jax.nn.sigmoid(c)
        dout = jnp.where(j < nq, dq_ref[...], jnp.where(j < 2 * nq, dk_ref[...], dv_ref[...]))
        _, vjp = jax.vjp(lambda a: _conv_post(j, a), c * sig)
        dc = vjp(dout)[0] * (sig * (1.0 + c * (1.0 - sig)))
        du = w[3:4] * dc + w[2:3] * _shift_up(dc, 1) + w[1:2] * _shift_up(dc, 2) + w[0:1] * _shift_up(dc, 3)
        du_ref[...] = du.astype(du_ref.dtype)
        rows = lax.broadcasted_iota(I32, (8, LANES), 0)
        dw = jnp.zeros((8, LANES), F32)
        for k in range(4):
            dw = dw + jnp.where(rows == k, jnp.sum(dc * taps[k], axis=0, keepdims=True), 0.0)
        dw_ref[...] = dw

    blk = lambda f: pl.BlockSpec((S_, LANES), f)
    return _pc(body, grid=(GDN_CONV_CH // LANES,),
               in_specs=[blk(lambda j: (0, j)), pl.BlockSpec((8, LANES), lambda j: (0, j)),
                         blk(lambda j: (0, jnp.minimum(j, nq - 1))), blk(lambda j: (0, jnp.clip(j - nq, 0, nq - 1))),
                         blk(lambda j: (0, jnp.clip(j - 2 * nq, 0, 2 * nq - 1))), pl.BlockSpec(memory_space=pl.ANY)],
               out_specs=[blk(lambda j: (0, j)), pl.BlockSpec((8, LANES), lambda j: (0, j))],
               out_shape=[_sds(into.shape, into.dtype), _sds((8, GDN_CONV_CH), F32)], input_output_aliases={5: 0},
               compiler_params=_cparams(("parallel",)), name=name)(proj, conv_w, dq, dk, dv, into)


_NN, _NT, _TN = ((1,), (0,)), ((1,), (1,)), ((0,), (0,))


def _split(x, n):
    parts = []
    for _ in range(n):
        h = x.astype(BF16)
        parts.append(h)
        x = x - h.astype(F32)
    return parts


def _dot(a, b, dn, mode):
    d = lambda p, q: lax.dot_general(p, q, (dn, ((), ())), preferred_element_type=F32)
    if mode == "lo":
        return d(a.astype(BF16), b.astype(BF16))
    if mode == "x3":
        (ah, al), (bh, bl) = _split(a, 2), _split(b, 2)
        return d(ah, bh) + (d(ah, bl) + d(al, bh))
    b0, b1, b2 = _split(b, 3)
    ab = a.astype(BF16)
    return d(ab, b0) + (d(ab, b1) + d(ab, b2))


def _make_dots(mode):
    @jax.custom_vjp
    def nn(a, b):
        return _dot(a, b, _NN, mode)

    @jax.custom_vjp
    def nt(a, b):
        return _dot(a, b, _NT, mode)

    @jax.custom_vjp
    def tn(a, b):
        return _dot(a, b, _TN, mode)

    nn.defvjp(lambda a, b: (nn(a, b), (a, b)), lambda r, d: (nt(d, r[1]), tn(r[0], d)))
    nt.defvjp(lambda a, b: (nt(a, b), (a, b)), lambda r, d: (nn(d, r[1]), tn(d, r[0])))
    tn.defvjp(lambda a, b: (tn(a, b), (a, b)), lambda r, d: (nt(r[1], d), nn(r[0], d)))
    return nn, nt, tn


_nn_hi, _nt_hi, _tn_hi = _make_dots("x3")
_nn_lo, _nt_lo, _tn_lo = _make_dots("lo")


@jax.custom_vjp
def _nn_const(a, b):
    return _dot(a, b, _NN, "xl")


_nn_const.defvjp(lambda a, b: (_nn_const(a, b), a), lambda a, d: (jnp.zeros_like(a), _dot(a, d, _TN, "xl")))


def _each(f, *lists):
    return [f(*xs) for xs in zip(*lists)]


@jax.custom_vjp
def _unit_inverses(xs):
    C = xs[0].shape[0]
    eye = (lax.broadcasted_iota(I32, (C, C), 0) == lax.broadcasted_iota(I32, (C, C), 1)).astype(F32)
    ainv, p = [eye + a for a in xs], xs
    for _ in range(int(math.log2(C)) - 1):
        p = _each(lambda a: _dot(a, a, _NN, "x3"), p)
        ainv = _each(lambda a, b: a + _dot(a, b, _NN, "x3"), ainv, p)
    return ainv


def _unit_inverses_bwd(ainv, d):
    left = _each(lambda a, g: _dot(a, g, _TN, "x3"), ainv, d)
    return (_each(lambda t, a: _dot(t, a, _NT, "x3"), left, ainv),)


_unit_inverses.defvjp(lambda xs: (lambda a: (a, a))(_unit_inverses(xs)), _unit_inverses_bwd)


def _gdn_chunk(q, k, v, gb, bb, state):
    C = GDN_C
    e0 = (lax.broadcasted_iota(I32, (1, LANES), 1) == 0).astype(F32)
    ri = lax.broadcasted_iota(I32, (C, C), 0)
    ci = lax.broadcasted_iota(I32, (C, C), 1)
    causal, strict = ri >= ci, ri > ci
    tri, eye, ones = causal.astype(F32), (ri == ci).astype(F32), jnp.ones((C, C), F32)
    last = lax.broadcasted_iota(I32, (C, LANES), 0) == C - 1
    g1 = _each(lambda a: jnp.sum(a * e0, -1, keepdims=True), gb)
    b1 = _each(lambda a: jnp.sum(a * e0, -1, keepdims=True), bb)
    gc_c = _each(lambda g: _nn_const(tri, jnp.broadcast_to(g, (C, C))), g1)
    gc_d = _each(lambda g: _nn_const(tri, jnp.broadcast_to(g, (C, LANES))), g1)
    gr_c = _each(lambda g: _nn_const(ones, eye * g), gc_c)
    decay = _each(lambda a, r: jnp.where(causal, jnp.exp(jnp.where(causal, a - r, 0.0)), 0.0), gc_c, gr_c)
    kb = _each(lambda a, b: a * b, k, b1)
    vb = _each(lambda a, b: a * b, v, b1)
    x = _each(lambda a, b, d: -jnp.where(strict, _nt_lo(a, b) * d, 0.0), kb, k, decay)
    ainv = _unit_inverses(x)
    u = _each(_nn_hi, ainv, vb)
    w = _each(lambda a, b, g: _nn_hi(a, b * jnp.exp(g)), ainv, kb, gc_d)
    attn = _each(lambda a, b, d: jnp.where(causal, _nt_lo(a, b) * d, 0.0), q, k, decay)
    v_new = _each(lambda a, b, s: a - _nn_lo(b, s), u, w, state)
    o = _each(lambda a, g, s, t, vn: _nn_lo(a * jnp.exp(g), s) + _nn_lo(t, vn), q, gc_d, state, attn, v_new)
    gl = _each(lambda g: jnp.sum(jnp.where(last, g, 0.0), axis=0, keepdims=True), gc_d)
    new_state = _each(lambda s, g, a, gd, vn: s * jnp.exp(jnp.sum(g * e0, -1, keepdims=True)) + _tn_lo(a * jnp.exp(g - gd), vn),
                      state, gl, k, gc_d, v_new)
    return o, new_state


def _head_slices(ref, width):
    return [ref[:, h * width:(h + 1) * width] for h in range(GDN_H)]


def gdn_chunk_fwd(qkv, g_b, beta_b, name):
    S_ = qkv.shape[0]
    N = S_ // GDN_C

    def body(q_ref, k_ref, v_ref, g_ref, b_ref, o_ref, s_ref, state):
        @pl.when(pl.program_id(0) == 0)
        def _():
            state[...] = jnp.zeros_like(state)

        st = [state[h] for h in range(GDN_H)]
        s_ref[0] = state[...]
        o, st2 = _gdn_chunk(_head_slices(q_ref, GDN_DK), _head_slices(k_ref, GDN_DK), _head_slices(v_ref, GDN_DV),
                            _head_slices(g_ref, GDN_DK), _head_slices(b_ref, GDN_DK), st)
        for h in range(GDN_H):
            o_ref[:, h * GDN_DV:(h + 1) * GDN_DV] = o[h]
            state[h] = st2[h]

    return _pc(body, grid=(N,),
               in_specs=[pl.BlockSpec((GDN_C, GDN_QK), lambda n: (n, 0)), pl.BlockSpec((GDN_C, GDN_QK), lambda n: (n, 1)),
                         pl.BlockSpec((GDN_C, GDN_V), lambda n: (n, 1)), pl.BlockSpec((GDN_C, GDN_QK), lambda n: (n, 0)),
                         pl.BlockSpec((GDN_C, GDN_QK), lambda n: (n, 0))],
               out_specs=[pl.BlockSpec((GDN_C, GDN_V), lambda n: (n, 0)),
                          pl.BlockSpec((1, GDN_H, GDN_DK, GDN_DV), lambda n: (n, 0, 0, 0))],
               out_shape=[_sds((S_, GDN_V), F32), _sds((N, GDN_H, GDN_DK, GDN_DV), F32)],
               scratch_shapes=[pltpu.VMEM((GDN_H, GDN_DK, GDN_DV), F32)],
               compiler_params=_cparams(("arbitrary",)), name=name)(qkv, qkv, qkv, g_b, beta_b)


def gdn_chunk_bwd(qkv, g_b, beta_b, states, do, name):
    S_ = qkv.shape[0]
    N = S_ // GDN_C

    def body(q_ref, k_ref, v_ref, g_ref, b_ref, s_ref, do_ref, dq_ref, dk_ref, dv_ref, dg_ref, db_ref, dstate):
        @pl.when(pl.program_id(0) == 0)
        def _():
            dstate[...] = jnp.zeros_like(dstate)

        _, vjp = jax.vjp(_gdn_chunk, _head_slices(q_ref, GDN_DK), _head_slices(k_ref, GDN_DK), _head_slices(v_ref, GDN_DV),
                         _head_slices(g_ref, GDN_DK), _head_slices(b_ref, GDN_DK), [s_ref[0, h] for h in range(GDN_H)])
        dq, dk, dv, dg, db, ds = vjp((_head_slices(do_ref, GDN_DV), [dstate[h] for h in range(GDN_H)]))
        for h in range(GDN_H):
            kk, vv = slice(h * GDN_DK, (h + 1) * GDN_DK), slice(h * GDN_DV, (h + 1) * GDN_DV)
            dq_ref[:, kk] = dq[h]
            dk_ref[:, kk] = dk[h]
            dv_ref[:, vv] = dv[h]
            dg_ref[:, kk] = dg[h]
            db_ref[:, kk] = db[h]
            dstate[h] = ds[h]

    r = lambda n: N - 1 - n
    qk = lambda c: pl.BlockSpec((GDN_C, GDN_QK), lambda n: (r(n), c))
    vs = lambda c: pl.BlockSpec((GDN_C, GDN_V), lambda n: (r(n), c))
    return _pc(body, grid=(N,),
               in_specs=[qk(0), qk(1), vs(1), qk(0), qk(0),
                         pl.BlockSpec((1, GDN_H, GDN_DK, GDN_DV), lambda n: (r(n), 0, 0, 0)), vs(0)],
               out_specs=[qk(0), qk(0), vs(0), qk(0), qk(0)],
               out_shape=[_sds((S_, GDN_QK), F32), _sds((S_, GDN_QK), F32), _sds((S_, GDN_V), F32),
                          _sds((S_, GDN_QK), F32), _sds((S_, GDN_QK), F32)],
               scratch_shapes=[pltpu.VMEM((GDN_H, GDN_DK, GDN_DV), F32)],
               compiler_params=_cparams(("arbitrary",)), name=name)(qkv, qkv, qkv, g_b, beta_b, states, do)


def _rope_tables(pos_ref, inv_ref, cm_ref, sg_ref):
    ang = pos_ref[...] * inv_ref[...]
    return jnp.cos(ang) * cm_ref[...], jnp.sin(ang) * sg_ref[...]


def mla_prep_fwd(qpad, kv, proj, pos, rope_consts, name):
    S_ = qpad.shape[0]
    ts = 512
    W = 2 * LANES

    def body(q_ref, kv_ref, kr_ref, pos_ref, inv_ref, cm_ref, sg_ref, qh_ref, kh_ref, vh_ref):
        cs, sn = _rope_tables(pos_ref, inv_ref, cm_ref, sg_ref)
        rope = lambda r: r * cs + pltpu.roll(r, LANES // 2, 1) * sn
        krr = rope(kr_ref[...].astype(F32)).astype(BF16)
        for h in range(MLA_H):
            qh_ref[h, :, 0:LANES] = (q_ref[:, h * W:h * W + LANES].astype(F32) * MLA_SCALE).astype(BF16)
            qh_ref[h, :, LANES:W] = (rope(q_ref[:, h * W + LANES:(h + 1) * W].astype(F32)) * MLA_SCALE).astype(BF16)
            kh_ref[h, :, 0:LANES] = kv_ref[:, h * W:h * W + LANES].astype(BF16)
            kh_ref[h, :, LANES:W] = krr
            vh_ref[h] = kv_ref[:, h * W + LANES:(h + 1) * W].astype(BF16)

    one = pl.BlockSpec((1, LANES), lambda i: (0, 0))
    return _pc(body, grid=(S_ // ts,),
               in_specs=[pl.BlockSpec((ts, MLA_H * W), lambda i: (i, 0)), pl.BlockSpec((ts, MLA_H * W), lambda i: (i, 0)),
                         pl.BlockSpec((ts, LANES), lambda i: (i, 1536 // LANES)), pl.BlockSpec((ts, 1), lambda i: (i, 0)),
                         one, one, one],
               out_specs=[pl.BlockSpec((MLA_H, ts, W), lambda i: (0, i, 0)), pl.BlockSpec((MLA_H, ts, W), lambda i: (0, i, 0)),
                          pl.BlockSpec((MLA_H, ts, LANES), lambda i: (0, i, 0))],
               out_shape=[_sds((MLA_H, S_, W), BF16), _sds((MLA_H, S_, W), BF16), _sds((MLA_H, S_, LANES), BF16)],
               compiler_params=_cparams(("parallel",)), name=name)(qpad, kv, proj, pos, *rope_consts)


def mla_prep_bwd(dqh, dkh, dvh, pos, rope_consts, into, name):
    S_ = dqh.shape[1]
    ts = 512
    W = 2 * LANES

    def body(dq_ref, dk_ref, dv_ref, pos_ref, inv_ref, cm_ref, sg_ref, _, dqp_ref, dkv_ref, dkr_ref):
        cs, sn = _rope_tables(pos_ref, inv_ref, cm_ref, sg_ref)
        rope_t = lambda g: g * cs + pltpu.roll(g * sn, LANES // 2, 1)
        acc = jnp.zeros((ts, LANES), F32)
        for h in range(MLA_H):
            dqp_ref[:, h * W:h * W + LANES] = (dq_ref[h, :, 0:LANES].astype(F32) * MLA_SCALE).astype(BF16)
            dqp_ref[:, h * W + LANES:(h + 1) * W] = (rope_t(dq_ref[h, :, LANES:W].astype(F32)) * MLA_SCALE).astype(BF16)
            dkv_ref[:, h * W:h * W + LANES] = dk_ref[h, :, 0:LANES]
            dkv_ref[:, h * W + LANES:(h + 1) * W] = dv_ref[h]
            acc = acc + dk_ref[h, :, LANES:W].astype(F32)
        dkr_ref[...] = rope_t(acc).astype(dkr_ref.dtype)

    one = pl.BlockSpec((1, LANES), lambda i: (0, 0))
    return _pc(body, grid=(S_ // ts,),
               in_specs=[pl.BlockSpec((MLA_H, ts, W), lambda i: (0, i, 0)), pl.BlockSpec((MLA_H, ts, W), lambda i: (0, i, 0)),
                         pl.BlockSpec((MLA_H, ts, LANES), lambda i: (0, i, 0)), pl.BlockSpec((ts, 1), lambda i: (i, 0)),
                         one, one, one, pl.BlockSpec(memory_space=pl.ANY)],
               out_specs=[pl.BlockSpec((ts, MLA_H * W), lambda i: (i, 0)), pl.BlockSpec((ts, MLA_H * W), lambda i: (i, 0)),
                          pl.BlockSpec((ts, LANES), lambda i: (i, 1536 // LANES))],
               out_shape=[_sds((S_, MLA_H * W), BF16), _sds((S_, MLA_H * W), BF16), _sds(into.shape, into.dtype)],
               input_output_aliases={7: 2}, compiler_params=_cparams(("parallel",)), name=name)(dqh, dkh, dvh, pos, *rope_consts, into)


NEG = -1e30


FLASH_TILE = 1024
FLASH_SUB = 512


def _scores(q, k, diagonal):
    s = lax.dot_general(q, k, (_NT, ((), ())), preferred_element_type=F32)
    if not diagonal:
        return s
    return jnp.where(lax.broadcasted_iota(I32, s.shape, 1) <= lax.broadcasted_iota(I32, s.shape, 0), s, NEG)


def _sub_blocks(t, diagonal):
    sub = min(FLASH_SUB, t) if diagonal else t
    return [(c * sub if diagonal else 0, slice(c * sub, (c + 1) * sub)) for c in range(t // sub)]


FLASH_HEADS = 2


def flash_fwd(qh, kh, vh, name):
    H, S_, W = qh.shape
    t = _tile(S_, FLASH_TILE)
    n = S_ // t
    G = FLASH_HEADS
    heads = list(range(G))

    def body(q_ref, k_ref, v_ref, o_ref, lse_ref, m_s, l_s, acc):
        qi, kj = pl.program_id(1), pl.program_id(2)

        @pl.when(kj == 0)
        def _():
            m_s[...] = jnp.full_like(m_s, NEG)
            l_s[...] = jnp.zeros_like(l_s)
            acc[...] = jnp.zeros_like(acc)

        def step(diagonal):
            s = _each(lambda a: _scores(q_ref[a], k_ref[a], diagonal), heads)
            m_old = _each(lambda a: m_s[a], heads)
            m_new = _each(lambda mo, sa: jnp.maximum(mo, jnp.max(sa, axis=-1, keepdims=True)), m_old, s)
            alpha = _each(lambda mo, mn: jnp.exp(mo - mn), m_old, m_new)
            p = _each(lambda sa, mn: jnp.exp(sa - mn[:, :1]), s, m_new)
            pv = _each(lambda pa, a: lax.dot_general(pa.astype(BF16), v_ref[a], (_NN, ((), ())), preferred_element_type=F32), p, heads)
            for a in heads:
                l_s[a] = alpha[a] * l_s[a] + jnp.sum(p[a], axis=-1, keepdims=True)
                acc[a] = alpha[a] * acc[a] + pv[a]
                m_s[a] = m_new[a]

        pl.when(kj < qi)(lambda: step(False))
        pl.when(kj == qi)(lambda: step(True))

        @pl.when(kj == n - 1)
        def _():
            for a in heads:
                o_ref[:, a * LANES:(a + 1) * LANES] = (acc[a] / l_s[a]).astype(o_ref.dtype)
                lse_ref[a] = m_s[a] + jnp.log(l_s[a])

    return _pc(body, grid=(H // G, n, n),
               in_specs=[pl.BlockSpec((G, t, W), lambda h, i, j: (h, i, 0)),
                         pl.BlockSpec((G, t, W), lambda h, i, j: (h, jnp.minimum(i, j), 0)),
                         pl.BlockSpec((G, t, LANES), lambda h, i, j: (h, jnp.minimum(i, j), 0))],
               out_specs=[pl.BlockSpec((t, G * LANES), lambda h, i, j: (i, h)), pl.BlockSpec((G, t, LANES), lambda h, i, j: (h, i, 0))],
               out_shape=[_sds((S_, H * LANES), BF16), _sds((H, S_, LANES), F32)],
               scratch_shapes=[pltpu.VMEM((G, t, LANES), F32)] * 3,
               compiler_params=_cparams(("parallel", "parallel", "arbitrary")), name=name)(qh, kh, vh)


def flash_bwd(qh, kh, vh, o, lse, do, name):
    H, S_, W = qh.shape
    t = _tile(S_, FLASH_TILE)
    n = S_ // t
    G = FLASH_HEADS
    heads = list(range(G))

    def body(q_ref, k_ref, v_ref, o_ref, lse_ref, do_ref, dq_ref, dk_ref, dv_ref, dq_acc, dk_acc, dv_acc):
        kj, qi = pl.program_id(1), pl.program_id(2)

        @pl.when(jnp.logical_and(kj == 0, qi == 0))
        def _():
            dq_acc[...] = jnp.zeros_like(dq_acc)

        @pl.when(qi == 0)
        def _():
            dk_acc[...] = jnp.zeros_like(dk_acc)
            dv_acc[...] = jnp.zeros_like(dv_acc)

        def step(diagonal):
            lanes = lambda a: slice(a * LANES, (a + 1) * LANES)
            do_ = _each(lambda a: do_ref[:, lanes(a)], heads)
            dob = _each(lambda d: d.astype(BF16), do_)
            delta = _each(lambda d, a: jnp.sum(d.astype(F32) * o_ref[:, lanes(a)].astype(F32), axis=-1, keepdims=True), do_, heads)
            for r0, keys in _sub_blocks(t, diagonal):
                p = _each(lambda a: jnp.exp(_scores(q_ref[a, r0:, :], k_ref[a, keys, :], diagonal) - lse_ref[a, r0:, :1]), heads)
                dp = _each(lambda d, a: lax.dot_general(d[r0:], v_ref[a, keys, :], (_NT, ((), ())), preferred_element_type=F32), dob, heads)
                ds = _each(lambda pa, dpa, de: (pa * (dpa - de[r0:])).astype(BF16), p, dp, delta)
                rows = pl.ds(pl.multiple_of(qi * t, t) + r0, t - r0)
                for a in heads:
                    dv_acc[a, keys, :] += lax.dot_general(p[a].astype(BF16), dob[a][r0:], (_TN, ((), ())), preferred_element_type=F32)
                    dk_acc[a, keys, :] += lax.dot_general(ds[a], q_ref[a, r0:, :], (_TN, ((), ())), preferred_element_type=F32)
                    dq_acc[a, rows, :] += lax.dot_general(ds[a], k_ref[a, keys, :], (_NN, ((), ())), preferred_element_type=F32)

        pl.when(qi > kj)(lambda: step(False))
        pl.when(qi == kj)(lambda: step(True))

        @pl.when(qi == n - 1)
        def _():
            dk_ref[...] = dk_acc[...].astype(BF16)
            dv_ref[...] = dv_acc[...].astype(BF16)

        @pl.when(jnp.logical_and(kj == n - 1, qi == n - 1))
        def _():
            dq_ref[...] = dq_acc[...].astype(BF16)

    qrow = lambda h, j, i: jnp.maximum(i, j)
    return _pc(body, grid=(H // G, n, n),
               in_specs=[pl.BlockSpec((G, t, W), lambda h, j, i: (h, qrow(h, j, i), 0)),
                         pl.BlockSpec((G, t, W), lambda h, j, i: (h, j, 0)),
                         pl.BlockSpec((G, t, LANES), lambda h, j, i: (h, j, 0)),
                         pl.BlockSpec((t, G * LANES), lambda h, j, i: (qrow(h, j, i), h)),
                         pl.BlockSpec((G, t, LANES), lambda h, j, i: (h, qrow(h, j, i), 0)),
                         pl.BlockSpec((t, G * LANES), lambda h, j, i: (qrow(h, j, i), h))],
               out_specs=[pl.BlockSpec((G, S_, W), lambda h, j, i: (h, 0, 0)),
                          pl.BlockSpec((G, t, W), lambda h, j, i: (h, j, 0)),
                          pl.BlockSpec((G, t, LANES), lambda h, j, i: (h, j, 0))],
               out_shape=[_sds((H, S_, W), BF16), _sds((H, S_, W), BF16), _sds((H, S_, LANES), BF16)],
               scratch_shapes=[pltpu.VMEM((G, S_, W), F32), pltpu.VMEM((G, t, W), F32), pltpu.VMEM((G, t, LANES), F32)],
               compiler_params=_cparams(("parallel", "arbitrary", "arbitrary")), name=name)(qh, kh, vh, o, lse, do)


def loss_head(x, target, g, name):
    S_ = x.shape[0]
    ts = 512

    def body(x_ref, t_ref, g_ref, l_ref, dx_ref, dg_ref):
        @pl.when(pl.program_id(0) == 0)
        def _():
            l_ref[...] = jnp.zeros_like(l_ref)
            dg_ref[...] = jnp.zeros_like(dg_ref)

        y, vjp = jax.vjp(_rms, x_ref[...], g_ref[...])
        err = y - t_ref[...]
        l_ref[...] += 0.5 * jnp.sum(jnp.sum(err * err, axis=-1, keepdims=True), axis=0, keepdims=True) / D
        dx, dg = vjp(err / D)
        dx_ref[...] = dx
        dg_ref[...] += dg

    row = pl.BlockSpec((ts, D), lambda i: (i, 0))
    return _pc(body, grid=(S_ // ts,), in_specs=[row, row, pl.BlockSpec((1, D), lambda i: (0, 0))],
               out_specs=[pl.BlockSpec((1, LANES), lambda i: (0, 0)), row, pl.BlockSpec((1, D), lambda i: (0, 0))],
               out_shape=[_sds((1, LANES), F32), _sds((S_, D), F32), _sds((1, D), F32)],
               compiler_params=_cparams(("arbitrary",)), name=name)(x, target, g)


def adamw(w, parts, m, v, name, after=None):
    R, C = w.shape
    rows = [p.shape[1] for p in parts[0]]
    tr = R
    for cand in (512, 256, 128, 64, 32, 16, 8):
        if all(r % cand == 0 for r in rows) and cand * C * 4 * len(rows) <= 2 * 1024 * 1024:
            tr = cand
            break
    c1 = 1.0 - ADAM_B1 ** ADAM_STEP
    c2 = 1.0 - ADAM_B2 ** ADAM_STEP
    starts = [sum(rows[:k]) // tr for k in range(len(rows))]
    flat = [p for part in parts for p in part]

    def body(*refs):
        w_ref, m_ref, v_ref = refs[0], refs[1 + len(flat)], refs[2 + len(flat)]
        g_ref, d_ref, nm_ref, nv_ref = refs[-4:]
        i = pl.program_id(0)
        gg, at = None, 1
        for part in parts:
            val = None
            for k in range(len(part)):
                p_ref = refs[at]
                at += 1
                s = p_ref[0].astype(F32)
                for n in range(1, p_ref.shape[0]):
                    s = s + p_ref[n].astype(F32)
                val = s if val is None else jnp.where(i >= starts[k], s, val)
            gg = val if gg is None else gg + val
        m2 = ADAM_B1 * m_ref[...] + (1.0 - ADAM_B1) * gg
        v2 = ADAM_B2 * v_ref[...] + (1.0 - ADAM_B2) * (gg * gg)
        g_ref[...] = gg
        d_ref[...] = -ADAM_LR * ((m2 / c1) / (jnp.sqrt(v2 / c2) + ADAM_EPS) + ADAM_WD * w_ref[...])
        nm_ref[...] = m2
        nv_ref[...] = v2

    blk = pl.BlockSpec((tr, C), lambda i: (i, 0))
    piece = lambda p, k: pl.BlockSpec((p.shape[0], tr, C), lambda i: (0, jnp.clip(i - starts[k], 0, rows[k] // tr - 1), 0))
    pblk = [piece(p, k) for part in parts for k, p in enumerate(part)]
    extra = [] if after is None else [after]
    return _pc(body, grid=(R // tr,), in_specs=[blk] + pblk + [blk, blk] + [pl.BlockSpec(memory_space=pl.ANY)] * len(extra),
               out_specs=[blk] * 4, out_shape=[_sds((R, C), F32)] * 4,
               compiler_params=_cparams(("parallel",)), name=name)(w, *flat, m, v, *extra)


def sum_slots(recv, name):
    n, R, C = recv.shape

    def body(r_ref, o_ref):
        acc = r_ref[0]
        for s in range(1, n):
            acc = acc + r_ref[s]
        o_ref[...] = acc

    return _pc(body, grid=(1,), in_specs=[pl.BlockSpec((n, R, C), lambda i: (0, 0, 0))],
               out_specs=pl.BlockSpec((R, C), lambda i: (0, 0)), out_shape=_sds((R, C), F32),
               compiler_params=_cparams(("arbitrary",)), name=name)(recv)


def _chip_peers():
    x, y, c = lax.axis_index("x"), lax.axis_index("y"), lax.axis_index("c")
    return (x, y, c), [(1 - x, y, c), (x, 1 - y, c), (1 - x, 1 - y, c)]


def _chip_index(p):
    return 2 * p[0] + p[1]


def _win(ref, axis, chip, size):
    if axis is None:
        return ref.at[chip]
    idx = [slice(None)] * len(ref.shape)
    idx[axis] = pl.ds(pl.multiple_of(chip * size, size), size)
    return ref.at[tuple(idx)]


def _remote(src, dst, send_sem, recv_sem, peer):
    return pltpu.make_async_remote_copy(src_ref=src, dst_ref=dst, send_sem=send_sem, recv_sem=recv_sem, device_id=peer,
                                        device_id_type=MESH)


HBM_SPEC = pl.BlockSpec(memory_space=pltpu.HBM)
SEM_SPEC = pl.BlockSpec(memory_space=pltpu.SEMAPHORE)
ANY_SPEC = pl.BlockSpec(memory_space=pl.ANY)
DATAFLOW = pltpu.SideEffectType.DATAFLOW_SIDE_EFFECTING


def gather_piece(i, l, o, axis, size):
    return (i, lambda r, chip: r.at[l], o, lambda r, chip: _win(r, axis, chip, size))


def scatter_piece(i, o, axis, size):
    return (i, lambda r, chip: _win(r, axis, chip, size), o, lambda r, chip: r.at[chip])


def whole_piece(i):
    return (i, lambda r, chip: r, i, lambda r, chip: r)


def _copies(pieces, in_refs, out_refs, send, recv, sibling):
    me, peers = _chip_peers()
    if sibling:
        peers = [(me[0], me[1], 1 - me[2])]
    mine = _chip_index(me)
    remote = []
    for n, (i, src, o, dst) in enumerate(pieces):
        d = dst(out_refs[o], mine)
        remote += [_remote(src(in_refs[i], _chip_index(p)), d, send.at[len(peers) * n + k], recv.at[len(peers) * n + k], p)
                   for k, p in enumerate(peers)]
    return remote


def own_window(a, axis, size, chip):
    if axis is None:
        return lax.dynamic_index_in_dim(a, chip, 0, keepdims=False)
    return lax.dynamic_slice_in_dim(a, chip * size, size, axis=axis)


def place_own(land, own, axis, size, chip):
    if axis is None:
        return lax.dynamic_update_slice_in_dim(land, own[None], chip, axis=0)
    return lax.dynamic_update_slice_in_dim(land, own, chip * size, axis=axis)


def exchange_start(pieces, ins, out_shapes, after, name, sibling=False):
    n_in, n_out, ncp = len(ins), len(out_shapes), len(pieces)

    def body(*refs):
        in_refs, land_refs = refs[:n_in], refs[n_in:n_in + n_out]
        send, recv = refs[n_in + n_out + 1], refs[n_in + n_out + 2]
        token = refs[-1]
        for cp in _copies(pieces, in_refs, land_refs, send, recv, sibling):
            cp.start()
        token[...] = jnp.zeros_like(token)

    hbm = lambda a: pltpu.with_memory_space_constraint(a, pltpu.HBM)
    lands = [hbm(lax.empty(s.shape, s.dtype)) for s in out_shapes]
    sem = pltpu.SemaphoreType.DMA(((1 if sibling else 3) * ncp,))
    thru = [pltpu.HBM(a.shape, a.dtype) for a in ins] + [pltpu.HBM(s.shape, s.dtype) for s in out_shapes]
    res = _pc(body, in_specs=[HBM_SPEC] * (n_in + n_out) + [ANY_SPEC],
              out_specs=[SEM_SPEC, SEM_SPEC] + [HBM_SPEC] * (n_in + n_out) + [pl.BlockSpec(memory_space=pltpu.VMEM)],
              out_shape=[sem, sem] + thru + [_sds((8, LANES), F32)],
              input_output_aliases={i: 2 + i for i in range(n_in + n_out)},
              compiler_params=pltpu.CompilerParams(has_side_effects=DATAFLOW), name=name)(*[hbm(a) for a in ins], *lands, after)
    return (res[0], res[1]), list(res[2:2 + n_in]), list(res[2 + n_in:2 + n_in + n_out]), res[-1]


def exchange_wait(pieces, sems, ins, lands, after, name, sibling=False):
    n_in, n_out = len(ins), len(lands)

    def body(*refs):
        in_refs, land_refs = refs[:n_in], refs[n_in:n_in + n_out]
        send, recv = refs[n_in + n_out], refs[n_in + n_out + 1]
        for cp in _copies(pieces, in_refs, land_refs, send, recv, sibling):
            cp.wait_send()
            cp.wait_recv()

    thru = [pltpu.HBM(a.shape, a.dtype) for a in ins] + [pltpu.HBM(a.shape, a.dtype) for a in lands]
    res = _pc(body, in_specs=[HBM_SPEC] * (n_in + n_out) + [SEM_SPEC, SEM_SPEC, ANY_SPEC], out_specs=[HBM_SPEC] * (n_in + n_out),
              out_shape=thru, input_output_aliases={i: i for i in range(n_in + n_out)},
              compiler_params=pltpu.CompilerParams(has_side_effects=DATAFLOW), name=name)(*ins, *lands, sems[0], sems[1], after)
    return list(res[:n_in]), list(res[n_in:])


def exchange_all(buf, name):
    def body(in_ref, out_ref, send, recv, local):
        x, y, c = lax.axis_index("x"), lax.axis_index("y"), lax.axis_index("c")
        mine = 4 * x + 2 * y + c
        loc = pltpu.make_async_copy(in_ref, out_ref.at[mine], local)
        loc.start()
        copies = [loc]
        for k in range(1, 8):
            peer = (x ^ (k >> 2), y ^ ((k >> 1) & 1), c ^ (k & 1))
            cp = pltpu.make_async_remote_copy(src_ref=in_ref, dst_ref=out_ref.at[mine], send_sem=send.at[k - 1],
                                              recv_sem=recv.at[k - 1], device_id=peer, device_id_type=MESH)
            cp.start()
            copies.append(cp)
        for cp in copies:
            cp.wait()

    anyspec = pl.BlockSpec(memory_space=pl.ANY)
    return _pc(body, in_specs=[anyspec], out_specs=anyspec, out_shape=_sds((8,) + buf.shape, buf.dtype),
               scratch_shapes=[pltpu.SemaphoreType.DMA((7,)), pltpu.SemaphoreType.DMA((7,)), pltpu.SemaphoreType.DMA],
               name=name)(buf)


def _norm_fwd(x, g, name):
    return rowwise(f_rms, [(x, D, 0, 0)], [(g, D, 0, 0)], [(D, 0, BF16)], ts=1024, name=name)[0]


def _norm_bwd(x, g, dh, dres, name):
    (dx,), (dg,) = rowwise_bwd(f_rms, [(x, D, 0, 0)], [(g, D, 0, 0)], [(dh, D, 0, 0)], need=[True],
                               adds={0: (dres, D, 0, 0)}, ts=1024, name=name)
    return dx, dg


def pool_fwd(x, W, tag, late=None):
    h = _norm_fwd(x, W["ng"], tag + "_norm")
    proj = mm(h, W["w_in"], out_dtype=BF16, name=tag + "_in")
    if late is not None:
        W = dict(W, **late(proj))
    p = pool_time_fwd(proj, tag + "_win")
    pg = gmm("nn", p, W["w_grp"], G=4, out_dtype=BF16, name=tag + "_grp")
    y = rowwise(f_pool_gate, [(pg, POOL_GROUP, 0, 1), (proj, POOL_GROUP, 4, 1)], [(W["scale"], POOL_GROUP, 0, 1)],
                [(POOL_GROUP, 1, BF16)], ncol=4, ts=1024, name=tag + "_gate")[0]
    xn = mm(y, W["w_out"], add=x, name=tag + "_out")
    return xn, (x, h, proj, p, pg, y)


def pool_bwd(dxn, W, saved, tag, after=None, emit=None):
    x, h, proj, p, pg, y = saved
    emit = emit or (lambda grads: None)
    dy = mm(dxn, W["w_out"], tb=True, after=after, out_dtype=BF16, name=tag + "_dy")
    g = {}
    (dpg, dproj), (g["scale"],) = rowwise_bwd(
        f_pool_gate, [(pg, POOL_GROUP, 0, 1), (proj, POOL_GROUP, 4, 1)], [(W["scale"], POOL_GROUP, 0, 1)],
        [(dy, POOL_GROUP, 0, 1)], need=[True, True], place={1: (2 * POOL_WIDTH, 4)}, narrow=(0, 1), ncol=4, ts=1024, name=tag + "_dgate")
    dp = gmm("nt", dpg, W["w_grp"], G=4, out_dtype=BF16, name=tag + "_dp")
    dproj = pool_time_bwd(dp, dproj, tag + "_dwin")
    g["w_in"] = mm(h, dproj, ta=True, out_dtype=BF16, name=tag + "_dw_in")
    t1 = emit({"w_in": g["w_in"]})
    g["w_out"] = mm(y, dxn, ta=True, after=t1, out_dtype=BF16, name=tag + "_dwout")
    g["w_grp"] = gmm("tn", p, dpg, G=4, out_dtype=BF16, name=tag + "_dwgrp")
    t2 = emit({"w_out": g["w_out"], "w_grp": g["w_grp"]})
    dh = mm(dproj, W["w_in"], tb=True, after=t2, name=tag + "_dh")
    dx, g["ng"] = _norm_bwd(x, W["ng"], dh, dxn, tag + "_dnorm")
    return dx, g


def gdn_fwd(x, W, tag, late=None):
    h = _norm_fwd(x, W["ng"], tag + "_norm")
    proj = mm(h, W["w_in"], name=tag + "_in")
    qkv = gdn_conv_fwd(proj, W["conv"], tag + "_conv")
    g_b, beta_b = rowwise(f_gdn_gates, [(proj, LANES, 6144 // LANES, 0)], [(W["a_log"], LANES, 0, 0), (W["dt_bias"], LANES, 0, 0)],
                          [(GDN_QK, 0, F32), (GDN_QK, 0, F32)], ts=1024, name=tag + "_gates")
    o, states = gdn_chunk_fwd(qkv, g_b, beta_b, tag + "_chunk")
    og = rowwise(f_gdn_out, [(o, GDN_DV, 0, 1), (proj, GDN_DV, 4096 // GDN_DV, 1)], [(W["norm_g"], GDN_DV, 0, 0)],
                 [(GDN_DV, 1, BF16)], ncol=GDN_H, ts=2048, name=tag + "_onorm")[0]
    if late is not None:
        W = dict(W, **late(og))
    xn = mm(og, W["w_out"], add=x, name=tag + "_out")
    return xn, (x, h, proj, qkv, g_b, beta_b, o, states, og)


def gdn_bwd(dxn, W, saved, tag, after=None):
    x, h, proj, qkv, g_b, beta_b, o, states, og = saved
    dog = mm(dxn, W["w_out"], tb=True, after=after, name=tag + "_dog")
    g = {"w_out": mm(og, dxn, ta=True, out_dtype=BF16, name=tag + "_dwout")}
    (do, dproj), (g["norm_g"],) = rowwise_bwd(
        f_gdn_out, [(o, GDN_DV, 0, 1), (proj, GDN_DV, 4096 // GDN_DV, 1)], [(W["norm_g"], GDN_DV, 0, 0)],
        [(dog, GDN_DV, 0, 1)], need=[True, True], place={1: (GDN_IN_PAD, 4096 // GDN_DV)}, narrow=(1,), ncol=GDN_H, ts=2048, name=tag + "_donorm")
    dq, dk, dv, dg_b, dbeta_b = gdn_chunk_bwd(qkv, g_b, beta_b, states, do, tag + "_dchunk")
    (dproj,), (g["a_log"], g["dt_bias"]) = rowwise_bwd(
        f_gdn_gates, [(proj, LANES, 6144 // LANES, 0)], [(W["a_log"], LANES, 0, 0), (W["dt_bias"], LANES, 0, 0)],
        [(dg_b, GDN_QK, 0, 0), (dbeta_b, GDN_QK, 0, 0)], need=[True], place={0: (dproj, 6144 // LANES)}, ts=1024, name=tag + "_dgates")
    dproj, g["conv"] = gdn_conv_bwd(proj, W["conv"], dq, dk, dv, dproj, tag + "_dconv")
    dh = mm(dproj, W["w_in"], tb=True, name=tag + "_dh")
    g["w_in"] = mm(h, dproj, ta=True, out_dtype=BF16, name=tag + "_dw_in")
    dx, g["ng"] = _norm_bwd(x, W["ng"], dh, dxn, tag + "_dnorm")
    return dx, g


def mla_fwd(x, pos, W, tag):
    h = _norm_fwd(x, W["ng"], tag + "_norm")
    proj = mm(h, W["w_in"], out_dtype=BF16, name=tag + "_in")
    hq = rowwise(f_rms, [(proj, MLA_Q_LORA, 0, 0)], [(W["q_g"], MLA_Q_LORA, 0, 0)], [(MLA_Q_LORA, 0, BF16)], ts=1024, name=tag + "_qnorm")[0]
    hkv = rowwise(f_rms, [(proj, MLA_KV_LORA, 2, 0)], [(W["kv_g"], MLA_KV_LORA, 0, 0)], [(MLA_KV_LORA, 0, BF16)], ts=1024, name=tag + "_kvnorm")[0]
    qpad = mm(hq, W["w_uq"], out_dtype=BF16, name=tag + "_uq")
    kv = mm(hkv, W["w_ukv"], out_dtype=BF16, name=tag + "_ukv")
    qh, kh, vh = mla_prep_fwd(qpad, kv, proj, pos, W["rope"], tag + "_prep")
    o, lse = flash_fwd(qh, kh, vh, tag + "_attn")
    og = rowwise(f_ogate, [(o, 512, 0, 1), (proj, 512, 4, 1)], [], [(512, 1, BF16)], ncol=4, ts=1024, name=tag + "_ogate")[0]
    xn = mm(og, W["w_out"], add=x, name=tag + "_out")
    return xn, (x, h, proj, hq, hkv, qh, kh, vh, o, lse, og)


def mla_bwd(dxn, pos, W, saved, tag, after=None):
    x, h, proj, hq, hkv, qh, kh, vh, o, lse, og = saved
    dog = mm(dxn, W["w_out"], tb=True, after=after, out_dtype=BF16, name=tag + "_dog")
    g = {"w_out": mm(og, dxn, ta=True, out_dtype=BF16, name=tag + "_dwout")}
    dproj = jnp.zeros(proj.shape, BF16)
    (do, dproj), _ = rowwise_bwd(f_ogate, [(o, 512, 0, 1), (proj, 512, 4, 1)], [], [(dog, 512, 0, 1)], need=[True, True],
                                 place={1: (dproj, 4)}, narrow=(0,), ncol=4, ts=1024, name=tag + "_dogate")
    dqh, dkh, dvh = flash_bwd(qh, kh, vh, o, lse, do, tag + "_dattn")
    dqpad, dkv, dproj = mla_prep_bwd(dqh, dkh, dvh, pos, W["rope"], dproj, tag + "_dprep")
    dhq = mm(dqpad, W["w_uq"], tb=True, name=tag + "_dhq")
    g["w_uq"] = mm(hq, dqpad, ta=True, out_dtype=BF16, name=tag + "_dwuq")
    dhkv = mm(dkv, W["w_ukv"], tb=True, name=tag + "_dhkv")
    g["w_ukv"] = mm(hkv, dkv, ta=True, out_dtype=BF16, name=tag + "_dwukv")
    (dproj,), (g["q_g"],) = rowwise_bwd(f_rms, [(proj, MLA_Q_LORA, 0, 0)], [(W["q_g"], MLA_Q_LORA, 0, 0)], [(dhq, MLA_Q_LORA, 0, 0)],
                                        need=[True], place={0: (dproj, 0)}, ts=512, name=tag + "_dqnorm")
    (dproj,), (g["kv_g"],) = rowwise_bwd(f_rms, [(proj, MLA_KV_LORA, 2, 0)], [(W["kv_g"], MLA_KV_LORA, 0, 0)], [(dhkv, MLA_KV_LORA, 0, 0)],
                                         need=[True], place={0: (dproj, 2)}, ts=512, name=tag + "_dkvnorm")
    dh = mm(dproj, W["w_in"], tb=True, name=tag + "_dh")
    g["w_in"] = mm(h, dproj, ta=True, out_dtype=BF16, name=tag + "_dw_in")
    dx, g["ng"] = _norm_bwd(x, W["ng"], dh, dxn, tag + "_dnorm")
    return dx, g


def _pad_cols(a, n):
    return jnp.pad(a, ((0, 0), (0, n - a.shape[1])))


def _mla_w_in_layout(w):
    z = lambda n: jnp.zeros((w.shape[0], n), w.dtype)
    kr = w[:, 1280:1344]
    return jnp.concatenate([w[:, :768], z(256), w[:, 768:1280], kr[:, :32], z(32), kr[:, 32:], z(32), z(384), w[:, 1344:]], axis=1)


def _mla_w_in_unlayout(g):
    return jnp.concatenate([g[:, :768], g[:, 1024:1536], g[:, 1536:1568], g[:, 1600:1632], g[:, 2048:]], axis=1)


def _mla_w_uq_layout(w):
    w3 = w.reshape(w.shape[0], MLA_H, MLA_NOPE + MLA_ROPE)
    z = jnp.zeros((w.shape[0], MLA_H, 32), w.dtype)
    return jnp.concatenate([w3[..., :128], w3[..., 128:160], z, w3[..., 160:192], z], axis=-1).reshape(w.shape[0], MLA_H * 256)


def _mla_w_uq_unlayout(g):
    g3 = g.reshape(g.shape[0], MLA_H, 256)
    return jnp.concatenate([g3[..., :128], g3[..., 128:160], g3[..., 192:224]], axis=-1).reshape(g.shape[0], MLA_H * 192)


def _rope_consts():
    half = MLA_ROPE // 2
    inv = ROPE_THETA ** (-jnp.arange(half, dtype=F32) / half)
    z = jnp.zeros((half,), F32)
    o = jnp.ones((half,), F32)
    row = lambda *p: jnp.concatenate(p).reshape(1, LANES)
    return row(inv, z, inv, z), row(o, z, o, z), row(-o, z, o, z)


BIG = ["pool_w_in", "pool_w_grp", "pool_w_out", "gdn_w_in", "gdn_w_out", "mla_w_in", "mla_w_uq", "mla_w_ukv", "mla_w_out"]
BIG_LAYOUT = {"pool_w_in": (1, 1024, (1024, 4096)), "pool_w_grp": (1, 128, (4, 512, 512)), "pool_w_out": (0, 512, (2048, 1024)),
              "gdn_w_in": (None, None, (4, 1024, 1540)), "gdn_w_out": (0, 512, (2048, 1024)),
              "mla_w_in": (None, None, (4, 1024, 848)), "mla_w_uq": (1, 768, (768, 3072)), "mla_w_ukv": (1, 1024, (512, 4096)),
              "mla_w_out": (0, 512, (2048, 1024))}
SMALL_SHARDED = ["pool_scale", "gdn_conv", "mla_q_norm_g", "mla_kv_norm_g"]
SMALL_AXIS = {"pool_scale": 1, "gdn_conv": 2, "mla_q_norm_g": 1, "mla_kv_norm_g": 1}
REPLICATED = ["norm_g", "gdn_a_log", "gdn_dt_bias", "gdn_norm_g", "final_g"]
PACK_C = 1024


def _pack(parts, dtype, row_mult):
    flat = jnp.concatenate([p.reshape(-1).astype(dtype) for p in parts])
    rows = -(-flat.shape[0] // PACK_C)
    rows = -(-rows // row_mult) * row_mult
    return jnp.pad(flat, (0, rows * PACK_C - flat.shape[0])).reshape(rows, PACK_C)


def _unpack(buf, shapes):
    lead = buf.shape[:-2]
    flat = buf.reshape(lead + (-1,))
    out, off = [], 0
    for s in shapes:
        n = int(np.prod(s))
        out.append(flat[..., off:off + n].reshape(lead + tuple(s)))
        off += n
    return out


def _unshard(g4, axis):
    a = jnp.moveaxis(g4, 0, axis)
    s = a.shape
    return a.reshape(s[:axis] + (s[axis] * s[axis + 1],) + s[axis + 2:])


def _to_shards(a, axis):
    s = a.shape
    return jnp.moveaxis(a.reshape(s[:axis] + (4, s[axis] // 4) + s[axis + 1:]), axis, 0)


def layer_weights(full, small, rep, layer):
    ng = rep["norm_g"][layer:layer + 1]
    side_by_side = lambda a4: jnp.moveaxis(a4, 0, 1).reshape(a4.shape[1], 4 * a4.shape[2])
    if layer in (0, 3):
        j = layer // 3
        return dict(ng=ng, w_in=full[("pool_w_in", j)], w_grp=full[("pool_w_grp", j)], scale=small["pool_scale"][j:j + 1],
                    w_out=full[("pool_w_out", j)])
    if layer == 1:
        return dict(ng=ng, w_in=_pad_cols(side_by_side(full[("gdn_w_in", 0)]), GDN_IN_PAD),
                    conv=jnp.pad(small["gdn_conv"][0], ((0, 4), (0, 0))), a_log=_pad_cols(rep["gdn_a_log"], LANES),
                    dt_bias=_pad_cols(rep["gdn_dt_bias"], LANES), norm_g=rep["gdn_norm_g"], w_out=full.get(("gdn_w_out", 0)))
    return dict(ng=ng, w_in=_mla_w_in_layout(side_by_side(full[("mla_w_in", 0)])), q_g=small["mla_q_norm_g"],
                kv_g=small["mla_kv_norm_g"], w_uq=_mla_w_uq_layout(full[("mla_w_uq", 0)]), w_ukv=full[("mla_w_ukv", 0)],
                w_out=full[("mla_w_out", 0)], rope=_rope_consts())


def big_grad_pieces(gl):
    g0, g1, g2, g3 = gl
    slots = lambda a: jnp.moveaxis(a.reshape(a.shape[0], 4, a.shape[1] // 4), 1, 0)
    out = {}
    for l, g in ((0, g0), (1, g3)):
        if g is not None:
            out.update({("pool_w_in", l): g["w_in"], ("pool_w_grp", l): g["w_grp"], ("pool_w_out", l): g["w_out"]})
    if g1 is not None:
        out.update({("gdn_w_in", 0): slots(g1["w_in"][:, :GDN_IN]), ("gdn_w_out", 0): g1["w_out"]})
    if g2 is not None:
        out.update({("mla_w_in", 0): slots(_mla_w_in_unlayout(g2["w_in"])), ("mla_w_uq", 0): _mla_w_uq_unlayout(g2["w_uq"]),
                    ("mla_w_ukv", 0): g2["w_ukv"], ("mla_w_out", 0): g2["w_out"]})
    return out


def small_grads(gl, dfinal):
    g0, g1, g2, g3 = gl
    return {"norm_g": jnp.concatenate([g0["ng"], g1["ng"], g2["ng"], g3["ng"]], axis=0),
            "pool_scale": jnp.concatenate([g0["scale"], g3["scale"]], axis=0), "gdn_conv": g1["conv"][None, :4],
            "gdn_a_log": g1["a_log"][:, :GDN_H], "gdn_dt_bias": g1["dt_bias"][:, :GDN_H], "gdn_norm_g": g1["norm_g"],
            "mla_q_norm_g": g2["q_g"], "mla_kv_norm_g": g2["kv_g"], "final_g": dfinal.reshape(D)}


NAMES = ["norm_g", "pool_w_in", "pool_w_grp", "pool_scale", "pool_w_out", "gdn_w_in", "gdn_conv", "gdn_a_log", "gdn_dt_bias",
         "gdn_norm_g", "gdn_w_out", "mla_w_in", "mla_q_norm_g", "mla_w_uq", "mla_kv_norm_g", "mla_w_ukv", "mla_w_out", "final_g"]


def kernel(x, positions, norm_g, pool_w_in, pool_w_grp, pool_scale, pool_w_out, gdn_w_in, gdn_conv, gdn_a_log, gdn_dt_bias, gdn_norm_g, gdn_w_out, mla_w_in, mla_q_norm_g, mla_w_uq, mla_kv_norm_g, mla_w_ukv, mla_w_out, final_g, loss_target, m_norm_g, m_pool_w_in, m_pool_w_grp, m_pool_scale, m_pool_w_out, m_gdn_w_in, m_gdn_conv, m_gdn_a_log, m_gdn_dt_bias, m_gdn_norm_g, m_gdn_w_out, m_mla_w_in, m_mla_q_norm_g, m_mla_w_uq, m_mla_kv_norm_g, m_mla_w_ukv, m_mla_w_out, m_final_g, v_norm_g, v_pool_w_in, v_pool_w_grp, v_pool_scale, v_pool_w_out, v_gdn_w_in, v_gdn_conv, v_gdn_a_log, v_gdn_dt_bias, v_gdn_norm_g, v_gdn_w_out, v_mla_w_in, v_mla_q_norm_g, v_mla_w_uq, v_mla_kv_norm_g, v_mla_w_ukv, v_mla_w_out, v_final_g):
    args = locals()
    w = {n: args[n] for n in NAMES}
    m = {n: args["m_" + n] for n in NAMES}
    v = {n: args["v_" + n] for n in NAMES}
    my_chip = (2 * lax.axis_index("x") + lax.axis_index("y")).astype(I32)
    S_ = x.shape[1]
    x0, pos, target = x[0], positions.reshape(S_, 1).astype(F32), loss_target[0]
    rep = {n: w[n] for n in REPLICATED}

    small_shapes = [w[n].shape for n in SMALL_SHARDED]
    small_pack = _pack([w[n] for n in SMALL_SHARDED], F32, 8)[None]
    layout = dict(BIG_LAYOUT, small=(None, None, (4,) + small_pack.shape[1:]))

    def shard(key, token):
        n, l = key
        if n == "small":
            return small_pack
        a = w[n][l:l + 1]
        return (a if token is None else a + token[0, 0]).astype(BF16)

    def gather_start(group, after, tag, token=None):
        pieces = [gather_piece(i, 0, i, layout[n][0], layout[n][1]) for i, (n, l) in enumerate(group)]
        ins = [shard(k, token) for k in group]
        shapes = [_sds(layout[n][2], a.dtype) for (n, l), a in zip(group, ins)]
        sems, ins, lands, token = exchange_start(pieces, ins, shapes, after, tag + "_start")
        return (pieces, sems, ins, lands), token

    def finish(handle, after, tag):
        return exchange_wait(*handle, after, tag + "_wait")

    def gathered(group, handle, after, tag):
        srcs, lands = finish(handle, after, tag)
        return {(n, l): place_own(a, s[0], layout[n][0], layout[n][1], my_chip) for (n, l), s, a in zip(group, srcs, lands)}

    group_a = [("small", 0), ("pool_w_in", 0)]
    group_a2 = [("pool_w_grp", 0), ("pool_w_out", 0)]
    group_b = [("gdn_w_in", 0)]
    group_c = [("gdn_w_out", 0), ("mla_w_in", 0), ("mla_w_uq", 0), ("mla_w_ukv", 0), ("mla_w_out", 0), ("pool_w_in", 1),
               ("pool_w_grp", 1), ("pool_w_out", 1)]
    full = {}
    h_a, t_a = gather_start(group_a, x0, "gather_a")
    h_a2, t_a2 = gather_start(group_a2, t_a, "gather_a2", t_a)
    h_b, t_b = gather_start(group_b, t_a2, "gather_b", t_a2)
    h_c, t_c = gather_start(group_c, t_b, "gather_c", t_b)
    full.update(gathered(group_a, h_a, t_c, "gather_a"))
    small = {n: _unshard(a, SMALL_AXIS[n]) for n, a in zip(SMALL_SHARDED, _unpack(full[("small", 0)], small_shapes))}

    def late_l0(proj):
        full.update(gathered(group_a2, h_a2, proj, "gather_a2"))
        return dict(w_grp=full[("pool_w_grp", 0)], w_out=full[("pool_w_out", 0)])

    first = dict(ng=rep["norm_g"][0:1] + t_c[0:1, 0:1], w_in=full[("pool_w_in", 0)], scale=small["pool_scale"][0:1])
    x1, s0 = pool_fwd(x0, first, "l0", late=late_l0)
    W0 = layer_weights(full, small, rep, 0)
    full.update(gathered(group_b, h_b, x1, "gather_b"))

    def late_l1(og):
        full.update(gathered(group_c, h_c, og, "gather_c"))
        return dict(w_out=full[("gdn_w_out", 0)])

    x2, s1 = gdn_fwd(x1, layer_weights(full, small, rep, 1), "l1", late=late_l1)
    W1, W2, W3 = (layer_weights(full, small, rep, i) for i in (1, 2, 3))
    x3, s2 = mla_fwd(x2, pos, W2, "l2")
    x4, s3 = pool_fwd(x3, W3, "l3")
    loss_part, dx4, dfinal = loss_head(x4, target, final_g.reshape(1, D), "loss_head")

    def scatter_start(pieces_of, after, tag):
        keys = list(pieces_of)
        pieces = [scatter_piece(i, i, BIG_LAYOUT[n][0], BIG_LAYOUT[n][1]) for i, (n, l) in enumerate(keys)]
        shapes = [_sds((4,) + tuple(w[n].shape[1:]), BF16) for n, l in keys]
        sems, ins, lands, token = exchange_start(pieces, [pieces_of[k] for k in keys], shapes, after, tag + "_start")
        return keys, (pieces, sems, ins, lands), token

    def scattered(keys, handle, after, tag):
        srcs, lands = finish(handle, after, tag)
        return {(n, l): place_own(a, own_window(g, BIG_LAYOUT[n][0], BIG_LAYOUT[n][1], my_chip), None, None, my_chip)
                for (n, l), g, a in zip(keys, srcs, lands)}

    dx3, g3 = pool_bwd(dx4, W3, s3, "l3")
    k3, h3, t3 = scatter_start(big_grad_pieces((None, None, None, g3)), dx3, "scatter_l3")
    dx2, g2 = mla_bwd(dx3, pos, W2, s2, "l2", after=t3)
    k2, h2, t2 = scatter_start(big_grad_pieces((None, None, g2, None)), dx2, "scatter_l2")
    dx1, g1 = gdn_bwd(dx2, W1, s1, "l1", after=t2)
    k1, h1, t1 = scatter_start(big_grad_pieces((None, g1, None, None)), dx1, "scatter_l1")
    def swap_start(part, tag):
        keys = list(part)
        ins = [part[k] for k in keys]
        pieces = [whole_piece(i) for i in range(len(keys))]
        sems, ins, lands, token = exchange_start(pieces, ins, [_sds(a.shape, a.dtype) for a in ins], ins[0], tag + "_start", sibling=True)
        swaps.append((keys, (pieces, sems, ins, lands), tag))
        return token

    last, swaps = [], []

    def emit_l0(grads):
        first = not last
        now = next(iter(grads.values()))
        early = [(k3, h3, "scatter_l3"), (k2, h2, "scatter_l2")] if first else [(k1, h1, "scatter_l1")]
        landed = {}
        for keys, handle, tag in early:
            landed.update(scattered(keys, handle, now, tag))
        swapping = swap_start(landed, "swap_a" if first else "swap_b")
        tag = "scatter_l0a" if first else "scatter_l0b"
        keys, handle, token = scatter_start({("pool_" + k, 0): a for k, a in grads.items()}, swapping, tag)
        last.append((keys, handle, tag))
        return token

    dx0, g0 = pool_bwd(dx1, W0, s0, "l0", after=t1, emit=emit_l0)
    landed = {}
    for keys, handle, tag in last:
        landed.update(scattered(keys, handle, dx0, tag))
    swapping = swap_start(landed, "swap_c")
    recv, sib = {}, {}
    for keys, handle, tag in swaps:
        mine_, theirs = exchange_wait(*handle, dx0, tag + "_wait", sibling=True)
        recv.update(zip(keys, mine_))
        sib.update(zip(keys, theirs))

    sg = small_grads((g0, g1, g2, g3), dfinal)
    small_names = SMALL_SHARDED + REPLICATED
    small_buf = _pack([sg[n] for n in small_names] + [loss_part], F32, 8)
    small_sum = sum_slots(exchange_all(small_buf, "gather_small"), "sum_small")
    full_small = _unpack(small_sum, [sg[n].shape for n in small_names] + [(1, LANES)])
    loss = full_small[-1][0, 0]
    small_part = {}
    for n, a in zip(small_names, full_small[:-1]):
        if n in SMALL_AXIS:
            a = lax.dynamic_index_in_dim(_to_shards(a, SMALL_AXIS[n]), my_chip, axis=0, keepdims=False)
        small_part[n] = a

    outs = []
    for n in NAMES:
        shp = w[n].shape
        two = (int(np.prod(shp[:-1])), shp[-1]) if len(shp) > 1 else (1, shp[0])
        if n in BIG_LAYOUT:
            layers = shp[0]
            rows = lambda a: a.reshape(4, two[0] // layers, two[1])
            parts = [[rows(recv[(n, l)]) for l in range(layers)], [rows(sib[(n, l)]) for l in range(layers)]]
        else:
            parts = [[small_part[n].reshape((1,) + two)]]
        upper = n in BIG_LAYOUT and not n.startswith("pool")
        res = adamw(w[n].reshape(two), parts, m[n].reshape(two), v[n].reshape(two), "adamw_" + n, after=swapping if upper else None)
        outs.append([r.reshape(shp) for r in res])
    return (loss, dx0[None], *[o[0] for o in outs], *[o[1] for o in outs], *[o[2] for o in outs], *[o[3] for o in outs])
```

```python
import math

import jax
import jax.numpy as jnp
import numpy as np
from jax import lax
from jax.experimental import pallas as pl
from jax.experimental.pallas import tpu as pltpu

F32 = jnp.float32
BF16 = jnp.bfloat16
I32 = jnp.int32

D = 1024
EPS = 1e-6
POOL_WIDTH = 2048
POOL_GROUP = 512
GDN_H, GDN_DK, GDN_DV, GDN_C = 8, 128, 256, 64
GDN_QK, GDN_V, GDN_CONV_CH, GDN_IN = 1024, 2048, 4096, 6160
GDN_IN_PAD = 6272
MLA_H, MLA_NOPE, MLA_ROPE, MLA_V = 16, 128, 64, 128
MLA_Q_LORA, MLA_KV_LORA, MLA_WIDTH, MLA_IN = 768, 512, 2048, 3392
MLA_IN_PAD = 4096
MLA_SCALE = (MLA_NOPE + MLA_ROPE) ** -0.5
ROPE_THETA = 10000.0
ADAM_LR, ADAM_B1, ADAM_B2, ADAM_EPS, ADAM_WD, ADAM_STEP = 0.001, 0.9, 0.999, 1e-08, 0.01, 10

VMEM_LIMIT_V7X = 56 * 1024 * 1024
LANES = 128
MESH = pl.DeviceIdType.MESH


def _pc(body, **kw):
    return pl.pallas_call(body, **kw)


def _cparams(sem):
    return pltpu.CompilerParams(dimension_semantics=sem, vmem_limit_bytes=VMEM_LIMIT_V7X)


def _tile(n, cap):
    t = (cap // LANES) * LANES
    while t >= LANES:
        if n % t == 0:
            return t
        t -= LANES
    return n


def _sds(shape, dt):
    return jax.ShapeDtypeStruct(shape, dt)


def mm(a, b, *, ta=False, tb=False, add=None, after=None, out_dtype=F32, name):
    if ta:
        K, M = a.shape
    else:
        M, K = a.shape
    if tb:
        N, K2 = b.shape
    else:
        K2, N = b.shape
    assert K == K2, (a.shape, b.shape, ta, tb)
    tm, tn, tk = _tile(M, 1024), _tile(N, 1024), _tile(K, 1024)
    nk = K // tk
    a_spec = pl.BlockSpec((tk, tm), lambda i, j, k: (k, i)) if ta else pl.BlockSpec((tm, tk), lambda i, j, k: (i, k))
    b_spec = pl.BlockSpec((tn, tk), lambda i, j, k: (j, k)) if tb else pl.BlockSpec((tk, tn), lambda i, j, k: (k, j))
    o_spec = pl.BlockSpec((tm, tn), lambda i, j, k: (i, j))
    dn = (((0 if ta else 1,), (1 if tb else 0,)), ((), ()))
    has_add = add is not None

    def body(*refs):
        a_ref, b_ref = refs[0], refs[1]
        part = lax.dot_general(a_ref[...].astype(BF16), b_ref[...].astype(BF16), dn, preferred_element_type=F32)
        if nk == 1:
            refs[-1][...] = (part + refs[2][...] if has_add else part).astype(out_dtype)
            return
        o_ref, acc = refs[-2], refs[-1]
        k = pl.program_id(2)

        @pl.when(k == 0)
        def _():
            acc[...] = part

        @pl.when(k > 0)
        def _():
            acc[...] += part

        @pl.when(k == nk - 1)
        def _():
            r = acc[...]
            if has_add:
                r = r + refs[2][...]
            o_ref[...] = r.astype(out_dtype)

    ins = [a, b] + ([add] if has_add else []) + ([after] if after is not None else [])
    specs = [a_spec, b_spec] + ([o_spec] if has_add else []) + ([pl.BlockSpec(memory_space=pl.ANY)] if after is not None else [])
    return _pc(body, grid=(M // tm, N // tn, nk), in_specs=specs, out_specs=o_spec, out_shape=_sds((M, N), out_dtype),
               scratch_shapes=[pltpu.VMEM((tm, tn), F32)] if nk > 1 else [], compiler_params=_cparams(("parallel", "parallel", "arbitrary")),
               name=name)(*ins)


def gmm(kind, a, b, *, G, name, out_dtype=F32):
    S_ = a.shape[0]
    Ka = a.shape[1] // G
    if kind == "tn":
        N = b.shape[1] // G
        tk = _tile(S_, 2048)
        nk = S_ // tk

        def body(a_ref, b_ref, o_ref, acc):
            k = pl.program_id(1)

            @pl.when(k == 0)
            def _():
                acc[...] = jnp.zeros_like(acc)

            acc[...] += lax.dot_general(a_ref[...].astype(BF16), b_ref[...].astype(BF16), (((0,), (0,)), ((), ())),
                                        preferred_element_type=F32)

            @pl.when(k == nk - 1)
            def _():
                o_ref[...] = acc[...].astype(out_dtype)

        return _pc(body, grid=(G, nk),
                   in_specs=[pl.BlockSpec((tk, Ka), lambda g, k: (k, g)), pl.BlockSpec((tk, N), lambda g, k: (k, g))],
                   out_specs=pl.BlockSpec((None, Ka, N), lambda g, k: (g, 0, 0)), out_shape=_sds((G, Ka, N), out_dtype),
                   scratch_shapes=[pltpu.VMEM((Ka, N), F32)], compiler_params=_cparams(("parallel", "arbitrary")), name=name)(a, b)
    N = b.shape[2] if kind == "nn" else b.shape[1]
    tm = _tile(S_, 4096)
    dn = (((1,), (0 if kind == "nn" else 1,)), ((), ()))

    def body(a_ref, b_ref, o_ref):
        o_ref[...] = lax.dot_general(a_ref[...].astype(BF16), b_ref[...].astype(BF16), dn, preferred_element_type=F32).astype(out_dtype)

    bshape = (None,) + tuple(b.shape[1:])
    return _pc(body, grid=(G, S_ // tm),
               in_specs=[pl.BlockSpec((tm, Ka), lambda g, i: (i, g)), pl.BlockSpec(bshape, lambda g, i: (g, 0, 0))],
               out_specs=pl.BlockSpec((tm, N), lambda g, i: (i, g)), out_shape=_sds((S_, G * N), out_dtype),
               compiler_params=_cparams(("parallel", "parallel")), name=name)(a, b)


def _rw_spec(ts, w, c, s):
    return pl.BlockSpec((ts, w), lambda j, i: (i, c + j * s))


def _rw_pspec(p, w, c, s):
    return pl.BlockSpec((p.shape[0], w), lambda j, i: (0, c + j * s))


def rowwise(f, tiles, params, outs, *, ncol=1, ts, name):
    S_ = tiles[0][0].shape[0]
    nin = len(tiles) + len(params)

    def body(*refs):
        res = f(pl.program_id(0), *[r[...].astype(F32) for r in refs[:nin]])
        for r, o in zip(refs[nin:], res):
            r[...] = o.astype(r.dtype)

    return _pc(body, grid=(ncol, S_ // ts),
               in_specs=[_rw_spec(ts, w, c, s) for (_, w, c, s) in tiles] + [_rw_pspec(*p) for p in params],
               out_specs=[_rw_spec(ts, w, 0, s) for (w, s, _) in outs],
               out_shape=[_sds((S_, w * (ncol if s else 1)), dt) for (w, s, dt) in outs],
               compiler_params=_cparams(("parallel", "parallel")), name=name)(*[t[0] for t in tiles], *[p[0] for p in params])


def rowwise_bwd(f, tiles, params, cots, *, need, adds=None, place=None, narrow=(), ncol=1, ts, name):
    S_ = tiles[0][0].shape[0]
    adds = adds or {}
    place = place or {}
    nt, npar, nc = len(tiles), len(params), len(cots)
    add_keys = sorted(adds)
    need_idx = [k for k in range(nt) if need[k]]
    into_keys = [k for k in need_idx if k in place and not isinstance(place[k][0], int)]
    n_extra = len(add_keys) + len(into_keys)

    def body(*refs):
        j, i = pl.program_id(0), pl.program_id(1)
        vals = [r[...].astype(F32) for r in refs[:nt + npar]]
        cvals = tuple(r[...].astype(F32) for r in refs[nt + npar:nt + npar + nc])
        add_refs = refs[nt + npar + nc:nt + npar + nc + len(add_keys)]
        out_refs = refs[nt + npar + nc + n_extra:]
        _, vjp = jax.vjp(lambda *v: tuple(f(j, *v)), *vals)
        grads = vjp(cvals)
        for n, k in enumerate(need_idx):
            g = grads[k]
            if k in adds:
                g = g + add_refs[add_keys.index(k)][...]
            out_refs[n][...] = g.astype(out_refs[n].dtype)
        for n in range(npar):
            ref = out_refs[len(need_idx) + n]
            first = (i == 0) if params[n][3] else jnp.logical_and(i == 0, j == 0)

            @pl.when(first)
            def _():
                ref[...] = jnp.zeros_like(ref)

            ref[...] += grads[nt + n]

    in_specs = ([_rw_spec(ts, w, c, s) for (_, w, c, s) in tiles] + [_rw_pspec(*p) for p in params]
                + [_rw_spec(ts, w, c, s) for (_, w, c, s) in cots] + [_rw_spec(ts, *adds[k][1:]) for k in add_keys]
                + [pl.BlockSpec(memory_space=pl.ANY) for _ in into_keys])
    out_specs, out_shape, aliases = [], [], {}
    for n, k in enumerate(need_idx):
        w, s = tiles[k][1], tiles[k][3]
        if k in place:
            dst, c0 = place[k]
            total = dst if isinstance(dst, int) else dst.shape[1]
            out_specs.append(_rw_spec(ts, w, c0, s))
            out_shape.append(_sds((S_, total), (BF16 if k in narrow else F32) if isinstance(dst, int) else dst.dtype))
            if k in into_keys:
                aliases[nt + npar + nc + len(add_keys) + into_keys.index(k)] = n
        else:
            out_specs.append(_rw_spec(ts, w, 0, s))
            out_shape.append(_sds((S_, w * (ncol if s else 1)), BF16 if k in narrow else F32))
    out_specs += [_rw_pspec(p[0], p[1], p[2], p[3]) for p in params]
    out_shape += [_sds(p[0].shape, F32) for p in params]
    res = _pc(body, grid=(ncol, S_ // ts), in_specs=in_specs, out_specs=out_specs, out_shape=out_shape,
              input_output_aliases=aliases, compiler_params=_cparams(("arbitrary", "arbitrary")), name=name)(
        *[t[0] for t in tiles], *[p[0] for p in params], *[c[0] for c in cots], *[adds[k][0] for k in add_keys],
        *[place[k][0] for k in into_keys])
    return list(res[:len(need_idx)]), list(res[len(need_idx):])


def _rms(x, g):
    r = lax.rsqrt(jnp.mean(x * x, axis=-1, keepdims=True) + EPS)
    return x * r * g


def _silu(x):
    return x * jax.nn.sigmoid(x)


@jax.custom_vjp
def _softplus(x):
    return jnp.maximum(x, 0.0) + jnp.log1p(jnp.exp(-jnp.abs(x)))


_softplus.defvjp(lambda x: (_softplus(x), x), lambda x, d: (d * jax.nn.sigmoid(x),))


def f_rms(j, x, g):
    return (_rms(x, g),)


def f_pool_gate(j, pg, gate, scale):
    return (pg * scale * _silu(gate),)


def f_ogate(j, o, gate):
    return (o * _silu(gate),)


def f_gdn_out(j, o, gate, g):
    return (_rms(o, g) * _silu(gate),)


def f_gdn_gates(j, ba, alog, dtb):
    lane = lax.broadcasted_iota(I32, (1, LANES), 1)
    gs, bs = [], []
    for h in range(GDN_H):
        eb = (lane == h).astype(F32)
        ea = (lane == GDN_H + h).astype(F32)
        b = jnp.sum(ba * eb, -1, keepdims=True)
        a = jnp.sum(ba * ea, -1, keepdims=True)
        al = jnp.sum(alog * eb, -1, keepdims=True)
        dt = jnp.sum(dtb * eb, -1, keepdims=True)
        g = -jnp.exp(al) * _softplus(a + dt)
        gs.append(jnp.broadcast_to(g, ba.shape))
        bs.append(jnp.broadcast_to(jax.nn.sigmoid(b), ba.shape))
    return jnp.concatenate(gs, 1), jnp.concatenate(bs, 1)


def _shift_dn(x, k):
    rows = lax.broadcasted_iota(I32, x.shape, 0)
    return jnp.where(rows < k, 0.0, pltpu.roll(x, k, 0))


def _shift_up(x, k):
    n = x.shape[0]
    rows = lax.broadcasted_iota(I32, x.shape, 0)
    return jnp.where(rows >= n - k, 0.0, pltpu.roll(x, n - k, 0))


def _pool_window(j):
    g = lax.div(j, POOL_GROUP // LANES)
    return jnp.where(g == 0, 2.0, jnp.where(g == 1, 4.0, jnp.where(g == 2, 8.0, 16.0))), g


def _pick(g, a2, a4, a8, a16):
    return jnp.where(g == 0, a2, jnp.where(g == 1, a4, jnp.where(g == 2, a8, a16)))


def pool_time_fwd(proj, name):
    S_ = proj.shape[0]

    def body(u_ref, p_ref):
        u = u_ref[...].astype(F32)
        w, g = _pool_window(pl.program_id(0))
        s2 = u + _shift_dn(u, 1)
        s4 = s2 + _shift_dn(s2, 2)
        s8 = s4 + _shift_dn(s4, 4)
        s16 = s8 + _shift_dn(s8, 8)
        t1 = (lax.broadcasted_iota(I32, u.shape, 0) + 1).astype(F32)
        p_ref[...] = (_pick(g, s2, s4, s8, s16) / jnp.minimum(t1, w) - u).astype(p_ref.dtype)

    return _pc(body, grid=(POOL_WIDTH // LANES,), in_specs=[pl.BlockSpec((S_, LANES), lambda j: (0, j))],
               out_specs=pl.BlockSpec((S_, LANES), lambda j: (0, j)), out_shape=_sds((S_, POOL_WIDTH), BF16),
               compiler_params=_cparams(("parallel",)), name=name)(proj)


def pool_time_bwd(dp, into, name):
    S_ = dp.shape[0]

    def body(dp_ref, _, du_ref):
        d = dp_ref[...].astype(F32)
        w, g = _pool_window(pl.program_id(0))
        t1 = (lax.broadcasted_iota(I32, d.shape, 0) + 1).astype(F32)
        q = d / jnp.minimum(t1, w)
        r2 = q + _shift_up(q, 1)
        r4 = r2 + _shift_up(r2, 2)
        r8 = r4 + _shift_up(r4, 4)
        r16 = r8 + _shift_up(r8, 8)
        du_ref[...] = (_pick(g, r2, r4, r8, r16) - d).astype(du_ref.dtype)

    return _pc(body, grid=(POOL_WIDTH // LANES,),
               in_specs=[pl.BlockSpec((S_, LANES), lambda j: (0, j)), pl.BlockSpec(memory_space=pl.ANY)],
               out_specs=pl.BlockSpec((S_, LANES), lambda j: (0, j)), out_shape=_sds(into.shape, into.dtype),
               input_output_aliases={1: 0}, compiler_params=_cparams(("parallel",)), name=name)(dp, into)


def _conv_post(j, a):
    n = a * lax.rsqrt(jnp.sum(a * a, axis=-1, keepdims=True) + EPS)
    nq = GDN_QK // LANES
    return jnp.where(j < nq, n * (GDN_DK ** -0.5), jnp.where(j < 2 * nq, n, a))


def _conv_taps(u):
    return [_shift_dn(u, 3), _shift_dn(u, 2), _shift_dn(u, 1), u]


def _conv_pre(taps, w):
    return w[0:1] * taps[0] + w[1:2] * taps[1] + w[2:3] * taps[2] + w[3:4] * taps[3]


def gdn_conv_fwd(proj, conv_w, name):
    S_ = proj.shape[0]

    def body(u_ref, w_ref, o_ref):
        o_ref[...] = _conv_post(pl.program_id(0), _silu(_conv_pre(_conv_taps(u_ref[...]), w_ref[...])))

    return _pc(body, grid=(GDN_CONV_CH // LANES,),
               in_specs=[pl.BlockSpec((S_, LANES), lambda j: (0, j)), pl.BlockSpec((8, LANES), lambda j: (0, j))],
               out_specs=pl.BlockSpec((S_, LANES), lambda j: (0, j)), out_shape=_sds((S_, GDN_CONV_CH), F32),
               compiler_params=_cparams(("parallel",)), name=name)(proj, conv_w)


def gdn_conv_bwd(proj, conv_w, dq, dk, dv, into, name):
    S_ = proj.shape[0]
    nq = GDN_QK // LANES

    def body(u_ref, w_ref, dq_ref, dk_ref, dv_ref, _, du_ref, dw_ref):
        j = pl.program_id(0)
        u, w = u_ref[...], w_ref[...]
        taps = _conv_taps(u)
        c = _conv_pre(taps, w)
        sig = jax.nn.sigmoid(c)
        dout = jnp.where(j < nq, dq_ref[...], jnp.where(j < 2 * nq, dk_ref[...], dv_ref[...]))
        _, vjp = jax.vjp(lambda a: _conv_post(j, a), c * sig)
        dc = vjp(dout)[0] * (sig * (1.0 + c * (1.0 - sig)))
        du = w[3:4] * dc + w[2:3] * _shift_up(dc, 1) + w[1:2] * _shift_up(dc, 2) + w[0:1] * _shift_up(dc, 3)
        du_ref[...] = du.astype(du_ref.dtype)
        rows = lax.broadcasted_iota(I32, (8, LANES), 0)
        dw = jnp.zeros((8, LANES), F32)
        for k in range(4):
            dw = dw + jnp.where(rows == k, jnp.sum(dc * taps[k], axis=0, keepdims=True), 0.0)
        dw_ref[...] = dw

    blk = lambda f: pl.BlockSpec((S_, LANES), f)
    return _pc(body, grid=(GDN_CONV_CH // LANES,),
               in_specs=[blk(lambda j: (0, j)), pl.BlockSpec((8, LANES), lambda j: (0, j)),
                         blk(lambda j: (0, jnp.minimum(j, nq - 1))), blk(lambda j: (0, jnp.clip(j - nq, 0, nq - 1))),
                         blk(lambda j: (0, jnp.clip(j - 2 * nq, 0, 2 * nq - 1))), pl.BlockSpec(memory_space=pl.ANY)],
               out_specs=[blk(lambda j: (0, j)), pl.BlockSpec((8, LANES), lambda j: (0, j))],
               out_shape=[_sds(into.shape, into.dtype), _sds((8, GDN_CONV_CH), F32)], input_output_aliases={5: 0},
               compiler_params=_cparams(("parallel",)), name=name)(proj, conv_w, dq, dk, dv, into)


_NN, _NT, _TN = ((1,), (0,)), ((1,), (1,)), ((0,), (0,))


def _split(x, n):
    parts = []
    for _ in range(n):
        h = x.astype(BF16)
        parts.append(h)
        x = x - h.astype(F32)
    return parts


def _dot(a, b, dn, mode):
    d = lambda p, q: lax.dot_general(p, q, (dn, ((), ())), preferred_element_type=F32)
    if mode == "lo":
        return d(a.astype(BF16), b.astype(BF16))
    if mode == "x3":
        (ah, al), (bh, bl) = _split(a, 2), _split(b, 2)
        return d(ah, bh) + (d(ah, bl) + d(al, bh))
    b0, b1, b2 = _split(b, 3)
    ab = a.astype(BF16)
    return d(ab, b0) + (d(ab, b1) + d(ab, b2))


def _make_dots(mode):
    @jax.custom_vjp
    def nn(a, b):
        return _dot(a, b, _NN, mode)

    @jax.custom_vjp
    def nt(a, b):
        return _dot(a, b, _NT, mode)

    @jax.custom_vjp
    def tn(a, b):
        return _dot(a, b, _TN, mode)

    nn.defvjp(lambda a, b: (nn(a, b), (a, b)), lambda r, d: (nt(d, r[1]), tn(r[0], d)))
    nt.defvjp(lambda a, b: (nt(a, b), (a, b)), lambda r, d: (nn(d, r[1]), tn(d, r[0])))
    tn.defvjp(lambda a, b: (tn(a, b), (a, b)), lambda r, d: (nt(r[1], d), nn(r[0], d)))
    return nn, nt, tn


_nn_hi, _nt_hi, _tn_hi = _make_dots("x3")
_nn_lo, _nt_lo, _tn_lo = _make_dots("lo")


@jax.custom_vjp
def _nn_const(a, b):
    return _dot(a, b, _NN, "xl")


_nn_const.defvjp(lambda a, b: (_nn_const(a, b), a), lambda a, d: (jnp.zeros_like(a), _dot(a, d, _TN, "xl")))


def _each(f, *lists):
    return [f(*xs) for xs in zip(*lists)]


@jax.custom_vjp
def _unit_inverses(xs):
    C = xs[0].shape[0]
    eye = (lax.broadcasted_iota(I32, (C, C), 0) == lax.broadcasted_iota(I32, (C, C), 1)).astype(F32)
    ainv, p = [eye + a for a in xs], xs
    for _ in range(int(math.log2(C)) - 1):
        p = _each(lambda a: _dot(a, a, _NN, "x3"), p)
        ainv = _each(lambda a, b: a + _dot(a, b, _NN, "x3"), ainv, p)
    return ainv


def _unit_inverses_bwd(ainv, d):
    left = _each(lambda a, g: _dot(a, g, _TN, "x3"), ainv, d)
    return (_each(lambda t, a: _dot(t, a, _NT, "x3"), left, ainv),)


_unit_inverses.defvjp(lambda xs: (lambda a: (a, a))(_unit_inverses(xs)), _unit_inverses_bwd)


def _gdn_chunk(q, k, v, gb, bb, state):
    C = GDN_C
    e0 = (lax.broadcasted_iota(I32, (1, LANES), 1) == 0).astype(F32)
    ri = lax.broadcasted_iota(I32, (C, C), 0)
    ci = lax.broadcasted_iota(I32, (C, C), 1)
    causal, strict = ri >= ci, ri > ci
    tri, eye, ones = causal.astype(F32), (ri == ci).astype(F32), jnp.ones((C, C), F32)
    last = lax.broadcasted_iota(I32, (C, LANES), 0) == C - 1
    g1 = _each(lambda a: jnp.sum(a * e0, -1, keepdims=True), gb)
    b1 = _each(lambda a: jnp.sum(a * e0, -1, keepdims=True), bb)
    gc_c = _each(lambda g: _nn_const(tri, jnp.broadcast_to(g, (C, C))), g1)
    gc_d = _each(lambda g: _nn_const(tri, jnp.broadcast_to(g, (C, LANES))), g1)
    gr_c = _each(lambda g: _nn_const(ones, eye * g), gc_c)
    decay = _each(lambda a, r: jnp.where(causal, jnp.exp(jnp.where(causal, a - r, 0.0)), 0.0), gc_c, gr_c)
    kb = _each(lambda a, b: a * b, k, b1)
    vb = _each(lambda a, b: a * b, v, b1)
    x = _each(lambda a, b, d: -jnp.where(strict, _nt_lo(a, b) * d, 0.0), kb, k, decay)
    ainv = _unit_inverses(x)
    u = _each(_nn_hi, ainv, vb)
    w = _each(lambda a, b, g: _nn_hi(a, b * jnp.exp(g)), ainv, kb, gc_d)
    attn = _each(lambda a, b, d: jnp.where(causal, _nt_lo(a, b) * d, 0.0), q, k, decay)
    v_new = _each(lambda a, b, s: a - _nn_lo(b, s), u, w, state)
    o = _each(lambda a, g, s, t, vn: _nn_lo(a * jnp.exp(g), s) + _nn_lo(t, vn), q, gc_d, state, attn, v_new)
    gl = _each(lambda g: jnp.sum(jnp.where(last, g, 0.0), axis=0, keepdims=True), gc_d)
    new_state = _each(lambda s, g, a, gd, vn: s * jnp.exp(jnp.sum(g * e0, -1, keepdims=True)) + _tn_lo(a * jnp.exp(g - gd), vn),
                      state, gl, k, gc_d, v_new)
    return o, new_state


def _head_slices(ref, width):
    return [ref[:, h * width:(h + 1) * width] for h in range(GDN_H)]


def gdn_chunk_fwd(qkv, g_b, beta_b, name):
    S_ = qkv.shape[0]
    N = S_ // GDN_C

    def body(q_ref, k_ref, v_ref, g_ref, b_ref, o_ref, s_ref, state):
        @pl.when(pl.program_id(0) == 0)
        def _():
            state[...] = jnp.zeros_like(state)

        st = [state[h] for h in range(GDN_H)]
        s_ref[0] = state[...]
        o, st2 = _gdn_chunk(_head_slices(q_ref, GDN_DK), _head_slices(k_ref, GDN_DK), _head_slices(v_ref, GDN_DV),
                            _head_slices(g_ref, GDN_DK), _head_slices(b_ref, GDN_DK), st)
        for h in range(GDN_H):
            o_ref[:, h * GDN_DV:(h + 1) * GDN_DV] = o[h]
            state[h] = st2[h]

    return _pc(body, grid=(N,),
               in_specs=[pl.BlockSpec((GDN_C, GDN_QK), lambda n: (n, 0)), pl.BlockSpec((GDN_C, GDN_QK), lambda n: (n, 1)),
                         pl.BlockSpec((GDN_C, GDN_V), lambda n: (n, 1)), pl.BlockSpec((GDN_C, GDN_QK), lambda n: (n, 0)),
                         pl.BlockSpec((GDN_C, GDN_QK), lambda n: (n, 0))],
               out_specs=[pl.BlockSpec((GDN_C, GDN_V), lambda n: (n, 0)),
                          pl.BlockSpec((1, GDN_H, GDN_DK, GDN_DV), lambda n: (n, 0, 0, 0))],
               out_shape=[_sds((S_, GDN_V), F32), _sds((N, GDN_H, GDN_DK, GDN_DV), F32)],
               scratch_shapes=[pltpu.VMEM((GDN_H, GDN_DK, GDN_DV), F32)],
               compiler_params=_cparams(("arbitrary",)), name=name)(qkv, qkv, qkv, g_b, beta_b)


def gdn_chunk_bwd(qkv, g_b, beta_b, states, do, name):
    S_ = qkv.shape[0]
    N = S_ // GDN_C

    def body(q_ref, k_ref, v_ref, g_ref, b_ref, s_ref, do_ref, dq_ref, dk_ref, dv_ref, dg_ref, db_ref, dstate):
        @pl.when(pl.program_id(0) == 0)
        def _():
            dstate[...] = jnp.zeros_like(dstate)

        _, vjp = jax.vjp(_gdn_chunk, _head_slices(q_ref, GDN_DK), _head_slices(k_ref, GDN_DK), _head_slices(v_ref, GDN_DV),
                         _head_slices(g_ref, GDN_DK), _head_slices(b_ref, GDN_DK), [s_ref[0, h] for h in range(GDN_H)])
        dq, dk, dv, dg, db, ds = vjp((_head_slices(do_ref, GDN_DV), [dstate[h] for h in range(GDN_H)]))
        for h in range(GDN_H):
            kk, vv = slice(h * GDN_DK, (h + 1) * GDN_DK), slice(h * GDN_DV, (h + 1) * GDN_DV)
            dq_ref[:, kk] = dq[h]
            dk_ref[:, kk] = dk[h]
            dv_ref[:, vv] = dv[h]
            dg_ref[:, kk] = dg[h]
            db_ref[:, kk] = db[h]
            dstate[h] = ds[h]

    r = lambda n: N - 1 - n
    qk = lambda c: pl.BlockSpec((GDN_C, GDN_QK), lambda n: (r(n), c))
    vs = lambda c: pl.BlockSpec((GDN_C, GDN_V), lambda n: (r(n), c))
    return _pc(body, grid=(N,),
               in_specs=[qk(0), qk(1), vs(1), qk(0), qk(0),
                         pl.BlockSpec((1, GDN_H, GDN_DK, GDN_DV), lambda n: (r(n), 0, 0, 0)), vs(0)],
               out_specs=[qk(0), qk(0), vs(0), qk(0), qk(0)],
               out_shape=[_sds((S_, GDN_QK), F32), _sds((S_, GDN_QK), F32), _sds((S_, GDN_V), F32),
                          _sds((S_, GDN_QK), F32), _sds((S_, GDN_QK), F32)],
               scratch_shapes=[pltpu.VMEM((GDN_H, GDN_DK, GDN_DV), F32)],
               compiler_params=_cparams(("arbitrary",)), name=name)(qkv, qkv, qkv, g_b, beta_b, states, do)


def _rope_tables(pos_ref, inv_ref, cm_ref, sg_ref):
    ang = pos_ref[...] * inv_ref[...]
    return jnp.cos(ang) * cm_ref[...], jnp.sin(ang) * sg_ref[...]


def mla_prep_fwd(qpad, kv, proj, pos, rope_consts, name):
    S_ = qpad.shape[0]
    ts = 512
    W = 2 * LANES

    def body(q_ref, kv_ref, kr_ref, pos_ref, inv_ref, cm_ref, sg_ref, qh_ref, kh_ref, vh_ref):
        cs, sn = _rope_tables(pos_ref, inv_ref, cm_ref, sg_ref)
        rope = lambda r: r * cs + pltpu.roll(r, LANES // 2, 1) * sn
        krr = rope(kr_ref[...].astype(F32)).astype(BF16)
        for h in range(MLA_H):
            qh_ref[h, :, 0:LANES] = (q_ref[:, h * W:h * W + LANES].astype(F32) * MLA_SCALE).astype(BF16)
            qh_ref[h, :, LANES:W] = (rope(q_ref[:, h * W + LANES:(h + 1) * W].astype(F32)) * MLA_SCALE).astype(BF16)
            kh_ref[h, :, 0:LANES] = kv_ref[:, h * W:h * W + LANES].astype(BF16)
            kh_ref[h, :, LANES:W] = krr
            vh_ref[h] = kv_ref[:, h * W + LANES:(h + 1) * W].astype(BF16)

    one = pl.BlockSpec((1, LANES), lambda i: (0, 0))
    return _pc(body, grid=(S_ // ts,),
               in_specs=[pl.BlockSpec((ts, MLA_H * W), lambda i: (i, 0)), pl.BlockSpec((ts, MLA_H * W), lambda i: (i, 0)),
                         pl.BlockSpec((ts, LANES), lambda i: (i, 1536 // LANES)), pl.BlockSpec((ts, 1), lambda i: (i, 0)),
                         one, one, one],
               out_specs=[pl.BlockSpec((MLA_H, ts, W), lambda i: (0, i, 0)), pl.BlockSpec((MLA_H, ts, W), lambda i: (0, i, 0)),
                          pl.BlockSpec((MLA_H, ts, LANES), lambda i: (0, i, 0))],
               out_shape=[_sds((MLA_H, S_, W), BF16), _sds((MLA_H, S_, W), BF16), _sds((MLA_H, S_, LANES), BF16)],
               compiler_params=_cparams(("parallel",)), name=name)(qpad, kv, proj, pos, *rope_consts)


def mla_prep_bwd(dqh, dkh, dvh, pos, rope_consts, into, name):
    S_ = dqh.shape[1]
    ts = 512
    W = 2 * LANES

    def body(dq_ref, dk_ref, dv_ref, pos_ref, inv_ref, cm_ref, sg_ref, _, dqp_ref, dkv_ref, dkr_ref):
        cs, sn = _rope_tables(pos_ref, inv_ref, cm_ref, sg_ref)
        rope_t = lambda g: g * cs + pltpu.roll(g * sn, LANES // 2, 1)
        acc = jnp.zeros((ts, LANES), F32)
        for h in range(MLA_H):
            dqp_ref[:, h * W:h * W + LANES] = (dq_ref[h, :, 0:LANES].astype(F32) * MLA_SCALE).astype(BF16)
            dqp_ref[:, h * W + LANES:(h + 1) * W] = (rope_t(dq_ref[h, :, LANES:W].astype(F32)) * MLA_SCALE).astype(BF16)
            dkv_ref[:, h * W:h * W + LANES] = dk_ref[h, :, 0:LANES]
            dkv_ref[:, h * W + LANES:(h + 1) * W] = dv_ref[h]
            acc = acc + dk_ref[h, :, LANES:W].astype(F32)
        dkr_ref[...] = rope_t(acc).astype(dkr_ref.dtype)

    one = pl.BlockSpec((1, LANES), lambda i: (0, 0))
    return _pc(body, grid=(S_ // ts,),
               in_specs=[pl.BlockSpec((MLA_H, ts, W), lambda i: (0, i, 0)), pl.BlockSpec((MLA_H, ts, W), lambda i: (0, i, 0)),
                         pl.BlockSpec((MLA_H, ts, LANES), lambda i: (0, i, 0)), pl.BlockSpec((ts, 1), lambda i: (i, 0)),
                         one, one, one, pl.BlockSpec(memory_space=pl.ANY)],
               out_specs=[pl.BlockSpec((ts, MLA_H * W), lambda i: (i, 0)), pl.BlockSpec((ts, MLA_H * W), lambda i: (i, 0)),
                          pl.BlockSpec((ts, LANES), lambda i: (i, 1536 // LANES))],
               out_shape=[_sds((S_, MLA_H * W), BF16), _sds((S_, MLA_H * W), BF16), _sds(into.shape, into.dtype)],
               input_output_aliases={7: 2}, compiler_params=_cparams(("parallel",)), name=name)(dqh, dkh, dvh, pos, *rope_consts, into)


NEG = -1e30


FLASH_TILE = 1024
FLASH_SUB = 512


def _scores(q, k, diagonal):
    s = lax.dot_general(q, k, (_NT, ((), ())), preferred_element_type=F32)
    if not diagonal:
        return s
    return jnp.where(lax.broadcasted_iota(I32, s.shape, 1) <= lax.broadcasted_iota(I32, s.shape, 0), s, NEG)


def _sub_blocks(t, diagonal):
    sub = min(FLASH_SUB, t) if diagonal else t
    return [(c * sub if diagonal else 0, slice(c * sub, (c + 1) * sub)) for c in range(t // sub)]


FLASH_HEADS = 2


def flash_fwd(qh, kh, vh, name):
    H, S_, W = qh.shape
    t = _tile(S_, FLASH_TILE)
    n = S_ // t
    G = FLASH_HEADS
    heads = list(range(G))

    def body(q_ref, k_ref, v_ref, o_ref, lse_ref, m_s, l_s, acc):
        qi, kj = pl.program_id(1), pl.program_id(2)

        @pl.when(kj == 0)
        def _():
            m_s[...] = jnp.full_like(m_s, NEG)
            l_s[...] = jnp.zeros_like(l_s)
            acc[...] = jnp.zeros_like(acc)

        def step(diagonal):
            s = _each(lambda a: _scores(q_ref[a], k_ref[a], diagonal), heads)
            m_old = _each(lambda a: m_s[a], heads)
            m_new = _each(lambda mo, sa: jnp.maximum(mo, jnp.max(sa, axis=-1, keepdims=True)), m_old, s)
            alpha = _each(lambda mo, mn: jnp.exp(mo - mn), m_old, m_new)
            p = _each(lambda sa, mn: jnp.exp(sa - mn[:, :1]), s, m_new)
            pv = _each(lambda pa, a: lax.dot_general(pa.astype(BF16), v_ref[a], (_NN, ((), ())), preferred_element_type=F32), p, heads)
            for a in heads:
                l_s[a] = alpha[a] * l_s[a] + jnp.sum(p[a], axis=-1, keepdims=True)
                acc[a] = alpha[a] * acc[a] + pv[a]
                m_s[a] = m_new[a]

        pl.when(kj < qi)(lambda: step(False))
        pl.when(kj == qi)(lambda: step(True))

        @pl.when(kj == n - 1)
        def _():
            for a in heads:
                o_ref[:, a * LANES:(a + 1) * LANES] = (acc[a] / l_s[a]).astype(o_ref.dtype)
                lse_ref[a] = m_s[a] + jnp.log(l_s[a])

    return _pc(body, grid=(H // G, n, n),
               in_specs=[pl.BlockSpec((G, t, W), lambda h, i, j: (h, i, 0)),
                         pl.BlockSpec((G, t, W), lambda h, i, j: (h, jnp.minimum(i, j), 0)),
                         pl.BlockSpec((G, t, LANES), lambda h, i, j: (h, jnp.minimum(i, j), 0))],
               out_specs=[pl.BlockSpec((t, G * LANES), lambda h, i, j: (i, h)), pl.BlockSpec((G, t, LANES), lambda h, i, j: (h, i, 0))],
               out_shape=[_sds((S_, H * LANES), BF16), _sds((H, S_, LANES), F32)],
               scratch_shapes=[pltpu.VMEM((G, t, LANES), F32)] * 3,
               compiler_params=_cparams(("parallel", "parallel", "arbitrary")), name=name)(qh, kh, vh)


def flash_bwd(qh, kh, vh, o, lse, do, name):
    H, S_, W = qh.shape
    t = _tile(S_, FLASH_TILE)
    n = S_ // t
    G = FLASH_HEADS
    heads = list(range(G))

    def body(q_ref, k_ref, v_ref, o_ref, lse_ref, do_ref, dq_ref, dk_ref, dv_ref, dq_acc, dk_acc, dv_acc):
        kj, qi = pl.program_id(1), pl.program_id(2)

        @pl.when(jnp.logical_and(kj == 0, qi == 0))
        def _():
            dq_acc[...] = jnp.zeros_like(dq_acc)

        @pl.when(qi == 0)
        def _():
            dk_acc[...] = jnp.zeros_like(dk_acc)
            dv_acc[...] = jnp.zeros_like(dv_acc)

        def step(diagonal):
            lanes = lambda a: slice(a * LANES, (a + 1) * LANES)
            do_ = _each(lambda a: do_ref[:, lanes(a)], heads)
            dob = _each(lambda d: d.astype(BF16), do_)
            delta = _each(lambda d, a: jnp.sum(d.astype(F32) * o_ref[:, lanes(a)].astype(F32), axis=-1, keepdims=True), do_, heads)
            for r0, keys in _sub_blocks(t, diagonal):
                p = _each(lambda a: jnp.exp(_scores(q_ref[a, r0:, :], k_ref[a, keys, :], diagonal) - lse_ref[a, r0:, :1]), heads)
                dp = _each(lambda d, a: lax.dot_general(d[r0:], v_ref[a, keys, :], (_NT, ((), ())), preferred_element_type=F32), dob, heads)
                ds = _each(lambda pa, dpa, de: (pa * (dpa - de[r0:])).astype(BF16), p, dp, delta)
                rows = pl.ds(pl.multiple_of(qi * t, t) + r0, t - r0)
                for a in heads:
                    dv_acc[a, keys, :] += lax.dot_general(p[a].astype(BF16), dob[a][r0:], (_TN, ((), ())), preferred_element_type=F32)
                    dk_acc[a, keys, :] += lax.dot_general(ds[a], q_ref[a, r0:, :], (_TN, ((), ())), preferred_element_type=F32)
                    dq_acc[a, rows, :] += lax.dot_general(ds[a], k_ref[a, keys, :], (_NN, ((), ())), preferred_element_type=F32)

        pl.when(qi > kj)(lambda: step(False))
        pl.when(qi == kj)(lambda: step(True))

        @pl.when(qi == n - 1)
        def _():
            dk_ref[...] = dk_acc[...].astype(BF16)
            dv_ref[...] = dv_acc[...].astype(BF16)

        @pl.when(jnp.logical_and(kj == n - 1, qi == n - 1))
        def _():
            dq_ref[...] = dq_acc[...].astype(BF16)

    qrow = lambda h, j, i: jnp.maximum(i, j)
    return _pc(body, grid=(H // G, n, n),
               in_specs=[pl.BlockSpec((G, t, W), lambda h, j, i: (h, qrow(h, j, i), 0)),
                         pl.BlockSpec((G, t, W), lambda h, j, i: (h, j, 0)),
                         pl.BlockSpec((G, t, LANES), lambda h, j, i: (h, j, 0)),
                         pl.BlockSpec((t, G * LANES), lambda h, j, i: (qrow(h, j, i), h)),
                         pl.BlockSpec((G, t, LANES), lambda h, j, i: (h, qrow(h, j, i), 0)),
                         pl.BlockSpec((t, G * LANES), lambda h, j, i: (qrow(h, j, i), h))],
               out_specs=[pl.BlockSpec((G, S_, W), lambda h, j, i: (h, 0, 0)),
                          pl.BlockSpec((G, t, W), lambda h, j, i: (h, j, 0)),
                          pl.BlockSpec((G, t, LANES), lambda h, j, i: (h, j, 0))],
               out_shape=[_sds((H, S_, W), BF16), _sds((H, S_, W), BF16), _sds((H, S_, LANES), BF16)],
               scratch_shapes=[pltpu.VMEM((G, S_, W), F32), pltpu.VMEM((G, t, W), F32), pltpu.VMEM((G, t, LANES), F32)],
               compiler_params=_cparams(("parallel", "arbitrary", "arbitrary")), name=name)(qh, kh, vh, o, lse, do)


def loss_head(x, target, g, name):
    S_ = x.shape[0]
    ts = 512

    def body(x_ref, t_ref, g_ref, l_ref, dx_ref, dg_ref):
        @pl.when(pl.program_id(0) == 0)
        def _():
            l_ref[...] = jnp.zeros_like(l_ref)
            dg_ref[...] = jnp.zeros_like(dg_ref)

        y, vjp = jax.vjp(_rms, x_ref[...], g_ref[...])
        err = y - t_ref[...]
        l_ref[...] += 0.5 * jnp.sum(jnp.sum(err * err, axis=-1, keepdims=True), axis=0, keepdims=True) / D
        dx, dg = vjp(err / D)
        dx_ref[...] = dx
        dg_ref[...] += dg

    row = pl.BlockSpec((ts, D), lambda i: (i, 0))
    return _pc(body, grid=(S_ // ts,), in_specs=[row, row, pl.BlockSpec((1, D), lambda i: (0, 0))],
               out_specs=[pl.BlockSpec((1, LANES), lambda i: (0, 0)), row, pl.BlockSpec((1, D), lambda i: (0, 0))],
               out_shape=[_sds((1, LANES), F32), _sds((S_, D), F32), _sds((1, D), F32)],
               compiler_params=_cparams(("arbitrary",)), name=name)(x, target, g)


def adamw(w, parts, m, v, name, after=None):
    R, C = w.shape
    rows = [p.shape[1] for p in parts[0]]
    tr = R
    for cand in (512, 256, 128, 64, 32, 16, 8):
        if all(r % cand == 0 for r in rows) and cand * C * 4 * len(rows) <= 2 * 1024 * 1024:
            tr = cand
            break
    c1 = 1.0 - ADAM_B1 ** ADAM_STEP
    c2 = 1.0 - ADAM_B2 ** ADAM_STEP
    starts = [sum(rows[:k]) // tr for k in range(len(rows))]
    flat = [p for part in parts for p in part]

    def body(*refs):
        w_ref, m_ref, v_ref = refs[0], refs[1 + len(flat)], refs[2 + len(flat)]
        g_ref, d_ref, nm_ref, nv_ref = refs[-4:]
        i = pl.program_id(0)
        gg, at = None, 1
        for part in parts:
            val = None
            for k in range(len(part)):
                p_ref = refs[at]
                at += 1
                s = p_ref[0].astype(F32)
                for n in range(1, p_ref.shape[0]):
                    s = s + p_ref[n].astype(F32)
                val = s if val is None else jnp.where(i >= starts[k], s, val)
            gg = val if gg is None else gg + val
        m2 = ADAM_B1 * m_ref[...] + (1.0 - ADAM_B1) * gg
        v2 = ADAM_B2 * v_ref[...] + (1.0 - ADAM_B2) * (gg * gg)
        g_ref[...] = gg
        d_ref[...] = -ADAM_LR * ((m2 / c1) / (jnp.sqrt(v2 / c2) + ADAM_EPS) + ADAM_WD * w_ref[...])
        nm_ref[...] = m2
        nv_ref[...] = v2

    blk = pl.BlockSpec((tr, C), lambda i: (i, 0))
    piece = lambda p, k: pl.BlockSpec((p.shape[0], tr, C), lambda i: (0, jnp.clip(i - starts[k], 0, rows[k] // tr - 1), 0))
    pblk = [piece(p, k) for part in parts for k, p in enumerate(part)]
    extra = [] if after is None else [after]
    return _pc(body, grid=(R // tr,), in_specs=[blk] + pblk + [blk, blk] + [pl.BlockSpec(memory_space=pl.ANY)] * len(extra),
               out_specs=[blk] * 4, out_shape=[_sds((R, C), F32)] * 4,
               compiler_params=_cparams(("parallel",)), name=name)(w, *flat, m, v, *extra)


def sum_slots(recv, name):
    n, R, C = recv.shape

    def body(r_ref, o_ref):
        acc = r_ref[0]
        for s in range(1, n):
            acc = acc + r_ref[s]
        o_ref[...] = acc

    return _pc(body, grid=(1,), in_specs=[pl.BlockSpec((n, R, C), lambda i: (0, 0, 0))],
               out_specs=pl.BlockSpec((R, C), lambda i: (0, 0)), out_shape=_sds((R, C), F32),
               compiler_params=_cparams(("arbitrary",)), name=name)(recv)


def _chip_peers():
    x, y, c = lax.axis_index("x"), lax.axis_index("y"), lax.axis_index("c")
    return (x, y, c), [(1 - x, y, c), (x, 1 - y, c), (1 - x, 1 - y, c)]


def _chip_index(p):
    return 2 * p[0] + p[1]


def _win(ref, axis, chip, size):
    if axis is None:
        return ref.at[chip]
    idx = [slice(None)] * len(ref.shape)
    idx[axis] = pl.ds(pl.multiple_of(chip * size, size), size)
    return ref.at[tuple(idx)]


def _remote(src, dst, send_sem, recv_sem, peer):
    return pltpu.make_async_remote_copy(src_ref=src, dst_ref=dst, send_sem=send_sem, recv_sem=recv_sem, device_id=peer,
                                        device_id_type=MESH)


HBM_SPEC = pl.BlockSpec(memory_space=pltpu.HBM)
SEM_SPEC = pl.BlockSpec(memory_space=pltpu.SEMAPHORE)
ANY_SPEC = pl.BlockSpec(memory_space=pl.ANY)
DATAFLOW = pltpu.SideEffectType.DATAFLOW_SIDE_EFFECTING


def gather_piece(i, l, o, axis, size):
    return (i, lambda r, chip: r.at[l], o, lambda r, chip: _win(r, axis, chip, size))


def scatter_piece(i, o, axis, size):
    return (i, lambda r, chip: _win(r, axis, chip, size), o, lambda r, chip: r.at[chip])


def whole_piece(i):
    return (i, lambda r, chip: r, i, lambda r, chip: r)


def _copies(pieces, in_refs, out_refs, send, recv, sibling):
    me, peers = _chip_peers()
    if sibling:
        peers = [(me[0], me[1], 1 - me[2])]
    mine = _chip_index(me)
    remote = []
    for n, (i, src, o, dst) in enumerate(pieces):
        d = dst(out_refs[o], mine)
        remote += [_remote(src(in_refs[i], _chip_index(p)), d, send.at[len(peers) * n + k], recv.at[len(peers) * n + k], p)
                   for k, p in enumerate(peers)]
    return remote


def own_window(a, axis, size, chip):
    if axis is None:
        return lax.dynamic_index_in_dim(a, chip, 0, keepdims=False)
    return lax.dynamic_slice_in_dim(a, chip * size, size, axis=axis)


def place_own(land, own, axis, size, chip):
    if axis is None:
        return lax.dynamic_update_slice_in_dim(land, own[None], chip, axis=0)
    return lax.dynamic_update_slice_in_dim(land, own, chip * size, axis=axis)


def exchange_start(pieces, ins, out_shapes, after, name, sibling=False):
    n_in, n_out, ncp = len(ins), len(out_shapes), len(pieces)

    def body(*refs):
        in_refs, land_refs = refs[:n_in], refs[n_in:n_in + n_out]
        send, recv = refs[n_in + n_out + 1], refs[n_in + n_out + 2]
        token = refs[-1]
        for cp in _copies(pieces, in_refs, land_refs, send, recv, sibling):
            cp.start()
        token[...] = jnp.zeros_like(token)

    hbm = lambda a: pltpu.with_memory_space_constraint(a, pltpu.HBM)
    lands = [hbm(lax.empty(s.shape, s.dtype)) for s in out_shapes]
    sem = pltpu.SemaphoreType.DMA(((1 if sibling else 3) * ncp,))
    thru = [pltpu.HBM(a.shape, a.dtype) for a in ins] + [pltpu.HBM(s.shape, s.dtype) for s in out_shapes]
    res = _pc(body, in_specs=[HBM_SPEC] * (n_in + n_out) + [ANY_SPEC],
              out_specs=[SEM_SPEC, SEM_SPEC] + [HBM_SPEC] * (n_in + n_out) + [pl.BlockSpec(memory_space=pltpu.VMEM)],
              out_shape=[sem, sem] + thru + [_sds((8, LANES), F32)],
              input_output_aliases={i: 2 + i for i in range(n_in + n_out)},
              compiler_params=pltpu.CompilerParams(has_side_effects=DATAFLOW), name=name)(*[hbm(a) for a in ins], *lands, after)
    return (res[0], res[1]), list(res[2:2 + n_in]), list(res[2 + n_in:2 + n_in + n_out]), res[-1]


def exchange_wait(pieces, sems, ins, lands, after, name, sibling=False):
    n_in, n_out = len(ins), len(lands)

    def body(*refs):
        in_refs, land_refs = refs[:n_in], refs[n_in:n_in + n_out]
        send, recv = refs[n_in + n_out], refs[n_in + n_out + 1]
        for cp in _copies(pieces, in_refs, land_refs, send, recv, sibling):
            cp.wait_send()
            cp.wait_recv()

    thru = [pltpu.HBM(a.shape, a.dtype) for a in ins] + [pltpu.HBM(a.shape, a.dtype) for a in lands]
    res = _pc(body, in_specs=[HBM_SPEC] * (n_in + n_out) + [SEM_SPEC, SEM_SPEC, ANY_SPEC], out_specs=[HBM_SPEC] * (n_in + n_out),
              out_shape=thru, input_output_aliases={i: i for i in range(n_in + n_out)},
              compiler_params=pltpu.CompilerParams(has_side_effects=DATAFLOW), name=name)(*ins, *lands, sems[0], sems[1], after)
    return list(res[:n_in]), list(res[n_in:])


def exchange_all(buf, name):
    def body(in_ref, out_ref, send, recv, local):
        x, y, c = lax.axis_index("x"), lax.axis_index("y"), lax.axis_index("c")
        mine = 4 * x + 2 * y + c
        loc = pltpu.make_async_copy(in_ref, out_ref.at[mine], local)
        loc.start()
        copies = [loc]
        for k in range(1, 8):
            peer = (x ^ (k >> 2), y ^ ((k >> 1) & 1), c ^ (k & 1))
            cp = pltpu.make_async_remote_copy(src_ref=in_ref, dst_ref=out_ref.at[mine], send_sem=send.at[k - 1],
                                              recv_sem=recv.at[k - 1], device_id=peer, device_id_type=MESH)
            cp.start()
            copies.append(cp)
        for cp in copies:
            cp.wait()

    anyspec = pl.BlockSpec(memory_space=pl.ANY)
    return _pc(body, in_specs=[anyspec], out_specs=anyspec, out_shape=_sds((8,) + buf.shape, buf.dtype),
               scratch_shapes=[pltpu.SemaphoreType.DMA((7,)), pltpu.SemaphoreType.DMA((7,)), pltpu.SemaphoreType.DMA],
               name=name)(buf)


def _norm_fwd(x, g, name):
    return rowwise(f_rms, [(x, D, 0, 0)], [(g, D, 0, 0)], [(D, 0, BF16)], ts=1024, name=name)[0]


def _norm_bwd(x, g, dh, dres, name):
    (dx,), (dg,) = rowwise_bwd(f_rms, [(x, D, 0, 0)], [(g, D, 0, 0)], [(dh, D, 0, 0)], need=[True],
                               adds={0: (dres, D, 0, 0)}, ts=1024, name=name)
    return dx, dg


def pool_fwd(x, W, tag, late_grp=None, late_out=None):
    h = _norm_fwd(x, W["ng"], tag + "_norm")
    proj = mm(h, W["w_in"], out_dtype=BF16, name=tag + "_in")
    p = pool_time_fwd(proj, tag + "_win")
    w_grp = W["w_grp"] if late_grp is None else late_grp(proj)
    pg = gmm("nn", p, w_grp, G=4, out_dtype=BF16, name=tag + "_grp")
    y = rowwise(f_pool_gate, [(pg, POOL_GROUP, 0, 1), (proj, POOL_GROUP, 4, 1)], [(W["scale"], POOL_GROUP, 0, 1)],
                [(POOL_GROUP, 1, BF16)], ncol=4, ts=1024, name=tag + "_gate")[0]
    w_out = W["w_out"] if late_out is None else late_out(y)
    xn = mm(y, w_out, add=x, name=tag + "_out")
    return xn, (x, h, proj, p, pg, y)


def pool_bwd(dxn, W, saved, tag, after=None, emit=None):
    x, h, proj, p, pg, y = saved
    emit = emit or (lambda grads: None)
    dy = mm(dxn, W["w_out"], tb=True, after=after, out_dtype=BF16, name=tag + "_dy")
    g = {}
    (dpg, dproj), (g["scale"],) = rowwise_bwd(
        f_pool_gate, [(pg, POOL_GROUP, 0, 1), (proj, POOL_GROUP, 4, 1)], [(W["scale"], POOL_GROUP, 0, 1)],
        [(dy, POOL_GROUP, 0, 1)], need=[True, True], place={1: (2 * POOL_WIDTH, 4)}, narrow=(0, 1), ncol=4, ts=1024, name=tag + "_dgate")
    dp = gmm("nt", dpg, W["w_grp"], G=4, out_dtype=BF16, name=tag + "_dp")
    dproj = pool_time_bwd(dp, dproj, tag + "_dwin")
    g["w_in"] = mm(h, dproj, ta=True, out_dtype=BF16, name=tag + "_dw_in")
    t1 = emit({"w_in": g["w_in"]})
    g["w_out"] = mm(y, dxn, ta=True, after=t1, out_dtype=BF16, name=tag + "_dwout")
    g["w_grp"] = gmm("tn", p, dpg, G=4, out_dtype=BF16, name=tag + "_dwgrp")
    t2 = emit({"w_out": g["w_out"], "w_grp": g["w_grp"]})
    dh = mm(dproj, W["w_in"], tb=True, after=t2, name=tag + "_dh")
    dx, g["ng"] = _norm_bwd(x, W["ng"], dh, dxn, tag + "_dnorm")
    return dx, g


def gdn_fwd(x, W, tag, late=None):
    h = _norm_fwd(x, W["ng"], tag + "_norm")
    proj = mm(h, W["w_in"], name=tag + "_in")
    qkv = gdn_conv_fwd(proj, W["conv"], tag + "_conv")
    g_b, beta_b = rowwise(f_gdn_gates, [(proj, LANES, 6144 // LANES, 0)], [(W["a_log"], LANES, 0, 0), (W["dt_bias"], LANES, 0, 0)],
                          [(GDN_QK, 0, F32), (GDN_QK, 0, F32)], ts=1024, name=tag + "_gates")
    o, states = gdn_chunk_fwd(qkv, g_b, beta_b, tag + "_chunk")
    og = rowwise(f_gdn_out, [(o, GDN_DV, 0, 1), (proj, GDN_DV, 4096 // GDN_DV, 1)], [(W["norm_g"], GDN_DV, 0, 0)],
                 [(GDN_DV, 1, BF16)], ncol=GDN_H, ts=2048, name=tag + "_onorm")[0]
    if late is not None:
        W = dict(W, **late(og))
    xn = mm(og, W["w_out"], add=x, name=tag + "_out")
    return xn, (x, h, proj, qkv, g_b, beta_b, o, states, og)


def gdn_bwd(dxn, W, saved, tag, after=None):
    x, h, proj, qkv, g_b, beta_b, o, states, og = saved
    dog = mm(dxn, W["w_out"], tb=True, after=after, name=tag + "_dog")
    g = {"w_out": mm(og, dxn, ta=True, out_dtype=BF16, name=tag + "_dwout")}
    (do, dproj), (g["norm_g"],) = rowwise_bwd(
        f_gdn_out, [(o, GDN_DV, 0, 1), (proj, GDN_DV, 4096 // GDN_DV, 1)], [(W["norm_g"], GDN_DV, 0, 0)],
        [(dog, GDN_DV, 0, 1)], need=[True, True], place={1: (GDN_IN_PAD, 4096 // GDN_DV)}, narrow=(1,), ncol=GDN_H, ts=2048, name=tag + "_donorm")
    dq, dk, dv, dg_b, dbeta_b = gdn_chunk_bwd(qkv, g_b, beta_b, states, do, tag + "_dchunk")
    (dproj,), (g["a_log"], g["dt_bias"]) = rowwise_bwd(
        f_gdn_gates, [(proj, LANES, 6144 // LANES, 0)], [(W["a_log"], LANES, 0, 0), (W["dt_bias"], LANES, 0, 0)],
        [(dg_b, GDN_QK, 0, 0), (dbeta_b, GDN_QK, 0, 0)], need=[True], place={0: (dproj, 6144 // LANES)}, ts=1024, name=tag + "_dgates")
    dproj, g["conv"] = gdn_conv_bwd(proj, W["conv"], dq, dk, dv, dproj, tag + "_dconv")
    dh = mm(dproj, W["w_in"], tb=True, name=tag + "_dh")
    g["w_in"] = mm(h, dproj, ta=True, out_dtype=BF16, name=tag + "_dw_in")
    dx, g["ng"] = _norm_bwd(x, W["ng"], dh, dxn, tag + "_dnorm")
    return dx, g


def mla_fwd(x, pos, W, tag):
    h = _norm_fwd(x, W["ng"], tag + "_norm")
    proj = mm(h, W["w_in"], out_dtype=BF16, name=tag + "_in")
    hq = rowwise(f_rms, [(proj, MLA_Q_LORA, 0, 0)], [(W["q_g"], MLA_Q_LORA, 0, 0)], [(MLA_Q_LORA, 0, BF16)], ts=1024, name=tag + "_qnorm")[0]
    hkv = rowwise(f_rms, [(proj, MLA_KV_LORA, 2, 0)], [(W["kv_g"], MLA_KV_LORA, 0, 0)], [(MLA_KV_LORA, 0, BF16)], ts=1024, name=tag + "_kvnorm")[0]
    qpad = mm(hq, W["w_uq"], out_dtype=BF16, name=tag + "_uq")
    kv = mm(hkv, W["w_ukv"], out_dtype=BF16, name=tag + "_ukv")
    qh, kh, vh = mla_prep_fwd(qpad, kv, proj, pos, W["rope"], tag + "_prep")
    o, lse = flash_fwd(qh, kh, vh, tag + "_attn")
    og = rowwise(f_ogate, [(o, 512, 0, 1), (proj, 512, 4, 1)], [], [(512, 1, BF16)], ncol=4, ts=1024, name=tag + "_ogate")[0]
    xn = mm(og, W["w_out"], add=x, name=tag + "_out")
    return xn, (x, h, proj, hq, hkv, qh, kh, vh, o, lse, og)


def mla_bwd(dxn, pos, W, saved, tag, after=None):
    x, h, proj, hq, hkv, qh, kh, vh, o, lse, og = saved
    dog = mm(dxn, W["w_out"], tb=True, after=after, out_dtype=BF16, name=tag + "_dog")
    g = {"w_out": mm(og, dxn, ta=True, out_dtype=BF16, name=tag + "_dwout")}
    dproj = jnp.zeros(proj.shape, BF16)
    (do, dproj), _ = rowwise_bwd(f_ogate, [(o, 512, 0, 1), (proj, 512, 4, 1)], [], [(dog, 512, 0, 1)], need=[True, True],
                                 place={1: (dproj, 4)}, narrow=(0,), ncol=4, ts=1024, name=tag + "_dogate")
    dqh, dkh, dvh = flash_bwd(qh, kh, vh, o, lse, do, tag + "_dattn")
    dqpad, dkv, dproj = mla_prep_bwd(dqh, dkh, dvh, pos, W["rope"], dproj, tag + "_dprep")
    dhq = mm(dqpad, W["w_uq"], tb=True, name=tag + "_dhq")
    g["w_uq"] = mm(hq, dqpad, ta=True, out_dtype=BF16, name=tag + "_dwuq")
    dhkv = mm(dkv, W["w_ukv"], tb=True, name=tag + "_dhkv")
    g["w_ukv"] = mm(hkv, dkv, ta=True, out_dtype=BF16, name=tag + "_dwukv")
    (dproj,), (g["q_g"],) = rowwise_bwd(f_rms, [(proj, MLA_Q_LORA, 0, 0)], [(W["q_g"], MLA_Q_LORA, 0, 0)], [(dhq, MLA_Q_LORA, 0, 0)],
                                        need=[True], place={0: (dproj, 0)}, ts=512, name=tag + "_dqnorm")
    (dproj,), (g["kv_g"],) = rowwise_bwd(f_rms, [(proj, MLA_KV_LORA, 2, 0)], [(W["kv_g"], MLA_KV_LORA, 0, 0)], [(dhkv, MLA_KV_LORA, 0, 0)],
                                         need=[True], place={0: (dproj, 2)}, ts=512, name=tag + "_dkvnorm")
    dh = mm(dproj, W["w_in"], tb=True, name=tag + "_dh")
    g["w_in"] = mm(h, dproj, ta=True, out_dtype=BF16, name=tag + "_dw_in")
    dx, g["ng"] = _norm_bwd(x, W["ng"], dh, dxn, tag + "_dnorm")
    return dx, g


def _pad_cols(a, n):
    return jnp.pad(a, ((0, 0), (0, n - a.shape[1])))


def _mla_w_in_layout(w):
    z = lambda n: jnp.zeros((w.shape[0], n), w.dtype)
    kr = w[:, 1280:1344]
    return jnp.concatenate([w[:, :768], z(256), w[:, 768:1280], kr[:, :32], z(32), kr[:, 32:], z(32), z(384), w[:, 1344:]], axis=1)


def _mla_w_in_unlayout(g):
    return jnp.concatenate([g[:, :768], g[:, 1024:1536], g[:, 1536:1568], g[:, 1600:1632], g[:, 2048:]], axis=1)


def _mla_w_uq_layout(w):
    w3 = w.reshape(w.shape[0], MLA_H, MLA_NOPE + MLA_ROPE)
    z = jnp.zeros((w.shape[0], MLA_H, 32), w.dtype)
    return jnp.concatenate([w3[..., :128], w3[..., 128:160], z, w3[..., 160:192], z], axis=-1).reshape(w.shape[0], MLA_H * 256)


def _mla_w_uq_unlayout(g):
    g3 = g.reshape(g.shape[0], MLA_H, 256)
    return jnp.concatenate([g3[..., :128], g3[..., 128:160], g3[..., 192:224]], axis=-1).reshape(g.shape[0], MLA_H * 192)


def _rope_consts():
    half = MLA_ROPE // 2
    inv = ROPE_THETA ** (-jnp.arange(half, dtype=F32) / half)
    z = jnp.zeros((half,), F32)
    o = jnp.ones((half,), F32)
    row = lambda *p: jnp.concatenate(p).reshape(1, LANES)
    return row(inv, z, inv, z), row(o, z, o, z), row(-o, z, o, z)


BIG = ["pool_w_in", "pool_w_grp", "pool_w_out", "gdn_w_in", "gdn_w_out", "mla_w_in", "mla_w_uq", "mla_w_ukv", "mla_w_out"]
BIG_LAYOUT = {"pool_w_in": (1, 1024, (1024, 4096)), "pool_w_grp": (1, 128, (4, 512, 512)), "pool_w_out": (0, 512, (2048, 1024)),
              "gdn_w_in": (None, None, (4, 1024, 1540)), "gdn_w_out": (0, 512, (2048, 1024)),
              "mla_w_in": (None, None, (4, 1024, 848)), "mla_w_uq": (1, 768, (768, 3072)), "mla_w_ukv": (1, 1024, (512, 4096)),
              "mla_w_out": (0, 512, (2048, 1024))}
SMALL_SHARDED = ["pool_scale", "gdn_conv", "mla_q_norm_g", "mla_kv_norm_g"]
SMALL_AXIS = {"pool_scale": 1, "gdn_conv": 2, "mla_q_norm_g": 1, "mla_kv_norm_g": 1}
REPLICATED = ["norm_g", "gdn_a_log", "gdn_dt_bias", "gdn_norm_g", "final_g"]
PACK_C = 1024


def _pack(parts, dtype, row_mult):
    flat = jnp.concatenate([p.reshape(-1).astype(dtype) for p in parts])
    rows = -(-flat.shape[0] // PACK_C)
    rows = -(-rows // row_mult) * row_mult
    return jnp.pad(flat, (0, rows * PACK_C - flat.shape[0])).reshape(rows, PACK_C)


def _unpack(buf, shapes):
    lead = buf.shape[:-2]
    flat = buf.reshape(lead + (-1,))
    out, off = [], 0
    for s in shapes:
        n = int(np.prod(s))
        out.append(flat[..., off:off + n].reshape(lead + tuple(s)))
        off += n
    return out


def _unshard(g4, axis):
    a = jnp.moveaxis(g4, 0, axis)
    s = a.shape
    return a.reshape(s[:axis] + (s[axis] * s[axis + 1],) + s[axis + 2:])


def _to_shards(a, axis):
    s = a.shape
    return jnp.moveaxis(a.reshape(s[:axis] + (4, s[axis] // 4) + s[axis + 1:]), axis, 0)


def layer_weights(full, small, rep, layer):
    ng = rep["norm_g"][layer:layer + 1]
    side_by_side = lambda a4: jnp.moveaxis(a4, 0, 1).reshape(a4.shape[1], 4 * a4.shape[2])
    if layer in (0, 3):
        j = layer // 3
        return dict(ng=ng, w_in=full[("pool_w_in", j)], w_grp=full[("pool_w_grp", j)], scale=small["pool_scale"][j:j + 1],
                    w_out=full[("pool_w_out", j)])
    if layer == 1:
        return dict(ng=ng, w_in=_pad_cols(side_by_side(full[("gdn_w_in", 0)]), GDN_IN_PAD),
                    conv=jnp.pad(small["gdn_conv"][0], ((0, 4), (0, 0))), a_log=_pad_cols(rep["gdn_a_log"], LANES),
                    dt_bias=_pad_cols(rep["gdn_dt_bias"], LANES), norm_g=rep["gdn_norm_g"], w_out=full.get(("gdn_w_out", 0)))
    return dict(ng=ng, w_in=_mla_w_in_layout(side_by_side(full[("mla_w_in", 0)])), q_g=small["mla_q_norm_g"],
                kv_g=small["mla_kv_norm_g"], w_uq=_mla_w_uq_layout(full[("mla_w_uq", 0)]), w_ukv=full[("mla_w_ukv", 0)],
                w_out=full[("mla_w_out", 0)], rope=_rope_consts())


def big_grad_pieces(gl):
    g0, g1, g2, g3 = gl
    slots = lambda a: jnp.moveaxis(a.reshape(a.shape[0], 4, a.shape[1] // 4), 1, 0)
    out = {}
    for l, g in ((0, g0), (1, g3)):
        if g is not None:
            out.update({("pool_w_in", l): g["w_in"], ("pool_w_grp", l): g["w_grp"], ("pool_w_out", l): g["w_out"]})
    if g1 is not None:
        out.update({("gdn_w_in", 0): slots(g1["w_in"][:, :GDN_IN]), ("gdn_w_out", 0): g1["w_out"]})
    if g2 is not None:
        out.update({("mla_w_in", 0): slots(_mla_w_in_unlayout(g2["w_in"])), ("mla_w_uq", 0): _mla_w_uq_unlayout(g2["w_uq"]),
                    ("mla_w_ukv", 0): g2["w_ukv"], ("mla_w_out", 0): g2["w_out"]})
    return out


def small_grads(gl, dfinal):
    g0, g1, g2, g3 = gl
    return {"norm_g": jnp.concatenate([g0["ng"], g1["ng"], g2["ng"], g3["ng"]], axis=0),
            "pool_scale": jnp.concatenate([g0["scale"], g3["scale"]], axis=0), "gdn_conv": g1["conv"][None, :4],
            "gdn_a_log": g1["a_log"][:, :GDN_H], "gdn_dt_bias": g1["dt_bias"][:, :GDN_H], "gdn_norm_g": g1["norm_g"],
            "mla_q_norm_g": g2["q_g"], "mla_kv_norm_g": g2["kv_g"], "final_g": dfinal.reshape(D)}


NAMES = ["norm_g", "pool_w_in", "pool_w_grp", "pool_scale", "pool_w_out", "gdn_w_in", "gdn_conv", "gdn_a_log", "gdn_dt_bias",
         "gdn_norm_g", "gdn_w_out", "mla_w_in", "mla_q_norm_g", "mla_w_uq", "mla_kv_norm_g", "mla_w_ukv", "mla_w_out", "final_g"]


def kernel(x, positions, norm_g, pool_w_in, pool_w_grp, pool_scale, pool_w_out, gdn_w_in, gdn_conv, gdn_a_log, gdn_dt_bias, gdn_norm_g, gdn_w_out, mla_w_in, mla_q_norm_g, mla_w_uq, mla_kv_norm_g, mla_w_ukv, mla_w_out, final_g, loss_target, m_norm_g, m_pool_w_in, m_pool_w_grp, m_pool_scale, m_pool_w_out, m_gdn_w_in, m_gdn_conv, m_gdn_a_log, m_gdn_dt_bias, m_gdn_norm_g, m_gdn_w_out, m_mla_w_in, m_mla_q_norm_g, m_mla_w_uq, m_mla_kv_norm_g, m_mla_w_ukv, m_mla_w_out, m_final_g, v_norm_g, v_pool_w_in, v_pool_w_grp, v_pool_scale, v_pool_w_out, v_gdn_w_in, v_gdn_conv, v_gdn_a_log, v_gdn_dt_bias, v_gdn_norm_g, v_gdn_w_out, v_mla_w_in, v_mla_q_norm_g, v_mla_w_uq, v_mla_kv_norm_g, v_mla_w_ukv, v_mla_w_out, v_final_g):
    args = locals()
    w = {n: args[n] for n in NAMES}
    m = {n: args["m_" + n] for n in NAMES}
    v = {n: args["v_" + n] for n in NAMES}
    my_chip = (2 * lax.axis_index("x") + lax.axis_index("y")).astype(I32)
    S_ = x.shape[1]
    x0, pos, target = x[0], positions.reshape(S_, 1).astype(F32), loss_target[0]
    rep = {n: w[n] for n in REPLICATED}

    small_shapes = [w[n].shape for n in SMALL_SHARDED]
    small_pack = _pack([w[n] for n in SMALL_SHARDED], F32, 8)[None]
    layout = dict(BIG_LAYOUT, small=(None, None, (4,) + small_pack.shape[1:]))

    def shard(key, token):
        n, l = key
        if n == "small":
            return small_pack
        a = w[n][l:l + 1]
        return (a if token is None else a + token[0, 0]).astype(BF16)

    def gather_start(group, after, tag, token=None):
        pieces = [gather_piece(i, 0, i, layout[n][0], layout[n][1]) for i, (n, l) in enumerate(group)]
        ins = [shard(k, token) for k in group]
        shapes = [_sds(layout[n][2], a.dtype) for (n, l), a in zip(group, ins)]
        sems, ins, lands, token = exchange_start(pieces, ins, shapes, after, tag + "_start")
        return (pieces, sems, ins, lands), token

    def finish(handle, after, tag):
        return exchange_wait(*handle, after, tag + "_wait")

    def gathered(group, handle, after, tag):
        srcs, lands = finish(handle, after, tag)
        return {(n, l): place_own(a, s[0], layout[n][0], layout[n][1], my_chip) for (n, l), s, a in zip(group, srcs, lands)}

    group_a = [("small", 0), ("pool_w_in", 0)]
    group_a2 = [("pool_w_grp", 0)]
    group_b = [("gdn_w_in", 0)]
    group_b2 = [("pool_w_out", 0)]
    group_c = [("gdn_w_out", 0), ("mla_w_in", 0), ("mla_w_uq", 0), ("mla_w_ukv", 0), ("mla_w_out", 0), ("pool_w_in", 1),
               ("pool_w_grp", 1), ("pool_w_out", 1)]
    full = {}
    h_a, t_a = gather_start(group_a, x0, "gather_a")
    h_a2, t_a2 = gather_start(group_a2, t_a, "gather_a2", t_a)
    h_b, t_b = gather_start(group_b, t_a2, "gather_b", t_a2)
    h_b2, t_b2 = gather_start(group_b2, t_b, "gather_b2", t_b)
    h_c, t_c = gather_start(group_c, t_b2, "gather_c", t_b2)
    full.update(gathered(group_a, h_a, t_c, "gather_a"))
    small = {n: _unshard(a, SMALL_AXIS[n]) for n, a in zip(SMALL_SHARDED, _unpack(full[("small", 0)], small_shapes))}

    def late_grp(proj):
        full.update(gathered(group_a2, h_a2, proj, "gather_a2"))
        return full[("pool_w_grp", 0)]

    def late_out(y):
        full.update(gathered(group_b2, h_b2, y, "gather_b2"))
        return full[("pool_w_out", 0)]

    first = dict(ng=rep["norm_g"][0:1] + t_c[0:1, 0:1], w_in=full[("pool_w_in", 0)], scale=small["pool_scale"][0:1])
    x1, s0 = pool_fwd(x0, first, "l0", late_grp=late_grp, late_out=late_out)
    W0 = layer_weights(full, small, rep, 0)
    full.update(gathered(group_b, h_b, x1, "gather_b"))

    def late_l1(og):
        full.update(gathered(group_c, h_c, og, "gather_c"))
        return dict(w_out=full[("gdn_w_out", 0)])

    x2, s1 = gdn_fwd(x1, layer_weights(full, small, rep, 1), "l1", late=late_l1)
    W1, W2, W3 = (layer_weights(full, small, rep, i) for i in (1, 2, 3))
    x3, s2 = mla_fwd(x2, pos, W2, "l2")
    x4, s3 = pool_fwd(x3, W3, "l3")
    loss_part, dx4, dfinal = loss_head(x4, target, final_g.reshape(1, D), "loss_head")

    def scatter_start(pieces_of, after, tag):
        keys = list(pieces_of)
        pieces = [scatter_piece(i, i, BIG_LAYOUT[n][0], BIG_LAYOUT[n][1]) for i, (n, l) in enumerate(keys)]
        shapes = [_sds((4,) + tuple(w[n].shape[1:]), BF16) for n, l in keys]
        sems, ins, lands, token = exchange_start(pieces, [pieces_of[k] for k in keys], shapes, after, tag + "_start")
        return keys, (pieces, sems, ins, lands), token

    def scattered(keys, handle, after, tag):
        srcs, lands = finish(handle, after, tag)
        return {(n, l): place_own(a, own_window(g, BIG_LAYOUT[n][0], BIG_LAYOUT[n][1], my_chip), None, None, my_chip)
                for (n, l), g, a in zip(keys, srcs, lands)}

    dx3, g3 = pool_bwd(dx4, W3, s3, "l3")
    k3, h3, t3 = scatter_start(big_grad_pieces((None, None, None, g3)), dx3, "scatter_l3")
    dx2, g2 = mla_bwd(dx3, pos, W2, s2, "l2", after=t3)
    k2, h2, t2 = scatter_start(big_grad_pieces((None, None, g2, None)), dx2, "scatter_l2")
    dx1, g1 = gdn_bwd(dx2, W1, s1, "l1", after=t2)
    k1, h1, t1 = scatter_start(big_grad_pieces((None, g1, None, None)), dx1, "scatter_l1")
    def swap_start(part, tag):
        keys = list(part)
        ins = [part[k] for k in keys]
        pieces = [whole_piece(i) for i in range(len(keys))]
        sems, ins, lands, token = exchange_start(pieces, ins, [_sds(a.shape, a.dtype) for a in ins], ins[0], tag + "_start", sibling=True)
        swaps.append((keys, (pieces, sems, ins, lands), tag))
        return token

    last, swaps = [], []

    def emit_l0(grads):
        first = not last
        now = next(iter(grads.values()))
        early = [(k3, h3, "scatter_l3"), (k2, h2, "scatter_l2")] if first else [(k1, h1, "scatter_l1")]
        landed = {}
        for keys, handle, tag in early:
            landed.update(scattered(keys, handle, now, tag))
        swapping = swap_start(landed, "swap_a" if first else "swap_b")
        tag = "scatter_l0a" if first else "scatter_l0b"
        keys, handle, token = scatter_start({("pool_" + k, 0): a for k, a in grads.items()}, swapping, tag)
        last.append((keys, handle, tag))
        return token

    dx0, g0 = pool_bwd(dx1, W0, s0, "l0", after=t1, emit=emit_l0)
    landed = {}
    for keys, handle, tag in last:
        landed.update(scattered(keys, handle, dx0, tag))
    swapping = swap_start(landed, "swap_c")
    recv, sib = {}, {}
    for keys, handle, tag in swaps:
        mine_, theirs = exchange_wait(*handle, dx0, tag + "_wait", sibling=True)
        recv.update(zip(keys, mine_))
        sib.update(zip(keys, theirs))

    sg = small_grads((g0, g1, g2, g3), dfinal)
    small_names = SMALL_SHARDED + REPLICATED
    small_buf = _pack([sg[n] for n in small_names] + [loss_part], F32, 8)
    small_sum = sum_slots(exchange_all(small_buf, "gather_small"), "sum_small")
    full_small = _unpack(small_sum, [sg[n].shape for n in small_names] + [(1, LANES)])
    loss = full_small[-1][0, 0]
    small_part = {}
    for n, a in zip(small_names, full_small[:-1]):
        if n in SMALL_AXIS:
            a = lax.dynamic_index_in_dim(_to_shards(a, SMALL_AXIS[n]), my_chip, axis=0, keepdims=False)
        small_part[n] = a

    outs = []
    for n in NAMES:
        shp = w[n].shape
        two = (int(np.prod(shp[:-1])), shp[-1]) if len(shp) > 1 else (1, shp[0])
        if n in BIG_LAYOUT:
            layers = shp[0]
            rows = lambda a: a.reshape(4, two[0] // layers, two[1])
            parts = [[rows(recv[(n, l)]) for l in range(layers)], [rows(sib[(n, l)]) for l in range(layers)]]
        else:
            parts = [[small_part[n].reshape((1,) + two)]]
        upper = n in BIG_LAYOUT and not n.startswith("pool")
        res = adamw(w[n].reshape(two), parts, m[n].reshape(two), v[n].reshape(two), "adamw_" + n, after=swapping if upper else None)
        outs.append([r.reshape(shp) for r in res])
    return (loss, dx0[None], *[o[0] for o in outs], *[o[1] for o in outs], *[o[2] for o in outs], *[o[3] for o in outs])
```

```python
import math

import jax
import jax.numpy as jnp
import numpy as np
from jax import lax
from jax.experimental import pallas as pl
from jax.experimental.pallas import tpu as pltpu

F32 = jnp.float32
BF16 = jnp.bfloat16
I32 = jnp.int32

D = 1024
EPS = 1e-6
POOL_WIDTH = 2048
POOL_GROUP = 512
GDN_H, GDN_DK, GDN_DV, GDN_C = 8, 128, 256, 64
GDN_QK, GDN_V, GDN_CONV_CH, GDN_IN = 1024, 2048, 4096, 6160
GDN_IN_PAD = 6272
MLA_H, MLA_NOPE, MLA_ROPE, MLA_V = 16, 128, 64, 128
MLA_Q_LORA, MLA_KV_LORA, MLA_WIDTH, MLA_IN = 768, 512, 2048, 3392
MLA_IN_PAD = 4096
MLA_SCALE = (MLA_NOPE + MLA_ROPE) ** -0.5
ROPE_THETA = 10000.0
ADAM_LR, ADAM_B1, ADAM_B2, ADAM_EPS, ADAM_WD, ADAM_STEP = 0.001, 0.9, 0.999, 1e-08, 0.01, 10

VMEM_LIMIT_V7X = 56 * 1024 * 1024
LANES = 128
MESH = pl.DeviceIdType.MESH


def _pc(body, **kw):
    return pl.pallas_call(body, **kw)


def _cparams(sem):
    return pltpu.CompilerParams(dimension_semantics=sem, vmem_limit_bytes=VMEM_LIMIT_V7X)


def _tile(n, cap):
    t = (cap // LANES) * LANES
    while t >= LANES:
        if n % t == 0:
            return t
        t -= LANES
    return n


def _sds(shape, dt):
    return jax.ShapeDtypeStruct(shape, dt)


def mm(a, b, *, ta=False, tb=False, add=None, after=None, out_dtype=F32, name):
    if ta:
        K, M = a.shape
    else:
        M, K = a.shape
    if tb:
        N, K2 = b.shape
    else:
        K2, N = b.shape
    assert K == K2, (a.shape, b.shape, ta, tb)
    tm, tn, tk = _tile(M, 1024), _tile(N, 1024), _tile(K, 1024)
    nk = K // tk
    a_spec = pl.BlockSpec((tk, tm), lambda i, j, k: (k, i)) if ta else pl.BlockSpec((tm, tk), lambda i, j, k: (i, k))
    b_spec = pl.BlockSpec((tn, tk), lambda i, j, k: (j, k)) if tb else pl.BlockSpec((tk, tn), lambda i, j, k: (k, j))
    o_spec = pl.BlockSpec((tm, tn), lambda i, j, k: (i, j))
    dn = (((0 if ta else 1,), (1 if tb else 0,)), ((), ()))
    has_add = add is not None

    def body(*refs):
        a_ref, b_ref = refs[0], refs[1]
        part = lax.dot_general(a_ref[...].astype(BF16), b_ref[...].astype(BF16), dn, preferred_element_type=F32)
        if nk == 1:
            refs[-1][...] = (part + refs[2][...] if has_add else part).astype(out_dtype)
            return
        o_ref, acc = refs[-2], refs[-1]
        k = pl.program_id(2)

        @pl.when(k == 0)
        def _():
            acc[...] = part

        @pl.when(k > 0)
        def _():
            acc[...] += part

        @pl.when(k == nk - 1)
        def _():
            r = acc[...]
            if has_add:
                r = r + refs[2][...]
            o_ref[...] = r.astype(out_dtype)

    ins = [a, b] + ([add] if has_add else []) + ([after] if after is not None else [])
    specs = [a_spec, b_spec] + ([o_spec] if has_add else []) + ([pl.BlockSpec(memory_space=pl.ANY)] if after is not None else [])
    return _pc(body, grid=(M // tm, N // tn, nk), in_specs=specs, out_specs=o_spec, out_shape=_sds((M, N), out_dtype),
               scratch_shapes=[pltpu.VMEM((tm, tn), F32)] if nk > 1 else [], compiler_params=_cparams(("parallel", "parallel", "arbitrary")),
               name=name)(*ins)


def gmm(kind, a, b, *, G, name, out_dtype=F32):
    S_ = a.shape[0]
    Ka = a.shape[1] // G
    if kind == "tn":
        N = b.shape[1] // G
        tk = _tile(S_, 2048)
        nk = S_ // tk

        def body(a_ref, b_ref, o_ref, acc):
            k = pl.program_id(1)

            @pl.when(k == 0)
            def _():
                acc[...] = jnp.zeros_like(acc)

            acc[...] += lax.dot_general(a_ref[...].astype(BF16), b_ref[...].astype(BF16), (((0,), (0,)), ((), ())),
                                        preferred_element_type=F32)

            @pl.when(k == nk - 1)
            def _():
                o_ref[...] = acc[...].astype(out_dtype)

        return _pc(body, grid=(G, nk),
                   in_specs=[pl.BlockSpec((tk, Ka), lambda g, k: (k, g)), pl.BlockSpec((tk, N), lambda g, k: (k, g))],
                   out_specs=pl.BlockSpec((None, Ka, N), lambda g, k: (g, 0, 0)), out_shape=_sds((G, Ka, N), out_dtype),
                   scratch_shapes=[pltpu.VMEM((Ka, N), F32)], compiler_params=_cparams(("parallel", "arbitrary")), name=name)(a, b)
    N = b.shape[2] if kind == "nn" else b.shape[1]
    tm = _tile(S_, 4096)
    dn = (((1,), (0 if kind == "nn" else 1,)), ((), ()))

    def body(a_ref, b_ref, o_ref):
        o_ref[...] = lax.dot_general(a_ref[...].astype(BF16), b_ref[...].astype(BF16), dn, preferred_element_type=F32).astype(out_dtype)

    bshape = (None,) + tuple(b.shape[1:])
    return _pc(body, grid=(G, S_ // tm),
               in_specs=[pl.BlockSpec((tm, Ka), lambda g, i: (i, g)), pl.BlockSpec(bshape, lambda g, i: (g, 0, 0))],
               out_specs=pl.BlockSpec((tm, N), lambda g, i: (i, g)), out_shape=_sds((S_, G * N), out_dtype),
               compiler_params=_cparams(("parallel", "parallel")), name=name)(a, b)


def _rw_spec(ts, w, c, s):
    return pl.BlockSpec((ts, w), lambda j, i: (i, c + j * s))


def _rw_pspec(p, w, c, s):
    return pl.BlockSpec((p.shape[0], w), lambda j, i: (0, c + j * s))


def rowwise(f, tiles, params, outs, *, ncol=1, ts, name):
    S_ = tiles[0][0].shape[0]
    nin = len(tiles) + len(params)

    def body(*refs):
        res = f(pl.program_id(0), *[r[...].astype(F32) for r in refs[:nin]])
        for r, o in zip(refs[nin:], res):
            r[...] = o.astype(r.dtype)

    return _pc(body, grid=(ncol, S_ // ts),
               in_specs=[_rw_spec(ts, w, c, s) for (_, w, c, s) in tiles] + [_rw_pspec(*p) for p in params],
               out_specs=[_rw_spec(ts, w, 0, s) for (w, s, _) in outs],
               out_shape=[_sds((S_, w * (ncol if s else 1)), dt) for (w, s, dt) in outs],
               compiler_params=_cparams(("parallel", "parallel")), name=name)(*[t[0] for t in tiles], *[p[0] for p in params])


def rowwise_bwd(f, tiles, params, cots, *, need, adds=None, place=None, narrow=(), ncol=1, ts, name):
    S_ = tiles[0][0].shape[0]
    adds = adds or {}
    place = place or {}
    nt, npar, nc = len(tiles), len(params), len(cots)
    add_keys = sorted(adds)
    need_idx = [k for k in range(nt) if need[k]]
    into_keys = [k for k in need_idx if k in place and not isinstance(place[k][0], int)]
    n_extra = len(add_keys) + len(into_keys)

    def body(*refs):
        j, i = pl.program_id(0), pl.program_id(1)
        vals = [r[...].astype(F32) for r in refs[:nt + npar]]
        cvals = tuple(r[...].astype(F32) for r in refs[nt + npar:nt + npar + nc])
        add_refs = refs[nt + npar + nc:nt + npar + nc + len(add_keys)]
        out_refs = refs[nt + npar + nc + n_extra:]
        _, vjp = jax.vjp(lambda *v: tuple(f(j, *v)), *vals)
        grads = vjp(cvals)
        for n, k in enumerate(need_idx):
            g = grads[k]
            if k in adds:
                g = g + add_refs[add_keys.index(k)][...]
            out_refs[n][...] = g.astype(out_refs[n].dtype)
        for n in range(npar):
            ref = out_refs[len(need_idx) + n]
            first = (i == 0) if params[n][3] else jnp.logical_and(i == 0, j == 0)

            @pl.when(first)
            def _():
                ref[...] = jnp.zeros_like(ref)

            ref[...] += grads[nt + n]

    in_specs = ([_rw_spec(ts, w, c, s) for (_, w, c, s) in tiles] + [_rw_pspec(*p) for p in params]
                + [_rw_spec(ts, w, c, s) for (_, w, c, s) in cots] + [_rw_spec(ts, *adds[k][1:]) for k in add_keys]
                + [pl.BlockSpec(memory_space=pl.ANY) for _ in into_keys])
    out_specs, out_shape, aliases = [], [], {}
    for n, k in enumerate(need_idx):
        w, s = tiles[k][1], tiles[k][3]
        if k in place:
            dst, c0 = place[k]
            total = dst if isinstance(dst, int) else dst.shape[1]
            out_specs.append(_rw_spec(ts, w, c0, s))
            out_shape.append(_sds((S_, total), (BF16 if k in narrow else F32) if isinstance(dst, int) else dst.dtype))
            if k in into_keys:
                aliases[nt + npar + nc + len(add_keys) + into_keys.index(k)] = n
        else:
            out_specs.append(_rw_spec(ts, w, 0, s))
            out_shape.append(_sds((S_, w * (ncol if s else 1)), BF16 if k in narrow else F32))
    out_specs += [_rw_pspec(p[0], p[1], p[2], p[3]) for p in params]
    out_shape += [_sds(p[0].shape, F32) for p in params]
    res = _pc(body, grid=(ncol, S_ // ts), in_specs=in_specs, out_specs=out_specs, out_shape=out_shape,
              input_output_aliases=aliases, compiler_params=_cparams(("arbitrary", "arbitrary")), name=name)(
        *[t[0] for t in tiles], *[p[0] for p in params], *[c[0] for c in cots], *[adds[k][0] for k in add_keys],
        *[place[k][0] for k in into_keys])
    return list(res[:len(need_idx)]), list(res[len(need_idx):])


def _rms(x, g):
    r = lax.rsqrt(jnp.mean(x * x, axis=-1, keepdims=True) + EPS)
    return x * r * g


def _silu(x):
    return x * jax.nn.sigmoid(x)


@jax.custom_vjp
def _softplus(x):
    return jnp.maximum(x, 0.0) + jnp.log1p(jnp.exp(-jnp.abs(x)))


_softplus.defvjp(lambda x: (_softplus(x), x), lambda x, d: (d * jax.nn.sigmoid(x),))


def f_rms(j, x, g):
    return (_rms(x, g),)


def f_pool_gate(j, pg, gate, scale):
    return (pg * scale * _silu(gate),)


def f_ogate(j, o, gate):
    return (o * _silu(gate),)


def f_gdn_out(j, o, gate, g):
    return (_rms(o, g) * _silu(gate),)


def f_gdn_gates(j, ba, alog, dtb):
    lane = lax.broadcasted_iota(I32, (1, LANES), 1)
    gs, bs = [], []
    for h in range(GDN_H):
        eb = (lane == h).astype(F32)
        ea = (lane == GDN_H + h).astype(F32)
        b = jnp.sum(ba * eb, -1, keepdims=True)
        a = jnp.sum(ba * ea, -1, keepdims=True)
        al = jnp.sum(alog * eb, -1, keepdims=True)
        dt = jnp.sum(dtb * eb, -1, keepdims=True)
        g = -jnp.exp(al) * _softplus(a + dt)
        gs.append(jnp.broadcast_to(g, ba.shape))
        bs.append(jnp.broadcast_to(jax.nn.sigmoid(b), ba.shape))
    return jnp.concatenate(gs, 1), jnp.concatenate(bs, 1)


def _shift_dn(x, k):
    rows = lax.broadcasted_iota(I32, x.shape, 0)
    return jnp.where(rows < k, 0.0, pltpu.roll(x, k, 0))


def _shift_up(x, k):
    n = x.shape[0]
    rows = lax.broadcasted_iota(I32, x.shape, 0)
    return jnp.where(rows >= n - k, 0.0, pltpu.roll(x, n - k, 0))


def _pool_window(j):
    g = lax.div(j, POOL_GROUP // LANES)
    return jnp.where(g == 0, 2.0, jnp.where(g == 1, 4.0, jnp.where(g == 2, 8.0, 16.0))), g


def _pick(g, a2, a4, a8, a16):
    return jnp.where(g == 0, a2, jnp.where(g == 1, a4, jnp.where(g == 2, a8, a16)))


def pool_time_fwd(proj, name):
    S_ = proj.shape[0]

    def body(u_ref, p_ref):
        u = u_ref[...].astype(F32)
        w, g = _pool_window(pl.program_id(0))
        s2 = u + _shift_dn(u, 1)
        s4 = s2 + _shift_dn(s2, 2)
        s8 = s4 + _shift_dn(s4, 4)
        s16 = s8 + _shift_dn(s8, 8)
        t1 = (lax.broadcasted_iota(I32, u.shape, 0) + 1).astype(F32)
        p_ref[...] = (_pick(g, s2, s4, s8, s16) / jnp.minimum(t1, w) - u).astype(p_ref.dtype)

    return _pc(body, grid=(POOL_WIDTH // LANES,), in_specs=[pl.BlockSpec((S_, LANES), lambda j: (0, j))],
               out_specs=pl.BlockSpec((S_, LANES), lambda j: (0, j)), out_shape=_sds((S_, POOL_WIDTH), BF16),
               compiler_params=_cparams(("parallel",)), name=name)(proj)


def pool_time_bwd(dp, into, name):
    S_ = dp.shape[0]

    def body(dp_ref, _, du_ref):
        d = dp_ref[...].astype(F32)
        w, g = _pool_window(pl.program_id(0))
        t1 = (lax.broadcasted_iota(I32, d.shape, 0) + 1).astype(F32)
        q = d / jnp.minimum(t1, w)
        r2 = q + _shift_up(q, 1)
        r4 = r2 + _shift_up(r2, 2)
        r8 = r4 + _shift_up(r4, 4)
        r16 = r8 + _shift_up(r8, 8)
        du_ref[...] = (_pick(g, r2, r4, r8, r16) - d).astype(du_ref.dtype)

    return _pc(body, grid=(POOL_WIDTH // LANES,),
               in_specs=[pl.BlockSpec((S_, LANES), lambda j: (0, j)), pl.BlockSpec(memory_space=pl.ANY)],
               out_specs=pl.BlockSpec((S_, LANES), lambda j: (0, j)), out_shape=_sds(into.shape, into.dtype),
               input_output_aliases={1: 0}, compiler_params=_cparams(("parallel",)), name=name)(dp, into)


def _conv_post(j, a):
    n = a * lax.rsqrt(jnp.sum(a * a, axis=-1, keepdims=True) + EPS)
    nq = GDN_QK // LANES
    return jnp.where(j < nq, n * (GDN_DK ** -0.5), jnp.where(j < 2 * nq, n, a))


def _conv_taps(u):
    return [_shift_dn(u, 3), _shift_dn(u, 2), _shift_dn(u, 1), u]


def _conv_pre(taps, w):
    return w[0:1] * taps[0] + w[1:2] * taps[1] + w[2:3] * taps[2] + w[3:4] * taps[3]


def gdn_conv_fwd(proj, conv_w, name):
    S_ = proj.shape[0]

    def body(u_ref, w_ref, o_ref):
        o_ref[...] = _conv_post(pl.program_id(0), _silu(_conv_pre(_conv_taps(u_ref[...]), w_ref[...])))

    return _pc(body, grid=(GDN_CONV_CH // LANES,),
               in_specs=[pl.BlockSpec((S_, LANES), lambda j: (0, j)), pl.BlockSpec((8, LANES), lambda j: (0, j))],
               out_specs=pl.BlockSpec((S_, LANES), lambda j: (0, j)), out_shape=_sds((S_, GDN_CONV_CH), F32),
               compiler_params=_cparams(("parallel",)), name=name)(proj, conv_w)


def gdn_conv_bwd(proj, conv_w, dq, dk, dv, into, name):
    S_ = proj.shape[0]
    nq = GDN_QK // LANES

    def body(u_ref, w_ref, dq_ref, dk_ref, dv_ref, _, du_ref, dw_ref):
        j = pl.program_id(0)
        u, w = u_ref[...], w_ref[...]
        taps = _conv_taps(u)
        c = _conv_pre(taps, w)
        sig = jax.nn.sigmoid(c)
        dout = jnp.where(j < nq, dq_ref[...], jnp.where(j < 2 * nq, dk_ref[...], dv_ref[...]))
        _, vjp = jax.vjp(lambda a: _conv_post(j, a), c * sig)
        dc = vjp(dout)[0] * (sig * (1.0 + c * (1.0 - sig)))
        du = w[3:4] * dc + w[2:3] * _shift_up(dc, 1) + w[1:2] * _shift_up(dc, 2) + w[0:1] * _shift_up(dc, 3)
        du_ref[...] = du.astype(du_ref.dtype)
        rows = lax.broadcasted_iota(I32, (8, LANES), 0)
        dw = jnp.zeros((8, LANES), F32)
        for k in range(4):
            dw = dw + jnp.where(rows == k, jnp.sum(dc * taps[k], axis=0, keepdims=True), 0.0)
        dw_ref[...] = dw

    blk = lambda f: pl.BlockSpec((S_, LANES), f)
    return _pc(body, grid=(GDN_CONV_CH // LANES,),
               in_specs=[blk(lambda j: (0, j)), pl.BlockSpec((8, LANES), lambda j: (0, j)),
                         blk(lambda j: (0, jnp.minimum(j, nq - 1))), blk(lambda j: (0, jnp.clip(j - nq, 0, nq - 1))),
                         blk(lambda j: (0, jnp.clip(j - 2 * nq, 0, 2 * nq - 1))), pl.BlockSpec(memory_space=pl.ANY)],
               out_specs=[blk(lambda j: (0, j)), pl.BlockSpec((8, LANES), lambda j: (0, j))],
               out_shape=[_sds(into.shape, into.dtype), _sds((8, GDN_CONV_CH), F32)], input_output_aliases={5: 0},
               compiler_params=_cparams(("parallel",)), name=name)(proj, conv_w, dq, dk, dv, into)


_NN, _NT, _TN = ((1,), (0,)), ((1,), (1,)), ((0,), (0,))


def _split(x, n):
    parts = []
    for _ in range(n):
        h = x.astype(BF16)
        parts.append(h)
        x = x - h.astype(F32)
    return parts


def _dot(a, b, dn, mode):
    d = lambda p, q: lax.dot_general(p, q, (dn, ((), ())), preferred_element_type=F32)
    if mode == "lo":
        return d(a.astype(BF16), b.astype(BF16))
    if mode == "x3":
        (ah, al), (bh, bl) = _split(a, 2), _split(b, 2)
        return d(ah, bh) + (d(ah, bl) + d(al, bh))
    b0, b1, b2 = _split(b, 3)
    ab = a.astype(BF16)
    return d(ab, b0) + (d(ab, b1) + d(ab, b2))


def _make_dots(mode):
    @jax.custom_vjp
    def nn(a, b):
        return _dot(a, b, _NN, mode)

    @jax.custom_vjp
    def nt(a, b):
        return _dot(a, b, _NT, mode)

    @jax.custom_vjp
    def tn(a, b):
        return _dot(a, b, _TN, mode)

    nn.defvjp(lambda a, b: (nn(a, b), (a, b)), lambda r, d: (nt(d, r[1]), tn(r[0], d)))
    nt.defvjp(lambda a, b: (nt(a, b), (a, b)), lambda r, d: (nn(d, r[1]), tn(d, r[0])))
    tn.defvjp(lambda a, b: (tn(a, b), (a, b)), lambda r, d: (nt(r[1], d), nn(r[0], d)))
    return nn, nt, tn


_nn_hi, _nt_hi, _tn_hi = _make_dots("x3")
_nn_lo, _nt_lo, _tn_lo = _make_dots("lo")


@jax.custom_vjp
def _nn_const(a, b):
    return _dot(a, b, _NN, "xl")


_nn_const.defvjp(lambda a, b: (_nn_const(a, b), a), lambda a, d: (jnp.zeros_like(a), _dot(a, d, _TN, "xl")))


def _each(f, *lists):
    return [f(*xs) for xs in zip(*lists)]


@jax.custom_vjp
def _unit_inverses(xs):
    C = xs[0].shape[0]
    eye = (lax.broadcasted_iota(I32, (C, C), 0) == lax.broadcasted_iota(I32, (C, C), 1)).astype(F32)
    ainv, p = [eye + a for a in xs], xs
    for _ in range(int(math.log2(C)) - 1):
        p = _each(lambda a: _dot(a, a, _NN, "x3"), p)
        ainv = _each(lambda a, b: a + _dot(a, b, _NN, "x3"), ainv, p)
    return ainv


def _unit_inverses_bwd(ainv, d):
    left = _each(lambda a, g: _dot(a, g, _TN, "x3"), ainv, d)
    return (_each(lambda t, a: _dot(t, a, _NT, "x3"), left, ainv),)


_unit_inverses.defvjp(lambda xs: (lambda a: (a, a))(_unit_inverses(xs)), _unit_inverses_bwd)


def _gdn_chunk(q, k, v, gb, bb, state):
    C = GDN_C
    e0 = (lax.broadcasted_iota(I32, (1, LANES), 1) == 0).astype(F32)
    ri = lax.broadcasted_iota(I32, (C, C), 0)
    ci = lax.broadcasted_iota(I32, (C, C), 1)
    causal, strict = ri >= ci, ri > ci
    tri, eye, ones = causal.astype(F32), (ri == ci).astype(F32), jnp.ones((C, C), F32)
    last = lax.broadcasted_iota(I32, (C, LANES), 0) == C - 1
    g1 = _each(lambda a: jnp.sum(a * e0, -1, keepdims=True), gb)
    b1 = _each(lambda a: jnp.sum(a * e0, -1, keepdims=True), bb)
    gc_c = _each(lambda g: _nn_const(tri, jnp.broadcast_to(g, (C, C))), g1)
    gc_d = _each(lambda g: _nn_const(tri, jnp.broadcast_to(g, (C, LANES))), g1)
    gr_c = _each(lambda g: _nn_const(ones, eye * g), gc_c)
    decay = _each(lambda a, r: jnp.where(causal, jnp.exp(jnp.where(causal, a - r, 0.0)), 0.0), gc_c, gr_c)
    kb = _each(lambda a, b: a * b, k, b1)
    vb = _each(lambda a, b: a * b, v, b1)
    x = _each(lambda a, b, d: -jnp.where(strict, _nt_lo(a, b) * d, 0.0), kb, k, decay)
    ainv = _unit_inverses(x)
    u = _each(_nn_hi, ainv, vb)
    w = _each(lambda a, b, g: _nn_hi(a, b * jnp.exp(g)), ainv, kb, gc_d)
    attn = _each(lambda a, b, d: jnp.where(causal, _nt_lo(a, b) * d, 0.0), q, k, decay)
    v_new = _each(lambda a, b, s: a - _nn_lo(b, s), u, w, state)
    o = _each(lambda a, g, s, t, vn: _nn_lo(a * jnp.exp(g), s) + _nn_lo(t, vn), q, gc_d, state, attn, v_new)
    gl = _each(lambda g: jnp.sum(jnp.where(last, g, 0.0), axis=0, keepdims=True), gc_d)
    new_state = _each(lambda s, g, a, gd, vn: s * jnp.exp(jnp.sum(g * e0, -1, keepdims=True)) + _tn_lo(a * jnp.exp(g - gd), vn),
                      state, gl, k, gc_d, v_new)
    return o, new_state


def _head_slices(ref, width):
    return [ref[:, h * width:(h + 1) * width] for h in range(GDN_H)]


def gdn_chunk_fwd(qkv, g_b, beta_b, name):
    S_ = qkv.shape[0]
    N = S_ // GDN_C

    def body(q_ref, k_ref, v_ref, g_ref, b_ref, o_ref, s_ref, state):
        @pl.when(pl.program_id(0) == 0)
        def _():
            state[...] = jnp.zeros_like(state)

        st = [state[h] for h in range(GDN_H)]
        s_ref[0] = state[...]
        o, st2 = _gdn_chunk(_head_slices(q_ref, GDN_DK), _head_slices(k_ref, GDN_DK), _head_slices(v_ref, GDN_DV),
                            _head_slices(g_ref, GDN_DK), _head_slices(b_ref, GDN_DK), st)
        for h in range(GDN_H):
            o_ref[:, h * GDN_DV:(h + 1) * GDN_DV] = o[h]
            state[h] = st2[h]

    return _pc(body, grid=(N,),
               in_specs=[pl.BlockSpec((GDN_C, GDN_QK), lambda n: (n, 0)), pl.BlockSpec((GDN_C, GDN_QK), lambda n: (n, 1)),
                         pl.BlockSpec((GDN_C, GDN_V), lambda n: (n, 1)), pl.BlockSpec((GDN_C, GDN_QK), lambda n: (n, 0)),
                         pl.BlockSpec((GDN_C, GDN_QK), lambda n: (n, 0))],
               out_specs=[pl.BlockSpec((GDN_C, GDN_V), lambda n: (n, 0)),
                          pl.BlockSpec((1, GDN_H, GDN_DK, GDN_DV), lambda n: (n, 0, 0, 0))],
               out_shape=[_sds((S_, GDN_V), F32), _sds((N, GDN_H, GDN_DK, GDN_DV), F32)],
               scratch_shapes=[pltpu.VMEM((GDN_H, GDN_DK, GDN_DV), F32)],
               compiler_params=_cparams(("arbitrary",)), name=name)(qkv, qkv, qkv, g_b, beta_b)


def gdn_chunk_bwd(qkv, g_b, beta_b, states, do, name):
    S_ = qkv.shape[0]
    N = S_ // GDN_C

    def body(q_ref, k_ref, v_ref, g_ref, b_ref, s_ref, do_ref, dq_ref, dk_ref, dv_ref, dg_ref, db_ref, dstate):
        @pl.when(pl.program_id(0) == 0)
        def _():
            dstate[...] = jnp.zeros_like(dstate)

        _, vjp = jax.vjp(_gdn_chunk, _head_slices(q_ref, GDN_DK), _head_slices(k_ref, GDN_DK), _head_slices(v_ref, GDN_DV),
                         _head_slices(g_ref, GDN_DK), _head_slices(b_ref, GDN_DK), [s_ref[0, h] for h in range(GDN_H)])
        dq, dk, dv, dg, db, ds = vjp((_head_slices(do_ref, GDN_DV), [dstate[h] for h in range(GDN_H)]))
        for h in range(GDN_H):
            kk, vv = slice(h * GDN_DK, (h + 1) * GDN_DK), slice(h * GDN_DV, (h + 1) * GDN_DV)
            dq_ref[:, kk] = dq[h]
            dk_ref[:, kk] = dk[h]
            dv_ref[:, vv] = dv[h]
            dg_ref[:, kk] = dg[h]
            db_ref[:, kk] = db[h]
            dstate[h] = ds[h]

    r = lambda n: N - 1 - n
    qk = lambda c: pl.BlockSpec((GDN_C, GDN_QK), lambda n: (r(n), c))
    vs = lambda c: pl.BlockSpec((GDN_C, GDN_V), lambda n: (r(n), c))
    return _pc(body, grid=(N,),
               in_specs=[qk(0), qk(1), vs(1), qk(0), qk(0),
                         pl.BlockSpec((1, GDN_H, GDN_DK, GDN_DV), lambda n: (r(n), 0, 0, 0)), vs(0)],
               out_specs=[qk(0), qk(0), vs(0), qk(0), qk(0)],
               out_shape=[_sds((S_, GDN_QK), F32), _sds((S_, GDN_QK), F32), _sds((S_, GDN_V), F32),
                          _sds((S_, GDN_QK), F32), _sds((S_, GDN_QK), F32)],
               scratch_shapes=[pltpu.VMEM((GDN_H, GDN_DK, GDN_DV), F32)],
               compiler_params=_cparams(("arbitrary",)), name=name)(qkv, qkv, qkv, g_b, beta_b, states, do)


def _rope_tables(pos_ref, inv_ref, cm_ref, sg_ref):
    ang = pos_ref[...] * inv_ref[...]
    return jnp.cos(ang) * cm_ref[...], jnp.sin(ang) * sg_ref[...]


def mla_prep_fwd(qpad, kv, proj, pos, rope_consts, name):
    S_ = qpad.shape[0]
    ts = 512
    W = 2 * LANES

    def body(q_ref, kv_ref, kr_ref, pos_ref, inv_ref, cm_ref, sg_ref, qh_ref, kh_ref, vh_ref):
        cs, sn = _rope_tables(pos_ref, inv_ref, cm_ref, sg_ref)
        rope = lambda r: r * cs + pltpu.roll(r, LANES // 2, 1) * sn
        krr = rope(kr_ref[...].astype(F32)).astype(BF16)
        for h in range(MLA_H):
            qh_ref[h, :, 0:LANES] = (q_ref[:, h * W:h * W + LANES].astype(F32) * MLA_SCALE).astype(BF16)
            qh_ref[h, :, LANES:W] = (rope(q_ref[:, h * W + LANES:(h + 1) * W].astype(F32)) * MLA_SCALE).astype(BF16)
            kh_ref[h, :, 0:LANES] = kv_ref[:, h * W:h * W + LANES].astype(BF16)
            kh_ref[h, :, LANES:W] = krr
            vh_ref[h] = kv_ref[:, h * W + LANES:(h + 1) * W].astype(BF16)

    one = pl.BlockSpec((1, LANES), lambda i: (0, 0))
    return _pc(body, grid=(S_ // ts,),
               in_specs=[pl.BlockSpec((ts, MLA_H * W), lambda i: (i, 0)), pl.BlockSpec((ts, MLA_H * W), lambda i: (i, 0)),
                         pl.BlockSpec((ts, LANES), lambda i: (i, 1536 // LANES)), pl.BlockSpec((ts, 1), lambda i: (i, 0)),
                         one, one, one],
               out_specs=[pl.BlockSpec((MLA_H, ts, W), lambda i: (0, i, 0)), pl.BlockSpec((MLA_H, ts, W), lambda i: (0, i, 0)),
                          pl.BlockSpec((MLA_H, ts, LANES), lambda i: (0, i, 0))],
               out_shape=[_sds((MLA_H, S_, W), BF16), _sds((MLA_H, S_, W), BF16), _sds((MLA_H, S_, LANES), BF16)],
               compiler_params=_cparams(("parallel",)), name=name)(qpad, kv, proj, pos, *rope_consts)


def mla_prep_bwd(dqh, dkh, dvh, pos, rope_consts, into, name):
    S_ = dqh.shape[1]
    ts = 512
    W = 2 * LANES

    def body(dq_ref, dk_ref, dv_ref, pos_ref, inv_ref, cm_ref, sg_ref, _, dqp_ref, dkv_ref, dkr_ref):
        cs, sn = _rope_tables(pos_ref, inv_ref, cm_ref, sg_ref)
        rope_t = lambda g: g * cs + pltpu.roll(g * sn, LANES // 2, 1)
        acc = jnp.zeros((ts, LANES), F32)
        for h in range(MLA_H):
            dqp_ref[:, h * W:h * W + LANES] = (dq_ref[h, :, 0:LANES].astype(F32) * MLA_SCALE).astype(BF16)
            dqp_ref[:, h * W + LANES:(h + 1) * W] = (rope_t(dq_ref[h, :, LANES:W].astype(F32)) * MLA_SCALE).astype(BF16)
            dkv_ref[:, h * W:h * W + LANES] = dk_ref[h, :, 0:LANES]
            dkv_ref[:, h * W + LANES:(h + 1) * W] = dv_ref[h]
            acc = acc + dk_ref[h, :, LANES:W].astype(F32)
        dkr_ref[...] = rope_t(acc).astype(dkr_ref.dtype)

    one = pl.BlockSpec((1, LANES), lambda i: (0, 0))
    return _pc(body, grid=(S_ // ts,),
               in_specs=[pl.BlockSpec((MLA_H, ts, W), lambda i: (0, i, 0)), pl.BlockSpec((MLA_H, ts, W), lambda i: (0, i, 0)),
                         pl.BlockSpec((MLA_H, ts, LANES), lambda i: (0, i, 0)), pl.BlockSpec((ts, 1), lambda i: (i, 0)),
                         one, one, one, pl.BlockSpec(memory_space=pl.ANY)],
               out_specs=[pl.BlockSpec((ts, MLA_H * W), lambda i: (i, 0)), pl.BlockSpec((ts, MLA_H * W), lambda i: (i, 0)),
                          pl.BlockSpec((ts, LANES), lambda i: (i, 1536 // LANES))],
               out_shape=[_sds((S_, MLA_H * W), BF16), _sds((S_, MLA_H * W), BF16), _sds(into.shape, into.dtype)],
               input_output_aliases={7: 2}, compiler_params=_cparams(("parallel",)), name=name)(dqh, dkh, dvh, pos, *rope_consts, into)


NEG = -1e30


FLASH_TILE = 1024
FLASH_SUB = 256


def _scores(q, k, diagonal):
    s = lax.dot_general(q, k, (_NT, ((), ())), preferred_element_type=F32)
    if not diagonal:
        return s
    return jnp.where(lax.broadcasted_iota(I32, s.shape, 1) <= lax.broadcasted_iota(I32, s.shape, 0), s, NEG)


def _sub_blocks(t, diagonal):
    sub = min(FLASH_SUB, t) if diagonal else t
    return [(c * sub if diagonal else 0, slice(c * sub, (c + 1) * sub)) for c in range(t // sub)]


FLASH_HEADS = 2


def flash_fwd(qh, kh, vh, name):
    H, S_, W = qh.shape
    t = _tile(S_, FLASH_TILE)
    n = S_ // t
    G = FLASH_HEADS
    heads = list(range(G))

    def body(q_ref, k_ref, v_ref, o_ref, lse_ref, m_s, l_s, acc):
        qi, kj = pl.program_id(1), pl.program_id(2)

        @pl.when(kj == 0)
        def _():
            m_s[...] = jnp.full_like(m_s, NEG)
            l_s[...] = jnp.zeros_like(l_s)
            acc[...] = jnp.zeros_like(acc)

        def step(diagonal):
            s = _each(lambda a: _scores(q_ref[a], k_ref[a], diagonal), heads)
            m_old = _each(lambda a: m_s[a], heads)
            m_new = _each(lambda mo, sa: jnp.maximum(mo, jnp.max(sa, axis=-1, keepdims=True)), m_old, s)
            alpha = _each(lambda mo, mn: jnp.exp(mo - mn), m_old, m_new)
            p = _each(lambda sa, mn: jnp.exp(sa - mn[:, :1]), s, m_new)
            pv = _each(lambda pa, a: lax.dot_general(pa.astype(BF16), v_ref[a], (_NN, ((), ())), preferred_element_type=F32), p, heads)
            for a in heads:
                l_s[a] = alpha[a] * l_s[a] + jnp.sum(p[a], axis=-1, keepdims=True)
                acc[a] = alpha[a] * acc[a] + pv[a]
                m_s[a] = m_new[a]

        pl.when(kj < qi)(lambda: step(False))
        pl.when(kj == qi)(lambda: step(True))

        @pl.when(kj == n - 1)
        def _():
            for a in heads:
                o_ref[:, a * LANES:(a + 1) * LANES] = (acc[a] / l_s[a]).astype(o_ref.dtype)
                lse_ref[a] = m_s[a] + jnp.log(l_s[a])

    return _pc(body, grid=(H // G, n, n),
               in_specs=[pl.BlockSpec((G, t, W), lambda h, i, j: (h, i, 0)),
                         pl.BlockSpec((G, t, W), lambda h, i, j: (h, jnp.minimum(i, j), 0)),
                         pl.BlockSpec((G, t, LANES), lambda h, i, j: (h, jnp.minimum(i, j), 0))],
               out_specs=[pl.BlockSpec((t, G * LANES), lambda h, i, j: (i, h)), pl.BlockSpec((G, t, LANES), lambda h, i, j: (h, i, 0))],
               out_shape=[_sds((S_, H * LANES), BF16), _sds((H, S_, LANES), F32)],
               scratch_shapes=[pltpu.VMEM((G, t, LANES), F32)] * 3,
               compiler_params=_cparams(("parallel", "parallel", "arbitrary")), name=name)(qh, kh, vh)


def flash_bwd(qh, kh, vh, o, lse, do, name):
    H, S_, W = qh.shape
    t = _tile(S_, FLASH_TILE)
    n = S_ // t
    G = FLASH_HEADS
    heads = list(range(G))

    def body(q_ref, k_ref, v_ref, o_ref, lse_ref, do_ref, dq_ref, dk_ref, dv_ref, dq_acc, dk_acc, dv_acc):
        kj, qi = pl.program_id(1), pl.program_id(2)

        @pl.when(jnp.logical_and(kj == 0, qi == 0))
        def _():
            dq_acc[...] = jnp.zeros_like(dq_acc)

        @pl.when(qi == 0)
        def _():
            dk_acc[...] = jnp.zeros_like(dk_acc)
            dv_acc[...] = jnp.zeros_like(dv_acc)

        def step(diagonal):
            lanes = lambda a: slice(a * LANES, (a + 1) * LANES)
            do_ = _each(lambda a: do_ref[:, lanes(a)], heads)
            dob = _each(lambda d: d.astype(BF16), do_)
            delta = _each(lambda d, a: jnp.sum(d.astype(F32) * o_ref[:, lanes(a)].astype(F32), axis=-1, keepdims=True), do_, heads)
            for r0, keys in _sub_blocks(t, diagonal):
                p = _each(lambda a: jnp.exp(_scores(q_ref[a, r0:, :], k_ref[a, keys, :], diagonal) - lse_ref[a, r0:, :1]), heads)
                dp = _each(lambda d, a: lax.dot_general(d[r0:], v_ref[a, keys, :], (_NT, ((), ())), preferred_element_type=F32), dob, heads)
                ds = _each(lambda pa, dpa, de: (pa * (dpa - de[r0:])).astype(BF16), p, dp, delta)
                rows = pl.ds(pl.multiple_of(qi * t, t) + r0, t - r0)
                for a in heads:
                    dv_acc[a, keys, :] += lax.dot_general(p[a].astype(BF16), dob[a][r0:], (_TN, ((), ())), preferred_element_type=F32)
                    dk_acc[a, keys, :] += lax.dot_general(ds[a], q_ref[a, r0:, :], (_TN, ((), ())), preferred_element_type=F32)
                    dq_acc[a, rows, :] += lax.dot_general(ds[a], k_ref[a, keys, :], (_NN, ((), ())), preferred_element_type=F32)

        pl.when(qi > kj)(lambda: step(False))
        pl.when(qi == kj)(lambda: step(True))

        @pl.when(qi == n - 1)
        def _():
            dk_ref[...] = dk_acc[...].astype(BF16)
            dv_ref[...] = dv_acc[...].astype(BF16)

        @pl.when(jnp.logical_and(kj == n - 1, qi == n - 1))
        def _():
            dq_ref[...] = dq_acc[...].astype(BF16)

    qrow = lambda h, j, i: jnp.maximum(i, j)
    return _pc(body, grid=(H // G, n, n),
               in_specs=[pl.BlockSpec((G, t, W), lambda h, j, i: (h, qrow(h, j, i), 0)),
                         pl.BlockSpec((G, t, W), lambda h, j, i: (h, j, 0)),
                         pl.BlockSpec((G, t, LANES), lambda h, j, i: (h, j, 0)),
                         pl.BlockSpec((t, G * LANES), lambda h, j, i: (qrow(h, j, i), h)),
                         pl.BlockSpec((G, t, LANES), lambda h, j, i: (h, qrow(h, j, i), 0)),
                         pl.BlockSpec((t, G * LANES), lambda h, j, i: (qrow(h, j, i), h))],
               out_specs=[pl.BlockSpec((G, S_, W), lambda h, j, i: (h, 0, 0)),
                          pl.BlockSpec((G, t, W), lambda h, j, i: (h, j, 0)),
                          pl.BlockSpec((G, t, LANES), lambda h, j, i: (h, j, 0))],
               out_shape=[_sds((H, S_, W), BF16), _sds((H, S_, W), BF16), _sds((H, S_, LANES), BF16)],
               scratch_shapes=[pltpu.VMEM((G, S_, W), F32), pltpu.VMEM((G, t, W), F32), pltpu.VMEM((G, t, LANES), F32)],
               compiler_params=_cparams(("parallel", "arbitrary", "arbitrary")), name=name)(qh, kh, vh, o, lse, do)


def loss_head(x, target, g, name):
    S_ = x.shape[0]
    ts = 512

    def body(x_ref, t_ref, g_ref, l_ref, dx_ref, dg_ref):
        @pl.when(pl.program_id(0) == 0)
        def _():
            l_ref[...] = jnp.zeros_like(l_ref)
            dg_ref[...] = jnp.zeros_like(dg_ref)

        y, vjp = jax.vjp(_rms, x_ref[...], g_ref[...])
        err = y - t_ref[...]
        l_ref[...] += 0.5 * jnp.sum(jnp.sum(err * err, axis=-1, keepdims=True), axis=0, keepdims=True) / D
        dx, dg = vjp(err / D)
        dx_ref[...] = dx
        dg_ref[...] += dg

    row = pl.BlockSpec((ts, D), lambda i: (i, 0))
    return _pc(body, grid=(S_ // ts,), in_specs=[row, row, pl.BlockSpec((1, D), lambda i: (0, 0))],
               out_specs=[pl.BlockSpec((1, LANES), lambda i: (0, 0)), row, pl.BlockSpec((1, D), lambda i: (0, 0))],
               out_shape=[_sds((1, LANES), F32), _sds((S_, D), F32), _sds((1, D), F32)],
               compiler_params=_cparams(("arbitrary",)), name=name)(x, target, g)


def adamw(w, parts, m, v, name, after=None):
    R, C = w.shape
    rows = [p.shape[1] for p in parts[0]]
    tr = R
    for cand in (512, 256, 128, 64, 32, 16, 8):
        if all(r % cand == 0 for r in rows) and cand * C * 4 * len(rows) <= 2 * 1024 * 1024:
            tr = cand
            break
    c1 = 1.0 - ADAM_B1 ** ADAM_STEP
    c2 = 1.0 - ADAM_B2 ** ADAM_STEP
    starts = [sum(rows[:k]) // tr for k in range(len(rows))]
    flat = [p for part in parts for p in part]

    def body(*refs):
        w_ref, m_ref, v_ref = refs[0], refs[1 + len(flat)], refs[2 + len(flat)]
        g_ref, d_ref, nm_ref, nv_ref = refs[-4:]
        i = pl.program_id(0)
        gg, at = None, 1
        for part in parts:
            val = None
            for k in range(len(part)):
                p_ref = refs[at]
                at += 1
                s = p_ref[0].astype(F32)
                for n in range(1, p_ref.shape[0]):
                    s = s + p_ref[n].astype(F32)
                val = s if val is None else jnp.where(i >= starts[k], s, val)
            gg = val if gg is None else gg + val
        m2 = ADAM_B1 * m_ref[...] + (1.0 - ADAM_B1) * gg
        v2 = ADAM_B2 * v_ref[...] + (1.0 - ADAM_B2) * (gg * gg)
        g_ref[...] = gg
        d_ref[...] = -ADAM_LR * ((m2 / c1) / (jnp.sqrt(v2 / c2) + ADAM_EPS) + ADAM_WD * w_ref[...])
        nm_ref[...] = m2
        nv_ref[...] = v2

    blk = pl.BlockSpec((tr, C), lambda i: (i, 0))
    piece = lambda p, k: pl.BlockSpec((p.shape[0], tr, C), lambda i: (0, jnp.clip(i - starts[k], 0, rows[k] // tr - 1), 0))
    pblk = [piece(p, k) for part in parts for k, p in enumerate(part)]
    extra = [] if after is None else [after]
    return _pc(body, grid=(R // tr,), in_specs=[blk] + pblk + [blk, blk] + [pl.BlockSpec(memory_space=pl.ANY)] * len(extra),
               out_specs=[blk] * 4, out_shape=[_sds((R, C), F32)] * 4,
               compiler_params=_cparams(("parallel",)), name=name)(w, *flat, m, v, *extra)


def sum_slots(recv, name):
    n, R, C = recv.shape

    def body(r_ref, o_ref):
        acc = r_ref[0]
        for s in range(1, n):
            acc = acc + r_ref[s]
        o_ref[...] = acc

    return _pc(body, grid=(1,), in_specs=[pl.BlockSpec((n, R, C), lambda i: (0, 0, 0))],
               out_specs=pl.BlockSpec((R, C), lambda i: (0, 0)), out_shape=_sds((R, C), F32),
               compiler_params=_cparams(("arbitrary",)), name=name)(recv)


def _chip_peers():
    x, y, c = lax.axis_index("x"), lax.axis_index("y"), lax.axis_index("c")
    return (x, y, c), [(1 - x, y, c), (x, 1 - y, c), (1 - x, 1 - y, c)]


def _chip_index(p):
    return 2 * p[0] + p[1]


def _win(ref, axis, chip, size):
    if axis is None:
        return ref.at[chip]
    idx = [slice(None)] * len(ref.shape)
    idx[axis] = pl.ds(pl.multiple_of(chip * size, size), size)
    return ref.at[tuple(idx)]


def _remote(src, dst, send_sem, recv_sem, peer):
    return pltpu.make_async_remote_copy(src_ref=src, dst_ref=dst, send_sem=send_sem, recv_sem=recv_sem, device_id=peer,
                                        device_id_type=MESH)


HBM_SPEC = pl.BlockSpec(memory_space=pltpu.HBM)
SEM_SPEC = pl.BlockSpec(memory_space=pltpu.SEMAPHORE)
ANY_SPEC = pl.BlockSpec(memory_space=pl.ANY)
DATAFLOW = pltpu.SideEffectType.DATAFLOW_SIDE_EFFECTING


def gather_piece(i, l, o, axis, size):
    return (i, lambda r, chip: r.at[l], o, lambda r, chip: _win(r, axis, chip, size))


def scatter_piece(i, o, axis, size):
    return (i, lambda r, chip: _win(r, axis, chip, size), o, lambda r, chip: r.at[chip])


def whole_piece(i):
    return (i, lambda r, chip: r, i, lambda r, chip: r)


def _copies(pieces, in_refs, out_refs, send, recv, sibling):
    me, peers = _chip_peers()
    if sibling:
        peers = [(me[0], me[1], 1 - me[2])]
    mine = _chip_index(me)
    remote = []
    for n, (i, src, o, dst) in enumerate(pieces):
        d = dst(out_refs[o], mine)
        remote += [_remote(src(in_refs[i], _chip_index(p)), d, send.at[len(peers) * n + k], recv.at[len(peers) * n + k], p)
                   for k, p in enumerate(peers)]
    return remote


def own_window(a, axis, size, chip):
    if axis is None:
        return lax.dynamic_index_in_dim(a, chip, 0, keepdims=False)
    return lax.dynamic_slice_in_dim(a, chip * size, size, axis=axis)


def place_own(land, own, axis, size, chip):
    if axis is None:
        return lax.dynamic_update_slice_in_dim(land, own[None], chip, axis=0)
    return lax.dynamic_update_slice_in_dim(land, own, chip * size, axis=axis)


def exchange_start(pieces, ins, out_shapes, after, name, sibling=False):
    n_in, n_out, ncp = len(ins), len(out_shapes), len(pieces)

    def body(*refs):
        in_refs, land_refs = refs[:n_in], refs[n_in:n_in + n_out]
        send, recv = refs[n_in + n_out + 1], refs[n_in + n_out + 2]
        token = refs[-1]
        for cp in _copies(pieces, in_refs, land_refs, send, recv, sibling):
            cp.start()
        token[...] = jnp.zeros_like(token)

    hbm = lambda a: pltpu.with_memory_space_constraint(a, pltpu.HBM)
    lands = [hbm(lax.empty(s.shape, s.dtype)) for s in out_shapes]
    sem = pltpu.SemaphoreType.DMA(((1 if sibling else 3) * ncp,))
    thru = [pltpu.HBM(a.shape, a.dtype) for a in ins] + [pltpu.HBM(s.shape, s.dtype) for s in out_shapes]
    res = _pc(body, in_specs=[HBM_SPEC] * (n_in + n_out) + [ANY_SPEC],
              out_specs=[SEM_SPEC, SEM_SPEC] + [HBM_SPEC] * (n_in + n_out) + [pl.BlockSpec(memory_space=pltpu.VMEM)],
              out_shape=[sem, sem] + thru + [_sds((8, LANES), F32)],
              input_output_aliases={i: 2 + i for i in range(n_in + n_out)},
              compiler_params=pltpu.CompilerParams(has_side_effects=DATAFLOW), name=name)(*[hbm(a) for a in ins], *lands, after)
    return (res[0], res[1]), list(res[2:2 + n_in]), list(res[2 + n_in:2 + n_in + n_out]), res[-1]


def exchange_wait(pieces, sems, ins, lands, after, name, sibling=False):
    n_in, n_out = len(ins), len(lands)

    def body(*refs):
        in_refs, land_refs = refs[:n_in], refs[n_in:n_in + n_out]
        send, recv = refs[n_in + n_out], refs[n_in + n_out + 1]
        for cp in _copies(pieces, in_refs, land_refs, send, recv, sibling):
            cp.wait_send()
            cp.wait_recv()

    thru = [pltpu.HBM(a.shape, a.dtype) for a in ins] + [pltpu.HBM(a.shape, a.dtype) for a in lands]
    res = _pc(body, in_specs=[HBM_SPEC] * (n_in + n_out) + [SEM_SPEC, SEM_SPEC, ANY_SPEC], out_specs=[HBM_SPEC] * (n_in + n_out),
              out_shape=thru, input_output_aliases={i: i for i in range(n_in + n_out)},
              compiler_params=pltpu.CompilerParams(has_side_effects=DATAFLOW), name=name)(*ins, *lands, sems[0], sems[1], after)
    return list(res[:n_in]), list(res[n_in:])


def exchange_all(buf, name):
    def body(in_ref, out_ref, send, recv, local):
        x, y, c = lax.axis_index("x"), lax.axis_index("y"), lax.axis_index("c")
        mine = 4 * x + 2 * y + c
        loc = pltpu.make_async_copy(in_ref, out_ref.at[mine], local)
        loc.start()
        copies = [loc]
        for k in range(1, 8):
            peer = (x ^ (k >> 2), y ^ ((k >> 1) & 1), c ^ (k & 1))
            cp = pltpu.make_async_remote_copy(src_ref=in_ref, dst_ref=out_ref.at[mine], send_sem=send.at[k - 1],
                                              recv_sem=recv.at[k - 1], device_id=peer, device_id_type=MESH)
            cp.start()
            copies.append(cp)
        for cp in copies:
            cp.wait()

    anyspec = pl.BlockSpec(memory_space=pl.ANY)
    return _pc(body, in_specs=[anyspec], out_specs=anyspec, out_shape=_sds((8,) + buf.shape, buf.dtype),
               scratch_shapes=[pltpu.SemaphoreType.DMA((7,)), pltpu.SemaphoreType.DMA((7,)), pltpu.SemaphoreType.DMA],
               name=name)(buf)


def _norm_fwd(x, g, name):
    return rowwise(f_rms, [(x, D, 0, 0)], [(g, D, 0, 0)], [(D, 0, BF16)], ts=1024, name=name)[0]


def _norm_bwd(x, g, dh, dres, name):
    (dx,), (dg,) = rowwise_bwd(f_rms, [(x, D, 0, 0)], [(g, D, 0, 0)], [(dh, D, 0, 0)], need=[True],
                               adds={0: (dres, D, 0, 0)}, ts=1024, name=name)
    return dx, dg


def pool_fwd(x, W, tag, late=None):
    h = _norm_fwd(x, W["ng"], tag + "_norm")
    proj = mm(h, W["w_in"], out_dtype=BF16, name=tag + "_in")
    if late is not None:
        W = dict(W, **late(proj))
    p = pool_time_fwd(proj, tag + "_win")
    pg = gmm("nn", p, W["w_grp"], G=4, out_dtype=BF16, name=tag + "_grp")
    y = rowwise(f_pool_gate, [(pg, POOL_GROUP, 0, 1), (proj, POOL_GROUP, 4, 1)], [(W["scale"], POOL_GROUP, 0, 1)],
                [(POOL_GROUP, 1, BF16)], ncol=4, ts=1024, name=tag + "_gate")[0]
    xn = mm(y, W["w_out"], add=x, name=tag + "_out")
    return xn, (x, h, proj, p, pg, y)


def pool_bwd(dxn, W, saved, tag, after=None, emit=None):
    x, h, proj, p, pg, y = saved
    emit = emit or (lambda grads: None)
    dy = mm(dxn, W["w_out"], tb=True, after=after, out_dtype=BF16, name=tag + "_dy")
    g = {}
    (dpg, dproj), (g["scale"],) = rowwise_bwd(
        f_pool_gate, [(pg, POOL_GROUP, 0, 1), (proj, POOL_GROUP, 4, 1)], [(W["scale"], POOL_GROUP, 0, 1)],
        [(dy, POOL_GROUP, 0, 1)], need=[True, True], place={1: (2 * POOL_WIDTH, 4)}, narrow=(0, 1), ncol=4, ts=1024, name=tag + "_dgate")
    dp = gmm("nt", dpg, W["w_grp"], G=4, out_dtype=BF16, name=tag + "_dp")
    dproj = pool_time_bwd(dp, dproj, tag + "_dwin")
    g["w_in"] = mm(h, dproj, ta=True, out_dtype=BF16, name=tag + "_dw_in")
    t1 = emit({"w_in": g["w_in"]})
    g["w_out"] = mm(y, dxn, ta=True, after=t1, out_dtype=BF16, name=tag + "_dwout")
    g["w_grp"] = gmm("tn", p, dpg, G=4, out_dtype=BF16, name=tag + "_dwgrp")
    t2 = emit({"w_out": g["w_out"], "w_grp": g["w_grp"]})
    dh = mm(dproj, W["w_in"], tb=True, after=t2, name=tag + "_dh")
    dx, g["ng"] = _norm_bwd(x, W["ng"], dh, dxn, tag + "_dnorm")
    return dx, g


def gdn_fwd(x, W, tag, late=None):
    h = _norm_fwd(x, W["ng"], tag + "_norm")
    proj = mm(h, W["w_in"], name=tag + "_in")
    qkv = gdn_conv_fwd(proj, W["conv"], tag + "_conv")
    g_b, beta_b = rowwise(f_gdn_gates, [(proj, LANES, 6144 // LANES, 0)], [(W["a_log"], LANES, 0, 0), (W["dt_bias"], LANES, 0, 0)],
                          [(GDN_QK, 0, F32), (GDN_QK, 0, F32)], ts=1024, name=tag + "_gates")
    o, states = gdn_chunk_fwd(qkv, g_b, beta_b, tag + "_chunk")
    og = rowwise(f_gdn_out, [(o, GDN_DV, 0, 1), (proj, GDN_DV, 4096 // GDN_DV, 1)], [(W["norm_g"], GDN_DV, 0, 0)],
                 [(GDN_DV, 1, BF16)], ncol=GDN_H, ts=2048, name=tag + "_onorm")[0]
    if late is not None:
        W = dict(W, **late(og))
    xn = mm(og, W["w_out"], add=x, name=tag + "_out")
    return xn, (x, h, proj, qkv, g_b, beta_b, o, states, og)


def gdn_bwd(dxn, W, saved, tag, after=None):
    x, h, proj, qkv, g_b, beta_b, o, states, og = saved
    dog = mm(dxn, W["w_out"], tb=True, after=after, name=tag + "_dog")
    g = {"w_out": mm(og, dxn, ta=True, out_dtype=BF16, name=tag + "_dwout")}
    (do, dproj), (g["norm_g"],) = rowwise_bwd(
        f_gdn_out, [(o, GDN_DV, 0, 1), (proj, GDN_DV, 4096 // GDN_DV, 1)], [(W["norm_g"], GDN_DV, 0, 0)],
        [(dog, GDN_DV, 0, 1)], need=[True, True], place={1: (GDN_IN_PAD, 4096 // GDN_DV)}, narrow=(1,), ncol=GDN_H, ts=2048, name=tag + "_donorm")
    dq, dk, dv, dg_b, dbeta_b = gdn_chunk_bwd(qkv, g_b, beta_b, states, do, tag + "_dchunk")
    (dproj,), (g["a_log"], g["dt_bias"]) = rowwise_bwd(
        f_gdn_gates, [(proj, LANES, 6144 // LANES, 0)], [(W["a_log"], LANES, 0, 0), (W["dt_bias"], LANES, 0, 0)],
        [(dg_b, GDN_QK, 0, 0), (dbeta_b, GDN_QK, 0, 0)], need=[True], place={0: (dproj, 6144 // LANES)}, ts=1024, name=tag + "_dgates")
    dproj, g["conv"] = gdn_conv_bwd(proj, W["conv"], dq, dk, dv, dproj, tag + "_dconv")
    dh = mm(dproj, W["w_in"], tb=True, name=tag + "_dh")
    g["w_in"] = mm(h, dproj, ta=True, out_dtype=BF16, name=tag + "_dw_in")
    dx, g["ng"] = _norm_bwd(x, W["ng"], dh, dxn, tag + "_dnorm")
    return dx, g


def mla_fwd(x, pos, W, tag):
    h = _norm_fwd(x, W["ng"], tag + "_norm")
    proj = mm(h, W["w_in"], out_dtype=BF16, name=tag + "_in")
    hq = rowwise(f_rms, [(proj, MLA_Q_LORA, 0, 0)], [(W["q_g"], MLA_Q_LORA, 0, 0)], [(MLA_Q_LORA, 0, BF16)], ts=1024, name=tag + "_qnorm")[0]
    hkv = rowwise(f_rms, [(proj, MLA_KV_LORA, 2, 0)], [(W["kv_g"], MLA_KV_LORA, 0, 0)], [(MLA_KV_LORA, 0, BF16)], ts=1024, name=tag + "_kvnorm")[0]
    qpad = mm(hq, W["w_uq"], out_dtype=BF16, name=tag + "_uq")
    kv = mm(hkv, W["w_ukv"], out_dtype=BF16, name=tag + "_ukv")
    qh, kh, vh = mla_prep_fwd(qpad, kv, proj, pos, W["rope"], tag + "_prep")
    o, lse = flash_fwd(qh, kh, vh, tag + "_attn")
    og = rowwise(f_ogate, [(o, 512, 0, 1), (proj, 512, 4, 1)], [], [(512, 1, BF16)], ncol=4, ts=1024, name=tag + "_ogate")[0]
    xn = mm(og, W["w_out"], add=x, name=tag + "_out")
    return xn, (x, h, proj, hq, hkv, qh, kh, vh, o, lse, og)


def mla_bwd(dxn, pos, W, saved, tag, after=None):
    x, h, proj, hq, hkv, qh, kh, vh, o, lse, og = saved
    dog = mm(dxn, W["w_out"], tb=True, after=after, out_dtype=BF16, name=tag + "_dog")
    g = {"w_out": mm(og, dxn, ta=True, out_dtype=BF16, name=tag + "_dwout")}
    dproj = jnp.zeros(proj.shape, BF16)
    (do, dproj), _ = rowwise_bwd(f_ogate, [(o, 512, 0, 1), (proj, 512, 4, 1)], [], [(dog, 512, 0, 1)], need=[True, True],
                                 place={1: (dproj, 4)}, narrow=(0,), ncol=4, ts=1024, name=tag + "_dogate")
    dqh, dkh, dvh = flash_bwd(qh, kh, vh, o, lse, do, tag + "_dattn")
    dqpad, dkv, dproj = mla_prep_bwd(dqh, dkh, dvh, pos, W["rope"], dproj, tag + "_dprep")
    dhq = mm(dqpad, W["w_uq"], tb=True, name=tag + "_dhq")
    g["w_uq"] = mm(hq, dqpad, ta=True, out_dtype=BF16, name=tag + "_dwuq")
    dhkv = mm(dkv, W["w_ukv"], tb=True, name=tag + "_dhkv")
    g["w_ukv"] = mm(hkv, dkv, ta=True, out_dtype=BF16, name=tag + "_dwukv")
    (dproj,), (g["q_g"],) = rowwise_bwd(f_rms, [(proj, MLA_Q_LORA, 0, 0)], [(W["q_g"], MLA_Q_LORA, 0, 0)], [(dhq, MLA_Q_LORA, 0, 0)],
                                        need=[True], place={0: (dproj, 0)}, ts=512, name=tag + "_dqnorm")
    (dproj,), (g["kv_g"],) = rowwise_bwd(f_rms, [(proj, MLA_KV_LORA, 2, 0)], [(W["kv_g"], MLA_KV_LORA, 0, 0)], [(dhkv, MLA_KV_LORA, 0, 0)],
                                         need=[True], place={0: (dproj, 2)}, ts=512, name=tag + "_dkvnorm")
    dh = mm(dproj, W["w_in"], tb=True, name=tag + "_dh")
    g["w_in"] = mm(h, dproj, ta=True, out_dtype=BF16, name=tag + "_dw_in")
    dx, g["ng"] = _norm_bwd(x, W["ng"], dh, dxn, tag + "_dnorm")
    return dx, g


def _pad_cols(a, n):
    return jnp.pad(a, ((0, 0), (0, n - a.shape[1])))


def _mla_w_in_layout(w):
    z = lambda n: jnp.zeros((w.shape[0], n), w.dtype)
    kr = w[:, 1280:1344]
    return jnp.concatenate([w[:, :768], z(256), w[:, 768:1280], kr[:, :32], z(32), kr[:, 32:], z(32), z(384), w[:, 1344:]], axis=1)


def _mla_w_in_unlayout(g):
    return jnp.concatenate([g[:, :768], g[:, 1024:1536], g[:, 1536:1568], g[:, 1600:1632], g[:, 2048:]], axis=1)


def _mla_w_uq_layout(w):
    w3 = w.reshape(w.shape[0], MLA_H, MLA_NOPE + MLA_ROPE)
    z = jnp.zeros((w.shape[0], MLA_H, 32), w.dtype)
    return jnp.concatenate([w3[..., :128], w3[..., 128:160], z, w3[..., 160:192], z], axis=-1).reshape(w.shape[0], MLA_H * 256)


def _mla_w_uq_unlayout(g):
    g3 = g.reshape(g.shape[0], MLA_H, 256)
    return jnp.concatenate([g3[..., :128], g3[..., 128:160], g3[..., 192:224]], axis=-1).reshape(g.shape[0], MLA_H * 192)


def _rope_consts():
    half = MLA_ROPE // 2
    inv = ROPE_THETA ** (-jnp.arange(half, dtype=F32) / half)
    z = jnp.zeros((half,), F32)
    o = jnp.ones((half,), F32)
    row = lambda *p: jnp.concatenate(p).reshape(1, LANES)
    return row(inv, z, inv, z), row(o, z, o, z), row(-o, z, o, z)


BIG = ["pool_w_in", "pool_w_grp", "pool_w_out", "gdn_w_in", "gdn_w_out", "mla_w_in", "mla_w_uq", "mla_w_ukv", "mla_w_out"]
BIG_LAYOUT = {"pool_w_in": (1, 1024, (1024, 4096)), "pool_w_grp": (1, 128, (4, 512, 512)), "pool_w_out": (0, 512, (2048, 1024)),
              "gdn_w_in": (None, None, (4, 1024, 1540)), "gdn_w_out": (0, 512, (2048, 1024)),
              "mla_w_in": (None, None, (4, 1024, 848)), "mla_w_uq": (1, 768, (768, 3072)), "mla_w_ukv": (1, 1024, (512, 4096)),
              "mla_w_out": (0, 512, (2048, 1024))}
SMALL_SHARDED = ["pool_scale", "gdn_conv", "mla_q_norm_g", "mla_kv_norm_g"]
SMALL_AXIS = {"pool_scale": 1, "gdn_conv": 2, "mla_q_norm_g": 1, "mla_kv_norm_g": 1}
REPLICATED = ["norm_g", "gdn_a_log", "gdn_dt_bias", "gdn_norm_g", "final_g"]
PACK_C = 1024


def _pack(parts, dtype, row_mult):
    flat = jnp.concatenate([p.reshape(-1).astype(dtype) for p in parts])
    rows = -(-flat.shape[0] // PACK_C)
    rows = -(-rows // row_mult) * row_mult
    return jnp.pad(flat, (0, rows * PACK_C - flat.shape[0])).reshape(rows, PACK_C)


def _unpack(buf, shapes):
    lead = buf.shape[:-2]
    flat = buf.reshape(lead + (-1,))
    out, off = [], 0
    for s in shapes:
        n = int(np.prod(s))
        out.append(flat[..., off:off + n].reshape(lead + tuple(s)))
        off += n
    return out


def _unshard(g4, axis):
    a = jnp.moveaxis(g4, 0, axis)
    s = a.shape
    return a.reshape(s[:axis] + (s[axis] * s[axis + 1],) + s[axis + 2:])


def _to_shards(a, axis):
    s = a.shape
    return jnp.moveaxis(a.reshape(s[:axis] + (4, s[axis] // 4) + s[axis + 1:]), axis, 0)


def layer_weights(full, small, rep, layer):
    ng = rep["norm_g"][layer:layer + 1]
    side_by_side = lambda a4: jnp.moveaxis(a4, 0, 1).reshape(a4.shape[1], 4 * a4.shape[2])
    if layer in (0, 3):
        j = layer // 3
        return dict(ng=ng, w_in=full[("pool_w_in", j)], w_grp=full[("pool_w_grp", j)], scale=small["pool_scale"][j:j + 1],
                    w_out=full[("pool_w_out", j)])
    if layer == 1:
        return dict(ng=ng, w_in=_pad_cols(side_by_side(full[("gdn_w_in", 0)]), GDN_IN_PAD),
                    conv=jnp.pad(small["gdn_conv"][0], ((0, 4), (0, 0))), a_log=_pad_cols(rep["gdn_a_log"], LANES),
                    dt_bias=_pad_cols(rep["gdn_dt_bias"], LANES), norm_g=rep["gdn_norm_g"], w_out=full.get(("gdn_w_out", 0)))
    return dict(ng=ng, w_in=_mla_w_in_layout(side_by_side(full[("mla_w_in", 0)])), q_g=small["mla_q_norm_g"],
                kv_g=small["mla_kv_norm_g"], w_uq=_mla_w_uq_layout(full[("mla_w_uq", 0)]), w_ukv=full[("mla_w_ukv", 0)],
                w_out=full[("mla_w_out", 0)], rope=_rope_consts())


def big_grad_pieces(gl):
    g0, g1, g2, g3 = gl
    slots = lambda a: jnp.moveaxis(a.reshape(a.shape[0], 4, a.shape[1] // 4), 1, 0)
    out = {}
    for l, g in ((0, g0), (1, g3)):
        if g is not None:
            out.update({("pool_w_in", l): g["w_in"], ("pool_w_grp", l): g["w_grp"], ("pool_w_out", l): g["w_out"]})
    if g1 is not None:
        out.update({("gdn_w_in", 0): slots(g1["w_in"][:, :GDN_IN]), ("gdn_w_out", 0): g1["w_out"]})
    if g2 is not None:
        out.update({("mla_w_in", 0): slots(_mla_w_in_unlayout(g2["w_in"])), ("mla_w_uq", 0): _mla_w_uq_unlayout(g2["w_uq"]),
                    ("mla_w_ukv", 0): g2["w_ukv"], ("mla_w_out", 0): g2["w_out"]})
    return out


def small_grads(gl, dfinal):
    g0, g1, g2, g3 = gl
    return {"norm_g": jnp.concatenate([g0["ng"], g1["ng"], g2["ng"], g3["ng"]], axis=0),
            "pool_scale": jnp.concatenate([g0["scale"], g3["scale"]], axis=0), "gdn_conv": g1["conv"][None, :4],
            "gdn_a_log": g1["a_log"][:, :GDN_H], "gdn_dt_bias": g1["dt_bias"][:, :GDN_H], "gdn_norm_g": g1["norm_g"],
            "mla_q_norm_g": g2["q_g"], "mla_kv_norm_g": g2["kv_g"], "final_g": dfinal.reshape(D)}


NAMES = ["norm_g", "pool_w_in", "pool_w_grp", "pool_scale", "pool_w_out", "gdn_w_in", "gdn_conv", "gdn_a_log", "gdn_dt_bias",
         "gdn_norm_g", "gdn_w_out", "mla_w_in", "mla_q_norm_g", "mla_w_uq", "mla_kv_norm_g", "mla_w_ukv", "mla_w_out", "final_g"]


def kernel(x, positions, norm_g, pool_w_in, pool_w_grp, pool_scale, pool_w_out, gdn_w_in, gdn_conv, gdn_a_log, gdn_dt_bias, gdn_norm_g, gdn_w_out, mla_w_in, mla_q_norm_g, mla_w_uq, mla_kv_norm_g, mla_w_ukv, mla_w_out, final_g, loss_target, m_norm_g, m_pool_w_in, m_pool_w_grp, m_pool_scale, m_pool_w_out, m_gdn_w_in, m_gdn_conv, m_gdn_a_log, m_gdn_dt_bias, m_gdn_norm_g, m_gdn_w_out, m_mla_w_in, m_mla_q_norm_g, m_mla_w_uq, m_mla_kv_norm_g, m_mla_w_ukv, m_mla_w_out, m_final_g, v_norm_g, v_pool_w_in, v_pool_w_grp, v_pool_scale, v_pool_w_out, v_gdn_w_in, v_gdn_conv, v_gdn_a_log, v_gdn_dt_bias, v_gdn_norm_g, v_gdn_w_out, v_mla_w_in, v_mla_q_norm_g, v_mla_w_uq, v_mla_kv_norm_g, v_mla_w_ukv, v_mla_w_out, v_final_g):
    args = locals()
    w = {n: args[n] for n in NAMES}
    m = {n: args["m_" + n] for n in NAMES}
    v = {n: args["v_" + n] for n in NAMES}
    my_chip = (2 * lax.axis_index("x") + lax.axis_index("y")).astype(I32)
    S_ = x.shape[1]
    x0, pos, target = x[0], positions.reshape(S_, 1).astype(F32), loss_target[0]
    rep = {n: w[n] for n in REPLICATED}

    small_shapes = [w[n].shape for n in SMALL_SHARDED]
    small_pack = _pack([w[n] for n in SMALL_SHARDED], F32, 8)[None]
    layout = dict(BIG_LAYOUT, small=(None, None, (4,) + small_pack.shape[1:]))

    def shard(key, token):
        n, l = key
        if n == "small":
            return small_pack
        a = w[n][l:l + 1]
        return (a if token is None else a + token[0, 0]).astype(BF16)

    def gather_start(group, after, tag, token=None):
        pieces = [gather_piece(i, 0, i, layout[n][0], layout[n][1]) for i, (n, l) in enumerate(group)]
        ins = [shard(k, token) for k in group]
        shapes = [_sds(layout[n][2], a.dtype) for (n, l), a in zip(group, ins)]
        sems, ins, lands, token = exchange_start(pieces, ins, shapes, after, tag + "_start")
        return (pieces, sems, ins, lands), token

    def finish(handle, after, tag):
        return exchange_wait(*handle, after, tag + "_wait")

    def gathered(group, handle, after, tag):
        srcs, lands = finish(handle, after, tag)
        return {(n, l): place_own(a, s[0], layout[n][0], layout[n][1], my_chip) for (n, l), s, a in zip(group, srcs, lands)}

    group_a = [("small", 0), ("pool_w_in", 0)]
    group_a2 = [("pool_w_grp", 0), ("pool_w_out", 0)]
    group_b = [("gdn_w_in", 0)]
    group_c = [("gdn_w_out", 0), ("mla_w_in", 0), ("mla_w_uq", 0), ("mla_w_ukv", 0), ("mla_w_out", 0), ("pool_w_in", 1),
               ("pool_w_grp", 1), ("pool_w_out", 1)]
    full = {}
    h_a, t_a = gather_start(group_a, x0, "gather_a")
    h_a2, t_a2 = gather_start(group_a2, t_a, "gather_a2", t_a)
    h_b, t_b = gather_start(group_b, t_a2, "gather_b", t_a2)
    h_c, t_c = gather_start(group_c, t_b, "gather_c", t_b)
    full.update(gathered(group_a, h_a, t_c, "gather_a"))
    small = {n: _unshard(a, SMALL_AXIS[n]) for n, a in zip(SMALL_SHARDED, _unpack(full[("small", 0)], small_shapes))}

    def late_l0(proj):
        full.update(gathered(group_a2, h_a2, proj, "gather_a2"))
        return dict(w_grp=full[("pool_w_grp", 0)], w_out=full[("pool_w_out", 0)])

    first = dict(ng=rep["norm_g"][0:1] + t_c[0:1, 0:1], w_in=full[("pool_w_in", 0)], scale=small["pool_scale"][0:1])
    x1, s0 = pool_fwd(x0, first, "l0", late=late_l0)
    W0 = layer_weights(full, small, rep, 0)
    full.update(gathered(group_b, h_b, x1, "gather_b"))

    def late_l1(og):
        full.update(gathered(group_c, h_c, og, "gather_c"))
        return dict(w_out=full[("gdn_w_out", 0)])

    x2, s1 = gdn_fwd(x1, layer_weights(full, small, rep, 1), "l1", late=late_l1)
    W1, W2, W3 = (layer_weights(full, small, rep, i) for i in (1, 2, 3))
    x3, s2 = mla_fwd(x2, pos, W2, "l2")
    x4, s3 = pool_fwd(x3, W3, "l3")
    loss_part, dx4, dfinal = loss_head(x4, target, final_g.reshape(1, D), "loss_head")

    def scatter_start(pieces_of, after, tag):
        keys = list(pieces_of)
        pieces = [scatter_piece(i, i, BIG_LAYOUT[n][0], BIG_LAYOUT[n][1]) for i, (n, l) in enumerate(keys)]
        shapes = [_sds((4,) + tuple(w[n].shape[1:]), BF16) for n, l in keys]
        sems, ins, lands, token = exchange_start(pieces, [pieces_of[k] for k in keys], shapes, after, tag + "_start")
        return keys, (pieces, sems, ins, lands), token

    def scattered(keys, handle, after, tag):
        srcs, lands = finish(handle, after, tag)
        return {(n, l): place_own(a, own_window(g, BIG_LAYOUT[n][0], BIG_LAYOUT[n][1], my_chip), None, None, my_chip)
                for (n, l), g, a in zip(keys, srcs, lands)}

    dx3, g3 = pool_bwd(dx4, W3, s3, "l3")
    k3, h3, t3 = scatter_start(big_grad_pieces((None, None, None, g3)), dx3, "scatter_l3")
    dx2, g2 = mla_bwd(dx3, pos, W2, s2, "l2", after=t3)
    k2, h2, t2 = scatter_start(big_grad_pieces((None, None, g2, None)), dx2, "scatter_l2")
    dx1, g1 = gdn_bwd(dx2, W1, s1, "l1", after=t2)
    k1, h1, t1 = scatter_start(big_grad_pieces((None, g1, None, None)), dx1, "scatter_l1")
    def swap_start(part, tag):
        keys = list(part)
        ins = [part[k] for k in keys]
        pieces = [whole_piece(i) for i in range(len(keys))]
        sems, ins, lands, token = exchange_start(pieces, ins, [_sds(a.shape, a.dtype) for a in ins], ins[0], tag + "_start", sibling=True)
        swaps.append((keys, (pieces, sems, ins, lands), tag))
        return token

    last, swaps = [], []

    def emit_l0(grads):
        first = not last
        now = next(iter(grads.values()))
        early = [(k3, h3, "scatter_l3"), (k2, h2, "scatter_l2")] if first else [(k1, h1, "scatter_l1")]
        landed = {}
        for keys, handle, tag in early:
            landed.update(scattered(keys, handle, now, tag))
        swapping = swap_start(landed, "swap_a" if first else "swap_b")
        tag = "scatter_l0a" if first else "scatter_l0b"
        keys, handle, token = scatter_start({("pool_" + k, 0): a for k, a in grads.items()}, swapping, tag)
        last.append((keys, handle, tag))
        return token

    dx0, g0 = pool_bwd(dx1, W0, s0, "l0", after=t1, emit=emit_l0)
    landed = {}
    for keys, handle, tag in last:
        landed.update(scattered(keys, handle, dx0, tag))
    swapping = swap_start(landed, "swap_c")
    recv, sib = {}, {}
    for keys, handle, tag in swaps:
        mine_, theirs = exchange_wait(*handle, dx0, tag + "_wait", sibling=True)
        recv.update(zip(keys, mine_))
        sib.update(zip(keys, theirs))

    sg = small_grads((g0, g1, g2, g3), dfinal)
    small_names = SMALL_SHARDED + REPLICATED
    small_buf = _pack([sg[n] for n in small_names] + [loss_part], F32, 8)
    small_sum = sum_slots(exchange_all(small_buf, "gather_small"), "sum_small")
    full_small = _unpack(small_sum, [sg[n].shape for n in small_names] + [(1, LANES)])
    loss = full_small[-1][0, 0]
    small_part = {}
    for n, a in zip(small_names, full_small[:-1]):
        if n in SMALL_AXIS:
            a = lax.dynamic_index_in_dim(_to_shards(a, SMALL_AXIS[n]), my_chip, axis=0, keepdims=False)
        small_part[n] = a

    outs = []
    for n in NAMES:
        shp = w[n].shape
        two = (int(np.prod(shp[:-1])), shp[-1]) if len(shp) > 1 else (1, shp[0])
        if n in BIG_LAYOUT:
            layers = shp[0]
            rows = lambda a: a.reshape(4, two[0] // layers, two[1])
            parts = [[rows(recv[(n, l)]) for l in range(layers)], [rows(sib[(n, l)]) for l in range(layers)]]
        else:
            parts = [[small_part[n].reshape((1,) + two)]]
        upper = n in BIG_LAYOUT and not n.startswith("pool")
        res = adamw(w[n].reshape(two), parts, m[n].reshape(two), v[n].reshape(two), "adamw_" + n, after=swapping if upper else None)
        outs.append([r.reshape(shp) for r in res])
    return (loss, dx0[None], *[o[0] for o in outs], *[o[1] for o in outs], *[o[2] for o in outs], *[o[3] for o in outs])
```

```python
import math

import jax
import jax.numpy as jnp
import numpy as np
from jax import lax
from jax.experimental import pallas as pl
from jax.experimental.pallas import tpu as pltpu

F32 = jnp.float32
BF16 = jnp.bfloat16
I32 = jnp.int32

D = 1024
EPS = 1e-6
POOL_WIDTH = 2048
POOL_GROUP = 512
GDN_H, GDN_DK, GDN_DV, GDN_C = 8, 128, 256, 64
GDN_QK, GDN_V, GDN_CONV_CH, GDN_IN = 1024, 2048, 4096, 6160
GDN_IN_PAD = 6272
MLA_H, MLA_NOPE, MLA_ROPE, MLA_V = 16, 128, 64, 128
MLA_Q_LORA, MLA_KV_LORA, MLA_WIDTH, MLA_IN = 768, 512, 2048, 3392
MLA_IN_PAD = 4096
MLA_SCALE = (MLA_NOPE + MLA_ROPE) ** -0.5
ROPE_THETA = 10000.0
ADAM_LR, ADAM_B1, ADAM_B2, ADAM_EPS, ADAM_WD, ADAM_STEP = 0.001, 0.9, 0.999, 1e-08, 0.01, 10

VMEM_LIMIT_V7X = 56 * 1024 * 1024
LANES = 128
MESH = pl.DeviceIdType.MESH


def _pc(body, **kw):
    return pl.pallas_call(body, **kw)


def _cparams(sem):
    return pltpu.CompilerParams(dimension_semantics=sem, vmem_limit_bytes=VMEM_LIMIT_V7X)


def _tile(n, cap):
    t = (cap // LANES) * LANES
    while t >= LANES:
        if n % t == 0:
            return t
        t -= LANES
    return n


def _sds(shape, dt):
    return jax.ShapeDtypeStruct(shape, dt)


def mm(a, b, *, ta=False, tb=False, add=None, after=None, out_dtype=F32, name):
    if ta:
        K, M = a.shape
    else:
        M, K = a.shape
    if tb:
        N, K2 = b.shape
    else:
        K2, N = b.shape
    assert K == K2, (a.shape, b.shape, ta, tb)
    tm, tn, tk = _tile(M, 1024), _tile(N, 1024), _tile(K, 1024)
    nk = K // tk
    a_spec = pl.BlockSpec((tk, tm), lambda i, j, k: (k, i)) if ta else pl.BlockSpec((tm, tk), lambda i, j, k: (i, k))
    b_spec = pl.BlockSpec((tn, tk), lambda i, j, k: (j, k)) if tb else pl.BlockSpec((tk, tn), lambda i, j, k: (k, j))
    o_spec = pl.BlockSpec((tm, tn), lambda i, j, k: (i, j))
    dn = (((0 if ta else 1,), (1 if tb else 0,)), ((), ()))
    has_add = add is not None

    def body(*refs):
        a_ref, b_ref = refs[0], refs[1]
        part = lax.dot_general(a_ref[...].astype(BF16), b_ref[...].astype(BF16), dn, preferred_element_type=F32)
        if nk == 1:
            refs[-1][...] = (part + refs[2][...] if has_add else part).astype(out_dtype)
            return
        o_ref, acc = refs[-2], refs[-1]
        k = pl.program_id(2)

        @pl.when(k == 0)
        def _():
            acc[...] = part

        @pl.when(k > 0)
        def _():
            acc[...] += part

        @pl.when(k == nk - 1)
        def _():
            r = acc[...]
            if has_add:
                r = r + refs[2][...]
            o_ref[...] = r.astype(out_dtype)

    ins = [a, b] + ([add] if has_add else []) + ([after] if after is not None else [])
    specs = [a_spec, b_spec] + ([o_spec] if has_add else []) + ([pl.BlockSpec(memory_space=pl.ANY)] if after is not None else [])
    return _pc(body, grid=(M // tm, N // tn, nk), in_specs=specs, out_specs=o_spec, out_shape=_sds((M, N), out_dtype),
               scratch_shapes=[pltpu.VMEM((tm, tn), F32)] if nk > 1 else [], compiler_params=_cparams(("parallel", "parallel", "arbitrary")),
               name=name)(*ins)


def gmm(kind, a, b, *, G, name, out_dtype=F32):
    S_ = a.shape[0]
    Ka = a.shape[1] // G
    if kind == "tn":
        N = b.shape[1] // G
        tk = _tile(S_, 2048)
        nk = S_ // tk

        def body(a_ref, b_ref, o_ref, acc):
            k = pl.program_id(1)

            @pl.when(k == 0)
            def _():
                acc[...] = jnp.zeros_like(acc)

            acc[...] += lax.dot_general(a_ref[...].astype(BF16), b_ref[...].astype(BF16), (((0,), (0,)), ((), ())),
                                        preferred_element_type=F32)

            @pl.when(k == nk - 1)
            def _():
                o_ref[...] = acc[...].astype(out_dtype)

        return _pc(body, grid=(G, nk),
                   in_specs=[pl.BlockSpec((tk, Ka), lambda g, k: (k, g)), pl.BlockSpec((tk, N), lambda g, k: (k, g))],
                   out_specs=pl.BlockSpec((None, Ka, N), lambda g, k: (g, 0, 0)), out_shape=_sds((G, Ka, N), out_dtype),
                   scratch_shapes=[pltpu.VMEM((Ka, N), F32)], compiler_params=_cparams(("parallel", "arbitrary")), name=name)(a, b)
    N = b.shape[2] if kind == "nn" else b.shape[1]
    tm = _tile(S_, 4096)
    dn = (((1,), (0 if kind == "nn" else 1,)), ((), ()))

    def body(a_ref, b_ref, o_ref):
        o_ref[...] = lax.dot_general(a_ref[...].astype(BF16), b_ref[...].astype(BF16), dn, preferred_element_type=F32).astype(out_dtype)

    bshape = (None,) + tuple(b.shape[1:])
    return _pc(body, grid=(G, S_ // tm),
               in_specs=[pl.BlockSpec((tm, Ka), lambda g, i: (i, g)), pl.BlockSpec(bshape, lambda g, i: (g, 0, 0))],
               out_specs=pl.BlockSpec((tm, N), lambda g, i: (i, g)), out_shape=_sds((S_, G * N), out_dtype),
               compiler_params=_cparams(("parallel", "parallel")), name=name)(a, b)


def _rw_spec(ts, w, c, s):
    return pl.BlockSpec((ts, w), lambda j, i: (i, c + j * s))


def _rw_pspec(p, w, c, s):
    return pl.BlockSpec((p.shape[0], w), lambda j, i: (0, c + j * s))


def rowwise(f, tiles, params, outs, *, ncol=1, ts, name):
    S_ = tiles[0][0].shape[0]
    nin = len(tiles) + len(params)

    def body(*refs):
        res = f(pl.program_id(0), *[r[...].astype(F32) for r in refs[:nin]])
        for r, o in zip(refs[nin:], res):
            r[...] = o.astype(r.dtype)

    return _pc(body, grid=(ncol, S_ // ts),
               in_specs=[_rw_spec(ts, w, c, s) for (_, w, c, s) in tiles] + [_rw_pspec(*p) for p in params],
               out_specs=[_rw_spec(ts, w, 0, s) for (w, s, _) in outs],
               out_shape=[_sds((S_, w * (ncol if s else 1)), dt) for (w, s, dt) in outs],
               compiler_params=_cparams(("parallel", "parallel")), name=name)(*[t[0] for t in tiles], *[p[0] for p in params])


def rowwise_bwd(f, tiles, params, cots, *, need, adds=None, place=None, narrow=(), ncol=1, ts, name):
    S_ = tiles[0][0].shape[0]
    adds = adds or {}
    place = place or {}
    nt, npar, nc = len(tiles), len(params), len(cots)
    add_keys = sorted(adds)
    need_idx = [k for k in range(nt) if need[k]]
    into_keys = [k for k in need_idx if k in place and not isinstance(place[k][0], int)]
    n_extra = len(add_keys) + len(into_keys)

    def body(*refs):
        j, i = pl.program_id(0), pl.program_id(1)
        vals = [r[...].astype(F32) for r in refs[:nt + npar]]
        cvals = tuple(r[...].astype(F32) for r in refs[nt + npar:nt + npar + nc])
        add_refs = refs[nt + npar + nc:nt + npar + nc + len(add_keys)]
        out_refs = refs[nt + npar + nc + n_extra:]
        _, vjp = jax.vjp(lambda *v: tuple(f(j, *v)), *vals)
        grads = vjp(cvals)
        for n, k in enumerate(need_idx):
            g = grads[k]
            if k in adds:
                g = g + add_refs[add_keys.index(k)][...]
            out_refs[n][...] = g.astype(out_refs[n].dtype)
        for n in range(npar):
            ref = out_refs[len(need_idx) + n]
            first = (i == 0) if params[n][3] else jnp.logical_and(i == 0, j == 0)

            @pl.when(first)
            def _():
                ref[...] = jnp.zeros_like(ref)

            ref[...] += grads[nt + n]

    in_specs = ([_rw_spec(ts, w, c, s) for (_, w, c, s) in tiles] + [_rw_pspec(*p) for p in params]
                + [_rw_spec(ts, w, c, s) for (_, w, c, s) in cots] + [_rw_spec(ts, *adds[k][1:]) for k in add_keys]
                + [pl.BlockSpec(memory_space=pl.ANY) for _ in into_keys])
    out_specs, out_shape, aliases = [], [], {}
    for n, k in enumerate(need_idx):
        w, s = tiles[k][1], tiles[k][3]
        if k in place:
            dst, c0 = place[k]
            total = dst if isinstance(dst, int) else dst.shape[1]
            out_specs.append(_rw_spec(ts, w, c0, s))
            out_shape.append(_sds((S_, total), (BF16 if k in narrow else F32) if isinstance(dst, int) else dst.dtype))
            if k in into_keys:
                aliases[nt + npar + nc + len(add_keys) + into_keys.index(k)] = n
        else:
            out_specs.append(_rw_spec(ts, w, 0, s))
            out_shape.append(_sds((S_, w * (ncol if s else 1)), BF16 if k in narrow else F32))
    out_specs += [_rw_pspec(p[0], p[1], p[2], p[3]) for p in params]
    out_shape += [_sds(p[0].shape, F32) for p in params]
    res = _pc(body, grid=(ncol, S_ // ts), in_specs=in_specs, out_specs=out_specs, out_shape=out_shape,
              input_output_aliases=aliases, compiler_params=_cparams(("arbitrary", "arbitrary")), name=name)(
        *[t[0] for t in tiles], *[p[0] for p in params], *[c[0] for c in cots], *[adds[k][0] for k in add_keys],
        *[place[k][0] for k in into_keys])
    return list(res[:len(need_idx)]), list(res[len(need_idx):])


def _rms(x, g):
    r = lax.rsqrt(jnp.mean(x * x, axis=-1, keepdims=True) + EPS)
    return x * r * g


def _silu(x):
    return x * jax.nn.sigmoid(x)


@jax.custom_vjp
def _softplus(x):
    return jnp.maximum(x, 0.0) + jnp.log1p(jnp.exp(-jnp.abs(x)))


_softplus.defvjp(lambda x: (_softplus(x), x), lambda x, d: (d * jax.nn.sigmoid(x),))


def f_rms(j, x, g):
    return (_rms(x, g),)


def f_pool_gate(j, pg, gate, scale):
    return (pg * scale * _silu(gate),)


def f_ogate(j, o, gate):
    return (o * _silu(gate),)


def f_gdn_out(j, o, gate, g):
    return (_rms(o, g) * _silu(gate),)


def f_gdn_gates(j, ba, alog, dtb):
    lane = lax.broadcasted_iota(I32, (1, LANES), 1)
    gs, bs = [], []
    for h in range(GDN_H):
        eb = (lane == h).astype(F32)
        ea = (lane == GDN_H + h).astype(F32)
        b = jnp.sum(ba * eb, -1, keepdims=True)
        a = jnp.sum(ba * ea, -1, keepdims=True)
        al = jnp.sum(alog * eb, -1, keepdims=True)
        dt = jnp.sum(dtb * eb, -1, keepdims=True)
        g = -jnp.exp(al) * _softplus(a + dt)
        gs.append(jnp.broadcast_to(g, ba.shape))
        bs.append(jnp.broadcast_to(jax.nn.sigmoid(b), ba.shape))
    return jnp.concatenate(gs, 1), jnp.concatenate(bs, 1)


def _shift_dn(x, k):
    rows = lax.broadcasted_iota(I32, x.shape, 0)
    return jnp.where(rows < k, 0.0, pltpu.roll(x, k, 0))


def _shift_up(x, k):
    n = x.shape[0]
    rows = lax.broadcasted_iota(I32, x.shape, 0)
    return jnp.where(rows >= n - k, 0.0, pltpu.roll(x, n - k, 0))


def _pool_window(j):
    g = lax.div(j, POOL_GROUP // LANES)
    return jnp.where(g == 0, 2.0, jnp.where(g == 1, 4.0, jnp.where(g == 2, 8.0, 16.0))), g


def _pick(g, a2, a4, a8, a16):
    return jnp.where(g == 0, a2, jnp.where(g == 1, a4, jnp.where(g == 2, a8, a16)))


def pool_time_fwd(proj, name):
    S_ = proj.shape[0]

    def body(u_ref, p_ref):
        u = u_ref[...].astype(F32)
        w, g = _pool_window(pl.program_id(0))
        s2 = u + _shift_dn(u, 1)
        s4 = s2 + _shift_dn(s2, 2)
        s8 = s4 + _shift_dn(s4, 4)
        s16 = s8 + _shift_dn(s8, 8)
        t1 = (lax.broadcasted_iota(I32, u.shape, 0) + 1).astype(F32)
        p_ref[...] = (_pick(g, s2, s4, s8, s16) / jnp.minimum(t1, w) - u).astype(p_ref.dtype)

    return _pc(body, grid=(POOL_WIDTH // LANES,), in_specs=[pl.BlockSpec((S_, LANES), lambda j: (0, j))],
               out_specs=pl.BlockSpec((S_, LANES), lambda j: (0, j)), out_shape=_sds((S_, POOL_WIDTH), BF16),
               compiler_params=_cparams(("parallel",)), name=name)(proj)


def pool_time_bwd(dp, into, name):
    S_ = dp.shape[0]

    def body(dp_ref, _, du_ref):
        d = dp_ref[...].astype(F32)
        w, g = _pool_window(pl.program_id(0))
        t1 = (lax.broadcasted_iota(I32, d.shape, 0) + 1).astype(F32)
        q = d / jnp.minimum(t1, w)
        r2 = q + _shift_up(q, 1)
        r4 = r2 + _shift_up(r2, 2)
        r8 = r4 + _shift_up(r4, 4)
        r16 = r8 + _shift_up(r8, 8)
        du_ref[...] = (_pick(g, r2, r4, r8, r16) - d).astype(du_ref.dtype)

    return _pc(body, grid=(POOL_WIDTH // LANES,),
               in_specs=[pl.BlockSpec((S_, LANES), lambda j: (0, j)), pl.BlockSpec(memory_space=pl.ANY)],
               out_specs=pl.BlockSpec((S_, LANES), lambda j: (0, j)), out_shape=_sds(into.shape, into.dtype),
               input_output_aliases={1: 0}, compiler_params=_cparams(("parallel",)), name=name)(dp, into)


def _conv_post(j, a):
    n = a * lax.rsqrt(jnp.sum(a * a, axis=-1, keepdims=True) + EPS)
    nq = GDN_QK // LANES
    return jnp.where(j < nq, n * (GDN_DK ** -0.5), jnp.where(j < 2 * nq, n, a))


def _conv_taps(u):
    return [_shift_dn(u, 3), _shift_dn(u, 2), _shift_dn(u, 1), u]


def _conv_pre(taps, w):
    return w[0:1] * taps[0] + w[1:2] * taps[1] + w[2:3] * taps[2] + w[3:4] * taps[3]


def gdn_conv_fwd(proj, conv_w, name):
    S_ = proj.shape[0]

    def body(u_ref, w_ref, o_ref):
        o_ref[...] = _conv_post(pl.program_id(0), _silu(_conv_pre(_conv_taps(u_ref[...]), w_ref[...])))

    return _pc(body, grid=(GDN_CONV_CH // LANES,),
               in_specs=[pl.BlockSpec((S_, LANES), lambda j: (0, j)), pl.BlockSpec((8, LANES), lambda j: (0, j))],
               out_specs=pl.BlockSpec((S_, LANES), lambda j: (0, j)), out_shape=_sds((S_, GDN_CONV_CH), F32),
               compiler_params=_cparams(("parallel",)), name=name)(proj, conv_w)


def gdn_conv_bwd(proj, conv_w, dq, dk, dv, into, name):
    S_ = proj.shape[0]
    nq = GDN_QK // LANES

    def body(u_ref, w_ref, dq_ref, dk_ref, dv_ref, _, du_ref, dw_ref):
        j = pl.program_id(0)
        u, w = u_ref[...], w_ref[...]
        taps = _conv_taps(u)
        c = _conv_pre(taps, w)
        sig = jax.nn.sigmoid(c)
        dout = jnp.where(j < nq, dq_ref[...], jnp.where(j < 2 * nq, dk_ref[...], dv_ref[...]))
        _, vjp = jax.vjp(lambda a: _conv_post(j, a), c * sig)
        dc = vjp(dout)[0] * (sig * (1.0 + c * (1.0 - sig)))
        du = w[3:4] * dc + w[2:3] * _shift_up(dc, 1) + w[1:2] * _shift_up(dc, 2) + w[0:1] * _shift_up(dc, 3)
        du_ref[...] = du.astype(du_ref.dtype)
        rows = lax.broadcasted_iota(I32, (8, LANES), 0)
        dw = jnp.zeros((8, LANES), F32)
        for k in range(4):
            dw = dw + jnp.where(rows == k, jnp.sum(dc * taps[k], axis=0, keepdims=True), 0.0)
        dw_ref[...] = dw

    blk = lambda f: pl.BlockSpec((S_, LANES), f)
    return _pc(body, grid=(GDN_CONV_CH // LANES,),
               in_specs=[blk(lambda j: (0, j)), pl.BlockSpec((8, LANES), lambda j: (0, j)),
                         blk(lambda j: (0, jnp.minimum(j, nq - 1))), blk(lambda j: (0, jnp.clip(j - nq, 0, nq - 1))),
                         blk(lambda j: (0, jnp.clip(j - 2 * nq, 0, 2 * nq - 1))), pl.BlockSpec(memory_space=pl.ANY)],
               out_specs=[blk(lambda j: (0, j)), pl.BlockSpec((8, LANES), lambda j: (0, j))],
               out_shape=[_sds(into.shape, into.dtype), _sds((8, GDN_CONV_CH), F32)], input_output_aliases={5: 0},
               compiler_params=_cparams(("parallel",)), name=name)(proj, conv_w, dq, dk, dv, into)


_NN, _NT, _TN = ((1,), (0,)), ((1,), (1,)), ((0,), (0,))


def _split(x, n):
    parts = []
    for _ in range(n):
        h = x.astype(BF16)
        parts.append(h)
        x = x - h.astype(F32)
    return parts


def _dot(a, b, dn, mode):
    d = lambda p, q: lax.dot_general(p, q, (dn, ((), ())), preferred_element_type=F32)
    if mode == "lo":
        return d(a.astype(BF16), b.astype(BF16))
    if mode == "x3":
        (ah, al), (bh, bl) = _split(a, 2), _split(b, 2)
        return d(ah, bh) + (d(ah, bl) + d(al, bh))
    b0, b1, b2 = _split(b, 3)
    ab = a.astype(BF16)
    return d(ab, b0) + (d(ab, b1) + d(ab, b2))


def _make_dots(mode):
    @jax.custom_vjp
    def nn(a, b):
        return _dot(a, b, _NN, mode)

    @jax.custom_vjp
    def nt(a, b):
        return _dot(a, b, _NT, mode)

    @jax.custom_vjp
    def tn(a, b):
        return _dot(a, b, _TN, mode)

    nn.defvjp(lambda a, b: (nn(a, b), (a, b)), lambda r, d: (nt(d, r[1]), tn(r[0], d)))
    nt.defvjp(lambda a, b: (nt(a, b), (a, b)), lambda r, d: (nn(d, r[1]), tn(d, r[0])))
    tn.defvjp(lambda a, b: (tn(a, b), (a, b)), lambda r, d: (nt(r[1], d), nn(r[0], d)))
    return nn, nt, tn


_nn_hi, _nt_hi, _tn_hi = _make_dots("x3")
_nn_lo, _nt_lo, _tn_lo = _make_dots("lo")


@jax.custom_vjp
def _nn_const(a, b):
    return _dot(a, b, _NN, "xl")


_nn_const.defvjp(lambda a, b: (_nn_const(a, b), a), lambda a, d: (jnp.zeros_like(a), _dot(a, d, _TN, "xl")))


def _each(f, *lists):
    return [f(*xs) for xs in zip(*lists)]


@jax.custom_vjp
def _unit_inverses(xs):
    C = xs[0].shape[0]
    eye = (lax.broadcasted_iota(I32, (C, C), 0) == lax.broadcasted_iota(I32, (C, C), 1)).astype(F32)
    ainv, p = [eye + a for a in xs], xs
    for _ in range(int(math.log2(C)) - 1):
        p = _each(lambda a: _dot(a, a, _NN, "x3"), p)
        ainv = _each(lambda a, b: a + _dot(a, b, _NN, "x3"), ainv, p)
    return ainv


def _unit_inverses_bwd(ainv, d):
    left = _each(lambda a, g: _dot(a, g, _TN, "x3"), ainv, d)
    return (_each(lambda t, a: _dot(t, a, _NT, "x3"), left, ainv),)


_unit_inverses.defvjp(lambda xs: (lambda a: (a, a))(_unit_inverses(xs)), _unit_inverses_bwd)


def _gdn_chunk(q, k, v, gb, bb, state):
    C = GDN_C
    e0 = (lax.broadcasted_iota(I32, (1, LANES), 1) == 0).astype(F32)
    ri = lax.broadcasted_iota(I32, (C, C), 0)
    ci = lax.broadcasted_iota(I32, (C, C), 1)
    causal, strict = ri >= ci, ri > ci
    tri, eye, ones = causal.astype(F32), (ri == ci).astype(F32), jnp.ones((C, C), F32)
    last = lax.broadcasted_iota(I32, (C, LANES), 0) == C - 1
    g1 = _each(lambda a: jnp.sum(a * e0, -1, keepdims=True), gb)
    b1 = _each(lambda a: jnp.sum(a * e0, -1, keepdims=True), bb)
    gc_c = _each(lambda g: _nn_const(tri, jnp.broadcast_to(g, (C, C))), g1)
    gc_d = _each(lambda g: _nn_const(tri, jnp.broadcast_to(g, (C, LANES))), g1)
    gr_c = _each(lambda g: _nn_const(ones, eye * g), gc_c)
    decay = _each(lambda a, r: jnp.where(causal, jnp.exp(jnp.where(causal, a - r, 0.0)), 0.0), gc_c, gr_c)
    kb = _each(lambda a, b: a * b, k, b1)
    vb = _each(lambda a, b: a * b, v, b1)
    x = _each(lambda a, b, d: -jnp.where(strict, _nt_lo(a, b) * d, 0.0), kb, k, decay)
    ainv = _unit_inverses(x)
    u = _each(_nn_hi, ainv, vb)
    w = _each(lambda a, b, g: _nn_hi(a, b * jnp.exp(g)), ainv, kb, gc_d)
    attn = _each(lambda a, b, d: jnp.where(causal, _nt_lo(a, b) * d, 0.0), q, k, decay)
    v_new = _each(lambda a, b, s: a - _nn_lo(b, s), u, w, state)
    o = _each(lambda a, g, s, t, vn: _nn_lo(a * jnp.exp(g), s) + _nn_lo(t, vn), q, gc_d, state, attn, v_new)
    gl = _each(lambda g: jnp.sum(jnp.where(last, g, 0.0), axis=0, keepdims=True), gc_d)
    new_state = _each(lambda s, g, a, gd, vn: s * jnp.exp(jnp.sum(g * e0, -1, keepdims=True)) + _tn_lo(a * jnp.exp(g - gd), vn),
                      state, gl, k, gc_d, v_new)
    return o, new_state


def _head_slices(ref, width):
    return [ref[:, h * width:(h + 1) * width] for h in range(GDN_H)]


def gdn_chunk_fwd(qkv, g_b, beta_b, name):
    S_ = qkv.shape[0]
    N = S_ // GDN_C

    def body(q_ref, k_ref, v_ref, g_ref, b_ref, o_ref, s_ref, state):
        @pl.when(pl.program_id(0) == 0)
        def _():
            state[...] = jnp.zeros_like(state)

        st = [state[h] for h in range(GDN_H)]
        s_ref[0] = state[...]
        o, st2 = _gdn_chunk(_head_slices(q_ref, GDN_DK), _head_slices(k_ref, GDN_DK), _head_slices(v_ref, GDN_DV),
                            _head_slices(g_ref, GDN_DK), _head_slices(b_ref, GDN_DK), st)
        for h in range(GDN_H):
            o_ref[:, h * GDN_DV:(h + 1) * GDN_DV] = o[h]
            state[h] = st2[h]

    return _pc(body, grid=(N,),
               in_specs=[pl.BlockSpec((GDN_C, GDN_QK), lambda n: (n, 0)), pl.BlockSpec((GDN_C, GDN_QK), lambda n: (n, 1)),
                         pl.BlockSpec((GDN_C, GDN_V), lambda n: (n, 1)), pl.BlockSpec((GDN_C, GDN_QK), lambda n: (n, 0)),
                         pl.BlockSpec((GDN_C, GDN_QK), lambda n: (n, 0))],
               out_specs=[pl.BlockSpec((GDN_C, GDN_V), lambda n: (n, 0)),
                          pl.BlockSpec((1, GDN_H, GDN_DK, GDN_DV), lambda n: (n, 0, 0, 0))],
               out_shape=[_sds((S_, GDN_V), F32), _sds((N, GDN_H, GDN_DK, GDN_DV), F32)],
               scratch_shapes=[pltpu.VMEM((GDN_H, GDN_DK, GDN_DV), F32)],
               compiler_params=_cparams(("arbitrary",)), name=name)(qkv, qkv, qkv, g_b, beta_b)


def gdn_chunk_bwd(qkv, g_b, beta_b, states, do, name):
    S_ = qkv.shape[0]
    N = S_ // GDN_C

    def body(q_ref, k_ref, v_ref, g_ref, b_ref, s_ref, do_ref, dq_ref, dk_ref, dv_ref, dg_ref, db_ref, dstate):
        @pl.when(pl.program_id(0) == 0)
        def _():
            dstate[...] = jnp.zeros_like(dstate)

        _, vjp = jax.vjp(_gdn_chunk, _head_slices(q_ref, GDN_DK), _head_slices(k_ref, GDN_DK), _head_slices(v_ref, GDN_DV),
                         _head_slices(g_ref, GDN_DK), _head_slices(b_ref, GDN_DK), [s_ref[0, h] for h in range(GDN_H)])
        dq, dk, dv, dg, db, ds = vjp((_head_slices(do_ref, GDN_DV), [dstate[h] for h in range(GDN_H)]))
        for h in range(GDN_H):
            kk, vv = slice(h * GDN_DK, (h + 1) * GDN_DK), slice(h * GDN_DV, (h + 1) * GDN_DV)
            dq_ref[:, kk] = dq[h]
            dk_ref[:, kk] = dk[h]
            dv_ref[:, vv] = dv[h]
            dg_ref[:, kk] = dg[h]
            db_ref[:, kk] = db[h]
            dstate[h] = ds[h]

    r = lambda n: N - 1 - n
    qk = lambda c: pl.BlockSpec((GDN_C, GDN_QK), lambda n: (r(n), c))
    vs = lambda c: pl.BlockSpec((GDN_C, GDN_V), lambda n: (r(n), c))
    return _pc(body, grid=(N,),
               in_specs=[qk(0), qk(1), vs(1), qk(0), qk(0),
                         pl.BlockSpec((1, GDN_H, GDN_DK, GDN_DV), lambda n: (r(n), 0, 0, 0)), vs(0)],
               out_specs=[qk(0), qk(0), vs(0), qk(0), qk(0)],
               out_shape=[_sds((S_, GDN_QK), F32), _sds((S_, GDN_QK), F32), _sds((S_, GDN_V), F32),
                          _sds((S_, GDN_QK), F32), _sds((S_, GDN_QK), F32)],
               scratch_shapes=[pltpu.VMEM((GDN_H, GDN_DK, GDN_DV), F32)],
               compiler_params=_cparams(("arbitrary",)), name=name)(qkv, qkv, qkv, g_b, beta_b, states, do)


def _rope_tables(pos_ref, inv_ref, cm_ref, sg_ref):
    ang = pos_ref[...] * inv_ref[...]
    return jnp.cos(ang) * cm_ref[...], jnp.sin(ang) * sg_ref[...]


def mla_prep_fwd(qpad, kv, proj, pos, rope_consts, name):
    S_ = qpad.shape[0]
    ts = 512
    W = 2 * LANES

    def body(q_ref, kv_ref, kr_ref, pos_ref, inv_ref, cm_ref, sg_ref, qh_ref, kh_ref, vh_ref):
        cs, sn = _rope_tables(pos_ref, inv_ref, cm_ref, sg_ref)
        rope = lambda r: r * cs + pltpu.roll(r, LANES // 2, 1) * sn
        krr = rope(kr_ref[...].astype(F32)).astype(BF16)
        for h in range(MLA_H):
            qh_ref[h, :, 0:LANES] = (q_ref[:, h * W:h * W + LANES].astype(F32) * MLA_SCALE).astype(BF16)
            qh_ref[h, :, LANES:W] = (rope(q_ref[:, h * W + LANES:(h + 1) * W].astype(F32)) * MLA_SCALE).astype(BF16)
            kh_ref[h, :, 0:LANES] = kv_ref[:, h * W:h * W + LANES].astype(BF16)
            kh_ref[h, :, LANES:W] = krr
            vh_ref[h] = kv_ref[:, h * W + LANES:(h + 1) * W].astype(BF16)

    one = pl.BlockSpec((1, LANES), lambda i: (0, 0))
    return _pc(body, grid=(S_ // ts,),
               in_specs=[pl.BlockSpec((ts, MLA_H * W), lambda i: (i, 0)), pl.BlockSpec((ts, MLA_H * W), lambda i: (i, 0)),
                         pl.BlockSpec((ts, LANES), lambda i: (i, 1536 // LANES)), pl.BlockSpec((ts, 1), lambda i: (i, 0)),
                         one, one, one],
               out_specs=[pl.BlockSpec((MLA_H, ts, W), lambda i: (0, i, 0)), pl.BlockSpec((MLA_H, ts, W), lambda i: (0, i, 0)),
                          pl.BlockSpec((MLA_H, ts, LANES), lambda i: (0, i, 0))],
               out_shape=[_sds((MLA_H, S_, W), BF16), _sds((MLA_H, S_, W), BF16), _sds((MLA_H, S_, LANES), BF16)],
               compiler_params=_cparams(("parallel",)), name=name)(qpad, kv, proj, pos, *rope_consts)


def mla_prep_bwd(dqh, dkh, dvh, pos, rope_consts, into, name):
    S_ = dqh.shape[1]
    ts = 512
    W = 2 * LANES

    def body(dq_ref, dk_ref, dv_ref, pos_ref, inv_ref, cm_ref, sg_ref, _, dqp_ref, dkv_ref, dkr_ref):
        cs, sn = _rope_tables(pos_ref, inv_ref, cm_ref, sg_ref)
        rope_t = lambda g: g * cs + pltpu.roll(g * sn, LANES // 2, 1)
        acc = jnp.zeros((ts, LANES), F32)
        for h in range(MLA_H):
            dqp_ref[:, h * W:h * W + LANES] = (dq_ref[h, :, 0:LANES].astype(F32) * MLA_SCALE).astype(BF16)
            dqp_ref[:, h * W + LANES:(h + 1) * W] = (rope_t(dq_ref[h, :, LANES:W].astype(F32)) * MLA_SCALE).astype(BF16)
            dkv_ref[:, h * W:h * W + LANES] = dk_ref[h, :, 0:LANES]
            dkv_ref[:, h * W + LANES:(h + 1) * W] = dv_ref[h]
            acc = acc + dk_ref[h, :, LANES:W].astype(F32)
        dkr_ref[...] = rope_t(acc).astype(dkr_ref.dtype)

    one = pl.BlockSpec((1, LANES), lambda i: (0, 0))
    return _pc(body, grid=(S_ // ts,),
               in_specs=[pl.BlockSpec((MLA_H, ts, W), lambda i: (0, i, 0)), pl.BlockSpec((MLA_H, ts, W), lambda i: (0, i, 0)),
                         pl.BlockSpec((MLA_H, ts, LANES), lambda i: (0, i, 0)), pl.BlockSpec((ts, 1), lambda i: (i, 0)),
                         one, one, one, pl.BlockSpec(memory_space=pl.ANY)],
               out_specs=[pl.BlockSpec((ts, MLA_H * W), lambda i: (i, 0)), pl.BlockSpec((ts, MLA_H * W), lambda i: (i, 0)),
                          pl.BlockSpec((ts, LANES), lambda i: (i, 1536 // LANES))],
               out_shape=[_sds((S_, MLA_H * W), BF16), _sds((S_, MLA_H * W), BF16), _sds(into.shape, into.dtype)],
               input_output_aliases={7: 2}, compiler_params=_cparams(("parallel",)), name=name)(dqh, dkh, dvh, pos, *rope_consts, into)


NEG = -1e30


FLASH_TILE = 1024
FLASH_SUB = 256


def _scores(q, k, diagonal):
    s = lax.dot_general(q, k, (_NT, ((), ())), preferred_element_type=F32)
    if not diagonal:
        return s
    return jnp.where(lax.broadcasted_iota(I32, s.shape, 1) <= lax.broadcasted_iota(I32, s.shape, 0), s, NEG)


def _sub_blocks(t, diagonal):
    sub = min(FLASH_SUB, t) if diagonal else t
    return [(c * sub if diagonal else 0, slice(c * sub, (c + 1) * sub)) for c in range(t // sub)]


FLASH_HEADS = 2


def flash_fwd(qh, kh, vh, name):
    H, S_, W = qh.shape
    t = _tile(S_, FLASH_TILE)
    n = S_ // t
    G = FLASH_HEADS
    heads = list(range(G))

    def body(q_ref, k_ref, v_ref, o_ref, lse_ref, m_s, l_s, acc):
        qi, kj = pl.program_id(1), pl.program_id(2)

        @pl.when(kj == 0)
        def _():
            m_s[...] = jnp.full_like(m_s, NEG)
            l_s[...] = jnp.zeros_like(l_s)
            acc[...] = jnp.zeros_like(acc)

        def step(diagonal):
            s = _each(lambda a: _scores(q_ref[a], k_ref[a], diagonal), heads)
            m_old = _each(lambda a: m_s[a], heads)
            m_new = _each(lambda mo, sa: jnp.maximum(mo, jnp.max(sa, axis=-1, keepdims=True)), m_old, s)
            alpha = _each(lambda mo, mn: jnp.exp(mo - mn), m_old, m_new)
            p = _each(lambda sa, mn: jnp.exp(sa - mn[:, :1]), s, m_new)
            pv = _each(lambda pa, a: lax.dot_general(pa.astype(BF16), v_ref[a], (_NN, ((), ())), preferred_element_type=F32), p, heads)
            for a in heads:
                l_s[a] = alpha[a] * l_s[a] + jnp.sum(p[a], axis=-1, keepdims=True)
                acc[a] = alpha[a] * acc[a] + pv[a]
                m_s[a] = m_new[a]

        pl.when(kj < qi)(lambda: step(False))
        pl.when(kj == qi)(lambda: step(True))

        @pl.when(kj == n - 1)
        def _():
            for a in heads:
                o_ref[:, a * LANES:(a + 1) * LANES] = (acc[a] / l_s[a]).astype(o_ref.dtype)
                lse_ref[a] = m_s[a] + jnp.log(l_s[a])

    return _pc(body, grid=(H // G, n, n),
               in_specs=[pl.BlockSpec((G, t, W), lambda h, i, j: (h, i, 0)),
                         pl.BlockSpec((G, t, W), lambda h, i, j: (h, jnp.minimum(i, j), 0)),
                         pl.BlockSpec((G, t, LANES), lambda h, i, j: (h, jnp.minimum(i, j), 0))],
               out_specs=[pl.BlockSpec((t, G * LANES), lambda h, i, j: (i, h)), pl.BlockSpec((G, t, LANES), lambda h, i, j: (h, i, 0))],
               out_shape=[_sds((S_, H * LANES), BF16), _sds((H, S_, LANES), F32)],
               scratch_shapes=[pltpu.VMEM((G, t, LANES), F32)] * 3,
               compiler_params=_cparams(("parallel", "parallel", "arbitrary")), name=name)(qh, kh, vh)


def flash_bwd(qh, kh, vh, o, lse, do, name):
    H, S_, W = qh.shape
    t = _tile(S_, FLASH_TILE)
    n = S_ // t
    G = FLASH_HEADS
    heads = list(range(G))

    def body(q_ref, k_ref, v_ref, o_ref, lse_ref, do_ref, dq_ref, dk_ref, dv_ref, dq_acc, dk_acc, dv_acc):
        kj, qi = pl.program_id(1), pl.program_id(2)

        @pl.when(jnp.logical_and(kj == 0, qi == 0))
        def _():
            dq_acc[...] = jnp.zeros_like(dq_acc)

        @pl.when(qi == 0)
        def _():
            dk_acc[...] = jnp.zeros_like(dk_acc)
            dv_acc[...] = jnp.zeros_like(dv_acc)

        def step(diagonal):
            lanes = lambda a: slice(a * LANES, (a + 1) * LANES)
            do_ = _each(lambda a: do_ref[:, lanes(a)], heads)
            dob = _each(lambda d: d.astype(BF16), do_)
            delta = _each(lambda d, a: jnp.sum(d.astype(F32) * o_ref[:, lanes(a)].astype(F32), axis=-1, keepdims=True), do_, heads)
            for r0, keys in _sub_blocks(t, diagonal):
                p = _each(lambda a: jnp.exp(_scores(q_ref[a, r0:, :], k_ref[a, keys, :], diagonal) - lse_ref[a, r0:, :1]), heads)
                dp = _each(lambda d, a: lax.dot_general(d[r0:], v_ref[a, keys, :], (_NT, ((), ())), preferred_element_type=F32), dob, heads)
                ds = _each(lambda pa, dpa, de: (pa * (dpa - de[r0:])).astype(BF16), p, dp, delta)
                rows = pl.ds(pl.multiple_of(qi * t, t) + r0, t - r0)
                for a in heads:
                    dv_acc[a, keys, :] += lax.dot_general(p[a].astype(BF16), dob[a][r0:], (_TN, ((), ())), preferred_element_type=F32)
                    dk_acc[a, keys, :] += lax.dot_general(ds[a], q_ref[a, r0:, :], (_TN, ((), ())), preferred_element_type=F32)
                    dq_acc[a, rows, :] += lax.dot_general(ds[a], k_ref[a, keys, :], (_NN, ((), ())), preferred_element_type=F32)

        pl.when(qi > kj)(lambda: step(False))
        pl.when(qi == kj)(lambda: step(True))

        @pl.when(qi == n - 1)
        def _():
            dk_ref[...] = dk_acc[...].astype(BF16)
            dv_ref[...] = dv_acc[...].astype(BF16)

        @pl.when(jnp.logical_and(kj == n - 1, qi == n - 1))
        def _():
            dq_ref[...] = dq_acc[...].astype(BF16)

    qrow = lambda h, j, i: jnp.maximum(i, j)
    return _pc(body, grid=(H // G, n, n),
               in_specs=[pl.BlockSpec((G, t, W), lambda h, j, i: (h, qrow(h, j, i), 0)),
                         pl.BlockSpec((G, t, W), lambda h, j, i: (h, j, 0)),
                         pl.BlockSpec((G, t, LANES), lambda h, j, i: (h, j, 0)),
                         pl.BlockSpec((t, G * LANES), lambda h, j, i: (qrow(h, j, i), h)),
                         pl.BlockSpec((G, t, LANES), lambda h, j, i: (h, qrow(h, j, i), 0)),
                         pl.BlockSpec((t, G * LANES), lambda h, j, i: (qrow(h, j, i), h))],
               out_specs=[pl.BlockSpec((G, S_, W), lambda h, j, i: (h, 0, 0)),
                          pl.BlockSpec((G, t, W), lambda h, j, i: (h, j, 0)),
                          pl.BlockSpec((G, t, LANES), lambda h, j, i: (h, j, 0))],
               out_shape=[_sds((H, S_, W), BF16), _sds((H, S_, W), BF16), _sds((H, S_, LANES), BF16)],
               scratch_shapes=[pltpu.VMEM((G, S_, W), F32), pltpu.VMEM((G, t, W), F32), pltpu.VMEM((G, t, LANES), F32)],
               compiler_params=_cparams(("parallel", "arbitrary", "arbitrary")), name=name)(qh, kh, vh, o, lse, do)


def loss_head(x, target, g, name):
    S_ = x.shape[0]
    ts = 512

    def body(x_ref, t_ref, g_ref, l_ref, dx_ref, dg_ref):
        @pl.when(pl.program_id(0) == 0)
        def _():
            l_ref[...] = jnp.zeros_like(l_ref)
            dg_ref[...] = jnp.zeros_like(dg_ref)

        y, vjp = jax.vjp(_rms, x_ref[...], g_ref[...])
        err = y - t_ref[...]
        l_ref[...] += 0.5 * jnp.sum(jnp.sum(err * err, axis=-1, keepdims=True), axis=0, keepdims=True) / D
        dx, dg = vjp(err / D)
        dx_ref[...] = dx
        dg_ref[...] += dg

    row = pl.BlockSpec((ts, D), lambda i: (i, 0))
    return _pc(body, grid=(S_ // ts,), in_specs=[row, row, pl.BlockSpec((1, D), lambda i: (0, 0))],
               out_specs=[pl.BlockSpec((1, LANES), lambda i: (0, 0)), row, pl.BlockSpec((1, D), lambda i: (0, 0))],
               out_shape=[_sds((1, LANES), F32), _sds((S_, D), F32), _sds((1, D), F32)],
               compiler_params=_cparams(("arbitrary",)), name=name)(x, target, g)


def adamw(w, parts, m, v, name, after=None):
    R, C = w.shape
    rows = [p.shape[1] for p in parts[0]]
    tr = R
    for cand in (512, 256, 128, 64, 32, 16, 8):
        if all(r % cand == 0 for r in rows) and cand * C * 4 * len(rows) <= 2 * 1024 * 1024:
            tr = cand
            break
    c1 = 1.0 - ADAM_B1 ** ADAM_STEP
    c2 = 1.0 - ADAM_B2 ** ADAM_STEP
    starts = [sum(rows[:k]) // tr for k in range(len(rows))]
    flat = [p for part in parts for p in part]

    def body(*refs):
        w_ref, m_ref, v_ref = refs[0], refs[1 + len(flat)], refs[2 + len(flat)]
        g_ref, d_ref, nm_ref, nv_ref = refs[-4:]
        i = pl.program_id(0)
        gg, at = None, 1
        for part in parts:
            val = None
            for k in range(len(part)):
                p_ref = refs[at]
                at += 1
                s = p_ref[0].astype(F32)
                for n in range(1, p_ref.shape[0]):
                    s = s + p_ref[n].astype(F32)
                val = s if val is None else jnp.where(i >= starts[k], s, val)
            gg = val if gg is None else gg + val
        m2 = ADAM_B1 * m_ref[...] + (1.0 - ADAM_B1) * gg
        v2 = ADAM_B2 * v_ref[...] + (1.0 - ADAM_B2) * (gg * gg)
        g_ref[...] = gg
        d_ref[...] = -ADAM_LR * ((m2 / c1) / (jnp.sqrt(v2 / c2) + ADAM_EPS) + ADAM_WD * w_ref[...])
        nm_ref[...] = m2
        nv_ref[...] = v2

    blk = pl.BlockSpec((tr, C), lambda i: (i, 0))
    piece = lambda p, k: pl.BlockSpec((p.shape[0], tr, C), lambda i: (0, jnp.clip(i - starts[k], 0, rows[k] // tr - 1), 0))
    pblk = [piece(p, k) for part in parts for k, p in enumerate(part)]
    extra = [] if after is None else [after]
    return _pc(body, grid=(R // tr,), in_specs=[blk] + pblk + [blk, blk] + [pl.BlockSpec(memory_space=pl.ANY)] * len(extra),
               out_specs=[blk] * 4, out_shape=[_sds((R, C), F32)] * 4,
               compiler_params=_cparams(("parallel",)), name=name)(w, *flat, m, v, *extra)


def sum_slots(recv, name):
    n, R, C = recv.shape

    def body(r_ref, o_ref):
        acc = r_ref[0]
        for s in range(1, n):
            acc = acc + r_ref[s]
        o_ref[...] = acc

    return _pc(body, grid=(1,), in_specs=[pl.BlockSpec((n, R, C), lambda i: (0, 0, 0))],
               out_specs=pl.BlockSpec((R, C), lambda i: (0, 0)), out_shape=_sds((R, C), F32),
               compiler_params=_cparams(("arbitrary",)), name=name)(recv)


def _chip_peers():
    x, y, c = lax.axis_index("x"), lax.axis_index("y"), lax.axis_index("c")
    return (x, y, c), [(1 - x, 1 - y, c), (1 - x, y, c), (x, 1 - y, c)]


def _chip_index(p):
    return 2 * p[0] + p[1]


def _win(ref, axis, chip, size):
    if axis is None:
        return ref.at[chip]
    idx = [slice(None)] * len(ref.shape)
    idx[axis] = pl.ds(pl.multiple_of(chip * size, size), size)
    return ref.at[tuple(idx)]


def _remote(src, dst, send_sem, recv_sem, peer):
    return pltpu.make_async_remote_copy(src_ref=src, dst_ref=dst, send_sem=send_sem, recv_sem=recv_sem, device_id=peer,
                                        device_id_type=MESH)


HBM_SPEC = pl.BlockSpec(memory_space=pltpu.HBM)
SEM_SPEC = pl.BlockSpec(memory_space=pltpu.SEMAPHORE)
ANY_SPEC = pl.BlockSpec(memory_space=pl.ANY)
DATAFLOW = pltpu.SideEffectType.DATAFLOW_SIDE_EFFECTING


def gather_piece(i, l, o, axis, size):
    return (i, lambda r, chip: r.at[l], o, lambda r, chip: _win(r, axis, chip, size))


def scatter_piece(i, o, axis, size):
    return (i, lambda r, chip: _win(r, axis, chip, size), o, lambda r, chip: r.at[chip])


def whole_piece(i):
    return (i, lambda r, chip: r, i, lambda r, chip: r)


def _copies(pieces, in_refs, out_refs, send, recv, sibling):
    me, peers = _chip_peers()
    if sibling:
        peers = [(me[0], me[1], 1 - me[2])]
    mine = _chip_index(me)
    remote = []
    for n, (i, src, o, dst) in enumerate(pieces):
        d = dst(out_refs[o], mine)
        remote += [_remote(src(in_refs[i], _chip_index(p)), d, send.at[len(peers) * n + k], recv.at[len(peers) * n + k], p)
                   for k, p in enumerate(peers)]
    return remote


def own_window(a, axis, size, chip):
    if axis is None:
        return lax.dynamic_index_in_dim(a, chip, 0, keepdims=False)
    return lax.dynamic_slice_in_dim(a, chip * size, size, axis=axis)


def place_own(land, own, axis, size, chip):
    if axis is None:
        return lax.dynamic_update_slice_in_dim(land, own[None], chip, axis=0)
    return lax.dynamic_update_slice_in_dim(land, own, chip * size, axis=axis)


def exchange_start(pieces, ins, out_shapes, after, name, sibling=False):
    n_in, n_out, ncp = len(ins), len(out_shapes), len(pieces)

    def body(*refs):
        in_refs, land_refs = refs[:n_in], refs[n_in:n_in + n_out]
        send, recv = refs[n_in + n_out + 1], refs[n_in + n_out + 2]
        token = refs[-1]
        for cp in _copies(pieces, in_refs, land_refs, send, recv, sibling):
            cp.start()
        token[...] = jnp.zeros_like(token)

    hbm = lambda a: pltpu.with_memory_space_constraint(a, pltpu.HBM)
    lands = [hbm(lax.empty(s.shape, s.dtype)) for s in out_shapes]
    sem = pltpu.SemaphoreType.DMA(((1 if sibling else 3) * ncp,))
    thru = [pltpu.HBM(a.shape, a.dtype) for a in ins] + [pltpu.HBM(s.shape, s.dtype) for s in out_shapes]
    res = _pc(body, in_specs=[HBM_SPEC] * (n_in + n_out) + [ANY_SPEC],
              out_specs=[SEM_SPEC, SEM_SPEC] + [HBM_SPEC] * (n_in + n_out) + [pl.BlockSpec(memory_space=pltpu.VMEM)],
              out_shape=[sem, sem] + thru + [_sds((8, LANES), F32)],
              input_output_aliases={i: 2 + i for i in range(n_in + n_out)},
              compiler_params=pltpu.CompilerParams(has_side_effects=DATAFLOW), name=name)(*[hbm(a) for a in ins], *lands, after)
    return (res[0], res[1]), list(res[2:2 + n_in]), list(res[2 + n_in:2 + n_in + n_out]), res[-1]


def exchange_wait(pieces, sems, ins, lands, after, name, sibling=False):
    n_in, n_out = len(ins), len(lands)

    def body(*refs):
        in_refs, land_refs = refs[:n_in], refs[n_in:n_in + n_out]
        send, recv = refs[n_in + n_out], refs[n_in + n_out + 1]
        for cp in _copies(pieces, in_refs, land_refs, send, recv, sibling):
            cp.wait_send()
            cp.wait_recv()

    thru = [pltpu.HBM(a.shape, a.dtype) for a in ins] + [pltpu.HBM(a.shape, a.dtype) for a in lands]
    res = _pc(body, in_specs=[HBM_SPEC] * (n_in + n_out) + [SEM_SPEC, SEM_SPEC, ANY_SPEC], out_specs=[HBM_SPEC] * (n_in + n_out),
              out_shape=thru, input_output_aliases={i: i for i in range(n_in + n_out)},
              compiler_params=pltpu.CompilerParams(has_side_effects=DATAFLOW), name=name)(*ins, *lands, sems[0], sems[1], after)
    return list(res[:n_in]), list(res[n_in:])


def exchange_all(buf, name):
    def body(in_ref, out_ref, send, recv, local):
        x, y, c = lax.axis_index("x"), lax.axis_index("y"), lax.axis_index("c")
        mine = 4 * x + 2 * y + c
        loc = pltpu.make_async_copy(in_ref, out_ref.at[mine], local)
        loc.start()
        copies = [loc]
        for k in range(1, 8):
            peer = (x ^ (k >> 2), y ^ ((k >> 1) & 1), c ^ (k & 1))
            cp = pltpu.make_async_remote_copy(src_ref=in_ref, dst_ref=out_ref.at[mine], send_sem=send.at[k - 1],
                                              recv_sem=recv.at[k - 1], device_id=peer, device_id_type=MESH)
            cp.start()
            copies.append(cp)
        for cp in copies:
            cp.wait()

    anyspec = pl.BlockSpec(memory_space=pl.ANY)
    return _pc(body, in_specs=[anyspec], out_specs=anyspec, out_shape=_sds((8,) + buf.shape, buf.dtype),
               scratch_shapes=[pltpu.SemaphoreType.DMA((7,)), pltpu.SemaphoreType.DMA((7,)), pltpu.SemaphoreType.DMA],
               name=name)(buf)


def _norm_fwd(x, g, name):
    return rowwise(f_rms, [(x, D, 0, 0)], [(g, D, 0, 0)], [(D, 0, BF16)], ts=1024, name=name)[0]


def _norm_bwd(x, g, dh, dres, name):
    (dx,), (dg,) = rowwise_bwd(f_rms, [(x, D, 0, 0)], [(g, D, 0, 0)], [(dh, D, 0, 0)], need=[True],
                               adds={0: (dres, D, 0, 0)}, ts=1024, name=name)
    return dx, dg


def pool_fwd(x, W, tag, late=None):
    h = _norm_fwd(x, W["ng"], tag + "_norm")
    proj = mm(h, W["w_in"], out_dtype=BF16, name=tag + "_in")
    if late is not None:
        W = dict(W, **late(proj))
    p = pool_time_fwd(proj, tag + "_win")
    pg = gmm("nn", p, W["w_grp"], G=4, out_dtype=BF16, name=tag + "_grp")
    y = rowwise(f_pool_gate, [(pg, POOL_GROUP, 0, 1), (proj, POOL_GROUP, 4, 1)], [(W["scale"], POOL_GROUP, 0, 1)],
                [(POOL_GROUP, 1, BF16)], ncol=4, ts=1024, name=tag + "_gate")[0]
    xn = mm(y, W["w_out"], add=x, name=tag + "_out")
    return xn, (x, h, proj, p, pg, y)


def pool_bwd(dxn, W, saved, tag, after=None, emit=None):
    x, h, proj, p, pg, y = saved
    emit = emit or (lambda grads: None)
    dy = mm(dxn, W["w_out"], tb=True, after=after, out_dtype=BF16, name=tag + "_dy")
    g = {}
    (dpg, dproj), (g["scale"],) = rowwise_bwd(
        f_pool_gate, [(pg, POOL_GROUP, 0, 1), (proj, POOL_GROUP, 4, 1)], [(W["scale"], POOL_GROUP, 0, 1)],
        [(dy, POOL_GROUP, 0, 1)], need=[True, True], place={1: (2 * POOL_WIDTH, 4)}, narrow=(0, 1), ncol=4, ts=1024, name=tag + "_dgate")
    dp = gmm("nt", dpg, W["w_grp"], G=4, out_dtype=BF16, name=tag + "_dp")
    dproj = pool_time_bwd(dp, dproj, tag + "_dwin")
    g["w_in"] = mm(h, dproj, ta=True, out_dtype=BF16, name=tag + "_dw_in")
    t1 = emit({"w_in": g["w_in"]})
    g["w_out"] = mm(y, dxn, ta=True, after=t1, out_dtype=BF16, name=tag + "_dwout")
    g["w_grp"] = gmm("tn", p, dpg, G=4, out_dtype=BF16, name=tag + "_dwgrp")
    t2 = emit({"w_out": g["w_out"], "w_grp": g["w_grp"]})
    dh = mm(dproj, W["w_in"], tb=True, after=t2, name=tag + "_dh")
    dx, g["ng"] = _norm_bwd(x, W["ng"], dh, dxn, tag + "_dnorm")
    return dx, g


def gdn_fwd(x, W, tag, late=None):
    h = _norm_fwd(x, W["ng"], tag + "_norm")
    proj = mm(h, W["w_in"], name=tag + "_in")
    qkv = gdn_conv_fwd(proj, W["conv"], tag + "_conv")
    g_b, beta_b = rowwise(f_gdn_gates, [(proj, LANES, 6144 // LANES, 0)], [(W["a_log"], LANES, 0, 0), (W["dt_bias"], LANES, 0, 0)],
                          [(GDN_QK, 0, F32), (GDN_QK, 0, F32)], ts=1024, name=tag + "_gates")
    o, states = gdn_chunk_fwd(qkv, g_b, beta_b, tag + "_chunk")
    og = rowwise(f_gdn_out, [(o, GDN_DV, 0, 1), (proj, GDN_DV, 4096 // GDN_DV, 1)], [(W["norm_g"], GDN_DV, 0, 0)],
                 [(GDN_DV, 1, BF16)], ncol=GDN_H, ts=2048, name=tag + "_onorm")[0]
    if late is not None:
        W = dict(W, **late(og))
    xn = mm(og, W["w_out"], add=x, name=tag + "_out")
    return xn, (x, h, proj, qkv, g_b, beta_b, o, states, og)


def gdn_bwd(dxn, W, saved, tag, after=None):
    x, h, proj, qkv, g_b, beta_b, o, states, og = saved
    dog = mm(dxn, W["w_out"], tb=True, after=after, name=tag + "_dog")
    g = {"w_out": mm(og, dxn, ta=True, out_dtype=BF16, name=tag + "_dwout")}
    (do, dproj), (g["norm_g"],) = rowwise_bwd(
        f_gdn_out, [(o, GDN_DV, 0, 1), (proj, GDN_DV, 4096 // GDN_DV, 1)], [(W["norm_g"], GDN_DV, 0, 0)],
        [(dog, GDN_DV, 0, 1)], need=[True, True], place={1: (GDN_IN_PAD, 4096 // GDN_DV)}, narrow=(1,), ncol=GDN_H, ts=2048, name=tag + "_donorm")
    dq, dk, dv, dg_b, dbeta_b = gdn_chunk_bwd(qkv, g_b, beta_b, states, do, tag + "_dchunk")
    (dproj,), (g["a_log"], g["dt_bias"]) = rowwise_bwd(
        f_gdn_gates, [(proj, LANES, 6144 // LANES, 0)], [(W["a_log"], LANES, 0, 0), (W["dt_bias"], LANES, 0, 0)],
        [(dg_b, GDN_QK, 0, 0), (dbeta_b, GDN_QK, 0, 0)], need=[True], place={0: (dproj, 6144 // LANES)}, ts=1024, name=tag + "_dgates")
    dproj, g["conv"] = gdn_conv_bwd(proj, W["conv"], dq, dk, dv, dproj, tag + "_dconv")
    dh = mm(dproj, W["w_in"], tb=True, name=tag + "_dh")
    g["w_in"] = mm(h, dproj, ta=True, out_dtype=BF16, name=tag + "_dw_in")
    dx, g["ng"] = _norm_bwd(x, W["ng"], dh, dxn, tag + "_dnorm")
    return dx, g


def mla_fwd(x, pos, W, tag):
    h = _norm_fwd(x, W["ng"], tag + "_norm")
    proj = mm(h, W["w_in"], out_dtype=BF16, name=tag + "_in")
    hq = rowwise(f_rms, [(proj, MLA_Q_LORA, 0, 0)], [(W["q_g"], MLA_Q_LORA, 0, 0)], [(MLA_Q_LORA, 0, BF16)], ts=1024, name=tag + "_qnorm")[0]
    hkv = rowwise(f_rms, [(proj, MLA_KV_LORA, 2, 0)], [(W["kv_g"], MLA_KV_LORA, 0, 0)], [(MLA_KV_LORA, 0, BF16)], ts=1024, name=tag + "_kvnorm")[0]
    qpad = mm(hq, W["w_uq"], out_dtype=BF16, name=tag + "_uq")
    kv = mm(hkv, W["w_ukv"], out_dtype=BF16, name=tag + "_ukv")
    qh, kh, vh = mla_prep_fwd(qpad, kv, proj, pos, W["rope"], tag + "_prep")
    o, lse = flash_fwd(qh, kh, vh, tag + "_attn")
    og = rowwise(f_ogate, [(o, 512, 0, 1), (proj, 512, 4, 1)], [], [(512, 1, BF16)], ncol=4, ts=1024, name=tag + "_ogate")[0]
    xn = mm(og, W["w_out"], add=x, name=tag + "_out")
    return xn, (x, h, proj, hq, hkv, qh, kh, vh, o, lse, og)


def mla_bwd(dxn, pos, W, saved, tag, after=None):
    x, h, proj, hq, hkv, qh, kh, vh, o, lse, og = saved
    dog = mm(dxn, W["w_out"], tb=True, after=after, out_dtype=BF16, name=tag + "_dog")
    g = {"w_out": mm(og, dxn, ta=True, out_dtype=BF16, name=tag + "_dwout")}
    dproj = jnp.zeros(proj.shape, BF16)
    (do, dproj), _ = rowwise_bwd(f_ogate, [(o, 512, 0, 1), (proj, 512, 4, 1)], [], [(dog, 512, 0, 1)], need=[True, True],
                                 place={1: (dproj, 4)}, narrow=(0,), ncol=4, ts=1024, name=tag + "_dogate")
    dqh, dkh, dvh = flash_bwd(qh, kh, vh, o, lse, do, tag + "_dattn")
    dqpad, dkv, dproj = mla_prep_bwd(dqh, dkh, dvh, pos, W["rope"], dproj, tag + "_dprep")
    dhq = mm(dqpad, W["w_uq"], tb=True, name=tag + "_dhq")
    g["w_uq"] = mm(hq, dqpad, ta=True, out_dtype=BF16, name=tag + "_dwuq")
    dhkv = mm(dkv, W["w_ukv"], tb=True, name=tag + "_dhkv")
    g["w_ukv"] = mm(hkv, dkv, ta=True, out_dtype=BF16, name=tag + "_dwukv")
    (dproj,), (g["q_g"],) = rowwise_bwd(f_rms, [(proj, MLA_Q_LORA, 0, 0)], [(W["q_g"], MLA_Q_LORA, 0, 0)], [(dhq, MLA_Q_LORA, 0, 0)],
                                        need=[True], place={0: (dproj, 0)}, ts=512, name=tag + "_dqnorm")
    (dproj,), (g["kv_g"],) = rowwise_bwd(f_rms, [(proj, MLA_KV_LORA, 2, 0)], [(W["kv_g"], MLA_KV_LORA, 0, 0)], [(dhkv, MLA_KV_LORA, 0, 0)],
                                         need=[True], place={0: (dproj, 2)}, ts=512, name=tag + "_dkvnorm")
    dh = mm(dproj, W["w_in"], tb=True, name=tag + "_dh")
    g["w_in"] = mm(h, dproj, ta=True, out_dtype=BF16, name=tag + "_dw_in")
    dx, g["ng"] = _norm_bwd(x, W["ng"], dh, dxn, tag + "_dnorm")
    return dx, g


def _pad_cols(a, n):
    return jnp.pad(a, ((0, 0), (0, n - a.shape[1])))


def _mla_w_in_layout(w):
    z = lambda n: jnp.zeros((w.shape[0], n), w.dtype)
    kr = w[:, 1280:1344]
    return jnp.concatenate([w[:, :768], z(256), w[:, 768:1280], kr[:, :32], z(32), kr[:, 32:], z(32), z(384), w[:, 1344:]], axis=1)


def _mla_w_in_unlayout(g):
    return jnp.concatenate([g[:, :768], g[:, 1024:1536], g[:, 1536:1568], g[:, 1600:1632], g[:, 2048:]], axis=1)


def _mla_w_uq_layout(w):
    w3 = w.reshape(w.shape[0], MLA_H, MLA_NOPE + MLA_ROPE)
    z = jnp.zeros((w.shape[0], MLA_H, 32), w.dtype)
    return jnp.concatenate([w3[..., :128], w3[..., 128:160], z, w3[..., 160:192], z], axis=-1).reshape(w.shape[0], MLA_H * 256)


def _mla_w_uq_unlayout(g):
    g3 = g.reshape(g.shape[0], MLA_H, 256)
    return jnp.concatenate([g3[..., :128], g3[..., 128:160], g3[..., 192:224]], axis=-1).reshape(g.shape[0], MLA_H * 192)


def _rope_consts():
    half = MLA_ROPE // 2
    inv = ROPE_THETA ** (-jnp.arange(half, dtype=F32) / half)
    z = jnp.zeros((half,), F32)
    o = jnp.ones((half,), F32)
    row = lambda *p: jnp.concatenate(p).reshape(1, LANES)
    return row(inv, z, inv, z), row(o, z, o, z), row(-o, z, o, z)


BIG = ["pool_w_in", "pool_w_grp", "pool_w_out", "gdn_w_in", "gdn_w_out", "mla_w_in", "mla_w_uq", "mla_w_ukv", "mla_w_out"]
BIG_LAYOUT = {"pool_w_in": (1, 1024, (1024, 4096)), "pool_w_grp": (1, 128, (4, 512, 512)), "pool_w_out": (0, 512, (2048, 1024)),
              "gdn_w_in": (None, None, (4, 1024, 1540)), "gdn_w_out": (0, 512, (2048, 1024)),
              "mla_w_in": (None, None, (4, 1024, 848)), "mla_w_uq": (1, 768, (768, 3072)), "mla_w_ukv": (1, 1024, (512, 4096)),
              "mla_w_out": (0, 512, (2048, 1024))}
SMALL_SHARDED = ["pool_scale", "gdn_conv", "mla_q_norm_g", "mla_kv_norm_g"]
SMALL_AXIS = {"pool_scale": 1, "gdn_conv": 2, "mla_q_norm_g": 1, "mla_kv_norm_g": 1}
REPLICATED = ["norm_g", "gdn_a_log", "gdn_dt_bias", "gdn_norm_g", "final_g"]
PACK_C = 1024


def _pack(parts, dtype, row_mult):
    flat = jnp.concatenate([p.reshape(-1).astype(dtype) for p in parts])
    rows = -(-flat.shape[0] // PACK_C)
    rows = -(-rows // row_mult) * row_mult
    return jnp.pad(flat, (0, rows * PACK_C - flat.shape[0])).reshape(rows, PACK_C)


def _unpack(buf, shapes):
    lead = buf.shape[:-2]
    flat = buf.reshape(lead + (-1,))
    out, off = [], 0
    for s in shapes:
        n = int(np.prod(s))
        out.append(flat[..., off:off + n].reshape(lead + tuple(s)))
        off += n
    return out


def _unshard(g4, axis):
    a = jnp.moveaxis(g4, 0, axis)
    s = a.shape
    return a.reshape(s[:axis] + (s[axis] * s[axis + 1],) + s[axis + 2:])


def _to_shards(a, axis):
    s = a.shape
    return jnp.moveaxis(a.reshape(s[:axis] + (4, s[axis] // 4) + s[axis + 1:]), axis, 0)


def layer_weights(full, small, rep, layer):
    ng = rep["norm_g"][layer:layer + 1]
    side_by_side = lambda a4: jnp.moveaxis(a4, 0, 1).reshape(a4.shape[1], 4 * a4.shape[2])
    if layer in (0, 3):
        j = layer // 3
        return dict(ng=ng, w_in=full[("pool_w_in", j)], w_grp=full[("pool_w_grp", j)], scale=small["pool_scale"][j:j + 1],
                    w_out=full[("pool_w_out", j)])
    if layer == 1:
        return dict(ng=ng, w_in=_pad_cols(side_by_side(full[("gdn_w_in", 0)]), GDN_IN_PAD),
                    conv=jnp.pad(small["gdn_conv"][0], ((0, 4), (0, 0))), a_log=_pad_cols(rep["gdn_a_log"], LANES),
                    dt_bias=_pad_cols(rep["gdn_dt_bias"], LANES), norm_g=rep["gdn_norm_g"], w_out=full.get(("gdn_w_out", 0)))
    return dict(ng=ng, w_in=_mla_w_in_layout(side_by_side(full[("mla_w_in", 0)])), q_g=small["mla_q_norm_g"],
                kv_g=small["mla_kv_norm_g"], w_uq=_mla_w_uq_layout(full[("mla_w_uq", 0)]), w_ukv=full[("mla_w_ukv", 0)],
                w_out=full[("mla_w_out", 0)], rope=_rope_consts())


def big_grad_pieces(gl):
    g0, g1, g2, g3 = gl
    slots = lambda a: jnp.moveaxis(a.reshape(a.shape[0], 4, a.shape[1] // 4), 1, 0)
    out = {}
    for l, g in ((0, g0), (1, g3)):
        if g is not None:
            out.update({("pool_w_in", l): g["w_in"], ("pool_w_grp", l): g["w_grp"], ("pool_w_out", l): g["w_out"]})
    if g1 is not None:
        out.update({("gdn_w_in", 0): slots(g1["w_in"][:, :GDN_IN]), ("gdn_w_out", 0): g1["w_out"]})
    if g2 is not None:
        out.update({("mla_w_in", 0): slots(_mla_w_in_unlayout(g2["w_in"])), ("mla_w_uq", 0): _mla_w_uq_unlayout(g2["w_uq"]),
                    ("mla_w_ukv", 0): g2["w_ukv"], ("mla_w_out", 0): g2["w_out"]})
    return out


def small_grads(gl, dfinal):
    g0, g1, g2, g3 = gl
    return {"norm_g": jnp.concatenate([g0["ng"], g1["ng"], g2["ng"], g3["ng"]], axis=0),
            "pool_scale": jnp.concatenate([g0["scale"], g3["scale"]], axis=0), "gdn_conv": g1["conv"][None, :4],
            "gdn_a_log": g1["a_log"][:, :GDN_H], "gdn_dt_bias": g1["dt_bias"][:, :GDN_H], "gdn_norm_g": g1["norm_g"],
            "mla_q_norm_g": g2["q_g"], "mla_kv_norm_g": g2["kv_g"], "final_g": dfinal.reshape(D)}


NAMES = ["norm_g", "pool_w_in", "pool_w_grp", "pool_scale", "pool_w_out", "gdn_w_in", "gdn_conv", "gdn_a_log", "gdn_dt_bias",
         "gdn_norm_g", "gdn_w_out", "mla_w_in", "mla_q_norm_g", "mla_w_uq", "mla_kv_norm_g", "mla_w_ukv", "mla_w_out", "final_g"]


def kernel(x, positions, norm_g, pool_w_in, pool_w_grp, pool_scale, pool_w_out, gdn_w_in, gdn_conv, gdn_a_log, gdn_dt_bias, gdn_norm_g, gdn_w_out, mla_w_in, mla_q_norm_g, mla_w_uq, mla_kv_norm_g, mla_w_ukv, mla_w_out, final_g, loss_target, m_norm_g, m_pool_w_in, m_pool_w_grp, m_pool_scale, m_pool_w_out, m_gdn_w_in, m_gdn_conv, m_gdn_a_log, m_gdn_dt_bias, m_gdn_norm_g, m_gdn_w_out, m_mla_w_in, m_mla_q_norm_g, m_mla_w_uq, m_mla_kv_norm_g, m_mla_w_ukv, m_mla_w_out, m_final_g, v_norm_g, v_pool_w_in, v_pool_w_grp, v_pool_scale, v_pool_w_out, v_gdn_w_in, v_gdn_conv, v_gdn_a_log, v_gdn_dt_bias, v_gdn_norm_g, v_gdn_w_out, v_mla_w_in, v_mla_q_norm_g, v_mla_w_uq, v_mla_kv_norm_g, v_mla_w_ukv, v_mla_w_out, v_final_g):
    args = locals()
    w = {n: args[n] for n in NAMES}
    m = {n: args["m_" + n] for n in NAMES}
    v = {n: args["v_" + n] for n in NAMES}
    my_chip = (2 * lax.axis_index("x") + lax.axis_index("y")).astype(I32)
    S_ = x.shape[1]
    x0, pos, target = x[0], positions.reshape(S_, 1).astype(F32), loss_target[0]
    rep = {n: w[n] for n in REPLICATED}

    small_shapes = [w[n].shape for n in SMALL_SHARDED]
    small_pack = _pack([w[n] for n in SMALL_SHARDED], F32, 8)[None]
    layout = dict(BIG_LAYOUT, small=(None, None, (4,) + small_pack.shape[1:]))

    def shard(key, token):
        n, l = key
        if n == "small":
            return small_pack
        a = w[n][l:l + 1]
        return (a if token is None else a + token[0, 0]).astype(BF16)

    def gather_start(group, after, tag, token=None):
        pieces = [gather_piece(i, 0, i, layout[n][0], layout[n][1]) for i, (n, l) in enumerate(group)]
        ins = [shard(k, token) for k in group]
        shapes = [_sds(layout[n][2], a.dtype) for (n, l), a in zip(group, ins)]
        sems, ins, lands, token = exchange_start(pieces, ins, shapes, after, tag + "_start")
        return (pieces, sems, ins, lands), token

    def finish(handle, after, tag):
        return exchange_wait(*handle, after, tag + "_wait")

    def gathered(group, handle, after, tag):
        srcs, lands = finish(handle, after, tag)
        return {(n, l): place_own(a, s[0], layout[n][0], layout[n][1], my_chip) for (n, l), s, a in zip(group, srcs, lands)}

    group_a = [("small", 0), ("pool_w_in", 0)]
    group_a2 = [("pool_w_grp", 0), ("pool_w_out", 0)]
    group_b = [("gdn_w_in", 0)]
    group_c = [("gdn_w_out", 0), ("mla_w_in", 0), ("mla_w_uq", 0), ("mla_w_ukv", 0), ("mla_w_out", 0), ("pool_w_in", 1),
               ("pool_w_grp", 1), ("pool_w_out", 1)]
    full = {}
    h_a, t_a = gather_start(group_a, x0, "gather_a")
    h_a2, t_a2 = gather_start(group_a2, t_a, "gather_a2", t_a)
    h_b, t_b = gather_start(group_b, t_a2, "gather_b", t_a2)
    h_c, t_c = gather_start(group_c, t_b, "gather_c", t_b)
    full.update(gathered(group_a, h_a, t_c, "gather_a"))
    small = {n: _unshard(a, SMALL_AXIS[n]) for n, a in zip(SMALL_SHARDED, _unpack(full[("small", 0)], small_shapes))}

    def late_l0(proj):
        full.update(gathered(group_a2, h_a2, proj, "gather_a2"))
        return dict(w_grp=full[("pool_w_grp", 0)], w_out=full[("pool_w_out", 0)])

    first = dict(ng=rep["norm_g"][0:1] + t_c[0:1, 0:1], w_in=full[("pool_w_in", 0)], scale=small["pool_scale"][0:1])
    x1, s0 = pool_fwd(x0, first, "l0", late=late_l0)
    W0 = layer_weights(full, small, rep, 0)
    full.update(gathered(group_b, h_b, x1, "gather_b"))

    def late_l1(og):
        full.update(gathered(group_c, h_c, og, "gather_c"))
        return dict(w_out=full[("gdn_w_out", 0)])

    x2, s1 = gdn_fwd(x1, layer_weights(full, small, rep, 1), "l1", late=late_l1)
    W1, W2, W3 = (layer_weights(full, small, rep, i) for i in (1, 2, 3))
    x3, s2 = mla_fwd(x2, pos, W2, "l2")
    x4, s3 = pool_fwd(x3, W3, "l3")
    loss_part, dx4, dfinal = loss_head(x4, target, final_g.reshape(1, D), "loss_head")

    def scatter_start(pieces_of, after, tag):
        keys = list(pieces_of)
        pieces = [scatter_piece(i, i, BIG_LAYOUT[n][0], BIG_LAYOUT[n][1]) for i, (n, l) in enumerate(keys)]
        shapes = [_sds((4,) + tuple(w[n].shape[1:]), BF16) for n, l in keys]
        sems, ins, lands, token = exchange_start(pieces, [pieces_of[k] for k in keys], shapes, after, tag + "_start")
        return keys, (pieces, sems, ins, lands), token

    def scattered(keys, handle, after, tag):
        srcs, lands = finish(handle, after, tag)
        return {(n, l): place_own(a, own_window(g, BIG_LAYOUT[n][0], BIG_LAYOUT[n][1], my_chip), None, None, my_chip)
                for (n, l), g, a in zip(keys, srcs, lands)}

    dx3, g3 = pool_bwd(dx4, W3, s3, "l3")
    k3, h3, t3 = scatter_start(big_grad_pieces((None, None, None, g3)), dx3, "scatter_l3")
    dx2, g2 = mla_bwd(dx3, pos, W2, s2, "l2", after=t3)
    k2, h2, t2 = scatter_start(big_grad_pieces((None, None, g2, None)), dx2, "scatter_l2")
    dx1, g1 = gdn_bwd(dx2, W1, s1, "l1", after=t2)
    k1, h1, t1 = scatter_start(big_grad_pieces((None, g1, None, None)), dx1, "scatter_l1")
    def swap_start(part, tag):
        keys = list(part)
        ins = [part[k] for k in keys]
        pieces = [whole_piece(i) for i in range(len(keys))]
        sems, ins, lands, token = exchange_start(pieces, ins, [_sds(a.shape, a.dtype) for a in ins], ins[0], tag + "_start", sibling=True)
        swaps.append((keys, (pieces, sems, ins, lands), tag))
        return token

    last, swaps = [], []

    def emit_l0(grads):
        first = not last
        now = next(iter(grads.values()))
        early = [(k3, h3, "scatter_l3"), (k2, h2, "scatter_l2")] if first else [(k1, h1, "scatter_l1")]
        landed = {}
        for keys, handle, tag in early:
            landed.update(scattered(keys, handle, now, tag))
        swapping = swap_start(landed, "swap_a" if first else "swap_b")
        tag = "scatter_l0a" if first else "scatter_l0b"
        keys, handle, token = scatter_start({("pool_" + k, 0): a for k, a in grads.items()}, swapping, tag)
        last.append((keys, handle, tag))
        return token

    dx0, g0 = pool_bwd(dx1, W0, s0, "l0", after=t1, emit=emit_l0)
    landed = {}
    for keys, handle, tag in last:
        landed.update(scattered(keys, handle, dx0, tag))
    swapping = swap_start(landed, "swap_c")
    recv, sib = {}, {}
    for keys, handle, tag in swaps:
        mine_, theirs = exchange_wait(*handle, dx0, tag + "_wait", sibling=True)
        recv.update(zip(keys, mine_))
        sib.update(zip(keys, theirs))

    sg = small_grads((g0, g1, g2, g3), dfinal)
    small_names = SMALL_SHARDED + REPLICATED
    small_buf = _pack([sg[n] for n in small_names] + [loss_part], F32, 8)
    small_sum = sum_slots(exchange_all(small_buf, "gather_small"), "sum_small")
    full_small = _unpack(small_sum, [sg[n].shape for n in small_names] + [(1, LANES)])
    loss = full_small[-1][0, 0]
    small_part = {}
    for n, a in zip(small_names, full_small[:-1]):
        if n in SMALL_AXIS:
            a = lax.dynamic_index_in_dim(_to_shards(a, SMALL_AXIS[n]), my_chip, axis=0, keepdims=False)
        small_part[n] = a

    outs = []
    for n in NAMES:
        shp = w[n].shape
        two = (int(np.prod(shp[:-1])), shp[-1]) if len(shp) > 1 else (1, shp[0])
        if n in BIG_LAYOUT:
            layers = shp[0]
            rows = lambda a: a.reshape(4, two[0] // layers, two[1])
            parts = [[rows(recv[(n, l)]) for l in range(layers)], [rows(sib[(n, l)]) for l in range(layers)]]
        else:
            parts = [[small_part[n].reshape((1,) + two)]]
        upper = n in BIG_LAYOUT and not n.startswith("pool")
        res = adamw(w[n].reshape(two), parts, m[n].reshape(two), v[n].reshape(two), "adamw_" + n, after=swapping if upper else None)
        outs.append([r.reshape(shp) for r in res])
    return (loss, dx0[None], *[o[0] for o in outs], *[o[1] for o in outs], *[o[2] for o in outs], *[o[3] for o in outs])
```

```python
import math

import jax
import jax.numpy as jnp
import numpy as np
from jax import lax
from jax.experimental import pallas as pl
from jax.experimental.pallas import tpu as pltpu

F32 = jnp.float32
BF16 = jnp.bfloat16
I32 = jnp.int32

D = 1024
EPS = 1e-6
POOL_WIDTH = 2048
POOL_GROUP = 512
GDN_H, GDN_DK, GDN_DV, GDN_C = 8, 128, 256, 64
GDN_QK, GDN_V, GDN_CONV_CH, GDN_IN = 1024, 2048, 4096, 6160
GDN_IN_PAD = 6272
MLA_H, MLA_NOPE, MLA_ROPE, MLA_V = 16, 128, 64, 128
MLA_Q_LORA, MLA_KV_LORA, MLA_WIDTH, MLA_IN = 768, 512, 2048, 3392
MLA_IN_PAD = 4096
MLA_SCALE = (MLA_NOPE + MLA_ROPE) ** -0.5
ROPE_THETA = 10000.0
ADAM_LR, ADAM_B1, ADAM_B2, ADAM_EPS, ADAM_WD, ADAM_STEP = 0.001, 0.9, 0.999, 1e-08, 0.01, 10

VMEM_LIMIT_V7X = 56 * 1024 * 1024
LANES = 128
MESH = pl.DeviceIdType.MESH


def _pc(body, **kw):
    return pl.pallas_call(body, **kw)


def _cparams(sem):
    return pltpu.CompilerParams(dimension_semantics=sem, vmem_limit_bytes=VMEM_LIMIT_V7X)


def _tile(n, cap):
    t = (cap // LANES) * LANES
    while t >= LANES:
        if n % t == 0:
            return t
        t -= LANES
    return n


def _sds(shape, dt):
    return jax.ShapeDtypeStruct(shape, dt)


def mm(a, b, *, ta=False, tb=False, add=None, after=None, out_dtype=F32, name):
    if ta:
        K, M = a.shape
    else:
        M, K = a.shape
    if tb:
        N, K2 = b.shape
    else:
        K2, N = b.shape
    assert K == K2, (a.shape, b.shape, ta, tb)
    tm, tn, tk = _tile(M, 1024), _tile(N, 1024), _tile(K, 2048)
    nk = K // tk
    a_spec = pl.BlockSpec((tk, tm), lambda i, j, k: (k, i)) if ta else pl.BlockSpec((tm, tk), lambda i, j, k: (i, k))
    b_spec = pl.BlockSpec((tn, tk), lambda i, j, k: (j, k)) if tb else pl.BlockSpec((tk, tn), lambda i, j, k: (k, j))
    o_spec = pl.BlockSpec((tm, tn), lambda i, j, k: (i, j))
    dn = (((0 if ta else 1,), (1 if tb else 0,)), ((), ()))
    has_add = add is not None

    def body(*refs):
        a_ref, b_ref = refs[0], refs[1]
        part = lax.dot_general(a_ref[...].astype(BF16), b_ref[...].astype(BF16), dn, preferred_element_type=F32)
        if nk == 1:
            refs[-1][...] = (part + refs[2][...] if has_add else part).astype(out_dtype)
            return
        o_ref, acc = refs[-2], refs[-1]
        k = pl.program_id(2)

        @pl.when(k == 0)
        def _():
            acc[...] = part

        @pl.when(k > 0)
        def _():
            acc[...] += part

        @pl.when(k == nk - 1)
        def _():
            r = acc[...]
            if has_add:
                r = r + refs[2][...]
            o_ref[...] = r.astype(out_dtype)

    ins = [a, b] + ([add] if has_add else []) + ([after] if after is not None else [])
    specs = [a_spec, b_spec] + ([o_spec] if has_add else []) + ([pl.BlockSpec(memory_space=pl.ANY)] if after is not None else [])
    return _pc(body, grid=(M // tm, N // tn, nk), in_specs=specs, out_specs=o_spec, out_shape=_sds((M, N), out_dtype),
               scratch_shapes=[pltpu.VMEM((tm, tn), F32)] if nk > 1 else [], compiler_params=_cparams(("parallel", "parallel", "arbitrary")),
               name=name)(*ins)


def gmm(kind, a, b, *, G, name, out_dtype=F32):
    S_ = a.shape[0]
    Ka = a.shape[1] // G
    if kind == "tn":
        N = b.shape[1] // G
        tk = _tile(S_, 2048)
        nk = S_ // tk

        def body(a_ref, b_ref, o_ref, acc):
            k = pl.program_id(1)

            @pl.when(k == 0)
            def _():
                acc[...] = jnp.zeros_like(acc)

            acc[...] += lax.dot_general(a_ref[...].astype(BF16), b_ref[...].astype(BF16), (((0,), (0,)), ((), ())),
                                        preferred_element_type=F32)

            @pl.when(k == nk - 1)
            def _():
                o_ref[...] = acc[...].astype(out_dtype)

        return _pc(body, grid=(G, nk),
                   in_specs=[pl.BlockSpec((tk, Ka), lambda g, k: (k, g)), pl.BlockSpec((tk, N), lambda g, k: (k, g))],
                   out_specs=pl.BlockSpec((None, Ka, N), lambda g, k: (g, 0, 0)), out_shape=_sds((G, Ka, N), out_dtype),
                   scratch_shapes=[pltpu.VMEM((Ka, N), F32)], compiler_params=_cparams(("parallel", "arbitrary")), name=name)(a, b)
    N = b.shape[2] if kind == "nn" else b.shape[1]
    tm = _tile(S_, 4096)
    dn = (((1,), (0 if kind == "nn" else 1,)), ((), ()))

    def body(a_ref, b_ref, o_ref):
        o_ref[...] = lax.dot_general(a_ref[...].astype(BF16), b_ref[...].astype(BF16), dn, preferred_element_type=F32).astype(out_dtype)

    bshape = (None,) + tuple(b.shape[1:])
    return _pc(body, grid=(G, S_ // tm),
               in_specs=[pl.BlockSpec((tm, Ka), lambda g, i: (i, g)), pl.BlockSpec(bshape, lambda g, i: (g, 0, 0))],
               out_specs=pl.BlockSpec((tm, N), lambda g, i: (i, g)), out_shape=_sds((S_, G * N), out_dtype),
               compiler_params=_cparams(("parallel", "parallel")), name=name)(a, b)


def _rw_spec(ts, w, c, s):
    return pl.BlockSpec((ts, w), lambda j, i: (i, c + j * s))


def _rw_pspec(p, w, c, s):
    return pl.BlockSpec((p.shape[0], w), lambda j, i: (0, c + j * s))


def rowwise(f, tiles, params, outs, *, ncol=1, ts, name):
    S_ = tiles[0][0].shape[0]
    nin = len(tiles) + len(params)

    def body(*refs):
        res = f(pl.program_id(0), *[r[...].astype(F32) for r in refs[:nin]])
        for r, o in zip(refs[nin:], res):
            r[...] = o.astype(r.dtype)

    return _pc(body, grid=(ncol, S_ // ts),
               in_specs=[_rw_spec(ts, w, c, s) for (_, w, c, s) in tiles] + [_rw_pspec(*p) for p in params],
               out_specs=[_rw_spec(ts, w, 0, s) for (w, s, _) in outs],
               out_shape=[_sds((S_, w * (ncol if s else 1)), dt) for (w, s, dt) in outs],
               compiler_params=_cparams(("parallel", "parallel")), name=name)(*[t[0] for t in tiles], *[p[0] for p in params])


def rowwise_bwd(f, tiles, params, cots, *, need, adds=None, place=None, narrow=(), ncol=1, ts, name):
    S_ = tiles[0][0].shape[0]
    adds = adds or {}
    place = place or {}
    nt, npar, nc = len(tiles), len(params), len(cots)
    add_keys = sorted(adds)
    need_idx = [k for k in range(nt) if need[k]]
    into_keys = [k for k in need_idx if k in place and not isinstance(place[k][0], int)]
    n_extra = len(add_keys) + len(into_keys)

    def body(*refs):
        j, i = pl.program_id(0), pl.program_id(1)
        vals = [r[...].astype(F32) for r in refs[:nt + npar]]
        cvals = tuple(r[...].astype(F32) for r in refs[nt + npar:nt + npar + nc])
        add_refs = refs[nt + npar + nc:nt + npar + nc + len(add_keys)]
        out_refs = refs[nt + npar + nc + n_extra:]
        _, vjp = jax.vjp(lambda *v: tuple(f(j, *v)), *vals)
        grads = vjp(cvals)
        for n, k in enumerate(need_idx):
            g = grads[k]
            if k in adds:
                g = g + add_refs[add_keys.index(k)][...]
            out_refs[n][...] = g.astype(out_refs[n].dtype)
        for n in range(npar):
            ref = out_refs[len(need_idx) + n]
            first = (i == 0) if params[n][3] else jnp.logical_and(i == 0, j == 0)

            @pl.when(first)
            def _():
                ref[...] = jnp.zeros_like(ref)

            ref[...] += grads[nt + n]

    in_specs = ([_rw_spec(ts, w, c, s) for (_, w, c, s) in tiles] + [_rw_pspec(*p) for p in params]
                + [_rw_spec(ts, w, c, s) for (_, w, c, s) in cots] + [_rw_spec(ts, *adds[k][1:]) for k in add_keys]
                + [pl.BlockSpec(memory_space=pl.ANY) for _ in into_keys])
    out_specs, out_shape, aliases = [], [], {}
    for n, k in enumerate(need_idx):
        w, s = tiles[k][1], tiles[k][3]
        if k in place:
            dst, c0 = place[k]
            total = dst if isinstance(dst, int) else dst.shape[1]
            out_specs.append(_rw_spec(ts, w, c0, s))
            out_shape.append(_sds((S_, total), (BF16 if k in narrow else F32) if isinstance(dst, int) else dst.dtype))
            if k in into_keys:
                aliases[nt + npar + nc + len(add_keys) + into_keys.index(k)] = n
        else:
            out_specs.append(_rw_spec(ts, w, 0, s))
            out_shape.append(_sds((S_, w * (ncol if s else 1)), BF16 if k in narrow else F32))
    out_specs += [_rw_pspec(p[0], p[1], p[2], p[3]) for p in params]
    out_shape += [_sds(p[0].shape, F32) for p in params]
    res = _pc(body, grid=(ncol, S_ // ts), in_specs=in_specs, out_specs=out_specs, out_shape=out_shape,
              input_output_aliases=aliases, compiler_params=_cparams(("arbitrary", "arbitrary")), name=name)(
        *[t[0] for t in tiles], *[p[0] for p in params], *[c[0] for c in cots], *[adds[k][0] for k in add_keys],
        *[place[k][0] for k in into_keys])
    return list(res[:len(need_idx)]), list(res[len(need_idx):])


def _rms(x, g):
    r = lax.rsqrt(jnp.mean(x * x, axis=-1, keepdims=True) + EPS)
    return x * r * g


def _silu(x):
    return x * jax.nn.sigmoid(x)


@jax.custom_vjp
def _softplus(x):
    return jnp.maximum(x, 0.0) + jnp.log1p(jnp.exp(-jnp.abs(x)))


_softplus.defvjp(lambda x: (_softplus(x), x), lambda x, d: (d * jax.nn.sigmoid(x),))


def f_rms(j, x, g):
    return (_rms(x, g),)


def f_pool_gate(j, pg, gate, scale):
    return (pg * scale * _silu(gate),)


def f_ogate(j, o, gate):
    return (o * _silu(gate),)


def f_gdn_out(j, o, gate, g):
    return (_rms(o, g) * _silu(gate),)


def f_gdn_gates(j, ba, alog, dtb):
    lane = lax.broadcasted_iota(I32, (1, LANES), 1)
    gs, bs = [], []
    for h in range(GDN_H):
        eb = (lane == h).astype(F32)
        ea = (lane == GDN_H + h).astype(F32)
        b = jnp.sum(ba * eb, -1, keepdims=True)
        a = jnp.sum(ba * ea, -1, keepdims=True)
        al = jnp.sum(alog * eb, -1, keepdims=True)
        dt = jnp.sum(dtb * eb, -1, keepdims=True)
        g = -jnp.exp(al) * _softplus(a + dt)
        gs.append(jnp.broadcast_to(g, ba.shape))
        bs.append(jnp.broadcast_to(jax.nn.sigmoid(b), ba.shape))
    return jnp.concatenate(gs, 1), jnp.concatenate(bs, 1)


def _shift_dn(x, k):
    rows = lax.broadcasted_iota(I32, x.shape, 0)
    return jnp.where(rows < k, 0.0, pltpu.roll(x, k, 0))


def _shift_up(x, k):
    n = x.shape[0]
    rows = lax.broadcasted_iota(I32, x.shape, 0)
    return jnp.where(rows >= n - k, 0.0, pltpu.roll(x, n - k, 0))


def _pool_window(j):
    g = lax.div(j, POOL_GROUP // LANES)
    return jnp.where(g == 0, 2.0, jnp.where(g == 1, 4.0, jnp.where(g == 2, 8.0, 16.0))), g


def _pick(g, a2, a4, a8, a16):
    return jnp.where(g == 0, a2, jnp.where(g == 1, a4, jnp.where(g == 2, a8, a16)))


def pool_time_fwd(proj, name):
    S_ = proj.shape[0]

    def body(u_ref, p_ref):
        u = u_ref[...].astype(F32)
        w, g = _pool_window(pl.program_id(0))
        s2 = u + _shift_dn(u, 1)
        s4 = s2 + _shift_dn(s2, 2)
        s8 = s4 + _shift_dn(s4, 4)
        s16 = s8 + _shift_dn(s8, 8)
        t1 = (lax.broadcasted_iota(I32, u.shape, 0) + 1).astype(F32)
        p_ref[...] = (_pick(g, s2, s4, s8, s16) / jnp.minimum(t1, w) - u).astype(p_ref.dtype)

    return _pc(body, grid=(POOL_WIDTH // LANES,), in_specs=[pl.BlockSpec((S_, LANES), lambda j: (0, j))],
               out_specs=pl.BlockSpec((S_, LANES), lambda j: (0, j)), out_shape=_sds((S_, POOL_WIDTH), BF16),
               compiler_params=_cparams(("parallel",)), name=name)(proj)


def pool_time_bwd(dp, into, name):
    S_ = dp.shape[0]

    def body(dp_ref, _, du_ref):
        d = dp_ref[...].astype(F32)
        w, g = _pool_window(pl.program_id(0))
        t1 = (lax.broadcasted_iota(I32, d.shape, 0) + 1).astype(F32)
        q = d / jnp.minimum(t1, w)
        r2 = q + _shift_up(q, 1)
        r4 = r2 + _shift_up(r2, 2)
        r8 = r4 + _shift_up(r4, 4)
        r16 = r8 + _shift_up(r8, 8)
        du_ref[...] = (_pick(g, r2, r4, r8, r16) - d).astype(du_ref.dtype)

    return _pc(body, grid=(POOL_WIDTH // LANES,),
               in_specs=[pl.BlockSpec((S_, LANES), lambda j: (0, j)), pl.BlockSpec(memory_space=pl.ANY)],
               out_specs=pl.BlockSpec((S_, LANES), lambda j: (0, j)), out_shape=_sds(into.shape, into.dtype),
               input_output_aliases={1: 0}, compiler_params=_cparams(("parallel",)), name=name)(dp, into)


def _conv_post(j, a):
    n = a * lax.rsqrt(jnp.sum(a * a, axis=-1, keepdims=True) + EPS)
    nq = GDN_QK // LANES
    return jnp.where(j < nq, n * (GDN_DK ** -0.5), jnp.where(j < 2 * nq, n, a))


def _conv_taps(u):
    return [_shift_dn(u, 3), _shift_dn(u, 2), _shift_dn(u, 1), u]


def _conv_pre(taps, w):
    return w[0:1] * taps[0] + w[1:2] * taps[1] + w[2:3] * taps[2] + w[3:4] * taps[3]


def gdn_conv_fwd(proj, conv_w, name):
    S_ = proj.shape[0]

    def body(u_ref, w_ref, o_ref):
        o_ref[...] = _conv_post(pl.program_id(0), _silu(_conv_pre(_conv_taps(u_ref[...]), w_ref[...])))

    return _pc(body, grid=(GDN_CONV_CH // LANES,),
               in_specs=[pl.BlockSpec((S_, LANES), lambda j: (0, j)), pl.BlockSpec((8, LANES), lambda j: (0, j))],
               out_specs=pl.BlockSpec((S_, LANES), lambda j: (0, j)), out_shape=_sds((S_, GDN_CONV_CH), F32),
               compiler_params=_cparams(("parallel",)), name=name)(proj, conv_w)


def gdn_conv_bwd(proj, conv_w, dq, dk, dv, into, name):
    S_ = proj.shape[0]
    nq = GDN_QK // LANES

    def body(u_ref, w_ref, dq_ref, dk_ref, dv_ref, _, du_ref, dw_ref):
        j = pl.program_id(0)
        u, w = u_ref[...], w_ref[...]
        taps = _conv_taps(u)
        c = _conv_pre(taps, w)
        sig = jax.nn.sigmoid(c)
        dout = jnp.where(j < nq, dq_ref[...], jnp.where(j < 2 * nq, dk_ref[...], dv_ref[...]))
        _, vjp = jax.vjp(lambda a: _conv_post(j, a), c * sig)
        dc = vjp(dout)[0] * (sig * (1.0 + c * (1.0 - sig)))
        du = w[3:4] * dc + w[2:3] * _shift_up(dc, 1) + w[1:2] * _shift_up(dc, 2) + w[0:1] * _shift_up(dc, 3)
        du_ref[...] = du.astype(du_ref.dtype)
        rows = lax.broadcasted_iota(I32, (8, LANES), 0)
        dw = jnp.zeros((8, LANES), F32)
        for k in range(4):
            dw = dw + jnp.where(rows == k, jnp.sum(dc * taps[k], axis=0, keepdims=True), 0.0)
        dw_ref[...] = dw

    blk = lambda f: pl.BlockSpec((S_, LANES), f)
    return _pc(body, grid=(GDN_CONV_CH // LANES,),
               in_specs=[blk(lambda j: (0, j)), pl.BlockSpec((8, LANES), lambda j: (0, j)),
                         blk(lambda j: (0, jnp.minimum(j, nq - 1))), blk(lambda j: (0, jnp.clip(j - nq, 0, nq - 1))),
                         blk(lambda j: (0, jnp.clip(j - 2 * nq, 0, 2 * nq - 1))), pl.BlockSpec(memory_space=pl.ANY)],
               out_specs=[blk(lambda j: (0, j)), pl.BlockSpec((8, LANES), lambda j: (0, j))],
               out_shape=[_sds(into.shape, into.dtype), _sds((8, GDN_CONV_CH), F32)], input_output_aliases={5: 0},
               compiler_params=_cparams(("parallel",)), name=name)(proj, conv_w, dq, dk, dv, into)


_NN, _NT, _TN = ((1,), (0,)), ((1,), (1,)), ((0,), (0,))


def _split(x, n):
    parts = []
    for _ in range(n):
        h = x.astype(BF16)
        parts.append(h)
        x = x - h.astype(F32)
    return parts


def _dot(a, b, dn, mode):
    d = lambda p, q: lax.dot_general(p, q, (dn, ((), ())), preferred_element_type=F32)
    if mode == "lo":
        return d(a.astype(BF16), b.astype(BF16))
    if mode == "x3":
        (ah, al), (bh, bl) = _split(a, 2), _split(b, 2)
        return d(ah, bh) + (d(ah, bl) + d(al, bh))
    b0, b1, b2 = _split(b, 3)
    ab = a.astype(BF16)
    return d(ab, b0) + (d(ab, b1) + d(ab, b2))


def _make_dots(mode):
    @jax.custom_vjp
    def nn(a, b):
        return _dot(a, b, _NN, mode)

    @jax.custom_vjp
    def nt(a, b):
        return _dot(a, b, _NT, mode)

    @jax.custom_vjp
    def tn(a, b):
        return _dot(a, b, _TN, mode)

    nn.defvjp(lambda a, b: (nn(a, b), (a, b)), lambda r, d: (nt(d, r[1]), tn(r[0], d)))
    nt.defvjp(lambda a, b: (nt(a, b), (a, b)), lambda r, d: (nn(d, r[1]), tn(d, r[0])))
    tn.defvjp(lambda a, b: (tn(a, b), (a, b)), lambda r, d: (nt(r[1], d), nn(r[0], d)))
    return nn, nt, tn


_nn_hi, _nt_hi, _tn_hi = _make_dots("x3")
_nn_lo, _nt_lo, _tn_lo = _make_dots("lo")


@jax.custom_vjp
def _nn_const(a, b):
    return _dot(a, b, _NN, "xl")


_nn_const.defvjp(lambda a, b: (_nn_const(a, b), a), lambda a, d: (jnp.zeros_like(a), _dot(a, d, _TN, "xl")))


def _each(f, *lists):
    return [f(*xs) for xs in zip(*lists)]


@jax.custom_vjp
def _unit_inverses(xs):
    C = xs[0].shape[0]
    eye = (lax.broadcasted_iota(I32, (C, C), 0) == lax.broadcasted_iota(I32, (C, C), 1)).astype(F32)
    ainv, p = [eye + a for a in xs], xs
    for _ in range(int(math.log2(C)) - 1):
        p = _each(lambda a: _dot(a, a, _NN, "x3"), p)
        ainv = _each(lambda a, b: a + _dot(a, b, _NN, "x3"), ainv, p)
    return ainv


def _unit_inverses_bwd(ainv, d):
    left = _each(lambda a, g: _dot(a, g, _TN, "x3"), ainv, d)
    return (_each(lambda t, a: _dot(t, a, _NT, "x3"), left, ainv),)


_unit_inverses.defvjp(lambda xs: (lambda a: (a, a))(_unit_inverses(xs)), _unit_inverses_bwd)


def _gdn_chunk(q, k, v, gb, bb, state):
    C = GDN_C
    e0 = (lax.broadcasted_iota(I32, (1, LANES), 1) == 0).astype(F32)
    ri = lax.broadcasted_iota(I32, (C, C), 0)
    ci = lax.broadcasted_iota(I32, (C, C), 1)
    causal, strict = ri >= ci, ri > ci
    tri, eye, ones = causal.astype(F32), (ri == ci).astype(F32), jnp.ones((C, C), F32)
    last = lax.broadcasted_iota(I32, (C, LANES), 0) == C - 1
    g1 = _each(lambda a: jnp.sum(a * e0, -1, keepdims=True), gb)
    b1 = _each(lambda a: jnp.sum(a * e0, -1, keepdims=True), bb)
    gc_c = _each(lambda g: _nn_const(tri, jnp.broadcast_to(g, (C, C))), g1)
    gc_d = _each(lambda g: _nn_const(tri, jnp.broadcast_to(g, (C, LANES))), g1)
    gr_c = _each(lambda g: _nn_const(ones, eye * g), gc_c)
    decay = _each(lambda a, r: jnp.where(causal, jnp.exp(jnp.where(causal, a - r, 0.0)), 0.0), gc_c, gr_c)
    kb = _each(lambda a, b: a * b, k, b1)
    vb = _each(lambda a, b: a * b, v, b1)
    x = _each(lambda a, b, d: -jnp.where(strict, _nt_lo(a, b) * d, 0.0), kb, k, decay)
    ainv = _unit_inverses(x)
    u = _each(_nn_hi, ainv, vb)
    w = _each(lambda a, b, g: _nn_hi(a, b * jnp.exp(g)), ainv, kb, gc_d)
    attn = _each(lambda a, b, d: jnp.where(causal, _nt_lo(a, b) * d, 0.0), q, k, decay)
    v_new = _each(lambda a, b, s: a - _nn_lo(b, s), u, w, state)
    o = _each(lambda a, g, s, t, vn: _nn_lo(a * jnp.exp(g), s) + _nn_lo(t, vn), q, gc_d, state, attn, v_new)
    gl = _each(lambda g: jnp.sum(jnp.where(last, g, 0.0), axis=0, keepdims=True), gc_d)
    new_state = _each(lambda s, g, a, gd, vn: s * jnp.exp(jnp.sum(g * e0, -1, keepdims=True)) + _tn_lo(a * jnp.exp(g - gd), vn),
                      state, gl, k, gc_d, v_new)
    return o, new_state


def _head_slices(ref, width):
    return [ref[:, h * width:(h + 1) * width] for h in range(GDN_H)]


def gdn_chunk_fwd(qkv, g_b, beta_b, name):
    S_ = qkv.shape[0]
    N = S_ // GDN_C

    def body(q_ref, k_ref, v_ref, g_ref, b_ref, o_ref, s_ref, state):
        @pl.when(pl.program_id(0) == 0)
        def _():
            state[...] = jnp.zeros_like(state)

        st = [state[h] for h in range(GDN_H)]
        s_ref[0] = state[...]
        o, st2 = _gdn_chunk(_head_slices(q_ref, GDN_DK), _head_slices(k_ref, GDN_DK), _head_slices(v_ref, GDN_DV),
                            _head_slices(g_ref, GDN_DK), _head_slices(b_ref, GDN_DK), st)
        for h in range(GDN_H):
            o_ref[:, h * GDN_DV:(h + 1) * GDN_DV] = o[h]
            state[h] = st2[h]

    return _pc(body, grid=(N,),
               in_specs=[pl.BlockSpec((GDN_C, GDN_QK), lambda n: (n, 0)), pl.BlockSpec((GDN_C, GDN_QK), lambda n: (n, 1)),
                         pl.BlockSpec((GDN_C, GDN_V), lambda n: (n, 1)), pl.BlockSpec((GDN_C, GDN_QK), lambda n: (n, 0)),
                         pl.BlockSpec((GDN_C, GDN_QK), lambda n: (n, 0))],
               out_specs=[pl.BlockSpec((GDN_C, GDN_V), lambda n: (n, 0)),
                          pl.BlockSpec((1, GDN_H, GDN_DK, GDN_DV), lambda n: (n, 0, 0, 0))],
               out_shape=[_sds((S_, GDN_V), F32), _sds((N, GDN_H, GDN_DK, GDN_DV), F32)],
               scratch_shapes=[pltpu.VMEM((GDN_H, GDN_DK, GDN_DV), F32)],
               compiler_params=_cparams(("arbitrary",)), name=name)(qkv, qkv, qkv, g_b, beta_b)


def gdn_chunk_bwd(qkv, g_b, beta_b, states, do, name):
    S_ = qkv.shape[0]
    N = S_ // GDN_C

    def body(q_ref, k_ref, v_ref, g_ref, b_ref, s_ref, do_ref, dq_ref, dk_ref, dv_ref, dg_ref, db_ref, dstate):
        @pl.when(pl.program_id(0) == 0)
        def _():
            dstate[...] = jnp.zeros_like(dstate)

        _, vjp = jax.vjp(_gdn_chunk, _head_slices(q_ref, GDN_DK), _head_slices(k_ref, GDN_DK), _head_slices(v_ref, GDN_DV),
                         _head_slices(g_ref, GDN_DK), _head_slices(b_ref, GDN_DK), [s_ref[0, h] for h in range(GDN_H)])
        dq, dk, dv, dg, db, ds = vjp((_head_slices(do_ref, GDN_DV), [dstate[h] for h in range(GDN_H)]))
        for h in range(GDN_H):
            kk, vv = slice(h * GDN_DK, (h + 1) * GDN_DK), slice(h * GDN_DV, (h + 1) * GDN_DV)
            dq_ref[:, kk] = dq[h]
            dk_ref[:, kk] = dk[h]
            dv_ref[:, vv] = dv[h]
            dg_ref[:, kk] = dg[h]
            db_ref[:, kk] = db[h]
            dstate[h] = ds[h]

    r = lambda n: N - 1 - n
    qk = lambda c: pl.BlockSpec((GDN_C, GDN_QK), lambda n: (r(n), c))
    vs = lambda c: pl.BlockSpec((GDN_C, GDN_V), lambda n: (r(n), c))
    return _pc(body, grid=(N,),
               in_specs=[qk(0), qk(1), vs(1), qk(0), qk(0),
                         pl.BlockSpec((1, GDN_H, GDN_DK, GDN_DV), lambda n: (r(n), 0, 0, 0)), vs(0)],
               out_specs=[qk(0), qk(0), vs(0), qk(0), qk(0)],
               out_shape=[_sds((S_, GDN_QK), F32), _sds((S_, GDN_QK), F32), _sds((S_, GDN_V), F32),
                          _sds((S_, GDN_QK), F32), _sds((S_, GDN_QK), F32)],
               scratch_shapes=[pltpu.VMEM((GDN_H, GDN_DK, GDN_DV), F32)],
               compiler_params=_cparams(("arbitrary",)), name=name)(qkv, qkv, qkv, g_b, beta_b, states, do)


def _rope_tables(pos_ref, inv_ref, cm_ref, sg_ref):
    ang = pos_ref[...] * inv_ref[...]
    return jnp.cos(ang) * cm_ref[...], jnp.sin(ang) * sg_ref[...]


def mla_prep_fwd(qpad, kv, proj, pos, rope_consts, name):
    S_ = qpad.shape[0]
    ts = 512
    W = 2 * LANES

    def body(q_ref, kv_ref, kr_ref, pos_ref, inv_ref, cm_ref, sg_ref, qh_ref, kh_ref, vh_ref):
        cs, sn = _rope_tables(pos_ref, inv_ref, cm_ref, sg_ref)
        rope = lambda r: r * cs + pltpu.roll(r, LANES // 2, 1) * sn
        krr = rope(kr_ref[...].astype(F32)).astype(BF16)
        for h in range(MLA_H):
            qh_ref[h, :, 0:LANES] = (q_ref[:, h * W:h * W + LANES].astype(F32) * MLA_SCALE).astype(BF16)
            qh_ref[h, :, LANES:W] = (rope(q_ref[:, h * W + LANES:(h + 1) * W].astype(F32)) * MLA_SCALE).astype(BF16)
            kh_ref[h, :, 0:LANES] = kv_ref[:, h * W:h * W + LANES].astype(BF16)
            kh_ref[h, :, LANES:W] = krr
            vh_ref[h] = kv_ref[:, h * W + LANES:(h + 1) * W].astype(BF16)

    one = pl.BlockSpec((1, LANES), lambda i: (0, 0))
    return _pc(body, grid=(S_ // ts,),
               in_specs=[pl.BlockSpec((ts, MLA_H * W), lambda i: (i, 0)), pl.BlockSpec((ts, MLA_H * W), lambda i: (i, 0)),
                         pl.BlockSpec((ts, LANES), lambda i: (i, 1536 // LANES)), pl.BlockSpec((ts, 1), lambda i: (i, 0)),
                         one, one, one],
               out_specs=[pl.BlockSpec((MLA_H, ts, W), lambda i: (0, i, 0)), pl.BlockSpec((MLA_H, ts, W), lambda i: (0, i, 0)),
                          pl.BlockSpec((MLA_H, ts, LANES), lambda i: (0, i, 0))],
               out_shape=[_sds((MLA_H, S_, W), BF16), _sds((MLA_H, S_, W), BF16), _sds((MLA_H, S_, LANES), BF16)],
               compiler_params=_cparams(("parallel",)), name=name)(qpad, kv, proj, pos, *rope_consts)


def mla_prep_bwd(dqh, dkh, dvh, pos, rope_consts, into, name):
    S_ = dqh.shape[1]
    ts = 512
    W = 2 * LANES

    def body(dq_ref, dk_ref, dv_ref, pos_ref, inv_ref, cm_ref, sg_ref, _, dqp_ref, dkv_ref, dkr_ref):
        cs, sn = _rope_tables(pos_ref, inv_ref, cm_ref, sg_ref)
        rope_t = lambda g: g * cs + pltpu.roll(g * sn, LANES // 2, 1)
        acc = jnp.zeros((ts, LANES), F32)
        for h in range(MLA_H):
            dqp_ref[:, h * W:h * W + LANES] = (dq_ref[h, :, 0:LANES].astype(F32) * MLA_SCALE).astype(BF16)
            dqp_ref[:, h * W + LANES:(h + 1) * W] = (rope_t(dq_ref[h, :, LANES:W].astype(F32)) * MLA_SCALE).astype(BF16)
            dkv_ref[:, h * W:h * W + LANES] = dk_ref[h, :, 0:LANES]
            dkv_ref[:, h * W + LANES:(h + 1) * W] = dv_ref[h]
            acc = acc + dk_ref[h, :, LANES:W].astype(F32)
        dkr_ref[...] = rope_t(acc).astype(dkr_ref.dtype)

    one = pl.BlockSpec((1, LANES), lambda i: (0, 0))
    return _pc(body, grid=(S_ // ts,),
               in_specs=[pl.BlockSpec((MLA_H, ts, W), lambda i: (0, i, 0)), pl.BlockSpec((MLA_H, ts, W), lambda i: (0, i, 0)),
                         pl.BlockSpec((MLA_H, ts, LANES), lambda i: (0, i, 0)), pl.BlockSpec((ts, 1), lambda i: (i, 0)),
                         one, one, one, pl.BlockSpec(memory_space=pl.ANY)],
               out_specs=[pl.BlockSpec((ts, MLA_H * W), lambda i: (i, 0)), pl.BlockSpec((ts, MLA_H * W), lambda i: (i, 0)),
                          pl.BlockSpec((ts, LANES), lambda i: (i, 1536 // LANES))],
               out_shape=[_sds((S_, MLA_H * W), BF16), _sds((S_, MLA_H * W), BF16), _sds(into.shape, into.dtype)],
               input_output_aliases={7: 2}, compiler_params=_cparams(("parallel",)), name=name)(dqh, dkh, dvh, pos, *rope_consts, into)


NEG = -1e30


FLASH_TILE = 1024
FLASH_SUB = 256


def _scores(q, k, diagonal):
    s = lax.dot_general(q, k, (_NT, ((), ())), preferred_element_type=F32)
    if not diagonal:
        return s
    return jnp.where(lax.broadcasted_iota(I32, s.shape, 1) <= lax.broadcasted_iota(I32, s.shape, 0), s, NEG)


def _sub_blocks(t, diagonal):
    sub = min(FLASH_SUB, t) if diagonal else t
    return [(c * sub if diagonal else 0, slice(c * sub, (c + 1) * sub)) for c in range(t // sub)]


FLASH_HEADS = 2


def flash_fwd(qh, kh, vh, name):
    H, S_, W = qh.shape
    t = _tile(S_, FLASH_TILE)
    n = S_ // t
    G = FLASH_HEADS
    heads = list(range(G))

    def body(q_ref, k_ref, v_ref, o_ref, lse_ref, m_s, l_s, acc):
        qi, kj = pl.program_id(1), pl.program_id(2)

        @pl.when(kj == 0)
        def _():
            m_s[...] = jnp.full_like(m_s, NEG)
            l_s[...] = jnp.zeros_like(l_s)
            acc[...] = jnp.zeros_like(acc)

        def step(diagonal):
            s = _each(lambda a: _scores(q_ref[a], k_ref[a], diagonal), heads)
            m_old = _each(lambda a: m_s[a], heads)
            m_new = _each(lambda mo, sa: jnp.maximum(mo, jnp.max(sa, axis=-1, keepdims=True)), m_old, s)
            alpha = _each(lambda mo, mn: jnp.exp(mo - mn), m_old, m_new)
            p = _each(lambda sa, mn: jnp.exp(sa - mn[:, :1]), s, m_new)
            pv = _each(lambda pa, a: lax.dot_general(pa.astype(BF16), v_ref[a], (_NN, ((), ())), preferred_element_type=F32), p, heads)
            for a in heads:
                l_s[a] = alpha[a] * l_s[a] + jnp.sum(p[a], axis=-1, keepdims=True)
                acc[a] = alpha[a] * acc[a] + pv[a]
                m_s[a] = m_new[a]

        pl.when(kj < qi)(lambda: step(False))
        pl.when(kj == qi)(lambda: step(True))

        @pl.when(kj == n - 1)
        def _():
            for a in heads:
                o_ref[:, a * LANES:(a + 1) * LANES] = (acc[a] / l_s[a]).astype(o_ref.dtype)
                lse_ref[a] = m_s[a] + jnp.log(l_s[a])

    return _pc(body, grid=(H // G, n, n),
               in_specs=[pl.BlockSpec((G, t, W), lambda h, i, j: (h, i, 0)),
                         pl.BlockSpec((G, t, W), lambda h, i, j: (h, jnp.minimum(i, j), 0)),
                         pl.BlockSpec((G, t, LANES), lambda h, i, j: (h, jnp.minimum(i, j), 0))],
               out_specs=[pl.BlockSpec((t, G * LANES), lambda h, i, j: (i, h)), pl.BlockSpec((G, t, LANES), lambda h, i, j: (h, i, 0))],
               out_shape=[_sds((S_, H * LANES), BF16), _sds((H, S_, LANES), F32)],
               scratch_shapes=[pltpu.VMEM((G, t, LANES), F32)] * 3,
               compiler_params=_cparams(("parallel", "parallel", "arbitrary")), name=name)(qh, kh, vh)


def flash_bwd(qh, kh, vh, o, lse, do, name):
    H, S_, W = qh.shape
    t = _tile(S_, FLASH_TILE)
    n = S_ // t
    G = FLASH_HEADS
    heads = list(range(G))

    def body(q_ref, k_ref, v_ref, o_ref, lse_ref, do_ref, dq_ref, dk_ref, dv_ref, dq_acc, dk_acc, dv_acc):
        kj, qi = pl.program_id(1), pl.program_id(2)

        @pl.when(jnp.logical_and(kj == 0, qi == 0))
        def _():
            dq_acc[...] = jnp.zeros_like(dq_acc)

        @pl.when(qi == 0)
        def _():
            dk_acc[...] = jnp.zeros_like(dk_acc)
            dv_acc[...] = jnp.zeros_like(dv_acc)

        def step(diagonal):
            lanes = lambda a: slice(a * LANES, (a + 1) * LANES)
            do_ = _each(lambda a: do_ref[:, lanes(a)], heads)
            dob = _each(lambda d: d.astype(BF16), do_)
            delta = _each(lambda d, a: jnp.sum(d.astype(F32) * o_ref[:, lanes(a)].astype(F32), axis=-1, keepdims=True), do_, heads)
            for r0, keys in _sub_blocks(t, diagonal):
                p = _each(lambda a: jnp.exp(_scores(q_ref[a, r0:, :], k_ref[a, keys, :], diagonal) - lse_ref[a, r0:, :1]), heads)
                dp = _each(lambda d, a: lax.dot_general(d[r0:], v_ref[a, keys, :], (_NT, ((), ())), preferred_element_type=F32), dob, heads)
                ds = _each(lambda pa, dpa, de: (pa * (dpa - de[r0:])).astype(BF16), p, dp, delta)
                rows = pl.ds(pl.multiple_of(qi * t, t) + r0, t - r0)
                for a in heads:
                    dv_acc[a, keys, :] += lax.dot_general(p[a].astype(BF16), dob[a][r0:], (_TN, ((), ())), preferred_element_type=F32)
                    dk_acc[a, keys, :] += lax.dot_general(ds[a], q_ref[a, r0:, :], (_TN, ((), ())), preferred_element_type=F32)
                    dq_acc[a, rows, :] += lax.dot_general(ds[a], k_ref[a, keys, :], (_NN, ((), ())), preferred_element_type=F32)

        pl.when(qi > kj)(lambda: step(False))
        pl.when(qi == kj)(lambda: step(True))

        @pl.when(qi == n - 1)
        def _():
            dk_ref[...] = dk_acc[...].astype(BF16)
            dv_ref[...] = dv_acc[...].astype(BF16)

        @pl.when(jnp.logical_and(kj == n - 1, qi == n - 1))
        def _():
            dq_ref[...] = dq_acc[...].astype(BF16)

    qrow = lambda h, j, i: jnp.maximum(i, j)
    return _pc(body, grid=(H // G, n, n),
               in_specs=[pl.BlockSpec((G, t, W), lambda h, j, i: (h, qrow(h, j, i), 0)),
                         pl.BlockSpec((G, t, W), lambda h, j, i: (h, j, 0)),
                         pl.BlockSpec((G, t, LANES), lambda h, j, i: (h, j, 0)),
                         pl.BlockSpec((t, G * LANES), lambda h, j, i: (qrow(h, j, i), h)),
                         pl.BlockSpec((G, t, LANES), lambda h, j, i: (h, qrow(h, j, i), 0)),
                         pl.BlockSpec((t, G * LANES), lambda h, j, i: (qrow(h, j, i), h))],
               out_specs=[pl.BlockSpec((G, S_, W), lambda h, j, i: (h, 0, 0)),
                          pl.BlockSpec((G, t, W), lambda h, j, i: (h, j, 0)),
                          pl.BlockSpec((G, t, LANES), lambda h, j, i: (h, j, 0))],
               out_shape=[_sds((H, S_, W), BF16), _sds((H, S_, W), BF16), _sds((H, S_, LANES), BF16)],
               scratch_shapes=[pltpu.VMEM((G, S_, W), F32), pltpu.VMEM((G, t, W), F32), pltpu.VMEM((G, t, LANES), F32)],
               compiler_params=_cparams(("parallel", "arbitrary", "arbitrary")), name=name)(qh, kh, vh, o, lse, do)


def loss_head(x, target, g, name):
    S_ = x.shape[0]
    ts = 512

    def body(x_ref, t_ref, g_ref, l_ref, dx_ref, dg_ref):
        @pl.when(pl.program_id(0) == 0)
        def _():
            l_ref[...] = jnp.zeros_like(l_ref)
            dg_ref[...] = jnp.zeros_like(dg_ref)

        y, vjp = jax.vjp(_rms, x_ref[...], g_ref[...])
        err = y - t_ref[...]
        l_ref[...] += 0.5 * jnp.sum(jnp.sum(err * err, axis=-1, keepdims=True), axis=0, keepdims=True) / D
        dx, dg = vjp(err / D)
        dx_ref[...] = dx
        dg_ref[...] += dg

    row = pl.BlockSpec((ts, D), lambda i: (i, 0))
    return _pc(body, grid=(S_ // ts,), in_specs=[row, row, pl.BlockSpec((1, D), lambda i: (0, 0))],
               out_specs=[pl.BlockSpec((1, LANES), lambda i: (0, 0)), row, pl.BlockSpec((1, D), lambda i: (0, 0))],
               out_shape=[_sds((1, LANES), F32), _sds((S_, D), F32), _sds((1, D), F32)],
               compiler_params=_cparams(("arbitrary",)), name=name)(x, target, g)


def adamw(w, parts, m, v, name, after=None):
    R, C = w.shape
    rows = [p.shape[1] for p in parts[0]]
    tr = R
    for cand in (512, 256, 128, 64, 32, 16, 8):
        if all(r % cand == 0 for r in rows) and cand * C * 4 * len(rows) <= 2 * 1024 * 1024:
            tr = cand
            break
    c1 = 1.0 - ADAM_B1 ** ADAM_STEP
    c2 = 1.0 - ADAM_B2 ** ADAM_STEP
    starts = [sum(rows[:k]) // tr for k in range(len(rows))]
    flat = [p for part in parts for p in part]

    def body(*refs):
        w_ref, m_ref, v_ref = refs[0], refs[1 + len(flat)], refs[2 + len(flat)]
        g_ref, d_ref, nm_ref, nv_ref = refs[-4:]
        i = pl.program_id(0)
        gg, at = None, 1
        for part in parts:
            val = None
            for k in range(len(part)):
                p_ref = refs[at]
                at += 1
                s = p_ref[0].astype(F32)
                for n in range(1, p_ref.shape[0]):
                    s = s + p_ref[n].astype(F32)
                val = s if val is None else jnp.where(i >= starts[k], s, val)
            gg = val if gg is None else gg + val
        m2 = ADAM_B1 * m_ref[...] + (1.0 - ADAM_B1) * gg
        v2 = ADAM_B2 * v_ref[...] + (1.0 - ADAM_B2) * (gg * gg)
        g_ref[...] = gg
        d_ref[...] = -ADAM_LR * ((m2 / c1) / (jnp.sqrt(v2 / c2) + ADAM_EPS) + ADAM_WD * w_ref[...])
        nm_ref[...] = m2
        nv_ref[...] = v2

    blk = pl.BlockSpec((tr, C), lambda i: (i, 0))
    piece = lambda p, k: pl.BlockSpec((p.shape[0], tr, C), lambda i: (0, jnp.clip(i - starts[k], 0, rows[k] // tr - 1), 0))
    pblk = [piece(p, k) for part in parts for k, p in enumerate(part)]
    extra = [] if after is None else [after]
    return _pc(body, grid=(R // tr,), in_specs=[blk] + pblk + [blk, blk] + [pl.BlockSpec(memory_space=pl.ANY)] * len(extra),
               out_specs=[blk] * 4, out_shape=[_sds((R, C), F32)] * 4,
               compiler_params=_cparams(("parallel",)), name=name)(w, *flat, m, v, *extra)


def sum_slots(recv, name):
    n, R, C = recv.shape

    def body(r_ref, o_ref):
        acc = r_ref[0]
        for s in range(1, n):
            acc = acc + r_ref[s]
        o_ref[...] = acc

    return _pc(body, grid=(1,), in_specs=[pl.BlockSpec((n, R, C), lambda i: (0, 0, 0))],
               out_specs=pl.BlockSpec((R, C), lambda i: (0, 0)), out_shape=_sds((R, C), F32),
               compiler_params=_cparams(("arbitrary",)), name=name)(recv)


def _chip_peers():
    x, y, c = lax.axis_index("x"), lax.axis_index("y"), lax.axis_index("c")
    return (x, y, c), [(1 - x, y, c), (x, 1 - y, c), (1 - x, 1 - y, c)]


def _chip_index(p):
    return 2 * p[0] + p[1]


def _win(ref, axis, chip, size):
    if axis is None:
        return ref.at[chip]
    idx = [slice(None)] * len(ref.shape)
    idx[axis] = pl.ds(pl.multiple_of(chip * size, size), size)
    return ref.at[tuple(idx)]


def _remote(src, dst, send_sem, recv_sem, peer):
    return pltpu.make_async_remote_copy(src_ref=src, dst_ref=dst, send_sem=send_sem, recv_sem=recv_sem, device_id=peer,
                                        device_id_type=MESH)


HBM_SPEC = pl.BlockSpec(memory_space=pltpu.HBM)
SEM_SPEC = pl.BlockSpec(memory_space=pltpu.SEMAPHORE)
ANY_SPEC = pl.BlockSpec(memory_space=pl.ANY)
DATAFLOW = pltpu.SideEffectType.DATAFLOW_SIDE_EFFECTING


def gather_piece(i, l, o, axis, size):
    return (i, lambda r, chip: r.at[l], o, lambda r, chip: _win(r, axis, chip, size))


def scatter_piece(i, o, axis, size):
    return (i, lambda r, chip: _win(r, axis, chip, size), o, lambda r, chip: r.at[chip])


def whole_piece(i):
    return (i, lambda r, chip: r, i, lambda r, chip: r)


def _copies(pieces, in_refs, out_refs, send, recv, sibling):
    me, peers = _chip_peers()
    if sibling:
        peers = [(me[0], me[1], 1 - me[2])]
    mine = _chip_index(me)
    remote = []
    for n, (i, src, o, dst) in enumerate(pieces):
        d = dst(out_refs[o], mine)
        remote += [_remote(src(in_refs[i], _chip_index(p)), d, send.at[len(peers) * n + k], recv.at[len(peers) * n + k], p)
                   for k, p in enumerate(peers)]
    return remote


def own_window(a, axis, size, chip):
    if axis is None:
        return lax.dynamic_index_in_dim(a, chip, 0, keepdims=False)
    return lax.dynamic_slice_in_dim(a, chip * size, size, axis=axis)


def place_own(land, own, axis, size, chip):
    if axis is None:
        return lax.dynamic_update_slice_in_dim(land, own[None], chip, axis=0)
    return lax.dynamic_update_slice_in_dim(land, own, chip * size, axis=axis)


def exchange_start(pieces, ins, out_shapes, after, name, sibling=False):
    n_in, n_out, ncp = len(ins), len(out_shapes), len(pieces)

    def body(*refs):
        in_refs, land_refs = refs[:n_in], refs[n_in:n_in + n_out]
        send, recv = refs[n_in + n_out + 1], refs[n_in + n_out + 2]
        token = refs[-1]
        for cp in _copies(pieces, in_refs, land_refs, send, recv, sibling):
            cp.start()
        token[...] = jnp.zeros_like(token)

    hbm = lambda a: pltpu.with_memory_space_constraint(a, pltpu.HBM)
    lands = [hbm(lax.empty(s.shape, s.dtype)) for s in out_shapes]
    sem = pltpu.SemaphoreType.DMA(((1 if sibling else 3) * ncp,))
    thru = [pltpu.HBM(a.shape, a.dtype) for a in ins] + [pltpu.HBM(s.shape, s.dtype) for s in out_shapes]
    res = _pc(body, in_specs=[HBM_SPEC] * (n_in + n_out) + [ANY_SPEC],
              out_specs=[SEM_SPEC, SEM_SPEC] + [HBM_SPEC] * (n_in + n_out) + [pl.BlockSpec(memory_space=pltpu.VMEM)],
              out_shape=[sem, sem] + thru + [_sds((8, LANES), F32)],
              input_output_aliases={i: 2 + i for i in range(n_in + n_out)},
              compiler_params=pltpu.CompilerParams(has_side_effects=DATAFLOW), name=name)(*[hbm(a) for a in ins], *lands, after)
    return (res[0], res[1]), list(res[2:2 + n_in]), list(res[2 + n_in:2 + n_in + n_out]), res[-1]


def exchange_wait(pieces, sems, ins, lands, after, name, sibling=False):
    n_in, n_out = len(ins), len(lands)

    def body(*refs):
        in_refs, land_refs = refs[:n_in], refs[n_in:n_in + n_out]
        send, recv = refs[n_in + n_out], refs[n_in + n_out + 1]
        for cp in _copies(pieces, in_refs, land_refs, send, recv, sibling):
            cp.wait_send()
            cp.wait_recv()

    thru = [pltpu.HBM(a.shape, a.dtype) for a in ins] + [pltpu.HBM(a.shape, a.dtype) for a in lands]
    res = _pc(body, in_specs=[HBM_SPEC] * (n_in + n_out) + [SEM_SPEC, SEM_SPEC, ANY_SPEC], out_specs=[HBM_SPEC] * (n_in + n_out),
              out_shape=thru, input_output_aliases={i: i for i in range(n_in + n_out)},
              compiler_params=pltpu.CompilerParams(has_side_effects=DATAFLOW), name=name)(*ins, *lands, sems[0], sems[1], after)
    return list(res[:n_in]), list(res[n_in:])


def exchange_all(buf, name):
    def body(in_ref, out_ref, send, recv, local):
        x, y, c = lax.axis_index("x"), lax.axis_index("y"), lax.axis_index("c")
        mine = 4 * x + 2 * y + c
        loc = pltpu.make_async_copy(in_ref, out_ref.at[mine], local)
        loc.start()
        copies = [loc]
        for k in range(1, 8):
            peer = (x ^ (k >> 2), y ^ ((k >> 1) & 1), c ^ (k & 1))
            cp = pltpu.make_async_remote_copy(src_ref=in_ref, dst_ref=out_ref.at[mine], send_sem=send.at[k - 1],
                                              recv_sem=recv.at[k - 1], device_id=peer, device_id_type=MESH)
            cp.start()
            copies.append(cp)
        for cp in copies:
            cp.wait()

    anyspec = pl.BlockSpec(memory_space=pl.ANY)
    return _pc(body, in_specs=[anyspec], out_specs=anyspec, out_shape=_sds((8,) + buf.shape, buf.dtype),
               scratch_shapes=[pltpu.SemaphoreType.DMA((7,)), pltpu.SemaphoreType.DMA((7,)), pltpu.SemaphoreType.DMA],
               name=name)(buf)


def _norm_fwd(x, g, name):
    return rowwise(f_rms, [(x, D, 0, 0)], [(g, D, 0, 0)], [(D, 0, BF16)], ts=1024, name=name)[0]


def _norm_bwd(x, g, dh, dres, name):
    (dx,), (dg,) = rowwise_bwd(f_rms, [(x, D, 0, 0)], [(g, D, 0, 0)], [(dh, D, 0, 0)], need=[True],
                               adds={0: (dres, D, 0, 0)}, ts=1024, name=name)
    return dx, dg


def pool_fwd(x, W, tag, late=None):
    h = _norm_fwd(x, W["ng"], tag + "_norm")
    proj = mm(h, W["w_in"], out_dtype=BF16, name=tag + "_in")
    if late is not None:
        W = dict(W, **late(proj))
    p = pool_time_fwd(proj, tag + "_win")
    pg = gmm("nn", p, W["w_grp"], G=4, out_dtype=BF16, name=tag + "_grp")
    y = rowwise(f_pool_gate, [(pg, POOL_GROUP, 0, 1), (proj, POOL_GROUP, 4, 1)], [(W["scale"], POOL_GROUP, 0, 1)],
                [(POOL_GROUP, 1, BF16)], ncol=4, ts=1024, name=tag + "_gate")[0]
    xn = mm(y, W["w_out"], add=x, name=tag + "_out")
    return xn, (x, h, proj, p, pg, y)


def pool_bwd(dxn, W, saved, tag, after=None, emit=None):
    x, h, proj, p, pg, y = saved
    emit = emit or (lambda grads: None)
    dy = mm(dxn, W["w_out"], tb=True, after=after, out_dtype=BF16, name=tag + "_dy")
    g = {}
    (dpg, dproj), (g["scale"],) = rowwise_bwd(
        f_pool_gate, [(pg, POOL_GROUP, 0, 1), (proj, POOL_GROUP, 4, 1)], [(W["scale"], POOL_GROUP, 0, 1)],
        [(dy, POOL_GROUP, 0, 1)], need=[True, True], place={1: (2 * POOL_WIDTH, 4)}, narrow=(0, 1), ncol=4, ts=1024, name=tag + "_dgate")
    dp = gmm("nt", dpg, W["w_grp"], G=4, out_dtype=BF16, name=tag + "_dp")
    dproj = pool_time_bwd(dp, dproj, tag + "_dwin")
    g["w_in"] = mm(h, dproj, ta=True, out_dtype=BF16, name=tag + "_dw_in")
    t1 = emit({"w_in": g["w_in"]})
    g["w_out"] = mm(y, dxn, ta=True, after=t1, out_dtype=BF16, name=tag + "_dwout")
    g["w_grp"] = gmm("tn", p, dpg, G=4, out_dtype=BF16, name=tag + "_dwgrp")
    t2 = emit({"w_out": g["w_out"], "w_grp": g["w_grp"]})
    dh = mm(dproj, W["w_in"], tb=True, after=t2, name=tag + "_dh")
    dx, g["ng"] = _norm_bwd(x, W["ng"], dh, dxn, tag + "_dnorm")
    return dx, g


def gdn_fwd(x, W, tag, late=None):
    h = _norm_fwd(x, W["ng"], tag + "_norm")
    proj = mm(h, W["w_in"], name=tag + "_in")
    qkv = gdn_conv_fwd(proj, W["conv"], tag + "_conv")
    g_b, beta_b = rowwise(f_gdn_gates, [(proj, LANES, 6144 // LANES, 0)], [(W["a_log"], LANES, 0, 0), (W["dt_bias"], LANES, 0, 0)],
                          [(GDN_QK, 0, F32), (GDN_QK, 0, F32)], ts=1024, name=tag + "_gates")
    o, states = gdn_chunk_fwd(qkv, g_b, beta_b, tag + "_chunk")
    og = rowwise(f_gdn_out, [(o, GDN_DV, 0, 1), (proj, GDN_DV, 4096 // GDN_DV, 1)], [(W["norm_g"], GDN_DV, 0, 0)],
                 [(GDN_DV, 1, BF16)], ncol=GDN_H, ts=2048, name=tag + "_onorm")[0]
    if late is not None:
        W = dict(W, **late(og))
    xn = mm(og, W["w_out"], add=x, name=tag + "_out")
    return xn, (x, h, proj, qkv, g_b, beta_b, o, states, og)


def gdn_bwd(dxn, W, saved, tag, after=None):
    x, h, proj, qkv, g_b, beta_b, o, states, og = saved
    dog = mm(dxn, W["w_out"], tb=True, after=after, name=tag + "_dog")
    g = {"w_out": mm(og, dxn, ta=True, out_dtype=BF16, name=tag + "_dwout")}
    (do, dproj), (g["norm_g"],) = rowwise_bwd(
        f_gdn_out, [(o, GDN_DV, 0, 1), (proj, GDN_DV, 4096 // GDN_DV, 1)], [(W["norm_g"], GDN_DV, 0, 0)],
        [(dog, GDN_DV, 0, 1)], need=[True, True], place={1: (GDN_IN_PAD, 4096 // GDN_DV)}, narrow=(1,), ncol=GDN_H, ts=2048, name=tag + "_donorm")
    dq, dk, dv, dg_b, dbeta_b = gdn_chunk_bwd(qkv, g_b, beta_b, states, do, tag + "_dchunk")
    (dproj,), (g["a_log"], g["dt_bias"]) = rowwise_bwd(
        f_gdn_gates, [(proj, LANES, 6144 // LANES, 0)], [(W["a_log"], LANES, 0, 0), (W["dt_bias"], LANES, 0, 0)],
        [(dg_b, GDN_QK, 0, 0), (dbeta_b, GDN_QK, 0, 0)], need=[True], place={0: (dproj, 6144 // LANES)}, ts=1024, name=tag + "_dgates")
    dproj, g["conv"] = gdn_conv_bwd(proj, W["conv"], dq, dk, dv, dproj, tag + "_dconv")
    dh = mm(dproj, W["w_in"], tb=True, name=tag + "_dh")
    g["w_in"] = mm(h, dproj, ta=True, out_dtype=BF16, name=tag + "_dw_in")
    dx, g["ng"] = _norm_bwd(x, W["ng"], dh, dxn, tag + "_dnorm")
    return dx, g


def mla_fwd(x, pos, W, tag):
    h = _norm_fwd(x, W["ng"], tag + "_norm")
    proj = mm(h, W["w_in"], out_dtype=BF16, name=tag + "_in")
    hq = rowwise(f_rms, [(proj, MLA_Q_LORA, 0, 0)], [(W["q_g"], MLA_Q_LORA, 0, 0)], [(MLA_Q_LORA, 0, BF16)], ts=1024, name=tag + "_qnorm")[0]
    hkv = rowwise(f_rms, [(proj, MLA_KV_LORA, 2, 0)], [(W["kv_g"], MLA_KV_LORA, 0, 0)], [(MLA_KV_LORA, 0, BF16)], ts=1024, name=tag + "_kvnorm")[0]
    qpad = mm(hq, W["w_uq"], out_dtype=BF16, name=tag + "_uq")
    kv = mm(hkv, W["w_ukv"], out_dtype=BF16, name=tag + "_ukv")
    qh, kh, vh = mla_prep_fwd(qpad, kv, proj, pos, W["rope"], tag + "_prep")
    o, lse = flash_fwd(qh, kh, vh, tag + "_attn")
    og = rowwise(f_ogate, [(o, 512, 0, 1), (proj, 512, 4, 1)], [], [(512, 1, BF16)], ncol=4, ts=1024, name=tag + "_ogate")[0]
    xn = mm(og, W["w_out"], add=x, name=tag + "_out")
    return xn, (x, h, proj, hq, hkv, qh, kh, vh, o, lse, og)


def mla_bwd(dxn, pos, W, saved, tag, after=None):
    x, h, proj, hq, hkv, qh, kh, vh, o, lse, og = saved
    dog = mm(dxn, W["w_out"], tb=True, after=after, out_dtype=BF16, name=tag + "_dog")
    g = {"w_out": mm(og, dxn, ta=True, out_dtype=BF16, name=tag + "_dwout")}
    dproj = jnp.zeros(proj.shape, BF16)
    (do, dproj), _ = rowwise_bwd(f_ogate, [(o, 512, 0, 1), (proj, 512, 4, 1)], [], [(dog, 512, 0, 1)], need=[True, True],
                                 place={1: (dproj, 4)}, narrow=(0,), ncol=4, ts=1024, name=tag + "_dogate")
    dqh, dkh, dvh = flash_bwd(qh, kh, vh, o, lse, do, tag + "_dattn")
    dqpad, dkv, dproj = mla_prep_bwd(dqh, dkh, dvh, pos, W["rope"], dproj, tag + "_dprep")
    dhq = mm(dqpad, W["w_uq"], tb=True, name=tag + "_dhq")
    g["w_uq"] = mm(hq, dqpad, ta=True, out_dtype=BF16, name=tag + "_dwuq")
    dhkv = mm(dkv, W["w_ukv"], tb=True, name=tag + "_dhkv")
    g["w_ukv"] = mm(hkv, dkv, ta=True, out_dtype=BF16, name=tag + "_dwukv")
    (dproj,), (g["q_g"],) = rowwise_bwd(f_rms, [(proj, MLA_Q_LORA, 0, 0)], [(W["q_g"], MLA_Q_LORA, 0, 0)], [(dhq, MLA_Q_LORA, 0, 0)],
                                        need=[True], place={0: (dproj, 0)}, ts=512, name=tag + "_dqnorm")
    (dproj,), (g["kv_g"],) = rowwise_bwd(f_rms, [(proj, MLA_KV_LORA, 2, 0)], [(W["kv_g"], MLA_KV_LORA, 0, 0)], [(dhkv, MLA_KV_LORA, 0, 0)],
                                         need=[True], place={0: (dproj, 2)}, ts=512, name=tag + "_dkvnorm")
    dh = mm(dproj, W["w_in"], tb=True, name=tag + "_dh")
    g["w_in"] = mm(h, dproj, ta=True, out_dtype=BF16, name=tag + "_dw_in")
    dx, g["ng"] = _norm_bwd(x, W["ng"], dh, dxn, tag + "_dnorm")
    return dx, g


def _pad_cols(a, n):
    return jnp.pad(a, ((0, 0), (0, n - a.shape[1])))


def _mla_w_in_layout(w):
    z = lambda n: jnp.zeros((w.shape[0], n), w.dtype)
    kr = w[:, 1280:1344]
    return jnp.concatenate([w[:, :768], z(256), w[:, 768:1280], kr[:, :32], z(32), kr[:, 32:], z(32), z(384), w[:, 1344:]], axis=1)


def _mla_w_in_unlayout(g):
    return jnp.concatenate([g[:, :768], g[:, 1024:1536], g[:, 1536:1568], g[:, 1600:1632], g[:, 2048:]], axis=1)


def _mla_w_uq_layout(w):
    w3 = w.reshape(w.shape[0], MLA_H, MLA_NOPE + MLA_ROPE)
    z = jnp.zeros((w.shape[0], MLA_H, 32), w.dtype)
    return jnp.concatenate([w3[..., :128], w3[..., 128:160], z, w3[..., 160:192], z], axis=-1).reshape(w.shape[0], MLA_H * 256)


def _mla_w_uq_unlayout(g):
    g3 = g.reshape(g.shape[0], MLA_H, 256)
    return jnp.concatenate([g3[..., :128], g3[..., 128:160], g3[..., 192:224]], axis=-1).reshape(g.shape[0], MLA_H * 192)


def _rope_consts():
    half = MLA_ROPE // 2
    inv = ROPE_THETA ** (-jnp.arange(half, dtype=F32) / half)
    z = jnp.zeros((half,), F32)
    o = jnp.ones((half,), F32)
    row = lambda *p: jnp.concatenate(p).reshape(1, LANES)
    return row(inv, z, inv, z), row(o, z, o, z), row(-o, z, o, z)


BIG = ["pool_w_in", "pool_w_grp", "pool_w_out", "gdn_w_in", "gdn_w_out", "mla_w_in", "mla_w_uq", "mla_w_ukv", "mla_w_out"]
BIG_LAYOUT = {"pool_w_in": (1, 1024, (1024, 4096)), "pool_w_grp": (1, 128, (4, 512, 512)), "pool_w_out": (0, 512, (2048, 1024)),
              "gdn_w_in": (None, None, (4, 1024, 1540)), "gdn_w_out": (0, 512, (2048, 1024)),
              "mla_w_in": (None, None, (4, 1024, 848)), "mla_w_uq": (1, 768, (768, 3072)), "mla_w_ukv": (1, 1024, (512, 4096)),
              "mla_w_out": (0, 512, (2048, 1024))}
SMALL_SHARDED = ["pool_scale", "gdn_conv", "mla_q_norm_g", "mla_kv_norm_g"]
SMALL_AXIS = {"pool_scale": 1, "gdn_conv": 2, "mla_q_norm_g": 1, "mla_kv_norm_g": 1}
REPLICATED = ["norm_g", "gdn_a_log", "gdn_dt_bias", "gdn_norm_g", "final_g"]
PACK_C = 1024


def _pack(parts, dtype, row_mult):
    flat = jnp.concatenate([p.reshape(-1).astype(dtype) for p in parts])
    rows = -(-flat.shape[0] // PACK_C)
    rows = -(-rows // row_mult) * row_mult
    return jnp.pad(flat, (0, rows * PACK_C - flat.shape[0])).reshape(rows, PACK_C)


def _unpack(buf, shapes):
    lead = buf.shape[:-2]
    flat = buf.reshape(lead + (-1,))
    out, off = [], 0
    for s in shapes:
        n = int(np.prod(s))
        out.append(flat[..., off:off + n].reshape(lead + tuple(s)))
        off += n
    return out


def _unshard(g4, axis):
    a = jnp.moveaxis(g4, 0, axis)
    s = a.shape
    return a.reshape(s[:axis] + (s[axis] * s[axis + 1],) + s[axis + 2:])


def _to_shards(a, axis):
    s = a.shape
    return jnp.moveaxis(a.reshape(s[:axis] + (4, s[axis] // 4) + s[axis + 1:]), axis, 0)


def layer_weights(full, small, rep, layer):
    ng = rep["norm_g"][layer:layer + 1]
    side_by_side = lambda a4: jnp.moveaxis(a4, 0, 1).reshape(a4.shape[1], 4 * a4.shape[2])
    if layer in (0, 3):
        j = layer // 3
        return dict(ng=ng, w_in=full[("pool_w_in", j)], w_grp=full[("pool_w_grp", j)], scale=small["pool_scale"][j:j + 1],
                    w_out=full[("pool_w_out", j)])
    if layer == 1:
        return dict(ng=ng, w_in=_pad_cols(side_by_side(full[("gdn_w_in", 0)]), GDN_IN_PAD),
                    conv=jnp.pad(small["gdn_conv"][0], ((0, 4), (0, 0))), a_log=_pad_cols(rep["gdn_a_log"], LANES),
                    dt_bias=_pad_cols(rep["gdn_dt_bias"], LANES), norm_g=rep["gdn_norm_g"], w_out=full.get(("gdn_w_out", 0)))
    return dict(ng=ng, w_in=_mla_w_in_layout(side_by_side(full[("mla_w_in", 0)])), q_g=small["mla_q_norm_g"],
                kv_g=small["mla_kv_norm_g"], w_uq=_mla_w_uq_layout(full[("mla_w_uq", 0)]), w_ukv=full[("mla_w_ukv", 0)],
                w_out=full[("mla_w_out", 0)], rope=_rope_consts())


def big_grad_pieces(gl):
    g0, g1, g2, g3 = gl
    slots = lambda a: jnp.moveaxis(a.reshape(a.shape[0], 4, a.shape[1] // 4), 1, 0)
    out = {}
    for l, g in ((0, g0), (1, g3)):
        if g is not None:
            out.update({("pool_w_in", l): g["w_in"], ("pool_w_grp", l): g["w_grp"], ("pool_w_out", l): g["w_out"]})
    if g1 is not None:
        out.update({("gdn_w_in", 0): slots(g1["w_in"][:, :GDN_IN]), ("gdn_w_out", 0): g1["w_out"]})
    if g2 is not None:
        out.update({("mla_w_in", 0): slots(_mla_w_in_unlayout(g2["w_in"])), ("mla_w_uq", 0): _mla_w_uq_unlayout(g2["w_uq"]),
                    ("mla_w_ukv", 0): g2["w_ukv"], ("mla_w_out", 0): g2["w_out"]})
    return out


def small_grads(gl, dfinal):
    g0, g1, g2, g3 = gl
    return {"norm_g": jnp.concatenate([g0["ng"], g1["ng"], g2["ng"], g3["ng"]], axis=0),
            "pool_scale": jnp.concatenate([g0["scale"], g3["scale"]], axis=0), "gdn_conv": g1["conv"][None, :4],
            "gdn_a_log": g1["a_log"][:, :GDN_H], "gdn_dt_bias": g1["dt_bias"][:, :GDN_H], "gdn_norm_g": g1["norm_g"],
            "mla_q_norm_g": g2["q_g"], "mla_kv_norm_g": g2["kv_g"], "final_g": dfinal.reshape(D)}


NAMES = ["norm_g", "pool_w_in", "pool_w_grp", "pool_scale", "pool_w_out", "gdn_w_in", "gdn_conv", "gdn_a_log", "gdn_dt_bias",
         "gdn_norm_g", "gdn_w_out", "mla_w_in", "mla_q_norm_g", "mla_w_uq", "mla_kv_norm_g", "mla_w_ukv", "mla_w_out", "final_g"]


def kernel(x, positions, norm_g, pool_w_in, pool_w_grp, pool_scale, pool_w_out, gdn_w_in, gdn_conv, gdn_a_log, gdn_dt_bias, gdn_norm_g, gdn_w_out, mla_w_in, mla_q_norm_g, mla_w_uq, mla_kv_norm_g, mla_w_ukv, mla_w_out, final_g, loss_target, m_norm_g, m_pool_w_in, m_pool_w_grp, m_pool_scale, m_pool_w_out, m_gdn_w_in, m_gdn_conv, m_gdn_a_log, m_gdn_dt_bias, m_gdn_norm_g, m_gdn_w_out, m_mla_w_in, m_mla_q_norm_g, m_mla_w_uq, m_mla_kv_norm_g, m_mla_w_ukv, m_mla_w_out, m_final_g, v_norm_g, v_pool_w_in, v_pool_w_grp, v_pool_scale, v_pool_w_out, v_gdn_w_in, v_gdn_conv, v_gdn_a_log, v_gdn_dt_bias, v_gdn_norm_g, v_gdn_w_out, v_mla_w_in, v_mla_q_norm_g, v_mla_w_uq, v_mla_kv_norm_g, v_mla_w_ukv, v_mla_w_out, v_final_g):
    args = locals()
    w = {n: args[n] for n in NAMES}
    m = {n: args["m_" + n] for n in NAMES}
    v = {n: args["v_" + n] for n in NAMES}
    my_chip = (2 * lax.axis_index("x") + lax.axis_index("y")).astype(I32)
    S_ = x.shape[1]
    x0, pos, target = x[0], positions.reshape(S_, 1).astype(F32), loss_target[0]
    rep = {n: w[n] for n in REPLICATED}

    small_shapes = [w[n].shape for n in SMALL_SHARDED]
    small_pack = _pack([w[n] for n in SMALL_SHARDED], F32, 8)[None]
    layout = dict(BIG_LAYOUT, small=(None, None, (4,) + small_pack.shape[1:]))

    def shard(key, token):
        n, l = key
        if n == "small":
            return small_pack
        a = w[n][l:l + 1]
        return (a if token is None else a + token[0, 0]).astype(BF16)

    def gather_start(group, after, tag, token=None):
        pieces = [gather_piece(i, 0, i, layout[n][0], layout[n][1]) for i, (n, l) in enumerate(group)]
        ins = [shard(k, token) for k in group]
        shapes = [_sds(layout[n][2], a.dtype) for (n, l), a in zip(group, ins)]
        sems, ins, lands, token = exchange_start(pieces, ins, shapes, after, tag + "_start")
        return (pieces, sems, ins, lands), token

    def finish(handle, after, tag):
        return exchange_wait(*handle, after, tag + "_wait")

    def gathered(group, handle, after, tag):
        srcs, lands = finish(handle, after, tag)
        return {(n, l): place_own(a, s[0], layout[n][0], layout[n][1], my_chip) for (n, l), s, a in zip(group, srcs, lands)}

    group_a = [("small", 0), ("pool_w_in", 0)]
    group_a2 = [("pool_w_grp", 0), ("pool_w_out", 0)]
    group_b = [("gdn_w_in", 0)]
    group_c = [("gdn_w_out", 0), ("mla_w_in", 0), ("mla_w_uq", 0), ("mla_w_ukv", 0), ("mla_w_out", 0), ("pool_w_in", 1),
               ("pool_w_grp", 1), ("pool_w_out", 1)]
    full = {}
    h_a, t_a = gather_start(group_a, x0, "gather_a")
    h_a2, t_a2 = gather_start(group_a2, t_a, "gather_a2", t_a)
    h_b, t_b = gather_start(group_b, t_a2, "gather_b", t_a2)
    h_c, t_c = gather_start(group_c, t_b, "gather_c", t_b)
    full.update(gathered(group_a, h_a, t_c, "gather_a"))
    small = {n: _unshard(a, SMALL_AXIS[n]) for n, a in zip(SMALL_SHARDED, _unpack(full[("small", 0)], small_shapes))}

    def late_l0(proj):
        full.update(gathered(group_a2, h_a2, proj, "gather_a2"))
        return dict(w_grp=full[("pool_w_grp", 0)], w_out=full[("pool_w_out", 0)])

    first = dict(ng=rep["norm_g"][0:1] + t_c[0:1, 0:1], w_in=full[("pool_w_in", 0)], scale=small["pool_scale"][0:1])
    x1, s0 = pool_fwd(x0, first, "l0", late=late_l0)
    W0 = layer_weights(full, small, rep, 0)
    full.update(gathered(group_b, h_b, x1, "gather_b"))

    def late_l1(og):
        full.update(gathered(group_c, h_c, og, "gather_c"))
        return dict(w_out=full[("gdn_w_out", 0)])

    x2, s1 = gdn_fwd(x1, layer_weights(full, small, rep, 1), "l1", late=late_l1)
    W1, W2, W3 = (layer_weights(full, small, rep, i) for i in (1, 2, 3))
    x3, s2 = mla_fwd(x2, pos, W2, "l2")
    x4, s3 = pool_fwd(x3, W3, "l3")
    loss_part, dx4, dfinal = loss_head(x4, target, final_g.reshape(1, D), "loss_head")

    def scatter_start(pieces_of, after, tag):
        keys = list(pieces_of)
        pieces = [scatter_piece(i, i, BIG_LAYOUT[n][0], BIG_LAYOUT[n][1]) for i, (n, l) in enumerate(keys)]
        shapes = [_sds((4,) + tuple(w[n].shape[1:]), BF16) for n, l in keys]
        sems, ins, lands, token = exchange_start(pieces, [pieces_of[k] for k in keys], shapes, after, tag + "_start")
        return keys, (pieces, sems, ins, lands), token

    def scattered(keys, handle, after, tag):
        srcs, lands = finish(handle, after, tag)
        return {(n, l): place_own(a, own_window(g, BIG_LAYOUT[n][0], BIG_LAYOUT[n][1], my_chip), None, None, my_chip)
                for (n, l), g, a in zip(keys, srcs, lands)}

    dx3, g3 = pool_bwd(dx4, W3, s3, "l3")
    k3, h3, t3 = scatter_start(big_grad_pieces((None, None, None, g3)), dx3, "scatter_l3")
    dx2, g2 = mla_bwd(dx3, pos, W2, s2, "l2", after=t3)
    k2, h2, t2 = scatter_start(big_grad_pieces((None, None, g2, None)), dx2, "scatter_l2")
    dx1, g1 = gdn_bwd(dx2, W1, s1, "l1", after=t2)
    k1, h1, t1 = scatter_start(big_grad_pieces((None, g1, None, None)), dx1, "scatter_l1")
    def swap_start(part, tag):
        keys = list(part)
        ins = [part[k] for k in keys]
        pieces = [whole_piece(i) for i in range(len(keys))]
        sems, ins, lands, token = exchange_start(pieces, ins, [_sds(a.shape, a.dtype) for a in ins], ins[0], tag + "_start", sibling=True)
        swaps.append((keys, (pieces, sems, ins, lands), tag))
        return token

    last, swaps = [], []

    def emit_l0(grads):
        first = not last
        now = next(iter(grads.values()))
        early = [(k3, h3, "scatter_l3"), (k2, h2, "scatter_l2")] if first else [(k1, h1, "scatter_l1")]
        landed = {}
        for keys, handle, tag in early:
            landed.update(scattered(keys, handle, now, tag))
        swapping = swap_start(landed, "swap_a" if first else "swap_b")
        tag = "scatter_l0a" if first else "scatter_l0b"
        keys, handle, token = scatter_start({("pool_" + k, 0): a for k, a in grads.items()}, swapping, tag)
        last.append((keys, handle, tag))
        return token

    dx0, g0 = pool_bwd(dx1, W0, s0, "l0", after=t1, emit=emit_l0)
    landed = {}
    for keys, handle, tag in last:
        landed.update(scattered(keys, handle, dx0, tag))
    swapping = swap_start(landed, "swap_c")
    recv, sib = {}, {}
    for keys, handle, tag in swaps:
        mine_, theirs = exchange_wait(*handle, dx0, tag + "_wait", sibling=True)
        recv.update(zip(keys, mine_))
        sib.update(zip(keys, theirs))

    sg = small_grads((g0, g1, g2, g3), dfinal)
    small_names = SMALL_SHARDED + REPLICATED
    small_buf = _pack([sg[n] for n in small_names] + [loss_part], F32, 8)
    small_sum = sum_slots(exchange_all(small_buf, "gather_small"), "sum_small")
    full_small = _unpack(small_sum, [sg[n].shape for n in small_names] + [(1, LANES)])
    loss = full_small[-1][0, 0]
    small_part = {}
    for n, a in zip(small_names, full_small[:-1]):
        if n in SMALL_AXIS:
            a = lax.dynamic_index_in_dim(_to_shards(a, SMALL_AXIS[n]), my_chip, axis=0, keepdims=False)
        small_part[n] = a

    outs = []
    for n in NAMES:
        shp = w[n].shape
        two = (int(np.prod(shp[:-1])), shp[-1]) if len(shp) > 1 else (1, shp[0])
        if n in BIG_LAYOUT:
            layers = shp[0]
            rows = lambda a: a.reshape(4, two[0] // layers, two[1])
            parts = [[rows(recv[(n, l)]) for l in range(layers)], [rows(sib[(n, l)]) for l in range(layers)]]
        else:
            parts = [[small_part[n].reshape((1,) + two)]]
        upper = n in BIG_LAYOUT and not n.startswith("pool")
        res = adamw(w[n].reshape(two), parts, m[n].reshape(two), v[n].reshape(two), "adamw_" + n, after=swapping if upper else None)
        outs.append([r.reshape(shp) for r in res])
    return (loss, dx0[None], *[o[0] for o in outs], *[o[1] for o in outs], *[o[2] for o in outs], *[o[3] for o in outs])
```

```python
import math

import jax
import jax.numpy as jnp
import numpy as np
from jax import lax
from jax.experimental import pallas as pl
from jax.experimental.pallas import tpu as pltpu

F32 = jnp.float32
BF16 = jnp.bfloat16
I32 = jnp.int32

D = 1024
EPS = 1e-6
POOL_WIDTH = 2048
POOL_GROUP = 512
GDN_H, GDN_DK, GDN_DV, GDN_C = 8, 128, 256, 64
GDN_QK, GDN_V, GDN_CONV_CH, GDN_IN = 1024, 2048, 4096, 6160
GDN_IN_PAD = 6272
MLA_H, MLA_NOPE, MLA_ROPE, MLA_V = 16, 128, 64, 128
MLA_Q_LORA, MLA_KV_LORA, MLA_WIDTH, MLA_IN = 768, 512, 2048, 3392
MLA_IN_PAD = 4096
MLA_SCALE = (MLA_NOPE + MLA_ROPE) ** -0.5
ROPE_THETA = 10000.0
ADAM_LR, ADAM_B1, ADAM_B2, ADAM_EPS, ADAM_WD, ADAM_STEP = 0.001, 0.9, 0.999, 1e-08, 0.01, 10

VMEM_LIMIT_V7X = 56 * 1024 * 1024
LANES = 128
MESH = pl.DeviceIdType.MESH


def _pc(body, **kw):
    return pl.pallas_call(body, **kw)


def _cparams(sem):
    return pltpu.CompilerParams(dimension_semantics=sem, vmem_limit_bytes=VMEM_LIMIT_V7X)


def _tile(n, cap):
    t = (cap // LANES) * LANES
    while t >= LANES:
        if n % t == 0:
            return t
        t -= LANES
    return n


def _sds(shape, dt):
    return jax.ShapeDtypeStruct(shape, dt)


def mm(a, b, *, ta=False, tb=False, add=None, after=None, out_dtype=F32, name):
    if ta:
        K, M = a.shape
    else:
        M, K = a.shape
    if tb:
        N, K2 = b.shape
    else:
        K2, N = b.shape
    assert K == K2, (a.shape, b.shape, ta, tb)
    tm, tn, tk = _tile(M, 1024), _tile(N, 1024), _tile(K, 4096)
    nk = K // tk
    a_spec = pl.BlockSpec((tk, tm), lambda i, j, k: (k, i)) if ta else pl.BlockSpec((tm, tk), lambda i, j, k: (i, k))
    b_spec = pl.BlockSpec((tn, tk), lambda i, j, k: (j, k)) if tb else pl.BlockSpec((tk, tn), lambda i, j, k: (k, j))
    o_spec = pl.BlockSpec((tm, tn), lambda i, j, k: (i, j))
    dn = (((0 if ta else 1,), (1 if tb else 0,)), ((), ()))
    has_add = add is not None

    def body(*refs):
        a_ref, b_ref = refs[0], refs[1]
        part = lax.dot_general(a_ref[...].astype(BF16), b_ref[...].astype(BF16), dn, preferred_element_type=F32)
        if nk == 1:
            refs[-1][...] = (part + refs[2][...] if has_add else part).astype(out_dtype)
            return
        o_ref, acc = refs[-2], refs[-1]
        k = pl.program_id(2)

        @pl.when(k == 0)
        def _():
            acc[...] = part

        @pl.when(k > 0)
        def _():
            acc[...] += part

        @pl.when(k == nk - 1)
        def _():
            r = acc[...]
            if has_add:
                r = r + refs[2][...]
            o_ref[...] = r.astype(out_dtype)

    ins = [a, b] + ([add] if has_add else []) + ([after] if after is not None else [])
    specs = [a_spec, b_spec] + ([o_spec] if has_add else []) + ([pl.BlockSpec(memory_space=pl.ANY)] if after is not None else [])
    return _pc(body, grid=(M // tm, N // tn, nk), in_specs=specs, out_specs=o_spec, out_shape=_sds((M, N), out_dtype),
               scratch_shapes=[pltpu.VMEM((tm, tn), F32)] if nk > 1 else [], compiler_params=_cparams(("parallel", "parallel", "arbitrary")),
               name=name)(*ins)


def gmm(kind, a, b, *, G, name, out_dtype=F32):
    S_ = a.shape[0]
    Ka = a.shape[1] // G
    if kind == "tn":
        N = b.shape[1] // G
        tk = _tile(S_, 2048)
        nk = S_ // tk

        def body(a_ref, b_ref, o_ref, acc):
            k = pl.program_id(1)

            @pl.when(k == 0)
            def _():
                acc[...] = jnp.zeros_like(acc)

            acc[...] += lax.dot_general(a_ref[...].astype(BF16), b_ref[...].astype(BF16), (((0,), (0,)), ((), ())),
                                        preferred_element_type=F32)

            @pl.when(k == nk - 1)
            def _():
                o_ref[...] = acc[...].astype(out_dtype)

        return _pc(body, grid=(G, nk),
                   in_specs=[pl.BlockSpec((tk, Ka), lambda g, k: (k, g)), pl.BlockSpec((tk, N), lambda g, k: (k, g))],
                   out_specs=pl.BlockSpec((None, Ka, N), lambda g, k: (g, 0, 0)), out_shape=_sds((G, Ka, N), out_dtype),
                   scratch_shapes=[pltpu.VMEM((Ka, N), F32)], compiler_params=_cparams(("parallel", "arbitrary")), name=name)(a, b)
    N = b.shape[2] if kind == "nn" else b.shape[1]
    tm = _tile(S_, 4096)
    dn = (((1,), (0 if kind == "nn" else 1,)), ((), ()))

    def body(a_ref, b_ref, o_ref):
        o_ref[...] = lax.dot_general(a_ref[...].astype(BF16), b_ref[...].astype(BF16), dn, preferred_element_type=F32).astype(out_dtype)

    bshape = (None,) + tuple(b.shape[1:])
    return _pc(body, grid=(G, S_ // tm),
               in_specs=[pl.BlockSpec((tm, Ka), lambda g, i: (i, g)), pl.BlockSpec(bshape, lambda g, i: (g, 0, 0))],
               out_specs=pl.BlockSpec((tm, N), lambda g, i: (i, g)), out_shape=_sds((S_, G * N), out_dtype),
               compiler_params=_cparams(("parallel", "parallel")), name=name)(a, b)


def _rw_spec(ts, w, c, s):
    return pl.BlockSpec((ts, w), lambda j, i: (i, c + j * s))


def _rw_pspec(p, w, c, s):
    return pl.BlockSpec((p.shape[0], w), lambda j, i: (0, c + j * s))


def rowwise(f, tiles, params, outs, *, ncol=1, ts, name):
    S_ = tiles[0][0].shape[0]
    nin = len(tiles) + len(params)

    def body(*refs):
        res = f(pl.program_id(0), *[r[...].astype(F32) for r in refs[:nin]])
        for r, o in zip(refs[nin:], res):
            r[...] = o.astype(r.dtype)

    return _pc(body, grid=(ncol, S_ // ts),
               in_specs=[_rw_spec(ts, w, c, s) for (_, w, c, s) in tiles] + [_rw_pspec(*p) for p in params],
               out_specs=[_rw_spec(ts, w, 0, s) for (w, s, _) in outs],
               out_shape=[_sds((S_, w * (ncol if s else 1)), dt) for (w, s, dt) in outs],
               compiler_params=_cparams(("parallel", "parallel")), name=name)(*[t[0] for t in tiles], *[p[0] for p in params])


def rowwise_bwd(f, tiles, params, cots, *, need, adds=None, place=None, narrow=(), ncol=1, ts, name):
    S_ = tiles[0][0].shape[0]
    adds = adds or {}
    place = place or {}
    nt, npar, nc = len(tiles), len(params), len(cots)
    add_keys = sorted(adds)
    need_idx = [k for k in range(nt) if need[k]]
    into_keys = [k for k in need_idx if k in place and not isinstance(place[k][0], int)]
    n_extra = len(add_keys) + len(into_keys)

    def body(*refs):
        j, i = pl.program_id(0), pl.program_id(1)
        vals = [r[...].astype(F32) for r in refs[:nt + npar]]
        cvals = tuple(r[...].astype(F32) for r in refs[nt + npar:nt + npar + nc])
        add_refs = refs[nt + npar + nc:nt + npar + nc + len(add_keys)]
        out_refs = refs[nt + npar + nc + n_extra:]
        _, vjp = jax.vjp(lambda *v: tuple(f(j, *v)), *vals)
        grads = vjp(cvals)
        for n, k in enumerate(need_idx):
            g = grads[k]
            if k in adds:
                g = g + add_refs[add_keys.index(k)][...]
            out_refs[n][...] = g.astype(out_refs[n].dtype)
        for n in range(npar):
            ref = out_refs[len(need_idx) + n]
            first = (i == 0) if params[n][3] else jnp.logical_and(i == 0, j == 0)

            @pl.when(first)
            def _():
                ref[...] = jnp.zeros_like(ref)

            ref[...] += grads[nt + n]

    in_specs = ([_rw_spec(ts, w, c, s) for (_, w, c, s) in tiles] + [_rw_pspec(*p) for p in params]
                + [_rw_spec(ts, w, c, s) for (_, w, c, s) in cots] + [_rw_spec(ts, *adds[k][1:]) for k in add_keys]
                + [pl.BlockSpec(memory_space=pl.ANY) for _ in into_keys])
    out_specs, out_shape, aliases = [], [], {}
    for n, k in enumerate(need_idx):
        w, s = tiles[k][1], tiles[k][3]
        if k in place:
            dst, c0 = place[k]
            total = dst if isinstance(dst, int) else dst.shape[1]
            out_specs.append(_rw_spec(ts, w, c0, s))
            out_shape.append(_sds((S_, total), (BF16 if k in narrow else F32) if isinstance(dst, int) else dst.dtype))
            if k in into_keys:
                aliases[nt + npar + nc + len(add_keys) + into_keys.index(k)] = n
        else:
            out_specs.append(_rw_spec(ts, w, 0, s))
            out_shape.append(_sds((S_, w * (ncol if s else 1)), BF16 if k in narrow else F32))
    out_specs += [_rw_pspec(p[0], p[1], p[2], p[3]) for p in params]
    out_shape += [_sds(p[0].shape, F32) for p in params]
    res = _pc(body, grid=(ncol, S_ // ts), in_specs=in_specs, out_specs=out_specs, out_shape=out_shape,
              input_output_aliases=aliases, compiler_params=_cparams(("arbitrary", "arbitrary")), name=name)(
        *[t[0] for t in tiles], *[p[0] for p in params], *[c[0] for c in cots], *[adds[k][0] for k in add_keys],
        *[place[k][0] for k in into_keys])
    return list(res[:len(need_idx)]), list(res[len(need_idx):])


def _rms(x, g):
    r = lax.rsqrt(jnp.mean(x * x, axis=-1, keepdims=True) + EPS)
    return x * r * g


def _silu(x):
    return x * jax.nn.sigmoid(x)


@jax.custom_vjp
def _softplus(x):
    return jnp.maximum(x, 0.0) + jnp.log1p(jnp.exp(-jnp.abs(x)))


_softplus.defvjp(lambda x: (_softplus(x), x), lambda x, d: (d * jax.nn.sigmoid(x),))


def f_rms(j, x, g):
    return (_rms(x, g),)


def f_pool_gate(j, pg, gate, scale):
    return (pg * scale * _silu(gate),)


def f_ogate(j, o, gate):
    return (o * _silu(gate),)


def f_gdn_out(j, o, gate, g):
    return (_rms(o, g) * _silu(gate),)


def f_gdn_gates(j, ba, alog, dtb):
    lane = lax.broadcasted_iota(I32, (1, LANES), 1)
    gs, bs = [], []
    for h in range(GDN_H):
        eb = (lane == h).astype(F32)
        ea = (lane == GDN_H + h).astype(F32)
        b = jnp.sum(ba * eb, -1, keepdims=True)
        a = jnp.sum(ba * ea, -1, keepdims=True)
        al = jnp.sum(alog * eb, -1, keepdims=True)
        dt = jnp.sum(dtb * eb, -1, keepdims=True)
        g = -jnp.exp(al) * _softplus(a + dt)
        gs.append(jnp.broadcast_to(g, ba.shape))
        bs.append(jnp.broadcast_to(jax.nn.sigmoid(b), ba.shape))
    return jnp.concatenate(gs, 1), jnp.concatenate(bs, 1)


def _shift_dn(x, k):
    rows = lax.broadcasted_iota(I32, x.shape, 0)
    return jnp.where(rows < k, 0.0, pltpu.roll(x, k, 0))


def _shift_up(x, k):
    n = x.shape[0]
    rows = lax.broadcasted_iota(I32, x.shape, 0)
    return jnp.where(rows >= n - k, 0.0, pltpu.roll(x, n - k, 0))


def _pool_window(j):
    g = lax.div(j, POOL_GROUP // LANES)
    return jnp.where(g == 0, 2.0, jnp.where(g == 1, 4.0, jnp.where(g == 2, 8.0, 16.0))), g


def _pick(g, a2, a4, a8, a16):
    return jnp.where(g == 0, a2, jnp.where(g == 1, a4, jnp.where(g == 2, a8, a16)))


def pool_time_fwd(proj, name):
    S_ = proj.shape[0]

    def body(u_ref, p_ref):
        u = u_ref[...].astype(F32)
        w, g = _pool_window(pl.program_id(0))
        s2 = u + _shift_dn(u, 1)
        s4 = s2 + _shift_dn(s2, 2)
        s8 = s4 + _shift_dn(s4, 4)
        s16 = s8 + _shift_dn(s8, 8)
        t1 = (lax.broadcasted_iota(I32, u.shape, 0) + 1).astype(F32)
        p_ref[...] = (_pick(g, s2, s4, s8, s16) / jnp.minimum(t1, w) - u).astype(p_ref.dtype)

    return _pc(body, grid=(POOL_WIDTH // LANES,), in_specs=[pl.BlockSpec((S_, LANES), lambda j: (0, j))],
               out_specs=pl.BlockSpec((S_, LANES), lambda j: (0, j)), out_shape=_sds((S_, POOL_WIDTH), BF16),
               compiler_params=_cparams(("parallel",)), name=name)(proj)


def pool_time_bwd(dp, into, name):
    S_ = dp.shape[0]

    def body(dp_ref, _, du_ref):
        d = dp_ref[...].astype(F32)
        w, g = _pool_window(pl.program_id(0))
        t1 = (lax.broadcasted_iota(I32, d.shape, 0) + 1).astype(F32)
        q = d / jnp.minimum(t1, w)
        r2 = q + _shift_up(q, 1)
        r4 = r2 + _shift_up(r2, 2)
        r8 = r4 + _shift_up(r4, 4)
        r16 = r8 + _shift_up(r8, 8)
        du_ref[...] = (_pick(g, r2, r4, r8, r16) - d).astype(du_ref.dtype)

    return _pc(body, grid=(POOL_WIDTH // LANES,),
               in_specs=[pl.BlockSpec((S_, LANES), lambda j: (0, j)), pl.BlockSpec(memory_space=pl.ANY)],
               out_specs=pl.BlockSpec((S_, LANES), lambda j: (0, j)), out_shape=_sds(into.shape, into.dtype),
               input_output_aliases={1: 0}, compiler_params=_cparams(("parallel",)), name=name)(dp, into)


def _conv_post(j, a):
    n = a * lax.rsqrt(jnp.sum(a * a, axis=-1, keepdims=True) + EPS)
    nq = GDN_QK // LANES
    return jnp.where(j < nq, n * (GDN_DK ** -0.5), jnp.where(j < 2 * nq, n, a))


def _conv_taps(u):
    return [_shift_dn(u, 3), _shift_dn(u, 2), _shift_dn(u, 1), u]


def _conv_pre(taps, w):
    return w[0:1] * taps[0] + w[1:2] * taps[1] + w[2:3] * taps[2] + w[3:4] * taps[3]


def gdn_conv_fwd(proj, conv_w, name):
    S_ = proj.shape[0]

    def body(u_ref, w_ref, o_ref):
        o_ref[...] = _conv_post(pl.program_id(0), _silu(_conv_pre(_conv_taps(u_ref[...]), w_ref[...])))

    return _pc(body, grid=(GDN_CONV_CH // LANES,),
               in_specs=[pl.BlockSpec((S_, LANES), lambda j: (0, j)), pl.BlockSpec((8, LANES), lambda j: (0, j))],
               out_specs=pl.BlockSpec((S_, LANES), lambda j: (0, j)), out_shape=_sds((S_, GDN_CONV_CH), F32),
               compiler_params=_cparams(("parallel",)), name=name)(proj, conv_w)


def gdn_conv_bwd(proj, conv_w, dq, dk, dv, into, name):
    S_ = proj.shape[0]
    nq = GDN_QK // LANES

    def body(u_ref, w_ref, dq_ref, dk_ref, dv_ref, _, du_ref, dw_ref):
        j = pl.program_id(0)
        u, w = u_ref[...], w_ref[...]
        taps = _conv_taps(u)
        c = _conv_pre(taps, w)
        sig = jax.nn.sigmoid(c)
        dout = jnp.where(j < nq, dq_ref[...], jnp.where(j < 2 * nq, dk_ref[...], dv_ref[...]))
        _, vjp = jax.vjp(lambda a: _conv_post(j, a), c * sig)
        dc = vjp(dout)[0] * (sig * (1.0 + c * (1.0 - sig)))
        du = w[3:4] * dc + w[2:3] * _shift_up(dc, 1) + w[1:2] * _shift_up(dc, 2) + w[0:1] * _shift_up(dc, 3)
        du_ref[...] = du.astype(du_ref.dtype)
        rows = lax.broadcasted_iota(I32, (8, LANES), 0)
        dw = jnp.zeros((8, LANES), F32)
        for k in range(4):
            dw = dw + jnp.where(rows == k, jnp.sum(dc * taps[k], axis=0, keepdims=True), 0.0)
        dw_ref[...] = dw

    blk = lambda f: pl.BlockSpec((S_, LANES), f)
    return _pc(body, grid=(GDN_CONV_CH // LANES,),
               in_specs=[blk(lambda j: (0, j)), pl.BlockSpec((8, LANES), lambda j: (0, j)),
                         blk(lambda j: (0, jnp.minimum(j, nq - 1))), blk(lambda j: (0, jnp.clip(j - nq, 0, nq - 1))),
                         blk(lambda j: (0, jnp.clip(j - 2 * nq, 0, 2 * nq - 1))), pl.BlockSpec(memory_space=pl.ANY)],
               out_specs=[blk(lambda j: (0, j)), pl.BlockSpec((8, LANES), lambda j: (0, j))],
               out_shape=[_sds(into.shape, into.dtype), _sds((8, GDN_CONV_CH), F32)], input_output_aliases={5: 0},
               compiler_params=_cparams(("parallel",)), name=name)(proj, conv_w, dq, dk, dv, into)


_NN, _NT, _TN = ((1,), (0,)), ((1,), (1,)), ((0,), (0,))


def _split(x, n):
    parts = []
    for _ in range(n):
        h = x.astype(BF16)
        parts.append(h)
        x = x - h.astype(F32)
    return parts


def _dot(a, b, dn, mode):
    d = lambda p, q: lax.dot_general(p, q, (dn, ((), ())), preferred_element_type=F32)
    if mode == "lo":
        return d(a.astype(BF16), b.astype(BF16))
    if mode == "x3":
        (ah, al), (bh, bl) = _split(a, 2), _split(b, 2)
        return d(ah, bh) + (d(ah, bl) + d(al, bh))
    b0, b1, b2 = _split(b, 3)
    ab = a.astype(BF16)
    return d(ab, b0) + (d(ab, b1) + d(ab, b2))


def _make_dots(mode):
    @jax.custom_vjp
    def nn(a, b):
        return _dot(a, b, _NN, mode)

    @jax.custom_vjp
    def nt(a, b):
        return _dot(a, b, _NT, mode)

    @jax.custom_vjp
    def tn(a, b):
        return _dot(a, b, _TN, mode)

    nn.defvjp(lambda a, b: (nn(a, b), (a, b)), lambda r, d: (nt(d, r[1]), tn(r[0], d)))
    nt.defvjp(lambda a, b: (nt(a, b), (a, b)), lambda r, d: (nn(d, r[1]), tn(d, r[0])))
    tn.defvjp(lambda a, b: (tn(a, b), (a, b)), lambda r, d: (nt(r[1], d), nn(r[0], d)))
    return nn, nt, tn


_nn_hi, _nt_hi, _tn_hi = _make_dots("x3")
_nn_lo, _nt_lo, _tn_lo = _make_dots("lo")


@jax.custom_vjp
def _nn_const(a, b):
    return _dot(a, b, _NN, "xl")


_nn_const.defvjp(lambda a, b: (_nn_const(a, b), a), lambda a, d: (jnp.zeros_like(a), _dot(a, d, _TN, "xl")))


def _each(f, *lists):
    return [f(*xs) for xs in zip(*lists)]


@jax.custom_vjp
def _unit_inverses(xs):
    C = xs[0].shape[0]
    eye = (lax.broadcasted_iota(I32, (C, C), 0) == lax.broadcasted_iota(I32, (C, C), 1)).astype(F32)
    ainv, p = [eye + a for a in xs], xs
    for _ in range(int(math.log2(C)) - 1):
        p = _each(lambda a: _dot(a, a, _NN, "x3"), p)
        ainv = _each(lambda a, b: a + _dot(a, b, _NN, "x3"), ainv, p)
    return ainv


def _unit_inverses_bwd(ainv, d):
    left = _each(lambda a, g: _dot(a, g, _TN, "x3"), ainv, d)
    return (_each(lambda t, a: _dot(t, a, _NT, "x3"), left, ainv),)


_unit_inverses.defvjp(lambda xs: (lambda a: (a, a))(_unit_inverses(xs)), _unit_inverses_bwd)


def _gdn_chunk(q, k, v, gb, bb, state):
    C = GDN_C
    e0 = (lax.broadcasted_iota(I32, (1, LANES), 1) == 0).astype(F32)
    ri = lax.broadcasted_iota(I32, (C, C), 0)
    ci = lax.broadcasted_iota(I32, (C, C), 1)
    causal, strict = ri >= ci, ri > ci
    tri, eye, ones = causal.astype(F32), (ri == ci).astype(F32), jnp.ones((C, C), F32)
    last = lax.broadcasted_iota(I32, (C, LANES), 0) == C - 1
    g1 = _each(lambda a: jnp.sum(a * e0, -1, keepdims=True), gb)
    b1 = _each(lambda a: jnp.sum(a * e0, -1, keepdims=True), bb)
    gc_c = _each(lambda g: _nn_const(tri, jnp.broadcast_to(g, (C, C))), g1)
    gc_d = _each(lambda g: _nn_const(tri, jnp.broadcast_to(g, (C, LANES))), g1)
    gr_c = _each(lambda g: _nn_const(ones, eye * g), gc_c)
    decay = _each(lambda a, r: jnp.where(causal, jnp.exp(jnp.where(causal, a - r, 0.0)), 0.0), gc_c, gr_c)
    kb = _each(lambda a, b: a * b, k, b1)
    vb = _each(lambda a, b: a * b, v, b1)
    x = _each(lambda a, b, d: -jnp.where(strict, _nt_lo(a, b) * d, 0.0), kb, k, decay)
    ainv = _unit_inverses(x)
    u = _each(_nn_hi, ainv, vb)
    w = _each(lambda a, b, g: _nn_hi(a, b * jnp.exp(g)), ainv, kb, gc_d)
    attn = _each(lambda a, b, d: jnp.where(causal, _nt_lo(a, b) * d, 0.0), q, k, decay)
    v_new = _each(lambda a, b, s: a - _nn_lo(b, s), u, w, state)
    o = _each(lambda a, g, s, t, vn: _nn_lo(a * jnp.exp(g), s) + _nn_lo(t, vn), q, gc_d, state, attn, v_new)
    gl = _each(lambda g: jnp.sum(jnp.where(last, g, 0.0), axis=0, keepdims=True), gc_d)
    new_state = _each(lambda s, g, a, gd, vn: s * jnp.exp(jnp.sum(g * e0, -1, keepdims=True)) + _tn_lo(a * jnp.exp(g - gd), vn),
                      state, gl, k, gc_d, v_new)
    return o, new_state


def _head_slices(ref, width):
    return [ref[:, h * width:(h + 1) * width] for h in range(GDN_H)]


def gdn_chunk_fwd(qkv, g_b, beta_b, name):
    S_ = qkv.shape[0]
    N = S_ // GDN_C

    def body(q_ref, k_ref, v_ref, g_ref, b_ref, o_ref, s_ref, state):
        @pl.when(pl.program_id(0) == 0)
        def _():
            state[...] = jnp.zeros_like(state)

        st = [state[h] for h in range(GDN_H)]
        s_ref[0] = state[...]
        o, st2 = _gdn_chunk(_head_slices(q_ref, GDN_DK), _head_slices(k_ref, GDN_DK), _head_slices(v_ref, GDN_DV),
                            _head_slices(g_ref, GDN_DK), _head_slices(b_ref, GDN_DK), st)
        for h in range(GDN_H):
            o_ref[:, h * GDN_DV:(h + 1) * GDN_DV] = o[h]
            state[h] = st2[h]

    return _pc(body, grid=(N,),
               in_specs=[pl.BlockSpec((GDN_C, GDN_QK), lambda n: (n, 0)), pl.BlockSpec((GDN_C, GDN_QK), lambda n: (n, 1)),
                         pl.BlockSpec((GDN_C, GDN_V), lambda n: (n, 1)), pl.BlockSpec((GDN_C, GDN_QK), lambda n: (n, 0)),
                         pl.BlockSpec((GDN_C, GDN_QK), lambda n: (n, 0))],
               out_specs=[pl.BlockSpec((GDN_C, GDN_V), lambda n: (n, 0)),
                          pl.BlockSpec((1, GDN_H, GDN_DK, GDN_DV), lambda n: (n, 0, 0, 0))],
               out_shape=[_sds((S_, GDN_V), F32), _sds((N, GDN_H, GDN_DK, GDN_DV), F32)],
               scratch_shapes=[pltpu.VMEM((GDN_H, GDN_DK, GDN_DV), F32)],
               compiler_params=_cparams(("arbitrary",)), name=name)(qkv, qkv, qkv, g_b, beta_b)


def gdn_chunk_bwd(qkv, g_b, beta_b, states, do, name):
    S_ = qkv.shape[0]
    N = S_ // GDN_C

    def body(q_ref, k_ref, v_ref, g_ref, b_ref, s_ref, do_ref, dq_ref, dk_ref, dv_ref, dg_ref, db_ref, dstate):
        @pl.when(pl.program_id(0) == 0)
        def _():
            dstate[...] = jnp.zeros_like(dstate)

        _, vjp = jax.vjp(_gdn_chunk, _head_slices(q_ref, GDN_DK), _head_slices(k_ref, GDN_DK), _head_slices(v_ref, GDN_DV),
                         _head_slices(g_ref, GDN_DK), _head_slices(b_ref, GDN_DK), [s_ref[0, h] for h in range(GDN_H)])
        dq, dk, dv, dg, db, ds = vjp((_head_slices(do_ref, GDN_DV), [dstate[h] for h in range(GDN_H)]))
        for h in range(GDN_H):
            kk, vv = slice(h * GDN_DK, (h + 1) * GDN_DK), slice(h * GDN_DV, (h + 1) * GDN_DV)
            dq_ref[:, kk] = dq[h]
            dk_ref[:, kk] = dk[h]
            dv_ref[:, vv] = dv[h]
            dg_ref[:, kk] = dg[h]
            db_ref[:, kk] = db[h]
            dstate[h] = ds[h]

    r = lambda n: N - 1 - n
    qk = lambda c: pl.BlockSpec((GDN_C, GDN_QK), lambda n: (r(n), c))
    vs = lambda c: pl.BlockSpec((GDN_C, GDN_V), lambda n: (r(n), c))
    return _pc(body, grid=(N,),
               in_specs=[qk(0), qk(1), vs(1), qk(0), qk(0),
                         pl.BlockSpec((1, GDN_H, GDN_DK, GDN_DV), lambda n: (r(n), 0, 0, 0)), vs(0)],
               out_specs=[qk(0), qk(0), vs(0), qk(0), qk(0)],
               out_shape=[_sds((S_, GDN_QK), F32), _sds((S_, GDN_QK), F32), _sds((S_, GDN_V), F32),
                          _sds((S_, GDN_QK), F32), _sds((S_, GDN_QK), F32)],
               scratch_shapes=[pltpu.VMEM((GDN_H, GDN_DK, GDN_DV), F32)],
               compiler_params=_cparams(("arbitrary",)), name=name)(qkv, qkv, qkv, g_b, beta_b, states, do)


def _rope_tables(pos_ref, inv_ref, cm_ref, sg_ref):
    ang = pos_ref[...] * inv_ref[...]
    return jnp.cos(ang) * cm_ref[...], jnp.sin(ang) * sg_ref[...]


def mla_prep_fwd(qpad, kv, proj, pos, rope_consts, name):
    S_ = qpad.shape[0]
    ts = 512
    W = 2 * LANES

    def body(q_ref, kv_ref, kr_ref, pos_ref, inv_ref, cm_ref, sg_ref, qh_ref, kh_ref, vh_ref):
        cs, sn = _rope_tables(pos_ref, inv_ref, cm_ref, sg_ref)
        rope = lambda r: r * cs + pltpu.roll(r, LANES // 2, 1) * sn
        krr = rope(kr_ref[...].astype(F32)).astype(BF16)
        for h in range(MLA_H):
            qh_ref[h, :, 0:LANES] = (q_ref[:, h * W:h * W + LANES].astype(F32) * MLA_SCALE).astype(BF16)
            qh_ref[h, :, LANES:W] = (rope(q_ref[:, h * W + LANES:(h + 1) * W].astype(F32)) * MLA_SCALE).astype(BF16)
            kh_ref[h, :, 0:LANES] = kv_ref[:, h * W:h * W + LANES].astype(BF16)
            kh_ref[h, :, LANES:W] = krr
            vh_ref[h] = kv_ref[:, h * W + LANES:(h + 1) * W].astype(BF16)

    one = pl.BlockSpec((1, LANES), lambda i: (0, 0))
    return _pc(body, grid=(S_ // ts,),
               in_specs=[pl.BlockSpec((ts, MLA_H * W), lambda i: (i, 0)), pl.BlockSpec((ts, MLA_H * W), lambda i: (i, 0)),
                         pl.BlockSpec((ts, LANES), lambda i: (i, 1536 // LANES)), pl.BlockSpec((ts, 1), lambda i: (i, 0)),
                         one, one, one],
               out_specs=[pl.BlockSpec((MLA_H, ts, W), lambda i: (0, i, 0)), pl.BlockSpec((MLA_H, ts, W), lambda i: (0, i, 0)),
                          pl.BlockSpec((MLA_H, ts, LANES), lambda i: (0, i, 0))],
               out_shape=[_sds((MLA_H, S_, W), BF16), _sds((MLA_H, S_, W), BF16), _sds((MLA_H, S_, LANES), BF16)],
               compiler_params=_cparams(("parallel",)), name=name)(qpad, kv, proj, pos, *rope_consts)


def mla_prep_bwd(dqh, dkh, dvh, pos, rope_consts, into, name):
    S_ = dqh.shape[1]
    ts = 512
    W = 2 * LANES

    def body(dq_ref, dk_ref, dv_ref, pos_ref, inv_ref, cm_ref, sg_ref, _, dqp_ref, dkv_ref, dkr_ref):
        cs, sn = _rope_tables(pos_ref, inv_ref, cm_ref, sg_ref)
        rope_t = lambda g: g * cs + pltpu.roll(g * sn, LANES // 2, 1)
        acc = jnp.zeros((ts, LANES), F32)
        for h in range(MLA_H):
            dqp_ref[:, h * W:h * W + LANES] = (dq_ref[h, :, 0:LANES].astype(F32) * MLA_SCALE).astype(BF16)
            dqp_ref[:, h * W + LANES:(h + 1) * W] = (rope_t(dq_ref[h, :, LANES:W].astype(F32)) * MLA_SCALE).astype(BF16)
            dkv_ref[:, h * W:h * W + LANES] = dk_ref[h, :, 0:LANES]
            dkv_ref[:, h * W + LANES:(h + 1) * W] = dv_ref[h]
            acc = acc + dk_ref[h, :, LANES:W].astype(F32)
        dkr_ref[...] = rope_t(acc).astype(dkr_ref.dtype)

    one = pl.BlockSpec((1, LANES), lambda i: (0, 0))
    return _pc(body, grid=(S_ // ts,),
               in_specs=[pl.BlockSpec((MLA_H, ts, W), lambda i: (0, i, 0)), pl.BlockSpec((MLA_H, ts, W), lambda i: (0, i, 0)),
                         pl.BlockSpec((MLA_H, ts, LANES), lambda i: (0, i, 0)), pl.BlockSpec((ts, 1), lambda i: (i, 0)),
                         one, one, one, pl.BlockSpec(memory_space=pl.ANY)],
               out_specs=[pl.BlockSpec((ts, MLA_H * W), lambda i: (i, 0)), pl.BlockSpec((ts, MLA_H * W), lambda i: (i, 0)),
                          pl.BlockSpec((ts, LANES), lambda i: (i, 1536 // LANES))],
               out_shape=[_sds((S_, MLA_H * W), BF16), _sds((S_, MLA_H * W), BF16), _sds(into.shape, into.dtype)],
               input_output_aliases={7: 2}, compiler_params=_cparams(("parallel",)), name=name)(dqh, dkh, dvh, pos, *rope_consts, into)


NEG = -1e30


FLASH_TILE = 1024
FLASH_SUB = 256


def _scores(q, k, diagonal):
    s = lax.dot_general(q, k, (_NT, ((), ())), preferred_element_type=F32)
    if not diagonal:
        return s
    return jnp.where(lax.broadcasted_iota(I32, s.shape, 1) <= lax.broadcasted_iota(I32, s.shape, 0), s, NEG)


def _sub_blocks(t, diagonal):
    sub = min(FLASH_SUB, t) if diagonal else t
    return [(c * sub if diagonal else 0, slice(c * sub, (c + 1) * sub)) for c in range(t // sub)]


FLASH_HEADS = 2


def flash_fwd(qh, kh, vh, name):
    H, S_, W = qh.shape
    t = _tile(S_, FLASH_TILE)
    n = S_ // t
    G = FLASH_HEADS
    heads = list(range(G))

    def body(q_ref, k_ref, v_ref, o_ref, lse_ref, m_s, l_s, acc):
        qi, kj = pl.program_id(1), pl.program_id(2)

        @pl.when(kj == 0)
        def _():
            m_s[...] = jnp.full_like(m_s, NEG)
            l_s[...] = jnp.zeros_like(l_s)
            acc[...] = jnp.zeros_like(acc)

        def step(diagonal):
            s = _each(lambda a: _scores(q_ref[a], k_ref[a], diagonal), heads)
            m_old = _each(lambda a: m_s[a], heads)
            m_new = _each(lambda mo, sa: jnp.maximum(mo, jnp.max(sa, axis=-1, keepdims=True)), m_old, s)
            alpha = _each(lambda mo, mn: jnp.exp(mo - mn), m_old, m_new)
            p = _each(lambda sa, mn: jnp.exp(sa - mn[:, :1]), s, m_new)
            pv = _each(lambda pa, a: lax.dot_general(pa.astype(BF16), v_ref[a], (_NN, ((), ())), preferred_element_type=F32), p, heads)
            for a in heads:
                l_s[a] = alpha[a] * l_s[a] + jnp.sum(p[a], axis=-1, keepdims=True)
                acc[a] = alpha[a] * acc[a] + pv[a]
                m_s[a] = m_new[a]

        pl.when(kj < qi)(lambda: step(False))
        pl.when(kj == qi)(lambda: step(True))

        @pl.when(kj == n - 1)
        def _():
            for a in heads:
                o_ref[:, a * LANES:(a + 1) * LANES] = (acc[a] / l_s[a]).astype(o_ref.dtype)
                lse_ref[a] = m_s[a] + jnp.log(l_s[a])

    return _pc(body, grid=(H // G, n, n),
               in_specs=[pl.BlockSpec((G, t, W), lambda h, i, j: (h, i, 0)),
                         pl.BlockSpec((G, t, W), lambda h, i, j: (h, jnp.minimum(i, j), 0)),
                         pl.BlockSpec((G, t, LANES), lambda h, i, j: (h, jnp.minimum(i, j), 0))],
               out_specs=[pl.BlockSpec((t, G * LANES), lambda h, i, j: (i, h)), pl.BlockSpec((G, t, LANES), lambda h, i, j: (h, i, 0))],
               out_shape=[_sds((S_, H * LANES), BF16), _sds((H, S_, LANES), F32)],
               scratch_shapes=[pltpu.VMEM((G, t, LANES), F32)] * 3,
               compiler_params=_cparams(("parallel", "parallel", "arbitrary")), name=name)(qh, kh, vh)


def flash_bwd(qh, kh, vh, o, lse, do, name):
    H, S_, W = qh.shape
    t = _tile(S_, FLASH_TILE)
    n = S_ // t
    G = FLASH_HEADS
    heads = list(range(G))

    def body(q_ref, k_ref, v_ref, o_ref, lse_ref, do_ref, dq_ref, dk_ref, dv_ref, dq_acc, dk_acc, dv_acc):
        kj, qi = pl.program_id(1), pl.program_id(2)

        @pl.when(jnp.logical_and(kj == 0, qi == 0))
        def _():
            dq_acc[...] = jnp.zeros_like(dq_acc)

        @pl.when(qi == 0)
        def _():
            dk_acc[...] = jnp.zeros_like(dk_acc)
            dv_acc[...] = jnp.zeros_like(dv_acc)

        def step(diagonal):
            lanes = lambda a: slice(a * LANES, (a + 1) * LANES)
            do_ = _each(lambda a: do_ref[:, lanes(a)], heads)
            dob = _each(lambda d: d.astype(BF16), do_)
            delta = _each(lambda d, a: jnp.sum(d.astype(F32) * o_ref[:, lanes(a)].astype(F32), axis=-1, keepdims=True), do_, heads)
            for r0, keys in _sub_blocks(t, diagonal):
                p = _each(lambda a: jnp.exp(_scores(q_ref[a, r0:, :], k_ref[a, keys, :], diagonal) - lse_ref[a, r0:, :1]), heads)
                dp = _each(lambda d, a: lax.dot_general(d[r0:], v_ref[a, keys, :], (_NT, ((), ())), preferred_element_type=F32), dob, heads)
                ds = _each(lambda pa, dpa, de: (pa * (dpa - de[r0:])).astype(BF16), p, dp, delta)
                rows = pl.ds(pl.multiple_of(qi * t, t) + r0, t - r0)
                for a in heads:
                    dv_acc[a, keys, :] += lax.dot_general(p[a].astype(BF16), dob[a][r0:], (_TN, ((), ())), preferred_element_type=F32)
                    dk_acc[a, keys, :] += lax.dot_general(ds[a], q_ref[a, r0:, :], (_TN, ((), ())), preferred_element_type=F32)
                    dq_acc[a, rows, :] += lax.dot_general(ds[a], k_ref[a, keys, :], (_NN, ((), ())), preferred_element_type=F32)

        pl.when(qi > kj)(lambda: step(False))
        pl.when(qi == kj)(lambda: step(True))

        @pl.when(qi == n - 1)
        def _():
            dk_ref[...] = dk_acc[...].astype(BF16)
            dv_ref[...] = dv_acc[...].astype(BF16)

        @pl.when(jnp.logical_and(kj == n - 1, qi == n - 1))
        def _():
            dq_ref[...] = dq_acc[...].astype(BF16)

    qrow = lambda h, j, i: jnp.maximum(i, j)
    return _pc(body, grid=(H // G, n, n),
               in_specs=[pl.BlockSpec((G, t, W), lambda h, j, i: (h, qrow(h, j, i), 0)),
                         pl.BlockSpec((G, t, W), lambda h, j, i: (h, j, 0)),
                         pl.BlockSpec((G, t, LANES), lambda h, j, i: (h, j, 0)),
                         pl.BlockSpec((t, G * LANES), lambda h, j, i: (qrow(h, j, i), h)),
                         pl.BlockSpec((G, t, LANES), lambda h, j, i: (h, qrow(h, j, i), 0)),
                         pl.BlockSpec((t, G * LANES), lambda h, j, i: (qrow(h, j, i), h))],
               out_specs=[pl.BlockSpec((G, S_, W), lambda h, j, i: (h, 0, 0)),
                          pl.BlockSpec((G, t, W), lambda h, j, i: (h, j, 0)),
                          pl.BlockSpec((G, t, LANES), lambda h, j, i: (h, j, 0))],
               out_shape=[_sds((H, S_, W), BF16), _sds((H, S_, W), BF16), _sds((H, S_, LANES), BF16)],
               scratch_shapes=[pltpu.VMEM((G, S_, W), F32), pltpu.VMEM((G, t, W), F32), pltpu.VMEM((G, t, LANES), F32)],
               compiler_params=_cparams(("parallel", "arbitrary", "arbitrary")), name=name)(qh, kh, vh, o, lse, do)


def loss_head(x, target, g, name):
    S_ = x.shape[0]
    ts = 512

    def body(x_ref, t_ref, g_ref, l_ref, dx_ref, dg_ref):
        @pl.when(pl.program_id(0) == 0)
        def _():
            l_ref[...] = jnp.zeros_like(l_ref)
            dg_ref[...] = jnp.zeros_like(dg_ref)

        y, vjp = jax.vjp(_rms, x_ref[...], g_ref[...])
        err = y - t_ref[...]
        l_ref[...] += 0.5 * jnp.sum(jnp.sum(err * err, axis=-1, keepdims=True), axis=0, keepdims=True) / D
        dx, dg = vjp(err / D)
        dx_ref[...] = dx
        dg_ref[...] += dg

    row = pl.BlockSpec((ts, D), lambda i: (i, 0))
    return _pc(body, grid=(S_ // ts,), in_specs=[row, row, pl.BlockSpec((1, D), lambda i: (0, 0))],
               out_specs=[pl.BlockSpec((1, LANES), lambda i: (0, 0)), row, pl.BlockSpec((1, D), lambda i: (0, 0))],
               out_shape=[_sds((1, LANES), F32), _sds((S_, D), F32), _sds((1, D), F32)],
               compiler_params=_cparams(("arbitrary",)), name=name)(x, target, g)


def adamw(w, parts, m, v, name, after=None):
    R, C = w.shape
    rows = [p.shape[1] for p in parts[0]]
    tr = R
    for cand in (512, 256, 128, 64, 32, 16, 8):
        if all(r % cand == 0 for r in rows) and cand * C * 4 * len(rows) <= 2 * 1024 * 1024:
            tr = cand
            break
    c1 = 1.0 - ADAM_B1 ** ADAM_STEP
    c2 = 1.0 - ADAM_B2 ** ADAM_STEP
    starts = [sum(rows[:k]) // tr for k in range(len(rows))]
    flat = [p for part in parts for p in part]

    def body(*refs):
        w_ref, m_ref, v_ref = refs[0], refs[1 + len(flat)], refs[2 + len(flat)]
        g_ref, d_ref, nm_ref, nv_ref = refs[-4:]
        i = pl.program_id(0)
        gg, at = None, 1
        for part in parts:
            val = None
            for k in range(len(part)):
                p_ref = refs[at]
                at += 1
                s = p_ref[0].astype(F32)
                for n in range(1, p_ref.shape[0]):
                    s = s + p_ref[n].astype(F32)
                val = s if val is None else jnp.where(i >= starts[k], s, val)
            gg = val if gg is None else gg + val
        m2 = ADAM_B1 * m_ref[...] + (1.0 - ADAM_B1) * gg
        v2 = ADAM_B2 * v_ref[...] + (1.0 - ADAM_B2) * (gg * gg)
        g_ref[...] = gg
        d_ref[...] = -ADAM_LR * ((m2 / c1) / (jnp.sqrt(v2 / c2) + ADAM_EPS) + ADAM_WD * w_ref[...])
        nm_ref[...] = m2
        nv_ref[...] = v2

    blk = pl.BlockSpec((tr, C), lambda i: (i, 0))
    piece = lambda p, k: pl.BlockSpec((p.shape[0], tr, C), lambda i: (0, jnp.clip(i - starts[k], 0, rows[k] // tr - 1), 0))
    pblk = [piece(p, k) for part in parts for k, p in enumerate(part)]
    extra = [] if after is None else [after]
    return _pc(body, grid=(R // tr,), in_specs=[blk] + pblk + [blk, blk] + [pl.BlockSpec(memory_space=pl.ANY)] * len(extra),
               out_specs=[blk] * 4, out_shape=[_sds((R, C), F32)] * 4,
               compiler_params=_cparams(("parallel",)), name=name)(w, *flat, m, v, *extra)


def sum_slots(recv, name):
    n, R, C = recv.shape

    def body(r_ref, o_ref):
        acc = r_ref[0]
        for s in range(1, n):
            acc = acc + r_ref[s]
        o_ref[...] = acc

    return _pc(body, grid=(1,), in_specs=[pl.BlockSpec((n, R, C), lambda i: (0, 0, 0))],
               out_specs=pl.BlockSpec((R, C), lambda i: (0, 0)), out_shape=_sds((R, C), F32),
               compiler_params=_cparams(("arbitrary",)), name=name)(recv)


def _chip_peers():
    x, y, c = lax.axis_index("x"), lax.axis_index("y"), lax.axis_index("c")
    return (x, y, c), [(1 - x, y, c), (x, 1 - y, c), (1 - x, 1 - y, c)]


def _chip_index(p):
    return 2 * p[0] + p[1]


def _win(ref, axis, chip, size):
    if axis is None:
        return ref.at[chip]
    idx = [slice(None)] * len(ref.shape)
    idx[axis] = pl.ds(pl.multiple_of(chip * size, size), size)
    return ref.at[tuple(idx)]


def _remote(src, dst, send_sem, recv_sem, peer):
    return pltpu.make_async_remote_copy(src_ref=src, dst_ref=dst, send_sem=send_sem, recv_sem=recv_sem, device_id=peer,
                                        device_id_type=MESH)


HBM_SPEC = pl.BlockSpec(memory_space=pltpu.HBM)
SEM_SPEC = pl.BlockSpec(memory_space=pltpu.SEMAPHORE)
ANY_SPEC = pl.BlockSpec(memory_space=pl.ANY)
DATAFLOW = pltpu.SideEffectType.DATAFLOW_SIDE_EFFECTING


def gather_piece(i, l, o, axis, size):
    return (i, lambda r, chip: r.at[l], o, lambda r, chip: _win(r, axis, chip, size))


def scatter_piece(i, o, axis, size):
    return (i, lambda r, chip: _win(r, axis, chip, size), o, lambda r, chip: r.at[chip])


def whole_piece(i):
    return (i, lambda r, chip: r, i, lambda r, chip: r)


def _copies(pieces, in_refs, out_refs, send, recv, sibling):
    me, peers = _chip_peers()
    if sibling:
        peers = [(me[0], me[1], 1 - me[2])]
    mine = _chip_index(me)
    remote = []
    for n, (i, src, o, dst) in enumerate(pieces):
        d = dst(out_refs[o], mine)
        remote += [_remote(src(in_refs[i], _chip_index(p)), d, send.at[len(peers) * n + k], recv.at[len(peers) * n + k], p)
                   for k, p in enumerate(peers)]
    return remote


def own_window(a, axis, size, chip):
    if axis is None:
        return lax.dynamic_index_in_dim(a, chip, 0, keepdims=False)
    return lax.dynamic_slice_in_dim(a, chip * size, size, axis=axis)


def place_own(land, own, axis, size, chip):
    if axis is None:
        return lax.dynamic_update_slice_in_dim(land, own[None], chip, axis=0)
    return lax.dynamic_update_slice_in_dim(land, own, chip * size, axis=axis)


def exchange_start(pieces, ins, out_shapes, after, name, sibling=False):
    n_in, n_out, ncp = len(ins), len(out_shapes), len(pieces)

    def body(*refs):
        in_refs, land_refs = refs[:n_in], refs[n_in:n_in + n_out]
        send, recv = refs[n_in + n_out + 1], refs[n_in + n_out + 2]
        token = refs[-1]
        for cp in _copies(pieces, in_refs, land_refs, send, recv, sibling):
            cp.start()
        token[...] = jnp.zeros_like(token)

    hbm = lambda a: pltpu.with_memory_space_constraint(a, pltpu.HBM)
    lands = [hbm(lax.empty(s.shape, s.dtype)) for s in out_shapes]
    sem = pltpu.SemaphoreType.DMA(((1 if sibling else 3) * ncp,))
    thru = [pltpu.HBM(a.shape, a.dtype) for a in ins] + [pltpu.HBM(s.shape, s.dtype) for s in out_shapes]
    res = _pc(body, in_specs=[HBM_SPEC] * (n_in + n_out) + [ANY_SPEC],
              out_specs=[SEM_SPEC, SEM_SPEC] + [HBM_SPEC] * (n_in + n_out) + [pl.BlockSpec(memory_space=pltpu.VMEM)],
              out_shape=[sem, sem] + thru + [_sds((8, LANES), F32)],
              input_output_aliases={i: 2 + i for i in range(n_in + n_out)},
              compiler_params=pltpu.CompilerParams(has_side_effects=DATAFLOW), name=name)(*[hbm(a) for a in ins], *lands, after)
    return (res[0], res[1]), list(res[2:2 + n_in]), list(res[2 + n_in:2 + n_in + n_out]), res[-1]


def exchange_wait(pieces, sems, ins, lands, after, name, sibling=False):
    n_in, n_out = len(ins), len(lands)

    def body(*refs):
        in_refs, land_refs = refs[:n_in], refs[n_in:n_in + n_out]
        send, recv = refs[n_in + n_out], refs[n_in + n_out + 1]
        for cp in _copies(pieces, in_refs, land_refs, send, recv, sibling):
            cp.wait_send()
            cp.wait_recv()

    thru = [pltpu.HBM(a.shape, a.dtype) for a in ins] + [pltpu.HBM(a.shape, a.dtype) for a in lands]
    res = _pc(body, in_specs=[HBM_SPEC] * (n_in + n_out) + [SEM_SPEC, SEM_SPEC, ANY_SPEC], out_specs=[HBM_SPEC] * (n_in + n_out),
              out_shape=thru, input_output_aliases={i: i for i in range(n_in + n_out)},
              compiler_params=pltpu.CompilerParams(has_side_effects=DATAFLOW), name=name)(*ins, *lands, sems[0], sems[1], after)
    return list(res[:n_in]), list(res[n_in:])


def exchange_all(buf, name):
    def body(in_ref, out_ref, send, recv, local):
        x, y, c = lax.axis_index("x"), lax.axis_index("y"), lax.axis_index("c")
        mine = 4 * x + 2 * y + c
        loc = pltpu.make_async_copy(in_ref, out_ref.at[mine], local)
        loc.start()
        copies = [loc]
        for k in range(1, 8):
            peer = (x ^ (k >> 2), y ^ ((k >> 1) & 1), c ^ (k & 1))
            cp = pltpu.make_async_remote_copy(src_ref=in_ref, dst_ref=out_ref.at[mine], send_sem=send.at[k - 1],
                                              recv_sem=recv.at[k - 1], device_id=peer, device_id_type=MESH)
            cp.start()
            copies.append(cp)
        for cp in copies:
            cp.wait()

    anyspec = pl.BlockSpec(memory_space=pl.ANY)
    return _pc(body, in_specs=[anyspec], out_specs=anyspec, out_shape=_sds((8,) + buf.shape, buf.dtype),
               scratch_shapes=[pltpu.SemaphoreType.DMA((7,)), pltpu.SemaphoreType.DMA((7,)), pltpu.SemaphoreType.DMA],
               name=name)(buf)


def _norm_fwd(x, g, name):
    return rowwise(f_rms, [(x, D, 0, 0)], [(g, D, 0, 0)], [(D, 0, BF16)], ts=1024, name=name)[0]


def _norm_bwd(x, g, dh, dres, name):
    (dx,), (dg,) = rowwise_bwd(f_rms, [(x, D, 0, 0)], [(g, D, 0, 0)], [(dh, D, 0, 0)], need=[True],
                               adds={0: (dres, D, 0, 0)}, ts=1024, name=name)
    return dx, dg


def pool_fwd(x, W, tag, late=None):
    h = _norm_fwd(x, W["ng"], tag + "_norm")
    proj = mm(h, W["w_in"], out_dtype=BF16, name=tag + "_in")
    if late is not None:
        W = dict(W, **late(proj))
    p = pool_time_fwd(proj, tag + "_win")
    pg = gmm("nn", p, W["w_grp"], G=4, out_dtype=BF16, name=tag + "_grp")
    y = rowwise(f_pool_gate, [(pg, POOL_GROUP, 0, 1), (proj, POOL_GROUP, 4, 1)], [(W["scale"], POOL_GROUP, 0, 1)],
                [(POOL_GROUP, 1, BF16)], ncol=4, ts=1024, name=tag + "_gate")[0]
    xn = mm(y, W["w_out"], add=x, name=tag + "_out")
    return xn, (x, h, proj, p, pg, y)


def pool_bwd(dxn, W, saved, tag, after=None, emit=None):
    x, h, proj, p, pg, y = saved
    emit = emit or (lambda grads: None)
    dy = mm(dxn, W["w_out"], tb=True, after=after, out_dtype=BF16, name=tag + "_dy")
    g = {}
    (dpg, dproj), (g["scale"],) = rowwise_bwd(
        f_pool_gate, [(pg, POOL_GROUP, 0, 1), (proj, POOL_GROUP, 4, 1)], [(W["scale"], POOL_GROUP, 0, 1)],
        [(dy, POOL_GROUP, 0, 1)], need=[True, True], place={1: (2 * POOL_WIDTH, 4)}, narrow=(0, 1), ncol=4, ts=1024, name=tag + "_dgate")
    dp = gmm("nt", dpg, W["w_grp"], G=4, out_dtype=BF16, name=tag + "_dp")
    dproj = pool_time_bwd(dp, dproj, tag + "_dwin")
    g["w_in"] = mm(h, dproj, ta=True, out_dtype=BF16, name=tag + "_dw_in")
    t1 = emit({"w_in": g["w_in"]})
    g["w_out"] = mm(y, dxn, ta=True, after=t1, out_dtype=BF16, name=tag + "_dwout")
    g["w_grp"] = gmm("tn", p, dpg, G=4, out_dtype=BF16, name=tag + "_dwgrp")
    t2 = emit({"w_out": g["w_out"], "w_grp": g["w_grp"]})
    dh = mm(dproj, W["w_in"], tb=True, after=t2, name=tag + "_dh")
    dx, g["ng"] = _norm_bwd(x, W["ng"], dh, dxn, tag + "_dnorm")
    return dx, g


def gdn_fwd(x, W, tag, late=None):
    h = _norm_fwd(x, W["ng"], tag + "_norm")
    proj = mm(h, W["w_in"], name=tag + "_in")
    qkv = gdn_conv_fwd(proj, W["conv"], tag + "_conv")
    g_b, beta_b = rowwise(f_gdn_gates, [(proj, LANES, 6144 // LANES, 0)], [(W["a_log"], LANES, 0, 0), (W["dt_bias"], LANES, 0, 0)],
                          [(GDN_QK, 0, F32), (GDN_QK, 0, F32)], ts=1024, name=tag + "_gates")
    o, states = gdn_chunk_fwd(qkv, g_b, beta_b, tag + "_chunk")
    og = rowwise(f_gdn_out, [(o, GDN_DV, 0, 1), (proj, GDN_DV, 4096 // GDN_DV, 1)], [(W["norm_g"], GDN_DV, 0, 0)],
                 [(GDN_DV, 1, BF16)], ncol=GDN_H, ts=2048, name=tag + "_onorm")[0]
    if late is not None:
        W = dict(W, **late(og))
    xn = mm(og, W["w_out"], add=x, name=tag + "_out")
    return xn, (x, h, proj, qkv, g_b, beta_b, o, states, og)


def gdn_bwd(dxn, W, saved, tag, after=None):
    x, h, proj, qkv, g_b, beta_b, o, states, og = saved
    dog = mm(dxn, W["w_out"], tb=True, after=after, name=tag + "_dog")
    g = {"w_out": mm(og, dxn, ta=True, out_dtype=BF16, name=tag + "_dwout")}
    (do, dproj), (g["norm_g"],) = rowwise_bwd(
        f_gdn_out, [(o, GDN_DV, 0, 1), (proj, GDN_DV, 4096 // GDN_DV, 1)], [(W["norm_g"], GDN_DV, 0, 0)],
        [(dog, GDN_DV, 0, 1)], need=[True, True], place={1: (GDN_IN_PAD, 4096 // GDN_DV)}, narrow=(1,), ncol=GDN_H, ts=2048, name=tag + "_donorm")
    dq, dk, dv, dg_b, dbeta_b = gdn_chunk_bwd(qkv, g_b, beta_b, states, do, tag + "_dchunk")
    (dproj,), (g["a_log"], g["dt_bias"]) = rowwise_bwd(
        f_gdn_gates, [(proj, LANES, 6144 // LANES, 0)], [(W["a_log"], LANES, 0, 0), (W["dt_bias"], LANES, 0, 0)],
        [(dg_b, GDN_QK, 0, 0), (dbeta_b, GDN_QK, 0, 0)], need=[True], place={0: (dproj, 6144 // LANES)}, ts=1024, name=tag + "_dgates")
    dproj, g["conv"] = gdn_conv_bwd(proj, W["conv"], dq, dk, dv, dproj, tag + "_dconv")
    dh = mm(dproj, W["w_in"], tb=True, name=tag + "_dh")
    g["w_in"] = mm(h, dproj, ta=True, out_dtype=BF16, name=tag + "_dw_in")
    dx, g["ng"] = _norm_bwd(x, W["ng"], dh, dxn, tag + "_dnorm")
    return dx, g


def mla_fwd(x, pos, W, tag):
    h = _norm_fwd(x, W["ng"], tag + "_norm")
    proj = mm(h, W["w_in"], out_dtype=BF16, name=tag + "_in")
    hq = rowwise(f_rms, [(proj, MLA_Q_LORA, 0, 0)], [(W["q_g"], MLA_Q_LORA, 0, 0)], [(MLA_Q_LORA, 0, BF16)], ts=1024, name=tag + "_qnorm")[0]
    hkv = rowwise(f_rms, [(proj, MLA_KV_LORA, 2, 0)], [(W["kv_g"], MLA_KV_LORA, 0, 0)], [(MLA_KV_LORA, 0, BF16)], ts=1024, name=tag + "_kvnorm")[0]
    qpad = mm(hq, W["w_uq"], out_dtype=BF16, name=tag + "_uq")
    kv = mm(hkv, W["w_ukv"], out_dtype=BF16, name=tag + "_ukv")
    qh, kh, vh = mla_prep_fwd(qpad, kv, proj, pos, W["rope"], tag + "_prep")
    o, lse = flash_fwd(qh, kh, vh, tag + "_attn")
    og = rowwise(f_ogate, [(o, 512, 0, 1), (proj, 512, 4, 1)], [], [(512, 1, BF16)], ncol=4, ts=1024, name=tag + "_ogate")[0]
    xn = mm(og, W["w_out"], add=x, name=tag + "_out")
    return xn, (x, h, proj, hq, hkv, qh, kh, vh, o, lse, og)


def mla_bwd(dxn, pos, W, saved, tag, after=None):
    x, h, proj, hq, hkv, qh, kh, vh, o, lse, og = saved
    dog = mm(dxn, W["w_out"], tb=True, after=after, out_dtype=BF16, name=tag + "_dog")
    g = {"w_out": mm(og, dxn, ta=True, out_dtype=BF16, name=tag + "_dwout")}
    dproj = jnp.zeros(proj.shape, BF16)
    (do, dproj), _ = rowwise_bwd(f_ogate, [(o, 512, 0, 1), (proj, 512, 4, 1)], [], [(dog, 512, 0, 1)], need=[True, True],
                                 place={1: (dproj, 4)}, narrow=(0,), ncol=4, ts=1024, name=tag + "_dogate")
    dqh, dkh, dvh = flash_bwd(qh, kh, vh, o, lse, do, tag + "_dattn")
    dqpad, dkv, dproj = mla_prep_bwd(dqh, dkh, dvh, pos, W["rope"], dproj, tag + "_dprep")
    dhq = mm(dqpad, W["w_uq"], tb=True, name=tag + "_dhq")
    g["w_uq"] = mm(hq, dqpad, ta=True, out_dtype=BF16, name=tag + "_dwuq")
    dhkv = mm(dkv, W["w_ukv"], tb=True, name=tag + "_dhkv")
    g["w_ukv"] = mm(hkv, dkv, ta=True, out_dtype=BF16, name=tag + "_dwukv")
    (dproj,), (g["q_g"],) = rowwise_bwd(f_rms, [(proj, MLA_Q_LORA, 0, 0)], [(W["q_g"], MLA_Q_LORA, 0, 0)], [(dhq, MLA_Q_LORA, 0, 0)],
                                        need=[True], place={0: (dproj, 0)}, ts=512, name=tag + "_dqnorm")
    (dproj,), (g["kv_g"],) = rowwise_bwd(f_rms, [(proj, MLA_KV_LORA, 2, 0)], [(W["kv_g"], MLA_KV_LORA, 0, 0)], [(dhkv, MLA_KV_LORA, 0, 0)],
                                         need=[True], place={0: (dproj, 2)}, ts=512, name=tag + "_dkvnorm")
    dh = mm(dproj, W["w_in"], tb=True, name=tag + "_dh")
    g["w_in"] = mm(h, dproj, ta=True, out_dtype=BF16, name=tag + "_dw_in")
    dx, g["ng"] = _norm_bwd(x, W["ng"], dh, dxn, tag + "_dnorm")
    return dx, g


def _pad_cols(a, n):
    return jnp.pad(a, ((0, 0), (0, n - a.shape[1])))


def _mla_w_in_layout(w):
    z = lambda n: jnp.zeros((w.shape[0], n), w.dtype)
    kr = w[:, 1280:1344]
    return jnp.concatenate([w[:, :768], z(256), w[:, 768:1280], kr[:, :32], z(32), kr[:, 32:], z(32), z(384), w[:, 1344:]], axis=1)


def _mla_w_in_unlayout(g):
    return jnp.concatenate([g[:, :768], g[:, 1024:1536], g[:, 1536:1568], g[:, 1600:1632], g[:, 2048:]], axis=1)


def _mla_w_uq_layout(w):
    w3 = w.reshape(w.shape[0], MLA_H, MLA_NOPE + MLA_ROPE)
    z = jnp.zeros((w.shape[0], MLA_H, 32), w.dtype)
    return jnp.concatenate([w3[..., :128], w3[..., 128:160], z, w3[..., 160:192], z], axis=-1).reshape(w.shape[0], MLA_H * 256)


def _mla_w_uq_unlayout(g):
    g3 = g.reshape(g.shape[0], MLA_H, 256)
    return jnp.concatenate([g3[..., :128], g3[..., 128:160], g3[..., 192:224]], axis=-1).reshape(g.shape[0], MLA_H * 192)


def _rope_consts():
    half = MLA_ROPE // 2
    inv = ROPE_THETA ** (-jnp.arange(half, dtype=F32) / half)
    z = jnp.zeros((half,), F32)
    o = jnp.ones((half,), F32)
    row = lambda *p: jnp.concatenate(p).reshape(1, LANES)
    return row(inv, z, inv, z), row(o, z, o, z), row(-o, z, o, z)


BIG = ["pool_w_in", "pool_w_grp", "pool_w_out", "gdn_w_in", "gdn_w_out", "mla_w_in", "mla_w_uq", "mla_w_ukv", "mla_w_out"]
BIG_LAYOUT = {"pool_w_in": (1, 1024, (1024, 4096)), "pool_w_grp": (1, 128, (4, 512, 512)), "pool_w_out": (0, 512, (2048, 1024)),
              "gdn_w_in": (None, None, (4, 1024, 1540)), "gdn_w_out": (0, 512, (2048, 1024)),
              "mla_w_in": (None, None, (4, 1024, 848)), "mla_w_uq": (1, 768, (768, 3072)), "mla_w_ukv": (1, 1024, (512, 4096)),
              "mla_w_out": (0, 512, (2048, 1024))}
SMALL_SHARDED = ["pool_scale", "gdn_conv", "mla_q_norm_g", "mla_kv_norm_g"]
SMALL_AXIS = {"pool_scale": 1, "gdn_conv": 2, "mla_q_norm_g": 1, "mla_kv_norm_g": 1}
REPLICATED = ["norm_g", "gdn_a_log", "gdn_dt_bias", "gdn_norm_g", "final_g"]
PACK_C = 1024


def _pack(parts, dtype, row_mult):
    flat = jnp.concatenate([p.reshape(-1).astype(dtype) for p in parts])
    rows = -(-flat.shape[0] // PACK_C)
    rows = -(-rows // row_mult) * row_mult
    return jnp.pad(flat, (0, rows * PACK_C - flat.shape[0])).reshape(rows, PACK_C)


def _unpack(buf, shapes):
    lead = buf.shape[:-2]
    flat = buf.reshape(lead + (-1,))
    out, off = [], 0
    for s in shapes:
        n = int(np.prod(s))
        out.append(flat[..., off:off + n].reshape(lead + tuple(s)))
        off += n
    return out


def _unshard(g4, axis):
    a = jnp.moveaxis(g4, 0, axis)
    s = a.shape
    return a.reshape(s[:axis] + (s[axis] * s[axis + 1],) + s[axis + 2:])


def _to_shards(a, axis):
    s = a.shape
    return jnp.moveaxis(a.reshape(s[:axis] + (4, s[axis] // 4) + s[axis + 1:]), axis, 0)


def layer_weights(full, small, rep, layer):
    ng = rep["norm_g"][layer:layer + 1]
    side_by_side = lambda a4: jnp.moveaxis(a4, 0, 1).reshape(a4.shape[1], 4 * a4.shape[2])
    if layer in (0, 3):
        j = layer // 3
        return dict(ng=ng, w_in=full[("pool_w_in", j)], w_grp=full[("pool_w_grp", j)], scale=small["pool_scale"][j:j + 1],
                    w_out=full[("pool_w_out", j)])
    if layer == 1:
        return dict(ng=ng, w_in=_pad_cols(side_by_side(full[("gdn_w_in", 0)]), GDN_IN_PAD),
                    conv=jnp.pad(small["gdn_conv"][0], ((0, 4), (0, 0))), a_log=_pad_cols(rep["gdn_a_log"], LANES),
                    dt_bias=_pad_cols(rep["gdn_dt_bias"], LANES), norm_g=rep["gdn_norm_g"], w_out=full.get(("gdn_w_out", 0)))
    return dict(ng=ng, w_in=_mla_w_in_layout(side_by_side(full[("mla_w_in", 0)])), q_g=small["mla_q_norm_g"],
                kv_g=small["mla_kv_norm_g"], w_uq=_mla_w_uq_layout(full[("mla_w_uq", 0)]), w_ukv=full[("mla_w_ukv", 0)],
                w_out=full[("mla_w_out", 0)], rope=_rope_consts())


def big_grad_pieces(gl):
    g0, g1, g2, g3 = gl
    slots = lambda a: jnp.moveaxis(a.reshape(a.shape[0], 4, a.shape[1] // 4), 1, 0)
    out = {}
    for l, g in ((0, g0), (1, g3)):
        if g is not None:
            out.update({("pool_w_in", l): g["w_in"], ("pool_w_grp", l): g["w_grp"], ("pool_w_out", l): g["w_out"]})
    if g1 is not None:
        out.update({("gdn_w_in", 0): slots(g1["w_in"][:, :GDN_IN]), ("gdn_w_out", 0): g1["w_out"]})
    if g2 is not None:
        out.update({("mla_w_in", 0): slots(_mla_w_in_unlayout(g2["w_in"])), ("mla_w_uq", 0): _mla_w_uq_unlayout(g2["w_uq"]),
                    ("mla_w_ukv", 0): g2["w_ukv"], ("mla_w_out", 0): g2["w_out"]})
    return out


def small_grads(gl, dfinal):
    g0, g1, g2, g3 = gl
    return {"norm_g": jnp.concatenate([g0["ng"], g1["ng"], g2["ng"], g3["ng"]], axis=0),
            "pool_scale": jnp.concatenate([g0["scale"], g3["scale"]], axis=0), "gdn_conv": g1["conv"][None, :4],
            "gdn_a_log": g1["a_log"][:, :GDN_H], "gdn_dt_bias": g1["dt_bias"][:, :GDN_H], "gdn_norm_g": g1["norm_g"],
            "mla_q_norm_g": g2["q_g"], "mla_kv_norm_g": g2["kv_g"], "final_g": dfinal.reshape(D)}


NAMES = ["norm_g", "pool_w_in", "pool_w_grp", "pool_scale", "pool_w_out", "gdn_w_in", "gdn_conv", "gdn_a_log", "gdn_dt_bias",
         "gdn_norm_g", "gdn_w_out", "mla_w_in", "mla_q_norm_g", "mla_w_uq", "mla_kv_norm_g", "mla_w_ukv", "mla_w_out", "final_g"]


def kernel(x, positions, norm_g, pool_w_in, pool_w_grp, pool_scale, pool_w_out, gdn_w_in, gdn_conv, gdn_a_log, gdn_dt_bias, gdn_norm_g, gdn_w_out, mla_w_in, mla_q_norm_g, mla_w_uq, mla_kv_norm_g, mla_w_ukv, mla_w_out, final_g, loss_target, m_norm_g, m_pool_w_in, m_pool_w_grp, m_pool_scale, m_pool_w_out, m_gdn_w_in, m_gdn_conv, m_gdn_a_log, m_gdn_dt_bias, m_gdn_norm_g, m_gdn_w_out, m_mla_w_in, m_mla_q_norm_g, m_mla_w_uq, m_mla_kv_norm_g, m_mla_w_ukv, m_mla_w_out, m_final_g, v_norm_g, v_pool_w_in, v_pool_w_grp, v_pool_scale, v_pool_w_out, v_gdn_w_in, v_gdn_conv, v_gdn_a_log, v_gdn_dt_bias, v_gdn_norm_g, v_gdn_w_out, v_mla_w_in, v_mla_q_norm_g, v_mla_w_uq, v_mla_kv_norm_g, v_mla_w_ukv, v_mla_w_out, v_final_g):
    args = locals()
    w = {n: args[n] for n in NAMES}
    m = {n: args["m_" + n] for n in NAMES}
    v = {n: args["v_" + n] for n in NAMES}
    my_chip = (2 * lax.axis_index("x") + lax.axis_index("y")).astype(I32)
    S_ = x.shape[1]
    x0, pos, target = x[0], positions.reshape(S_, 1).astype(F32), loss_target[0]
    rep = {n: w[n] for n in REPLICATED}

    small_shapes = [w[n].shape for n in SMALL_SHARDED]
    small_pack = _pack([w[n] for n in SMALL_SHARDED], F32, 8)[None]
    layout = dict(BIG_LAYOUT, small=(None, None, (4,) + small_pack.shape[1:]))

    def shard(key, token):
        n, l = key
        if n == "small":
            return small_pack
        a = w[n][l:l + 1]
        return (a if token is None else a + token[0, 0]).astype(BF16)

    def gather_start(group, after, tag, token=None):
        pieces = [gather_piece(i, 0, i, layout[n][0], layout[n][1]) for i, (n, l) in enumerate(group)]
        ins = [shard(k, token) for k in group]
        shapes = [_sds(layout[n][2], a.dtype) for (n, l), a in zip(group, ins)]
        sems, ins, lands, token = exchange_start(pieces, ins, shapes, after, tag + "_start")
        return (pieces, sems, ins, lands), token

    def finish(handle, after, tag):
        return exchange_wait(*handle, after, tag + "_wait")

    def gathered(group, handle, after, tag):
        srcs, lands = finish(handle, after, tag)
        return {(n, l): place_own(a, s[0], layout[n][0], layout[n][1], my_chip) for (n, l), s, a in zip(group, srcs, lands)}

    group_a = [("small", 0), ("pool_w_in", 0)]
    group_a2 = [("pool_w_grp", 0), ("pool_w_out", 0)]
    group_b = [("gdn_w_in", 0)]
    group_c = [("gdn_w_out", 0), ("mla_w_in", 0), ("mla_w_uq", 0), ("mla_w_ukv", 0), ("mla_w_out", 0), ("pool_w_in", 1),
               ("pool_w_grp", 1), ("pool_w_out", 1)]
    full = {}
    h_a, t_a = gather_start(group_a, x0, "gather_a")
    h_a2, t_a2 = gather_start(group_a2, t_a, "gather_a2", t_a)
    h_b, t_b = gather_start(group_b, t_a2, "gather_b", t_a2)
    h_c, t_c = gather_start(group_c, t_b, "gather_c", t_b)
    full.update(gathered(group_a, h_a, t_c, "gather_a"))
    small = {n: _unshard(a, SMALL_AXIS[n]) for n, a in zip(SMALL_SHARDED, _unpack(full[("small", 0)], small_shapes))}

    def late_l0(proj):
        full.update(gathered(group_a2, h_a2, proj, "gather_a2"))
        return dict(w_grp=full[("pool_w_grp", 0)], w_out=full[("pool_w_out", 0)])

    first = dict(ng=rep["norm_g"][0:1] + t_c[0:1, 0:1], w_in=full[("pool_w_in", 0)], scale=small["pool_scale"][0:1])
    x1, s0 = pool_fwd(x0, first, "l0", late=late_l0)
    W0 = layer_weights(full, small, rep, 0)
    full.update(gathered(group_b, h_b, x1, "gather_b"))

    def late_l1(og):
        full.update(gathered(group_c, h_c, og, "gather_c"))
        return dict(w_out=full[("gdn_w_out", 0)])

    x2, s1 = gdn_fwd(x1, layer_weights(full, small, rep, 1), "l1", late=late_l1)
    W1, W2, W3 = (layer_weights(full, small, rep, i) for i in (1, 2, 3))
    x3, s2 = mla_fwd(x2, pos, W2, "l2")
    x4, s3 = pool_fwd(x3, W3, "l3")
    loss_part, dx4, dfinal = loss_head(x4, target, final_g.reshape(1, D), "loss_head")

    def scatter_start(pieces_of, after, tag):
        keys = list(pieces_of)
        pieces = [scatter_piece(i, i, BIG_LAYOUT[n][0], BIG_LAYOUT[n][1]) for i, (n, l) in enumerate(keys)]
        shapes = [_sds((4,) + tuple(w[n].shape[1:]), BF16) for n, l in keys]
        sems, ins, lands, token = exchange_start(pieces, [pieces_of[k] for k in keys], shapes, after, tag + "_start")
        return keys, (pieces, sems, ins, lands), token

    def scattered(keys, handle, after, tag):
        srcs, lands = finish(handle, after, tag)
        return {(n, l): place_own(a, own_window(g, BIG_LAYOUT[n][0], BIG_LAYOUT[n][1], my_chip), None, None, my_chip)
                for (n, l), g, a in zip(keys, srcs, lands)}

    dx3, g3 = pool_bwd(dx4, W3, s3, "l3")
    k3, h3, t3 = scatter_start(big_grad_pieces((None, None, None, g3)), dx3, "scatter_l3")
    dx2, g2 = mla_bwd(dx3, pos, W2, s2, "l2", after=t3)
    k2, h2, t2 = scatter_start(big_grad_pieces((None, None, g2, None)), dx2, "scatter_l2")
    dx1, g1 = gdn_bwd(dx2, W1, s1, "l1", after=t2)
    k1, h1, t1 = scatter_start(big_grad_pieces((None, g1, None, None)), dx1, "scatter_l1")
    def swap_start(part, tag):
        keys = list(part)
        ins = [part[k] for k in keys]
        pieces = [whole_piece(i) for i in range(len(keys))]
        sems, ins, lands, token = exchange_start(pieces, ins, [_sds(a.shape, a.dtype) for a in ins], ins[0], tag + "_start", sibling=True)
        swaps.append((keys, (pieces, sems, ins, lands), tag))
        return token

    last, swaps = [], []

    def emit_l0(grads):
        first = not last
        now = next(iter(grads.values()))
        early = [(k3, h3, "scatter_l3"), (k2, h2, "scatter_l2")] if first else [(k1, h1, "scatter_l1")]
        landed = {}
        for keys, handle, tag in early:
            landed.update(scattered(keys, handle, now, tag))
        swapping = swap_start(landed, "swap_a" if first else "swap_b")
        tag = "scatter_l0a" if first else "scatter_l0b"
        keys, handle, token = scatter_start({("pool_" + k, 0): a for k, a in grads.items()}, swapping, tag)
        last.append((keys, handle, tag))
        return token

    dx0, g0 = pool_bwd(dx1, W0, s0, "l0", after=t1, emit=emit_l0)
    landed = {}
    for keys, handle, tag in last:
        landed.update(scattered(keys, handle, dx0, tag))
    swapping = swap_start(landed, "swap_c")
    recv, sib = {}, {}
    for keys, handle, tag in swaps:
        mine_, theirs = exchange_wait(*handle, dx0, tag + "_wait", sibling=True)
        recv.update(zip(keys, mine_))
        sib.update(zip(keys, theirs))

    sg = small_grads((g0, g1, g2, g3), dfinal)
    small_names = SMALL_SHARDED + REPLICATED
    small_buf = _pack([sg[n] for n in small_names] + [loss_part], F32, 8)
    small_sum = sum_slots(exchange_all(small_buf, "gather_small"), "sum_small")
    full_small = _unpack(small_sum, [sg[n].shape for n in small_names] + [(1, LANES)])
    loss = full_small[-1][0, 0]
    small_part = {}
    for n, a in zip(small_names, full_small[:-1]):
        if n in SMALL_AXIS:
            a = lax.dynamic_index_in_dim(_to_shards(a, SMALL_AXIS[n]), my_chip, axis=0, keepdims=False)
        small_part[n] = a

    outs = []
    for n in NAMES:
        shp = w[n].shape
        two = (int(np.prod(shp[:-1])), shp[-1]) if len(shp) > 1 else (1, shp[0])
        if n in BIG_LAYOUT:
            layers = shp[0]
            rows = lambda a: a.reshape(4, two[0] // layers, two[1])
            parts = [[rows(recv[(n, l)]) for l in range(layers)], [rows(sib[(n, l)]) for l in range(layers)]]
        else:
            parts = [[small_part[n].reshape((1,) + two)]]
        upper = n in BIG_LAYOUT and not n.startswith("pool")
        res = adamw(w[n].reshape(two), parts, m[n].reshape(two), v[n].reshape(two), "adamw_" + n, after=swapping if upper else None)
        outs.append([r.reshape(shp) for r in res])
    return (loss, dx0[None], *[o[0] for o in outs], *[o[1] for o in outs], *[o[2] for o in outs], *[o[3] for o in outs])
```
